```python
import jax, jax.numpy as jnp
from jax import lax
import numpy as np

D_MODEL = 1024
BATCH = 8
SEQ = 4096
DEPTH = 1

SSD_EXPAND = 2
SSD_D_INNER = SSD_EXPAND * D_MODEL
SSD_HEAD_DIM = 64
SSD_HEADS = SSD_D_INNER // SSD_HEAD_DIM
SSD_GROUPS = 4
SSD_HEADS_PER_GROUP = SSD_HEADS // SSD_GROUPS
SSD_STATE = 128
SSD_CONV = 4
SSD_CHUNK = 128
SSD_BC = SSD_GROUPS * SSD_STATE
SSD_XBC = SSD_D_INNER + 2 * SSD_BC
SSD_IN = SSD_D_INNER + SSD_XBC + SSD_HEADS

SGU_WIDTH = D_MODEL
SGU_GROUPS = 8
SGU_GROUP_DIM = SGU_WIDTH // SGU_GROUPS
SGU_CHUNK = 128
SGU_IN = 2 * SGU_WIDTH

N_BRANCH = 2
GATE_IN = N_BRANCH * D_MODEL
IN_COLS = SSD_IN + SGU_IN + GATE_IN
MIX_WIDTH = SSD_D_INNER + SGU_WIDTH

D_FF = 2816
FFN_CONV = 3

NORM_EPS = 1e-6
LN_EPS = 1e-5

kernel_name = "hybrid_ssd_sgu_convffn_block"


def rms_norm(x, w, eps=NORM_EPS):
    xf = x.astype(jnp.float32)
    y = xf * lax.rsqrt(jnp.mean(xf * xf, axis=-1, keepdims=True) + eps)
    return (y * w.astype(jnp.float32)).astype(x.dtype)


def layer_norm(x, w, b, eps=LN_EPS):
    xf = x.astype(jnp.float32)
    mu = jnp.mean(xf, axis=-1, keepdims=True)
    var = jnp.mean(jnp.square(xf - mu), axis=-1, keepdims=True)
    y = (xf - mu) * lax.rsqrt(var + eps) * w.astype(jnp.float32) + b.astype(jnp.float32)
    return y.astype(x.dtype)


def causal_dwconv(x, w, b):
    k = w.shape[0]
    s = x.shape[1]
    xp = jnp.pad(x, ((0, 0), (k - 1, 0), (0, 0)))
    y = b
    for i in range(k):
        y = y + w[i] * xp[:, i:i + s]
    return y


def segsum_exp(a_cum):
    t = a_cum.shape[-1]
    diff = a_cum[..., :, None] - a_cum[..., None, :]
    mask = jnp.tril(jnp.ones((t, t), dtype=bool))
    return jnp.exp(jnp.where(mask, diff, -jnp.inf))


def ssd_chunked(xh, dt, a, b_mat, c_mat):
    bsz, s = xh.shape[0], xh.shape[1]
    nc = s // SSD_CHUNK
    G, R, L, P, N = SSD_GROUPS, SSD_HEADS_PER_GROUP, SSD_CHUNK, SSD_HEAD_DIM, SSD_STATE
    xdt = (xh.astype(jnp.float32) * dt[..., None]).reshape(bsz, nc, L, G, R, P)
    adt = (dt * a).reshape(bsz, nc, L, G, R)
    a_cum = jnp.cumsum(jnp.transpose(adt, (0, 3, 4, 1, 2)), axis=-1)
    bm = b_mat.reshape(bsz, nc, L, G, N)
    cm = c_mat.reshape(bsz, nc, L, G, N)
    cb = jnp.einsum('bclgn,bcsgn->bgcls', cm, bm)
    decay = segsum_exp(a_cum)
    y_diag = jnp.einsum('bgcls,bgrcls,bcsgrp->bclgrp', cb, decay, xdt)
    decay_states = jnp.exp(a_cum[..., -1:] - a_cum)
    states = jnp.einsum('bcsgn,bgrcs,bcsgrp->bcgrpn', bm, decay_states, xdt)
    chunk_tot = jnp.pad(a_cum[..., -1], ((0, 0), (0, 0), (0, 0), (1, 0)))
    decay_chunk = segsum_exp(jnp.cumsum(chunk_tot, axis=-1))
    states = jnp.concatenate([jnp.zeros_like(states[:, :1]), states], axis=1)
    states = jnp.einsum('bgrzc,bcgrpn->bzgrpn', decay_chunk, states)[:, :-1]
    y_off = jnp.einsum('bclgn,bcgrpn,bgrcl->bclgrp', cm, states, jnp.exp(a_cum))
    return (y_diag + y_off).reshape(bsz, s, SSD_HEADS, P)


def ssd_branch(z, xbc, dt_raw, conv_w, conv_b, dt_bias, a_log, d_skip, norm_w):
    xbc = jax.nn.silu(causal_dwconv(xbc, conv_w, conv_b))
    xs, b_mat, c_mat = jnp.split(xbc, [SSD_D_INNER, SSD_D_INNER + SSD_BC], axis=-1)
    bsz, s = xs.shape[0], xs.shape[1]
    xh = xs.reshape(bsz, s, SSD_HEADS, SSD_HEAD_DIM)
    dt = jax.nn.softplus(dt_raw.astype(jnp.float32) + dt_bias.astype(jnp.float32))
    a = -jnp.exp(a_log.astype(jnp.float32))
    y = ssd_chunked(xh, dt, a,
                    b_mat.reshape(bsz, s, SSD_GROUPS, SSD_STATE),
                    c_mat.reshape(bsz, s, SSD_GROUPS, SSD_STATE))
    y = y + d_skip.astype(jnp.float32)[:, None] * xh.astype(jnp.float32)
    y = y.reshape(bsz, s, SSD_D_INNER).astype(xs.dtype)
    return rms_norm(y * jax.nn.silu(z), norm_w)


def sgu_branch(uv, ln_w, ln_b, w_spatial, b_spatial):
    uv = jax.nn.gelu(uv)
    u, v = jnp.split(uv, 2, axis=-1)
    v = layer_norm(v, ln_w, ln_b)
    bsz, s = v.shape[0], v.shape[1]
    nc = s // SGU_CHUNK
    vc = v.reshape(bsz, nc, SGU_CHUNK, SGU_GROUPS, SGU_GROUP_DIM)
    w_causal = w_spatial * jnp.tril(jnp.ones((SGU_CHUNK, SGU_CHUNK), w_spatial.dtype))
    mixed = jnp.einsum('gts,bcsgd->bctgd', w_causal, vc) + b_spatial.T[None, None, :, :, None]
    return u * mixed.reshape(bsz, s, SGU_WIDTH)


def _fwd_setup_inputs(seed: int = 0) -> dict:
    key = jax.random.key(seed)
    ks = jax.random.split(key, 24)
    f32 = jnp.float32

    def nrm(k, shape, scale):
        return jax.random.normal(k, shape, f32) * scale

    x = jax.random.normal(ks[0], (BATCH, SEQ, D_MODEL), f32)
    norm1_w = 1.0 + nrm(ks[1], (DEPTH, D_MODEL), 0.02)
    w_in = nrm(ks[2], (DEPTH, D_MODEL, IN_COLS), D_MODEL ** -0.5)
    b_gate = nrm(ks[3], (DEPTH, GATE_IN), 0.1)
    conv_a_w = nrm(ks[4], (DEPTH, SSD_CONV, SSD_XBC), SSD_CONV ** -0.5)
    conv_a_b = nrm(ks[5], (DEPTH, SSD_XBC), 0.02)
    dt0 = jnp.exp(jax.random.uniform(ks[6], (DEPTH, SSD_HEADS), f32,
                                     float(np.log(1e-3)), float(np.log(1e-1))))
    dt_bias = dt0 + jnp.log(-jnp.expm1(-dt0))
    a_log = jnp.log(jax.random.uniform(ks[7], (DEPTH, SSD_HEADS), f32, 1.0, 16.0))
    d_skip = 1.0 + nrm(ks[8], (DEPTH, SSD_HEADS), 0.1)
    ssd_norm_w = 1.0 + nrm(ks[9], (DEPTH, SSD_D_INNER), 0.02)
    uv_b = nrm(ks[10], (DEPTH, SGU_IN), 0.02)
    v_ln_w = 1.0 + nrm(ks[11], (DEPTH, SGU_WIDTH), 0.02)
    v_ln_b = nrm(ks[12], (DEPTH, SGU_WIDTH), 0.02)
    w_spatial = nrm(ks[13], (DEPTH, SGU_GROUPS, SGU_CHUNK, SGU_CHUNK), SGU_CHUNK ** -0.5)
    b_spatial = 1.0 + nrm(ks[14], (DEPTH, SGU_GROUPS, SGU_CHUNK), 0.1)
    w_branch = jnp.concatenate(
        [nrm(ks[15], (DEPTH, SSD_D_INNER, D_MODEL), SSD_D_INNER ** -0.5),
         nrm(ks[16], (DEPTH, SGU_WIDTH, D_MODEL), SGU_WIDTH ** -0.5)], axis=1)
    w_out = nrm(ks[17], (DEPTH, D_MODEL, D_MODEL), D_MODEL ** -0.5)
    norm2_w = 1.0 + nrm(ks[18], (DEPTH, D_MODEL), 0.02)
    w_up = nrm(ks[19], (DEPTH, D_MODEL, 2 * D_FF), D_MODEL ** -0.5)
    conv_f_w = nrm(ks[20], (DEPTH, FFN_CONV, 2 * D_FF), FFN_CONV ** -0.5)
    conv_f_b = nrm(ks[21], (DEPTH, 2 * D_FF), 0.02)
    w_down = nrm(ks[22], (DEPTH, D_FF, D_MODEL), D_FF ** -0.5)
    final_norm_w = 1.0 + nrm(ks[23], (D_MODEL,), 0.02)
    return {"x": x, "norm1_w": norm1_w, "w_in": w_in, "b_gate": b_gate,
            "conv_a_w": conv_a_w, "conv_a_b": conv_a_b, "dt_bias": dt_bias,
            "a_log": a_log, "d_skip": d_skip, "ssd_norm_w": ssd_norm_w,
            "uv_b": uv_b, "v_ln_w": v_ln_w, "v_ln_b": v_ln_b,
            "w_spatial": w_spatial, "b_spatial": b_spatial, "w_branch": w_branch,
            "w_out": w_out, "norm2_w": norm2_w, "w_up": w_up, "conv_f_w": conv_f_w,
            "conv_f_b": conv_f_b, "w_down": w_down, "final_norm_w": final_norm_w}


def _fwd_reference(x, norm1_w, w_in, b_gate, conv_a_w, conv_a_b, dt_bias, a_log, d_skip,
              ssd_norm_w, uv_b, v_ln_w, v_ln_b, w_spatial, b_spatial, w_branch, w_out,
              norm2_w, w_up, conv_f_w, conv_f_b, w_down, final_norm_w):
    h = x
    for l in range(DEPTH):
        n = rms_norm(h, norm1_w[l])
        proj = n @ w_in[l]
        z, xbc, dt_raw, uv, gates = jnp.split(
            proj, [SSD_D_INNER, SSD_D_INNER + SSD_XBC, SSD_IN, SSD_IN + SGU_IN], axis=-1)
        y_a = ssd_branch(z, xbc, dt_raw, conv_a_w[l], conv_a_b[l], dt_bias[l],
                         a_log[l], d_skip[l], ssd_norm_w[l])
        y_b = sgu_branch(uv + uv_b[l], v_ln_w[l], v_ln_b[l],
                         w_spatial[l], b_spatial[l])
        g_a, g_b = jnp.split(jax.nn.sigmoid(gates + b_gate[l]), 2, axis=-1)
        p_a = y_a @ w_branch[l, :SSD_D_INNER]
        p_b = y_b @ w_branch[l, SSD_D_INNER:]
        h = h + (g_a * p_a + g_b * p_b) @ w_out[l]
        n = rms_norm(h, norm2_w[l])
        up = causal_dwconv(n @ w_up[l], conv_f_w[l], conv_f_b[l])
        a_up, v_up = jnp.split(up, 2, axis=-1)
        h = h + (jax.nn.silu(a_up) * v_up) @ w_down[l]
    return rms_norm(h, final_norm_w)


import jax as _jax
import jax.numpy as _jnp

TWIN_FORMAT = 'train_step'
FWD_PARAMS = ['x', 'norm1_w', 'w_in', 'b_gate', 'conv_a_w', 'conv_a_b', 'dt_bias', 'a_log', 'd_skip', 'ssd_norm_w', 'uv_b', 'v_ln_w', 'v_ln_b', 'w_spatial', 'b_spatial', 'w_branch', 'w_out', 'norm2_w', 'w_up', 'conv_f_w', 'conv_f_b', 'w_down', 'final_norm_w']
TWIN_WEIGHTS = ['norm1_w', 'w_in', 'b_gate', 'conv_a_w', 'conv_a_b', 'dt_bias', 'a_log', 'd_skip', 'ssd_norm_w', 'uv_b', 'v_ln_w', 'v_ln_b', 'w_spatial', 'b_spatial', 'w_branch', 'w_out', 'norm2_w', 'w_up', 'conv_f_w', 'conv_f_b', 'w_down', 'final_norm_w']
TWIN_DIFF_INPUT = 'x'
TWIN_INPUTS = ['x', 'norm1_w', 'w_in', 'b_gate', 'conv_a_w', 'conv_a_b', 'dt_bias', 'a_log', 'd_skip', 'ssd_norm_w', 'uv_b', 'v_ln_w', 'v_ln_b', 'w_spatial', 'b_spatial', 'w_branch', 'w_out', 'norm2_w', 'w_up', 'conv_f_w', 'conv_f_b', 'w_down', 'final_norm_w', 'loss_target', 'm_norm1_w', 'm_w_in', 'm_b_gate', 'm_conv_a_w', 'm_conv_a_b', 'm_dt_bias', 'm_a_log', 'm_d_skip', 'm_ssd_norm_w', 'm_uv_b', 'm_v_ln_w', 'm_v_ln_b', 'm_w_spatial', 'm_b_spatial', 'm_w_branch', 'm_w_out', 'm_norm2_w', 'm_w_up', 'm_conv_f_w', 'm_conv_f_b', 'm_w_down', 'm_final_norm_w', 'v_norm1_w', 'v_w_in', 'v_b_gate', 'v_conv_a_w', 'v_conv_a_b', 'v_dt_bias', 'v_a_log', 'v_d_skip', 'v_ssd_norm_w', 'v_uv_b', 'v_v_ln_w', 'v_v_ln_b', 'v_w_spatial', 'v_b_spatial', 'v_w_branch', 'v_w_out', 'v_norm2_w', 'v_w_up', 'v_conv_f_w', 'v_conv_f_b', 'v_w_down', 'v_final_norm_w']
TWIN_OUTPUTS = ['loss', 'grad_x', 'grad_norm1_w', 'grad_w_in', 'grad_b_gate', 'grad_conv_a_w', 'grad_conv_a_b', 'grad_dt_bias', 'grad_a_log', 'grad_d_skip', 'grad_ssd_norm_w', 'grad_uv_b', 'grad_v_ln_w', 'grad_v_ln_b', 'grad_w_spatial', 'grad_b_spatial', 'grad_w_branch', 'grad_w_out', 'grad_norm2_w', 'grad_w_up', 'grad_conv_f_w', 'grad_conv_f_b', 'grad_w_down', 'grad_final_norm_w', 'delta_norm1_w', 'delta_w_in', 'delta_b_gate', 'delta_conv_a_w', 'delta_conv_a_b', 'delta_dt_bias', 'delta_a_log', 'delta_d_skip', 'delta_ssd_norm_w', 'delta_uv_b', 'delta_v_ln_w', 'delta_v_ln_b', 'delta_w_spatial', 'delta_b_spatial', 'delta_w_branch', 'delta_w_out', 'delta_norm2_w', 'delta_w_up', 'delta_conv_f_w', 'delta_conv_f_b', 'delta_w_down', 'delta_final_norm_w', 'new_m_norm1_w', 'new_m_w_in', 'new_m_b_gate', 'new_m_conv_a_w', 'new_m_conv_a_b', 'new_m_dt_bias', 'new_m_a_log', 'new_m_d_skip', 'new_m_ssd_norm_w', 'new_m_uv_b', 'new_m_v_ln_w', 'new_m_v_ln_b', 'new_m_w_spatial', 'new_m_b_spatial', 'new_m_w_branch', 'new_m_w_out', 'new_m_norm2_w', 'new_m_w_up', 'new_m_conv_f_w', 'new_m_conv_f_b', 'new_m_w_down', 'new_m_final_norm_w', 'new_v_norm1_w', 'new_v_w_in', 'new_v_b_gate', 'new_v_conv_a_w', 'new_v_conv_a_b', 'new_v_dt_bias', 'new_v_a_log', 'new_v_d_skip', 'new_v_ssd_norm_w', 'new_v_uv_b', 'new_v_v_ln_w', 'new_v_v_ln_b', 'new_v_w_spatial', 'new_v_b_spatial', 'new_v_w_branch', 'new_v_w_out', 'new_v_norm2_w', 'new_v_w_up', 'new_v_conv_f_w', 'new_v_conv_f_b', 'new_v_w_down', 'new_v_final_norm_w']
TWIN_LEAF_KINDS = {'loss': 'loss', 'grad_x': 'grad_x', 'grad_norm1_w': 'grad_w', 'grad_w_in': 'grad_w', 'grad_b_gate': 'grad_w', 'grad_conv_a_w': 'grad_w', 'grad_conv_a_b': 'grad_w', 'grad_dt_bias': 'grad_w', 'grad_a_log': 'grad_w', 'grad_d_skip': 'grad_w', 'grad_ssd_norm_w': 'grad_w', 'grad_uv_b': 'grad_w', 'grad_v_ln_w': 'grad_w', 'grad_v_ln_b': 'grad_w', 'grad_w_spatial': 'grad_w', 'grad_b_spatial': 'grad_w', 'grad_w_branch': 'grad_w', 'grad_w_out': 'grad_w', 'grad_norm2_w': 'grad_w', 'grad_w_up': 'grad_w', 'grad_conv_f_w': 'grad_w', 'grad_conv_f_b': 'grad_w', 'grad_w_down': 'grad_w', 'grad_final_norm_w': 'grad_w', 'delta_norm1_w': 'delta_w', 'delta_w_in': 'delta_w', 'delta_b_gate': 'delta_w', 'delta_conv_a_w': 'delta_w', 'delta_conv_a_b': 'delta_w', 'delta_dt_bias': 'delta_w', 'delta_a_log': 'delta_w', 'delta_d_skip': 'delta_w', 'delta_ssd_norm_w': 'delta_w', 'delta_uv_b': 'delta_w', 'delta_v_ln_w': 'delta_w', 'delta_v_ln_b': 'delta_w', 'delta_w_spatial': 'delta_w', 'delta_b_spatial': 'delta_w', 'delta_w_branch': 'delta_w', 'delta_w_out': 'delta_w', 'delta_norm2_w': 'delta_w', 'delta_w_up': 'delta_w', 'delta_conv_f_w': 'delta_w', 'delta_conv_f_b': 'delta_w', 'delta_w_down': 'delta_w', 'delta_final_norm_w': 'delta_w', 'new_m_norm1_w': 'new_m', 'new_m_w_in': 'new_m', 'new_m_b_gate': 'new_m', 'new_m_conv_a_w': 'new_m', 'new_m_conv_a_b': 'new_m', 'new_m_dt_bias': 'new_m', 'new_m_a_log': 'new_m', 'new_m_d_skip': 'new_m', 'new_m_ssd_norm_w': 'new_m', 'new_m_uv_b': 'new_m', 'new_m_v_ln_w': 'new_m', 'new_m_v_ln_b': 'new_m', 'new_m_w_spatial': 'new_m', 'new_m_b_spatial': 'new_m', 'new_m_w_branch': 'new_m', 'new_m_w_out': 'new_m', 'new_m_norm2_w': 'new_m', 'new_m_w_up': 'new_m', 'new_m_conv_f_w': 'new_m', 'new_m_conv_f_b': 'new_m', 'new_m_w_down': 'new_m', 'new_m_final_norm_w': 'new_m', 'new_v_norm1_w': 'new_v', 'new_v_w_in': 'new_v', 'new_v_b_gate': 'new_v', 'new_v_conv_a_w': 'new_v', 'new_v_conv_a_b': 'new_v', 'new_v_dt_bias': 'new_v', 'new_v_a_log': 'new_v', 'new_v_d_skip': 'new_v', 'new_v_ssd_norm_w': 'new_v', 'new_v_uv_b': 'new_v', 'new_v_v_ln_w': 'new_v', 'new_v_v_ln_b': 'new_v', 'new_v_w_spatial': 'new_v', 'new_v_b_spatial': 'new_v', 'new_v_w_branch': 'new_v', 'new_v_w_out': 'new_v', 'new_v_norm2_w': 'new_v', 'new_v_w_up': 'new_v', 'new_v_conv_f_w': 'new_v', 'new_v_conv_f_b': 'new_v', 'new_v_w_down': 'new_v', 'new_v_final_norm_w': 'new_v'}


def _forward(args):
    return _fwd_reference(*[args[k] for k in FWD_PARAMS])


def _output_shape():
    def fwd():
        inp = _fwd_setup_inputs(0)
        return _fwd_reference(*[inp[k] for k in FWD_PARAMS])
    out = _jax.eval_shape(fwd)
    return out.shape, out.dtype

N_MICROBATCH = 1
ADAM_LR = 0.001
ADAM_B1 = 0.9
ADAM_B2 = 0.999
ADAM_EPS = 1e-08
ADAM_WD = 0.01
ADAM_STEP = 10
PER_EXAMPLE_BATCH_AXIS = {'x': 0, 'loss_target': 0}
SHARED_INPUTS = []
_WEIGHT_DTYPES = {'norm1_w': _jnp.float32, 'w_in': _jnp.float32, 'b_gate': _jnp.float32, 'conv_a_w': _jnp.float32, 'conv_a_b': _jnp.float32, 'dt_bias': _jnp.float32, 'a_log': _jnp.float32, 'd_skip': _jnp.float32, 'ssd_norm_w': _jnp.float32, 'uv_b': _jnp.float32, 'v_ln_w': _jnp.float32, 'v_ln_b': _jnp.float32, 'w_spatial': _jnp.float32, 'b_spatial': _jnp.float32, 'w_branch': _jnp.float32, 'w_out': _jnp.float32, 'norm2_w': _jnp.float32, 'w_up': _jnp.float32, 'conv_f_w': _jnp.float32, 'conv_f_b': _jnp.float32, 'w_down': _jnp.float32, 'final_norm_w': _jnp.float32}
MOMENT_SCALE = {'norm1_w': 1.672058e-01, 'w_in': 5.432555e-02, 'b_gate': 3.123164e-02, 'conv_a_w': 5.312897e-02, 'conv_a_b': 7.574796e-02, 'dt_bias': 3.271498e-01, 'a_log': 2.290641e-01, 'd_skip': 3.833600e-01, 'ssd_norm_w': 6.519456e-02, 'uv_b': 6.161286e-02, 'v_ln_w': 4.113394e-02, 'v_ln_b': 3.943693e-02, 'w_spatial': 3.969078e-02, 'b_spatial': 5.590284e-02, 'w_branch': 8.203279e-02, 'w_out': 1.128322e-01, 'norm2_w': 1.136933e-01, 'w_up': 4.845221e-02, 'conv_f_w': 4.977002e-02, 'conv_f_b': 4.847135e-02, 'w_down': 7.915762e-02, 'final_norm_w': 3.201366e+01}


def _to_microbatches(a, axis):
    t = _jnp.moveaxis(a, axis, 0)
    t = t.reshape((N_MICROBATCH, t.shape[0] // N_MICROBATCH) + t.shape[1:])
    return _jnp.moveaxis(t, 1, axis + 1)


def setup_inputs(seed: int = 0) -> dict:
    inp = _fwd_setup_inputs(seed)
    key = _jax.random.fold_in(_jax.random.key(seed), 7919)
    shape, _ = _output_shape()
    out = dict(inp)
    out["loss_target"] = _jax.random.normal(_jax.random.fold_in(key, 0), shape, _jnp.float32)
    for i, name in enumerate(TWIN_WEIGHTS):
        w = inp[name].astype(_jnp.float32)
        if MOMENT_SCALE is None:
            s = _jnp.sqrt(_jnp.mean(_jnp.square(w)) + 1e-30)
        else:
            s = MOMENT_SCALE[name]
        km, kv = _jax.random.split(_jax.random.fold_in(key, i + 1))
        out[name] = w
        out["m_" + name] = s * _jax.random.normal(km, w.shape, _jnp.float32)
        out["v_" + name] = (s * s) * _jax.random.uniform(kv, w.shape, _jnp.float32, 0.5, 1.5)
    if N_MICROBATCH > 1:
        for name, axis in PER_EXAMPLE_BATCH_AXIS.items():
            out[name] = _to_microbatches(out[name], axis)
    return {'x': out['x'], 'norm1_w': out['norm1_w'], 'w_in': out['w_in'], 'b_gate': out['b_gate'], 'conv_a_w': out['conv_a_w'], 'conv_a_b': out['conv_a_b'], 'dt_bias': out['dt_bias'], 'a_log': out['a_log'], 'd_skip': out['d_skip'], 'ssd_norm_w': out['ssd_norm_w'], 'uv_b': out['uv_b'], 'v_ln_w': out['v_ln_w'], 'v_ln_b': out['v_ln_b'], 'w_spatial': out['w_spatial'], 'b_spatial': out['b_spatial'], 'w_branch': out['w_branch'], 'w_out': out['w_out'], 'norm2_w': out['norm2_w'], 'w_up': out['w_up'], 'conv_f_w': out['conv_f_w'], 'conv_f_b': out['conv_f_b'], 'w_down': out['w_down'], 'final_norm_w': out['final_norm_w'], 'loss_target': out['loss_target'], 'm_norm1_w': out['m_norm1_w'], 'm_w_in': out['m_w_in'], 'm_b_gate': out['m_b_gate'], 'm_conv_a_w': out['m_conv_a_w'], 'm_conv_a_b': out['m_conv_a_b'], 'm_dt_bias': out['m_dt_bias'], 'm_a_log': out['m_a_log'], 'm_d_skip': out['m_d_skip'], 'm_ssd_norm_w': out['m_ssd_norm_w'], 'm_uv_b': out['m_uv_b'], 'm_v_ln_w': out['m_v_ln_w'], 'm_v_ln_b': out['m_v_ln_b'], 'm_w_spatial': out['m_w_spatial'], 'm_b_spatial': out['m_b_spatial'], 'm_w_branch': out['m_w_branch'], 'm_w_out': out['m_w_out'], 'm_norm2_w': out['m_norm2_w'], 'm_w_up': out['m_w_up'], 'm_conv_f_w': out['m_conv_f_w'], 'm_conv_f_b': out['m_conv_f_b'], 'm_w_down': out['m_w_down'], 'm_final_norm_w': out['m_final_norm_w'], 'v_norm1_w': out['v_norm1_w'], 'v_w_in': out['v_w_in'], 'v_b_gate': out['v_b_gate'], 'v_conv_a_w': out['v_conv_a_w'], 'v_conv_a_b': out['v_conv_a_b'], 'v_dt_bias': out['v_dt_bias'], 'v_a_log': out['v_a_log'], 'v_d_skip': out['v_d_skip'], 'v_ssd_norm_w': out['v_ssd_norm_w'], 'v_uv_b': out['v_uv_b'], 'v_v_ln_w': out['v_v_ln_w'], 'v_v_ln_b': out['v_v_ln_b'], 'v_w_spatial': out['v_w_spatial'], 'v_b_spatial': out['v_b_spatial'], 'v_w_branch': out['v_w_branch'], 'v_w_out': out['v_w_out'], 'v_norm2_w': out['v_norm2_w'], 'v_w_up': out['v_w_up'], 'v_conv_f_w': out['v_conv_f_w'], 'v_conv_f_b': out['v_conv_f_b'], 'v_w_down': out['v_w_down'], 'v_final_norm_w': out['v_final_norm_w']}


def _loss(weights, diff, rest, loss_target):
    with _jax.named_scope("forward"):
        args = {**rest, TWIN_DIFF_INPUT: diff, **{k: w.astype(_WEIGHT_DTYPES[k]) for k, w in weights.items()}}
        y = _forward(args)
    with _jax.named_scope("loss_head"):
        err = _jnp.square(y.astype(_jnp.float32) - loss_target)
        return 0.5 * _jnp.sum(_jnp.mean(err, axis=-1)) if err.ndim else 0.5 * err


def _adamw(w, g, m, v):
    m = ADAM_B1 * m + (1.0 - ADAM_B1) * g
    v = ADAM_B2 * v + (1.0 - ADAM_B2) * _jnp.square(g)
    m_hat = m / (1.0 - ADAM_B1 ** ADAM_STEP)
    v_hat = v / (1.0 - ADAM_B2 ** ADAM_STEP)
    delta = -ADAM_LR * (m_hat / (_jnp.sqrt(v_hat) + ADAM_EPS) + ADAM_WD * w)
    return delta, m, v


def reference(x, norm1_w, w_in, b_gate, conv_a_w, conv_a_b, dt_bias, a_log, d_skip, ssd_norm_w, uv_b, v_ln_w, v_ln_b, w_spatial, b_spatial, w_branch, w_out, norm2_w, w_up, conv_f_w, conv_f_b, w_down, final_norm_w, loss_target, m_norm1_w, m_w_in, m_b_gate, m_conv_a_w, m_conv_a_b, m_dt_bias, m_a_log, m_d_skip, m_ssd_norm_w, m_uv_b, m_v_ln_w, m_v_ln_b, m_w_spatial, m_b_spatial, m_w_branch, m_w_out, m_norm2_w, m_w_up, m_conv_f_w, m_conv_f_b, m_w_down, m_final_norm_w, v_norm1_w, v_w_in, v_b_gate, v_conv_a_w, v_conv_a_b, v_dt_bias, v_a_log, v_d_skip, v_ssd_norm_w, v_uv_b, v_v_ln_w, v_v_ln_b, v_w_spatial, v_b_spatial, v_w_branch, v_w_out, v_norm2_w, v_w_up, v_conv_f_w, v_conv_f_b, v_w_down, v_final_norm_w):
    given = dict(x=x, norm1_w=norm1_w, w_in=w_in, b_gate=b_gate, conv_a_w=conv_a_w, conv_a_b=conv_a_b, dt_bias=dt_bias, a_log=a_log, d_skip=d_skip, ssd_norm_w=ssd_norm_w, uv_b=uv_b, v_ln_w=v_ln_w, v_ln_b=v_ln_b, w_spatial=w_spatial, b_spatial=b_spatial, w_branch=w_branch, w_out=w_out, norm2_w=norm2_w, w_up=w_up, conv_f_w=conv_f_w, conv_f_b=conv_f_b, w_down=w_down, final_norm_w=final_norm_w, loss_target=loss_target, m_norm1_w=m_norm1_w, m_w_in=m_w_in, m_b_gate=m_b_gate, m_conv_a_w=m_conv_a_w, m_conv_a_b=m_conv_a_b, m_dt_bias=m_dt_bias, m_a_log=m_a_log, m_d_skip=m_d_skip, m_ssd_norm_w=m_ssd_norm_w, m_uv_b=m_uv_b, m_v_ln_w=m_v_ln_w, m_v_ln_b=m_v_ln_b, m_w_spatial=m_w_spatial, m_b_spatial=m_b_spatial, m_w_branch=m_w_branch, m_w_out=m_w_out, m_norm2_w=m_norm2_w, m_w_up=m_w_up, m_conv_f_w=m_conv_f_w, m_conv_f_b=m_conv_f_b, m_w_down=m_w_down, m_final_norm_w=m_final_norm_w, v_norm1_w=v_norm1_w, v_w_in=v_w_in, v_b_gate=v_b_gate, v_conv_a_w=v_conv_a_w, v_conv_a_b=v_conv_a_b, v_dt_bias=v_dt_bias, v_a_log=v_a_log, v_d_skip=v_d_skip, v_ssd_norm_w=v_ssd_norm_w, v_uv_b=v_uv_b, v_v_ln_w=v_v_ln_w, v_v_ln_b=v_v_ln_b, v_w_spatial=v_w_spatial, v_b_spatial=v_b_spatial, v_w_branch=v_w_branch, v_w_out=v_w_out, v_norm2_w=v_norm2_w, v_w_up=v_w_up, v_conv_f_w=v_conv_f_w, v_conv_f_b=v_conv_f_b, v_w_down=v_w_down, v_final_norm_w=v_final_norm_w)
    weights = {n: given[n] for n in TWIN_WEIGHTS}
    shared = {n: given[n] for n in SHARED_INPUTS}
    per_example = {n: given[n] for n in ['x']}
    grad_fn = _jax.value_and_grad(_loss, argnums=(0, 1))

    def one_microbatch(ex, loss_target):
        ex = dict(ex)
        diff = ex.pop(TWIN_DIFF_INPUT)
        return grad_fn(weights, diff, {**shared, **ex}, loss_target)

    if N_MICROBATCH == 1:
        loss, (grad_w, grad_x) = one_microbatch(per_example, given["loss_target"])
    else:
        def body(carry, xs):
            loss_sum, grad_sum = carry
            l_k, (gw_k, gx_k) = one_microbatch(xs[0], xs[1])
            with _jax.named_scope("update"):
                return (loss_sum + l_k, _jax.tree.map(_jnp.add, grad_sum, gw_k)), gx_k

        init = (_jnp.zeros((), _jnp.float32), _jax.tree.map(_jnp.zeros_like, weights))
        (loss, grad_w), grad_x = _jax.lax.scan(body, init, (per_example, given["loss_target"]))
    with _jax.named_scope("update"):
        delta_w, new_m, new_v = {}, {}, {}
        for n in TWIN_WEIGHTS:
            delta_w[n], new_m[n], new_v[n] = _adamw(weights[n], grad_w[n], given["m_" + n], given["v_" + n])
    return (loss, grad_x, *[grad_w[n] for n in TWIN_WEIGHTS], *[delta_w[n] for n in TWIN_WEIGHTS],
            *[new_m[n] for n in TWIN_WEIGHTS], *[new_v[n] for n in TWIN_WEIGHTS])
```

```python
import functools
import math

import jax
import jax.numpy as jnp
from jax import lax
from jax.experimental import pallas as pl
from jax.experimental.pallas import tpu as pltpu

F32 = jnp.float32
BF16 = jnp.bfloat16
HI = lax.Precision.HIGHEST

D_MODEL = 1024
SSD_D_INNER = 2048
SSD_HEADS = 32
SSD_HEAD_DIM = 64
SSD_GROUPS = 4
SSD_HEADS_PER_GROUP = 8
SSD_STATE = 128
SSD_BC = 512
SSD_XBC = 3072
SSD_IN = 5152
SGU_WIDTH = 1024
SGU_GROUPS = 8
CHUNK = 128
IN_COLS = 9248
D_FF = 2816
NORM_EPS = 1e-6
LN_EPS = 1e-5
GROUP_COLS = SSD_HEADS_PER_GROUP * SSD_HEAD_DIM
LANES = 128

ADAM_LR = 0.001
ADAM_B1 = 0.9
ADAM_B2 = 0.999
ADAM_EPS = 1e-08
ADAM_WD = 0.01
ADAM_STEP = 10

N_CHIPS = 4
VMEM_LIMIT = 56 * 1024 * 1024

NT = (((1,), (1,)), ((), ()))
TN = (((0,), (0,)), ((), ()))
NN = (((1,), (0,)), ((), ()))


def _params(dims):
    return pltpu.CompilerParams(dimension_semantics=dims, vmem_limit_bytes=VMEM_LIMIT)


def _dot(a, b, dn=NN, precision=None):
    return lax.dot_general(a, b, dn, precision=precision, preferred_element_type=F32)


def _sigmoid(x):
    return 1.0 / (1.0 + jnp.exp(-x))


def _softplus(x):
    return jnp.maximum(x, 0.0) + jnp.log(1.0 + jnp.exp(-jnp.abs(x)))


def _matmul(pairs, *, trans_b=False, add=None, out_dtype=F32, tm=512, tn=512, name):
    m = pairs[0][0].shape[0]
    n = pairs[0][1].shape[0] if trans_b else pairs[0][1].shape[1]
    tm, tn = min(tm, m), min(tn, n)
    assert m % tm == 0 and n % tn == 0, (name, m, n, tm, tn)
    npairs = len(pairs)
    dn = NT if trans_b else NN

    def body(*refs):
        o_ref = refs[-1]
        acc = None
        for i in range(npairs):
            p = _dot(refs[2 * i][...].astype(BF16), refs[2 * i + 1][...].astype(BF16), dn)
            acc = p if acc is None else acc + p
        if add is not None:
            acc = acc + refs[2 * npairs][...]
        o_ref[...] = acc.astype(out_dtype)

    in_specs, args = [], []
    for a, b in pairs:
        k = a.shape[1]
        in_specs.append(pl.BlockSpec((tm, k), lambda i, j: (i, 0)))
        if trans_b:
            assert b.shape == (n, k), (name, a.shape, b.shape)
            in_specs.append(pl.BlockSpec((tn, k), lambda i, j: (j, 0)))
        else:
            assert b.shape == (k, n), (name, a.shape, b.shape)
            in_specs.append(pl.BlockSpec((k, tn), lambda i, j: (0, j)))
        args += [a, b]
    if add is not None:
        in_specs.append(pl.BlockSpec((tm, tn), lambda i, j: (i, j)))
        args.append(add)
    return pl.pallas_call(
        body, name=name, grid=(m // tm, n // tn), in_specs=in_specs,
        out_specs=pl.BlockSpec((tm, tn), lambda i, j: (i, j)),
        out_shape=jax.ShapeDtypeStruct((m, n), out_dtype),
        compiler_params=_params(("parallel", "parallel")),
    )(*args)


def _matmul_tn(a, b, *, tk, tn, tm=1024, out_dtype=BF16, name):
    m, k = a.shape
    n = b.shape[1]
    tm, tk, tn = min(tm, m), min(tk, k), min(tn, n)
    assert m % tm == 0 and k % tk == 0 and n % tn == 0, (name, m, k, n)
    nm = m // tm

    def body(a_ref, b_ref, o_ref, acc):
        mi = pl.program_id(2)

        @pl.when(mi == 0)
        def _():
            acc[...] = jnp.zeros_like(acc)

        acc[...] += _dot(a_ref[...].astype(BF16), b_ref[...].astype(BF16), TN)

        @pl.when(mi == nm - 1)
        def _():
            o_ref[...] = acc[...].astype(out_dtype)

    return pl.pallas_call(
        body, name=name, grid=(k // tk, n // tn, nm),
        in_specs=[pl.BlockSpec((tm, tk), lambda i, j, l: (l, i)), pl.BlockSpec((tm, tn), lambda i, j, l: (l, j))],
        out_specs=pl.BlockSpec((tk, tn), lambda i, j, l: (i, j)),
        out_shape=jax.ShapeDtypeStruct((k, n), out_dtype),
        scratch_shapes=[pltpu.VMEM((tk, tn), F32)],
        compiler_params=_params(("parallel", "parallel", "arbitrary")),
    )(a, b)


def _rms_fwd(x, w, *, name, tm=512):
    s, d = x.shape
    tm = min(tm, s)

    def body(x_ref, w_ref, o_ref):
        xv = x_ref[...]
        r = lax.rsqrt(jnp.mean(xv * xv, axis=-1, keepdims=True) + NORM_EPS)
        o_ref[...] = (xv * r * w_ref[...]).astype(BF16)

    return pl.pallas_call(
        body, name=name, grid=(s // tm,),
        in_specs=[pl.BlockSpec((tm, d), lambda i: (i, 0)), pl.BlockSpec((1, d), lambda i: (0, 0))],
        out_specs=pl.BlockSpec((tm, d), lambda i: (i, 0)),
        out_shape=jax.ShapeDtypeStruct((s, d), BF16),
        compiler_params=_params(("parallel",)),
    )(x, w)


def _rms_bwd(x, w, dn, dres, *, name, tm=512):
    s, d = x.shape
    tm = min(tm, s)

    def body(x_ref, w_ref, dn_ref, dres_ref, dx_ref, dxb_ref, dw_ref):
        @pl.when(pl.program_id(0) == 0)
        def _():
            dw_ref[...] = jnp.zeros_like(dw_ref)

        xv = x_ref[...]
        r = lax.rsqrt(jnp.mean(xv * xv, axis=-1, keepdims=True) + NORM_EPS)
        xhat = xv * r
        dnv = dn_ref[...]
        dxhat = dnv * w_ref[...]
        dx = dres_ref[...] + r * (dxhat - xhat * jnp.mean(dxhat * xhat, axis=-1, keepdims=True))
        dx_ref[...] = dx
        dxb_ref[...] = dx.astype(BF16)
        dw_ref[...] += jnp.sum(dnv * xhat, axis=0, keepdims=True)

    tile = pl.BlockSpec((tm, d), lambda i: (i, 0))
    row = pl.BlockSpec((1, d), lambda i: (0, 0))
    return pl.pallas_call(
        body, name=name, grid=(s // tm,),
        in_specs=[tile, row, tile, tile], out_specs=[tile, tile, row],
        out_shape=[jax.ShapeDtypeStruct((s, d), F32), jax.ShapeDtypeStruct((s, d), BF16),
                   jax.ShapeDtypeStruct((1, d), F32)],
        compiler_params=_params(("arbitrary",)),
    )(x, w, dn, dres)


def _final_fwd_bwd(h2, wf, target, *, name, tm=512):
    s, d = h2.shape
    tm = min(tm, s)

    def body(h_ref, w_ref, t_ref, loss_ref, dh_ref, dhb_ref, dw_ref):
        @pl.when(pl.program_id(0) == 0)
        def _():
            dw_ref[...] = jnp.zeros_like(dw_ref)
            loss_ref[...] = jnp.zeros_like(loss_ref)

        hv = h_ref[...]
        r = lax.rsqrt(jnp.mean(hv * hv, axis=-1, keepdims=True) + NORM_EPS)
        xhat = hv * r
        err = xhat * w_ref[...] - t_ref[...]
        per_tok = jnp.mean(err * err, axis=-1, keepdims=True)
        loss_ref[...] += 0.5 * jnp.sum(per_tok, axis=0, keepdims=True)
        dy = err * (1.0 / d)
        dxhat = dy * w_ref[...]
        dh = r * (dxhat - xhat * jnp.mean(dxhat * xhat, axis=-1, keepdims=True))
        dh_ref[...] = dh
        dhb_ref[...] = dh.astype(BF16)
        dw_ref[...] += jnp.sum(dy * xhat, axis=0, keepdims=True)

    tile = pl.BlockSpec((tm, d), lambda i: (i, 0))
    row = pl.BlockSpec((1, d), lambda i: (0, 0))
    return pl.pallas_call(
        body, name=name, grid=(s // tm,),
        in_specs=[tile, row, tile],
        out_specs=[pl.BlockSpec((1, 1), lambda i: (0, 0)), tile, tile, row],
        out_shape=[jax.ShapeDtypeStruct((1, 1), F32), jax.ShapeDtypeStruct((s, d), F32),
                   jax.ShapeDtypeStruct((s, d), BF16), jax.ShapeDtypeStruct((1, d), F32)],
        compiler_params=_params(("arbitrary",)),
    )(h2, wf, target)


def _shift_down(x, k):
    if k == 0:
        return x
    rows = lax.broadcasted_iota(jnp.int32, x.shape, 0)
    return jnp.where(rows >= k, pltpu.roll(x, k, 0), 0.0)


def _shift_up(x, k):
    if k == 0:
        return x
    s = x.shape[0]
    rows = lax.broadcasted_iota(jnp.int32, x.shape, 0)
    return jnp.where(rows < s - k, pltpu.roll(x, s - k, 0), 0.0)


def _conv_taps(x, w_ref, kk):
    acc = None
    for i in range(kk):
        term = w_ref[i:i + 1, :] * _shift_down(x, kk - 1 - i)
        acc = term if acc is None else acc + term
    return acc


def _conv_a_fwd(xraw, w, b, *, name, tc=128):
    s, c = xraw.shape
    kk = 4

    def body(x_ref, w_ref, b_ref, o_ref):
        pre = _conv_taps(x_ref[...], w_ref, kk) + b_ref[...]
        o_ref[...] = pre * _sigmoid(pre)

    col = pl.BlockSpec((s, tc), lambda j: (0, j))
    return pl.pallas_call(
        body, name=name, grid=(c // tc,),
        in_specs=[col, pl.BlockSpec((8, tc), lambda j: (0, j)), pl.BlockSpec((1, tc), lambda j: (0, j))],
        out_specs=col, out_shape=jax.ShapeDtypeStruct((s, c), F32),
        compiler_params=_params(("parallel",)),
    )(xraw, w, b)


def _conv_a_bwd(xraw, w, b, dy, *, name, tc=128):
    s, c = xraw.shape
    kk = 4

    def body(x_ref, w_ref, b_ref, dy_ref, dx_ref, dw_ref, db_ref):
        x = x_ref[...]
        pre = _conv_taps(x, w_ref, kk) + b_ref[...]
        sg = _sigmoid(pre)
        dpre = dy_ref[...] * (sg * (1.0 + pre * (1.0 - sg)))
        db_ref[...] = jnp.sum(dpre, axis=0, keepdims=True)
        dx = None
        rows = []
        for i in range(kk):
            rows.append(jnp.sum(dpre * _shift_down(x, kk - 1 - i), axis=0, keepdims=True))
            term = w_ref[i:i + 1, :] * _shift_up(dpre, kk - 1 - i)
            dx = term if dx is None else dx + term
        dx_ref[...] = dx.astype(BF16)
        dw_ref[...] = jnp.concatenate(rows + [jnp.zeros((8 - kk, x.shape[1]), F32)], axis=0)

    col = pl.BlockSpec((s, tc), lambda j: (0, j))
    w8 = pl.BlockSpec((8, tc), lambda j: (0, j))
    row = pl.BlockSpec((1, tc), lambda j: (0, j))
    return pl.pallas_call(
        body, name=name, grid=(c // tc,),
        in_specs=[col, w8, row, col], out_specs=[col, w8, row],
        out_shape=[jax.ShapeDtypeStruct((s, c), BF16), jax.ShapeDtypeStruct((8, c), F32),
                   jax.ShapeDtypeStruct((1, c), F32)],
        compiler_params=_params(("parallel",)),
    )(xraw, w, b, dy)


def _conv_f_fwd(up_raw, w, b, *, name, tc=128):
    s, c2 = up_raw.shape
    c = c2 // 2
    nb = c // tc
    kk = 3

    def body(xa_ref, xv_ref, wa_ref, wv_ref, ba_ref, bv_ref, o_ref):
        a = _conv_taps(xa_ref[...], wa_ref, kk) + ba_ref[...]
        v = _conv_taps(xv_ref[...], wv_ref, kk) + bv_ref[...]
        o_ref[...] = (a * _sigmoid(a) * v).astype(BF16)

    col_a = pl.BlockSpec((s, tc), lambda j: (0, j))
    col_v = pl.BlockSpec((s, tc), lambda j: (0, j + nb))
    return pl.pallas_call(
        body, name=name, grid=(nb,),
        in_specs=[col_a, col_v, pl.BlockSpec((8, tc), lambda j: (0, j)), pl.BlockSpec((8, tc), lambda j: (0, j + nb)),
                  pl.BlockSpec((1, tc), lambda j: (0, j)), pl.BlockSpec((1, tc), lambda j: (0, j + nb))],
        out_specs=col_a, out_shape=jax.ShapeDtypeStruct((s, c), BF16),
        compiler_params=_params(("parallel",)),
    )(up_raw, up_raw, w, w, b, b)


def _conv_f_bwd(up_raw, w, b, dact, *, name, tc=128):
    s, c2 = up_raw.shape
    c = c2 // 2
    nb = c // tc
    kk = 3

    def body(xa_ref, xv_ref, wa_ref, wv_ref, ba_ref, bv_ref, d_ref,
             dxa_ref, dxv_ref, dwa_ref, dwv_ref, dba_ref, dbv_ref):
        xa, xv = xa_ref[...], xv_ref[...]
        a = _conv_taps(xa, wa_ref, kk) + ba_ref[...]
        v = _conv_taps(xv, wv_ref, kk) + bv_ref[...]
        sg = _sigmoid(a)
        d = d_ref[...]
        da = d * v * (sg * (1.0 + a * (1.0 - sg)))
        dv = d * (a * sg)
        for x, dp, w_ref, dx_ref, dw_ref, db_ref in ((xa, da, wa_ref, dxa_ref, dwa_ref, dba_ref),
                                                     (xv, dv, wv_ref, dxv_ref, dwv_ref, dbv_ref)):
            db_ref[...] = jnp.sum(dp, axis=0, keepdims=True)
            dx = None
            rows = []
            for i in range(kk):
                rows.append(jnp.sum(dp * _shift_down(x, kk - 1 - i), axis=0, keepdims=True))
                term = w_ref[i:i + 1, :] * _shift_up(dp, kk - 1 - i)
                dx = term if dx is None else dx + term
            dx_ref[...] = dx.astype(BF16)
            dw_ref[...] = jnp.concatenate(rows + [jnp.zeros((8 - kk, x.shape[1]), F32)], axis=0)

    col_a = pl.BlockSpec((s, tc), lambda j: (0, j))
    col_v = pl.BlockSpec((s, tc), lambda j: (0, j + nb))
    w_a = pl.BlockSpec((8, tc), lambda j: (0, j))
    w_v = pl.BlockSpec((8, tc), lambda j: (0, j + nb))
    r_a = pl.BlockSpec((1, tc), lambda j: (0, j))
    r_v = pl.BlockSpec((1, tc), lambda j: (0, j + nb))
    outs = pl.pallas_call(
        body, name=name, grid=(nb,),
        in_specs=[col_a, col_v, w_a, w_v, r_a, r_v, col_a],
        out_specs=[col_a, col_a, w_a, w_a, r_a, r_a],
        out_shape=[jax.ShapeDtypeStruct((s, c), BF16), jax.ShapeDtypeStruct((s, c), BF16),
                   jax.ShapeDtypeStruct((8, c), F32), jax.ShapeDtypeStruct((8, c), F32),
                   jax.ShapeDtypeStruct((1, c), F32), jax.ShapeDtypeStruct((1, c), F32)],
        compiler_params=_params(("parallel",)),
    )(up_raw, up_raw, w, w, b, b, dact)
    return outs


def _tri_masks():
    row = lax.broadcasted_iota(jnp.int32, (CHUNK, CHUNK), 0)
    col = lax.broadcasted_iota(jnp.int32, (CHUNK, CHUNK), 1)
    return row >= col, row <= col


def _ssd_fwd(xbc, dt_raw, z, dt_bias, a_log, a_log_x, d_skip_x, norm_w, expand, *, name):
    s = xbc.shape[0]
    nc = s // CHUNK

    def body(xbc_ref, dtr_ref, z_ref, dtb_ref, alog_ref, alogx_ref, dskx_ref, nw_ref, e_ref,
             y_ref, ya_ref, st_ref, state):
        @pl.when(pl.program_id(0) == 0)
        def _():
            state[...] = jnp.zeros_like(state)

        st_ref[0] = state[...]
        lower, _ = _tri_masks()
        tril = lower.astype(F32)
        dt = _softplus(dtr_ref[...] + dtb_ref[...])
        adt = dt * (-jnp.exp(alog_ref[...]))
        acum = _dot(tril, adt, precision=HI)
        acum_t = acum.T
        for g in range(SSD_GROUPS):
            sl = slice(GROUP_COLS * g, GROUP_COLS * (g + 1))
            dt_x = _dot(dt, e_ref[:, sl], precision=HI)
            adt_x = dt_x * (-jnp.exp(alogx_ref[:, sl]))
            acum_x = _dot(tril, adt_x, precision=HI)
            tot_x = jnp.sum(adt_x, axis=0, keepdims=True)
            xs = xbc_ref[:, sl]
            xdt = xs * dt_x
            xdt_b = xdt.astype(BF16)
            bg = xbc_ref[:, SSD_D_INNER + SSD_STATE * g:SSD_D_INNER + SSD_STATE * (g + 1)].astype(BF16)
            cg = xbc_ref[:, SSD_D_INNER + SSD_BC + SSD_STATE * g:SSD_D_INNER + SSD_BC + SSD_STATE * (g + 1)].astype(BF16)
            cb = _dot(cg, bg, NT)
            st_g = state[:, sl]
            y_off = _dot(cg, st_g.astype(BF16)) * jnp.exp(acum_x)
            parts = []
            for r in range(SSD_HEADS_PER_GROUP):
                h = SSD_HEADS_PER_GROUP * g + r
                dec = jnp.exp(jnp.where(lower, acum[:, h:h + 1] - acum_t[h:h + 1, :], -jnp.inf))
                parts.append(_dot((cb * dec).astype(BF16), xdt_b[:, SSD_HEAD_DIM * r:SSD_HEAD_DIM * (r + 1)]))
            y_ref[:, sl] = jnp.concatenate(parts, axis=1) + y_off + dskx_ref[:, sl] * xs
            wgt = (xdt * jnp.exp(tot_x - acum_x)).astype(BF16)
            state[:, sl] = st_g * jnp.exp(tot_x) + _dot(bg, wgt, TN)
        zv = z_ref[...]
        q = y_ref[...] * (zv * _sigmoid(zv))
        r = lax.rsqrt(jnp.mean(q * q, axis=-1, keepdims=True) + NORM_EPS)
        ya_ref[...] = (q * r * nw_ref[...]).astype(BF16)

    def chunk(w):
        return pl.BlockSpec((CHUNK, w), lambda c: (c, 0))

    def const(shape):
        return pl.BlockSpec(shape, lambda c: (0,) * len(shape))

    return pl.pallas_call(
        body, name=name, grid=(nc,),
        in_specs=[chunk(SSD_XBC), chunk(LANES), chunk(SSD_D_INNER), const((1, LANES)), const((1, LANES)),
                  const((1, SSD_D_INNER)), const((1, SSD_D_INNER)), const((1, SSD_D_INNER)),
                  const((LANES, SSD_D_INNER))],
        out_specs=[chunk(SSD_D_INNER), chunk(SSD_D_INNER),
                   pl.BlockSpec((1, SSD_STATE, SSD_D_INNER), lambda c: (c, 0, 0))],
        out_shape=[jax.ShapeDtypeStruct((s, SSD_D_INNER), F32), jax.ShapeDtypeStruct((s, SSD_D_INNER), BF16),
                   jax.ShapeDtypeStruct((nc, SSD_STATE, SSD_D_INNER), F32)],
        scratch_shapes=[pltpu.VMEM((SSD_STATE, SSD_D_INNER), F32)],
        compiler_params=_params(("arbitrary",)),
    )(xbc, dt_raw, z, dt_bias, a_log, a_log_x, d_skip_x, norm_w, expand)


def _ssd_bwd(dya, y, z, xbc, dt_raw, states, dt_bias, a_log, a_log_x, d_skip_x, norm_w, expand, expand_t, *, name):
    s = xbc.shape[0]
    nc = s // CHUNK

    def body(dya_ref, y_ref, z_ref, xbc_ref, dtr_ref, stp_ref, dtb_ref, alog_ref, alogx_ref, dskx_ref, nw_ref,
             e_ref, et_ref, dz_ref, dxbc_ref, ddt_ref, dnw_ref, ddsk_ref, dalog_ref, ddtb_ref,
             dstate, dy_sc, dskcol):
        i = pl.program_id(0)

        @pl.when(i == 0)
        def _():
            dstate[...] = jnp.zeros_like(dstate)
            dskcol[...] = jnp.zeros_like(dskcol)
            dnw_ref[...] = jnp.zeros_like(dnw_ref)
            dalog_ref[...] = jnp.zeros_like(dalog_ref)
            ddtb_ref[...] = jnp.zeros_like(ddtb_ref)
            ddsk_ref[...] = jnp.zeros_like(ddsk_ref)

        lower, upper = _tri_masks()
        tril = lower.astype(F32)
        rows = lax.broadcasted_iota(jnp.int32, (CHUNK, LANES), 0)
        pre = dtr_ref[...] + dtb_ref[...]
        dt = _softplus(pre)
        a = -jnp.exp(alog_ref[...])
        adt = dt * a
        acum = _dot(tril, adt, precision=HI)
        acum_t = acum.T

        yv = y_ref[...]
        zv = z_ref[...]
        sz = _sigmoid(zv)
        silu_z = zv * sz
        q = yv * silu_z
        r = lax.rsqrt(jnp.mean(q * q, axis=-1, keepdims=True) + NORM_EPS)
        qhat = q * r
        dyav = dya_ref[...]
        dqhat = dyav * nw_ref[...]
        dnw_ref[...] += jnp.sum(dyav * qhat, axis=0, keepdims=True)
        dq = r * (dqhat - qhat * jnp.mean(dqhat * qhat, axis=-1, keepdims=True))
        dy_sc[...] = dq * silu_z
        dz_ref[...] = (dq * yv * (sz * (1.0 + zv * (1.0 - sz)))).astype(BF16)

        da_cum = jnp.zeros((CHUNK, LANES), F32)
        ddt = jnp.zeros((CHUNK, LANES), F32)
        for g in range(SSD_GROUPS):
            sl = slice(GROUP_COLS * g, GROUP_COLS * (g + 1))
            et_g = et_ref[sl, :]
            dt_x = _dot(dt, e_ref[:, sl], precision=HI)
            adt_x = dt_x * (-jnp.exp(alogx_ref[:, sl]))
            acum_x = _dot(tril, adt_x, precision=HI)
            tot_x = jnp.sum(adt_x, axis=0, keepdims=True)
            e_tot = jnp.exp(tot_x)
            dec_s = jnp.exp(tot_x - acum_x)
            xs = xbc_ref[:, sl]
            xdt = xs * dt_x
            xdt_b = xdt.astype(BF16)
            dy = dy_sc[:, sl]
            dy_b = dy.astype(BF16)
            dskx = dskx_ref[:, sl]
            y_ssd = y_ref[:, sl] - dskx * xs
            dskcol[:, sl] += jnp.sum(dy * xs, axis=0, keepdims=True)
            bg = xbc_ref[:, SSD_D_INNER + SSD_STATE * g:SSD_D_INNER + SSD_STATE * (g + 1)].astype(BF16)
            cg = xbc_ref[:, SSD_D_INNER + SSD_BC + SSD_STATE * g:SSD_D_INNER + SSD_BC + SSD_STATE * (g + 1)].astype(BF16)
            cb = _dot(cg, bg, NT)
            sp = stp_ref[0, :, sl]
            ds_g = dstate[:, sl]
            ds_b = ds_g.astype(BF16)
            dye_b = (dy * jnp.exp(acum_x)).astype(BF16)
            dc = _dot(dye_b, sp.astype(BF16), NT)
            dxdt_state = dec_s * _dot(bg, ds_b)
            db = _dot((xdt * dec_s).astype(BF16), ds_b, NT)
            dcb = jnp.zeros((CHUNK, CHUNK), F32)
            parts = []
            for rr in range(SSD_HEADS_PER_GROUP):
                h = SSD_HEADS_PER_GROUP * g + rr
                hs = slice(SSD_HEAD_DIM * rr, SSD_HEAD_DIM * (rr + 1))
                dec = jnp.exp(jnp.where(lower, acum[:, h:h + 1] - acum_t[h:h + 1, :], -jnp.inf))
                parts.append(_dot((cb * dec).astype(BF16), dy_b[:, hs], TN))
                dcb = dcb + _dot(dy_b[:, hs], xdt_b[:, hs], NT) * dec
            dxdt = jnp.concatenate(parts, axis=1) + dxdt_state
            dcb_b = dcb.astype(BF16)
            dc = dc + _dot(dcb_b, bg)
            db = db + _dot(dcb_b, cg, TN)
            tot_col = jnp.sum(ds_g * sp, axis=0, keepdims=True) * e_tot + jnp.sum(dxdt_state * xdt, axis=0, keepdims=True)
            d_tot = _dot(jnp.broadcast_to(tot_col, (8, GROUP_COLS)), et_g, precision=HI)
            d_tot = jnp.max(d_tot, axis=0, keepdims=True)
            pair_sums = dy_b.astype(F32) * y_ssd - xdt_b.astype(F32) * dxdt
            da_cum = da_cum + _dot(pair_sums, et_g, precision=HI) + jnp.where(rows == CHUNK - 1, d_tot, 0.0)
            ddt = ddt + _dot(dxdt * xs, et_g, precision=HI)
            dxbc_ref[:, sl] = dy * dskx + dxdt * dt_x
            dxbc_ref[:, SSD_D_INNER + SSD_STATE * g:SSD_D_INNER + SSD_STATE * (g + 1)] = db
            dxbc_ref[:, SSD_D_INNER + SSD_BC + SSD_STATE * g:SSD_D_INNER + SSD_BC + SSD_STATE * (g + 1)] = dc
            dstate[:, sl] = e_tot * ds_g + _dot(cg, dye_b, TN)

        dadt = _dot(upper.astype(F32), da_cum, precision=HI)
        ddt = ddt + dadt * a
        dalog_ref[...] += jnp.sum(dadt * dt, axis=0, keepdims=True)
        dpre = ddt * _sigmoid(pre)
        ddtb_ref[...] += jnp.sum(dpre, axis=0, keepdims=True)
        ddt_ref[...] = dpre.astype(BF16)

        @pl.when(i == nc - 1)
        def _():
            dalog_ref[...] = dalog_ref[...] * a
            dsk = _dot(jnp.broadcast_to(dskcol[...], (8, SSD_D_INNER)), et_ref[...], precision=HI)
            ddsk_ref[...] = jnp.max(dsk, axis=0, keepdims=True)

    def chunk(w):
        return pl.BlockSpec((CHUNK, w), lambda i: (nc - 1 - i, 0))

    def const(shape):
        return pl.BlockSpec(shape, lambda i: (0,) * len(shape))

    return pl.pallas_call(
        body, name=name, grid=(nc,),
        in_specs=[chunk(SSD_D_INNER), chunk(SSD_D_INNER), chunk(SSD_D_INNER), chunk(SSD_XBC), chunk(LANES),
                  pl.BlockSpec((1, SSD_STATE, SSD_D_INNER), lambda i: (nc - 1 - i, 0, 0)),
                  const((1, LANES)), const((1, LANES)), const((1, SSD_D_INNER)), const((1, SSD_D_INNER)),
                  const((1, SSD_D_INNER)), const((LANES, SSD_D_INNER)), const((SSD_D_INNER, LANES))],
        out_specs=[chunk(SSD_D_INNER), chunk(SSD_XBC), chunk(LANES), const((1, SSD_D_INNER)), const((1, LANES)),
                   const((1, LANES)), const((1, LANES))],
        out_shape=[jax.ShapeDtypeStruct((s, SSD_D_INNER), BF16), jax.ShapeDtypeStruct((s, SSD_XBC), F32),
                   jax.ShapeDtypeStruct((s, LANES), BF16), jax.ShapeDtypeStruct((1, SSD_D_INNER), F32),
                   jax.ShapeDtypeStruct((1, LANES), F32), jax.ShapeDtypeStruct((1, LANES), F32),
                   jax.ShapeDtypeStruct((1, LANES), F32)],
        scratch_shapes=[pltpu.VMEM((SSD_STATE, SSD_D_INNER), F32), pltpu.VMEM((CHUNK, SSD_D_INNER), F32),
                        pltpu.VMEM((1, SSD_D_INNER), F32)],
        compiler_params=_params(("arbitrary",)),
    )(dya, y, z, xbc, dt_raw, states, dt_bias, a_log, a_log_x, d_skip_x, norm_w, expand, expand_t)


GELU_K = math.sqrt(2.0 / math.pi)
GELU_C = 0.044715


def _gelu(x):
    return 0.5 * x * (1.0 + jnp.tanh(GELU_K * (x + GELU_C * x * x * x)))


def _gelu_grad(x):
    t = jnp.tanh(GELU_K * (x + GELU_C * x * x * x))
    return 0.5 * (1.0 + t) + 0.5 * x * (1.0 - t * t) * (GELU_K * (1.0 + 3.0 * GELU_C * x * x))


def _sgu_pre(uv_ref, uvb_ref, lnw_ref, lnb_ref):
    uv = uv_ref[...] + uvb_ref[...]
    guv = _gelu(uv)
    u = guv[:, :SGU_WIDTH]
    v = guv[:, SGU_WIDTH:]
    mu = jnp.mean(v, axis=-1, keepdims=True)
    vc = v - mu
    rstd = lax.rsqrt(jnp.mean(vc * vc, axis=-1, keepdims=True) + LN_EPS)
    vhat = vc * rstd
    vn = vhat * lnw_ref[...] + lnb_ref[...]
    return uv, u, vhat, rstd, vn


def _sgu_fwd(uv_raw, uv_b, ln_w, ln_b, w_sp, b_sp_t, *, name):
    s = uv_raw.shape[0]
    nc = s // CHUNK

    def body(uv_ref, uvb_ref, lnw_ref, lnb_ref, w_ref, bt_ref, o_ref):
        lower, _ = _tri_masks()
        _, u, _, _, vn = _sgu_pre(uv_ref, uvb_ref, lnw_ref, lnb_ref)
        vn_b = vn.astype(BF16)
        bt = bt_ref[...]
        for g in range(SGU_GROUPS):
            gs = slice(LANES * g, LANES * (g + 1))
            wc = jnp.where(lower, w_ref[g], 0.0).astype(BF16)
            mixed = _dot(wc, vn_b[:, gs]) + bt[:, g:g + 1]
            o_ref[:, gs] = (u[:, gs] * mixed).astype(BF16)

    def const(shape):
        return pl.BlockSpec(shape, lambda c: (0,) * len(shape))

    return pl.pallas_call(
        body, name=name, grid=(nc,),
        in_specs=[pl.BlockSpec((CHUNK, 2 * SGU_WIDTH), lambda c: (c, 0)), const((1, 2 * SGU_WIDTH)),
                  const((1, SGU_WIDTH)), const((1, SGU_WIDTH)), const((SGU_GROUPS, CHUNK, CHUNK)),
                  const((CHUNK, LANES))],
        out_specs=pl.BlockSpec((CHUNK, SGU_WIDTH), lambda c: (c, 0)),
        out_shape=jax.ShapeDtypeStruct((s, SGU_WIDTH), BF16),
        compiler_params=_params(("parallel",)),
    )(uv_raw, uv_b, ln_w, ln_b, w_sp, b_sp_t)


def _sgu_bwd(uv_raw, dyb, uv_b, ln_w, ln_b, w_sp, b_sp_t, group_sum, *, name):
    s = uv_raw.shape[0]
    nc = s // CHUNK

    def body(uv_ref, dy_ref, uvb_ref, lnw_ref, lnb_ref, w_ref, bt_ref, gsum_ref,
             duv_ref, dw_ref, dbt_ref, dlnw_ref, dlnb_ref, duvb_ref):
        @pl.when(pl.program_id(0) == 0)
        def _():
            dw_ref[...] = jnp.zeros_like(dw_ref)
            dbt_ref[...] = jnp.zeros_like(dbt_ref)
            dlnw_ref[...] = jnp.zeros_like(dlnw_ref)
            dlnb_ref[...] = jnp.zeros_like(dlnb_ref)
            duvb_ref[...] = jnp.zeros_like(duvb_ref)

        lower, _ = _tri_masks()
        uv, u, vhat, rstd, vn = _sgu_pre(uv_ref, uvb_ref, lnw_ref, lnb_ref)
        vn_b = vn.astype(BF16)
        bt = bt_ref[...]
        dy = dy_ref[...]
        du_parts, dvn_parts, dmix_parts = [], [], []
        for g in range(SGU_GROUPS):
            gs = slice(LANES * g, LANES * (g + 1))
            wc = jnp.where(lower, w_ref[g], 0.0).astype(BF16)
            mixed = _dot(wc, vn_b[:, gs]) + bt[:, g:g + 1]
            du_parts.append(dy[:, gs] * mixed)
            dmix = dy[:, gs] * u[:, gs]
            dmix_b = dmix.astype(BF16)
            dmix_parts.append(dmix)
            dw_ref[g] += jnp.where(lower, _dot(dmix_b, vn_b[:, gs], NT), 0.0)
            dvn_parts.append(_dot(wc, dmix_b, TN))
        dmixed = jnp.concatenate(dmix_parts, axis=1)
        dbt_ref[...] += _dot(dmixed, gsum_ref[...], precision=HI)
        dvn = jnp.concatenate(dvn_parts, axis=1)
        dlnw_ref[...] += jnp.sum(dvn * vhat, axis=0, keepdims=True)
        dlnb_ref[...] += jnp.sum(dvn, axis=0, keepdims=True)
        dvhat = dvn * lnw_ref[...]
        dv = rstd * (dvhat - jnp.mean(dvhat, axis=-1, keepdims=True)
                     - vhat * jnp.mean(dvhat * vhat, axis=-1, keepdims=True))
        dguv = jnp.concatenate(du_parts + [dv], axis=1)
        duv = dguv * _gelu_grad(uv)
        duvb_ref[...] += jnp.sum(duv, axis=0, keepdims=True)
        duv_ref[...] = duv.astype(BF16)

    def const(shape):
        return pl.BlockSpec(shape, lambda c: (0,) * len(shape))

    return pl.pallas_call(
        body, name=name, grid=(nc,),
        in_specs=[pl.BlockSpec((CHUNK, 2 * SGU_WIDTH), lambda c: (c, 0)),
                  pl.BlockSpec((CHUNK, SGU_WIDTH), lambda c: (c, 0)), const((1, 2 * SGU_WIDTH)),
                  const((1, SGU_WIDTH)), const((1, SGU_WIDTH)), const((SGU_GROUPS, CHUNK, CHUNK)),
                  const((CHUNK, LANES)), const((SGU_WIDTH, LANES))],
        out_specs=[pl.BlockSpec((CHUNK, 2 * SGU_WIDTH), lambda c: (c, 0)), const((SGU_GROUPS, CHUNK, CHUNK)),
                   const((CHUNK, LANES)), const((1, SGU_WIDTH)), const((1, SGU_WIDTH)), const((1, 2 * SGU_WIDTH))],
        out_shape=[jax.ShapeDtypeStruct((s, 2 * SGU_WIDTH), BF16),
                   jax.ShapeDtypeStruct((SGU_GROUPS, CHUNK, CHUNK), F32), jax.ShapeDtypeStruct((CHUNK, LANES), F32),
                   jax.ShapeDtypeStruct((1, SGU_WIDTH), F32), jax.ShapeDtypeStruct((1, SGU_WIDTH), F32),
                   jax.ShapeDtypeStruct((1, 2 * SGU_WIDTH), F32)],
        compiler_params=_params(("arbitrary",)),
    )(uv_raw, dyb, uv_b, ln_w, ln_b, w_sp, b_sp_t, group_sum)


def _gate_fwd(gates_raw, b_gate, p_a, p_b, *, name, tm=512):
    s = p_a.shape[0]
    tm = min(tm, s)

    def body(ga_ref, gb_ref, ba_ref, bb_ref, pa_ref, pb_ref, o_ref):
        ga = _sigmoid(ga_ref[...] + ba_ref[...])
        gb = _sigmoid(gb_ref[...] + bb_ref[...])
        o_ref[...] = (ga * pa_ref[...] + gb * pb_ref[...]).astype(BF16)

    t_a = pl.BlockSpec((tm, D_MODEL), lambda i: (i, 0))
    t_b = pl.BlockSpec((tm, D_MODEL), lambda i: (i, 1))
    r_a = pl.BlockSpec((1, D_MODEL), lambda i: (0, 0))
    r_b = pl.BlockSpec((1, D_MODEL), lambda i: (0, 1))
    return pl.pallas_call(
        body, name=name, grid=(s // tm,),
        in_specs=[t_a, t_b, r_a, r_b, t_a, t_a], out_specs=t_a,
        out_shape=jax.ShapeDtypeStruct((s, D_MODEL), BF16),
        compiler_params=_params(("parallel",)),
    )(gates_raw, gates_raw, b_gate, b_gate, p_a, p_b)


def _gate_bwd(gates_raw, b_gate, p_a, p_b, dm, *, name, tm=512):
    s = p_a.shape[0]
    tm = min(tm, s)

    def body(ga_ref, gb_ref, ba_ref, bb_ref, pa_ref, pb_ref, dm_ref, dpa_ref, dpb_ref, dga_ref, dgb_ref,
             dba_ref, dbb_ref):
        @pl.when(pl.program_id(0) == 0)
        def _():
            dba_ref[...] = jnp.zeros_like(dba_ref)
            dbb_ref[...] = jnp.zeros_like(dbb_ref)

        d = dm_ref[...]
        for g_ref, b_ref, p_ref, dp_ref, dg_ref, db_ref in ((ga_ref, ba_ref, pa_ref, dpa_ref, dga_ref, dba_ref),
                                                            (gb_ref, bb_ref, pb_ref, dpb_ref, dgb_ref, dbb_ref)):
            sg = _sigmoid(g_ref[...] + b_ref[...])
            dp_ref[...] = (d * sg).astype(BF16)
            dg = d * p_ref[...] * (sg * (1.0 - sg))
            dg_ref[...] = dg.astype(BF16)
            db_ref[...] += jnp.sum(dg, axis=0, keepdims=True)

    t_a = pl.BlockSpec((tm, D_MODEL), lambda i: (i, 0))
    t_b = pl.BlockSpec((tm, D_MODEL), lambda i: (i, 1))
    r_a = pl.BlockSpec((1, D_MODEL), lambda i: (0, 0))
    r_b = pl.BlockSpec((1, D_MODEL), lambda i: (0, 1))
    big = jax.ShapeDtypeStruct((s, D_MODEL), BF16)
    row = jax.ShapeDtypeStruct((1, D_MODEL), F32)
    return pl.pallas_call(
        body, name=name, grid=(s // tm,),
        in_specs=[t_a, t_b, r_a, r_b, t_a, t_a, t_a], out_specs=[t_a, t_a, t_a, t_a, r_a, r_a],
        out_shape=[big, big, big, big, row, row],
        compiler_params=_params(("arbitrary",)),
    )(gates_raw, gates_raw, b_gate, b_gate, p_a, p_b, dm)


def _adamw(w, g, m, v, *, name, tr=128):
    r, c = w.shape
    tr = min(tr, r)
    assert r % tr == 0, (name, r, tr)

    def body(w_ref, g_ref, m_ref, v_ref, d_ref, mo_ref, vo_ref):
        gv = g_ref[...]
        mn = ADAM_B1 * m_ref[...] + (1.0 - ADAM_B1) * gv
        vn = ADAM_B2 * v_ref[...] + (1.0 - ADAM_B2) * (gv * gv)
        m_hat = mn / (1.0 - ADAM_B1 ** ADAM_STEP)
        v_hat = vn / (1.0 - ADAM_B2 ** ADAM_STEP)
        d_ref[...] = -ADAM_LR * (m_hat / (jnp.sqrt(v_hat) + ADAM_EPS) + ADAM_WD * w_ref[...])
        mo_ref[...] = mn
        vo_ref[...] = vn

    blk = pl.BlockSpec((tr, c), lambda i: (i, 0))
    sds = jax.ShapeDtypeStruct((r, c), F32)
    return pl.pallas_call(
        body, name=name, grid=(r // tr,), in_specs=[blk] * 4, out_specs=[blk] * 3, out_shape=[sds] * 3,
        compiler_params=_params(("parallel",)),
    )(w, g, m, v)


def _tile(n, pref):
    if n <= pref:
        return n
    best = LANES
    for t in range(LANES, pref + 1, LANES):
        if n % t == 0:
            best = t
    return best


def _mm(pairs, name, **kw):
    n = pairs[0][1].shape[0] if kw.get("trans_b") else pairs[0][1].shape[1]
    return _matmul(pairs, tn=_tile(n, 512), name=name, **kw)


def _wgrad(a, b, name):
    return _matmul_tn(a, b, tk=_tile(a.shape[1], 512), tn=_tile(b.shape[1], 512), name=name)


def _local_step(x, target, wts, small):
    heads = jnp.arange(SSD_D_INNER) // SSD_HEAD_DIM
    expand = (jnp.arange(LANES)[:, None] == heads[None, :]).astype(F32)
    expand_t = expand.T
    group_sum = (jnp.arange(SGU_WIDTH)[:, None] // LANES == jnp.arange(LANES)[None, :]).astype(F32)
    pad_h = LANES - SSD_HEADS
    dt_bias = jnp.pad(small["dt_bias"], ((0, 0), (0, pad_h)))
    a_log = jnp.pad(small["a_log"], ((0, 0), (0, pad_h)))
    a_log_x = jnp.repeat(small["a_log"], SSD_HEAD_DIM, axis=1)
    d_skip_x = jnp.repeat(small["d_skip"], SSD_HEAD_DIM, axis=1)
    b_sp_t = jnp.pad(small["b_spatial"][0].T, ((0, 0), (0, LANES - SGU_GROUPS)))
    w_sp = small["w_spatial"][0]
    conv_a_w = jnp.pad(small["conv_a_w"], ((0, 4), (0, 0)))
    conv_f_w = jnp.pad(small["conv_f_w"], ((0, 5), (0, 0)))
    final_w = small["final_norm_w"].reshape(1, D_MODEL)

    n1 = _rms_fwd(x, small["norm1_w"], name="rms1_fwd")
    z = _mm([(n1, wts["in_z"])], "in_z")
    xbc_raw = _mm([(n1, wts["in_xbc"])], "in_xbc")
    dt_raw = _mm([(n1, wts["in_dt"])], "in_dt")
    uv_raw = _mm([(n1, wts["in_uv"])], "in_uv")
    gates_raw = _mm([(n1, wts["in_gate"])], "in_gate")
    xbc = _conv_a_fwd(xbc_raw, conv_a_w, small["conv_a_b"], name="conv_a_fwd")
    y, y_a, states = _ssd_fwd(xbc, dt_raw, z, dt_bias, a_log, a_log_x, d_skip_x, small["ssd_norm_w"], expand,
                              name="ssd_fwd")
    y_b = _sgu_fwd(uv_raw, small["uv_b"], small["v_ln_w"], small["v_ln_b"], w_sp, b_sp_t, name="sgu_fwd")
    p_a = _mm([(y_a, wts["branch_a"])], "branch_a")
    p_b = _mm([(y_b, wts["branch_b"])], "branch_b")
    mix = _gate_fwd(gates_raw, small["b_gate"], p_a, p_b, name="gate_fwd")
    h1 = _mm([(mix, wts["out"])], "out_proj", add=x)
    n2 = _rms_fwd(h1, small["norm2_w"], name="rms2_fwd")
    up_raw = _mm([(n2, wts["up"])], "up_proj")
    act = _conv_f_fwd(up_raw, conv_f_w, small["conv_f_b"], name="conv_f_fwd")
    h2 = _mm([(act, wts["down"])], "down_proj", add=h1)
    loss, dh2, dh2_b, d_final = _final_fwd_bwd(h2, final_w, target, name="final_norm_loss")

    dact = _mm([(dh2_b, wts["down"])], "down_dgrad", trans_b=True)
    g_down = _wgrad(act, dh2_b, "down_wgrad")
    dup_a, dup_v, dwf_a, dwf_v, dbf_a, dbf_v = _conv_f_bwd(up_raw, conv_f_w, small["conv_f_b"], dact,
                                                           name="conv_f_bwd")
    dn2 = _mm([(dup_a, wts["up"][:, :D_FF]), (dup_v, wts["up"][:, D_FF:])], "up_dgrad", trans_b=True)
    g_up = jnp.concatenate([_wgrad(n2, dup_a, "up_wgrad_a"), _wgrad(n2, dup_v, "up_wgrad_v")], axis=1)
    dh1, dh1_b, d_norm2 = _rms_bwd(h1, small["norm2_w"], dn2, dh2, name="rms2_bwd")
    dmix = _mm([(dh1_b, wts["out"])], "out_dgrad", trans_b=True)
    g_out = _wgrad(mix, dh1_b, "out_wgrad")
    dp_a, dp_b, dg_a, dg_b, dbg_a, dbg_b = _gate_bwd(gates_raw, small["b_gate"], p_a, p_b, dmix, name="gate_bwd")
    dya = _mm([(dp_a, wts["branch_a"])], "branch_a_dgrad", trans_b=True)
    dyb = _mm([(dp_b, wts["branch_b"])], "branch_b_dgrad", trans_b=True)
    g_branch = jnp.concatenate([_wgrad(y_a, dp_a, "branch_a_wgrad"), _wgrad(y_b, dp_b, "branch_b_wgrad")], axis=0)
    duv, d_wsp, d_bsp_t, d_lnw, d_lnb, d_uvb = _sgu_bwd(uv_raw, dyb, small["uv_b"], small["v_ln_w"],
                                                        small["v_ln_b"], w_sp, b_sp_t, group_sum, name="sgu_bwd")
    dz, dxbc, ddt, d_ssd_nw, d_dskip, d_alog, d_dtb = _ssd_bwd(
        dya, y, z, xbc, dt_raw, states, dt_bias, a_log, a_log_x, d_skip_x, small["ssd_norm_w"], expand, expand_t,
        name="ssd_bwd")
    dxbc_raw, d_conv_a_w, d_conv_a_b = _conv_a_bwd(xbc_raw, conv_a_w, small["conv_a_b"], dxbc, name="conv_a_bwd")
    w_gate = wts["in_gate"]
    dn1 = _mm([(dz, wts["in_z"]), (dxbc_raw, wts["in_xbc"]), (ddt, wts["in_dt"]), (duv, wts["in_uv"]),
               (dg_a, w_gate[:, :D_MODEL]), (dg_b, w_gate[:, D_MODEL:])], "in_dgrad", trans_b=True)
    g_in = jnp.concatenate(
        [_wgrad(n1, dz, "in_z_wgrad"), _wgrad(n1, dxbc_raw, "in_xbc_wgrad"),
         _wgrad(n1, ddt, "in_dt_wgrad")[:, :SSD_HEADS], _wgrad(n1, duv, "in_uv_wgrad"),
         _wgrad(n1, dg_a, "in_gate_a_wgrad"), _wgrad(n1, dg_b, "in_gate_b_wgrad")], axis=1)
    dx, _, d_norm1 = _rms_bwd(x, small["norm1_w"], dn1, dh1, name="rms1_bwd")

    grads_big = {"w_in": g_in, "w_branch": g_branch, "w_out": g_out, "w_up": g_up, "w_down": g_down}
    grads_small = {
        "norm1_w": d_norm1, "b_gate": jnp.concatenate([dbg_a, dbg_b], axis=1),
        "conv_a_w": d_conv_a_w[:4], "conv_a_b": d_conv_a_b,
        "dt_bias": d_dtb[:, :SSD_HEADS], "a_log": d_alog[:, :SSD_HEADS], "d_skip": d_dskip[:, :SSD_HEADS],
        "ssd_norm_w": d_ssd_nw, "uv_b": d_uvb, "v_ln_w": d_lnw, "v_ln_b": d_lnb,
        "w_spatial": d_wsp[None], "b_spatial": d_bsp_t[:, :SGU_GROUPS].T[None],
        "norm2_w": d_norm2, "conv_f_w": jnp.concatenate([dwf_a[:3], dwf_v[:3]], axis=1),
        "conv_f_b": jnp.concatenate([dbf_a, dbf_v], axis=1), "final_norm_w": d_final.reshape(D_MODEL),
    }
    return loss, dx, grads_big, grads_small


HBM = pl.BlockSpec(memory_space=pl.ANY)
MESH = pl.DeviceIdType.MESH


def _mesh_pos():
    return lax.axis_index("x"), lax.axis_index("y"), lax.axis_index("c")


def _other_chips(x, y):
    return [(1 - x, y), (x, 1 - y), (1 - x, 1 - y)]


def _remote(src, dst, send_sems, recv_sems, k, dev):
    return pltpu.make_async_remote_copy(src_ref=src, dst_ref=dst, send_sem=send_sems.at[k], recv_sem=recv_sems.at[k],
                                        device_id=dev, device_id_type=MESH)


def _dma_sems(n):
    return [pltpu.SemaphoreType.DMA((n,)), pltpu.SemaphoreType.DMA((n,))]


def _gather_ici(shard, *, name):
    _, rh, _ = shard.shape

    def body(w_ref, o_ref, send_sems, recv_sems, local_sem):
        x, y, c = _mesh_pos()
        mine = 2 * x + y
        local = pltpu.make_async_copy(w_ref, o_ref.at[mine], local_sem)
        local.start()
        sends = []
        for k, (px, py) in enumerate(_other_chips(x, y)):
            cp = _remote(w_ref.at[c], o_ref.at[mine, c], send_sems, recv_sems, k, (px, py, c))
            cp.start()
            sends.append(cp)
        for k, (px, py) in enumerate(_other_chips(x, y)):
            _remote(w_ref.at[c], o_ref.at[2 * px + py, c], send_sems, recv_sems, k, (px, py, c)).wait_recv()
        for cp in sends:
            cp.wait_send()
        local.wait()

    return pl.pallas_call(
        body, name=name, in_specs=[HBM], out_specs=HBM,
        out_shape=jax.ShapeDtypeStruct((N_CHIPS, 2, rh, LANES), shard.dtype),
        scratch_shapes=_dma_sems(3) + [pltpu.SemaphoreType.DMA(())],
    )(shard)


def _gather_d2d(parts, *, name):
    def body(a_ref, o_ref, send_sems, recv_sems):
        x, y, c = _mesh_pos()
        sibling = (x, y, 1 - c)
        sends = []
        for k, (px, py) in enumerate(_other_chips(x, y)):
            cp = _remote(a_ref.at[2 * px + py, c], o_ref.at[2 * px + py, c], send_sems, recv_sems, k, sibling)
            cp.start()
            sends.append(cp)
        for k, (px, py) in enumerate(_other_chips(x, y)):
            _remote(a_ref.at[2 * px + py, c], o_ref.at[2 * px + py, 1 - c], send_sems, recv_sems, k, sibling).wait_recv()
        for cp in sends:
            cp.wait_send()

    return pl.pallas_call(
        body, name=name, in_specs=[HBM], out_specs=HBM,
        out_shape=jax.ShapeDtypeStruct(parts.shape, parts.dtype),
        input_output_aliases={0: 0}, scratch_shapes=_dma_sems(3),
    )(parts)


def _all_gather_chips(shard_flat, name):
    rows = shard_flat.shape[0]
    parts = _gather_ici(shard_flat.reshape(2, rows // 2, LANES), name=name + "_ici")
    return _gather_d2d(parts, name=name + "_d2d").reshape(N_CHIPS, rows, LANES)


def _row_tile(rows, mult, cap):
    best = mult
    for t in range(mult, min(rows, cap) + 1, mult):
        if rows % t == 0:
            best = t
    assert rows % best == 0, (rows, mult)
    return best


def _swap_halves_d2d(g, *, name):
    _, _, rh, _ = g.shape

    def body(g_ref, o_ref, send_sems, recv_sems):
        x, y, c = _mesh_pos()
        sibling = (x, y, 1 - c)
        sends = []
        for s in range(N_CHIPS):
            cp = _remote(g_ref.at[s, 1 - c], o_ref.at[s], send_sems, recv_sems, s, sibling)
            cp.start()
            sends.append(cp)
        for s in range(N_CHIPS):
            _remote(g_ref.at[s, c], o_ref.at[s], send_sems, recv_sems, s, sibling).wait_recv()
        for cp in sends:
            cp.wait_send()

    return pl.pallas_call(
        body, name=name, in_specs=[HBM], out_specs=HBM,
        out_shape=jax.ShapeDtypeStruct((N_CHIPS, rh, LANES), g.dtype), scratch_shapes=_dma_sems(N_CHIPS),
    )(g)


def _add_own_half(g, arrived, core, *, name):
    _, _, rh, _ = g.shape
    mult = 16 if g.dtype == BF16 else 8
    tr = _row_tile(rh, mult, 4096)

    def body(core_ref, g_ref, a_ref, o_ref):
        o_ref[...] = (g_ref[0].astype(F32) + a_ref[...].astype(F32)).astype(o_ref.dtype)

    grid_spec = pltpu.PrefetchScalarGridSpec(
        num_scalar_prefetch=1, grid=(N_CHIPS, rh // tr),
        in_specs=[pl.BlockSpec((1, 1, tr, LANES), lambda s, i, core_ref: (s, core_ref[0], i, 0)),
                  pl.BlockSpec((1, tr, LANES), lambda s, i, core_ref: (s, i, 0))],
        out_specs=pl.BlockSpec((1, tr, LANES), lambda s, i, core_ref: (s, i, 0)))
    return pl.pallas_call(
        body, name=name, grid_spec=grid_spec, out_shape=jax.ShapeDtypeStruct((N_CHIPS, rh, LANES), g.dtype),
        compiler_params=_params(("parallel", "parallel")),
    )(core, g, arrived)


def _scatter_ici(h, *, name):
    def body(h_ref, o_ref, send_sems, recv_sems, local_sem):
        x, y, c = _mesh_pos()
        mine = 2 * x + y
        local = pltpu.make_async_copy(h_ref.at[mine], o_ref.at[mine], local_sem)
        local.start()
        sends = []
        for k, (px, py) in enumerate(_other_chips(x, y)):
            cp = _remote(h_ref.at[2 * px + py], o_ref.at[mine], send_sems, recv_sems, k, (px, py, c))
            cp.start()
            sends.append(cp)
        for k, (px, py) in enumerate(_other_chips(x, y)):
            _remote(h_ref.at[mine], o_ref.at[2 * px + py], send_sems, recv_sems, k, (px, py, c)).wait_recv()
        for cp in sends:
            cp.wait_send()
        local.wait()

    return pl.pallas_call(
        body, name=name, in_specs=[HBM], out_specs=HBM, out_shape=jax.ShapeDtypeStruct(h.shape, h.dtype),
        scratch_shapes=_dma_sems(3) + [pltpu.SemaphoreType.DMA(())],
    )(h)


def _sum_chips(parts, *, name):
    _, rh, _ = parts.shape
    mult = 16 if parts.dtype == BF16 else 8
    tr = _row_tile(rh, mult, 4096)

    def body(p_ref, o_ref):
        acc = p_ref[0].astype(F32)
        for s in range(1, N_CHIPS):
            acc = acc + p_ref[s].astype(F32)
        o_ref[...] = acc

    return pl.pallas_call(
        body, name=name, grid=(rh // tr,),
        in_specs=[pl.BlockSpec((N_CHIPS, tr, LANES), lambda i: (0, i, 0))],
        out_specs=pl.BlockSpec((tr, LANES), lambda i: (i, 0)),
        out_shape=jax.ShapeDtypeStruct((rh, LANES), F32), compiler_params=_params(("parallel",)),
    )(parts)


def _share_d2d(f, *, name):
    rh = f.shape[0]

    def body(f_ref, o_ref, send_sems, recv_sems, local_sem):
        x, y, c = _mesh_pos()
        sibling = (x, y, 1 - c)
        local = pltpu.make_async_copy(f_ref, o_ref.at[c], local_sem)
        local.start()
        cp = _remote(f_ref, o_ref.at[c], send_sems, recv_sems, 0, sibling)
        cp.start()
        _remote(f_ref, o_ref.at[1 - c], send_sems, recv_sems, 0, sibling).wait_recv()
        cp.wait_send()
        local.wait()

    return pl.pallas_call(
        body, name=name, in_specs=[HBM], out_specs=HBM, out_shape=jax.ShapeDtypeStruct((2, rh, LANES), f.dtype),
        scratch_shapes=_dma_sems(1) + [pltpu.SemaphoreType.DMA(())],
    )(f)


def _reduce_scatter_chips(g, core, name):
    _, rows, _ = g.shape
    g = g.reshape(N_CHIPS, 2, rows // 2, LANES)
    arrived = _swap_halves_d2d(g, name=name + "_swap")
    chip_sum = _add_own_half(g, arrived, core, name=name + "_add2")
    parts = _scatter_ici(chip_sum, name=name + "_ici")
    total = _sum_chips(parts, name=name + "_sum4")
    return _share_d2d(total, name=name + "_share").reshape(rows, LANES)


BIG = ("w_in", "w_branch", "w_out", "w_up", "w_down")
BIG_COLUMN_SHARDED = ("w_in", "w_up")
CONV = ("conv_a_w", "conv_f_w")
REPLICATED = ("norm1_w", "b_gate", "conv_a_b", "dt_bias", "a_log", "d_skip", "ssd_norm_w", "uv_b", "v_ln_w",
              "v_ln_b", "w_spatial", "b_spatial", "norm2_w", "conv_f_b", "final_norm_w")
WEIGHT_ORDER = ("norm1_w", "w_in", "b_gate", "conv_a_w", "conv_a_b", "dt_bias", "a_log", "d_skip", "ssd_norm_w",
                "uv_b", "v_ln_w", "v_ln_b", "w_spatial", "b_spatial", "w_branch", "w_out", "norm2_w", "w_up",
                "conv_f_w", "conv_f_b", "w_down", "final_norm_w")
SMALL_EXCHANGE_ROWS = 64


def _flat_rows(arrays, row_multiple):
    flat = jnp.concatenate([a.reshape(-1) for a in arrays])
    rows = -(-flat.shape[0] // (LANES * row_multiple)) * row_multiple
    return jnp.pad(flat, (0, rows * LANES - flat.shape[0])).reshape(rows, LANES)


def _unflatten(flat, shapes):
    flat = flat.reshape(-1)
    out, off = [], 0
    for shp in shapes:
        n = math.prod(shp)
        out.append(flat[off:off + n].reshape(shp))
        off += n
    return out


def _from_chip_blocks(blocks, name):
    if name in BIG_COLUMN_SHARDED or name in CONV:
        k = blocks.shape[1]
        return jnp.transpose(blocks, (1, 0, 2)).reshape(k, -1)
    return blocks.reshape(-1, blocks.shape[-1])


def _to_chip_blocks(whole, name):
    if name in BIG_COLUMN_SHARDED or name in CONV:
        k, n = whole.shape
        return jnp.transpose(whole.reshape(k, N_CHIPS, n // N_CHIPS), (1, 0, 2))
    return whole.reshape(N_CHIPS, whole.shape[0] // N_CHIPS, whole.shape[1])


def kernel(x, norm1_w, w_in, b_gate, conv_a_w, conv_a_b, dt_bias, a_log, d_skip, ssd_norm_w, uv_b, v_ln_w, v_ln_b, w_spatial, b_spatial, w_branch, w_out, norm2_w, w_up, conv_f_w, conv_f_b, w_down, final_norm_w, loss_target, m_norm1_w, m_w_in, m_b_gate, m_conv_a_w, m_conv_a_b, m_dt_bias, m_a_log, m_d_skip, m_ssd_norm_w, m_uv_b, m_v_ln_w, m_v_ln_b, m_w_spatial, m_b_spatial, m_w_branch, m_w_out, m_norm2_w, m_w_up, m_conv_f_w, m_conv_f_b, m_w_down, m_final_norm_w, v_norm1_w, v_w_in, v_b_gate, v_conv_a_w, v_conv_a_b, v_dt_bias, v_a_log, v_d_skip, v_ssd_norm_w, v_uv_b, v_v_ln_w, v_v_ln_b, v_w_spatial, v_b_spatial, v_w_branch, v_w_out, v_norm2_w, v_w_up, v_conv_f_w, v_conv_f_b, v_w_down, v_final_norm_w):
    weights = dict(norm1_w=norm1_w, w_in=w_in, b_gate=b_gate, conv_a_w=conv_a_w, conv_a_b=conv_a_b, dt_bias=dt_bias,
                   a_log=a_log, d_skip=d_skip, ssd_norm_w=ssd_norm_w, uv_b=uv_b, v_ln_w=v_ln_w, v_ln_b=v_ln_b,
                   w_spatial=w_spatial, b_spatial=b_spatial, w_branch=w_branch, w_out=w_out, norm2_w=norm2_w,
                   w_up=w_up, conv_f_w=conv_f_w, conv_f_b=conv_f_b, w_down=w_down, final_norm_w=final_norm_w)
    mom1 = dict(norm1_w=m_norm1_w, w_in=m_w_in, b_gate=m_b_gate, conv_a_w=m_conv_a_w, conv_a_b=m_conv_a_b,
                dt_bias=m_dt_bias, a_log=m_a_log, d_skip=m_d_skip, ssd_norm_w=m_ssd_norm_w, uv_b=m_uv_b,
                v_ln_w=m_v_ln_w, v_ln_b=m_v_ln_b, w_spatial=m_w_spatial, b_spatial=m_b_spatial, w_branch=m_w_branch,
                w_out=m_w_out, norm2_w=m_norm2_w, w_up=m_w_up, conv_f_w=m_conv_f_w, conv_f_b=m_conv_f_b,
                w_down=m_w_down, final_norm_w=m_final_norm_w)
    mom2 = dict(norm1_w=v_norm1_w, w_in=v_w_in, b_gate=v_b_gate, conv_a_w=v_conv_a_w, conv_a_b=v_conv_a_b,
                dt_bias=v_dt_bias, a_log=v_a_log, d_skip=v_d_skip, ssd_norm_w=v_ssd_norm_w, uv_b=v_uv_b,
                v_ln_w=v_v_ln_w, v_ln_b=v_v_ln_b, w_spatial=v_w_spatial, b_spatial=v_b_spatial, w_branch=v_w_branch,
                w_out=v_w_out, norm2_w=v_norm2_w, w_up=v_w_up, conv_f_w=v_conv_f_w, conv_f_b=v_conv_f_b,
                w_down=v_w_down, final_norm_w=v_final_norm_w)
    chip = 2 * lax.axis_index("x") + lax.axis_index("y")
    core = lax.axis_index("c").astype(jnp.int32).reshape(1)

    big_shapes = [weights[n].shape[1:] for n in BIG]
    shard = _flat_rows([weights[n].astype(BF16) for n in BIG], 32)
    gathered = _all_gather_chips(shard, "gather_w").reshape(N_CHIPS, -1)
    whole = {}
    off = 0
    for n, shp in zip(BIG, big_shapes):
        size = math.prod(shp)
        whole[n] = _from_chip_blocks(gathered[:, off:off + size].reshape((N_CHIPS,) + shp), n)
        off += size
    conv_shapes = [weights[n].shape[1:] for n in CONV]
    conv_gathered = _all_gather_chips(_flat_rows([weights[n] for n in CONV], 16), "gather_conv").reshape(N_CHIPS, -1)
    off = 0
    for n, shp in zip(CONV, conv_shapes):
        size = math.prod(shp)
        whole[n] = _from_chip_blocks(conv_gathered[:, off:off + size].reshape((N_CHIPS,) + shp), n)
        off += size

    w_in_full = whole["w_in"]
    wts = {
        "in_z": w_in_full[:, :SSD_D_INNER],
        "in_xbc": w_in_full[:, SSD_D_INNER:SSD_D_INNER + SSD_XBC],
        "in_dt": jnp.pad(w_in_full[:, SSD_D_INNER + SSD_XBC:SSD_IN], ((0, 0), (0, LANES - SSD_HEADS))),
        "in_uv": w_in_full[:, SSD_IN:SSD_IN + 2 * SGU_WIDTH],
        "in_gate": w_in_full[:, SSD_IN + 2 * SGU_WIDTH:],
        "branch_a": whole["w_branch"][:SSD_D_INNER], "branch_b": whole["w_branch"][SSD_D_INNER:],
        "out": whole["w_out"], "up": whole["w_up"], "down": whole["w_down"],
    }
    small = {n: weights[n] for n in REPLICATED}
    small["conv_a_w"] = whole["conv_a_w"]
    small["conv_f_w"] = whole["conv_f_w"]

    loss, dx, grads_big, grads_small = _local_step(x[0], loss_target[0], wts, small)

    g_big = jnp.concatenate([_to_chip_blocks(grads_big[n], n).reshape(N_CHIPS, -1) for n in BIG], axis=1)
    pad = shard.shape[0] * LANES - g_big.shape[1]
    g_big = jnp.pad(g_big, ((0, 0), (0, pad))).reshape(N_CHIPS, -1, LANES)
    red_big = _reduce_scatter_chips(g_big, core, "reduce_w")
    grads = dict(zip(BIG, _unflatten(red_big, big_shapes)))

    small_names = REPLICATED + CONV
    small_shapes = [grads_small[n].shape for n in small_names]
    g_small = _flat_rows([grads_small[n] for n in small_names], N_CHIPS * 2 * SMALL_EXCHANGE_ROWS)
    red_small = _reduce_scatter_chips(g_small.reshape(N_CHIPS, -1, LANES), core, "reduce_small")
    all_small = _all_gather_chips(red_small, "gather_small")
    for n, g in zip(small_names, _unflatten(all_small, small_shapes)):
        if n in CONV:
            width = g.shape[1] // N_CHIPS
            g = lax.dynamic_slice_in_dim(g, chip * width, width, axis=1)
        grads[n] = g.reshape(weights[n].shape[1:]) if n != "final_norm_w" else g

    delta, new_m, new_v = {}, {}, {}
    for n in BIG:
        shp = weights[n].shape
        tr = _row_tile(shp[1], 8, 128)
        d, m1, m2 = _adamw(weights[n][0], grads[n], mom1[n][0], mom2[n][0], name="adamw_" + n, tr=tr)
        delta[n], new_m[n], new_v[n] = d.reshape(shp), m1.reshape(shp), m2.reshape(shp)
    small_all = [n for n in WEIGHT_ORDER if n not in BIG]
    shapes = [weights[n].shape for n in small_all]
    packed = [_flat_rows([src[n] for n in small_all], 8)
              for src in (weights, {n: grads[n] for n in small_all}, mom1, mom2)]
    d, m1, m2 = _adamw(*packed, name="adamw_small", tr=packed[0].shape[0])
    for n, dv, mv, vv in zip(small_all, _unflatten(d, shapes), _unflatten(m1, shapes), _unflatten(m2, shapes)):
        delta[n], new_m[n], new_v[n] = dv, mv, vv

    total_loss = lax.psum(loss[0, 0], ("x", "y", "c"))
    grad_out = [grads[n].reshape(weights[n].shape) for n in WEIGHT_ORDER]
    return (total_loss, dx[None], *grad_out, *[delta[n] for n in WEIGHT_ORDER], *[new_m[n] for n in WEIGHT_ORDER],
            *[new_v[n] for n in WEIGHT_ORDER])
```

```python
import functools
import math

import jax
import jax.numpy as jnp
from jax import lax
from jax.experimental import pallas as pl
from jax.experimental.pallas import tpu as pltpu

F32 = jnp.float32
BF16 = jnp.bfloat16
HI = lax.Precision.HIGHEST

D_MODEL = 1024
SSD_D_INNER = 2048
SSD_HEADS = 32
SSD_HEAD_DIM = 64
SSD_GROUPS = 4
SSD_HEADS_PER_GROUP = 8
SSD_STATE = 128
SSD_BC = 512
SSD_XBC = 3072
SSD_IN = 5152
SGU_WIDTH = 1024
SGU_GROUPS = 8
CHUNK = 128
IN_COLS = 9248
D_FF = 2816
NORM_EPS = 1e-6
LN_EPS = 1e-5
GROUP_COLS = SSD_HEADS_PER_GROUP * SSD_HEAD_DIM
LANES = 128

ADAM_LR = 0.001
ADAM_B1 = 0.9
ADAM_B2 = 0.999
ADAM_EPS = 1e-08
ADAM_WD = 0.01
ADAM_STEP = 10

N_CHIPS = 4
VMEM_LIMIT = 56 * 1024 * 1024

NT = (((1,), (1,)), ((), ()))
TN = (((0,), (0,)), ((), ()))
NN = (((1,), (0,)), ((), ()))


def _params(dims):
    return pltpu.CompilerParams(dimension_semantics=dims, vmem_limit_bytes=VMEM_LIMIT)


def _dot(a, b, dn=NN, precision=None):
    return lax.dot_general(a, b, dn, precision=precision, preferred_element_type=F32)


def _sigmoid(x):
    return 1.0 / (1.0 + jnp.exp(-x))


def _softplus(x):
    return jnp.maximum(x, 0.0) + jnp.log(1.0 + jnp.exp(-jnp.abs(x)))


def _matmul(pairs, *, trans_b=False, add=None, out_dtype=F32, tm=512, tn=512, name):
    def mat_shape(b):
        if isinstance(b, tuple) and b[1] == "cols":
            return (b[0].shape[1], b[0].shape[0] * b[0].shape[2])
        return b[0].shape[1:] if isinstance(b, tuple) else b.shape

    if isinstance(pairs[0][1], tuple) and pairs[0][1][1] == "cols":
        assert not trans_b and tn % LANES == 0 and pairs[0][1][0].shape[2] % tn == 0, name

    m = (pairs[0][0][0] if isinstance(pairs[0][0], tuple) else pairs[0][0]).shape[0]
    n = mat_shape(pairs[0][1])[0] if trans_b else mat_shape(pairs[0][1])[1]
    tm, tn = min(tm, m), min(tn, n)
    assert m % tm == 0 and n % tn == 0, (name, m, n, tm, tn)
    npairs = len(pairs)
    dn = NT if trans_b else NN

    def body(*refs):
        o_ref = refs[-1]
        acc = None
        for i in range(npairs):
            p = _dot(refs[2 * i][...].astype(BF16), refs[2 * i + 1][...].astype(BF16), dn)
            acc = p if acc is None else acc + p
        if add is not None:
            acc = acc + refs[2 * npairs][...]
        o_ref[...] = acc.astype(out_dtype)

    in_specs, args = [], []
    for a, b in pairs:
        bshape = mat_shape(b)
        k = bshape[1] if trans_b else bshape[0]
        assert bshape == ((n, k) if trans_b else (k, n)), (name, bshape)
        a, qa = a if isinstance(a, tuple) else (a, 0)
        assert a.shape[0] == m and a.shape[1] % k == 0, (name, a.shape, k)
        in_specs.append(pl.BlockSpec((tm, k), lambda i, j, qa=qa: (i, qa)))
        if isinstance(b, tuple) and b[1] == "cols":
            b = b[0]
            per = b.shape[2] // tn
            in_specs.append(pl.BlockSpec((None, k, tn), lambda i, j, per=per: (j // per, 0, j % per)))
        elif isinstance(b, tuple):
            b, qb = b
            if trans_b:
                in_specs.append(pl.BlockSpec((None, tn, k), lambda i, j, qb=qb: (qb, j, 0)))
            else:
                in_specs.append(pl.BlockSpec((None, k, tn), lambda i, j, qb=qb: (qb, 0, j)))
        elif trans_b:
            in_specs.append(pl.BlockSpec((tn, k), lambda i, j: (j, 0)))
        else:
            in_specs.append(pl.BlockSpec((k, tn), lambda i, j: (0, j)))
        args += [a, b]
    if add is not None:
        in_specs.append(pl.BlockSpec((tm, tn), lambda i, j: (i, j)))
        args.append(add)
    return pl.pallas_call(
        body, name=name, grid=(m // tm, n // tn), in_specs=in_specs,
        out_specs=pl.BlockSpec((tm, tn), lambda i, j: (i, j)),
        out_shape=jax.ShapeDtypeStruct((m, n), out_dtype),
        compiler_params=_params(("parallel", "parallel")),
    )(*args)


def _matmul_tn(a, b, *, tk, tn, tm=1024, out_dtype=BF16, stack_out=False, name):
    m, k = a.shape
    n = b.shape[1]
    tm, tk, tn = min(tm, m), min(tk, k), min(tn, n)
    assert m % tm == 0 and k % tk == 0 and n % tn == 0, (name, m, k, n)
    nm = m // tm
    if stack_out:
        out_spec = pl.BlockSpec((None, tk, tn), lambda i, j, l: (j, i, 0))
        out_shape = jax.ShapeDtypeStruct((n // tn, k, tn), out_dtype)
    else:
        out_spec = pl.BlockSpec((tk, tn), lambda i, j, l: (i, j))
        out_shape = jax.ShapeDtypeStruct((k, n), out_dtype)

    def body(a_ref, b_ref, o_ref, acc):
        mi = pl.program_id(2)

        @pl.when(mi == 0)
        def _():
            acc[...] = jnp.zeros_like(acc)

        acc[...] += _dot(a_ref[...].astype(BF16), b_ref[...].astype(BF16), TN)

        @pl.when(mi == nm - 1)
        def _():
            o_ref[...] = acc[...].astype(out_dtype)

    return pl.pallas_call(
        body, name=name, grid=(k // tk, n // tn, nm),
        in_specs=[pl.BlockSpec((tm, tk), lambda i, j, l: (l, i)), pl.BlockSpec((tm, tn), lambda i, j, l: (l, j))],
        out_specs=out_spec, out_shape=out_shape,
        scratch_shapes=[pltpu.VMEM((tk, tn), F32)],
        compiler_params=_params(("parallel", "parallel", "arbitrary")),
    )(a, b)


def _rms_fwd(x, w, *, name, tm=512):
    s, d = x.shape
    tm = min(tm, s)

    def body(x_ref, w_ref, o_ref):
        xv = x_ref[...]
        r = lax.rsqrt(jnp.mean(xv * xv, axis=-1, keepdims=True) + NORM_EPS)
        o_ref[...] = (xv * r * w_ref[...]).astype(BF16)

    return pl.pallas_call(
        body, name=name, grid=(s // tm,),
        in_specs=[pl.BlockSpec((tm, d), lambda i: (i, 0)), pl.BlockSpec((1, d), lambda i: (0, 0))],
        out_specs=pl.BlockSpec((tm, d), lambda i: (i, 0)),
        out_shape=jax.ShapeDtypeStruct((s, d), BF16),
        compiler_params=_params(("parallel",)),
    )(x, w)


def _rms_bwd(x, w, dn, dres, *, name, tm=512):
    s, d = x.shape
    tm = min(tm, s)

    def body(x_ref, w_ref, dn_ref, dres_ref, dx_ref, dxb_ref, dw_ref):
        @pl.when(pl.program_id(0) == 0)
        def _():
            dw_ref[...] = jnp.zeros_like(dw_ref)

        xv = x_ref[...]
        r = lax.rsqrt(jnp.mean(xv * xv, axis=-1, keepdims=True) + NORM_EPS)
        xhat = xv * r
        dnv = dn_ref[...]
        dxhat = dnv * w_ref[...]
        dx = dres_ref[...] + r * (dxhat - xhat * jnp.mean(dxhat * xhat, axis=-1, keepdims=True))
        dx_ref[...] = dx
        dxb_ref[...] = dx.astype(BF16)
        dw_ref[...] += jnp.sum(dnv * xhat, axis=0, keepdims=True)

    tile = pl.BlockSpec((tm, d), lambda i: (i, 0))
    row = pl.BlockSpec((1, d), lambda i: (0, 0))
    return pl.pallas_call(
        body, name=name, grid=(s // tm,),
        in_specs=[tile, row, tile, tile], out_specs=[tile, tile, row],
        out_shape=[jax.ShapeDtypeStruct((s, d), F32), jax.ShapeDtypeStruct((s, d), BF16),
                   jax.ShapeDtypeStruct((1, d), F32)],
        compiler_params=_params(("arbitrary",)),
    )(x, w, dn, dres)


def _final_fwd_bwd(h2, wf, target, *, name, tm=512):
    s, d = h2.shape
    tm = min(tm, s)

    def body(h_ref, w_ref, t_ref, loss_ref, dh_ref, dhb_ref, dw_ref):
        @pl.when(pl.program_id(0) == 0)
        def _():
            dw_ref[...] = jnp.zeros_like(dw_ref)
            loss_ref[...] = jnp.zeros_like(loss_ref)

        hv = h_ref[...]
        r = lax.rsqrt(jnp.mean(hv * hv, axis=-1, keepdims=True) + NORM_EPS)
        xhat = hv * r
        err = xhat * w_ref[...] - t_ref[...]
        per_tok = jnp.mean(err * err, axis=-1, keepdims=True)
        loss_ref[...] += 0.5 * jnp.sum(per_tok, axis=0, keepdims=True)
        dy = err * (1.0 / d)
        dxhat = dy * w_ref[...]
        dh = r * (dxhat - xhat * jnp.mean(dxhat * xhat, axis=-1, keepdims=True))
        dh_ref[...] = dh
        dhb_ref[...] = dh.astype(BF16)
        dw_ref[...] += jnp.sum(dy * xhat, axis=0, keepdims=True)

    tile = pl.BlockSpec((tm, d), lambda i: (i, 0))
    row = pl.BlockSpec((1, d), lambda i: (0, 0))
    return pl.pallas_call(
        body, name=name, grid=(s // tm,),
        in_specs=[tile, row, tile],
        out_specs=[pl.BlockSpec((1, 1), lambda i: (0, 0)), tile, tile, row],
        out_shape=[jax.ShapeDtypeStruct((1, 1), F32), jax.ShapeDtypeStruct((s, d), F32),
                   jax.ShapeDtypeStruct((s, d), BF16), jax.ShapeDtypeStruct((1, d), F32)],
        compiler_params=_params(("arbitrary",)),
    )(h2, wf, target)


def _shift_down(x, k):
    if k == 0:
        return x
    rows = lax.broadcasted_iota(jnp.int32, x.shape, 0)
    return jnp.where(rows >= k, pltpu.roll(x, k, 0), 0.0)


def _shift_up(x, k):
    if k == 0:
        return x
    s = x.shape[0]
    rows = lax.broadcasted_iota(jnp.int32, x.shape, 0)
    return jnp.where(rows < s - k, pltpu.roll(x, s - k, 0), 0.0)


def _conv_taps(x, w_ref, kk):
    acc = None
    for i in range(kk):
        term = w_ref[i:i + 1, :] * _shift_down(x, kk - 1 - i)
        acc = term if acc is None else acc + term
    return acc


def _conv_a_fwd(xraw, w, b, *, name, tc=128):
    s, c = xraw.shape
    kk = 4

    def body(x_ref, w_ref, b_ref, o_ref):
        pre = _conv_taps(x_ref[...], w_ref, kk) + b_ref[...]
        o_ref[...] = pre * _sigmoid(pre)

    col = pl.BlockSpec((s, tc), lambda j: (0, j))
    return pl.pallas_call(
        body, name=name, grid=(c // tc,),
        in_specs=[col, pl.BlockSpec((8, tc), lambda j: (0, j)), pl.BlockSpec((1, tc), lambda j: (0, j))],
        out_specs=col, out_shape=jax.ShapeDtypeStruct((s, c), F32),
        compiler_params=_params(("parallel",)),
    )(xraw, w, b)


def _conv_a_bwd(xraw, w, b, dy, *, name, tc=128):
    s, c = xraw.shape
    kk = 4

    def body(x_ref, w_ref, b_ref, dy_ref, dx_ref, dw_ref, db_ref):
        x = x_ref[...]
        pre = _conv_taps(x, w_ref, kk) + b_ref[...]
        sg = _sigmoid(pre)
        dpre = dy_ref[...] * (sg * (1.0 + pre * (1.0 - sg)))
        db_ref[...] = jnp.sum(dpre, axis=0, keepdims=True)
        dx = None
        rows = []
        for i in range(kk):
            rows.append(jnp.sum(dpre * _shift_down(x, kk - 1 - i), axis=0, keepdims=True))
            term = w_ref[i:i + 1, :] * _shift_up(dpre, kk - 1 - i)
            dx = term if dx is None else dx + term
        dx_ref[...] = dx.astype(BF16)
        dw_ref[...] = jnp.concatenate(rows + [jnp.zeros((8 - kk, x.shape[1]), F32)], axis=0)

    col = pl.BlockSpec((s, tc), lambda j: (0, j))
    w8 = pl.BlockSpec((8, tc), lambda j: (0, j))
    row = pl.BlockSpec((1, tc), lambda j: (0, j))
    return pl.pallas_call(
        body, name=name, grid=(c // tc,),
        in_specs=[col, w8, row, col], out_specs=[col, w8, row],
        out_shape=[jax.ShapeDtypeStruct((s, c), BF16), jax.ShapeDtypeStruct((8, c), F32),
                   jax.ShapeDtypeStruct((1, c), F32)],
        compiler_params=_params(("parallel",)),
    )(xraw, w, b, dy)


def _conv_f_fwd(up_raw, w, b, *, name, tc=128):
    s, c2 = up_raw.shape
    c = c2 // 2
    nb = c // tc
    kk = 3

    def body(xa_ref, xv_ref, wa_ref, wv_ref, ba_ref, bv_ref, o_ref):
        a = _conv_taps(xa_ref[...], wa_ref, kk) + ba_ref[...]
        v = _conv_taps(xv_ref[...], wv_ref, kk) + bv_ref[...]
        o_ref[...] = (a * _sigmoid(a) * v).astype(BF16)

    col_a = pl.BlockSpec((s, tc), lambda j: (0, j))
    col_v = pl.BlockSpec((s, tc), lambda j: (0, j + nb))
    return pl.pallas_call(
        body, name=name, grid=(nb,),
        in_specs=[col_a, col_v, pl.BlockSpec((8, tc), lambda j: (0, j)), pl.BlockSpec((8, tc), lambda j: (0, j + nb)),
                  pl.BlockSpec((1, tc), lambda j: (0, j)), pl.BlockSpec((1, tc), lambda j: (0, j + nb))],
        out_specs=col_a, out_shape=jax.ShapeDtypeStruct((s, c), BF16),
        compiler_params=_params(("parallel",)),
    )(up_raw, up_raw, w, w, b, b)


def _conv_f_bwd(up_raw, w, b, dact, *, name, tc=128):
    s, c2 = up_raw.shape
    c = c2 // 2
    nb = c // tc
    kk = 3

    def body(xa_ref, xv_ref, wa_ref, wv_ref, ba_ref, bv_ref, d_ref,
             dxa_ref, dxv_ref, dwa_ref, dwv_ref, dba_ref, dbv_ref):
        xa, xv = xa_ref[...], xv_ref[...]
        a = _conv_taps(xa, wa_ref, kk) + ba_ref[...]
        v = _conv_taps(xv, wv_ref, kk) + bv_ref[...]
        sg = _sigmoid(a)
        d = d_ref[...]
        da = d * v * (sg * (1.0 + a * (1.0 - sg)))
        dv = d * (a * sg)
        for x, dp, w_ref, dx_ref, dw_ref, db_ref in ((xa, da, wa_ref, dxa_ref, dwa_ref, dba_ref),
                                                     (xv, dv, wv_ref, dxv_ref, dwv_ref, dbv_ref)):
            db_ref[...] = jnp.sum(dp, axis=0, keepdims=True)
            dx = None
            rows = []
            for i in range(kk):
                rows.append(jnp.sum(dp * _shift_down(x, kk - 1 - i), axis=0, keepdims=True))
                term = w_ref[i:i + 1, :] * _shift_up(dp, kk - 1 - i)
                dx = term if dx is None else dx + term
            dx_ref[...] = dx.astype(BF16)
            dw_ref[...] = jnp.concatenate(rows + [jnp.zeros((8 - kk, x.shape[1]), F32)], axis=0)

    col_a = pl.BlockSpec((s, tc), lambda j: (0, j))
    col_v = pl.BlockSpec((s, tc), lambda j: (0, j + nb))
    w_a = pl.BlockSpec((8, tc), lambda j: (0, j))
    w_v = pl.BlockSpec((8, tc), lambda j: (0, j + nb))
    r_a = pl.BlockSpec((1, tc), lambda j: (0, j))
    r_v = pl.BlockSpec((1, tc), lambda j: (0, j + nb))
    outs = pl.pallas_call(
        body, name=name, grid=(nb,),
        in_specs=[col_a, col_v, w_a, w_v, r_a, r_v, col_a],
        out_specs=[col_a, col_a, w_a, w_a, r_a, r_a],
        out_shape=[jax.ShapeDtypeStruct((s, c), BF16), jax.ShapeDtypeStruct((s, c), BF16),
                   jax.ShapeDtypeStruct((8, c), F32), jax.ShapeDtypeStruct((8, c), F32),
                   jax.ShapeDtypeStruct((1, c), F32), jax.ShapeDtypeStruct((1, c), F32)],
        compiler_params=_params(("parallel",)),
    )(up_raw, up_raw, w, w, b, b, dact)
    return outs


def _tri_masks():
    row = lax.broadcasted_iota(jnp.int32, (CHUNK, CHUNK), 0)
    col = lax.broadcasted_iota(jnp.int32, (CHUNK, CHUNK), 1)
    return row >= col, row <= col


def _ssd_fwd(xbc, dt_raw, z, dt_bias, a_log, a_log_x, d_skip_x, norm_w, expand, *, name):
    s = xbc.shape[0]
    nc = s // CHUNK

    def body(xbc_ref, dtr_ref, z_ref, dtb_ref, alog_ref, alogx_ref, dskx_ref, nw_ref, e_ref,
             y_ref, ya_ref, st_ref, state):
        @pl.when(pl.program_id(0) == 0)
        def _():
            state[...] = jnp.zeros_like(state)

        st_ref[0] = state[...]
        lower, _ = _tri_masks()
        tril = lower.astype(F32)
        dt = _softplus(dtr_ref[...] + dtb_ref[...])
        adt = dt * (-jnp.exp(alog_ref[...]))
        acum = _dot(tril, adt, precision=HI)
        acum_t = acum.T
        for g in range(SSD_GROUPS):
            sl = slice(GROUP_COLS * g, GROUP_COLS * (g + 1))
            dt_x = _dot(dt, e_ref[:, sl], precision=HI)
            adt_x = dt_x * (-jnp.exp(alogx_ref[:, sl]))
            acum_x = _dot(tril, adt_x, precision=HI)
            tot_x = jnp.sum(adt_x, axis=0, keepdims=True)
            xs = xbc_ref[:, sl]
            xdt = xs * dt_x
            xdt_b = xdt.astype(BF16)
            bg = xbc_ref[:, SSD_D_INNER + SSD_STATE * g:SSD_D_INNER + SSD_STATE * (g + 1)].astype(BF16)
            cg = xbc_ref[:, SSD_D_INNER + SSD_BC + SSD_STATE * g:SSD_D_INNER + SSD_BC + SSD_STATE * (g + 1)].astype(BF16)
            cb = _dot(cg, bg, NT)
            st_g = state[:, sl]
            y_off = _dot(cg, st_g.astype(BF16)) * jnp.exp(acum_x)
            parts = []
            for r in range(SSD_HEADS_PER_GROUP):
                h = SSD_HEADS_PER_GROUP * g + r
                dec = jnp.exp(jnp.where(lower, acum[:, h:h + 1] - acum_t[h:h + 1, :], -jnp.inf))
                parts.append(_dot((cb * dec).astype(BF16), xdt_b[:, SSD_HEAD_DIM * r:SSD_HEAD_DIM * (r + 1)]))
            y_ref[:, sl] = jnp.concatenate(parts, axis=1) + y_off + dskx_ref[:, sl] * xs
            wgt = (xdt * jnp.exp(tot_x - acum_x)).astype(BF16)
            state[:, sl] = st_g * jnp.exp(tot_x) + _dot(bg, wgt, TN)
        zv = z_ref[...]
        q = y_ref[...] * (zv * _sigmoid(zv))
        r = lax.rsqrt(jnp.mean(q * q, axis=-1, keepdims=True) + NORM_EPS)
        ya_ref[...] = (q * r * nw_ref[...]).astype(BF16)

    def chunk(w):
        return pl.BlockSpec((CHUNK, w), lambda c: (c, 0))

    def const(shape):
        return pl.BlockSpec(shape, lambda c: (0,) * len(shape))

    return pl.pallas_call(
        body, name=name, grid=(nc,),
        in_specs=[chunk(SSD_XBC), chunk(LANES), chunk(SSD_D_INNER), const((1, LANES)), const((1, LANES)),
                  const((1, SSD_D_INNER)), const((1, SSD_D_INNER)), const((1, SSD_D_INNER)),
                  const((LANES, SSD_D_INNER))],
        out_specs=[chunk(SSD_D_INNER), chunk(SSD_D_INNER),
                   pl.BlockSpec((1, SSD_STATE, SSD_D_INNER), lambda c: (c, 0, 0))],
        out_shape=[jax.ShapeDtypeStruct((s, SSD_D_INNER), F32), jax.ShapeDtypeStruct((s, SSD_D_INNER), BF16),
                   jax.ShapeDtypeStruct((nc, SSD_STATE, SSD_D_INNER), F32)],
        scratch_shapes=[pltpu.VMEM((SSD_STATE, SSD_D_INNER), F32)],
        compiler_params=_params(("arbitrary",)),
    )(xbc, dt_raw, z, dt_bias, a_log, a_log_x, d_skip_x, norm_w, expand)


def _ssd_bwd(dya, y, z, xbc, dt_raw, states, dt_bias, a_log, a_log_x, d_skip_x, norm_w, expand, expand_t, *, name):
    s = xbc.shape[0]
    nc = s // CHUNK

    def body(dya_ref, y_ref, z_ref, xbc_ref, dtr_ref, stp_ref, dtb_ref, alog_ref, alogx_ref, dskx_ref, nw_ref,
             e_ref, et_ref, dz_ref, dxbc_ref, ddt_ref, dnw_ref, ddsk_ref, dalog_ref, ddtb_ref,
             dstate, dy_sc, dskcol):
        i = pl.program_id(0)

        @pl.when(i == 0)
        def _():
            dstate[...] = jnp.zeros_like(dstate)
            dskcol[...] = jnp.zeros_like(dskcol)
            dnw_ref[...] = jnp.zeros_like(dnw_ref)
            dalog_ref[...] = jnp.zeros_like(dalog_ref)
            ddtb_ref[...] = jnp.zeros_like(ddtb_ref)
            ddsk_ref[...] = jnp.zeros_like(ddsk_ref)

        lower, upper = _tri_masks()
        tril = lower.astype(F32)
        rows = lax.broadcasted_iota(jnp.int32, (CHUNK, LANES), 0)
        pre = dtr_ref[...] + dtb_ref[...]
        dt = _softplus(pre)
        a = -jnp.exp(alog_ref[...])
        adt = dt * a
        acum = _dot(tril, adt, precision=HI)
        acum_t = acum.T

        yv = y_ref[...]
        zv = z_ref[...]
        sz = _sigmoid(zv)
        silu_z = zv * sz
        q = yv * silu_z
        r = lax.rsqrt(jnp.mean(q * q, axis=-1, keepdims=True) + NORM_EPS)
        qhat = q * r
        dyav = dya_ref[...]
        dqhat = dyav * nw_ref[...]
        dnw_ref[...] += jnp.sum(dyav * qhat, axis=0, keepdims=True)
        dq = r * (dqhat - qhat * jnp.mean(dqhat * qhat, axis=-1, keepdims=True))
        dy_sc[...] = dq * silu_z
        dz_ref[...] = (dq * yv * (sz * (1.0 + zv * (1.0 - sz)))).astype(BF16)

        da_cum = jnp.zeros((CHUNK, LANES), F32)
        ddt = jnp.zeros((CHUNK, LANES), F32)
        for g in range(SSD_GROUPS):
            sl = slice(GROUP_COLS * g, GROUP_COLS * (g + 1))
            et_g = et_ref[sl, :]
            dt_x = _dot(dt, e_ref[:, sl], precision=HI)
            adt_x = dt_x * (-jnp.exp(alogx_ref[:, sl]))
            acum_x = _dot(tril, adt_x, precision=HI)
            tot_x = jnp.sum(adt_x, axis=0, keepdims=True)
            e_tot = jnp.exp(tot_x)
            dec_s = jnp.exp(tot_x - acum_x)
            xs = xbc_ref[:, sl]
            xdt = xs * dt_x
            xdt_b = xdt.astype(BF16)
            dy = dy_sc[:, sl]
            dy_b = dy.astype(BF16)
            dskx = dskx_ref[:, sl]
            y_ssd = y_ref[:, sl] - dskx * xs
            dskcol[:, sl] += jnp.sum(dy * xs, axis=0, keepdims=True)
            bg = xbc_ref[:, SSD_D_INNER + SSD_STATE * g:SSD_D_INNER + SSD_STATE * (g + 1)].astype(BF16)
            cg = xbc_ref[:, SSD_D_INNER + SSD_BC + SSD_STATE * g:SSD_D_INNER + SSD_BC + SSD_STATE * (g + 1)].astype(BF16)
            cb = _dot(cg, bg, NT)
            sp = stp_ref[0, :, sl]
            ds_g = dstate[:, sl]
            ds_b = ds_g.astype(BF16)
            dye_b = (dy * jnp.exp(acum_x)).astype(BF16)
            dc = _dot(dye_b, sp.astype(BF16), NT)
            dxdt_state = dec_s * _dot(bg, ds_b)
            db = _dot((xdt * dec_s).astype(BF16), ds_b, NT)
            dcb = jnp.zeros((CHUNK, CHUNK), F32)
            parts = []
            for rr in range(SSD_HEADS_PER_GROUP):
                h = SSD_HEADS_PER_GROUP * g + rr
                hs = slice(SSD_HEAD_DIM * rr, SSD_HEAD_DIM * (rr + 1))
                dec = jnp.exp(jnp.where(lower, acum[:, h:h + 1] - acum_t[h:h + 1, :], -jnp.inf))
                parts.append(_dot((cb * dec).astype(BF16), dy_b[:, hs], TN))
                dcb = dcb + _dot(dy_b[:, hs], xdt_b[:, hs], NT) * dec
            dxdt = jnp.concatenate(parts, axis=1) + dxdt_state
            dcb_b = dcb.astype(BF16)
            dc = dc + _dot(dcb_b, bg)
            db = db + _dot(dcb_b, cg, TN)
            tot_col = jnp.sum(ds_g * sp, axis=0, keepdims=True) * e_tot + jnp.sum(dxdt_state * xdt, axis=0, keepdims=True)
            d_tot = _dot(jnp.broadcast_to(tot_col, (8, GROUP_COLS)), et_g, precision=HI)
            d_tot = jnp.max(d_tot, axis=0, keepdims=True)
            pair_sums = dy_b.astype(F32) * y_ssd - xdt_b.astype(F32) * dxdt
            da_cum = da_cum + _dot(pair_sums, et_g, precision=HI) + jnp.where(rows == CHUNK - 1, d_tot, 0.0)
            ddt = ddt + _dot(dxdt * xs, et_g, precision=HI)
            dxbc_ref[:, sl] = dy * dskx + dxdt * dt_x
            dxbc_ref[:, SSD_D_INNER + SSD_STATE * g:SSD_D_INNER + SSD_STATE * (g + 1)] = db
            dxbc_ref[:, SSD_D_INNER + SSD_BC + SSD_STATE * g:SSD_D_INNER + SSD_BC + SSD_STATE * (g + 1)] = dc
            dstate[:, sl] = e_tot * ds_g + _dot(cg, dye_b, TN)

        dadt = _dot(upper.astype(F32), da_cum, precision=HI)
        ddt = ddt + dadt * a
        dalog_ref[...] += jnp.sum(dadt * dt, axis=0, keepdims=True)
        dpre = ddt * _sigmoid(pre)
        ddtb_ref[...] += jnp.sum(dpre, axis=0, keepdims=True)
        ddt_ref[...] = dpre.astype(BF16)

        @pl.when(i == nc - 1)
        def _():
            dalog_ref[...] = dalog_ref[...] * a
            dsk = _dot(jnp.broadcast_to(dskcol[...], (8, SSD_D_INNER)), et_ref[...], precision=HI)
            ddsk_ref[...] = jnp.max(dsk, axis=0, keepdims=True)

    def chunk(w):
        return pl.BlockSpec((CHUNK, w), lambda i: (nc - 1 - i, 0))

    def const(shape):
        return pl.BlockSpec(shape, lambda i: (0,) * len(shape))

    return pl.pallas_call(
        body, name=name, grid=(nc,),
        in_specs=[chunk(SSD_D_INNER), chunk(SSD_D_INNER), chunk(SSD_D_INNER), chunk(SSD_XBC), chunk(LANES),
                  pl.BlockSpec((1, SSD_STATE, SSD_D_INNER), lambda i: (nc - 1 - i, 0, 0)),
                  const((1, LANES)), const((1, LANES)), const((1, SSD_D_INNER)), const((1, SSD_D_INNER)),
                  const((1, SSD_D_INNER)), const((LANES, SSD_D_INNER)), const((SSD_D_INNER, LANES))],
        out_specs=[chunk(SSD_D_INNER), chunk(SSD_XBC), chunk(LANES), const((1, SSD_D_INNER)), const((1, LANES)),
                   const((1, LANES)), const((1, LANES))],
        out_shape=[jax.ShapeDtypeStruct((s, SSD_D_INNER), BF16), jax.ShapeDtypeStruct((s, SSD_XBC), F32),
                   jax.ShapeDtypeStruct((s, LANES), BF16), jax.ShapeDtypeStruct((1, SSD_D_INNER), F32),
                   jax.ShapeDtypeStruct((1, LANES), F32), jax.ShapeDtypeStruct((1, LANES), F32),
                   jax.ShapeDtypeStruct((1, LANES), F32)],
        scratch_shapes=[pltpu.VMEM((SSD_STATE, SSD_D_INNER), F32), pltpu.VMEM((CHUNK, SSD_D_INNER), F32),
                        pltpu.VMEM((1, SSD_D_INNER), F32)],
        compiler_params=_params(("arbitrary",)),
    )(dya, y, z, xbc, dt_raw, states, dt_bias, a_log, a_log_x, d_skip_x, norm_w, expand, expand_t)


GELU_K = math.sqrt(2.0 / math.pi)
GELU_C = 0.044715


def _gelu(x):
    return 0.5 * x * (1.0 + jnp.tanh(GELU_K * (x + GELU_C * x * x * x)))


def _gelu_grad(x):
    t = jnp.tanh(GELU_K * (x + GELU_C * x * x * x))
    return 0.5 * (1.0 + t) + 0.5 * x * (1.0 - t * t) * (GELU_K * (1.0 + 3.0 * GELU_C * x * x))


def _sgu_pre(uv_ref, uvb_ref, lnw_ref, lnb_ref):
    uv = uv_ref[...] + uvb_ref[...]
    guv = _gelu(uv)
    u = guv[:, :SGU_WIDTH]
    v = guv[:, SGU_WIDTH:]
    mu = jnp.mean(v, axis=-1, keepdims=True)
    vc = v - mu
    rstd = lax.rsqrt(jnp.mean(vc * vc, axis=-1, keepdims=True) + LN_EPS)
    vhat = vc * rstd
    vn = vhat * lnw_ref[...] + lnb_ref[...]
    return uv, u, vhat, rstd, vn


def _sgu_fwd(uv_raw, uv_b, ln_w, ln_b, w_sp, b_sp_t, *, name):
    s = uv_raw.shape[0]
    nc = s // CHUNK

    def body(uv_ref, uvb_ref, lnw_ref, lnb_ref, w_ref, bt_ref, o_ref):
        lower, _ = _tri_masks()
        _, u, _, _, vn = _sgu_pre(uv_ref, uvb_ref, lnw_ref, lnb_ref)
        vn_b = vn.astype(BF16)
        bt = bt_ref[...]
        for g in range(SGU_GROUPS):
            gs = slice(LANES * g, LANES * (g + 1))
            wc = jnp.where(lower, w_ref[g], 0.0).astype(BF16)
            mixed = _dot(wc, vn_b[:, gs]) + bt[:, g:g + 1]
            o_ref[:, gs] = (u[:, gs] * mixed).astype(BF16)

    def const(shape):
        return pl.BlockSpec(shape, lambda c: (0,) * len(shape))

    return pl.pallas_call(
        body, name=name, grid=(nc,),
        in_specs=[pl.BlockSpec((CHUNK, 2 * SGU_WIDTH), lambda c: (c, 0)), const((1, 2 * SGU_WIDTH)),
                  const((1, SGU_WIDTH)), const((1, SGU_WIDTH)), const((SGU_GROUPS, CHUNK, CHUNK)),
                  const((CHUNK, LANES))],
        out_specs=pl.BlockSpec((CHUNK, SGU_WIDTH), lambda c: (c, 0)),
        out_shape=jax.ShapeDtypeStruct((s, SGU_WIDTH), BF16),
        compiler_params=_params(("parallel",)),
    )(uv_raw, uv_b, ln_w, ln_b, w_sp, b_sp_t)


def _sgu_bwd(uv_raw, dyb, uv_b, ln_w, ln_b, w_sp, b_sp_t, group_sum, *, name):
    s = uv_raw.shape[0]
    nc = s // CHUNK

    def body(uv_ref, dy_ref, uvb_ref, lnw_ref, lnb_ref, w_ref, bt_ref, gsum_ref,
             duv_ref, dw_ref, dbt_ref, dlnw_ref, dlnb_ref, duvb_ref):
        @pl.when(pl.program_id(0) == 0)
        def _():
            dw_ref[...] = jnp.zeros_like(dw_ref)
            dbt_ref[...] = jnp.zeros_like(dbt_ref)
            dlnw_ref[...] = jnp.zeros_like(dlnw_ref)
            dlnb_ref[...] = jnp.zeros_like(dlnb_ref)
            duvb_ref[...] = jnp.zeros_like(duvb_ref)

        lower, _ = _tri_masks()
        uv, u, vhat, rstd, vn = _sgu_pre(uv_ref, uvb_ref, lnw_ref, lnb_ref)
        vn_b = vn.astype(BF16)
        bt = bt_ref[...]
        dy = dy_ref[...]
        du_parts, dvn_parts, dmix_parts = [], [], []
        for g in range(SGU_GROUPS):
            gs = slice(LANES * g, LANES * (g + 1))
            wc = jnp.where(lower, w_ref[g], 0.0).astype(BF16)
            mixed = _dot(wc, vn_b[:, gs]) + bt[:, g:g + 1]
            du_parts.append(dy[:, gs] * mixed)
            dmix = dy[:, gs] * u[:, gs]
            dmix_b = dmix.astype(BF16)
            dmix_parts.append(dmix)
            dw_ref[g] += jnp.where(lower, _dot(dmix_b, vn_b[:, gs], NT), 0.0)
            dvn_parts.append(_dot(wc, dmix_b, TN))
        dmixed = jnp.concatenate(dmix_parts, axis=1)
        dbt_ref[...] += _dot(dmixed, gsum_ref[...], precision=HI)
        dvn = jnp.concatenate(dvn_parts, axis=1)
        dlnw_ref[...] += jnp.sum(dvn * vhat, axis=0, keepdims=True)
        dlnb_ref[...] += jnp.sum(dvn, axis=0, keepdims=True)
        dvhat = dvn * lnw_ref[...]
        dv = rstd * (dvhat - jnp.mean(dvhat, axis=-1, keepdims=True)
                     - vhat * jnp.mean(dvhat * vhat, axis=-1, keepdims=True))
        dguv = jnp.concatenate(du_parts + [dv], axis=1)
        duv = dguv * _gelu_grad(uv)
        duvb_ref[...] += jnp.sum(duv, axis=0, keepdims=True)
        duv_ref[...] = duv.astype(BF16)

    def const(shape):
        return pl.BlockSpec(shape, lambda c: (0,) * len(shape))

    return pl.pallas_call(
        body, name=name, grid=(nc,),
        in_specs=[pl.BlockSpec((CHUNK, 2 * SGU_WIDTH), lambda c: (c, 0)),
                  pl.BlockSpec((CHUNK, SGU_WIDTH), lambda c: (c, 0)), const((1, 2 * SGU_WIDTH)),
                  const((1, SGU_WIDTH)), const((1, SGU_WIDTH)), const((SGU_GROUPS, CHUNK, CHUNK)),
                  const((CHUNK, LANES)), const((SGU_WIDTH, LANES))],
        out_specs=[pl.BlockSpec((CHUNK, 2 * SGU_WIDTH), lambda c: (c, 0)), const((SGU_GROUPS, CHUNK, CHUNK)),
                   const((CHUNK, LANES)), const((1, SGU_WIDTH)), const((1, SGU_WIDTH)), const((1, 2 * SGU_WIDTH))],
        out_shape=[jax.ShapeDtypeStruct((s, 2 * SGU_WIDTH), BF16),
                   jax.ShapeDtypeStruct((SGU_GROUPS, CHUNK, CHUNK), F32), jax.ShapeDtypeStruct((CHUNK, LANES), F32),
                   jax.ShapeDtypeStruct((1, SGU_WIDTH), F32), jax.ShapeDtypeStruct((1, SGU_WIDTH), F32),
                   jax.ShapeDtypeStruct((1, 2 * SGU_WIDTH), F32)],
        compiler_params=_params(("arbitrary",)),
    )(uv_raw, dyb, uv_b, ln_w, ln_b, w_sp, b_sp_t, group_sum)


def _gate_fwd(gates_raw, b_gate, p_a, p_b, *, name, tm=512):
    s = p_a.shape[0]
    tm = min(tm, s)

    def body(ga_ref, gb_ref, ba_ref, bb_ref, pa_ref, pb_ref, o_ref):
        ga = _sigmoid(ga_ref[...] + ba_ref[...])
        gb = _sigmoid(gb_ref[...] + bb_ref[...])
        o_ref[...] = (ga * pa_ref[...] + gb * pb_ref[...]).astype(BF16)

    t_a = pl.BlockSpec((tm, D_MODEL), lambda i: (i, 0))
    t_b = pl.BlockSpec((tm, D_MODEL), lambda i: (i, 1))
    r_a = pl.BlockSpec((1, D_MODEL), lambda i: (0, 0))
    r_b = pl.BlockSpec((1, D_MODEL), lambda i: (0, 1))
    return pl.pallas_call(
        body, name=name, grid=(s // tm,),
        in_specs=[t_a, t_b, r_a, r_b, t_a, t_a], out_specs=t_a,
        out_shape=jax.ShapeDtypeStruct((s, D_MODEL), BF16),
        compiler_params=_params(("parallel",)),
    )(gates_raw, gates_raw, b_gate, b_gate, p_a, p_b)


def _gate_bwd(gates_raw, b_gate, p_a, p_b, dm, *, name, tm=512):
    s = p_a.shape[0]
    tm = min(tm, s)

    def body(ga_ref, gb_ref, ba_ref, bb_ref, pa_ref, pb_ref, dm_ref, dpa_ref, dpb_ref, dga_ref, dgb_ref,
             dba_ref, dbb_ref):
        @pl.when(pl.program_id(0) == 0)
        def _():
            dba_ref[...] = jnp.zeros_like(dba_ref)
            dbb_ref[...] = jnp.zeros_like(dbb_ref)

        d = dm_ref[...]
        for g_ref, b_ref, p_ref, dp_ref, dg_ref, db_ref in ((ga_ref, ba_ref, pa_ref, dpa_ref, dga_ref, dba_ref),
                                                            (gb_ref, bb_ref, pb_ref, dpb_ref, dgb_ref, dbb_ref)):
            sg = _sigmoid(g_ref[...] + b_ref[...])
            dp_ref[...] = (d * sg).astype(BF16)
            dg = d * p_ref[...] * (sg * (1.0 - sg))
            dg_ref[...] = dg.astype(BF16)
            db_ref[...] += jnp.sum(dg, axis=0, keepdims=True)

    t_a = pl.BlockSpec((tm, D_MODEL), lambda i: (i, 0))
    t_b = pl.BlockSpec((tm, D_MODEL), lambda i: (i, 1))
    r_a = pl.BlockSpec((1, D_MODEL), lambda i: (0, 0))
    r_b = pl.BlockSpec((1, D_MODEL), lambda i: (0, 1))
    big = jax.ShapeDtypeStruct((s, D_MODEL), BF16)
    row = jax.ShapeDtypeStruct((1, D_MODEL), F32)
    return pl.pallas_call(
        body, name=name, grid=(s // tm,),
        in_specs=[t_a, t_b, r_a, r_b, t_a, t_a, t_a], out_specs=[t_a, t_a, t_a, t_a, r_a, r_a],
        out_shape=[big, big, big, big, row, row],
        compiler_params=_params(("arbitrary",)),
    )(gates_raw, gates_raw, b_gate, b_gate, p_a, p_b, dm)


def _adamw(w, g, m, v, *, name, tr=128):
    r, c = w.shape
    tr = min(tr, r)
    assert r % tr == 0, (name, r, tr)

    def body(w_ref, g_ref, m_ref, v_ref, d_ref, mo_ref, vo_ref):
        gv = g_ref[...]
        mn = ADAM_B1 * m_ref[...] + (1.0 - ADAM_B1) * gv
        vn = ADAM_B2 * v_ref[...] + (1.0 - ADAM_B2) * (gv * gv)
        m_hat = mn / (1.0 - ADAM_B1 ** ADAM_STEP)
        v_hat = vn / (1.0 - ADAM_B2 ** ADAM_STEP)
        d_ref[...] = -ADAM_LR * (m_hat / (jnp.sqrt(v_hat) + ADAM_EPS) + ADAM_WD * w_ref[...])
        mo_ref[...] = mn
        vo_ref[...] = vn

    blk = pl.BlockSpec((tr, c), lambda i: (i, 0))
    sds = jax.ShapeDtypeStruct((r, c), F32)
    return pl.pallas_call(
        body, name=name, grid=(r // tr,), in_specs=[blk] * 4, out_specs=[blk] * 3, out_shape=[sds] * 3,
        compiler_params=_params(("parallel",)),
    )(w, g, m, v)


def _tile(n, pref):
    if n <= pref:
        return n
    best = LANES
    for t in range(LANES, pref + 1, LANES):
        if n % t == 0:
            best = t
    return best


def _mm(pairs, name, **kw):
    b = pairs[0][1]
    shape = b[0].shape[1:] if isinstance(b, tuple) else b.shape
    n = shape[0] if kw.get("trans_b") else shape[1]
    return _matmul(pairs, tn=_tile(n, 512), name=name, **kw)


def _wgrad(a, b, name, **kw):
    return _matmul_tn(a, b, tk=_tile(a.shape[1], 512), tn=kw.pop("tn", _tile(b.shape[1], 512)), name=name, **kw)


def _local_step(x, target, wts, small):
    heads = jnp.arange(SSD_D_INNER) // SSD_HEAD_DIM
    expand = (jnp.arange(LANES)[:, None] == heads[None, :]).astype(F32)
    expand_t = expand.T
    group_sum = (jnp.arange(SGU_WIDTH)[:, None] // LANES == jnp.arange(LANES)[None, :]).astype(F32)
    pad_h = LANES - SSD_HEADS
    dt_bias = jnp.pad(small["dt_bias"], ((0, 0), (0, pad_h)))
    a_log = jnp.pad(small["a_log"], ((0, 0), (0, pad_h)))
    a_log_x = jnp.repeat(small["a_log"], SSD_HEAD_DIM, axis=1)
    d_skip_x = jnp.repeat(small["d_skip"], SSD_HEAD_DIM, axis=1)
    b_sp_t = jnp.pad(small["b_spatial"][0].T, ((0, 0), (0, LANES - SGU_GROUPS)))
    w_sp = small["w_spatial"][0]
    conv_a_w = jnp.pad(small["conv_a_w"], ((0, 4), (0, 0)))
    conv_f_w = jnp.pad(small["conv_f_w"], ((0, 5), (0, 0)))
    final_w = small["final_norm_w"].reshape(1, D_MODEL)

    n1 = _rms_fwd(x, small["norm1_w"], name="rms1_fwd")
    z = _mm([(n1, wts["in_z"])], "in_z")
    xbc_raw = _mm([(n1, wts["in_xbc"])], "in_xbc")
    dt_raw = _mm([(n1, wts["in_dt"])], "in_dt")
    uv_raw = _mm([(n1, wts["in_uv"])], "in_uv")
    gates_raw = _mm([(n1, wts["in_gate"])], "in_gate")
    xbc = _conv_a_fwd(xbc_raw, conv_a_w, small["conv_a_b"], name="conv_a_fwd")
    y, y_a, states = _ssd_fwd(xbc, dt_raw, z, dt_bias, a_log, a_log_x, d_skip_x, small["ssd_norm_w"], expand,
                              name="ssd_fwd")
    y_b = _sgu_fwd(uv_raw, small["uv_b"], small["v_ln_w"], small["v_ln_b"], w_sp, b_sp_t, name="sgu_fwd")
    p_a = _mm([(y_a, wts["branch_a"])], "branch_a")
    p_b = _mm([(y_b, wts["branch_b"])], "branch_b")
    mix = _gate_fwd(gates_raw, small["b_gate"], p_a, p_b, name="gate_fwd")
    h1 = _mm([(mix, wts["out"])], "out_proj", add=x)
    n2 = _rms_fwd(h1, small["norm2_w"], name="rms2_fwd")
    up_w = wts["up"]
    up_cols = up_w.shape[2]
    up_raw = _matmul([(n2, (up_w, "cols"))], tn=up_cols, name="up_proj")
    act = _conv_f_fwd(up_raw, conv_f_w, small["conv_f_b"], name="conv_f_fwd")
    h2 = _mm([(act, wts["down"])], "down_proj", add=h1)
    loss, dh2, dh2_b, d_final = _final_fwd_bwd(h2, final_w, target, name="final_norm_loss")

    dact = _mm([(dh2_b, wts["down"])], "down_dgrad", trans_b=True)
    g_down = _wgrad(act, dh2_b, "down_wgrad")
    dup_a, dup_v, dwf_a, dwf_v, dbf_a, dbf_v = _conv_f_bwd(up_raw, conv_f_w, small["conv_f_b"], dact,
                                                           name="conv_f_bwd")
    dn2 = _mm([((dup_a, 0), (up_w, 0)), ((dup_a, 1), (up_w, 1)), ((dup_v, 0), (up_w, 2)), ((dup_v, 1), (up_w, 3))],
              "up_dgrad", trans_b=True)
    g_up = jnp.concatenate([_wgrad(n2, dup_a, "up_wgrad_a", tn=up_cols, stack_out=True),
                            _wgrad(n2, dup_v, "up_wgrad_v", tn=up_cols, stack_out=True)], axis=0)
    dh1, dh1_b, d_norm2 = _rms_bwd(h1, small["norm2_w"], dn2, dh2, name="rms2_bwd")
    dmix = _mm([(dh1_b, wts["out"])], "out_dgrad", trans_b=True)
    g_out = _wgrad(mix, dh1_b, "out_wgrad")
    dp_a, dp_b, dg_a, dg_b, dbg_a, dbg_b = _gate_bwd(gates_raw, small["b_gate"], p_a, p_b, dmix, name="gate_bwd")
    dya = _mm([(dp_a, wts["branch_a"])], "branch_a_dgrad", trans_b=True)
    dyb = _mm([(dp_b, wts["branch_b"])], "branch_b_dgrad", trans_b=True)
    g_branch = jnp.concatenate([_wgrad(y_a, dp_a, "branch_a_wgrad"), _wgrad(y_b, dp_b, "branch_b_wgrad")], axis=0)
    duv, d_wsp, d_bsp_t, d_lnw, d_lnb, d_uvb = _sgu_bwd(uv_raw, dyb, small["uv_b"], small["v_ln_w"],
                                                        small["v_ln_b"], w_sp, b_sp_t, group_sum, name="sgu_bwd")
    dz, dxbc, ddt, d_ssd_nw, d_dskip, d_alog, d_dtb = _ssd_bwd(
        dya, y, z, xbc, dt_raw, states, dt_bias, a_log, a_log_x, d_skip_x, small["ssd_norm_w"], expand, expand_t,
        name="ssd_bwd")
    dxbc_raw, d_conv_a_w, d_conv_a_b = _conv_a_bwd(xbc_raw, conv_a_w, small["conv_a_b"], dxbc, name="conv_a_bwd")
    dn1 = _mm([(dz, wts["in_z"]), (dxbc_raw, wts["in_xbc"]), (ddt, wts["in_dt"]), (duv, wts["in_uv"]),
               (dg_a, wts["in_gate_a"]), (dg_b, wts["in_gate_b"])], "in_dgrad", trans_b=True)
    g_in = jnp.concatenate(
        [_wgrad(n1, dz, "in_z_wgrad"), _wgrad(n1, dxbc_raw, "in_xbc_wgrad"),
         _wgrad(n1, ddt, "in_dt_wgrad")[:, :SSD_HEADS], _wgrad(n1, duv, "in_uv_wgrad"),
         _wgrad(n1, dg_a, "in_gate_a_wgrad"), _wgrad(n1, dg_b, "in_gate_b_wgrad")], axis=1)
    dx, _, d_norm1 = _rms_bwd(x, small["norm1_w"], dn1, dh1, name="rms1_bwd")

    grads_big = {"w_in": g_in, "w_branch": g_branch, "w_out": g_out, "w_up": g_up, "w_down": g_down}
    grads_small = {
        "norm1_w": d_norm1, "b_gate": jnp.concatenate([dbg_a, dbg_b], axis=1),
        "conv_a_w": d_conv_a_w[:4], "conv_a_b": d_conv_a_b,
        "dt_bias": d_dtb[:, :SSD_HEADS], "a_log": d_alog[:, :SSD_HEADS], "d_skip": d_dskip[:, :SSD_HEADS],
        "ssd_norm_w": d_ssd_nw, "uv_b": d_uvb, "v_ln_w": d_lnw, "v_ln_b": d_lnb,
        "w_spatial": d_wsp[None], "b_spatial": d_bsp_t[:, :SGU_GROUPS].T[None],
        "norm2_w": d_norm2, "conv_f_w": jnp.concatenate([dwf_a[:3], dwf_v[:3]], axis=1),
        "conv_f_b": jnp.concatenate([dbf_a, dbf_v], axis=1), "final_norm_w": d_final.reshape(D_MODEL),
    }
    return loss, dx, grads_big, grads_small


HBM = pl.BlockSpec(memory_space=pl.ANY)
MESH = pl.DeviceIdType.MESH


def _mesh_pos():
    return lax.axis_index("x"), lax.axis_index("y"), lax.axis_index("c")


def _other_chips(x, y):
    return [(1 - x, y), (x, 1 - y), (1 - x, 1 - y)]


def _remote(src, dst, send_sems, recv_sems, k, dev):
    return pltpu.make_async_remote_copy(src_ref=src, dst_ref=dst, send_sem=send_sems.at[k], recv_sem=recv_sems.at[k],
                                        device_id=dev, device_id_type=MESH)


def _dma_sems(n):
    return [pltpu.SemaphoreType.DMA((n,)), pltpu.SemaphoreType.DMA((n,))]


def _gather_ici(shard, *, name):
    _, rh, cols = shard.shape

    def body(w_ref, o_ref, send_sems, recv_sems, local_sem):
        x, y, c = _mesh_pos()
        mine = 2 * x + y
        local = pltpu.make_async_copy(w_ref, o_ref.at[mine], local_sem)
        local.start()
        sends = []
        for k, (px, py) in enumerate(_other_chips(x, y)):
            cp = _remote(w_ref.at[c], o_ref.at[mine, c], send_sems, recv_sems, k, (px, py, c))
            cp.start()
            sends.append(cp)
        for k, (px, py) in enumerate(_other_chips(x, y)):
            _remote(w_ref.at[c], o_ref.at[2 * px + py, c], send_sems, recv_sems, k, (px, py, c)).wait_recv()
        for cp in sends:
            cp.wait_send()
        local.wait()

    return pl.pallas_call(
        body, name=name, in_specs=[HBM], out_specs=HBM,
        out_shape=jax.ShapeDtypeStruct((N_CHIPS, 2, rh, cols), shard.dtype),
        scratch_shapes=_dma_sems(3) + [pltpu.SemaphoreType.DMA(())],
    )(shard)


def _gather_d2d(parts, *, name):
    def body(a_ref, o_ref, send_sems, recv_sems):
        x, y, c = _mesh_pos()
        sibling = (x, y, 1 - c)
        sends = []
        for k, (px, py) in enumerate(_other_chips(x, y)):
            cp = _remote(a_ref.at[2 * px + py, c], o_ref.at[2 * px + py, c], send_sems, recv_sems, k, sibling)
            cp.start()
            sends.append(cp)
        for k, (px, py) in enumerate(_other_chips(x, y)):
            _remote(a_ref.at[2 * px + py, c], o_ref.at[2 * px + py, 1 - c], send_sems, recv_sems, k, sibling).wait_recv()
        for cp in sends:
            cp.wait_send()

    return pl.pallas_call(
        body, name=name, in_specs=[HBM], out_specs=HBM,
        out_shape=jax.ShapeDtypeStruct(parts.shape, parts.dtype),
        input_output_aliases={0: 0}, scratch_shapes=_dma_sems(3),
    )(parts)


def _all_gather_chips(shard_flat, name):
    rows, cols = shard_flat.shape
    parts = _gather_ici(shard_flat.reshape(2, rows // 2, cols), name=name + "_ici")
    return _gather_d2d(parts, name=name + "_d2d").reshape(N_CHIPS, rows, cols)


def _row_tile(rows, mult, cap):
    best = mult
    for t in range(mult, min(rows, cap) + 1, mult):
        if rows % t == 0:
            best = t
    assert rows % best == 0, (rows, mult)
    return best


def _swap_halves_d2d(g, *, name):
    _, _, rh, cols = g.shape

    def body(g_ref, o_ref, send_sems, recv_sems):
        x, y, c = _mesh_pos()
        sibling = (x, y, 1 - c)
        sends = []
        for s in range(N_CHIPS):
            cp = _remote(g_ref.at[s, 1 - c], o_ref.at[s], send_sems, recv_sems, s, sibling)
            cp.start()
            sends.append(cp)
        for s in range(N_CHIPS):
            _remote(g_ref.at[s, c], o_ref.at[s], send_sems, recv_sems, s, sibling).wait_recv()
        for cp in sends:
            cp.wait_send()

    return pl.pallas_call(
        body, name=name, in_specs=[HBM], out_specs=HBM,
        out_shape=jax.ShapeDtypeStruct((N_CHIPS, rh, cols), g.dtype), scratch_shapes=_dma_sems(N_CHIPS),
    )(g)


def _add_own_half(g, arrived, core, *, name):
    _, _, rh, cols = g.shape
    mult = 16 if g.dtype == BF16 else 8
    tr = _row_tile(rh, mult, max(mult, (512 * 1024) // cols))

    def body(core_ref, g_ref, a_ref, o_ref):
        o_ref[...] = (g_ref[0].astype(F32) + a_ref[...].astype(F32)).astype(o_ref.dtype)

    grid_spec = pltpu.PrefetchScalarGridSpec(
        num_scalar_prefetch=1, grid=(N_CHIPS, rh // tr),
        in_specs=[pl.BlockSpec((1, 1, tr, cols), lambda s, i, core_ref: (s, core_ref[0], i, 0)),
                  pl.BlockSpec((1, tr, cols), lambda s, i, core_ref: (s, i, 0))],
        out_specs=pl.BlockSpec((1, tr, cols), lambda s, i, core_ref: (s, i, 0)))
    return pl.pallas_call(
        body, name=name, grid_spec=grid_spec, out_shape=jax.ShapeDtypeStruct((N_CHIPS, rh, cols), g.dtype),
        compiler_params=_params(("parallel", "parallel")),
    )(core, g, arrived)


def _scatter_ici(h, *, name):
    def body(h_ref, o_ref, send_sems, recv_sems, local_sem):
        x, y, c = _mesh_pos()
        mine = 2 * x + y
        local = pltpu.make_async_copy(h_ref.at[mine], o_ref.at[mine], local_sem)
        local.start()
        sends = []
        for k, (px, py) in enumerate(_other_chips(x, y)):
            cp = _remote(h_ref.at[2 * px + py], o_ref.at[mine], send_sems, recv_sems, k, (px, py, c))
            cp.start()
            sends.append(cp)
        for k, (px, py) in enumerate(_other_chips(x, y)):
            _remote(h_ref.at[mine], o_ref.at[2 * px + py], send_sems, recv_sems, k, (px, py, c)).wait_recv()
        for cp in sends:
            cp.wait_send()
        local.wait()

    return pl.pallas_call(
        body, name=name, in_specs=[HBM], out_specs=HBM, out_shape=jax.ShapeDtypeStruct(h.shape, h.dtype),
        scratch_shapes=_dma_sems(3) + [pltpu.SemaphoreType.DMA(())],
    )(h)


def _sum_chips(parts, *, name):
    _, rh, cols = parts.shape
    mult = 16 if parts.dtype == BF16 else 8
    tr = _row_tile(rh, mult, max(mult, (512 * 1024) // cols))

    def body(p_ref, o_ref):
        acc = p_ref[0].astype(F32)
        for s in range(1, N_CHIPS):
            acc = acc + p_ref[s].astype(F32)
        o_ref[...] = acc

    return pl.pallas_call(
        body, name=name, grid=(rh // tr,),
        in_specs=[pl.BlockSpec((N_CHIPS, tr, cols), lambda i: (0, i, 0))],
        out_specs=pl.BlockSpec((tr, cols), lambda i: (i, 0)),
        out_shape=jax.ShapeDtypeStruct((rh, cols), F32), compiler_params=_params(("parallel",)),
    )(parts)


def _share_d2d(f, *, name):
    rh, cols = f.shape

    def body(f_ref, o_ref, send_sems, recv_sems, local_sem):
        x, y, c = _mesh_pos()
        sibling = (x, y, 1 - c)
        local = pltpu.make_async_copy(f_ref, o_ref.at[c], local_sem)
        local.start()
        cp = _remote(f_ref, o_ref.at[c], send_sems, recv_sems, 0, sibling)
        cp.start()
        _remote(f_ref, o_ref.at[1 - c], send_sems, recv_sems, 0, sibling).wait_recv()
        cp.wait_send()
        local.wait()

    return pl.pallas_call(
        body, name=name, in_specs=[HBM], out_specs=HBM, out_shape=jax.ShapeDtypeStruct((2, rh, cols), f.dtype),
        scratch_shapes=_dma_sems(1) + [pltpu.SemaphoreType.DMA(())],
    )(f)


def _reduce_scatter_chips(g, core, name):
    _, rows, cols = g.shape
    g = g.reshape(N_CHIPS, 2, rows // 2, cols)
    arrived = _swap_halves_d2d(g, name=name + "_swap")
    chip_sum = _add_own_half(g, arrived, core, name=name + "_add2")
    parts = _scatter_ici(chip_sum, name=name + "_ici")
    total = _sum_chips(parts, name=name + "_sum4")
    return _share_d2d(total, name=name + "_share").reshape(rows, cols)


BIG = ("w_in", "w_branch", "w_out", "w_up", "w_down")
BIG_COLUMN_SHARDED = ("w_in", "w_up")
CONV = ("conv_a_w", "conv_f_w")
REPLICATED = ("norm1_w", "b_gate", "conv_a_b", "dt_bias", "a_log", "d_skip", "ssd_norm_w", "uv_b", "v_ln_w",
              "v_ln_b", "w_spatial", "b_spatial", "norm2_w", "conv_f_b", "final_norm_w")
WEIGHT_ORDER = ("norm1_w", "w_in", "b_gate", "conv_a_w", "conv_a_b", "dt_bias", "a_log", "d_skip", "ssd_norm_w",
                "uv_b", "v_ln_w", "v_ln_b", "w_spatial", "b_spatial", "w_branch", "w_out", "norm2_w", "w_up",
                "conv_f_w", "conv_f_b", "w_down", "final_norm_w")
SMALL_EXCHANGE_ROWS = 64


def _flat_rows(arrays, row_multiple):
    flat = jnp.concatenate([a.reshape(-1) for a in arrays])
    rows = -(-flat.shape[0] // (LANES * row_multiple)) * row_multiple
    return jnp.pad(flat, (0, rows * LANES - flat.shape[0])).reshape(rows, LANES)


def _unflatten(flat, shapes):
    flat = flat.reshape(-1)
    out, off = [], 0
    for shp in shapes:
        n = math.prod(shp)
        out.append(flat[off:off + n].reshape(shp))
        off += n
    return out


def _from_chip_blocks(blocks, name):
    if name in BIG_COLUMN_SHARDED or name in CONV:
        k = blocks.shape[1]
        return jnp.transpose(blocks, (1, 0, 2)).reshape(k, -1)
    return blocks.reshape(-1, blocks.shape[-1])


def _to_chip_blocks(whole, name):
    if name in BIG_COLUMN_SHARDED or name in CONV:
        k, n = whole.shape
        return jnp.transpose(whole.reshape(k, N_CHIPS, n // N_CHIPS), (1, 0, 2))
    return whole.reshape(N_CHIPS, whole.shape[0] // N_CHIPS, whole.shape[1])


def kernel(x, norm1_w, w_in, b_gate, conv_a_w, conv_a_b, dt_bias, a_log, d_skip, ssd_norm_w, uv_b, v_ln_w, v_ln_b, w_spatial, b_spatial, w_branch, w_out, norm2_w, w_up, conv_f_w, conv_f_b, w_down, final_norm_w, loss_target, m_norm1_w, m_w_in, m_b_gate, m_conv_a_w, m_conv_a_b, m_dt_bias, m_a_log, m_d_skip, m_ssd_norm_w, m_uv_b, m_v_ln_w, m_v_ln_b, m_w_spatial, m_b_spatial, m_w_branch, m_w_out, m_norm2_w, m_w_up, m_conv_f_w, m_conv_f_b, m_w_down, m_final_norm_w, v_norm1_w, v_w_in, v_b_gate, v_conv_a_w, v_conv_a_b, v_dt_bias, v_a_log, v_d_skip, v_ssd_norm_w, v_uv_b, v_v_ln_w, v_v_ln_b, v_w_spatial, v_b_spatial, v_w_branch, v_w_out, v_norm2_w, v_w_up, v_conv_f_w, v_conv_f_b, v_w_down, v_final_norm_w):
    weights = dict(norm1_w=norm1_w, w_in=w_in, b_gate=b_gate, conv_a_w=conv_a_w, conv_a_b=conv_a_b, dt_bias=dt_bias,
                   a_log=a_log, d_skip=d_skip, ssd_norm_w=ssd_norm_w, uv_b=uv_b, v_ln_w=v_ln_w, v_ln_b=v_ln_b,
                   w_spatial=w_spatial, b_spatial=b_spatial, w_branch=w_branch, w_out=w_out, norm2_w=norm2_w,
                   w_up=w_up, conv_f_w=conv_f_w, conv_f_b=conv_f_b, w_down=w_down, final_norm_w=final_norm_w)
    mom1 = dict(norm1_w=m_norm1_w, w_in=m_w_in, b_gate=m_b_gate, conv_a_w=m_conv_a_w, conv_a_b=m_conv_a_b,
                dt_bias=m_dt_bias, a_log=m_a_log, d_skip=m_d_skip, ssd_norm_w=m_ssd_norm_w, uv_b=m_uv_b,
                v_ln_w=m_v_ln_w, v_ln_b=m_v_ln_b, w_spatial=m_w_spatial, b_spatial=m_b_spatial, w_branch=m_w_branch,
                w_out=m_w_out, norm2_w=m_norm2_w, w_up=m_w_up, conv_f_w=m_conv_f_w, conv_f_b=m_conv_f_b,
                w_down=m_w_down, final_norm_w=m_final_norm_w)
    mom2 = dict(norm1_w=v_norm1_w, w_in=v_w_in, b_gate=v_b_gate, conv_a_w=v_conv_a_w, conv_a_b=v_conv_a_b,
                dt_bias=v_dt_bias, a_log=v_a_log, d_skip=v_d_skip, ssd_norm_w=v_ssd_norm_w, uv_b=v_uv_b,
                v_ln_w=v_v_ln_w, v_ln_b=v_v_ln_b, w_spatial=v_w_spatial, b_spatial=v_b_spatial, w_branch=v_w_branch,
                w_out=v_w_out, norm2_w=v_norm2_w, w_up=v_w_up, conv_f_w=v_conv_f_w, conv_f_b=v_conv_f_b,
                w_down=v_w_down, final_norm_w=v_final_norm_w)
    chip = 2 * lax.axis_index("x") + lax.axis_index("y")
    core = lax.axis_index("c").astype(jnp.int32).reshape(1)

    blocks = {n: _all_gather_chips(weights[n][0].astype(BF16), "gather_" + n) for n in BIG}
    whole = {n: _from_chip_blocks(blocks[n], n) for n in BIG if n != "w_up"}
    conv_shapes = [weights[n].shape[1:] for n in CONV]
    conv_gathered = _all_gather_chips(_flat_rows([weights[n] for n in CONV], 16), "gather_conv").reshape(N_CHIPS, -1)
    off = 0
    for n, shp in zip(CONV, conv_shapes):
        size = math.prod(shp)
        whole[n] = _from_chip_blocks(conv_gathered[:, off:off + size].reshape((N_CHIPS,) + shp), n)
        off += size

    w_in_full = whole["w_in"]
    wts = {
        "in_z": w_in_full[:, :SSD_D_INNER],
        "in_xbc": w_in_full[:, SSD_D_INNER:SSD_D_INNER + SSD_XBC],
        "in_dt": jnp.pad(w_in_full[:, SSD_D_INNER + SSD_XBC:SSD_IN], ((0, 0), (0, LANES - SSD_HEADS))),
        "in_uv": w_in_full[:, SSD_IN:SSD_IN + 2 * SGU_WIDTH],
        "in_gate": w_in_full[:, SSD_IN + 2 * SGU_WIDTH:],
        "in_gate_a": w_in_full[:, SSD_IN + 2 * SGU_WIDTH:SSD_IN + 2 * SGU_WIDTH + D_MODEL],
        "in_gate_b": w_in_full[:, SSD_IN + 2 * SGU_WIDTH + D_MODEL:],
        "branch_a": whole["w_branch"][:SSD_D_INNER], "branch_b": whole["w_branch"][SSD_D_INNER:],
        "out": whole["w_out"], "up": blocks["w_up"], "down": whole["w_down"],
    }
    small = {n: weights[n] for n in REPLICATED}
    small["conv_a_w"] = whole["conv_a_w"]
    small["conv_f_w"] = whole["conv_f_w"]

    loss, dx, grads_big, grads_small = _local_step(x[0], loss_target[0], wts, small)

    grads = {}
    for n in BIG:
        g_blocks = grads_big[n] if n == "w_up" else _to_chip_blocks(grads_big[n], n)
        grads[n] = _reduce_scatter_chips(g_blocks, core, "reduce_" + n)

    small_names = REPLICATED + CONV
    small_shapes = [grads_small[n].shape for n in small_names]
    g_small = _flat_rows([grads_small[n] for n in small_names], N_CHIPS * 2 * SMALL_EXCHANGE_ROWS)
    red_small = _reduce_scatter_chips(g_small.reshape(N_CHIPS, -1, LANES), core, "reduce_small")
    all_small = _all_gather_chips(red_small, "gather_small")
    for n, g in zip(small_names, _unflatten(all_small, small_shapes)):
        if n in CONV:
            width = g.shape[1] // N_CHIPS
            g = lax.dynamic_slice_in_dim(g, chip * width, width, axis=1)
        grads[n] = g.reshape(weights[n].shape[1:]) if n != "final_norm_w" else g

    delta, new_m, new_v = {}, {}, {}
    for n in BIG:
        shp = weights[n].shape
        tr = _row_tile(shp[1], 8, 128)
        d, m1, m2 = _adamw(weights[n][0], grads[n], mom1[n][0], mom2[n][0], name="adamw_" + n, tr=tr)
        delta[n], new_m[n], new_v[n] = d.reshape(shp), m1.reshape(shp), m2.reshape(shp)
    small_all = [n for n in WEIGHT_ORDER if n not in BIG]
    shapes = [weights[n].shape for n in small_all]
    packed = [_flat_rows([src[n] for n in small_all], 8)
              for src in (weights, {n: grads[n] for n in small_all}, mom1, mom2)]
    d, m1, m2 = _adamw(*packed, name="adamw_small", tr=packed[0].shape[0])
    for n, dv, mv, vv in zip(small_all, _unflatten(d, shapes), _unflatten(m1, shapes), _unflatten(m2, shapes)):
        delta[n], new_m[n], new_v[n] = dv, mv, vv

    total_loss = lax.psum(loss[0, 0], ("x", "y", "c"))
    grad_out = [grads[n].reshape(weights[n].shape) for n in WEIGHT_ORDER]
    return (total_loss, dx[None], *grad_out, *[delta[n] for n in WEIGHT_ORDER], *[new_m[n] for n in WEIGHT_ORDER],
            *[new_v[n] for n in WEIGHT_ORDER])
```

```python
import functools
import math

import jax
import jax.numpy as jnp
from jax import lax
from jax.experimental import pallas as pl
from jax.experimental.pallas import tpu as pltpu

F32 = jnp.float32
BF16 = jnp.bfloat16
HI = lax.Precision.HIGHEST

D_MODEL = 1024
SSD_D_INNER = 2048
SSD_HEADS = 32
SSD_HEAD_DIM = 64
SSD_GROUPS = 4
SSD_HEADS_PER_GROUP = 8
SSD_STATE = 128
SSD_BC = 512
SSD_XBC = 3072
SSD_IN = 5152
SGU_WIDTH = 1024
SGU_GROUPS = 8
CHUNK = 128
IN_COLS = 9248
D_FF = 2816
NORM_EPS = 1e-6
LN_EPS = 1e-5
GROUP_COLS = SSD_HEADS_PER_GROUP * SSD_HEAD_DIM
LANES = 128

ADAM_LR = 0.001
ADAM_B1 = 0.9
ADAM_B2 = 0.999
ADAM_EPS = 1e-08
ADAM_WD = 0.01
ADAM_STEP = 10

N_CHIPS = 4
VMEM_LIMIT = 56 * 1024 * 1024

NT = (((1,), (1,)), ((), ()))
TN = (((0,), (0,)), ((), ()))
NN = (((1,), (0,)), ((), ()))


def _params(dims):
    return pltpu.CompilerParams(dimension_semantics=dims, vmem_limit_bytes=VMEM_LIMIT)


def _dot(a, b, dn=NN, precision=None):
    return lax.dot_general(a, b, dn, precision=precision, preferred_element_type=F32)


def _sigmoid(x):
    return 1.0 / (1.0 + jnp.exp(-x))


def _softplus(x):
    return jnp.maximum(x, 0.0) + jnp.log(1.0 + jnp.exp(-jnp.abs(x)))


def _matmul(pairs, *, trans_b=False, add=None, out_dtype=F32, tm=512, tn=512, name):
    def mat_shape(b):
        if isinstance(b, tuple) and b[1] == "cols":
            return (b[0].shape[1], b[0].shape[0] * b[0].shape[2])
        return b[0].shape[1:] if isinstance(b, tuple) else b.shape

    if isinstance(pairs[0][1], tuple) and pairs[0][1][1] == "cols":
        assert not trans_b and tn % LANES == 0 and pairs[0][1][0].shape[2] % tn == 0, name

    m = (pairs[0][0][0] if isinstance(pairs[0][0], tuple) else pairs[0][0]).shape[0]
    n = mat_shape(pairs[0][1])[0] if trans_b else mat_shape(pairs[0][1])[1]
    tm, tn = min(tm, m), min(tn, n)
    assert m % tm == 0 and n % tn == 0, (name, m, n, tm, tn)
    npairs = len(pairs)
    dn = NT if trans_b else NN

    def body(*refs):
        o_ref = refs[-1]
        acc = None
        for i in range(npairs):
            p = _dot(refs[2 * i][...].astype(BF16), refs[2 * i + 1][...].astype(BF16), dn)
            acc = p if acc is None else acc + p
        if add is not None:
            acc = acc + refs[2 * npairs][...]
        o_ref[...] = acc.astype(out_dtype)

    in_specs, args = [], []
    for a, b in pairs:
        bshape = mat_shape(b)
        k = bshape[1] if trans_b else bshape[0]
        assert bshape == ((n, k) if trans_b else (k, n)), (name, bshape)
        a, qa = a if isinstance(a, tuple) else (a, 0)
        assert a.shape[0] == m and a.shape[1] % k == 0, (name, a.shape, k)
        in_specs.append(pl.BlockSpec((tm, k), lambda i, j, qa=qa: (i, qa)))
        if isinstance(b, tuple) and b[1] == "cols":
            b = b[0]
            per = b.shape[2] // tn
            in_specs.append(pl.BlockSpec((None, k, tn), lambda i, j, per=per: (j // per, 0, j % per)))
        elif isinstance(b, tuple):
            b, qb = b
            if trans_b:
                in_specs.append(pl.BlockSpec((None, tn, k), lambda i, j, qb=qb: (qb, j, 0)))
            else:
                in_specs.append(pl.BlockSpec((None, k, tn), lambda i, j, qb=qb: (qb, 0, j)))
        elif trans_b:
            in_specs.append(pl.BlockSpec((tn, k), lambda i, j: (j, 0)))
        else:
            in_specs.append(pl.BlockSpec((k, tn), lambda i, j: (0, j)))
        args += [a, b]
    if add is not None:
        in_specs.append(pl.BlockSpec((tm, tn), lambda i, j: (i, j)))
        args.append(add)
    return pl.pallas_call(
        body, name=name, grid=(m // tm, n // tn), in_specs=in_specs,
        out_specs=pl.BlockSpec((tm, tn), lambda i, j: (i, j)),
        out_shape=jax.ShapeDtypeStruct((m, n), out_dtype),
        compiler_params=_params(("parallel", "parallel")),
    )(*args)


def _matmul_tn(a, b, *, tk, tn, tm=1024, out_dtype=BF16, stack_out=False, name):
    m, k = a.shape
    n = b.shape[1]
    tm, tk, tn = min(tm, m), min(tk, k), min(tn, n)
    assert m % tm == 0 and k % tk == 0 and n % tn == 0, (name, m, k, n)
    nm = m // tm
    if stack_out:
        out_spec = pl.BlockSpec((None, tk, tn), lambda i, j, l: (j, i, 0))
        out_shape = jax.ShapeDtypeStruct((n // tn, k, tn), out_dtype)
    else:
        out_spec = pl.BlockSpec((tk, tn), lambda i, j, l: (i, j))
        out_shape = jax.ShapeDtypeStruct((k, n), out_dtype)

    def body(a_ref, b_ref, o_ref, acc):
        mi = pl.program_id(2)

        @pl.when(mi == 0)
        def _():
            acc[...] = jnp.zeros_like(acc)

        acc[...] += _dot(a_ref[...].astype(BF16), b_ref[...].astype(BF16), TN)

        @pl.when(mi == nm - 1)
        def _():
            o_ref[...] = acc[...].astype(out_dtype)

    return pl.pallas_call(
        body, name=name, grid=(k // tk, n // tn, nm),
        in_specs=[pl.BlockSpec((tm, tk), lambda i, j, l: (l, i)), pl.BlockSpec((tm, tn), lambda i, j, l: (l, j))],
        out_specs=out_spec, out_shape=out_shape,
        scratch_shapes=[pltpu.VMEM((tk, tn), F32)],
        compiler_params=_params(("parallel", "parallel", "arbitrary")),
    )(a, b)


def _rms_fwd(x, w, *, name, tm=512):
    s, d = x.shape
    tm = min(tm, s)

    def body(x_ref, w_ref, o_ref):
        xv = x_ref[...]
        r = lax.rsqrt(jnp.mean(xv * xv, axis=-1, keepdims=True) + NORM_EPS)
        o_ref[...] = (xv * r * w_ref[...]).astype(BF16)

    return pl.pallas_call(
        body, name=name, grid=(s // tm,),
        in_specs=[pl.BlockSpec((tm, d), lambda i: (i, 0)), pl.BlockSpec((1, d), lambda i: (0, 0))],
        out_specs=pl.BlockSpec((tm, d), lambda i: (i, 0)),
        out_shape=jax.ShapeDtypeStruct((s, d), BF16),
        compiler_params=_params(("parallel",)),
    )(x, w)


def _rms_bwd(x, w, dn, dres, *, name, tm=512):
    s, d = x.shape
    tm = min(tm, s)

    def body(x_ref, w_ref, dn_ref, dres_ref, dx_ref, dxb_ref, dw_ref):
        @pl.when(pl.program_id(0) == 0)
        def _():
            dw_ref[...] = jnp.zeros_like(dw_ref)

        xv = x_ref[...]
        r = lax.rsqrt(jnp.mean(xv * xv, axis=-1, keepdims=True) + NORM_EPS)
        xhat = xv * r
        dnv = dn_ref[...]
        dxhat = dnv * w_ref[...]
        dx = dres_ref[...] + r * (dxhat - xhat * jnp.mean(dxhat * xhat, axis=-1, keepdims=True))
        dx_ref[...] = dx
        dxb_ref[...] = dx.astype(BF16)
        dw_ref[...] += jnp.sum(dnv * xhat, axis=0, keepdims=True)

    tile = pl.BlockSpec((tm, d), lambda i: (i, 0))
    row = pl.BlockSpec((1, d), lambda i: (0, 0))
    return pl.pallas_call(
        body, name=name, grid=(s // tm,),
        in_specs=[tile, row, tile, tile], out_specs=[tile, tile, row],
        out_shape=[jax.ShapeDtypeStruct((s, d), F32), jax.ShapeDtypeStruct((s, d), BF16),
                   jax.ShapeDtypeStruct((1, d), F32)],
        compiler_params=_params(("arbitrary",)),
    )(x, w, dn, dres)


def _final_fwd_bwd(h2, wf, target, *, name, tm=512):
    s, d = h2.shape
    tm = min(tm, s)

    def body(h_ref, w_ref, t_ref, loss_ref, dh_ref, dhb_ref, dw_ref):
        @pl.when(pl.program_id(0) == 0)
        def _():
            dw_ref[...] = jnp.zeros_like(dw_ref)
            loss_ref[...] = jnp.zeros_like(loss_ref)

        hv = h_ref[...]
        r = lax.rsqrt(jnp.mean(hv * hv, axis=-1, keepdims=True) + NORM_EPS)
        xhat = hv * r
        err = xhat * w_ref[...] - t_ref[...]
        per_tok = jnp.mean(err * err, axis=-1, keepdims=True)
        loss_ref[...] += 0.5 * jnp.sum(per_tok, axis=0, keepdims=True)
        dy = err * (1.0 / d)
        dxhat = dy * w_ref[...]
        dh = r * (dxhat - xhat * jnp.mean(dxhat * xhat, axis=-1, keepdims=True))
        dh_ref[...] = dh
        dhb_ref[...] = dh.astype(BF16)
        dw_ref[...] += jnp.sum(dy * xhat, axis=0, keepdims=True)

    tile = pl.BlockSpec((tm, d), lambda i: (i, 0))
    row = pl.BlockSpec((1, d), lambda i: (0, 0))
    return pl.pallas_call(
        body, name=name, grid=(s // tm,),
        in_specs=[tile, row, tile],
        out_specs=[pl.BlockSpec((1, 1), lambda i: (0, 0)), tile, tile, row],
        out_shape=[jax.ShapeDtypeStruct((1, 1), F32), jax.ShapeDtypeStruct((s, d), F32),
                   jax.ShapeDtypeStruct((s, d), BF16), jax.ShapeDtypeStruct((1, d), F32)],
        compiler_params=_params(("arbitrary",)),
    )(h2, wf, target)


def _shift_down(x, k):
    if k == 0:
        return x
    rows = lax.broadcasted_iota(jnp.int32, x.shape, 0)
    return jnp.where(rows >= k, pltpu.roll(x, k, 0), 0.0)


def _shift_up(x, k):
    if k == 0:
        return x
    s = x.shape[0]
    rows = lax.broadcasted_iota(jnp.int32, x.shape, 0)
    return jnp.where(rows < s - k, pltpu.roll(x, s - k, 0), 0.0)


def _conv_taps(x, w_ref, kk):
    acc = None
    for i in range(kk):
        term = w_ref[i:i + 1, :] * _shift_down(x, kk - 1 - i)
        acc = term if acc is None else acc + term
    return acc


def _conv_a_fwd(xraw, w, b, *, name, tc=128):
    s, c = xraw.shape
    kk = 4

    def body(x_ref, w_ref, b_ref, o_ref):
        pre = _conv_taps(x_ref[...], w_ref, kk) + b_ref[...]
        o_ref[...] = pre * _sigmoid(pre)

    col = pl.BlockSpec((s, tc), lambda j: (0, j))
    return pl.pallas_call(
        body, name=name, grid=(c // tc,),
        in_specs=[col, pl.BlockSpec((8, tc), lambda j: (0, j)), pl.BlockSpec((1, tc), lambda j: (0, j))],
        out_specs=col, out_shape=jax.ShapeDtypeStruct((s, c), F32),
        compiler_params=_params(("parallel",)),
    )(xraw, w, b)


def _conv_a_bwd(xraw, w, b, dy, *, name, tc=128):
    s, c = xraw.shape
    kk = 4

    def body(x_ref, w_ref, b_ref, dy_ref, dx_ref, dw_ref, db_ref):
        x = x_ref[...]
        pre = _conv_taps(x, w_ref, kk) + b_ref[...]
        sg = _sigmoid(pre)
        dpre = dy_ref[...] * (sg * (1.0 + pre * (1.0 - sg)))
        db_ref[...] = jnp.sum(dpre, axis=0, keepdims=True)
        dx = None
        rows = []
        for i in range(kk):
            rows.append(jnp.sum(dpre * _shift_down(x, kk - 1 - i), axis=0, keepdims=True))
            term = w_ref[i:i + 1, :] * _shift_up(dpre, kk - 1 - i)
            dx = term if dx is None else dx + term
        dx_ref[...] = dx.astype(BF16)
        dw_ref[...] = jnp.concatenate(rows + [jnp.zeros((8 - kk, x.shape[1]), F32)], axis=0)

    col = pl.BlockSpec((s, tc), lambda j: (0, j))
    w8 = pl.BlockSpec((8, tc), lambda j: (0, j))
    row = pl.BlockSpec((1, tc), lambda j: (0, j))
    return pl.pallas_call(
        body, name=name, grid=(c // tc,),
        in_specs=[col, w8, row, col], out_specs=[col, w8, row],
        out_shape=[jax.ShapeDtypeStruct((s, c), BF16), jax.ShapeDtypeStruct((8, c), F32),
                   jax.ShapeDtypeStruct((1, c), F32)],
        compiler_params=_params(("parallel",)),
    )(xraw, w, b, dy)


def _conv_f_fwd(up_raw, w, b, *, name, tc=128):
    s, c2 = up_raw.shape
    c = c2 // 2
    nb = c // tc
    kk = 3

    def body(xa_ref, xv_ref, wa_ref, wv_ref, ba_ref, bv_ref, o_ref):
        a = _conv_taps(xa_ref[...], wa_ref, kk) + ba_ref[...]
        v = _conv_taps(xv_ref[...], wv_ref, kk) + bv_ref[...]
        o_ref[...] = (a * _sigmoid(a) * v).astype(BF16)

    col_a = pl.BlockSpec((s, tc), lambda j: (0, j))
    col_v = pl.BlockSpec((s, tc), lambda j: (0, j + nb))
    return pl.pallas_call(
        body, name=name, grid=(nb,),
        in_specs=[col_a, col_v, pl.BlockSpec((8, tc), lambda j: (0, j)), pl.BlockSpec((8, tc), lambda j: (0, j + nb)),
                  pl.BlockSpec((1, tc), lambda j: (0, j)), pl.BlockSpec((1, tc), lambda j: (0, j + nb))],
        out_specs=col_a, out_shape=jax.ShapeDtypeStruct((s, c), BF16),
        compiler_params=_params(("parallel",)),
    )(up_raw, up_raw, w, w, b, b)


def _conv_f_bwd(up_raw, w, b, dact, *, name, tc=128):
    s, c2 = up_raw.shape
    c = c2 // 2
    nb = c // tc
    kk = 3

    def body(xa_ref, xv_ref, wa_ref, wv_ref, ba_ref, bv_ref, d_ref,
             dxa_ref, dxv_ref, dwa_ref, dwv_ref, dba_ref, dbv_ref):
        xa, xv = xa_ref[...], xv_ref[...]
        a = _conv_taps(xa, wa_ref, kk) + ba_ref[...]
        v = _conv_taps(xv, wv_ref, kk) + bv_ref[...]
        sg = _sigmoid(a)
        d = d_ref[...]
        da = d * v * (sg * (1.0 + a * (1.0 - sg)))
        dv = d * (a * sg)
        for x, dp, w_ref, dx_ref, dw_ref, db_ref in ((xa, da, wa_ref, dxa_ref, dwa_ref, dba_ref),
                                                     (xv, dv, wv_ref, dxv_ref, dwv_ref, dbv_ref)):
            db_ref[...] = jnp.sum(dp, axis=0, keepdims=True)
            dx = None
            rows = []
            for i in range(kk):
                rows.append(jnp.sum(dp * _shift_down(x, kk - 1 - i), axis=0, keepdims=True))
                term = w_ref[i:i + 1, :] * _shift_up(dp, kk - 1 - i)
                dx = term if dx is None else dx + term
            dx_ref[...] = dx.astype(BF16)
            dw_ref[...] = jnp.concatenate(rows + [jnp.zeros((8 - kk, x.shape[1]), F32)], axis=0)

    col_a = pl.BlockSpec((s, tc), lambda j: (0, j))
    col_v = pl.BlockSpec((s, tc), lambda j: (0, j + nb))
    w_a = pl.BlockSpec((8, tc), lambda j: (0, j))
    w_v = pl.BlockSpec((8, tc), lambda j: (0, j + nb))
    r_a = pl.BlockSpec((1, tc), lambda j: (0, j))
    r_v = pl.BlockSpec((1, tc), lambda j: (0, j + nb))
    outs = pl.pallas_call(
        body, name=name, grid=(nb,),
        in_specs=[col_a, col_v, w_a, w_v, r_a, r_v, col_a],
        out_specs=[col_a, col_a, w_a, w_a, r_a, r_a],
        out_shape=[jax.ShapeDtypeStruct((s, c), BF16), jax.ShapeDtypeStruct((s, c), BF16),
                   jax.ShapeDtypeStruct((8, c), F32), jax.ShapeDtypeStruct((8, c), F32),
                   jax.ShapeDtypeStruct((1, c), F32), jax.ShapeDtypeStruct((1, c), F32)],
        compiler_params=_params(("parallel",)),
    )(up_raw, up_raw, w, w, b, b, dact)
    return outs


def _tri_masks():
    row = lax.broadcasted_iota(jnp.int32, (CHUNK, CHUNK), 0)
    col = lax.broadcasted_iota(jnp.int32, (CHUNK, CHUNK), 1)
    return row >= col, row <= col


def _ssd_fwd(xbc, dt_raw, z, dt_bias, a_log, a_log_x, d_skip_x, norm_w, expand, *, name):
    s = xbc.shape[0]
    nc = s // CHUNK

    def body(xbc_ref, dtr_ref, z_ref, dtb_ref, alog_ref, alogx_ref, dskx_ref, nw_ref, e_ref,
             y_ref, ya_ref, st_ref, state):
        @pl.when(pl.program_id(0) == 0)
        def _():
            state[...] = jnp.zeros_like(state)

        st_ref[0] = state[...]
        lower, _ = _tri_masks()
        tril = lower.astype(F32)
        dt = _softplus(dtr_ref[...] + dtb_ref[...])
        adt = dt * (-jnp.exp(alog_ref[...]))
        acum = _dot(tril, adt, precision=HI)
        acum_t = acum.T
        for g in range(SSD_GROUPS):
            sl = slice(GROUP_COLS * g, GROUP_COLS * (g + 1))
            dt_x = _dot(dt, e_ref[:, sl], precision=HI)
            adt_x = dt_x * (-jnp.exp(alogx_ref[:, sl]))
            acum_x = _dot(tril, adt_x, precision=HI)
            tot_x = jnp.sum(adt_x, axis=0, keepdims=True)
            xs = xbc_ref[:, sl]
            xdt = xs * dt_x
            xdt_b = xdt.astype(BF16)
            bg = xbc_ref[:, SSD_D_INNER + SSD_STATE * g:SSD_D_INNER + SSD_STATE * (g + 1)].astype(BF16)
            cg = xbc_ref[:, SSD_D_INNER + SSD_BC + SSD_STATE * g:SSD_D_INNER + SSD_BC + SSD_STATE * (g + 1)].astype(BF16)
            cb = _dot(cg, bg, NT)
            st_g = state[:, sl]
            y_off = _dot(cg, st_g.astype(BF16)) * jnp.exp(acum_x)
            parts = []
            for r in range(SSD_HEADS_PER_GROUP):
                h = SSD_HEADS_PER_GROUP * g + r
                dec = jnp.exp(jnp.where(lower, acum[:, h:h + 1] - acum_t[h:h + 1, :], -jnp.inf))
                parts.append(_dot((cb * dec).astype(BF16), xdt_b[:, SSD_HEAD_DIM * r:SSD_HEAD_DIM * (r + 1)]))
            y_ref[:, sl] = jnp.concatenate(parts, axis=1) + y_off + dskx_ref[:, sl] * xs
            wgt = (xdt * jnp.exp(tot_x - acum_x)).astype(BF16)
            state[:, sl] = st_g * jnp.exp(tot_x) + _dot(bg, wgt, TN)
        zv = z_ref[...]
        q = y_ref[...] * (zv * _sigmoid(zv))
        r = lax.rsqrt(jnp.mean(q * q, axis=-1, keepdims=True) + NORM_EPS)
        ya_ref[...] = (q * r * nw_ref[...]).astype(BF16)

    def chunk(w):
        return pl.BlockSpec((CHUNK, w), lambda c: (c, 0))

    def const(shape):
        return pl.BlockSpec(shape, lambda c: (0,) * len(shape))

    return pl.pallas_call(
        body, name=name, grid=(nc,),
        in_specs=[chunk(SSD_XBC), chunk(LANES), chunk(SSD_D_INNER), const((1, LANES)), const((1, LANES)),
                  const((1, SSD_D_INNER)), const((1, SSD_D_INNER)), const((1, SSD_D_INNER)),
                  const((LANES, SSD_D_INNER))],
        out_specs=[chunk(SSD_D_INNER), chunk(SSD_D_INNER),
                   pl.BlockSpec((1, SSD_STATE, SSD_D_INNER), lambda c: (c, 0, 0))],
        out_shape=[jax.ShapeDtypeStruct((s, SSD_D_INNER), F32), jax.ShapeDtypeStruct((s, SSD_D_INNER), BF16),
                   jax.ShapeDtypeStruct((nc, SSD_STATE, SSD_D_INNER), F32)],
        scratch_shapes=[pltpu.VMEM((SSD_STATE, SSD_D_INNER), F32)],
        compiler_params=_params(("arbitrary",)),
    )(xbc, dt_raw, z, dt_bias, a_log, a_log_x, d_skip_x, norm_w, expand)


def _ssd_bwd(dya, y, z, xbc, dt_raw, states, dt_bias, a_log, a_log_x, d_skip_x, norm_w, expand, expand_t, *, name):
    s = xbc.shape[0]
    nc = s // CHUNK

    def body(dya_ref, y_ref, z_ref, xbc_ref, dtr_ref, stp_ref, dtb_ref, alog_ref, alogx_ref, dskx_ref, nw_ref,
             e_ref, et_ref, dz_ref, dxbc_ref, ddt_ref, dnw_ref, ddsk_ref, dalog_ref, ddtb_ref,
             dstate, dy_sc, dskcol):
        i = pl.program_id(0)

        @pl.when(i == 0)
        def _():
            dstate[...] = jnp.zeros_like(dstate)
            dskcol[...] = jnp.zeros_like(dskcol)
            dnw_ref[...] = jnp.zeros_like(dnw_ref)
            dalog_ref[...] = jnp.zeros_like(dalog_ref)
            ddtb_ref[...] = jnp.zeros_like(ddtb_ref)
            ddsk_ref[...] = jnp.zeros_like(ddsk_ref)

        lower, upper = _tri_masks()
        tril = lower.astype(F32)
        rows = lax.broadcasted_iota(jnp.int32, (CHUNK, LANES), 0)
        pre = dtr_ref[...] + dtb_ref[...]
        dt = _softplus(pre)
        a = -jnp.exp(alog_ref[...])
        adt = dt * a
        acum = _dot(tril, adt, precision=HI)
        acum_t = acum.T

        yv = y_ref[...]
        zv = z_ref[...]
        sz = _sigmoid(zv)
        silu_z = zv * sz
        q = yv * silu_z
        r = lax.rsqrt(jnp.mean(q * q, axis=-1, keepdims=True) + NORM_EPS)
        qhat = q * r
        dyav = dya_ref[...]
        dqhat = dyav * nw_ref[...]
        dnw_ref[...] += jnp.sum(dyav * qhat, axis=0, keepdims=True)
        dq = r * (dqhat - qhat * jnp.mean(dqhat * qhat, axis=-1, keepdims=True))
        dy_sc[...] = dq * silu_z
        dz_ref[...] = (dq * yv * (sz * (1.0 + zv * (1.0 - sz)))).astype(BF16)

        da_cum = jnp.zeros((CHUNK, LANES), F32)
        ddt = jnp.zeros((CHUNK, LANES), F32)
        for g in range(SSD_GROUPS):
            sl = slice(GROUP_COLS * g, GROUP_COLS * (g + 1))
            et_g = et_ref[sl, :]
            dt_x = _dot(dt, e_ref[:, sl], precision=HI)
            adt_x = dt_x * (-jnp.exp(alogx_ref[:, sl]))
            acum_x = _dot(tril, adt_x, precision=HI)
            tot_x = jnp.sum(adt_x, axis=0, keepdims=True)
            e_tot = jnp.exp(tot_x)
            dec_s = jnp.exp(tot_x - acum_x)
            xs = xbc_ref[:, sl]
            xdt = xs * dt_x
            xdt_b = xdt.astype(BF16)
            dy = dy_sc[:, sl]
            dy_b = dy.astype(BF16)
            dskx = dskx_ref[:, sl]
            y_ssd = y_ref[:, sl] - dskx * xs
            dskcol[:, sl] += jnp.sum(dy * xs, axis=0, keepdims=True)
            bg = xbc_ref[:, SSD_D_INNER + SSD_STATE * g:SSD_D_INNER + SSD_STATE * (g + 1)].astype(BF16)
            cg = xbc_ref[:, SSD_D_INNER + SSD_BC + SSD_STATE * g:SSD_D_INNER + SSD_BC + SSD_STATE * (g + 1)].astype(BF16)
            cb = _dot(cg, bg, NT)
            sp = stp_ref[0, :, sl]
            ds_g = dstate[:, sl]
            ds_b = ds_g.astype(BF16)
            dye_b = (dy * jnp.exp(acum_x)).astype(BF16)
            dc = _dot(dye_b, sp.astype(BF16), NT)
            dxdt_state = dec_s * _dot(bg, ds_b)
            db = _dot((xdt * dec_s).astype(BF16), ds_b, NT)
            dcb = jnp.zeros((CHUNK, CHUNK), F32)
            parts = []
            for rr in range(SSD_HEADS_PER_GROUP):
                h = SSD_HEADS_PER_GROUP * g + rr
                hs = slice(SSD_HEAD_DIM * rr, SSD_HEAD_DIM * (rr + 1))
                dec = jnp.exp(jnp.where(lower, acum[:, h:h + 1] - acum_t[h:h + 1, :], -jnp.inf))
                parts.append(_dot((cb * dec).astype(BF16), dy_b[:, hs], TN))
                dcb = dcb + _dot(dy_b[:, hs], xdt_b[:, hs], NT) * dec
            dxdt = jnp.concatenate(parts, axis=1) + dxdt_state
            dcb_b = dcb.astype(BF16)
            dc = dc + _dot(dcb_b, bg)
            db = db + _dot(dcb_b, cg, TN)
            tot_col = jnp.sum(ds_g * sp, axis=0, keepdims=True) * e_tot + jnp.sum(dxdt_state * xdt, axis=0, keepdims=True)
            d_tot = _dot(jnp.broadcast_to(tot_col, (8, GROUP_COLS)), et_g, precision=HI)
            d_tot = jnp.max(d_tot, axis=0, keepdims=True)
            pair_sums = dy_b.astype(F32) * y_ssd - xdt_b.astype(F32) * dxdt
            da_cum = da_cum + _dot(pair_sums, et_g, precision=HI) + jnp.where(rows == CHUNK - 1, d_tot, 0.0)
            ddt = ddt + _dot(dxdt * xs, et_g, precision=HI)
            dxbc_ref[:, sl] = dy * dskx + dxdt * dt_x
            dxbc_ref[:, SSD_D_INNER + SSD_STATE * g:SSD_D_INNER + SSD_STATE * (g + 1)] = db
            dxbc_ref[:, SSD_D_INNER + SSD_BC + SSD_STATE * g:SSD_D_INNER + SSD_BC + SSD_STATE * (g + 1)] = dc
            dstate[:, sl] = e_tot * ds_g + _dot(cg, dye_b, TN)

        dadt = _dot(upper.astype(F32), da_cum, precision=HI)
        ddt = ddt + dadt * a
        dalog_ref[...] += jnp.sum(dadt * dt, axis=0, keepdims=True)
        dpre = ddt * _sigmoid(pre)
        ddtb_ref[...] += jnp.sum(dpre, axis=0, keepdims=True)
        ddt_ref[...] = dpre.astype(BF16)

        @pl.when(i == nc - 1)
        def _():
            dalog_ref[...] = dalog_ref[...] * a
            dsk = _dot(jnp.broadcast_to(dskcol[...], (8, SSD_D_INNER)), et_ref[...], precision=HI)
            ddsk_ref[...] = jnp.max(dsk, axis=0, keepdims=True)

    def chunk(w):
        return pl.BlockSpec((CHUNK, w), lambda i: (nc - 1 - i, 0))

    def const(shape):
        return pl.BlockSpec(shape, lambda i: (0,) * len(shape))

    return pl.pallas_call(
        body, name=name, grid=(nc,),
        in_specs=[chunk(SSD_D_INNER), chunk(SSD_D_INNER), chunk(SSD_D_INNER), chunk(SSD_XBC), chunk(LANES),
                  pl.BlockSpec((1, SSD_STATE, SSD_D_INNER), lambda i: (nc - 1 - i, 0, 0)),
                  const((1, LANES)), const((1, LANES)), const((1, SSD_D_INNER)), const((1, SSD_D_INNER)),
                  const((1, SSD_D_INNER)), const((LANES, SSD_D_INNER)), const((SSD_D_INNER, LANES))],
        out_specs=[chunk(SSD_D_INNER), chunk(SSD_XBC), chunk(LANES), const((1, SSD_D_INNER)), const((1, LANES)),
                   const((1, LANES)), const((1, LANES))],
        out_shape=[jax.ShapeDtypeStruct((s, SSD_D_INNER), BF16), jax.ShapeDtypeStruct((s, SSD_XBC), F32),
                   jax.ShapeDtypeStruct((s, LANES), BF16), jax.ShapeDtypeStruct((1, SSD_D_INNER), F32),
                   jax.ShapeDtypeStruct((1, LANES), F32), jax.ShapeDtypeStruct((1, LANES), F32),
                   jax.ShapeDtypeStruct((1, LANES), F32)],
        scratch_shapes=[pltpu.VMEM((SSD_STATE, SSD_D_INNER), F32), pltpu.VMEM((CHUNK, SSD_D_INNER), F32),
                        pltpu.VMEM((1, SSD_D_INNER), F32)],
        compiler_params=_params(("arbitrary",)),
    )(dya, y, z, xbc, dt_raw, states, dt_bias, a_log, a_log_x, d_skip_x, norm_w, expand, expand_t)


GELU_K = math.sqrt(2.0 / math.pi)
GELU_C = 0.044715


def _gelu(x):
    return 0.5 * x * (1.0 + jnp.tanh(GELU_K * (x + GELU_C * x * x * x)))


def _gelu_grad(x):
    t = jnp.tanh(GELU_K * (x + GELU_C * x * x * x))
    return 0.5 * (1.0 + t) + 0.5 * x * (1.0 - t * t) * (GELU_K * (1.0 + 3.0 * GELU_C * x * x))


def _sgu_pre(uv_ref, uvb_ref, lnw_ref, lnb_ref):
    uv = uv_ref[...] + uvb_ref[...]
    guv = _gelu(uv)
    u = guv[:, :SGU_WIDTH]
    v = guv[:, SGU_WIDTH:]
    mu = jnp.mean(v, axis=-1, keepdims=True)
    vc = v - mu
    rstd = lax.rsqrt(jnp.mean(vc * vc, axis=-1, keepdims=True) + LN_EPS)
    vhat = vc * rstd
    vn = vhat * lnw_ref[...] + lnb_ref[...]
    return uv, u, vhat, rstd, vn


def _sgu_fwd(uv_raw, uv_b, ln_w, ln_b, w_sp, b_sp_t, *, name):
    s = uv_raw.shape[0]
    nc = s // CHUNK

    def body(uv_ref, uvb_ref, lnw_ref, lnb_ref, w_ref, bt_ref, o_ref):
        lower, _ = _tri_masks()
        _, u, _, _, vn = _sgu_pre(uv_ref, uvb_ref, lnw_ref, lnb_ref)
        vn_b = vn.astype(BF16)
        bt = bt_ref[...]
        for g in range(SGU_GROUPS):
            gs = slice(LANES * g, LANES * (g + 1))
            wc = jnp.where(lower, w_ref[g], 0.0).astype(BF16)
            mixed = _dot(wc, vn_b[:, gs]) + bt[:, g:g + 1]
            o_ref[:, gs] = (u[:, gs] * mixed).astype(BF16)

    def const(shape):
        return pl.BlockSpec(shape, lambda c: (0,) * len(shape))

    return pl.pallas_call(
        body, name=name, grid=(nc,),
        in_specs=[pl.BlockSpec((CHUNK, 2 * SGU_WIDTH), lambda c: (c, 0)), const((1, 2 * SGU_WIDTH)),
                  const((1, SGU_WIDTH)), const((1, SGU_WIDTH)), const((SGU_GROUPS, CHUNK, CHUNK)),
                  const((CHUNK, LANES))],
        out_specs=pl.BlockSpec((CHUNK, SGU_WIDTH), lambda c: (c, 0)),
        out_shape=jax.ShapeDtypeStruct((s, SGU_WIDTH), BF16),
        compiler_params=_params(("parallel",)),
    )(uv_raw, uv_b, ln_w, ln_b, w_sp, b_sp_t)


def _sgu_bwd(uv_raw, dyb, uv_b, ln_w, ln_b, w_sp, b_sp_t, group_sum, *, name):
    s = uv_raw.shape[0]
    nc = s // CHUNK

    def body(uv_ref, dy_ref, uvb_ref, lnw_ref, lnb_ref, w_ref, bt_ref, gsum_ref,
             duv_ref, dw_ref, dbt_ref, dlnw_ref, dlnb_ref, duvb_ref):
        @pl.when(pl.program_id(0) == 0)
        def _():
            dw_ref[...] = jnp.zeros_like(dw_ref)
            dbt_ref[...] = jnp.zeros_like(dbt_ref)
            dlnw_ref[...] = jnp.zeros_like(dlnw_ref)
            dlnb_ref[...] = jnp.zeros_like(dlnb_ref)
            duvb_ref[...] = jnp.zeros_like(duvb_ref)

        lower, _ = _tri_masks()
        uv, u, vhat, rstd, vn = _sgu_pre(uv_ref, uvb_ref, lnw_ref, lnb_ref)
        vn_b = vn.astype(BF16)
        bt = bt_ref[...]
        dy = dy_ref[...]
        du_parts, dvn_parts, dmix_parts = [], [], []
        for g in range(SGU_GROUPS):
            gs = slice(LANES * g, LANES * (g + 1))
            wc = jnp.where(lower, w_ref[g], 0.0).astype(BF16)
            mixed = _dot(wc, vn_b[:, gs]) + bt[:, g:g + 1]
            du_parts.append(dy[:, gs] * mixed)
            dmix = dy[:, gs] * u[:, gs]
            dmix_b = dmix.astype(BF16)
            dmix_parts.append(dmix)
            dw_ref[g] += jnp.where(lower, _dot(dmix_b, vn_b[:, gs], NT), 0.0)
            dvn_parts.append(_dot(wc, dmix_b, TN))
        dmixed = jnp.concatenate(dmix_parts, axis=1)
        dbt_ref[...] += _dot(dmixed, gsum_ref[...], precision=HI)
        dvn = jnp.concatenate(dvn_parts, axis=1)
        dlnw_ref[...] += jnp.sum(dvn * vhat, axis=0, keepdims=True)
        dlnb_ref[...] += jnp.sum(dvn, axis=0, keepdims=True)
        dvhat = dvn * lnw_ref[...]
        dv = rstd * (dvhat - jnp.mean(dvhat, axis=-1, keepdims=True)
                     - vhat * jnp.mean(dvhat * vhat, axis=-1, keepdims=True))
        dguv = jnp.concatenate(du_parts + [dv], axis=1)
        duv = dguv * _gelu_grad(uv)
        duvb_ref[...] += jnp.sum(duv, axis=0, keepdims=True)
        duv_ref[...] = duv.astype(BF16)

    def const(shape):
        return pl.BlockSpec(shape, lambda c: (0,) * len(shape))

    return pl.pallas_call(
        body, name=name, grid=(nc,),
        in_specs=[pl.BlockSpec((CHUNK, 2 * SGU_WIDTH), lambda c: (c, 0)),
                  pl.BlockSpec((CHUNK, SGU_WIDTH), lambda c: (c, 0)), const((1, 2 * SGU_WIDTH)),
                  const((1, SGU_WIDTH)), const((1, SGU_WIDTH)), const((SGU_GROUPS, CHUNK, CHUNK)),
                  const((CHUNK, LANES)), const((SGU_WIDTH, LANES))],
        out_specs=[pl.BlockSpec((CHUNK, 2 * SGU_WIDTH), lambda c: (c, 0)), const((SGU_GROUPS, CHUNK, CHUNK)),
                   const((CHUNK, LANES)), const((1, SGU_WIDTH)), const((1, SGU_WIDTH)), const((1, 2 * SGU_WIDTH))],
        out_shape=[jax.ShapeDtypeStruct((s, 2 * SGU_WIDTH), BF16),
                   jax.ShapeDtypeStruct((SGU_GROUPS, CHUNK, CHUNK), F32), jax.ShapeDtypeStruct((CHUNK, LANES), F32),
                   jax.ShapeDtypeStruct((1, SGU_WIDTH), F32), jax.ShapeDtypeStruct((1, SGU_WIDTH), F32),
                   jax.ShapeDtypeStruct((1, 2 * SGU_WIDTH), F32)],
        compiler_params=_params(("arbitrary",)),
    )(uv_raw, dyb, uv_b, ln_w, ln_b, w_sp, b_sp_t, group_sum)


def _gate_fwd(gates_raw, b_gate, p_a, p_b, *, name, tm=512):
    s = p_a.shape[0]
    tm = min(tm, s)

    def body(ga_ref, gb_ref, ba_ref, bb_ref, pa_ref, pb_ref, o_ref):
        ga = _sigmoid(ga_ref[...] + ba_ref[...])
        gb = _sigmoid(gb_ref[...] + bb_ref[...])
        o_ref[...] = (ga * pa_ref[...] + gb * pb_ref[...]).astype(BF16)

    t_a = pl.BlockSpec((tm, D_MODEL), lambda i: (i, 0))
    t_b = pl.BlockSpec((tm, D_MODEL), lambda i: (i, 1))
    r_a = pl.BlockSpec((1, D_MODEL), lambda i: (0, 0))
    r_b = pl.BlockSpec((1, D_MODEL), lambda i: (0, 1))
    return pl.pallas_call(
        body, name=name, grid=(s // tm,),
        in_specs=[t_a, t_b, r_a, r_b, t_a, t_a], out_specs=t_a,
        out_shape=jax.ShapeDtypeStruct((s, D_MODEL), BF16),
        compiler_params=_params(("parallel",)),
    )(gates_raw, gates_raw, b_gate, b_gate, p_a, p_b)


def _gate_bwd(gates_raw, b_gate, p_a, p_b, dm, *, name, tm=512):
    s = p_a.shape[0]
    tm = min(tm, s)

    def body(ga_ref, gb_ref, ba_ref, bb_ref, pa_ref, pb_ref, dm_ref, dpa_ref, dpb_ref, dga_ref, dgb_ref,
             dba_ref, dbb_ref):
        @pl.when(pl.program_id(0) == 0)
        def _():
            dba_ref[...] = jnp.zeros_like(dba_ref)
            dbb_ref[...] = jnp.zeros_like(dbb_ref)

        d = dm_ref[...]
        for g_ref, b_ref, p_ref, dp_ref, dg_ref, db_ref in ((ga_ref, ba_ref, pa_ref, dpa_ref, dga_ref, dba_ref),
                                                            (gb_ref, bb_ref, pb_ref, dpb_ref, dgb_ref, dbb_ref)):
            sg = _sigmoid(g_ref[...] + b_ref[...])
            dp_ref[...] = (d * sg).astype(BF16)
            dg = d * p_ref[...] * (sg * (1.0 - sg))
            dg_ref[...] = dg.astype(BF16)
            db_ref[...] += jnp.sum(dg, axis=0, keepdims=True)

    t_a = pl.BlockSpec((tm, D_MODEL), lambda i: (i, 0))
    t_b = pl.BlockSpec((tm, D_MODEL), lambda i: (i, 1))
    r_a = pl.BlockSpec((1, D_MODEL), lambda i: (0, 0))
    r_b = pl.BlockSpec((1, D_MODEL), lambda i: (0, 1))
    big = jax.ShapeDtypeStruct((s, D_MODEL), BF16)
    row = jax.ShapeDtypeStruct((1, D_MODEL), F32)
    return pl.pallas_call(
        body, name=name, grid=(s // tm,),
        in_specs=[t_a, t_b, r_a, r_b, t_a, t_a, t_a], out_specs=[t_a, t_a, t_a, t_a, r_a, r_a],
        out_shape=[big, big, big, big, row, row],
        compiler_params=_params(("arbitrary",)),
    )(gates_raw, gates_raw, b_gate, b_gate, p_a, p_b, dm)


def _adamw(w, g, m, v, *, name, tr=128):
    r, c = w.shape
    tr = min(tr, r)
    assert r % tr == 0, (name, r, tr)

    def body(w_ref, g_ref, m_ref, v_ref, d_ref, mo_ref, vo_ref):
        gv = g_ref[...]
        mn = ADAM_B1 * m_ref[...] + (1.0 - ADAM_B1) * gv
        vn = ADAM_B2 * v_ref[...] + (1.0 - ADAM_B2) * (gv * gv)
        m_hat = mn / (1.0 - ADAM_B1 ** ADAM_STEP)
        v_hat = vn / (1.0 - ADAM_B2 ** ADAM_STEP)
        d_ref[...] = -ADAM_LR * (m_hat / (jnp.sqrt(v_hat) + ADAM_EPS) + ADAM_WD * w_ref[...])
        mo_ref[...] = mn
        vo_ref[...] = vn

    blk = pl.BlockSpec((tr, c), lambda i: (i, 0))
    sds = jax.ShapeDtypeStruct((r, c), F32)
    return pl.pallas_call(
        body, name=name, grid=(r // tr,), in_specs=[blk] * 4, out_specs=[blk] * 3, out_shape=[sds] * 3,
        compiler_params=_params(("parallel",)),
    )(w, g, m, v)


def _tile(n, pref):
    if n <= pref:
        return n
    best = LANES
    for t in range(LANES, pref + 1, LANES):
        if n % t == 0:
            best = t
    return best


MATMUL_BLOCK_BYTES = 20 * 1024 * 1024


def _mm(pairs, name, **kw):
    trans_b = kw.get("trans_b", False)
    m = (pairs[0][0][0] if isinstance(pairs[0][0], tuple) else pairs[0][0]).shape[0]
    ktot, n = 0, None
    for _, b in pairs:
        shape = b[0].shape[1:] if isinstance(b, tuple) else b.shape
        ktot += shape[1] if trans_b else shape[0]
        n = shape[0] if trans_b else shape[1]
    out_bytes = 4 * (2 if kw.get("add") is not None else 1)
    best = None
    for tm in (256, 512, 1024):
        for tn in range(LANES, min(n, 1536) + 1, LANES):
            if m % min(tm, m) or n % tn:
                continue
            fits = 2 * ktot * (min(tm, m) + tn) + out_bytes * min(tm, m) * tn <= MATMUL_BLOCK_BYTES
            if fits and (best is None or min(tm, m) * tn >= best[0] * best[1]):
                best = (min(tm, m), tn)
    return _matmul(pairs, tm=best[0], tn=best[1], name=name, **kw)


def _wgrad(a, b, name, **kw):
    return _matmul_tn(a, b, tk=_tile(a.shape[1], 1408), tn=kw.pop("tn", _tile(b.shape[1], 1024)), tm=2048,
                      name=name, **kw)


def _local_step(x, target, wts, small):
    heads = jnp.arange(SSD_D_INNER) // SSD_HEAD_DIM
    expand = (jnp.arange(LANES)[:, None] == heads[None, :]).astype(F32)
    expand_t = expand.T
    group_sum = (jnp.arange(SGU_WIDTH)[:, None] // LANES == jnp.arange(LANES)[None, :]).astype(F32)
    pad_h = LANES - SSD_HEADS
    dt_bias = jnp.pad(small["dt_bias"], ((0, 0), (0, pad_h)))
    a_log = jnp.pad(small["a_log"], ((0, 0), (0, pad_h)))
    a_log_x = jnp.repeat(small["a_log"], SSD_HEAD_DIM, axis=1)
    d_skip_x = jnp.repeat(small["d_skip"], SSD_HEAD_DIM, axis=1)
    b_sp_t = jnp.pad(small["b_spatial"][0].T, ((0, 0), (0, LANES - SGU_GROUPS)))
    w_sp = small["w_spatial"][0]
    conv_a_w = jnp.pad(small["conv_a_w"], ((0, 4), (0, 0)))
    conv_f_w = jnp.pad(small["conv_f_w"], ((0, 5), (0, 0)))
    final_w = small["final_norm_w"].reshape(1, D_MODEL)

    n1 = _rms_fwd(x, small["norm1_w"], name="rms1_fwd")
    z = _mm([(n1, wts["in_z"])], "in_z")
    xbc_raw = _mm([(n1, wts["in_xbc"])], "in_xbc")
    dt_raw = _mm([(n1, wts["in_dt"])], "in_dt")
    uv_raw = _mm([(n1, wts["in_uv"])], "in_uv")
    gates_raw = _mm([(n1, wts["in_gate"])], "in_gate")
    xbc = _conv_a_fwd(xbc_raw, conv_a_w, small["conv_a_b"], name="conv_a_fwd")
    y, y_a, states = _ssd_fwd(xbc, dt_raw, z, dt_bias, a_log, a_log_x, d_skip_x, small["ssd_norm_w"], expand,
                              name="ssd_fwd")
    y_b = _sgu_fwd(uv_raw, small["uv_b"], small["v_ln_w"], small["v_ln_b"], w_sp, b_sp_t, name="sgu_fwd")
    p_a = _mm([(y_a, wts["branch_a"])], "branch_a")
    p_b = _mm([(y_b, wts["branch_b"])], "branch_b")
    mix = _gate_fwd(gates_raw, small["b_gate"], p_a, p_b, name="gate_fwd")
    h1 = _mm([(mix, wts["out"])], "out_proj", add=x)
    n2 = _rms_fwd(h1, small["norm2_w"], name="rms2_fwd")
    up_w = wts["up"]
    up_cols = up_w.shape[2]
    up_raw = _matmul([(n2, (up_w, "cols"))], tm=1024, tn=up_cols, name="up_proj")
    act = _conv_f_fwd(up_raw, conv_f_w, small["conv_f_b"], name="conv_f_fwd")
    h2 = _mm([(act, wts["down"])], "down_proj", add=h1)
    loss, dh2, dh2_b, d_final = _final_fwd_bwd(h2, final_w, target, name="final_norm_loss")

    dact = _mm([(dh2_b, wts["down"])], "down_dgrad", trans_b=True)
    g_down = _wgrad(act, dh2_b, "down_wgrad")
    dup_a, dup_v, dwf_a, dwf_v, dbf_a, dbf_v = _conv_f_bwd(up_raw, conv_f_w, small["conv_f_b"], dact,
                                                           name="conv_f_bwd")
    dn2 = _mm([((dup_a, 0), (up_w, 0)), ((dup_a, 1), (up_w, 1)), ((dup_v, 0), (up_w, 2)), ((dup_v, 1), (up_w, 3))],
              "up_dgrad", trans_b=True)
    g_up = jnp.concatenate([_wgrad(n2, dup_a, "up_wgrad_a", tn=up_cols, stack_out=True),
                            _wgrad(n2, dup_v, "up_wgrad_v", tn=up_cols, stack_out=True)], axis=0)
    dh1, dh1_b, d_norm2 = _rms_bwd(h1, small["norm2_w"], dn2, dh2, name="rms2_bwd")
    dmix = _mm([(dh1_b, wts["out"])], "out_dgrad", trans_b=True)
    g_out = _wgrad(mix, dh1_b, "out_wgrad")
    dp_a, dp_b, dg_a, dg_b, dbg_a, dbg_b = _gate_bwd(gates_raw, small["b_gate"], p_a, p_b, dmix, name="gate_bwd")
    dya = _mm([(dp_a, wts["branch_a"])], "branch_a_dgrad", trans_b=True)
    dyb = _mm([(dp_b, wts["branch_b"])], "branch_b_dgrad", trans_b=True)
    g_branch = jnp.concatenate([_wgrad(y_a, dp_a, "branch_a_wgrad"), _wgrad(y_b, dp_b, "branch_b_wgrad")], axis=0)
    duv, d_wsp, d_bsp_t, d_lnw, d_lnb, d_uvb = _sgu_bwd(uv_raw, dyb, small["uv_b"], small["v_ln_w"],
                                                        small["v_ln_b"], w_sp, b_sp_t, group_sum, name="sgu_bwd")
    dz, dxbc, ddt, d_ssd_nw, d_dskip, d_alog, d_dtb = _ssd_bwd(
        dya, y, z, xbc, dt_raw, states, dt_bias, a_log, a_log_x, d_skip_x, small["ssd_norm_w"], expand, expand_t,
        name="ssd_bwd")
    dxbc_raw, d_conv_a_w, d_conv_a_b = _conv_a_bwd(xbc_raw, conv_a_w, small["conv_a_b"], dxbc, name="conv_a_bwd")
    dn1 = _mm([(dz, wts["in_z"]), (dxbc_raw, wts["in_xbc"]), (ddt, wts["in_dt"]), (duv, wts["in_uv"]),
               (dg_a, wts["in_gate_a"]), (dg_b, wts["in_gate_b"])], "in_dgrad", trans_b=True)
    g_in = jnp.concatenate(
        [_wgrad(n1, dz, "in_z_wgrad"), _wgrad(n1, dxbc_raw, "in_xbc_wgrad"),
         _wgrad(n1, ddt, "in_dt_wgrad")[:, :SSD_HEADS], _wgrad(n1, duv, "in_uv_wgrad"),
         _wgrad(n1, dg_a, "in_gate_a_wgrad"), _wgrad(n1, dg_b, "in_gate_b_wgrad")], axis=1)
    dx, _, d_norm1 = _rms_bwd(x, small["norm1_w"], dn1, dh1, name="rms1_bwd")

    grads_big = {"w_in": g_in, "w_branch": g_branch, "w_out": g_out, "w_up": g_up, "w_down": g_down}
    grads_small = {
        "norm1_w": d_norm1, "b_gate": jnp.concatenate([dbg_a, dbg_b], axis=1),
        "conv_a_w": d_conv_a_w[:4], "conv_a_b": d_conv_a_b,
        "dt_bias": d_dtb[:, :SSD_HEADS], "a_log": d_alog[:, :SSD_HEADS], "d_skip": d_dskip[:, :SSD_HEADS],
        "ssd_norm_w": d_ssd_nw, "uv_b": d_uvb, "v_ln_w": d_lnw, "v_ln_b": d_lnb,
        "w_spatial": d_wsp[None], "b_spatial": d_bsp_t[:, :SGU_GROUPS].T[None],
        "norm2_w": d_norm2, "conv_f_w": jnp.concatenate([dwf_a[:3], dwf_v[:3]], axis=1),
        "conv_f_b": jnp.concatenate([dbf_a, dbf_v], axis=1), "final_norm_w": d_final.reshape(D_MODEL),
    }
    return loss, dx, grads_big, grads_small


HBM = pl.BlockSpec(memory_space=pl.ANY)
MESH = pl.DeviceIdType.MESH


def _mesh_pos():
    return lax.axis_index("x"), lax.axis_index("y"), lax.axis_index("c")


def _other_chips(x, y):
    return [(1 - x, y), (x, 1 - y), (1 - x, 1 - y)]


def _remote(src, dst, send_sems, recv_sems, k, dev):
    return pltpu.make_async_remote_copy(src_ref=src, dst_ref=dst, send_sem=send_sems.at[k], recv_sem=recv_sems.at[k],
                                        device_id=dev, device_id_type=MESH)


def _dma_sems(n):
    return [pltpu.SemaphoreType.DMA((n,)), pltpu.SemaphoreType.DMA((n,))]


def _gather_ici(shard, *, name):
    _, rh, cols = shard.shape

    def body(w_ref, o_ref, send_sems, recv_sems):
        x, y, c = _mesh_pos()
        mine = 2 * x + y
        sends = []
        for k, (px, py) in enumerate(_other_chips(x, y)):
            cp = _remote(w_ref.at[c], o_ref.at[mine, c], send_sems, recv_sems, k, (px, py, c))
            cp.start()
            sends.append(cp)
        for k, (px, py) in enumerate(_other_chips(x, y)):
            _remote(w_ref.at[c], o_ref.at[2 * px + py, c], send_sems, recv_sems, k, (px, py, c)).wait_recv()
        for cp in sends:
            cp.wait_send()

    return pl.pallas_call(
        body, name=name, in_specs=[HBM], out_specs=HBM,
        out_shape=jax.ShapeDtypeStruct((N_CHIPS, 2, rh, cols), shard.dtype), scratch_shapes=_dma_sems(3),
    )(shard)


def _gather_d2d(parts, *, name):
    def body(a_ref, o_ref, send_sems, recv_sems):
        x, y, c = _mesh_pos()
        sibling = (x, y, 1 - c)
        sends = []
        for k, (px, py) in enumerate(_other_chips(x, y)):
            cp = _remote(a_ref.at[2 * px + py, c], o_ref.at[2 * px + py, c], send_sems, recv_sems, k, sibling)
            cp.start()
            sends.append(cp)
        for k, (px, py) in enumerate(_other_chips(x, y)):
            _remote(a_ref.at[2 * px + py, c], o_ref.at[2 * px + py, 1 - c], send_sems, recv_sems, k, sibling).wait_recv()
        for cp in sends:
            cp.wait_send()

    return pl.pallas_call(
        body, name=name, in_specs=[HBM], out_specs=HBM,
        out_shape=jax.ShapeDtypeStruct(parts.shape, parts.dtype),
        input_output_aliases={0: 0}, scratch_shapes=_dma_sems(3),
    )(parts)


def _all_gather_chips(shard_flat, name):
    rows, cols = shard_flat.shape
    parts = _gather_ici(shard_flat.reshape(2, rows // 2, cols), name=name + "_ici")
    others = _gather_d2d(parts, name=name + "_d2d").reshape(N_CHIPS, rows, cols)
    chip = 2 * lax.axis_index("x") + lax.axis_index("y")
    return lax.dynamic_update_slice(others, shard_flat[None], (chip, 0, 0))


def _row_tile(rows, mult, cap):
    best = mult
    for t in range(mult, min(rows, cap) + 1, mult):
        if rows % t == 0:
            best = t
    assert rows % best == 0, (rows, mult)
    return best


def _swap_halves_d2d(g, *, name):
    _, _, rh, cols = g.shape

    def body(g_ref, o_ref, send_sems, recv_sems):
        x, y, c = _mesh_pos()
        sibling = (x, y, 1 - c)
        sends = []
        for s in range(N_CHIPS):
            cp = _remote(g_ref.at[s, 1 - c], o_ref.at[s], send_sems, recv_sems, s, sibling)
            cp.start()
            sends.append(cp)
        for s in range(N_CHIPS):
            _remote(g_ref.at[s, c], o_ref.at[s], send_sems, recv_sems, s, sibling).wait_recv()
        for cp in sends:
            cp.wait_send()

    return pl.pallas_call(
        body, name=name, in_specs=[HBM], out_specs=HBM,
        out_shape=jax.ShapeDtypeStruct((N_CHIPS, rh, cols), g.dtype), scratch_shapes=_dma_sems(N_CHIPS),
    )(g)


def _add_own_half(g, arrived, core, *, name):
    _, _, rh, cols = g.shape
    mult = 16 if g.dtype == BF16 else 8
    tr = _row_tile(rh, mult, max(mult, (512 * 1024) // cols))

    def body(core_ref, g_ref, a_ref, o_ref):
        o_ref[...] = (g_ref[0].astype(F32) + a_ref[...].astype(F32)).astype(o_ref.dtype)

    grid_spec = pltpu.PrefetchScalarGridSpec(
        num_scalar_prefetch=1, grid=(N_CHIPS, rh // tr),
        in_specs=[pl.BlockSpec((1, 1, tr, cols), lambda s, i, core_ref: (s, core_ref[0], i, 0)),
                  pl.BlockSpec((1, tr, cols), lambda s, i, core_ref: (s, i, 0))],
        out_specs=pl.BlockSpec((1, tr, cols), lambda s, i, core_ref: (s, i, 0)))
    return pl.pallas_call(
        body, name=name, grid_spec=grid_spec, out_shape=jax.ShapeDtypeStruct((N_CHIPS, rh, cols), g.dtype),
        compiler_params=_params(("parallel", "parallel")),
    )(core, g, arrived)


def _scatter_ici(h, *, name):
    def body(h_ref, o_ref, send_sems, recv_sems):
        x, y, c = _mesh_pos()
        mine = 2 * x + y
        sends = []
        for k, (px, py) in enumerate(_other_chips(x, y)):
            cp = _remote(h_ref.at[2 * px + py], o_ref.at[mine], send_sems, recv_sems, k, (px, py, c))
            cp.start()
            sends.append(cp)
        for k, (px, py) in enumerate(_other_chips(x, y)):
            _remote(h_ref.at[mine], o_ref.at[2 * px + py], send_sems, recv_sems, k, (px, py, c)).wait_recv()
        for cp in sends:
            cp.wait_send()

    others = pl.pallas_call(
        body, name=name, in_specs=[HBM], out_specs=HBM, out_shape=jax.ShapeDtypeStruct(h.shape, h.dtype),
        scratch_shapes=_dma_sems(3),
    )(h)
    chip = 2 * lax.axis_index("x") + lax.axis_index("y")
    own = lax.dynamic_slice_in_dim(h, chip, 1, axis=0)
    return lax.dynamic_update_slice(others, own, (chip, 0, 0))


def _sum_chips(parts, *, name):
    _, rh, cols = parts.shape
    mult = 16 if parts.dtype == BF16 else 8
    tr = _row_tile(rh, mult, max(mult, (512 * 1024) // cols))

    def body(p_ref, o_ref):
        acc = p_ref[0].astype(F32)
        for s in range(1, N_CHIPS):
            acc = acc + p_ref[s].astype(F32)
        o_ref[...] = acc

    return pl.pallas_call(
        body, name=name, grid=(rh // tr,),
        in_specs=[pl.BlockSpec((N_CHIPS, tr, cols), lambda i: (0, i, 0))],
        out_specs=pl.BlockSpec((tr, cols), lambda i: (i, 0)),
        out_shape=jax.ShapeDtypeStruct((rh, cols), F32), compiler_params=_params(("parallel",)),
    )(parts)


def _share_d2d(f, *, name):
    def body(f_ref, o_ref, send_sems, recv_sems):
        x, y, c = _mesh_pos()
        sibling = (x, y, 1 - c)
        cp = _remote(f_ref, o_ref, send_sems, recv_sems, 0, sibling)
        cp.start()
        cp.wait()

    other = pl.pallas_call(
        body, name=name, in_specs=[HBM], out_specs=HBM, out_shape=jax.ShapeDtypeStruct(f.shape, f.dtype),
        scratch_shapes=_dma_sems(1),
    )(f)
    first = lax.axis_index("c") == 0
    return jnp.stack([jnp.where(first, f, other), jnp.where(first, other, f)])


def _reduce_scatter_chips(g, core, name):
    _, rows, cols = g.shape
    g = g.reshape(N_CHIPS, 2, rows // 2, cols)
    arrived = _swap_halves_d2d(g, name=name + "_swap")
    chip_sum = _add_own_half(g, arrived, core, name=name + "_add2")
    parts = _scatter_ici(chip_sum, name=name + "_ici")
    total = _sum_chips(parts, name=name + "_sum4")
    return _share_d2d(total, name=name + "_share").reshape(rows, cols)


BIG = ("w_in", "w_branch", "w_out", "w_up", "w_down")
BIG_COLUMN_SHARDED = ("w_in", "w_up")
CONV = ("conv_a_w", "conv_f_w")
REPLICATED = ("norm1_w", "b_gate", "conv_a_b", "dt_bias", "a_log", "d_skip", "ssd_norm_w", "uv_b", "v_ln_w",
              "v_ln_b", "w_spatial", "b_spatial", "norm2_w", "conv_f_b", "final_norm_w")
WEIGHT_ORDER = ("norm1_w", "w_in", "b_gate", "conv_a_w", "conv_a_b", "dt_bias", "a_log", "d_skip", "ssd_norm_w",
                "uv_b", "v_ln_w", "v_ln_b", "w_spatial", "b_spatial", "w_branch", "w_out", "norm2_w", "w_up",
                "conv_f_w", "conv_f_b", "w_down", "final_norm_w")
SMALL_EXCHANGE_ROWS = 64


def _flat_rows(arrays, row_multiple):
    flat = jnp.concatenate([a.reshape(-1) for a in arrays])
    rows = -(-flat.shape[0] // (LANES * row_multiple)) * row_multiple
    return jnp.pad(flat, (0, rows * LANES - flat.shape[0])).reshape(rows, LANES)


def _unflatten(flat, shapes):
    flat = flat.reshape(-1)
    out, off = [], 0
    for shp in shapes:
        n = math.prod(shp)
        out.append(flat[off:off + n].reshape(shp))
        off += n
    return out


def _from_chip_blocks(blocks, name):
    if name in BIG_COLUMN_SHARDED or name in CONV:
        k = blocks.shape[1]
        return jnp.transpose(blocks, (1, 0, 2)).reshape(k, -1)
    return blocks.reshape(-1, blocks.shape[-1])


def _to_chip_blocks(whole, name):
    if name in BIG_COLUMN_SHARDED or name in CONV:
        k, n = whole.shape
        return jnp.transpose(whole.reshape(k, N_CHIPS, n // N_CHIPS), (1, 0, 2))
    return whole.reshape(N_CHIPS, whole.shape[0] // N_CHIPS, whole.shape[1])


def kernel(x, norm1_w, w_in, b_gate, conv_a_w, conv_a_b, dt_bias, a_log, d_skip, ssd_norm_w, uv_b, v_ln_w, v_ln_b, w_spatial, b_spatial, w_branch, w_out, norm2_w, w_up, conv_f_w, conv_f_b, w_down, final_norm_w, loss_target, m_norm1_w, m_w_in, m_b_gate, m_conv_a_w, m_conv_a_b, m_dt_bias, m_a_log, m_d_skip, m_ssd_norm_w, m_uv_b, m_v_ln_w, m_v_ln_b, m_w_spatial, m_b_spatial, m_w_branch, m_w_out, m_norm2_w, m_w_up, m_conv_f_w, m_conv_f_b, m_w_down, m_final_norm_w, v_norm1_w, v_w_in, v_b_gate, v_conv_a_w, v_conv_a_b, v_dt_bias, v_a_log, v_d_skip, v_ssd_norm_w, v_uv_b, v_v_ln_w, v_v_ln_b, v_w_spatial, v_b_spatial, v_w_branch, v_w_out, v_norm2_w, v_w_up, v_conv_f_w, v_conv_f_b, v_w_down, v_final_norm_w):
    weights = dict(norm1_w=norm1_w, w_in=w_in, b_gate=b_gate, conv_a_w=conv_a_w, conv_a_b=conv_a_b, dt_bias=dt_bias,
                   a_log=a_log, d_skip=d_skip, ssd_norm_w=ssd_norm_w, uv_b=uv_b, v_ln_w=v_ln_w, v_ln_b=v_ln_b,
                   w_spatial=w_spatial, b_spatial=b_spatial, w_branch=w_branch, w_out=w_out, norm2_w=norm2_w,
                   w_up=w_up, conv_f_w=conv_f_w, conv_f_b=conv_f_b, w_down=w_down, final_norm_w=final_norm_w)
    mom1 = dict(norm1_w=m_norm1_w, w_in=m_w_in, b_gate=m_b_gate, conv_a_w=m_conv_a_w, conv_a_b=m_conv_a_b,
                dt_bias=m_dt_bias, a_log=m_a_log, d_skip=m_d_skip, ssd_norm_w=m_ssd_norm_w, uv_b=m_uv_b,
                v_ln_w=m_v_ln_w, v_ln_b=m_v_ln_b, w_spatial=m_w_spatial, b_spatial=m_b_spatial, w_branch=m_w_branch,
                w_out=m_w_out, norm2_w=m_norm2_w, w_up=m_w_up, conv_f_w=m_conv_f_w, conv_f_b=m_conv_f_b,
                w_down=m_w_down, final_norm_w=m_final_norm_w)
    mom2 = dict(norm1_w=v_norm1_w, w_in=v_w_in, b_gate=v_b_gate, conv_a_w=v_conv_a_w, conv_a_b=v_conv_a_b,
                dt_bias=v_dt_bias, a_log=v_a_log, d_skip=v_d_skip, ssd_norm_w=v_ssd_norm_w, uv_b=v_uv_b,
                v_ln_w=v_v_ln_w, v_ln_b=v_v_ln_b, w_spatial=v_w_spatial, b_spatial=v_b_spatial, w_branch=v_w_branch,
                w_out=v_w_out, norm2_w=v_norm2_w, w_up=v_w_up, conv_f_w=v_conv_f_w, conv_f_b=v_conv_f_b,
                w_down=v_w_down, final_norm_w=v_final_norm_w)
    chip = 2 * lax.axis_index("x") + lax.axis_index("y")
    core = lax.axis_index("c").astype(jnp.int32).reshape(1)

    blocks = {n: _all_gather_chips(weights[n][0].astype(BF16), "gather_" + n) for n in BIG}
    whole = {n: _from_chip_blocks(blocks[n], n) for n in BIG if n != "w_up"}
    conv_shapes = [weights[n].shape[1:] for n in CONV]
    conv_gathered = _all_gather_chips(_flat_rows([weights[n] for n in CONV], 16), "gather_conv").reshape(N_CHIPS, -1)
    off = 0
    for n, shp in zip(CONV, conv_shapes):
        size = math.prod(shp)
        whole[n] = _from_chip_blocks(conv_gathered[:, off:off + size].reshape((N_CHIPS,) + shp), n)
        off += size

    w_in_full = whole["w_in"]
    wts = {
        "in_z": w_in_full[:, :SSD_D_INNER],
        "in_xbc": w_in_full[:, SSD_D_INNER:SSD_D_INNER + SSD_XBC],
        "in_dt": jnp.pad(w_in_full[:, SSD_D_INNER + SSD_XBC:SSD_IN], ((0, 0), (0, LANES - SSD_HEADS))),
        "in_uv": w_in_full[:, SSD_IN:SSD_IN + 2 * SGU_WIDTH],
        "in_gate": w_in_full[:, SSD_IN + 2 * SGU_WIDTH:],
        "in_gate_a": w_in_full[:, SSD_IN + 2 * SGU_WIDTH:SSD_IN + 2 * SGU_WIDTH + D_MODEL],
        "in_gate_b": w_in_full[:, SSD_IN + 2 * SGU_WIDTH + D_MODEL:],
        "branch_a": whole["w_branch"][:SSD_D_INNER], "branch_b": whole["w_branch"][SSD_D_INNER:],
        "out": whole["w_out"], "up": blocks["w_up"], "down": whole["w_down"],
    }
    small = {n: weights[n] for n in REPLICATED}
    small["conv_a_w"] = whole["conv_a_w"]
    small["conv_f_w"] = whole["conv_f_w"]

    loss, dx, grads_big, grads_small = _local_step(x[0], loss_target[0], wts, small)

    grads = {}
    for n in BIG:
        g_blocks = grads_big[n] if n == "w_up" else _to_chip_blocks(grads_big[n], n)
        grads[n] = _reduce_scatter_chips(g_blocks, core, "reduce_" + n)

    small_names = REPLICATED + CONV
    small_shapes = [grads_small[n].shape for n in small_names]
    g_small = _flat_rows([grads_small[n] for n in small_names], N_CHIPS * 2 * SMALL_EXCHANGE_ROWS)
    red_small = _reduce_scatter_chips(g_small.reshape(N_CHIPS, -1, LANES), core, "reduce_small")
    all_small = _all_gather_chips(red_small, "gather_small")
    for n, g in zip(small_names, _unflatten(all_small, small_shapes)):
        if n in CONV:
            width = g.shape[1] // N_CHIPS
            g = lax.dynamic_slice_in_dim(g, chip * width, width, axis=1)
        grads[n] = g.reshape(weights[n].shape[1:]) if n != "final_norm_w" else g

    delta, new_m, new_v = {}, {}, {}
    for n in BIG:
        shp = weights[n].shape
        tr = _row_tile(shp[1], 8, 128)
        d, m1, m2 = _adamw(weights[n][0], grads[n], mom1[n][0], mom2[n][0], name="adamw_" + n, tr=tr)
        delta[n], new_m[n], new_v[n] = d.reshape(shp), m1.reshape(shp), m2.reshape(shp)
    small_all = [n for n in WEIGHT_ORDER if n not in BIG]
    shapes = [weights[n].shape for n in small_all]
    packed = [_flat_rows([src[n] for n in small_all], 8)
              for src in (weights, {n: grads[n] for n in small_all}, mom1, mom2)]
    d, m1, m2 = _adamw(*packed, name="adamw_small", tr=packed[0].shape[0])
    for n, dv, mv, vv in zip(small_all, _unflatten(d, shapes), _unflatten(m1, shapes), _unflatten(m2, shapes)):
        delta[n], new_m[n], new_v[n] = dv, mv, vv

    total_loss = lax.psum(loss[0, 0], ("x", "y", "c"))
    grad_out = [grads[n].reshape(weights[n].shape) for n in WEIGHT_ORDER]
    return (total_loss, dx[None], *grad_out, *[delta[n] for n in WEIGHT_ORDER], *[new_m[n] for n in WEIGHT_ORDER],
            *[new_v[n] for n in WEIGHT_ORDER])
```

```python
import functools
import math

import jax
import jax.numpy as jnp
from jax import lax
from jax.experimental import pallas as pl
from jax.experimental.pallas import tpu as pltpu

F32 = jnp.float32
BF16 = jnp.bfloat16
HI = lax.Precision.HIGHEST

D_MODEL = 1024
SSD_D_INNER = 2048
SSD_HEADS = 32
SSD_HEAD_DIM = 64
SSD_GROUPS = 4
SSD_HEADS_PER_GROUP = 8
SSD_STATE = 128
SSD_BC = 512
SSD_XBC = 3072
SSD_IN = 5152
SGU_WIDTH = 1024
SGU_GROUPS = 8
CHUNK = 128
IN_COLS = 9248
D_FF = 2816
NORM_EPS = 1e-6
LN_EPS = 1e-5
GROUP_COLS = SSD_HEADS_PER_GROUP * SSD_HEAD_DIM
LANES = 128

ADAM_LR = 0.001
ADAM_B1 = 0.9
ADAM_B2 = 0.999
ADAM_EPS = 1e-08
ADAM_WD = 0.01
ADAM_STEP = 10

N_CHIPS = 4
VMEM_LIMIT = 56 * 1024 * 1024

NT = (((1,), (1,)), ((), ()))
TN = (((0,), (0,)), ((), ()))
NN = (((1,), (0,)), ((), ()))


def _params(dims):
    return pltpu.CompilerParams(dimension_semantics=dims, vmem_limit_bytes=VMEM_LIMIT)


def _dot(a, b, dn=NN, precision=None):
    return lax.dot_general(a, b, dn, precision=precision, preferred_element_type=F32)


def _sigmoid(x):
    return 1.0 / (1.0 + jnp.exp(-x))


def _softplus(x):
    return jnp.maximum(x, 0.0) + jnp.log(1.0 + jnp.exp(-jnp.abs(x)))


def _matmul(pairs, *, trans_b=False, add=None, after=None, out_dtype=F32, tm=512, tn=512, name):
    def mat_shape(b):
        if isinstance(b, tuple) and b[1] == "cols":
            return (b[0].shape[1], b[0].shape[0] * b[0].shape[2])
        return b[0].shape[1:] if isinstance(b, tuple) else b.shape

    if isinstance(pairs[0][1], tuple) and pairs[0][1][1] == "cols":
        assert not trans_b and tn % LANES == 0 and pairs[0][1][0].shape[2] % tn == 0, name

    m = (pairs[0][0][0] if isinstance(pairs[0][0], tuple) else pairs[0][0]).shape[0]
    n = mat_shape(pairs[0][1])[0] if trans_b else mat_shape(pairs[0][1])[1]
    tm, tn = min(tm, m), min(tn, n)
    assert m % tm == 0 and n % tn == 0, (name, m, n, tm, tn)
    npairs = len(pairs)
    dn = NT if trans_b else NN

    def body(*refs):
        o_ref = refs[-1]
        acc = None
        for i in range(npairs):
            p = _dot(refs[2 * i][...].astype(BF16), refs[2 * i + 1][...].astype(BF16), dn)
            acc = p if acc is None else acc + p
        if add is not None:
            acc = acc + refs[2 * npairs][...]
        o_ref[...] = acc.astype(out_dtype)

    in_specs, args = [], []
    for a, b in pairs:
        bshape = mat_shape(b)
        k = bshape[1] if trans_b else bshape[0]
        assert bshape == ((n, k) if trans_b else (k, n)), (name, bshape)
        a, qa = a if isinstance(a, tuple) else (a, 0)
        assert a.shape[0] == m and a.shape[1] % k == 0, (name, a.shape, k)
        in_specs.append(pl.BlockSpec((tm, k), lambda i, j, qa=qa: (i, qa)))
        if isinstance(b, tuple) and b[1] == "cols":
            b = b[0]
            per = b.shape[2] // tn
            in_specs.append(pl.BlockSpec((None, k, tn), lambda i, j, per=per: (j // per, 0, j % per)))
        elif isinstance(b, tuple):
            b, qb = b
            if trans_b:
                in_specs.append(pl.BlockSpec((None, tn, k), lambda i, j, qb=qb: (qb, j, 0)))
            else:
                in_specs.append(pl.BlockSpec((None, k, tn), lambda i, j, qb=qb: (qb, 0, j)))
        elif trans_b:
            in_specs.append(pl.BlockSpec((tn, k), lambda i, j: (j, 0)))
        else:
            in_specs.append(pl.BlockSpec((k, tn), lambda i, j: (0, j)))
        args += [a, b]
    if add is not None:
        in_specs.append(pl.BlockSpec((tm, tn), lambda i, j: (i, j)))
        args.append(add)
    if after is not None:
        in_specs.append(pl.BlockSpec(memory_space=pl.ANY))
        args.append(after)
    return pl.pallas_call(
        body, name=name, grid=(m // tm, n // tn), in_specs=in_specs,
        out_specs=pl.BlockSpec((tm, tn), lambda i, j: (i, j)),
        out_shape=jax.ShapeDtypeStruct((m, n), out_dtype),
        compiler_params=_params(("parallel", "parallel")),
    )(*args)


def _matmul_tn(a, b, *, tk, tn, tm=1024, out_dtype=BF16, stack_out=False, after=None, name):
    m, k = a.shape
    n = b.shape[1]
    tm, tk, tn = min(tm, m), min(tk, k), min(tn, n)
    assert m % tm == 0 and k % tk == 0 and n % tn == 0, (name, m, k, n)
    nm = m // tm
    if stack_out:
        out_spec = pl.BlockSpec((None, tk, tn), lambda i, j, l: (j, i, 0))
        out_shape = jax.ShapeDtypeStruct((n // tn, k, tn), out_dtype)
    else:
        out_spec = pl.BlockSpec((tk, tn), lambda i, j, l: (i, j))
        out_shape = jax.ShapeDtypeStruct((k, n), out_dtype)

    def body(a_ref, b_ref, *rest):
        o_ref, acc = rest[-2:]
        mi = pl.program_id(2)

        @pl.when(mi == 0)
        def _():
            acc[...] = jnp.zeros_like(acc)

        acc[...] += _dot(a_ref[...].astype(BF16), b_ref[...].astype(BF16), TN)

        @pl.when(mi == nm - 1)
        def _():
            o_ref[...] = acc[...].astype(out_dtype)

    in_specs = [pl.BlockSpec((tm, tk), lambda i, j, l: (l, i)), pl.BlockSpec((tm, tn), lambda i, j, l: (l, j))]
    args = [a, b]
    if after is not None:
        in_specs.append(pl.BlockSpec(memory_space=pl.ANY))
        args.append(after)
    return pl.pallas_call(
        body, name=name, grid=(k // tk, n // tn, nm), in_specs=in_specs,
        out_specs=out_spec, out_shape=out_shape,
        scratch_shapes=[pltpu.VMEM((tk, tn), F32)],
        compiler_params=_params(("parallel", "parallel", "arbitrary")),
    )(*args)


def _rms_fwd(x, w, *, name, tm=512):
    s, d = x.shape
    tm = min(tm, s)

    def body(x_ref, w_ref, o_ref):
        xv = x_ref[...]
        r = lax.rsqrt(jnp.mean(xv * xv, axis=-1, keepdims=True) + NORM_EPS)
        o_ref[...] = (xv * r * w_ref[...]).astype(BF16)

    return pl.pallas_call(
        body, name=name, grid=(s // tm,),
        in_specs=[pl.BlockSpec((tm, d), lambda i: (i, 0)), pl.BlockSpec((1, d), lambda i: (0, 0))],
        out_specs=pl.BlockSpec((tm, d), lambda i: (i, 0)),
        out_shape=jax.ShapeDtypeStruct((s, d), BF16),
        compiler_params=_params(("parallel",)),
    )(x, w)


def _rms_bwd(x, w, dn, dres, *, name, tm=512):
    s, d = x.shape
    tm = min(tm, s)

    def body(x_ref, w_ref, dn_ref, dres_ref, dx_ref, dxb_ref, dw_ref):
        @pl.when(pl.program_id(0) == 0)
        def _():
            dw_ref[...] = jnp.zeros_like(dw_ref)

        xv = x_ref[...]
        r = lax.rsqrt(jnp.mean(xv * xv, axis=-1, keepdims=True) + NORM_EPS)
        xhat = xv * r
        dnv = dn_ref[...]
        dxhat = dnv * w_ref[...]
        dx = dres_ref[...] + r * (dxhat - xhat * jnp.mean(dxhat * xhat, axis=-1, keepdims=True))
        dx_ref[...] = dx
        dxb_ref[...] = dx.astype(BF16)
        dw_ref[...] += jnp.sum(dnv * xhat, axis=0, keepdims=True)

    tile = pl.BlockSpec((tm, d), lambda i: (i, 0))
    row = pl.BlockSpec((1, d), lambda i: (0, 0))
    return pl.pallas_call(
        body, name=name, grid=(s // tm,),
        in_specs=[tile, row, tile, tile], out_specs=[tile, tile, row],
        out_shape=[jax.ShapeDtypeStruct((s, d), F32), jax.ShapeDtypeStruct((s, d), BF16),
                   jax.ShapeDtypeStruct((1, d), F32)],
        compiler_params=_params(("arbitrary",)),
    )(x, w, dn, dres)


def _final_fwd_bwd(h2, wf, target, *, name, tm=512):
    s, d = h2.shape
    tm = min(tm, s)

    def body(h_ref, w_ref, t_ref, loss_ref, dh_ref, dhb_ref, dw_ref):
        @pl.when(pl.program_id(0) == 0)
        def _():
            dw_ref[...] = jnp.zeros_like(dw_ref)
            loss_ref[...] = jnp.zeros_like(loss_ref)

        hv = h_ref[...]
        r = lax.rsqrt(jnp.mean(hv * hv, axis=-1, keepdims=True) + NORM_EPS)
        xhat = hv * r
        err = xhat * w_ref[...] - t_ref[...]
        per_tok = jnp.mean(err * err, axis=-1, keepdims=True)
        loss_ref[...] += 0.5 * jnp.sum(per_tok, axis=0, keepdims=True)
        dy = err * (1.0 / d)
        dxhat = dy * w_ref[...]
        dh = r * (dxhat - xhat * jnp.mean(dxhat * xhat, axis=-1, keepdims=True))
        dh_ref[...] = dh
        dhb_ref[...] = dh.astype(BF16)
        dw_ref[...] += jnp.sum(dy * xhat, axis=0, keepdims=True)

    tile = pl.BlockSpec((tm, d), lambda i: (i, 0))
    row = pl.BlockSpec((1, d), lambda i: (0, 0))
    return pl.pallas_call(
        body, name=name, grid=(s // tm,),
        in_specs=[tile, row, tile],
        out_specs=[pl.BlockSpec((1, 1), lambda i: (0, 0)), tile, tile, row],
        out_shape=[jax.ShapeDtypeStruct((1, 1), F32), jax.ShapeDtypeStruct((s, d), F32),
                   jax.ShapeDtypeStruct((s, d), BF16), jax.ShapeDtypeStruct((1, d), F32)],
        compiler_params=_params(("arbitrary",)),
    )(h2, wf, target)


CONV_ROWS = 512
HALO = 8


def _rows_with_halo(ref, r0, rows, s, before, after):
    parts = []
    if before:
        prev = ref[pl.ds(pl.multiple_of(jnp.maximum(r0 - HALO, 0), HALO), HALO), :]
        parts.append(jnp.where(r0 > 0, prev, 0.0))
    parts.append(ref[pl.ds(r0, rows), :])
    if after:
        nxt = ref[pl.ds(pl.multiple_of(jnp.minimum(r0 + rows, s - HALO), HALO), HALO), :]
        parts.append(jnp.where(r0 + rows < s, nxt, 0.0))
    return jnp.concatenate(parts, axis=0) if len(parts) > 1 else parts[0]


def _earlier(xe, k, rows):
    if k == 0:
        return xe[HALO:HALO + rows]
    return pltpu.roll(xe, k, 0)[HALO:HALO + rows]


def _later(ve, k, rows):
    if k == 0:
        return ve[:rows]
    return pltpu.roll(ve, ve.shape[0] - k, 0)[:rows]


def _conv_taps(xe, w_ref, kk, rows):
    acc = None
    for i in range(kk):
        term = w_ref[i:i + 1, :] * _earlier(xe, kk - 1 - i, rows)
        acc = term if acc is None else acc + term
    return acc


def _row_loop(s, step):
    def body(r, carry):
        return step(pl.multiple_of(r * CONV_ROWS, CONV_ROWS), carry)
    return body


def _conv_bwd_rows(xe, dpe, w_ref, kk):
    dp = dpe[:CONV_ROWS]
    dx = None
    dws = []
    for i in range(kk):
        dws.append(jnp.sum(dp * _earlier(xe, kk - 1 - i, CONV_ROWS), axis=0, keepdims=True))
        term = w_ref[i:i + 1, :] * _later(dpe, kk - 1 - i, CONV_ROWS)
        dx = term if dx is None else dx + term
    return dx, dws, jnp.sum(dp, axis=0, keepdims=True)


def _conv_a_fwd(xraw, w, b, *, name, tc=128):
    s, c = xraw.shape
    kk = 4

    def body(x_ref, w_ref, b_ref, o_ref):
        def step(r0, carry):
            xe = _rows_with_halo(x_ref, r0, CONV_ROWS, s, True, False)
            pre = _conv_taps(xe, w_ref, kk, CONV_ROWS) + b_ref[...]
            o_ref[pl.ds(r0, CONV_ROWS), :] = pre * _sigmoid(pre)
            return carry

        lax.fori_loop(0, s // CONV_ROWS, _row_loop(s, step), 0)

    col = pl.BlockSpec((s, tc), lambda j: (0, j))
    return pl.pallas_call(
        body, name=name, grid=(c // tc,),
        in_specs=[col, pl.BlockSpec((8, tc), lambda j: (0, j)), pl.BlockSpec((1, tc), lambda j: (0, j))],
        out_specs=col, out_shape=jax.ShapeDtypeStruct((s, c), F32),
        compiler_params=_params(("parallel",)),
    )(xraw, w, b)


def _conv_a_bwd(xraw, w, b, dy, *, name, tc=128):
    s, c = xraw.shape
    kk = 4

    def body(x_ref, w_ref, b_ref, dy_ref, dx_ref, dw_ref, db_ref):
        def step(r0, carry):
            xe = _rows_with_halo(x_ref, r0, CONV_ROWS, s, True, True)
            pre = _conv_taps(xe, w_ref, kk, CONV_ROWS + HALO) + b_ref[...]
            sg = _sigmoid(pre)
            dpe = _rows_with_halo(dy_ref, r0, CONV_ROWS, s, False, True) * (sg * (1.0 + pre * (1.0 - sg)))
            dx, dws, db = _conv_bwd_rows(xe, dpe, w_ref, kk)
            dx_ref[pl.ds(r0, CONV_ROWS), :] = dx.astype(BF16)
            return tuple(acc + new for acc, new in zip(carry, dws + [db]))

        zero = jnp.zeros((1, tc), F32)
        sums = lax.fori_loop(0, s // CONV_ROWS, _row_loop(s, step), (zero,) * (kk + 1))
        db_ref[...] = sums[kk]
        dw_ref[...] = jnp.concatenate(list(sums[:kk]) + [jnp.zeros((8 - kk, tc), F32)], axis=0)

    col = pl.BlockSpec((s, tc), lambda j: (0, j))
    w8 = pl.BlockSpec((8, tc), lambda j: (0, j))
    row = pl.BlockSpec((1, tc), lambda j: (0, j))
    return pl.pallas_call(
        body, name=name, grid=(c // tc,),
        in_specs=[col, w8, row, col], out_specs=[col, w8, row],
        out_shape=[jax.ShapeDtypeStruct((s, c), BF16), jax.ShapeDtypeStruct((8, c), F32),
                   jax.ShapeDtypeStruct((1, c), F32)],
        compiler_params=_params(("parallel",)),
    )(xraw, w, b, dy)


def _conv_f_fwd(up_raw, w, b, *, name, tc=128):
    s, c2 = up_raw.shape
    c = c2 // 2
    nb = c // tc
    kk = 3

    def body(xa_ref, xv_ref, wa_ref, wv_ref, ba_ref, bv_ref, o_ref):
        def step(r0, carry):
            a = _conv_taps(_rows_with_halo(xa_ref, r0, CONV_ROWS, s, True, False), wa_ref, kk, CONV_ROWS) + ba_ref[...]
            v = _conv_taps(_rows_with_halo(xv_ref, r0, CONV_ROWS, s, True, False), wv_ref, kk, CONV_ROWS) + bv_ref[...]
            o_ref[pl.ds(r0, CONV_ROWS), :] = (a * _sigmoid(a) * v).astype(BF16)
            return carry

        lax.fori_loop(0, s // CONV_ROWS, _row_loop(s, step), 0)

    col_a = pl.BlockSpec((s, tc), lambda j: (0, j))
    col_v = pl.BlockSpec((s, tc), lambda j: (0, j + nb))
    return pl.pallas_call(
        body, name=name, grid=(nb,),
        in_specs=[col_a, col_v, pl.BlockSpec((8, tc), lambda j: (0, j)), pl.BlockSpec((8, tc), lambda j: (0, j + nb)),
                  pl.BlockSpec((1, tc), lambda j: (0, j)), pl.BlockSpec((1, tc), lambda j: (0, j + nb))],
        out_specs=col_a, out_shape=jax.ShapeDtypeStruct((s, c), BF16),
        compiler_params=_params(("parallel",)),
    )(up_raw, up_raw, w, w, b, b)


def _conv_f_bwd(up_raw, w, b, dact, *, name, tc=128):
    s, c2 = up_raw.shape
    c = c2 // 2
    nb = c // tc
    kk = 3

    def body(xa_ref, xv_ref, wa_ref, wv_ref, ba_ref, bv_ref, d_ref,
             dxa_ref, dxv_ref, dwa_ref, dwv_ref, dba_ref, dbv_ref):
        def step(r0, carry):
            xae = _rows_with_halo(xa_ref, r0, CONV_ROWS, s, True, True)
            xve = _rows_with_halo(xv_ref, r0, CONV_ROWS, s, True, True)
            a = _conv_taps(xae, wa_ref, kk, CONV_ROWS + HALO) + ba_ref[...]
            v = _conv_taps(xve, wv_ref, kk, CONV_ROWS + HALO) + bv_ref[...]
            sg = _sigmoid(a)
            d = _rows_with_halo(d_ref, r0, CONV_ROWS, s, False, True)
            dxa, dwas, dba = _conv_bwd_rows(xae, d * v * (sg * (1.0 + a * (1.0 - sg))), wa_ref, kk)
            dxv, dwvs, dbv = _conv_bwd_rows(xve, d * (a * sg), wv_ref, kk)
            dxa_ref[pl.ds(r0, CONV_ROWS), :] = dxa.astype(BF16)
            dxv_ref[pl.ds(r0, CONV_ROWS), :] = dxv.astype(BF16)
            return tuple(acc + new for acc, new in zip(carry, dwas + [dba] + dwvs + [dbv]))

        zero = jnp.zeros((1, tc), F32)
        sums = lax.fori_loop(0, s // CONV_ROWS, _row_loop(s, step), (zero,) * (2 * kk + 2))
        pad = [jnp.zeros((8 - kk, tc), F32)]
        dwa_ref[...] = jnp.concatenate(list(sums[:kk]) + pad, axis=0)
        dba_ref[...] = sums[kk]
        dwv_ref[...] = jnp.concatenate(list(sums[kk + 1:2 * kk + 1]) + pad, axis=0)
        dbv_ref[...] = sums[2 * kk + 1]

    col_a = pl.BlockSpec((s, tc), lambda j: (0, j))
    col_v = pl.BlockSpec((s, tc), lambda j: (0, j + nb))
    w_a = pl.BlockSpec((8, tc), lambda j: (0, j))
    w_v = pl.BlockSpec((8, tc), lambda j: (0, j + nb))
    r_a = pl.BlockSpec((1, tc), lambda j: (0, j))
    r_v = pl.BlockSpec((1, tc), lambda j: (0, j + nb))
    outs = pl.pallas_call(
        body, name=name, grid=(nb,),
        in_specs=[col_a, col_v, w_a, w_v, r_a, r_v, col_a],
        out_specs=[col_a, col_a, w_a, w_a, r_a, r_a],
        out_shape=[jax.ShapeDtypeStruct((s, c), BF16), jax.ShapeDtypeStruct((s, c), BF16),
                   jax.ShapeDtypeStruct((8, c), F32), jax.ShapeDtypeStruct((8, c), F32),
                   jax.ShapeDtypeStruct((1, c), F32), jax.ShapeDtypeStruct((1, c), F32)],
        compiler_params=_params(("parallel",)),
    )(up_raw, up_raw, w, w, b, b, dact)
    return outs


def _tri_masks():
    row = lax.broadcasted_iota(jnp.int32, (CHUNK, CHUNK), 0)
    col = lax.broadcasted_iota(jnp.int32, (CHUNK, CHUNK), 1)
    return row >= col, row <= col


def _ssd_fwd(xbc, dt_raw, z, dt_bias, a_log, a_log_x, d_skip_x, norm_w, expand, *, name):
    s = xbc.shape[0]
    nc = s // CHUNK

    def body(xbc_ref, dtr_ref, z_ref, dtb_ref, alog_ref, alogx_ref, dskx_ref, nw_ref, e_ref,
             y_ref, ya_ref, st_ref, state):
        @pl.when(pl.program_id(0) == 0)
        def _():
            state[...] = jnp.zeros_like(state)

        st_ref[0] = state[...]
        lower, _ = _tri_masks()
        tril = lower.astype(F32)
        dt = _softplus(dtr_ref[...] + dtb_ref[...])
        adt = dt * (-jnp.exp(alog_ref[...]))
        acum = _dot(tril, adt, precision=HI)
        acum_t = acum.T
        for g in range(SSD_GROUPS):
            sl = slice(GROUP_COLS * g, GROUP_COLS * (g + 1))
            dt_x = _dot(dt, e_ref[:, sl], precision=HI)
            adt_x = dt_x * (-jnp.exp(alogx_ref[:, sl]))
            acum_x = _dot(tril, adt_x, precision=HI)
            tot_x = jnp.sum(adt_x, axis=0, keepdims=True)
            xs = xbc_ref[:, sl]
            xdt = xs * dt_x
            xdt_b = xdt.astype(BF16)
            bg = xbc_ref[:, SSD_D_INNER + SSD_STATE * g:SSD_D_INNER + SSD_STATE * (g + 1)].astype(BF16)
            cg = xbc_ref[:, SSD_D_INNER + SSD_BC + SSD_STATE * g:SSD_D_INNER + SSD_BC + SSD_STATE * (g + 1)].astype(BF16)
            cb = _dot(cg, bg, NT)
            st_g = state[:, sl]
            y_off = _dot(cg, st_g.astype(BF16)) * jnp.exp(acum_x)
            parts = []
            for r in range(SSD_HEADS_PER_GROUP):
                h = SSD_HEADS_PER_GROUP * g + r
                dec = jnp.exp(jnp.where(lower, acum[:, h:h + 1] - acum_t[h:h + 1, :], -jnp.inf))
                parts.append(_dot((cb * dec).astype(BF16), xdt_b[:, SSD_HEAD_DIM * r:SSD_HEAD_DIM * (r + 1)]))
            y_ref[:, sl] = jnp.concatenate(parts, axis=1) + y_off + dskx_ref[:, sl] * xs
            wgt = (xdt * jnp.exp(tot_x - acum_x)).astype(BF16)
            state[:, sl] = st_g * jnp.exp(tot_x) + _dot(bg, wgt, TN)
        zv = z_ref[...]
        q = y_ref[...] * (zv * _sigmoid(zv))
        r = lax.rsqrt(jnp.mean(q * q, axis=-1, keepdims=True) + NORM_EPS)
        ya_ref[...] = (q * r * nw_ref[...]).astype(BF16)

    def chunk(w):
        return pl.BlockSpec((CHUNK, w), lambda c: (c, 0))

    def const(shape):
        return pl.BlockSpec(shape, lambda c: (0,) * len(shape))

    return pl.pallas_call(
        body, name=name, grid=(nc,),
        in_specs=[chunk(SSD_XBC), chunk(LANES), chunk(SSD_D_INNER), const((1, LANES)), const((1, LANES)),
                  const((1, SSD_D_INNER)), const((1, SSD_D_INNER)), const((1, SSD_D_INNER)),
                  const((LANES, SSD_D_INNER))],
        out_specs=[chunk(SSD_D_INNER), chunk(SSD_D_INNER),
                   pl.BlockSpec((1, SSD_STATE, SSD_D_INNER), lambda c: (c, 0, 0))],
        out_shape=[jax.ShapeDtypeStruct((s, SSD_D_INNER), F32), jax.ShapeDtypeStruct((s, SSD_D_INNER), BF16),
                   jax.ShapeDtypeStruct((nc, SSD_STATE, SSD_D_INNER), F32)],
        scratch_shapes=[pltpu.VMEM((SSD_STATE, SSD_D_INNER), F32)],
        compiler_params=_params(("arbitrary",)),
    )(xbc, dt_raw, z, dt_bias, a_log, a_log_x, d_skip_x, norm_w, expand)


def _ssd_bwd(dya, y, z, xbc, dt_raw, states, dt_bias, a_log, a_log_x, d_skip_x, norm_w, expand, expand_t, *, name):
    s = xbc.shape[0]
    nc = s // CHUNK

    def body(dya_ref, y_ref, z_ref, xbc_ref, dtr_ref, stp_ref, dtb_ref, alog_ref, alogx_ref, dskx_ref, nw_ref,
             e_ref, et_ref, dz_ref, dxbc_ref, ddt_ref, dnw_ref, ddsk_ref, dalog_ref, ddtb_ref,
             dstate, dy_sc, dskcol):
        i = pl.program_id(0)

        @pl.when(i == 0)
        def _():
            dstate[...] = jnp.zeros_like(dstate)
            dskcol[...] = jnp.zeros_like(dskcol)
            dnw_ref[...] = jnp.zeros_like(dnw_ref)
            dalog_ref[...] = jnp.zeros_like(dalog_ref)
            ddtb_ref[...] = jnp.zeros_like(ddtb_ref)
            ddsk_ref[...] = jnp.zeros_like(ddsk_ref)

        lower, upper = _tri_masks()
        tril = lower.astype(F32)
        rows = lax.broadcasted_iota(jnp.int32, (CHUNK, LANES), 0)
        pre = dtr_ref[...] + dtb_ref[...]
        dt = _softplus(pre)
        a = -jnp.exp(alog_ref[...])
        adt = dt * a
        acum = _dot(tril, adt, precision=HI)
        acum_t = acum.T

        yv = y_ref[...]
        zv = z_ref[...]
        sz = _sigmoid(zv)
        silu_z = zv * sz
        q = yv * silu_z
        r = lax.rsqrt(jnp.mean(q * q, axis=-1, keepdims=True) + NORM_EPS)
        qhat = q * r
        dyav = dya_ref[...]
        dqhat = dyav * nw_ref[...]
        dnw_ref[...] += jnp.sum(dyav * qhat, axis=0, keepdims=True)
        dq = r * (dqhat - qhat * jnp.mean(dqhat * qhat, axis=-1, keepdims=True))
        dy_sc[...] = dq * silu_z
        dz_ref[...] = (dq * yv * (sz * (1.0 + zv * (1.0 - sz)))).astype(BF16)

        da_cum = jnp.zeros((CHUNK, LANES), F32)
        ddt = jnp.zeros((CHUNK, LANES), F32)
        for g in range(SSD_GROUPS):
            sl = slice(GROUP_COLS * g, GROUP_COLS * (g + 1))
            et_g = et_ref[sl, :]
            dt_x = _dot(dt, e_ref[:, sl], precision=HI)
            adt_x = dt_x * (-jnp.exp(alogx_ref[:, sl]))
            acum_x = _dot(tril, adt_x, precision=HI)
            tot_x = jnp.sum(adt_x, axis=0, keepdims=True)
            e_tot = jnp.exp(tot_x)
            dec_s = jnp.exp(tot_x - acum_x)
            xs = xbc_ref[:, sl]
            xdt = xs * dt_x
            xdt_b = xdt.astype(BF16)
            dy = dy_sc[:, sl]
            dy_b = dy.astype(BF16)
            dskx = dskx_ref[:, sl]
            y_ssd = y_ref[:, sl] - dskx * xs
            dskcol[:, sl] += jnp.sum(dy * xs, axis=0, keepdims=True)
            bg = xbc_ref[:, SSD_D_INNER + SSD_STATE * g:SSD_D_INNER + SSD_STATE * (g + 1)].astype(BF16)
            cg = xbc_ref[:, SSD_D_INNER + SSD_BC + SSD_STATE * g:SSD_D_INNER + SSD_BC + SSD_STATE * (g + 1)].astype(BF16)
            cb = _dot(cg, bg, NT)
            sp = stp_ref[0, :, sl]
            ds_g = dstate[:, sl]
            ds_b = ds_g.astype(BF16)
            dye_b = (dy * jnp.exp(acum_x)).astype(BF16)
            dc = _dot(dye_b, sp.astype(BF16), NT)
            dxdt_state = dec_s * _dot(bg, ds_b)
            db = _dot((xdt * dec_s).astype(BF16), ds_b, NT)
            dcb = jnp.zeros((CHUNK, CHUNK), F32)
            parts = []
            for rr in range(SSD_HEADS_PER_GROUP):
                h = SSD_HEADS_PER_GROUP * g + rr
                hs = slice(SSD_HEAD_DIM * rr, SSD_HEAD_DIM * (rr + 1))
                dec = jnp.exp(jnp.where(lower, acum[:, h:h + 1] - acum_t[h:h + 1, :], -jnp.inf))
                parts.append(_dot((cb * dec).astype(BF16), dy_b[:, hs], TN))
                dcb = dcb + _dot(dy_b[:, hs], xdt_b[:, hs], NT) * dec
            dxdt = jnp.concatenate(parts, axis=1) + dxdt_state
            dcb_b = dcb.astype(BF16)
            dc = dc + _dot(dcb_b, bg)
            db = db + _dot(dcb_b, cg, TN)
            tot_col = jnp.sum(ds_g * sp, axis=0, keepdims=True) * e_tot + jnp.sum(dxdt_state * xdt, axis=0, keepdims=True)
            d_tot = _dot(jnp.broadcast_to(tot_col, (8, GROUP_COLS)), et_g, precision=HI)
            d_tot = jnp.max(d_tot, axis=0, keepdims=True)
            pair_sums = dy_b.astype(F32) * y_ssd - xdt_b.astype(F32) * dxdt
            da_cum = da_cum + _dot(pair_sums, et_g, precision=HI) + jnp.where(rows == CHUNK - 1, d_tot, 0.0)
            ddt = ddt + _dot(dxdt * xs, et_g, precision=HI)
            dxbc_ref[:, sl] = dy * dskx + dxdt * dt_x
            dxbc_ref[:, SSD_D_INNER + SSD_STATE * g:SSD_D_INNER + SSD_STATE * (g + 1)] = db
            dxbc_ref[:, SSD_D_INNER + SSD_BC + SSD_STATE * g:SSD_D_INNER + SSD_BC + SSD_STATE * (g + 1)] = dc
            dstate[:, sl] = e_tot * ds_g + _dot(cg, dye_b, TN)

        dadt = _dot(upper.astype(F32), da_cum, precision=HI)
        ddt = ddt + dadt * a
        dalog_ref[...] += jnp.sum(dadt * dt, axis=0, keepdims=True)
        dpre = ddt * _sigmoid(pre)
        ddtb_ref[...] += jnp.sum(dpre, axis=0, keepdims=True)
        ddt_ref[...] = dpre.astype(BF16)

        @pl.when(i == nc - 1)
        def _():
            dalog_ref[...] = dalog_ref[...] * a
            dsk = _dot(jnp.broadcast_to(dskcol[...], (8, SSD_D_INNER)), et_ref[...], precision=HI)
            ddsk_ref[...] = jnp.max(dsk, axis=0, keepdims=True)

    def chunk(w):
        return pl.BlockSpec((CHUNK, w), lambda i: (nc - 1 - i, 0))

    def const(shape):
        return pl.BlockSpec(shape, lambda i: (0,) * len(shape))

    return pl.pallas_call(
        body, name=name, grid=(nc,),
        in_specs=[chunk(SSD_D_INNER), chunk(SSD_D_INNER), chunk(SSD_D_INNER), chunk(SSD_XBC), chunk(LANES),
                  pl.BlockSpec((1, SSD_STATE, SSD_D_INNER), lambda i: (nc - 1 - i, 0, 0)),
                  const((1, LANES)), const((1, LANES)), const((1, SSD_D_INNER)), const((1, SSD_D_INNER)),
                  const((1, SSD_D_INNER)), const((LANES, SSD_D_INNER)), const((SSD_D_INNER, LANES))],
        out_specs=[chunk(SSD_D_INNER), chunk(SSD_XBC), chunk(LANES), const((1, SSD_D_INNER)), const((1, LANES)),
                   const((1, LANES)), const((1, LANES))],
        out_shape=[jax.ShapeDtypeStruct((s, SSD_D_INNER), BF16), jax.ShapeDtypeStruct((s, SSD_XBC), F32),
                   jax.ShapeDtypeStruct((s, LANES), BF16), jax.ShapeDtypeStruct((1, SSD_D_INNER), F32),
                   jax.ShapeDtypeStruct((1, LANES), F32), jax.ShapeDtypeStruct((1, LANES), F32),
                   jax.ShapeDtypeStruct((1, LANES), F32)],
        scratch_shapes=[pltpu.VMEM((SSD_STATE, SSD_D_INNER), F32), pltpu.VMEM((CHUNK, SSD_D_INNER), F32),
                        pltpu.VMEM((1, SSD_D_INNER), F32)],
        compiler_params=_params(("arbitrary",)),
    )(dya, y, z, xbc, dt_raw, states, dt_bias, a_log, a_log_x, d_skip_x, norm_w, expand, expand_t)


GELU_K = math.sqrt(2.0 / math.pi)
GELU_C = 0.044715


def _gelu(x):
    return 0.5 * x * (1.0 + jnp.tanh(GELU_K * (x + GELU_C * x * x * x)))


def _gelu_grad(x):
    t = jnp.tanh(GELU_K * (x + GELU_C * x * x * x))
    return 0.5 * (1.0 + t) + 0.5 * x * (1.0 - t * t) * (GELU_K * (1.0 + 3.0 * GELU_C * x * x))


def _sgu_pre(uv_ref, uvb_ref, lnw_ref, lnb_ref):
    uv = uv_ref[...] + uvb_ref[...]
    guv = _gelu(uv)
    u = guv[:, :SGU_WIDTH]
    v = guv[:, SGU_WIDTH:]
    mu = jnp.mean(v, axis=-1, keepdims=True)
    vc = v - mu
    rstd = lax.rsqrt(jnp.mean(vc * vc, axis=-1, keepdims=True) + LN_EPS)
    vhat = vc * rstd
    vn = vhat * lnw_ref[...] + lnb_ref[...]
    return uv, u, vhat, rstd, vn


def _sgu_fwd(uv_raw, uv_b, ln_w, ln_b, w_sp, b_sp_t, *, name):
    s = uv_raw.shape[0]
    nc = s // CHUNK

    def body(uv_ref, uvb_ref, lnw_ref, lnb_ref, w_ref, bt_ref, o_ref):
        lower, _ = _tri_masks()
        _, u, _, _, vn = _sgu_pre(uv_ref, uvb_ref, lnw_ref, lnb_ref)
        vn_b = vn.astype(BF16)
        bt = bt_ref[...]
        for g in range(SGU_GROUPS):
            gs = slice(LANES * g, LANES * (g + 1))
            wc = jnp.where(lower, w_ref[g], 0.0).astype(BF16)
            mixed = _dot(wc, vn_b[:, gs]) + bt[:, g:g + 1]
            o_ref[:, gs] = (u[:, gs] * mixed).astype(BF16)

    def const(shape):
        return pl.BlockSpec(shape, lambda c: (0,) * len(shape))

    return pl.pallas_call(
        body, name=name, grid=(nc,),
        in_specs=[pl.BlockSpec((CHUNK, 2 * SGU_WIDTH), lambda c: (c, 0)), const((1, 2 * SGU_WIDTH)),
                  const((1, SGU_WIDTH)), const((1, SGU_WIDTH)), const((SGU_GROUPS, CHUNK, CHUNK)),
                  const((CHUNK, LANES))],
        out_specs=pl.BlockSpec((CHUNK, SGU_WIDTH), lambda c: (c, 0)),
        out_shape=jax.ShapeDtypeStruct((s, SGU_WIDTH), BF16),
        compiler_params=_params(("parallel",)),
    )(uv_raw, uv_b, ln_w, ln_b, w_sp, b_sp_t)


def _sgu_bwd(uv_raw, dyb, uv_b, ln_w, ln_b, w_sp, b_sp_t, group_sum, *, name):
    s = uv_raw.shape[0]
    nc = s // CHUNK

    def body(uv_ref, dy_ref, uvb_ref, lnw_ref, lnb_ref, w_ref, bt_ref, gsum_ref,
             duv_ref, dw_ref, dbt_ref, dlnw_ref, dlnb_ref, duvb_ref):
        @pl.when(pl.program_id(0) == 0)
        def _():
            dw_ref[...] = jnp.zeros_like(dw_ref)
            dbt_ref[...] = jnp.zeros_like(dbt_ref)
            dlnw_ref[...] = jnp.zeros_like(dlnw_ref)
            dlnb_ref[...] = jnp.zeros_like(dlnb_ref)
            duvb_ref[...] = jnp.zeros_like(duvb_ref)

        lower, _ = _tri_masks()
        uv, u, vhat, rstd, vn = _sgu_pre(uv_ref, uvb_ref, lnw_ref, lnb_ref)
        vn_b = vn.astype(BF16)
        bt = bt_ref[...]
        dy = dy_ref[...]
        du_parts, dvn_parts, dmix_parts = [], [], []
        for g in range(SGU_GROUPS):
            gs = slice(LANES * g, LANES * (g + 1))
            wc = jnp.where(lower, w_ref[g], 0.0).astype(BF16)
            mixed = _dot(wc, vn_b[:, gs]) + bt[:, g:g + 1]
            du_parts.append(dy[:, gs] * mixed)
            dmix = dy[:, gs] * u[:, gs]
            dmix_b = dmix.astype(BF16)
            dmix_parts.append(dmix)
            dw_ref[g] += jnp.where(lower, _dot(dmix_b, vn_b[:, gs], NT), 0.0)
            dvn_parts.append(_dot(wc, dmix_b, TN))
        dmixed = jnp.concatenate(dmix_parts, axis=1)
        dbt_ref[...] += _dot(dmixed, gsum_ref[...], precision=HI)
        dvn = jnp.concatenate(dvn_parts, axis=1)
        dlnw_ref[...] += jnp.sum(dvn * vhat, axis=0, keepdims=True)
        dlnb_ref[...] += jnp.sum(dvn, axis=0, keepdims=True)
        dvhat = dvn * lnw_ref[...]
        dv = rstd * (dvhat - jnp.mean(dvhat, axis=-1, keepdims=True)
                     - vhat * jnp.mean(dvhat * vhat, axis=-1, keepdims=True))
        dguv = jnp.concatenate(du_parts + [dv], axis=1)
        duv = dguv * _gelu_grad(uv)
        duvb_ref[...] += jnp.sum(duv, axis=0, keepdims=True)
        duv_ref[...] = duv.astype(BF16)

    def const(shape):
        return pl.BlockSpec(shape, lambda c: (0,) * len(shape))

    return pl.pallas_call(
        body, name=name, grid=(nc,),
        in_specs=[pl.BlockSpec((CHUNK, 2 * SGU_WIDTH), lambda c: (c, 0)),
                  pl.BlockSpec((CHUNK, SGU_WIDTH), lambda c: (c, 0)), const((1, 2 * SGU_WIDTH)),
                  const((1, SGU_WIDTH)), const((1, SGU_WIDTH)), const((SGU_GROUPS, CHUNK, CHUNK)),
                  const((CHUNK, LANES)), const((SGU_WIDTH, LANES))],
        out_specs=[pl.BlockSpec((CHUNK, 2 * SGU_WIDTH), lambda c: (c, 0)), const((SGU_GROUPS, CHUNK, CHUNK)),
                   const((CHUNK, LANES)), const((1, SGU_WIDTH)), const((1, SGU_WIDTH)), const((1, 2 * SGU_WIDTH))],
        out_shape=[jax.ShapeDtypeStruct((s, 2 * SGU_WIDTH), BF16),
                   jax.ShapeDtypeStruct((SGU_GROUPS, CHUNK, CHUNK), F32), jax.ShapeDtypeStruct((CHUNK, LANES), F32),
                   jax.ShapeDtypeStruct((1, SGU_WIDTH), F32), jax.ShapeDtypeStruct((1, SGU_WIDTH), F32),
                   jax.ShapeDtypeStruct((1, 2 * SGU_WIDTH), F32)],
        compiler_params=_params(("arbitrary",)),
    )(uv_raw, dyb, uv_b, ln_w, ln_b, w_sp, b_sp_t, group_sum)


def _gate_fwd(gates_raw, b_gate, p_a, p_b, *, name, tm=512):
    s = p_a.shape[0]
    tm = min(tm, s)

    def body(ga_ref, gb_ref, ba_ref, bb_ref, pa_ref, pb_ref, o_ref):
        ga = _sigmoid(ga_ref[...] + ba_ref[...])
        gb = _sigmoid(gb_ref[...] + bb_ref[...])
        o_ref[...] = (ga * pa_ref[...] + gb * pb_ref[...]).astype(BF16)

    t_a = pl.BlockSpec((tm, D_MODEL), lambda i: (i, 0))
    t_b = pl.BlockSpec((tm, D_MODEL), lambda i: (i, 1))
    r_a = pl.BlockSpec((1, D_MODEL), lambda i: (0, 0))
    r_b = pl.BlockSpec((1, D_MODEL), lambda i: (0, 1))
    return pl.pallas_call(
        body, name=name, grid=(s // tm,),
        in_specs=[t_a, t_b, r_a, r_b, t_a, t_a], out_specs=t_a,
        out_shape=jax.ShapeDtypeStruct((s, D_MODEL), BF16),
        compiler_params=_params(("parallel",)),
    )(gates_raw, gates_raw, b_gate, b_gate, p_a, p_b)


def _gate_bwd(gates_raw, b_gate, p_a, p_b, dm, *, name, tm=512):
    s = p_a.shape[0]
    tm = min(tm, s)

    def body(ga_ref, gb_ref, ba_ref, bb_ref, pa_ref, pb_ref, dm_ref, dpa_ref, dpb_ref, dga_ref, dgb_ref,
             dba_ref, dbb_ref):
        @pl.when(pl.program_id(0) == 0)
        def _():
            dba_ref[...] = jnp.zeros_like(dba_ref)
            dbb_ref[...] = jnp.zeros_like(dbb_ref)

        d = dm_ref[...]
        for g_ref, b_ref, p_ref, dp_ref, dg_ref, db_ref in ((ga_ref, ba_ref, pa_ref, dpa_ref, dga_ref, dba_ref),
                                                            (gb_ref, bb_ref, pb_ref, dpb_ref, dgb_ref, dbb_ref)):
            sg = _sigmoid(g_ref[...] + b_ref[...])
            dp_ref[...] = (d * sg).astype(BF16)
            dg = d * p_ref[...] * (sg * (1.0 - sg))
            dg_ref[...] = dg.astype(BF16)
            db_ref[...] += jnp.sum(dg, axis=0, keepdims=True)

    t_a = pl.BlockSpec((tm, D_MODEL), lambda i: (i, 0))
    t_b = pl.BlockSpec((tm, D_MODEL), lambda i: (i, 1))
    r_a = pl.BlockSpec((1, D_MODEL), lambda i: (0, 0))
    r_b = pl.BlockSpec((1, D_MODEL), lambda i: (0, 1))
    big = jax.ShapeDtypeStruct((s, D_MODEL), BF16)
    row = jax.ShapeDtypeStruct((1, D_MODEL), F32)
    return pl.pallas_call(
        body, name=name, grid=(s // tm,),
        in_specs=[t_a, t_b, r_a, r_b, t_a, t_a, t_a], out_specs=[t_a, t_a, t_a, t_a, r_a, r_a],
        out_shape=[big, big, big, big, row, row],
        compiler_params=_params(("arbitrary",)),
    )(gates_raw, gates_raw, b_gate, b_gate, p_a, p_b, dm)


def _adamw(w, g, m, v, *, name, tr=128):
    r, c = w.shape
    tr = min(tr, r)
    assert r % tr == 0, (name, r, tr)

    def body(w_ref, g_ref, m_ref, v_ref, d_ref, mo_ref, vo_ref):
        gv = g_ref[...]
        mn = ADAM_B1 * m_ref[...] + (1.0 - ADAM_B1) * gv
        vn = ADAM_B2 * v_ref[...] + (1.0 - ADAM_B2) * (gv * gv)
        m_hat = mn / (1.0 - ADAM_B1 ** ADAM_STEP)
        v_hat = vn / (1.0 - ADAM_B2 ** ADAM_STEP)
        d_ref[...] = -ADAM_LR * (m_hat / (jnp.sqrt(v_hat) + ADAM_EPS) + ADAM_WD * w_ref[...])
        mo_ref[...] = mn
        vo_ref[...] = vn

    blk = pl.BlockSpec((tr, c), lambda i: (i, 0))
    sds = jax.ShapeDtypeStruct((r, c), F32)
    return pl.pallas_call(
        body, name=name, grid=(r // tr,), in_specs=[blk] * 4, out_specs=[blk] * 3, out_shape=[sds] * 3,
        compiler_params=_params(("parallel",)),
    )(w, g, m, v)


def _tile(n, pref):
    if n <= pref:
        return n
    best = LANES
    for t in range(LANES, pref + 1, LANES):
        if n % t == 0:
            best = t
    return best


MATMUL_BLOCK_BYTES = 20 * 1024 * 1024


def _mm(pairs, name, **kw):
    trans_b = kw.get("trans_b", False)
    m = (pairs[0][0][0] if isinstance(pairs[0][0], tuple) else pairs[0][0]).shape[0]
    ktot, n = 0, None
    for _, b in pairs:
        shape = b[0].shape[1:] if isinstance(b, tuple) else b.shape
        ktot += shape[1] if trans_b else shape[0]
        n = shape[0] if trans_b else shape[1]
    out_bytes = 4 * (2 if kw.get("add") is not None else 1)
    best = None
    for tm in (256, 512, 1024):
        for tn in range(LANES, min(n, 1536) + 1, LANES):
            if m % min(tm, m) or n % tn:
                continue
            fits = 2 * ktot * (min(tm, m) + tn) + out_bytes * min(tm, m) * tn <= MATMUL_BLOCK_BYTES
            if fits and (best is None or min(tm, m) * tn >= best[0] * best[1]):
                best = (min(tm, m), tn)
    return _matmul(pairs, tm=best[0], tn=best[1], name=name, **kw)


def _wgrad(a, b, name, **kw):
    return _matmul_tn(a, b, tk=_tile(a.shape[1], 1408), tn=kw.pop("tn", _tile(b.shape[1], 1024)), tm=2048,
                      name=name, **kw)


def _local_step(x, target, get_weight, small, emit_grad):
    heads = jnp.arange(SSD_D_INNER) // SSD_HEAD_DIM
    expand = (jnp.arange(LANES)[:, None] == heads[None, :]).astype(F32)
    expand_t = expand.T
    group_sum = (jnp.arange(SGU_WIDTH)[:, None] // LANES == jnp.arange(LANES)[None, :]).astype(F32)
    pad_h = LANES - SSD_HEADS
    dt_bias = jnp.pad(small["dt_bias"], ((0, 0), (0, pad_h)))
    a_log = jnp.pad(small["a_log"], ((0, 0), (0, pad_h)))
    a_log_x = jnp.repeat(small["a_log"], SSD_HEAD_DIM, axis=1)
    d_skip_x = jnp.repeat(small["d_skip"], SSD_HEAD_DIM, axis=1)
    b_sp_t = jnp.pad(small["b_spatial"][0].T, ((0, 0), (0, LANES - SGU_GROUPS)))
    w_sp = small["w_spatial"][0]
    conv_a_w = jnp.pad(small["conv_a_w"], ((0, 4), (0, 0)))
    conv_f_w = jnp.pad(small["conv_f_w"], ((0, 5), (0, 0)))
    final_w = small["final_norm_w"].reshape(1, D_MODEL)

    n1 = _rms_fwd(x, small["norm1_w"], name="rms1_fwd")
    wts = dict(get_weight("w_in", n1))
    z = _mm([(n1, wts["in_z"])], "in_z")
    xbc_raw = _mm([(n1, wts["in_xbc"])], "in_xbc")
    dt_raw = _mm([(n1, wts["in_dt"])], "in_dt")
    uv_raw = _mm([(n1, wts["in_uv"])], "in_uv")
    gates_raw = _mm([(n1, wts["in_gate"])], "in_gate")
    xbc = _conv_a_fwd(xbc_raw, conv_a_w, small["conv_a_b"], name="conv_a_fwd")
    y, y_a, states = _ssd_fwd(xbc, dt_raw, z, dt_bias, a_log, a_log_x, d_skip_x, small["ssd_norm_w"], expand,
                              name="ssd_fwd")
    y_b = _sgu_fwd(uv_raw, small["uv_b"], small["v_ln_w"], small["v_ln_b"], w_sp, b_sp_t, name="sgu_fwd")
    wts.update(get_weight("w_branch", y_b))
    p_a = _mm([(y_a, wts["branch_a"])], "branch_a")
    p_b = _mm([(y_b, wts["branch_b"])], "branch_b")
    mix = _gate_fwd(gates_raw, small["b_gate"], p_a, p_b, name="gate_fwd")
    wts.update(get_weight("w_out", mix))
    h1 = _mm([(mix, wts["out"])], "out_proj", add=x)
    n2 = _rms_fwd(h1, small["norm2_w"], name="rms2_fwd")
    wts.update(get_weight("w_up", n2))
    up_w = wts["up"]
    up_cols = up_w.shape[2]
    up_raw = _matmul([(n2, (up_w, "cols"))], tm=1024, tn=up_cols, name="up_proj")
    act = _conv_f_fwd(up_raw, conv_f_w, small["conv_f_b"], name="conv_f_fwd")
    wts.update(get_weight("w_down", act))
    h2 = _mm([(act, wts["down"])], "down_proj", add=h1)
    loss, dh2, dh2_b, d_final = _final_fwd_bwd(h2, final_w, target, name="final_norm_loss")

    dact = _mm([(dh2_b, wts["down"])], "down_dgrad", trans_b=True)
    started = emit_grad("w_down", _wgrad(act, dh2_b, "down_wgrad"))
    dup_a, dup_v, dwf_a, dwf_v, dbf_a, dbf_v = _conv_f_bwd(up_raw, conv_f_w, small["conv_f_b"], dact,
                                                           name="conv_f_bwd")
    dn2 = _mm([((dup_a, 0), (up_w, 0)), ((dup_a, 1), (up_w, 1)), ((dup_v, 0), (up_w, 2)), ((dup_v, 1), (up_w, 3))],
              "up_dgrad", trans_b=True, after=started)
    started = emit_grad("w_up", jnp.concatenate([_wgrad(n2, dup_a, "up_wgrad_a", tn=up_cols, stack_out=True),
                                                 _wgrad(n2, dup_v, "up_wgrad_v", tn=up_cols, stack_out=True)], axis=0))
    dh1, dh1_b, d_norm2 = _rms_bwd(h1, small["norm2_w"], dn2, dh2, name="rms2_bwd")
    dmix = _mm([(dh1_b, wts["out"])], "out_dgrad", trans_b=True, after=started)
    started = emit_grad("w_out", _wgrad(mix, dh1_b, "out_wgrad"))
    dp_a, dp_b, dg_a, dg_b, dbg_a, dbg_b = _gate_bwd(gates_raw, small["b_gate"], p_a, p_b, dmix, name="gate_bwd")
    dya = _mm([(dp_a, wts["branch_a"])], "branch_a_dgrad", trans_b=True, after=started)
    dyb = _mm([(dp_b, wts["branch_b"])], "branch_b_dgrad", trans_b=True)
    started_branch = emit_grad("w_branch", jnp.concatenate([_wgrad(y_a, dp_a, "branch_a_wgrad"),
                                                            _wgrad(y_b, dp_b, "branch_b_wgrad")], axis=0))
    duv, d_wsp, d_bsp_t, d_lnw, d_lnb, d_uvb = _sgu_bwd(uv_raw, dyb, small["uv_b"], small["v_ln_w"],
                                                        small["v_ln_b"], w_sp, b_sp_t, group_sum, name="sgu_bwd")
    dz, dxbc, ddt, d_ssd_nw, d_dskip, d_alog, d_dtb = _ssd_bwd(
        dya, y, z, xbc, dt_raw, states, dt_bias, a_log, a_log_x, d_skip_x, small["ssd_norm_w"], expand, expand_t,
        name="ssd_bwd")
    dxbc_raw, d_conv_a_w, d_conv_a_b = _conv_a_bwd(xbc_raw, conv_a_w, small["conv_a_b"], dxbc, name="conv_a_bwd")
    started = emit_grad("w_in", jnp.concatenate(
        [_wgrad(n1, dz, "in_z_wgrad", after=started_branch), _wgrad(n1, dxbc_raw, "in_xbc_wgrad"),
         _wgrad(n1, ddt, "in_dt_wgrad")[:, :SSD_HEADS], _wgrad(n1, duv, "in_uv_wgrad"),
         _wgrad(n1, dg_a, "in_gate_a_wgrad"), _wgrad(n1, dg_b, "in_gate_b_wgrad")], axis=1))
    dn1 = _mm([(dz, wts["in_z"]), (dxbc_raw, wts["in_xbc"]), (ddt, wts["in_dt"]), (duv, wts["in_uv"]),
               (dg_a, wts["in_gate_a"]), (dg_b, wts["in_gate_b"])], "in_dgrad", trans_b=True, after=started)
    dx, _, d_norm1 = _rms_bwd(x, small["norm1_w"], dn1, dh1, name="rms1_bwd")

    grads_small = {
        "norm1_w": d_norm1, "b_gate": jnp.concatenate([dbg_a, dbg_b], axis=1),
        "conv_a_w": d_conv_a_w[:4], "conv_a_b": d_conv_a_b,
        "dt_bias": d_dtb[:, :SSD_HEADS], "a_log": d_alog[:, :SSD_HEADS], "d_skip": d_dskip[:, :SSD_HEADS],
        "ssd_norm_w": d_ssd_nw, "uv_b": d_uvb, "v_ln_w": d_lnw, "v_ln_b": d_lnb,
        "w_spatial": d_wsp[None], "b_spatial": d_bsp_t[:, :SGU_GROUPS].T[None],
        "norm2_w": d_norm2, "conv_f_w": jnp.concatenate([dwf_a[:3], dwf_v[:3]], axis=1),
        "conv_f_b": jnp.concatenate([dbf_a, dbf_v], axis=1), "final_norm_w": d_final.reshape(D_MODEL),
    }
    return loss, dx, grads_small


HBM = pl.BlockSpec(memory_space=pl.ANY)
MESH = pl.DeviceIdType.MESH


def _mesh_pos():
    return lax.axis_index("x"), lax.axis_index("y"), lax.axis_index("c")


def _other_chips(x, y):
    return [(1 - x, y), (x, 1 - y), (1 - x, 1 - y)]


def _remote(src, dst, send_sems, recv_sems, k, dev):
    return pltpu.make_async_remote_copy(src_ref=src, dst_ref=dst, send_sem=send_sems.at[k], recv_sem=recv_sems.at[k],
                                        device_id=dev, device_id_type=MESH)


def _dma_sems(n):
    return [pltpu.SemaphoreType.DMA((n,)), pltpu.SemaphoreType.DMA((n,))]


HBM_ONLY = pl.BlockSpec(memory_space=pltpu.HBM)
SEMAPHORES = pl.BlockSpec(memory_space=pltpu.SEMAPHORE)
DATAFLOW_EFFECT = pltpu.SideEffectType.DATAFLOW_SIDE_EFFECTING
N_PEER_CHIPS = N_CHIPS - 1


def _gather_sends(w_ref, land_ref, send_sems, recv_sems):
    x, y, c = _mesh_pos()
    return [_remote(w_ref.at[c], land_ref.at[2 * x + y, c], send_sems, recv_sems, k, (px, py, c))
            for k, (px, py) in enumerate(_other_chips(x, y))]


def _gather_arrivals(w_ref, land_ref, send_sems, recv_sems):
    x, y, c = _mesh_pos()
    return [_remote(w_ref.at[c], land_ref.at[2 * px + py, c], send_sems, recv_sems, k, (px, py, c))
            for k, (px, py) in enumerate(_other_chips(x, y))]


def _scatter_sends(h_ref, land_ref, send_sems, recv_sems):
    x, y, c = _mesh_pos()
    return [_remote(h_ref.at[2 * px + py], land_ref.at[2 * x + y], send_sems, recv_sems, k, (px, py, c))
            for k, (px, py) in enumerate(_other_chips(x, y))]


def _scatter_arrivals(h_ref, land_ref, send_sems, recv_sems):
    x, y, c = _mesh_pos()
    return [_remote(h_ref.at[2 * x + y], land_ref.at[2 * px + py], send_sems, recv_sems, k, (px, py, c))
            for k, (px, py) in enumerate(_other_chips(x, y))]


def _exchange_start(sources, landing_shapes, sends, *, name):
    n = len(sources)

    def body(*refs):
        sems = refs[2 * n:4 * n]
        for i in range(n):
            for cp in sends(refs[i], refs[n + i], sems[2 * i], sems[2 * i + 1]):
                cp.start()
        refs[-1][...] = jnp.zeros_like(refs[-1])

    hbm = [pltpu.HBM(s.shape, s.dtype) for s in sources] + [pltpu.HBM(shp, s.dtype)
                                                             for shp, s in zip(landing_shapes, sources)]
    outs = pl.pallas_call(
        body, name=name,
        out_shape=tuple([pltpu.SemaphoreType.DMA((N_PEER_CHIPS,))] * (2 * n) + hbm
                        + [jax.ShapeDtypeStruct((8, LANES), F32)]),
        in_specs=[HBM_ONLY] * (2 * n),
        out_specs=tuple([SEMAPHORES] * (2 * n) + [HBM_ONLY] * (2 * n) + [pl.BlockSpec(memory_space=pltpu.VMEM)]),
        input_output_aliases={i: 2 * n + i for i in range(2 * n)},
        compiler_params=pltpu.CompilerParams(has_side_effects=DATAFLOW_EFFECT),
    )(*[pltpu.with_memory_space_constraint(s, pltpu.HBM) for s in sources],
      *[pltpu.with_memory_space_constraint(lax.empty(shp, s.dtype), pltpu.HBM)
        for shp, s in zip(landing_shapes, sources)])
    pending = [(outs[2 * i], outs[2 * i + 1], outs[2 * n + i], outs[3 * n + i]) for i in range(n)]
    return pending, outs[-1]


def _exchange_wait(pending, after, sends, arrivals, *, name):
    send_sems, recv_sems, source, landing = pending

    def body(src_ref, land_ref, send_ref, recv_ref, after_ref, src_out, land_out):
        for cp in sends(src_ref, land_ref, send_ref, recv_ref):
            cp.wait_send()
        for cp in arrivals(src_ref, land_ref, send_ref, recv_ref):
            cp.wait_recv()

    return pl.pallas_call(
        body, name=name,
        out_shape=(pltpu.HBM(source.shape, source.dtype), pltpu.HBM(landing.shape, landing.dtype)),
        in_specs=[HBM_ONLY, HBM_ONLY, SEMAPHORES, SEMAPHORES, pl.BlockSpec(memory_space=pl.ANY)],
        out_specs=(HBM_ONLY, HBM_ONLY), input_output_aliases={0: 0, 1: 1},
        compiler_params=pltpu.CompilerParams(has_side_effects=DATAFLOW_EFFECT),
    )(source, landing, send_sems, recv_sems, after)


def _gather_ici(shard, *, name):
    _, rh, cols = shard.shape

    def body(w_ref, o_ref, send_sems, recv_sems):
        x, y, c = _mesh_pos()
        mine = 2 * x + y
        sends = []
        for k, (px, py) in enumerate(_other_chips(x, y)):
            cp = _remote(w_ref.at[c], o_ref.at[mine, c], send_sems, recv_sems, k, (px, py, c))
            cp.start()
            sends.append(cp)
        for k, (px, py) in enumerate(_other_chips(x, y)):
            _remote(w_ref.at[c], o_ref.at[2 * px + py, c], send_sems, recv_sems, k, (px, py, c)).wait_recv()
        for cp in sends:
            cp.wait_send()

    return pl.pallas_call(
        body, name=name, in_specs=[HBM], out_specs=HBM,
        out_shape=jax.ShapeDtypeStruct((N_CHIPS, 2, rh, cols), shard.dtype), scratch_shapes=_dma_sems(3),
    )(shard)


def _gather_d2d(parts, *, name):
    def body(a_ref, o_ref, send_sems, recv_sems):
        x, y, c = _mesh_pos()
        sibling = (x, y, 1 - c)
        sends = []
        for k, (px, py) in enumerate(_other_chips(x, y)):
            cp = _remote(a_ref.at[2 * px + py, c], o_ref.at[2 * px + py, c], send_sems, recv_sems, k, sibling)
            cp.start()
            sends.append(cp)
        for k, (px, py) in enumerate(_other_chips(x, y)):
            _remote(a_ref.at[2 * px + py, c], o_ref.at[2 * px + py, 1 - c], send_sems, recv_sems, k, sibling).wait_recv()
        for cp in sends:
            cp.wait_send()

    return pl.pallas_call(
        body, name=name, in_specs=[HBM], out_specs=HBM,
        out_shape=jax.ShapeDtypeStruct(parts.shape, parts.dtype),
        input_output_aliases={0: 0}, scratch_shapes=_dma_sems(3),
    )(parts)


def _all_gather_chips(shard_flat, name):
    rows, cols = shard_flat.shape
    parts = _gather_ici(shard_flat.reshape(2, rows // 2, cols), name=name + "_ici")
    others = _gather_d2d(parts, name=name + "_d2d").reshape(N_CHIPS, rows, cols)
    chip = 2 * lax.axis_index("x") + lax.axis_index("y")
    return lax.dynamic_update_slice(others, shard_flat[None], (chip, 0, 0))


def _row_tile(rows, mult, cap):
    best = mult
    for t in range(mult, min(rows, cap) + 1, mult):
        if rows % t == 0:
            best = t
    assert rows % best == 0, (rows, mult)
    return best


def _swap_halves_d2d(g, *, name):
    _, _, rh, cols = g.shape

    def body(g_ref, o_ref, send_sems, recv_sems):
        x, y, c = _mesh_pos()
        sibling = (x, y, 1 - c)
        sends = []
        for s in range(N_CHIPS):
            cp = _remote(g_ref.at[s, 1 - c], o_ref.at[s], send_sems, recv_sems, s, sibling)
            cp.start()
            sends.append(cp)
        for s in range(N_CHIPS):
            _remote(g_ref.at[s, c], o_ref.at[s], send_sems, recv_sems, s, sibling).wait_recv()
        for cp in sends:
            cp.wait_send()

    return pl.pallas_call(
        body, name=name, in_specs=[HBM], out_specs=HBM,
        out_shape=jax.ShapeDtypeStruct((N_CHIPS, rh, cols), g.dtype), scratch_shapes=_dma_sems(N_CHIPS),
    )(g)


def _add_own_half(g, arrived, core, *, name):
    _, _, rh, cols = g.shape
    mult = 16 if g.dtype == BF16 else 8
    tr = _row_tile(rh, mult, max(mult, (512 * 1024) // cols))

    def body(core_ref, g_ref, a_ref, o_ref):
        o_ref[...] = (g_ref[0].astype(F32) + a_ref[...].astype(F32)).astype(o_ref.dtype)

    grid_spec = pltpu.PrefetchScalarGridSpec(
        num_scalar_prefetch=1, grid=(N_CHIPS, rh // tr),
        in_specs=[pl.BlockSpec((1, 1, tr, cols), lambda s, i, core_ref: (s, core_ref[0], i, 0)),
                  pl.BlockSpec((1, tr, cols), lambda s, i, core_ref: (s, i, 0))],
        out_specs=pl.BlockSpec((1, tr, cols), lambda s, i, core_ref: (s, i, 0)))
    return pl.pallas_call(
        body, name=name, grid_spec=grid_spec, out_shape=jax.ShapeDtypeStruct((N_CHIPS, rh, cols), g.dtype),
        compiler_params=_params(("parallel", "parallel")),
    )(core, g, arrived)


def _scatter_ici(h, *, name):
    def body(h_ref, o_ref, send_sems, recv_sems):
        x, y, c = _mesh_pos()
        mine = 2 * x + y
        sends = []
        for k, (px, py) in enumerate(_other_chips(x, y)):
            cp = _remote(h_ref.at[2 * px + py], o_ref.at[mine], send_sems, recv_sems, k, (px, py, c))
            cp.start()
            sends.append(cp)
        for k, (px, py) in enumerate(_other_chips(x, y)):
            _remote(h_ref.at[mine], o_ref.at[2 * px + py], send_sems, recv_sems, k, (px, py, c)).wait_recv()
        for cp in sends:
            cp.wait_send()

    others = pl.pallas_call(
        body, name=name, in_specs=[HBM], out_specs=HBM, out_shape=jax.ShapeDtypeStruct(h.shape, h.dtype),
        scratch_shapes=_dma_sems(3),
    )(h)
    chip = 2 * lax.axis_index("x") + lax.axis_index("y")
    own = lax.dynamic_slice_in_dim(h, chip, 1, axis=0)
    return lax.dynamic_update_slice(others, own, (chip, 0, 0))


def _sum_chips(parts, *, name):
    _, rh, cols = parts.shape
    mult = 16 if parts.dtype == BF16 else 8
    tr = _row_tile(rh, mult, max(mult, (512 * 1024) // cols))

    def body(p_ref, o_ref):
        acc = p_ref[0].astype(F32)
        for s in range(1, N_CHIPS):
            acc = acc + p_ref[s].astype(F32)
        o_ref[...] = acc

    return pl.pallas_call(
        body, name=name, grid=(rh // tr,),
        in_specs=[pl.BlockSpec((N_CHIPS, tr, cols), lambda i: (0, i, 0))],
        out_specs=pl.BlockSpec((tr, cols), lambda i: (i, 0)),
        out_shape=jax.ShapeDtypeStruct((rh, cols), F32), compiler_params=_params(("parallel",)),
    )(parts)


def _share_d2d(f, *, name):
    def body(f_ref, o_ref, send_sems, recv_sems):
        x, y, c = _mesh_pos()
        sibling = (x, y, 1 - c)
        cp = _remote(f_ref, o_ref, send_sems, recv_sems, 0, sibling)
        cp.start()
        cp.wait()

    other = pl.pallas_call(
        body, name=name, in_specs=[HBM], out_specs=HBM, out_shape=jax.ShapeDtypeStruct(f.shape, f.dtype),
        scratch_shapes=_dma_sems(1),
    )(f)
    first = lax.axis_index("c") == 0
    return jnp.stack([jnp.where(first, f, other), jnp.where(first, other, f)])


def _reduce_scatter_chips(g, core, name):
    _, rows, cols = g.shape
    g = g.reshape(N_CHIPS, 2, rows // 2, cols)
    arrived = _swap_halves_d2d(g, name=name + "_swap")
    chip_sum = _add_own_half(g, arrived, core, name=name + "_add2")
    parts = _scatter_ici(chip_sum, name=name + "_ici")
    total = _sum_chips(parts, name=name + "_sum4")
    return _share_d2d(total, name=name + "_share").reshape(rows, cols)


BIG = ("w_in", "w_branch", "w_out", "w_up", "w_down")
BIG_COLUMN_SHARDED = ("w_in", "w_up")
CONV = ("conv_a_w", "conv_f_w")
REPLICATED = ("norm1_w", "b_gate", "conv_a_b", "dt_bias", "a_log", "d_skip", "ssd_norm_w", "uv_b", "v_ln_w",
              "v_ln_b", "w_spatial", "b_spatial", "norm2_w", "conv_f_b", "final_norm_w")
WEIGHT_ORDER = ("norm1_w", "w_in", "b_gate", "conv_a_w", "conv_a_b", "dt_bias", "a_log", "d_skip", "ssd_norm_w",
                "uv_b", "v_ln_w", "v_ln_b", "w_spatial", "b_spatial", "w_branch", "w_out", "norm2_w", "w_up",
                "conv_f_w", "conv_f_b", "w_down", "final_norm_w")
SMALL_EXCHANGE_ROWS = 64


def _flat_rows(arrays, row_multiple):
    flat = jnp.concatenate([a.reshape(-1) for a in arrays])
    rows = -(-flat.shape[0] // (LANES * row_multiple)) * row_multiple
    return jnp.pad(flat, (0, rows * LANES - flat.shape[0])).reshape(rows, LANES)


def _unflatten(flat, shapes):
    flat = flat.reshape(-1)
    out, off = [], 0
    for shp in shapes:
        n = math.prod(shp)
        out.append(flat[off:off + n].reshape(shp))
        off += n
    return out


def _from_chip_blocks(blocks, name):
    if name in BIG_COLUMN_SHARDED or name in CONV:
        k = blocks.shape[1]
        return jnp.transpose(blocks, (1, 0, 2)).reshape(k, -1)
    return blocks.reshape(-1, blocks.shape[-1])


def _to_chip_blocks(whole, name):
    if name in BIG_COLUMN_SHARDED or name in CONV:
        k, n = whole.shape
        return jnp.transpose(whole.reshape(k, N_CHIPS, n // N_CHIPS), (1, 0, 2))
    return whole.reshape(N_CHIPS, whole.shape[0] // N_CHIPS, whole.shape[1])


def kernel(x, norm1_w, w_in, b_gate, conv_a_w, conv_a_b, dt_bias, a_log, d_skip, ssd_norm_w, uv_b, v_ln_w, v_ln_b, w_spatial, b_spatial, w_branch, w_out, norm2_w, w_up, conv_f_w, conv_f_b, w_down, final_norm_w, loss_target, m_norm1_w, m_w_in, m_b_gate, m_conv_a_w, m_conv_a_b, m_dt_bias, m_a_log, m_d_skip, m_ssd_norm_w, m_uv_b, m_v_ln_w, m_v_ln_b, m_w_spatial, m_b_spatial, m_w_branch, m_w_out, m_norm2_w, m_w_up, m_conv_f_w, m_conv_f_b, m_w_down, m_final_norm_w, v_norm1_w, v_w_in, v_b_gate, v_conv_a_w, v_conv_a_b, v_dt_bias, v_a_log, v_d_skip, v_ssd_norm_w, v_uv_b, v_v_ln_w, v_v_ln_b, v_w_spatial, v_b_spatial, v_w_branch, v_w_out, v_norm2_w, v_w_up, v_conv_f_w, v_conv_f_b, v_w_down, v_final_norm_w):
    weights = dict(norm1_w=norm1_w, w_in=w_in, b_gate=b_gate, conv_a_w=conv_a_w, conv_a_b=conv_a_b, dt_bias=dt_bias,
                   a_log=a_log, d_skip=d_skip, ssd_norm_w=ssd_norm_w, uv_b=uv_b, v_ln_w=v_ln_w, v_ln_b=v_ln_b,
                   w_spatial=w_spatial, b_spatial=b_spatial, w_branch=w_branch, w_out=w_out, norm2_w=norm2_w,
                   w_up=w_up, conv_f_w=conv_f_w, conv_f_b=conv_f_b, w_down=w_down, final_norm_w=final_norm_w)
    mom1 = dict(norm1_w=m_norm1_w, w_in=m_w_in, b_gate=m_b_gate, conv_a_w=m_conv_a_w, conv_a_b=m_conv_a_b,
                dt_bias=m_dt_bias, a_log=m_a_log, d_skip=m_d_skip, ssd_norm_w=m_ssd_norm_w, uv_b=m_uv_b,
                v_ln_w=m_v_ln_w, v_ln_b=m_v_ln_b, w_spatial=m_w_spatial, b_spatial=m_b_spatial, w_branch=m_w_branch,
                w_out=m_w_out, norm2_w=m_norm2_w, w_up=m_w_up, conv_f_w=m_conv_f_w, conv_f_b=m_conv_f_b,
                w_down=m_w_down, final_norm_w=m_final_norm_w)
    mom2 = dict(norm1_w=v_norm1_w, w_in=v_w_in, b_gate=v_b_gate, conv_a_w=v_conv_a_w, conv_a_b=v_conv_a_b,
                dt_bias=v_dt_bias, a_log=v_a_log, d_skip=v_d_skip, ssd_norm_w=v_ssd_norm_w, uv_b=v_uv_b,
                v_ln_w=v_v_ln_w, v_ln_b=v_v_ln_b, w_spatial=v_w_spatial, b_spatial=v_b_spatial, w_branch=v_w_branch,
                w_out=v_w_out, norm2_w=v_norm2_w, w_up=v_w_up, conv_f_w=v_conv_f_w, conv_f_b=v_conv_f_b,
                w_down=v_w_down, final_norm_w=v_final_norm_w)
    chip = 2 * lax.axis_index("x") + lax.axis_index("y")
    core = lax.axis_index("c").astype(jnp.int32).reshape(1)

    shard_shapes = {n: weights[n].shape[1:] for n in BIG}
    halves = [weights[n][0].astype(BF16).reshape(2, shard_shapes[n][0] // 2, shard_shapes[n][1]) for n in BIG]
    gathers, _ = _exchange_start(halves, [(N_CHIPS,) + h.shape for h in halves], _gather_sends, name="gather_start")
    gathers = dict(zip(BIG, gathers))

    def get_weight(name, after):
        rows, cols = shard_shapes[name]
        own, landed = _exchange_wait(gathers[name], after, _gather_sends, _gather_arrivals,
                                     name="gather_" + name + "_wait")
        others = _gather_d2d(landed, name="gather_" + name + "_d2d").reshape(N_CHIPS, rows, cols)
        blocks = lax.dynamic_update_slice(others, own.reshape(1, rows, cols), (chip, 0, 0))
        if name == "w_up":
            return {"up": blocks}
        full = _from_chip_blocks(blocks, name)
        if name == "w_branch":
            return {"branch_a": full[:SSD_D_INNER], "branch_b": full[SSD_D_INNER:]}
        if name != "w_in":
            return {name[2:]: full}
        gate0 = SSD_IN + 2 * SGU_WIDTH
        return {
            "in_z": full[:, :SSD_D_INNER],
            "in_xbc": full[:, SSD_D_INNER:SSD_D_INNER + SSD_XBC],
            "in_dt": jnp.pad(full[:, SSD_D_INNER + SSD_XBC:SSD_IN], ((0, 0), (0, LANES - SSD_HEADS))),
            "in_uv": full[:, SSD_IN:gate0],
            "in_gate": full[:, gate0:], "in_gate_a": full[:, gate0:gate0 + D_MODEL], "in_gate_b": full[:, gate0 + D_MODEL:],
        }

    whole = {}
    conv_shapes = [weights[n].shape[1:] for n in CONV]
    conv_gathered = _all_gather_chips(_flat_rows([weights[n] for n in CONV], 16), "gather_conv").reshape(N_CHIPS, -1)
    off = 0
    for n, shp in zip(CONV, conv_shapes):
        size = math.prod(shp)
        whole[n] = _from_chip_blocks(conv_gathered[:, off:off + size].reshape((N_CHIPS,) + shp), n)
        off += size

    small = {n: weights[n] for n in REPLICATED}
    small["conv_a_w"] = whole["conv_a_w"]
    small["conv_f_w"] = whole["conv_f_w"]

    reductions = {}

    def emit_grad(name, g):
        g_blocks = g if name == "w_up" else _to_chip_blocks(g, name)
        _, rows, cols = g_blocks.shape
        g_halves = g_blocks.reshape(N_CHIPS, 2, rows // 2, cols)
        arrived = _swap_halves_d2d(g_halves, name="reduce_" + name + "_swap")
        chip_sum = _add_own_half(g_halves, arrived, core, name="reduce_" + name + "_add2")
        own = lax.dynamic_slice_in_dim(chip_sum, chip, 1, axis=0)
        (pending,), started = _exchange_start([chip_sum], [chip_sum.shape], _scatter_sends,
                                              name="reduce_" + name + "_start")
        reductions[name] = (pending, own)
        return started

    loss, dx, grads_small = _local_step(x[0], loss_target[0], get_weight, small, emit_grad)

    grads = {}
    for n in ("w_down", "w_up", "w_out", "w_branch", "w_in"):
        pending, own = reductions[n]
        _, landed = _exchange_wait(pending, dx, _scatter_sends, _scatter_arrivals, name="reduce_" + n + "_wait")
        parts = lax.dynamic_update_slice(landed, own, (chip, 0, 0))
        total = _sum_chips(parts, name="reduce_" + n + "_sum4")
        grads[n] = _share_d2d(total, name="reduce_" + n + "_share").reshape(shard_shapes[n])

    small_names = REPLICATED + CONV
    small_shapes = [grads_small[n].shape for n in small_names]
    g_small = _flat_rows([grads_small[n] for n in small_names], N_CHIPS * 2 * SMALL_EXCHANGE_ROWS)
    red_small = _reduce_scatter_chips(g_small.reshape(N_CHIPS, -1, LANES), core, "reduce_small")
    all_small = _all_gather_chips(red_small, "gather_small")
    for n, g in zip(small_names, _unflatten(all_small, small_shapes)):
        if n in CONV:
            width = g.shape[1] // N_CHIPS
            g = lax.dynamic_slice_in_dim(g, chip * width, width, axis=1)
        grads[n] = g.reshape(weights[n].shape[1:]) if n != "final_norm_w" else g

    delta, new_m, new_v = {}, {}, {}
    for n in BIG:
        shp = weights[n].shape
        tr = _row_tile(shp[1], 8, 128)
        d, m1, m2 = _adamw(weights[n][0], grads[n], mom1[n][0], mom2[n][0], name="adamw_" + n, tr=tr)
        delta[n], new_m[n], new_v[n] = d.reshape(shp), m1.reshape(shp), m2.reshape(shp)
    small_all = [n for n in WEIGHT_ORDER if n not in BIG]
    shapes = [weights[n].shape for n in small_all]
    packed = [_flat_rows([src[n] for n in small_all], 8)
              for src in (weights, {n: grads[n] for n in small_all}, mom1, mom2)]
    d, m1, m2 = _adamw(*packed, name="adamw_small", tr=packed[0].shape[0])
    for n, dv, mv, vv in zip(small_all, _unflatten(d, shapes), _unflatten(m1, shapes), _unflatten(m2, shapes)):
        delta[n], new_m[n], new_v[n] = dv, mv, vv

    total_loss = lax.psum(loss[0, 0], ("x", "y", "c"))
    grad_out = [grads[n].reshape(weights[n].shape) for n in WEIGHT_ORDER]
    return (total_loss, dx[None], *grad_out, *[delta[n] for n in WEIGHT_ORDER], *[new_m[n] for n in WEIGHT_ORDER],
            *[new_v[n] for n in WEIGHT_ORDER])
```

```python
import functools
import math

import jax
import jax.numpy as jnp
from jax import lax
from jax.experimental import pallas as pl
from jax.experimental.pallas import tpu as pltpu

F32 = jnp.float32
BF16 = jnp.bfloat16
HI = lax.Precision.HIGHEST

D_MODEL = 1024
SSD_D_INNER = 2048
SSD_HEADS = 32
SSD_HEAD_DIM = 64
SSD_GROUPS = 4
SSD_HEADS_PER_GROUP = 8
SSD_STATE = 128
SSD_BC = 512
SSD_XBC = 3072
SSD_IN = 5152
SGU_WIDTH = 1024
SGU_GROUPS = 8
CHUNK = 128
IN_COLS = 9248
D_FF = 2816
NORM_EPS = 1e-6
LN_EPS = 1e-5
GROUP_COLS = SSD_HEADS_PER_GROUP * SSD_HEAD_DIM
LANES = 128

ADAM_LR = 0.001
ADAM_B1 = 0.9
ADAM_B2 = 0.999
ADAM_EPS = 1e-08
ADAM_WD = 0.01
ADAM_STEP = 10

N_CHIPS = 4
VMEM_LIMIT = 56 * 1024 * 1024

NT = (((1,), (1,)), ((), ()))
TN = (((0,), (0,)), ((), ()))
NN = (((1,), (0,)), ((), ()))


def _params(dims):
    return pltpu.CompilerParams(dimension_semantics=dims, vmem_limit_bytes=VMEM_LIMIT)


def _dot(a, b, dn=NN, precision=None):
    return lax.dot_general(a, b, dn, precision=precision, preferred_element_type=F32)


def _split3(x):
    hi = x.astype(BF16)
    rest = x - hi.astype(F32)
    mid = rest.astype(BF16)
    return hi, mid, (rest - mid.astype(F32)).astype(BF16)


def _dot_terms(terms, exact, dn=NN):
    out = None
    for t in terms:
        p = _dot(t, exact, dn)
        out = p if out is None else out + p
    return out


def _dot_exact_lhs(exact, terms):
    out = None
    for t in terms:
        p = _dot(exact, t)
        out = p if out is None else out + p
    return out


def _sigmoid(x):
    return 1.0 / (1.0 + jnp.exp(-x))


def _softplus(x):
    return jnp.maximum(x, 0.0) + jnp.log(1.0 + jnp.exp(-jnp.abs(x)))


def _matmul(pairs, *, trans_b=False, add=None, after=None, out_dtype=F32, tm=512, tn=512, name):
    def mat_shape(b):
        if isinstance(b, tuple) and b[1] == "cols":
            return (b[0].shape[1], b[0].shape[0] * b[0].shape[2])
        return b[0].shape[1:] if isinstance(b, tuple) else b.shape

    if isinstance(pairs[0][1], tuple) and pairs[0][1][1] == "cols":
        assert not trans_b and tn % LANES == 0 and pairs[0][1][0].shape[2] % tn == 0, name

    m = (pairs[0][0][0] if isinstance(pairs[0][0], tuple) else pairs[0][0]).shape[0]
    n = mat_shape(pairs[0][1])[0] if trans_b else mat_shape(pairs[0][1])[1]
    tm, tn = min(tm, m), min(tn, n)
    assert m % tm == 0 and n % tn == 0, (name, m, n, tm, tn)
    npairs = len(pairs)
    dn = NT if trans_b else NN

    def body(*refs):
        o_ref = refs[-1]
        acc = None
        for i in range(npairs):
            p = _dot(refs[2 * i][...].astype(BF16), refs[2 * i + 1][...].astype(BF16), dn)
            acc = p if acc is None else acc + p
        if add is not None:
            acc = acc + refs[2 * npairs][...]
        o_ref[...] = acc.astype(out_dtype)

    in_specs, args = [], []
    for a, b in pairs:
        bshape = mat_shape(b)
        k = bshape[1] if trans_b else bshape[0]
        assert bshape == ((n, k) if trans_b else (k, n)), (name, bshape)
        a, qa = a if isinstance(a, tuple) else (a, 0)
        assert a.shape[0] == m and a.shape[1] % k == 0, (name, a.shape, k)
        in_specs.append(pl.BlockSpec((tm, k), lambda i, j, qa=qa: (i, qa)))
        if isinstance(b, tuple) and b[1] == "cols":
            b = b[0]
            per = b.shape[2] // tn
            in_specs.append(pl.BlockSpec((None, k, tn), lambda i, j, per=per: (j // per, 0, j % per)))
        elif isinstance(b, tuple):
            b, qb = b
            if trans_b:
                in_specs.append(pl.BlockSpec((None, tn, k), lambda i, j, qb=qb: (qb, j, 0)))
            else:
                in_specs.append(pl.BlockSpec((None, k, tn), lambda i, j, qb=qb: (qb, 0, j)))
        elif trans_b:
            in_specs.append(pl.BlockSpec((tn, k), lambda i, j: (j, 0)))
        else:
            in_specs.append(pl.BlockSpec((k, tn), lambda i, j: (0, j)))
        args += [a, b]
    if add is not None:
        in_specs.append(pl.BlockSpec((tm, tn), lambda i, j: (i, j)))
        args.append(add)
    if after is not None:
        in_specs.append(pl.BlockSpec(memory_space=pl.ANY))
        args.append(after)
    return pl.pallas_call(
        body, name=name, grid=(m // tm, n // tn), in_specs=in_specs,
        out_specs=pl.BlockSpec((tm, tn), lambda i, j: (i, j)),
        out_shape=jax.ShapeDtypeStruct((m, n), out_dtype),
        compiler_params=_params(("parallel", "parallel")),
    )(*args)


def _matmul_tn(a, b, *, tk, tn, tm=1024, out_dtype=BF16, stack_out=False, after=None, name):
    m, k = a.shape
    n = b.shape[1]
    tm, tk, tn = min(tm, m), min(tk, k), min(tn, n)
    assert m % tm == 0 and k % tk == 0 and n % tn == 0, (name, m, k, n)
    nm = m // tm
    if stack_out:
        out_spec = pl.BlockSpec((None, tk, tn), lambda i, j, l: (j, i, 0))
        out_shape = jax.ShapeDtypeStruct((n // tn, k, tn), out_dtype)
    else:
        out_spec = pl.BlockSpec((tk, tn), lambda i, j, l: (i, j))
        out_shape = jax.ShapeDtypeStruct((k, n), out_dtype)

    def body(a_ref, b_ref, *rest):
        o_ref, acc = rest[-2:]
        mi = pl.program_id(2)

        @pl.when(mi == 0)
        def _():
            acc[...] = jnp.zeros_like(acc)

        acc[...] += _dot(a_ref[...].astype(BF16), b_ref[...].astype(BF16), TN)

        @pl.when(mi == nm - 1)
        def _():
            o_ref[...] = acc[...].astype(out_dtype)

    in_specs = [pl.BlockSpec((tm, tk), lambda i, j, l: (l, i)), pl.BlockSpec((tm, tn), lambda i, j, l: (l, j))]
    args = [a, b]
    if after is not None:
        in_specs.append(pl.BlockSpec(memory_space=pl.ANY))
        args.append(after)
    return pl.pallas_call(
        body, name=name, grid=(k // tk, n // tn, nm), in_specs=in_specs,
        out_specs=out_spec, out_shape=out_shape,
        scratch_shapes=[pltpu.VMEM((tk, tn), F32)],
        compiler_params=_params(("parallel", "parallel", "arbitrary")),
    )(*args)


def _rms_fwd(x, w, *, name, tm=512):
    s, d = x.shape
    tm = min(tm, s)

    def body(x_ref, w_ref, o_ref):
        xv = x_ref[...]
        r = lax.rsqrt(jnp.mean(xv * xv, axis=-1, keepdims=True) + NORM_EPS)
        o_ref[...] = (xv * r * w_ref[...]).astype(BF16)

    return pl.pallas_call(
        body, name=name, grid=(s // tm,),
        in_specs=[pl.BlockSpec((tm, d), lambda i: (i, 0)), pl.BlockSpec((1, d), lambda i: (0, 0))],
        out_specs=pl.BlockSpec((tm, d), lambda i: (i, 0)),
        out_shape=jax.ShapeDtypeStruct((s, d), BF16),
        compiler_params=_params(("parallel",)),
    )(x, w)


def _rms_bwd(x, w, dn, dres, *, name, tm=512):
    s, d = x.shape
    tm = min(tm, s)

    def body(x_ref, w_ref, dn_ref, dres_ref, dx_ref, dxb_ref, dw_ref):
        @pl.when(pl.program_id(0) == 0)
        def _():
            dw_ref[...] = jnp.zeros_like(dw_ref)

        xv = x_ref[...]
        r = lax.rsqrt(jnp.mean(xv * xv, axis=-1, keepdims=True) + NORM_EPS)
        xhat = xv * r
        dnv = dn_ref[...]
        dxhat = dnv * w_ref[...]
        dx = dres_ref[...] + r * (dxhat - xhat * jnp.mean(dxhat * xhat, axis=-1, keepdims=True))
        dx_ref[...] = dx
        dxb_ref[...] = dx.astype(BF16)
        dw_ref[...] += jnp.sum(dnv * xhat, axis=0, keepdims=True)

    tile = pl.BlockSpec((tm, d), lambda i: (i, 0))
    row = pl.BlockSpec((1, d), lambda i: (0, 0))
    return pl.pallas_call(
        body, name=name, grid=(s // tm,),
        in_specs=[tile, row, tile, tile], out_specs=[tile, tile, row],
        out_shape=[jax.ShapeDtypeStruct((s, d), F32), jax.ShapeDtypeStruct((s, d), BF16),
                   jax.ShapeDtypeStruct((1, d), F32)],
        compiler_params=_params(("arbitrary",)),
    )(x, w, dn, dres)


def _final_fwd_bwd(h2, wf, target, *, name, tm=512):
    s, d = h2.shape
    tm = min(tm, s)

    def body(h_ref, w_ref, t_ref, loss_ref, dh_ref, dhb_ref, dw_ref):
        @pl.when(pl.program_id(0) == 0)
        def _():
            dw_ref[...] = jnp.zeros_like(dw_ref)
            loss_ref[...] = jnp.zeros_like(loss_ref)

        hv = h_ref[...]
        r = lax.rsqrt(jnp.mean(hv * hv, axis=-1, keepdims=True) + NORM_EPS)
        xhat = hv * r
        err = xhat * w_ref[...] - t_ref[...]
        per_tok = jnp.mean(err * err, axis=-1, keepdims=True)
        loss_ref[...] += 0.5 * jnp.sum(per_tok, axis=0, keepdims=True)
        dy = err * (1.0 / d)
        dxhat = dy * w_ref[...]
        dh = r * (dxhat - xhat * jnp.mean(dxhat * xhat, axis=-1, keepdims=True))
        dh_ref[...] = dh
        dhb_ref[...] = dh.astype(BF16)
        dw_ref[...] += jnp.sum(dy * xhat, axis=0, keepdims=True)

    tile = pl.BlockSpec((tm, d), lambda i: (i, 0))
    row = pl.BlockSpec((1, d), lambda i: (0, 0))
    return pl.pallas_call(
        body, name=name, grid=(s // tm,),
        in_specs=[tile, row, tile],
        out_specs=[pl.BlockSpec((1, 1), lambda i: (0, 0)), tile, tile, row],
        out_shape=[jax.ShapeDtypeStruct((1, 1), F32), jax.ShapeDtypeStruct((s, d), F32),
                   jax.ShapeDtypeStruct((s, d), BF16), jax.ShapeDtypeStruct((1, d), F32)],
        compiler_params=_params(("arbitrary",)),
    )(h2, wf, target)


CONV_ROWS = 512
HALO = 8


def _rows_with_halo(ref, r0, rows, s, before, after):
    parts = []
    if before:
        prev = ref[pl.ds(pl.multiple_of(jnp.maximum(r0 - HALO, 0), HALO), HALO), :]
        parts.append(jnp.where(r0 > 0, prev, 0.0))
    parts.append(ref[pl.ds(r0, rows), :])
    if after:
        nxt = ref[pl.ds(pl.multiple_of(jnp.minimum(r0 + rows, s - HALO), HALO), HALO), :]
        parts.append(jnp.where(r0 + rows < s, nxt, 0.0))
    return jnp.concatenate(parts, axis=0) if len(parts) > 1 else parts[0]


def _earlier(xe, k, rows):
    if k == 0:
        return xe[HALO:HALO + rows]
    return pltpu.roll(xe, k, 0)[HALO:HALO + rows]


def _later(ve, k, rows):
    if k == 0:
        return ve[:rows]
    return pltpu.roll(ve, ve.shape[0] - k, 0)[:rows]


def _conv_taps(xe, w_ref, kk, rows):
    acc = None
    for i in range(kk):
        term = w_ref[i:i + 1, :] * _earlier(xe, kk - 1 - i, rows)
        acc = term if acc is None else acc + term
    return acc


def _row_loop(s, step):
    def body(r, carry):
        return step(pl.multiple_of(r * CONV_ROWS, CONV_ROWS), carry)
    return body


def _conv_bwd_rows(xe, dpe, w_ref, kk):
    dp = dpe[:CONV_ROWS]
    dx = None
    dws = []
    for i in range(kk):
        dws.append(jnp.sum(dp * _earlier(xe, kk - 1 - i, CONV_ROWS), axis=0, keepdims=True))
        term = w_ref[i:i + 1, :] * _later(dpe, kk - 1 - i, CONV_ROWS)
        dx = term if dx is None else dx + term
    return dx, dws, jnp.sum(dp, axis=0, keepdims=True)


def _conv_a_fwd(xraw, w, b, *, name, tc=128):
    s, c = xraw.shape
    kk = 4

    def body(x_ref, w_ref, b_ref, o_ref):
        def step(r0, carry):
            xe = _rows_with_halo(x_ref, r0, CONV_ROWS, s, True, False)
            pre = _conv_taps(xe, w_ref, kk, CONV_ROWS) + b_ref[...]
            o_ref[pl.ds(r0, CONV_ROWS), :] = pre * _sigmoid(pre)
            return carry

        lax.fori_loop(0, s // CONV_ROWS, _row_loop(s, step), 0)

    col = pl.BlockSpec((s, tc), lambda j: (0, j))
    return pl.pallas_call(
        body, name=name, grid=(c // tc,),
        in_specs=[col, pl.BlockSpec((8, tc), lambda j: (0, j)), pl.BlockSpec((1, tc), lambda j: (0, j))],
        out_specs=col, out_shape=jax.ShapeDtypeStruct((s, c), F32),
        compiler_params=_params(("parallel",)),
    )(xraw, w, b)


def _conv_a_bwd(xraw, w, b, dy, *, name, tc=128):
    s, c = xraw.shape
    kk = 4

    def body(x_ref, w_ref, b_ref, dy_ref, dx_ref, dw_ref, db_ref):
        def step(r0, carry):
            xe = _rows_with_halo(x_ref, r0, CONV_ROWS, s, True, True)
            pre = _conv_taps(xe, w_ref, kk, CONV_ROWS + HALO) + b_ref[...]
            sg = _sigmoid(pre)
            dpe = _rows_with_halo(dy_ref, r0, CONV_ROWS, s, False, True) * (sg * (1.0 + pre * (1.0 - sg)))
            dx, dws, db = _conv_bwd_rows(xe, dpe, w_ref, kk)
            dx_ref[pl.ds(r0, CONV_ROWS), :] = dx.astype(BF16)
            return tuple(acc + new for acc, new in zip(carry, dws + [db]))

        zero = jnp.zeros((1, tc), F32)
        sums = lax.fori_loop(0, s // CONV_ROWS, _row_loop(s, step), (zero,) * (kk + 1))
        db_ref[...] = sums[kk]
        dw_ref[...] = jnp.concatenate(list(sums[:kk]) + [jnp.zeros((8 - kk, tc), F32)], axis=0)

    col = pl.BlockSpec((s, tc), lambda j: (0, j))
    w8 = pl.BlockSpec((8, tc), lambda j: (0, j))
    row = pl.BlockSpec((1, tc), lambda j: (0, j))
    return pl.pallas_call(
        body, name=name, grid=(c // tc,),
        in_specs=[col, w8, row, col], out_specs=[col, w8, row],
        out_shape=[jax.ShapeDtypeStruct((s, c), BF16), jax.ShapeDtypeStruct((8, c), F32),
                   jax.ShapeDtypeStruct((1, c), F32)],
        compiler_params=_params(("parallel",)),
    )(xraw, w, b, dy)


def _conv_f_fwd(up_raw, w, b, *, name, tc=128):
    s, c2 = up_raw.shape
    c = c2 // 2
    nb = c // tc
    kk = 3

    def body(xa_ref, xv_ref, wa_ref, wv_ref, ba_ref, bv_ref, o_ref):
        def step(r0, carry):
            a = _conv_taps(_rows_with_halo(xa_ref, r0, CONV_ROWS, s, True, False), wa_ref, kk, CONV_ROWS) + ba_ref[...]
            v = _conv_taps(_rows_with_halo(xv_ref, r0, CONV_ROWS, s, True, False), wv_ref, kk, CONV_ROWS) + bv_ref[...]
            o_ref[pl.ds(r0, CONV_ROWS), :] = (a * _sigmoid(a) * v).astype(BF16)
            return carry

        lax.fori_loop(0, s // CONV_ROWS, _row_loop(s, step), 0)

    col_a = pl.BlockSpec((s, tc), lambda j: (0, j))
    col_v = pl.BlockSpec((s, tc), lambda j: (0, j + nb))
    return pl.pallas_call(
        body, name=name, grid=(nb,),
        in_specs=[col_a, col_v, pl.BlockSpec((8, tc), lambda j: (0, j)), pl.BlockSpec((8, tc), lambda j: (0, j + nb)),
                  pl.BlockSpec((1, tc), lambda j: (0, j)), pl.BlockSpec((1, tc), lambda j: (0, j + nb))],
        out_specs=col_a, out_shape=jax.ShapeDtypeStruct((s, c), BF16),
        compiler_params=_params(("parallel",)),
    )(up_raw, up_raw, w, w, b, b)


def _conv_f_bwd(up_raw, w, b, dact, *, name, tc=128):
    s, c2 = up_raw.shape
    c = c2 // 2
    nb = c // tc
    kk = 3

    def body(xa_ref, xv_ref, wa_ref, wv_ref, ba_ref, bv_ref, d_ref,
             dxa_ref, dxv_ref, dwa_ref, dwv_ref, dba_ref, dbv_ref):
        def step(r0, carry):
            xae = _rows_with_halo(xa_ref, r0, CONV_ROWS, s, True, True)
            xve = _rows_with_halo(xv_ref, r0, CONV_ROWS, s, True, True)
            a = _conv_taps(xae, wa_ref, kk, CONV_ROWS + HALO) + ba_ref[...]
            v = _conv_taps(xve, wv_ref, kk, CONV_ROWS + HALO) + bv_ref[...]
            sg = _sigmoid(a)
            d = _rows_with_halo(d_ref, r0, CONV_ROWS, s, False, True)
            dxa, dwas, dba = _conv_bwd_rows(xae, d * v * (sg * (1.0 + a * (1.0 - sg))), wa_ref, kk)
            dxv, dwvs, dbv = _conv_bwd_rows(xve, d * (a * sg), wv_ref, kk)
            dxa_ref[pl.ds(r0, CONV_ROWS), :] = dxa.astype(BF16)
            dxv_ref[pl.ds(r0, CONV_ROWS), :] = dxv.astype(BF16)
            return tuple(acc + new for acc, new in zip(carry, dwas + [dba] + dwvs + [dbv]))

        zero = jnp.zeros((1, tc), F32)
        sums = lax.fori_loop(0, s // CONV_ROWS, _row_loop(s, step), (zero,) * (2 * kk + 2))
        pad = [jnp.zeros((8 - kk, tc), F32)]
        dwa_ref[...] = jnp.concatenate(list(sums[:kk]) + pad, axis=0)
        dba_ref[...] = sums[kk]
        dwv_ref[...] = jnp.concatenate(list(sums[kk + 1:2 * kk + 1]) + pad, axis=0)
        dbv_ref[...] = sums[2 * kk + 1]

    col_a = pl.BlockSpec((s, tc), lambda j: (0, j))
    col_v = pl.BlockSpec((s, tc), lambda j: (0, j + nb))
    w_a = pl.BlockSpec((8, tc), lambda j: (0, j))
    w_v = pl.BlockSpec((8, tc), lambda j: (0, j + nb))
    r_a = pl.BlockSpec((1, tc), lambda j: (0, j))
    r_v = pl.BlockSpec((1, tc), lambda j: (0, j + nb))
    outs = pl.pallas_call(
        body, name=name, grid=(nb,),
        in_specs=[col_a, col_v, w_a, w_v, r_a, r_v, col_a],
        out_specs=[col_a, col_a, w_a, w_a, r_a, r_a],
        out_shape=[jax.ShapeDtypeStruct((s, c), BF16), jax.ShapeDtypeStruct((s, c), BF16),
                   jax.ShapeDtypeStruct((8, c), F32), jax.ShapeDtypeStruct((8, c), F32),
                   jax.ShapeDtypeStruct((1, c), F32), jax.ShapeDtypeStruct((1, c), F32)],
        compiler_params=_params(("parallel",)),
    )(up_raw, up_raw, w, w, b, b, dact)
    return outs


def _tri_masks():
    row = lax.broadcasted_iota(jnp.int32, (CHUNK, CHUNK), 0)
    col = lax.broadcasted_iota(jnp.int32, (CHUNK, CHUNK), 1)
    return row >= col, row <= col


def _ssd_fwd(xbc, dt_raw, z, dt_bias, a_log, a_log_x, d_skip_x, norm_w, expand, *, name):
    s = xbc.shape[0]
    nc = s // CHUNK

    def body(xbc_ref, dtr_ref, z_ref, dtb_ref, alog_ref, alogx_ref, dskx_ref, nw_ref, e_ref,
             y_ref, ya_ref, st_ref, state):
        @pl.when(pl.program_id(0) == 0)
        def _():
            state[...] = jnp.zeros_like(state)

        st_ref[0] = state[...]
        lower, _ = _tri_masks()
        dt = _softplus(dtr_ref[...] + dtb_ref[...])
        adt = dt * (-jnp.exp(alog_ref[...]))
        acum = _dot_exact_lhs(lower.astype(BF16), _split3(adt))
        acum_t = acum.T
        dt_terms, acum_terms = _split3(dt), _split3(acum)
        for g in range(SSD_GROUPS):
            sl = slice(GROUP_COLS * g, GROUP_COLS * (g + 1))
            dt_x = _dot_terms(dt_terms, e_ref[:, sl])
            acum_x = _dot_terms(acum_terms, e_ref[:, sl])
            tot_x = jnp.sum(dt_x * (-jnp.exp(alogx_ref[:, sl])), axis=0, keepdims=True)
            xs = xbc_ref[:, sl]
            xdt = xs * dt_x
            xdt_b = xdt.astype(BF16)
            bg = xbc_ref[:, SSD_D_INNER + SSD_STATE * g:SSD_D_INNER + SSD_STATE * (g + 1)].astype(BF16)
            cg = xbc_ref[:, SSD_D_INNER + SSD_BC + SSD_STATE * g:SSD_D_INNER + SSD_BC + SSD_STATE * (g + 1)].astype(BF16)
            cb = _dot(cg, bg, NT)
            st_g = state[:, sl]
            y_off = _dot(cg, st_g.astype(BF16)) * jnp.exp(acum_x)
            parts = []
            for r in range(SSD_HEADS_PER_GROUP):
                h = SSD_HEADS_PER_GROUP * g + r
                dec = jnp.exp(jnp.where(lower, acum[:, h:h + 1] - acum_t[h:h + 1, :], -jnp.inf))
                parts.append(_dot((cb * dec).astype(BF16), xdt_b[:, SSD_HEAD_DIM * r:SSD_HEAD_DIM * (r + 1)]))
            y_ref[:, sl] = jnp.concatenate(parts, axis=1) + y_off + dskx_ref[:, sl] * xs
            wgt = (xdt * jnp.exp(tot_x - acum_x)).astype(BF16)
            state[:, sl] = st_g * jnp.exp(tot_x) + _dot(bg, wgt, TN)
        zv = z_ref[...]
        q = y_ref[...] * (zv * _sigmoid(zv))
        r = lax.rsqrt(jnp.mean(q * q, axis=-1, keepdims=True) + NORM_EPS)
        ya_ref[...] = (q * r * nw_ref[...]).astype(BF16)

    def chunk(w):
        return pl.BlockSpec((CHUNK, w), lambda c: (c, 0))

    def const(shape):
        return pl.BlockSpec(shape, lambda c: (0,) * len(shape))

    return pl.pallas_call(
        body, name=name, grid=(nc,),
        in_specs=[chunk(SSD_XBC), chunk(LANES), chunk(SSD_D_INNER), const((1, LANES)), const((1, LANES)),
                  const((1, SSD_D_INNER)), const((1, SSD_D_INNER)), const((1, SSD_D_INNER)),
                  const((LANES, SSD_D_INNER))],
        out_specs=[chunk(SSD_D_INNER), chunk(SSD_D_INNER),
                   pl.BlockSpec((1, SSD_STATE, SSD_D_INNER), lambda c: (c, 0, 0))],
        out_shape=[jax.ShapeDtypeStruct((s, SSD_D_INNER), F32), jax.ShapeDtypeStruct((s, SSD_D_INNER), BF16),
                   jax.ShapeDtypeStruct((nc, SSD_STATE, SSD_D_INNER), F32)],
        scratch_shapes=[pltpu.VMEM((SSD_STATE, SSD_D_INNER), F32)],
        compiler_params=_params(("arbitrary",)),
    )(xbc, dt_raw, z, dt_bias, a_log, a_log_x, d_skip_x, norm_w, expand)


def _ssd_bwd(dya, y, z, xbc, dt_raw, states, dt_bias, a_log, a_log_x, d_skip_x, norm_w, expand, expand_t, *, name):
    s = xbc.shape[0]
    nc = s // CHUNK

    def body(dya_ref, y_ref, z_ref, xbc_ref, dtr_ref, stp_ref, dtb_ref, alog_ref, alogx_ref, dskx_ref, nw_ref,
             e_ref, et_ref, dz_ref, dxbc_ref, ddt_ref, dnw_ref, ddsk_ref, dalog_ref, ddtb_ref,
             dstate, dy_sc, dskcol):
        i = pl.program_id(0)

        @pl.when(i == 0)
        def _():
            dstate[...] = jnp.zeros_like(dstate)
            dskcol[...] = jnp.zeros_like(dskcol)
            dnw_ref[...] = jnp.zeros_like(dnw_ref)
            dalog_ref[...] = jnp.zeros_like(dalog_ref)
            ddtb_ref[...] = jnp.zeros_like(ddtb_ref)
            ddsk_ref[...] = jnp.zeros_like(ddsk_ref)

        lower, upper = _tri_masks()
        rows = lax.broadcasted_iota(jnp.int32, (CHUNK, LANES), 0)
        pre = dtr_ref[...] + dtb_ref[...]
        dt = _softplus(pre)
        a = -jnp.exp(alog_ref[...])
        acum = _dot_exact_lhs(lower.astype(BF16), _split3(dt * a))
        acum_t = acum.T
        dt_terms, acum_terms = _split3(dt), _split3(acum)

        yv = y_ref[...]
        zv = z_ref[...]
        sz = _sigmoid(zv)
        silu_z = zv * sz
        q = yv * silu_z
        r = lax.rsqrt(jnp.mean(q * q, axis=-1, keepdims=True) + NORM_EPS)
        qhat = q * r
        dyav = dya_ref[...]
        dqhat = dyav * nw_ref[...]
        dnw_ref[...] += jnp.sum(dyav * qhat, axis=0, keepdims=True)
        dq = r * (dqhat - qhat * jnp.mean(dqhat * qhat, axis=-1, keepdims=True))
        dy_sc[...] = dq * silu_z
        dz_ref[...] = (dq * yv * (sz * (1.0 + zv * (1.0 - sz)))).astype(BF16)

        da_cum = jnp.zeros((CHUNK, LANES), F32)
        ddt = jnp.zeros((CHUNK, LANES), F32)
        for g in range(SSD_GROUPS):
            sl = slice(GROUP_COLS * g, GROUP_COLS * (g + 1))
            et_g = et_ref[sl, :]
            dt_x = _dot_terms(dt_terms, e_ref[:, sl])
            acum_x = _dot_terms(acum_terms, e_ref[:, sl])
            tot_x = jnp.sum(dt_x * (-jnp.exp(alogx_ref[:, sl])), axis=0, keepdims=True)
            e_tot = jnp.exp(tot_x)
            dec_s = jnp.exp(tot_x - acum_x)
            xs = xbc_ref[:, sl]
            xdt = xs * dt_x
            xdt_b = xdt.astype(BF16)
            dy = dy_sc[:, sl]
            dy_b = dy.astype(BF16)
            dskx = dskx_ref[:, sl]
            y_ssd = y_ref[:, sl] - dskx * xs
            dskcol[:, sl] += jnp.sum(dy * xs, axis=0, keepdims=True)
            bg = xbc_ref[:, SSD_D_INNER + SSD_STATE * g:SSD_D_INNER + SSD_STATE * (g + 1)].astype(BF16)
            cg = xbc_ref[:, SSD_D_INNER + SSD_BC + SSD_STATE * g:SSD_D_INNER + SSD_BC + SSD_STATE * (g + 1)].astype(BF16)
            cb = _dot(cg, bg, NT)
            sp = stp_ref[0, :, sl]
            ds_g = dstate[:, sl]
            ds_b = ds_g.astype(BF16)
            dye_b = (dy * jnp.exp(acum_x)).astype(BF16)
            dc = _dot(dye_b, sp.astype(BF16), NT)
            dxdt_state = dec_s * _dot(bg, ds_b)
            db = _dot((xdt * dec_s).astype(BF16), ds_b, NT)
            dcb = jnp.zeros((CHUNK, CHUNK), F32)
            parts = []
            for rr in range(SSD_HEADS_PER_GROUP):
                h = SSD_HEADS_PER_GROUP * g + rr
                hs = slice(SSD_HEAD_DIM * rr, SSD_HEAD_DIM * (rr + 1))
                dec = jnp.exp(jnp.where(lower, acum[:, h:h + 1] - acum_t[h:h + 1, :], -jnp.inf))
                parts.append(_dot((cb * dec).astype(BF16), dy_b[:, hs], TN))
                dcb = dcb + _dot(dy_b[:, hs], xdt_b[:, hs], NT) * dec
            dxdt = jnp.concatenate(parts, axis=1) + dxdt_state
            dcb_b = dcb.astype(BF16)
            dc = dc + _dot(dcb_b, bg)
            db = db + _dot(dcb_b, cg, TN)
            tot_col = jnp.sum(ds_g * sp, axis=0, keepdims=True) * e_tot + jnp.sum(dxdt_state * xdt, axis=0, keepdims=True)
            d_tot = _dot_terms(_split3(jnp.broadcast_to(tot_col, (8, GROUP_COLS))), et_g)
            d_tot = jnp.max(d_tot, axis=0, keepdims=True)
            pair_sums = dy_b.astype(F32) * y_ssd - xdt_b.astype(F32) * dxdt
            da_cum = da_cum + _dot_terms(_split3(pair_sums), et_g) + jnp.where(rows == CHUNK - 1, d_tot, 0.0)
            ddt = ddt + _dot_terms(_split3(dxdt * xs), et_g)
            dxbc_ref[:, sl] = dy * dskx + dxdt * dt_x
            dxbc_ref[:, SSD_D_INNER + SSD_STATE * g:SSD_D_INNER + SSD_STATE * (g + 1)] = db
            dxbc_ref[:, SSD_D_INNER + SSD_BC + SSD_STATE * g:SSD_D_INNER + SSD_BC + SSD_STATE * (g + 1)] = dc
            dstate[:, sl] = e_tot * ds_g + _dot(cg, dye_b, TN)

        dadt = _dot_exact_lhs(upper.astype(BF16), _split3(da_cum))
        ddt = ddt + dadt * a
        dalog_ref[...] += jnp.sum(dadt * dt, axis=0, keepdims=True)
        dpre = ddt * _sigmoid(pre)
        ddtb_ref[...] += jnp.sum(dpre, axis=0, keepdims=True)
        ddt_ref[...] = dpre.astype(BF16)

        @pl.when(i == nc - 1)
        def _():
            dalog_ref[...] = dalog_ref[...] * a
            dsk = _dot_terms(_split3(jnp.broadcast_to(dskcol[...], (8, SSD_D_INNER))), et_ref[...])
            ddsk_ref[...] = jnp.max(dsk, axis=0, keepdims=True)

    def chunk(w):
        return pl.BlockSpec((CHUNK, w), lambda i: (nc - 1 - i, 0))

    def const(shape):
        return pl.BlockSpec(shape, lambda i: (0,) * len(shape))

    return pl.pallas_call(
        body, name=name, grid=(nc,),
        in_specs=[chunk(SSD_D_INNER), chunk(SSD_D_INNER), chunk(SSD_D_INNER), chunk(SSD_XBC), chunk(LANES),
                  pl.BlockSpec((1, SSD_STATE, SSD_D_INNER), lambda i: (nc - 1 - i, 0, 0)),
                  const((1, LANES)), const((1, LANES)), const((1, SSD_D_INNER)), const((1, SSD_D_INNER)),
                  const((1, SSD_D_INNER)), const((LANES, SSD_D_INNER)), const((SSD_D_INNER, LANES))],
        out_specs=[chunk(SSD_D_INNER), chunk(SSD_XBC), chunk(LANES), const((1, SSD_D_INNER)), const((1, LANES)),
                   const((1, LANES)), const((1, LANES))],
        out_shape=[jax.ShapeDtypeStruct((s, SSD_D_INNER), BF16), jax.ShapeDtypeStruct((s, SSD_XBC), F32),
                   jax.ShapeDtypeStruct((s, LANES), BF16), jax.ShapeDtypeStruct((1, SSD_D_INNER), F32),
                   jax.ShapeDtypeStruct((1, LANES), F32), jax.ShapeDtypeStruct((1, LANES), F32),
                   jax.ShapeDtypeStruct((1, LANES), F32)],
        scratch_shapes=[pltpu.VMEM((SSD_STATE, SSD_D_INNER), F32), pltpu.VMEM((CHUNK, SSD_D_INNER), F32),
                        pltpu.VMEM((1, SSD_D_INNER), F32)],
        compiler_params=_params(("arbitrary",)),
    )(dya, y, z, xbc, dt_raw, states, dt_bias, a_log, a_log_x, d_skip_x, norm_w, expand, expand_t)


GELU_K = math.sqrt(2.0 / math.pi)
GELU_C = 0.044715


def _gelu(x):
    return 0.5 * x * (1.0 + jnp.tanh(GELU_K * (x + GELU_C * x * x * x)))


def _gelu_grad(x):
    t = jnp.tanh(GELU_K * (x + GELU_C * x * x * x))
    return 0.5 * (1.0 + t) + 0.5 * x * (1.0 - t * t) * (GELU_K * (1.0 + 3.0 * GELU_C * x * x))


def _sgu_pre(uv_ref, uvb_ref, lnw_ref, lnb_ref):
    uv = uv_ref[...] + uvb_ref[...]
    guv = _gelu(uv)
    u = guv[:, :SGU_WIDTH]
    v = guv[:, SGU_WIDTH:]
    mu = jnp.mean(v, axis=-1, keepdims=True)
    vc = v - mu
    rstd = lax.rsqrt(jnp.mean(vc * vc, axis=-1, keepdims=True) + LN_EPS)
    vhat = vc * rstd
    vn = vhat * lnw_ref[...] + lnb_ref[...]
    return uv, u, vhat, rstd, vn


def _sgu_fwd(uv_raw, uv_b, ln_w, ln_b, w_sp, b_sp_t, *, name):
    s = uv_raw.shape[0]
    nc = s // CHUNK

    def body(uv_ref, uvb_ref, lnw_ref, lnb_ref, w_ref, bt_ref, o_ref):
        lower, _ = _tri_masks()
        _, u, _, _, vn = _sgu_pre(uv_ref, uvb_ref, lnw_ref, lnb_ref)
        vn_b = vn.astype(BF16)
        bt = bt_ref[...]
        for g in range(SGU_GROUPS):
            gs = slice(LANES * g, LANES * (g + 1))
            wc = jnp.where(lower, w_ref[g], 0.0).astype(BF16)
            mixed = _dot(wc, vn_b[:, gs]) + bt[:, g:g + 1]
            o_ref[:, gs] = (u[:, gs] * mixed).astype(BF16)

    def const(shape):
        return pl.BlockSpec(shape, lambda c: (0,) * len(shape))

    return pl.pallas_call(
        body, name=name, grid=(nc,),
        in_specs=[pl.BlockSpec((CHUNK, 2 * SGU_WIDTH), lambda c: (c, 0)), const((1, 2 * SGU_WIDTH)),
                  const((1, SGU_WIDTH)), const((1, SGU_WIDTH)), const((SGU_GROUPS, CHUNK, CHUNK)),
                  const((CHUNK, LANES))],
        out_specs=pl.BlockSpec((CHUNK, SGU_WIDTH), lambda c: (c, 0)),
        out_shape=jax.ShapeDtypeStruct((s, SGU_WIDTH), BF16),
        compiler_params=_params(("parallel",)),
    )(uv_raw, uv_b, ln_w, ln_b, w_sp, b_sp_t)


def _sgu_bwd(uv_raw, dyb, uv_b, ln_w, ln_b, w_sp, b_sp_t, group_sum, *, name):
    s = uv_raw.shape[0]
    nc = s // CHUNK

    def body(uv_ref, dy_ref, uvb_ref, lnw_ref, lnb_ref, w_ref, bt_ref, gsum_ref,
             duv_ref, dw_ref, dbt_ref, dlnw_ref, dlnb_ref, duvb_ref):
        @pl.when(pl.program_id(0) == 0)
        def _():
            dw_ref[...] = jnp.zeros_like(dw_ref)
            dbt_ref[...] = jnp.zeros_like(dbt_ref)
            dlnw_ref[...] = jnp.zeros_like(dlnw_ref)
            dlnb_ref[...] = jnp.zeros_like(dlnb_ref)
            duvb_ref[...] = jnp.zeros_like(duvb_ref)

        lower, _ = _tri_masks()
        uv, u, vhat, rstd, vn = _sgu_pre(uv_ref, uvb_ref, lnw_ref, lnb_ref)
        vn_b = vn.astype(BF16)
        bt = bt_ref[...]
        dy = dy_ref[...]
        du_parts, dvn_parts, dmix_parts = [], [], []
        for g in range(SGU_GROUPS):
            gs = slice(LANES * g, LANES * (g + 1))
            wc = jnp.where(lower, w_ref[g], 0.0).astype(BF16)
            mixed = _dot(wc, vn_b[:, gs]) + bt[:, g:g + 1]
            du_parts.append(dy[:, gs] * mixed)
            dmix = dy[:, gs] * u[:, gs]
            dmix_b = dmix.astype(BF16)
            dmix_parts.append(dmix)
            dw_ref[g] += jnp.where(lower, _dot(dmix_b, vn_b[:, gs], NT), 0.0)
            dvn_parts.append(_dot(wc, dmix_b, TN))
        dmixed = jnp.concatenate(dmix_parts, axis=1)
        dbt_ref[...] += _dot_terms(_split3(dmixed), gsum_ref[...])
        dvn = jnp.concatenate(dvn_parts, axis=1)
        dlnw_ref[...] += jnp.sum(dvn * vhat, axis=0, keepdims=True)
        dlnb_ref[...] += jnp.sum(dvn, axis=0, keepdims=True)
        dvhat = dvn * lnw_ref[...]
        dv = rstd * (dvhat - jnp.mean(dvhat, axis=-1, keepdims=True)
                     - vhat * jnp.mean(dvhat * vhat, axis=-1, keepdims=True))
        dguv = jnp.concatenate(du_parts + [dv], axis=1)
        duv = dguv * _gelu_grad(uv)
        duvb_ref[...] += jnp.sum(duv, axis=0, keepdims=True)
        duv_ref[...] = duv.astype(BF16)

    def const(shape):
        return pl.BlockSpec(shape, lambda c: (0,) * len(shape))

    return pl.pallas_call(
        body, name=name, grid=(nc,),
        in_specs=[pl.BlockSpec((CHUNK, 2 * SGU_WIDTH), lambda c: (c, 0)),
                  pl.BlockSpec((CHUNK, SGU_WIDTH), lambda c: (c, 0)), const((1, 2 * SGU_WIDTH)),
                  const((1, SGU_WIDTH)), const((1, SGU_WIDTH)), const((SGU_GROUPS, CHUNK, CHUNK)),
                  const((CHUNK, LANES)), const((SGU_WIDTH, LANES))],
        out_specs=[pl.BlockSpec((CHUNK, 2 * SGU_WIDTH), lambda c: (c, 0)), const((SGU_GROUPS, CHUNK, CHUNK)),
                   const((CHUNK, LANES)), const((1, SGU_WIDTH)), const((1, SGU_WIDTH)), const((1, 2 * SGU_WIDTH))],
        out_shape=[jax.ShapeDtypeStruct((s, 2 * SGU_WIDTH), BF16),
                   jax.ShapeDtypeStruct((SGU_GROUPS, CHUNK, CHUNK), F32), jax.ShapeDtypeStruct((CHUNK, LANES), F32),
                   jax.ShapeDtypeStruct((1, SGU_WIDTH), F32), jax.ShapeDtypeStruct((1, SGU_WIDTH), F32),
                   jax.ShapeDtypeStruct((1, 2 * SGU_WIDTH), F32)],
        compiler_params=_params(("arbitrary",)),
    )(uv_raw, dyb, uv_b, ln_w, ln_b, w_sp, b_sp_t, group_sum)


def _gate_fwd(gates_raw, b_gate, p_a, p_b, *, name, tm=512):
    s = p_a.shape[0]
    tm = min(tm, s)

    def body(ga_ref, gb_ref, ba_ref, bb_ref, pa_ref, pb_ref, o_ref):
        ga = _sigmoid(ga_ref[...] + ba_ref[...])
        gb = _sigmoid(gb_ref[...] + bb_ref[...])
        o_ref[...] = (ga * pa_ref[...] + gb * pb_ref[...]).astype(BF16)

    t_a = pl.BlockSpec((tm, D_MODEL), lambda i: (i, 0))
    t_b = pl.BlockSpec((tm, D_MODEL), lambda i: (i, 1))
    r_a = pl.BlockSpec((1, D_MODEL), lambda i: (0, 0))
    r_b = pl.BlockSpec((1, D_MODEL), lambda i: (0, 1))
    return pl.pallas_call(
        body, name=name, grid=(s // tm,),
        in_specs=[t_a, t_b, r_a, r_b, t_a, t_a], out_specs=t_a,
        out_shape=jax.ShapeDtypeStruct((s, D_MODEL), BF16),
        compiler_params=_params(("parallel",)),
    )(gates_raw, gates_raw, b_gate, b_gate, p_a, p_b)


def _gate_bwd(gates_raw, b_gate, p_a, p_b, dm, *, name, tm=512):
    s = p_a.shape[0]
    tm = min(tm, s)

    def body(ga_ref, gb_ref, ba_ref, bb_ref, pa_ref, pb_ref, dm_ref, dpa_ref, dpb_ref, dga_ref, dgb_ref,
             dba_ref, dbb_ref):
        @pl.when(pl.program_id(0) == 0)
        def _():
            dba_ref[...] = jnp.zeros_like(dba_ref)
            dbb_ref[...] = jnp.zeros_like(dbb_ref)

        d = dm_ref[...]
        for g_ref, b_ref, p_ref, dp_ref, dg_ref, db_ref in ((ga_ref, ba_ref, pa_ref, dpa_ref, dga_ref, dba_ref),
                                                            (gb_ref, bb_ref, pb_ref, dpb_ref, dgb_ref, dbb_ref)):
            sg = _sigmoid(g_ref[...] + b_ref[...])
            dp_ref[...] = (d * sg).astype(BF16)
            dg = d * p_ref[...] * (sg * (1.0 - sg))
            dg_ref[...] = dg.astype(BF16)
            db_ref[...] += jnp.sum(dg, axis=0, keepdims=True)

    t_a = pl.BlockSpec((tm, D_MODEL), lambda i: (i, 0))
    t_b = pl.BlockSpec((tm, D_MODEL), lambda i: (i, 1))
    r_a = pl.BlockSpec((1, D_MODEL), lambda i: (0, 0))
    r_b = pl.BlockSpec((1, D_MODEL), lambda i: (0, 1))
    big = jax.ShapeDtypeStruct((s, D_MODEL), BF16)
    row = jax.ShapeDtypeStruct((1, D_MODEL), F32)
    return pl.pallas_call(
        body, name=name, grid=(s // tm,),
        in_specs=[t_a, t_b, r_a, r_b, t_a, t_a, t_a], out_specs=[t_a, t_a, t_a, t_a, r_a, r_a],
        out_shape=[big, big, big, big, row, row],
        compiler_params=_params(("arbitrary",)),
    )(gates_raw, gates_raw, b_gate, b_gate, p_a, p_b, dm)


def _adamw(w, g, m, v, *, name, tr=128):
    r, c = w.shape
    tr = min(tr, r)
    assert r % tr == 0, (name, r, tr)

    def body(w_ref, g_ref, m_ref, v_ref, d_ref, mo_ref, vo_ref):
        gv = g_ref[...]
        mn = ADAM_B1 * m_ref[...] + (1.0 - ADAM_B1) * gv
        vn = ADAM_B2 * v_ref[...] + (1.0 - ADAM_B2) * (gv * gv)
        m_hat = mn / (1.0 - ADAM_B1 ** ADAM_STEP)
        v_hat = vn / (1.0 - ADAM_B2 ** ADAM_STEP)
        d_ref[...] = -ADAM_LR * (m_hat / (jnp.sqrt(v_hat) + ADAM_EPS) + ADAM_WD * w_ref[...])
        mo_ref[...] = mn
        vo_ref[...] = vn

    blk = pl.BlockSpec((tr, c), lambda i: (i, 0))
    sds = jax.ShapeDtypeStruct((r, c), F32)
    return pl.pallas_call(
        body, name=name, grid=(r // tr,), in_specs=[blk] * 4, out_specs=[blk] * 3, out_shape=[sds] * 3,
        compiler_params=_params(("parallel",)),
    )(w, g, m, v)


def _tile(n, pref):
    if n <= pref:
        return n
    best = LANES
    for t in range(LANES, pref + 1, LANES):
        if n % t == 0:
            best = t
    return best


MATMUL_BLOCK_BYTES = 20 * 1024 * 1024


def _mm(pairs, name, **kw):
    trans_b = kw.get("trans_b", False)
    m = (pairs[0][0][0] if isinstance(pairs[0][0], tuple) else pairs[0][0]).shape[0]
    ktot, n = 0, None
    for _, b in pairs:
        shape = b[0].shape[1:] if isinstance(b, tuple) else b.shape
        ktot += shape[1] if trans_b else shape[0]
        n = shape[0] if trans_b else shape[1]
    out_bytes = 4 * (2 if kw.get("add") is not None else 1)
    best = None
    for tm in (256, 512, 1024):
        for tn in range(LANES, min(n, 1536) + 1, LANES):
            if m % min(tm, m) or n % tn:
                continue
            fits = 2 * ktot * (min(tm, m) + tn) + out_bytes * min(tm, m) * tn <= MATMUL_BLOCK_BYTES
            if fits and (best is None or min(tm, m) * tn >= best[0] * best[1]):
                best = (min(tm, m), tn)
    return _matmul(pairs, tm=best[0], tn=best[1], name=name, **kw)


def _wgrad(a, b, name, **kw):
    return _matmul_tn(a, b, tk=_tile(a.shape[1], 1408), tn=kw.pop("tn", _tile(b.shape[1], 1024)), tm=2048,
                      name=name, **kw)


def _local_step(x, target, get_weight, small, emit_grad):
    heads = jnp.arange(SSD_D_INNER) // SSD_HEAD_DIM
    expand = (jnp.arange(LANES)[:, None] == heads[None, :]).astype(BF16)
    expand_t = expand.T
    group_sum = (jnp.arange(SGU_WIDTH)[:, None] // LANES == jnp.arange(LANES)[None, :]).astype(BF16)
    pad_h = LANES - SSD_HEADS
    dt_bias = jnp.pad(small["dt_bias"], ((0, 0), (0, pad_h)))
    a_log = jnp.pad(small["a_log"], ((0, 0), (0, pad_h)))
    a_log_x = jnp.repeat(small["a_log"], SSD_HEAD_DIM, axis=1)
    d_skip_x = jnp.repeat(small["d_skip"], SSD_HEAD_DIM, axis=1)
    b_sp_t = jnp.pad(small["b_spatial"][0].T, ((0, 0), (0, LANES - SGU_GROUPS)))
    w_sp = small["w_spatial"][0]
    conv_a_w = jnp.pad(small["conv_a_w"], ((0, 4), (0, 0)))
    conv_f_w = jnp.pad(small["conv_f_w"], ((0, 5), (0, 0)))
    final_w = small["final_norm_w"].reshape(1, D_MODEL)

    n1 = _rms_fwd(x, small["norm1_w"], name="rms1_fwd")
    wts = dict(get_weight("w_in", n1))
    z = _mm([(n1, wts["in_z"])], "in_z")
    xbc_raw = _mm([(n1, wts["in_xbc"])], "in_xbc")
    dt_raw = _mm([(n1, wts["in_dt"])], "in_dt")
    uv_raw = _mm([(n1, wts["in_uv"])], "in_uv")
    gates_raw = _mm([(n1, wts["in_gate"])], "in_gate")
    xbc = _conv_a_fwd(xbc_raw, conv_a_w, small["conv_a_b"], name="conv_a_fwd")
    y, y_a, states = _ssd_fwd(xbc, dt_raw, z, dt_bias, a_log, a_log_x, d_skip_x, small["ssd_norm_w"], expand,
                              name="ssd_fwd")
    y_b = _sgu_fwd(uv_raw, small["uv_b"], small["v_ln_w"], small["v_ln_b"], w_sp, b_sp_t, name="sgu_fwd")
    wts.update(get_weight("w_branch", y_b))
    p_a = _mm([(y_a, wts["branch_a"])], "branch_a")
    p_b = _mm([(y_b, wts["branch_b"])], "branch_b")
    mix = _gate_fwd(gates_raw, small["b_gate"], p_a, p_b, name="gate_fwd")
    wts.update(get_weight("w_out", mix))
    h1 = _mm([(mix, wts["out"])], "out_proj", add=x)
    n2 = _rms_fwd(h1, small["norm2_w"], name="rms2_fwd")
    wts.update(get_weight("w_up", n2))
    up_w = wts["up"]
    up_cols = up_w.shape[2]
    up_raw = _matmul([(n2, (up_w, "cols"))], tm=1024, tn=up_cols, name="up_proj")
    act = _conv_f_fwd(up_raw, conv_f_w, small["conv_f_b"], name="conv_f_fwd")
    wts.update(get_weight("w_down", act))
    h2 = _mm([(act, wts["down"])], "down_proj", add=h1)
    loss, dh2, dh2_b, d_final = _final_fwd_bwd(h2, final_w, target, name="final_norm_loss")

    dact = _mm([(dh2_b, wts["down"])], "down_dgrad", trans_b=True)
    started = emit_grad("w_down", _wgrad(act, dh2_b, "down_wgrad"))
    dup_a, dup_v, dwf_a, dwf_v, dbf_a, dbf_v = _conv_f_bwd(up_raw, conv_f_w, small["conv_f_b"], dact,
                                                           name="conv_f_bwd")
    dn2 = _mm([((dup_a, 0), (up_w, 0)), ((dup_a, 1), (up_w, 1)), ((dup_v, 0), (up_w, 2)), ((dup_v, 1), (up_w, 3))],
              "up_dgrad", trans_b=True, after=started)
    started = emit_grad("w_up", jnp.concatenate([_wgrad(n2, dup_a, "up_wgrad_a", tn=up_cols, stack_out=True),
                                                 _wgrad(n2, dup_v, "up_wgrad_v", tn=up_cols, stack_out=True)], axis=0))
    dh1, dh1_b, d_norm2 = _rms_bwd(h1, small["norm2_w"], dn2, dh2, name="rms2_bwd")
    dmix = _mm([(dh1_b, wts["out"])], "out_dgrad", trans_b=True, after=started)
    started = emit_grad("w_out", _wgrad(mix, dh1_b, "out_wgrad"))
    dp_a, dp_b, dg_a, dg_b, dbg_a, dbg_b = _gate_bwd(gates_raw, small["b_gate"], p_a, p_b, dmix, name="gate_bwd")
    dya = _mm([(dp_a, wts["branch_a"])], "branch_a_dgrad", trans_b=True, after=started)
    dyb = _mm([(dp_b, wts["branch_b"])], "branch_b_dgrad", trans_b=True)
    started_branch = emit_grad("w_branch", jnp.concatenate([_wgrad(y_a, dp_a, "branch_a_wgrad"),
                                                            _wgrad(y_b, dp_b, "branch_b_wgrad")], axis=0))
    duv, d_wsp, d_bsp_t, d_lnw, d_lnb, d_uvb = _sgu_bwd(uv_raw, dyb, small["uv_b"], small["v_ln_w"],
                                                        small["v_ln_b"], w_sp, b_sp_t, group_sum, name="sgu_bwd")
    dz, dxbc, ddt, d_ssd_nw, d_dskip, d_alog, d_dtb = _ssd_bwd(
        dya, y, z, xbc, dt_raw, states, dt_bias, a_log, a_log_x, d_skip_x, small["ssd_norm_w"], expand, expand_t,
        name="ssd_bwd")
    dxbc_raw, d_conv_a_w, d_conv_a_b = _conv_a_bwd(xbc_raw, conv_a_w, small["conv_a_b"], dxbc, name="conv_a_bwd")
    started = emit_grad("w_in", jnp.concatenate(
        [_wgrad(n1, dz, "in_z_wgrad", after=started_branch), _wgrad(n1, dxbc_raw, "in_xbc_wgrad"),
         _wgrad(n1, ddt, "in_dt_wgrad")[:, :SSD_HEADS], _wgrad(n1, duv, "in_uv_wgrad"),
         _wgrad(n1, dg_a, "in_gate_a_wgrad"), _wgrad(n1, dg_b, "in_gate_b_wgrad")], axis=1))
    dn1 = _mm([(dz, wts["in_z"]), (dxbc_raw, wts["in_xbc"]), (ddt, wts["in_dt"]), (duv, wts["in_uv"]),
               (dg_a, wts["in_gate_a"]), (dg_b, wts["in_gate_b"])], "in_dgrad", trans_b=True, after=started)
    dx, _, d_norm1 = _rms_bwd(x, small["norm1_w"], dn1, dh1, name="rms1_bwd")

    grads_small = {
        "norm1_w": d_norm1, "b_gate": jnp.concatenate([dbg_a, dbg_b], axis=1),
        "conv_a_w": d_conv_a_w[:4], "conv_a_b": d_conv_a_b,
        "dt_bias": d_dtb[:, :SSD_HEADS], "a_log": d_alog[:, :SSD_HEADS], "d_skip": d_dskip[:, :SSD_HEADS],
        "ssd_norm_w": d_ssd_nw, "uv_b": d_uvb, "v_ln_w": d_lnw, "v_ln_b": d_lnb,
        "w_spatial": d_wsp[None], "b_spatial": d_bsp_t[:, :SGU_GROUPS].T[None],
        "norm2_w": d_norm2, "conv_f_w": jnp.concatenate([dwf_a[:3], dwf_v[:3]], axis=1),
        "conv_f_b": jnp.concatenate([dbf_a, dbf_v], axis=1), "final_norm_w": d_final.reshape(D_MODEL),
    }
    return loss, dx, grads_small


HBM = pl.BlockSpec(memory_space=pl.ANY)
MESH = pl.DeviceIdType.MESH


def _mesh_pos():
    return lax.axis_index("x"), lax.axis_index("y"), lax.axis_index("c")


def _other_chips(x, y):
    return [(1 - x, y), (x, 1 - y), (1 - x, 1 - y)]


def _remote(src, dst, send_sems, recv_sems, k, dev):
    return pltpu.make_async_remote_copy(src_ref=src, dst_ref=dst, send_sem=send_sems.at[k], recv_sem=recv_sems.at[k],
                                        device_id=dev, device_id_type=MESH)


def _dma_sems(n):
    return [pltpu.SemaphoreType.DMA((n,)), pltpu.SemaphoreType.DMA((n,))]


HBM_ONLY = pl.BlockSpec(memory_space=pltpu.HBM)
SEMAPHORES = pl.BlockSpec(memory_space=pltpu.SEMAPHORE)
DATAFLOW_EFFECT = pltpu.SideEffectType.DATAFLOW_SIDE_EFFECTING
N_PEER_CHIPS = N_CHIPS - 1


def _gather_sends(w_ref, land_ref, send_sems, recv_sems):
    x, y, c = _mesh_pos()
    return [_remote(w_ref.at[c], land_ref.at[2 * x + y, c], send_sems, recv_sems, k, (px, py, c))
            for k, (px, py) in enumerate(_other_chips(x, y))]


def _gather_arrivals(w_ref, land_ref, send_sems, recv_sems):
    x, y, c = _mesh_pos()
    return [_remote(w_ref.at[c], land_ref.at[2 * px + py, c], send_sems, recv_sems, k, (px, py, c))
            for k, (px, py) in enumerate(_other_chips(x, y))]


def _scatter_sends(h_ref, land_ref, send_sems, recv_sems):
    x, y, c = _mesh_pos()
    return [_remote(h_ref.at[2 * px + py], land_ref.at[2 * x + y], send_sems, recv_sems, k, (px, py, c))
            for k, (px, py) in enumerate(_other_chips(x, y))]


def _scatter_arrivals(h_ref, land_ref, send_sems, recv_sems):
    x, y, c = _mesh_pos()
    return [_remote(h_ref.at[2 * x + y], land_ref.at[2 * px + py], send_sems, recv_sems, k, (px, py, c))
            for k, (px, py) in enumerate(_other_chips(x, y))]


def _exchange_start(sources, landing_shapes, sends, *, after=None, name):
    n = len(sources)
    extra = [] if after is None else [after]

    def body(*refs):
        sems = refs[2 * n + len(extra):4 * n + len(extra)]
        for i in range(n):
            for cp in sends(refs[i], refs[n + i], sems[2 * i], sems[2 * i + 1]):
                cp.start()
        refs[-1][...] = jnp.zeros_like(refs[-1])

    hbm = [pltpu.HBM(s.shape, s.dtype) for s in sources] + [pltpu.HBM(shp, s.dtype)
                                                             for shp, s in zip(landing_shapes, sources)]
    outs = pl.pallas_call(
        body, name=name,
        out_shape=tuple([pltpu.SemaphoreType.DMA((N_PEER_CHIPS,))] * (2 * n) + hbm
                        + [jax.ShapeDtypeStruct((8, LANES), F32)]),
        in_specs=[HBM_ONLY] * (2 * n) + [pl.BlockSpec(memory_space=pl.ANY)] * len(extra),
        out_specs=tuple([SEMAPHORES] * (2 * n) + [HBM_ONLY] * (2 * n) + [pl.BlockSpec(memory_space=pltpu.VMEM)]),
        input_output_aliases={i: 2 * n + i for i in range(2 * n)},
        compiler_params=pltpu.CompilerParams(has_side_effects=DATAFLOW_EFFECT),
    )(*[pltpu.with_memory_space_constraint(s, pltpu.HBM) for s in sources],
      *[pltpu.with_memory_space_constraint(lax.empty(shp, s.dtype), pltpu.HBM)
        for shp, s in zip(landing_shapes, sources)], *extra)
    pending = [(outs[2 * i], outs[2 * i + 1], outs[2 * n + i], outs[3 * n + i]) for i in range(n)]
    return pending, outs[-1]


def _exchange_wait(pending, after, sends, arrivals, *, name):
    send_sems, recv_sems, source, landing = pending

    def body(src_ref, land_ref, send_ref, recv_ref, after_ref, src_out, land_out):
        for cp in sends(src_ref, land_ref, send_ref, recv_ref):
            cp.wait_send()
        for cp in arrivals(src_ref, land_ref, send_ref, recv_ref):
            cp.wait_recv()

    return pl.pallas_call(
        body, name=name,
        out_shape=(pltpu.HBM(source.shape, source.dtype), pltpu.HBM(landing.shape, landing.dtype)),
        in_specs=[HBM_ONLY, HBM_ONLY, SEMAPHORES, SEMAPHORES, pl.BlockSpec(memory_space=pl.ANY)],
        out_specs=(HBM_ONLY, HBM_ONLY), input_output_aliases={0: 0, 1: 1},
        compiler_params=pltpu.CompilerParams(has_side_effects=DATAFLOW_EFFECT),
    )(source, landing, send_sems, recv_sems, after)


def _gather_ici(shard, *, name):
    _, rh, cols = shard.shape

    def body(w_ref, o_ref, send_sems, recv_sems):
        x, y, c = _mesh_pos()
        mine = 2 * x + y
        sends = []
        for k, (px, py) in enumerate(_other_chips(x, y)):
            cp = _remote(w_ref.at[c], o_ref.at[mine, c], send_sems, recv_sems, k, (px, py, c))
            cp.start()
            sends.append(cp)
        for k, (px, py) in enumerate(_other_chips(x, y)):
            _remote(w_ref.at[c], o_ref.at[2 * px + py, c], send_sems, recv_sems, k, (px, py, c)).wait_recv()
        for cp in sends:
            cp.wait_send()

    return pl.pallas_call(
        body, name=name, in_specs=[HBM], out_specs=HBM,
        out_shape=jax.ShapeDtypeStruct((N_CHIPS, 2, rh, cols), shard.dtype), scratch_shapes=_dma_sems(3),
    )(shard)


def _gather_d2d(parts, *, name):
    def body(a_ref, o_ref, send_sems, recv_sems):
        x, y, c = _mesh_pos()
        sibling = (x, y, 1 - c)
        sends = []
        for k, (px, py) in enumerate(_other_chips(x, y)):
            cp = _remote(a_ref.at[2 * px + py, c], o_ref.at[2 * px + py, c], send_sems, recv_sems, k, sibling)
            cp.start()
            sends.append(cp)
        for k, (px, py) in enumerate(_other_chips(x, y)):
            _remote(a_ref.at[2 * px + py, c], o_ref.at[2 * px + py, 1 - c], send_sems, recv_sems, k, sibling).wait_recv()
        for cp in sends:
            cp.wait_send()

    return pl.pallas_call(
        body, name=name, in_specs=[HBM], out_specs=HBM,
        out_shape=jax.ShapeDtypeStruct(parts.shape, parts.dtype),
        input_output_aliases={0: 0}, scratch_shapes=_dma_sems(3),
    )(parts)


def _all_gather_chips(shard_flat, name):
    rows, cols = shard_flat.shape
    parts = _gather_ici(shard_flat.reshape(2, rows // 2, cols), name=name + "_ici")
    others = _gather_d2d(parts, name=name + "_d2d").reshape(N_CHIPS, rows, cols)
    chip = 2 * lax.axis_index("x") + lax.axis_index("y")
    return lax.dynamic_update_slice(others, shard_flat[None], (chip, 0, 0))


def _row_tile(rows, mult, cap):
    best = mult
    for t in range(mult, min(rows, cap) + 1, mult):
        if rows % t == 0:
            best = t
    assert rows % best == 0, (rows, mult)
    return best


def _swap_halves_d2d(g, *, name):
    _, _, rh, cols = g.shape

    def body(g_ref, o_ref, send_sems, recv_sems):
        x, y, c = _mesh_pos()
        sibling = (x, y, 1 - c)
        sends = []
        for s in range(N_CHIPS):
            cp = _remote(g_ref.at[s, 1 - c], o_ref.at[s], send_sems, recv_sems, s, sibling)
            cp.start()
            sends.append(cp)
        for s in range(N_CHIPS):
            _remote(g_ref.at[s, c], o_ref.at[s], send_sems, recv_sems, s, sibling).wait_recv()
        for cp in sends:
            cp.wait_send()

    return pl.pallas_call(
        body, name=name, in_specs=[HBM], out_specs=HBM,
        out_shape=jax.ShapeDtypeStruct((N_CHIPS, rh, cols), g.dtype), scratch_shapes=_dma_sems(N_CHIPS),
    )(g)


def _add_own_half(g, arrived, core, *, name):
    _, _, rh, cols = g.shape
    mult = 16 if g.dtype == BF16 else 8
    tr = _row_tile(rh, mult, max(mult, (512 * 1024) // cols))

    def body(core_ref, g_ref, a_ref, o_ref):
        o_ref[...] = (g_ref[0].astype(F32) + a_ref[...].astype(F32)).astype(o_ref.dtype)

    grid_spec = pltpu.PrefetchScalarGridSpec(
        num_scalar_prefetch=1, grid=(N_CHIPS, rh // tr),
        in_specs=[pl.BlockSpec((1, 1, tr, cols), lambda s, i, core_ref: (s, core_ref[0], i, 0)),
                  pl.BlockSpec((1, tr, cols), lambda s, i, core_ref: (s, i, 0))],
        out_specs=pl.BlockSpec((1, tr, cols), lambda s, i, core_ref: (s, i, 0)))
    return pl.pallas_call(
        body, name=name, grid_spec=grid_spec, out_shape=jax.ShapeDtypeStruct((N_CHIPS, rh, cols), g.dtype),
        compiler_params=_params(("parallel", "parallel")),
    )(core, g, arrived)


def _scatter_ici(h, *, name):
    def body(h_ref, o_ref, send_sems, recv_sems):
        x, y, c = _mesh_pos()
        mine = 2 * x + y
        sends = []
        for k, (px, py) in enumerate(_other_chips(x, y)):
            cp = _remote(h_ref.at[2 * px + py], o_ref.at[mine], send_sems, recv_sems, k, (px, py, c))
            cp.start()
            sends.append(cp)
        for k, (px, py) in enumerate(_other_chips(x, y)):
            _remote(h_ref.at[mine], o_ref.at[2 * px + py], send_sems, recv_sems, k, (px, py, c)).wait_recv()
        for cp in sends:
            cp.wait_send()

    others = pl.pallas_call(
        body, name=name, in_specs=[HBM], out_specs=HBM, out_shape=jax.ShapeDtypeStruct(h.shape, h.dtype),
        scratch_shapes=_dma_sems(3),
    )(h)
    chip = 2 * lax.axis_index("x") + lax.axis_index("y")
    own = lax.dynamic_slice_in_dim(h, chip, 1, axis=0)
    return lax.dynamic_update_slice(others, own, (chip, 0, 0))


def _sum_chips(parts, *, name):
    _, rh, cols = parts.shape
    mult = 16 if parts.dtype == BF16 else 8
    tr = _row_tile(rh, mult, max(mult, (512 * 1024) // cols))

    def body(p_ref, o_ref):
        acc = p_ref[0].astype(F32)
        for s in range(1, N_CHIPS):
            acc = acc + p_ref[s].astype(F32)
        o_ref[...] = acc

    return pl.pallas_call(
        body, name=name, grid=(rh // tr,),
        in_specs=[pl.BlockSpec((N_CHIPS, tr, cols), lambda i: (0, i, 0))],
        out_specs=pl.BlockSpec((tr, cols), lambda i: (i, 0)),
        out_shape=jax.ShapeDtypeStruct((rh, cols), F32), compiler_params=_params(("parallel",)),
    )(parts)


def _share_d2d(f, *, name):
    def body(f_ref, o_ref, send_sems, recv_sems):
        x, y, c = _mesh_pos()
        sibling = (x, y, 1 - c)
        cp = _remote(f_ref, o_ref, send_sems, recv_sems, 0, sibling)
        cp.start()
        cp.wait()

    other = pl.pallas_call(
        body, name=name, in_specs=[HBM], out_specs=HBM, out_shape=jax.ShapeDtypeStruct(f.shape, f.dtype),
        scratch_shapes=_dma_sems(1),
    )(f)
    first = lax.axis_index("c") == 0
    return jnp.stack([jnp.where(first, f, other), jnp.where(first, other, f)])


def _reduce_scatter_chips(g, core, name):
    _, rows, cols = g.shape
    g = g.reshape(N_CHIPS, 2, rows // 2, cols)
    arrived = _swap_halves_d2d(g, name=name + "_swap")
    chip_sum = _add_own_half(g, arrived, core, name=name + "_add2")
    parts = _scatter_ici(chip_sum, name=name + "_ici")
    total = _sum_chips(parts, name=name + "_sum4")
    return _share_d2d(total, name=name + "_share").reshape(rows, cols)


BIG = ("w_in", "w_branch", "w_out", "w_up", "w_down")
BIG_COLUMN_SHARDED = ("w_in", "w_up")
CONV = ("conv_a_w", "conv_f_w")
REPLICATED = ("norm1_w", "b_gate", "conv_a_b", "dt_bias", "a_log", "d_skip", "ssd_norm_w", "uv_b", "v_ln_w",
              "v_ln_b", "w_spatial", "b_spatial", "norm2_w", "conv_f_b", "final_norm_w")
WEIGHT_ORDER = ("norm1_w", "w_in", "b_gate", "conv_a_w", "conv_a_b", "dt_bias", "a_log", "d_skip", "ssd_norm_w",
                "uv_b", "v_ln_w", "v_ln_b", "w_spatial", "b_spatial", "w_branch", "w_out", "norm2_w", "w_up",
                "conv_f_w", "conv_f_b", "w_down", "final_norm_w")
SMALL_EXCHANGE_ROWS = 64


def _flat_rows(arrays, row_multiple):
    flat = jnp.concatenate([a.reshape(-1) for a in arrays])
    rows = -(-flat.shape[0] // (LANES * row_multiple)) * row_multiple
    return jnp.pad(flat, (0, rows * LANES - flat.shape[0])).reshape(rows, LANES)


def _unflatten(flat, shapes):
    flat = flat.reshape(-1)
    out, off = [], 0
    for shp in shapes:
        n = math.prod(shp)
        out.append(flat[off:off + n].reshape(shp))
        off += n
    return out


def _from_chip_blocks(blocks, name):
    if name in BIG_COLUMN_SHARDED or name in CONV:
        k = blocks.shape[1]
        return jnp.transpose(blocks, (1, 0, 2)).reshape(k, -1)
    return blocks.reshape(-1, blocks.shape[-1])


def _to_chip_blocks(whole, name):
    if name in BIG_COLUMN_SHARDED or name in CONV:
        k, n = whole.shape
        return jnp.transpose(whole.reshape(k, N_CHIPS, n // N_CHIPS), (1, 0, 2))
    return whole.reshape(N_CHIPS, whole.shape[0] // N_CHIPS, whole.shape[1])


def kernel(x, norm1_w, w_in, b_gate, conv_a_w, conv_a_b, dt_bias, a_log, d_skip, ssd_norm_w, uv_b, v_ln_w, v_ln_b, w_spatial, b_spatial, w_branch, w_out, norm2_w, w_up, conv_f_w, conv_f_b, w_down, final_norm_w, loss_target, m_norm1_w, m_w_in, m_b_gate, m_conv_a_w, m_conv_a_b, m_dt_bias, m_a_log, m_d_skip, m_ssd_norm_w, m_uv_b, m_v_ln_w, m_v_ln_b, m_w_spatial, m_b_spatial, m_w_branch, m_w_out, m_norm2_w, m_w_up, m_conv_f_w, m_conv_f_b, m_w_down, m_final_norm_w, v_norm1_w, v_w_in, v_b_gate, v_conv_a_w, v_conv_a_b, v_dt_bias, v_a_log, v_d_skip, v_ssd_norm_w, v_uv_b, v_v_ln_w, v_v_ln_b, v_w_spatial, v_b_spatial, v_w_branch, v_w_out, v_norm2_w, v_w_up, v_conv_f_w, v_conv_f_b, v_w_down, v_final_norm_w):
    weights = dict(norm1_w=norm1_w, w_in=w_in, b_gate=b_gate, conv_a_w=conv_a_w, conv_a_b=conv_a_b, dt_bias=dt_bias,
                   a_log=a_log, d_skip=d_skip, ssd_norm_w=ssd_norm_w, uv_b=uv_b, v_ln_w=v_ln_w, v_ln_b=v_ln_b,
                   w_spatial=w_spatial, b_spatial=b_spatial, w_branch=w_branch, w_out=w_out, norm2_w=norm2_w,
                   w_up=w_up, conv_f_w=conv_f_w, conv_f_b=conv_f_b, w_down=w_down, final_norm_w=final_norm_w)
    mom1 = dict(norm1_w=m_norm1_w, w_in=m_w_in, b_gate=m_b_gate, conv_a_w=m_conv_a_w, conv_a_b=m_conv_a_b,
                dt_bias=m_dt_bias, a_log=m_a_log, d_skip=m_d_skip, ssd_norm_w=m_ssd_norm_w, uv_b=m_uv_b,
                v_ln_w=m_v_ln_w, v_ln_b=m_v_ln_b, w_spatial=m_w_spatial, b_spatial=m_b_spatial, w_branch=m_w_branch,
                w_out=m_w_out, norm2_w=m_norm2_w, w_up=m_w_up, conv_f_w=m_conv_f_w, conv_f_b=m_conv_f_b,
                w_down=m_w_down, final_norm_w=m_final_norm_w)
    mom2 = dict(norm1_w=v_norm1_w, w_in=v_w_in, b_gate=v_b_gate, conv_a_w=v_conv_a_w, conv_a_b=v_conv_a_b,
                dt_bias=v_dt_bias, a_log=v_a_log, d_skip=v_d_skip, ssd_norm_w=v_ssd_norm_w, uv_b=v_uv_b,
                v_ln_w=v_v_ln_w, v_ln_b=v_v_ln_b, w_spatial=v_w_spatial, b_spatial=v_b_spatial, w_branch=v_w_branch,
                w_out=v_w_out, norm2_w=v_norm2_w, w_up=v_w_up, conv_f_w=v_conv_f_w, conv_f_b=v_conv_f_b,
                w_down=v_w_down, final_norm_w=v_final_norm_w)
    chip = 2 * lax.axis_index("x") + lax.axis_index("y")
    core = lax.axis_index("c").astype(jnp.int32).reshape(1)

    whole = {}
    conv_shapes = [weights[n].shape[1:] for n in CONV]
    conv_gathered = _all_gather_chips(_flat_rows([weights[n] for n in CONV], 16), "gather_conv").reshape(N_CHIPS, -1)
    off = 0
    for n, shp in zip(CONV, conv_shapes):
        size = math.prod(shp)
        whole[n] = _from_chip_blocks(conv_gathered[:, off:off + size].reshape((N_CHIPS,) + shp), n)
        off += size
    shard_shapes = {n: weights[n].shape[1:] for n in BIG}
    halves = [weights[n][0].astype(BF16).reshape(2, shard_shapes[n][0] // 2, shard_shapes[n][1]) for n in BIG]
    gathers, _ = _exchange_start(halves, [(N_CHIPS,) + h.shape for h in halves], _gather_sends, after=conv_gathered,
                                 name="gather_start")
    gathers = dict(zip(BIG, gathers))

    def get_weight(name, after):
        rows, cols = shard_shapes[name]
        own, landed = _exchange_wait(gathers[name], after, _gather_sends, _gather_arrivals,
                                     name="gather_" + name + "_wait")
        others = _gather_d2d(landed, name="gather_" + name + "_d2d").reshape(N_CHIPS, rows, cols)
        blocks = lax.dynamic_update_slice(others, own.reshape(1, rows, cols), (chip, 0, 0))
        if name == "w_up":
            return {"up": blocks}
        full = _from_chip_blocks(blocks, name)
        if name == "w_branch":
            return {"branch_a": full[:SSD_D_INNER], "branch_b": full[SSD_D_INNER:]}
        if name != "w_in":
            return {name[2:]: full}
        gate0 = SSD_IN + 2 * SGU_WIDTH
        return {
            "in_z": full[:, :SSD_D_INNER],
            "in_xbc": full[:, SSD_D_INNER:SSD_D_INNER + SSD_XBC],
            "in_dt": jnp.pad(full[:, SSD_D_INNER + SSD_XBC:SSD_IN], ((0, 0), (0, LANES - SSD_HEADS))),
            "in_uv": full[:, SSD_IN:gate0],
            "in_gate": full[:, gate0:], "in_gate_a": full[:, gate0:gate0 + D_MODEL], "in_gate_b": full[:, gate0 + D_MODEL:],
        }

    small = {n: weights[n] for n in REPLICATED}
    small["conv_a_w"] = whole["conv_a_w"]
    small["conv_f_w"] = whole["conv_f_w"]

    reductions = {}

    def emit_grad(name, g):
        g_blocks = g if name == "w_up" else _to_chip_blocks(g, name)
        _, rows, cols = g_blocks.shape
        g_halves = g_blocks.reshape(N_CHIPS, 2, rows // 2, cols)
        arrived = _swap_halves_d2d(g_halves, name="reduce_" + name + "_swap")
        chip_sum = _add_own_half(g_halves, arrived, core, name="reduce_" + name + "_add2")
        own = lax.dynamic_slice_in_dim(chip_sum, chip, 1, axis=0)
        (pending,), started = _exchange_start([chip_sum], [chip_sum.shape], _scatter_sends,
                                              name="reduce_" + name + "_start")
        reductions[name] = (pending, own)
        return started

    loss, dx, grads_small = _local_step(x[0], loss_target[0], get_weight, small, emit_grad)

    grads = {}
    for n in ("w_down", "w_up", "w_out", "w_branch", "w_in"):
        pending, own = reductions[n]
        _, landed = _exchange_wait(pending, dx, _scatter_sends, _scatter_arrivals, name="reduce_" + n + "_wait")
        parts = lax.dynamic_update_slice(landed, own, (chip, 0, 0))
        total = _sum_chips(parts, name="reduce_" + n + "_sum4")
        grads[n] = _share_d2d(total, name="reduce_" + n + "_share").reshape(shard_shapes[n])

    small_names = REPLICATED + CONV
    small_shapes = [grads_small[n].shape for n in small_names]
    g_small = _flat_rows([grads_small[n] for n in small_names], N_CHIPS * 2 * SMALL_EXCHANGE_ROWS)
    red_small = _reduce_scatter_chips(g_small.reshape(N_CHIPS, -1, LANES), core, "reduce_small")
    all_small = _all_gather_chips(red_small, "gather_small")
    for n, g in zip(small_names, _unflatten(all_small, small_shapes)):
        if n in CONV:
            width = g.shape[1] // N_CHIPS
            g = lax.dynamic_slice_in_dim(g, chip * width, width, axis=1)
        grads[n] = g.reshape(weights[n].shape[1:]) if n != "final_norm_w" else g

    delta, new_m, new_v = {}, {}, {}
    for n in BIG:
        shp = weights[n].shape
        tr = _row_tile(shp[1], 8, 128)
        d, m1, m2 = _adamw(weights[n][0], grads[n], mom1[n][0], mom2[n][0], name="adamw_" + n, tr=tr)
        delta[n], new_m[n], new_v[n] = d.reshape(shp), m1.reshape(shp), m2.reshape(shp)
    small_all = [n for n in WEIGHT_ORDER if n not in BIG]
    shapes = [weights[n].shape for n in small_all]
    packed = [_flat_rows([src[n] for n in small_all], 8)
              for src in (weights, {n: grads[n] for n in small_all}, mom1, mom2)]
    d, m1, m2 = _adamw(*packed, name="adamw_small", tr=packed[0].shape[0])
    for n, dv, mv, vv in zip(small_all, _unflatten(d, shapes), _unflatten(m1, shapes), _unflatten(m2, shapes)):
        delta[n], new_m[n], new_v[n] = dv, mv, vv

    total_loss = lax.psum(loss[0, 0], ("x", "y", "c"))
    grad_out = [grads[n].reshape(weights[n].shape) for n in WEIGHT_ORDER]
    return (total_loss, dx[None], *grad_out, *[delta[n] for n in WEIGHT_ORDER], *[new_m[n] for n in WEIGHT_ORDER],
            *[new_v[n] for n in WEIGHT_ORDER])
```

```python
import functools
import math

import jax
import jax.numpy as jnp
from jax import lax
from jax.experimental import pallas as pl
from jax.experimental.pallas import tpu as pltpu

F32 = jnp.float32
BF16 = jnp.bfloat16
HI = lax.Precision.HIGHEST

D_MODEL = 1024
SSD_D_INNER = 2048
SSD_HEADS = 32
SSD_HEAD_DIM = 64
SSD_GROUPS = 4
SSD_HEADS_PER_GROUP = 8
SSD_STATE = 128
SSD_BC = 512
SSD_XBC = 3072
SSD_IN = 5152
SGU_WIDTH = 1024
SGU_GROUPS = 8
CHUNK = 128
IN_COLS = 9248
D_FF = 2816
NORM_EPS = 1e-6
LN_EPS = 1e-5
GROUP_COLS = SSD_HEADS_PER_GROUP * SSD_HEAD_DIM
LANES = 128

ADAM_LR = 0.001
ADAM_B1 = 0.9
ADAM_B2 = 0.999
ADAM_EPS = 1e-08
ADAM_WD = 0.01
ADAM_STEP = 10

N_CHIPS = 4
VMEM_LIMIT = 56 * 1024 * 1024

NT = (((1,), (1,)), ((), ()))
TN = (((0,), (0,)), ((), ()))
NN = (((1,), (0,)), ((), ()))


def _params(dims):
    return pltpu.CompilerParams(dimension_semantics=dims, vmem_limit_bytes=VMEM_LIMIT)


def _dot(a, b, dn=NN, precision=None):
    return lax.dot_general(a, b, dn, precision=precision, preferred_element_type=F32)


def _split3(x):
    hi = x.astype(BF16)
    rest = x - hi.astype(F32)
    mid = rest.astype(BF16)
    return hi, mid, (rest - mid.astype(F32)).astype(BF16)


def _dot_terms(terms, exact, dn=NN):
    out = None
    for t in terms:
        p = _dot(t, exact, dn)
        out = p if out is None else out + p
    return out


def _dot_exact_lhs(exact, terms):
    out = None
    for t in terms:
        p = _dot(exact, t)
        out = p if out is None else out + p
    return out


def _sigmoid(x):
    return 1.0 / (1.0 + jnp.exp(-x))


def _softplus(x):
    return jnp.maximum(x, 0.0) + jnp.log(1.0 + jnp.exp(-jnp.abs(x)))


def _matmul(pairs, *, trans_b=False, add=None, after=None, out_dtype=F32, tm=512, tn=512, name):
    def mat_shape(b):
        if isinstance(b, tuple) and b[1] == "cols":
            return (b[0].shape[1], b[0].shape[0] * b[0].shape[2])
        return b[0].shape[1:] if isinstance(b, tuple) else b.shape

    if isinstance(pairs[0][1], tuple) and pairs[0][1][1] == "cols":
        assert not trans_b and tn % LANES == 0 and pairs[0][1][0].shape[2] % tn == 0, name

    m = (pairs[0][0][0] if isinstance(pairs[0][0], tuple) else pairs[0][0]).shape[0]
    n = mat_shape(pairs[0][1])[0] if trans_b else mat_shape(pairs[0][1])[1]
    tm, tn = min(tm, m), min(tn, n)
    assert m % tm == 0 and n % tn == 0, (name, m, n, tm, tn)
    npairs = len(pairs)
    dn = NT if trans_b else NN

    def body(*refs):
        o_ref = refs[-1]
        acc = None
        for i in range(npairs):
            p = _dot(refs[2 * i][...].astype(BF16), refs[2 * i + 1][...].astype(BF16), dn)
            acc = p if acc is None else acc + p
        if add is not None:
            acc = acc + refs[2 * npairs][...]
        o_ref[...] = acc.astype(out_dtype)

    in_specs, args = [], []
    for a, b in pairs:
        bshape = mat_shape(b)
        k = bshape[1] if trans_b else bshape[0]
        assert bshape == ((n, k) if trans_b else (k, n)), (name, bshape)
        a, qa = a if isinstance(a, tuple) else (a, 0)
        assert a.shape[0] == m and a.shape[1] % k == 0, (name, a.shape, k)
        in_specs.append(pl.BlockSpec((tm, k), lambda i, j, qa=qa: (i, qa)))
        if isinstance(b, tuple) and b[1] == "cols":
            b = b[0]
            per = b.shape[2] // tn
            in_specs.append(pl.BlockSpec((None, k, tn), lambda i, j, per=per: (j // per, 0, j % per)))
        elif isinstance(b, tuple):
            b, qb = b
            if trans_b:
                in_specs.append(pl.BlockSpec((None, tn, k), lambda i, j, qb=qb: (qb, j, 0)))
            else:
                in_specs.append(pl.BlockSpec((None, k, tn), lambda i, j, qb=qb: (qb, 0, j)))
        elif trans_b:
            in_specs.append(pl.BlockSpec((tn, k), lambda i, j: (j, 0)))
        else:
            in_specs.append(pl.BlockSpec((k, tn), lambda i, j: (0, j)))
        args += [a, b]
    if add is not None:
        in_specs.append(pl.BlockSpec((tm, tn), lambda i, j: (i, j)))
        args.append(add)
    if after is not None:
        in_specs.append(pl.BlockSpec(memory_space=pl.ANY))
        args.append(after)
    return pl.pallas_call(
        body, name=name, grid=(m // tm, n // tn), in_specs=in_specs,
        out_specs=pl.BlockSpec((tm, tn), lambda i, j: (i, j)),
        out_shape=jax.ShapeDtypeStruct((m, n), out_dtype),
        compiler_params=_params(("parallel", "parallel")),
    )(*args)


def _matmul_tn(a, b, *, tk, tn, tm=1024, out_dtype=BF16, stack_out=False, after=None, name):
    m, k = a.shape
    n = b.shape[1]
    tm, tk, tn = min(tm, m), min(tk, k), min(tn, n)
    assert m % tm == 0 and k % tk == 0 and n % tn == 0, (name, m, k, n)
    nm = m // tm
    if stack_out:
        out_spec = pl.BlockSpec((None, tk, tn), lambda i, j, l: (j, i, 0))
        out_shape = jax.ShapeDtypeStruct((n // tn, k, tn), out_dtype)
    else:
        out_spec = pl.BlockSpec((tk, tn), lambda i, j, l: (i, j))
        out_shape = jax.ShapeDtypeStruct((k, n), out_dtype)

    def body(a_ref, b_ref, *rest):
        o_ref, acc = rest[-2:]
        mi = pl.program_id(2)

        @pl.when(mi == 0)
        def _():
            acc[...] = jnp.zeros_like(acc)

        acc[...] += _dot(a_ref[...].astype(BF16), b_ref[...].astype(BF16), TN)

        @pl.when(mi == nm - 1)
        def _():
            o_ref[...] = acc[...].astype(out_dtype)

    in_specs = [pl.BlockSpec((tm, tk), lambda i, j, l: (l, i)), pl.BlockSpec((tm, tn), lambda i, j, l: (l, j))]
    args = [a, b]
    if after is not None:
        in_specs.append(pl.BlockSpec(memory_space=pl.ANY))
        args.append(after)
    return pl.pallas_call(
        body, name=name, grid=(k // tk, n // tn, nm), in_specs=in_specs,
        out_specs=out_spec, out_shape=out_shape,
        scratch_shapes=[pltpu.VMEM((tk, tn), F32)],
        compiler_params=_params(("parallel", "parallel", "arbitrary")),
    )(*args)


def _rms_fwd(x, w, *, name, tm=512):
    s, d = x.shape
    tm = min(tm, s)

    def body(x_ref, w_ref, o_ref):
        xv = x_ref[...]
        r = lax.rsqrt(jnp.mean(xv * xv, axis=-1, keepdims=True) + NORM_EPS)
        o_ref[...] = (xv * r * w_ref[...]).astype(BF16)

    return pl.pallas_call(
        body, name=name, grid=(s // tm,),
        in_specs=[pl.BlockSpec((tm, d), lambda i: (i, 0)), pl.BlockSpec((1, d), lambda i: (0, 0))],
        out_specs=pl.BlockSpec((tm, d), lambda i: (i, 0)),
        out_shape=jax.ShapeDtypeStruct((s, d), BF16),
        compiler_params=_params(("parallel",)),
    )(x, w)


def _rms_bwd(x, w, dn, dres, *, name, tm=512):
    s, d = x.shape
    tm = min(tm, s)

    def body(x_ref, w_ref, dn_ref, dres_ref, dx_ref, dxb_ref, dw_ref):
        @pl.when(pl.program_id(0) == 0)
        def _():
            dw_ref[...] = jnp.zeros_like(dw_ref)

        xv = x_ref[...]
        r = lax.rsqrt(jnp.mean(xv * xv, axis=-1, keepdims=True) + NORM_EPS)
        xhat = xv * r
        dnv = dn_ref[...]
        dxhat = dnv * w_ref[...]
        dx = dres_ref[...] + r * (dxhat - xhat * jnp.mean(dxhat * xhat, axis=-1, keepdims=True))
        dx_ref[...] = dx
        dxb_ref[...] = dx.astype(BF16)
        dw_ref[...] += jnp.sum(dnv * xhat, axis=0, keepdims=True)

    tile = pl.BlockSpec((tm, d), lambda i: (i, 0))
    row = pl.BlockSpec((1, d), lambda i: (0, 0))
    return pl.pallas_call(
        body, name=name, grid=(s // tm,),
        in_specs=[tile, row, tile, tile], out_specs=[tile, tile, row],
        out_shape=[jax.ShapeDtypeStruct((s, d), F32), jax.ShapeDtypeStruct((s, d), BF16),
                   jax.ShapeDtypeStruct((1, d), F32)],
        compiler_params=_params(("arbitrary",)),
    )(x, w, dn, dres)


def _final_fwd_bwd(h2, wf, target, *, name, tm=512):
    s, d = h2.shape
    tm = min(tm, s)

    def body(h_ref, w_ref, t_ref, loss_ref, dh_ref, dhb_ref, dw_ref):
        @pl.when(pl.program_id(0) == 0)
        def _():
            dw_ref[...] = jnp.zeros_like(dw_ref)
            loss_ref[...] = jnp.zeros_like(loss_ref)

        hv = h_ref[...]
        r = lax.rsqrt(jnp.mean(hv * hv, axis=-1, keepdims=True) + NORM_EPS)
        xhat = hv * r
        err = xhat * w_ref[...] - t_ref[...]
        per_tok = jnp.mean(err * err, axis=-1, keepdims=True)
        loss_ref[...] += 0.5 * jnp.sum(per_tok, axis=0, keepdims=True)
        dy = err * (1.0 / d)
        dxhat = dy * w_ref[...]
        dh = r * (dxhat - xhat * jnp.mean(dxhat * xhat, axis=-1, keepdims=True))
        dh_ref[...] = dh
        dhb_ref[...] = dh.astype(BF16)
        dw_ref[...] += jnp.sum(dy * xhat, axis=0, keepdims=True)

    tile = pl.BlockSpec((tm, d), lambda i: (i, 0))
    row = pl.BlockSpec((1, d), lambda i: (0, 0))
    return pl.pallas_call(
        body, name=name, grid=(s // tm,),
        in_specs=[tile, row, tile],
        out_specs=[pl.BlockSpec((1, 1), lambda i: (0, 0)), tile, tile, row],
        out_shape=[jax.ShapeDtypeStruct((1, 1), F32), jax.ShapeDtypeStruct((s, d), F32),
                   jax.ShapeDtypeStruct((s, d), BF16), jax.ShapeDtypeStruct((1, d), F32)],
        compiler_params=_params(("arbitrary",)),
    )(h2, wf, target)


CONV_ROWS = 512
HALO = 8


def _rows_with_halo(ref, r0, rows, s, before, after):
    parts = []
    if before:
        prev = ref[pl.ds(pl.multiple_of(jnp.maximum(r0 - HALO, 0), HALO), HALO), :]
        parts.append(jnp.where(r0 > 0, prev, 0.0))
    parts.append(ref[pl.ds(r0, rows), :])
    if after:
        nxt = ref[pl.ds(pl.multiple_of(jnp.minimum(r0 + rows, s - HALO), HALO), HALO), :]
        parts.append(jnp.where(r0 + rows < s, nxt, 0.0))
    return jnp.concatenate(parts, axis=0) if len(parts) > 1 else parts[0]


def _earlier(xe, k, rows):
    if k == 0:
        return xe[HALO:HALO + rows]
    return pltpu.roll(xe, k, 0)[HALO:HALO + rows]


def _later(ve, k, rows):
    if k == 0:
        return ve[:rows]
    return pltpu.roll(ve, ve.shape[0] - k, 0)[:rows]


def _conv_taps(xe, w_ref, kk, rows):
    acc = None
    for i in range(kk):
        term = w_ref[i:i + 1, :] * _earlier(xe, kk - 1 - i, rows)
        acc = term if acc is None else acc + term
    return acc


def _row_loop(s, step):
    def body(r, carry):
        return step(pl.multiple_of(r * CONV_ROWS, CONV_ROWS), carry)
    return body


def _conv_bwd_rows(xe, dpe, w_ref, kk):
    dp = dpe[:CONV_ROWS]
    dx = None
    dws = []
    for i in range(kk):
        dws.append(jnp.sum(dp * _earlier(xe, kk - 1 - i, CONV_ROWS), axis=0, keepdims=True))
        term = w_ref[i:i + 1, :] * _later(dpe, kk - 1 - i, CONV_ROWS)
        dx = term if dx is None else dx + term
    return dx, dws, jnp.sum(dp, axis=0, keepdims=True)


def _conv_a_fwd(xraw, w, b, *, name, tc=128):
    s, c = xraw.shape
    kk = 4

    def body(x_ref, w_ref, b_ref, o_ref):
        def step(r0, carry):
            xe = _rows_with_halo(x_ref, r0, CONV_ROWS, s, True, False)
            pre = _conv_taps(xe, w_ref, kk, CONV_ROWS) + b_ref[...]
            o_ref[pl.ds(r0, CONV_ROWS), :] = pre * _sigmoid(pre)
            return carry

        lax.fori_loop(0, s // CONV_ROWS, _row_loop(s, step), 0)

    col = pl.BlockSpec((s, tc), lambda j: (0, j))
    return pl.pallas_call(
        body, name=name, grid=(c // tc,),
        in_specs=[col, pl.BlockSpec((8, tc), lambda j: (0, j)), pl.BlockSpec((1, tc), lambda j: (0, j))],
        out_specs=col, out_shape=jax.ShapeDtypeStruct((s, c), F32),
        compiler_params=_params(("parallel",)),
    )(xraw, w, b)


def _conv_a_bwd(xraw, w, b, dy, *, name, tc=128):
    s, c = xraw.shape
    kk = 4

    def body(x_ref, w_ref, b_ref, dy_ref, dx_ref, dw_ref, db_ref):
        def step(r0, carry):
            xe = _rows_with_halo(x_ref, r0, CONV_ROWS, s, True, True)
            pre = _conv_taps(xe, w_ref, kk, CONV_ROWS + HALO) + b_ref[...]
            sg = _sigmoid(pre)
            dpe = _rows_with_halo(dy_ref, r0, CONV_ROWS, s, False, True) * (sg * (1.0 + pre * (1.0 - sg)))
            dx, dws, db = _conv_bwd_rows(xe, dpe, w_ref, kk)
            dx_ref[pl.ds(r0, CONV_ROWS), :] = dx.astype(BF16)
            return tuple(acc + new for acc, new in zip(carry, dws + [db]))

        zero = jnp.zeros((1, tc), F32)
        sums = lax.fori_loop(0, s // CONV_ROWS, _row_loop(s, step), (zero,) * (kk + 1))
        db_ref[...] = sums[kk]
        dw_ref[...] = jnp.concatenate(list(sums[:kk]) + [jnp.zeros((8 - kk, tc), F32)], axis=0)

    col = pl.BlockSpec((s, tc), lambda j: (0, j))
    w8 = pl.BlockSpec((8, tc), lambda j: (0, j))
    row = pl.BlockSpec((1, tc), lambda j: (0, j))
    return pl.pallas_call(
        body, name=name, grid=(c // tc,),
        in_specs=[col, w8, row, col], out_specs=[col, w8, row],
        out_shape=[jax.ShapeDtypeStruct((s, c), BF16), jax.ShapeDtypeStruct((8, c), F32),
                   jax.ShapeDtypeStruct((1, c), F32)],
        compiler_params=_params(("parallel",)),
    )(xraw, w, b, dy)


def _conv_f_fwd(up_raw, w, b, *, name, tc=128):
    s, c2 = up_raw.shape
    c = c2 // 2
    nb = c // tc
    kk = 3

    def body(xa_ref, xv_ref, wa_ref, wv_ref, ba_ref, bv_ref, o_ref):
        def step(r0, carry):
            a = _conv_taps(_rows_with_halo(xa_ref, r0, CONV_ROWS, s, True, False), wa_ref, kk, CONV_ROWS) + ba_ref[...]
            v = _conv_taps(_rows_with_halo(xv_ref, r0, CONV_ROWS, s, True, False), wv_ref, kk, CONV_ROWS) + bv_ref[...]
            o_ref[pl.ds(r0, CONV_ROWS), :] = (a * _sigmoid(a) * v).astype(BF16)
            return carry

        lax.fori_loop(0, s // CONV_ROWS, _row_loop(s, step), 0)

    col_a = pl.BlockSpec((s, tc), lambda j: (0, j))
    col_v = pl.BlockSpec((s, tc), lambda j: (0, j + nb))
    return pl.pallas_call(
        body, name=name, grid=(nb,),
        in_specs=[col_a, col_v, pl.BlockSpec((8, tc), lambda j: (0, j)), pl.BlockSpec((8, tc), lambda j: (0, j + nb)),
                  pl.BlockSpec((1, tc), lambda j: (0, j)), pl.BlockSpec((1, tc), lambda j: (0, j + nb))],
        out_specs=col_a, out_shape=jax.ShapeDtypeStruct((s, c), BF16),
        compiler_params=_params(("parallel",)),
    )(up_raw, up_raw, w, w, b, b)


def _conv_f_bwd(up_raw, w, b, dact, *, name, tc=128):
    s, c2 = up_raw.shape
    c = c2 // 2
    nb = c // tc
    kk = 3

    def body(xa_ref, xv_ref, wa_ref, wv_ref, ba_ref, bv_ref, d_ref,
             dxa_ref, dxv_ref, dwa_ref, dwv_ref, dba_ref, dbv_ref):
        def step(r0, carry):
            xae = _rows_with_halo(xa_ref, r0, CONV_ROWS, s, True, True)
            xve = _rows_with_halo(xv_ref, r0, CONV_ROWS, s, True, True)
            a = _conv_taps(xae, wa_ref, kk, CONV_ROWS + HALO) + ba_ref[...]
            v = _conv_taps(xve, wv_ref, kk, CONV_ROWS + HALO) + bv_ref[...]
            sg = _sigmoid(a)
            d = _rows_with_halo(d_ref, r0, CONV_ROWS, s, False, True)
            dxa, dwas, dba = _conv_bwd_rows(xae, d * v * (sg * (1.0 + a * (1.0 - sg))), wa_ref, kk)
            dxv, dwvs, dbv = _conv_bwd_rows(xve, d * (a * sg), wv_ref, kk)
            dxa_ref[pl.ds(r0, CONV_ROWS), :] = dxa.astype(BF16)
            dxv_ref[pl.ds(r0, CONV_ROWS), :] = dxv.astype(BF16)
            return tuple(acc + new for acc, new in zip(carry, dwas + [dba] + dwvs + [dbv]))

        zero = jnp.zeros((1, tc), F32)
        sums = lax.fori_loop(0, s // CONV_ROWS, _row_loop(s, step), (zero,) * (2 * kk + 2))
        pad = [jnp.zeros((8 - kk, tc), F32)]
        dwa_ref[...] = jnp.concatenate(list(sums[:kk]) + pad, axis=0)
        dba_ref[...] = sums[kk]
        dwv_ref[...] = jnp.concatenate(list(sums[kk + 1:2 * kk + 1]) + pad, axis=0)
        dbv_ref[...] = sums[2 * kk + 1]

    col_a = pl.BlockSpec((s, tc), lambda j: (0, j))
    col_v = pl.BlockSpec((s, tc), lambda j: (0, j + nb))
    w_a = pl.BlockSpec((8, tc), lambda j: (0, j))
    w_v = pl.BlockSpec((8, tc), lambda j: (0, j + nb))
    r_a = pl.BlockSpec((1, tc), lambda j: (0, j))
    r_v = pl.BlockSpec((1, tc), lambda j: (0, j + nb))
    outs = pl.pallas_call(
        body, name=name, grid=(nb,),
        in_specs=[col_a, col_v, w_a, w_v, r_a, r_v, col_a],
        out_specs=[col_a, col_a, w_a, w_a, r_a, r_a],
        out_shape=[jax.ShapeDtypeStruct((s, c), BF16), jax.ShapeDtypeStruct((s, c), BF16),
                   jax.ShapeDtypeStruct((8, c), F32), jax.ShapeDtypeStruct((8, c), F32),
                   jax.ShapeDtypeStruct((1, c), F32), jax.ShapeDtypeStruct((1, c), F32)],
        compiler_params=_params(("parallel",)),
    )(up_raw, up_raw, w, w, b, b, dact)
    return outs


def _tri_masks():
    row = lax.broadcasted_iota(jnp.int32, (CHUNK, CHUNK), 0)
    col = lax.broadcasted_iota(jnp.int32, (CHUNK, CHUNK), 1)
    return row >= col, row <= col


def _ssd_fwd(xbc, dt_raw, z, dt_bias, a_log, a_log_x, d_skip_x, norm_w, expand, *, name):
    s = xbc.shape[0]
    nc = s // CHUNK

    def body(xbc_ref, dtr_ref, z_ref, dtb_ref, alog_ref, alogx_ref, dskx_ref, nw_ref, e_ref,
             y_ref, ya_ref, st_ref, state):
        @pl.when(pl.program_id(0) == 0)
        def _():
            state[...] = jnp.zeros_like(state)

        st_ref[0] = state[...]
        lower, _ = _tri_masks()
        dt = _softplus(dtr_ref[...] + dtb_ref[...])
        adt = dt * (-jnp.exp(alog_ref[...]))
        acum = _dot_exact_lhs(lower.astype(BF16), _split3(adt))
        acum_t = acum.T
        dt_terms, acum_terms = _split3(dt), _split3(acum)
        for g in range(SSD_GROUPS):
            sl = slice(GROUP_COLS * g, GROUP_COLS * (g + 1))
            dt_x = _dot_terms(dt_terms, e_ref[:, sl])
            acum_x = _dot_terms(acum_terms, e_ref[:, sl])
            tot_x = jnp.sum(dt_x * (-jnp.exp(alogx_ref[:, sl])), axis=0, keepdims=True)
            xs = xbc_ref[:, sl]
            xdt = xs * dt_x
            xdt_b = xdt.astype(BF16)
            bg = xbc_ref[:, SSD_D_INNER + SSD_STATE * g:SSD_D_INNER + SSD_STATE * (g + 1)].astype(BF16)
            cg = xbc_ref[:, SSD_D_INNER + SSD_BC + SSD_STATE * g:SSD_D_INNER + SSD_BC + SSD_STATE * (g + 1)].astype(BF16)
            cb = _dot(cg, bg, NT)
            st_g = state[:, sl]
            y_off = _dot(cg, st_g.astype(BF16)) * jnp.exp(acum_x)
            parts = []
            for r in range(SSD_HEADS_PER_GROUP):
                h = SSD_HEADS_PER_GROUP * g + r
                dec = jnp.exp(jnp.where(lower, acum[:, h:h + 1] - acum_t[h:h + 1, :], -jnp.inf))
                parts.append(_dot((cb * dec).astype(BF16), xdt_b[:, SSD_HEAD_DIM * r:SSD_HEAD_DIM * (r + 1)]))
            y_ref[:, sl] = jnp.concatenate(parts, axis=1) + y_off + dskx_ref[:, sl] * xs
            wgt = (xdt * jnp.exp(tot_x - acum_x)).astype(BF16)
            state[:, sl] = st_g * jnp.exp(tot_x) + _dot(bg, wgt, TN)
        zv = z_ref[...]
        q = y_ref[...] * (zv * _sigmoid(zv))
        r = lax.rsqrt(jnp.mean(q * q, axis=-1, keepdims=True) + NORM_EPS)
        ya_ref[...] = (q * r * nw_ref[...]).astype(BF16)

    def chunk(w):
        return pl.BlockSpec((CHUNK, w), lambda c: (c, 0))

    def const(shape):
        return pl.BlockSpec(shape, lambda c: (0,) * len(shape))

    return pl.pallas_call(
        body, name=name, grid=(nc,),
        in_specs=[chunk(SSD_XBC), chunk(LANES), chunk(SSD_D_INNER), const((1, LANES)), const((1, LANES)),
                  const((1, SSD_D_INNER)), const((1, SSD_D_INNER)), const((1, SSD_D_INNER)),
                  const((LANES, SSD_D_INNER))],
        out_specs=[chunk(SSD_D_INNER), chunk(SSD_D_INNER),
                   pl.BlockSpec((1, SSD_STATE, SSD_D_INNER), lambda c: (c, 0, 0))],
        out_shape=[jax.ShapeDtypeStruct((s, SSD_D_INNER), F32), jax.ShapeDtypeStruct((s, SSD_D_INNER), BF16),
                   jax.ShapeDtypeStruct((nc, SSD_STATE, SSD_D_INNER), F32)],
        scratch_shapes=[pltpu.VMEM((SSD_STATE, SSD_D_INNER), F32)],
        compiler_params=_params(("arbitrary",)),
    )(xbc, dt_raw, z, dt_bias, a_log, a_log_x, d_skip_x, norm_w, expand)


def _ssd_bwd(dya, y, z, xbc, dt_raw, states, dt_bias, a_log, a_log_x, d_skip_x, norm_w, expand, expand_t, *, name):
    s = xbc.shape[0]
    nc = s // CHUNK

    def body(dya_ref, y_ref, z_ref, xbc_ref, dtr_ref, stp_ref, dtb_ref, alog_ref, alogx_ref, dskx_ref, nw_ref,
             e_ref, et_ref, dz_ref, dxbc_ref, ddt_ref, dnw_ref, ddsk_ref, dalog_ref, ddtb_ref,
             dstate, dy_sc, dskcol):
        i = pl.program_id(0)

        @pl.when(i == 0)
        def _():
            dstate[...] = jnp.zeros_like(dstate)
            dskcol[...] = jnp.zeros_like(dskcol)
            dnw_ref[...] = jnp.zeros_like(dnw_ref)
            dalog_ref[...] = jnp.zeros_like(dalog_ref)
            ddtb_ref[...] = jnp.zeros_like(ddtb_ref)
            ddsk_ref[...] = jnp.zeros_like(ddsk_ref)

        lower, upper = _tri_masks()
        rows = lax.broadcasted_iota(jnp.int32, (CHUNK, LANES), 0)
        pre = dtr_ref[...] + dtb_ref[...]
        dt = _softplus(pre)
        a = -jnp.exp(alog_ref[...])
        acum = _dot_exact_lhs(lower.astype(BF16), _split3(dt * a))
        acum_t = acum.T
        dt_terms, acum_terms = _split3(dt), _split3(acum)

        yv = y_ref[...]
        zv = z_ref[...]
        sz = _sigmoid(zv)
        silu_z = zv * sz
        q = yv * silu_z
        r = lax.rsqrt(jnp.mean(q * q, axis=-1, keepdims=True) + NORM_EPS)
        qhat = q * r
        dyav = dya_ref[...]
        dqhat = dyav * nw_ref[...]
        dnw_ref[...] += jnp.sum(dyav * qhat, axis=0, keepdims=True)
        dq = r * (dqhat - qhat * jnp.mean(dqhat * qhat, axis=-1, keepdims=True))
        dy_sc[...] = dq * silu_z
        dz_ref[...] = (dq * yv * (sz * (1.0 + zv * (1.0 - sz)))).astype(BF16)

        da_cum = jnp.zeros((CHUNK, LANES), F32)
        ddt = jnp.zeros((CHUNK, LANES), F32)
        for g in range(SSD_GROUPS):
            sl = slice(GROUP_COLS * g, GROUP_COLS * (g + 1))
            et_g = et_ref[sl, :]
            dt_x = _dot_terms(dt_terms, e_ref[:, sl])
            acum_x = _dot_terms(acum_terms, e_ref[:, sl])
            tot_x = jnp.sum(dt_x * (-jnp.exp(alogx_ref[:, sl])), axis=0, keepdims=True)
            e_tot = jnp.exp(tot_x)
            dec_s = jnp.exp(tot_x - acum_x)
            xs = xbc_ref[:, sl]
            xdt = xs * dt_x
            xdt_b = xdt.astype(BF16)
            dy = dy_sc[:, sl]
            dy_b = dy.astype(BF16)
            dskx = dskx_ref[:, sl]
            y_ssd = y_ref[:, sl] - dskx * xs
            dskcol[:, sl] += jnp.sum(dy * xs, axis=0, keepdims=True)
            bg = xbc_ref[:, SSD_D_INNER + SSD_STATE * g:SSD_D_INNER + SSD_STATE * (g + 1)].astype(BF16)
            cg = xbc_ref[:, SSD_D_INNER + SSD_BC + SSD_STATE * g:SSD_D_INNER + SSD_BC + SSD_STATE * (g + 1)].astype(BF16)
            cb = _dot(cg, bg, NT)
            sp = stp_ref[0, :, sl]
            ds_g = dstate[:, sl]
            ds_b = ds_g.astype(BF16)
            dye_b = (dy * jnp.exp(acum_x)).astype(BF16)
            dc = _dot(dye_b, sp.astype(BF16), NT)
            dxdt_state = dec_s * _dot(bg, ds_b)
            db = _dot((xdt * dec_s).astype(BF16), ds_b, NT)
            dcb = jnp.zeros((CHUNK, CHUNK), F32)
            parts = []
            for rr in range(SSD_HEADS_PER_GROUP):
                h = SSD_HEADS_PER_GROUP * g + rr
                hs = slice(SSD_HEAD_DIM * rr, SSD_HEAD_DIM * (rr + 1))
                dec = jnp.exp(jnp.where(lower, acum[:, h:h + 1] - acum_t[h:h + 1, :], -jnp.inf))
                parts.append(_dot((cb * dec).astype(BF16), dy_b[:, hs], TN))
                dcb = dcb + _dot(dy_b[:, hs], xdt_b[:, hs], NT) * dec
            dxdt = jnp.concatenate(parts, axis=1) + dxdt_state
            dcb_b = dcb.astype(BF16)
            dc = dc + _dot(dcb_b, bg)
            db = db + _dot(dcb_b, cg, TN)
            tot_col = jnp.sum(ds_g * sp, axis=0, keepdims=True) * e_tot + jnp.sum(dxdt_state * xdt, axis=0, keepdims=True)
            d_tot = _dot_terms(_split3(jnp.broadcast_to(tot_col, (8, GROUP_COLS))), et_g)
            d_tot = jnp.max(d_tot, axis=0, keepdims=True)
            pair_sums = dy_b.astype(F32) * y_ssd - xdt_b.astype(F32) * dxdt
            da_cum = da_cum + _dot_terms(_split3(pair_sums), et_g) + jnp.where(rows == CHUNK - 1, d_tot, 0.0)
            ddt = ddt + _dot_terms(_split3(dxdt * xs), et_g)
            dxbc_ref[:, sl] = dy * dskx + dxdt * dt_x
            dxbc_ref[:, SSD_D_INNER + SSD_STATE * g:SSD_D_INNER + SSD_STATE * (g + 1)] = db
            dxbc_ref[:, SSD_D_INNER + SSD_BC + SSD_STATE * g:SSD_D_INNER + SSD_BC + SSD_STATE * (g + 1)] = dc
            dstate[:, sl] = e_tot * ds_g + _dot(cg, dye_b, TN)

        dadt = _dot_exact_lhs(upper.astype(BF16), _split3(da_cum))
        ddt = ddt + dadt * a
        dalog_ref[...] += jnp.sum(dadt * dt, axis=0, keepdims=True)
        dpre = ddt * _sigmoid(pre)
        ddtb_ref[...] += jnp.sum(dpre, axis=0, keepdims=True)
        ddt_ref[...] = dpre.astype(BF16)

        @pl.when(i == nc - 1)
        def _():
            dalog_ref[...] = dalog_ref[...] * a
            dsk = _dot_terms(_split3(jnp.broadcast_to(dskcol[...], (8, SSD_D_INNER))), et_ref[...])
            ddsk_ref[...] = jnp.max(dsk, axis=0, keepdims=True)

    def chunk(w):
        return pl.BlockSpec((CHUNK, w), lambda i: (nc - 1 - i, 0))

    def const(shape):
        return pl.BlockSpec(shape, lambda i: (0,) * len(shape))

    return pl.pallas_call(
        body, name=name, grid=(nc,),
        in_specs=[chunk(SSD_D_INNER), chunk(SSD_D_INNER), chunk(SSD_D_INNER), chunk(SSD_XBC), chunk(LANES),
                  pl.BlockSpec((1, SSD_STATE, SSD_D_INNER), lambda i: (nc - 1 - i, 0, 0)),
                  const((1, LANES)), const((1, LANES)), const((1, SSD_D_INNER)), const((1, SSD_D_INNER)),
                  const((1, SSD_D_INNER)), const((LANES, SSD_D_INNER)), const((SSD_D_INNER, LANES))],
        out_specs=[chunk(SSD_D_INNER), chunk(SSD_XBC), chunk(LANES), const((1, SSD_D_INNER)), const((1, LANES)),
                   const((1, LANES)), const((1, LANES))],
        out_shape=[jax.ShapeDtypeStruct((s, SSD_D_INNER), BF16), jax.ShapeDtypeStruct((s, SSD_XBC), F32),
                   jax.ShapeDtypeStruct((s, LANES), BF16), jax.ShapeDtypeStruct((1, SSD_D_INNER), F32),
                   jax.ShapeDtypeStruct((1, LANES), F32), jax.ShapeDtypeStruct((1, LANES), F32),
                   jax.ShapeDtypeStruct((1, LANES), F32)],
        scratch_shapes=[pltpu.VMEM((SSD_STATE, SSD_D_INNER), F32), pltpu.VMEM((CHUNK, SSD_D_INNER), F32),
                        pltpu.VMEM((1, SSD_D_INNER), F32)],
        compiler_params=_params(("arbitrary",)),
    )(dya, y, z, xbc, dt_raw, states, dt_bias, a_log, a_log_x, d_skip_x, norm_w, expand, expand_t)


GELU_K = math.sqrt(2.0 / math.pi)
GELU_C = 0.044715


def _gelu(x):
    return 0.5 * x * (1.0 + jnp.tanh(GELU_K * (x + GELU_C * x * x * x)))


def _gelu_grad(x):
    t = jnp.tanh(GELU_K * (x + GELU_C * x * x * x))
    return 0.5 * (1.0 + t) + 0.5 * x * (1.0 - t * t) * (GELU_K * (1.0 + 3.0 * GELU_C * x * x))


def _sgu_pre(uv_ref, uvb_ref, lnw_ref, lnb_ref):
    uv = uv_ref[...] + uvb_ref[...]
    guv = _gelu(uv)
    u = guv[:, :SGU_WIDTH]
    v = guv[:, SGU_WIDTH:]
    mu = jnp.mean(v, axis=-1, keepdims=True)
    vc = v - mu
    rstd = lax.rsqrt(jnp.mean(vc * vc, axis=-1, keepdims=True) + LN_EPS)
    vhat = vc * rstd
    vn = vhat * lnw_ref[...] + lnb_ref[...]
    return uv, u, vhat, rstd, vn


def _sgu_fwd(uv_raw, uv_b, ln_w, ln_b, w_sp, b_sp_t, *, name):
    s = uv_raw.shape[0]
    nc = s // CHUNK

    def body(uv_ref, uvb_ref, lnw_ref, lnb_ref, w_ref, bt_ref, o_ref):
        lower, _ = _tri_masks()
        _, u, _, _, vn = _sgu_pre(uv_ref, uvb_ref, lnw_ref, lnb_ref)
        vn_b = vn.astype(BF16)
        bt = bt_ref[...]
        for g in range(SGU_GROUPS):
            gs = slice(LANES * g, LANES * (g + 1))
            wc = jnp.where(lower, w_ref[g], 0.0).astype(BF16)
            mixed = _dot(wc, vn_b[:, gs]) + bt[:, g:g + 1]
            o_ref[:, gs] = (u[:, gs] * mixed).astype(BF16)

    def const(shape):
        return pl.BlockSpec(shape, lambda c: (0,) * len(shape))

    return pl.pallas_call(
        body, name=name, grid=(nc,),
        in_specs=[pl.BlockSpec((CHUNK, 2 * SGU_WIDTH), lambda c: (c, 0)), const((1, 2 * SGU_WIDTH)),
                  const((1, SGU_WIDTH)), const((1, SGU_WIDTH)), const((SGU_GROUPS, CHUNK, CHUNK)),
                  const((CHUNK, LANES))],
        out_specs=pl.BlockSpec((CHUNK, SGU_WIDTH), lambda c: (c, 0)),
        out_shape=jax.ShapeDtypeStruct((s, SGU_WIDTH), BF16),
        compiler_params=_params(("parallel",)),
    )(uv_raw, uv_b, ln_w, ln_b, w_sp, b_sp_t)


def _sgu_bwd(uv_raw, dyb, uv_b, ln_w, ln_b, w_sp, b_sp_t, group_sum, *, name):
    s = uv_raw.shape[0]
    nc = s // CHUNK

    def body(uv_ref, dy_ref, uvb_ref, lnw_ref, lnb_ref, w_ref, bt_ref, gsum_ref,
             duv_ref, dw_ref, dbt_ref, dlnw_ref, dlnb_ref, duvb_ref):
        @pl.when(pl.program_id(0) == 0)
        def _():
            dw_ref[...] = jnp.zeros_like(dw_ref)
            dbt_ref[...] = jnp.zeros_like(dbt_ref)
            dlnw_ref[...] = jnp.zeros_like(dlnw_ref)
            dlnb_ref[...] = jnp.zeros_like(dlnb_ref)
            duvb_ref[...] = jnp.zeros_like(duvb_ref)

        lower, _ = _tri_masks()
        uv, u, vhat, rstd, vn = _sgu_pre(uv_ref, uvb_ref, lnw_ref, lnb_ref)
        vn_b = vn.astype(BF16)
        bt = bt_ref[...]
        dy = dy_ref[...]
        du_parts, dvn_parts, dmix_parts = [], [], []
        for g in range(SGU_GROUPS):
            gs = slice(LANES * g, LANES * (g + 1))
            wc = jnp.where(lower, w_ref[g], 0.0).astype(BF16)
            mixed = _dot(wc, vn_b[:, gs]) + bt[:, g:g + 1]
            du_parts.append(dy[:, gs] * mixed)
            dmix = dy[:, gs] * u[:, gs]
            dmix_b = dmix.astype(BF16)
            dmix_parts.append(dmix)
            dw_ref[g] += jnp.where(lower, _dot(dmix_b, vn_b[:, gs], NT), 0.0)
            dvn_parts.append(_dot(wc, dmix_b, TN))
        dmixed = jnp.concatenate(dmix_parts, axis=1)
        dbt_ref[...] += _dot_terms(_split3(dmixed), gsum_ref[...])
        dvn = jnp.concatenate(dvn_parts, axis=1)
        dlnw_ref[...] += jnp.sum(dvn * vhat, axis=0, keepdims=True)
        dlnb_ref[...] += jnp.sum(dvn, axis=0, keepdims=True)
        dvhat = dvn * lnw_ref[...]
        dv = rstd * (dvhat - jnp.mean(dvhat, axis=-1, keepdims=True)
                     - vhat * jnp.mean(dvhat * vhat, axis=-1, keepdims=True))
        dguv = jnp.concatenate(du_parts + [dv], axis=1)
        duv = dguv * _gelu_grad(uv)
        duvb_ref[...] += jnp.sum(duv, axis=0, keepdims=True)
        duv_ref[...] = duv.astype(BF16)

    def const(shape):
        return pl.BlockSpec(shape, lambda c: (0,) * len(shape))

    return pl.pallas_call(
        body, name=name, grid=(nc,),
        in_specs=[pl.BlockSpec((CHUNK, 2 * SGU_WIDTH), lambda c: (c, 0)),
                  pl.BlockSpec((CHUNK, SGU_WIDTH), lambda c: (c, 0)), const((1, 2 * SGU_WIDTH)),
                  const((1, SGU_WIDTH)), const((1, SGU_WIDTH)), const((SGU_GROUPS, CHUNK, CHUNK)),
                  const((CHUNK, LANES)), const((SGU_WIDTH, LANES))],
        out_specs=[pl.BlockSpec((CHUNK, 2 * SGU_WIDTH), lambda c: (c, 0)), const((SGU_GROUPS, CHUNK, CHUNK)),
                   const((CHUNK, LANES)), const((1, SGU_WIDTH)), const((1, SGU_WIDTH)), const((1, 2 * SGU_WIDTH))],
        out_shape=[jax.ShapeDtypeStruct((s, 2 * SGU_WIDTH), BF16),
                   jax.ShapeDtypeStruct((SGU_GROUPS, CHUNK, CHUNK), F32), jax.ShapeDtypeStruct((CHUNK, LANES), F32),
                   jax.ShapeDtypeStruct((1, SGU_WIDTH), F32), jax.ShapeDtypeStruct((1, SGU_WIDTH), F32),
                   jax.ShapeDtypeStruct((1, 2 * SGU_WIDTH), F32)],
        compiler_params=_params(("arbitrary",)),
    )(uv_raw, dyb, uv_b, ln_w, ln_b, w_sp, b_sp_t, group_sum)


def _gate_fwd(gates_raw, b_gate, p_a, p_b, *, name, tm=512):
    s = p_a.shape[0]
    tm = min(tm, s)

    def body(ga_ref, gb_ref, ba_ref, bb_ref, pa_ref, pb_ref, o_ref):
        ga = _sigmoid(ga_ref[...] + ba_ref[...])
        gb = _sigmoid(gb_ref[...] + bb_ref[...])
        o_ref[...] = (ga * pa_ref[...] + gb * pb_ref[...]).astype(BF16)

    t_a = pl.BlockSpec((tm, D_MODEL), lambda i: (i, 0))
    t_b = pl.BlockSpec((tm, D_MODEL), lambda i: (i, 1))
    r_a = pl.BlockSpec((1, D_MODEL), lambda i: (0, 0))
    r_b = pl.BlockSpec((1, D_MODEL), lambda i: (0, 1))
    return pl.pallas_call(
        body, name=name, grid=(s // tm,),
        in_specs=[t_a, t_b, r_a, r_b, t_a, t_a], out_specs=t_a,
        out_shape=jax.ShapeDtypeStruct((s, D_MODEL), BF16),
        compiler_params=_params(("parallel",)),
    )(gates_raw, gates_raw, b_gate, b_gate, p_a, p_b)


def _gate_bwd(gates_raw, b_gate, p_a, p_b, dm, *, name, tm=512):
    s = p_a.shape[0]
    tm = min(tm, s)

    def body(ga_ref, gb_ref, ba_ref, bb_ref, pa_ref, pb_ref, dm_ref, dpa_ref, dpb_ref, dga_ref, dgb_ref,
             dba_ref, dbb_ref):
        @pl.when(pl.program_id(0) == 0)
        def _():
            dba_ref[...] = jnp.zeros_like(dba_ref)
            dbb_ref[...] = jnp.zeros_like(dbb_ref)

        d = dm_ref[...]
        for g_ref, b_ref, p_ref, dp_ref, dg_ref, db_ref in ((ga_ref, ba_ref, pa_ref, dpa_ref, dga_ref, dba_ref),
                                                            (gb_ref, bb_ref, pb_ref, dpb_ref, dgb_ref, dbb_ref)):
            sg = _sigmoid(g_ref[...] + b_ref[...])
            dp_ref[...] = (d * sg).astype(BF16)
            dg = d * p_ref[...] * (sg * (1.0 - sg))
            dg_ref[...] = dg.astype(BF16)
            db_ref[...] += jnp.sum(dg, axis=0, keepdims=True)

    t_a = pl.BlockSpec((tm, D_MODEL), lambda i: (i, 0))
    t_b = pl.BlockSpec((tm, D_MODEL), lambda i: (i, 1))
    r_a = pl.BlockSpec((1, D_MODEL), lambda i: (0, 0))
    r_b = pl.BlockSpec((1, D_MODEL), lambda i: (0, 1))
    big = jax.ShapeDtypeStruct((s, D_MODEL), BF16)
    row = jax.ShapeDtypeStruct((1, D_MODEL), F32)
    return pl.pallas_call(
        body, name=name, grid=(s // tm,),
        in_specs=[t_a, t_b, r_a, r_b, t_a, t_a, t_a], out_specs=[t_a, t_a, t_a, t_a, r_a, r_a],
        out_shape=[big, big, big, big, row, row],
        compiler_params=_params(("arbitrary",)),
    )(gates_raw, gates_raw, b_gate, b_gate, p_a, p_b, dm)


def _adamw_update(w_ref, g_ref, m_ref, v_ref, d_ref, mo_ref, vo_ref):
    gv = g_ref[...]
    mn = ADAM_B1 * m_ref[...] + (1.0 - ADAM_B1) * gv
    vn = ADAM_B2 * v_ref[...] + (1.0 - ADAM_B2) * (gv * gv)
    m_hat = mn / (1.0 - ADAM_B1 ** ADAM_STEP)
    v_hat = vn / (1.0 - ADAM_B2 ** ADAM_STEP)
    d_ref[...] = -ADAM_LR * (m_hat / (jnp.sqrt(v_hat) + ADAM_EPS) + ADAM_WD * w_ref[...])
    mo_ref[...] = mn
    vo_ref[...] = vn


def _adamw_many(ws, gs, ms, vs, *, name):
    n = len(ws)

    def body(*refs):
        for i in range(n):
            _adamw_update(*[refs[k * n + i] for k in range(7)])

    whole = pl.BlockSpec(memory_space=pltpu.VMEM)
    sds = [jax.ShapeDtypeStruct(w.shape, F32) for w in ws]
    outs = pl.pallas_call(
        body, name=name, in_specs=[whole] * (4 * n), out_specs=[whole] * (3 * n), out_shape=sds * 3,
        compiler_params=pltpu.CompilerParams(vmem_limit_bytes=VMEM_LIMIT),
    )(*ws, *gs, *ms, *vs)
    return outs[:n], outs[n:2 * n], outs[2 * n:]


def _adamw(w, g, m, v, *, name, tr=128):
    r, c = w.shape
    tr = min(tr, r)
    assert r % tr == 0, (name, r, tr)
    body = functools.partial(_adamw_update)

    blk = pl.BlockSpec((tr, c), lambda i: (i, 0))
    sds = jax.ShapeDtypeStruct((r, c), F32)
    return pl.pallas_call(
        body, name=name, grid=(r // tr,), in_specs=[blk] * 4, out_specs=[blk] * 3, out_shape=[sds] * 3,
        compiler_params=_params(("parallel",)),
    )(w, g, m, v)


def _tile(n, pref):
    if n <= pref:
        return n
    best = LANES
    for t in range(LANES, pref + 1, LANES):
        if n % t == 0:
            best = t
    return best


MATMUL_BLOCK_BYTES = 20 * 1024 * 1024


def _mm(pairs, name, **kw):
    trans_b = kw.get("trans_b", False)
    m = (pairs[0][0][0] if isinstance(pairs[0][0], tuple) else pairs[0][0]).shape[0]
    ktot, n = 0, None
    for _, b in pairs:
        shape = b[0].shape[1:] if isinstance(b, tuple) else b.shape
        ktot += shape[1] if trans_b else shape[0]
        n = shape[0] if trans_b else shape[1]
    out_bytes = 4 * (2 if kw.get("add") is not None else 1)
    best = None
    for tm in (256, 512, 1024):
        for tn in range(LANES, min(n, 1536) + 1, LANES):
            if m % min(tm, m) or n % tn:
                continue
            fits = 2 * ktot * (min(tm, m) + tn) + out_bytes * min(tm, m) * tn <= MATMUL_BLOCK_BYTES
            if fits and (best is None or min(tm, m) * tn >= best[0] * best[1]):
                best = (min(tm, m), tn)
    return _matmul(pairs, tm=best[0], tn=best[1], name=name, **kw)


def _wgrad(a, b, name, **kw):
    return _matmul_tn(a, b, tk=_tile(a.shape[1], 1408), tn=kw.pop("tn", _tile(b.shape[1], 1024)), tm=2048,
                      name=name, **kw)


def _local_step(x, target, get_weight, small, emit_grad):
    heads = jnp.arange(SSD_D_INNER) // SSD_HEAD_DIM
    expand = (jnp.arange(LANES)[:, None] == heads[None, :]).astype(BF16)
    expand_t = expand.T
    group_sum = (jnp.arange(SGU_WIDTH)[:, None] // LANES == jnp.arange(LANES)[None, :]).astype(BF16)
    pad_h = LANES - SSD_HEADS
    dt_bias = jnp.pad(small["dt_bias"], ((0, 0), (0, pad_h)))
    a_log = jnp.pad(small["a_log"], ((0, 0), (0, pad_h)))
    a_log_x = jnp.repeat(small["a_log"], SSD_HEAD_DIM, axis=1)
    d_skip_x = jnp.repeat(small["d_skip"], SSD_HEAD_DIM, axis=1)
    b_sp_t = jnp.pad(small["b_spatial"][0].T, ((0, 0), (0, LANES - SGU_GROUPS)))
    w_sp = small["w_spatial"][0]
    conv_a_w = jnp.pad(small["conv_a_w"], ((0, 4), (0, 0)))
    conv_f_w = jnp.pad(small["conv_f_w"], ((0, 5), (0, 0)))
    final_w = small["final_norm_w"].reshape(1, D_MODEL)

    n1 = _rms_fwd(x, small["norm1_w"], name="rms1_fwd")
    wts = dict(get_weight("w_in", n1))
    z = _mm([(n1, wts["in_z"])], "in_z")
    xbc_raw = _mm([(n1, wts["in_xbc"])], "in_xbc")
    dt_raw = _mm([(n1, wts["in_dt"])], "in_dt")
    uv_raw = _mm([(n1, wts["in_uv"])], "in_uv")
    gates_raw = _mm([(n1, wts["in_gate"])], "in_gate")
    xbc = _conv_a_fwd(xbc_raw, conv_a_w, small["conv_a_b"], name="conv_a_fwd")
    y, y_a, states = _ssd_fwd(xbc, dt_raw, z, dt_bias, a_log, a_log_x, d_skip_x, small["ssd_norm_w"], expand,
                              name="ssd_fwd")
    y_b = _sgu_fwd(uv_raw, small["uv_b"], small["v_ln_w"], small["v_ln_b"], w_sp, b_sp_t, name="sgu_fwd")
    wts.update(get_weight("w_branch", y_b))
    p_a = _mm([(y_a, wts["branch_a"])], "branch_a")
    p_b = _mm([(y_b, wts["branch_b"])], "branch_b")
    mix = _gate_fwd(gates_raw, small["b_gate"], p_a, p_b, name="gate_fwd")
    wts.update(get_weight("w_out", mix))
    h1 = _mm([(mix, wts["out"])], "out_proj", add=x)
    n2 = _rms_fwd(h1, small["norm2_w"], name="rms2_fwd")
    wts.update(get_weight("w_up", n2))
    up_w = wts["up"]
    up_cols = up_w.shape[2]
    up_raw = _matmul([(n2, (up_w, "cols"))], tm=1024, tn=up_cols, name="up_proj")
    act = _conv_f_fwd(up_raw, conv_f_w, small["conv_f_b"], name="conv_f_fwd")
    wts.update(get_weight("w_down", act))
    h2 = _mm([(act, wts["down"])], "down_proj", add=h1)
    loss, dh2, dh2_b, d_final = _final_fwd_bwd(h2, final_w, target, name="final_norm_loss")

    dact = _mm([(dh2_b, wts["down"])], "down_dgrad", trans_b=True)
    started = emit_grad("w_down", _wgrad(act, dh2_b, "down_wgrad"))
    dup_a, dup_v, dwf_a, dwf_v, dbf_a, dbf_v = _conv_f_bwd(up_raw, conv_f_w, small["conv_f_b"], dact,
                                                           name="conv_f_bwd")
    dn2 = _mm([((dup_a, 0), (up_w, 0)), ((dup_a, 1), (up_w, 1)), ((dup_v, 0), (up_w, 2)), ((dup_v, 1), (up_w, 3))],
              "up_dgrad", trans_b=True, after=started)
    started = emit_grad("w_up", jnp.concatenate([_wgrad(n2, dup_a, "up_wgrad_a", tn=up_cols, stack_out=True),
                                                 _wgrad(n2, dup_v, "up_wgrad_v", tn=up_cols, stack_out=True)], axis=0))
    dh1, dh1_b, d_norm2 = _rms_bwd(h1, small["norm2_w"], dn2, dh2, name="rms2_bwd")
    dmix = _mm([(dh1_b, wts["out"])], "out_dgrad", trans_b=True, after=started)
    started = emit_grad("w_out", _wgrad(mix, dh1_b, "out_wgrad"))
    dp_a, dp_b, dg_a, dg_b, dbg_a, dbg_b = _gate_bwd(gates_raw, small["b_gate"], p_a, p_b, dmix, name="gate_bwd")
    dya = _mm([(dp_a, wts["branch_a"])], "branch_a_dgrad", trans_b=True, after=started)
    dyb = _mm([(dp_b, wts["branch_b"])], "branch_b_dgrad", trans_b=True)
    started_branch = emit_grad("w_branch", jnp.concatenate([_wgrad(y_a, dp_a, "branch_a_wgrad"),
                                                            _wgrad(y_b, dp_b, "branch_b_wgrad")], axis=0))
    duv, d_wsp, d_bsp_t, d_lnw, d_lnb, d_uvb = _sgu_bwd(uv_raw, dyb, small["uv_b"], small["v_ln_w"],
                                                        small["v_ln_b"], w_sp, b_sp_t, group_sum, name="sgu_bwd")
    dz, dxbc, ddt, d_ssd_nw, d_dskip, d_alog, d_dtb = _ssd_bwd(
        dya, y, z, xbc, dt_raw, states, dt_bias, a_log, a_log_x, d_skip_x, small["ssd_norm_w"], expand, expand_t,
        name="ssd_bwd")
    dxbc_raw, d_conv_a_w, d_conv_a_b = _conv_a_bwd(xbc_raw, conv_a_w, small["conv_a_b"], dxbc, name="conv_a_bwd")
    started = emit_grad("w_in", jnp.concatenate(
        [_wgrad(n1, dz, "in_z_wgrad", after=started_branch), _wgrad(n1, dxbc_raw, "in_xbc_wgrad"),
         _wgrad(n1, ddt, "in_dt_wgrad")[:, :SSD_HEADS], _wgrad(n1, duv, "in_uv_wgrad"),
         _wgrad(n1, dg_a, "in_gate_a_wgrad"), _wgrad(n1, dg_b, "in_gate_b_wgrad")], axis=1))
    dn1 = _mm([(dz, wts["in_z"]), (dxbc_raw, wts["in_xbc"]), (ddt, wts["in_dt"]), (duv, wts["in_uv"]),
               (dg_a, wts["in_gate_a"]), (dg_b, wts["in_gate_b"])], "in_dgrad", trans_b=True, after=started)
    dx, _, d_norm1 = _rms_bwd(x, small["norm1_w"], dn1, dh1, name="rms1_bwd")

    grads_small = {
        "norm1_w": d_norm1, "b_gate": jnp.concatenate([dbg_a, dbg_b], axis=1),
        "conv_a_w": d_conv_a_w[:4], "conv_a_b": d_conv_a_b,
        "dt_bias": d_dtb[:, :SSD_HEADS], "a_log": d_alog[:, :SSD_HEADS], "d_skip": d_dskip[:, :SSD_HEADS],
        "ssd_norm_w": d_ssd_nw, "uv_b": d_uvb, "v_ln_w": d_lnw, "v_ln_b": d_lnb,
        "w_spatial": d_wsp[None], "b_spatial": d_bsp_t[:, :SGU_GROUPS].T[None],
        "norm2_w": d_norm2, "conv_f_w": jnp.concatenate([dwf_a[:3], dwf_v[:3]], axis=1),
        "conv_f_b": jnp.concatenate([dbf_a, dbf_v], axis=1), "final_norm_w": d_final.reshape(D_MODEL),
    }
    return loss, dx, grads_small


HBM = pl.BlockSpec(memory_space=pl.ANY)
MESH = pl.DeviceIdType.MESH


def _mesh_pos():
    return lax.axis_index("x"), lax.axis_index("y"), lax.axis_index("c")


def _other_chips(x, y):
    return [(1 - x, y), (x, 1 - y), (1 - x, 1 - y)]


def _remote(src, dst, send_sems, recv_sems, k, dev):
    return pltpu.make_async_remote_copy(src_ref=src, dst_ref=dst, send_sem=send_sems.at[k], recv_sem=recv_sems.at[k],
                                        device_id=dev, device_id_type=MESH)


def _dma_sems(n):
    return [pltpu.SemaphoreType.DMA((n,)), pltpu.SemaphoreType.DMA((n,))]


HBM_ONLY = pl.BlockSpec(memory_space=pltpu.HBM)
SEMAPHORES = pl.BlockSpec(memory_space=pltpu.SEMAPHORE)
DATAFLOW_EFFECT = pltpu.SideEffectType.DATAFLOW_SIDE_EFFECTING
N_PEER_CHIPS = N_CHIPS - 1


def _gather_sends(w_ref, land_ref, send_sems, recv_sems):
    x, y, c = _mesh_pos()
    return [_remote(w_ref.at[c], land_ref.at[2 * x + y, c], send_sems, recv_sems, k, (px, py, c))
            for k, (px, py) in enumerate(_other_chips(x, y))]


def _gather_arrivals(w_ref, land_ref, send_sems, recv_sems):
    x, y, c = _mesh_pos()
    return [_remote(w_ref.at[c], land_ref.at[2 * px + py, c], send_sems, recv_sems, k, (px, py, c))
            for k, (px, py) in enumerate(_other_chips(x, y))]


def _scatter_sends(h_ref, land_ref, send_sems, recv_sems):
    x, y, c = _mesh_pos()
    return [_remote(h_ref.at[2 * px + py], land_ref.at[2 * x + y], send_sems, recv_sems, k, (px, py, c))
            for k, (px, py) in enumerate(_other_chips(x, y))]


def _scatter_arrivals(h_ref, land_ref, send_sems, recv_sems):
    x, y, c = _mesh_pos()
    return [_remote(h_ref.at[2 * x + y], land_ref.at[2 * px + py], send_sems, recv_sems, k, (px, py, c))
            for k, (px, py) in enumerate(_other_chips(x, y))]


def _exchange_start(sources, landing_shapes, sends, *, after=None, name):
    n = len(sources)
    extra = [] if after is None else [after]

    def body(*refs):
        sems = refs[2 * n + len(extra):4 * n + len(extra)]
        for i in range(n):
            for cp in sends(refs[i], refs[n + i], sems[2 * i], sems[2 * i + 1]):
                cp.start()
        refs[-1][...] = jnp.zeros_like(refs[-1])

    hbm = [pltpu.HBM(s.shape, s.dtype) for s in sources] + [pltpu.HBM(shp, s.dtype)
                                                             for shp, s in zip(landing_shapes, sources)]
    outs = pl.pallas_call(
        body, name=name,
        out_shape=tuple([pltpu.SemaphoreType.DMA((N_PEER_CHIPS,))] * (2 * n) + hbm
                        + [jax.ShapeDtypeStruct((8, LANES), F32)]),
        in_specs=[HBM_ONLY] * (2 * n) + [pl.BlockSpec(memory_space=pl.ANY)] * len(extra),
        out_specs=tuple([SEMAPHORES] * (2 * n) + [HBM_ONLY] * (2 * n) + [pl.BlockSpec(memory_space=pltpu.VMEM)]),
        input_output_aliases={i: 2 * n + i for i in range(2 * n)},
        compiler_params=pltpu.CompilerParams(has_side_effects=DATAFLOW_EFFECT),
    )(*[pltpu.with_memory_space_constraint(s, pltpu.HBM) for s in sources],
      *[pltpu.with_memory_space_constraint(lax.empty(shp, s.dtype), pltpu.HBM)
        for shp, s in zip(landing_shapes, sources)], *extra)
    pending = [(outs[2 * i], outs[2 * i + 1], outs[2 * n + i], outs[3 * n + i]) for i in range(n)]
    return pending, outs[-1]


def _exchange_wait(pending, after, sends, arrivals, *, name):
    send_sems, recv_sems, source, landing = pending

    def body(src_ref, land_ref, send_ref, recv_ref, after_ref, src_out, land_out):
        for cp in sends(src_ref, land_ref, send_ref, recv_ref):
            cp.wait_send()
        for cp in arrivals(src_ref, land_ref, send_ref, recv_ref):
            cp.wait_recv()

    return pl.pallas_call(
        body, name=name,
        out_shape=(pltpu.HBM(source.shape, source.dtype), pltpu.HBM(landing.shape, landing.dtype)),
        in_specs=[HBM_ONLY, HBM_ONLY, SEMAPHORES, SEMAPHORES, pl.BlockSpec(memory_space=pl.ANY)],
        out_specs=(HBM_ONLY, HBM_ONLY), input_output_aliases={0: 0, 1: 1},
        compiler_params=pltpu.CompilerParams(has_side_effects=DATAFLOW_EFFECT),
    )(source, landing, send_sems, recv_sems, after)


def _gather_ici(shard, *, name):
    _, rh, cols = shard.shape

    def body(w_ref, o_ref, send_sems, recv_sems):
        x, y, c = _mesh_pos()
        mine = 2 * x + y
        sends = []
        for k, (px, py) in enumerate(_other_chips(x, y)):
            cp = _remote(w_ref.at[c], o_ref.at[mine, c], send_sems, recv_sems, k, (px, py, c))
            cp.start()
            sends.append(cp)
        for k, (px, py) in enumerate(_other_chips(x, y)):
            _remote(w_ref.at[c], o_ref.at[2 * px + py, c], send_sems, recv_sems, k, (px, py, c)).wait_recv()
        for cp in sends:
            cp.wait_send()

    return pl.pallas_call(
        body, name=name, in_specs=[HBM], out_specs=HBM,
        out_shape=jax.ShapeDtypeStruct((N_CHIPS, 2, rh, cols), shard.dtype), scratch_shapes=_dma_sems(3),
    )(shard)


def _gather_d2d(parts, *, name):
    def body(a_ref, o_ref, send_sems, recv_sems):
        x, y, c = _mesh_pos()
        sibling = (x, y, 1 - c)
        sends = []
        for k, (px, py) in enumerate(_other_chips(x, y)):
            cp = _remote(a_ref.at[2 * px + py, c], o_ref.at[2 * px + py, c], send_sems, recv_sems, k, sibling)
            cp.start()
            sends.append(cp)
        for k, (px, py) in enumerate(_other_chips(x, y)):
            _remote(a_ref.at[2 * px + py, c], o_ref.at[2 * px + py, 1 - c], send_sems, recv_sems, k, sibling).wait_recv()
        for cp in sends:
            cp.wait_send()

    return pl.pallas_call(
        body, name=name, in_specs=[HBM], out_specs=HBM,
        out_shape=jax.ShapeDtypeStruct(parts.shape, parts.dtype),
        input_output_aliases={0: 0}, scratch_shapes=_dma_sems(3),
    )(parts)


def _all_gather_chips(shard_flat, name):
    rows, cols = shard_flat.shape
    parts = _gather_ici(shard_flat.reshape(2, rows // 2, cols), name=name + "_ici")
    others = _gather_d2d(parts, name=name + "_d2d").reshape(N_CHIPS, rows, cols)
    chip = 2 * lax.axis_index("x") + lax.axis_index("y")
    return lax.dynamic_update_slice(others, shard_flat[None], (chip, 0, 0))


def _row_tile(rows, mult, cap):
    best = mult
    for t in range(mult, min(rows, cap) + 1, mult):
        if rows % t == 0:
            best = t
    assert rows % best == 0, (rows, mult)
    return best


def _swap_halves_d2d(g, *, name):
    _, _, rh, cols = g.shape

    def body(g_ref, o_ref, send_sems, recv_sems):
        x, y, c = _mesh_pos()
        sibling = (x, y, 1 - c)
        sends = []
        for s in range(N_CHIPS):
            cp = _remote(g_ref.at[s, 1 - c], o_ref.at[s], send_sems, recv_sems, s, sibling)
            cp.start()
            sends.append(cp)
        for s in range(N_CHIPS):
            _remote(g_ref.at[s, c], o_ref.at[s], send_sems, recv_sems, s, sibling).wait_recv()
        for cp in sends:
            cp.wait_send()

    return pl.pallas_call(
        body, name=name, in_specs=[HBM], out_specs=HBM,
        out_shape=jax.ShapeDtypeStruct((N_CHIPS, rh, cols), g.dtype), scratch_shapes=_dma_sems(N_CHIPS),
    )(g)


def _add_own_half(g, arrived, core, *, name):
    _, _, rh, cols = g.shape
    mult = 16 if g.dtype == BF16 else 8
    tr = _row_tile(rh, mult, max(mult, (512 * 1024) // cols))

    def body(core_ref, g_ref, a_ref, o_ref):
        o_ref[...] = (g_ref[0].astype(F32) + a_ref[...].astype(F32)).astype(o_ref.dtype)

    grid_spec = pltpu.PrefetchScalarGridSpec(
        num_scalar_prefetch=1, grid=(N_CHIPS, rh // tr),
        in_specs=[pl.BlockSpec((1, 1, tr, cols), lambda s, i, core_ref: (s, core_ref[0], i, 0)),
                  pl.BlockSpec((1, tr, cols), lambda s, i, core_ref: (s, i, 0))],
        out_specs=pl.BlockSpec((1, tr, cols), lambda s, i, core_ref: (s, i, 0)))
    return pl.pallas_call(
        body, name=name, grid_spec=grid_spec, out_shape=jax.ShapeDtypeStruct((N_CHIPS, rh, cols), g.dtype),
        compiler_params=_params(("parallel", "parallel")),
    )(core, g, arrived)


def _scatter_ici(h, *, name):
    def body(h_ref, o_ref, send_sems, recv_sems):
        x, y, c = _mesh_pos()
        mine = 2 * x + y
        sends = []
        for k, (px, py) in enumerate(_other_chips(x, y)):
            cp = _remote(h_ref.at[2 * px + py], o_ref.at[mine], send_sems, recv_sems, k, (px, py, c))
            cp.start()
            sends.append(cp)
        for k, (px, py) in enumerate(_other_chips(x, y)):
            _remote(h_ref.at[mine], o_ref.at[2 * px + py], send_sems, recv_sems, k, (px, py, c)).wait_recv()
        for cp in sends:
            cp.wait_send()

    others = pl.pallas_call(
        body, name=name, in_specs=[HBM], out_specs=HBM, out_shape=jax.ShapeDtypeStruct(h.shape, h.dtype),
        scratch_shapes=_dma_sems(3),
    )(h)
    chip = 2 * lax.axis_index("x") + lax.axis_index("y")
    own = lax.dynamic_slice_in_dim(h, chip, 1, axis=0)
    return lax.dynamic_update_slice(others, own, (chip, 0, 0))


def _sum_chips(parts, *, name):
    _, rh, cols = parts.shape
    mult = 16 if parts.dtype == BF16 else 8
    tr = _row_tile(rh, mult, max(mult, (512 * 1024) // cols))

    def body(p_ref, o_ref):
        acc = p_ref[0].astype(F32)
        for s in range(1, N_CHIPS):
            acc = acc + p_ref[s].astype(F32)
        o_ref[...] = acc

    return pl.pallas_call(
        body, name=name, grid=(rh // tr,),
        in_specs=[pl.BlockSpec((N_CHIPS, tr, cols), lambda i: (0, i, 0))],
        out_specs=pl.BlockSpec((tr, cols), lambda i: (i, 0)),
        out_shape=jax.ShapeDtypeStruct((rh, cols), F32), compiler_params=_params(("parallel",)),
    )(parts)


def _share_d2d(f, *, name):
    fs = f if isinstance(f, (list, tuple)) else [f]
    n = len(fs)

    def body(*refs):
        x, y, c = _mesh_pos()
        sibling = (x, y, 1 - c)
        send_sems, recv_sems = refs[2 * n:]
        copies = [_remote(refs[i], refs[n + i], send_sems, recv_sems, i, sibling) for i in range(n)]
        for cp in copies:
            cp.start()
        for cp in copies:
            cp.wait()

    others = pl.pallas_call(
        body, name=name, in_specs=[HBM] * n, out_specs=[HBM] * n,
        out_shape=[jax.ShapeDtypeStruct(a.shape, a.dtype) for a in fs], scratch_shapes=_dma_sems(n),
    )(*fs)
    first = lax.axis_index("c") == 0
    both = [jnp.stack([jnp.where(first, a, b), jnp.where(first, b, a)]) for a, b in zip(fs, others)]
    return both if isinstance(f, (list, tuple)) else both[0]


def _reduce_scatter_chips(g, core, name):
    _, rows, cols = g.shape
    g = g.reshape(N_CHIPS, 2, rows // 2, cols)
    arrived = _swap_halves_d2d(g, name=name + "_swap")
    chip_sum = _add_own_half(g, arrived, core, name=name + "_add2")
    parts = _scatter_ici(chip_sum, name=name + "_ici")
    total = _sum_chips(parts, name=name + "_sum4")
    return _share_d2d(total, name=name + "_share").reshape(rows, cols)


BIG = ("w_in", "w_branch", "w_out", "w_up", "w_down")
BIG_COLUMN_SHARDED = ("w_in", "w_up")
CONV = ("conv_a_w", "conv_f_w")
REPLICATED = ("norm1_w", "b_gate", "conv_a_b", "dt_bias", "a_log", "d_skip", "ssd_norm_w", "uv_b", "v_ln_w",
              "v_ln_b", "w_spatial", "b_spatial", "norm2_w", "conv_f_b", "final_norm_w")
WEIGHT_ORDER = ("norm1_w", "w_in", "b_gate", "conv_a_w", "conv_a_b", "dt_bias", "a_log", "d_skip", "ssd_norm_w",
                "uv_b", "v_ln_w", "v_ln_b", "w_spatial", "b_spatial", "w_branch", "w_out", "norm2_w", "w_up",
                "conv_f_w", "conv_f_b", "w_down", "final_norm_w")
SMALL_EXCHANGE_ROWS = 64


def _flat_rows(arrays, row_multiple):
    flat = jnp.concatenate([a.reshape(-1) for a in arrays])
    rows = -(-flat.shape[0] // (LANES * row_multiple)) * row_multiple
    return jnp.pad(flat, (0, rows * LANES - flat.shape[0])).reshape(rows, LANES)


def _unflatten(flat, shapes):
    flat = flat.reshape(-1)
    out, off = [], 0
    for shp in shapes:
        n = math.prod(shp)
        out.append(flat[off:off + n].reshape(shp))
        off += n
    return out


def _from_chip_blocks(blocks, name):
    if name in BIG_COLUMN_SHARDED or name in CONV:
        k = blocks.shape[1]
        return jnp.transpose(blocks, (1, 0, 2)).reshape(k, -1)
    return blocks.reshape(-1, blocks.shape[-1])


def _to_chip_blocks(whole, name):
    if name in BIG_COLUMN_SHARDED or name in CONV:
        k, n = whole.shape
        return jnp.transpose(whole.reshape(k, N_CHIPS, n // N_CHIPS), (1, 0, 2))
    return whole.reshape(N_CHIPS, whole.shape[0] // N_CHIPS, whole.shape[1])


def kernel(x, norm1_w, w_in, b_gate, conv_a_w, conv_a_b, dt_bias, a_log, d_skip, ssd_norm_w, uv_b, v_ln_w, v_ln_b, w_spatial, b_spatial, w_branch, w_out, norm2_w, w_up, conv_f_w, conv_f_b, w_down, final_norm_w, loss_target, m_norm1_w, m_w_in, m_b_gate, m_conv_a_w, m_conv_a_b, m_dt_bias, m_a_log, m_d_skip, m_ssd_norm_w, m_uv_b, m_v_ln_w, m_v_ln_b, m_w_spatial, m_b_spatial, m_w_branch, m_w_out, m_norm2_w, m_w_up, m_conv_f_w, m_conv_f_b, m_w_down, m_final_norm_w, v_norm1_w, v_w_in, v_b_gate, v_conv_a_w, v_conv_a_b, v_dt_bias, v_a_log, v_d_skip, v_ssd_norm_w, v_uv_b, v_v_ln_w, v_v_ln_b, v_w_spatial, v_b_spatial, v_w_branch, v_w_out, v_norm2_w, v_w_up, v_conv_f_w, v_conv_f_b, v_w_down, v_final_norm_w):
    weights = dict(norm1_w=norm1_w, w_in=w_in, b_gate=b_gate, conv_a_w=conv_a_w, conv_a_b=conv_a_b, dt_bias=dt_bias,
                   a_log=a_log, d_skip=d_skip, ssd_norm_w=ssd_norm_w, uv_b=uv_b, v_ln_w=v_ln_w, v_ln_b=v_ln_b,
                   w_spatial=w_spatial, b_spatial=b_spatial, w_branch=w_branch, w_out=w_out, norm2_w=norm2_w,
                   w_up=w_up, conv_f_w=conv_f_w, conv_f_b=conv_f_b, w_down=w_down, final_norm_w=final_norm_w)
    mom1 = dict(norm1_w=m_norm1_w, w_in=m_w_in, b_gate=m_b_gate, conv_a_w=m_conv_a_w, conv_a_b=m_conv_a_b,
                dt_bias=m_dt_bias, a_log=m_a_log, d_skip=m_d_skip, ssd_norm_w=m_ssd_norm_w, uv_b=m_uv_b,
                v_ln_w=m_v_ln_w, v_ln_b=m_v_ln_b, w_spatial=m_w_spatial, b_spatial=m_b_spatial, w_branch=m_w_branch,
                w_out=m_w_out, norm2_w=m_norm2_w, w_up=m_w_up, conv_f_w=m_conv_f_w, conv_f_b=m_conv_f_b,
                w_down=m_w_down, final_norm_w=m_final_norm_w)
    mom2 = dict(norm1_w=v_norm1_w, w_in=v_w_in, b_gate=v_b_gate, conv_a_w=v_conv_a_w, conv_a_b=v_conv_a_b,
                dt_bias=v_dt_bias, a_log=v_a_log, d_skip=v_d_skip, ssd_norm_w=v_ssd_norm_w, uv_b=v_uv_b,
                v_ln_w=v_v_ln_w, v_ln_b=v_v_ln_b, w_spatial=v_w_spatial, b_spatial=v_b_spatial, w_branch=v_w_branch,
                w_out=v_w_out, norm2_w=v_norm2_w, w_up=v_w_up, conv_f_w=v_conv_f_w, conv_f_b=v_conv_f_b,
                w_down=v_w_down, final_norm_w=v_final_norm_w)
    chip = 2 * lax.axis_index("x") + lax.axis_index("y")
    core = lax.axis_index("c").astype(jnp.int32).reshape(1)

    whole = {}
    conv_shapes = [weights[n].shape[1:] for n in CONV]
    conv_gathered = _all_gather_chips(_flat_rows([weights[n] for n in CONV], 16), "gather_conv").reshape(N_CHIPS, -1)
    off = 0
    for n, shp in zip(CONV, conv_shapes):
        size = math.prod(shp)
        whole[n] = _from_chip_blocks(conv_gathered[:, off:off + size].reshape((N_CHIPS,) + shp), n)
        off += size
    shard_shapes = {n: weights[n].shape[1:] for n in BIG}
    halves = [weights[n][0].astype(BF16).reshape(2, shard_shapes[n][0] // 2, shard_shapes[n][1]) for n in BIG]
    gathers, _ = _exchange_start(halves, [(N_CHIPS,) + h.shape for h in halves], _gather_sends, after=conv_gathered,
                                 name="gather_start")
    gathers = dict(zip(BIG, gathers))

    def get_weight(name, after):
        rows, cols = shard_shapes[name]
        own, landed = _exchange_wait(gathers[name], after, _gather_sends, _gather_arrivals,
                                     name="gather_" + name + "_wait")
        others = _gather_d2d(landed, name="gather_" + name + "_d2d").reshape(N_CHIPS, rows, cols)
        blocks = lax.dynamic_update_slice(others, own.reshape(1, rows, cols), (chip, 0, 0))
        if name == "w_up":
            return {"up": blocks}
        full = _from_chip_blocks(blocks, name)
        if name == "w_branch":
            return {"branch_a": full[:SSD_D_INNER], "branch_b": full[SSD_D_INNER:]}
        if name != "w_in":
            return {name[2:]: full}
        gate0 = SSD_IN + 2 * SGU_WIDTH
        return {
            "in_z": full[:, :SSD_D_INNER],
            "in_xbc": full[:, SSD_D_INNER:SSD_D_INNER + SSD_XBC],
            "in_dt": jnp.pad(full[:, SSD_D_INNER + SSD_XBC:SSD_IN], ((0, 0), (0, LANES - SSD_HEADS))),
            "in_uv": full[:, SSD_IN:gate0],
            "in_gate": full[:, gate0:], "in_gate_a": full[:, gate0:gate0 + D_MODEL], "in_gate_b": full[:, gate0 + D_MODEL:],
        }

    small = {n: weights[n] for n in REPLICATED}
    small["conv_a_w"] = whole["conv_a_w"]
    small["conv_f_w"] = whole["conv_f_w"]

    reductions = {}

    def emit_grad(name, g):
        g_blocks = g if name == "w_up" else _to_chip_blocks(g, name)
        _, rows, cols = g_blocks.shape
        g_halves = g_blocks.reshape(N_CHIPS, 2, rows // 2, cols)
        arrived = _swap_halves_d2d(g_halves, name="reduce_" + name + "_swap")
        chip_sum = _add_own_half(g_halves, arrived, core, name="reduce_" + name + "_add2")
        own = lax.dynamic_slice_in_dim(chip_sum, chip, 1, axis=0)
        (pending,), started = _exchange_start([chip_sum], [chip_sum.shape], _scatter_sends,
                                              name="reduce_" + name + "_start")
        reductions[name] = (pending, own)
        return started

    loss, dx, grads_small = _local_step(x[0], loss_target[0], get_weight, small, emit_grad)

    order = ("w_down", "w_up", "w_out", "w_branch", "w_in")
    totals = []
    for n in order:
        pending, own = reductions[n]
        _, landed = _exchange_wait(pending, dx, _scatter_sends, _scatter_arrivals, name="reduce_" + n + "_wait")
        parts = lax.dynamic_update_slice(landed, own, (chip, 0, 0))
        totals.append(_sum_chips(parts, name="reduce_" + n + "_sum4"))
    grads = {n: both.reshape(shard_shapes[n]) for n, both in zip(order, _share_d2d(totals, name="reduce_share"))}

    small_names = REPLICATED + CONV
    small_shapes = [grads_small[n].shape for n in small_names]
    g_small = _flat_rows([grads_small[n] for n in small_names], N_CHIPS * 2 * SMALL_EXCHANGE_ROWS)
    red_small = _reduce_scatter_chips(g_small.reshape(N_CHIPS, -1, LANES), core, "reduce_small")
    all_small = _all_gather_chips(red_small, "gather_small")
    for n, g in zip(small_names, _unflatten(all_small, small_shapes)):
        if n in CONV:
            width = g.shape[1] // N_CHIPS
            g = lax.dynamic_slice_in_dim(g, chip * width, width, axis=1)
        grads[n] = g.reshape(weights[n].shape[1:]) if n != "final_norm_w" else g

    delta, new_m, new_v = {}, {}, {}
    for n in BIG:
        shp = weights[n].shape
        operands = [weights[n][0], grads[n], mom1[n][0], mom2[n][0]]
        if n == "w_in":
            operands = [a.T for a in operands]
            grads[n] = operands[1].T
        tr = _row_tile(operands[0].shape[0], 8, 136)
        results = _adamw(*operands, name="adamw_" + n, tr=tr)
        if n == "w_in":
            results = [a.T for a in results]
        delta[n], new_m[n], new_v[n] = [a.reshape(shp) for a in results]
    small_all = [n for n in WEIGHT_ORDER if n not in BIG]

    def as_2d(a):
        return a.reshape(-1, a.shape[-1])

    results = _adamw_many(*[[as_2d(src[n]) for n in small_all] for src in (weights, grads, mom1, mom2)],
                          name="adamw_small")
    for n, dv, mv, vv in zip(small_all, *results):
        shp = weights[n].shape
        delta[n], new_m[n], new_v[n] = dv.reshape(shp), mv.reshape(shp), vv.reshape(shp)

    total_loss = lax.psum(loss[0, 0], ("x", "y", "c"))
    grad_out = [grads[n].reshape(weights[n].shape) for n in WEIGHT_ORDER]
    return (total_loss, dx[None], *grad_out, *[delta[n] for n in WEIGHT_ORDER], *[new_m[n] for n in WEIGHT_ORDER],
            *[new_v[n] for n in WEIGHT_ORDER])
```

```python
import functools
import math

import jax
import jax.numpy as jnp
from jax import lax
from jax.experimental import pallas as pl
from jax.experimental.pallas import tpu as pltpu

F32 = jnp.float32
BF16 = jnp.bfloat16
HI = lax.Precision.HIGHEST

D_MODEL = 1024
SSD_D_INNER = 2048
SSD_HEADS = 32
SSD_HEAD_DIM = 64
SSD_GROUPS = 4
SSD_HEADS_PER_GROUP = 8
SSD_STATE = 128
SSD_BC = 512
SSD_XBC = 3072
SSD_IN = 5152
SGU_WIDTH = 1024
SGU_GROUPS = 8
CHUNK = 128
IN_COLS = 9248
D_FF = 2816
NORM_EPS = 1e-6
LN_EPS = 1e-5
GROUP_COLS = SSD_HEADS_PER_GROUP * SSD_HEAD_DIM
LANES = 128

ADAM_LR = 0.001
ADAM_B1 = 0.9
ADAM_B2 = 0.999
ADAM_EPS = 1e-08
ADAM_WD = 0.01
ADAM_STEP = 10

N_CHIPS = 4
VMEM_LIMIT = 56 * 1024 * 1024

NT = (((1,), (1,)), ((), ()))
TN = (((0,), (0,)), ((), ()))
NN = (((1,), (0,)), ((), ()))


def _params(dims):
    return pltpu.CompilerParams(dimension_semantics=dims, vmem_limit_bytes=VMEM_LIMIT)


def _dot(a, b, dn=NN, precision=None):
    return lax.dot_general(a, b, dn, precision=precision, preferred_element_type=F32)


def _split3(x):
    hi = x.astype(BF16)
    rest = x - hi.astype(F32)
    mid = rest.astype(BF16)
    return hi, mid, (rest - mid.astype(F32)).astype(BF16)


def _dot_terms(terms, exact, dn=NN):
    out = None
    for t in terms:
        p = _dot(t, exact, dn)
        out = p if out is None else out + p
    return out


def _dot_exact_lhs(exact, terms):
    out = None
    for t in terms:
        p = _dot(exact, t)
        out = p if out is None else out + p
    return out


def _sigmoid(x):
    return 1.0 / (1.0 + jnp.exp(-x))


def _softplus(x):
    return jnp.maximum(x, 0.0) + jnp.log(1.0 + jnp.exp(-jnp.abs(x)))


def _matmul(pairs, *, trans_b=False, add=None, after=None, out_dtype=F32, tm=512, tn=512, name):
    def mat_shape(b):
        if isinstance(b, tuple) and b[1] == "cols":
            return (b[0].shape[1], b[0].shape[0] * b[0].shape[2])
        return b[0].shape[1:] if isinstance(b, tuple) else b.shape

    if isinstance(pairs[0][1], tuple) and pairs[0][1][1] == "cols":
        assert not trans_b and tn % LANES == 0 and pairs[0][1][0].shape[2] % tn == 0, name

    m = (pairs[0][0][0] if isinstance(pairs[0][0], tuple) else pairs[0][0]).shape[0]
    n = mat_shape(pairs[0][1])[0] if trans_b else mat_shape(pairs[0][1])[1]
    tm, tn = min(tm, m), min(tn, n)
    assert m % tm == 0 and n % tn == 0, (name, m, n, tm, tn)
    npairs = len(pairs)
    dn = NT if trans_b else NN

    def body(*refs):
        o_ref = refs[-1]
        acc = None
        for i in range(npairs):
            p = _dot(refs[2 * i][...].astype(BF16), refs[2 * i + 1][...].astype(BF16), dn)
            acc = p if acc is None else acc + p
        if add is not None:
            acc = acc + refs[2 * npairs][...]
        o_ref[...] = acc.astype(out_dtype)

    in_specs, args = [], []
    for a, b in pairs:
        bshape = mat_shape(b)
        k = bshape[1] if trans_b else bshape[0]
        assert bshape == ((n, k) if trans_b else (k, n)), (name, bshape)
        a, qa = a if isinstance(a, tuple) else (a, 0)
        assert a.shape[0] == m and a.shape[1] % k == 0, (name, a.shape, k)
        in_specs.append(pl.BlockSpec((tm, k), lambda i, j, qa=qa: (i, qa)))
        if isinstance(b, tuple) and b[1] == "cols":
            b = b[0]
            per = b.shape[2] // tn
            in_specs.append(pl.BlockSpec((None, k, tn), lambda i, j, per=per: (j // per, 0, j % per)))
        elif isinstance(b, tuple):
            b, qb = b
            if trans_b:
                in_specs.append(pl.BlockSpec((None, tn, k), lambda i, j, qb=qb: (qb, j, 0)))
            else:
                in_specs.append(pl.BlockSpec((None, k, tn), lambda i, j, qb=qb: (qb, 0, j)))
        elif trans_b:
            in_specs.append(pl.BlockSpec((tn, k), lambda i, j: (j, 0)))
        else:
            in_specs.append(pl.BlockSpec((k, tn), lambda i, j: (0, j)))
        args += [a, b]
    if add is not None:
        in_specs.append(pl.BlockSpec((tm, tn), lambda i, j: (i, j)))
        args.append(add)
    if after is not None:
        in_specs.append(pl.BlockSpec(memory_space=pl.ANY))
        args.append(after)
    return pl.pallas_call(
        body, name=name, grid=(m // tm, n // tn), in_specs=in_specs,
        out_specs=pl.BlockSpec((tm, tn), lambda i, j: (i, j)),
        out_shape=jax.ShapeDtypeStruct((m, n), out_dtype),
        compiler_params=_params(("parallel", "parallel")),
    )(*args)


def _matmul_tn(a, b, *, tk, tn, tm=1024, out_dtype=BF16, stack_out=False, after=None, name):
    m, k = a.shape
    n = b.shape[1]
    tm, tk, tn = min(tm, m), min(tk, k), min(tn, n)
    assert m % tm == 0 and k % tk == 0 and n % tn == 0, (name, m, k, n)
    nm = m // tm
    if stack_out:
        out_spec = pl.BlockSpec((None, tk, tn), lambda i, j, l: (j, i, 0))
        out_shape = jax.ShapeDtypeStruct((n // tn, k, tn), out_dtype)
    else:
        out_spec = pl.BlockSpec((tk, tn), lambda i, j, l: (i, j))
        out_shape = jax.ShapeDtypeStruct((k, n), out_dtype)

    def body(a_ref, b_ref, *rest):
        o_ref, acc = rest[-2:]
        mi = pl.program_id(2)

        @pl.when(mi == 0)
        def _():
            acc[...] = jnp.zeros_like(acc)

        acc[...] += _dot(a_ref[...].astype(BF16), b_ref[...].astype(BF16), TN)

        @pl.when(mi == nm - 1)
        def _():
            o_ref[...] = acc[...].astype(out_dtype)

    in_specs = [pl.BlockSpec((tm, tk), lambda i, j, l: (l, i)), pl.BlockSpec((tm, tn), lambda i, j, l: (l, j))]
    args = [a, b]
    if after is not None:
        in_specs.append(pl.BlockSpec(memory_space=pl.ANY))
        args.append(after)
    return pl.pallas_call(
        body, name=name, grid=(k // tk, n // tn, nm), in_specs=in_specs,
        out_specs=out_spec, out_shape=out_shape,
        scratch_shapes=[pltpu.VMEM((tk, tn), F32)],
        compiler_params=_params(("parallel", "parallel", "arbitrary")),
    )(*args)


def _rms_fwd(x, w, *, name, tm=512):
    s, d = x.shape
    tm = min(tm, s)

    def body(x_ref, w_ref, o_ref):
        xv = x_ref[...]
        r = lax.rsqrt(jnp.mean(xv * xv, axis=-1, keepdims=True) + NORM_EPS)
        o_ref[...] = (xv * r * w_ref[...]).astype(BF16)

    return pl.pallas_call(
        body, name=name, grid=(s // tm,),
        in_specs=[pl.BlockSpec((tm, d), lambda i: (i, 0)), pl.BlockSpec((1, d), lambda i: (0, 0))],
        out_specs=pl.BlockSpec((tm, d), lambda i: (i, 0)),
        out_shape=jax.ShapeDtypeStruct((s, d), BF16),
        compiler_params=_params(("parallel",)),
    )(x, w)


def _rms_bwd(x, w, dn, dres, *, name, tm=512):
    s, d = x.shape
    tm = min(tm, s)

    def body(x_ref, w_ref, dn_ref, dres_ref, dx_ref, dxb_ref, dw_ref):
        @pl.when(pl.program_id(0) == 0)
        def _():
            dw_ref[...] = jnp.zeros_like(dw_ref)

        xv = x_ref[...]
        r = lax.rsqrt(jnp.mean(xv * xv, axis=-1, keepdims=True) + NORM_EPS)
        xhat = xv * r
        dnv = dn_ref[...]
        dxhat = dnv * w_ref[...]
        dx = dres_ref[...] + r * (dxhat - xhat * jnp.mean(dxhat * xhat, axis=-1, keepdims=True))
        dx_ref[...] = dx
        dxb_ref[...] = dx.astype(BF16)
        dw_ref[...] += jnp.sum(dnv * xhat, axis=0, keepdims=True)

    tile = pl.BlockSpec((tm, d), lambda i: (i, 0))
    row = pl.BlockSpec((1, d), lambda i: (0, 0))
    return pl.pallas_call(
        body, name=name, grid=(s // tm,),
        in_specs=[tile, row, tile, tile], out_specs=[tile, tile, row],
        out_shape=[jax.ShapeDtypeStruct((s, d), F32), jax.ShapeDtypeStruct((s, d), BF16),
                   jax.ShapeDtypeStruct((1, d), F32)],
        compiler_params=_params(("arbitrary",)),
    )(x, w, dn, dres)


def _final_fwd_bwd(h2, wf, target, *, name, tm=512):
    s, d = h2.shape
    tm = min(tm, s)

    def body(h_ref, w_ref, t_ref, loss_ref, dh_ref, dhb_ref, dw_ref):
        @pl.when(pl.program_id(0) == 0)
        def _():
            dw_ref[...] = jnp.zeros_like(dw_ref)
            loss_ref[...] = jnp.zeros_like(loss_ref)

        hv = h_ref[...]
        r = lax.rsqrt(jnp.mean(hv * hv, axis=-1, keepdims=True) + NORM_EPS)
        xhat = hv * r
        err = xhat * w_ref[...] - t_ref[...]
        per_tok = jnp.mean(err * err, axis=-1, keepdims=True)
        loss_ref[...] += 0.5 * jnp.sum(per_tok, axis=0, keepdims=True)
        dy = err * (1.0 / d)
        dxhat = dy * w_ref[...]
        dh = r * (dxhat - xhat * jnp.mean(dxhat * xhat, axis=-1, keepdims=True))
        dh_ref[...] = dh
        dhb_ref[...] = dh.astype(BF16)
        dw_ref[...] += jnp.sum(dy * xhat, axis=0, keepdims=True)

    tile = pl.BlockSpec((tm, d), lambda i: (i, 0))
    row = pl.BlockSpec((1, d), lambda i: (0, 0))
    return pl.pallas_call(
        body, name=name, grid=(s // tm,),
        in_specs=[tile, row, tile],
        out_specs=[pl.BlockSpec((1, 1), lambda i: (0, 0)), tile, tile, row],
        out_shape=[jax.ShapeDtypeStruct((1, 1), F32), jax.ShapeDtypeStruct((s, d), F32),
                   jax.ShapeDtypeStruct((s, d), BF16), jax.ShapeDtypeStruct((1, d), F32)],
        compiler_params=_params(("arbitrary",)),
    )(h2, wf, target)


CONV_ROWS = 512
HALO = 8


def _rows_with_halo(ref, r0, rows, s, before, after):
    parts = []
    if before:
        prev = ref[pl.ds(pl.multiple_of(jnp.maximum(r0 - HALO, 0), HALO), HALO), :]
        parts.append(jnp.where(r0 > 0, prev, 0.0))
    parts.append(ref[pl.ds(r0, rows), :])
    if after:
        nxt = ref[pl.ds(pl.multiple_of(jnp.minimum(r0 + rows, s - HALO), HALO), HALO), :]
        parts.append(jnp.where(r0 + rows < s, nxt, 0.0))
    return jnp.concatenate(parts, axis=0) if len(parts) > 1 else parts[0]


def _earlier(xe, k, rows):
    if k == 0:
        return xe[HALO:HALO + rows]
    return pltpu.roll(xe, k, 0)[HALO:HALO + rows]


def _later(ve, k, rows):
    if k == 0:
        return ve[:rows]
    return pltpu.roll(ve, ve.shape[0] - k, 0)[:rows]


def _conv_taps(xe, w_ref, kk, rows):
    acc = None
    for i in range(kk):
        term = w_ref[i:i + 1, :] * _earlier(xe, kk - 1 - i, rows)
        acc = term if acc is None else acc + term
    return acc


def _row_loop(s, step):
    def body(r, carry):
        return step(pl.multiple_of(r * CONV_ROWS, CONV_ROWS), carry)
    return body


def _conv_bwd_rows(xe, dpe, w_ref, kk):
    dp = dpe[:CONV_ROWS]
    dx = None
    dws = []
    for i in range(kk):
        dws.append(jnp.sum(dp * _earlier(xe, kk - 1 - i, CONV_ROWS), axis=0, keepdims=True))
        term = w_ref[i:i + 1, :] * _later(dpe, kk - 1 - i, CONV_ROWS)
        dx = term if dx is None else dx + term
    return dx, dws, jnp.sum(dp, axis=0, keepdims=True)


def _conv_a_fwd(xraw, w, b, *, name, tc=128):
    s, c = xraw.shape
    kk = 4

    def body(x_ref, w_ref, b_ref, o_ref):
        def step(r0, carry):
            xe = _rows_with_halo(x_ref, r0, CONV_ROWS, s, True, False)
            pre = _conv_taps(xe, w_ref, kk, CONV_ROWS) + b_ref[...]
            o_ref[pl.ds(r0, CONV_ROWS), :] = pre * _sigmoid(pre)
            return carry

        lax.fori_loop(0, s // CONV_ROWS, _row_loop(s, step), 0)

    col = pl.BlockSpec((s, tc), lambda j: (0, j))
    return pl.pallas_call(
        body, name=name, grid=(c // tc,),
        in_specs=[col, pl.BlockSpec((8, tc), lambda j: (0, j)), pl.BlockSpec((1, tc), lambda j: (0, j))],
        out_specs=col, out_shape=jax.ShapeDtypeStruct((s, c), F32),
        compiler_params=_params(("parallel",)),
    )(xraw, w, b)


def _conv_a_bwd(xraw, w, b, dy, *, name, tc=128):
    s, c = xraw.shape
    kk = 4

    def body(x_ref, w_ref, b_ref, dy_ref, dx_ref, dw_ref, db_ref):
        def step(r0, carry):
            xe = _rows_with_halo(x_ref, r0, CONV_ROWS, s, True, True)
            pre = _conv_taps(xe, w_ref, kk, CONV_ROWS + HALO) + b_ref[...]
            sg = _sigmoid(pre)
            dpe = _rows_with_halo(dy_ref, r0, CONV_ROWS, s, False, True) * (sg * (1.0 + pre * (1.0 - sg)))
            dx, dws, db = _conv_bwd_rows(xe, dpe, w_ref, kk)
            dx_ref[pl.ds(r0, CONV_ROWS), :] = dx.astype(BF16)
            return tuple(acc + new for acc, new in zip(carry, dws + [db]))

        zero = jnp.zeros((1, tc), F32)
        sums = lax.fori_loop(0, s // CONV_ROWS, _row_loop(s, step), (zero,) * (kk + 1))
        db_ref[...] = sums[kk]
        dw_ref[...] = jnp.concatenate(list(sums[:kk]) + [jnp.zeros((8 - kk, tc), F32)], axis=0)

    col = pl.BlockSpec((s, tc), lambda j: (0, j))
    w8 = pl.BlockSpec((8, tc), lambda j: (0, j))
    row = pl.BlockSpec((1, tc), lambda j: (0, j))
    return pl.pallas_call(
        body, name=name, grid=(c // tc,),
        in_specs=[col, w8, row, col], out_specs=[col, w8, row],
        out_shape=[jax.ShapeDtypeStruct((s, c), BF16), jax.ShapeDtypeStruct((8, c), F32),
                   jax.ShapeDtypeStruct((1, c), F32)],
        compiler_params=_params(("parallel",)),
    )(xraw, w, b, dy)


def _conv_f_fwd(up_raw, w, b, *, name, tc=128):
    s, c2 = up_raw.shape
    c = c2 // 2
    nb = c // tc
    kk = 3

    def body(xa_ref, xv_ref, wa_ref, wv_ref, ba_ref, bv_ref, o_ref):
        def step(r0, carry):
            a = _conv_taps(_rows_with_halo(xa_ref, r0, CONV_ROWS, s, True, False), wa_ref, kk, CONV_ROWS) + ba_ref[...]
            v = _conv_taps(_rows_with_halo(xv_ref, r0, CONV_ROWS, s, True, False), wv_ref, kk, CONV_ROWS) + bv_ref[...]
            o_ref[pl.ds(r0, CONV_ROWS), :] = (a * _sigmoid(a) * v).astype(BF16)
            return carry

        lax.fori_loop(0, s // CONV_ROWS, _row_loop(s, step), 0)

    col_a = pl.BlockSpec((s, tc), lambda j: (0, j))
    col_v = pl.BlockSpec((s, tc), lambda j: (0, j + nb))
    return pl.pallas_call(
        body, name=name, grid=(nb,),
        in_specs=[col_a, col_v, pl.BlockSpec((8, tc), lambda j: (0, j)), pl.BlockSpec((8, tc), lambda j: (0, j + nb)),
                  pl.BlockSpec((1, tc), lambda j: (0, j)), pl.BlockSpec((1, tc), lambda j: (0, j + nb))],
        out_specs=col_a, out_shape=jax.ShapeDtypeStruct((s, c), BF16),
        compiler_params=_params(("parallel",)),
    )(up_raw, up_raw, w, w, b, b)


def _conv_f_bwd(up_raw, w, b, dact, *, name, tc=128):
    s, c2 = up_raw.shape
    c = c2 // 2
    nb = c // tc
    kk = 3

    def body(xa_ref, xv_ref, wa_ref, wv_ref, ba_ref, bv_ref, d_ref,
             dxa_ref, dxv_ref, dwa_ref, dwv_ref, dba_ref, dbv_ref):
        def step(r0, carry):
            xae = _rows_with_halo(xa_ref, r0, CONV_ROWS, s, True, True)
            xve = _rows_with_halo(xv_ref, r0, CONV_ROWS, s, True, True)
            a = _conv_taps(xae, wa_ref, kk, CONV_ROWS + HALO) + ba_ref[...]
            v = _conv_taps(xve, wv_ref, kk, CONV_ROWS + HALO) + bv_ref[...]
            sg = _sigmoid(a)
            d = _rows_with_halo(d_ref, r0, CONV_ROWS, s, False, True)
            dxa, dwas, dba = _conv_bwd_rows(xae, d * v * (sg * (1.0 + a * (1.0 - sg))), wa_ref, kk)
            dxv, dwvs, dbv = _conv_bwd_rows(xve, d * (a * sg), wv_ref, kk)
            dxa_ref[pl.ds(r0, CONV_ROWS), :] = dxa.astype(BF16)
            dxv_ref[pl.ds(r0, CONV_ROWS), :] = dxv.astype(BF16)
            return tuple(acc + new for acc, new in zip(carry, dwas + [dba] + dwvs + [dbv]))

        zero = jnp.zeros((1, tc), F32)
        sums = lax.fori_loop(0, s // CONV_ROWS, _row_loop(s, step), (zero,) * (2 * kk + 2))
        pad = [jnp.zeros((8 - kk, tc), F32)]
        dwa_ref[...] = jnp.concatenate(list(sums[:kk]) + pad, axis=0)
        dba_ref[...] = sums[kk]
        dwv_ref[...] = jnp.concatenate(list(sums[kk + 1:2 * kk + 1]) + pad, axis=0)
        dbv_ref[...] = sums[2 * kk + 1]

    col_a = pl.BlockSpec((s, tc), lambda j: (0, j))
    col_v = pl.BlockSpec((s, tc), lambda j: (0, j + nb))
    w_a = pl.BlockSpec((8, tc), lambda j: (0, j))
    w_v = pl.BlockSpec((8, tc), lambda j: (0, j + nb))
    r_a = pl.BlockSpec((1, tc), lambda j: (0, j))
    r_v = pl.BlockSpec((1, tc), lambda j: (0, j + nb))
    outs = pl.pallas_call(
        body, name=name, grid=(nb,),
        in_specs=[col_a, col_v, w_a, w_v, r_a, r_v, col_a],
        out_specs=[col_a, col_a, w_a, w_a, r_a, r_a],
        out_shape=[jax.ShapeDtypeStruct((s, c), BF16), jax.ShapeDtypeStruct((s, c), BF16),
                   jax.ShapeDtypeStruct((8, c), F32), jax.ShapeDtypeStruct((8, c), F32),
                   jax.ShapeDtypeStruct((1, c), F32), jax.ShapeDtypeStruct((1, c), F32)],
        compiler_params=_params(("parallel",)),
    )(up_raw, up_raw, w, w, b, b, dact)
    return outs


def _tri_masks():
    row = lax.broadcasted_iota(jnp.int32, (CHUNK, CHUNK), 0)
    col = lax.broadcasted_iota(jnp.int32, (CHUNK, CHUNK), 1)
    return row >= col, row <= col


def _ssd_fwd(xbc, dt_raw, z, dt_bias, a_log, a_log_x, d_skip_x, norm_w, expand, *, name):
    s = xbc.shape[0]
    nc = s // CHUNK

    def body(xbc_ref, dtr_ref, z_ref, dtb_ref, alog_ref, alogx_ref, dskx_ref, nw_ref, e_ref,
             y_ref, ya_ref, st_ref, state):
        @pl.when(pl.program_id(0) == 0)
        def _():
            state[...] = jnp.zeros_like(state)

        st_ref[0] = state[...]
        lower, _ = _tri_masks()
        dt = _softplus(dtr_ref[...] + dtb_ref[...])
        adt = dt * (-jnp.exp(alog_ref[...]))
        acum = _dot_exact_lhs(lower.astype(BF16), _split3(adt))
        acum_t = acum.T
        dt_terms, acum_terms = _split3(dt), _split3(acum)
        for g in range(SSD_GROUPS):
            sl = slice(GROUP_COLS * g, GROUP_COLS * (g + 1))
            dt_x = _dot_terms(dt_terms, e_ref[:, sl])
            acum_x = _dot_terms(acum_terms, e_ref[:, sl])
            tot_x = jnp.sum(dt_x * (-jnp.exp(alogx_ref[:, sl])), axis=0, keepdims=True)
            xs = xbc_ref[:, sl]
            xdt = xs * dt_x
            xdt_b = xdt.astype(BF16)
            bg = xbc_ref[:, SSD_D_INNER + SSD_STATE * g:SSD_D_INNER + SSD_STATE * (g + 1)].astype(BF16)
            cg = xbc_ref[:, SSD_D_INNER + SSD_BC + SSD_STATE * g:SSD_D_INNER + SSD_BC + SSD_STATE * (g + 1)].astype(BF16)
            cb = _dot(cg, bg, NT)
            st_g = state[:, sl]
            y_off = _dot(cg, st_g.astype(BF16)) * jnp.exp(acum_x)
            parts = []
            for r in range(SSD_HEADS_PER_GROUP):
                h = SSD_HEADS_PER_GROUP * g + r
                dec = jnp.exp(jnp.where(lower, acum[:, h:h + 1] - acum_t[h:h + 1, :], -jnp.inf))
                parts.append(_dot((cb * dec).astype(BF16), xdt_b[:, SSD_HEAD_DIM * r:SSD_HEAD_DIM * (r + 1)]))
            y_ref[:, sl] = jnp.concatenate(parts, axis=1) + y_off + dskx_ref[:, sl] * xs
            wgt = (xdt * jnp.exp(tot_x - acum_x)).astype(BF16)
            state[:, sl] = st_g * jnp.exp(tot_x) + _dot(bg, wgt, TN)
        zv = z_ref[...]
        q = y_ref[...] * (zv * _sigmoid(zv))
        r = lax.rsqrt(jnp.mean(q * q, axis=-1, keepdims=True) + NORM_EPS)
        ya_ref[...] = (q * r * nw_ref[...]).astype(BF16)

    def chunk(w):
        return pl.BlockSpec((CHUNK, w), lambda c: (c, 0))

    def const(shape):
        return pl.BlockSpec(shape, lambda c: (0,) * len(shape))

    return pl.pallas_call(
        body, name=name, grid=(nc,),
        in_specs=[chunk(SSD_XBC), chunk(LANES), chunk(SSD_D_INNER), const((1, LANES)), const((1, LANES)),
                  const((1, SSD_D_INNER)), const((1, SSD_D_INNER)), const((1, SSD_D_INNER)),
                  const((LANES, SSD_D_INNER))],
        out_specs=[chunk(SSD_D_INNER), chunk(SSD_D_INNER),
                   pl.BlockSpec((1, SSD_STATE, SSD_D_INNER), lambda c: (c, 0, 0))],
        out_shape=[jax.ShapeDtypeStruct((s, SSD_D_INNER), F32), jax.ShapeDtypeStruct((s, SSD_D_INNER), BF16),
                   jax.ShapeDtypeStruct((nc, SSD_STATE, SSD_D_INNER), F32)],
        scratch_shapes=[pltpu.VMEM((SSD_STATE, SSD_D_INNER), F32)],
        compiler_params=_params(("arbitrary",)),
    )(xbc, dt_raw, z, dt_bias, a_log, a_log_x, d_skip_x, norm_w, expand)


def _ssd_bwd(dya, y, z, xbc, dt_raw, states, dt_bias, a_log, a_log_x, d_skip_x, norm_w, expand, expand_t, *, name):
    s = xbc.shape[0]
    nc = s // CHUNK

    def body(dya_ref, y_ref, z_ref, xbc_ref, dtr_ref, stp_ref, dtb_ref, alog_ref, alogx_ref, dskx_ref, nw_ref,
             e_ref, et_ref, dz_ref, dxbc_ref, ddt_ref, dnw_ref, ddsk_ref, dalog_ref, ddtb_ref,
             dstate, dy_sc, dskcol):
        i = pl.program_id(0)

        @pl.when(i == 0)
        def _():
            dstate[...] = jnp.zeros_like(dstate)
            dskcol[...] = jnp.zeros_like(dskcol)
            dnw_ref[...] = jnp.zeros_like(dnw_ref)
            dalog_ref[...] = jnp.zeros_like(dalog_ref)
            ddtb_ref[...] = jnp.zeros_like(ddtb_ref)
            ddsk_ref[...] = jnp.zeros_like(ddsk_ref)

        lower, upper = _tri_masks()
        rows = lax.broadcasted_iota(jnp.int32, (CHUNK, LANES), 0)
        pre = dtr_ref[...] + dtb_ref[...]
        dt = _softplus(pre)
        a = -jnp.exp(alog_ref[...])
        acum = _dot_exact_lhs(lower.astype(BF16), _split3(dt * a))
        acum_t = acum.T
        dt_terms, acum_terms = _split3(dt), _split3(acum)

        yv = y_ref[...]
        zv = z_ref[...]
        sz = _sigmoid(zv)
        silu_z = zv * sz
        q = yv * silu_z
        r = lax.rsqrt(jnp.mean(q * q, axis=-1, keepdims=True) + NORM_EPS)
        qhat = q * r
        dyav = dya_ref[...]
        dqhat = dyav * nw_ref[...]
        dnw_ref[...] += jnp.sum(dyav * qhat, axis=0, keepdims=True)
        dq = r * (dqhat - qhat * jnp.mean(dqhat * qhat, axis=-1, keepdims=True))
        dy_sc[...] = dq * silu_z
        dz_ref[...] = (dq * yv * (sz * (1.0 + zv * (1.0 - sz)))).astype(BF16)

        da_cum = jnp.zeros((CHUNK, LANES), F32)
        ddt = jnp.zeros((CHUNK, LANES), F32)
        for g in range(SSD_GROUPS):
            sl = slice(GROUP_COLS * g, GROUP_COLS * (g + 1))
            et_g = et_ref[sl, :]
            dt_x = _dot_terms(dt_terms, e_ref[:, sl])
            acum_x = _dot_terms(acum_terms, e_ref[:, sl])
            tot_x = jnp.sum(dt_x * (-jnp.exp(alogx_ref[:, sl])), axis=0, keepdims=True)
            e_tot = jnp.exp(tot_x)
            dec_s = jnp.exp(tot_x - acum_x)
            xs = xbc_ref[:, sl]
            xdt = xs * dt_x
            xdt_b = xdt.astype(BF16)
            dy = dy_sc[:, sl]
            dy_b = dy.astype(BF16)
            dskx = dskx_ref[:, sl]
            y_ssd = y_ref[:, sl] - dskx * xs
            dskcol[:, sl] += jnp.sum(dy * xs, axis=0, keepdims=True)
            bg = xbc_ref[:, SSD_D_INNER + SSD_STATE * g:SSD_D_INNER + SSD_STATE * (g + 1)].astype(BF16)
            cg = xbc_ref[:, SSD_D_INNER + SSD_BC + SSD_STATE * g:SSD_D_INNER + SSD_BC + SSD_STATE * (g + 1)].astype(BF16)
            cb_t = _dot(bg, cg, NT)
            sp = stp_ref[0, :, sl]
            ds_g = dstate[:, sl]
            ds_b = ds_g.astype(BF16)
            dye_b = (dy * jnp.exp(acum_x)).astype(BF16)
            dc = _dot(dye_b, sp.astype(BF16), NT)
            dxdt_state = dec_s * _dot(bg, ds_b)
            db = _dot((xdt * dec_s).astype(BF16), ds_b, NT)
            dcb_t = jnp.zeros((CHUNK, CHUNK), F32)
            parts = []
            for rr in range(SSD_HEADS_PER_GROUP):
                h = SSD_HEADS_PER_GROUP * g + rr
                hs = slice(SSD_HEAD_DIM * rr, SSD_HEAD_DIM * (rr + 1))
                dec_t = jnp.exp(jnp.where(upper, acum_t[h:h + 1, :] - acum[:, h:h + 1], -jnp.inf))
                parts.append(_dot((cb_t * dec_t).astype(BF16), dy_b[:, hs]))
                dcb_t = dcb_t + _dot(xdt_b[:, hs], dy_b[:, hs], NT) * dec_t
            dxdt = jnp.concatenate(parts, axis=1) + dxdt_state
            dcb_tb = dcb_t.astype(BF16)
            dc = dc + _dot(dcb_tb, bg, TN)
            db = db + _dot(dcb_tb, cg)
            tot_col = jnp.sum(ds_g * sp, axis=0, keepdims=True) * e_tot + jnp.sum(dxdt_state * xdt, axis=0, keepdims=True)
            d_tot = _dot_terms(_split3(jnp.broadcast_to(tot_col, (8, GROUP_COLS))), et_g)
            d_tot = jnp.max(d_tot, axis=0, keepdims=True)
            pair_sums = dy_b.astype(F32) * y_ssd - xdt_b.astype(F32) * dxdt
            da_cum = da_cum + _dot_terms(_split3(pair_sums), et_g) + jnp.where(rows == CHUNK - 1, d_tot, 0.0)
            ddt = ddt + _dot_terms(_split3(dxdt * xs), et_g)
            dxbc_ref[:, sl] = dy * dskx + dxdt * dt_x
            dxbc_ref[:, SSD_D_INNER + SSD_STATE * g:SSD_D_INNER + SSD_STATE * (g + 1)] = db
            dxbc_ref[:, SSD_D_INNER + SSD_BC + SSD_STATE * g:SSD_D_INNER + SSD_BC + SSD_STATE * (g + 1)] = dc
            dstate[:, sl] = e_tot * ds_g + _dot(cg, dye_b, TN)

        dadt = _dot_exact_lhs(upper.astype(BF16), _split3(da_cum))
        ddt = ddt + dadt * a
        dalog_ref[...] += jnp.sum(dadt * dt, axis=0, keepdims=True)
        dpre = ddt * _sigmoid(pre)
        ddtb_ref[...] += jnp.sum(dpre, axis=0, keepdims=True)
        ddt_ref[...] = dpre.astype(BF16)

        @pl.when(i == nc - 1)
        def _():
            dalog_ref[...] = dalog_ref[...] * a
            dsk = _dot_terms(_split3(jnp.broadcast_to(dskcol[...], (8, SSD_D_INNER))), et_ref[...])
            ddsk_ref[...] = jnp.max(dsk, axis=0, keepdims=True)

    def chunk(w):
        return pl.BlockSpec((CHUNK, w), lambda i: (nc - 1 - i, 0))

    def const(shape):
        return pl.BlockSpec(shape, lambda i: (0,) * len(shape))

    return pl.pallas_call(
        body, name=name, grid=(nc,),
        in_specs=[chunk(SSD_D_INNER), chunk(SSD_D_INNER), chunk(SSD_D_INNER), chunk(SSD_XBC), chunk(LANES),
                  pl.BlockSpec((1, SSD_STATE, SSD_D_INNER), lambda i: (nc - 1 - i, 0, 0)),
                  const((1, LANES)), const((1, LANES)), const((1, SSD_D_INNER)), const((1, SSD_D_INNER)),
                  const((1, SSD_D_INNER)), const((LANES, SSD_D_INNER)), const((SSD_D_INNER, LANES))],
        out_specs=[chunk(SSD_D_INNER), chunk(SSD_XBC), chunk(LANES), const((1, SSD_D_INNER)), const((1, LANES)),
                   const((1, LANES)), const((1, LANES))],
        out_shape=[jax.ShapeDtypeStruct((s, SSD_D_INNER), BF16), jax.ShapeDtypeStruct((s, SSD_XBC), F32),
                   jax.ShapeDtypeStruct((s, LANES), BF16), jax.ShapeDtypeStruct((1, SSD_D_INNER), F32),
                   jax.ShapeDtypeStruct((1, LANES), F32), jax.ShapeDtypeStruct((1, LANES), F32),
                   jax.ShapeDtypeStruct((1, LANES), F32)],
        scratch_shapes=[pltpu.VMEM((SSD_STATE, SSD_D_INNER), F32), pltpu.VMEM((CHUNK, SSD_D_INNER), F32),
                        pltpu.VMEM((1, SSD_D_INNER), F32)],
        compiler_params=_params(("arbitrary",)),
    )(dya, y, z, xbc, dt_raw, states, dt_bias, a_log, a_log_x, d_skip_x, norm_w, expand, expand_t)


GELU_K = math.sqrt(2.0 / math.pi)
GELU_C = 0.044715


def _gelu(x):
    return 0.5 * x * (1.0 + jnp.tanh(GELU_K * (x + GELU_C * x * x * x)))


def _gelu_grad(x):
    t = jnp.tanh(GELU_K * (x + GELU_C * x * x * x))
    return 0.5 * (1.0 + t) + 0.5 * x * (1.0 - t * t) * (GELU_K * (1.0 + 3.0 * GELU_C * x * x))


def _sgu_pre(uv_ref, uvb_ref, lnw_ref, lnb_ref):
    uv = uv_ref[...] + uvb_ref[...]
    guv = _gelu(uv)
    u = guv[:, :SGU_WIDTH]
    v = guv[:, SGU_WIDTH:]
    mu = jnp.mean(v, axis=-1, keepdims=True)
    vc = v - mu
    rstd = lax.rsqrt(jnp.mean(vc * vc, axis=-1, keepdims=True) + LN_EPS)
    vhat = vc * rstd
    vn = vhat * lnw_ref[...] + lnb_ref[...]
    return uv, u, vhat, rstd, vn


def _sgu_fwd(uv_raw, uv_b, ln_w, ln_b, w_sp, b_sp_t, *, name):
    s = uv_raw.shape[0]
    nc = s // CHUNK

    def body(uv_ref, uvb_ref, lnw_ref, lnb_ref, w_ref, bt_ref, o_ref):
        lower, _ = _tri_masks()
        _, u, _, _, vn = _sgu_pre(uv_ref, uvb_ref, lnw_ref, lnb_ref)
        vn_b = vn.astype(BF16)
        bt = bt_ref[...]
        for g in range(SGU_GROUPS):
            gs = slice(LANES * g, LANES * (g + 1))
            wc = jnp.where(lower, w_ref[g], 0.0).astype(BF16)
            mixed = _dot(wc, vn_b[:, gs]) + bt[:, g:g + 1]
            o_ref[:, gs] = (u[:, gs] * mixed).astype(BF16)

    def const(shape):
        return pl.BlockSpec(shape, lambda c: (0,) * len(shape))

    return pl.pallas_call(
        body, name=name, grid=(nc,),
        in_specs=[pl.BlockSpec((CHUNK, 2 * SGU_WIDTH), lambda c: (c, 0)), const((1, 2 * SGU_WIDTH)),
                  const((1, SGU_WIDTH)), const((1, SGU_WIDTH)), const((SGU_GROUPS, CHUNK, CHUNK)),
                  const((CHUNK, LANES))],
        out_specs=pl.BlockSpec((CHUNK, SGU_WIDTH), lambda c: (c, 0)),
        out_shape=jax.ShapeDtypeStruct((s, SGU_WIDTH), BF16),
        compiler_params=_params(("parallel",)),
    )(uv_raw, uv_b, ln_w, ln_b, w_sp, b_sp_t)


def _sgu_bwd(uv_raw, dyb, uv_b, ln_w, ln_b, w_sp, b_sp_t, group_sum, *, name):
    s = uv_raw.shape[0]
    nc = s // CHUNK

    def body(uv_ref, dy_ref, uvb_ref, lnw_ref, lnb_ref, w_ref, bt_ref, gsum_ref,
             duv_ref, dw_ref, dbt_ref, dlnw_ref, dlnb_ref, duvb_ref):
        @pl.when(pl.program_id(0) == 0)
        def _():
            dw_ref[...] = jnp.zeros_like(dw_ref)
            dbt_ref[...] = jnp.zeros_like(dbt_ref)
            dlnw_ref[...] = jnp.zeros_like(dlnw_ref)
            dlnb_ref[...] = jnp.zeros_like(dlnb_ref)
            duvb_ref[...] = jnp.zeros_like(duvb_ref)

        lower, _ = _tri_masks()
        uv, u, vhat, rstd, vn = _sgu_pre(uv_ref, uvb_ref, lnw_ref, lnb_ref)
        vn_b = vn.astype(BF16)
        bt = bt_ref[...]
        dy = dy_ref[...]
        du_parts, dvn_parts, dmix_parts = [], [], []
        for g in range(SGU_GROUPS):
            gs = slice(LANES * g, LANES * (g + 1))
            wc = jnp.where(lower, w_ref[g], 0.0).astype(BF16)
            mixed = _dot(wc, vn_b[:, gs]) + bt[:, g:g + 1]
            du_parts.append(dy[:, gs] * mixed)
            dmix = dy[:, gs] * u[:, gs]
            dmix_b = dmix.astype(BF16)
            dmix_parts.append(dmix)
            dw_ref[g] += jnp.where(lower, _dot(dmix_b, vn_b[:, gs], NT), 0.0)
            dvn_parts.append(_dot(wc, dmix_b, TN))
        dmixed = jnp.concatenate(dmix_parts, axis=1)
        dbt_ref[...] += _dot_terms(_split3(dmixed), gsum_ref[...])
        dvn = jnp.concatenate(dvn_parts, axis=1)
        dlnw_ref[...] += jnp.sum(dvn * vhat, axis=0, keepdims=True)
        dlnb_ref[...] += jnp.sum(dvn, axis=0, keepdims=True)
        dvhat = dvn * lnw_ref[...]
        dv = rstd * (dvhat - jnp.mean(dvhat, axis=-1, keepdims=True)
                     - vhat * jnp.mean(dvhat * vhat, axis=-1, keepdims=True))
        dguv = jnp.concatenate(du_parts + [dv], axis=1)
        duv = dguv * _gelu_grad(uv)
        duvb_ref[...] += jnp.sum(duv, axis=0, keepdims=True)
        duv_ref[...] = duv.astype(BF16)

    def const(shape):
        return pl.BlockSpec(shape, lambda c: (0,) * len(shape))

    return pl.pallas_call(
        body, name=name, grid=(nc,),
        in_specs=[pl.BlockSpec((CHUNK, 2 * SGU_WIDTH), lambda c: (c, 0)),
                  pl.BlockSpec((CHUNK, SGU_WIDTH), lambda c: (c, 0)), const((1, 2 * SGU_WIDTH)),
                  const((1, SGU_WIDTH)), const((1, SGU_WIDTH)), const((SGU_GROUPS, CHUNK, CHUNK)),
                  const((CHUNK, LANES)), const((SGU_WIDTH, LANES))],
        out_specs=[pl.BlockSpec((CHUNK, 2 * SGU_WIDTH), lambda c: (c, 0)), const((SGU_GROUPS, CHUNK, CHUNK)),
                   const((CHUNK, LANES)), const((1, SGU_WIDTH)), const((1, SGU_WIDTH)), const((1, 2 * SGU_WIDTH))],
        out_shape=[jax.ShapeDtypeStruct((s, 2 * SGU_WIDTH), BF16),
                   jax.ShapeDtypeStruct((SGU_GROUPS, CHUNK, CHUNK), F32), jax.ShapeDtypeStruct((CHUNK, LANES), F32),
                   jax.ShapeDtypeStruct((1, SGU_WIDTH), F32), jax.ShapeDtypeStruct((1, SGU_WIDTH), F32),
                   jax.ShapeDtypeStruct((1, 2 * SGU_WIDTH), F32)],
        compiler_params=_params(("arbitrary",)),
    )(uv_raw, dyb, uv_b, ln_w, ln_b, w_sp, b_sp_t, group_sum)


def _gate_fwd(gates_raw, b_gate, p_a, p_b, *, name, tm=512):
    s = p_a.shape[0]
    tm = min(tm, s)

    def body(ga_ref, gb_ref, ba_ref, bb_ref, pa_ref, pb_ref, o_ref):
        ga = _sigmoid(ga_ref[...] + ba_ref[...])
        gb = _sigmoid(gb_ref[...] + bb_ref[...])
        o_ref[...] = (ga * pa_ref[...] + gb * pb_ref[...]).astype(BF16)

    t_a = pl.BlockSpec((tm, D_MODEL), lambda i: (i, 0))
    t_b = pl.BlockSpec((tm, D_MODEL), lambda i: (i, 1))
    r_a = pl.BlockSpec((1, D_MODEL), lambda i: (0, 0))
    r_b = pl.BlockSpec((1, D_MODEL), lambda i: (0, 1))
    return pl.pallas_call(
        body, name=name, grid=(s // tm,),
        in_specs=[t_a, t_b, r_a, r_b, t_a, t_a], out_specs=t_a,
        out_shape=jax.ShapeDtypeStruct((s, D_MODEL), BF16),
        compiler_params=_params(("parallel",)),
    )(gates_raw, gates_raw, b_gate, b_gate, p_a, p_b)


def _gate_bwd(gates_raw, b_gate, p_a, p_b, dm, *, name, tm=512):
    s = p_a.shape[0]
    tm = min(tm, s)

    def body(ga_ref, gb_ref, ba_ref, bb_ref, pa_ref, pb_ref, dm_ref, dpa_ref, dpb_ref, dga_ref, dgb_ref,
             dba_ref, dbb_ref):
        @pl.when(pl.program_id(0) == 0)
        def _():
            dba_ref[...] = jnp.zeros_like(dba_ref)
            dbb_ref[...] = jnp.zeros_like(dbb_ref)

        d = dm_ref[...]
        for g_ref, b_ref, p_ref, dp_ref, dg_ref, db_ref in ((ga_ref, ba_ref, pa_ref, dpa_ref, dga_ref, dba_ref),
                                                            (gb_ref, bb_ref, pb_ref, dpb_ref, dgb_ref, dbb_ref)):
            sg = _sigmoid(g_ref[...] + b_ref[...])
            dp_ref[...] = (d * sg).astype(BF16)
            dg = d * p_ref[...] * (sg * (1.0 - sg))
            dg_ref[...] = dg.astype(BF16)
            db_ref[...] += jnp.sum(dg, axis=0, keepdims=True)

    t_a = pl.BlockSpec((tm, D_MODEL), lambda i: (i, 0))
    t_b = pl.BlockSpec((tm, D_MODEL), lambda i: (i, 1))
    r_a = pl.BlockSpec((1, D_MODEL), lambda i: (0, 0))
    r_b = pl.BlockSpec((1, D_MODEL), lambda i: (0, 1))
    big = jax.ShapeDtypeStruct((s, D_MODEL), BF16)
    row = jax.ShapeDtypeStruct((1, D_MODEL), F32)
    return pl.pallas_call(
        body, name=name, grid=(s // tm,),
        in_specs=[t_a, t_b, r_a, r_b, t_a, t_a, t_a], out_specs=[t_a, t_a, t_a, t_a, r_a, r_a],
        out_shape=[big, big, big, big, row, row],
        compiler_params=_params(("arbitrary",)),
    )(gates_raw, gates_raw, b_gate, b_gate, p_a, p_b, dm)


def _adamw_update(w_ref, g_ref, m_ref, v_ref, d_ref, mo_ref, vo_ref):
    gv = g_ref[...]
    mn = ADAM_B1 * m_ref[...] + (1.0 - ADAM_B1) * gv
    vn = ADAM_B2 * v_ref[...] + (1.0 - ADAM_B2) * (gv * gv)
    m_hat = mn / (1.0 - ADAM_B1 ** ADAM_STEP)
    v_hat = vn / (1.0 - ADAM_B2 ** ADAM_STEP)
    d_ref[...] = -ADAM_LR * (m_hat / (jnp.sqrt(v_hat) + ADAM_EPS) + ADAM_WD * w_ref[...])
    mo_ref[...] = mn
    vo_ref[...] = vn


def _adamw_many(ws, gs, ms, vs, *, name):
    n = len(ws)

    def body(*refs):
        for i in range(n):
            _adamw_update(*[refs[k * n + i] for k in range(7)])

    whole = pl.BlockSpec(memory_space=pltpu.VMEM)
    sds = [jax.ShapeDtypeStruct(w.shape, F32) for w in ws]
    outs = pl.pallas_call(
        body, name=name, in_specs=[whole] * (4 * n), out_specs=[whole] * (3 * n), out_shape=sds * 3,
        compiler_params=pltpu.CompilerParams(vmem_limit_bytes=VMEM_LIMIT),
    )(*ws, *gs, *ms, *vs)
    return outs[:n], outs[n:2 * n], outs[2 * n:]


def _adamw(w, g, m, v, *, name, tr=128):
    r, c = w.shape
    tr = min(tr, r)
    assert r % tr == 0, (name, r, tr)
    body = functools.partial(_adamw_update)

    blk = pl.BlockSpec((tr, c), lambda i: (i, 0))
    sds = jax.ShapeDtypeStruct((r, c), F32)
    return pl.pallas_call(
        body, name=name, grid=(r // tr,), in_specs=[blk] * 4, out_specs=[blk] * 3, out_shape=[sds] * 3,
        compiler_params=_params(("parallel",)),
    )(w, g, m, v)


def _tile(n, pref):
    if n <= pref:
        return n
    best = LANES
    for t in range(LANES, pref + 1, LANES):
        if n % t == 0:
            best = t
    return best


MATMUL_BLOCK_BYTES = 20 * 1024 * 1024


def _mm(pairs, name, **kw):
    trans_b = kw.get("trans_b", False)
    m = (pairs[0][0][0] if isinstance(pairs[0][0], tuple) else pairs[0][0]).shape[0]
    ktot, n = 0, None
    for _, b in pairs:
        shape = b[0].shape[1:] if isinstance(b, tuple) else b.shape
        ktot += shape[1] if trans_b else shape[0]
        n = shape[0] if trans_b else shape[1]
    out_bytes = 4 * (2 if kw.get("add") is not None else 1)
    best = None
    for tm in (256, 512, 1024):
        for tn in range(LANES, min(n, 1536) + 1, LANES):
            if m % min(tm, m) or n % tn:
                continue
            fits = 2 * ktot * (min(tm, m) + tn) + out_bytes * min(tm, m) * tn <= MATMUL_BLOCK_BYTES
            if fits and (best is None or min(tm, m) * tn >= best[0] * best[1]):
                best = (min(tm, m), tn)
    return _matmul(pairs, tm=best[0], tn=best[1], name=name, **kw)


def _wgrad(a, b, name, **kw):
    return _matmul_tn(a, b, tk=_tile(a.shape[1], 1408), tn=kw.pop("tn", _tile(b.shape[1], 1024)), tm=2048,
                      name=name, **kw)


def _local_step(x, target, get_weight, small, emit_grad):
    heads = jnp.arange(SSD_D_INNER) // SSD_HEAD_DIM
    expand = (jnp.arange(LANES)[:, None] == heads[None, :]).astype(BF16)
    expand_t = expand.T
    group_sum = (jnp.arange(SGU_WIDTH)[:, None] // LANES == jnp.arange(LANES)[None, :]).astype(BF16)
    pad_h = LANES - SSD_HEADS
    dt_bias = jnp.pad(small["dt_bias"], ((0, 0), (0, pad_h)))
    a_log = jnp.pad(small["a_log"], ((0, 0), (0, pad_h)))
    a_log_x = jnp.repeat(small["a_log"], SSD_HEAD_DIM, axis=1)
    d_skip_x = jnp.repeat(small["d_skip"], SSD_HEAD_DIM, axis=1)
    b_sp_t = jnp.pad(small["b_spatial"][0].T, ((0, 0), (0, LANES - SGU_GROUPS)))
    w_sp = small["w_spatial"][0]
    conv_a_w = jnp.pad(small["conv_a_w"], ((0, 4), (0, 0)))
    conv_f_w = jnp.pad(small["conv_f_w"], ((0, 5), (0, 0)))
    final_w = small["final_norm_w"].reshape(1, D_MODEL)

    n1 = _rms_fwd(x, small["norm1_w"], name="rms1_fwd")
    wts = dict(get_weight("w_in", n1))
    z = _mm([(n1, wts["in_z"])], "in_z")
    xbc_raw = _mm([(n1, wts["in_xbc"])], "in_xbc")
    dt_raw = _mm([(n1, wts["in_dt"])], "in_dt")
    uv_raw = _mm([(n1, wts["in_uv"])], "in_uv")
    gates_raw = _mm([(n1, wts["in_gate"])], "in_gate")
    xbc = _conv_a_fwd(xbc_raw, conv_a_w, small["conv_a_b"], name="conv_a_fwd")
    y, y_a, states = _ssd_fwd(xbc, dt_raw, z, dt_bias, a_log, a_log_x, d_skip_x, small["ssd_norm_w"], expand,
                              name="ssd_fwd")
    y_b = _sgu_fwd(uv_raw, small["uv_b"], small["v_ln_w"], small["v_ln_b"], w_sp, b_sp_t, name="sgu_fwd")
    wts.update(get_weight("w_branch", y_b))
    p_a = _mm([(y_a, wts["branch_a"])], "branch_a")
    p_b = _mm([(y_b, wts["branch_b"])], "branch_b")
    mix = _gate_fwd(gates_raw, small["b_gate"], p_a, p_b, name="gate_fwd")
    wts.update(get_weight("w_out", mix))
    h1 = _mm([(mix, wts["out"])], "out_proj", add=x)
    n2 = _rms_fwd(h1, small["norm2_w"], name="rms2_fwd")
    wts.update(get_weight("w_up", n2))
    up_w = wts["up"]
    up_cols = up_w.shape[2]
    up_raw = _matmul([(n2, (up_w, "cols"))], tm=1024, tn=up_cols, name="up_proj")
    act = _conv_f_fwd(up_raw, conv_f_w, small["conv_f_b"], name="conv_f_fwd")
    wts.update(get_weight("w_down", act))
    h2 = _mm([(act, wts["down"])], "down_proj", add=h1)
    loss, dh2, dh2_b, d_final = _final_fwd_bwd(h2, final_w, target, name="final_norm_loss")

    dact = _mm([(dh2_b, wts["down"])], "down_dgrad", trans_b=True)
    started = emit_grad("w_down", _wgrad(act, dh2_b, "down_wgrad"))
    dup_a, dup_v, dwf_a, dwf_v, dbf_a, dbf_v = _conv_f_bwd(up_raw, conv_f_w, small["conv_f_b"], dact,
                                                           name="conv_f_bwd")
    dn2 = _mm([((dup_a, 0), (up_w, 0)), ((dup_a, 1), (up_w, 1)), ((dup_v, 0), (up_w, 2)), ((dup_v, 1), (up_w, 3))],
              "up_dgrad", trans_b=True, after=started)
    started = emit_grad("w_up", jnp.concatenate([_wgrad(n2, dup_a, "up_wgrad_a", tn=up_cols, stack_out=True),
                                                 _wgrad(n2, dup_v, "up_wgrad_v", tn=up_cols, stack_out=True)], axis=0))
    dh1, dh1_b, d_norm2 = _rms_bwd(h1, small["norm2_w"], dn2, dh2, name="rms2_bwd")
    dmix = _mm([(dh1_b, wts["out"])], "out_dgrad", trans_b=True, after=started)
    started = emit_grad("w_out", _wgrad(mix, dh1_b, "out_wgrad"))
    dp_a, dp_b, dg_a, dg_b, dbg_a, dbg_b = _gate_bwd(gates_raw, small["b_gate"], p_a, p_b, dmix, name="gate_bwd")
    dya = _mm([(dp_a, wts["branch_a"])], "branch_a_dgrad", trans_b=True, after=started)
    dyb = _mm([(dp_b, wts["branch_b"])], "branch_b_dgrad", trans_b=True)
    started_branch = emit_grad("w_branch", jnp.concatenate([_wgrad(y_a, dp_a, "branch_a_wgrad"),
                                                            _wgrad(y_b, dp_b, "branch_b_wgrad")], axis=0))
    duv, d_wsp, d_bsp_t, d_lnw, d_lnb, d_uvb = _sgu_bwd(uv_raw, dyb, small["uv_b"], small["v_ln_w"],
                                                        small["v_ln_b"], w_sp, b_sp_t, group_sum, name="sgu_bwd")
    dz, dxbc, ddt, d_ssd_nw, d_dskip, d_alog, d_dtb = _ssd_bwd(
        dya, y, z, xbc, dt_raw, states, dt_bias, a_log, a_log_x, d_skip_x, small["ssd_norm_w"], expand, expand_t,
        name="ssd_bwd")
    dxbc_raw, d_conv_a_w, d_conv_a_b = _conv_a_bwd(xbc_raw, conv_a_w, small["conv_a_b"], dxbc, name="conv_a_bwd")
    started = emit_grad("w_in", {
        "in_z": _wgrad(n1, dz, "in_z_wgrad", after=started_branch), "in_xbc": _wgrad(n1, dxbc_raw, "in_xbc_wgrad"),
        "in_dt": _wgrad(n1, ddt, "in_dt_wgrad")[:, :SSD_HEADS], "in_uv": _wgrad(n1, duv, "in_uv_wgrad"),
        "in_gate_a": _wgrad(n1, dg_a, "in_gate_a_wgrad"), "in_gate_b": _wgrad(n1, dg_b, "in_gate_b_wgrad")})
    dn1 = _mm([(dz, wts["in_z"]), (dxbc_raw, wts["in_xbc"]), (ddt, wts["in_dt"]), (duv, wts["in_uv"]),
               (dg_a, wts["in_gate_a"]), (dg_b, wts["in_gate_b"])], "in_dgrad", trans_b=True, after=started)
    dx, _, d_norm1 = _rms_bwd(x, small["norm1_w"], dn1, dh1, name="rms1_bwd")

    grads_small = {
        "norm1_w": d_norm1, "b_gate": jnp.concatenate([dbg_a, dbg_b], axis=1),
        "conv_a_w": d_conv_a_w[:4], "conv_a_b": d_conv_a_b,
        "dt_bias": d_dtb[:, :SSD_HEADS], "a_log": d_alog[:, :SSD_HEADS], "d_skip": d_dskip[:, :SSD_HEADS],
        "ssd_norm_w": d_ssd_nw, "uv_b": d_uvb, "v_ln_w": d_lnw, "v_ln_b": d_lnb,
        "w_spatial": d_wsp[None], "b_spatial": d_bsp_t[:, :SGU_GROUPS].T[None],
        "norm2_w": d_norm2, "conv_f_w": jnp.concatenate([dwf_a[:3], dwf_v[:3]], axis=1),
        "conv_f_b": jnp.concatenate([dbf_a, dbf_v], axis=1), "final_norm_w": d_final.reshape(D_MODEL),
    }
    return loss, dx, grads_small


HBM = pl.BlockSpec(memory_space=pl.ANY)
MESH = pl.DeviceIdType.MESH


def _mesh_pos():
    return lax.axis_index("x"), lax.axis_index("y"), lax.axis_index("c")


def _other_chips(x, y):
    return [(1 - x, y), (x, 1 - y), (1 - x, 1 - y)]


def _remote(src, dst, send_sems, recv_sems, k, dev):
    return pltpu.make_async_remote_copy(src_ref=src, dst_ref=dst, send_sem=send_sems.at[k], recv_sem=recv_sems.at[k],
                                        device_id=dev, device_id_type=MESH)


def _dma_sems(n):
    return [pltpu.SemaphoreType.DMA((n,)), pltpu.SemaphoreType.DMA((n,))]


HBM_ONLY = pl.BlockSpec(memory_space=pltpu.HBM)
SEMAPHORES = pl.BlockSpec(memory_space=pltpu.SEMAPHORE)
DATAFLOW_EFFECT = pltpu.SideEffectType.DATAFLOW_SIDE_EFFECTING
N_PEER_CHIPS = N_CHIPS - 1


def _gather_sends(w_ref, land_ref, send_sems, recv_sems):
    x, y, c = _mesh_pos()
    return [_remote(w_ref.at[c], land_ref.at[2 * x + y, c], send_sems, recv_sems, k, (px, py, c))
            for k, (px, py) in enumerate(_other_chips(x, y))]


def _gather_arrivals(w_ref, land_ref, send_sems, recv_sems):
    x, y, c = _mesh_pos()
    return [_remote(w_ref.at[c], land_ref.at[2 * px + py, c], send_sems, recv_sems, k, (px, py, c))
            for k, (px, py) in enumerate(_other_chips(x, y))]


def _scatter_sends(h_ref, land_ref, send_sems, recv_sems):
    x, y, c = _mesh_pos()
    return [_remote(h_ref.at[2 * px + py], land_ref.at[2 * x + y], send_sems, recv_sems, k, (px, py, c))
            for k, (px, py) in enumerate(_other_chips(x, y))]


def _scatter_arrivals(h_ref, land_ref, send_sems, recv_sems):
    x, y, c = _mesh_pos()
    return [_remote(h_ref.at[2 * x + y], land_ref.at[2 * px + py], send_sems, recv_sems, k, (px, py, c))
            for k, (px, py) in enumerate(_other_chips(x, y))]


def _exchange_start(sources, landing_shapes, sends, *, after=None, name):
    n = len(sources)
    extra = [] if after is None else [after]

    def body(*refs):
        sems = refs[2 * n + len(extra):4 * n + len(extra)]
        for i in range(n):
            for cp in sends(refs[i], refs[n + i], sems[2 * i], sems[2 * i + 1]):
                cp.start()
        refs[-1][...] = jnp.zeros_like(refs[-1])

    hbm = [pltpu.HBM(s.shape, s.dtype) for s in sources] + [pltpu.HBM(shp, s.dtype)
                                                             for shp, s in zip(landing_shapes, sources)]
    outs = pl.pallas_call(
        body, name=name,
        out_shape=tuple([pltpu.SemaphoreType.DMA((N_PEER_CHIPS,))] * (2 * n) + hbm
                        + [jax.ShapeDtypeStruct((8, LANES), F32)]),
        in_specs=[HBM_ONLY] * (2 * n) + [pl.BlockSpec(memory_space=pl.ANY)] * len(extra),
        out_specs=tuple([SEMAPHORES] * (2 * n) + [HBM_ONLY] * (2 * n) + [pl.BlockSpec(memory_space=pltpu.VMEM)]),
        input_output_aliases={i: 2 * n + i for i in range(2 * n)},
        compiler_params=pltpu.CompilerParams(has_side_effects=DATAFLOW_EFFECT),
    )(*[pltpu.with_memory_space_constraint(s, pltpu.HBM) for s in sources],
      *[pltpu.with_memory_space_constraint(lax.empty(shp, s.dtype), pltpu.HBM)
        for shp, s in zip(landing_shapes, sources)], *extra)
    pending = [(outs[2 * i], outs[2 * i + 1], outs[2 * n + i], outs[3 * n + i]) for i in range(n)]
    return pending, outs[-1]


def _exchange_wait(pending, after, sends, arrivals, *, name):
    send_sems, recv_sems, source, landing = pending

    def body(src_ref, land_ref, send_ref, recv_ref, after_ref, src_out, land_out):
        for cp in sends(src_ref, land_ref, send_ref, recv_ref):
            cp.wait_send()
        for cp in arrivals(src_ref, land_ref, send_ref, recv_ref):
            cp.wait_recv()

    return pl.pallas_call(
        body, name=name,
        out_shape=(pltpu.HBM(source.shape, source.dtype), pltpu.HBM(landing.shape, landing.dtype)),
        in_specs=[HBM_ONLY, HBM_ONLY, SEMAPHORES, SEMAPHORES, pl.BlockSpec(memory_space=pl.ANY)],
        out_specs=(HBM_ONLY, HBM_ONLY), input_output_aliases={0: 0, 1: 1},
        compiler_params=pltpu.CompilerParams(has_side_effects=DATAFLOW_EFFECT),
    )(source, landing, send_sems, recv_sems, after)


def _gather_ici(shard, *, name):
    _, rh, cols = shard.shape

    def body(w_ref, o_ref, send_sems, recv_sems):
        x, y, c = _mesh_pos()
        mine = 2 * x + y
        sends = []
        for k, (px, py) in enumerate(_other_chips(x, y)):
            cp = _remote(w_ref.at[c], o_ref.at[mine, c], send_sems, recv_sems, k, (px, py, c))
            cp.start()
            sends.append(cp)
        for k, (px, py) in enumerate(_other_chips(x, y)):
            _remote(w_ref.at[c], o_ref.at[2 * px + py, c], send_sems, recv_sems, k, (px, py, c)).wait_recv()
        for cp in sends:
            cp.wait_send()

    return pl.pallas_call(
        body, name=name, in_specs=[HBM], out_specs=HBM,
        out_shape=jax.ShapeDtypeStruct((N_CHIPS, 2, rh, cols), shard.dtype), scratch_shapes=_dma_sems(3),
    )(shard)


def _gather_d2d(parts, *, name):
    def body(a_ref, o_ref, send_sems, recv_sems):
        x, y, c = _mesh_pos()
        sibling = (x, y, 1 - c)
        sends = []
        for k, (px, py) in enumerate(_other_chips(x, y)):
            cp = _remote(a_ref.at[2 * px + py, c], o_ref.at[2 * px + py, c], send_sems, recv_sems, k, sibling)
            cp.start()
            sends.append(cp)
        for k, (px, py) in enumerate(_other_chips(x, y)):
            _remote(a_ref.at[2 * px + py, c], o_ref.at[2 * px + py, 1 - c], send_sems, recv_sems, k, sibling).wait_recv()
        for cp in sends:
            cp.wait_send()

    return pl.pallas_call(
        body, name=name, in_specs=[HBM], out_specs=HBM,
        out_shape=jax.ShapeDtypeStruct(parts.shape, parts.dtype),
        input_output_aliases={0: 0}, scratch_shapes=_dma_sems(3),
    )(parts)


def _all_gather_chips(shard_flat, name):
    rows, cols = shard_flat.shape
    parts = _gather_ici(shard_flat.reshape(2, rows // 2, cols), name=name + "_ici")
    others = _gather_d2d(parts, name=name + "_d2d").reshape(N_CHIPS, rows, cols)
    chip = 2 * lax.axis_index("x") + lax.axis_index("y")
    return lax.dynamic_update_slice(others, shard_flat[None], (chip, 0, 0))


def _row_tile(rows, mult, cap):
    best = mult
    for t in range(mult, min(rows, cap) + 1, mult):
        if rows % t == 0:
            best = t
    assert rows % best == 0, (rows, mult)
    return best


def _swap_halves_d2d(g, *, name):
    _, _, rh, cols = g.shape

    def body(g_ref, o_ref, send_sems, recv_sems):
        x, y, c = _mesh_pos()
        sibling = (x, y, 1 - c)
        sends = []
        for s in range(N_CHIPS):
            cp = _remote(g_ref.at[s, 1 - c], o_ref.at[s], send_sems, recv_sems, s, sibling)
            cp.start()
            sends.append(cp)
        for s in range(N_CHIPS):
            _remote(g_ref.at[s, c], o_ref.at[s], send_sems, recv_sems, s, sibling).wait_recv()
        for cp in sends:
            cp.wait_send()

    return pl.pallas_call(
        body, name=name, in_specs=[HBM], out_specs=HBM,
        out_shape=jax.ShapeDtypeStruct((N_CHIPS, rh, cols), g.dtype), scratch_shapes=_dma_sems(N_CHIPS),
    )(g)


def _add_own_half(g, arrived, core, *, name):
    _, _, rh, cols = g.shape
    mult = 16 if g.dtype == BF16 else 8
    tr = _row_tile(rh, mult, max(mult, (512 * 1024) // cols))

    def body(core_ref, g_ref, a_ref, o_ref):
        o_ref[...] = (g_ref[0].astype(F32) + a_ref[...].astype(F32)).astype(o_ref.dtype)

    grid_spec = pltpu.PrefetchScalarGridSpec(
        num_scalar_prefetch=1, grid=(N_CHIPS, rh // tr),
        in_specs=[pl.BlockSpec((1, 1, tr, cols), lambda s, i, core_ref: (s, core_ref[0], i, 0)),
                  pl.BlockSpec((1, tr, cols), lambda s, i, core_ref: (s, i, 0))],
        out_specs=pl.BlockSpec((1, tr, cols), lambda s, i, core_ref: (s, i, 0)))
    return pl.pallas_call(
        body, name=name, grid_spec=grid_spec, out_shape=jax.ShapeDtypeStruct((N_CHIPS, rh, cols), g.dtype),
        compiler_params=_params(("parallel", "parallel")),
    )(core, g, arrived)


def _scatter_ici(h, *, name):
    def body(h_ref, o_ref, send_sems, recv_sems):
        x, y, c = _mesh_pos()
        mine = 2 * x + y
        sends = []
        for k, (px, py) in enumerate(_other_chips(x, y)):
            cp = _remote(h_ref.at[2 * px + py], o_ref.at[mine], send_sems, recv_sems, k, (px, py, c))
            cp.start()
            sends.append(cp)
        for k, (px, py) in enumerate(_other_chips(x, y)):
            _remote(h_ref.at[mine], o_ref.at[2 * px + py], send_sems, recv_sems, k, (px, py, c)).wait_recv()
        for cp in sends:
            cp.wait_send()

    others = pl.pallas_call(
        body, name=name, in_specs=[HBM], out_specs=HBM, out_shape=jax.ShapeDtypeStruct(h.shape, h.dtype),
        scratch_shapes=_dma_sems(3),
    )(h)
    chip = 2 * lax.axis_index("x") + lax.axis_index("y")
    own = lax.dynamic_slice_in_dim(h, chip, 1, axis=0)
    return lax.dynamic_update_slice(others, own, (chip, 0, 0))


def _sum_chips(parts, *, name):
    _, rh, cols = parts.shape
    mult = 16 if parts.dtype == BF16 else 8
    tr = _row_tile(rh, mult, max(mult, (512 * 1024) // cols))

    def body(p_ref, o_ref):
        acc = p_ref[0].astype(F32)
        for s in range(1, N_CHIPS):
            acc = acc + p_ref[s].astype(F32)
        o_ref[...] = acc

    return pl.pallas_call(
        body, name=name, grid=(rh // tr,),
        in_specs=[pl.BlockSpec((N_CHIPS, tr, cols), lambda i: (0, i, 0))],
        out_specs=pl.BlockSpec((tr, cols), lambda i: (i, 0)),
        out_shape=jax.ShapeDtypeStruct((rh, cols), F32), compiler_params=_params(("parallel",)),
    )(parts)


def _share_d2d(f, *, name):
    fs = f if isinstance(f, (list, tuple)) else [f]
    n = len(fs)

    def body(*refs):
        x, y, c = _mesh_pos()
        sibling = (x, y, 1 - c)
        send_sems, recv_sems = refs[2 * n:]
        copies = [_remote(refs[i], refs[n + i], send_sems, recv_sems, i, sibling) for i in range(n)]
        for cp in copies:
            cp.start()
        for cp in copies:
            cp.wait()

    others = pl.pallas_call(
        body, name=name, in_specs=[HBM] * n, out_specs=[HBM] * n,
        out_shape=[jax.ShapeDtypeStruct(a.shape, a.dtype) for a in fs], scratch_shapes=_dma_sems(n),
    )(*fs)
    first = lax.axis_index("c") == 0
    both = [jnp.stack([jnp.where(first, a, b), jnp.where(first, b, a)]) for a, b in zip(fs, others)]
    return both if isinstance(f, (list, tuple)) else both[0]


def _reduce_scatter_chips(g, core, name):
    _, rows, cols = g.shape
    g = g.reshape(N_CHIPS, 2, rows // 2, cols)
    arrived = _swap_halves_d2d(g, name=name + "_swap")
    chip_sum = _add_own_half(g, arrived, core, name=name + "_add2")
    parts = _scatter_ici(chip_sum, name=name + "_ici")
    total = _sum_chips(parts, name=name + "_sum4")
    return _share_d2d(total, name=name + "_share").reshape(rows, cols)


BIG = ("w_in", "w_branch", "w_out", "w_up", "w_down")
BIG_COLUMN_SHARDED = ("w_in", "w_up")
CONV = ("conv_a_w", "conv_f_w")
REPLICATED = ("norm1_w", "b_gate", "conv_a_b", "dt_bias", "a_log", "d_skip", "ssd_norm_w", "uv_b", "v_ln_w",
              "v_ln_b", "w_spatial", "b_spatial", "norm2_w", "conv_f_b", "final_norm_w")
WEIGHT_ORDER = ("norm1_w", "w_in", "b_gate", "conv_a_w", "conv_a_b", "dt_bias", "a_log", "d_skip", "ssd_norm_w",
                "uv_b", "v_ln_w", "v_ln_b", "w_spatial", "b_spatial", "w_branch", "w_out", "norm2_w", "w_up",
                "conv_f_w", "conv_f_b", "w_down", "final_norm_w")
SMALL_EXCHANGE_ROWS = 64


_GATE0 = SSD_IN + 2 * SGU_WIDTH
IN_SEGMENTS = {
    "in_z": (0, SSD_D_INNER), "in_xbc": (SSD_D_INNER, SSD_D_INNER + SSD_XBC), "in_dt": (SSD_D_INNER + SSD_XBC, SSD_IN),
    "in_uv": (SSD_IN, _GATE0), "in_gate": (_GATE0, IN_COLS), "in_gate_a": (_GATE0, _GATE0 + D_MODEL),
    "in_gate_b": (_GATE0 + D_MODEL, IN_COLS),
}
IN_GRAD_SEGMENTS = ("in_z", "in_xbc", "in_dt", "in_uv", "in_gate_a", "in_gate_b")


def _take_columns(parts, start, stop):
    out = []
    for a, first in parts:
        lo, hi = max(start, first), min(stop, first + a.shape[1])
        if lo < hi:
            out.append(a[:, lo - first:hi - first])
    return out[0] if len(out) == 1 else jnp.concatenate(out, axis=1)


def _flat_rows(arrays, row_multiple):
    flat = jnp.concatenate([a.reshape(-1) for a in arrays])
    rows = -(-flat.shape[0] // (LANES * row_multiple)) * row_multiple
    return jnp.pad(flat, (0, rows * LANES - flat.shape[0])).reshape(rows, LANES)


def _unflatten(flat, shapes):
    flat = flat.reshape(-1)
    out, off = [], 0
    for shp in shapes:
        n = math.prod(shp)
        out.append(flat[off:off + n].reshape(shp))
        off += n
    return out


def _from_chip_blocks(blocks, name):
    if name in BIG_COLUMN_SHARDED or name in CONV:
        k = blocks.shape[1]
        return jnp.transpose(blocks, (1, 0, 2)).reshape(k, -1)
    return blocks.reshape(-1, blocks.shape[-1])


def _to_chip_blocks(whole, name):
    if name in BIG_COLUMN_SHARDED or name in CONV:
        k, n = whole.shape
        return jnp.transpose(whole.reshape(k, N_CHIPS, n // N_CHIPS), (1, 0, 2))
    return whole.reshape(N_CHIPS, whole.shape[0] // N_CHIPS, whole.shape[1])


def kernel(x, norm1_w, w_in, b_gate, conv_a_w, conv_a_b, dt_bias, a_log, d_skip, ssd_norm_w, uv_b, v_ln_w, v_ln_b, w_spatial, b_spatial, w_branch, w_out, norm2_w, w_up, conv_f_w, conv_f_b, w_down, final_norm_w, loss_target, m_norm1_w, m_w_in, m_b_gate, m_conv_a_w, m_conv_a_b, m_dt_bias, m_a_log, m_d_skip, m_ssd_norm_w, m_uv_b, m_v_ln_w, m_v_ln_b, m_w_spatial, m_b_spatial, m_w_branch, m_w_out, m_norm2_w, m_w_up, m_conv_f_w, m_conv_f_b, m_w_down, m_final_norm_w, v_norm1_w, v_w_in, v_b_gate, v_conv_a_w, v_conv_a_b, v_dt_bias, v_a_log, v_d_skip, v_ssd_norm_w, v_uv_b, v_v_ln_w, v_v_ln_b, v_w_spatial, v_b_spatial, v_w_branch, v_w_out, v_norm2_w, v_w_up, v_conv_f_w, v_conv_f_b, v_w_down, v_final_norm_w):
    weights = dict(norm1_w=norm1_w, w_in=w_in, b_gate=b_gate, conv_a_w=conv_a_w, conv_a_b=conv_a_b, dt_bias=dt_bias,
                   a_log=a_log, d_skip=d_skip, ssd_norm_w=ssd_norm_w, uv_b=uv_b, v_ln_w=v_ln_w, v_ln_b=v_ln_b,
                   w_spatial=w_spatial, b_spatial=b_spatial, w_branch=w_branch, w_out=w_out, norm2_w=norm2_w,
                   w_up=w_up, conv_f_w=conv_f_w, conv_f_b=conv_f_b, w_down=w_down, final_norm_w=final_norm_w)
    mom1 = dict(norm1_w=m_norm1_w, w_in=m_w_in, b_gate=m_b_gate, conv_a_w=m_conv_a_w, conv_a_b=m_conv_a_b,
                dt_bias=m_dt_bias, a_log=m_a_log, d_skip=m_d_skip, ssd_norm_w=m_ssd_norm_w, uv_b=m_uv_b,
                v_ln_w=m_v_ln_w, v_ln_b=m_v_ln_b, w_spatial=m_w_spatial, b_spatial=m_b_spatial, w_branch=m_w_branch,
                w_out=m_w_out, norm2_w=m_norm2_w, w_up=m_w_up, conv_f_w=m_conv_f_w, conv_f_b=m_conv_f_b,
                w_down=m_w_down, final_norm_w=m_final_norm_w)
    mom2 = dict(norm1_w=v_norm1_w, w_in=v_w_in, b_gate=v_b_gate, conv_a_w=v_conv_a_w, conv_a_b=v_conv_a_b,
                dt_bias=v_dt_bias, a_log=v_a_log, d_skip=v_d_skip, ssd_norm_w=v_ssd_norm_w, uv_b=v_uv_b,
                v_ln_w=v_v_ln_w, v_ln_b=v_v_ln_b, w_spatial=v_w_spatial, b_spatial=v_b_spatial, w_branch=v_w_branch,
                w_out=v_w_out, norm2_w=v_norm2_w, w_up=v_w_up, conv_f_w=v_conv_f_w, conv_f_b=v_conv_f_b,
                w_down=v_w_down, final_norm_w=v_final_norm_w)
    chip = 2 * lax.axis_index("x") + lax.axis_index("y")
    core = lax.axis_index("c").astype(jnp.int32).reshape(1)

    whole = {}
    conv_shapes = [weights[n].shape[1:] for n in CONV]
    conv_gathered = _all_gather_chips(_flat_rows([weights[n] for n in CONV], 16), "gather_conv").reshape(N_CHIPS, -1)
    off = 0
    for n, shp in zip(CONV, conv_shapes):
        size = math.prod(shp)
        whole[n] = _from_chip_blocks(conv_gathered[:, off:off + size].reshape((N_CHIPS,) + shp), n)
        off += size
    shard_shapes = {n: weights[n].shape[1:] for n in BIG}
    halves = [weights[n][0].astype(BF16).reshape(2, shard_shapes[n][0] // 2, shard_shapes[n][1]) for n in BIG]
    gathers, _ = _exchange_start(halves, [(N_CHIPS,) + h.shape for h in halves], _gather_sends, after=conv_gathered,
                                 name="gather_start")
    gathers = dict(zip(BIG, gathers))

    def get_weight(name, after):
        rows, cols = shard_shapes[name]
        own, landed = _exchange_wait(gathers[name], after, _gather_sends, _gather_arrivals,
                                     name="gather_" + name + "_wait")
        others = _gather_d2d(landed, name="gather_" + name + "_d2d").reshape(N_CHIPS, rows, cols)
        blocks = lax.dynamic_update_slice(others, own.reshape(1, rows, cols), (chip, 0, 0))
        if name == "w_up":
            return {"up": blocks}
        if name == "w_in":
            parts = [(blocks[k], cols * k) for k in range(N_CHIPS)]
            segs = {n: _take_columns(parts, a, b) for n, (a, b) in IN_SEGMENTS.items()}
            segs["in_dt"] = jnp.pad(segs["in_dt"], ((0, 0), (0, LANES - SSD_HEADS)))
            return segs
        full = _from_chip_blocks(blocks, name)
        if name == "w_branch":
            return {"branch_a": full[:SSD_D_INNER], "branch_b": full[SSD_D_INNER:]}
        return {name[2:]: full}

    small = {n: weights[n] for n in REPLICATED}
    small["conv_a_w"] = whole["conv_a_w"]
    small["conv_f_w"] = whole["conv_f_w"]

    reductions = {}

    def emit_grad(name, g):
        if name == "w_in":
            parts = [(g[n], IN_SEGMENTS[n][0]) for n in IN_GRAD_SEGMENTS]
            cols = shard_shapes[name][1]
            g_blocks = jnp.stack([_take_columns(parts, cols * k, cols * (k + 1)) for k in range(N_CHIPS)])
        else:
            g_blocks = g if name == "w_up" else _to_chip_blocks(g, name)
        _, rows, cols = g_blocks.shape
        g_halves = g_blocks.reshape(N_CHIPS, 2, rows // 2, cols)
        arrived = _swap_halves_d2d(g_halves, name="reduce_" + name + "_swap")
        chip_sum = _add_own_half(g_halves, arrived, core, name="reduce_" + name + "_add2")
        own = lax.dynamic_slice_in_dim(chip_sum, chip, 1, axis=0)
        (pending,), started = _exchange_start([chip_sum], [chip_sum.shape], _scatter_sends,
                                              name="reduce_" + name + "_start")
        reductions[name] = (pending, own)
        return started

    loss, dx, grads_small = _local_step(x[0], loss_target[0], get_weight, small, emit_grad)

    order = ("w_down", "w_up", "w_out", "w_branch", "w_in")
    totals = []
    for n in order:
        pending, own = reductions[n]
        _, landed = _exchange_wait(pending, dx, _scatter_sends, _scatter_arrivals, name="reduce_" + n + "_wait")
        parts = lax.dynamic_update_slice(landed, own, (chip, 0, 0))
        totals.append(_sum_chips(parts, name="reduce_" + n + "_sum4"))
    grads = {n: both.reshape(shard_shapes[n]) for n, both in zip(order, _share_d2d(totals, name="reduce_share"))}

    small_names = REPLICATED + CONV
    small_shapes = [grads_small[n].shape for n in small_names]
    g_small = _flat_rows([grads_small[n] for n in small_names], N_CHIPS * 2 * SMALL_EXCHANGE_ROWS)
    red_small = _reduce_scatter_chips(g_small.reshape(N_CHIPS, -1, LANES), core, "reduce_small")
    all_small = _all_gather_chips(red_small, "gather_small")
    for n, g in zip(small_names, _unflatten(all_small, small_shapes)):
        if n in CONV:
            width = g.shape[1] // N_CHIPS
            g = lax.dynamic_slice_in_dim(g, chip * width, width, axis=1)
        grads[n] = g.reshape(weights[n].shape[1:]) if n != "final_norm_w" else g

    delta, new_m, new_v = {}, {}, {}
    for n in BIG:
        shp = weights[n].shape
        operands = [weights[n][0], grads[n], mom1[n][0], mom2[n][0]]
        if n == "w_in":
            operands = [a.T for a in operands]
            grads[n] = operands[1].T
        tr = _row_tile(operands[0].shape[0], 8, 136)
        results = _adamw(*operands, name="adamw_" + n, tr=tr)
        if n == "w_in":
            results = [a.T for a in results]
        delta[n], new_m[n], new_v[n] = [a.reshape(shp) for a in results]
    small_all = [n for n in WEIGHT_ORDER if n not in BIG]

    def as_2d(a):
        return a.reshape(-1, a.shape[-1])

    results = _adamw_many(*[[as_2d(src[n]) for n in small_all] for src in (weights, grads, mom1, mom2)],
                          name="adamw_small")
    for n, dv, mv, vv in zip(small_all, *results):
        shp = weights[n].shape
        delta[n], new_m[n], new_v[n] = dv.reshape(shp), mv.reshape(shp), vv.reshape(shp)

    total_loss = lax.psum(loss[0, 0], ("x", "y", "c"))
    grad_out = [grads[n].reshape(weights[n].shape) for n in WEIGHT_ORDER]
    return (total_loss, dx[None], *grad_out, *[delta[n] for n in WEIGHT_ORDER], *[new_m[n] for n in WEIGHT_ORDER],
            *[new_v[n] for n in WEIGHT_ORDER])
```

```python
import functools
import math

import jax
import jax.numpy as jnp
from jax import lax
from jax.experimental import pallas as pl
from jax.experimental.pallas import tpu as pltpu

F32 = jnp.float32
BF16 = jnp.bfloat16
HI = lax.Precision.HIGHEST

D_MODEL = 1024
SSD_D_INNER = 2048
SSD_HEADS = 32
SSD_HEAD_DIM = 64
SSD_GROUPS = 4
SSD_HEADS_PER_GROUP = 8
SSD_STATE = 128
SSD_BC = 512
SSD_XBC = 3072
SSD_IN = 5152
SGU_WIDTH = 1024
SGU_GROUPS = 8
CHUNK = 128
IN_COLS = 9248
D_FF = 2816
NORM_EPS = 1e-6
LN_EPS = 1e-5
GROUP_COLS = SSD_HEADS_PER_GROUP * SSD_HEAD_DIM
LANES = 128

ADAM_LR = 0.001
ADAM_B1 = 0.9
ADAM_B2 = 0.999
ADAM_EPS = 1e-08
ADAM_WD = 0.01
ADAM_STEP = 10

N_CHIPS = 4
VMEM_LIMIT = 56 * 1024 * 1024

NT = (((1,), (1,)), ((), ()))
TN = (((0,), (0,)), ((), ()))
NN = (((1,), (0,)), ((), ()))


def _params(dims):
    return pltpu.CompilerParams(dimension_semantics=dims, vmem_limit_bytes=VMEM_LIMIT)


def _dot(a, b, dn=NN, precision=None):
    return lax.dot_general(a, b, dn, precision=precision, preferred_element_type=F32)


def _split3(x):
    hi = x.astype(BF16)
    rest = x - hi.astype(F32)
    mid = rest.astype(BF16)
    return hi, mid, (rest - mid.astype(F32)).astype(BF16)


def _dot_terms(terms, exact, dn=NN):
    out = None
    for t in terms:
        p = _dot(t, exact, dn)
        out = p if out is None else out + p
    return out


def _dot_exact_lhs(exact, terms):
    out = None
    for t in terms:
        p = _dot(exact, t)
        out = p if out is None else out + p
    return out


def _sigmoid(x):
    return 1.0 / (1.0 + jnp.exp(-x))


def _softplus(x):
    return jnp.maximum(x, 0.0) + jnp.log(1.0 + jnp.exp(-jnp.abs(x)))


def _matmul(pairs, *, trans_b=False, add=None, after=None, out_dtype=F32, tm=512, tn=512, name):
    def mat_shape(b):
        if isinstance(b, tuple) and b[1] == "cols":
            return (b[0].shape[1], b[0].shape[0] * b[0].shape[2])
        return b[0].shape[1:] if isinstance(b, tuple) else b.shape

    if isinstance(pairs[0][1], tuple) and pairs[0][1][1] == "cols":
        assert not trans_b and tn % LANES == 0 and pairs[0][1][0].shape[2] % tn == 0, name

    m = (pairs[0][0][0] if isinstance(pairs[0][0], tuple) else pairs[0][0]).shape[0]
    n = mat_shape(pairs[0][1])[0] if trans_b else mat_shape(pairs[0][1])[1]
    tm, tn = min(tm, m), min(tn, n)
    assert m % tm == 0 and n % tn == 0, (name, m, n, tm, tn)
    npairs = len(pairs)
    dn = NT if trans_b else NN

    def body(*refs):
        o_ref = refs[-1]
        acc = None
        for i in range(npairs):
            p = _dot(refs[2 * i][...].astype(BF16), refs[2 * i + 1][...].astype(BF16), dn)
            acc = p if acc is None else acc + p
        if add is not None:
            acc = acc + refs[2 * npairs][...]
        o_ref[...] = acc.astype(out_dtype)

    in_specs, args = [], []
    for a, b in pairs:
        bshape = mat_shape(b)
        k = bshape[1] if trans_b else bshape[0]
        assert bshape == ((n, k) if trans_b else (k, n)), (name, bshape)
        a, qa = a if isinstance(a, tuple) else (a, 0)
        assert a.shape[0] == m and a.shape[1] % k == 0, (name, a.shape, k)
        in_specs.append(pl.BlockSpec((tm, k), lambda i, j, qa=qa: (i, qa)))
        if isinstance(b, tuple) and b[1] == "cols":
            b = b[0]
            per = b.shape[2] // tn
            in_specs.append(pl.BlockSpec((None, k, tn), lambda i, j, per=per: (j // per, 0, j % per)))
        elif isinstance(b, tuple):
            b, qb = b
            if trans_b:
                in_specs.append(pl.BlockSpec((None, tn, k), lambda i, j, qb=qb: (qb, j, 0)))
            else:
                in_specs.append(pl.BlockSpec((None, k, tn), lambda i, j, qb=qb: (qb, 0, j)))
        elif trans_b:
            in_specs.append(pl.BlockSpec((tn, k), lambda i, j: (j, 0)))
        else:
            in_specs.append(pl.BlockSpec((k, tn), lambda i, j: (0, j)))
        args += [a, b]
    if add is not None:
        in_specs.append(pl.BlockSpec((tm, tn), lambda i, j: (i, j)))
        args.append(add)
    if after is not None:
        in_specs.append(pl.BlockSpec(memory_space=pl.ANY))
        args.append(after)
    return pl.pallas_call(
        body, name=name, grid=(m // tm, n // tn), in_specs=in_specs,
        out_specs=pl.BlockSpec((tm, tn), lambda i, j: (i, j)),
        out_shape=jax.ShapeDtypeStruct((m, n), out_dtype),
        compiler_params=_params(("parallel", "parallel")),
    )(*args)


def _matmul_tn(a, b, *, tk, tn, tm=1024, out_dtype=BF16, stack_out=False, after=None, name):
    m, k = a.shape
    n = b.shape[1]
    tm, tk, tn = min(tm, m), min(tk, k), min(tn, n)
    assert m % tm == 0 and k % tk == 0 and n % tn == 0, (name, m, k, n)
    nm = m // tm
    if stack_out:
        out_spec = pl.BlockSpec((None, tk, tn), lambda i, j, l: (j, i, 0))
        out_shape = jax.ShapeDtypeStruct((n // tn, k, tn), out_dtype)
    else:
        out_spec = pl.BlockSpec((tk, tn), lambda i, j, l: (i, j))
        out_shape = jax.ShapeDtypeStruct((k, n), out_dtype)

    def body(a_ref, b_ref, *rest):
        o_ref, acc = rest[-2:]
        mi = pl.program_id(2)

        @pl.when(mi == 0)
        def _():
            acc[...] = jnp.zeros_like(acc)

        acc[...] += _dot(a_ref[...].astype(BF16), b_ref[...].astype(BF16), TN)

        @pl.when(mi == nm - 1)
        def _():
            o_ref[...] = acc[...].astype(out_dtype)

    in_specs = [pl.BlockSpec((tm, tk), lambda i, j, l: (l, i)), pl.BlockSpec((tm, tn), lambda i, j, l: (l, j))]
    args = [a, b]
    if after is not None:
        in_specs.append(pl.BlockSpec(memory_space=pl.ANY))
        args.append(after)
    return pl.pallas_call(
        body, name=name, grid=(k // tk, n // tn, nm), in_specs=in_specs,
        out_specs=out_spec, out_shape=out_shape,
        scratch_shapes=[pltpu.VMEM((tk, tn), F32)],
        compiler_params=_params(("parallel", "parallel", "arbitrary")),
    )(*args)


def _rms_fwd(x, w, *, name, tm=512):
    s, d = x.shape
    tm = min(tm, s)

    def body(x_ref, w_ref, o_ref):
        xv = x_ref[...]
        r = lax.rsqrt(jnp.mean(xv * xv, axis=-1, keepdims=True) + NORM_EPS)
        o_ref[...] = (xv * r * w_ref[...]).astype(BF16)

    return pl.pallas_call(
        body, name=name, grid=(s // tm,),
        in_specs=[pl.BlockSpec((tm, d), lambda i: (i, 0)), pl.BlockSpec((1, d), lambda i: (0, 0))],
        out_specs=pl.BlockSpec((tm, d), lambda i: (i, 0)),
        out_shape=jax.ShapeDtypeStruct((s, d), BF16),
        compiler_params=_params(("parallel",)),
    )(x, w)


def _rms_bwd(x, w, dn, dres, *, name, tm=512):
    s, d = x.shape
    tm = min(tm, s)

    def body(x_ref, w_ref, dn_ref, dres_ref, dx_ref, dxb_ref, dw_ref):
        @pl.when(pl.program_id(0) == 0)
        def _():
            dw_ref[...] = jnp.zeros_like(dw_ref)

        xv = x_ref[...]
        r = lax.rsqrt(jnp.mean(xv * xv, axis=-1, keepdims=True) + NORM_EPS)
        xhat = xv * r
        dnv = dn_ref[...]
        dxhat = dnv * w_ref[...]
        dx = dres_ref[...] + r * (dxhat - xhat * jnp.mean(dxhat * xhat, axis=-1, keepdims=True))
        dx_ref[...] = dx
        dxb_ref[...] = dx.astype(BF16)
        dw_ref[...] += jnp.sum(dnv * xhat, axis=0, keepdims=True)

    tile = pl.BlockSpec((tm, d), lambda i: (i, 0))
    row = pl.BlockSpec((1, d), lambda i: (0, 0))
    return pl.pallas_call(
        body, name=name, grid=(s // tm,),
        in_specs=[tile, row, tile, tile], out_specs=[tile, tile, row],
        out_shape=[jax.ShapeDtypeStruct((s, d), F32), jax.ShapeDtypeStruct((s, d), BF16),
                   jax.ShapeDtypeStruct((1, d), F32)],
        compiler_params=_params(("arbitrary",)),
    )(x, w, dn, dres)


def _final_fwd_bwd(h2, wf, target, *, name, tm=512):
    s, d = h2.shape
    tm = min(tm, s)

    def body(h_ref, w_ref, t_ref, loss_ref, dh_ref, dhb_ref, dw_ref):
        @pl.when(pl.program_id(0) == 0)
        def _():
            dw_ref[...] = jnp.zeros_like(dw_ref)
            loss_ref[...] = jnp.zeros_like(loss_ref)

        hv = h_ref[...]
        r = lax.rsqrt(jnp.mean(hv * hv, axis=-1, keepdims=True) + NORM_EPS)
        xhat = hv * r
        err = xhat * w_ref[...] - t_ref[...]
        per_tok = jnp.mean(err * err, axis=-1, keepdims=True)
        loss_ref[...] += 0.5 * jnp.sum(per_tok, axis=0, keepdims=True)
        dy = err * (1.0 / d)
        dxhat = dy * w_ref[...]
        dh = r * (dxhat - xhat * jnp.mean(dxhat * xhat, axis=-1, keepdims=True))
        dh_ref[...] = dh
        dhb_ref[...] = dh.astype(BF16)
        dw_ref[...] += jnp.sum(dy * xhat, axis=0, keepdims=True)

    tile = pl.BlockSpec((tm, d), lambda i: (i, 0))
    row = pl.BlockSpec((1, d), lambda i: (0, 0))
    return pl.pallas_call(
        body, name=name, grid=(s // tm,),
        in_specs=[tile, row, tile],
        out_specs=[pl.BlockSpec((1, 1), lambda i: (0, 0)), tile, tile, row],
        out_shape=[jax.ShapeDtypeStruct((1, 1), F32), jax.ShapeDtypeStruct((s, d), F32),
                   jax.ShapeDtypeStruct((s, d), BF16), jax.ShapeDtypeStruct((1, d), F32)],
        compiler_params=_params(("arbitrary",)),
    )(h2, wf, target)


CONV_ROWS = 256
HALO = 8


def _rows_with_halo(ref, r0, rows, s, before, after):
    parts = []
    if before:
        prev = ref[pl.ds(pl.multiple_of(jnp.maximum(r0 - HALO, 0), HALO), HALO), :]
        parts.append(jnp.where(r0 > 0, prev, 0.0))
    parts.append(ref[pl.ds(r0, rows), :])
    if after:
        nxt = ref[pl.ds(pl.multiple_of(jnp.minimum(r0 + rows, s - HALO), HALO), HALO), :]
        parts.append(jnp.where(r0 + rows < s, nxt, 0.0))
    return jnp.concatenate(parts, axis=0) if len(parts) > 1 else parts[0]


def _fill_padded(x_ref, xp, s):
    zeros = jnp.zeros((HALO, xp.shape[1]), F32)
    xp[pl.ds(0, HALO), :] = zeros
    xp[pl.ds(HALO, s), :] = x_ref[...]
    xp[pl.ds(HALO + s, HALO), :] = zeros


def _shifted(xp, r0, k, rows):
    return xp[pl.ds(r0 + HALO - k, rows), :]


def _conv_taps(xp, r0, w_ref, kk, rows):
    acc = None
    for i in range(kk):
        term = w_ref[i:i + 1, :] * _shifted(xp, r0, kk - 1 - i, rows)
        acc = term if acc is None else acc + term
    return acc


def _row_loop(s, step):
    def body(r, carry):
        return step(pl.multiple_of(r * CONV_ROWS, CONV_ROWS), carry)
    return body


def _conv_bwd_rows(xp, r0, dpe, dp_sc, w_ref, kk):
    dp_sc[...] = dpe
    dp = dpe[:CONV_ROWS]
    dx = None
    dws = []
    for i in range(kk):
        dws.append(jnp.sum(dp * _shifted(xp, r0, kk - 1 - i, CONV_ROWS), axis=0, keepdims=True))
        term = w_ref[i:i + 1, :] * dp_sc[pl.ds(kk - 1 - i, CONV_ROWS), :]
        dx = term if dx is None else dx + term
    return dx, dws, jnp.sum(dp, axis=0, keepdims=True)


def _conv_scratch(s, tc, n_padded, n_dp):
    return ([pltpu.VMEM((s + 2 * HALO, tc), F32)] * n_padded
            + [pltpu.VMEM((CONV_ROWS + HALO, tc), F32)] * n_dp)


def _conv_a_fwd(xraw, w, b, *, name, tc=128):
    s, c = xraw.shape
    kk = 4

    def body(x_ref, w_ref, b_ref, o_ref, xp):
        _fill_padded(x_ref, xp, s)

        def step(r0, carry):
            pre = _conv_taps(xp, r0, w_ref, kk, CONV_ROWS) + b_ref[...]
            o_ref[pl.ds(r0, CONV_ROWS), :] = pre * _sigmoid(pre)
            return carry

        lax.fori_loop(0, s // CONV_ROWS, _row_loop(s, step), 0)

    col = pl.BlockSpec((s, tc), lambda j: (0, j))
    return pl.pallas_call(
        body, name=name, grid=(c // tc,),
        in_specs=[col, pl.BlockSpec((8, tc), lambda j: (0, j)), pl.BlockSpec((1, tc), lambda j: (0, j))],
        out_specs=col, out_shape=jax.ShapeDtypeStruct((s, c), F32),
        scratch_shapes=_conv_scratch(s, tc, 1, 0),
        compiler_params=_params(("parallel",)),
    )(xraw, w, b)


def _conv_a_bwd(xraw, w, b, dy, *, name, tc=128):
    s, c = xraw.shape
    kk = 4

    def body(x_ref, w_ref, b_ref, dy_ref, dx_ref, dw_ref, db_ref, xp, dp_sc):
        _fill_padded(x_ref, xp, s)

        def step(r0, carry):
            pre = _conv_taps(xp, r0, w_ref, kk, CONV_ROWS + HALO) + b_ref[...]
            sg = _sigmoid(pre)
            dpe = _rows_with_halo(dy_ref, r0, CONV_ROWS, s, False, True) * (sg * (1.0 + pre * (1.0 - sg)))
            dx, dws, db = _conv_bwd_rows(xp, r0, dpe, dp_sc, w_ref, kk)
            dx_ref[pl.ds(r0, CONV_ROWS), :] = dx.astype(BF16)
            return tuple(acc + new for acc, new in zip(carry, dws + [db]))

        zero = jnp.zeros((1, tc), F32)
        sums = lax.fori_loop(0, s // CONV_ROWS, _row_loop(s, step), (zero,) * (kk + 1))
        db_ref[...] = sums[kk]
        dw_ref[...] = jnp.concatenate(list(sums[:kk]) + [jnp.zeros((8 - kk, tc), F32)], axis=0)

    col = pl.BlockSpec((s, tc), lambda j: (0, j))
    w8 = pl.BlockSpec((8, tc), lambda j: (0, j))
    row = pl.BlockSpec((1, tc), lambda j: (0, j))
    return pl.pallas_call(
        body, name=name, grid=(c // tc,),
        in_specs=[col, w8, row, col], out_specs=[col, w8, row],
        out_shape=[jax.ShapeDtypeStruct((s, c), BF16), jax.ShapeDtypeStruct((8, c), F32),
                   jax.ShapeDtypeStruct((1, c), F32)],
        scratch_shapes=_conv_scratch(s, tc, 1, 1),
        compiler_params=_params(("parallel",)),
    )(xraw, w, b, dy)


def _conv_f_fwd(up_raw, w, b, *, name, tc=128):
    s, c2 = up_raw.shape
    c = c2 // 2
    nb = c // tc
    kk = 3

    def body(xa_ref, xv_ref, wa_ref, wv_ref, ba_ref, bv_ref, o_ref, xap, xvp):
        _fill_padded(xa_ref, xap, s)
        _fill_padded(xv_ref, xvp, s)

        def step(r0, carry):
            a = _conv_taps(xap, r0, wa_ref, kk, CONV_ROWS) + ba_ref[...]
            v = _conv_taps(xvp, r0, wv_ref, kk, CONV_ROWS) + bv_ref[...]
            o_ref[pl.ds(r0, CONV_ROWS), :] = (a * _sigmoid(a) * v).astype(BF16)
            return carry

        lax.fori_loop(0, s // CONV_ROWS, _row_loop(s, step), 0)

    col_a = pl.BlockSpec((s, tc), lambda j: (0, j))
    col_v = pl.BlockSpec((s, tc), lambda j: (0, j + nb))
    return pl.pallas_call(
        body, name=name, grid=(nb,),
        in_specs=[col_a, col_v, pl.BlockSpec((8, tc), lambda j: (0, j)), pl.BlockSpec((8, tc), lambda j: (0, j + nb)),
                  pl.BlockSpec((1, tc), lambda j: (0, j)), pl.BlockSpec((1, tc), lambda j: (0, j + nb))],
        out_specs=col_a, out_shape=jax.ShapeDtypeStruct((s, c), BF16),
        scratch_shapes=_conv_scratch(s, tc, 2, 0),
        compiler_params=_params(("parallel",)),
    )(up_raw, up_raw, w, w, b, b)


def _conv_f_bwd(up_raw, w, b, dact, *, name, tc=128):
    s, c2 = up_raw.shape
    c = c2 // 2
    nb = c // tc
    kk = 3

    def body(xa_ref, xv_ref, wa_ref, wv_ref, ba_ref, bv_ref, d_ref,
             dxa_ref, dxv_ref, dwa_ref, dwv_ref, dba_ref, dbv_ref, xap, xvp, dpa_sc, dpv_sc):
        _fill_padded(xa_ref, xap, s)
        _fill_padded(xv_ref, xvp, s)

        def step(r0, carry):
            a = _conv_taps(xap, r0, wa_ref, kk, CONV_ROWS + HALO) + ba_ref[...]
            v = _conv_taps(xvp, r0, wv_ref, kk, CONV_ROWS + HALO) + bv_ref[...]
            sg = _sigmoid(a)
            d = _rows_with_halo(d_ref, r0, CONV_ROWS, s, False, True)
            dxa, dwas, dba = _conv_bwd_rows(xap, r0, d * v * (sg * (1.0 + a * (1.0 - sg))), dpa_sc, wa_ref, kk)
            dxv, dwvs, dbv = _conv_bwd_rows(xvp, r0, d * (a * sg), dpv_sc, wv_ref, kk)
            dxa_ref[pl.ds(r0, CONV_ROWS), :] = dxa.astype(BF16)
            dxv_ref[pl.ds(r0, CONV_ROWS), :] = dxv.astype(BF16)
            return tuple(acc + new for acc, new in zip(carry, dwas + [dba] + dwvs + [dbv]))

        zero = jnp.zeros((1, tc), F32)
        sums = lax.fori_loop(0, s // CONV_ROWS, _row_loop(s, step), (zero,) * (2 * kk + 2))
        pad = [jnp.zeros((8 - kk, tc), F32)]
        dwa_ref[...] = jnp.concatenate(list(sums[:kk]) + pad, axis=0)
        dba_ref[...] = sums[kk]
        dwv_ref[...] = jnp.concatenate(list(sums[kk + 1:2 * kk + 1]) + pad, axis=0)
        dbv_ref[...] = sums[2 * kk + 1]

    col_a = pl.BlockSpec((s, tc), lambda j: (0, j))
    col_v = pl.BlockSpec((s, tc), lambda j: (0, j + nb))
    w_a = pl.BlockSpec((8, tc), lambda j: (0, j))
    w_v = pl.BlockSpec((8, tc), lambda j: (0, j + nb))
    r_a = pl.BlockSpec((1, tc), lambda j: (0, j))
    r_v = pl.BlockSpec((1, tc), lambda j: (0, j + nb))
    outs = pl.pallas_call(
        body, name=name, grid=(nb,),
        in_specs=[col_a, col_v, w_a, w_v, r_a, r_v, col_a],
        out_specs=[col_a, col_a, w_a, w_a, r_a, r_a],
        out_shape=[jax.ShapeDtypeStruct((s, c), BF16), jax.ShapeDtypeStruct((s, c), BF16),
                   jax.ShapeDtypeStruct((8, c), F32), jax.ShapeDtypeStruct((8, c), F32),
                   jax.ShapeDtypeStruct((1, c), F32), jax.ShapeDtypeStruct((1, c), F32)],
        scratch_shapes=_conv_scratch(s, tc, 2, 2),
        compiler_params=_params(("parallel",)),
    )(up_raw, up_raw, w, w, b, b, dact)
    return outs


def _tri_masks():
    row = lax.broadcasted_iota(jnp.int32, (CHUNK, CHUNK), 0)
    col = lax.broadcasted_iota(jnp.int32, (CHUNK, CHUNK), 1)
    return row >= col, row <= col


def _ssd_fwd(xbc, dt_raw, z, dt_bias, a_log, a_log_x, d_skip_x, norm_w, expand, *, name):
    s = xbc.shape[0]
    nc = s // CHUNK

    def body(xbc_ref, dtr_ref, z_ref, dtb_ref, alog_ref, alogx_ref, dskx_ref, nw_ref, e_ref,
             y_ref, ya_ref, st_ref, state):
        @pl.when(pl.program_id(0) == 0)
        def _():
            state[...] = jnp.zeros_like(state)

        st_ref[0] = state[...]
        lower, _ = _tri_masks()
        dt = _softplus(dtr_ref[...] + dtb_ref[...])
        adt = dt * (-jnp.exp(alog_ref[...]))
        acum = _dot_exact_lhs(lower.astype(BF16), _split3(adt))
        acum_t = acum.T
        dt_terms, acum_terms = _split3(dt), _split3(acum)
        for g in range(SSD_GROUPS):
            sl = slice(GROUP_COLS * g, GROUP_COLS * (g + 1))
            dt_x = _dot_terms(dt_terms, e_ref[:, sl])
            acum_x = _dot_terms(acum_terms, e_ref[:, sl])
            tot_x = jnp.sum(dt_x * (-jnp.exp(alogx_ref[:, sl])), axis=0, keepdims=True)
            xs = xbc_ref[:, sl]
            xdt = xs * dt_x
            xdt_b = xdt.astype(BF16)
            bg = xbc_ref[:, SSD_D_INNER + SSD_STATE * g:SSD_D_INNER + SSD_STATE * (g + 1)].astype(BF16)
            cg = xbc_ref[:, SSD_D_INNER + SSD_BC + SSD_STATE * g:SSD_D_INNER + SSD_BC + SSD_STATE * (g + 1)].astype(BF16)
            cb = _dot(cg, bg, NT)
            st_g = state[:, sl]
            y_off = _dot(cg, st_g.astype(BF16)) * jnp.exp(acum_x)
            parts = []
            for r in range(SSD_HEADS_PER_GROUP):
                h = SSD_HEADS_PER_GROUP * g + r
                dec = jnp.exp(jnp.where(lower, acum[:, h:h + 1] - acum_t[h:h + 1, :], -jnp.inf))
                parts.append(_dot((cb * dec).astype(BF16), xdt_b[:, SSD_HEAD_DIM * r:SSD_HEAD_DIM * (r + 1)]))
            y_ref[:, sl] = jnp.concatenate(parts, axis=1) + y_off + dskx_ref[:, sl] * xs
            wgt = (xdt * jnp.exp(tot_x - acum_x)).astype(BF16)
            state[:, sl] = st_g * jnp.exp(tot_x) + _dot(bg, wgt, TN)
        zv = z_ref[...]
        q = y_ref[...] * (zv * _sigmoid(zv))
        r = lax.rsqrt(jnp.mean(q * q, axis=-1, keepdims=True) + NORM_EPS)
        ya_ref[...] = (q * r * nw_ref[...]).astype(BF16)

    def chunk(w):
        return pl.BlockSpec((CHUNK, w), lambda c: (c, 0))

    def const(shape):
        return pl.BlockSpec(shape, lambda c: (0,) * len(shape))

    return pl.pallas_call(
        body, name=name, grid=(nc,),
        in_specs=[chunk(SSD_XBC), chunk(LANES), chunk(SSD_D_INNER), const((1, LANES)), const((1, LANES)),
                  const((1, SSD_D_INNER)), const((1, SSD_D_INNER)), const((1, SSD_D_INNER)),
                  const((LANES, SSD_D_INNER))],
        out_specs=[chunk(SSD_D_INNER), chunk(SSD_D_INNER),
                   pl.BlockSpec((1, SSD_STATE, SSD_D_INNER), lambda c: (c, 0, 0))],
        out_shape=[jax.ShapeDtypeStruct((s, SSD_D_INNER), F32), jax.ShapeDtypeStruct((s, SSD_D_INNER), BF16),
                   jax.ShapeDtypeStruct((nc, SSD_STATE, SSD_D_INNER), F32)],
        scratch_shapes=[pltpu.VMEM((SSD_STATE, SSD_D_INNER), F32)],
        compiler_params=_params(("arbitrary",)),
    )(xbc, dt_raw, z, dt_bias, a_log, a_log_x, d_skip_x, norm_w, expand)


def _ssd_bwd(dya, y, z, xbc, dt_raw, states, dt_bias, a_log, a_log_x, d_skip_x, norm_w, expand, expand_t, *, name):
    s = xbc.shape[0]
    nc = s // CHUNK

    def body(dya_ref, y_ref, z_ref, xbc_ref, dtr_ref, stp_ref, dtb_ref, alog_ref, alogx_ref, dskx_ref, nw_ref,
             e_ref, et_ref, dz_ref, dxbc_ref, ddt_ref, dnw_ref, ddsk_ref, dalog_ref, ddtb_ref,
             dstate, dy_sc, dskcol):
        i = pl.program_id(0)

        @pl.when(i == 0)
        def _():
            dstate[...] = jnp.zeros_like(dstate)
            dskcol[...] = jnp.zeros_like(dskcol)
            dnw_ref[...] = jnp.zeros_like(dnw_ref)
            dalog_ref[...] = jnp.zeros_like(dalog_ref)
            ddtb_ref[...] = jnp.zeros_like(ddtb_ref)
            ddsk_ref[...] = jnp.zeros_like(ddsk_ref)

        lower, upper = _tri_masks()
        rows = lax.broadcasted_iota(jnp.int32, (CHUNK, LANES), 0)
        pre = dtr_ref[...] + dtb_ref[...]
        dt = _softplus(pre)
        a = -jnp.exp(alog_ref[...])
        acum = _dot_exact_lhs(lower.astype(BF16), _split3(dt * a))
        acum_t = acum.T
        dt_terms, acum_terms = _split3(dt), _split3(acum)

        yv = y_ref[...]
        zv = z_ref[...]
        sz = _sigmoid(zv)
        silu_z = zv * sz
        q = yv * silu_z
        r = lax.rsqrt(jnp.mean(q * q, axis=-1, keepdims=True) + NORM_EPS)
        qhat = q * r
        dyav = dya_ref[...]
        dqhat = dyav * nw_ref[...]
        dnw_ref[...] += jnp.sum(dyav * qhat, axis=0, keepdims=True)
        dq = r * (dqhat - qhat * jnp.mean(dqhat * qhat, axis=-1, keepdims=True))
        dy_sc[...] = dq * silu_z
        dz_ref[...] = (dq * yv * (sz * (1.0 + zv * (1.0 - sz)))).astype(BF16)

        da_cum = jnp.zeros((CHUNK, LANES), F32)
        ddt = jnp.zeros((CHUNK, LANES), F32)
        for g in range(SSD_GROUPS):
            sl = slice(GROUP_COLS * g, GROUP_COLS * (g + 1))
            et_g = et_ref[sl, :]
            dt_x = _dot_terms(dt_terms, e_ref[:, sl])
            acum_x = _dot_terms(acum_terms, e_ref[:, sl])
            tot_x = jnp.sum(dt_x * (-jnp.exp(alogx_ref[:, sl])), axis=0, keepdims=True)
            e_tot = jnp.exp(tot_x)
            dec_s = jnp.exp(tot_x - acum_x)
            xs = xbc_ref[:, sl]
            xdt = xs * dt_x
            xdt_b = xdt.astype(BF16)
            dy = dy_sc[:, sl]
            dy_b = dy.astype(BF16)
            dskx = dskx_ref[:, sl]
            y_ssd = y_ref[:, sl] - dskx * xs
            dskcol[:, sl] += jnp.sum(dy * xs, axis=0, keepdims=True)
            bg = xbc_ref[:, SSD_D_INNER + SSD_STATE * g:SSD_D_INNER + SSD_STATE * (g + 1)].astype(BF16)
            cg = xbc_ref[:, SSD_D_INNER + SSD_BC + SSD_STATE * g:SSD_D_INNER + SSD_BC + SSD_STATE * (g + 1)].astype(BF16)
            cb_t = _dot(bg, cg, NT)
            sp = stp_ref[0, :, sl]
            ds_g = dstate[:, sl]
            ds_b = ds_g.astype(BF16)
            dye_b = (dy * jnp.exp(acum_x)).astype(BF16)
            dc = _dot(dye_b, sp.astype(BF16), NT)
            dxdt_state = dec_s * _dot(bg, ds_b)
            db = _dot((xdt * dec_s).astype(BF16), ds_b, NT)
            dcb_t = jnp.zeros((CHUNK, CHUNK), F32)
            parts = []
            for rr in range(SSD_HEADS_PER_GROUP):
                h = SSD_HEADS_PER_GROUP * g + rr
                hs = slice(SSD_HEAD_DIM * rr, SSD_HEAD_DIM * (rr + 1))
                dec_t = jnp.exp(jnp.where(upper, acum_t[h:h + 1, :] - acum[:, h:h + 1], -jnp.inf))
                parts.append(_dot((cb_t * dec_t).astype(BF16), dy_b[:, hs]))
                dcb_t = dcb_t + _dot(xdt_b[:, hs], dy_b[:, hs], NT) * dec_t
            dxdt = jnp.concatenate(parts, axis=1) + dxdt_state
            dcb_tb = dcb_t.astype(BF16)
            dc = dc + _dot(dcb_tb, bg, TN)
            db = db + _dot(dcb_tb, cg)
            tot_col = jnp.sum(ds_g * sp, axis=0, keepdims=True) * e_tot + jnp.sum(dxdt_state * xdt, axis=0, keepdims=True)
            d_tot = _dot_terms(_split3(jnp.broadcast_to(tot_col, (8, GROUP_COLS))), et_g)
            d_tot = jnp.max(d_tot, axis=0, keepdims=True)
            pair_sums = dy_b.astype(F32) * y_ssd - xdt_b.astype(F32) * dxdt
            da_cum = da_cum + _dot_terms(_split3(pair_sums), et_g) + jnp.where(rows == CHUNK - 1, d_tot, 0.0)
            ddt = ddt + _dot_terms(_split3(dxdt * xs), et_g)
            dxbc_ref[:, sl] = dy * dskx + dxdt * dt_x
            dxbc_ref[:, SSD_D_INNER + SSD_STATE * g:SSD_D_INNER + SSD_STATE * (g + 1)] = db
            dxbc_ref[:, SSD_D_INNER + SSD_BC + SSD_STATE * g:SSD_D_INNER + SSD_BC + SSD_STATE * (g + 1)] = dc
            dstate[:, sl] = e_tot * ds_g + _dot(cg, dye_b, TN)

        dadt = _dot_exact_lhs(upper.astype(BF16), _split3(da_cum))
        ddt = ddt + dadt * a
        dalog_ref[...] += jnp.sum(dadt * dt, axis=0, keepdims=True)
        dpre = ddt * _sigmoid(pre)
        ddtb_ref[...] += jnp.sum(dpre, axis=0, keepdims=True)
        ddt_ref[...] = dpre.astype(BF16)

        @pl.when(i == nc - 1)
        def _():
            dalog_ref[...] = dalog_ref[...] * a
            dsk = _dot_terms(_split3(jnp.broadcast_to(dskcol[...], (8, SSD_D_INNER))), et_ref[...])
            ddsk_ref[...] = jnp.max(dsk, axis=0, keepdims=True)

    def chunk(w):
        return pl.BlockSpec((CHUNK, w), lambda i: (nc - 1 - i, 0))

    def const(shape):
        return pl.BlockSpec(shape, lambda i: (0,) * len(shape))

    return pl.pallas_call(
        body, name=name, grid=(nc,),
        in_specs=[chunk(SSD_D_INNER), chunk(SSD_D_INNER), chunk(SSD_D_INNER), chunk(SSD_XBC), chunk(LANES),
                  pl.BlockSpec((1, SSD_STATE, SSD_D_INNER), lambda i: (nc - 1 - i, 0, 0)),
                  const((1, LANES)), const((1, LANES)), const((1, SSD_D_INNER)), const((1, SSD_D_INNER)),
                  const((1, SSD_D_INNER)), const((LANES, SSD_D_INNER)), const((SSD_D_INNER, LANES))],
        out_specs=[chunk(SSD_D_INNER), chunk(SSD_XBC), chunk(LANES), const((1, SSD_D_INNER)), const((1, LANES)),
                   const((1, LANES)), const((1, LANES))],
        out_shape=[jax.ShapeDtypeStruct((s, SSD_D_INNER), BF16), jax.ShapeDtypeStruct((s, SSD_XBC), F32),
                   jax.ShapeDtypeStruct((s, LANES), BF16), jax.ShapeDtypeStruct((1, SSD_D_INNER), F32),
                   jax.ShapeDtypeStruct((1, LANES), F32), jax.ShapeDtypeStruct((1, LANES), F32),
                   jax.ShapeDtypeStruct((1, LANES), F32)],
        scratch_shapes=[pltpu.VMEM((SSD_STATE, SSD_D_INNER), F32), pltpu.VMEM((CHUNK, SSD_D_INNER), F32),
                        pltpu.VMEM((1, SSD_D_INNER), F32)],
        compiler_params=_params(("arbitrary",)),
    )(dya, y, z, xbc, dt_raw, states, dt_bias, a_log, a_log_x, d_skip_x, norm_w, expand, expand_t)


GELU_K = math.sqrt(2.0 / math.pi)
GELU_C = 0.044715


def _gelu(x):
    return 0.5 * x * (1.0 + jnp.tanh(GELU_K * (x + GELU_C * x * x * x)))


def _gelu_grad(x):
    t = jnp.tanh(GELU_K * (x + GELU_C * x * x * x))
    return 0.5 * (1.0 + t) + 0.5 * x * (1.0 - t * t) * (GELU_K * (1.0 + 3.0 * GELU_C * x * x))


def _sgu_pre(uv_ref, uvb_ref, lnw_ref, lnb_ref):
    uv = uv_ref[...] + uvb_ref[...]
    guv = _gelu(uv)
    u = guv[:, :SGU_WIDTH]
    v = guv[:, SGU_WIDTH:]
    mu = jnp.mean(v, axis=-1, keepdims=True)
    vc = v - mu
    rstd = lax.rsqrt(jnp.mean(vc * vc, axis=-1, keepdims=True) + LN_EPS)
    vhat = vc * rstd
    vn = vhat * lnw_ref[...] + lnb_ref[...]
    return uv, u, vhat, rstd, vn


def _sgu_fwd(uv_raw, uv_b, ln_w, ln_b, w_sp, b_sp_t, *, name):
    s = uv_raw.shape[0]
    nc = s // CHUNK

    def body(uv_ref, uvb_ref, lnw_ref, lnb_ref, w_ref, bt_ref, o_ref):
        lower, _ = _tri_masks()
        _, u, _, _, vn = _sgu_pre(uv_ref, uvb_ref, lnw_ref, lnb_ref)
        vn_b = vn.astype(BF16)
        bt = bt_ref[...]
        for g in range(SGU_GROUPS):
            gs = slice(LANES * g, LANES * (g + 1))
            wc = jnp.where(lower, w_ref[g], 0.0).astype(BF16)
            mixed = _dot(wc, vn_b[:, gs]) + bt[:, g:g + 1]
            o_ref[:, gs] = (u[:, gs] * mixed).astype(BF16)

    def const(shape):
        return pl.BlockSpec(shape, lambda c: (0,) * len(shape))

    return pl.pallas_call(
        body, name=name, grid=(nc,),
        in_specs=[pl.BlockSpec((CHUNK, 2 * SGU_WIDTH), lambda c: (c, 0)), const((1, 2 * SGU_WIDTH)),
                  const((1, SGU_WIDTH)), const((1, SGU_WIDTH)), const((SGU_GROUPS, CHUNK, CHUNK)),
                  const((CHUNK, LANES))],
        out_specs=pl.BlockSpec((CHUNK, SGU_WIDTH), lambda c: (c, 0)),
        out_shape=jax.ShapeDtypeStruct((s, SGU_WIDTH), BF16),
        compiler_params=_params(("parallel",)),
    )(uv_raw, uv_b, ln_w, ln_b, w_sp, b_sp_t)


def _sgu_bwd(uv_raw, dyb, uv_b, ln_w, ln_b, w_sp, b_sp_t, group_sum, *, name):
    s = uv_raw.shape[0]
    nc = s // CHUNK

    def body(uv_ref, dy_ref, uvb_ref, lnw_ref, lnb_ref, w_ref, bt_ref, gsum_ref,
             duv_ref, dw_ref, dbt_ref, dlnw_ref, dlnb_ref, duvb_ref):
        @pl.when(pl.program_id(0) == 0)
        def _():
            dw_ref[...] = jnp.zeros_like(dw_ref)
            dbt_ref[...] = jnp.zeros_like(dbt_ref)
            dlnw_ref[...] = jnp.zeros_like(dlnw_ref)
            dlnb_ref[...] = jnp.zeros_like(dlnb_ref)
            duvb_ref[...] = jnp.zeros_like(duvb_ref)

        lower, _ = _tri_masks()
        uv, u, vhat, rstd, vn = _sgu_pre(uv_ref, uvb_ref, lnw_ref, lnb_ref)
        vn_b = vn.astype(BF16)
        bt = bt_ref[...]
        dy = dy_ref[...]
        du_parts, dvn_parts, dmix_parts = [], [], []
        for g in range(SGU_GROUPS):
            gs = slice(LANES * g, LANES * (g + 1))
            wc = jnp.where(lower, w_ref[g], 0.0).astype(BF16)
            mixed = _dot(wc, vn_b[:, gs]) + bt[:, g:g + 1]
            du_parts.append(dy[:, gs] * mixed)
            dmix = dy[:, gs] * u[:, gs]
            dmix_b = dmix.astype(BF16)
            dmix_parts.append(dmix)
            dw_ref[g] += jnp.where(lower, _dot(dmix_b, vn_b[:, gs], NT), 0.0)
            dvn_parts.append(_dot(wc, dmix_b, TN))
        dmixed = jnp.concatenate(dmix_parts, axis=1)
        dbt_ref[...] += _dot_terms(_split3(dmixed), gsum_ref[...])
        dvn = jnp.concatenate(dvn_parts, axis=1)
        dlnw_ref[...] += jnp.sum(dvn * vhat, axis=0, keepdims=True)
        dlnb_ref[...] += jnp.sum(dvn, axis=0, keepdims=True)
        dvhat = dvn * lnw_ref[...]
        dv = rstd * (dvhat - jnp.mean(dvhat, axis=-1, keepdims=True)
                     - vhat * jnp.mean(dvhat * vhat, axis=-1, keepdims=True))
        dguv = jnp.concatenate(du_parts + [dv], axis=1)
        duv = dguv * _gelu_grad(uv)
        duvb_ref[...] += jnp.sum(duv, axis=0, keepdims=True)
        duv_ref[...] = duv.astype(BF16)

    def const(shape):
        return pl.BlockSpec(shape, lambda c: (0,) * len(shape))

    return pl.pallas_call(
        body, name=name, grid=(nc,),
        in_specs=[pl.BlockSpec((CHUNK, 2 * SGU_WIDTH), lambda c: (c, 0)),
                  pl.BlockSpec((CHUNK, SGU_WIDTH), lambda c: (c, 0)), const((1, 2 * SGU_WIDTH)),
                  const((1, SGU_WIDTH)), const((1, SGU_WIDTH)), const((SGU_GROUPS, CHUNK, CHUNK)),
                  const((CHUNK, LANES)), const((SGU_WIDTH, LANES))],
        out_specs=[pl.BlockSpec((CHUNK, 2 * SGU_WIDTH), lambda c: (c, 0)), const((SGU_GROUPS, CHUNK, CHUNK)),
                   const((CHUNK, LANES)), const((1, SGU_WIDTH)), const((1, SGU_WIDTH)), const((1, 2 * SGU_WIDTH))],
        out_shape=[jax.ShapeDtypeStruct((s, 2 * SGU_WIDTH), BF16),
                   jax.ShapeDtypeStruct((SGU_GROUPS, CHUNK, CHUNK), F32), jax.ShapeDtypeStruct((CHUNK, LANES), F32),
                   jax.ShapeDtypeStruct((1, SGU_WIDTH), F32), jax.ShapeDtypeStruct((1, SGU_WIDTH), F32),
                   jax.ShapeDtypeStruct((1, 2 * SGU_WIDTH), F32)],
        compiler_params=_params(("arbitrary",)),
    )(uv_raw, dyb, uv_b, ln_w, ln_b, w_sp, b_sp_t, group_sum)


def _gate_fwd(gates_raw, b_gate, p_a, p_b, *, name, tm=512):
    s = p_a.shape[0]
    tm = min(tm, s)

    def body(ga_ref, gb_ref, ba_ref, bb_ref, pa_ref, pb_ref, o_ref):
        ga = _sigmoid(ga_ref[...] + ba_ref[...])
        gb = _sigmoid(gb_ref[...] + bb_ref[...])
        o_ref[...] = (ga * pa_ref[...] + gb * pb_ref[...]).astype(BF16)

    t_a = pl.BlockSpec((tm, D_MODEL), lambda i: (i, 0))
    t_b = pl.BlockSpec((tm, D_MODEL), lambda i: (i, 1))
    r_a = pl.BlockSpec((1, D_MODEL), lambda i: (0, 0))
    r_b = pl.BlockSpec((1, D_MODEL), lambda i: (0, 1))
    return pl.pallas_call(
        body, name=name, grid=(s // tm,),
        in_specs=[t_a, t_b, r_a, r_b, t_a, t_a], out_specs=t_a,
        out_shape=jax.ShapeDtypeStruct((s, D_MODEL), BF16),
        compiler_params=_params(("parallel",)),
    )(gates_raw, gates_raw, b_gate, b_gate, p_a, p_b)


def _gate_bwd(gates_raw, b_gate, p_a, p_b, dm, *, name, tm=512):
    s = p_a.shape[0]
    tm = min(tm, s)

    def body(ga_ref, gb_ref, ba_ref, bb_ref, pa_ref, pb_ref, dm_ref, dpa_ref, dpb_ref, dga_ref, dgb_ref,
             dba_ref, dbb_ref):
        @pl.when(pl.program_id(0) == 0)
        def _():
            dba_ref[...] = jnp.zeros_like(dba_ref)
            dbb_ref[...] = jnp.zeros_like(dbb_ref)

        d = dm_ref[...]
        for g_ref, b_ref, p_ref, dp_ref, dg_ref, db_ref in ((ga_ref, ba_ref, pa_ref, dpa_ref, dga_ref, dba_ref),
                                                            (gb_ref, bb_ref, pb_ref, dpb_ref, dgb_ref, dbb_ref)):
            sg = _sigmoid(g_ref[...] + b_ref[...])
            dp_ref[...] = (d * sg).astype(BF16)
            dg = d * p_ref[...] * (sg * (1.0 - sg))
            dg_ref[...] = dg.astype(BF16)
            db_ref[...] += jnp.sum(dg, axis=0, keepdims=True)

    t_a = pl.BlockSpec((tm, D_MODEL), lambda i: (i, 0))
    t_b = pl.BlockSpec((tm, D_MODEL), lambda i: (i, 1))
    r_a = pl.BlockSpec((1, D_MODEL), lambda i: (0, 0))
    r_b = pl.BlockSpec((1, D_MODEL), lambda i: (0, 1))
    big = jax.ShapeDtypeStruct((s, D_MODEL), BF16)
    row = jax.ShapeDtypeStruct((1, D_MODEL), F32)
    return pl.pallas_call(
        body, name=name, grid=(s // tm,),
        in_specs=[t_a, t_b, r_a, r_b, t_a, t_a, t_a], out_specs=[t_a, t_a, t_a, t_a, r_a, r_a],
        out_shape=[big, big, big, big, row, row],
        compiler_params=_params(("arbitrary",)),
    )(gates_raw, gates_raw, b_gate, b_gate, p_a, p_b, dm)


def _adamw_update(w_ref, g_ref, m_ref, v_ref, d_ref, mo_ref, vo_ref):
    gv = g_ref[...]
    mn = ADAM_B1 * m_ref[...] + (1.0 - ADAM_B1) * gv
    vn = ADAM_B2 * v_ref[...] + (1.0 - ADAM_B2) * (gv * gv)
    m_hat = mn / (1.0 - ADAM_B1 ** ADAM_STEP)
    v_hat = vn / (1.0 - ADAM_B2 ** ADAM_STEP)
    d_ref[...] = -ADAM_LR * (m_hat / (jnp.sqrt(v_hat) + ADAM_EPS) + ADAM_WD * w_ref[...])
    mo_ref[...] = mn
    vo_ref[...] = vn


def _adamw_many(ws, gs, ms, vs, *, name):
    n = len(ws)

    def body(*refs):
        for i in range(n):
            _adamw_update(*[refs[k * n + i] for k in range(7)])

    whole = pl.BlockSpec(memory_space=pltpu.VMEM)
    sds = [jax.ShapeDtypeStruct(w.shape, F32) for w in ws]
    outs = pl.pallas_call(
        body, name=name, in_specs=[whole] * (4 * n), out_specs=[whole] * (3 * n), out_shape=sds * 3,
        compiler_params=pltpu.CompilerParams(vmem_limit_bytes=VMEM_LIMIT),
    )(*ws, *gs, *ms, *vs)
    return outs[:n], outs[n:2 * n], outs[2 * n:]


def _adamw(w, g, m, v, *, name, tr=128):
    r, c = w.shape
    tr = min(tr, r)
    assert r % tr == 0, (name, r, tr)
    body = functools.partial(_adamw_update)

    blk = pl.BlockSpec((tr, c), lambda i: (i, 0))
    sds = jax.ShapeDtypeStruct((r, c), F32)
    return pl.pallas_call(
        body, name=name, grid=(r // tr,), in_specs=[blk] * 4, out_specs=[blk] * 3, out_shape=[sds] * 3,
        compiler_params=_params(("parallel",)),
    )(w, g, m, v)


def _adamw_two_sums(w, g_a, g_b, m, v, *, name, tr=128):
    r, c = w.shape
    tr = min(tr, r)
    assert r % tr == 0, (name, r, tr)

    def body(w_ref, ga_ref, gb_ref, m_ref, v_ref, g_ref, d_ref, mo_ref, vo_ref):
        g_ref[...] = ga_ref[...] + gb_ref[...]
        _adamw_update(w_ref, g_ref, m_ref, v_ref, d_ref, mo_ref, vo_ref)

    blk = pl.BlockSpec((tr, c), lambda i: (i, 0))
    sds = jax.ShapeDtypeStruct((r, c), F32)
    return pl.pallas_call(
        body, name=name, grid=(r // tr,), in_specs=[blk] * 5, out_specs=[blk] * 4, out_shape=[sds] * 4,
        compiler_params=_params(("parallel",)),
    )(w, g_a, g_b, m, v)


def _tile(n, pref):
    if n <= pref:
        return n
    best = LANES
    for t in range(LANES, pref + 1, LANES):
        if n % t == 0:
            best = t
    return best


MATMUL_BLOCK_BYTES = 20 * 1024 * 1024


def _mm(pairs, name, **kw):
    trans_b = kw.get("trans_b", False)
    m = (pairs[0][0][0] if isinstance(pairs[0][0], tuple) else pairs[0][0]).shape[0]
    ktot, n = 0, None
    for _, b in pairs:
        shape = b[0].shape[1:] if isinstance(b, tuple) else b.shape
        ktot += shape[1] if trans_b else shape[0]
        n = shape[0] if trans_b else shape[1]
    out_bytes = 4 * (2 if kw.get("add") is not None else 1)
    best = None
    for tm in (256, 512, 1024):
        for tn in range(LANES, min(n, 1536) + 1, LANES):
            if m % min(tm, m) or n % tn:
                continue
            fits = 2 * ktot * (min(tm, m) + tn) + out_bytes * min(tm, m) * tn <= MATMUL_BLOCK_BYTES
            if fits and (best is None or min(tm, m) * tn >= best[0] * best[1]):
                best = (min(tm, m), tn)
    return _matmul(pairs, tm=best[0], tn=best[1], name=name, **kw)


def _wgrad(a, b, name, **kw):
    return _matmul_tn(a, b, tk=_tile(a.shape[1], 1408), tn=kw.pop("tn", _tile(b.shape[1], 1024)), tm=2048,
                      name=name, **kw)


def _local_step(x, target, get_weight, small, emit_grad):
    heads = jnp.arange(SSD_D_INNER) // SSD_HEAD_DIM
    expand = (jnp.arange(LANES)[:, None] == heads[None, :]).astype(BF16)
    expand_t = expand.T
    group_sum = (jnp.arange(SGU_WIDTH)[:, None] // LANES == jnp.arange(LANES)[None, :]).astype(BF16)
    pad_h = LANES - SSD_HEADS
    dt_bias = jnp.pad(small["dt_bias"], ((0, 0), (0, pad_h)))
    a_log = jnp.pad(small["a_log"], ((0, 0), (0, pad_h)))
    a_log_x = jnp.repeat(small["a_log"], SSD_HEAD_DIM, axis=1)
    d_skip_x = jnp.repeat(small["d_skip"], SSD_HEAD_DIM, axis=1)
    b_sp_t = jnp.pad(small["b_spatial"][0].T, ((0, 0), (0, LANES - SGU_GROUPS)))
    w_sp = small["w_spatial"][0]
    conv_a_w = jnp.pad(small["conv_a_w"], ((0, 4), (0, 0)))
    conv_f_w = jnp.pad(small["conv_f_w"], ((0, 5), (0, 0)))
    final_w = small["final_norm_w"].reshape(1, D_MODEL)

    n1 = _rms_fwd(x, small["norm1_w"], name="rms1_fwd")
    wts = dict(get_weight("w_in", n1))
    z = _mm([(n1, wts["in_z"])], "in_z")
    xbc_raw = _mm([(n1, wts["in_xbc"])], "in_xbc")
    dt_raw = _mm([(n1, wts["in_dt"])], "in_dt")
    uv_raw = _mm([(n1, wts["in_uv"])], "in_uv")
    gates_raw = _mm([(n1, wts["in_gate"])], "in_gate")
    xbc = _conv_a_fwd(xbc_raw, conv_a_w, small["conv_a_b"], name="conv_a_fwd")
    y, y_a, states = _ssd_fwd(xbc, dt_raw, z, dt_bias, a_log, a_log_x, d_skip_x, small["ssd_norm_w"], expand,
                              name="ssd_fwd")
    y_b = _sgu_fwd(uv_raw, small["uv_b"], small["v_ln_w"], small["v_ln_b"], w_sp, b_sp_t, name="sgu_fwd")
    wts.update(get_weight("w_branch", y_b))
    p_a = _mm([(y_a, wts["branch_a"])], "branch_a")
    p_b = _mm([(y_b, wts["branch_b"])], "branch_b")
    mix = _gate_fwd(gates_raw, small["b_gate"], p_a, p_b, name="gate_fwd")
    wts.update(get_weight("w_out", mix))
    h1 = _mm([(mix, wts["out"])], "out_proj", add=x)
    n2 = _rms_fwd(h1, small["norm2_w"], name="rms2_fwd")
    wts.update(get_weight("w_up", n2))
    up_w = wts["up"]
    up_cols = up_w.shape[2]
    up_raw = _matmul([(n2, (up_w, "cols"))], tm=1024, tn=up_cols, name="up_proj")
    act = _conv_f_fwd(up_raw, conv_f_w, small["conv_f_b"], name="conv_f_fwd")
    wts.update(get_weight("w_down", act))
    h2 = _mm([(act, wts["down"])], "down_proj", add=h1)
    loss, dh2, dh2_b, d_final = _final_fwd_bwd(h2, final_w, target, name="final_norm_loss")

    dact = _mm([(dh2_b, wts["down"])], "down_dgrad", trans_b=True)
    started = emit_grad("w_down", _wgrad(act, dh2_b, "down_wgrad"))
    dup_a, dup_v, dwf_a, dwf_v, dbf_a, dbf_v = _conv_f_bwd(up_raw, conv_f_w, small["conv_f_b"], dact,
                                                           name="conv_f_bwd")
    dn2 = _mm([((dup_a, 0), (up_w, 0)), ((dup_a, 1), (up_w, 1)), ((dup_v, 0), (up_w, 2)), ((dup_v, 1), (up_w, 3))],
              "up_dgrad", trans_b=True, after=started)
    started = emit_grad("w_up", jnp.concatenate([_wgrad(n2, dup_a, "up_wgrad_a", tn=up_cols, stack_out=True),
                                                 _wgrad(n2, dup_v, "up_wgrad_v", tn=up_cols, stack_out=True)], axis=0))
    dh1, dh1_b, d_norm2 = _rms_bwd(h1, small["norm2_w"], dn2, dh2, name="rms2_bwd")
    dmix = _mm([(dh1_b, wts["out"])], "out_dgrad", trans_b=True, after=started)
    started = emit_grad("w_out", _wgrad(mix, dh1_b, "out_wgrad"))
    dp_a, dp_b, dg_a, dg_b, dbg_a, dbg_b = _gate_bwd(gates_raw, small["b_gate"], p_a, p_b, dmix, name="gate_bwd")
    dya = _mm([(dp_a, wts["branch_a"])], "branch_a_dgrad", trans_b=True, after=started)
    dyb = _mm([(dp_b, wts["branch_b"])], "branch_b_dgrad", trans_b=True)
    started_branch = emit_grad("w_branch", jnp.concatenate([_wgrad(y_a, dp_a, "branch_a_wgrad"),
                                                            _wgrad(y_b, dp_b, "branch_b_wgrad")], axis=0))
    duv, d_wsp, d_bsp_t, d_lnw, d_lnb, d_uvb = _sgu_bwd(uv_raw, dyb, small["uv_b"], small["v_ln_w"],
                                                        small["v_ln_b"], w_sp, b_sp_t, group_sum, name="sgu_bwd")
    dz, dxbc, ddt, d_ssd_nw, d_dskip, d_alog, d_dtb = _ssd_bwd(
        dya, y, z, xbc, dt_raw, states, dt_bias, a_log, a_log_x, d_skip_x, small["ssd_norm_w"], expand, expand_t,
        name="ssd_bwd")
    dxbc_raw, d_conv_a_w, d_conv_a_b = _conv_a_bwd(xbc_raw, conv_a_w, small["conv_a_b"], dxbc, name="conv_a_bwd")
    started = emit_grad("w_in", {
        "in_z": _wgrad(n1, dz, "in_z_wgrad", after=started_branch), "in_xbc": _wgrad(n1, dxbc_raw, "in_xbc_wgrad"),
        "in_dt": _wgrad(n1, ddt, "in_dt_wgrad")[:, :SSD_HEADS], "in_uv": _wgrad(n1, duv, "in_uv_wgrad"),
        "in_gate_a": _wgrad(n1, dg_a, "in_gate_a_wgrad"), "in_gate_b": _wgrad(n1, dg_b, "in_gate_b_wgrad")})
    dn1 = _mm([(dz, wts["in_z"]), (dxbc_raw, wts["in_xbc"]), (ddt, wts["in_dt"]), (duv, wts["in_uv"]),
               (dg_a, wts["in_gate_a"]), (dg_b, wts["in_gate_b"])], "in_dgrad", trans_b=True, after=started)
    dx, _, d_norm1 = _rms_bwd(x, small["norm1_w"], dn1, dh1, name="rms1_bwd")

    grads_small = {
        "norm1_w": d_norm1, "b_gate": jnp.concatenate([dbg_a, dbg_b], axis=1),
        "conv_a_w": d_conv_a_w[:4], "conv_a_b": d_conv_a_b,
        "dt_bias": d_dtb[:, :SSD_HEADS], "a_log": d_alog[:, :SSD_HEADS], "d_skip": d_dskip[:, :SSD_HEADS],
        "ssd_norm_w": d_ssd_nw, "uv_b": d_uvb, "v_ln_w": d_lnw, "v_ln_b": d_lnb,
        "w_spatial": d_wsp[None], "b_spatial": d_bsp_t[:, :SGU_GROUPS].T[None],
        "norm2_w": d_norm2, "conv_f_w": jnp.concatenate([dwf_a[:3], dwf_v[:3]], axis=1),
        "conv_f_b": jnp.concatenate([dbf_a, dbf_v], axis=1), "final_norm_w": d_final.reshape(D_MODEL),
    }
    return loss, dx, grads_small


HBM = pl.BlockSpec(memory_space=pl.ANY)
MESH = pl.DeviceIdType.MESH


def _mesh_pos():
    return lax.axis_index("x"), lax.axis_index("y"), lax.axis_index("c")


def _other_chips(x, y):
    return [(1 - x, y), (x, 1 - y), (1 - x, 1 - y)]


def _remote(src, dst, send_sems, recv_sems, k, dev):
    return pltpu.make_async_remote_copy(src_ref=src, dst_ref=dst, send_sem=send_sems.at[k], recv_sem=recv_sems.at[k],
                                        device_id=dev, device_id_type=MESH)


def _dma_sems(n):
    return [pltpu.SemaphoreType.DMA((n,)), pltpu.SemaphoreType.DMA((n,))]


HBM_ONLY = pl.BlockSpec(memory_space=pltpu.HBM)
SEMAPHORES = pl.BlockSpec(memory_space=pltpu.SEMAPHORE)
DATAFLOW_EFFECT = pltpu.SideEffectType.DATAFLOW_SIDE_EFFECTING
N_PEER_CHIPS = N_CHIPS - 1


def _gather_sends(w_ref, land_ref, send_sems, recv_sems):
    x, y, c = _mesh_pos()
    return [_remote(w_ref.at[c], land_ref.at[2 * x + y, c], send_sems, recv_sems, k, (px, py, c))
            for k, (px, py) in enumerate(_other_chips(x, y))]


def _gather_arrivals(w_ref, land_ref, send_sems, recv_sems):
    x, y, c = _mesh_pos()
    return [_remote(w_ref.at[c], land_ref.at[2 * px + py, c], send_sems, recv_sems, k, (px, py, c))
            for k, (px, py) in enumerate(_other_chips(x, y))]


def _gather_whole_sends(w_ref, land_ref, send_sems, recv_sems):
    x, y, c = _mesh_pos()
    return [_remote(w_ref, land_ref.at[2 * x + y], send_sems, recv_sems, k, (px, py, c))
            for k, (px, py) in enumerate(_other_chips(x, y))]


def _gather_whole_arrivals(w_ref, land_ref, send_sems, recv_sems):
    x, y, c = _mesh_pos()
    return [_remote(w_ref, land_ref.at[2 * px + py], send_sems, recv_sems, k, (px, py, c))
            for k, (px, py) in enumerate(_other_chips(x, y))]


def _scatter_sends(h_ref, land_ref, send_sems, recv_sems):
    x, y, c = _mesh_pos()
    return [_remote(h_ref.at[2 * px + py], land_ref.at[2 * x + y], send_sems, recv_sems, k, (px, py, c))
            for k, (px, py) in enumerate(_other_chips(x, y))]


def _scatter_arrivals(h_ref, land_ref, send_sems, recv_sems):
    x, y, c = _mesh_pos()
    return [_remote(h_ref.at[2 * x + y], land_ref.at[2 * px + py], send_sems, recv_sems, k, (px, py, c))
            for k, (px, py) in enumerate(_other_chips(x, y))]


def _exchange_start(sources, landing_shapes, sends, *, after=None, name):
    n = len(sources)
    extra = [] if after is None else [after]

    def body(*refs):
        sems = refs[2 * n + len(extra):4 * n + len(extra)]
        for i in range(n):
            send_i = sends[i] if isinstance(sends, (list, tuple)) else sends
            for cp in send_i(refs[i], refs[n + i], sems[2 * i], sems[2 * i + 1]):
                cp.start()
        refs[-1][...] = jnp.zeros_like(refs[-1])

    hbm = [pltpu.HBM(s.shape, s.dtype) for s in sources] + [pltpu.HBM(shp, s.dtype)
                                                             for shp, s in zip(landing_shapes, sources)]
    outs = pl.pallas_call(
        body, name=name,
        out_shape=tuple([pltpu.SemaphoreType.DMA((N_PEER_CHIPS,))] * (2 * n) + hbm
                        + [jax.ShapeDtypeStruct((8, LANES), F32)]),
        in_specs=[HBM_ONLY] * (2 * n) + [pl.BlockSpec(memory_space=pl.ANY)] * len(extra),
        out_specs=tuple([SEMAPHORES] * (2 * n) + [HBM_ONLY] * (2 * n) + [pl.BlockSpec(memory_space=pltpu.VMEM)]),
        input_output_aliases={i: 2 * n + i for i in range(2 * n)},
        compiler_params=pltpu.CompilerParams(has_side_effects=DATAFLOW_EFFECT),
    )(*[pltpu.with_memory_space_constraint(s, pltpu.HBM) for s in sources],
      *[pltpu.with_memory_space_constraint(lax.empty(shp, s.dtype), pltpu.HBM)
        for shp, s in zip(landing_shapes, sources)], *extra)
    pending = [(outs[2 * i], outs[2 * i + 1], outs[2 * n + i], outs[3 * n + i]) for i in range(n)]
    return pending, outs[-1]


def _exchange_wait(pending, after, sends, arrivals, *, name):
    send_sems, recv_sems, source, landing = pending

    def body(src_ref, land_ref, send_ref, recv_ref, after_ref, src_out, land_out):
        for cp in sends(src_ref, land_ref, send_ref, recv_ref):
            cp.wait_send()
        for cp in arrivals(src_ref, land_ref, send_ref, recv_ref):
            cp.wait_recv()

    return pl.pallas_call(
        body, name=name,
        out_shape=(pltpu.HBM(source.shape, source.dtype), pltpu.HBM(landing.shape, landing.dtype)),
        in_specs=[HBM_ONLY, HBM_ONLY, SEMAPHORES, SEMAPHORES, pl.BlockSpec(memory_space=pl.ANY)],
        out_specs=(HBM_ONLY, HBM_ONLY), input_output_aliases={0: 0, 1: 1},
        compiler_params=pltpu.CompilerParams(has_side_effects=DATAFLOW_EFFECT),
    )(source, landing, send_sems, recv_sems, after)


def _gather_ici(shard, *, name):
    _, rh, cols = shard.shape

    def body(w_ref, o_ref, send_sems, recv_sems):
        x, y, c = _mesh_pos()
        mine = 2 * x + y
        sends = []
        for k, (px, py) in enumerate(_other_chips(x, y)):
            cp = _remote(w_ref.at[c], o_ref.at[mine, c], send_sems, recv_sems, k, (px, py, c))
            cp.start()
            sends.append(cp)
        for k, (px, py) in enumerate(_other_chips(x, y)):
            _remote(w_ref.at[c], o_ref.at[2 * px + py, c], send_sems, recv_sems, k, (px, py, c)).wait_recv()
        for cp in sends:
            cp.wait_send()

    return pl.pallas_call(
        body, name=name, in_specs=[HBM], out_specs=HBM,
        out_shape=jax.ShapeDtypeStruct((N_CHIPS, 2, rh, cols), shard.dtype), scratch_shapes=_dma_sems(3),
    )(shard)


def _gather_d2d(parts, *, name):
    def body(a_ref, o_ref, send_sems, recv_sems):
        x, y, c = _mesh_pos()
        sibling = (x, y, 1 - c)
        sends = []
        for k, (px, py) in enumerate(_other_chips(x, y)):
            cp = _remote(a_ref.at[2 * px + py, c], o_ref.at[2 * px + py, c], send_sems, recv_sems, k, sibling)
            cp.start()
            sends.append(cp)
        for k, (px, py) in enumerate(_other_chips(x, y)):
            _remote(a_ref.at[2 * px + py, c], o_ref.at[2 * px + py, 1 - c], send_sems, recv_sems, k, sibling).wait_recv()
        for cp in sends:
            cp.wait_send()

    return pl.pallas_call(
        body, name=name, in_specs=[HBM], out_specs=HBM,
        out_shape=jax.ShapeDtypeStruct(parts.shape, parts.dtype),
        input_output_aliases={0: 0}, scratch_shapes=_dma_sems(3),
    )(parts)


def _all_gather_chips(shard_flat, name):
    rows, cols = shard_flat.shape
    parts = _gather_ici(shard_flat.reshape(2, rows // 2, cols), name=name + "_ici")
    others = _gather_d2d(parts, name=name + "_d2d").reshape(N_CHIPS, rows, cols)
    chip = 2 * lax.axis_index("x") + lax.axis_index("y")
    return lax.dynamic_update_slice(others, shard_flat[None], (chip, 0, 0))


def _row_tile(rows, mult, cap):
    best = mult
    for t in range(mult, min(rows, cap) + 1, mult):
        if rows % t == 0:
            best = t
    assert rows % best == 0, (rows, mult)
    return best


def _swap_halves_d2d(g, *, name):
    _, _, rh, cols = g.shape

    def body(g_ref, o_ref, send_sems, recv_sems):
        x, y, c = _mesh_pos()
        sibling = (x, y, 1 - c)
        sends = []
        for s in range(N_CHIPS):
            cp = _remote(g_ref.at[s, 1 - c], o_ref.at[s], send_sems, recv_sems, s, sibling)
            cp.start()
            sends.append(cp)
        for s in range(N_CHIPS):
            _remote(g_ref.at[s, c], o_ref.at[s], send_sems, recv_sems, s, sibling).wait_recv()
        for cp in sends:
            cp.wait_send()

    return pl.pallas_call(
        body, name=name, in_specs=[HBM], out_specs=HBM,
        out_shape=jax.ShapeDtypeStruct((N_CHIPS, rh, cols), g.dtype), scratch_shapes=_dma_sems(N_CHIPS),
    )(g)


def _add_own_half(g, arrived, core, *, name):
    _, _, rh, cols = g.shape
    mult = 16 if g.dtype == BF16 else 8
    tr = _row_tile(rh, mult, max(mult, (512 * 1024) // cols))

    def body(core_ref, g_ref, a_ref, o_ref):
        o_ref[...] = (g_ref[0].astype(F32) + a_ref[...].astype(F32)).astype(o_ref.dtype)

    grid_spec = pltpu.PrefetchScalarGridSpec(
        num_scalar_prefetch=1, grid=(N_CHIPS, rh // tr),
        in_specs=[pl.BlockSpec((1, 1, tr, cols), lambda s, i, core_ref: (s, core_ref[0], i, 0)),
                  pl.BlockSpec((1, tr, cols), lambda s, i, core_ref: (s, i, 0))],
        out_specs=pl.BlockSpec((1, tr, cols), lambda s, i, core_ref: (s, i, 0)))
    return pl.pallas_call(
        body, name=name, grid_spec=grid_spec, out_shape=jax.ShapeDtypeStruct((N_CHIPS, rh, cols), g.dtype),
        compiler_params=_params(("parallel", "parallel")),
    )(core, g, arrived)


def _scatter_ici(h, *, name):
    def body(h_ref, o_ref, send_sems, recv_sems):
        x, y, c = _mesh_pos()
        mine = 2 * x + y
        sends = []
        for k, (px, py) in enumerate(_other_chips(x, y)):
            cp = _remote(h_ref.at[2 * px + py], o_ref.at[mine], send_sems, recv_sems, k, (px, py, c))
            cp.start()
            sends.append(cp)
        for k, (px, py) in enumerate(_other_chips(x, y)):
            _remote(h_ref.at[mine], o_ref.at[2 * px + py], send_sems, recv_sems, k, (px, py, c)).wait_recv()
        for cp in sends:
            cp.wait_send()

    others = pl.pallas_call(
        body, name=name, in_specs=[HBM], out_specs=HBM, out_shape=jax.ShapeDtypeStruct(h.shape, h.dtype),
        scratch_shapes=_dma_sems(3),
    )(h)
    chip = 2 * lax.axis_index("x") + lax.axis_index("y")
    own = lax.dynamic_slice_in_dim(h, chip, 1, axis=0)
    return lax.dynamic_update_slice(others, own, (chip, 0, 0))


def _sum_chips(parts, *, name):
    _, rh, cols = parts.shape
    mult = 16 if parts.dtype == BF16 else 8
    tr = _row_tile(rh, mult, max(mult, (512 * 1024) // cols))

    def body(p_ref, o_ref):
        acc = p_ref[0].astype(F32)
        for s in range(1, N_CHIPS):
            acc = acc + p_ref[s].astype(F32)
        o_ref[...] = acc

    return pl.pallas_call(
        body, name=name, grid=(rh // tr,),
        in_specs=[pl.BlockSpec((N_CHIPS, tr, cols), lambda i: (0, i, 0))],
        out_specs=pl.BlockSpec((tr, cols), lambda i: (i, 0)),
        out_shape=jax.ShapeDtypeStruct((rh, cols), F32), compiler_params=_params(("parallel",)),
    )(parts)


def _share_d2d(f, *, name):
    fs = f if isinstance(f, (list, tuple)) else [f]
    others = _swap_with_sibling(fs, name=name)
    first = lax.axis_index("c") == 0
    both = [jnp.stack([jnp.where(first, a, b), jnp.where(first, b, a)]) for a, b in zip(fs, others)]
    return both if isinstance(f, (list, tuple)) else both[0]


def _swap_with_sibling(fs, *, name):
    n = len(fs)

    def body(*refs):
        x, y, c = _mesh_pos()
        sibling = (x, y, 1 - c)
        send_sems, recv_sems = refs[2 * n:]
        copies = [_remote(refs[i], refs[n + i], send_sems, recv_sems, i, sibling) for i in range(n)]
        for cp in copies:
            cp.start()
        for cp in copies:
            cp.wait()

    return pl.pallas_call(
        body, name=name, in_specs=[HBM] * n, out_specs=[HBM] * n,
        out_shape=[jax.ShapeDtypeStruct(a.shape, a.dtype) for a in fs], scratch_shapes=_dma_sems(n),
    )(*fs)


def _reduce_scatter_chips(g, core, name):
    _, rows, cols = g.shape
    g = g.reshape(N_CHIPS, 2, rows // 2, cols)
    arrived = _swap_halves_d2d(g, name=name + "_swap")
    chip_sum = _add_own_half(g, arrived, core, name=name + "_add2")
    parts = _scatter_ici(chip_sum, name=name + "_ici")
    total = _sum_chips(parts, name=name + "_sum4")
    return _share_d2d(total, name=name + "_share").reshape(rows, cols)


BIG = ("w_in", "w_branch", "w_out", "w_up", "w_down")
BIG_COLUMN_SHARDED = ("w_in", "w_up")
CONV = ("conv_a_w", "conv_f_w")
REPLICATED = ("norm1_w", "b_gate", "conv_a_b", "dt_bias", "a_log", "d_skip", "ssd_norm_w", "uv_b", "v_ln_w",
              "v_ln_b", "w_spatial", "b_spatial", "norm2_w", "conv_f_b", "final_norm_w")
WEIGHT_ORDER = ("norm1_w", "w_in", "b_gate", "conv_a_w", "conv_a_b", "dt_bias", "a_log", "d_skip", "ssd_norm_w",
                "uv_b", "v_ln_w", "v_ln_b", "w_spatial", "b_spatial", "w_branch", "w_out", "norm2_w", "w_up",
                "conv_f_w", "conv_f_b", "w_down", "final_norm_w")
SMALL_EXCHANGE_ROWS = 64


_GATE0 = SSD_IN + 2 * SGU_WIDTH
IN_SEGMENTS = {
    "in_z": (0, SSD_D_INNER), "in_xbc": (SSD_D_INNER, SSD_D_INNER + SSD_XBC), "in_dt": (SSD_D_INNER + SSD_XBC, SSD_IN),
    "in_uv": (SSD_IN, _GATE0), "in_gate": (_GATE0, IN_COLS), "in_gate_a": (_GATE0, _GATE0 + D_MODEL),
    "in_gate_b": (_GATE0 + D_MODEL, IN_COLS),
}
IN_GRAD_SEGMENTS = ("in_z", "in_xbc", "in_dt", "in_uv", "in_gate_a", "in_gate_b")


def _take_columns(parts, start, stop):
    out = []
    for a, first in parts:
        lo, hi = max(start, first), min(stop, first + a.shape[1])
        if lo < hi:
            out.append(a[:, lo - first:hi - first])
    return out[0] if len(out) == 1 else jnp.concatenate(out, axis=1)


def _flat_rows(arrays, row_multiple):
    flat = jnp.concatenate([a.reshape(-1) for a in arrays])
    rows = -(-flat.shape[0] // (LANES * row_multiple)) * row_multiple
    return jnp.pad(flat, (0, rows * LANES - flat.shape[0])).reshape(rows, LANES)


def _unflatten(flat, shapes):
    flat = flat.reshape(-1)
    out, off = [], 0
    for shp in shapes:
        n = math.prod(shp)
        out.append(flat[off:off + n].reshape(shp))
        off += n
    return out


def _from_chip_blocks(blocks, name):
    if name in BIG_COLUMN_SHARDED or name in CONV:
        k = blocks.shape[1]
        return jnp.transpose(blocks, (1, 0, 2)).reshape(k, -1)
    return blocks.reshape(-1, blocks.shape[-1])


def _to_chip_blocks(whole, name):
    if name in BIG_COLUMN_SHARDED or name in CONV:
        k, n = whole.shape
        return jnp.transpose(whole.reshape(k, N_CHIPS, n // N_CHIPS), (1, 0, 2))
    return whole.reshape(N_CHIPS, whole.shape[0] // N_CHIPS, whole.shape[1])


def kernel(x, norm1_w, w_in, b_gate, conv_a_w, conv_a_b, dt_bias, a_log, d_skip, ssd_norm_w, uv_b, v_ln_w, v_ln_b, w_spatial, b_spatial, w_branch, w_out, norm2_w, w_up, conv_f_w, conv_f_b, w_down, final_norm_w, loss_target, m_norm1_w, m_w_in, m_b_gate, m_conv_a_w, m_conv_a_b, m_dt_bias, m_a_log, m_d_skip, m_ssd_norm_w, m_uv_b, m_v_ln_w, m_v_ln_b, m_w_spatial, m_b_spatial, m_w_branch, m_w_out, m_norm2_w, m_w_up, m_conv_f_w, m_conv_f_b, m_w_down, m_final_norm_w, v_norm1_w, v_w_in, v_b_gate, v_conv_a_w, v_conv_a_b, v_dt_bias, v_a_log, v_d_skip, v_ssd_norm_w, v_uv_b, v_v_ln_w, v_v_ln_b, v_w_spatial, v_b_spatial, v_w_branch, v_w_out, v_norm2_w, v_w_up, v_conv_f_w, v_conv_f_b, v_w_down, v_final_norm_w):
    weights = dict(norm1_w=norm1_w, w_in=w_in, b_gate=b_gate, conv_a_w=conv_a_w, conv_a_b=conv_a_b, dt_bias=dt_bias,
                   a_log=a_log, d_skip=d_skip, ssd_norm_w=ssd_norm_w, uv_b=uv_b, v_ln_w=v_ln_w, v_ln_b=v_ln_b,
                   w_spatial=w_spatial, b_spatial=b_spatial, w_branch=w_branch, w_out=w_out, norm2_w=norm2_w,
                   w_up=w_up, conv_f_w=conv_f_w, conv_f_b=conv_f_b, w_down=w_down, final_norm_w=final_norm_w)
    mom1 = dict(norm1_w=m_norm1_w, w_in=m_w_in, b_gate=m_b_gate, conv_a_w=m_conv_a_w, conv_a_b=m_conv_a_b,
                dt_bias=m_dt_bias, a_log=m_a_log, d_skip=m_d_skip, ssd_norm_w=m_ssd_norm_w, uv_b=m_uv_b,
                v_ln_w=m_v_ln_w, v_ln_b=m_v_ln_b, w_spatial=m_w_spatial, b_spatial=m_b_spatial, w_branch=m_w_branch,
                w_out=m_w_out, norm2_w=m_norm2_w, w_up=m_w_up, conv_f_w=m_conv_f_w, conv_f_b=m_conv_f_b,
                w_down=m_w_down, final_norm_w=m_final_norm_w)
    mom2 = dict(norm1_w=v_norm1_w, w_in=v_w_in, b_gate=v_b_gate, conv_a_w=v_conv_a_w, conv_a_b=v_conv_a_b,
                dt_bias=v_dt_bias, a_log=v_a_log, d_skip=v_d_skip, ssd_norm_w=v_ssd_norm_w, uv_b=v_uv_b,
                v_ln_w=v_v_ln_w, v_ln_b=v_v_ln_b, w_spatial=v_w_spatial, b_spatial=v_b_spatial, w_branch=v_w_branch,
                w_out=v_w_out, norm2_w=v_norm2_w, w_up=v_w_up, conv_f_w=v_conv_f_w, conv_f_b=v_conv_f_b,
                w_down=v_w_down, final_norm_w=v_final_norm_w)
    chip = 2 * lax.axis_index("x") + lax.axis_index("y")
    core = lax.axis_index("c").astype(jnp.int32).reshape(1)

    whole = {}
    conv_shapes = [weights[n].shape[1:] for n in CONV]
    conv_gathered = _all_gather_chips(_flat_rows([weights[n] for n in CONV], 16), "gather_conv").reshape(N_CHIPS, -1)
    off = 0
    for n, shp in zip(CONV, conv_shapes):
        size = math.prod(shp)
        whole[n] = _from_chip_blocks(conv_gathered[:, off:off + size].reshape((N_CHIPS,) + shp), n)
        off += size
    shard_shapes = {n: weights[n].shape[1:] for n in BIG}
    halves = [weights[n][0].astype(BF16).reshape(2, shard_shapes[n][0] // 2, shard_shapes[n][1]) for n in BIG]
    sends = [_gather_sends if n == "w_in" else _gather_whole_sends for n in BIG]
    gathers, _ = _exchange_start(halves, [(N_CHIPS,) + h.shape for h in halves], sends, after=conv_gathered,
                                 name="gather_start")
    gathers = dict(zip(BIG, gathers))

    def get_weight(name, after):
        rows, cols = shard_shapes[name]
        if name == "w_in":
            own, landed = _exchange_wait(gathers[name], after, _gather_sends, _gather_arrivals,
                                         name="gather_" + name + "_wait")
            landed = _gather_d2d(landed, name="gather_" + name + "_d2d")
        else:
            own, landed = _exchange_wait(gathers[name], after, _gather_whole_sends, _gather_whole_arrivals,
                                         name="gather_" + name + "_wait")
        blocks = lax.dynamic_update_slice(landed.reshape(N_CHIPS, rows, cols), own.reshape(1, rows, cols),
                                          (chip, 0, 0))
        if name == "w_up":
            return {"up": blocks}
        if name == "w_in":
            parts = [(blocks[k], cols * k) for k in range(N_CHIPS)]
            segs = {n: _take_columns(parts, a, b) for n, (a, b) in IN_SEGMENTS.items()}
            segs["in_dt"] = jnp.pad(segs["in_dt"], ((0, 0), (0, LANES - SSD_HEADS)))
            return segs
        full = _from_chip_blocks(blocks, name)
        if name == "w_branch":
            return {"branch_a": full[:SSD_D_INNER], "branch_b": full[SSD_D_INNER:]}
        return {name[2:]: full}

    small = {n: weights[n] for n in REPLICATED}
    small["conv_a_w"] = whole["conv_a_w"]
    small["conv_f_w"] = whole["conv_f_w"]

    reductions = {}

    def emit_grad(name, g):
        if name == "w_in":
            parts = [(g[n], IN_SEGMENTS[n][0]) for n in IN_GRAD_SEGMENTS]
            cols = shard_shapes[name][1]
            g_blocks = jnp.stack([_take_columns(parts, cols * k, cols * (k + 1)) for k in range(N_CHIPS)])
        else:
            g_blocks = g if name == "w_up" else _to_chip_blocks(g, name)
        own = lax.dynamic_slice_in_dim(g_blocks, chip, 1, axis=0)
        (pending,), started = _exchange_start([g_blocks], [g_blocks.shape], _scatter_sends,
                                              name="reduce_" + name + "_start")
        reductions[name] = (pending, own)
        return started

    loss, dx, grads_small = _local_step(x[0], loss_target[0], get_weight, small, emit_grad)

    order = ("w_down", "w_up", "w_out", "w_branch", "w_in")
    core_sums = []
    for n in order:
        pending, own = reductions[n]
        _, landed = _exchange_wait(pending, dx, _scatter_sends, _scatter_arrivals, name="reduce_" + n + "_wait")
        parts = lax.dynamic_update_slice(landed, own, (chip, 0, 0))
        core_sums.append(_sum_chips(parts, name="reduce_" + n + "_sum4"))
    sibling_sums = dict(zip(order, _swap_with_sibling(core_sums, name="reduce_swap")))
    core_sums = dict(zip(order, core_sums))
    grads = {}

    small_names = REPLICATED + CONV
    small_shapes = [grads_small[n].shape for n in small_names]
    g_small = _flat_rows([grads_small[n] for n in small_names], N_CHIPS * 2 * SMALL_EXCHANGE_ROWS)
    red_small = _reduce_scatter_chips(g_small.reshape(N_CHIPS, -1, LANES), core, "reduce_small")
    all_small = _all_gather_chips(red_small, "gather_small")
    for n, g in zip(small_names, _unflatten(all_small, small_shapes)):
        if n in CONV:
            width = g.shape[1] // N_CHIPS
            g = lax.dynamic_slice_in_dim(g, chip * width, width, axis=1)
        grads[n] = g.reshape(weights[n].shape[1:]) if n != "final_norm_w" else g

    delta, new_m, new_v = {}, {}, {}
    for n in BIG:
        shp = weights[n].shape
        if n == "w_in":
            g_t = (core_sums[n] + sibling_sums[n]).T
            results = [g_t] + list(_adamw(weights[n][0].T, g_t, mom1[n][0].T, mom2[n][0].T, name="adamw_" + n,
                                          tr=_row_tile(g_t.shape[0], 8, 136)))
            results = [a.T for a in results]
        else:
            results = _adamw_two_sums(weights[n][0], core_sums[n], sibling_sums[n], mom1[n][0], mom2[n][0],
                                      name="adamw_" + n, tr=_row_tile(shp[1], 8, 136))
        grads[n], delta[n], new_m[n], new_v[n] = [a.reshape(shp) for a in results]
    small_all = [n for n in WEIGHT_ORDER if n not in BIG]

    def as_2d(a):
        return a.reshape(-1, a.shape[-1])

    results = _adamw_many(*[[as_2d(src[n]) for n in small_all] for src in (weights, grads, mom1, mom2)],
                          name="adamw_small")
    for n, dv, mv, vv in zip(small_all, *results):
        shp = weights[n].shape
        delta[n], new_m[n], new_v[n] = dv.reshape(shp), mv.reshape(shp), vv.reshape(shp)

    total_loss = lax.psum(loss[0, 0], ("x", "y", "c"))
    grad_out = [grads[n].reshape(weights[n].shape) for n in WEIGHT_ORDER]
    return (total_loss, dx[None], *grad_out, *[delta[n] for n in WEIGHT_ORDER], *[new_m[n] for n in WEIGHT_ORDER],
            *[new_v[n] for n in WEIGHT_ORDER])
```

```python
import functools
import math

import jax
import jax.numpy as jnp
from jax import lax
from jax.experimental import pallas as pl
from jax.experimental.pallas import tpu as pltpu

F32 = jnp.float32
BF16 = jnp.bfloat16
HI = lax.Precision.HIGHEST

D_MODEL = 1024
SSD_D_INNER = 2048
SSD_HEADS = 32
SSD_HEAD_DIM = 64
SSD_GROUPS = 4
SSD_HEADS_PER_GROUP = 8
SSD_STATE = 128
SSD_BC = 512
SSD_XBC = 3072
SSD_IN = 5152
SGU_WIDTH = 1024
SGU_GROUPS = 8
CHUNK = 128
IN_COLS = 9248
D_FF = 2816
NORM_EPS = 1e-6
LN_EPS = 1e-5
GROUP_COLS = SSD_HEADS_PER_GROUP * SSD_HEAD_DIM
LANES = 128

ADAM_LR = 0.001
ADAM_B1 = 0.9
ADAM_B2 = 0.999
ADAM_EPS = 1e-08
ADAM_WD = 0.01
ADAM_STEP = 10

N_CHIPS = 4
VMEM_LIMIT = 56 * 1024 * 1024

NT = (((1,), (1,)), ((), ()))
TN = (((0,), (0,)), ((), ()))
NN = (((1,), (0,)), ((), ()))


def _params(dims):
    return pltpu.CompilerParams(dimension_semantics=dims, vmem_limit_bytes=VMEM_LIMIT)


def _dot(a, b, dn=NN, precision=None):
    return lax.dot_general(a, b, dn, precision=precision, preferred_element_type=F32)


def _split3(x):
    hi = x.astype(BF16)
    rest = x - hi.astype(F32)
    mid = rest.astype(BF16)
    return hi, mid, (rest - mid.astype(F32)).astype(BF16)


def _dot_terms(terms, exact, dn=NN):
    out = None
    for t in terms:
        p = _dot(t, exact, dn)
        out = p if out is None else out + p
    return out


def _dot_exact_lhs(exact, terms):
    out = None
    for t in terms:
        p = _dot(exact, t)
        out = p if out is None else out + p
    return out


def _sigmoid(x):
    return 1.0 / (1.0 + jnp.exp(-x))


def _softplus(x):
    return jnp.maximum(x, 0.0) + jnp.log(1.0 + jnp.exp(-jnp.abs(x)))


def _matmul(pairs, *, trans_b=False, add=None, after=None, out_dtype=F32, tm=512, tn=512, name):
    def mat_shape(b):
        if isinstance(b, tuple) and b[1] == "cols":
            return (b[0].shape[1], b[0].shape[0] * b[0].shape[2])
        return b[0].shape[1:] if isinstance(b, tuple) else b.shape

    if isinstance(pairs[0][1], tuple) and pairs[0][1][1] == "cols":
        assert not trans_b and tn % LANES == 0 and pairs[0][1][0].shape[2] % tn == 0, name

    m = (pairs[0][0][0] if isinstance(pairs[0][0], tuple) else pairs[0][0]).shape[0]
    n = mat_shape(pairs[0][1])[0] if trans_b else mat_shape(pairs[0][1])[1]
    tm, tn = min(tm, m), min(tn, n)
    assert m % tm == 0 and n % tn == 0, (name, m, n, tm, tn)
    npairs = len(pairs)
    dn = NT if trans_b else NN

    def body(*refs):
        o_ref = refs[-1]
        acc = None
        for i in range(npairs):
            p = _dot(refs[2 * i][...].astype(BF16), refs[2 * i + 1][...].astype(BF16), dn)
            acc = p if acc is None else acc + p
        if add is not None:
            acc = acc + refs[2 * npairs][...]
        o_ref[...] = acc.astype(out_dtype)

    in_specs, args = [], []
    for a, b in pairs:
        bshape = mat_shape(b)
        k = bshape[1] if trans_b else bshape[0]
        assert bshape == ((n, k) if trans_b else (k, n)), (name, bshape)
        a, qa = a if isinstance(a, tuple) else (a, 0)
        assert a.shape[0] == m and a.shape[1] % k == 0, (name, a.shape, k)
        in_specs.append(pl.BlockSpec((tm, k), lambda i, j, qa=qa: (i, qa)))
        if isinstance(b, tuple) and b[1] == "cols":
            b = b[0]
            per = b.shape[2] // tn
            in_specs.append(pl.BlockSpec((None, k, tn), lambda i, j, per=per: (j // per, 0, j % per)))
        elif isinstance(b, tuple):
            b, qb = b
            if trans_b:
                in_specs.append(pl.BlockSpec((None, tn, k), lambda i, j, qb=qb: (qb, j, 0)))
            else:
                in_specs.append(pl.BlockSpec((None, k, tn), lambda i, j, qb=qb: (qb, 0, j)))
        elif trans_b:
            in_specs.append(pl.BlockSpec((tn, k), lambda i, j: (j, 0)))
        else:
            in_specs.append(pl.BlockSpec((k, tn), lambda i, j: (0, j)))
        args += [a, b]
    if add is not None:
        in_specs.append(pl.BlockSpec((tm, tn), lambda i, j: (i, j)))
        args.append(add)
    if after is not None:
        in_specs.append(pl.BlockSpec(memory_space=pl.ANY))
        args.append(after)
    return pl.pallas_call(
        body, name=name, grid=(m // tm, n // tn), in_specs=in_specs,
        out_specs=pl.BlockSpec((tm, tn), lambda i, j: (i, j)),
        out_shape=jax.ShapeDtypeStruct((m, n), out_dtype),
        compiler_params=_params(("parallel", "parallel")),
    )(*args)


def _matmul_tn(a, b, *, tk, tn, tm=1024, out_dtype=BF16, stack_out=False, after=None, name):
    m, k = a.shape
    n = b.shape[1]
    tm, tk, tn = min(tm, m), min(tk, k), min(tn, n)
    assert m % tm == 0 and k % tk == 0 and n % tn == 0, (name, m, k, n)
    nm = m // tm
    if stack_out:
        out_spec = pl.BlockSpec((None, tk, tn), lambda i, j, l: (j, i, 0))
        out_shape = jax.ShapeDtypeStruct((n // tn, k, tn), out_dtype)
    else:
        out_spec = pl.BlockSpec((tk, tn), lambda i, j, l: (i, j))
        out_shape = jax.ShapeDtypeStruct((k, n), out_dtype)

    def body(a_ref, b_ref, *rest):
        o_ref, acc = rest[-2:]
        mi = pl.program_id(2)

        @pl.when(mi == 0)
        def _():
            acc[...] = jnp.zeros_like(acc)

        acc[...] += _dot(a_ref[...].astype(BF16), b_ref[...].astype(BF16), TN)

        @pl.when(mi == nm - 1)
        def _():
            o_ref[...] = acc[...].astype(out_dtype)

    in_specs = [pl.BlockSpec((tm, tk), lambda i, j, l: (l, i)), pl.BlockSpec((tm, tn), lambda i, j, l: (l, j))]
    args = [a, b]
    if after is not None:
        in_specs.append(pl.BlockSpec(memory_space=pl.ANY))
        args.append(after)
    return pl.pallas_call(
        body, name=name, grid=(k // tk, n // tn, nm), in_specs=in_specs,
        out_specs=out_spec, out_shape=out_shape,
        scratch_shapes=[pltpu.VMEM((tk, tn), F32)],
        compiler_params=_params(("parallel", "parallel", "arbitrary")),
    )(*args)


def _rms_fwd(x, w, *, name, tm=512):
    s, d = x.shape
    tm = min(tm, s)

    def body(x_ref, w_ref, o_ref):
        xv = x_ref[...]
        r = lax.rsqrt(jnp.mean(xv * xv, axis=-1, keepdims=True) + NORM_EPS)
        o_ref[...] = (xv * r * w_ref[...]).astype(BF16)

    return pl.pallas_call(
        body, name=name, grid=(s // tm,),
        in_specs=[pl.BlockSpec((tm, d), lambda i: (i, 0)), pl.BlockSpec((1, d), lambda i: (0, 0))],
        out_specs=pl.BlockSpec((tm, d), lambda i: (i, 0)),
        out_shape=jax.ShapeDtypeStruct((s, d), BF16),
        compiler_params=_params(("parallel",)),
    )(x, w)


def _rms_bwd(x, w, dn, dres, *, name, tm=512):
    s, d = x.shape
    tm = min(tm, s)

    def body(x_ref, w_ref, dn_ref, dres_ref, dx_ref, dxb_ref, dw_ref):
        @pl.when(pl.program_id(0) == 0)
        def _():
            dw_ref[...] = jnp.zeros_like(dw_ref)

        xv = x_ref[...]
        r = lax.rsqrt(jnp.mean(xv * xv, axis=-1, keepdims=True) + NORM_EPS)
        xhat = xv * r
        dnv = dn_ref[...]
        dxhat = dnv * w_ref[...]
        dx = dres_ref[...] + r * (dxhat - xhat * jnp.mean(dxhat * xhat, axis=-1, keepdims=True))
        dx_ref[...] = dx
        dxb_ref[...] = dx.astype(BF16)
        dw_ref[...] += jnp.sum(dnv * xhat, axis=0, keepdims=True)

    tile = pl.BlockSpec((tm, d), lambda i: (i, 0))
    row = pl.BlockSpec((1, d), lambda i: (0, 0))
    return pl.pallas_call(
        body, name=name, grid=(s // tm,),
        in_specs=[tile, row, tile, tile], out_specs=[tile, tile, row],
        out_shape=[jax.ShapeDtypeStruct((s, d), F32), jax.ShapeDtypeStruct((s, d), BF16),
                   jax.ShapeDtypeStruct((1, d), F32)],
        compiler_params=_params(("arbitrary",)),
    )(x, w, dn, dres)


def _final_fwd_bwd(h2, wf, target, *, name, tm=512):
    s, d = h2.shape
    tm = min(tm, s)

    def body(h_ref, w_ref, t_ref, loss_ref, dh_ref, dhb_ref, dw_ref):
        @pl.when(pl.program_id(0) == 0)
        def _():
            dw_ref[...] = jnp.zeros_like(dw_ref)
            loss_ref[...] = jnp.zeros_like(loss_ref)

        hv = h_ref[...]
        r = lax.rsqrt(jnp.mean(hv * hv, axis=-1, keepdims=True) + NORM_EPS)
        xhat = hv * r
        err = xhat * w_ref[...] - t_ref[...]
        per_tok = jnp.mean(err * err, axis=-1, keepdims=True)
        loss_ref[...] += 0.5 * jnp.sum(per_tok, axis=0, keepdims=True)
        dy = err * (1.0 / d)
        dxhat = dy * w_ref[...]
        dh = r * (dxhat - xhat * jnp.mean(dxhat * xhat, axis=-1, keepdims=True))
        dh_ref[...] = dh
        dhb_ref[...] = dh.astype(BF16)
        dw_ref[...] += jnp.sum(dy * xhat, axis=0, keepdims=True)

    tile = pl.BlockSpec((tm, d), lambda i: (i, 0))
    row = pl.BlockSpec((1, d), lambda i: (0, 0))
    return pl.pallas_call(
        body, name=name, grid=(s // tm,),
        in_specs=[tile, row, tile],
        out_specs=[pl.BlockSpec((1, 1), lambda i: (0, 0)), tile, tile, row],
        out_shape=[jax.ShapeDtypeStruct((1, 1), F32), jax.ShapeDtypeStruct((s, d), F32),
                   jax.ShapeDtypeStruct((s, d), BF16), jax.ShapeDtypeStruct((1, d), F32)],
        compiler_params=_params(("arbitrary",)),
    )(h2, wf, target)


CONV_ROWS = 512
HALO = 8


def _rows_with_halo(ref, r0, rows, s, before, after):
    parts = []
    if before:
        prev = ref[pl.ds(pl.multiple_of(jnp.maximum(r0 - HALO, 0), HALO), HALO), :]
        parts.append(jnp.where(r0 > 0, prev, 0.0))
    parts.append(ref[pl.ds(r0, rows), :])
    if after:
        nxt = ref[pl.ds(pl.multiple_of(jnp.minimum(r0 + rows, s - HALO), HALO), HALO), :]
        parts.append(jnp.where(r0 + rows < s, nxt, 0.0))
    return jnp.concatenate(parts, axis=0) if len(parts) > 1 else parts[0]


def _fill_padded(x_ref, xp, s):
    zeros = jnp.zeros((HALO, xp.shape[1]), F32)
    xp[pl.ds(0, HALO), :] = zeros
    xp[pl.ds(HALO, s), :] = x_ref[...].astype(F32)
    xp[pl.ds(HALO + s, HALO), :] = zeros


def _shifted(xp, r0, k, rows):
    window = xp[pl.ds(r0, rows + HALO), :]
    if k == 0:
        return window[HALO:]
    return pltpu.roll(window, k, 0)[HALO:]


def _conv_taps(xp, r0, w_ref, kk, rows):
    acc = None
    for i in range(kk):
        term = w_ref[i:i + 1, :] * _shifted(xp, r0, kk - 1 - i, rows)
        acc = term if acc is None else acc + term
    return acc


def _row_loop(s, step):
    def body(r, carry):
        return step(pl.multiple_of(r * CONV_ROWS, CONV_ROWS), carry)
    return body


def _conv_bwd_rows(xp, r0, dpe, dp_sc, w_ref, kk):
    del dp_sc
    dp = dpe[:CONV_ROWS]
    dx = None
    dws = []
    for i in range(kk):
        k = kk - 1 - i
        dws.append(jnp.sum(dp * _shifted(xp, r0, k, CONV_ROWS), axis=0, keepdims=True))
        later = dp if k == 0 else pltpu.roll(dpe, dpe.shape[0] - k, 0)[:CONV_ROWS]
        term = w_ref[i:i + 1, :] * later
        dx = term if dx is None else dx + term
    return dx, dws, jnp.sum(dp, axis=0, keepdims=True)


def _conv_scratch(s, tc, n_padded, n_dp):
    return ([pltpu.VMEM((s + 2 * HALO, tc), F32)] * n_padded
            + [pltpu.VMEM((CONV_ROWS + HALO, tc), F32)] * n_dp)


def _conv_a_fwd(xraw, w, b, *, name, tc=128):
    s, c = xraw.shape
    kk = 4

    def body(x_ref, w_ref, b_ref, o_ref, xp):
        _fill_padded(x_ref, xp, s)

        def step(r0, carry):
            pre = _conv_taps(xp, r0, w_ref, kk, CONV_ROWS) + b_ref[...]
            o_ref[pl.ds(r0, CONV_ROWS), :] = pre * _sigmoid(pre)
            return carry

        lax.fori_loop(0, s // CONV_ROWS, _row_loop(s, step), 0)

    col = pl.BlockSpec((s, tc), lambda j: (0, j))
    return pl.pallas_call(
        body, name=name, grid=(c // tc,),
        in_specs=[col, pl.BlockSpec((8, tc), lambda j: (0, j)), pl.BlockSpec((1, tc), lambda j: (0, j))],
        out_specs=col, out_shape=jax.ShapeDtypeStruct((s, c), F32),
        scratch_shapes=_conv_scratch(s, tc, 1, 0),
        compiler_params=_params(("parallel",)),
    )(xraw, w, b)


def _conv_a_bwd(xraw, w, b, dy, *, name, tc=128):
    s, c = xraw.shape
    kk = 4

    def body(x_ref, w_ref, b_ref, dy_ref, dx_ref, dw_ref, db_ref, xp, dp_sc):
        _fill_padded(x_ref, xp, s)

        def step(r0, carry):
            pre = _conv_taps(xp, r0, w_ref, kk, CONV_ROWS + HALO) + b_ref[...]
            sg = _sigmoid(pre)
            dpe = _rows_with_halo(dy_ref, r0, CONV_ROWS, s, False, True) * (sg * (1.0 + pre * (1.0 - sg)))
            dx, dws, db = _conv_bwd_rows(xp, r0, dpe, dp_sc, w_ref, kk)
            dx_ref[pl.ds(r0, CONV_ROWS), :] = dx.astype(BF16)
            return tuple(acc + new for acc, new in zip(carry, dws + [db]))

        zero = jnp.zeros((1, tc), F32)
        sums = lax.fori_loop(0, s // CONV_ROWS, _row_loop(s, step), (zero,) * (kk + 1))
        db_ref[...] = sums[kk]
        dw_ref[...] = jnp.concatenate(list(sums[:kk]) + [jnp.zeros((8 - kk, tc), F32)], axis=0)

    col = pl.BlockSpec((s, tc), lambda j: (0, j))
    w8 = pl.BlockSpec((8, tc), lambda j: (0, j))
    row = pl.BlockSpec((1, tc), lambda j: (0, j))
    return pl.pallas_call(
        body, name=name, grid=(c // tc,),
        in_specs=[col, w8, row, col], out_specs=[col, w8, row],
        out_shape=[jax.ShapeDtypeStruct((s, c), BF16), jax.ShapeDtypeStruct((8, c), F32),
                   jax.ShapeDtypeStruct((1, c), F32)],
        scratch_shapes=_conv_scratch(s, tc, 1, 1),
        compiler_params=_params(("parallel",)),
    )(xraw, w, b, dy)


def _conv_f_fwd(up_raw, w, b, *, name, tc=128):
    s, c2 = up_raw.shape
    c = c2 // 2
    nb = c // tc
    kk = 3

    def body(xa_ref, xv_ref, wa_ref, wv_ref, ba_ref, bv_ref, o_ref, xap, xvp):
        _fill_padded(xa_ref, xap, s)
        _fill_padded(xv_ref, xvp, s)

        def step(r0, carry):
            a = _conv_taps(xap, r0, wa_ref, kk, CONV_ROWS) + ba_ref[...]
            v = _conv_taps(xvp, r0, wv_ref, kk, CONV_ROWS) + bv_ref[...]
            o_ref[pl.ds(r0, CONV_ROWS), :] = (a * _sigmoid(a) * v).astype(BF16)
            return carry

        lax.fori_loop(0, s // CONV_ROWS, _row_loop(s, step), 0)

    col_a = pl.BlockSpec((s, tc), lambda j: (0, j))
    col_v = pl.BlockSpec((s, tc), lambda j: (0, j + nb))
    return pl.pallas_call(
        body, name=name, grid=(nb,),
        in_specs=[col_a, col_v, pl.BlockSpec((8, tc), lambda j: (0, j)), pl.BlockSpec((8, tc), lambda j: (0, j + nb)),
                  pl.BlockSpec((1, tc), lambda j: (0, j)), pl.BlockSpec((1, tc), lambda j: (0, j + nb))],
        out_specs=col_a, out_shape=jax.ShapeDtypeStruct((s, c), BF16),
        scratch_shapes=_conv_scratch(s, tc, 2, 0),
        compiler_params=_params(("parallel",)),
    )(up_raw, up_raw, w, w, b, b)


def _conv_f_bwd(up_raw, w, b, dact, *, name, tc=128):
    s, c2 = up_raw.shape
    c = c2 // 2
    nb = c // tc
    kk = 3

    def body(xa_ref, xv_ref, wa_ref, wv_ref, ba_ref, bv_ref, d_ref,
             dxa_ref, dxv_ref, dwa_ref, dwv_ref, dba_ref, dbv_ref, xap, xvp, dpa_sc, dpv_sc):
        _fill_padded(xa_ref, xap, s)
        _fill_padded(xv_ref, xvp, s)

        def step(r0, carry):
            a = _conv_taps(xap, r0, wa_ref, kk, CONV_ROWS + HALO) + ba_ref[...]
            v = _conv_taps(xvp, r0, wv_ref, kk, CONV_ROWS + HALO) + bv_ref[...]
            sg = _sigmoid(a)
            d = _rows_with_halo(d_ref, r0, CONV_ROWS, s, False, True)
            dxa, dwas, dba = _conv_bwd_rows(xap, r0, d * v * (sg * (1.0 + a * (1.0 - sg))), dpa_sc, wa_ref, kk)
            dxv, dwvs, dbv = _conv_bwd_rows(xvp, r0, d * (a * sg), dpv_sc, wv_ref, kk)
            dxa_ref[pl.ds(r0, CONV_ROWS), :] = dxa.astype(BF16)
            dxv_ref[pl.ds(r0, CONV_ROWS), :] = dxv.astype(BF16)
            return tuple(acc + new for acc, new in zip(carry, dwas + [dba] + dwvs + [dbv]))

        zero = jnp.zeros((1, tc), F32)
        sums = lax.fori_loop(0, s // CONV_ROWS, _row_loop(s, step), (zero,) * (2 * kk + 2))
        pad = [jnp.zeros((8 - kk, tc), F32)]
        dwa_ref[...] = jnp.concatenate(list(sums[:kk]) + pad, axis=0)
        dba_ref[...] = sums[kk]
        dwv_ref[...] = jnp.concatenate(list(sums[kk + 1:2 * kk + 1]) + pad, axis=0)
        dbv_ref[...] = sums[2 * kk + 1]

    col_a = pl.BlockSpec((s, tc), lambda j: (0, j))
    col_v = pl.BlockSpec((s, tc), lambda j: (0, j + nb))
    w_a = pl.BlockSpec((8, tc), lambda j: (0, j))
    w_v = pl.BlockSpec((8, tc), lambda j: (0, j + nb))
    r_a = pl.BlockSpec((1, tc), lambda j: (0, j))
    r_v = pl.BlockSpec((1, tc), lambda j: (0, j + nb))
    outs = pl.pallas_call(
        body, name=name, grid=(nb,),
        in_specs=[col_a, col_v, w_a, w_v, r_a, r_v, col_a],
        out_specs=[col_a, col_a, w_a, w_a, r_a, r_a],
        out_shape=[jax.ShapeDtypeStruct((s, c), BF16), jax.ShapeDtypeStruct((s, c), BF16),
                   jax.ShapeDtypeStruct((8, c), F32), jax.ShapeDtypeStruct((8, c), F32),
                   jax.ShapeDtypeStruct((1, c), F32), jax.ShapeDtypeStruct((1, c), F32)],
        scratch_shapes=_conv_scratch(s, tc, 2, 2),
        compiler_params=_params(("parallel",)),
    )(up_raw, up_raw, w, w, b, b, dact)
    return outs


def _tri_masks():
    row = lax.broadcasted_iota(jnp.int32, (CHUNK, CHUNK), 0)
    col = lax.broadcasted_iota(jnp.int32, (CHUNK, CHUNK), 1)
    return row >= col, row <= col


def _ssd_fwd(xbc, dt_raw, z, dt_bias, a_log, a_log_x, d_skip_x, norm_w, expand, *, name):
    s = xbc.shape[0]
    nc = s // CHUNK

    def body(xbc_ref, dtr_ref, z_ref, dtb_ref, alog_ref, alogx_ref, dskx_ref, nw_ref, e_ref,
             y_ref, ya_ref, st_ref, state):
        @pl.when(pl.program_id(0) == 0)
        def _():
            state[...] = jnp.zeros_like(state)

        st_ref[0] = state[...]
        lower, _ = _tri_masks()
        dt = _softplus(dtr_ref[...] + dtb_ref[...])
        adt = dt * (-jnp.exp(alog_ref[...]))
        acum = _dot_exact_lhs(lower.astype(BF16), _split3(adt))
        acum_t = acum.T
        dt_terms, acum_terms = _split3(dt), _split3(acum)
        for g in range(SSD_GROUPS):
            sl = slice(GROUP_COLS * g, GROUP_COLS * (g + 1))
            dt_x = _dot_terms(dt_terms, e_ref[:, sl])
            acum_x = _dot_terms(acum_terms, e_ref[:, sl])
            tot_x = jnp.sum(dt_x * (-jnp.exp(alogx_ref[:, sl])), axis=0, keepdims=True)
            xs = xbc_ref[:, sl]
            xdt = xs * dt_x
            xdt_b = xdt.astype(BF16)
            bg = xbc_ref[:, SSD_D_INNER + SSD_STATE * g:SSD_D_INNER + SSD_STATE * (g + 1)].astype(BF16)
            cg = xbc_ref[:, SSD_D_INNER + SSD_BC + SSD_STATE * g:SSD_D_INNER + SSD_BC + SSD_STATE * (g + 1)].astype(BF16)
            cb = _dot(cg, bg, NT)
            st_g = state[:, sl]
            y_off = _dot(cg, st_g.astype(BF16)) * jnp.exp(acum_x)
            parts = []
            for r in range(SSD_HEADS_PER_GROUP):
                h = SSD_HEADS_PER_GROUP * g + r
                dec = jnp.exp(jnp.where(lower, acum[:, h:h + 1] - acum_t[h:h + 1, :], -jnp.inf))
                parts.append(_dot((cb * dec).astype(BF16), xdt_b[:, SSD_HEAD_DIM * r:SSD_HEAD_DIM * (r + 1)]))
            y_ref[:, sl] = jnp.concatenate(parts, axis=1) + y_off + dskx_ref[:, sl] * xs
            wgt = (xdt * jnp.exp(tot_x - acum_x)).astype(BF16)
            state[:, sl] = st_g * jnp.exp(tot_x) + _dot(bg, wgt, TN)
        zv = z_ref[...].astype(F32)
        q = y_ref[...] * (zv * _sigmoid(zv))
        r = lax.rsqrt(jnp.mean(q * q, axis=-1, keepdims=True) + NORM_EPS)
        ya_ref[...] = (q * r * nw_ref[...]).astype(BF16)

    def chunk(w):
        return pl.BlockSpec((CHUNK, w), lambda c: (c, 0))

    def const(shape):
        return pl.BlockSpec(shape, lambda c: (0,) * len(shape))

    return pl.pallas_call(
        body, name=name, grid=(nc,),
        in_specs=[chunk(SSD_XBC), chunk(LANES), chunk(SSD_D_INNER), const((1, LANES)), const((1, LANES)),
                  const((1, SSD_D_INNER)), const((1, SSD_D_INNER)), const((1, SSD_D_INNER)),
                  const((LANES, SSD_D_INNER))],
        out_specs=[chunk(SSD_D_INNER), chunk(SSD_D_INNER),
                   pl.BlockSpec((1, SSD_STATE, SSD_D_INNER), lambda c: (c, 0, 0))],
        out_shape=[jax.ShapeDtypeStruct((s, SSD_D_INNER), F32), jax.ShapeDtypeStruct((s, SSD_D_INNER), BF16),
                   jax.ShapeDtypeStruct((nc, SSD_STATE, SSD_D_INNER), F32)],
        scratch_shapes=[pltpu.VMEM((SSD_STATE, SSD_D_INNER), F32)],
        compiler_params=_params(("arbitrary",)),
    )(xbc, dt_raw, z, dt_bias, a_log, a_log_x, d_skip_x, norm_w, expand)


def _ssd_bwd(dya, y, z, xbc, dt_raw, states, dt_bias, a_log, a_log_x, d_skip_x, norm_w, expand, expand_t, *, name):
    s = xbc.shape[0]
    nc = s // CHUNK

    def body(dya_ref, y_ref, z_ref, xbc_ref, dtr_ref, stp_ref, dtb_ref, alog_ref, alogx_ref, dskx_ref, nw_ref,
             e_ref, et_ref, dz_ref, dxbc_ref, ddt_ref, dnw_ref, ddsk_ref, dalog_ref, ddtb_ref,
             dstate, dy_sc, dskcol):
        i = pl.program_id(0)

        @pl.when(i == 0)
        def _():
            dstate[...] = jnp.zeros_like(dstate)
            dskcol[...] = jnp.zeros_like(dskcol)
            dnw_ref[...] = jnp.zeros_like(dnw_ref)
            dalog_ref[...] = jnp.zeros_like(dalog_ref)
            ddtb_ref[...] = jnp.zeros_like(ddtb_ref)
            ddsk_ref[...] = jnp.zeros_like(ddsk_ref)

        lower, upper = _tri_masks()
        rows = lax.broadcasted_iota(jnp.int32, (CHUNK, LANES), 0)
        pre = dtr_ref[...] + dtb_ref[...]
        dt = _softplus(pre)
        a = -jnp.exp(alog_ref[...])
        acum = _dot_exact_lhs(lower.astype(BF16), _split3(dt * a))
        acum_t = acum.T
        dt_terms, acum_terms = _split3(dt), _split3(acum)

        yv = y_ref[...]
        zv = z_ref[...].astype(F32)
        sz = _sigmoid(zv)
        silu_z = zv * sz
        q = yv * silu_z
        r = lax.rsqrt(jnp.mean(q * q, axis=-1, keepdims=True) + NORM_EPS)
        qhat = q * r
        dyav = dya_ref[...]
        dqhat = dyav * nw_ref[...]
        dnw_ref[...] += jnp.sum(dyav * qhat, axis=0, keepdims=True)
        dq = r * (dqhat - qhat * jnp.mean(dqhat * qhat, axis=-1, keepdims=True))
        dy_sc[...] = dq * silu_z
        dz_ref[...] = (dq * yv * (sz * (1.0 + zv * (1.0 - sz)))).astype(BF16)

        da_cum = jnp.zeros((CHUNK, LANES), F32)
        ddt = jnp.zeros((CHUNK, LANES), F32)
        for g in range(SSD_GROUPS):
            sl = slice(GROUP_COLS * g, GROUP_COLS * (g + 1))
            et_g = et_ref[sl, :]
            dt_x = _dot_terms(dt_terms, e_ref[:, sl])
            acum_x = _dot_terms(acum_terms, e_ref[:, sl])
            tot_x = jnp.sum(dt_x * (-jnp.exp(alogx_ref[:, sl])), axis=0, keepdims=True)
            e_tot = jnp.exp(tot_x)
            dec_s = jnp.exp(tot_x - acum_x)
            xs = xbc_ref[:, sl]
            xdt = xs * dt_x
            xdt_b = xdt.astype(BF16)
            dy = dy_sc[:, sl]
            dy_b = dy.astype(BF16)
            dskx = dskx_ref[:, sl]
            y_ssd = y_ref[:, sl] - dskx * xs
            dskcol[:, sl] += jnp.sum(dy * xs, axis=0, keepdims=True)
            bg = xbc_ref[:, SSD_D_INNER + SSD_STATE * g:SSD_D_INNER + SSD_STATE * (g + 1)].astype(BF16)
            cg = xbc_ref[:, SSD_D_INNER + SSD_BC + SSD_STATE * g:SSD_D_INNER + SSD_BC + SSD_STATE * (g + 1)].astype(BF16)
            cb_t = _dot(bg, cg, NT)
            sp = stp_ref[0, :, sl]
            ds_g = dstate[:, sl]
            ds_b = ds_g.astype(BF16)
            dye_b = (dy * jnp.exp(acum_x)).astype(BF16)
            dc = _dot(dye_b, sp.astype(BF16), NT)
            dxdt_state = dec_s * _dot(bg, ds_b)
            db = _dot((xdt * dec_s).astype(BF16), ds_b, NT)
            dcb_t = jnp.zeros((CHUNK, CHUNK), F32)
            parts = []
            for rr in range(SSD_HEADS_PER_GROUP):
                h = SSD_HEADS_PER_GROUP * g + rr
                hs = slice(SSD_HEAD_DIM * rr, SSD_HEAD_DIM * (rr + 1))
                dec_t = jnp.exp(jnp.where(upper, acum_t[h:h + 1, :] - acum[:, h:h + 1], -jnp.inf))
                parts.append(_dot((cb_t * dec_t).astype(BF16), dy_b[:, hs]))
                dcb_t = dcb_t + _dot(xdt_b[:, hs], dy_b[:, hs], NT) * dec_t
            dxdt = jnp.concatenate(parts, axis=1) + dxdt_state
            dcb_tb = dcb_t.astype(BF16)
            dc = dc + _dot(dcb_tb, bg, TN)
            db = db + _dot(dcb_tb, cg)
            tot_col = jnp.sum(ds_g * sp, axis=0, keepdims=True) * e_tot + jnp.sum(dxdt_state * xdt, axis=0, keepdims=True)
            d_tot = _dot_terms(_split3(jnp.broadcast_to(tot_col, (8, GROUP_COLS))), et_g)
            d_tot = jnp.max(d_tot, axis=0, keepdims=True)
            pair_sums = dy_b.astype(F32) * y_ssd - xdt_b.astype(F32) * dxdt
            da_cum = da_cum + _dot_terms(_split3(pair_sums), et_g) + jnp.where(rows == CHUNK - 1, d_tot, 0.0)
            ddt = ddt + _dot_terms(_split3(dxdt * xs), et_g)
            dxbc_ref[:, sl] = dy * dskx + dxdt * dt_x
            dxbc_ref[:, SSD_D_INNER + SSD_STATE * g:SSD_D_INNER + SSD_STATE * (g + 1)] = db
            dxbc_ref[:, SSD_D_INNER + SSD_BC + SSD_STATE * g:SSD_D_INNER + SSD_BC + SSD_STATE * (g + 1)] = dc
            dstate[:, sl] = e_tot * ds_g + _dot(cg, dye_b, TN)

        dadt = _dot_exact_lhs(upper.astype(BF16), _split3(da_cum))
        ddt = ddt + dadt * a
        dalog_ref[...] += jnp.sum(dadt * dt, axis=0, keepdims=True)
        dpre = ddt * _sigmoid(pre)
        ddtb_ref[...] += jnp.sum(dpre, axis=0, keepdims=True)
        ddt_ref[...] = dpre.astype(BF16)

        @pl.when(i == nc - 1)
        def _():
            dalog_ref[...] = dalog_ref[...] * a
            dsk = _dot_terms(_split3(jnp.broadcast_to(dskcol[...], (8, SSD_D_INNER))), et_ref[...])
            ddsk_ref[...] = jnp.max(dsk, axis=0, keepdims=True)

    def chunk(w):
        return pl.BlockSpec((CHUNK, w), lambda i: (nc - 1 - i, 0))

    def const(shape):
        return pl.BlockSpec(shape, lambda i: (0,) * len(shape))

    return pl.pallas_call(
        body, name=name, grid=(nc,),
        in_specs=[chunk(SSD_D_INNER), chunk(SSD_D_INNER), chunk(SSD_D_INNER), chunk(SSD_XBC), chunk(LANES),
                  pl.BlockSpec((1, SSD_STATE, SSD_D_INNER), lambda i: (nc - 1 - i, 0, 0)),
                  const((1, LANES)), const((1, LANES)), const((1, SSD_D_INNER)), const((1, SSD_D_INNER)),
                  const((1, SSD_D_INNER)), const((LANES, SSD_D_INNER)), const((SSD_D_INNER, LANES))],
        out_specs=[chunk(SSD_D_INNER), chunk(SSD_XBC), chunk(LANES), const((1, SSD_D_INNER)), const((1, LANES)),
                   const((1, LANES)), const((1, LANES))],
        out_shape=[jax.ShapeDtypeStruct((s, SSD_D_INNER), BF16), jax.ShapeDtypeStruct((s, SSD_XBC), F32),
                   jax.ShapeDtypeStruct((s, LANES), BF16), jax.ShapeDtypeStruct((1, SSD_D_INNER), F32),
                   jax.ShapeDtypeStruct((1, LANES), F32), jax.ShapeDtypeStruct((1, LANES), F32),
                   jax.ShapeDtypeStruct((1, LANES), F32)],
        scratch_shapes=[pltpu.VMEM((SSD_STATE, SSD_D_INNER), F32), pltpu.VMEM((CHUNK, SSD_D_INNER), F32),
                        pltpu.VMEM((1, SSD_D_INNER), F32)],
        compiler_params=_params(("arbitrary",)),
    )(dya, y, z, xbc, dt_raw, states, dt_bias, a_log, a_log_x, d_skip_x, norm_w, expand, expand_t)


GELU_K = math.sqrt(2.0 / math.pi)
GELU_C = 0.044715


def _gelu(x):
    return 0.5 * x * (1.0 + jnp.tanh(GELU_K * (x + GELU_C * x * x * x)))


def _gelu_grad(x):
    t = jnp.tanh(GELU_K * (x + GELU_C * x * x * x))
    return 0.5 * (1.0 + t) + 0.5 * x * (1.0 - t * t) * (GELU_K * (1.0 + 3.0 * GELU_C * x * x))


def _sgu_pre(uv_ref, uvb_ref, lnw_ref, lnb_ref):
    uv = uv_ref[...].astype(F32) + uvb_ref[...]
    guv = _gelu(uv)
    u = guv[:, :SGU_WIDTH]
    v = guv[:, SGU_WIDTH:]
    mu = jnp.mean(v, axis=-1, keepdims=True)
    vc = v - mu
    rstd = lax.rsqrt(jnp.mean(vc * vc, axis=-1, keepdims=True) + LN_EPS)
    vhat = vc * rstd
    vn = vhat * lnw_ref[...] + lnb_ref[...]
    return uv, u, vhat, rstd, vn


def _sgu_fwd(uv_raw, uv_b, ln_w, ln_b, w_sp, b_sp_t, *, name):
    s = uv_raw.shape[0]
    nc = s // CHUNK

    def body(uv_ref, uvb_ref, lnw_ref, lnb_ref, w_ref, bt_ref, o_ref):
        lower, _ = _tri_masks()
        _, u, _, _, vn = _sgu_pre(uv_ref, uvb_ref, lnw_ref, lnb_ref)
        vn_b = vn.astype(BF16)
        bt = bt_ref[...]
        for g in range(SGU_GROUPS):
            gs = slice(LANES * g, LANES * (g + 1))
            wc = jnp.where(lower, w_ref[g], 0.0).astype(BF16)
            mixed = _dot(wc, vn_b[:, gs]) + bt[:, g:g + 1]
            o_ref[:, gs] = (u[:, gs] * mixed).astype(BF16)

    def const(shape):
        return pl.BlockSpec(shape, lambda c: (0,) * len(shape))

    return pl.pallas_call(
        body, name=name, grid=(nc,),
        in_specs=[pl.BlockSpec((CHUNK, 2 * SGU_WIDTH), lambda c: (c, 0)), const((1, 2 * SGU_WIDTH)),
                  const((1, SGU_WIDTH)), const((1, SGU_WIDTH)), const((SGU_GROUPS, CHUNK, CHUNK)),
                  const((CHUNK, LANES))],
        out_specs=pl.BlockSpec((CHUNK, SGU_WIDTH), lambda c: (c, 0)),
        out_shape=jax.ShapeDtypeStruct((s, SGU_WIDTH), BF16),
        compiler_params=_params(("parallel",)),
    )(uv_raw, uv_b, ln_w, ln_b, w_sp, b_sp_t)


def _sgu_bwd(uv_raw, dyb, uv_b, ln_w, ln_b, w_sp, b_sp_t, group_sum, *, name):
    s = uv_raw.shape[0]
    nc = s // CHUNK

    def body(uv_ref, dy_ref, uvb_ref, lnw_ref, lnb_ref, w_ref, bt_ref, gsum_ref,
             duv_ref, dw_ref, dbt_ref, dlnw_ref, dlnb_ref, duvb_ref):
        @pl.when(pl.program_id(0) == 0)
        def _():
            dw_ref[...] = jnp.zeros_like(dw_ref)
            dbt_ref[...] = jnp.zeros_like(dbt_ref)
            dlnw_ref[...] = jnp.zeros_like(dlnw_ref)
            dlnb_ref[...] = jnp.zeros_like(dlnb_ref)
            duvb_ref[...] = jnp.zeros_like(duvb_ref)

        lower, _ = _tri_masks()
        uv, u, vhat, rstd, vn = _sgu_pre(uv_ref, uvb_ref, lnw_ref, lnb_ref)
        vn_b = vn.astype(BF16)
        bt = bt_ref[...]
        dy = dy_ref[...]
        du_parts, dvn_parts, dmix_parts = [], [], []
        for g in range(SGU_GROUPS):
            gs = slice(LANES * g, LANES * (g + 1))
            wc = jnp.where(lower, w_ref[g], 0.0).astype(BF16)
            mixed = _dot(wc, vn_b[:, gs]) + bt[:, g:g + 1]
            du_parts.append(dy[:, gs] * mixed)
            dmix = dy[:, gs] * u[:, gs]
            dmix_b = dmix.astype(BF16)
            dmix_parts.append(dmix)
            dw_ref[g] += jnp.where(lower, _dot(dmix_b, vn_b[:, gs], NT), 0.0)
            dvn_parts.append(_dot(wc, dmix_b, TN))
        dmixed = jnp.concatenate(dmix_parts, axis=1)
        dbt_ref[...] += _dot_terms(_split3(dmixed), gsum_ref[...])
        dvn = jnp.concatenate(dvn_parts, axis=1)
        dlnw_ref[...] += jnp.sum(dvn * vhat, axis=0, keepdims=True)
        dlnb_ref[...] += jnp.sum(dvn, axis=0, keepdims=True)
        dvhat = dvn * lnw_ref[...]
        dv = rstd * (dvhat - jnp.mean(dvhat, axis=-1, keepdims=True)
                     - vhat * jnp.mean(dvhat * vhat, axis=-1, keepdims=True))
        dguv = jnp.concatenate(du_parts + [dv], axis=1)
        duv = dguv * _gelu_grad(uv)
        duvb_ref[...] += jnp.sum(duv, axis=0, keepdims=True)
        duv_ref[...] = duv.astype(BF16)

    def const(shape):
        return pl.BlockSpec(shape, lambda c: (0,) * len(shape))

    return pl.pallas_call(
        body, name=name, grid=(nc,),
        in_specs=[pl.BlockSpec((CHUNK, 2 * SGU_WIDTH), lambda c: (c, 0)),
                  pl.BlockSpec((CHUNK, SGU_WIDTH), lambda c: (c, 0)), const((1, 2 * SGU_WIDTH)),
                  const((1, SGU_WIDTH)), const((1, SGU_WIDTH)), const((SGU_GROUPS, CHUNK, CHUNK)),
                  const((CHUNK, LANES)), const((SGU_WIDTH, LANES))],
        out_specs=[pl.BlockSpec((CHUNK, 2 * SGU_WIDTH), lambda c: (c, 0)), const((SGU_GROUPS, CHUNK, CHUNK)),
                   const((CHUNK, LANES)), const((1, SGU_WIDTH)), const((1, SGU_WIDTH)), const((1, 2 * SGU_WIDTH))],
        out_shape=[jax.ShapeDtypeStruct((s, 2 * SGU_WIDTH), BF16),
                   jax.ShapeDtypeStruct((SGU_GROUPS, CHUNK, CHUNK), F32), jax.ShapeDtypeStruct((CHUNK, LANES), F32),
                   jax.ShapeDtypeStruct((1, SGU_WIDTH), F32), jax.ShapeDtypeStruct((1, SGU_WIDTH), F32),
                   jax.ShapeDtypeStruct((1, 2 * SGU_WIDTH), F32)],
        compiler_params=_params(("arbitrary",)),
    )(uv_raw, dyb, uv_b, ln_w, ln_b, w_sp, b_sp_t, group_sum)


def _gate_fwd(gates_raw, b_gate, p_a, p_b, *, name, tm=512):
    s = p_a.shape[0]
    tm = min(tm, s)

    def body(ga_ref, gb_ref, ba_ref, bb_ref, pa_ref, pb_ref, o_ref):
        ga = _sigmoid(ga_ref[...].astype(F32) + ba_ref[...])
        gb = _sigmoid(gb_ref[...].astype(F32) + bb_ref[...])
        o_ref[...] = (ga * pa_ref[...] + gb * pb_ref[...]).astype(BF16)

    t_a = pl.BlockSpec((tm, D_MODEL), lambda i: (i, 0))
    t_b = pl.BlockSpec((tm, D_MODEL), lambda i: (i, 1))
    r_a = pl.BlockSpec((1, D_MODEL), lambda i: (0, 0))
    r_b = pl.BlockSpec((1, D_MODEL), lambda i: (0, 1))
    return pl.pallas_call(
        body, name=name, grid=(s // tm,),
        in_specs=[t_a, t_b, r_a, r_b, t_a, t_a], out_specs=t_a,
        out_shape=jax.ShapeDtypeStruct((s, D_MODEL), BF16),
        compiler_params=_params(("parallel",)),
    )(gates_raw, gates_raw, b_gate, b_gate, p_a, p_b)


def _gate_bwd(gates_raw, b_gate, p_a, p_b, dm, *, name, tm=512):
    s = p_a.shape[0]
    tm = min(tm, s)

    def body(ga_ref, gb_ref, ba_ref, bb_ref, pa_ref, pb_ref, dm_ref, dpa_ref, dpb_ref, dga_ref, dgb_ref,
             dba_ref, dbb_ref):
        @pl.when(pl.program_id(0) == 0)
        def _():
            dba_ref[...] = jnp.zeros_like(dba_ref)
            dbb_ref[...] = jnp.zeros_like(dbb_ref)

        d = dm_ref[...]
        for g_ref, b_ref, p_ref, dp_ref, dg_ref, db_ref in ((ga_ref, ba_ref, pa_ref, dpa_ref, dga_ref, dba_ref),
                                                            (gb_ref, bb_ref, pb_ref, dpb_ref, dgb_ref, dbb_ref)):
            sg = _sigmoid(g_ref[...].astype(F32) + b_ref[...])
            dp_ref[...] = (d * sg).astype(BF16)
            dg = d * p_ref[...] * (sg * (1.0 - sg))
            dg_ref[...] = dg.astype(BF16)
            db_ref[...] += jnp.sum(dg, axis=0, keepdims=True)

    t_a = pl.BlockSpec((tm, D_MODEL), lambda i: (i, 0))
    t_b = pl.BlockSpec((tm, D_MODEL), lambda i: (i, 1))
    r_a = pl.BlockSpec((1, D_MODEL), lambda i: (0, 0))
    r_b = pl.BlockSpec((1, D_MODEL), lambda i: (0, 1))
    big = jax.ShapeDtypeStruct((s, D_MODEL), BF16)
    row = jax.ShapeDtypeStruct((1, D_MODEL), F32)
    return pl.pallas_call(
        body, name=name, grid=(s // tm,),
        in_specs=[t_a, t_b, r_a, r_b, t_a, t_a, t_a], out_specs=[t_a, t_a, t_a, t_a, r_a, r_a],
        out_shape=[big, big, big, big, row, row],
        compiler_params=_params(("arbitrary",)),
    )(gates_raw, gates_raw, b_gate, b_gate, p_a, p_b, dm)


def _adamw_update(w_ref, g_ref, m_ref, v_ref, d_ref, mo_ref, vo_ref):
    gv = g_ref[...]
    mn = ADAM_B1 * m_ref[...] + (1.0 - ADAM_B1) * gv
    vn = ADAM_B2 * v_ref[...] + (1.0 - ADAM_B2) * (gv * gv)
    m_hat = mn / (1.0 - ADAM_B1 ** ADAM_STEP)
    v_hat = vn / (1.0 - ADAM_B2 ** ADAM_STEP)
    d_ref[...] = -ADAM_LR * (m_hat / (jnp.sqrt(v_hat) + ADAM_EPS) + ADAM_WD * w_ref[...])
    mo_ref[...] = mn
    vo_ref[...] = vn


def _adamw_many(ws, gs, ms, vs, *, name):
    n = len(ws)

    def body(*refs):
        for i in range(n):
            _adamw_update(*[refs[k * n + i] for k in range(7)])

    whole = pl.BlockSpec(memory_space=pltpu.VMEM)
    sds = [jax.ShapeDtypeStruct(w.shape, F32) for w in ws]
    outs = pl.pallas_call(
        body, name=name, in_specs=[whole] * (4 * n), out_specs=[whole] * (3 * n), out_shape=sds * 3,
        compiler_params=pltpu.CompilerParams(vmem_limit_bytes=VMEM_LIMIT),
    )(*ws, *gs, *ms, *vs)
    return outs[:n], outs[n:2 * n], outs[2 * n:]


def _adamw(w, g, m, v, *, name, tr=128):
    r, c = w.shape
    tr = min(tr, r)
    assert r % tr == 0, (name, r, tr)
    body = functools.partial(_adamw_update)

    blk = pl.BlockSpec((tr, c), lambda i: (i, 0))
    sds = jax.ShapeDtypeStruct((r, c), F32)
    return pl.pallas_call(
        body, name=name, grid=(r // tr,), in_specs=[blk] * 4, out_specs=[blk] * 3, out_shape=[sds] * 3,
        compiler_params=_params(("parallel",)),
    )(w, g, m, v)


def _adamw_two_sums(w, g_a, g_b, m, v, *, name, tr=128):
    r, c = w.shape
    tr = min(tr, r)
    assert r % tr == 0, (name, r, tr)

    def body(w_ref, ga_ref, gb_ref, m_ref, v_ref, g_ref, d_ref, mo_ref, vo_ref):
        g_ref[...] = ga_ref[...] + gb_ref[...]
        _adamw_update(w_ref, g_ref, m_ref, v_ref, d_ref, mo_ref, vo_ref)

    blk = pl.BlockSpec((tr, c), lambda i: (i, 0))
    sds = jax.ShapeDtypeStruct((r, c), F32)
    return pl.pallas_call(
        body, name=name, grid=(r // tr,), in_specs=[blk] * 5, out_specs=[blk] * 4, out_shape=[sds] * 4,
        compiler_params=_params(("parallel",)),
    )(w, g_a, g_b, m, v)


def _tile(n, pref):
    if n <= pref:
        return n
    best = LANES
    for t in range(LANES, pref + 1, LANES):
        if n % t == 0:
            best = t
    return best


MATMUL_BLOCK_BYTES = 20 * 1024 * 1024


def _mm(pairs, name, **kw):
    trans_b = kw.get("trans_b", False)
    m = (pairs[0][0][0] if isinstance(pairs[0][0], tuple) else pairs[0][0]).shape[0]
    ktot, n = 0, None
    for _, b in pairs:
        shape = b[0].shape[1:] if isinstance(b, tuple) else b.shape
        ktot += shape[1] if trans_b else shape[0]
        n = shape[0] if trans_b else shape[1]
    out_bytes = 4 * (2 if kw.get("add") is not None else 1)
    best = None
    for tm in (256, 512, 1024):
        for tn in range(LANES, min(n, 1536) + 1, LANES):
            if m % min(tm, m) or n % tn:
                continue
            fits = 2 * ktot * (min(tm, m) + tn) + out_bytes * min(tm, m) * tn <= MATMUL_BLOCK_BYTES
            if fits and (best is None or min(tm, m) * tn >= best[0] * best[1]):
                best = (min(tm, m), tn)
    return _matmul(pairs, tm=best[0], tn=best[1], name=name, **kw)


def _wgrad(a, b, name, **kw):
    return _matmul_tn(a, b, tk=_tile(a.shape[1], 1408), tn=kw.pop("tn", _tile(b.shape[1], 1024)), tm=2048,
                      name=name, **kw)


def _local_step(x, target, get_weight, small, emit_grad):
    heads = jnp.arange(SSD_D_INNER) // SSD_HEAD_DIM
    expand = (jnp.arange(LANES)[:, None] == heads[None, :]).astype(BF16)
    expand_t = expand.T
    group_sum = (jnp.arange(SGU_WIDTH)[:, None] // LANES == jnp.arange(LANES)[None, :]).astype(BF16)
    pad_h = LANES - SSD_HEADS
    dt_bias = jnp.pad(small["dt_bias"], ((0, 0), (0, pad_h)))
    a_log = jnp.pad(small["a_log"], ((0, 0), (0, pad_h)))
    a_log_x = jnp.repeat(small["a_log"], SSD_HEAD_DIM, axis=1)
    d_skip_x = jnp.repeat(small["d_skip"], SSD_HEAD_DIM, axis=1)
    b_sp_t = jnp.pad(small["b_spatial"][0].T, ((0, 0), (0, LANES - SGU_GROUPS)))
    w_sp = small["w_spatial"][0]
    conv_a_w = jnp.pad(small["conv_a_w"], ((0, 4), (0, 0)))
    conv_f_w = jnp.pad(small["conv_f_w"], ((0, 5), (0, 0)))
    final_w = small["final_norm_w"].reshape(1, D_MODEL)

    n1 = _rms_fwd(x, small["norm1_w"], name="rms1_fwd")
    wts = dict(get_weight("w_in", n1))
    z = _mm([(n1, wts["in_z"])], "in_z")
    xbc_raw = _mm([(n1, wts["in_xbc"])], "in_xbc")
    dt_raw = _mm([(n1, wts["in_dt"])], "in_dt")
    uv_raw = _mm([(n1, wts["in_uv"])], "in_uv", out_dtype=BF16)
    gates_raw = _mm([(n1, wts["in_gate"])], "in_gate", out_dtype=BF16)
    xbc = _conv_a_fwd(xbc_raw, conv_a_w, small["conv_a_b"], name="conv_a_fwd")
    y, y_a, states = _ssd_fwd(xbc, dt_raw, z, dt_bias, a_log, a_log_x, d_skip_x, small["ssd_norm_w"], expand,
                              name="ssd_fwd")
    y_b = _sgu_fwd(uv_raw, small["uv_b"], small["v_ln_w"], small["v_ln_b"], w_sp, b_sp_t, name="sgu_fwd")
    wts.update(get_weight("w_branch", y_b))
    p_a = _mm([(y_a, wts["branch_a"])], "branch_a")
    p_b = _mm([(y_b, wts["branch_b"])], "branch_b")
    mix = _gate_fwd(gates_raw, small["b_gate"], p_a, p_b, name="gate_fwd")
    wts.update(get_weight("w_out", mix))
    h1 = _mm([(mix, wts["out"])], "out_proj", add=x)
    n2 = _rms_fwd(h1, small["norm2_w"], name="rms2_fwd")
    wts.update(get_weight("w_up", n2))
    up_w = wts["up"]
    up_cols = up_w.shape[2]
    up_raw = _matmul([(n2, (up_w, "cols"))], tm=1024, tn=up_cols, out_dtype=BF16, name="up_proj")
    act = _conv_f_fwd(up_raw, conv_f_w, small["conv_f_b"], name="conv_f_fwd")
    wts.update(get_weight("w_down", act))
    h2 = _mm([(act, wts["down"])], "down_proj", add=h1)
    loss, dh2, dh2_b, d_final = _final_fwd_bwd(h2, final_w, target, name="final_norm_loss")

    dact = _mm([(dh2_b, wts["down"])], "down_dgrad", trans_b=True)
    started = emit_grad("w_down", _wgrad(act, dh2_b, "down_wgrad"))
    dup_a, dup_v, dwf_a, dwf_v, dbf_a, dbf_v = _conv_f_bwd(up_raw, conv_f_w, small["conv_f_b"], dact,
                                                           name="conv_f_bwd")
    dn2 = _mm([((dup_a, 0), (up_w, 0)), ((dup_a, 1), (up_w, 1)), ((dup_v, 0), (up_w, 2)), ((dup_v, 1), (up_w, 3))],
              "up_dgrad", trans_b=True, after=started)
    started = emit_grad("w_up", jnp.concatenate([_wgrad(n2, dup_a, "up_wgrad_a", tn=up_cols, stack_out=True),
                                                 _wgrad(n2, dup_v, "up_wgrad_v", tn=up_cols, stack_out=True)], axis=0))
    dh1, dh1_b, d_norm2 = _rms_bwd(h1, small["norm2_w"], dn2, dh2, name="rms2_bwd")
    dmix = _mm([(dh1_b, wts["out"])], "out_dgrad", trans_b=True, after=started)
    started = emit_grad("w_out", _wgrad(mix, dh1_b, "out_wgrad"))
    dp_a, dp_b, dg_a, dg_b, dbg_a, dbg_b = _gate_bwd(gates_raw, small["b_gate"], p_a, p_b, dmix, name="gate_bwd")
    dya = _mm([(dp_a, wts["branch_a"])], "branch_a_dgrad", trans_b=True, after=started)
    dyb = _mm([(dp_b, wts["branch_b"])], "branch_b_dgrad", trans_b=True)
    started_branch = emit_grad("w_branch", jnp.concatenate([_wgrad(y_a, dp_a, "branch_a_wgrad"),
                                                            _wgrad(y_b, dp_b, "branch_b_wgrad")], axis=0))
    duv, d_wsp, d_bsp_t, d_lnw, d_lnb, d_uvb = _sgu_bwd(uv_raw, dyb, small["uv_b"], small["v_ln_w"],
                                                        small["v_ln_b"], w_sp, b_sp_t, group_sum, name="sgu_bwd")
    dz, dxbc, ddt, d_ssd_nw, d_dskip, d_alog, d_dtb = _ssd_bwd(
        dya, y, z, xbc, dt_raw, states, dt_bias, a_log, a_log_x, d_skip_x, small["ssd_norm_w"], expand, expand_t,
        name="ssd_bwd")
    dxbc_raw, d_conv_a_w, d_conv_a_b = _conv_a_bwd(xbc_raw, conv_a_w, small["conv_a_b"], dxbc, name="conv_a_bwd")
    started = emit_grad("w_in", {
        "in_z": _wgrad(n1, dz, "in_z_wgrad", after=started_branch), "in_xbc": _wgrad(n1, dxbc_raw, "in_xbc_wgrad"),
        "in_dt": _wgrad(n1, ddt, "in_dt_wgrad")[:, :SSD_HEADS], "in_uv": _wgrad(n1, duv, "in_uv_wgrad"),
        "in_gate_a": _wgrad(n1, dg_a, "in_gate_a_wgrad"), "in_gate_b": _wgrad(n1, dg_b, "in_gate_b_wgrad")})
    dn1 = _mm([(dz, wts["in_z"]), (dxbc_raw, wts["in_xbc"]), (ddt, wts["in_dt"]), (duv, wts["in_uv"]),
               (dg_a, wts["in_gate_a"]), (dg_b, wts["in_gate_b"])], "in_dgrad", trans_b=True, after=started)
    dx, _, d_norm1 = _rms_bwd(x, small["norm1_w"], dn1, dh1, name="rms1_bwd")

    grads_small = {
        "norm1_w": d_norm1, "b_gate": jnp.concatenate([dbg_a, dbg_b], axis=1),
        "conv_a_w": d_conv_a_w[:4], "conv_a_b": d_conv_a_b,
        "dt_bias": d_dtb[:, :SSD_HEADS], "a_log": d_alog[:, :SSD_HEADS], "d_skip": d_dskip[:, :SSD_HEADS],
        "ssd_norm_w": d_ssd_nw, "uv_b": d_uvb, "v_ln_w": d_lnw, "v_ln_b": d_lnb,
        "w_spatial": d_wsp[None], "b_spatial": d_bsp_t[:, :SGU_GROUPS].T[None],
        "norm2_w": d_norm2, "conv_f_w": jnp.concatenate([dwf_a[:3], dwf_v[:3]], axis=1),
        "conv_f_b": jnp.concatenate([dbf_a, dbf_v], axis=1), "final_norm_w": d_final.reshape(D_MODEL),
    }
    return loss, dx, grads_small


HBM = pl.BlockSpec(memory_space=pl.ANY)
MESH = pl.DeviceIdType.MESH


def _mesh_pos():
    return lax.axis_index("x"), lax.axis_index("y"), lax.axis_index("c")


def _other_chips(x, y):
    return [(1 - x, y), (x, 1 - y), (1 - x, 1 - y)]


def _remote(src, dst, send_sems, recv_sems, k, dev):
    return pltpu.make_async_remote_copy(src_ref=src, dst_ref=dst, send_sem=send_sems.at[k], recv_sem=recv_sems.at[k],
                                        device_id=dev, device_id_type=MESH)


def _dma_sems(n):
    return [pltpu.SemaphoreType.DMA((n,)), pltpu.SemaphoreType.DMA((n,))]


HBM_ONLY = pl.BlockSpec(memory_space=pltpu.HBM)
SEMAPHORES = pl.BlockSpec(memory_space=pltpu.SEMAPHORE)
DATAFLOW_EFFECT = pltpu.SideEffectType.DATAFLOW_SIDE_EFFECTING
N_PEER_CHIPS = N_CHIPS - 1


def _gather_sends(w_ref, land_ref, send_sems, recv_sems):
    x, y, c = _mesh_pos()
    return [_remote(w_ref.at[c], land_ref.at[2 * x + y, c], send_sems, recv_sems, k, (px, py, c))
            for k, (px, py) in enumerate(_other_chips(x, y))]


def _gather_arrivals(w_ref, land_ref, send_sems, recv_sems):
    x, y, c = _mesh_pos()
    return [_remote(w_ref.at[c], land_ref.at[2 * px + py, c], send_sems, recv_sems, k, (px, py, c))
            for k, (px, py) in enumerate(_other_chips(x, y))]


def _gather_whole_sends(w_ref, land_ref, send_sems, recv_sems):
    x, y, c = _mesh_pos()
    return [_remote(w_ref, land_ref.at[2 * x + y], send_sems, recv_sems, k, (px, py, c))
            for k, (px, py) in enumerate(_other_chips(x, y))]


def _gather_whole_arrivals(w_ref, land_ref, send_sems, recv_sems):
    x, y, c = _mesh_pos()
    return [_remote(w_ref, land_ref.at[2 * px + py], send_sems, recv_sems, k, (px, py, c))
            for k, (px, py) in enumerate(_other_chips(x, y))]


def _scatter_sends(h_ref, land_ref, send_sems, recv_sems):
    x, y, c = _mesh_pos()
    return [_remote(h_ref.at[2 * px + py], land_ref.at[2 * x + y], send_sems, recv_sems, k, (px, py, c))
            for k, (px, py) in enumerate(_other_chips(x, y))]


def _scatter_arrivals(h_ref, land_ref, send_sems, recv_sems):
    x, y, c = _mesh_pos()
    return [_remote(h_ref.at[2 * x + y], land_ref.at[2 * px + py], send_sems, recv_sems, k, (px, py, c))
            for k, (px, py) in enumerate(_other_chips(x, y))]


def _exchange_start(sources, landing_shapes, sends, *, after=None, name):
    n = len(sources)
    extra = [] if after is None else [after]

    def body(*refs):
        sems = refs[2 * n + len(extra):4 * n + len(extra)]
        for i in range(n):
            send_i = sends[i] if isinstance(sends, (list, tuple)) else sends
            for cp in send_i(refs[i], refs[n + i], sems[2 * i], sems[2 * i + 1]):
                cp.start()
        refs[-1][...] = jnp.zeros_like(refs[-1])

    hbm = [pltpu.HBM(s.shape, s.dtype) for s in sources] + [pltpu.HBM(shp, s.dtype)
                                                             for shp, s in zip(landing_shapes, sources)]
    outs = pl.pallas_call(
        body, name=name,
        out_shape=tuple([pltpu.SemaphoreType.DMA((N_PEER_CHIPS,))] * (2 * n) + hbm
                        + [jax.ShapeDtypeStruct((8, LANES), F32)]),
        in_specs=[HBM_ONLY] * (2 * n) + [pl.BlockSpec(memory_space=pl.ANY)] * len(extra),
        out_specs=tuple([SEMAPHORES] * (2 * n) + [HBM_ONLY] * (2 * n) + [pl.BlockSpec(memory_space=pltpu.VMEM)]),
        input_output_aliases={i: 2 * n + i for i in range(2 * n)},
        compiler_params=pltpu.CompilerParams(has_side_effects=DATAFLOW_EFFECT),
    )(*[pltpu.with_memory_space_constraint(s, pltpu.HBM) for s in sources],
      *[pltpu.with_memory_space_constraint(lax.empty(shp, s.dtype), pltpu.HBM)
        for shp, s in zip(landing_shapes, sources)], *extra)
    pending = [(outs[2 * i], outs[2 * i + 1], outs[2 * n + i], outs[3 * n + i]) for i in range(n)]
    return pending, outs[-1]


def _exchange_wait(pending, after, sends, arrivals, *, name):
    send_sems, recv_sems, source, landing = pending

    def body(src_ref, land_ref, send_ref, recv_ref, after_ref, src_out, land_out):
        for cp in sends(src_ref, land_ref, send_ref, recv_ref):
            cp.wait_send()
        for cp in arrivals(src_ref, land_ref, send_ref, recv_ref):
            cp.wait_recv()

    return pl.pallas_call(
        body, name=name,
        out_shape=(pltpu.HBM(source.shape, source.dtype), pltpu.HBM(landing.shape, landing.dtype)),
        in_specs=[HBM_ONLY, HBM_ONLY, SEMAPHORES, SEMAPHORES, pl.BlockSpec(memory_space=pl.ANY)],
        out_specs=(HBM_ONLY, HBM_ONLY), input_output_aliases={0: 0, 1: 1},
        compiler_params=pltpu.CompilerParams(has_side_effects=DATAFLOW_EFFECT),
    )(source, landing, send_sems, recv_sems, after)


def _gather_ici(shard, *, name):
    _, rh, cols = shard.shape

    def body(w_ref, o_ref, send_sems, recv_sems):
        x, y, c = _mesh_pos()
        mine = 2 * x + y
        sends = []
        for k, (px, py) in enumerate(_other_chips(x, y)):
            cp = _remote(w_ref.at[c], o_ref.at[mine, c], send_sems, recv_sems, k, (px, py, c))
            cp.start()
            sends.append(cp)
        for k, (px, py) in enumerate(_other_chips(x, y)):
            _remote(w_ref.at[c], o_ref.at[2 * px + py, c], send_sems, recv_sems, k, (px, py, c)).wait_recv()
        for cp in sends:
            cp.wait_send()

    return pl.pallas_call(
        body, name=name, in_specs=[HBM], out_specs=HBM,
        out_shape=jax.ShapeDtypeStruct((N_CHIPS, 2, rh, cols), shard.dtype), scratch_shapes=_dma_sems(3),
    )(shard)


def _gather_d2d(parts, *, name):
    def body(a_ref, o_ref, send_sems, recv_sems):
        x, y, c = _mesh_pos()
        sibling = (x, y, 1 - c)
        sends = []
        for k, (px, py) in enumerate(_other_chips(x, y)):
            cp = _remote(a_ref.at[2 * px + py, c], o_ref.at[2 * px + py, c], send_sems, recv_sems, k, sibling)
            cp.start()
            sends.append(cp)
        for k, (px, py) in enumerate(_other_chips(x, y)):
            _remote(a_ref.at[2 * px + py, c], o_ref.at[2 * px + py, 1 - c], send_sems, recv_sems, k, sibling).wait_recv()
        for cp in sends:
            cp.wait_send()

    return pl.pallas_call(
        body, name=name, in_specs=[HBM], out_specs=HBM,
        out_shape=jax.ShapeDtypeStruct(parts.shape, parts.dtype),
        input_output_aliases={0: 0}, scratch_shapes=_dma_sems(3),
    )(parts)


def _all_gather_chips(shard_flat, name):
    rows, cols = shard_flat.shape
    parts = _gather_ici(shard_flat.reshape(2, rows // 2, cols), name=name + "_ici")
    others = _gather_d2d(parts, name=name + "_d2d").reshape(N_CHIPS, rows, cols)
    chip = 2 * lax.axis_index("x") + lax.axis_index("y")
    return lax.dynamic_update_slice(others, shard_flat[None], (chip, 0, 0))


def _row_tile(rows, mult, cap):
    best = mult
    for t in range(mult, min(rows, cap) + 1, mult):
        if rows % t == 0:
            best = t
    assert rows % best == 0, (rows, mult)
    return best


def _swap_halves_d2d(g, *, name):
    _, _, rh, cols = g.shape

    def body(g_ref, o_ref, send_sems, recv_sems):
        x, y, c = _mesh_pos()
        sibling = (x, y, 1 - c)
        sends = []
        for s in range(N_CHIPS):
            cp = _remote(g_ref.at[s, 1 - c], o_ref.at[s], send_sems, recv_sems, s, sibling)
            cp.start()
            sends.append(cp)
        for s in range(N_CHIPS):
            _remote(g_ref.at[s, c], o_ref.at[s], send_sems, recv_sems, s, sibling).wait_recv()
        for cp in sends:
            cp.wait_send()

    return pl.pallas_call(
        body, name=name, in_specs=[HBM], out_specs=HBM,
        out_shape=jax.ShapeDtypeStruct((N_CHIPS, rh, cols), g.dtype), scratch_shapes=_dma_sems(N_CHIPS),
    )(g)


def _add_own_half(g, arrived, core, *, name):
    _, _, rh, cols = g.shape
    mult = 16 if g.dtype == BF16 else 8
    tr = _row_tile(rh, mult, max(mult, (512 * 1024) // cols))

    def body(core_ref, g_ref, a_ref, o_ref):
        o_ref[...] = (g_ref[0].astype(F32) + a_ref[...].astype(F32)).astype(o_ref.dtype)

    grid_spec = pltpu.PrefetchScalarGridSpec(
        num_scalar_prefetch=1, grid=(N_CHIPS, rh // tr),
        in_specs=[pl.BlockSpec((1, 1, tr, cols), lambda s, i, core_ref: (s, core_ref[0], i, 0)),
                  pl.BlockSpec((1, tr, cols), lambda s, i, core_ref: (s, i, 0))],
        out_specs=pl.BlockSpec((1, tr, cols), lambda s, i, core_ref: (s, i, 0)))
    return pl.pallas_call(
        body, name=name, grid_spec=grid_spec, out_shape=jax.ShapeDtypeStruct((N_CHIPS, rh, cols), g.dtype),
        compiler_params=_params(("parallel", "parallel")),
    )(core, g, arrived)


def _scatter_ici(h, *, name):
    def body(h_ref, o_ref, send_sems, recv_sems):
        x, y, c = _mesh_pos()
        mine = 2 * x + y
        sends = []
        for k, (px, py) in enumerate(_other_chips(x, y)):
            cp = _remote(h_ref.at[2 * px + py], o_ref.at[mine], send_sems, recv_sems, k, (px, py, c))
            cp.start()
            sends.append(cp)
        for k, (px, py) in enumerate(_other_chips(x, y)):
            _remote(h_ref.at[mine], o_ref.at[2 * px + py], send_sems, recv_sems, k, (px, py, c)).wait_recv()
        for cp in sends:
            cp.wait_send()

    others = pl.pallas_call(
        body, name=name, in_specs=[HBM], out_specs=HBM, out_shape=jax.ShapeDtypeStruct(h.shape, h.dtype),
        scratch_shapes=_dma_sems(3),
    )(h)
    chip = 2 * lax.axis_index("x") + lax.axis_index("y")
    own = lax.dynamic_slice_in_dim(h, chip, 1, axis=0)
    return lax.dynamic_update_slice(others, own, (chip, 0, 0))


def _sum_chips(parts, *, name):
    _, rh, cols = parts.shape
    mult = 16 if parts.dtype == BF16 else 8
    tr = _row_tile(rh, mult, max(mult, (512 * 1024) // cols))

    def body(p_ref, o_ref):
        acc = p_ref[0].astype(F32)
        for s in range(1, N_CHIPS):
            acc = acc + p_ref[s].astype(F32)
        o_ref[...] = acc

    return pl.pallas_call(
        body, name=name, grid=(rh // tr,),
        in_specs=[pl.BlockSpec((N_CHIPS, tr, cols), lambda i: (0, i, 0))],
        out_specs=pl.BlockSpec((tr, cols), lambda i: (i, 0)),
        out_shape=jax.ShapeDtypeStruct((rh, cols), F32), compiler_params=_params(("parallel",)),
    )(parts)


def _share_d2d(f, *, name):
    fs = f if isinstance(f, (list, tuple)) else [f]
    others = _swap_with_sibling(fs, name=name)
    first = lax.axis_index("c") == 0
    both = [jnp.stack([jnp.where(first, a, b), jnp.where(first, b, a)]) for a, b in zip(fs, others)]
    return both if isinstance(f, (list, tuple)) else both[0]


def _swap_with_sibling(fs, *, name):
    n = len(fs)

    def body(*refs):
        x, y, c = _mesh_pos()
        sibling = (x, y, 1 - c)
        send_sems, recv_sems = refs[2 * n:]
        copies = [_remote(refs[i], refs[n + i], send_sems, recv_sems, i, sibling) for i in range(n)]
        for cp in copies:
            cp.start()
        for cp in copies:
            cp.wait()

    return pl.pallas_call(
        body, name=name, in_specs=[HBM] * n, out_specs=[HBM] * n,
        out_shape=[jax.ShapeDtypeStruct(a.shape, a.dtype) for a in fs], scratch_shapes=_dma_sems(n),
    )(*fs)


def _reduce_scatter_chips(g, core, name):
    _, rows, cols = g.shape
    g = g.reshape(N_CHIPS, 2, rows // 2, cols)
    arrived = _swap_halves_d2d(g, name=name + "_swap")
    chip_sum = _add_own_half(g, arrived, core, name=name + "_add2")
    parts = _scatter_ici(chip_sum, name=name + "_ici")
    total = _sum_chips(parts, name=name + "_sum4")
    return _share_d2d(total, name=name + "_share").reshape(rows, cols)


BIG = ("w_in", "w_branch", "w_out", "w_up", "w_down")
BIG_COLUMN_SHARDED = ("w_in", "w_up")
CONV = ("conv_a_w", "conv_f_w")
REPLICATED = ("norm1_w", "b_gate", "conv_a_b", "dt_bias", "a_log", "d_skip", "ssd_norm_w", "uv_b", "v_ln_w",
              "v_ln_b", "w_spatial", "b_spatial", "norm2_w", "conv_f_b", "final_norm_w")
WEIGHT_ORDER = ("norm1_w", "w_in", "b_gate", "conv_a_w", "conv_a_b", "dt_bias", "a_log", "d_skip", "ssd_norm_w",
                "uv_b", "v_ln_w", "v_ln_b", "w_spatial", "b_spatial", "w_branch", "w_out", "norm2_w", "w_up",
                "conv_f_w", "conv_f_b", "w_down", "final_norm_w")
SMALL_EXCHANGE_ROWS = 64


_GATE0 = SSD_IN + 2 * SGU_WIDTH
IN_SEGMENTS = {
    "in_z": (0, SSD_D_INNER), "in_xbc": (SSD_D_INNER, SSD_D_INNER + SSD_XBC), "in_dt": (SSD_D_INNER + SSD_XBC, SSD_IN),
    "in_uv": (SSD_IN, _GATE0), "in_gate": (_GATE0, IN_COLS), "in_gate_a": (_GATE0, _GATE0 + D_MODEL),
    "in_gate_b": (_GATE0 + D_MODEL, IN_COLS),
}
IN_GRAD_SEGMENTS = ("in_z", "in_xbc", "in_dt", "in_uv", "in_gate_a", "in_gate_b")


def _take_columns(parts, start, stop):
    out = []
    for a, first in parts:
        lo, hi = max(start, first), min(stop, first + a.shape[1])
        if lo < hi:
            out.append(a[:, lo - first:hi - first])
    return out[0] if len(out) == 1 else jnp.concatenate(out, axis=1)


def _flat_rows(arrays, row_multiple):
    flat = jnp.concatenate([a.reshape(-1) for a in arrays])
    rows = -(-flat.shape[0] // (LANES * row_multiple)) * row_multiple
    return jnp.pad(flat, (0, rows * LANES - flat.shape[0])).reshape(rows, LANES)


def _unflatten(flat, shapes):
    flat = flat.reshape(-1)
    out, off = [], 0
    for shp in shapes:
        n = math.prod(shp)
        out.append(flat[off:off + n].reshape(shp))
        off += n
    return out


def _from_chip_blocks(blocks, name):
    if name in BIG_COLUMN_SHARDED or name in CONV:
        k = blocks.shape[1]
        return jnp.transpose(blocks, (1, 0, 2)).reshape(k, -1)
    return blocks.reshape(-1, blocks.shape[-1])


def _to_chip_blocks(whole, name):
    if name in BIG_COLUMN_SHARDED or name in CONV:
        k, n = whole.shape
        return jnp.transpose(whole.reshape(k, N_CHIPS, n // N_CHIPS), (1, 0, 2))
    return whole.reshape(N_CHIPS, whole.shape[0] // N_CHIPS, whole.shape[1])


def kernel(x, norm1_w, w_in, b_gate, conv_a_w, conv_a_b, dt_bias, a_log, d_skip, ssd_norm_w, uv_b, v_ln_w, v_ln_b, w_spatial, b_spatial, w_branch, w_out, norm2_w, w_up, conv_f_w, conv_f_b, w_down, final_norm_w, loss_target, m_norm1_w, m_w_in, m_b_gate, m_conv_a_w, m_conv_a_b, m_dt_bias, m_a_log, m_d_skip, m_ssd_norm_w, m_uv_b, m_v_ln_w, m_v_ln_b, m_w_spatial, m_b_spatial, m_w_branch, m_w_out, m_norm2_w, m_w_up, m_conv_f_w, m_conv_f_b, m_w_down, m_final_norm_w, v_norm1_w, v_w_in, v_b_gate, v_conv_a_w, v_conv_a_b, v_dt_bias, v_a_log, v_d_skip, v_ssd_norm_w, v_uv_b, v_v_ln_w, v_v_ln_b, v_w_spatial, v_b_spatial, v_w_branch, v_w_out, v_norm2_w, v_w_up, v_conv_f_w, v_conv_f_b, v_w_down, v_final_norm_w):
    weights = dict(norm1_w=norm1_w, w_in=w_in, b_gate=b_gate, conv_a_w=conv_a_w, conv_a_b=conv_a_b, dt_bias=dt_bias,
                   a_log=a_log, d_skip=d_skip, ssd_norm_w=ssd_norm_w, uv_b=uv_b, v_ln_w=v_ln_w, v_ln_b=v_ln_b,
                   w_spatial=w_spatial, b_spatial=b_spatial, w_branch=w_branch, w_out=w_out, norm2_w=norm2_w,
                   w_up=w_up, conv_f_w=conv_f_w, conv_f_b=conv_f_b, w_down=w_down, final_norm_w=final_norm_w)
    mom1 = dict(norm1_w=m_norm1_w, w_in=m_w_in, b_gate=m_b_gate, conv_a_w=m_conv_a_w, conv_a_b=m_conv_a_b,
                dt_bias=m_dt_bias, a_log=m_a_log, d_skip=m_d_skip, ssd_norm_w=m_ssd_norm_w, uv_b=m_uv_b,
                v_ln_w=m_v_ln_w, v_ln_b=m_v_ln_b, w_spatial=m_w_spatial, b_spatial=m_b_spatial, w_branch=m_w_branch,
                w_out=m_w_out, norm2_w=m_norm2_w, w_up=m_w_up, conv_f_w=m_conv_f_w, conv_f_b=m_conv_f_b,
                w_down=m_w_down, final_norm_w=m_final_norm_w)
    mom2 = dict(norm1_w=v_norm1_w, w_in=v_w_in, b_gate=v_b_gate, conv_a_w=v_conv_a_w, conv_a_b=v_conv_a_b,
                dt_bias=v_dt_bias, a_log=v_a_log, d_skip=v_d_skip, ssd_norm_w=v_ssd_norm_w, uv_b=v_uv_b,
                v_ln_w=v_v_ln_w, v_ln_b=v_v_ln_b, w_spatial=v_w_spatial, b_spatial=v_b_spatial, w_branch=v_w_branch,
                w_out=v_w_out, norm2_w=v_norm2_w, w_up=v_w_up, conv_f_w=v_conv_f_w, conv_f_b=v_conv_f_b,
                w_down=v_w_down, final_norm_w=v_final_norm_w)
    chip = 2 * lax.axis_index("x") + lax.axis_index("y")
    core = lax.axis_index("c").astype(jnp.int32).reshape(1)

    whole = {}
    conv_shapes = [weights[n].shape[1:] for n in CONV]
    conv_gathered = _all_gather_chips(_flat_rows([weights[n] for n in CONV], 16), "gather_conv").reshape(N_CHIPS, -1)
    off = 0
    for n, shp in zip(CONV, conv_shapes):
        size = math.prod(shp)
        whole[n] = _from_chip_blocks(conv_gathered[:, off:off + size].reshape((N_CHIPS,) + shp), n)
        off += size
    shard_shapes = {n: weights[n].shape[1:] for n in BIG}
    halves = [weights[n][0].astype(BF16).reshape(2, shard_shapes[n][0] // 2, shard_shapes[n][1]) for n in BIG]
    sends = [_gather_sends if n == "w_in" else _gather_whole_sends for n in BIG]
    gathers, _ = _exchange_start(halves, [(N_CHIPS,) + h.shape for h in halves], sends, after=conv_gathered,
                                 name="gather_start")
    gathers = dict(zip(BIG, gathers))

    def get_weight(name, after):
        rows, cols = shard_shapes[name]
        if name == "w_in":
            own, landed = _exchange_wait(gathers[name], after, _gather_sends, _gather_arrivals,
                                         name="gather_" + name + "_wait")
            landed = _gather_d2d(landed, name="gather_" + name + "_d2d")
        else:
            own, landed = _exchange_wait(gathers[name], after, _gather_whole_sends, _gather_whole_arrivals,
                                         name="gather_" + name + "_wait")
        blocks = lax.dynamic_update_slice(landed.reshape(N_CHIPS, rows, cols), own.reshape(1, rows, cols),
                                          (chip, 0, 0))
        if name == "w_up":
            return {"up": blocks}
        if name == "w_in":
            parts = [(blocks[k], cols * k) for k in range(N_CHIPS)]
            segs = {n: _take_columns(parts, a, b) for n, (a, b) in IN_SEGMENTS.items()}
            segs["in_dt"] = jnp.pad(segs["in_dt"], ((0, 0), (0, LANES - SSD_HEADS)))
            return segs
        full = _from_chip_blocks(blocks, name)
        if name == "w_branch":
            return {"branch_a": full[:SSD_D_INNER], "branch_b": full[SSD_D_INNER:]}
        return {name[2:]: full}

    small = {n: weights[n] for n in REPLICATED}
    small["conv_a_w"] = whole["conv_a_w"]
    small["conv_f_w"] = whole["conv_f_w"]

    reductions = {}

    def emit_grad(name, g):
        if name == "w_in":
            parts = [(g[n], IN_SEGMENTS[n][0]) for n in IN_GRAD_SEGMENTS]
            cols = shard_shapes[name][1]
            g_blocks = jnp.stack([_take_columns(parts, cols * k, cols * (k + 1)) for k in range(N_CHIPS)])
        else:
            g_blocks = g if name == "w_up" else _to_chip_blocks(g, name)
        if name == "w_in":
            _, rows, cols = g_blocks.shape
            g_halves = g_blocks.reshape(N_CHIPS, 2, rows // 2, cols)
            arrived = _swap_halves_d2d(g_halves, name="reduce_" + name + "_swap")
            g_blocks = _add_own_half(g_halves, arrived, core, name="reduce_" + name + "_add2")
        own = lax.dynamic_slice_in_dim(g_blocks, chip, 1, axis=0)
        (pending,), started = _exchange_start([g_blocks], [g_blocks.shape], _scatter_sends,
                                              name="reduce_" + name + "_start")
        reductions[name] = (pending, own)
        return started

    loss, dx, grads_small = _local_step(x[0], loss_target[0], get_weight, small, emit_grad)

    order = ("w_down", "w_up", "w_out", "w_branch", "w_in")
    core_sums = []
    for n in order:
        pending, own = reductions[n]
        _, landed = _exchange_wait(pending, dx, _scatter_sends, _scatter_arrivals, name="reduce_" + n + "_wait")
        parts = lax.dynamic_update_slice(landed, own, (chip, 0, 0))
        core_sums.append(_sum_chips(parts, name="reduce_" + n + "_sum4"))
    sibling_sums = dict(zip(order, _swap_with_sibling(core_sums, name="reduce_swap")))
    core_sums = dict(zip(order, core_sums))
    first = lax.axis_index("c") == 0
    w_in_halves = (core_sums["w_in"], sibling_sums["w_in"])
    w_in_grad = jnp.concatenate([jnp.where(first, w_in_halves[0], w_in_halves[1]),
                                 jnp.where(first, w_in_halves[1], w_in_halves[0])], axis=0)
    grads = {}

    small_names = REPLICATED + CONV
    small_shapes = [grads_small[n].shape for n in small_names]
    g_small = _flat_rows([grads_small[n] for n in small_names], N_CHIPS * 2 * SMALL_EXCHANGE_ROWS)
    red_small = _reduce_scatter_chips(g_small.reshape(N_CHIPS, -1, LANES), core, "reduce_small")
    all_small = _all_gather_chips(red_small, "gather_small")
    for n, g in zip(small_names, _unflatten(all_small, small_shapes)):
        if n in CONV:
            width = g.shape[1] // N_CHIPS
            g = lax.dynamic_slice_in_dim(g, chip * width, width, axis=1)
        grads[n] = g.reshape(weights[n].shape[1:]) if n != "final_norm_w" else g

    delta, new_m, new_v = {}, {}, {}
    for n in BIG:
        shp = weights[n].shape
        if n == "w_in":
            g_t = w_in_grad.T
            results = [g_t] + list(_adamw(weights[n][0].T, g_t, mom1[n][0].T, mom2[n][0].T, name="adamw_" + n,
                                          tr=_row_tile(g_t.shape[0], 8, 136)))
            results = [a.T for a in results]
        else:
            results = _adamw_two_sums(weights[n][0], core_sums[n], sibling_sums[n], mom1[n][0], mom2[n][0],
                                      name="adamw_" + n, tr=_row_tile(shp[1], 8, 136))
        grads[n], delta[n], new_m[n], new_v[n] = [a.reshape(shp) for a in results]
    small_all = [n for n in WEIGHT_ORDER if n not in BIG]

    def as_2d(a):
        return a.reshape(-1, a.shape[-1])

    results = _adamw_many(*[[as_2d(src[n]) for n in small_all] for src in (weights, grads, mom1, mom2)],
                          name="adamw_small")
    for n, dv, mv, vv in zip(small_all, *results):
        shp = weights[n].shape
        delta[n], new_m[n], new_v[n] = dv.reshape(shp), mv.reshape(shp), vv.reshape(shp)

    total_loss = lax.psum(loss[0, 0], ("x", "y", "c"))
    grad_out = [grads[n].reshape(weights[n].shape) for n in WEIGHT_ORDER]
    return (total_loss, dx[None], *grad_out, *[delta[n] for n in WEIGHT_ORDER], *[new_m[n] for n in WEIGHT_ORDER],
            *[new_v[n] for n in WEIGHT_ORDER])
```

```python
import functools
import math

import jax
import jax.numpy as jnp
from jax import lax
from jax.experimental import pallas as pl
from jax.experimental.pallas import tpu as pltpu

F32 = jnp.float32
BF16 = jnp.bfloat16
HI = lax.Precision.HIGHEST

D_MODEL = 1024
SSD_D_INNER = 2048
SSD_HEADS = 32
SSD_HEAD_DIM = 64
SSD_GROUPS = 4
SSD_HEADS_PER_GROUP = 8
SSD_STATE = 128
SSD_BC = 512
SSD_XBC = 3072
SSD_IN = 5152
SGU_WIDTH = 1024
SGU_GROUPS = 8
CHUNK = 128
IN_COLS = 9248
D_FF = 2816
NORM_EPS = 1e-6
LN_EPS = 1e-5
GROUP_COLS = SSD_HEADS_PER_GROUP * SSD_HEAD_DIM
LANES = 128

ADAM_LR = 0.001
ADAM_B1 = 0.9
ADAM_B2 = 0.999
ADAM_EPS = 1e-08
ADAM_WD = 0.01
ADAM_STEP = 10

N_CHIPS = 4
VMEM_LIMIT = 56 * 1024 * 1024

NT = (((1,), (1,)), ((), ()))
TN = (((0,), (0,)), ((), ()))
NN = (((1,), (0,)), ((), ()))


def _params(dims):
    return pltpu.CompilerParams(dimension_semantics=dims, vmem_limit_bytes=VMEM_LIMIT)


def _dot(a, b, dn=NN, precision=None):
    return lax.dot_general(a, b, dn, precision=precision, preferred_element_type=F32)


def _split3(x):
    hi = x.astype(BF16)
    rest = x - hi.astype(F32)
    mid = rest.astype(BF16)
    return hi, mid, (rest - mid.astype(F32)).astype(BF16)


def _dot_terms(terms, exact, dn=NN):
    out = None
    for t in terms:
        p = _dot(t, exact, dn)
        out = p if out is None else out + p
    return out


def _dot_exact_lhs(exact, terms):
    out = None
    for t in terms:
        p = _dot(exact, t)
        out = p if out is None else out + p
    return out


def _sigmoid(x):
    return 1.0 / (1.0 + jnp.exp(-x))


def _softplus(x):
    return jnp.maximum(x, 0.0) + jnp.log(1.0 + jnp.exp(-jnp.abs(x)))


def _matmul(pairs, *, trans_b=False, add=None, after=None, out_dtype=F32, tm=512, tn=512, name):
    def mat_shape(b):
        if isinstance(b, tuple) and b[1] == "cols":
            return (b[0].shape[1], b[0].shape[0] * b[0].shape[2])
        return b[0].shape[1:] if isinstance(b, tuple) else b.shape

    if isinstance(pairs[0][1], tuple) and pairs[0][1][1] == "cols":
        assert not trans_b and tn % LANES == 0 and pairs[0][1][0].shape[2] % tn == 0, name

    m = (pairs[0][0][0] if isinstance(pairs[0][0], tuple) else pairs[0][0]).shape[0]
    n = mat_shape(pairs[0][1])[0] if trans_b else mat_shape(pairs[0][1])[1]
    tm, tn = min(tm, m), min(tn, n)
    assert m % tm == 0 and n % tn == 0, (name, m, n, tm, tn)
    npairs = len(pairs)
    dn = NT if trans_b else NN

    def body(*refs):
        o_ref = refs[-1]
        acc = None
        for i in range(npairs):
            p = _dot(refs[2 * i][...].astype(BF16), refs[2 * i + 1][...].astype(BF16), dn)
            acc = p if acc is None else acc + p
        if add is not None:
            acc = acc + refs[2 * npairs][...]
        o_ref[...] = acc.astype(out_dtype)

    in_specs, args = [], []
    for a, b in pairs:
        bshape = mat_shape(b)
        k = bshape[1] if trans_b else bshape[0]
        assert bshape == ((n, k) if trans_b else (k, n)), (name, bshape)
        a, qa = a if isinstance(a, tuple) else (a, 0)
        assert a.shape[0] == m and a.shape[1] % k == 0, (name, a.shape, k)
        in_specs.append(pl.BlockSpec((tm, k), lambda i, j, qa=qa: (i, qa)))
        if isinstance(b, tuple) and b[1] == "cols":
            b = b[0]
            per = b.shape[2] // tn
            in_specs.append(pl.BlockSpec((None, k, tn), lambda i, j, per=per: (j // per, 0, j % per)))
        elif isinstance(b, tuple):
            b, qb = b
            if trans_b:
                in_specs.append(pl.BlockSpec((None, tn, k), lambda i, j, qb=qb: (qb, j, 0)))
            else:
                in_specs.append(pl.BlockSpec((None, k, tn), lambda i, j, qb=qb: (qb, 0, j)))
        elif trans_b:
            in_specs.append(pl.BlockSpec((tn, k), lambda i, j: (j, 0)))
        else:
            in_specs.append(pl.BlockSpec((k, tn), lambda i, j: (0, j)))
        args += [a, b]
    if add is not None:
        in_specs.append(pl.BlockSpec((tm, tn), lambda i, j: (i, j)))
        args.append(add)
    if after is not None:
        in_specs.append(pl.BlockSpec(memory_space=pl.ANY))
        args.append(after)
    return pl.pallas_call(
        body, name=name, grid=(m // tm, n // tn), in_specs=in_specs,
        out_specs=pl.BlockSpec((tm, tn), lambda i, j: (i, j)),
        out_shape=jax.ShapeDtypeStruct((m, n), out_dtype),
        compiler_params=_params(("parallel", "parallel")),
    )(*args)


def _matmul_tn(a, b, *, tk, tn, tm=1024, out_dtype=BF16, stack_out=False, after=None, name):
    m, k = a.shape
    n = b.shape[1]
    tm, tk, tn = min(tm, m), min(tk, k), min(tn, n)
    assert m % tm == 0 and k % tk == 0 and n % tn == 0, (name, m, k, n)
    nm = m // tm
    if stack_out:
        out_spec = pl.BlockSpec((None, tk, tn), lambda i, j, l: (j, i, 0))
        out_shape = jax.ShapeDtypeStruct((n // tn, k, tn), out_dtype)
    else:
        out_spec = pl.BlockSpec((tk, tn), lambda i, j, l: (i, j))
        out_shape = jax.ShapeDtypeStruct((k, n), out_dtype)

    def body(a_ref, b_ref, *rest):
        o_ref, acc = rest[-2:]
        mi = pl.program_id(2)

        @pl.when(mi == 0)
        def _():
            acc[...] = jnp.zeros_like(acc)

        acc[...] += _dot(a_ref[...].astype(BF16), b_ref[...].astype(BF16), TN)

        @pl.when(mi == nm - 1)
        def _():
            o_ref[...] = acc[...].astype(out_dtype)

    in_specs = [pl.BlockSpec((tm, tk), lambda i, j, l: (l, i)), pl.BlockSpec((tm, tn), lambda i, j, l: (l, j))]
    args = [a, b]
    if after is not None:
        in_specs.append(pl.BlockSpec(memory_space=pl.ANY))
        args.append(after)
    return pl.pallas_call(
        body, name=name, grid=(k // tk, n // tn, nm), in_specs=in_specs,
        out_specs=out_spec, out_shape=out_shape,
        scratch_shapes=[pltpu.VMEM((tk, tn), F32)],
        compiler_params=_params(("parallel", "parallel", "arbitrary")),
    )(*args)


def _rms_fwd(x, w, *, after=None, name, tm=512):
    s, d = x.shape
    tm = min(tm, s)
    extra = [] if after is None else [after]

    def body(x_ref, w_ref, *rest):
        o_ref = rest[-1]
        xv = x_ref[...]
        r = lax.rsqrt(jnp.mean(xv * xv, axis=-1, keepdims=True) + NORM_EPS)
        o_ref[...] = (xv * r * w_ref[...]).astype(BF16)

    return pl.pallas_call(
        body, name=name, grid=(s // tm,),
        in_specs=[pl.BlockSpec((tm, d), lambda i: (i, 0)), pl.BlockSpec((1, d), lambda i: (0, 0))]
        + [pl.BlockSpec(memory_space=pl.ANY)] * len(extra),
        out_specs=pl.BlockSpec((tm, d), lambda i: (i, 0)),
        out_shape=jax.ShapeDtypeStruct((s, d), BF16),
        compiler_params=_params(("parallel",)),
    )(x, w, *extra)


def _rms_bwd(x, w, dn, dres, *, name, tm=512):
    s, d = x.shape
    tm = min(tm, s)

    def body(x_ref, w_ref, dn_ref, dres_ref, dx_ref, dxb_ref, dw_ref):
        @pl.when(pl.program_id(0) == 0)
        def _():
            dw_ref[...] = jnp.zeros_like(dw_ref)

        xv = x_ref[...]
        r = lax.rsqrt(jnp.mean(xv * xv, axis=-1, keepdims=True) + NORM_EPS)
        xhat = xv * r
        dnv = dn_ref[...]
        dxhat = dnv * w_ref[...]
        dx = dres_ref[...] + r * (dxhat - xhat * jnp.mean(dxhat * xhat, axis=-1, keepdims=True))
        dx_ref[...] = dx
        dxb_ref[...] = dx.astype(BF16)
        dw_ref[...] += jnp.sum(dnv * xhat, axis=0, keepdims=True)

    tile = pl.BlockSpec((tm, d), lambda i: (i, 0))
    row = pl.BlockSpec((1, d), lambda i: (0, 0))
    return pl.pallas_call(
        body, name=name, grid=(s // tm,),
        in_specs=[tile, row, tile, tile], out_specs=[tile, tile, row],
        out_shape=[jax.ShapeDtypeStruct((s, d), F32), jax.ShapeDtypeStruct((s, d), BF16),
                   jax.ShapeDtypeStruct((1, d), F32)],
        compiler_params=_params(("arbitrary",)),
    )(x, w, dn, dres)


def _final_fwd_bwd(h2, wf, target, *, name, tm=512):
    s, d = h2.shape
    tm = min(tm, s)

    def body(h_ref, w_ref, t_ref, loss_ref, dh_ref, dhb_ref, dw_ref):
        @pl.when(pl.program_id(0) == 0)
        def _():
            dw_ref[...] = jnp.zeros_like(dw_ref)
            loss_ref[...] = jnp.zeros_like(loss_ref)

        hv = h_ref[...]
        r = lax.rsqrt(jnp.mean(hv * hv, axis=-1, keepdims=True) + NORM_EPS)
        xhat = hv * r
        err = xhat * w_ref[...] - t_ref[...]
        per_tok = jnp.mean(err * err, axis=-1, keepdims=True)
        loss_ref[...] += 0.5 * jnp.sum(per_tok, axis=0, keepdims=True)
        dy = err * (1.0 / d)
        dxhat = dy * w_ref[...]
        dh = r * (dxhat - xhat * jnp.mean(dxhat * xhat, axis=-1, keepdims=True))
        dh_ref[...] = dh
        dhb_ref[...] = dh.astype(BF16)
        dw_ref[...] += jnp.sum(dy * xhat, axis=0, keepdims=True)

    tile = pl.BlockSpec((tm, d), lambda i: (i, 0))
    row = pl.BlockSpec((1, d), lambda i: (0, 0))
    return pl.pallas_call(
        body, name=name, grid=(s // tm,),
        in_specs=[tile, row, tile],
        out_specs=[pl.BlockSpec((1, 1), lambda i: (0, 0)), tile, tile, row],
        out_shape=[jax.ShapeDtypeStruct((1, 1), F32), jax.ShapeDtypeStruct((s, d), F32),
                   jax.ShapeDtypeStruct((s, d), BF16), jax.ShapeDtypeStruct((1, d), F32)],
        compiler_params=_params(("arbitrary",)),
    )(h2, wf, target)


CONV_ROWS = 512
HALO = 8


def _rows_with_halo(ref, r0, rows, s, before, after):
    parts = []
    if before:
        prev = ref[pl.ds(pl.multiple_of(jnp.maximum(r0 - HALO, 0), HALO), HALO), :]
        parts.append(jnp.where(r0 > 0, prev, 0.0))
    parts.append(ref[pl.ds(r0, rows), :])
    if after:
        nxt = ref[pl.ds(pl.multiple_of(jnp.minimum(r0 + rows, s - HALO), HALO), HALO), :]
        parts.append(jnp.where(r0 + rows < s, nxt, 0.0))
    return jnp.concatenate(parts, axis=0) if len(parts) > 1 else parts[0]


def _fill_padded(x_ref, xp, s):
    zeros = jnp.zeros((HALO, xp.shape[1]), F32)
    xp[pl.ds(0, HALO), :] = zeros
    xp[pl.ds(HALO, s), :] = x_ref[...].astype(F32)
    xp[pl.ds(HALO + s, HALO), :] = zeros


def _shifted(xp, r0, k, rows):
    window = xp[pl.ds(r0, rows + HALO), :]
    if k == 0:
        return window[HALO:]
    return pltpu.roll(window, k, 0)[HALO:]


def _conv_taps(xp, r0, w_ref, kk, rows):
    acc = None
    for i in range(kk):
        term = w_ref[i:i + 1, :] * _shifted(xp, r0, kk - 1 - i, rows)
        acc = term if acc is None else acc + term
    return acc


def _row_loop(s, step):
    def body(r, carry):
        return step(pl.multiple_of(r * CONV_ROWS, CONV_ROWS), carry)
    return body


def _conv_bwd_rows(xp, r0, dpe, dp_sc, w_ref, kk):
    del dp_sc
    dp = dpe[:CONV_ROWS]
    dx = None
    dws = []
    for i in range(kk):
        k = kk - 1 - i
        dws.append(jnp.sum(dp * _shifted(xp, r0, k, CONV_ROWS), axis=0, keepdims=True))
        later = dp if k == 0 else pltpu.roll(dpe, dpe.shape[0] - k, 0)[:CONV_ROWS]
        term = w_ref[i:i + 1, :] * later
        dx = term if dx is None else dx + term
    return dx, dws, jnp.sum(dp, axis=0, keepdims=True)


def _conv_scratch(s, tc, n_padded, n_dp):
    return ([pltpu.VMEM((s + 2 * HALO, tc), F32)] * n_padded
            + [pltpu.VMEM((CONV_ROWS + HALO, tc), F32)] * n_dp)


def _conv_a_fwd(xraw, w, b, *, name, tc=128):
    s, c = xraw.shape
    kk = 4

    def body(x_ref, w_ref, b_ref, o_ref, xp):
        _fill_padded(x_ref, xp, s)

        def step(r0, carry):
            pre = _conv_taps(xp, r0, w_ref, kk, CONV_ROWS) + b_ref[...]
            o_ref[pl.ds(r0, CONV_ROWS), :] = pre * _sigmoid(pre)
            return carry

        lax.fori_loop(0, s // CONV_ROWS, _row_loop(s, step), 0)

    col = pl.BlockSpec((s, tc), lambda j: (0, j))
    return pl.pallas_call(
        body, name=name, grid=(c // tc,),
        in_specs=[col, pl.BlockSpec((8, tc), lambda j: (0, j)), pl.BlockSpec((1, tc), lambda j: (0, j))],
        out_specs=col, out_shape=jax.ShapeDtypeStruct((s, c), F32),
        scratch_shapes=_conv_scratch(s, tc, 1, 0),
        compiler_params=_params(("parallel",)),
    )(xraw, w, b)


def _conv_a_bwd(xraw, w, b, dy, *, name, tc=128):
    s, c = xraw.shape
    kk = 4

    def body(x_ref, w_ref, b_ref, dy_ref, dx_ref, dw_ref, db_ref, xp, dp_sc):
        _fill_padded(x_ref, xp, s)

        def step(r0, carry):
            pre = _conv_taps(xp, r0, w_ref, kk, CONV_ROWS + HALO) + b_ref[...]
            sg = _sigmoid(pre)
            dpe = _rows_with_halo(dy_ref, r0, CONV_ROWS, s, False, True) * (sg * (1.0 + pre * (1.0 - sg)))
            dx, dws, db = _conv_bwd_rows(xp, r0, dpe, dp_sc, w_ref, kk)
            dx_ref[pl.ds(r0, CONV_ROWS), :] = dx.astype(BF16)
            return tuple(acc + new for acc, new in zip(carry, dws + [db]))

        zero = jnp.zeros((1, tc), F32)
        sums = lax.fori_loop(0, s // CONV_ROWS, _row_loop(s, step), (zero,) * (kk + 1))
        db_ref[...] = sums[kk]
        dw_ref[...] = jnp.concatenate(list(sums[:kk]) + [jnp.zeros((8 - kk, tc), F32)], axis=0)

    col = pl.BlockSpec((s, tc), lambda j: (0, j))
    w8 = pl.BlockSpec((8, tc), lambda j: (0, j))
    row = pl.BlockSpec((1, tc), lambda j: (0, j))
    return pl.pallas_call(
        body, name=name, grid=(c // tc,),
        in_specs=[col, w8, row, col], out_specs=[col, w8, row],
        out_shape=[jax.ShapeDtypeStruct((s, c), BF16), jax.ShapeDtypeStruct((8, c), F32),
                   jax.ShapeDtypeStruct((1, c), F32)],
        scratch_shapes=_conv_scratch(s, tc, 1, 1),
        compiler_params=_params(("parallel",)),
    )(xraw, w, b, dy)


def _conv_f_fwd(up_raw, w, b, *, name, tc=128):
    s, c2 = up_raw.shape
    c = c2 // 2
    nb = c // tc
    kk = 3

    def body(xa_ref, xv_ref, wa_ref, wv_ref, ba_ref, bv_ref, o_ref, xap, xvp):
        _fill_padded(xa_ref, xap, s)
        _fill_padded(xv_ref, xvp, s)

        def step(r0, carry):
            a = _conv_taps(xap, r0, wa_ref, kk, CONV_ROWS) + ba_ref[...]
            v = _conv_taps(xvp, r0, wv_ref, kk, CONV_ROWS) + bv_ref[...]
            o_ref[pl.ds(r0, CONV_ROWS), :] = (a * _sigmoid(a) * v).astype(BF16)
            return carry

        lax.fori_loop(0, s // CONV_ROWS, _row_loop(s, step), 0)

    col_a = pl.BlockSpec((s, tc), lambda j: (0, j))
    col_v = pl.BlockSpec((s, tc), lambda j: (0, j + nb))
    return pl.pallas_call(
        body, name=name, grid=(nb,),
        in_specs=[col_a, col_v, pl.BlockSpec((8, tc), lambda j: (0, j)), pl.BlockSpec((8, tc), lambda j: (0, j + nb)),
                  pl.BlockSpec((1, tc), lambda j: (0, j)), pl.BlockSpec((1, tc), lambda j: (0, j + nb))],
        out_specs=col_a, out_shape=jax.ShapeDtypeStruct((s, c), BF16),
        scratch_shapes=_conv_scratch(s, tc, 2, 0),
        compiler_params=_params(("parallel",)),
    )(up_raw, up_raw, w, w, b, b)


def _conv_f_bwd(up_raw, w, b, dact, *, name, tc=128):
    s, c2 = up_raw.shape
    c = c2 // 2
    nb = c // tc
    kk = 3

    def body(xa_ref, xv_ref, wa_ref, wv_ref, ba_ref, bv_ref, d_ref,
             dxa_ref, dxv_ref, dwa_ref, dwv_ref, dba_ref, dbv_ref, xap, xvp, dpa_sc, dpv_sc):
        _fill_padded(xa_ref, xap, s)
        _fill_padded(xv_ref, xvp, s)

        def step(r0, carry):
            a = _conv_taps(xap, r0, wa_ref, kk, CONV_ROWS + HALO) + ba_ref[...]
            v = _conv_taps(xvp, r0, wv_ref, kk, CONV_ROWS + HALO) + bv_ref[...]
            sg = _sigmoid(a)
            d = _rows_with_halo(d_ref, r0, CONV_ROWS, s, False, True)
            dxa, dwas, dba = _conv_bwd_rows(xap, r0, d * v * (sg * (1.0 + a * (1.0 - sg))), dpa_sc, wa_ref, kk)
            dxv, dwvs, dbv = _conv_bwd_rows(xvp, r0, d * (a * sg), dpv_sc, wv_ref, kk)
            dxa_ref[pl.ds(r0, CONV_ROWS), :] = dxa.astype(BF16)
            dxv_ref[pl.ds(r0, CONV_ROWS), :] = dxv.astype(BF16)
            return tuple(acc + new for acc, new in zip(carry, dwas + [dba] + dwvs + [dbv]))

        zero = jnp.zeros((1, tc), F32)
        sums = lax.fori_loop(0, s // CONV_ROWS, _row_loop(s, step), (zero,) * (2 * kk + 2))
        pad = [jnp.zeros((8 - kk, tc), F32)]
        dwa_ref[...] = jnp.concatenate(list(sums[:kk]) + pad, axis=0)
        dba_ref[...] = sums[kk]
        dwv_ref[...] = jnp.concatenate(list(sums[kk + 1:2 * kk + 1]) + pad, axis=0)
        dbv_ref[...] = sums[2 * kk + 1]

    col_a = pl.BlockSpec((s, tc), lambda j: (0, j))
    col_v = pl.BlockSpec((s, tc), lambda j: (0, j + nb))
    w_a = pl.BlockSpec((8, tc), lambda j: (0, j))
    w_v = pl.BlockSpec((8, tc), lambda j: (0, j + nb))
    r_a = pl.BlockSpec((1, tc), lambda j: (0, j))
    r_v = pl.BlockSpec((1, tc), lambda j: (0, j + nb))
    outs = pl.pallas_call(
        body, name=name, grid=(nb,),
        in_specs=[col_a, col_v, w_a, w_v, r_a, r_v, col_a],
        out_specs=[col_a, col_a, w_a, w_a, r_a, r_a],
        out_shape=[jax.ShapeDtypeStruct((s, c), BF16), jax.ShapeDtypeStruct((s, c), BF16),
                   jax.ShapeDtypeStruct((8, c), F32), jax.ShapeDtypeStruct((8, c), F32),
                   jax.ShapeDtypeStruct((1, c), F32), jax.ShapeDtypeStruct((1, c), F32)],
        scratch_shapes=_conv_scratch(s, tc, 2, 2),
        compiler_params=_params(("parallel",)),
    )(up_raw, up_raw, w, w, b, b, dact)
    return outs


def _tri_masks():
    row = lax.broadcasted_iota(jnp.int32, (CHUNK, CHUNK), 0)
    col = lax.broadcasted_iota(jnp.int32, (CHUNK, CHUNK), 1)
    return row >= col, row <= col


def _ssd_fwd(xbc, dt_raw, z, dt_bias, a_log, a_log_x, d_skip_x, norm_w, expand, *, name):
    s = xbc.shape[0]
    nc = s // CHUNK

    def body(xbc_ref, dtr_ref, z_ref, dtb_ref, alog_ref, alogx_ref, dskx_ref, nw_ref, e_ref,
             y_ref, ya_ref, st_ref, state):
        @pl.when(pl.program_id(0) == 0)
        def _():
            state[...] = jnp.zeros_like(state)

        st_ref[0] = state[...]
        lower, _ = _tri_masks()
        dt = _softplus(dtr_ref[...] + dtb_ref[...])
        adt = dt * (-jnp.exp(alog_ref[...]))
        acum = _dot_exact_lhs(lower.astype(BF16), _split3(adt))
        acum_t = acum.T
        dt_terms, acum_terms = _split3(dt), _split3(acum)
        for g in range(SSD_GROUPS):
            sl = slice(GROUP_COLS * g, GROUP_COLS * (g + 1))
            dt_x = _dot_terms(dt_terms, e_ref[:, sl])
            acum_x = _dot_terms(acum_terms, e_ref[:, sl])
            tot_x = jnp.sum(dt_x * (-jnp.exp(alogx_ref[:, sl])), axis=0, keepdims=True)
            xs = xbc_ref[:, sl]
            xdt = xs * dt_x
            xdt_b = xdt.astype(BF16)
            bg = xbc_ref[:, SSD_D_INNER + SSD_STATE * g:SSD_D_INNER + SSD_STATE * (g + 1)].astype(BF16)
            cg = xbc_ref[:, SSD_D_INNER + SSD_BC + SSD_STATE * g:SSD_D_INNER + SSD_BC + SSD_STATE * (g + 1)].astype(BF16)
            cb = _dot(cg, bg, NT)
            st_g = state[:, sl]
            y_off = _dot(cg, st_g.astype(BF16)) * jnp.exp(acum_x)
            parts = []
            for r in range(SSD_HEADS_PER_GROUP):
                h = SSD_HEADS_PER_GROUP * g + r
                dec = jnp.exp(jnp.where(lower, acum[:, h:h + 1] - acum_t[h:h + 1, :], -jnp.inf))
                parts.append(_dot((cb * dec).astype(BF16), xdt_b[:, SSD_HEAD_DIM * r:SSD_HEAD_DIM * (r + 1)]))
            y_ref[:, sl] = jnp.concatenate(parts, axis=1) + y_off + dskx_ref[:, sl] * xs
            wgt = (xdt * jnp.exp(tot_x - acum_x)).astype(BF16)
            state[:, sl] = st_g * jnp.exp(tot_x) + _dot(bg, wgt, TN)
        zv = z_ref[...].astype(F32)
        q = y_ref[...] * (zv * _sigmoid(zv))
        r = lax.rsqrt(jnp.mean(q * q, axis=-1, keepdims=True) + NORM_EPS)
        ya_ref[...] = (q * r * nw_ref[...]).astype(BF16)

    def chunk(w):
        return pl.BlockSpec((CHUNK, w), lambda c: (c, 0))

    def const(shape):
        return pl.BlockSpec(shape, lambda c: (0,) * len(shape))

    return pl.pallas_call(
        body, name=name, grid=(nc,),
        in_specs=[chunk(SSD_XBC), chunk(LANES), chunk(SSD_D_INNER), const((1, LANES)), const((1, LANES)),
                  const((1, SSD_D_INNER)), const((1, SSD_D_INNER)), const((1, SSD_D_INNER)),
                  const((LANES, SSD_D_INNER))],
        out_specs=[chunk(SSD_D_INNER), chunk(SSD_D_INNER),
                   pl.BlockSpec((1, SSD_STATE, SSD_D_INNER), lambda c: (c, 0, 0))],
        out_shape=[jax.ShapeDtypeStruct((s, SSD_D_INNER), F32), jax.ShapeDtypeStruct((s, SSD_D_INNER), BF16),
                   jax.ShapeDtypeStruct((nc, SSD_STATE, SSD_D_INNER), F32)],
        scratch_shapes=[pltpu.VMEM((SSD_STATE, SSD_D_INNER), F32)],
        compiler_params=_params(("arbitrary",)),
    )(xbc, dt_raw, z, dt_bias, a_log, a_log_x, d_skip_x, norm_w, expand)


def _ssd_bwd(dya, y, z, xbc, dt_raw, states, dt_bias, a_log, a_log_x, d_skip_x, norm_w, expand, expand_t, *, name):
    s = xbc.shape[0]
    nc = s // CHUNK

    def body(dya_ref, y_ref, z_ref, xbc_ref, dtr_ref, stp_ref, dtb_ref, alog_ref, alogx_ref, dskx_ref, nw_ref,
             e_ref, et_ref, dz_ref, dxbc_ref, ddt_ref, dnw_ref, ddsk_ref, dalog_ref, ddtb_ref,
             dstate, dy_sc, dskcol):
        i = pl.program_id(0)

        @pl.when(i == 0)
        def _():
            dstate[...] = jnp.zeros_like(dstate)
            dskcol[...] = jnp.zeros_like(dskcol)
            dnw_ref[...] = jnp.zeros_like(dnw_ref)
            dalog_ref[...] = jnp.zeros_like(dalog_ref)
            ddtb_ref[...] = jnp.zeros_like(ddtb_ref)
            ddsk_ref[...] = jnp.zeros_like(ddsk_ref)

        lower, upper = _tri_masks()
        rows = lax.broadcasted_iota(jnp.int32, (CHUNK, LANES), 0)
        pre = dtr_ref[...] + dtb_ref[...]
        dt = _softplus(pre)
        a = -jnp.exp(alog_ref[...])
        acum = _dot_exact_lhs(lower.astype(BF16), _split3(dt * a))
        acum_t = acum.T
        dt_terms, acum_terms = _split3(dt), _split3(acum)

        yv = y_ref[...]
        zv = z_ref[...].astype(F32)
        sz = _sigmoid(zv)
        silu_z = zv * sz
        q = yv * silu_z
        r = lax.rsqrt(jnp.mean(q * q, axis=-1, keepdims=True) + NORM_EPS)
        qhat = q * r
        dyav = dya_ref[...]
        dqhat = dyav * nw_ref[...]
        dnw_ref[...] += jnp.sum(dyav * qhat, axis=0, keepdims=True)
        dq = r * (dqhat - qhat * jnp.mean(dqhat * qhat, axis=-1, keepdims=True))
        dy_sc[...] = dq * silu_z
        dz_ref[...] = (dq * yv * (sz * (1.0 + zv * (1.0 - sz)))).astype(BF16)

        da_cum = jnp.zeros((CHUNK, LANES), F32)
        ddt = jnp.zeros((CHUNK, LANES), F32)
        for g in range(SSD_GROUPS):
            sl = slice(GROUP_COLS * g, GROUP_COLS * (g + 1))
            et_g = et_ref[sl, :]
            dt_x = _dot_terms(dt_terms, e_ref[:, sl])
            acum_x = _dot_terms(acum_terms, e_ref[:, sl])
            tot_x = jnp.sum(dt_x * (-jnp.exp(alogx_ref[:, sl])), axis=0, keepdims=True)
            e_tot = jnp.exp(tot_x)
            dec_s = jnp.exp(tot_x - acum_x)
            xs = xbc_ref[:, sl]
            xdt = xs * dt_x
            xdt_b = xdt.astype(BF16)
            dy = dy_sc[:, sl]
            dy_b = dy.astype(BF16)
            dskx = dskx_ref[:, sl]
            y_ssd = y_ref[:, sl] - dskx * xs
            dskcol[:, sl] += jnp.sum(dy * xs, axis=0, keepdims=True)
            bg = xbc_ref[:, SSD_D_INNER + SSD_STATE * g:SSD_D_INNER + SSD_STATE * (g + 1)].astype(BF16)
            cg = xbc_ref[:, SSD_D_INNER + SSD_BC + SSD_STATE * g:SSD_D_INNER + SSD_BC + SSD_STATE * (g + 1)].astype(BF16)
            cb_t = _dot(bg, cg, NT)
            sp = stp_ref[0, :, sl]
            ds_g = dstate[:, sl]
            ds_b = ds_g.astype(BF16)
            dye_b = (dy * jnp.exp(acum_x)).astype(BF16)
            dc = _dot(dye_b, sp.astype(BF16), NT)
            dxdt_state = dec_s * _dot(bg, ds_b)
            db = _dot((xdt * dec_s).astype(BF16), ds_b, NT)
            dcb_t = jnp.zeros((CHUNK, CHUNK), F32)
            parts = []
            for rr in range(SSD_HEADS_PER_GROUP):
                h = SSD_HEADS_PER_GROUP * g + rr
                hs = slice(SSD_HEAD_DIM * rr, SSD_HEAD_DIM * (rr + 1))
                dec_t = jnp.exp(jnp.where(upper, acum_t[h:h + 1, :] - acum[:, h:h + 1], -jnp.inf))
                parts.append(_dot((cb_t * dec_t).astype(BF16), dy_b[:, hs]))
                dcb_t = dcb_t + _dot(xdt_b[:, hs], dy_b[:, hs], NT) * dec_t
            dxdt = jnp.concatenate(parts, axis=1) + dxdt_state
            dcb_tb = dcb_t.astype(BF16)
            dc = dc + _dot(dcb_tb, bg, TN)
            db = db + _dot(dcb_tb, cg)
            tot_col = jnp.sum(ds_g * sp, axis=0, keepdims=True) * e_tot + jnp.sum(dxdt_state * xdt, axis=0, keepdims=True)
            d_tot = _dot_terms(_split3(jnp.broadcast_to(tot_col, (8, GROUP_COLS))), et_g)
            d_tot = jnp.max(d_tot, axis=0, keepdims=True)
            pair_sums = dy_b.astype(F32) * y_ssd - xdt_b.astype(F32) * dxdt
            da_cum = da_cum + _dot_terms(_split3(pair_sums), et_g) + jnp.where(rows == CHUNK - 1, d_tot, 0.0)
            ddt = ddt + _dot_terms(_split3(dxdt * xs), et_g)
            dxbc_ref[:, sl] = dy * dskx + dxdt * dt_x
            dxbc_ref[:, SSD_D_INNER + SSD_STATE * g:SSD_D_INNER + SSD_STATE * (g + 1)] = db
            dxbc_ref[:, SSD_D_INNER + SSD_BC + SSD_STATE * g:SSD_D_INNER + SSD_BC + SSD_STATE * (g + 1)] = dc
            dstate[:, sl] = e_tot * ds_g + _dot(cg, dye_b, TN)

        dadt = _dot_exact_lhs(upper.astype(BF16), _split3(da_cum))
        ddt = ddt + dadt * a
        dalog_ref[...] += jnp.sum(dadt * dt, axis=0, keepdims=True)
        dpre = ddt * _sigmoid(pre)
        ddtb_ref[...] += jnp.sum(dpre, axis=0, keepdims=True)
        ddt_ref[...] = dpre.astype(BF16)

        @pl.when(i == nc - 1)
        def _():
            dalog_ref[...] = dalog_ref[...] * a
            dsk = _dot_terms(_split3(jnp.broadcast_to(dskcol[...], (8, SSD_D_INNER))), et_ref[...])
            ddsk_ref[...] = jnp.max(dsk, axis=0, keepdims=True)

    def chunk(w):
        return pl.BlockSpec((CHUNK, w), lambda i: (nc - 1 - i, 0))

    def const(shape):
        return pl.BlockSpec(shape, lambda i: (0,) * len(shape))

    return pl.pallas_call(
        body, name=name, grid=(nc,),
        in_specs=[chunk(SSD_D_INNER), chunk(SSD_D_INNER), chunk(SSD_D_INNER), chunk(SSD_XBC), chunk(LANES),
                  pl.BlockSpec((1, SSD_STATE, SSD_D_INNER), lambda i: (nc - 1 - i, 0, 0)),
                  const((1, LANES)), const((1, LANES)), const((1, SSD_D_INNER)), const((1, SSD_D_INNER)),
                  const((1, SSD_D_INNER)), const((LANES, SSD_D_INNER)), const((SSD_D_INNER, LANES))],
        out_specs=[chunk(SSD_D_INNER), chunk(SSD_XBC), chunk(LANES), const((1, SSD_D_INNER)), const((1, LANES)),
                   const((1, LANES)), const((1, LANES))],
        out_shape=[jax.ShapeDtypeStruct((s, SSD_D_INNER), BF16), jax.ShapeDtypeStruct((s, SSD_XBC), F32),
                   jax.ShapeDtypeStruct((s, LANES), BF16), jax.ShapeDtypeStruct((1, SSD_D_INNER), F32),
                   jax.ShapeDtypeStruct((1, LANES), F32), jax.ShapeDtypeStruct((1, LANES), F32),
                   jax.ShapeDtypeStruct((1, LANES), F32)],
        scratch_shapes=[pltpu.VMEM((SSD_STATE, SSD_D_INNER), F32), pltpu.VMEM((CHUNK, SSD_D_INNER), F32),
                        pltpu.VMEM((1, SSD_D_INNER), F32)],
        compiler_params=_params(("arbitrary",)),
    )(dya, y, z, xbc, dt_raw, states, dt_bias, a_log, a_log_x, d_skip_x, norm_w, expand, expand_t)


GELU_K = math.sqrt(2.0 / math.pi)
GELU_C = 0.044715


def _gelu(x):
    return 0.5 * x * (1.0 + jnp.tanh(GELU_K * (x + GELU_C * x * x * x)))


def _gelu_grad(x):
    t = jnp.tanh(GELU_K * (x + GELU_C * x * x * x))
    return 0.5 * (1.0 + t) + 0.5 * x * (1.0 - t * t) * (GELU_K * (1.0 + 3.0 * GELU_C * x * x))


def _sgu_pre(uv_ref, uvb_ref, lnw_ref, lnb_ref):
    uv = uv_ref[...].astype(F32) + uvb_ref[...]
    guv = _gelu(uv)
    u = guv[:, :SGU_WIDTH]
    v = guv[:, SGU_WIDTH:]
    mu = jnp.mean(v, axis=-1, keepdims=True)
    vc = v - mu
    rstd = lax.rsqrt(jnp.mean(vc * vc, axis=-1, keepdims=True) + LN_EPS)
    vhat = vc * rstd
    vn = vhat * lnw_ref[...] + lnb_ref[...]
    return uv, u, vhat, rstd, vn


def _sgu_fwd(uv_raw, uv_b, ln_w, ln_b, w_sp, b_sp_t, *, name):
    s = uv_raw.shape[0]
    nc = s // CHUNK

    def body(uv_ref, uvb_ref, lnw_ref, lnb_ref, w_ref, bt_ref, o_ref):
        lower, _ = _tri_masks()
        _, u, _, _, vn = _sgu_pre(uv_ref, uvb_ref, lnw_ref, lnb_ref)
        vn_b = vn.astype(BF16)
        bt = bt_ref[...]
        for g in range(SGU_GROUPS):
            gs = slice(LANES * g, LANES * (g + 1))
            wc = jnp.where(lower, w_ref[g], 0.0).astype(BF16)
            mixed = _dot(wc, vn_b[:, gs]) + bt[:, g:g + 1]
            o_ref[:, gs] = (u[:, gs] * mixed).astype(BF16)

    def const(shape):
        return pl.BlockSpec(shape, lambda c: (0,) * len(shape))

    return pl.pallas_call(
        body, name=name, grid=(nc,),
        in_specs=[pl.BlockSpec((CHUNK, 2 * SGU_WIDTH), lambda c: (c, 0)), const((1, 2 * SGU_WIDTH)),
                  const((1, SGU_WIDTH)), const((1, SGU_WIDTH)), const((SGU_GROUPS, CHUNK, CHUNK)),
                  const((CHUNK, LANES))],
        out_specs=pl.BlockSpec((CHUNK, SGU_WIDTH), lambda c: (c, 0)),
        out_shape=jax.ShapeDtypeStruct((s, SGU_WIDTH), BF16),
        compiler_params=_params(("parallel",)),
    )(uv_raw, uv_b, ln_w, ln_b, w_sp, b_sp_t)


def _sgu_bwd(uv_raw, dyb, uv_b, ln_w, ln_b, w_sp, b_sp_t, group_sum, *, name):
    s = uv_raw.shape[0]
    nc = s // CHUNK

    def body(uv_ref, dy_ref, uvb_ref, lnw_ref, lnb_ref, w_ref, bt_ref, gsum_ref,
             duv_ref, dw_ref, dbt_ref, dlnw_ref, dlnb_ref, duvb_ref):
        @pl.when(pl.program_id(0) == 0)
        def _():
            dw_ref[...] = jnp.zeros_like(dw_ref)
            dbt_ref[...] = jnp.zeros_like(dbt_ref)
            dlnw_ref[...] = jnp.zeros_like(dlnw_ref)
            dlnb_ref[...] = jnp.zeros_like(dlnb_ref)
            duvb_ref[...] = jnp.zeros_like(duvb_ref)

        lower, _ = _tri_masks()
        uv, u, vhat, rstd, vn = _sgu_pre(uv_ref, uvb_ref, lnw_ref, lnb_ref)
        vn_b = vn.astype(BF16)
        bt = bt_ref[...]
        dy = dy_ref[...]
        du_parts, dvn_parts, dmix_parts = [], [], []
        for g in range(SGU_GROUPS):
            gs = slice(LANES * g, LANES * (g + 1))
            wc = jnp.where(lower, w_ref[g], 0.0).astype(BF16)
            mixed = _dot(wc, vn_b[:, gs]) + bt[:, g:g + 1]
            du_parts.append(dy[:, gs] * mixed)
            dmix = dy[:, gs] * u[:, gs]
            dmix_b = dmix.astype(BF16)
            dmix_parts.append(dmix)
            dw_ref[g] += jnp.where(lower, _dot(dmix_b, vn_b[:, gs], NT), 0.0)
            dvn_parts.append(_dot(wc, dmix_b, TN))
        dmixed = jnp.concatenate(dmix_parts, axis=1)
        dbt_ref[...] += _dot_terms(_split3(dmixed), gsum_ref[...])
        dvn = jnp.concatenate(dvn_parts, axis=1)
        dlnw_ref[...] += jnp.sum(dvn * vhat, axis=0, keepdims=True)
        dlnb_ref[...] += jnp.sum(dvn, axis=0, keepdims=True)
        dvhat = dvn * lnw_ref[...]
        dv = rstd * (dvhat - jnp.mean(dvhat, axis=-1, keepdims=True)
                     - vhat * jnp.mean(dvhat * vhat, axis=-1, keepdims=True))
        dguv = jnp.concatenate(du_parts + [dv], axis=1)
        duv = dguv * _gelu_grad(uv)
        duvb_ref[...] += jnp.sum(duv, axis=0, keepdims=True)
        duv_ref[...] = duv.astype(BF16)

    def const(shape):
        return pl.BlockSpec(shape, lambda c: (0,) * len(shape))

    return pl.pallas_call(
        body, name=name, grid=(nc,),
        in_specs=[pl.BlockSpec((CHUNK, 2 * SGU_WIDTH), lambda c: (c, 0)),
                  pl.BlockSpec((CHUNK, SGU_WIDTH), lambda c: (c, 0)), const((1, 2 * SGU_WIDTH)),
                  const((1, SGU_WIDTH)), const((1, SGU_WIDTH)), const((SGU_GROUPS, CHUNK, CHUNK)),
                  const((CHUNK, LANES)), const((SGU_WIDTH, LANES))],
        out_specs=[pl.BlockSpec((CHUNK, 2 * SGU_WIDTH), lambda c: (c, 0)), const((SGU_GROUPS, CHUNK, CHUNK)),
                   const((CHUNK, LANES)), const((1, SGU_WIDTH)), const((1, SGU_WIDTH)), const((1, 2 * SGU_WIDTH))],
        out_shape=[jax.ShapeDtypeStruct((s, 2 * SGU_WIDTH), BF16),
                   jax.ShapeDtypeStruct((SGU_GROUPS, CHUNK, CHUNK), F32), jax.ShapeDtypeStruct((CHUNK, LANES), F32),
                   jax.ShapeDtypeStruct((1, SGU_WIDTH), F32), jax.ShapeDtypeStruct((1, SGU_WIDTH), F32),
                   jax.ShapeDtypeStruct((1, 2 * SGU_WIDTH), F32)],
        compiler_params=_params(("arbitrary",)),
    )(uv_raw, dyb, uv_b, ln_w, ln_b, w_sp, b_sp_t, group_sum)


def _gate_fwd(gates_raw, b_gate, p_a, p_b, *, name, tm=512):
    s = p_a.shape[0]
    tm = min(tm, s)

    def body(ga_ref, gb_ref, ba_ref, bb_ref, pa_ref, pb_ref, o_ref):
        ga = _sigmoid(ga_ref[...].astype(F32) + ba_ref[...])
        gb = _sigmoid(gb_ref[...].astype(F32) + bb_ref[...])
        o_ref[...] = (ga * pa_ref[...] + gb * pb_ref[...]).astype(BF16)

    t_a = pl.BlockSpec((tm, D_MODEL), lambda i: (i, 0))
    t_b = pl.BlockSpec((tm, D_MODEL), lambda i: (i, 1))
    r_a = pl.BlockSpec((1, D_MODEL), lambda i: (0, 0))
    r_b = pl.BlockSpec((1, D_MODEL), lambda i: (0, 1))
    return pl.pallas_call(
        body, name=name, grid=(s // tm,),
        in_specs=[t_a, t_b, r_a, r_b, t_a, t_a], out_specs=t_a,
        out_shape=jax.ShapeDtypeStruct((s, D_MODEL), BF16),
        compiler_params=_params(("parallel",)),
    )(gates_raw, gates_raw, b_gate, b_gate, p_a, p_b)


def _gate_bwd(gates_raw, b_gate, p_a, p_b, dm, *, name, tm=512):
    s = p_a.shape[0]
    tm = min(tm, s)

    def body(ga_ref, gb_ref, ba_ref, bb_ref, pa_ref, pb_ref, dm_ref, dpa_ref, dpb_ref, dga_ref, dgb_ref,
             dba_ref, dbb_ref):
        @pl.when(pl.program_id(0) == 0)
        def _():
            dba_ref[...] = jnp.zeros_like(dba_ref)
            dbb_ref[...] = jnp.zeros_like(dbb_ref)

        d = dm_ref[...]
        for g_ref, b_ref, p_ref, dp_ref, dg_ref, db_ref in ((ga_ref, ba_ref, pa_ref, dpa_ref, dga_ref, dba_ref),
                                                            (gb_ref, bb_ref, pb_ref, dpb_ref, dgb_ref, dbb_ref)):
            sg = _sigmoid(g_ref[...].astype(F32) + b_ref[...])
            dp_ref[...] = (d * sg).astype(BF16)
            dg = d * p_ref[...] * (sg * (1.0 - sg))
            dg_ref[...] = dg.astype(BF16)
            db_ref[...] += jnp.sum(dg, axis=0, keepdims=True)

    t_a = pl.BlockSpec((tm, D_MODEL), lambda i: (i, 0))
    t_b = pl.BlockSpec((tm, D_MODEL), lambda i: (i, 1))
    r_a = pl.BlockSpec((1, D_MODEL), lambda i: (0, 0))
    r_b = pl.BlockSpec((1, D_MODEL), lambda i: (0, 1))
    big = jax.ShapeDtypeStruct((s, D_MODEL), BF16)
    row = jax.ShapeDtypeStruct((1, D_MODEL), F32)
    return pl.pallas_call(
        body, name=name, grid=(s // tm,),
        in_specs=[t_a, t_b, r_a, r_b, t_a, t_a, t_a], out_specs=[t_a, t_a, t_a, t_a, r_a, r_a],
        out_shape=[big, big, big, big, row, row],
        compiler_params=_params(("arbitrary",)),
    )(gates_raw, gates_raw, b_gate, b_gate, p_a, p_b, dm)


def _adamw_update(w_ref, g_ref, m_ref, v_ref, d_ref, mo_ref, vo_ref):
    gv = g_ref[...]
    mn = ADAM_B1 * m_ref[...] + (1.0 - ADAM_B1) * gv
    vn = ADAM_B2 * v_ref[...] + (1.0 - ADAM_B2) * (gv * gv)
    m_hat = mn / (1.0 - ADAM_B1 ** ADAM_STEP)
    v_hat = vn / (1.0 - ADAM_B2 ** ADAM_STEP)
    d_ref[...] = -ADAM_LR * (m_hat / (jnp.sqrt(v_hat) + ADAM_EPS) + ADAM_WD * w_ref[...])
    mo_ref[...] = mn
    vo_ref[...] = vn


def _adamw_many(ws, gs, ms, vs, *, name):
    n = len(ws)

    def body(*refs):
        for i in range(n):
            _adamw_update(*[refs[k * n + i] for k in range(7)])

    whole = pl.BlockSpec(memory_space=pltpu.VMEM)
    sds = [jax.ShapeDtypeStruct(w.shape, F32) for w in ws]
    outs = pl.pallas_call(
        body, name=name, in_specs=[whole] * (4 * n), out_specs=[whole] * (3 * n), out_shape=sds * 3,
        compiler_params=pltpu.CompilerParams(vmem_limit_bytes=VMEM_LIMIT),
    )(*ws, *gs, *ms, *vs)
    return outs[:n], outs[n:2 * n], outs[2 * n:]


def _adamw(w, g, m, v, *, name, tr=128):
    r, c = w.shape
    tr = min(tr, r)
    assert r % tr == 0, (name, r, tr)
    body = functools.partial(_adamw_update)

    blk = pl.BlockSpec((tr, c), lambda i: (i, 0))
    sds = jax.ShapeDtypeStruct((r, c), F32)
    return pl.pallas_call(
        body, name=name, grid=(r // tr,), in_specs=[blk] * 4, out_specs=[blk] * 3, out_shape=[sds] * 3,
        compiler_params=_params(("parallel",)),
    )(w, g, m, v)


def _adamw_two_sums(w, g_a, g_b, m, v, *, name, tr=128):
    r, c = w.shape
    tr = min(tr, r)
    assert r % tr == 0, (name, r, tr)

    def body(w_ref, ga_ref, gb_ref, m_ref, v_ref, g_ref, d_ref, mo_ref, vo_ref):
        g_ref[...] = ga_ref[...] + gb_ref[...]
        _adamw_update(w_ref, g_ref, m_ref, v_ref, d_ref, mo_ref, vo_ref)

    blk = pl.BlockSpec((tr, c), lambda i: (i, 0))
    sds = jax.ShapeDtypeStruct((r, c), F32)
    return pl.pallas_call(
        body, name=name, grid=(r // tr,), in_specs=[blk] * 5, out_specs=[blk] * 4, out_shape=[sds] * 4,
        compiler_params=_params(("parallel",)),
    )(w, g_a, g_b, m, v)


def _tile(n, pref):
    if n <= pref:
        return n
    best = LANES
    for t in range(LANES, pref + 1, LANES):
        if n % t == 0:
            best = t
    return best


MATMUL_BLOCK_BYTES = 20 * 1024 * 1024


def _mm(pairs, name, **kw):
    trans_b = kw.get("trans_b", False)
    m = (pairs[0][0][0] if isinstance(pairs[0][0], tuple) else pairs[0][0]).shape[0]
    ktot, n = 0, None
    for _, b in pairs:
        shape = b[0].shape[1:] if isinstance(b, tuple) else b.shape
        ktot += shape[1] if trans_b else shape[0]
        n = shape[0] if trans_b else shape[1]
    out_bytes = 4 * (2 if kw.get("add") is not None else 1)
    best = None
    for tm in (256, 512, 1024, 2048):
        for tn in range(LANES, min(n, 1536) + 1, LANES):
            if m % min(tm, m) or n % tn:
                continue
            fits = 2 * ktot * (min(tm, m) + tn) + out_bytes * min(tm, m) * tn <= MATMUL_BLOCK_BYTES
            if fits and (best is None or min(tm, m) * tn >= best[0] * best[1]):
                best = (min(tm, m), tn)
    return _matmul(pairs, tm=best[0], tn=best[1], name=name, **kw)


def _wgrad(a, b, name, **kw):
    return _matmul_tn(a, b, tk=_tile(a.shape[1], 1408), tn=kw.pop("tn", _tile(b.shape[1], 1024)), tm=2048,
                      name=name, **kw)


def _local_step(x, target, get_weight, small, emit_grad):
    heads = jnp.arange(SSD_D_INNER) // SSD_HEAD_DIM
    expand = (jnp.arange(LANES)[:, None] == heads[None, :]).astype(BF16)
    expand_t = expand.T
    group_sum = (jnp.arange(SGU_WIDTH)[:, None] // LANES == jnp.arange(LANES)[None, :]).astype(BF16)
    pad_h = LANES - SSD_HEADS
    dt_bias = jnp.pad(small["dt_bias"], ((0, 0), (0, pad_h)))
    a_log = jnp.pad(small["a_log"], ((0, 0), (0, pad_h)))
    a_log_x = jnp.repeat(small["a_log"], SSD_HEAD_DIM, axis=1)
    d_skip_x = jnp.repeat(small["d_skip"], SSD_HEAD_DIM, axis=1)
    b_sp_t = jnp.pad(small["b_spatial"][0].T, ((0, 0), (0, LANES - SGU_GROUPS)))
    w_sp = small["w_spatial"][0]
    conv_a_w = jnp.pad(small["conv_a_w"], ((0, 4), (0, 0)))
    conv_f_w = jnp.pad(small["conv_f_w"], ((0, 5), (0, 0)))
    final_w = small["final_norm_w"].reshape(1, D_MODEL)

    n1 = _rms_fwd(x, small["norm1_w"], after=small.get("gathers_started"), name="rms1_fwd")
    wts = dict(get_weight("w_in", n1))
    z = _mm([(n1, wts["in_z"])], "in_z")
    xbc_raw = _mm([(n1, wts["in_xbc"])], "in_xbc")
    dt_raw = _mm([(n1, wts["in_dt"])], "in_dt")
    uv_raw = _mm([(n1, wts["in_uv"])], "in_uv", out_dtype=BF16)
    gates_raw = _mm([(n1, wts["in_gate"])], "in_gate", out_dtype=BF16)
    xbc = _conv_a_fwd(xbc_raw, conv_a_w, small["conv_a_b"], name="conv_a_fwd")
    y, y_a, states = _ssd_fwd(xbc, dt_raw, z, dt_bias, a_log, a_log_x, d_skip_x, small["ssd_norm_w"], expand,
                              name="ssd_fwd")
    y_b = _sgu_fwd(uv_raw, small["uv_b"], small["v_ln_w"], small["v_ln_b"], w_sp, b_sp_t, name="sgu_fwd")
    wts.update(get_weight("w_branch", y_b))
    p_a = _mm([(y_a, wts["branch_a"])], "branch_a")
    p_b = _mm([(y_b, wts["branch_b"])], "branch_b")
    mix = _gate_fwd(gates_raw, small["b_gate"], p_a, p_b, name="gate_fwd")
    wts.update(get_weight("w_out", mix))
    h1 = _mm([(mix, wts["out"])], "out_proj", add=x)
    n2 = _rms_fwd(h1, small["norm2_w"], name="rms2_fwd")
    wts.update(get_weight("w_up", n2))
    up_w = wts["up"]
    up_cols = up_w.shape[2]
    up_raw = _matmul([(n2, (up_w, "cols"))], tm=2048, tn=up_cols, out_dtype=BF16, name="up_proj")
    act = _conv_f_fwd(up_raw, conv_f_w, small["conv_f_b"], name="conv_f_fwd")
    wts.update(get_weight("w_down", act))
    h2 = _mm([(act, wts["down"])], "down_proj", add=h1)
    loss, dh2, dh2_b, d_final = _final_fwd_bwd(h2, final_w, target, name="final_norm_loss")

    dact = _mm([(dh2_b, wts["down"])], "down_dgrad", trans_b=True)
    started = emit_grad("w_down", _wgrad(act, dh2_b, "down_wgrad"))
    dup_a, dup_v, dwf_a, dwf_v, dbf_a, dbf_v = _conv_f_bwd(up_raw, conv_f_w, small["conv_f_b"], dact,
                                                           name="conv_f_bwd")
    dn2 = _mm([((dup_a, 0), (up_w, 0)), ((dup_a, 1), (up_w, 1)), ((dup_v, 0), (up_w, 2)), ((dup_v, 1), (up_w, 3))],
              "up_dgrad", trans_b=True, after=started)
    started = emit_grad("w_up", jnp.concatenate([_wgrad(n2, dup_a, "up_wgrad_a", tn=up_cols, stack_out=True),
                                                 _wgrad(n2, dup_v, "up_wgrad_v", tn=up_cols, stack_out=True)], axis=0))
    dh1, dh1_b, d_norm2 = _rms_bwd(h1, small["norm2_w"], dn2, dh2, name="rms2_bwd")
    dmix = _mm([(dh1_b, wts["out"])], "out_dgrad", trans_b=True, after=started)
    started = emit_grad("w_out", _wgrad(mix, dh1_b, "out_wgrad"))
    dp_a, dp_b, dg_a, dg_b, dbg_a, dbg_b = _gate_bwd(gates_raw, small["b_gate"], p_a, p_b, dmix, name="gate_bwd")
    dya = _mm([(dp_a, wts["branch_a"])], "branch_a_dgrad", trans_b=True, after=started)
    dyb = _mm([(dp_b, wts["branch_b"])], "branch_b_dgrad", trans_b=True)
    started_branch = emit_grad("w_branch", jnp.concatenate([_wgrad(y_a, dp_a, "branch_a_wgrad"),
                                                            _wgrad(y_b, dp_b, "branch_b_wgrad")], axis=0))
    duv, d_wsp, d_bsp_t, d_lnw, d_lnb, d_uvb = _sgu_bwd(uv_raw, dyb, small["uv_b"], small["v_ln_w"],
                                                        small["v_ln_b"], w_sp, b_sp_t, group_sum, name="sgu_bwd")
    dz, dxbc, ddt, d_ssd_nw, d_dskip, d_alog, d_dtb = _ssd_bwd(
        dya, y, z, xbc, dt_raw, states, dt_bias, a_log, a_log_x, d_skip_x, small["ssd_norm_w"], expand, expand_t,
        name="ssd_bwd")
    dxbc_raw, d_conv_a_w, d_conv_a_b = _conv_a_bwd(xbc_raw, conv_a_w, small["conv_a_b"], dxbc, name="conv_a_bwd")
    started = emit_grad("w_in", {
        "in_z": _wgrad(n1, dz, "in_z_wgrad", after=started_branch), "in_xbc": _wgrad(n1, dxbc_raw, "in_xbc_wgrad"),
        "in_dt": _wgrad(n1, ddt, "in_dt_wgrad")[:, :SSD_HEADS], "in_uv": _wgrad(n1, duv, "in_uv_wgrad"),
        "in_gate_a": _wgrad(n1, dg_a, "in_gate_a_wgrad"), "in_gate_b": _wgrad(n1, dg_b, "in_gate_b_wgrad")})
    dn1 = _mm([(dz, wts["in_z"]), (dxbc_raw, wts["in_xbc"]), (ddt, wts["in_dt"]), (duv, wts["in_uv"]),
               (dg_a, wts["in_gate_a"]), (dg_b, wts["in_gate_b"])], "in_dgrad", trans_b=True, after=started)
    dx, _, d_norm1 = _rms_bwd(x, small["norm1_w"], dn1, dh1, name="rms1_bwd")

    grads_small = {
        "norm1_w": d_norm1, "b_gate": jnp.concatenate([dbg_a, dbg_b], axis=1),
        "conv_a_w": d_conv_a_w[:4], "conv_a_b": d_conv_a_b,
        "dt_bias": d_dtb[:, :SSD_HEADS], "a_log": d_alog[:, :SSD_HEADS], "d_skip": d_dskip[:, :SSD_HEADS],
        "ssd_norm_w": d_ssd_nw, "uv_b": d_uvb, "v_ln_w": d_lnw, "v_ln_b": d_lnb,
        "w_spatial": d_wsp[None], "b_spatial": d_bsp_t[:, :SGU_GROUPS].T[None],
        "norm2_w": d_norm2, "conv_f_w": jnp.concatenate([dwf_a[:3], dwf_v[:3]], axis=1),
        "conv_f_b": jnp.concatenate([dbf_a, dbf_v], axis=1), "final_norm_w": d_final.reshape(D_MODEL),
    }
    return loss, dx, grads_small


HBM = pl.BlockSpec(memory_space=pl.ANY)
MESH = pl.DeviceIdType.MESH


def _mesh_pos():
    return lax.axis_index("x"), lax.axis_index("y"), lax.axis_index("c")


def _other_chips(x, y):
    return [(1 - x, y), (x, 1 - y), (1 - x, 1 - y)]


def _remote(src, dst, send_sems, recv_sems, k, dev):
    return pltpu.make_async_remote_copy(src_ref=src, dst_ref=dst, send_sem=send_sems.at[k], recv_sem=recv_sems.at[k],
                                        device_id=dev, device_id_type=MESH)


def _dma_sems(n):
    return [pltpu.SemaphoreType.DMA((n,)), pltpu.SemaphoreType.DMA((n,))]


HBM_ONLY = pl.BlockSpec(memory_space=pltpu.HBM)
SEMAPHORES = pl.BlockSpec(memory_space=pltpu.SEMAPHORE)
DATAFLOW_EFFECT = pltpu.SideEffectType.DATAFLOW_SIDE_EFFECTING
N_PEER_CHIPS = N_CHIPS - 1


def _gather_sends(w_ref, land_ref, send_sems, recv_sems):
    x, y, c = _mesh_pos()
    return [_remote(w_ref.at[c], land_ref.at[2 * x + y, c], send_sems, recv_sems, k, (px, py, c))
            for k, (px, py) in enumerate(_other_chips(x, y))]


def _gather_arrivals(w_ref, land_ref, send_sems, recv_sems):
    x, y, c = _mesh_pos()
    return [_remote(w_ref.at[c], land_ref.at[2 * px + py, c], send_sems, recv_sems, k, (px, py, c))
            for k, (px, py) in enumerate(_other_chips(x, y))]


def _gather_whole_sends(w_ref, land_ref, send_sems, recv_sems):
    x, y, c = _mesh_pos()
    return [_remote(w_ref, land_ref.at[2 * x + y], send_sems, recv_sems, k, (px, py, c))
            for k, (px, py) in enumerate(_other_chips(x, y))]


def _gather_whole_arrivals(w_ref, land_ref, send_sems, recv_sems):
    x, y, c = _mesh_pos()
    return [_remote(w_ref, land_ref.at[2 * px + py], send_sems, recv_sems, k, (px, py, c))
            for k, (px, py) in enumerate(_other_chips(x, y))]


def _scatter_sends(h_ref, land_ref, send_sems, recv_sems):
    x, y, c = _mesh_pos()
    return [_remote(h_ref.at[2 * px + py], land_ref.at[2 * x + y], send_sems, recv_sems, k, (px, py, c))
            for k, (px, py) in enumerate(_other_chips(x, y))]


def _scatter_arrivals(h_ref, land_ref, send_sems, recv_sems):
    x, y, c = _mesh_pos()
    return [_remote(h_ref.at[2 * x + y], land_ref.at[2 * px + py], send_sems, recv_sems, k, (px, py, c))
            for k, (px, py) in enumerate(_other_chips(x, y))]


def _exchange_start(sources, landing_shapes, sends, *, after=None, name):
    n = len(sources)
    extra = [] if after is None else [after]

    def body(*refs):
        sems = refs[2 * n + len(extra):4 * n + len(extra)]
        for i in range(n):
            send_i = sends[i] if isinstance(sends, (list, tuple)) else sends
            for cp in send_i(refs[i], refs[n + i], sems[2 * i], sems[2 * i + 1]):
                cp.start()
        refs[-1][...] = jnp.zeros_like(refs[-1])

    hbm = [pltpu.HBM(s.shape, s.dtype) for s in sources] + [pltpu.HBM(shp, s.dtype)
                                                             for shp, s in zip(landing_shapes, sources)]
    outs = pl.pallas_call(
        body, name=name,
        out_shape=tuple([pltpu.SemaphoreType.DMA((N_PEER_CHIPS,))] * (2 * n) + hbm
                        + [jax.ShapeDtypeStruct((8, LANES), F32)]),
        in_specs=[HBM_ONLY] * (2 * n) + [pl.BlockSpec(memory_space=pl.ANY)] * len(extra),
        out_specs=tuple([SEMAPHORES] * (2 * n) + [HBM_ONLY] * (2 * n) + [pl.BlockSpec(memory_space=pltpu.VMEM)]),
        input_output_aliases={i: 2 * n + i for i in range(2 * n)},
        compiler_params=pltpu.CompilerParams(has_side_effects=DATAFLOW_EFFECT),
    )(*[pltpu.with_memory_space_constraint(s, pltpu.HBM) for s in sources],
      *[pltpu.with_memory_space_constraint(lax.empty(shp, s.dtype), pltpu.HBM)
        for shp, s in zip(landing_shapes, sources)], *extra)
    pending = [(outs[2 * i], outs[2 * i + 1], outs[2 * n + i], outs[3 * n + i]) for i in range(n)]
    return pending, outs[-1]


def _exchange_wait(pending, after, sends, arrivals, *, name):
    send_sems, recv_sems, source, landing = pending

    def body(src_ref, land_ref, send_ref, recv_ref, after_ref, src_out, land_out):
        for cp in sends(src_ref, land_ref, send_ref, recv_ref):
            cp.wait_send()
        for cp in arrivals(src_ref, land_ref, send_ref, recv_ref):
            cp.wait_recv()

    return pl.pallas_call(
        body, name=name,
        out_shape=(pltpu.HBM(source.shape, source.dtype), pltpu.HBM(landing.shape, landing.dtype)),
        in_specs=[HBM_ONLY, HBM_ONLY, SEMAPHORES, SEMAPHORES, pl.BlockSpec(memory_space=pl.ANY)],
        out_specs=(HBM_ONLY, HBM_ONLY), input_output_aliases={0: 0, 1: 1},
        compiler_params=pltpu.CompilerParams(has_side_effects=DATAFLOW_EFFECT),
    )(source, landing, send_sems, recv_sems, after)


def _gather_ici(shard, *, name):
    _, rh, cols = shard.shape

    def body(w_ref, o_ref, send_sems, recv_sems):
        x, y, c = _mesh_pos()
        mine = 2 * x + y
        sends = []
        for k, (px, py) in enumerate(_other_chips(x, y)):
            cp = _remote(w_ref.at[c], o_ref.at[mine, c], send_sems, recv_sems, k, (px, py, c))
            cp.start()
            sends.append(cp)
        for k, (px, py) in enumerate(_other_chips(x, y)):
            _remote(w_ref.at[c], o_ref.at[2 * px + py, c], send_sems, recv_sems, k, (px, py, c)).wait_recv()
        for cp in sends:
            cp.wait_send()

    return pl.pallas_call(
        body, name=name, in_specs=[HBM], out_specs=HBM,
        out_shape=jax.ShapeDtypeStruct((N_CHIPS, 2, rh, cols), shard.dtype), scratch_shapes=_dma_sems(3),
    )(shard)


def _gather_d2d(parts, *, name):
    def body(a_ref, o_ref, send_sems, recv_sems):
        x, y, c = _mesh_pos()
        sibling = (x, y, 1 - c)
        sends = []
        for k, (px, py) in enumerate(_other_chips(x, y)):
            cp = _remote(a_ref.at[2 * px + py, c], o_ref.at[2 * px + py, c], send_sems, recv_sems, k, sibling)
            cp.start()
            sends.append(cp)
        for k, (px, py) in enumerate(_other_chips(x, y)):
            _remote(a_ref.at[2 * px + py, c], o_ref.at[2 * px + py, 1 - c], send_sems, recv_sems, k, sibling).wait_recv()
        for cp in sends:
            cp.wait_send()

    return pl.pallas_call(
        body, name=name, in_specs=[HBM], out_specs=HBM,
        out_shape=jax.ShapeDtypeStruct(parts.shape, parts.dtype),
        input_output_aliases={0: 0}, scratch_shapes=_dma_sems(3),
    )(parts)


def _all_gather_chips(shard_flat, name):
    rows, cols = shard_flat.shape
    parts = _gather_ici(shard_flat.reshape(2, rows // 2, cols), name=name + "_ici")
    others = _gather_d2d(parts, name=name + "_d2d").reshape(N_CHIPS, rows, cols)
    chip = 2 * lax.axis_index("x") + lax.axis_index("y")
    return lax.dynamic_update_slice(others, shard_flat[None], (chip, 0, 0))


def _row_tile(rows, mult, cap):
    best = mult
    for t in range(mult, min(rows, cap) + 1, mult):
        if rows % t == 0:
            best = t
    assert rows % best == 0, (rows, mult)
    return best


def _swap_halves_d2d(g, *, name):
    _, _, rh, cols = g.shape

    def body(g_ref, o_ref, send_sems, recv_sems):
        x, y, c = _mesh_pos()
        sibling = (x, y, 1 - c)
        sends = []
        for s in range(N_CHIPS):
            cp = _remote(g_ref.at[s, 1 - c], o_ref.at[s], send_sems, recv_sems, s, sibling)
            cp.start()
            sends.append(cp)
        for s in range(N_CHIPS):
            _remote(g_ref.at[s, c], o_ref.at[s], send_sems, recv_sems, s, sibling).wait_recv()
        for cp in sends:
            cp.wait_send()

    return pl.pallas_call(
        body, name=name, in_specs=[HBM], out_specs=HBM,
        out_shape=jax.ShapeDtypeStruct((N_CHIPS, rh, cols), g.dtype), scratch_shapes=_dma_sems(N_CHIPS),
    )(g)


def _add_own_half(g, arrived, core, *, name):
    _, _, rh, cols = g.shape
    mult = 16 if g.dtype == BF16 else 8
    tr = _row_tile(rh, mult, max(mult, (512 * 1024) // cols))

    def body(core_ref, g_ref, a_ref, o_ref):
        o_ref[...] = (g_ref[0].astype(F32) + a_ref[...].astype(F32)).astype(o_ref.dtype)

    grid_spec = pltpu.PrefetchScalarGridSpec(
        num_scalar_prefetch=1, grid=(N_CHIPS, rh // tr),
        in_specs=[pl.BlockSpec((1, 1, tr, cols), lambda s, i, core_ref: (s, core_ref[0], i, 0)),
                  pl.BlockSpec((1, tr, cols), lambda s, i, core_ref: (s, i, 0))],
        out_specs=pl.BlockSpec((1, tr, cols), lambda s, i, core_ref: (s, i, 0)))
    return pl.pallas_call(
        body, name=name, grid_spec=grid_spec, out_shape=jax.ShapeDtypeStruct((N_CHIPS, rh, cols), g.dtype),
        compiler_params=_params(("parallel", "parallel")),
    )(core, g, arrived)


def _scatter_ici(h, *, name):
    def body(h_ref, o_ref, send_sems, recv_sems):
        x, y, c = _mesh_pos()
        mine = 2 * x + y
        sends = []
        for k, (px, py) in enumerate(_other_chips(x, y)):
            cp = _remote(h_ref.at[2 * px + py], o_ref.at[mine], send_sems, recv_sems, k, (px, py, c))
            cp.start()
            sends.append(cp)
        for k, (px, py) in enumerate(_other_chips(x, y)):
            _remote(h_ref.at[mine], o_ref.at[2 * px + py], send_sems, recv_sems, k, (px, py, c)).wait_recv()
        for cp in sends:
            cp.wait_send()

    others = pl.pallas_call(
        body, name=name, in_specs=[HBM], out_specs=HBM, out_shape=jax.ShapeDtypeStruct(h.shape, h.dtype),
        scratch_shapes=_dma_sems(3),
    )(h)
    chip = 2 * lax.axis_index("x") + lax.axis_index("y")
    own = lax.dynamic_slice_in_dim(h, chip, 1, axis=0)
    return lax.dynamic_update_slice(others, own, (chip, 0, 0))


def _sum_chips(parts, *, name):
    _, rh, cols = parts.shape
    mult = 16 if parts.dtype == BF16 else 8
    tr = _row_tile(rh, mult, max(mult, (512 * 1024) // cols))

    def body(p_ref, o_ref):
        acc = p_ref[0].astype(F32)
        for s in range(1, N_CHIPS):
            acc = acc + p_ref[s].astype(F32)
        o_ref[...] = acc

    return pl.pallas_call(
        body, name=name, grid=(rh // tr,),
        in_specs=[pl.BlockSpec((N_CHIPS, tr, cols), lambda i: (0, i, 0))],
        out_specs=pl.BlockSpec((tr, cols), lambda i: (i, 0)),
        out_shape=jax.ShapeDtypeStruct((rh, cols), F32), compiler_params=_params(("parallel",)),
    )(parts)


def _share_d2d(f, *, name):
    fs = f if isinstance(f, (list, tuple)) else [f]
    others = _swap_with_sibling(fs, name=name)
    first = lax.axis_index("c") == 0
    both = [jnp.stack([jnp.where(first, a, b), jnp.where(first, b, a)]) for a, b in zip(fs, others)]
    return both if isinstance(f, (list, tuple)) else both[0]


def _swap_with_sibling(fs, *, name):
    n = len(fs)

    def body(*refs):
        x, y, c = _mesh_pos()
        sibling = (x, y, 1 - c)
        send_sems, recv_sems = refs[2 * n:]
        copies = [_remote(refs[i], refs[n + i], send_sems, recv_sems, i, sibling) for i in range(n)]
        for cp in copies:
            cp.start()
        for cp in copies:
            cp.wait()

    return pl.pallas_call(
        body, name=name, in_specs=[HBM] * n, out_specs=[HBM] * n,
        out_shape=[jax.ShapeDtypeStruct(a.shape, a.dtype) for a in fs], scratch_shapes=_dma_sems(n),
    )(*fs)


def _reduce_scatter_chips(g, core, name):
    _, rows, cols = g.shape
    g = g.reshape(N_CHIPS, 2, rows // 2, cols)
    arrived = _swap_halves_d2d(g, name=name + "_swap")
    chip_sum = _add_own_half(g, arrived, core, name=name + "_add2")
    parts = _scatter_ici(chip_sum, name=name + "_ici")
    total = _sum_chips(parts, name=name + "_sum4")
    return _share_d2d(total, name=name + "_share").reshape(rows, cols)


BIG = ("w_in", "w_branch", "w_out", "w_up", "w_down")
BIG_COLUMN_SHARDED = ("w_in", "w_up")
CONV = ("conv_a_w", "conv_f_w")
REPLICATED = ("norm1_w", "b_gate", "conv_a_b", "dt_bias", "a_log", "d_skip", "ssd_norm_w", "uv_b", "v_ln_w",
              "v_ln_b", "w_spatial", "b_spatial", "norm2_w", "conv_f_b", "final_norm_w")
WEIGHT_ORDER = ("norm1_w", "w_in", "b_gate", "conv_a_w", "conv_a_b", "dt_bias", "a_log", "d_skip", "ssd_norm_w",
                "uv_b", "v_ln_w", "v_ln_b", "w_spatial", "b_spatial", "w_branch", "w_out", "norm2_w", "w_up",
                "conv_f_w", "conv_f_b", "w_down", "final_norm_w")
SMALL_EXCHANGE_ROWS = 64


_GATE0 = SSD_IN + 2 * SGU_WIDTH
IN_SEGMENTS = {
    "in_z": (0, SSD_D_INNER), "in_xbc": (SSD_D_INNER, SSD_D_INNER + SSD_XBC), "in_dt": (SSD_D_INNER + SSD_XBC, SSD_IN),
    "in_uv": (SSD_IN, _GATE0), "in_gate": (_GATE0, IN_COLS), "in_gate_a": (_GATE0, _GATE0 + D_MODEL),
    "in_gate_b": (_GATE0 + D_MODEL, IN_COLS),
}
IN_GRAD_SEGMENTS = ("in_z", "in_xbc", "in_dt", "in_uv", "in_gate_a", "in_gate_b")


def _take_columns(parts, start, stop):
    out = []
    for a, first in parts:
        lo, hi = max(start, first), min(stop, first + a.shape[1])
        if lo < hi:
            out.append(a[:, lo - first:hi - first])
    return out[0] if len(out) == 1 else jnp.concatenate(out, axis=1)


def _flat_rows(arrays, row_multiple):
    flat = jnp.concatenate([a.reshape(-1) for a in arrays])
    rows = -(-flat.shape[0] // (LANES * row_multiple)) * row_multiple
    return jnp.pad(flat, (0, rows * LANES - flat.shape[0])).reshape(rows, LANES)


def _unflatten(flat, shapes):
    flat = flat.reshape(-1)
    out, off = [], 0
    for shp in shapes:
        n = math.prod(shp)
        out.append(flat[off:off + n].reshape(shp))
        off += n
    return out


def _from_chip_blocks(blocks, name):
    if name in BIG_COLUMN_SHARDED or name in CONV:
        k = blocks.shape[1]
        return jnp.transpose(blocks, (1, 0, 2)).reshape(k, -1)
    return blocks.reshape(-1, blocks.shape[-1])


def _to_chip_blocks(whole, name):
    if name in BIG_COLUMN_SHARDED or name in CONV:
        k, n = whole.shape
        return jnp.transpose(whole.reshape(k, N_CHIPS, n // N_CHIPS), (1, 0, 2))
    return whole.reshape(N_CHIPS, whole.shape[0] // N_CHIPS, whole.shape[1])


def kernel(x, norm1_w, w_in, b_gate, conv_a_w, conv_a_b, dt_bias, a_log, d_skip, ssd_norm_w, uv_b, v_ln_w, v_ln_b, w_spatial, b_spatial, w_branch, w_out, norm2_w, w_up, conv_f_w, conv_f_b, w_down, final_norm_w, loss_target, m_norm1_w, m_w_in, m_b_gate, m_conv_a_w, m_conv_a_b, m_dt_bias, m_a_log, m_d_skip, m_ssd_norm_w, m_uv_b, m_v_ln_w, m_v_ln_b, m_w_spatial, m_b_spatial, m_w_branch, m_w_out, m_norm2_w, m_w_up, m_conv_f_w, m_conv_f_b, m_w_down, m_final_norm_w, v_norm1_w, v_w_in, v_b_gate, v_conv_a_w, v_conv_a_b, v_dt_bias, v_a_log, v_d_skip, v_ssd_norm_w, v_uv_b, v_v_ln_w, v_v_ln_b, v_w_spatial, v_b_spatial, v_w_branch, v_w_out, v_norm2_w, v_w_up, v_conv_f_w, v_conv_f_b, v_w_down, v_final_norm_w):
    weights = dict(norm1_w=norm1_w, w_in=w_in, b_gate=b_gate, conv_a_w=conv_a_w, conv_a_b=conv_a_b, dt_bias=dt_bias,
                   a_log=a_log, d_skip=d_skip, ssd_norm_w=ssd_norm_w, uv_b=uv_b, v_ln_w=v_ln_w, v_ln_b=v_ln_b,
                   w_spatial=w_spatial, b_spatial=b_spatial, w_branch=w_branch, w_out=w_out, norm2_w=norm2_w,
                   w_up=w_up, conv_f_w=conv_f_w, conv_f_b=conv_f_b, w_down=w_down, final_norm_w=final_norm_w)
    mom1 = dict(norm1_w=m_norm1_w, w_in=m_w_in, b_gate=m_b_gate, conv_a_w=m_conv_a_w, conv_a_b=m_conv_a_b,
                dt_bias=m_dt_bias, a_log=m_a_log, d_skip=m_d_skip, ssd_norm_w=m_ssd_norm_w, uv_b=m_uv_b,
                v_ln_w=m_v_ln_w, v_ln_b=m_v_ln_b, w_spatial=m_w_spatial, b_spatial=m_b_spatial, w_branch=m_w_branch,
                w_out=m_w_out, norm2_w=m_norm2_w, w_up=m_w_up, conv_f_w=m_conv_f_w, conv_f_b=m_conv_f_b,
                w_down=m_w_down, final_norm_w=m_final_norm_w)
    mom2 = dict(norm1_w=v_norm1_w, w_in=v_w_in, b_gate=v_b_gate, conv_a_w=v_conv_a_w, conv_a_b=v_conv_a_b,
                dt_bias=v_dt_bias, a_log=v_a_log, d_skip=v_d_skip, ssd_norm_w=v_ssd_norm_w, uv_b=v_uv_b,
                v_ln_w=v_v_ln_w, v_ln_b=v_v_ln_b, w_spatial=v_w_spatial, b_spatial=v_b_spatial, w_branch=v_w_branch,
                w_out=v_w_out, norm2_w=v_norm2_w, w_up=v_w_up, conv_f_w=v_conv_f_w, conv_f_b=v_conv_f_b,
                w_down=v_w_down, final_norm_w=v_final_norm_w)
    chip = 2 * lax.axis_index("x") + lax.axis_index("y")
    core = lax.axis_index("c").astype(jnp.int32).reshape(1)

    whole = {}
    conv_shapes = [weights[n].shape[1:] for n in CONV]
    conv_gathered = _all_gather_chips(_flat_rows([weights[n] for n in CONV], 16), "gather_conv").reshape(N_CHIPS, -1)
    off = 0
    for n, shp in zip(CONV, conv_shapes):
        size = math.prod(shp)
        whole[n] = _from_chip_blocks(conv_gathered[:, off:off + size].reshape((N_CHIPS,) + shp), n)
        off += size
    shard_shapes = {n: weights[n].shape[1:] for n in BIG}
    halves = [weights[n][0].astype(BF16).reshape(2, shard_shapes[n][0] // 2, shard_shapes[n][1]) for n in BIG]
    sends = [_gather_sends if n == "w_in" else _gather_whole_sends for n in BIG]
    gathers, gathers_started = _exchange_start(halves, [(N_CHIPS,) + h.shape for h in halves], sends,
                                               after=conv_gathered, name="gather_start")
    gathers = dict(zip(BIG, gathers))

    def get_weight(name, after):
        rows, cols = shard_shapes[name]
        if name == "w_in":
            own, landed = _exchange_wait(gathers[name], after, _gather_sends, _gather_arrivals,
                                         name="gather_" + name + "_wait")
            landed = _gather_d2d(landed, name="gather_" + name + "_d2d")
        else:
            own, landed = _exchange_wait(gathers[name], after, _gather_whole_sends, _gather_whole_arrivals,
                                         name="gather_" + name + "_wait")
        blocks = lax.dynamic_update_slice(landed.reshape(N_CHIPS, rows, cols), own.reshape(1, rows, cols),
                                          (chip, 0, 0))
        if name == "w_up":
            return {"up": blocks}
        if name == "w_in":
            parts = [(blocks[k], cols * k) for k in range(N_CHIPS)]
            segs = {n: _take_columns(parts, a, b) for n, (a, b) in IN_SEGMENTS.items()}
            segs["in_dt"] = jnp.pad(segs["in_dt"], ((0, 0), (0, LANES - SSD_HEADS)))
            return segs
        full = _from_chip_blocks(blocks, name)
        if name == "w_branch":
            return {"branch_a": full[:SSD_D_INNER], "branch_b": full[SSD_D_INNER:]}
        return {name[2:]: full}

    small = {n: weights[n] for n in REPLICATED}
    small["conv_a_w"] = whole["conv_a_w"]
    small["conv_f_w"] = whole["conv_f_w"]
    small["gathers_started"] = gathers_started

    reductions = {}

    def emit_grad(name, g):
        if name == "w_in":
            parts = [(g[n], IN_SEGMENTS[n][0]) for n in IN_GRAD_SEGMENTS]
            cols = shard_shapes[name][1]
            g_blocks = jnp.stack([_take_columns(parts, cols * k, cols * (k + 1)) for k in range(N_CHIPS)])
        else:
            g_blocks = g if name == "w_up" else _to_chip_blocks(g, name)
        if name == "w_in":
            _, rows, cols = g_blocks.shape
            g_halves = g_blocks.reshape(N_CHIPS, 2, rows // 2, cols)
            arrived = _swap_halves_d2d(g_halves, name="reduce_" + name + "_swap")
            g_blocks = _add_own_half(g_halves, arrived, core, name="reduce_" + name + "_add2")
        own = lax.dynamic_slice_in_dim(g_blocks, chip, 1, axis=0)
        (pending,), started = _exchange_start([g_blocks], [g_blocks.shape], _scatter_sends,
                                              name="reduce_" + name + "_start")
        reductions[name] = (pending, own)
        return started

    loss, dx, grads_small = _local_step(x[0], loss_target[0], get_weight, small, emit_grad)

    order = ("w_down", "w_up", "w_out", "w_branch", "w_in")
    core_sums = []
    for n in order:
        pending, own = reductions[n]
        _, landed = _exchange_wait(pending, dx, _scatter_sends, _scatter_arrivals, name="reduce_" + n + "_wait")
        parts = lax.dynamic_update_slice(landed, own, (chip, 0, 0))
        core_sums.append(_sum_chips(parts, name="reduce_" + n + "_sum4"))
    sibling_sums = dict(zip(order, _swap_with_sibling(core_sums, name="reduce_swap")))
    core_sums = dict(zip(order, core_sums))
    first = lax.axis_index("c") == 0
    w_in_halves = (core_sums["w_in"], sibling_sums["w_in"])
    w_in_grad = jnp.concatenate([jnp.where(first, w_in_halves[0], w_in_halves[1]),
                                 jnp.where(first, w_in_halves[1], w_in_halves[0])], axis=0)
    grads = {}

    small_names = REPLICATED + CONV
    small_shapes = [grads_small[n].shape for n in small_names]
    g_small = _flat_rows([grads_small[n] for n in small_names], N_CHIPS * 2 * SMALL_EXCHANGE_ROWS)
    red_small = _reduce_scatter_chips(g_small.reshape(N_CHIPS, -1, LANES), core, "reduce_small")
    all_small = _all_gather_chips(red_small, "gather_small")
    for n, g in zip(small_names, _unflatten(all_small, small_shapes)):
        if n in CONV:
            width = g.shape[1] // N_CHIPS
            g = lax.dynamic_slice_in_dim(g, chip * width, width, axis=1)
        grads[n] = g.reshape(weights[n].shape[1:]) if n != "final_norm_w" else g

    delta, new_m, new_v = {}, {}, {}
    for n in BIG:
        shp = weights[n].shape
        if n == "w_in":
            g_t = w_in_grad.T
            results = [g_t] + list(_adamw(weights[n][0].T, g_t, mom1[n][0].T, mom2[n][0].T, name="adamw_" + n,
                                          tr=_row_tile(g_t.shape[0], 8, 136)))
            results = [a.T for a in results]
        else:
            results = _adamw_two_sums(weights[n][0], core_sums[n], sibling_sums[n], mom1[n][0], mom2[n][0],
                                      name="adamw_" + n, tr=_row_tile(shp[1], 8, 136))
        grads[n], delta[n], new_m[n], new_v[n] = [a.reshape(shp) for a in results]
    small_all = [n for n in WEIGHT_ORDER if n not in BIG]

    def as_2d(a):
        return a.reshape(-1, a.shape[-1])

    results = _adamw_many(*[[as_2d(src[n]) for n in small_all] for src in (weights, grads, mom1, mom2)],
                          name="adamw_small")
    for n, dv, mv, vv in zip(small_all, *results):
        shp = weights[n].shape
        delta[n], new_m[n], new_v[n] = dv.reshape(shp), mv.reshape(shp), vv.reshape(shp)

    total_loss = lax.psum(loss[0, 0], ("x", "y", "c"))
    grad_out = [grads[n].reshape(weights[n].shape) for n in WEIGHT_ORDER]
    return (total_loss, dx[None], *grad_out, *[delta[n] for n in WEIGHT_ORDER], *[new_m[n] for n in WEIGHT_ORDER],
            *[new_v[n] for n in WEIGHT_ORDER])
```

```python
import functools
import math

import jax
import jax.numpy as jnp
from jax import lax
from jax.experimental import pallas as pl
from jax.experimental.pallas import tpu as pltpu

F32 = jnp.float32
BF16 = jnp.bfloat16
HI = lax.Precision.HIGHEST

D_MODEL = 1024
SSD_D_INNER = 2048
SSD_HEADS = 32
SSD_HEAD_DIM = 64
SSD_GROUPS = 4
SSD_HEADS_PER_GROUP = 8
SSD_STATE = 128
SSD_BC = 512
SSD_XBC = 3072
SSD_IN = 5152
SGU_WIDTH = 1024
SGU_GROUPS = 8
CHUNK = 128
IN_COLS = 9248
D_FF = 2816
NORM_EPS = 1e-6
LN_EPS = 1e-5
GROUP_COLS = SSD_HEADS_PER_GROUP * SSD_HEAD_DIM
LANES = 128

ADAM_LR = 0.001
ADAM_B1 = 0.9
ADAM_B2 = 0.999
ADAM_EPS = 1e-08
ADAM_WD = 0.01
ADAM_STEP = 10

N_CHIPS = 4
VMEM_LIMIT = 56 * 1024 * 1024

NT = (((1,), (1,)), ((), ()))
TN = (((0,), (0,)), ((), ()))
NN = (((1,), (0,)), ((), ()))


def _params(dims):
    return pltpu.CompilerParams(dimension_semantics=dims, vmem_limit_bytes=VMEM_LIMIT)


def _dot(a, b, dn=NN, precision=None):
    return lax.dot_general(a, b, dn, precision=precision, preferred_element_type=F32)


def _split3(x):
    hi = x.astype(BF16)
    rest = x - hi.astype(F32)
    mid = rest.astype(BF16)
    return hi, mid, (rest - mid.astype(F32)).astype(BF16)


def _dot_terms(terms, exact, dn=NN):
    out = None
    for t in terms:
        p = _dot(t, exact, dn)
        out = p if out is None else out + p
    return out


def _dot_exact_lhs(exact, terms):
    out = None
    for t in terms:
        p = _dot(exact, t)
        out = p if out is None else out + p
    return out


def _sigmoid(x):
    return 1.0 / (1.0 + jnp.exp(-x))


def _softplus(x):
    return jnp.maximum(x, 0.0) + jnp.log(1.0 + jnp.exp(-jnp.abs(x)))


def _matmul(pairs, *, trans_b=False, add=None, after=None, out_dtype=F32, tm=512, tn=512, name):
    def mat_shape(b):
        if isinstance(b, tuple) and b[1] == "cols":
            return (b[0].shape[1], b[0].shape[0] * b[0].shape[2])
        return b[0].shape[1:] if isinstance(b, tuple) else b.shape

    if isinstance(pairs[0][1], tuple) and pairs[0][1][1] == "cols":
        assert not trans_b and tn % LANES == 0 and pairs[0][1][0].shape[2] % tn == 0, name

    m = (pairs[0][0][0] if isinstance(pairs[0][0], tuple) else pairs[0][0]).shape[0]
    n = mat_shape(pairs[0][1])[0] if trans_b else mat_shape(pairs[0][1])[1]
    tm, tn = min(tm, m), min(tn, n)
    assert m % tm == 0 and n % tn == 0, (name, m, n, tm, tn)
    npairs = len(pairs)
    dn = NT if trans_b else NN

    def body(*refs):
        o_ref = refs[-1]
        acc = None
        for i in range(npairs):
            p = _dot(refs[2 * i][...].astype(BF16), refs[2 * i + 1][...].astype(BF16), dn)
            acc = p if acc is None else acc + p
        if add is not None:
            acc = acc + refs[2 * npairs][...]
        o_ref[...] = acc.astype(out_dtype)

    in_specs, args = [], []
    for a, b in pairs:
        bshape = mat_shape(b)
        k = bshape[1] if trans_b else bshape[0]
        assert bshape == ((n, k) if trans_b else (k, n)), (name, bshape)
        a, qa = a if isinstance(a, tuple) else (a, 0)
        assert a.shape[0] == m and a.shape[1] % k == 0, (name, a.shape, k)
        in_specs.append(pl.BlockSpec((tm, k), lambda i, j, qa=qa: (i, qa)))
        if isinstance(b, tuple) and b[1] == "cols":
            b = b[0]
            per = b.shape[2] // tn
            in_specs.append(pl.BlockSpec((None, k, tn), lambda i, j, per=per: (j // per, 0, j % per)))
        elif isinstance(b, tuple):
            b, qb = b
            if trans_b:
                in_specs.append(pl.BlockSpec((None, tn, k), lambda i, j, qb=qb: (qb, j, 0)))
            else:
                in_specs.append(pl.BlockSpec((None, k, tn), lambda i, j, qb=qb: (qb, 0, j)))
        elif trans_b:
            in_specs.append(pl.BlockSpec((tn, k), lambda i, j: (j, 0)))
        else:
            in_specs.append(pl.BlockSpec((k, tn), lambda i, j: (0, j)))
        args += [a, b]
    if add is not None:
        in_specs.append(pl.BlockSpec((tm, tn), lambda i, j: (i, j)))
        args.append(add)
    if after is not None:
        in_specs.append(pl.BlockSpec(memory_space=pl.ANY))
        args.append(after)
    return pl.pallas_call(
        body, name=name, grid=(m // tm, n // tn), in_specs=in_specs,
        out_specs=pl.BlockSpec((tm, tn), lambda i, j: (i, j)),
        out_shape=jax.ShapeDtypeStruct((m, n), out_dtype),
        compiler_params=_params(("parallel", "parallel")),
    )(*args)


def _matmul_tn(a, b, *, tk, tn, tm=1024, out_dtype=BF16, stack_out=False, after=None, name):
    m, k = a.shape
    n = b.shape[1]
    tm, tk, tn = min(tm, m), min(tk, k), min(tn, n)
    assert m % tm == 0 and k % tk == 0 and n % tn == 0, (name, m, k, n)
    nm = m // tm
    if stack_out:
        out_spec = pl.BlockSpec((None, tk, tn), lambda i, j, l: (j, i, 0))
        out_shape = jax.ShapeDtypeStruct((n // tn, k, tn), out_dtype)
    else:
        out_spec = pl.BlockSpec((tk, tn), lambda i, j, l: (i, j))
        out_shape = jax.ShapeDtypeStruct((k, n), out_dtype)

    def body(a_ref, b_ref, *rest):
        o_ref, acc = rest[-2:]
        mi = pl.program_id(2)

        @pl.when(mi == 0)
        def _():
            acc[...] = jnp.zeros_like(acc)

        acc[...] += _dot(a_ref[...].astype(BF16), b_ref[...].astype(BF16), TN)

        @pl.when(mi == nm - 1)
        def _():
            o_ref[...] = acc[...].astype(out_dtype)

    in_specs = [pl.BlockSpec((tm, tk), lambda i, j, l: (l, i)), pl.BlockSpec((tm, tn), lambda i, j, l: (l, j))]
    args = [a, b]
    if after is not None:
        in_specs.append(pl.BlockSpec(memory_space=pl.ANY))
        args.append(after)
    return pl.pallas_call(
        body, name=name, grid=(k // tk, n // tn, nm), in_specs=in_specs,
        out_specs=out_spec, out_shape=out_shape,
        scratch_shapes=[pltpu.VMEM((tk, tn), F32)],
        compiler_params=_params(("parallel", "parallel", "arbitrary")),
    )(*args)


def _rms_fwd(x, w, *, after=None, name, tm=512):
    s, d = x.shape
    tm = min(tm, s)
    extra = [] if after is None else [after]

    def body(x_ref, w_ref, *rest):
        o_ref = rest[-1]
        xv = x_ref[...]
        r = lax.rsqrt(jnp.mean(xv * xv, axis=-1, keepdims=True) + NORM_EPS)
        o_ref[...] = (xv * r * w_ref[...]).astype(BF16)

    return pl.pallas_call(
        body, name=name, grid=(s // tm,),
        in_specs=[pl.BlockSpec((tm, d), lambda i: (i, 0)), pl.BlockSpec((1, d), lambda i: (0, 0))]
        + [pl.BlockSpec(memory_space=pl.ANY)] * len(extra),
        out_specs=pl.BlockSpec((tm, d), lambda i: (i, 0)),
        out_shape=jax.ShapeDtypeStruct((s, d), BF16),
        compiler_params=_params(("parallel",)),
    )(x, w, *extra)


def _rms_bwd(x, w, dn, dres, *, name, tm=512):
    s, d = x.shape
    tm = min(tm, s)

    def body(x_ref, w_ref, dn_ref, dres_ref, dx_ref, dxb_ref, dw_ref):
        @pl.when(pl.program_id(0) == 0)
        def _():
            dw_ref[...] = jnp.zeros_like(dw_ref)

        xv = x_ref[...]
        r = lax.rsqrt(jnp.mean(xv * xv, axis=-1, keepdims=True) + NORM_EPS)
        xhat = xv * r
        dnv = dn_ref[...].astype(F32)
        dxhat = dnv * w_ref[...]
        dx = dres_ref[...] + r * (dxhat - xhat * jnp.mean(dxhat * xhat, axis=-1, keepdims=True))
        dx_ref[...] = dx
        dxb_ref[...] = dx.astype(BF16)
        dw_ref[...] += jnp.sum(dnv * xhat, axis=0, keepdims=True)

    tile = pl.BlockSpec((tm, d), lambda i: (i, 0))
    row = pl.BlockSpec((1, d), lambda i: (0, 0))
    return pl.pallas_call(
        body, name=name, grid=(s // tm,),
        in_specs=[tile, row, tile, tile], out_specs=[tile, tile, row],
        out_shape=[jax.ShapeDtypeStruct((s, d), F32), jax.ShapeDtypeStruct((s, d), BF16),
                   jax.ShapeDtypeStruct((1, d), F32)],
        compiler_params=_params(("arbitrary",)),
    )(x, w, dn, dres)


def _final_fwd_bwd(h2, wf, target, *, name, tm=512):
    s, d = h2.shape
    tm = min(tm, s)

    def body(h_ref, w_ref, t_ref, loss_ref, dh_ref, dhb_ref, dw_ref):
        @pl.when(pl.program_id(0) == 0)
        def _():
            dw_ref[...] = jnp.zeros_like(dw_ref)
            loss_ref[...] = jnp.zeros_like(loss_ref)

        hv = h_ref[...]
        r = lax.rsqrt(jnp.mean(hv * hv, axis=-1, keepdims=True) + NORM_EPS)
        xhat = hv * r
        err = xhat * w_ref[...] - t_ref[...]
        per_tok = jnp.mean(err * err, axis=-1, keepdims=True)
        loss_ref[...] += 0.5 * jnp.sum(per_tok, axis=0, keepdims=True)
        dy = err * (1.0 / d)
        dxhat = dy * w_ref[...]
        dh = r * (dxhat - xhat * jnp.mean(dxhat * xhat, axis=-1, keepdims=True))
        dh_ref[...] = dh
        dhb_ref[...] = dh.astype(BF16)
        dw_ref[...] += jnp.sum(dy * xhat, axis=0, keepdims=True)

    tile = pl.BlockSpec((tm, d), lambda i: (i, 0))
    row = pl.BlockSpec((1, d), lambda i: (0, 0))
    return pl.pallas_call(
        body, name=name, grid=(s // tm,),
        in_specs=[tile, row, tile],
        out_specs=[pl.BlockSpec((1, 1), lambda i: (0, 0)), tile, tile, row],
        out_shape=[jax.ShapeDtypeStruct((1, 1), F32), jax.ShapeDtypeStruct((s, d), F32),
                   jax.ShapeDtypeStruct((s, d), BF16), jax.ShapeDtypeStruct((1, d), F32)],
        compiler_params=_params(("arbitrary",)),
    )(h2, wf, target)


CONV_ROWS = 512
HALO = 8


def _rows_with_halo(ref, r0, rows, s, before, after):
    parts = []
    if before:
        prev = ref[pl.ds(pl.multiple_of(jnp.maximum(r0 - HALO, 0), HALO), HALO), :]
        parts.append(jnp.where(r0 > 0, prev, 0.0))
    parts.append(ref[pl.ds(r0, rows), :])
    if after:
        nxt = ref[pl.ds(pl.multiple_of(jnp.minimum(r0 + rows, s - HALO), HALO), HALO), :]
        parts.append(jnp.where(r0 + rows < s, nxt, 0.0))
    return jnp.concatenate(parts, axis=0) if len(parts) > 1 else parts[0]


def _fill_padded(x_ref, xp, s):
    zeros = jnp.zeros((HALO, xp.shape[1]), F32)
    xp[pl.ds(0, HALO), :] = zeros
    xp[pl.ds(HALO, s), :] = x_ref[...].astype(F32)
    xp[pl.ds(HALO + s, HALO), :] = zeros


def _shifted(xp, r0, k, rows):
    window = xp[pl.ds(r0, rows + HALO), :]
    if k == 0:
        return window[HALO:]
    return pltpu.roll(window, k, 0)[HALO:]


def _conv_taps(xp, r0, w_ref, kk, rows):
    acc = None
    for i in range(kk):
        term = w_ref[i:i + 1, :] * _shifted(xp, r0, kk - 1 - i, rows)
        acc = term if acc is None else acc + term
    return acc


def _row_loop(s, step):
    def body(r, carry):
        return step(pl.multiple_of(r * CONV_ROWS, CONV_ROWS), carry)
    return body


def _conv_bwd_rows(xp, r0, dpe, dp_sc, w_ref, kk):
    del dp_sc
    dp = dpe[:CONV_ROWS]
    dx = None
    dws = []
    for i in range(kk):
        k = kk - 1 - i
        dws.append(jnp.sum(dp * _shifted(xp, r0, k, CONV_ROWS), axis=0, keepdims=True))
        later = dp if k == 0 else pltpu.roll(dpe, dpe.shape[0] - k, 0)[:CONV_ROWS]
        term = w_ref[i:i + 1, :] * later
        dx = term if dx is None else dx + term
    return dx, dws, jnp.sum(dp, axis=0, keepdims=True)


def _conv_scratch(s, tc, n_padded, n_dp):
    return ([pltpu.VMEM((s + 2 * HALO, tc), F32)] * n_padded
            + [pltpu.VMEM((CONV_ROWS + HALO, tc), F32)] * n_dp)


def _conv_a_fwd(xraw, w, b, *, name, tc=128):
    s, c = xraw.shape
    kk = 4

    def body(x_ref, w_ref, b_ref, o_ref, xp):
        _fill_padded(x_ref, xp, s)

        def step(r0, carry):
            pre = _conv_taps(xp, r0, w_ref, kk, CONV_ROWS) + b_ref[...]
            o_ref[pl.ds(r0, CONV_ROWS), :] = pre * _sigmoid(pre)
            return carry

        lax.fori_loop(0, s // CONV_ROWS, _row_loop(s, step), 0)

    col = pl.BlockSpec((s, tc), lambda j: (0, j))
    return pl.pallas_call(
        body, name=name, grid=(c // tc,),
        in_specs=[col, pl.BlockSpec((8, tc), lambda j: (0, j)), pl.BlockSpec((1, tc), lambda j: (0, j))],
        out_specs=col, out_shape=jax.ShapeDtypeStruct((s, c), F32),
        scratch_shapes=_conv_scratch(s, tc, 1, 0),
        compiler_params=_params(("parallel",)),
    )(xraw, w, b)


def _conv_a_bwd(xraw, w, b, dy, *, name, tc=128):
    s, c = xraw.shape
    kk = 4

    def body(x_ref, w_ref, b_ref, dy_ref, dx_ref, dw_ref, db_ref, xp, dp_sc):
        _fill_padded(x_ref, xp, s)

        def step(r0, carry):
            pre = _conv_taps(xp, r0, w_ref, kk, CONV_ROWS + HALO) + b_ref[...]
            sg = _sigmoid(pre)
            dpe = _rows_with_halo(dy_ref, r0, CONV_ROWS, s, False, True) * (sg * (1.0 + pre * (1.0 - sg)))
            dx, dws, db = _conv_bwd_rows(xp, r0, dpe, dp_sc, w_ref, kk)
            dx_ref[pl.ds(r0, CONV_ROWS), :] = dx.astype(BF16)
            return tuple(acc + new for acc, new in zip(carry, dws + [db]))

        zero = jnp.zeros((1, tc), F32)
        sums = lax.fori_loop(0, s // CONV_ROWS, _row_loop(s, step), (zero,) * (kk + 1))
        db_ref[...] = sums[kk]
        dw_ref[...] = jnp.concatenate(list(sums[:kk]) + [jnp.zeros((8 - kk, tc), F32)], axis=0)

    col = pl.BlockSpec((s, tc), lambda j: (0, j))
    w8 = pl.BlockSpec((8, tc), lambda j: (0, j))
    row = pl.BlockSpec((1, tc), lambda j: (0, j))
    return pl.pallas_call(
        body, name=name, grid=(c // tc,),
        in_specs=[col, w8, row, col], out_specs=[col, w8, row],
        out_shape=[jax.ShapeDtypeStruct((s, c), BF16), jax.ShapeDtypeStruct((8, c), F32),
                   jax.ShapeDtypeStruct((1, c), F32)],
        scratch_shapes=_conv_scratch(s, tc, 1, 1),
        compiler_params=_params(("parallel",)),
    )(xraw, w, b, dy)


def _conv_f_fwd(up_raw, w, b, *, name, tc=128):
    s, c2 = up_raw.shape
    c = c2 // 2
    nb = c // tc
    kk = 3

    def body(xa_ref, xv_ref, wa_ref, wv_ref, ba_ref, bv_ref, o_ref, xap, xvp):
        _fill_padded(xa_ref, xap, s)
        _fill_padded(xv_ref, xvp, s)

        def step(r0, carry):
            a = _conv_taps(xap, r0, wa_ref, kk, CONV_ROWS) + ba_ref[...]
            v = _conv_taps(xvp, r0, wv_ref, kk, CONV_ROWS) + bv_ref[...]
            o_ref[pl.ds(r0, CONV_ROWS), :] = (a * _sigmoid(a) * v).astype(BF16)
            return carry

        lax.fori_loop(0, s // CONV_ROWS, _row_loop(s, step), 0)

    col_a = pl.BlockSpec((s, tc), lambda j: (0, j))
    col_v = pl.BlockSpec((s, tc), lambda j: (0, j + nb))
    return pl.pallas_call(
        body, name=name, grid=(nb,),
        in_specs=[col_a, col_v, pl.BlockSpec((8, tc), lambda j: (0, j)), pl.BlockSpec((8, tc), lambda j: (0, j + nb)),
                  pl.BlockSpec((1, tc), lambda j: (0, j)), pl.BlockSpec((1, tc), lambda j: (0, j + nb))],
        out_specs=col_a, out_shape=jax.ShapeDtypeStruct((s, c), BF16),
        scratch_shapes=_conv_scratch(s, tc, 2, 0),
        compiler_params=_params(("parallel",)),
    )(up_raw, up_raw, w, w, b, b)


def _conv_f_bwd(up_raw, w, b, dact, *, name, tc=128):
    s, c2 = up_raw.shape
    c = c2 // 2
    nb = c // tc
    kk = 3

    def body(xa_ref, xv_ref, wa_ref, wv_ref, ba_ref, bv_ref, d_ref,
             dxa_ref, dxv_ref, dwa_ref, dwv_ref, dba_ref, dbv_ref, xap, xvp, dpa_sc, dpv_sc):
        _fill_padded(xa_ref, xap, s)
        _fill_padded(xv_ref, xvp, s)

        def step(r0, carry):
            a = _conv_taps(xap, r0, wa_ref, kk, CONV_ROWS + HALO) + ba_ref[...]
            v = _conv_taps(xvp, r0, wv_ref, kk, CONV_ROWS + HALO) + bv_ref[...]
            sg = _sigmoid(a)
            d = _rows_with_halo(d_ref, r0, CONV_ROWS, s, False, True)
            dxa, dwas, dba = _conv_bwd_rows(xap, r0, d * v * (sg * (1.0 + a * (1.0 - sg))), dpa_sc, wa_ref, kk)
            dxv, dwvs, dbv = _conv_bwd_rows(xvp, r0, d * (a * sg), dpv_sc, wv_ref, kk)
            dxa_ref[pl.ds(r0, CONV_ROWS), :] = dxa.astype(BF16)
            dxv_ref[pl.ds(r0, CONV_ROWS), :] = dxv.astype(BF16)
            return tuple(acc + new for acc, new in zip(carry, dwas + [dba] + dwvs + [dbv]))

        zero = jnp.zeros((1, tc), F32)
        sums = lax.fori_loop(0, s // CONV_ROWS, _row_loop(s, step), (zero,) * (2 * kk + 2))
        pad = [jnp.zeros((8 - kk, tc), F32)]
        dwa_ref[...] = jnp.concatenate(list(sums[:kk]) + pad, axis=0)
        dba_ref[...] = sums[kk]
        dwv_ref[...] = jnp.concatenate(list(sums[kk + 1:2 * kk + 1]) + pad, axis=0)
        dbv_ref[...] = sums[2 * kk + 1]

    col_a = pl.BlockSpec((s, tc), lambda j: (0, j))
    col_v = pl.BlockSpec((s, tc), lambda j: (0, j + nb))
    w_a = pl.BlockSpec((8, tc), lambda j: (0, j))
    w_v = pl.BlockSpec((8, tc), lambda j: (0, j + nb))
    r_a = pl.BlockSpec((1, tc), lambda j: (0, j))
    r_v = pl.BlockSpec((1, tc), lambda j: (0, j + nb))
    outs = pl.pallas_call(
        body, name=name, grid=(nb,),
        in_specs=[col_a, col_v, w_a, w_v, r_a, r_v, col_a],
        out_specs=[col_a, col_a, w_a, w_a, r_a, r_a],
        out_shape=[jax.ShapeDtypeStruct((s, c), BF16), jax.ShapeDtypeStruct((s, c), BF16),
                   jax.ShapeDtypeStruct((8, c), F32), jax.ShapeDtypeStruct((8, c), F32),
                   jax.ShapeDtypeStruct((1, c), F32), jax.ShapeDtypeStruct((1, c), F32)],
        scratch_shapes=_conv_scratch(s, tc, 2, 2),
        compiler_params=_params(("parallel",)),
    )(up_raw, up_raw, w, w, b, b, dact)
    return outs


def _tri_masks():
    row = lax.broadcasted_iota(jnp.int32, (CHUNK, CHUNK), 0)
    col = lax.broadcasted_iota(jnp.int32, (CHUNK, CHUNK), 1)
    return row >= col, row <= col


def _ssd_fwd(xbc, dt_raw, z, dt_bias, a_log, a_log_x, d_skip_x, norm_w, expand, *, name):
    s = xbc.shape[0]
    nc = s // CHUNK

    def body(xbc_ref, dtr_ref, z_ref, dtb_ref, alog_ref, alogx_ref, dskx_ref, nw_ref, e_ref,
             y_ref, ya_ref, st_ref, state):
        @pl.when(pl.program_id(0) == 0)
        def _():
            state[...] = jnp.zeros_like(state)

        st_ref[0] = state[...]
        lower, _ = _tri_masks()
        dt = _softplus(dtr_ref[...] + dtb_ref[...])
        adt = dt * (-jnp.exp(alog_ref[...]))
        acum = _dot_exact_lhs(lower.astype(BF16), _split3(adt))
        acum_t = acum.T
        dt_terms, acum_terms = _split3(dt), _split3(acum)
        for g in range(SSD_GROUPS):
            sl = slice(GROUP_COLS * g, GROUP_COLS * (g + 1))
            dt_x = _dot_terms(dt_terms, e_ref[:, sl])
            acum_x = _dot_terms(acum_terms, e_ref[:, sl])
            tot_x = jnp.sum(dt_x * (-jnp.exp(alogx_ref[:, sl])), axis=0, keepdims=True)
            xs = xbc_ref[:, sl]
            xdt = xs * dt_x
            xdt_b = xdt.astype(BF16)
            bg = xbc_ref[:, SSD_D_INNER + SSD_STATE * g:SSD_D_INNER + SSD_STATE * (g + 1)].astype(BF16)
            cg = xbc_ref[:, SSD_D_INNER + SSD_BC + SSD_STATE * g:SSD_D_INNER + SSD_BC + SSD_STATE * (g + 1)].astype(BF16)
            cb = _dot(cg, bg, NT)
            st_g = state[:, sl]
            y_off = _dot(cg, st_g.astype(BF16)) * jnp.exp(acum_x)
            parts = []
            for r in range(SSD_HEADS_PER_GROUP):
                h = SSD_HEADS_PER_GROUP * g + r
                dec = jnp.exp(jnp.where(lower, acum[:, h:h + 1] - acum_t[h:h + 1, :], -jnp.inf))
                parts.append(_dot((cb * dec).astype(BF16), xdt_b[:, SSD_HEAD_DIM * r:SSD_HEAD_DIM * (r + 1)]))
            y_ref[:, sl] = jnp.concatenate(parts, axis=1) + y_off + dskx_ref[:, sl] * xs
            wgt = (xdt * jnp.exp(tot_x - acum_x)).astype(BF16)
            state[:, sl] = st_g * jnp.exp(tot_x) + _dot(bg, wgt, TN)
        zv = z_ref[...].astype(F32)
        q = y_ref[...] * (zv * _sigmoid(zv))
        r = lax.rsqrt(jnp.mean(q * q, axis=-1, keepdims=True) + NORM_EPS)
        ya_ref[...] = (q * r * nw_ref[...]).astype(BF16)

    def chunk(w):
        return pl.BlockSpec((CHUNK, w), lambda c: (c, 0))

    def const(shape):
        return pl.BlockSpec(shape, lambda c: (0,) * len(shape))

    return pl.pallas_call(
        body, name=name, grid=(nc,),
        in_specs=[chunk(SSD_XBC), chunk(LANES), chunk(SSD_D_INNER), const((1, LANES)), const((1, LANES)),
                  const((1, SSD_D_INNER)), const((1, SSD_D_INNER)), const((1, SSD_D_INNER)),
                  const((LANES, SSD_D_INNER))],
        out_specs=[chunk(SSD_D_INNER), chunk(SSD_D_INNER),
                   pl.BlockSpec((1, SSD_STATE, SSD_D_INNER), lambda c: (c, 0, 0))],
        out_shape=[jax.ShapeDtypeStruct((s, SSD_D_INNER), F32), jax.ShapeDtypeStruct((s, SSD_D_INNER), BF16),
                   jax.ShapeDtypeStruct((nc, SSD_STATE, SSD_D_INNER), F32)],
        scratch_shapes=[pltpu.VMEM((SSD_STATE, SSD_D_INNER), F32)],
        compiler_params=_params(("arbitrary",)),
    )(xbc, dt_raw, z, dt_bias, a_log, a_log_x, d_skip_x, norm_w, expand)


def _ssd_bwd(dya, y, z, xbc, dt_raw, states, dt_bias, a_log, a_log_x, d_skip_x, norm_w, expand, expand_t, *, name):
    s = xbc.shape[0]
    nc = s // CHUNK

    def body(dya_ref, y_ref, z_ref, xbc_ref, dtr_ref, stp_ref, dtb_ref, alog_ref, alogx_ref, dskx_ref, nw_ref,
             e_ref, et_ref, dz_ref, dxbc_ref, ddt_ref, dnw_ref, ddsk_ref, dalog_ref, ddtb_ref,
             dstate, dy_sc, dskcol):
        i = pl.program_id(0)

        @pl.when(i == 0)
        def _():
            dstate[...] = jnp.zeros_like(dstate)
            dskcol[...] = jnp.zeros_like(dskcol)
            dnw_ref[...] = jnp.zeros_like(dnw_ref)
            dalog_ref[...] = jnp.zeros_like(dalog_ref)
            ddtb_ref[...] = jnp.zeros_like(ddtb_ref)
            ddsk_ref[...] = jnp.zeros_like(ddsk_ref)

        lower, upper = _tri_masks()
        rows = lax.broadcasted_iota(jnp.int32, (CHUNK, LANES), 0)
        pre = dtr_ref[...] + dtb_ref[...]
        dt = _softplus(pre)
        a = -jnp.exp(alog_ref[...])
        acum = _dot_exact_lhs(lower.astype(BF16), _split3(dt * a))
        acum_t = acum.T
        dt_terms, acum_terms = _split3(dt), _split3(acum)

        yv = y_ref[...]
        zv = z_ref[...].astype(F32)
        sz = _sigmoid(zv)
        silu_z = zv * sz
        q = yv * silu_z
        r = lax.rsqrt(jnp.mean(q * q, axis=-1, keepdims=True) + NORM_EPS)
        qhat = q * r
        dyav = dya_ref[...]
        dqhat = dyav * nw_ref[...]
        dnw_ref[...] += jnp.sum(dyav * qhat, axis=0, keepdims=True)
        dq = r * (dqhat - qhat * jnp.mean(dqhat * qhat, axis=-1, keepdims=True))
        dy_sc[...] = dq * silu_z
        dz_ref[...] = (dq * yv * (sz * (1.0 + zv * (1.0 - sz)))).astype(BF16)

        da_cum = jnp.zeros((CHUNK, LANES), F32)
        ddt = jnp.zeros((CHUNK, LANES), F32)
        for g in range(SSD_GROUPS):
            sl = slice(GROUP_COLS * g, GROUP_COLS * (g + 1))
            et_g = et_ref[sl, :]
            dt_x = _dot_terms(dt_terms, e_ref[:, sl])
            acum_x = _dot_terms(acum_terms, e_ref[:, sl])
            tot_x = jnp.sum(dt_x * (-jnp.exp(alogx_ref[:, sl])), axis=0, keepdims=True)
            e_tot = jnp.exp(tot_x)
            dec_s = jnp.exp(tot_x - acum_x)
            xs = xbc_ref[:, sl]
            xdt = xs * dt_x
            xdt_b = xdt.astype(BF16)
            dy = dy_sc[:, sl]
            dy_b = dy.astype(BF16)
            dskx = dskx_ref[:, sl]
            y_ssd = y_ref[:, sl] - dskx * xs
            dskcol[:, sl] += jnp.sum(dy * xs, axis=0, keepdims=True)
            bg = xbc_ref[:, SSD_D_INNER + SSD_STATE * g:SSD_D_INNER + SSD_STATE * (g + 1)].astype(BF16)
            cg = xbc_ref[:, SSD_D_INNER + SSD_BC + SSD_STATE * g:SSD_D_INNER + SSD_BC + SSD_STATE * (g + 1)].astype(BF16)
            cb_t = _dot(bg, cg, NT)
            sp = stp_ref[0, :, sl]
            ds_g = dstate[:, sl]
            ds_b = ds_g.astype(BF16)
            dye_b = (dy * jnp.exp(acum_x)).astype(BF16)
            dc = _dot(dye_b, sp.astype(BF16), NT)
            dxdt_state = dec_s * _dot(bg, ds_b)
            db = _dot((xdt * dec_s).astype(BF16), ds_b, NT)
            dcb_t = jnp.zeros((CHUNK, CHUNK), F32)
            parts = []
            for rr in range(SSD_HEADS_PER_GROUP):
                h = SSD_HEADS_PER_GROUP * g + rr
                hs = slice(SSD_HEAD_DIM * rr, SSD_HEAD_DIM * (rr + 1))
                dec_t = jnp.exp(jnp.where(upper, acum_t[h:h + 1, :] - acum[:, h:h + 1], -jnp.inf))
                parts.append(_dot((cb_t * dec_t).astype(BF16), dy_b[:, hs]))
                dcb_t = dcb_t + _dot(xdt_b[:, hs], dy_b[:, hs], NT) * dec_t
            dxdt = jnp.concatenate(parts, axis=1) + dxdt_state
            dcb_tb = dcb_t.astype(BF16)
            dc = dc + _dot(dcb_tb, bg, TN)
            db = db + _dot(dcb_tb, cg)
            tot_col = jnp.sum(ds_g * sp, axis=0, keepdims=True) * e_tot + jnp.sum(dxdt_state * xdt, axis=0, keepdims=True)
            d_tot = _dot_terms(_split3(jnp.broadcast_to(tot_col, (8, GROUP_COLS))), et_g)
            d_tot = jnp.max(d_tot, axis=0, keepdims=True)
            pair_sums = dy_b.astype(F32) * y_ssd - xdt_b.astype(F32) * dxdt
            da_cum = da_cum + _dot_terms(_split3(pair_sums), et_g) + jnp.where(rows == CHUNK - 1, d_tot, 0.0)
            ddt = ddt + _dot_terms(_split3(dxdt * xs), et_g)
            dxbc_ref[:, sl] = dy * dskx + dxdt * dt_x
            dxbc_ref[:, SSD_D_INNER + SSD_STATE * g:SSD_D_INNER + SSD_STATE * (g + 1)] = db
            dxbc_ref[:, SSD_D_INNER + SSD_BC + SSD_STATE * g:SSD_D_INNER + SSD_BC + SSD_STATE * (g + 1)] = dc
            dstate[:, sl] = e_tot * ds_g + _dot(cg, dye_b, TN)

        dadt = _dot_exact_lhs(upper.astype(BF16), _split3(da_cum))
        ddt = ddt + dadt * a
        dalog_ref[...] += jnp.sum(dadt * dt, axis=0, keepdims=True)
        dpre = ddt * _sigmoid(pre)
        ddtb_ref[...] += jnp.sum(dpre, axis=0, keepdims=True)
        ddt_ref[...] = dpre.astype(BF16)

        @pl.when(i == nc - 1)
        def _():
            dalog_ref[...] = dalog_ref[...] * a
            dsk = _dot_terms(_split3(jnp.broadcast_to(dskcol[...], (8, SSD_D_INNER))), et_ref[...])
            ddsk_ref[...] = jnp.max(dsk, axis=0, keepdims=True)

    def chunk(w):
        return pl.BlockSpec((CHUNK, w), lambda i: (nc - 1 - i, 0))

    def const(shape):
        return pl.BlockSpec(shape, lambda i: (0,) * len(shape))

    return pl.pallas_call(
        body, name=name, grid=(nc,),
        in_specs=[chunk(SSD_D_INNER), chunk(SSD_D_INNER), chunk(SSD_D_INNER), chunk(SSD_XBC), chunk(LANES),
                  pl.BlockSpec((1, SSD_STATE, SSD_D_INNER), lambda i: (nc - 1 - i, 0, 0)),
                  const((1, LANES)), const((1, LANES)), const((1, SSD_D_INNER)), const((1, SSD_D_INNER)),
                  const((1, SSD_D_INNER)), const((LANES, SSD_D_INNER)), const((SSD_D_INNER, LANES))],
        out_specs=[chunk(SSD_D_INNER), chunk(SSD_XBC), chunk(LANES), const((1, SSD_D_INNER)), const((1, LANES)),
                   const((1, LANES)), const((1, LANES))],
        out_shape=[jax.ShapeDtypeStruct((s, SSD_D_INNER), BF16), jax.ShapeDtypeStruct((s, SSD_XBC), F32),
                   jax.ShapeDtypeStruct((s, LANES), BF16), jax.ShapeDtypeStruct((1, SSD_D_INNER), F32),
                   jax.ShapeDtypeStruct((1, LANES), F32), jax.ShapeDtypeStruct((1, LANES), F32),
                   jax.ShapeDtypeStruct((1, LANES), F32)],
        scratch_shapes=[pltpu.VMEM((SSD_STATE, SSD_D_INNER), F32), pltpu.VMEM((CHUNK, SSD_D_INNER), F32),
                        pltpu.VMEM((1, SSD_D_INNER), F32)],
        compiler_params=_params(("arbitrary",)),
    )(dya, y, z, xbc, dt_raw, states, dt_bias, a_log, a_log_x, d_skip_x, norm_w, expand, expand_t)


GELU_K = math.sqrt(2.0 / math.pi)
GELU_C = 0.044715


def _gelu(x):
    return 0.5 * x * (1.0 + jnp.tanh(GELU_K * (x + GELU_C * x * x * x)))


def _gelu_grad(x):
    t = jnp.tanh(GELU_K * (x + GELU_C * x * x * x))
    return 0.5 * (1.0 + t) + 0.5 * x * (1.0 - t * t) * (GELU_K * (1.0 + 3.0 * GELU_C * x * x))


def _sgu_pre(uv_ref, uvb_ref, lnw_ref, lnb_ref):
    uv = uv_ref[...].astype(F32) + uvb_ref[...]
    guv = _gelu(uv)
    u = guv[:, :SGU_WIDTH]
    v = guv[:, SGU_WIDTH:]
    mu = jnp.mean(v, axis=-1, keepdims=True)
    vc = v - mu
    rstd = lax.rsqrt(jnp.mean(vc * vc, axis=-1, keepdims=True) + LN_EPS)
    vhat = vc * rstd
    vn = vhat * lnw_ref[...] + lnb_ref[...]
    return uv, u, vhat, rstd, vn


def _sgu_fwd(uv_raw, uv_b, ln_w, ln_b, w_sp, b_sp_t, *, name):
    s = uv_raw.shape[0]
    nc = s // CHUNK

    def body(uv_ref, uvb_ref, lnw_ref, lnb_ref, w_ref, bt_ref, o_ref):
        lower, _ = _tri_masks()
        _, u, _, _, vn = _sgu_pre(uv_ref, uvb_ref, lnw_ref, lnb_ref)
        vn_b = vn.astype(BF16)
        bt = bt_ref[...]
        for g in range(SGU_GROUPS):
            gs = slice(LANES * g, LANES * (g + 1))
            wc = jnp.where(lower, w_ref[g], 0.0).astype(BF16)
            mixed = _dot(wc, vn_b[:, gs]) + bt[:, g:g + 1]
            o_ref[:, gs] = (u[:, gs] * mixed).astype(BF16)

    def const(shape):
        return pl.BlockSpec(shape, lambda c: (0,) * len(shape))

    return pl.pallas_call(
        body, name=name, grid=(nc,),
        in_specs=[pl.BlockSpec((CHUNK, 2 * SGU_WIDTH), lambda c: (c, 0)), const((1, 2 * SGU_WIDTH)),
                  const((1, SGU_WIDTH)), const((1, SGU_WIDTH)), const((SGU_GROUPS, CHUNK, CHUNK)),
                  const((CHUNK, LANES))],
        out_specs=pl.BlockSpec((CHUNK, SGU_WIDTH), lambda c: (c, 0)),
        out_shape=jax.ShapeDtypeStruct((s, SGU_WIDTH), BF16),
        compiler_params=_params(("parallel",)),
    )(uv_raw, uv_b, ln_w, ln_b, w_sp, b_sp_t)


def _sgu_bwd(uv_raw, dyb, uv_b, ln_w, ln_b, w_sp, b_sp_t, group_sum, *, name):
    s = uv_raw.shape[0]
    nc = s // CHUNK

    def body(uv_ref, dy_ref, uvb_ref, lnw_ref, lnb_ref, w_ref, bt_ref, gsum_ref,
             duv_ref, dw_ref, dbt_ref, dlnw_ref, dlnb_ref, duvb_ref):
        @pl.when(pl.program_id(0) == 0)
        def _():
            dw_ref[...] = jnp.zeros_like(dw_ref)
            dbt_ref[...] = jnp.zeros_like(dbt_ref)
            dlnw_ref[...] = jnp.zeros_like(dlnw_ref)
            dlnb_ref[...] = jnp.zeros_like(dlnb_ref)
            duvb_ref[...] = jnp.zeros_like(duvb_ref)

        lower, _ = _tri_masks()
        uv, u, vhat, rstd, vn = _sgu_pre(uv_ref, uvb_ref, lnw_ref, lnb_ref)
        vn_b = vn.astype(BF16)
        bt = bt_ref[...]
        dy = dy_ref[...].astype(F32)
        du_parts, dvn_parts, dmix_parts = [], [], []
        for g in range(SGU_GROUPS):
            gs = slice(LANES * g, LANES * (g + 1))
            wc = jnp.where(lower, w_ref[g], 0.0).astype(BF16)
            mixed = _dot(wc, vn_b[:, gs]) + bt[:, g:g + 1]
            du_parts.append(dy[:, gs] * mixed)
            dmix = dy[:, gs] * u[:, gs]
            dmix_b = dmix.astype(BF16)
            dmix_parts.append(dmix)
            dw_ref[g] += jnp.where(lower, _dot(dmix_b, vn_b[:, gs], NT), 0.0)
            dvn_parts.append(_dot(wc, dmix_b, TN))
        dmixed = jnp.concatenate(dmix_parts, axis=1)
        dbt_ref[...] += _dot_terms(_split3(dmixed), gsum_ref[...])
        dvn = jnp.concatenate(dvn_parts, axis=1)
        dlnw_ref[...] += jnp.sum(dvn * vhat, axis=0, keepdims=True)
        dlnb_ref[...] += jnp.sum(dvn, axis=0, keepdims=True)
        dvhat = dvn * lnw_ref[...]
        dv = rstd * (dvhat - jnp.mean(dvhat, axis=-1, keepdims=True)
                     - vhat * jnp.mean(dvhat * vhat, axis=-1, keepdims=True))
        dguv = jnp.concatenate(du_parts + [dv], axis=1)
        duv = dguv * _gelu_grad(uv)
        duvb_ref[...] += jnp.sum(duv, axis=0, keepdims=True)
        duv_ref[...] = duv.astype(BF16)

    def const(shape):
        return pl.BlockSpec(shape, lambda c: (0,) * len(shape))

    return pl.pallas_call(
        body, name=name, grid=(nc,),
        in_specs=[pl.BlockSpec((CHUNK, 2 * SGU_WIDTH), lambda c: (c, 0)),
                  pl.BlockSpec((CHUNK, SGU_WIDTH), lambda c: (c, 0)), const((1, 2 * SGU_WIDTH)),
                  const((1, SGU_WIDTH)), const((1, SGU_WIDTH)), const((SGU_GROUPS, CHUNK, CHUNK)),
                  const((CHUNK, LANES)), const((SGU_WIDTH, LANES))],
        out_specs=[pl.BlockSpec((CHUNK, 2 * SGU_WIDTH), lambda c: (c, 0)), const((SGU_GROUPS, CHUNK, CHUNK)),
                   const((CHUNK, LANES)), const((1, SGU_WIDTH)), const((1, SGU_WIDTH)), const((1, 2 * SGU_WIDTH))],
        out_shape=[jax.ShapeDtypeStruct((s, 2 * SGU_WIDTH), BF16),
                   jax.ShapeDtypeStruct((SGU_GROUPS, CHUNK, CHUNK), F32), jax.ShapeDtypeStruct((CHUNK, LANES), F32),
                   jax.ShapeDtypeStruct((1, SGU_WIDTH), F32), jax.ShapeDtypeStruct((1, SGU_WIDTH), F32),
                   jax.ShapeDtypeStruct((1, 2 * SGU_WIDTH), F32)],
        compiler_params=_params(("arbitrary",)),
    )(uv_raw, dyb, uv_b, ln_w, ln_b, w_sp, b_sp_t, group_sum)


def _gate_fwd(gates_raw, b_gate, p_a, p_b, *, name, tm=512):
    s = p_a.shape[0]
    tm = min(tm, s)

    def body(ga_ref, gb_ref, ba_ref, bb_ref, pa_ref, pb_ref, o_ref):
        ga = _sigmoid(ga_ref[...].astype(F32) + ba_ref[...])
        gb = _sigmoid(gb_ref[...].astype(F32) + bb_ref[...])
        o_ref[...] = (ga * pa_ref[...].astype(F32) + gb * pb_ref[...].astype(F32)).astype(BF16)

    t_a = pl.BlockSpec((tm, D_MODEL), lambda i: (i, 0))
    t_b = pl.BlockSpec((tm, D_MODEL), lambda i: (i, 1))
    r_a = pl.BlockSpec((1, D_MODEL), lambda i: (0, 0))
    r_b = pl.BlockSpec((1, D_MODEL), lambda i: (0, 1))
    return pl.pallas_call(
        body, name=name, grid=(s // tm,),
        in_specs=[t_a, t_b, r_a, r_b, t_a, t_a], out_specs=t_a,
        out_shape=jax.ShapeDtypeStruct((s, D_MODEL), BF16),
        compiler_params=_params(("parallel",)),
    )(gates_raw, gates_raw, b_gate, b_gate, p_a, p_b)


def _gate_bwd(gates_raw, b_gate, p_a, p_b, dm, *, name, tm=512):
    s = p_a.shape[0]
    tm = min(tm, s)

    def body(ga_ref, gb_ref, ba_ref, bb_ref, pa_ref, pb_ref, dm_ref, dpa_ref, dpb_ref, dga_ref, dgb_ref,
             dba_ref, dbb_ref):
        @pl.when(pl.program_id(0) == 0)
        def _():
            dba_ref[...] = jnp.zeros_like(dba_ref)
            dbb_ref[...] = jnp.zeros_like(dbb_ref)

        d = dm_ref[...].astype(F32)
        for g_ref, b_ref, p_ref, dp_ref, dg_ref, db_ref in ((ga_ref, ba_ref, pa_ref, dpa_ref, dga_ref, dba_ref),
                                                            (gb_ref, bb_ref, pb_ref, dpb_ref, dgb_ref, dbb_ref)):
            sg = _sigmoid(g_ref[...].astype(F32) + b_ref[...])
            dp_ref[...] = (d * sg).astype(BF16)
            dg = d * p_ref[...].astype(F32) * (sg * (1.0 - sg))
            dg_ref[...] = dg.astype(BF16)
            db_ref[...] += jnp.sum(dg, axis=0, keepdims=True)

    t_a = pl.BlockSpec((tm, D_MODEL), lambda i: (i, 0))
    t_b = pl.BlockSpec((tm, D_MODEL), lambda i: (i, 1))
    r_a = pl.BlockSpec((1, D_MODEL), lambda i: (0, 0))
    r_b = pl.BlockSpec((1, D_MODEL), lambda i: (0, 1))
    big = jax.ShapeDtypeStruct((s, D_MODEL), BF16)
    row = jax.ShapeDtypeStruct((1, D_MODEL), F32)
    return pl.pallas_call(
        body, name=name, grid=(s // tm,),
        in_specs=[t_a, t_b, r_a, r_b, t_a, t_a, t_a], out_specs=[t_a, t_a, t_a, t_a, r_a, r_a],
        out_shape=[big, big, big, big, row, row],
        compiler_params=_params(("arbitrary",)),
    )(gates_raw, gates_raw, b_gate, b_gate, p_a, p_b, dm)


def _adamw_update(w_ref, g_ref, m_ref, v_ref, d_ref, mo_ref, vo_ref):
    gv = g_ref[...]
    mn = ADAM_B1 * m_ref[...] + (1.0 - ADAM_B1) * gv
    vn = ADAM_B2 * v_ref[...] + (1.0 - ADAM_B2) * (gv * gv)
    m_hat = mn / (1.0 - ADAM_B1 ** ADAM_STEP)
    v_hat = vn / (1.0 - ADAM_B2 ** ADAM_STEP)
    d_ref[...] = -ADAM_LR * (m_hat / (jnp.sqrt(v_hat) + ADAM_EPS) + ADAM_WD * w_ref[...])
    mo_ref[...] = mn
    vo_ref[...] = vn


def _adamw_many(ws, gs, ms, vs, *, name):
    n = len(ws)

    def body(*refs):
        for i in range(n):
            _adamw_update(*[refs[k * n + i] for k in range(7)])

    whole = pl.BlockSpec(memory_space=pltpu.VMEM)
    sds = [jax.ShapeDtypeStruct(w.shape, F32) for w in ws]
    outs = pl.pallas_call(
        body, name=name, in_specs=[whole] * (4 * n), out_specs=[whole] * (3 * n), out_shape=sds * 3,
        compiler_params=pltpu.CompilerParams(vmem_limit_bytes=VMEM_LIMIT),
    )(*ws, *gs, *ms, *vs)
    return outs[:n], outs[n:2 * n], outs[2 * n:]


def _adamw(w, g, m, v, *, name, tr=128):
    r, c = w.shape
    tr = min(tr, r)
    assert r % tr == 0, (name, r, tr)
    body = functools.partial(_adamw_update)

    blk = pl.BlockSpec((tr, c), lambda i: (i, 0))
    sds = jax.ShapeDtypeStruct((r, c), F32)
    return pl.pallas_call(
        body, name=name, grid=(r // tr,), in_specs=[blk] * 4, out_specs=[blk] * 3, out_shape=[sds] * 3,
        compiler_params=_params(("parallel",)),
    )(w, g, m, v)


def _adamw_two_sums(w, g_a, g_b, m, v, *, name, tr=128):
    r, c = w.shape
    tr = min(tr, r)
    assert r % tr == 0, (name, r, tr)

    def body(w_ref, ga_ref, gb_ref, m_ref, v_ref, g_ref, d_ref, mo_ref, vo_ref):
        g_ref[...] = ga_ref[...] + gb_ref[...]
        _adamw_update(w_ref, g_ref, m_ref, v_ref, d_ref, mo_ref, vo_ref)

    blk = pl.BlockSpec((tr, c), lambda i: (i, 0))
    sds = jax.ShapeDtypeStruct((r, c), F32)
    return pl.pallas_call(
        body, name=name, grid=(r // tr,), in_specs=[blk] * 5, out_specs=[blk] * 4, out_shape=[sds] * 4,
        compiler_params=_params(("parallel",)),
    )(w, g_a, g_b, m, v)


def _tile(n, pref):
    if n <= pref:
        return n
    best = LANES
    for t in range(LANES, pref + 1, LANES):
        if n % t == 0:
            best = t
    return best


MATMUL_BLOCK_BYTES = 20 * 1024 * 1024


def _mm(pairs, name, **kw):
    trans_b = kw.get("trans_b", False)
    m = (pairs[0][0][0] if isinstance(pairs[0][0], tuple) else pairs[0][0]).shape[0]
    ktot, n = 0, None
    for _, b in pairs:
        shape = b[0].shape[1:] if isinstance(b, tuple) else b.shape
        ktot += shape[1] if trans_b else shape[0]
        n = shape[0] if trans_b else shape[1]
    out_bytes = 4 * (2 if kw.get("add") is not None else 1)
    best = None
    for tm in (256, 512, 1024, 2048):
        for tn in range(LANES, min(n, 1536) + 1, LANES):
            if m % min(tm, m) or n % tn:
                continue
            fits = 2 * ktot * (min(tm, m) + tn) + out_bytes * min(tm, m) * tn <= MATMUL_BLOCK_BYTES
            if fits and (best is None or min(tm, m) * tn >= best[0] * best[1]):
                best = (min(tm, m), tn)
    return _matmul(pairs, tm=best[0], tn=best[1], name=name, **kw)


def _wgrad(a, b, name, **kw):
    return _matmul_tn(a, b, tk=_tile(a.shape[1], 1408), tn=kw.pop("tn", _tile(b.shape[1], 1024)), tm=2048,
                      name=name, **kw)


def _local_step(x, target, get_weight, small, emit_grad):
    heads = jnp.arange(SSD_D_INNER) // SSD_HEAD_DIM
    expand = (jnp.arange(LANES)[:, None] == heads[None, :]).astype(BF16)
    expand_t = expand.T
    group_sum = (jnp.arange(SGU_WIDTH)[:, None] // LANES == jnp.arange(LANES)[None, :]).astype(BF16)
    pad_h = LANES - SSD_HEADS
    dt_bias = jnp.pad(small["dt_bias"], ((0, 0), (0, pad_h)))
    a_log = jnp.pad(small["a_log"], ((0, 0), (0, pad_h)))
    a_log_x = jnp.repeat(small["a_log"], SSD_HEAD_DIM, axis=1)
    d_skip_x = jnp.repeat(small["d_skip"], SSD_HEAD_DIM, axis=1)
    b_sp_t = jnp.pad(small["b_spatial"][0].T, ((0, 0), (0, LANES - SGU_GROUPS)))
    w_sp = small["w_spatial"][0]
    conv_a_w = jnp.pad(small["conv_a_w"], ((0, 4), (0, 0)))
    conv_f_w = jnp.pad(small["conv_f_w"], ((0, 5), (0, 0)))
    final_w = small["final_norm_w"].reshape(1, D_MODEL)

    n1 = _rms_fwd(x, small["norm1_w"], after=small.get("gathers_started"), name="rms1_fwd")
    wts = dict(get_weight("w_in", n1))
    z = _mm([(n1, wts["in_z"])], "in_z")
    xbc_raw = _mm([(n1, wts["in_xbc"])], "in_xbc")
    dt_raw = _mm([(n1, wts["in_dt"])], "in_dt")
    uv_raw = _mm([(n1, wts["in_uv"])], "in_uv", out_dtype=BF16)
    gates_raw = _mm([(n1, wts["in_gate"])], "in_gate", out_dtype=BF16)
    xbc = _conv_a_fwd(xbc_raw, conv_a_w, small["conv_a_b"], name="conv_a_fwd")
    y, y_a, states = _ssd_fwd(xbc, dt_raw, z, dt_bias, a_log, a_log_x, d_skip_x, small["ssd_norm_w"], expand,
                              name="ssd_fwd")
    y_b = _sgu_fwd(uv_raw, small["uv_b"], small["v_ln_w"], small["v_ln_b"], w_sp, b_sp_t, name="sgu_fwd")
    wts.update(get_weight("w_branch", y_b))
    p_a = _mm([(y_a, wts["branch_a"])], "branch_a", out_dtype=BF16)
    p_b = _mm([(y_b, wts["branch_b"])], "branch_b", out_dtype=BF16)
    mix = _gate_fwd(gates_raw, small["b_gate"], p_a, p_b, name="gate_fwd")
    wts.update(get_weight("w_out", mix))
    h1 = _mm([(mix, wts["out"])], "out_proj", add=x)
    n2 = _rms_fwd(h1, small["norm2_w"], name="rms2_fwd")
    wts.update(get_weight("w_up", n2))
    up_w = wts["up"]
    up_cols = up_w.shape[2]
    up_raw = _matmul([(n2, (up_w, "cols"))], tm=2048, tn=up_cols, out_dtype=BF16, name="up_proj")
    act = _conv_f_fwd(up_raw, conv_f_w, small["conv_f_b"], name="conv_f_fwd")
    wts.update(get_weight("w_down", act))
    h2 = _mm([(act, wts["down"])], "down_proj", add=h1)
    loss, dh2, dh2_b, d_final = _final_fwd_bwd(h2, final_w, target, name="final_norm_loss")

    dact = _mm([(dh2_b, wts["down"])], "down_dgrad", trans_b=True)
    started = emit_grad("w_down", _wgrad(act, dh2_b, "down_wgrad"))
    dup_a, dup_v, dwf_a, dwf_v, dbf_a, dbf_v = _conv_f_bwd(up_raw, conv_f_w, small["conv_f_b"], dact,
                                                           name="conv_f_bwd")
    dn2 = _mm([((dup_a, 0), (up_w, 0)), ((dup_a, 1), (up_w, 1)), ((dup_v, 0), (up_w, 2)), ((dup_v, 1), (up_w, 3))],
              "up_dgrad", trans_b=True, after=started, out_dtype=BF16)
    started = emit_grad("w_up", jnp.concatenate([_wgrad(n2, dup_a, "up_wgrad_a", tn=up_cols, stack_out=True),
                                                 _wgrad(n2, dup_v, "up_wgrad_v", tn=up_cols, stack_out=True)], axis=0))
    dh1, dh1_b, d_norm2 = _rms_bwd(h1, small["norm2_w"], dn2, dh2, name="rms2_bwd")
    dmix = _mm([(dh1_b, wts["out"])], "out_dgrad", trans_b=True, after=started, out_dtype=BF16)
    started = emit_grad("w_out", _wgrad(mix, dh1_b, "out_wgrad"))
    dp_a, dp_b, dg_a, dg_b, dbg_a, dbg_b = _gate_bwd(gates_raw, small["b_gate"], p_a, p_b, dmix, name="gate_bwd")
    dya = _mm([(dp_a, wts["branch_a"])], "branch_a_dgrad", trans_b=True, after=started)
    dyb = _mm([(dp_b, wts["branch_b"])], "branch_b_dgrad", trans_b=True, out_dtype=BF16)
    started_branch = emit_grad("w_branch", jnp.concatenate([_wgrad(y_a, dp_a, "branch_a_wgrad"),
                                                            _wgrad(y_b, dp_b, "branch_b_wgrad")], axis=0))
    duv, d_wsp, d_bsp_t, d_lnw, d_lnb, d_uvb = _sgu_bwd(uv_raw, dyb, small["uv_b"], small["v_ln_w"],
                                                        small["v_ln_b"], w_sp, b_sp_t, group_sum, name="sgu_bwd")
    dz, dxbc, ddt, d_ssd_nw, d_dskip, d_alog, d_dtb = _ssd_bwd(
        dya, y, z, xbc, dt_raw, states, dt_bias, a_log, a_log_x, d_skip_x, small["ssd_norm_w"], expand, expand_t,
        name="ssd_bwd")
    dxbc_raw, d_conv_a_w, d_conv_a_b = _conv_a_bwd(xbc_raw, conv_a_w, small["conv_a_b"], dxbc, name="conv_a_bwd")
    started = emit_grad("w_in", {
        "in_z": _wgrad(n1, dz, "in_z_wgrad", after=started_branch), "in_xbc": _wgrad(n1, dxbc_raw, "in_xbc_wgrad"),
        "in_dt": _wgrad(n1, ddt, "in_dt_wgrad")[:, :SSD_HEADS], "in_uv": _wgrad(n1, duv, "in_uv_wgrad"),
        "in_gate_a": _wgrad(n1, dg_a, "in_gate_a_wgrad"), "in_gate_b": _wgrad(n1, dg_b, "in_gate_b_wgrad")})
    dn1 = _mm([(dz, wts["in_z"]), (dxbc_raw, wts["in_xbc"]), (ddt, wts["in_dt"]), (duv, wts["in_uv"]),
               (dg_a, wts["in_gate_a"]), (dg_b, wts["in_gate_b"])], "in_dgrad", trans_b=True, after=started)
    dx, _, d_norm1 = _rms_bwd(x, small["norm1_w"], dn1, dh1, name="rms1_bwd")

    grads_small = {
        "norm1_w": d_norm1, "b_gate": jnp.concatenate([dbg_a, dbg_b], axis=1),
        "conv_a_w": d_conv_a_w[:4], "conv_a_b": d_conv_a_b,
        "dt_bias": d_dtb[:, :SSD_HEADS], "a_log": d_alog[:, :SSD_HEADS], "d_skip": d_dskip[:, :SSD_HEADS],
        "ssd_norm_w": d_ssd_nw, "uv_b": d_uvb, "v_ln_w": d_lnw, "v_ln_b": d_lnb,
        "w_spatial": d_wsp[None], "b_spatial": d_bsp_t[:, :SGU_GROUPS].T[None],
        "norm2_w": d_norm2, "conv_f_w": jnp.concatenate([dwf_a[:3], dwf_v[:3]], axis=1),
        "conv_f_b": jnp.concatenate([dbf_a, dbf_v], axis=1), "final_norm_w": d_final.reshape(D_MODEL),
    }
    return loss, dx, grads_small


HBM = pl.BlockSpec(memory_space=pl.ANY)
MESH = pl.DeviceIdType.MESH


def _mesh_pos():
    return lax.axis_index("x"), lax.axis_index("y"), lax.axis_index("c")


def _other_chips(x, y):
    return [(1 - x, y), (x, 1 - y), (1 - x, 1 - y)]


def _remote(src, dst, send_sems, recv_sems, k, dev):
    return pltpu.make_async_remote_copy(src_ref=src, dst_ref=dst, send_sem=send_sems.at[k], recv_sem=recv_sems.at[k],
                                        device_id=dev, device_id_type=MESH)


def _dma_sems(n):
    return [pltpu.SemaphoreType.DMA((n,)), pltpu.SemaphoreType.DMA((n,))]


HBM_ONLY = pl.BlockSpec(memory_space=pltpu.HBM)
SEMAPHORES = pl.BlockSpec(memory_space=pltpu.SEMAPHORE)
DATAFLOW_EFFECT = pltpu.SideEffectType.DATAFLOW_SIDE_EFFECTING
N_PEER_CHIPS = N_CHIPS - 1


def _gather_sends(w_ref, land_ref, send_sems, recv_sems):
    x, y, c = _mesh_pos()
    return [_remote(w_ref.at[c], land_ref.at[2 * x + y, c], send_sems, recv_sems, k, (px, py, c))
            for k, (px, py) in enumerate(_other_chips(x, y))]


def _gather_arrivals(w_ref, land_ref, send_sems, recv_sems):
    x, y, c = _mesh_pos()
    return [_remote(w_ref.at[c], land_ref.at[2 * px + py, c], send_sems, recv_sems, k, (px, py, c))
            for k, (px, py) in enumerate(_other_chips(x, y))]


def _gather_whole_sends(w_ref, land_ref, send_sems, recv_sems):
    x, y, c = _mesh_pos()
    return [_remote(w_ref, land_ref.at[2 * x + y], send_sems, recv_sems, k, (px, py, c))
            for k, (px, py) in enumerate(_other_chips(x, y))]


def _gather_whole_arrivals(w_ref, land_ref, send_sems, recv_sems):
    x, y, c = _mesh_pos()
    return [_remote(w_ref, land_ref.at[2 * px + py], send_sems, recv_sems, k, (px, py, c))
            for k, (px, py) in enumerate(_other_chips(x, y))]


def _scatter_sends(h_ref, land_ref, send_sems, recv_sems):
    x, y, c = _mesh_pos()
    return [_remote(h_ref.at[2 * px + py], land_ref.at[2 * x + y], send_sems, recv_sems, k, (px, py, c))
            for k, (px, py) in enumerate(_other_chips(x, y))]


def _scatter_arrivals(h_ref, land_ref, send_sems, recv_sems):
    x, y, c = _mesh_pos()
    return [_remote(h_ref.at[2 * x + y], land_ref.at[2 * px + py], send_sems, recv_sems, k, (px, py, c))
            for k, (px, py) in enumerate(_other_chips(x, y))]


def _exchange_wait_many(pendings, after, sends, arrivals, *, name):
    n = len(pendings)

    def body(*refs):
        for i in range(n):
            src_ref, land_ref, send_ref, recv_ref = refs[i], refs[n + i], refs[2 * n + i], refs[3 * n + i]
            for cp in sends(src_ref, land_ref, send_ref, recv_ref):
                cp.wait_send()
            for cp in arrivals(src_ref, land_ref, send_ref, recv_ref):
                cp.wait_recv()

    sources = [p[2] for p in pendings]
    landings = [p[3] for p in pendings]
    outs = pl.pallas_call(
        body, name=name,
        out_shape=tuple(pltpu.HBM(a.shape, a.dtype) for a in sources + landings),
        in_specs=[HBM_ONLY] * (2 * n) + [SEMAPHORES] * (2 * n) + [pl.BlockSpec(memory_space=pl.ANY)],
        out_specs=tuple([HBM_ONLY] * (2 * n)), input_output_aliases={i: i for i in range(2 * n)},
        compiler_params=pltpu.CompilerParams(has_side_effects=DATAFLOW_EFFECT),
    )(*sources, *landings, *[p[0] for p in pendings], *[p[1] for p in pendings], after)
    return [(outs[i], outs[n + i]) for i in range(n)]


def _sibling_sends(src_ref, land_ref, send_sems, recv_sems):
    x, y, c = _mesh_pos()
    return [_remote(src_ref, land_ref, send_sems, recv_sems, 0, (x, y, 1 - c))]


def _exchange_start(sources, landing_shapes, sends, *, after=None, name):
    n = len(sources)
    extra = [] if after is None else [after]

    def body(*refs):
        sems = refs[2 * n + len(extra):4 * n + len(extra)]
        for i in range(n):
            send_i = sends[i] if isinstance(sends, (list, tuple)) else sends
            for cp in send_i(refs[i], refs[n + i], sems[2 * i], sems[2 * i + 1]):
                cp.start()
        refs[-1][...] = jnp.zeros_like(refs[-1])

    hbm = [pltpu.HBM(s.shape, s.dtype) for s in sources] + [pltpu.HBM(shp, s.dtype)
                                                             for shp, s in zip(landing_shapes, sources)]
    outs = pl.pallas_call(
        body, name=name,
        out_shape=tuple([pltpu.SemaphoreType.DMA((N_PEER_CHIPS,))] * (2 * n) + hbm
                        + [jax.ShapeDtypeStruct((8, LANES), F32)]),
        in_specs=[HBM_ONLY] * (2 * n) + [pl.BlockSpec(memory_space=pl.ANY)] * len(extra),
        out_specs=tuple([SEMAPHORES] * (2 * n) + [HBM_ONLY] * (2 * n) + [pl.BlockSpec(memory_space=pltpu.VMEM)]),
        input_output_aliases={i: 2 * n + i for i in range(2 * n)},
        compiler_params=pltpu.CompilerParams(has_side_effects=DATAFLOW_EFFECT),
    )(*[pltpu.with_memory_space_constraint(s, pltpu.HBM) for s in sources],
      *[pltpu.with_memory_space_constraint(lax.empty(shp, s.dtype), pltpu.HBM)
        for shp, s in zip(landing_shapes, sources)], *extra)
    pending = [(outs[2 * i], outs[2 * i + 1], outs[2 * n + i], outs[3 * n + i]) for i in range(n)]
    return pending, outs[-1]


def _exchange_wait(pending, after, sends, arrivals, *, name):
    send_sems, recv_sems, source, landing = pending

    def body(src_ref, land_ref, send_ref, recv_ref, after_ref, src_out, land_out):
        for cp in sends(src_ref, land_ref, send_ref, recv_ref):
            cp.wait_send()
        for cp in arrivals(src_ref, land_ref, send_ref, recv_ref):
            cp.wait_recv()

    return pl.pallas_call(
        body, name=name,
        out_shape=(pltpu.HBM(source.shape, source.dtype), pltpu.HBM(landing.shape, landing.dtype)),
        in_specs=[HBM_ONLY, HBM_ONLY, SEMAPHORES, SEMAPHORES, pl.BlockSpec(memory_space=pl.ANY)],
        out_specs=(HBM_ONLY, HBM_ONLY), input_output_aliases={0: 0, 1: 1},
        compiler_params=pltpu.CompilerParams(has_side_effects=DATAFLOW_EFFECT),
    )(source, landing, send_sems, recv_sems, after)


def _gather_ici(shard, *, name):
    _, rh, cols = shard.shape

    def body(w_ref, o_ref, send_sems, recv_sems):
        x, y, c = _mesh_pos()
        mine = 2 * x + y
        sends = []
        for k, (px, py) in enumerate(_other_chips(x, y)):
            cp = _remote(w_ref.at[c], o_ref.at[mine, c], send_sems, recv_sems, k, (px, py, c))
            cp.start()
            sends.append(cp)
        for k, (px, py) in enumerate(_other_chips(x, y)):
            _remote(w_ref.at[c], o_ref.at[2 * px + py, c], send_sems, recv_sems, k, (px, py, c)).wait_recv()
        for cp in sends:
            cp.wait_send()

    return pl.pallas_call(
        body, name=name, in_specs=[HBM], out_specs=HBM,
        out_shape=jax.ShapeDtypeStruct((N_CHIPS, 2, rh, cols), shard.dtype), scratch_shapes=_dma_sems(3),
    )(shard)


def _gather_d2d(parts, *, name):
    def body(a_ref, o_ref, send_sems, recv_sems):
        x, y, c = _mesh_pos()
        sibling = (x, y, 1 - c)
        sends = []
        for k, (px, py) in enumerate(_other_chips(x, y)):
            cp = _remote(a_ref.at[2 * px + py, c], o_ref.at[2 * px + py, c], send_sems, recv_sems, k, sibling)
            cp.start()
            sends.append(cp)
        for k, (px, py) in enumerate(_other_chips(x, y)):
            _remote(a_ref.at[2 * px + py, c], o_ref.at[2 * px + py, 1 - c], send_sems, recv_sems, k, sibling).wait_recv()
        for cp in sends:
            cp.wait_send()

    return pl.pallas_call(
        body, name=name, in_specs=[HBM], out_specs=HBM,
        out_shape=jax.ShapeDtypeStruct(parts.shape, parts.dtype),
        input_output_aliases={0: 0}, scratch_shapes=_dma_sems(3),
    )(parts)


def _all_gather_chips(shard_flat, name):
    rows, cols = shard_flat.shape
    parts = _gather_ici(shard_flat.reshape(2, rows // 2, cols), name=name + "_ici")
    others = _gather_d2d(parts, name=name + "_d2d").reshape(N_CHIPS, rows, cols)
    chip = 2 * lax.axis_index("x") + lax.axis_index("y")
    return lax.dynamic_update_slice(others, shard_flat[None], (chip, 0, 0))


def _row_tile(rows, mult, cap):
    best = mult
    for t in range(mult, min(rows, cap) + 1, mult):
        if rows % t == 0:
            best = t
    assert rows % best == 0, (rows, mult)
    return best


def _swap_halves_d2d(g, *, after=None, name):
    _, _, rh, cols = g.shape
    extra = [] if after is None else [after]

    def body(g_ref, *rest):
        o_ref, send_sems, recv_sems = rest[len(extra):]
        x, y, c = _mesh_pos()
        sibling = (x, y, 1 - c)
        sends = []
        for s in range(N_CHIPS):
            cp = _remote(g_ref.at[s, 1 - c], o_ref.at[s], send_sems, recv_sems, s, sibling)
            cp.start()
            sends.append(cp)
        for s in range(N_CHIPS):
            _remote(g_ref.at[s, c], o_ref.at[s], send_sems, recv_sems, s, sibling).wait_recv()
        for cp in sends:
            cp.wait_send()

    return pl.pallas_call(
        body, name=name, in_specs=[HBM] * (1 + len(extra)), out_specs=HBM,
        out_shape=jax.ShapeDtypeStruct((N_CHIPS, rh, cols), g.dtype), scratch_shapes=_dma_sems(N_CHIPS),
    )(g, *extra)


def _add_own_half(g, arrived, core, *, name):
    _, _, rh, cols = g.shape
    mult = 16 if g.dtype == BF16 else 8
    tr = _row_tile(rh, mult, max(mult, (512 * 1024) // cols))

    def body(core_ref, g_ref, a_ref, o_ref):
        o_ref[...] = (g_ref[0].astype(F32) + a_ref[...].astype(F32)).astype(o_ref.dtype)

    grid_spec = pltpu.PrefetchScalarGridSpec(
        num_scalar_prefetch=1, grid=(N_CHIPS, rh // tr),
        in_specs=[pl.BlockSpec((1, 1, tr, cols), lambda s, i, core_ref: (s, core_ref[0], i, 0)),
                  pl.BlockSpec((1, tr, cols), lambda s, i, core_ref: (s, i, 0))],
        out_specs=pl.BlockSpec((1, tr, cols), lambda s, i, core_ref: (s, i, 0)))
    return pl.pallas_call(
        body, name=name, grid_spec=grid_spec, out_shape=jax.ShapeDtypeStruct((N_CHIPS, rh, cols), g.dtype),
        compiler_params=_params(("parallel", "parallel")),
    )(core, g, arrived)


def _scatter_ici(h, *, name):
    def body(h_ref, o_ref, send_sems, recv_sems):
        x, y, c = _mesh_pos()
        mine = 2 * x + y
        sends = []
        for k, (px, py) in enumerate(_other_chips(x, y)):
            cp = _remote(h_ref.at[2 * px + py], o_ref.at[mine], send_sems, recv_sems, k, (px, py, c))
            cp.start()
            sends.append(cp)
        for k, (px, py) in enumerate(_other_chips(x, y)):
            _remote(h_ref.at[mine], o_ref.at[2 * px + py], send_sems, recv_sems, k, (px, py, c)).wait_recv()
        for cp in sends:
            cp.wait_send()

    others = pl.pallas_call(
        body, name=name, in_specs=[HBM], out_specs=HBM, out_shape=jax.ShapeDtypeStruct(h.shape, h.dtype),
        scratch_shapes=_dma_sems(3),
    )(h)
    chip = 2 * lax.axis_index("x") + lax.axis_index("y")
    own = lax.dynamic_slice_in_dim(h, chip, 1, axis=0)
    return lax.dynamic_update_slice(others, own, (chip, 0, 0))


def _sum_chips(parts, *, name):
    _, rh, cols = parts.shape
    mult = 16 if parts.dtype == BF16 else 8
    tr = _row_tile(rh, mult, max(mult, (512 * 1024) // cols))

    def body(p_ref, o_ref):
        acc = p_ref[0].astype(F32)
        for s in range(1, N_CHIPS):
            acc = acc + p_ref[s].astype(F32)
        o_ref[...] = acc

    return pl.pallas_call(
        body, name=name, grid=(rh // tr,),
        in_specs=[pl.BlockSpec((N_CHIPS, tr, cols), lambda i: (0, i, 0))],
        out_specs=pl.BlockSpec((tr, cols), lambda i: (i, 0)),
        out_shape=jax.ShapeDtypeStruct((rh, cols), F32), compiler_params=_params(("parallel",)),
    )(parts)


def _share_d2d(f, *, name):
    fs = f if isinstance(f, (list, tuple)) else [f]
    others = _swap_with_sibling(fs, name=name)
    first = lax.axis_index("c") == 0
    both = [jnp.stack([jnp.where(first, a, b), jnp.where(first, b, a)]) for a, b in zip(fs, others)]
    return both if isinstance(f, (list, tuple)) else both[0]


def _swap_with_sibling(fs, *, name):
    n = len(fs)

    def body(*refs):
        x, y, c = _mesh_pos()
        sibling = (x, y, 1 - c)
        send_sems, recv_sems = refs[2 * n:]
        copies = [_remote(refs[i], refs[n + i], send_sems, recv_sems, i, sibling) for i in range(n)]
        for cp in copies:
            cp.start()
        for cp in copies:
            cp.wait()

    return pl.pallas_call(
        body, name=name, in_specs=[HBM] * n, out_specs=[HBM] * n,
        out_shape=[jax.ShapeDtypeStruct(a.shape, a.dtype) for a in fs], scratch_shapes=_dma_sems(n),
    )(*fs)


def _reduce_scatter_chips(g, core, name, after=None):
    _, rows, cols = g.shape
    g = g.reshape(N_CHIPS, 2, rows // 2, cols)
    arrived = _swap_halves_d2d(g, after=after, name=name + "_swap")
    chip_sum = _add_own_half(g, arrived, core, name=name + "_add2")
    parts = _scatter_ici(chip_sum, name=name + "_ici")
    total = _sum_chips(parts, name=name + "_sum4")
    return _share_d2d(total, name=name + "_share").reshape(rows, cols)


BIG = ("w_in", "w_branch", "w_out", "w_up", "w_down")
BIG_COLUMN_SHARDED = ("w_in", "w_up")
CONV = ("conv_a_w", "conv_f_w")
REPLICATED = ("norm1_w", "b_gate", "conv_a_b", "dt_bias", "a_log", "d_skip", "ssd_norm_w", "uv_b", "v_ln_w",
              "v_ln_b", "w_spatial", "b_spatial", "norm2_w", "conv_f_b", "final_norm_w")
WEIGHT_ORDER = ("norm1_w", "w_in", "b_gate", "conv_a_w", "conv_a_b", "dt_bias", "a_log", "d_skip", "ssd_norm_w",
                "uv_b", "v_ln_w", "v_ln_b", "w_spatial", "b_spatial", "w_branch", "w_out", "norm2_w", "w_up",
                "conv_f_w", "conv_f_b", "w_down", "final_norm_w")
SMALL_EXCHANGE_ROWS = 64


_GATE0 = SSD_IN + 2 * SGU_WIDTH
IN_SEGMENTS = {
    "in_z": (0, SSD_D_INNER), "in_xbc": (SSD_D_INNER, SSD_D_INNER + SSD_XBC), "in_dt": (SSD_D_INNER + SSD_XBC, SSD_IN),
    "in_uv": (SSD_IN, _GATE0), "in_gate": (_GATE0, IN_COLS), "in_gate_a": (_GATE0, _GATE0 + D_MODEL),
    "in_gate_b": (_GATE0 + D_MODEL, IN_COLS),
}
IN_GRAD_SEGMENTS = ("in_z", "in_xbc", "in_dt", "in_uv", "in_gate_a", "in_gate_b")


def _take_columns(parts, start, stop):
    out = []
    for a, first in parts:
        lo, hi = max(start, first), min(stop, first + a.shape[1])
        if lo < hi:
            out.append(a[:, lo - first:hi - first])
    return out[0] if len(out) == 1 else jnp.concatenate(out, axis=1)


def _flat_rows(arrays, row_multiple):
    flat = jnp.concatenate([a.reshape(-1) for a in arrays])
    rows = -(-flat.shape[0] // (LANES * row_multiple)) * row_multiple
    return jnp.pad(flat, (0, rows * LANES - flat.shape[0])).reshape(rows, LANES)


def _unflatten(flat, shapes):
    flat = flat.reshape(-1)
    out, off = [], 0
    for shp in shapes:
        n = math.prod(shp)
        out.append(flat[off:off + n].reshape(shp))
        off += n
    return out


def _from_chip_blocks(blocks, name):
    if name in BIG_COLUMN_SHARDED or name in CONV:
        k = blocks.shape[1]
        return jnp.transpose(blocks, (1, 0, 2)).reshape(k, -1)
    return blocks.reshape(-1, blocks.shape[-1])


def _to_chip_blocks(whole, name):
    if name in BIG_COLUMN_SHARDED or name in CONV:
        k, n = whole.shape
        return jnp.transpose(whole.reshape(k, N_CHIPS, n // N_CHIPS), (1, 0, 2))
    return whole.reshape(N_CHIPS, whole.shape[0] // N_CHIPS, whole.shape[1])


def kernel(x, norm1_w, w_in, b_gate, conv_a_w, conv_a_b, dt_bias, a_log, d_skip, ssd_norm_w, uv_b, v_ln_w, v_ln_b, w_spatial, b_spatial, w_branch, w_out, norm2_w, w_up, conv_f_w, conv_f_b, w_down, final_norm_w, loss_target, m_norm1_w, m_w_in, m_b_gate, m_conv_a_w, m_conv_a_b, m_dt_bias, m_a_log, m_d_skip, m_ssd_norm_w, m_uv_b, m_v_ln_w, m_v_ln_b, m_w_spatial, m_b_spatial, m_w_branch, m_w_out, m_norm2_w, m_w_up, m_conv_f_w, m_conv_f_b, m_w_down, m_final_norm_w, v_norm1_w, v_w_in, v_b_gate, v_conv_a_w, v_conv_a_b, v_dt_bias, v_a_log, v_d_skip, v_ssd_norm_w, v_uv_b, v_v_ln_w, v_v_ln_b, v_w_spatial, v_b_spatial, v_w_branch, v_w_out, v_norm2_w, v_w_up, v_conv_f_w, v_conv_f_b, v_w_down, v_final_norm_w):
    weights = dict(norm1_w=norm1_w, w_in=w_in, b_gate=b_gate, conv_a_w=conv_a_w, conv_a_b=conv_a_b, dt_bias=dt_bias,
                   a_log=a_log, d_skip=d_skip, ssd_norm_w=ssd_norm_w, uv_b=uv_b, v_ln_w=v_ln_w, v_ln_b=v_ln_b,
                   w_spatial=w_spatial, b_spatial=b_spatial, w_branch=w_branch, w_out=w_out, norm2_w=norm2_w,
                   w_up=w_up, conv_f_w=conv_f_w, conv_f_b=conv_f_b, w_down=w_down, final_norm_w=final_norm_w)
    mom1 = dict(norm1_w=m_norm1_w, w_in=m_w_in, b_gate=m_b_gate, conv_a_w=m_conv_a_w, conv_a_b=m_conv_a_b,
                dt_bias=m_dt_bias, a_log=m_a_log, d_skip=m_d_skip, ssd_norm_w=m_ssd_norm_w, uv_b=m_uv_b,
                v_ln_w=m_v_ln_w, v_ln_b=m_v_ln_b, w_spatial=m_w_spatial, b_spatial=m_b_spatial, w_branch=m_w_branch,
                w_out=m_w_out, norm2_w=m_norm2_w, w_up=m_w_up, conv_f_w=m_conv_f_w, conv_f_b=m_conv_f_b,
                w_down=m_w_down, final_norm_w=m_final_norm_w)
    mom2 = dict(norm1_w=v_norm1_w, w_in=v_w_in, b_gate=v_b_gate, conv_a_w=v_conv_a_w, conv_a_b=v_conv_a_b,
                dt_bias=v_dt_bias, a_log=v_a_log, d_skip=v_d_skip, ssd_norm_w=v_ssd_norm_w, uv_b=v_uv_b,
                v_ln_w=v_v_ln_w, v_ln_b=v_v_ln_b, w_spatial=v_w_spatial, b_spatial=v_b_spatial, w_branch=v_w_branch,
                w_out=v_w_out, norm2_w=v_norm2_w, w_up=v_w_up, conv_f_w=v_conv_f_w, conv_f_b=v_conv_f_b,
                w_down=v_w_down, final_norm_w=v_final_norm_w)
    chip = 2 * lax.axis_index("x") + lax.axis_index("y")
    core = lax.axis_index("c").astype(jnp.int32).reshape(1)

    whole = {}
    conv_shapes = [weights[n].shape[1:] for n in CONV]
    conv_gathered = _all_gather_chips(_flat_rows([weights[n] for n in CONV], 16), "gather_conv").reshape(N_CHIPS, -1)
    off = 0
    for n, shp in zip(CONV, conv_shapes):
        size = math.prod(shp)
        whole[n] = _from_chip_blocks(conv_gathered[:, off:off + size].reshape((N_CHIPS,) + shp), n)
        off += size
    shard_shapes = {n: weights[n].shape[1:] for n in BIG}
    halves = [weights[n][0].astype(BF16).reshape(2, shard_shapes[n][0] // 2, shard_shapes[n][1]) for n in BIG]
    sends = [_gather_sends if n == "w_in" else _gather_whole_sends for n in BIG]
    gathers, gathers_started = _exchange_start(halves, [(N_CHIPS,) + h.shape for h in halves], sends,
                                               after=conv_gathered, name="gather_start")
    gathers = dict(zip(BIG, gathers))

    def get_weight(name, after):
        rows, cols = shard_shapes[name]
        if name == "w_in":
            own, landed = _exchange_wait(gathers[name], after, _gather_sends, _gather_arrivals,
                                         name="gather_" + name + "_wait")
            landed = _gather_d2d(landed, name="gather_" + name + "_d2d")
        else:
            own, landed = _exchange_wait(gathers[name], after, _gather_whole_sends, _gather_whole_arrivals,
                                         name="gather_" + name + "_wait")
        blocks = lax.dynamic_update_slice(landed.reshape(N_CHIPS, rows, cols), own.reshape(1, rows, cols),
                                          (chip, 0, 0))
        if name == "w_up":
            return {"up": blocks}
        if name == "w_in":
            parts = [(blocks[k], cols * k) for k in range(N_CHIPS)]
            segs = {n: _take_columns(parts, a, b) for n, (a, b) in IN_SEGMENTS.items()}
            segs["in_dt"] = jnp.pad(segs["in_dt"], ((0, 0), (0, LANES - SSD_HEADS)))
            return segs
        full = _from_chip_blocks(blocks, name)
        if name == "w_branch":
            return {"branch_a": full[:SSD_D_INNER], "branch_b": full[SSD_D_INNER:]}
        return {name[2:]: full}

    small = {n: weights[n] for n in REPLICATED}
    small["conv_a_w"] = whole["conv_a_w"]
    small["conv_f_w"] = whole["conv_f_w"]
    small["gathers_started"] = gathers_started

    reductions = {}

    def emit_grad(name, g):
        if name == "w_in":
            parts = [(g[n], IN_SEGMENTS[n][0]) for n in IN_GRAD_SEGMENTS]
            cols = shard_shapes[name][1]
            g_blocks = jnp.stack([_take_columns(parts, cols * k, cols * (k + 1)) for k in range(N_CHIPS)])
        else:
            g_blocks = g if name == "w_up" else _to_chip_blocks(g, name)
        if name == "w_in":
            _, rows, cols = g_blocks.shape
            g_halves = g_blocks.reshape(N_CHIPS, 2, rows // 2, cols)
            arrived = _swap_halves_d2d(g_halves, name="reduce_" + name + "_swap")
            g_blocks = _add_own_half(g_halves, arrived, core, name="reduce_" + name + "_add2")
        own = lax.dynamic_slice_in_dim(g_blocks, chip, 1, axis=0)
        (pending,), started = _exchange_start([g_blocks], [g_blocks.shape], _scatter_sends,
                                              name="reduce_" + name + "_start")
        reductions[name] = (pending, own)
        return started

    loss, dx, grads_small = _local_step(x[0], loss_target[0], get_weight, small, emit_grad)

    order = ("w_down", "w_up", "w_out", "w_branch", "w_in")
    core_sums = []
    for n in order:
        pending, own = reductions[n]
        _, landed = _exchange_wait(pending, dx, _scatter_sends, _scatter_arrivals, name="reduce_" + n + "_wait")
        parts = lax.dynamic_update_slice(landed, own, (chip, 0, 0))
        core_sums.append(_sum_chips(parts, name="reduce_" + n + "_sum4"))
    swaps, swaps_started = _exchange_start(core_sums, [a.shape for a in core_sums], _sibling_sends, name="reduce_swap_start")
    grads = {}

    small_names = REPLICATED + CONV
    small_shapes = [grads_small[n].shape for n in small_names]
    g_small = _flat_rows([grads_small[n] for n in small_names], N_CHIPS * 2 * SMALL_EXCHANGE_ROWS)
    red_small = _reduce_scatter_chips(g_small.reshape(N_CHIPS, -1, LANES), core, "reduce_small", after=swaps_started)
    all_small = _all_gather_chips(red_small, "gather_small")
    swapped = _exchange_wait_many(swaps, all_small, _sibling_sends, _sibling_sends, name="reduce_swap_wait")
    core_sums = {n: own for n, (own, _) in zip(order, swapped)}
    sibling_sums = {n: other for n, (_, other) in zip(order, swapped)}
    first = lax.axis_index("c") == 0
    w_in_halves = (core_sums["w_in"], sibling_sums["w_in"])
    w_in_grad = jnp.concatenate([jnp.where(first, w_in_halves[0], w_in_halves[1]),
                                 jnp.where(first, w_in_halves[1], w_in_halves[0])], axis=0)
    for n, g in zip(small_names, _unflatten(all_small, small_shapes)):
        if n in CONV:
            width = g.shape[1] // N_CHIPS
            g = lax.dynamic_slice_in_dim(g, chip * width, width, axis=1)
        grads[n] = g.reshape(weights[n].shape[1:]) if n != "final_norm_w" else g

    delta, new_m, new_v = {}, {}, {}
    for n in BIG:
        shp = weights[n].shape
        if n == "w_in":
            g_t = w_in_grad.T
            results = [g_t] + list(_adamw(weights[n][0].T, g_t, mom1[n][0].T, mom2[n][0].T, name="adamw_" + n,
                                          tr=_row_tile(g_t.shape[0], 8, 136)))
            results = [a.T for a in results]
        else:
            results = _adamw_two_sums(weights[n][0], core_sums[n], sibling_sums[n], mom1[n][0], mom2[n][0],
                                      name="adamw_" + n, tr=_row_tile(shp[1], 8, 136))
        grads[n], delta[n], new_m[n], new_v[n] = [a.reshape(shp) for a in results]
    small_all = [n for n in WEIGHT_ORDER if n not in BIG]

    def as_2d(a):
        return a.reshape(-1, a.shape[-1])

    results = _adamw_many(*[[as_2d(src[n]) for n in small_all] for src in (weights, grads, mom1, mom2)],
                          name="adamw_small")
    for n, dv, mv, vv in zip(small_all, *results):
        shp = weights[n].shape
        delta[n], new_m[n], new_v[n] = dv.reshape(shp), mv.reshape(shp), vv.reshape(shp)

    total_loss = lax.psum(loss[0, 0], ("x", "y", "c"))
    grad_out = [grads[n].reshape(weights[n].shape) for n in WEIGHT_ORDER]
    return (total_loss, dx[None], *grad_out, *[delta[n] for n in WEIGHT_ORDER], *[new_m[n] for n in WEIGHT_ORDER],
            *[new_v[n] for n in WEIGHT_ORDER])
```

```python
import functools
import math

import jax
import jax.numpy as jnp
from jax import lax
from jax.experimental import pallas as pl
from jax.experimental.pallas import tpu as pltpu

F32 = jnp.float32
BF16 = jnp.bfloat16
HI = lax.Precision.HIGHEST

D_MODEL = 1024
SSD_D_INNER = 2048
SSD_HEADS = 32
SSD_HEAD_DIM = 64
SSD_GROUPS = 4
SSD_HEADS_PER_GROUP = 8
SSD_STATE = 128
SSD_BC = 512
SSD_XBC = 3072
SSD_IN = 5152
SGU_WIDTH = 1024
SGU_GROUPS = 8
CHUNK = 128
IN_COLS = 9248
D_FF = 2816
NORM_EPS = 1e-6
LN_EPS = 1e-5
GROUP_COLS = SSD_HEADS_PER_GROUP * SSD_HEAD_DIM
LANES = 128

ADAM_LR = 0.001
ADAM_B1 = 0.9
ADAM_B2 = 0.999
ADAM_EPS = 1e-08
ADAM_WD = 0.01
ADAM_STEP = 10

N_CHIPS = 4
VMEM_LIMIT = 56 * 1024 * 1024

NT = (((1,), (1,)), ((), ()))
TN = (((0,), (0,)), ((), ()))
NN = (((1,), (0,)), ((), ()))


def _params(dims):
    return pltpu.CompilerParams(dimension_semantics=dims, vmem_limit_bytes=VMEM_LIMIT)


def _dot(a, b, dn=NN, precision=None):
    return lax.dot_general(a, b, dn, precision=precision, preferred_element_type=F32)


def _split3(x):
    hi = x.astype(BF16)
    rest = x - hi.astype(F32)
    mid = rest.astype(BF16)
    return hi, mid, (rest - mid.astype(F32)).astype(BF16)


def _dot_terms(terms, exact, dn=NN):
    out = None
    for t in terms:
        p = _dot(t, exact, dn)
        out = p if out is None else out + p
    return out


def _dot_exact_lhs(exact, terms):
    out = None
    for t in terms:
        p = _dot(exact, t)
        out = p if out is None else out + p
    return out


def _sigmoid(x):
    return 1.0 / (1.0 + jnp.exp(-x))


def _softplus(x):
    return jnp.maximum(x, 0.0) + jnp.log(1.0 + jnp.exp(-jnp.abs(x)))


def _matmul(pairs, *, trans_b=False, add=None, after=None, out_dtype=F32, tm=512, tn=512, name):
    def mat_shape(b):
        if isinstance(b, tuple) and b[1] == "cols":
            return (b[0].shape[1], b[0].shape[0] * b[0].shape[2])
        return b[0].shape[1:] if isinstance(b, tuple) else b.shape

    if isinstance(pairs[0][1], tuple) and pairs[0][1][1] == "cols":
        assert not trans_b and tn % LANES == 0 and pairs[0][1][0].shape[2] % tn == 0, name

    m = (pairs[0][0][0] if isinstance(pairs[0][0], tuple) else pairs[0][0]).shape[0]
    n = mat_shape(pairs[0][1])[0] if trans_b else mat_shape(pairs[0][1])[1]
    tm, tn = min(tm, m), min(tn, n)
    assert m % tm == 0 and n % tn == 0, (name, m, n, tm, tn)
    npairs = len(pairs)
    dn = NT if trans_b else NN

    def body(*refs):
        o_ref = refs[-1]
        acc = None
        for i in range(npairs):
            p = _dot(refs[2 * i][...].astype(BF16), refs[2 * i + 1][...].astype(BF16), dn)
            acc = p if acc is None else acc + p
        if add is not None:
            acc = acc + refs[2 * npairs][...]
        o_ref[...] = acc.astype(out_dtype)

    in_specs, args = [], []
    for a, b in pairs:
        bshape = mat_shape(b)
        k = bshape[1] if trans_b else bshape[0]
        assert bshape == ((n, k) if trans_b else (k, n)), (name, bshape)
        a, qa = a if isinstance(a, tuple) else (a, 0)
        assert a.shape[0] == m and a.shape[1] % k == 0, (name, a.shape, k)
        in_specs.append(pl.BlockSpec((tm, k), lambda i, j, qa=qa: (i, qa)))
        if isinstance(b, tuple) and b[1] == "cols":
            b = b[0]
            per = b.shape[2] // tn
            in_specs.append(pl.BlockSpec((None, k, tn), lambda i, j, per=per: (j // per, 0, j % per)))
        elif isinstance(b, tuple):
            b, qb = b
            if trans_b:
                in_specs.append(pl.BlockSpec((None, tn, k), lambda i, j, qb=qb: (qb, j, 0)))
            else:
                in_specs.append(pl.BlockSpec((None, k, tn), lambda i, j, qb=qb: (qb, 0, j)))
        elif trans_b:
            in_specs.append(pl.BlockSpec((tn, k), lambda i, j: (j, 0)))
        else:
            in_specs.append(pl.BlockSpec((k, tn), lambda i, j: (0, j)))
        args += [a, b]
    if add is not None:
        in_specs.append(pl.BlockSpec((tm, tn), lambda i, j: (i, j)))
        args.append(add)
    if after is not None:
        in_specs.append(pl.BlockSpec(memory_space=pl.ANY))
        args.append(after)
    return pl.pallas_call(
        body, name=name, grid=(m // tm, n // tn), in_specs=in_specs,
        out_specs=pl.BlockSpec((tm, tn), lambda i, j: (i, j)),
        out_shape=jax.ShapeDtypeStruct((m, n), out_dtype),
        compiler_params=_params(("parallel", "parallel")),
    )(*args)


def _matmul_tn(a, b, *, tk, tn, tm=1024, out_dtype=BF16, stack_out=False, after=None, name):
    m, k = a.shape
    n = b.shape[1]
    tm, tk, tn = min(tm, m), min(tk, k), min(tn, n)
    assert m % tm == 0 and k % tk == 0 and n % tn == 0, (name, m, k, n)
    nm = m // tm
    if stack_out:
        out_spec = pl.BlockSpec((None, tk, tn), lambda i, j, l: (j, i, 0))
        out_shape = jax.ShapeDtypeStruct((n // tn, k, tn), out_dtype)
    else:
        out_spec = pl.BlockSpec((tk, tn), lambda i, j, l: (i, j))
        out_shape = jax.ShapeDtypeStruct((k, n), out_dtype)

    def body(a_ref, b_ref, *rest):
        o_ref, acc = rest[-2:]
        mi = pl.program_id(2)

        @pl.when(mi == 0)
        def _():
            acc[...] = jnp.zeros_like(acc)

        acc[...] += _dot(a_ref[...].astype(BF16), b_ref[...].astype(BF16), TN)

        @pl.when(mi == nm - 1)
        def _():
            o_ref[...] = acc[...].astype(out_dtype)

    in_specs = [pl.BlockSpec((tm, tk), lambda i, j, l: (l, i)), pl.BlockSpec((tm, tn), lambda i, j, l: (l, j))]
    args = [a, b]
    if after is not None:
        in_specs.append(pl.BlockSpec(memory_space=pl.ANY))
        args.append(after)
    return pl.pallas_call(
        body, name=name, grid=(k // tk, n // tn, nm), in_specs=in_specs,
        out_specs=out_spec, out_shape=out_shape,
        scratch_shapes=[pltpu.VMEM((tk, tn), F32)],
        compiler_params=_params(("parallel", "parallel", "arbitrary")),
    )(*args)


def _rms_fwd(x, w, *, after=None, name, tm=512):
    s, d = x.shape
    tm = min(tm, s)
    extra = [] if after is None else [after]

    def body(x_ref, w_ref, *rest):
        o_ref = rest[-1]
        xv = x_ref[...]
        r = lax.rsqrt(jnp.mean(xv * xv, axis=-1, keepdims=True) + NORM_EPS)
        o_ref[...] = (xv * r * w_ref[...]).astype(BF16)

    return pl.pallas_call(
        body, name=name, grid=(s // tm,),
        in_specs=[pl.BlockSpec((tm, d), lambda i: (i, 0)), pl.BlockSpec((1, d), lambda i: (0, 0))]
        + [pl.BlockSpec(memory_space=pl.ANY)] * len(extra),
        out_specs=pl.BlockSpec((tm, d), lambda i: (i, 0)),
        out_shape=jax.ShapeDtypeStruct((s, d), BF16),
        compiler_params=_params(("parallel",)),
    )(x, w, *extra)


def _rms_bwd(x, w, dn, dres, *, name, tm=512):
    s, d = x.shape
    tm = min(tm, s)

    def body(x_ref, w_ref, dn_ref, dres_ref, dx_ref, dxb_ref, dw_ref):
        @pl.when(pl.program_id(0) == 0)
        def _():
            dw_ref[...] = jnp.zeros_like(dw_ref)

        xv = x_ref[...]
        r = lax.rsqrt(jnp.mean(xv * xv, axis=-1, keepdims=True) + NORM_EPS)
        xhat = xv * r
        dnv = dn_ref[...].astype(F32)
        dxhat = dnv * w_ref[...]
        dx = dres_ref[...] + r * (dxhat - xhat * jnp.mean(dxhat * xhat, axis=-1, keepdims=True))
        dx_ref[...] = dx
        dxb_ref[...] = dx.astype(BF16)
        dw_ref[...] += jnp.sum(dnv * xhat, axis=0, keepdims=True)

    tile = pl.BlockSpec((tm, d), lambda i: (i, 0))
    row = pl.BlockSpec((1, d), lambda i: (0, 0))
    return pl.pallas_call(
        body, name=name, grid=(s // tm,),
        in_specs=[tile, row, tile, tile], out_specs=[tile, tile, row],
        out_shape=[jax.ShapeDtypeStruct((s, d), F32), jax.ShapeDtypeStruct((s, d), BF16),
                   jax.ShapeDtypeStruct((1, d), F32)],
        compiler_params=_params(("arbitrary",)),
    )(x, w, dn, dres)


def _final_fwd_bwd(h2, wf, target, *, name, tm=512):
    s, d = h2.shape
    tm = min(tm, s)

    def body(h_ref, w_ref, t_ref, loss_ref, dh_ref, dhb_ref, dw_ref):
        @pl.when(pl.program_id(0) == 0)
        def _():
            dw_ref[...] = jnp.zeros_like(dw_ref)
            loss_ref[...] = jnp.zeros_like(loss_ref)

        hv = h_ref[...]
        r = lax.rsqrt(jnp.mean(hv * hv, axis=-1, keepdims=True) + NORM_EPS)
        xhat = hv * r
        err = xhat * w_ref[...] - t_ref[...]
        per_tok = jnp.mean(err * err, axis=-1, keepdims=True)
        loss_ref[...] += 0.5 * jnp.sum(per_tok, axis=0, keepdims=True)
        dy = err * (1.0 / d)
        dxhat = dy * w_ref[...]
        dh = r * (dxhat - xhat * jnp.mean(dxhat * xhat, axis=-1, keepdims=True))
        dh_ref[...] = dh
        dhb_ref[...] = dh.astype(BF16)
        dw_ref[...] += jnp.sum(dy * xhat, axis=0, keepdims=True)

    tile = pl.BlockSpec((tm, d), lambda i: (i, 0))
    row = pl.BlockSpec((1, d), lambda i: (0, 0))
    return pl.pallas_call(
        body, name=name, grid=(s // tm,),
        in_specs=[tile, row, tile],
        out_specs=[pl.BlockSpec((1, 1), lambda i: (0, 0)), tile, tile, row],
        out_shape=[jax.ShapeDtypeStruct((1, 1), F32), jax.ShapeDtypeStruct((s, d), F32),
                   jax.ShapeDtypeStruct((s, d), BF16), jax.ShapeDtypeStruct((1, d), F32)],
        compiler_params=_params(("arbitrary",)),
    )(h2, wf, target)


CONV_ROWS = 512
HALO = 8


def _rows_with_halo(ref, r0, rows, s, before, after):
    tile = 16 if ref.dtype == BF16 else HALO
    parts = []
    if before:
        prev = ref[pl.ds(pl.multiple_of(jnp.maximum(r0 - tile, 0), tile), tile), :].astype(F32)[tile - HALO:]
        parts.append(jnp.where(r0 > 0, prev, 0.0))
    parts.append(ref[pl.ds(r0, rows), :].astype(F32))
    if after:
        nxt = ref[pl.ds(pl.multiple_of(jnp.minimum(r0 + rows, s - tile), tile), tile), :].astype(F32)[:HALO]
        parts.append(jnp.where(r0 + rows < s, nxt, 0.0))
    return jnp.concatenate(parts, axis=0) if len(parts) > 1 else parts[0]


def _window(x_ref, r0, s, after):
    return _rows_with_halo(x_ref, r0, CONV_ROWS, s, True, after).astype(F32)


def _shifted(window, k, rows):
    if k == 0:
        return window[HALO:HALO + rows]
    return pltpu.roll(window, k, 0)[HALO:HALO + rows]


def _conv_taps(window, w_ref, kk, rows):
    acc = None
    for i in range(kk):
        term = w_ref[i:i + 1, :] * _shifted(window, kk - 1 - i, rows)
        acc = term if acc is None else acc + term
    return acc


def _row_loop(s, step):
    def body(r, carry):
        return step(pl.multiple_of(r * CONV_ROWS, CONV_ROWS), carry)
    return body


def _conv_bwd_rows(window, dpe, w_ref, kk):
    dp = dpe[:CONV_ROWS]
    dx = None
    dws = []
    for i in range(kk):
        k = kk - 1 - i
        dws.append(jnp.sum(dp * _shifted(window, k, CONV_ROWS), axis=0, keepdims=True))
        later = dp if k == 0 else pltpu.roll(dpe, dpe.shape[0] - k, 0)[:CONV_ROWS]
        term = w_ref[i:i + 1, :] * later
        dx = term if dx is None else dx + term
    return dx, dws, jnp.sum(dp, axis=0, keepdims=True)


def _conv_a_fwd(xraw, w, b, *, name, tc=128):
    s, c = xraw.shape
    kk = 4

    def body(x_ref, w_ref, b_ref, o_ref):
        def step(r0, carry):
            pre = _conv_taps(_window(x_ref, r0, s, False), w_ref, kk, CONV_ROWS) + b_ref[...]
            o_ref[pl.ds(r0, CONV_ROWS), :] = pre * _sigmoid(pre)
            return carry

        lax.fori_loop(0, s // CONV_ROWS, _row_loop(s, step), 0)

    col = pl.BlockSpec((s, tc), lambda j: (0, j))
    return pl.pallas_call(
        body, name=name, grid=(c // tc,),
        in_specs=[col, pl.BlockSpec((8, tc), lambda j: (0, j)), pl.BlockSpec((1, tc), lambda j: (0, j))],
        out_specs=col, out_shape=jax.ShapeDtypeStruct((s, c), F32),
        compiler_params=_params(("parallel",)),
    )(xraw, w, b)


def _conv_a_bwd(xraw, w, b, dy, *, name, tc=128):
    s, c = xraw.shape
    kk = 4

    def body(x_ref, w_ref, b_ref, dy_ref, dx_ref, dw_ref, db_ref):
        def step(r0, carry):
            window = _window(x_ref, r0, s, True)
            pre = _conv_taps(window, w_ref, kk, CONV_ROWS + HALO) + b_ref[...]
            sg = _sigmoid(pre)
            dpe = _rows_with_halo(dy_ref, r0, CONV_ROWS, s, False, True) * (sg * (1.0 + pre * (1.0 - sg)))
            dx, dws, db = _conv_bwd_rows(window, dpe, w_ref, kk)
            dx_ref[pl.ds(r0, CONV_ROWS), :] = dx.astype(BF16)
            return tuple(acc + new for acc, new in zip(carry, dws + [db]))

        zero = jnp.zeros((1, tc), F32)
        sums = lax.fori_loop(0, s // CONV_ROWS, _row_loop(s, step), (zero,) * (kk + 1))
        db_ref[...] = sums[kk]
        dw_ref[...] = jnp.concatenate(list(sums[:kk]) + [jnp.zeros((8 - kk, tc), F32)], axis=0)

    col = pl.BlockSpec((s, tc), lambda j: (0, j))
    w8 = pl.BlockSpec((8, tc), lambda j: (0, j))
    row = pl.BlockSpec((1, tc), lambda j: (0, j))
    return pl.pallas_call(
        body, name=name, grid=(c // tc,),
        in_specs=[col, w8, row, col], out_specs=[col, w8, row],
        out_shape=[jax.ShapeDtypeStruct((s, c), BF16), jax.ShapeDtypeStruct((8, c), F32),
                   jax.ShapeDtypeStruct((1, c), F32)],
        compiler_params=_params(("parallel",)),
    )(xraw, w, b, dy)


def _conv_f_fwd(up_raw, w, b, *, name, tc=128):
    s, c2 = up_raw.shape
    c = c2 // 2
    nb = c // tc
    kk = 3

    def body(xa_ref, xv_ref, wa_ref, wv_ref, ba_ref, bv_ref, o_ref):
        def step(r0, carry):
            a = _conv_taps(_window(xa_ref, r0, s, False), wa_ref, kk, CONV_ROWS) + ba_ref[...]
            v = _conv_taps(_window(xv_ref, r0, s, False), wv_ref, kk, CONV_ROWS) + bv_ref[...]
            o_ref[pl.ds(r0, CONV_ROWS), :] = (a * _sigmoid(a) * v).astype(BF16)
            return carry

        lax.fori_loop(0, s // CONV_ROWS, _row_loop(s, step), 0)

    col_a = pl.BlockSpec((s, tc), lambda j: (0, j))
    col_v = pl.BlockSpec((s, tc), lambda j: (0, j + nb))
    return pl.pallas_call(
        body, name=name, grid=(nb,),
        in_specs=[col_a, col_v, pl.BlockSpec((8, tc), lambda j: (0, j)), pl.BlockSpec((8, tc), lambda j: (0, j + nb)),
                  pl.BlockSpec((1, tc), lambda j: (0, j)), pl.BlockSpec((1, tc), lambda j: (0, j + nb))],
        out_specs=col_a, out_shape=jax.ShapeDtypeStruct((s, c), BF16),
        compiler_params=_params(("parallel",)),
    )(up_raw, up_raw, w, w, b, b)


def _conv_f_bwd(up_raw, w, b, dact, *, name, tc=128):
    s, c2 = up_raw.shape
    c = c2 // 2
    nb = c // tc
    kk = 3

    def body(xa_ref, xv_ref, wa_ref, wv_ref, ba_ref, bv_ref, d_ref,
             dxa_ref, dxv_ref, dwa_ref, dwv_ref, dba_ref, dbv_ref):
        def step(r0, carry):
            win_a = _window(xa_ref, r0, s, True)
            win_v = _window(xv_ref, r0, s, True)
            a = _conv_taps(win_a, wa_ref, kk, CONV_ROWS + HALO) + ba_ref[...]
            v = _conv_taps(win_v, wv_ref, kk, CONV_ROWS + HALO) + bv_ref[...]
            sg = _sigmoid(a)
            d = _rows_with_halo(d_ref, r0, CONV_ROWS, s, False, True)
            dxa, dwas, dba = _conv_bwd_rows(win_a, d * v * (sg * (1.0 + a * (1.0 - sg))), wa_ref, kk)
            dxv, dwvs, dbv = _conv_bwd_rows(win_v, d * (a * sg), wv_ref, kk)
            dxa_ref[pl.ds(r0, CONV_ROWS), :] = dxa.astype(BF16)
            dxv_ref[pl.ds(r0, CONV_ROWS), :] = dxv.astype(BF16)
            return tuple(acc + new for acc, new in zip(carry, dwas + [dba] + dwvs + [dbv]))

        zero = jnp.zeros((1, tc), F32)
        sums = lax.fori_loop(0, s // CONV_ROWS, _row_loop(s, step), (zero,) * (2 * kk + 2))
        pad = [jnp.zeros((8 - kk, tc), F32)]
        dwa_ref[...] = jnp.concatenate(list(sums[:kk]) + pad, axis=0)
        dba_ref[...] = sums[kk]
        dwv_ref[...] = jnp.concatenate(list(sums[kk + 1:2 * kk + 1]) + pad, axis=0)
        dbv_ref[...] = sums[2 * kk + 1]

    col_a = pl.BlockSpec((s, tc), lambda j: (0, j))
    col_v = pl.BlockSpec((s, tc), lambda j: (0, j + nb))
    w_a = pl.BlockSpec((8, tc), lambda j: (0, j))
    w_v = pl.BlockSpec((8, tc), lambda j: (0, j + nb))
    r_a = pl.BlockSpec((1, tc), lambda j: (0, j))
    r_v = pl.BlockSpec((1, tc), lambda j: (0, j + nb))
    outs = pl.pallas_call(
        body, name=name, grid=(nb,),
        in_specs=[col_a, col_v, w_a, w_v, r_a, r_v, col_a],
        out_specs=[col_a, col_a, w_a, w_a, r_a, r_a],
        out_shape=[jax.ShapeDtypeStruct((s, c), BF16), jax.ShapeDtypeStruct((s, c), BF16),
                   jax.ShapeDtypeStruct((8, c), F32), jax.ShapeDtypeStruct((8, c), F32),
                   jax.ShapeDtypeStruct((1, c), F32), jax.ShapeDtypeStruct((1, c), F32)],
        compiler_params=_params(("parallel",)),
    )(up_raw, up_raw, w, w, b, b, dact)
    return outs


def _tri_masks():
    row = lax.broadcasted_iota(jnp.int32, (CHUNK, CHUNK), 0)
    col = lax.broadcasted_iota(jnp.int32, (CHUNK, CHUNK), 1)
    return row >= col, row <= col


def _ssd_fwd(xbc, dt_raw, z, dt_bias, a_log, a_log_x, d_skip_x, norm_w, expand, *, name):
    s = xbc.shape[0]
    nc = s // CHUNK

    def body(xbc_ref, dtr_ref, z_ref, dtb_ref, alog_ref, alogx_ref, dskx_ref, nw_ref, e_ref,
             y_ref, ya_ref, st_ref, state):
        @pl.when(pl.program_id(0) == 0)
        def _():
            state[...] = jnp.zeros_like(state)

        st_ref[0] = state[...]
        lower, _ = _tri_masks()
        dt = _softplus(dtr_ref[...] + dtb_ref[...])
        adt = dt * (-jnp.exp(alog_ref[...]))
        acum = _dot_exact_lhs(lower.astype(BF16), _split3(adt))
        acum_t = acum.T
        dt_terms, acum_terms = _split3(dt), _split3(acum)
        for g in range(SSD_GROUPS):
            sl = slice(GROUP_COLS * g, GROUP_COLS * (g + 1))
            dt_x = _dot_terms(dt_terms, e_ref[:, sl])
            acum_x = _dot_terms(acum_terms, e_ref[:, sl])
            tot_x = jnp.sum(dt_x * (-jnp.exp(alogx_ref[:, sl])), axis=0, keepdims=True)
            xs = xbc_ref[:, sl]
            xdt = xs * dt_x
            xdt_b = xdt.astype(BF16)
            bg = xbc_ref[:, SSD_D_INNER + SSD_STATE * g:SSD_D_INNER + SSD_STATE * (g + 1)].astype(BF16)
            cg = xbc_ref[:, SSD_D_INNER + SSD_BC + SSD_STATE * g:SSD_D_INNER + SSD_BC + SSD_STATE * (g + 1)].astype(BF16)
            cb = _dot(cg, bg, NT)
            st_g = state[:, sl]
            y_off = _dot(cg, st_g.astype(BF16)) * jnp.exp(acum_x)
            parts = []
            for r in range(SSD_HEADS_PER_GROUP):
                h = SSD_HEADS_PER_GROUP * g + r
                dec = jnp.exp(jnp.where(lower, acum[:, h:h + 1] - acum_t[h:h + 1, :], -jnp.inf))
                parts.append(_dot((cb * dec).astype(BF16), xdt_b[:, SSD_HEAD_DIM * r:SSD_HEAD_DIM * (r + 1)]))
            y_ref[:, sl] = jnp.concatenate(parts, axis=1) + y_off + dskx_ref[:, sl] * xs
            wgt = (xdt * jnp.exp(tot_x - acum_x)).astype(BF16)
            state[:, sl] = st_g * jnp.exp(tot_x) + _dot(bg, wgt, TN)
        zv = z_ref[...].astype(F32)
        q = y_ref[...] * (zv * _sigmoid(zv))
        r = lax.rsqrt(jnp.mean(q * q, axis=-1, keepdims=True) + NORM_EPS)
        ya_ref[...] = (q * r * nw_ref[...]).astype(BF16)

    def chunk(w):
        return pl.BlockSpec((CHUNK, w), lambda c: (c, 0))

    def const(shape):
        return pl.BlockSpec(shape, lambda c: (0,) * len(shape))

    return pl.pallas_call(
        body, name=name, grid=(nc,),
        in_specs=[chunk(SSD_XBC), chunk(LANES), chunk(SSD_D_INNER), const((1, LANES)), const((1, LANES)),
                  const((1, SSD_D_INNER)), const((1, SSD_D_INNER)), const((1, SSD_D_INNER)),
                  const((LANES, SSD_D_INNER))],
        out_specs=[chunk(SSD_D_INNER), chunk(SSD_D_INNER),
                   pl.BlockSpec((1, SSD_STATE, SSD_D_INNER), lambda c: (c, 0, 0))],
        out_shape=[jax.ShapeDtypeStruct((s, SSD_D_INNER), F32), jax.ShapeDtypeStruct((s, SSD_D_INNER), BF16),
                   jax.ShapeDtypeStruct((nc, SSD_STATE, SSD_D_INNER), F32)],
        scratch_shapes=[pltpu.VMEM((SSD_STATE, SSD_D_INNER), F32)],
        compiler_params=_params(("arbitrary",)),
    )(xbc, dt_raw, z, dt_bias, a_log, a_log_x, d_skip_x, norm_w, expand)


def _ssd_bwd(dya, y, z, xbc, dt_raw, states, dt_bias, a_log, a_log_x, d_skip_x, norm_w, expand, expand_t, *, name):
    s = xbc.shape[0]
    nc = s // CHUNK

    def body(dya_ref, y_ref, z_ref, xbc_ref, dtr_ref, stp_ref, dtb_ref, alog_ref, alogx_ref, dskx_ref, nw_ref,
             e_ref, et_ref, dz_ref, dxbc_ref, ddt_ref, dnw_ref, ddsk_ref, dalog_ref, ddtb_ref,
             dstate, dy_sc, dskcol):
        i = pl.program_id(0)

        @pl.when(i == 0)
        def _():
            dstate[...] = jnp.zeros_like(dstate)
            dskcol[...] = jnp.zeros_like(dskcol)
            dnw_ref[...] = jnp.zeros_like(dnw_ref)
            dalog_ref[...] = jnp.zeros_like(dalog_ref)
            ddtb_ref[...] = jnp.zeros_like(ddtb_ref)
            ddsk_ref[...] = jnp.zeros_like(ddsk_ref)

        lower, upper = _tri_masks()
        rows = lax.broadcasted_iota(jnp.int32, (CHUNK, LANES), 0)
        pre = dtr_ref[...] + dtb_ref[...]
        dt = _softplus(pre)
        a = -jnp.exp(alog_ref[...])
        acum = _dot_exact_lhs(lower.astype(BF16), _split3(dt * a))
        acum_t = acum.T
        dt_terms, acum_terms = _split3(dt), _split3(acum)

        yv = y_ref[...]
        zv = z_ref[...].astype(F32)
        sz = _sigmoid(zv)
        silu_z = zv * sz
        q = yv * silu_z
        r = lax.rsqrt(jnp.mean(q * q, axis=-1, keepdims=True) + NORM_EPS)
        qhat = q * r
        dyav = dya_ref[...]
        dqhat = dyav * nw_ref[...]
        dnw_ref[...] += jnp.sum(dyav * qhat, axis=0, keepdims=True)
        dq = r * (dqhat - qhat * jnp.mean(dqhat * qhat, axis=-1, keepdims=True))
        dy_sc[...] = dq * silu_z
        dz_ref[...] = (dq * yv * (sz * (1.0 + zv * (1.0 - sz)))).astype(BF16)

        da_cum = jnp.zeros((CHUNK, LANES), F32)
        ddt = jnp.zeros((CHUNK, LANES), F32)
        for g in range(SSD_GROUPS):
            sl = slice(GROUP_COLS * g, GROUP_COLS * (g + 1))
            et_g = et_ref[sl, :]
            dt_x = _dot_terms(dt_terms, e_ref[:, sl])
            acum_x = _dot_terms(acum_terms, e_ref[:, sl])
            tot_x = jnp.sum(dt_x * (-jnp.exp(alogx_ref[:, sl])), axis=0, keepdims=True)
            e_tot = jnp.exp(tot_x)
            dec_s = jnp.exp(tot_x - acum_x)
            xs = xbc_ref[:, sl]
            xdt = xs * dt_x
            xdt_b = xdt.astype(BF16)
            dy = dy_sc[:, sl]
            dy_b = dy.astype(BF16)
            dskx = dskx_ref[:, sl]
            y_ssd = y_ref[:, sl] - dskx * xs
            dskcol[:, sl] += jnp.sum(dy * xs, axis=0, keepdims=True)
            bg = xbc_ref[:, SSD_D_INNER + SSD_STATE * g:SSD_D_INNER + SSD_STATE * (g + 1)].astype(BF16)
            cg = xbc_ref[:, SSD_D_INNER + SSD_BC + SSD_STATE * g:SSD_D_INNER + SSD_BC + SSD_STATE * (g + 1)].astype(BF16)
            cb_t = _dot(bg, cg, NT)
            sp = stp_ref[0, :, sl]
            ds_g = dstate[:, sl]
            ds_b = ds_g.astype(BF16)
            dye_b = (dy * jnp.exp(acum_x)).astype(BF16)
            dc = _dot(dye_b, sp.astype(BF16), NT)
            dxdt_state = dec_s * _dot(bg, ds_b)
            db = _dot((xdt * dec_s).astype(BF16), ds_b, NT)
            dcb_t = jnp.zeros((CHUNK, CHUNK), F32)
            parts = []
            for rr in range(SSD_HEADS_PER_GROUP):
                h = SSD_HEADS_PER_GROUP * g + rr
                hs = slice(SSD_HEAD_DIM * rr, SSD_HEAD_DIM * (rr + 1))
                dec_t = jnp.exp(jnp.where(upper, acum_t[h:h + 1, :] - acum[:, h:h + 1], -jnp.inf))
                parts.append(_dot((cb_t * dec_t).astype(BF16), dy_b[:, hs]))
                dcb_t = dcb_t + _dot(xdt_b[:, hs], dy_b[:, hs], NT) * dec_t
            dxdt = jnp.concatenate(parts, axis=1) + dxdt_state
            dcb_tb = dcb_t.astype(BF16)
            dc = dc + _dot(dcb_tb, bg, TN)
            db = db + _dot(dcb_tb, cg)
            tot_col = jnp.sum(ds_g * sp, axis=0, keepdims=True) * e_tot + jnp.sum(dxdt_state * xdt, axis=0, keepdims=True)
            d_tot = _dot_terms(_split3(jnp.broadcast_to(tot_col, (8, GROUP_COLS))), et_g)
            d_tot = jnp.max(d_tot, axis=0, keepdims=True)
            pair_sums = dy_b.astype(F32) * y_ssd - xdt_b.astype(F32) * dxdt
            da_cum = da_cum + _dot_terms(_split3(pair_sums), et_g) + jnp.where(rows == CHUNK - 1, d_tot, 0.0)
            ddt = ddt + _dot_terms(_split3(dxdt * xs), et_g)
            dxbc_ref[:, sl] = dy * dskx + dxdt * dt_x
            dxbc_ref[:, SSD_D_INNER + SSD_STATE * g:SSD_D_INNER + SSD_STATE * (g + 1)] = db
            dxbc_ref[:, SSD_D_INNER + SSD_BC + SSD_STATE * g:SSD_D_INNER + SSD_BC + SSD_STATE * (g + 1)] = dc
            dstate[:, sl] = e_tot * ds_g + _dot(cg, dye_b, TN)

        dadt = _dot_exact_lhs(upper.astype(BF16), _split3(da_cum))
        ddt = ddt + dadt * a
        dalog_ref[...] += jnp.sum(dadt * dt, axis=0, keepdims=True)
        dpre = ddt * _sigmoid(pre)
        ddtb_ref[...] += jnp.sum(dpre, axis=0, keepdims=True)
        ddt_ref[...] = dpre.astype(BF16)

        @pl.when(i == nc - 1)
        def _():
            dalog_ref[...] = dalog_ref[...] * a
            dsk = _dot_terms(_split3(jnp.broadcast_to(dskcol[...], (8, SSD_D_INNER))), et_ref[...])
            ddsk_ref[...] = jnp.max(dsk, axis=0, keepdims=True)

    def chunk(w):
        return pl.BlockSpec((CHUNK, w), lambda i: (nc - 1 - i, 0))

    def const(shape):
        return pl.BlockSpec(shape, lambda i: (0,) * len(shape))

    return pl.pallas_call(
        body, name=name, grid=(nc,),
        in_specs=[chunk(SSD_D_INNER), chunk(SSD_D_INNER), chunk(SSD_D_INNER), chunk(SSD_XBC), chunk(LANES),
                  pl.BlockSpec((1, SSD_STATE, SSD_D_INNER), lambda i: (nc - 1 - i, 0, 0)),
                  const((1, LANES)), const((1, LANES)), const((1, SSD_D_INNER)), const((1, SSD_D_INNER)),
                  const((1, SSD_D_INNER)), const((LANES, SSD_D_INNER)), const((SSD_D_INNER, LANES))],
        out_specs=[chunk(SSD_D_INNER), chunk(SSD_XBC), chunk(LANES), const((1, SSD_D_INNER)), const((1, LANES)),
                   const((1, LANES)), const((1, LANES))],
        out_shape=[jax.ShapeDtypeStruct((s, SSD_D_INNER), BF16), jax.ShapeDtypeStruct((s, SSD_XBC), F32),
                   jax.ShapeDtypeStruct((s, LANES), BF16), jax.ShapeDtypeStruct((1, SSD_D_INNER), F32),
                   jax.ShapeDtypeStruct((1, LANES), F32), jax.ShapeDtypeStruct((1, LANES), F32),
                   jax.ShapeDtypeStruct((1, LANES), F32)],
        scratch_shapes=[pltpu.VMEM((SSD_STATE, SSD_D_INNER), F32), pltpu.VMEM((CHUNK, SSD_D_INNER), F32),
                        pltpu.VMEM((1, SSD_D_INNER), F32)],
        compiler_params=_params(("arbitrary",)),
    )(dya, y, z, xbc, dt_raw, states, dt_bias, a_log, a_log_x, d_skip_x, norm_w, expand, expand_t)


GELU_K = math.sqrt(2.0 / math.pi)
GELU_C = 0.044715


def _gelu(x):
    return 0.5 * x * (1.0 + jnp.tanh(GELU_K * (x + GELU_C * x * x * x)))


def _gelu_grad(x):
    t = jnp.tanh(GELU_K * (x + GELU_C * x * x * x))
    return 0.5 * (1.0 + t) + 0.5 * x * (1.0 - t * t) * (GELU_K * (1.0 + 3.0 * GELU_C * x * x))


def _sgu_pre(uv_ref, uvb_ref, lnw_ref, lnb_ref):
    uv = uv_ref[...].astype(F32) + uvb_ref[...]
    guv = _gelu(uv)
    u = guv[:, :SGU_WIDTH]
    v = guv[:, SGU_WIDTH:]
    mu = jnp.mean(v, axis=-1, keepdims=True)
    vc = v - mu
    rstd = lax.rsqrt(jnp.mean(vc * vc, axis=-1, keepdims=True) + LN_EPS)
    vhat = vc * rstd
    vn = vhat * lnw_ref[...] + lnb_ref[...]
    return uv, u, vhat, rstd, vn


def _sgu_fwd(uv_raw, uv_b, ln_w, ln_b, w_sp, b_sp_t, *, name):
    s = uv_raw.shape[0]
    nc = s // CHUNK

    def body(uv_ref, uvb_ref, lnw_ref, lnb_ref, w_ref, bt_ref, o_ref):
        lower, _ = _tri_masks()
        _, u, _, _, vn = _sgu_pre(uv_ref, uvb_ref, lnw_ref, lnb_ref)
        vn_b = vn.astype(BF16)
        bt = bt_ref[...]
        for g in range(SGU_GROUPS):
            gs = slice(LANES * g, LANES * (g + 1))
            wc = jnp.where(lower, w_ref[g], 0.0).astype(BF16)
            mixed = _dot(wc, vn_b[:, gs]) + bt[:, g:g + 1]
            o_ref[:, gs] = (u[:, gs] * mixed).astype(BF16)

    def const(shape):
        return pl.BlockSpec(shape, lambda c: (0,) * len(shape))

    return pl.pallas_call(
        body, name=name, grid=(nc,),
        in_specs=[pl.BlockSpec((CHUNK, 2 * SGU_WIDTH), lambda c: (c, 0)), const((1, 2 * SGU_WIDTH)),
                  const((1, SGU_WIDTH)), const((1, SGU_WIDTH)), const((SGU_GROUPS, CHUNK, CHUNK)),
                  const((CHUNK, LANES))],
        out_specs=pl.BlockSpec((CHUNK, SGU_WIDTH), lambda c: (c, 0)),
        out_shape=jax.ShapeDtypeStruct((s, SGU_WIDTH), BF16),
        compiler_params=_params(("parallel",)),
    )(uv_raw, uv_b, ln_w, ln_b, w_sp, b_sp_t)


def _sgu_bwd(uv_raw, dyb, uv_b, ln_w, ln_b, w_sp, b_sp_t, group_sum, *, name):
    s = uv_raw.shape[0]
    nc = s // CHUNK

    def body(uv_ref, dy_ref, uvb_ref, lnw_ref, lnb_ref, w_ref, bt_ref, gsum_ref,
             duv_ref, dw_ref, dbt_ref, dlnw_ref, dlnb_ref, duvb_ref):
        @pl.when(pl.program_id(0) == 0)
        def _():
            dw_ref[...] = jnp.zeros_like(dw_ref)
            dbt_ref[...] = jnp.zeros_like(dbt_ref)
            dlnw_ref[...] = jnp.zeros_like(dlnw_ref)
            dlnb_ref[...] = jnp.zeros_like(dlnb_ref)
            duvb_ref[...] = jnp.zeros_like(duvb_ref)

        lower, _ = _tri_masks()
        uv, u, vhat, rstd, vn = _sgu_pre(uv_ref, uvb_ref, lnw_ref, lnb_ref)
        vn_b = vn.astype(BF16)
        bt = bt_ref[...]
        dy = dy_ref[...].astype(F32)
        du_parts, dvn_parts, dmix_parts = [], [], []
        for g in range(SGU_GROUPS):
            gs = slice(LANES * g, LANES * (g + 1))
            wc = jnp.where(lower, w_ref[g], 0.0).astype(BF16)
            mixed = _dot(wc, vn_b[:, gs]) + bt[:, g:g + 1]
            du_parts.append(dy[:, gs] * mixed)
            dmix = dy[:, gs] * u[:, gs]
            dmix_b = dmix.astype(BF16)
            dmix_parts.append(dmix)
            dw_ref[g] += jnp.where(lower, _dot(dmix_b, vn_b[:, gs], NT), 0.0)
            dvn_parts.append(_dot(wc, dmix_b, TN))
        dmixed = jnp.concatenate(dmix_parts, axis=1)
        dbt_ref[...] += _dot_terms(_split3(dmixed), gsum_ref[...])
        dvn = jnp.concatenate(dvn_parts, axis=1)
        dlnw_ref[...] += jnp.sum(dvn * vhat, axis=0, keepdims=True)
        dlnb_ref[...] += jnp.sum(dvn, axis=0, keepdims=True)
        dvhat = dvn * lnw_ref[...]
        dv = rstd * (dvhat - jnp.mean(dvhat, axis=-1, keepdims=True)
                     - vhat * jnp.mean(dvhat * vhat, axis=-1, keepdims=True))
        dguv = jnp.concatenate(du_parts + [dv], axis=1)
        duv = dguv * _gelu_grad(uv)
        duvb_ref[...] += jnp.sum(duv, axis=0, keepdims=True)
        duv_ref[...] = duv.astype(BF16)

    def const(shape):
        return pl.BlockSpec(shape, lambda c: (0,) * len(shape))

    return pl.pallas_call(
        body, name=name, grid=(nc,),
        in_specs=[pl.BlockSpec((CHUNK, 2 * SGU_WIDTH), lambda c: (c, 0)),
                  pl.BlockSpec((CHUNK, SGU_WIDTH), lambda c: (c, 0)), const((1, 2 * SGU_WIDTH)),
                  const((1, SGU_WIDTH)), const((1, SGU_WIDTH)), const((SGU_GROUPS, CHUNK, CHUNK)),
                  const((CHUNK, LANES)), const((SGU_WIDTH, LANES))],
        out_specs=[pl.BlockSpec((CHUNK, 2 * SGU_WIDTH), lambda c: (c, 0)), const((SGU_GROUPS, CHUNK, CHUNK)),
                   const((CHUNK, LANES)), const((1, SGU_WIDTH)), const((1, SGU_WIDTH)), const((1, 2 * SGU_WIDTH))],
        out_shape=[jax.ShapeDtypeStruct((s, 2 * SGU_WIDTH), BF16),
                   jax.ShapeDtypeStruct((SGU_GROUPS, CHUNK, CHUNK), F32), jax.ShapeDtypeStruct((CHUNK, LANES), F32),
                   jax.ShapeDtypeStruct((1, SGU_WIDTH), F32), jax.ShapeDtypeStruct((1, SGU_WIDTH), F32),
                   jax.ShapeDtypeStruct((1, 2 * SGU_WIDTH), F32)],
        compiler_params=_params(("arbitrary",)),
    )(uv_raw, dyb, uv_b, ln_w, ln_b, w_sp, b_sp_t, group_sum)


def _gate_fwd(gates_raw, b_gate, p_a, p_b, *, name, tm=512):
    s = p_a.shape[0]
    tm = min(tm, s)

    def body(ga_ref, gb_ref, ba_ref, bb_ref, pa_ref, pb_ref, o_ref):
        ga = _sigmoid(ga_ref[...].astype(F32) + ba_ref[...])
        gb = _sigmoid(gb_ref[...].astype(F32) + bb_ref[...])
        o_ref[...] = (ga * pa_ref[...].astype(F32) + gb * pb_ref[...].astype(F32)).astype(BF16)

    t_a = pl.BlockSpec((tm, D_MODEL), lambda i: (i, 0))
    t_b = pl.BlockSpec((tm, D_MODEL), lambda i: (i, 1))
    r_a = pl.BlockSpec((1, D_MODEL), lambda i: (0, 0))
    r_b = pl.BlockSpec((1, D_MODEL), lambda i: (0, 1))
    return pl.pallas_call(
        body, name=name, grid=(s // tm,),
        in_specs=[t_a, t_b, r_a, r_b, t_a, t_a], out_specs=t_a,
        out_shape=jax.ShapeDtypeStruct((s, D_MODEL), BF16),
        compiler_params=_params(("parallel",)),
    )(gates_raw, gates_raw, b_gate, b_gate, p_a, p_b)


def _gate_bwd(gates_raw, b_gate, p_a, p_b, dm, *, name, tm=512):
    s = p_a.shape[0]
    tm = min(tm, s)

    def body(ga_ref, gb_ref, ba_ref, bb_ref, pa_ref, pb_ref, dm_ref, dpa_ref, dpb_ref, dga_ref, dgb_ref,
             dba_ref, dbb_ref):
        @pl.when(pl.program_id(0) == 0)
        def _():
            dba_ref[...] = jnp.zeros_like(dba_ref)
            dbb_ref[...] = jnp.zeros_like(dbb_ref)

        d = dm_ref[...].astype(F32)
        for g_ref, b_ref, p_ref, dp_ref, dg_ref, db_ref in ((ga_ref, ba_ref, pa_ref, dpa_ref, dga_ref, dba_ref),
                                                            (gb_ref, bb_ref, pb_ref, dpb_ref, dgb_ref, dbb_ref)):
            sg = _sigmoid(g_ref[...].astype(F32) + b_ref[...])
            dp_ref[...] = (d * sg).astype(BF16)
            dg = d * p_ref[...].astype(F32) * (sg * (1.0 - sg))
            dg_ref[...] = dg.astype(BF16)
            db_ref[...] += jnp.sum(dg, axis=0, keepdims=True)

    t_a = pl.BlockSpec((tm, D_MODEL), lambda i: (i, 0))
    t_b = pl.BlockSpec((tm, D_MODEL), lambda i: (i, 1))
    r_a = pl.BlockSpec((1, D_MODEL), lambda i: (0, 0))
    r_b = pl.BlockSpec((1, D_MODEL), lambda i: (0, 1))
    big = jax.ShapeDtypeStruct((s, D_MODEL), BF16)
    row = jax.ShapeDtypeStruct((1, D_MODEL), F32)
    return pl.pallas_call(
        body, name=name, grid=(s // tm,),
        in_specs=[t_a, t_b, r_a, r_b, t_a, t_a, t_a], out_specs=[t_a, t_a, t_a, t_a, r_a, r_a],
        out_shape=[big, big, big, big, row, row],
        compiler_params=_params(("arbitrary",)),
    )(gates_raw, gates_raw, b_gate, b_gate, p_a, p_b, dm)


def _adamw_update(w_ref, g_ref, m_ref, v_ref, d_ref, mo_ref, vo_ref):
    gv = g_ref[...]
    mn = ADAM_B1 * m_ref[...] + (1.0 - ADAM_B1) * gv
    vn = ADAM_B2 * v_ref[...] + (1.0 - ADAM_B2) * (gv * gv)
    m_hat = mn / (1.0 - ADAM_B1 ** ADAM_STEP)
    v_hat = vn / (1.0 - ADAM_B2 ** ADAM_STEP)
    d_ref[...] = -ADAM_LR * (m_hat / (jnp.sqrt(v_hat) + ADAM_EPS) + ADAM_WD * w_ref[...])
    mo_ref[...] = mn
    vo_ref[...] = vn


def _adamw_many(ws, gs, ms, vs, *, name):
    n = len(ws)

    def body(*refs):
        for i in range(n):
            _adamw_update(*[refs[k * n + i] for k in range(7)])

    whole = pl.BlockSpec(memory_space=pltpu.VMEM)
    sds = [jax.ShapeDtypeStruct(w.shape, F32) for w in ws]
    outs = pl.pallas_call(
        body, name=name, in_specs=[whole] * (4 * n), out_specs=[whole] * (3 * n), out_shape=sds * 3,
        compiler_params=pltpu.CompilerParams(vmem_limit_bytes=VMEM_LIMIT),
    )(*ws, *gs, *ms, *vs)
    return outs[:n], outs[n:2 * n], outs[2 * n:]


def _adamw(w, g, m, v, *, name, tr=128):
    r, c = w.shape
    tr = min(tr, r)
    assert r % tr == 0, (name, r, tr)
    body = functools.partial(_adamw_update)

    blk = pl.BlockSpec((tr, c), lambda i: (i, 0))
    sds = jax.ShapeDtypeStruct((r, c), F32)
    return pl.pallas_call(
        body, name=name, grid=(r // tr,), in_specs=[blk] * 4, out_specs=[blk] * 3, out_shape=[sds] * 3,
        compiler_params=_params(("parallel",)),
    )(w, g, m, v)


def _adamw_two_sums(w, g_a, g_b, m, v, *, name, tr=128):
    r, c = w.shape
    tr = min(tr, r)
    assert r % tr == 0, (name, r, tr)

    def body(w_ref, ga_ref, gb_ref, m_ref, v_ref, g_ref, d_ref, mo_ref, vo_ref):
        g_ref[...] = ga_ref[...] + gb_ref[...]
        _adamw_update(w_ref, g_ref, m_ref, v_ref, d_ref, mo_ref, vo_ref)

    blk = pl.BlockSpec((tr, c), lambda i: (i, 0))
    sds = jax.ShapeDtypeStruct((r, c), F32)
    return pl.pallas_call(
        body, name=name, grid=(r // tr,), in_specs=[blk] * 5, out_specs=[blk] * 4, out_shape=[sds] * 4,
        compiler_params=_params(("parallel",)),
    )(w, g_a, g_b, m, v)


def _tile(n, pref):
    if n <= pref:
        return n
    best = LANES
    for t in range(LANES, pref + 1, LANES):
        if n % t == 0:
            best = t
    return best


MATMUL_BLOCK_BYTES = 20 * 1024 * 1024


def _mm(pairs, name, **kw):
    trans_b = kw.get("trans_b", False)
    m = (pairs[0][0][0] if isinstance(pairs[0][0], tuple) else pairs[0][0]).shape[0]
    ktot, n = 0, None
    for _, b in pairs:
        shape = b[0].shape[1:] if isinstance(b, tuple) else b.shape
        ktot += shape[1] if trans_b else shape[0]
        n = shape[0] if trans_b else shape[1]
    out_bytes = 4 * (2 if kw.get("add") is not None else 1)
    best = None
    for tm in (256, 512, 1024, 2048):
        for tn in range(LANES, min(n, 1536) + 1, LANES):
            if m % min(tm, m) or n % tn:
                continue
            fits = 2 * ktot * (min(tm, m) + tn) + out_bytes * min(tm, m) * tn <= MATMUL_BLOCK_BYTES
            if fits and (best is None or min(tm, m) * tn >= best[0] * best[1]):
                best = (min(tm, m), tn)
    return _matmul(pairs, tm=best[0], tn=best[1], name=name, **kw)


def _wgrad(a, b, name, **kw):
    return _matmul_tn(a, b, tk=_tile(a.shape[1], 1408), tn=kw.pop("tn", _tile(b.shape[1], 1024)), tm=2048,
                      name=name, **kw)


def _local_step(x, target, get_weight, small, emit_grad):
    heads = jnp.arange(SSD_D_INNER) // SSD_HEAD_DIM
    expand = (jnp.arange(LANES)[:, None] == heads[None, :]).astype(BF16)
    expand_t = expand.T
    group_sum = (jnp.arange(SGU_WIDTH)[:, None] // LANES == jnp.arange(LANES)[None, :]).astype(BF16)
    pad_h = LANES - SSD_HEADS
    dt_bias = jnp.pad(small["dt_bias"], ((0, 0), (0, pad_h)))
    a_log = jnp.pad(small["a_log"], ((0, 0), (0, pad_h)))
    a_log_x = jnp.repeat(small["a_log"], SSD_HEAD_DIM, axis=1)
    d_skip_x = jnp.repeat(small["d_skip"], SSD_HEAD_DIM, axis=1)
    b_sp_t = jnp.pad(small["b_spatial"][0].T, ((0, 0), (0, LANES - SGU_GROUPS)))
    w_sp = small["w_spatial"][0]
    conv_a_w = jnp.pad(small["conv_a_w"], ((0, 4), (0, 0)))
    conv_f_w = jnp.pad(small["conv_f_w"], ((0, 5), (0, 0)))
    final_w = small["final_norm_w"].reshape(1, D_MODEL)

    n1 = _rms_fwd(x, small["norm1_w"], after=small.get("gathers_started"), name="rms1_fwd")
    wts = dict(get_weight("w_in", n1))
    z = _mm([(n1, wts["in_z"])], "in_z")
    xbc_raw = _mm([(n1, wts["in_xbc"])], "in_xbc")
    dt_raw = _mm([(n1, wts["in_dt"])], "in_dt")
    uv_raw = _mm([(n1, wts["in_uv"])], "in_uv", out_dtype=BF16)
    gates_raw = _mm([(n1, wts["in_gate"])], "in_gate", out_dtype=BF16)
    xbc = _conv_a_fwd(xbc_raw, conv_a_w, small["conv_a_b"], name="conv_a_fwd")
    y, y_a, states = _ssd_fwd(xbc, dt_raw, z, dt_bias, a_log, a_log_x, d_skip_x, small["ssd_norm_w"], expand,
                              name="ssd_fwd")
    y_b = _sgu_fwd(uv_raw, small["uv_b"], small["v_ln_w"], small["v_ln_b"], w_sp, b_sp_t, name="sgu_fwd")
    wts.update(get_weight("w_branch", y_b))
    p_a = _mm([(y_a, wts["branch_a"])], "branch_a", out_dtype=BF16)
    p_b = _mm([(y_b, wts["branch_b"])], "branch_b", out_dtype=BF16)
    mix = _gate_fwd(gates_raw, small["b_gate"], p_a, p_b, name="gate_fwd")
    wts.update(get_weight("w_out", mix))
    h1 = _mm([(mix, wts["out"])], "out_proj", add=x)
    n2 = _rms_fwd(h1, small["norm2_w"], name="rms2_fwd")
    wts.update(get_weight("w_up", n2))
    up_w = wts["up"]
    up_cols = up_w.shape[2]
    up_raw = _matmul([(n2, (up_w, "cols"))], tm=2048, tn=up_cols, out_dtype=BF16, name="up_proj")
    act = _conv_f_fwd(up_raw, conv_f_w, small["conv_f_b"], name="conv_f_fwd")
    wts.update(get_weight("w_down", act))
    h2 = _mm([(act, wts["down"])], "down_proj", add=h1)
    loss, dh2, dh2_b, d_final = _final_fwd_bwd(h2, final_w, target, name="final_norm_loss")

    dact = _mm([(dh2_b, wts["down"])], "down_dgrad", trans_b=True)
    started = emit_grad("w_down", _wgrad(act, dh2_b, "down_wgrad"))
    dup_a, dup_v, dwf_a, dwf_v, dbf_a, dbf_v = _conv_f_bwd(up_raw, conv_f_w, small["conv_f_b"], dact,
                                                           name="conv_f_bwd")
    dn2 = _mm([((dup_a, 0), (up_w, 0)), ((dup_a, 1), (up_w, 1)), ((dup_v, 0), (up_w, 2)), ((dup_v, 1), (up_w, 3))],
              "up_dgrad", trans_b=True, after=started, out_dtype=BF16)
    started = emit_grad("w_up", jnp.concatenate([_wgrad(n2, dup_a, "up_wgrad_a", tn=up_cols, stack_out=True),
                                                 _wgrad(n2, dup_v, "up_wgrad_v", tn=up_cols, stack_out=True)], axis=0))
    dh1, dh1_b, d_norm2 = _rms_bwd(h1, small["norm2_w"], dn2, dh2, name="rms2_bwd")
    dmix = _mm([(dh1_b, wts["out"])], "out_dgrad", trans_b=True, after=started, out_dtype=BF16)
    started = emit_grad("w_out", _wgrad(mix, dh1_b, "out_wgrad"))
    dp_a, dp_b, dg_a, dg_b, dbg_a, dbg_b = _gate_bwd(gates_raw, small["b_gate"], p_a, p_b, dmix, name="gate_bwd")
    dya = _mm([(dp_a, wts["branch_a"])], "branch_a_dgrad", trans_b=True, after=started)
    dyb = _mm([(dp_b, wts["branch_b"])], "branch_b_dgrad", trans_b=True, out_dtype=BF16)
    started_branch = emit_grad("w_branch", jnp.concatenate([_wgrad(y_a, dp_a, "branch_a_wgrad"),
                                                            _wgrad(y_b, dp_b, "branch_b_wgrad")], axis=0))
    duv, d_wsp, d_bsp_t, d_lnw, d_lnb, d_uvb = _sgu_bwd(uv_raw, dyb, small["uv_b"], small["v_ln_w"],
                                                        small["v_ln_b"], w_sp, b_sp_t, group_sum, name="sgu_bwd")
    dz, dxbc, ddt, d_ssd_nw, d_dskip, d_alog, d_dtb = _ssd_bwd(
        dya, y, z, xbc, dt_raw, states, dt_bias, a_log, a_log_x, d_skip_x, small["ssd_norm_w"], expand, expand_t,
        name="ssd_bwd")
    dxbc_raw, d_conv_a_w, d_conv_a_b = _conv_a_bwd(xbc_raw, conv_a_w, small["conv_a_b"], dxbc, name="conv_a_bwd")
    started = emit_grad("w_in", {
        "in_z": _wgrad(n1, dz, "in_z_wgrad", after=started_branch), "in_xbc": _wgrad(n1, dxbc_raw, "in_xbc_wgrad"),
        "in_dt": _wgrad(n1, ddt, "in_dt_wgrad")[:, :SSD_HEADS], "in_uv": _wgrad(n1, duv, "in_uv_wgrad"),
        "in_gate_a": _wgrad(n1, dg_a, "in_gate_a_wgrad"), "in_gate_b": _wgrad(n1, dg_b, "in_gate_b_wgrad")})
    dn1 = _mm([(dz, wts["in_z"]), (dxbc_raw, wts["in_xbc"]), (ddt, wts["in_dt"]), (duv, wts["in_uv"]),
               (dg_a, wts["in_gate_a"]), (dg_b, wts["in_gate_b"])], "in_dgrad", trans_b=True, after=started,
              out_dtype=BF16)
    dx, _, d_norm1 = _rms_bwd(x, small["norm1_w"], dn1, dh1, name="rms1_bwd")

    grads_small = {
        "norm1_w": d_norm1, "b_gate": jnp.concatenate([dbg_a, dbg_b], axis=1),
        "conv_a_w": d_conv_a_w[:4], "conv_a_b": d_conv_a_b,
        "dt_bias": d_dtb[:, :SSD_HEADS], "a_log": d_alog[:, :SSD_HEADS], "d_skip": d_dskip[:, :SSD_HEADS],
        "ssd_norm_w": d_ssd_nw, "uv_b": d_uvb, "v_ln_w": d_lnw, "v_ln_b": d_lnb,
        "w_spatial": d_wsp[None], "b_spatial": d_bsp_t[:, :SGU_GROUPS].T[None],
        "norm2_w": d_norm2, "conv_f_w": jnp.concatenate([dwf_a[:3], dwf_v[:3]], axis=1),
        "conv_f_b": jnp.concatenate([dbf_a, dbf_v], axis=1), "final_norm_w": d_final.reshape(D_MODEL),
    }
    return loss, dx, grads_small


HBM = pl.BlockSpec(memory_space=pl.ANY)
MESH = pl.DeviceIdType.MESH


def _mesh_pos():
    return lax.axis_index("x"), lax.axis_index("y"), lax.axis_index("c")


def _other_chips(x, y):
    return [(1 - x, y), (x, 1 - y), (1 - x, 1 - y)]


def _remote(src, dst, send_sems, recv_sems, k, dev):
    return pltpu.make_async_remote_copy(src_ref=src, dst_ref=dst, send_sem=send_sems.at[k], recv_sem=recv_sems.at[k],
                                        device_id=dev, device_id_type=MESH)


def _dma_sems(n):
    return [pltpu.SemaphoreType.DMA((n,)), pltpu.SemaphoreType.DMA((n,))]


HBM_ONLY = pl.BlockSpec(memory_space=pltpu.HBM)
SEMAPHORES = pl.BlockSpec(memory_space=pltpu.SEMAPHORE)
DATAFLOW_EFFECT = pltpu.SideEffectType.DATAFLOW_SIDE_EFFECTING
N_PEER_CHIPS = N_CHIPS - 1


def _gather_sends(w_ref, land_ref, send_sems, recv_sems):
    x, y, c = _mesh_pos()
    return [_remote(w_ref.at[c], land_ref.at[2 * x + y, c], send_sems, recv_sems, k, (px, py, c))
            for k, (px, py) in enumerate(_other_chips(x, y))]


def _gather_arrivals(w_ref, land_ref, send_sems, recv_sems):
    x, y, c = _mesh_pos()
    return [_remote(w_ref.at[c], land_ref.at[2 * px + py, c], send_sems, recv_sems, k, (px, py, c))
            for k, (px, py) in enumerate(_other_chips(x, y))]


def _gather_whole_sends(w_ref, land_ref, send_sems, recv_sems):
    x, y, c = _mesh_pos()
    return [_remote(w_ref, land_ref.at[2 * x + y], send_sems, recv_sems, k, (px, py, c))
            for k, (px, py) in enumerate(_other_chips(x, y))]


def _gather_whole_arrivals(w_ref, land_ref, send_sems, recv_sems):
    x, y, c = _mesh_pos()
    return [_remote(w_ref, land_ref.at[2 * px + py], send_sems, recv_sems, k, (px, py, c))
            for k, (px, py) in enumerate(_other_chips(x, y))]


def _scatter_sends(h_ref, land_ref, send_sems, recv_sems):
    x, y, c = _mesh_pos()
    return [_remote(h_ref.at[2 * px + py], land_ref.at[2 * x + y], send_sems, recv_sems, k, (px, py, c))
            for k, (px, py) in enumerate(_other_chips(x, y))]


def _scatter_arrivals(h_ref, land_ref, send_sems, recv_sems):
    x, y, c = _mesh_pos()
    return [_remote(h_ref.at[2 * x + y], land_ref.at[2 * px + py], send_sems, recv_sems, k, (px, py, c))
            for k, (px, py) in enumerate(_other_chips(x, y))]


def _exchange_wait_many(pendings, after, sends, arrivals, *, name):
    n = len(pendings)

    def body(*refs):
        for i in range(n):
            src_ref, land_ref, send_ref, recv_ref = refs[i], refs[n + i], refs[2 * n + i], refs[3 * n + i]
            for cp in sends(src_ref, land_ref, send_ref, recv_ref):
                cp.wait_send()
            for cp in arrivals(src_ref, land_ref, send_ref, recv_ref):
                cp.wait_recv()

    sources = [p[2] for p in pendings]
    landings = [p[3] for p in pendings]
    outs = pl.pallas_call(
        body, name=name,
        out_shape=tuple(pltpu.HBM(a.shape, a.dtype) for a in sources + landings),
        in_specs=[HBM_ONLY] * (2 * n) + [SEMAPHORES] * (2 * n) + [pl.BlockSpec(memory_space=pl.ANY)],
        out_specs=tuple([HBM_ONLY] * (2 * n)), input_output_aliases={i: i for i in range(2 * n)},
        compiler_params=pltpu.CompilerParams(has_side_effects=DATAFLOW_EFFECT),
    )(*sources, *landings, *[p[0] for p in pendings], *[p[1] for p in pendings], after)
    return [(outs[i], outs[n + i]) for i in range(n)]


def _sibling_sends(src_ref, land_ref, send_sems, recv_sems):
    x, y, c = _mesh_pos()
    return [_remote(src_ref, land_ref, send_sems, recv_sems, 0, (x, y, 1 - c))]


def _exchange_start(sources, landing_shapes, sends, *, after=None, name):
    n = len(sources)
    extra = [] if after is None else [after]

    def body(*refs):
        sems = refs[2 * n + len(extra):4 * n + len(extra)]
        for i in range(n):
            send_i = sends[i] if isinstance(sends, (list, tuple)) else sends
            for cp in send_i(refs[i], refs[n + i], sems[2 * i], sems[2 * i + 1]):
                cp.start()
        refs[-1][...] = jnp.zeros_like(refs[-1])

    hbm = [pltpu.HBM(s.shape, s.dtype) for s in sources] + [pltpu.HBM(shp, s.dtype)
                                                             for shp, s in zip(landing_shapes, sources)]
    outs = pl.pallas_call(
        body, name=name,
        out_shape=tuple([pltpu.SemaphoreType.DMA((N_PEER_CHIPS,))] * (2 * n) + hbm
                        + [jax.ShapeDtypeStruct((8, LANES), F32)]),
        in_specs=[HBM_ONLY] * (2 * n) + [pl.BlockSpec(memory_space=pl.ANY)] * len(extra),
        out_specs=tuple([SEMAPHORES] * (2 * n) + [HBM_ONLY] * (2 * n) + [pl.BlockSpec(memory_space=pltpu.VMEM)]),
        input_output_aliases={i: 2 * n + i for i in range(2 * n)},
        compiler_params=pltpu.CompilerParams(has_side_effects=DATAFLOW_EFFECT),
    )(*[pltpu.with_memory_space_constraint(s, pltpu.HBM) for s in sources],
      *[pltpu.with_memory_space_constraint(lax.empty(shp, s.dtype), pltpu.HBM)
        for shp, s in zip(landing_shapes, sources)], *extra)
    pending = [(outs[2 * i], outs[2 * i + 1], outs[2 * n + i], outs[3 * n + i]) for i in range(n)]
    return pending, outs[-1]


def _exchange_wait(pending, after, sends, arrivals, *, name):
    send_sems, recv_sems, source, landing = pending

    def body(src_ref, land_ref, send_ref, recv_ref, after_ref, src_out, land_out):
        for cp in sends(src_ref, land_ref, send_ref, recv_ref):
            cp.wait_send()
        for cp in arrivals(src_ref, land_ref, send_ref, recv_ref):
            cp.wait_recv()

    return pl.pallas_call(
        body, name=name,
        out_shape=(pltpu.HBM(source.shape, source.dtype), pltpu.HBM(landing.shape, landing.dtype)),
        in_specs=[HBM_ONLY, HBM_ONLY, SEMAPHORES, SEMAPHORES, pl.BlockSpec(memory_space=pl.ANY)],
        out_specs=(HBM_ONLY, HBM_ONLY), input_output_aliases={0: 0, 1: 1},
        compiler_params=pltpu.CompilerParams(has_side_effects=DATAFLOW_EFFECT),
    )(source, landing, send_sems, recv_sems, after)


def _gather_ici(shard, *, name):
    _, rh, cols = shard.shape

    def body(w_ref, o_ref, send_sems, recv_sems):
        x, y, c = _mesh_pos()
        mine = 2 * x + y
        sends = []
        for k, (px, py) in enumerate(_other_chips(x, y)):
            cp = _remote(w_ref.at[c], o_ref.at[mine, c], send_sems, recv_sems, k, (px, py, c))
            cp.start()
            sends.append(cp)
        for k, (px, py) in enumerate(_other_chips(x, y)):
            _remote(w_ref.at[c], o_ref.at[2 * px + py, c], send_sems, recv_sems, k, (px, py, c)).wait_recv()
        for cp in sends:
            cp.wait_send()

    return pl.pallas_call(
        body, name=name, in_specs=[HBM], out_specs=HBM,
        out_shape=jax.ShapeDtypeStruct((N_CHIPS, 2, rh, cols), shard.dtype), scratch_shapes=_dma_sems(3),
    )(shard)


def _gather_d2d(parts, *, name):
    def body(a_ref, o_ref, send_sems, recv_sems):
        x, y, c = _mesh_pos()
        sibling = (x, y, 1 - c)
        sends = []
        for k, (px, py) in enumerate(_other_chips(x, y)):
            cp = _remote(a_ref.at[2 * px + py, c], o_ref.at[2 * px + py, c], send_sems, recv_sems, k, sibling)
            cp.start()
            sends.append(cp)
        for k, (px, py) in enumerate(_other_chips(x, y)):
            _remote(a_ref.at[2 * px + py, c], o_ref.at[2 * px + py, 1 - c], send_sems, recv_sems, k, sibling).wait_recv()
        for cp in sends:
            cp.wait_send()

    return pl.pallas_call(
        body, name=name, in_specs=[HBM], out_specs=HBM,
        out_shape=jax.ShapeDtypeStruct(parts.shape, parts.dtype),
        input_output_aliases={0: 0}, scratch_shapes=_dma_sems(3),
    )(parts)


def _all_gather_chips(shard_flat, name):
    rows, cols = shard_flat.shape
    parts = _gather_ici(shard_flat.reshape(2, rows // 2, cols), name=name + "_ici")
    others = _gather_d2d(parts, name=name + "_d2d").reshape(N_CHIPS, rows, cols)
    chip = 2 * lax.axis_index("x") + lax.axis_index("y")
    return lax.dynamic_update_slice(others, shard_flat[None], (chip, 0, 0))


def _row_tile(rows, mult, cap):
    best = mult
    for t in range(mult, min(rows, cap) + 1, mult):
        if rows % t == 0:
            best = t
    assert rows % best == 0, (rows, mult)
    return best


def _swap_halves_d2d(g, *, after=None, name):
    _, _, rh, cols = g.shape
    extra = [] if after is None else [after]

    def body(g_ref, *rest):
        o_ref, send_sems, recv_sems = rest[len(extra):]
        x, y, c = _mesh_pos()
        sibling = (x, y, 1 - c)
        sends = []
        for s in range(N_CHIPS):
            cp = _remote(g_ref.at[s, 1 - c], o_ref.at[s], send_sems, recv_sems, s, sibling)
            cp.start()
            sends.append(cp)
        for s in range(N_CHIPS):
            _remote(g_ref.at[s, c], o_ref.at[s], send_sems, recv_sems, s, sibling).wait_recv()
        for cp in sends:
            cp.wait_send()

    return pl.pallas_call(
        body, name=name, in_specs=[HBM] * (1 + len(extra)), out_specs=HBM,
        out_shape=jax.ShapeDtypeStruct((N_CHIPS, rh, cols), g.dtype), scratch_shapes=_dma_sems(N_CHIPS),
    )(g, *extra)


def _add_own_half(g, arrived, core, *, name):
    _, _, rh, cols = g.shape
    mult = 16 if g.dtype == BF16 else 8
    tr = _row_tile(rh, mult, max(mult, (512 * 1024) // cols))

    def body(core_ref, g_ref, a_ref, o_ref):
        o_ref[...] = (g_ref[0].astype(F32) + a_ref[...].astype(F32)).astype(o_ref.dtype)

    grid_spec = pltpu.PrefetchScalarGridSpec(
        num_scalar_prefetch=1, grid=(N_CHIPS, rh // tr),
        in_specs=[pl.BlockSpec((1, 1, tr, cols), lambda s, i, core_ref: (s, core_ref[0], i, 0)),
                  pl.BlockSpec((1, tr, cols), lambda s, i, core_ref: (s, i, 0))],
        out_specs=pl.BlockSpec((1, tr, cols), lambda s, i, core_ref: (s, i, 0)))
    return pl.pallas_call(
        body, name=name, grid_spec=grid_spec, out_shape=jax.ShapeDtypeStruct((N_CHIPS, rh, cols), g.dtype),
        compiler_params=_params(("parallel", "parallel")),
    )(core, g, arrived)


def _scatter_ici(h, *, name):
    def body(h_ref, o_ref, send_sems, recv_sems):
        x, y, c = _mesh_pos()
        mine = 2 * x + y
        sends = []
        for k, (px, py) in enumerate(_other_chips(x, y)):
            cp = _remote(h_ref.at[2 * px + py], o_ref.at[mine], send_sems, recv_sems, k, (px, py, c))
            cp.start()
            sends.append(cp)
        for k, (px, py) in enumerate(_other_chips(x, y)):
            _remote(h_ref.at[mine], o_ref.at[2 * px + py], send_sems, recv_sems, k, (px, py, c)).wait_recv()
        for cp in sends:
            cp.wait_send()

    others = pl.pallas_call(
        body, name=name, in_specs=[HBM], out_specs=HBM, out_shape=jax.ShapeDtypeStruct(h.shape, h.dtype),
        scratch_shapes=_dma_sems(3),
    )(h)
    chip = 2 * lax.axis_index("x") + lax.axis_index("y")
    own = lax.dynamic_slice_in_dim(h, chip, 1, axis=0)
    return lax.dynamic_update_slice(others, own, (chip, 0, 0))


def _sum_chips(parts, *, name):
    _, rh, cols = parts.shape
    mult = 16 if parts.dtype == BF16 else 8
    tr = _row_tile(rh, mult, max(mult, (512 * 1024) // cols))

    def body(p_ref, o_ref):
        acc = p_ref[0].astype(F32)
        for s in range(1, N_CHIPS):
            acc = acc + p_ref[s].astype(F32)
        o_ref[...] = acc

    return pl.pallas_call(
        body, name=name, grid=(rh // tr,),
        in_specs=[pl.BlockSpec((N_CHIPS, tr, cols), lambda i: (0, i, 0))],
        out_specs=pl.BlockSpec((tr, cols), lambda i: (i, 0)),
        out_shape=jax.ShapeDtypeStruct((rh, cols), F32), compiler_params=_params(("parallel",)),
    )(parts)


def _share_d2d(f, *, name):
    fs = f if isinstance(f, (list, tuple)) else [f]
    others = _swap_with_sibling(fs, name=name)
    first = lax.axis_index("c") == 0
    both = [jnp.stack([jnp.where(first, a, b), jnp.where(first, b, a)]) for a, b in zip(fs, others)]
    return both if isinstance(f, (list, tuple)) else both[0]


def _swap_with_sibling(fs, *, name):
    n = len(fs)

    def body(*refs):
        x, y, c = _mesh_pos()
        sibling = (x, y, 1 - c)
        send_sems, recv_sems = refs[2 * n:]
        copies = [_remote(refs[i], refs[n + i], send_sems, recv_sems, i, sibling) for i in range(n)]
        for cp in copies:
            cp.start()
        for cp in copies:
            cp.wait()

    return pl.pallas_call(
        body, name=name, in_specs=[HBM] * n, out_specs=[HBM] * n,
        out_shape=[jax.ShapeDtypeStruct(a.shape, a.dtype) for a in fs], scratch_shapes=_dma_sems(n),
    )(*fs)


def _reduce_scatter_chips(g, core, name, after=None):
    _, rows, cols = g.shape
    g = g.reshape(N_CHIPS, 2, rows // 2, cols)
    arrived = _swap_halves_d2d(g, after=after, name=name + "_swap")
    chip_sum = _add_own_half(g, arrived, core, name=name + "_add2")
    parts = _scatter_ici(chip_sum, name=name + "_ici")
    total = _sum_chips(parts, name=name + "_sum4")
    return _share_d2d(total, name=name + "_share").reshape(rows, cols)


BIG = ("w_in", "w_branch", "w_out", "w_up", "w_down")
BIG_COLUMN_SHARDED = ("w_in", "w_up")
CONV = ("conv_a_w", "conv_f_w")
REPLICATED = ("norm1_w", "b_gate", "conv_a_b", "dt_bias", "a_log", "d_skip", "ssd_norm_w", "uv_b", "v_ln_w",
              "v_ln_b", "w_spatial", "b_spatial", "norm2_w", "conv_f_b", "final_norm_w")
WEIGHT_ORDER = ("norm1_w", "w_in", "b_gate", "conv_a_w", "conv_a_b", "dt_bias", "a_log", "d_skip", "ssd_norm_w",
                "uv_b", "v_ln_w", "v_ln_b", "w_spatial", "b_spatial", "w_branch", "w_out", "norm2_w", "w_up",
                "conv_f_w", "conv_f_b", "w_down", "final_norm_w")
SMALL_EXCHANGE_ROWS = 64


_GATE0 = SSD_IN + 2 * SGU_WIDTH
IN_SEGMENTS = {
    "in_z": (0, SSD_D_INNER), "in_xbc": (SSD_D_INNER, SSD_D_INNER + SSD_XBC), "in_dt": (SSD_D_INNER + SSD_XBC, SSD_IN),
    "in_uv": (SSD_IN, _GATE0), "in_gate": (_GATE0, IN_COLS), "in_gate_a": (_GATE0, _GATE0 + D_MODEL),
    "in_gate_b": (_GATE0 + D_MODEL, IN_COLS),
}
IN_GRAD_SEGMENTS = ("in_z", "in_xbc", "in_dt", "in_uv", "in_gate_a", "in_gate_b")


def _take_columns(parts, start, stop):
    out = []
    for a, first in parts:
        lo, hi = max(start, first), min(stop, first + a.shape[1])
        if lo < hi:
            out.append(a[:, lo - first:hi - first])
    return out[0] if len(out) == 1 else jnp.concatenate(out, axis=1)


def _flat_rows(arrays, row_multiple):
    flat = jnp.concatenate([a.reshape(-1) for a in arrays])
    rows = -(-flat.shape[0] // (LANES * row_multiple)) * row_multiple
    return jnp.pad(flat, (0, rows * LANES - flat.shape[0])).reshape(rows, LANES)


def _unflatten(flat, shapes):
    flat = flat.reshape(-1)
    out, off = [], 0
    for shp in shapes:
        n = math.prod(shp)
        out.append(flat[off:off + n].reshape(shp))
        off += n
    return out


def _from_chip_blocks(blocks, name):
    if name in BIG_COLUMN_SHARDED or name in CONV:
        k = blocks.shape[1]
        return jnp.transpose(blocks, (1, 0, 2)).reshape(k, -1)
    return blocks.reshape(-1, blocks.shape[-1])


def _to_chip_blocks(whole, name):
    if name in BIG_COLUMN_SHARDED or name in CONV:
        k, n = whole.shape
        return jnp.transpose(whole.reshape(k, N_CHIPS, n // N_CHIPS), (1, 0, 2))
    return whole.reshape(N_CHIPS, whole.shape[0] // N_CHIPS, whole.shape[1])


def kernel(x, norm1_w, w_in, b_gate, conv_a_w, conv_a_b, dt_bias, a_log, d_skip, ssd_norm_w, uv_b, v_ln_w, v_ln_b, w_spatial, b_spatial, w_branch, w_out, norm2_w, w_up, conv_f_w, conv_f_b, w_down, final_norm_w, loss_target, m_norm1_w, m_w_in, m_b_gate, m_conv_a_w, m_conv_a_b, m_dt_bias, m_a_log, m_d_skip, m_ssd_norm_w, m_uv_b, m_v_ln_w, m_v_ln_b, m_w_spatial, m_b_spatial, m_w_branch, m_w_out, m_norm2_w, m_w_up, m_conv_f_w, m_conv_f_b, m_w_down, m_final_norm_w, v_norm1_w, v_w_in, v_b_gate, v_conv_a_w, v_conv_a_b, v_dt_bias, v_a_log, v_d_skip, v_ssd_norm_w, v_uv_b, v_v_ln_w, v_v_ln_b, v_w_spatial, v_b_spatial, v_w_branch, v_w_out, v_norm2_w, v_w_up, v_conv_f_w, v_conv_f_b, v_w_down, v_final_norm_w):
    weights = dict(norm1_w=norm1_w, w_in=w_in, b_gate=b_gate, conv_a_w=conv_a_w, conv_a_b=conv_a_b, dt_bias=dt_bias,
                   a_log=a_log, d_skip=d_skip, ssd_norm_w=ssd_norm_w, uv_b=uv_b, v_ln_w=v_ln_w, v_ln_b=v_ln_b,
                   w_spatial=w_spatial, b_spatial=b_spatial, w_branch=w_branch, w_out=w_out, norm2_w=norm2_w,
                   w_up=w_up, conv_f_w=conv_f_w, conv_f_b=conv_f_b, w_down=w_down, final_norm_w=final_norm_w)
    mom1 = dict(norm1_w=m_norm1_w, w_in=m_w_in, b_gate=m_b_gate, conv_a_w=m_conv_a_w, conv_a_b=m_conv_a_b,
                dt_bias=m_dt_bias, a_log=m_a_log, d_skip=m_d_skip, ssd_norm_w=m_ssd_norm_w, uv_b=m_uv_b,
                v_ln_w=m_v_ln_w, v_ln_b=m_v_ln_b, w_spatial=m_w_spatial, b_spatial=m_b_spatial, w_branch=m_w_branch,
                w_out=m_w_out, norm2_w=m_norm2_w, w_up=m_w_up, conv_f_w=m_conv_f_w, conv_f_b=m_conv_f_b,
                w_down=m_w_down, final_norm_w=m_final_norm_w)
    mom2 = dict(norm1_w=v_norm1_w, w_in=v_w_in, b_gate=v_b_gate, conv_a_w=v_conv_a_w, conv_a_b=v_conv_a_b,
                dt_bias=v_dt_bias, a_log=v_a_log, d_skip=v_d_skip, ssd_norm_w=v_ssd_norm_w, uv_b=v_uv_b,
                v_ln_w=v_v_ln_w, v_ln_b=v_v_ln_b, w_spatial=v_w_spatial, b_spatial=v_b_spatial, w_branch=v_w_branch,
                w_out=v_w_out, norm2_w=v_norm2_w, w_up=v_w_up, conv_f_w=v_conv_f_w, conv_f_b=v_conv_f_b,
                w_down=v_w_down, final_norm_w=v_final_norm_w)
    chip = 2 * lax.axis_index("x") + lax.axis_index("y")
    core = lax.axis_index("c").astype(jnp.int32).reshape(1)

    whole = {}
    conv_shapes = [weights[n].shape[1:] for n in CONV]
    conv_gathered = _all_gather_chips(_flat_rows([weights[n] for n in CONV], 16), "gather_conv").reshape(N_CHIPS, -1)
    off = 0
    for n, shp in zip(CONV, conv_shapes):
        size = math.prod(shp)
        whole[n] = _from_chip_blocks(conv_gathered[:, off:off + size].reshape((N_CHIPS,) + shp), n)
        off += size
    shard_shapes = {n: weights[n].shape[1:] for n in BIG}
    halves = [weights[n][0].astype(BF16).reshape(2, shard_shapes[n][0] // 2, shard_shapes[n][1]) for n in BIG]
    sends = [_gather_sends if n == "w_in" else _gather_whole_sends for n in BIG]
    gathers, gathers_started = _exchange_start(halves, [(N_CHIPS,) + h.shape for h in halves], sends,
                                               after=conv_gathered, name="gather_start")
    gathers = dict(zip(BIG, gathers))

    def get_weight(name, after):
        rows, cols = shard_shapes[name]
        if name == "w_in":
            own, landed = _exchange_wait(gathers[name], after, _gather_sends, _gather_arrivals,
                                         name="gather_" + name + "_wait")
            landed = _gather_d2d(landed, name="gather_" + name + "_d2d")
        else:
            own, landed = _exchange_wait(gathers[name], after, _gather_whole_sends, _gather_whole_arrivals,
                                         name="gather_" + name + "_wait")
        blocks = lax.dynamic_update_slice(landed.reshape(N_CHIPS, rows, cols), own.reshape(1, rows, cols),
                                          (chip, 0, 0))
        if name == "w_up":
            return {"up": blocks}
        if name == "w_in":
            parts = [(blocks[k], cols * k) for k in range(N_CHIPS)]
            segs = {n: _take_columns(parts, a, b) for n, (a, b) in IN_SEGMENTS.items()}
            segs["in_dt"] = jnp.pad(segs["in_dt"], ((0, 0), (0, LANES - SSD_HEADS)))
            return segs
        full = _from_chip_blocks(blocks, name)
        if name == "w_branch":
            return {"branch_a": full[:SSD_D_INNER], "branch_b": full[SSD_D_INNER:]}
        return {name[2:]: full}

    small = {n: weights[n] for n in REPLICATED}
    small["conv_a_w"] = whole["conv_a_w"]
    small["conv_f_w"] = whole["conv_f_w"]
    small["gathers_started"] = gathers_started

    reductions = {}

    def emit_grad(name, g):
        if name == "w_in":
            parts = [(g[n], IN_SEGMENTS[n][0]) for n in IN_GRAD_SEGMENTS]
            cols = shard_shapes[name][1]
            g_blocks = jnp.stack([_take_columns(parts, cols * k, cols * (k + 1)) for k in range(N_CHIPS)])
        else:
            g_blocks = g if name == "w_up" else _to_chip_blocks(g, name)
        if name == "w_in":
            _, rows, cols = g_blocks.shape
            g_halves = g_blocks.reshape(N_CHIPS, 2, rows // 2, cols)
            arrived = _swap_halves_d2d(g_halves, name="reduce_" + name + "_swap")
            g_blocks = _add_own_half(g_halves, arrived, core, name="reduce_" + name + "_add2")
        own = lax.dynamic_slice_in_dim(g_blocks, chip, 1, axis=0)
        (pending,), started = _exchange_start([g_blocks], [g_blocks.shape], _scatter_sends,
                                              name="reduce_" + name + "_start")
        reductions[name] = (pending, own)
        return started

    loss, dx, grads_small = _local_step(x[0], loss_target[0], get_weight, small, emit_grad)

    order = ("w_down", "w_up", "w_out", "w_branch", "w_in")
    core_sums = []
    for n in order:
        pending, own = reductions[n]
        _, landed = _exchange_wait(pending, dx, _scatter_sends, _scatter_arrivals, name="reduce_" + n + "_wait")
        parts = lax.dynamic_update_slice(landed, own, (chip, 0, 0))
        core_sums.append(_sum_chips(parts, name="reduce_" + n + "_sum4"))
    swaps, swaps_started = _exchange_start(core_sums, [a.shape for a in core_sums], _sibling_sends, name="reduce_swap_start")
    grads = {}

    small_names = REPLICATED + CONV
    small_shapes = [grads_small[n].shape for n in small_names]
    g_small = _flat_rows([grads_small[n] for n in small_names], N_CHIPS * 2 * SMALL_EXCHANGE_ROWS)
    red_small = _reduce_scatter_chips(g_small.reshape(N_CHIPS, -1, LANES), core, "reduce_small", after=swaps_started)
    all_small = _all_gather_chips(red_small, "gather_small")
    swapped = _exchange_wait_many(swaps, all_small, _sibling_sends, _sibling_sends, name="reduce_swap_wait")
    core_sums = {n: own for n, (own, _) in zip(order, swapped)}
    sibling_sums = {n: other for n, (_, other) in zip(order, swapped)}
    first = lax.axis_index("c") == 0
    w_in_halves = (core_sums["w_in"], sibling_sums["w_in"])
    w_in_grad = jnp.concatenate([jnp.where(first, w_in_halves[0], w_in_halves[1]),
                                 jnp.where(first, w_in_halves[1], w_in_halves[0])], axis=0)
    for n, g in zip(small_names, _unflatten(all_small, small_shapes)):
        if n in CONV:
            width = g.shape[1] // N_CHIPS
            g = lax.dynamic_slice_in_dim(g, chip * width, width, axis=1)
        grads[n] = g.reshape(weights[n].shape[1:]) if n != "final_norm_w" else g

    delta, new_m, new_v = {}, {}, {}
    for n in BIG:
        shp = weights[n].shape
        if n == "w_in":
            g_t = w_in_grad.T
            results = [g_t] + list(_adamw(weights[n][0].T, g_t, mom1[n][0].T, mom2[n][0].T, name="adamw_" + n,
                                          tr=_row_tile(g_t.shape[0], 8, 136)))
            results = [a.T for a in results]
        else:
            results = _adamw_two_sums(weights[n][0], core_sums[n], sibling_sums[n], mom1[n][0], mom2[n][0],
                                      name="adamw_" + n, tr=_row_tile(shp[1], 8, 136))
        grads[n], delta[n], new_m[n], new_v[n] = [a.reshape(shp) for a in results]
    small_all = [n for n in WEIGHT_ORDER if n not in BIG]

    def as_2d(a):
        return a.reshape(-1, a.shape[-1])

    results = _adamw_many(*[[as_2d(src[n]) for n in small_all] for src in (weights, grads, mom1, mom2)],
                          name="adamw_small")
    for n, dv, mv, vv in zip(small_all, *results):
        shp = weights[n].shape
        delta[n], new_m[n], new_v[n] = dv.reshape(shp), mv.reshape(shp), vv.reshape(shp)

    total_loss = lax.psum(loss[0, 0], ("x", "y", "c"))
    grad_out = [grads[n].reshape(weights[n].shape) for n in WEIGHT_ORDER]
    return (total_loss, dx[None], *grad_out, *[delta[n] for n in WEIGHT_ORDER], *[new_m[n] for n in WEIGHT_ORDER],
            *[new_v[n] for n in WEIGHT_ORDER])
```

```python
import functools
import math

import jax
import jax.numpy as jnp
from jax import lax
from jax.experimental import pallas as pl
from jax.experimental.pallas import tpu as pltpu

F32 = jnp.float32
BF16 = jnp.bfloat16
HI = lax.Precision.HIGHEST

D_MODEL = 1024
SSD_D_INNER = 2048
SSD_HEADS = 32
SSD_HEAD_DIM = 64
SSD_GROUPS = 4
SSD_HEADS_PER_GROUP = 8
SSD_STATE = 128
SSD_BC = 512
SSD_XBC = 3072
SSD_IN = 5152
SGU_WIDTH = 1024
SGU_GROUPS = 8
CHUNK = 128
IN_COLS = 9248
D_FF = 2816
NORM_EPS = 1e-6
LN_EPS = 1e-5
GROUP_COLS = SSD_HEADS_PER_GROUP * SSD_HEAD_DIM
LANES = 128

ADAM_LR = 0.001
ADAM_B1 = 0.9
ADAM_B2 = 0.999
ADAM_EPS = 1e-08
ADAM_WD = 0.01
ADAM_STEP = 10

N_CHIPS = 4
VMEM_LIMIT = 56 * 1024 * 1024

NT = (((1,), (1,)), ((), ()))
TN = (((0,), (0,)), ((), ()))
NN = (((1,), (0,)), ((), ()))


def _params(dims):
    return pltpu.CompilerParams(dimension_semantics=dims, vmem_limit_bytes=VMEM_LIMIT)


def _dot(a, b, dn=NN, precision=None):
    return lax.dot_general(a, b, dn, precision=precision, preferred_element_type=F32)


def _split3(x):
    hi = x.astype(BF16)
    rest = x - hi.astype(F32)
    mid = rest.astype(BF16)
    return hi, mid, (rest - mid.astype(F32)).astype(BF16)


def _dot_terms(terms, exact, dn=NN):
    out = None
    for t in terms:
        p = _dot(t, exact, dn)
        out = p if out is None else out + p
    return out


def _dot_exact_lhs(exact, terms):
    out = None
    for t in terms:
        p = _dot(exact, t)
        out = p if out is None else out + p
    return out


def _sigmoid(x):
    return 1.0 / (1.0 + jnp.exp(-x))


def _softplus(x):
    return jnp.maximum(x, 0.0) + jnp.log(1.0 + jnp.exp(-jnp.abs(x)))


def _matmul(pairs, *, trans_b=False, add=None, after=None, out_dtype=F32, tm=512, tn=512, name):
    def mat_shape(b):
        if isinstance(b, tuple) and b[1] == "cols":
            return (b[0].shape[1], b[0].shape[0] * b[0].shape[2])
        return b[0].shape[1:] if isinstance(b, tuple) else b.shape

    if isinstance(pairs[0][1], tuple) and pairs[0][1][1] == "cols":
        assert not trans_b and tn % LANES == 0 and pairs[0][1][0].shape[2] % tn == 0, name

    m = (pairs[0][0][0] if isinstance(pairs[0][0], tuple) else pairs[0][0]).shape[0]
    n = mat_shape(pairs[0][1])[0] if trans_b else mat_shape(pairs[0][1])[1]
    tm, tn = min(tm, m), min(tn, n)
    assert m % tm == 0 and n % tn == 0, (name, m, n, tm, tn)
    npairs = len(pairs)
    dn = NT if trans_b else NN

    def body(*refs):
        o_ref = refs[-1]
        acc = None
        for i in range(npairs):
            p = _dot(refs[2 * i][...].astype(BF16), refs[2 * i + 1][...].astype(BF16), dn)
            acc = p if acc is None else acc + p
        if add is not None:
            acc = acc + refs[2 * npairs][...]
        o_ref[...] = acc.astype(out_dtype)

    in_specs, args = [], []
    for a, b in pairs:
        bshape = mat_shape(b)
        k = bshape[1] if trans_b else bshape[0]
        assert bshape == ((n, k) if trans_b else (k, n)), (name, bshape)
        a, qa = a if isinstance(a, tuple) else (a, 0)
        assert a.shape[0] == m and a.shape[1] % k == 0, (name, a.shape, k)
        in_specs.append(pl.BlockSpec((tm, k), lambda i, j, qa=qa: (i, qa)))
        if isinstance(b, tuple) and b[1] == "cols":
            b = b[0]
            per = b.shape[2] // tn
            in_specs.append(pl.BlockSpec((None, k, tn), lambda i, j, per=per: (j // per, 0, j % per)))
        elif isinstance(b, tuple):
            b, qb = b
            if trans_b:
                in_specs.append(pl.BlockSpec((None, tn, k), lambda i, j, qb=qb: (qb, j, 0)))
            else:
                in_specs.append(pl.BlockSpec((None, k, tn), lambda i, j, qb=qb: (qb, 0, j)))
        elif trans_b:
            in_specs.append(pl.BlockSpec((tn, k), lambda i, j: (j, 0)))
        else:
            in_specs.append(pl.BlockSpec((k, tn), lambda i, j: (0, j)))
        args += [a, b]
    if add is not None:
        in_specs.append(pl.BlockSpec((tm, tn), lambda i, j: (i, j)))
        args.append(add)
    if after is not None:
        in_specs.append(pl.BlockSpec(memory_space=pl.ANY))
        args.append(after)
    return pl.pallas_call(
        body, name=name, grid=(m // tm, n // tn), in_specs=in_specs,
        out_specs=pl.BlockSpec((tm, tn), lambda i, j: (i, j)),
        out_shape=jax.ShapeDtypeStruct((m, n), out_dtype),
        compiler_params=_params(("parallel", "parallel")),
    )(*args)


def _matmul_tn(a, b, *, tk, tn, tm=1024, out_dtype=BF16, stack_out=False, after=None, name):
    m, k = a.shape
    n = b.shape[1]
    tm, tk, tn = min(tm, m), min(tk, k), min(tn, n)
    assert m % tm == 0 and k % tk == 0 and n % tn == 0, (name, m, k, n)
    nm = m // tm
    if stack_out:
        out_spec = pl.BlockSpec((None, tk, tn), lambda i, j, l: (j, i, 0))
        out_shape = jax.ShapeDtypeStruct((n // tn, k, tn), out_dtype)
    else:
        out_spec = pl.BlockSpec((tk, tn), lambda i, j, l: (i, j))
        out_shape = jax.ShapeDtypeStruct((k, n), out_dtype)

    def body(a_ref, b_ref, *rest):
        o_ref, acc = rest[-2:]
        mi = pl.program_id(2)

        @pl.when(mi == 0)
        def _():
            acc[...] = jnp.zeros_like(acc)

        acc[...] += _dot(a_ref[...].astype(BF16), b_ref[...].astype(BF16), TN)

        @pl.when(mi == nm - 1)
        def _():
            o_ref[...] = acc[...].astype(out_dtype)

    in_specs = [pl.BlockSpec((tm, tk), lambda i, j, l: (l, i)), pl.BlockSpec((tm, tn), lambda i, j, l: (l, j))]
    args = [a, b]
    if after is not None:
        in_specs.append(pl.BlockSpec(memory_space=pl.ANY))
        args.append(after)
    return pl.pallas_call(
        body, name=name, grid=(k // tk, n // tn, nm), in_specs=in_specs,
        out_specs=out_spec, out_shape=out_shape,
        scratch_shapes=[pltpu.VMEM((tk, tn), F32)],
        compiler_params=_params(("parallel", "parallel", "arbitrary")),
    )(*args)


def _rms_fwd(x, w, *, after=None, name, tm=512):
    s, d = x.shape
    tm = min(tm, s)
    extra = [] if after is None else [after]

    def body(x_ref, w_ref, *rest):
        o_ref = rest[-1]
        xv = x_ref[...]
        r = lax.rsqrt(jnp.mean(xv * xv, axis=-1, keepdims=True) + NORM_EPS)
        o_ref[...] = (xv * r * w_ref[...]).astype(BF16)

    return pl.pallas_call(
        body, name=name, grid=(s // tm,),
        in_specs=[pl.BlockSpec((tm, d), lambda i: (i, 0)), pl.BlockSpec((1, d), lambda i: (0, 0))]
        + [pl.BlockSpec(memory_space=pl.ANY)] * len(extra),
        out_specs=pl.BlockSpec((tm, d), lambda i: (i, 0)),
        out_shape=jax.ShapeDtypeStruct((s, d), BF16),
        compiler_params=_params(("parallel",)),
    )(x, w, *extra)


def _rms_bwd(x, w, dn, dres, *, name, tm=512):
    s, d = x.shape
    tm = min(tm, s)

    def body(x_ref, w_ref, dn_ref, dres_ref, dx_ref, dxb_ref, dw_ref):
        @pl.when(pl.program_id(0) == 0)
        def _():
            dw_ref[...] = jnp.zeros_like(dw_ref)

        xv = x_ref[...]
        r = lax.rsqrt(jnp.mean(xv * xv, axis=-1, keepdims=True) + NORM_EPS)
        xhat = xv * r
        dnv = dn_ref[...].astype(F32)
        dxhat = dnv * w_ref[...]
        dx = dres_ref[...] + r * (dxhat - xhat * jnp.mean(dxhat * xhat, axis=-1, keepdims=True))
        dx_ref[...] = dx
        dxb_ref[...] = dx.astype(BF16)
        dw_ref[...] += jnp.sum(dnv * xhat, axis=0, keepdims=True)

    tile = pl.BlockSpec((tm, d), lambda i: (i, 0))
    row = pl.BlockSpec((1, d), lambda i: (0, 0))
    return pl.pallas_call(
        body, name=name, grid=(s // tm,),
        in_specs=[tile, row, tile, tile], out_specs=[tile, tile, row],
        out_shape=[jax.ShapeDtypeStruct((s, d), F32), jax.ShapeDtypeStruct((s, d), BF16),
                   jax.ShapeDtypeStruct((1, d), F32)],
        compiler_params=_params(("arbitrary",)),
    )(x, w, dn, dres)


def _final_fwd_bwd(h2, wf, target, *, name, tm=512):
    s, d = h2.shape
    tm = min(tm, s)

    def body(h_ref, w_ref, t_ref, loss_ref, dh_ref, dhb_ref, dw_ref):
        @pl.when(pl.program_id(0) == 0)
        def _():
            dw_ref[...] = jnp.zeros_like(dw_ref)
            loss_ref[...] = jnp.zeros_like(loss_ref)

        hv = h_ref[...]
        r = lax.rsqrt(jnp.mean(hv * hv, axis=-1, keepdims=True) + NORM_EPS)
        xhat = hv * r
        err = xhat * w_ref[...] - t_ref[...]
        per_tok = jnp.mean(err * err, axis=-1, keepdims=True)
        loss_ref[...] += 0.5 * jnp.sum(per_tok, axis=0, keepdims=True)
        dy = err * (1.0 / d)
        dxhat = dy * w_ref[...]
        dh = r * (dxhat - xhat * jnp.mean(dxhat * xhat, axis=-1, keepdims=True))
        dh_ref[...] = dh
        dhb_ref[...] = dh.astype(BF16)
        dw_ref[...] += jnp.sum(dy * xhat, axis=0, keepdims=True)

    tile = pl.BlockSpec((tm, d), lambda i: (i, 0))
    row = pl.BlockSpec((1, d), lambda i: (0, 0))
    return pl.pallas_call(
        body, name=name, grid=(s // tm,),
        in_specs=[tile, row, tile],
        out_specs=[pl.BlockSpec((1, 1), lambda i: (0, 0)), tile, tile, row],
        out_shape=[jax.ShapeDtypeStruct((1, 1), F32), jax.ShapeDtypeStruct((s, d), F32),
                   jax.ShapeDtypeStruct((s, d), BF16), jax.ShapeDtypeStruct((1, d), F32)],
        compiler_params=_params(("arbitrary",)),
    )(h2, wf, target)


CONV_ROWS = 512
HALO = 8


def _rows_with_halo(ref, r0, rows, s, before, after):
    tile = 16 if ref.dtype == BF16 else HALO
    parts = []
    if before:
        prev = ref[pl.ds(pl.multiple_of(jnp.maximum(r0 - tile, 0), tile), tile), :].astype(F32)[tile - HALO:]
        parts.append(jnp.where(r0 > 0, prev, 0.0))
    parts.append(ref[pl.ds(r0, rows), :].astype(F32))
    if after:
        nxt = ref[pl.ds(pl.multiple_of(jnp.minimum(r0 + rows, s - tile), tile), tile), :].astype(F32)[:HALO]
        parts.append(jnp.where(r0 + rows < s, nxt, 0.0))
    return jnp.concatenate(parts, axis=0) if len(parts) > 1 else parts[0]


def _window(x_ref, r0, s, after):
    return _rows_with_halo(x_ref, r0, CONV_ROWS, s, True, after).astype(F32)


def _shifted(window, k, rows):
    if k == 0:
        return window[HALO:HALO + rows]
    return pltpu.roll(window, k, 0)[HALO:HALO + rows]


def _conv_taps(window, w_ref, kk, rows):
    acc = None
    for i in range(kk):
        term = w_ref[i:i + 1, :] * _shifted(window, kk - 1 - i, rows)
        acc = term if acc is None else acc + term
    return acc


def _row_loop(s, step):
    def body(r, carry):
        return step(pl.multiple_of(r * CONV_ROWS, CONV_ROWS), carry)
    return body


def _conv_bwd_rows(window, dpe, w_ref, kk):
    dp = dpe[:CONV_ROWS]
    dx = None
    dws = []
    for i in range(kk):
        k = kk - 1 - i
        dws.append(jnp.sum(dp * _shifted(window, k, CONV_ROWS), axis=0, keepdims=True))
        later = dp if k == 0 else pltpu.roll(dpe, dpe.shape[0] - k, 0)[:CONV_ROWS]
        term = w_ref[i:i + 1, :] * later
        dx = term if dx is None else dx + term
    return dx, dws, jnp.sum(dp, axis=0, keepdims=True)


def _conv_a_fwd(xraw, w, b, *, name, tc=128):
    s, c = xraw.shape
    kk = 4

    def body(x_ref, w_ref, b_ref, o_ref):
        def step(r0, carry):
            pre = _conv_taps(_window(x_ref, r0, s, False), w_ref, kk, CONV_ROWS) + b_ref[...]
            o_ref[pl.ds(r0, CONV_ROWS), :] = pre * _sigmoid(pre)
            return carry

        lax.fori_loop(0, s // CONV_ROWS, _row_loop(s, step), 0)

    col = pl.BlockSpec((s, tc), lambda j: (0, j))
    return pl.pallas_call(
        body, name=name, grid=(c // tc,),
        in_specs=[col, pl.BlockSpec((8, tc), lambda j: (0, j)), pl.BlockSpec((1, tc), lambda j: (0, j))],
        out_specs=col, out_shape=jax.ShapeDtypeStruct((s, c), F32),
        compiler_params=_params(("parallel",)),
    )(xraw, w, b)


def _conv_a_bwd(xraw, w, b, dy, *, name, tc=128):
    s, c = xraw.shape
    kk = 4

    def body(x_ref, w_ref, b_ref, dy_ref, dx_ref, dw_ref, db_ref):
        def step(r0, carry):
            window = _window(x_ref, r0, s, True)
            pre = _conv_taps(window, w_ref, kk, CONV_ROWS + HALO) + b_ref[...]
            sg = _sigmoid(pre)
            dpe = _rows_with_halo(dy_ref, r0, CONV_ROWS, s, False, True) * (sg * (1.0 + pre * (1.0 - sg)))
            dx, dws, db = _conv_bwd_rows(window, dpe, w_ref, kk)
            dx_ref[pl.ds(r0, CONV_ROWS), :] = dx.astype(BF16)
            return tuple(acc + new for acc, new in zip(carry, dws + [db]))

        zero = jnp.zeros((1, tc), F32)
        sums = lax.fori_loop(0, s // CONV_ROWS, _row_loop(s, step), (zero,) * (kk + 1))
        db_ref[...] = sums[kk]
        dw_ref[...] = jnp.concatenate(list(sums[:kk]) + [jnp.zeros((8 - kk, tc), F32)], axis=0)

    col = pl.BlockSpec((s, tc), lambda j: (0, j))
    w8 = pl.BlockSpec((8, tc), lambda j: (0, j))
    row = pl.BlockSpec((1, tc), lambda j: (0, j))
    return pl.pallas_call(
        body, name=name, grid=(c // tc,),
        in_specs=[col, w8, row, col], out_specs=[col, w8, row],
        out_shape=[jax.ShapeDtypeStruct((s, c), BF16), jax.ShapeDtypeStruct((8, c), F32),
                   jax.ShapeDtypeStruct((1, c), F32)],
        compiler_params=_params(("parallel",)),
    )(xraw, w, b, dy)


def _conv_f_fwd(up_raw, w, b, *, name, tc=128):
    s, c2 = up_raw.shape
    c = c2 // 2
    nb = c // tc
    kk = 3

    def body(xa_ref, xv_ref, wa_ref, wv_ref, ba_ref, bv_ref, o_ref):
        def step(r0, carry):
            a = _conv_taps(_window(xa_ref, r0, s, False), wa_ref, kk, CONV_ROWS) + ba_ref[...]
            v = _conv_taps(_window(xv_ref, r0, s, False), wv_ref, kk, CONV_ROWS) + bv_ref[...]
            o_ref[pl.ds(r0, CONV_ROWS), :] = (a * _sigmoid(a) * v).astype(BF16)
            return carry

        lax.fori_loop(0, s // CONV_ROWS, _row_loop(s, step), 0)

    col_a = pl.BlockSpec((s, tc), lambda j: (0, j))
    col_v = pl.BlockSpec((s, tc), lambda j: (0, j + nb))
    return pl.pallas_call(
        body, name=name, grid=(nb,),
        in_specs=[col_a, col_v, pl.BlockSpec((8, tc), lambda j: (0, j)), pl.BlockSpec((8, tc), lambda j: (0, j + nb)),
                  pl.BlockSpec((1, tc), lambda j: (0, j)), pl.BlockSpec((1, tc), lambda j: (0, j + nb))],
        out_specs=col_a, out_shape=jax.ShapeDtypeStruct((s, c), BF16),
        compiler_params=_params(("parallel",)),
    )(up_raw, up_raw, w, w, b, b)


def _conv_f_bwd(up_raw, w, b, dact, *, name, tc=128):
    s, c2 = up_raw.shape
    c = c2 // 2
    nb = c // tc
    kk = 3

    def body(xa_ref, xv_ref, wa_ref, wv_ref, ba_ref, bv_ref, d_ref,
             dxa_ref, dxv_ref, dwa_ref, dwv_ref, dba_ref, dbv_ref):
        def step(r0, carry):
            win_a = _window(xa_ref, r0, s, True)
            win_v = _window(xv_ref, r0, s, True)
            a = _conv_taps(win_a, wa_ref, kk, CONV_ROWS + HALO) + ba_ref[...]
            v = _conv_taps(win_v, wv_ref, kk, CONV_ROWS + HALO) + bv_ref[...]
            sg = _sigmoid(a)
            d = _rows_with_halo(d_ref, r0, CONV_ROWS, s, False, True)
            dxa, dwas, dba = _conv_bwd_rows(win_a, d * v * (sg * (1.0 + a * (1.0 - sg))), wa_ref, kk)
            dxv, dwvs, dbv = _conv_bwd_rows(win_v, d * (a * sg), wv_ref, kk)
            dxa_ref[pl.ds(r0, CONV_ROWS), :] = dxa.astype(BF16)
            dxv_ref[pl.ds(r0, CONV_ROWS), :] = dxv.astype(BF16)
            return tuple(acc + new for acc, new in zip(carry, dwas + [dba] + dwvs + [dbv]))

        zero = jnp.zeros((1, tc), F32)
        sums = lax.fori_loop(0, s // CONV_ROWS, _row_loop(s, step), (zero,) * (2 * kk + 2))
        pad = [jnp.zeros((8 - kk, tc), F32)]
        dwa_ref[...] = jnp.concatenate(list(sums[:kk]) + pad, axis=0)
        dba_ref[...] = sums[kk]
        dwv_ref[...] = jnp.concatenate(list(sums[kk + 1:2 * kk + 1]) + pad, axis=0)
        dbv_ref[...] = sums[2 * kk + 1]

    col_a = pl.BlockSpec((s, tc), lambda j: (0, j))
    col_v = pl.BlockSpec((s, tc), lambda j: (0, j + nb))
    w_a = pl.BlockSpec((8, tc), lambda j: (0, j))
    w_v = pl.BlockSpec((8, tc), lambda j: (0, j + nb))
    r_a = pl.BlockSpec((1, tc), lambda j: (0, j))
    r_v = pl.BlockSpec((1, tc), lambda j: (0, j + nb))
    outs = pl.pallas_call(
        body, name=name, grid=(nb,),
        in_specs=[col_a, col_v, w_a, w_v, r_a, r_v, col_a],
        out_specs=[col_a, col_a, w_a, w_a, r_a, r_a],
        out_shape=[jax.ShapeDtypeStruct((s, c), BF16), jax.ShapeDtypeStruct((s, c), BF16),
                   jax.ShapeDtypeStruct((8, c), F32), jax.ShapeDtypeStruct((8, c), F32),
                   jax.ShapeDtypeStruct((1, c), F32), jax.ShapeDtypeStruct((1, c), F32)],
        compiler_params=_params(("parallel",)),
    )(up_raw, up_raw, w, w, b, b, dact)
    return outs


def _tri_masks():
    row = lax.broadcasted_iota(jnp.int32, (CHUNK, CHUNK), 0)
    col = lax.broadcasted_iota(jnp.int32, (CHUNK, CHUNK), 1)
    return row >= col, row <= col


def _ssd_fwd(xbc, dt_raw, z, dt_bias, a_log, a_log_x, d_skip_x, norm_w, expand, *, name):
    s = xbc.shape[0]
    nc = s // CHUNK

    def body(xbc_ref, dtr_ref, z_ref, dtb_ref, alog_ref, alogx_ref, dskx_ref, nw_ref, e_ref,
             y_ref, ya_ref, st_ref, state):
        @pl.when(pl.program_id(0) == 0)
        def _():
            state[...] = jnp.zeros_like(state)

        st_ref[0] = state[...]
        lower, _ = _tri_masks()
        dt = _softplus(dtr_ref[...] + dtb_ref[...])
        adt = dt * (-jnp.exp(alog_ref[...]))
        acum = _dot_exact_lhs(lower.astype(BF16), _split3(adt))
        acum_t = acum.T
        dt_terms, acum_terms = _split3(dt), _split3(acum)
        for g in range(SSD_GROUPS):
            sl = slice(GROUP_COLS * g, GROUP_COLS * (g + 1))
            dt_x = _dot_terms(dt_terms, e_ref[:, sl])
            acum_x = _dot_terms(acum_terms, e_ref[:, sl])
            tot_x = jnp.sum(dt_x * (-jnp.exp(alogx_ref[:, sl])), axis=0, keepdims=True)
            xs = xbc_ref[:, sl]
            xdt = xs * dt_x
            xdt_b = xdt.astype(BF16)
            bg = xbc_ref[:, SSD_D_INNER + SSD_STATE * g:SSD_D_INNER + SSD_STATE * (g + 1)].astype(BF16)
            cg = xbc_ref[:, SSD_D_INNER + SSD_BC + SSD_STATE * g:SSD_D_INNER + SSD_BC + SSD_STATE * (g + 1)].astype(BF16)
            cb = _dot(cg, bg, NT)
            st_g = state[:, sl]
            y_off = _dot(cg, st_g.astype(BF16)) * jnp.exp(acum_x)
            parts = []
            for r in range(SSD_HEADS_PER_GROUP):
                h = SSD_HEADS_PER_GROUP * g + r
                dec = jnp.exp(jnp.where(lower, acum[:, h:h + 1] - acum_t[h:h + 1, :], -jnp.inf))
                parts.append(_dot((cb * dec).astype(BF16), xdt_b[:, SSD_HEAD_DIM * r:SSD_HEAD_DIM * (r + 1)]))
            y_ref[:, sl] = jnp.concatenate(parts, axis=1) + y_off + dskx_ref[:, sl] * xs
            wgt = (xdt * jnp.exp(tot_x - acum_x)).astype(BF16)
            state[:, sl] = st_g * jnp.exp(tot_x) + _dot(bg, wgt, TN)
        zv = z_ref[...].astype(F32)
        q = y_ref[...] * (zv * _sigmoid(zv))
        r = lax.rsqrt(jnp.mean(q * q, axis=-1, keepdims=True) + NORM_EPS)
        ya_ref[...] = (q * r * nw_ref[...]).astype(BF16)

    def chunk(w):
        return pl.BlockSpec((CHUNK, w), lambda c: (c, 0))

    def const(shape):
        return pl.BlockSpec(shape, lambda c: (0,) * len(shape))

    return pl.pallas_call(
        body, name=name, grid=(nc,),
        in_specs=[chunk(SSD_XBC), chunk(LANES), chunk(SSD_D_INNER), const((1, LANES)), const((1, LANES)),
                  const((1, SSD_D_INNER)), const((1, SSD_D_INNER)), const((1, SSD_D_INNER)),
                  const((LANES, SSD_D_INNER))],
        out_specs=[chunk(SSD_D_INNER), chunk(SSD_D_INNER),
                   pl.BlockSpec((1, SSD_STATE, SSD_D_INNER), lambda c: (c, 0, 0))],
        out_shape=[jax.ShapeDtypeStruct((s, SSD_D_INNER), F32), jax.ShapeDtypeStruct((s, SSD_D_INNER), BF16),
                   jax.ShapeDtypeStruct((nc, SSD_STATE, SSD_D_INNER), F32)],
        scratch_shapes=[pltpu.VMEM((SSD_STATE, SSD_D_INNER), F32)],
        compiler_params=_params(("arbitrary",)),
    )(xbc, dt_raw, z, dt_bias, a_log, a_log_x, d_skip_x, norm_w, expand)


def _ssd_bwd(dya, y, z, xbc, dt_raw, states, dt_bias, a_log, a_log_x, d_skip_x, norm_w, expand, expand_t, *, name):
    s = xbc.shape[0]
    nc = s // CHUNK

    def body(dya_ref, y_ref, z_ref, xbc_ref, dtr_ref, stp_ref, dtb_ref, alog_ref, alogx_ref, dskx_ref, nw_ref,
             e_ref, et_ref, dz_ref, dxbc_ref, ddt_ref, dnw_ref, ddsk_ref, dalog_ref, ddtb_ref,
             dstate, dy_sc, dskcol):
        i = pl.program_id(0)

        @pl.when(i == 0)
        def _():
            dstate[...] = jnp.zeros_like(dstate)
            dskcol[...] = jnp.zeros_like(dskcol)
            dnw_ref[...] = jnp.zeros_like(dnw_ref)
            dalog_ref[...] = jnp.zeros_like(dalog_ref)
            ddtb_ref[...] = jnp.zeros_like(ddtb_ref)
            ddsk_ref[...] = jnp.zeros_like(ddsk_ref)

        lower, upper = _tri_masks()
        rows = lax.broadcasted_iota(jnp.int32, (CHUNK, LANES), 0)
        pre = dtr_ref[...] + dtb_ref[...]
        dt = _softplus(pre)
        a = -jnp.exp(alog_ref[...])
        acum = _dot_exact_lhs(lower.astype(BF16), _split3(dt * a))
        acum_t = acum.T
        dt_terms, acum_terms = _split3(dt), _split3(acum)

        yv = y_ref[...]
        zv = z_ref[...].astype(F32)
        sz = _sigmoid(zv)
        silu_z = zv * sz
        q = yv * silu_z
        r = lax.rsqrt(jnp.mean(q * q, axis=-1, keepdims=True) + NORM_EPS)
        qhat = q * r
        dyav = dya_ref[...]
        dqhat = dyav * nw_ref[...]
        dnw_ref[...] += jnp.sum(dyav * qhat, axis=0, keepdims=True)
        dq = r * (dqhat - qhat * jnp.mean(dqhat * qhat, axis=-1, keepdims=True))
        dy_sc[...] = dq * silu_z
        dz_ref[...] = (dq * yv * (sz * (1.0 + zv * (1.0 - sz)))).astype(BF16)

        da_cum = jnp.zeros((CHUNK, LANES), F32)
        ddt = jnp.zeros((CHUNK, LANES), F32)
        for g in range(SSD_GROUPS):
            sl = slice(GROUP_COLS * g, GROUP_COLS * (g + 1))
            et_g = et_ref[sl, :]
            dt_x = _dot_terms(dt_terms, e_ref[:, sl])
            acum_x = _dot_terms(acum_terms, e_ref[:, sl])
            tot_x = jnp.sum(dt_x * (-jnp.exp(alogx_ref[:, sl])), axis=0, keepdims=True)
            e_tot = jnp.exp(tot_x)
            dec_s = jnp.exp(tot_x - acum_x)
            xs = xbc_ref[:, sl]
            xdt = xs * dt_x
            xdt_b = xdt.astype(BF16)
            dy = dy_sc[:, sl]
            dy_b = dy.astype(BF16)
            dskx = dskx_ref[:, sl]
            y_ssd = y_ref[:, sl] - dskx * xs
            dskcol[:, sl] += jnp.sum(dy * xs, axis=0, keepdims=True)
            bg = xbc_ref[:, SSD_D_INNER + SSD_STATE * g:SSD_D_INNER + SSD_STATE * (g + 1)].astype(BF16)
            cg = xbc_ref[:, SSD_D_INNER + SSD_BC + SSD_STATE * g:SSD_D_INNER + SSD_BC + SSD_STATE * (g + 1)].astype(BF16)
            cb_t = _dot(bg, cg, NT)
            sp = stp_ref[0, :, sl]
            ds_g = dstate[:, sl]
            ds_b = ds_g.astype(BF16)
            dye_b = (dy * jnp.exp(acum_x)).astype(BF16)
            dc = _dot(dye_b, sp.astype(BF16), NT)
            dxdt_state = dec_s * _dot(bg, ds_b)
            db = _dot((xdt * dec_s).astype(BF16), ds_b, NT)
            dcb_t = jnp.zeros((CHUNK, CHUNK), F32)
            parts = []
            for rr in range(SSD_HEADS_PER_GROUP):
                h = SSD_HEADS_PER_GROUP * g + rr
                hs = slice(SSD_HEAD_DIM * rr, SSD_HEAD_DIM * (rr + 1))
                dec_t = jnp.exp(jnp.where(upper, acum_t[h:h + 1, :] - acum[:, h:h + 1], -jnp.inf))
                parts.append(_dot((cb_t * dec_t).astype(BF16), dy_b[:, hs]))
                dcb_t = dcb_t + _dot(xdt_b[:, hs], dy_b[:, hs], NT) * dec_t
            dxdt = jnp.concatenate(parts, axis=1) + dxdt_state
            dcb_tb = dcb_t.astype(BF16)
            dc = dc + _dot(dcb_tb, bg, TN)
            db = db + _dot(dcb_tb, cg)
            tot_col = jnp.sum(ds_g * sp, axis=0, keepdims=True) * e_tot + jnp.sum(dxdt_state * xdt, axis=0, keepdims=True)
            d_tot = _dot_terms(_split3(jnp.broadcast_to(tot_col, (8, GROUP_COLS))), et_g)
            d_tot = jnp.max(d_tot, axis=0, keepdims=True)
            pair_sums = dy_b.astype(F32) * y_ssd - xdt_b.astype(F32) * dxdt
            da_cum = da_cum + _dot_terms(_split3(pair_sums), et_g) + jnp.where(rows == CHUNK - 1, d_tot, 0.0)
            ddt = ddt + _dot_terms(_split3(dxdt * xs), et_g)
            dxbc_ref[:, sl] = dy * dskx + dxdt * dt_x
            dxbc_ref[:, SSD_D_INNER + SSD_STATE * g:SSD_D_INNER + SSD_STATE * (g + 1)] = db
            dxbc_ref[:, SSD_D_INNER + SSD_BC + SSD_STATE * g:SSD_D_INNER + SSD_BC + SSD_STATE * (g + 1)] = dc
            dstate[:, sl] = e_tot * ds_g + _dot(cg, dye_b, TN)

        dadt = _dot_exact_lhs(upper.astype(BF16), _split3(da_cum))
        ddt = ddt + dadt * a
        dalog_ref[...] += jnp.sum(dadt * dt, axis=0, keepdims=True)
        dpre = ddt * _sigmoid(pre)
        ddtb_ref[...] += jnp.sum(dpre, axis=0, keepdims=True)
        ddt_ref[...] = dpre.astype(BF16)

        @pl.when(i == nc - 1)
        def _():
            dalog_ref[...] = dalog_ref[...] * a
            dsk = _dot_terms(_split3(jnp.broadcast_to(dskcol[...], (8, SSD_D_INNER))), et_ref[...])
            ddsk_ref[...] = jnp.max(dsk, axis=0, keepdims=True)

    def chunk(w):
        return pl.BlockSpec((CHUNK, w), lambda i: (nc - 1 - i, 0))

    def const(shape):
        return pl.BlockSpec(shape, lambda i: (0,) * len(shape))

    return pl.pallas_call(
        body, name=name, grid=(nc,),
        in_specs=[chunk(SSD_D_INNER), chunk(SSD_D_INNER), chunk(SSD_D_INNER), chunk(SSD_XBC), chunk(LANES),
                  pl.BlockSpec((1, SSD_STATE, SSD_D_INNER), lambda i: (nc - 1 - i, 0, 0)),
                  const((1, LANES)), const((1, LANES)), const((1, SSD_D_INNER)), const((1, SSD_D_INNER)),
                  const((1, SSD_D_INNER)), const((LANES, SSD_D_INNER)), const((SSD_D_INNER, LANES))],
        out_specs=[chunk(SSD_D_INNER), chunk(SSD_XBC), chunk(LANES), const((1, SSD_D_INNER)), const((1, LANES)),
                   const((1, LANES)), const((1, LANES))],
        out_shape=[jax.ShapeDtypeStruct((s, SSD_D_INNER), BF16), jax.ShapeDtypeStruct((s, SSD_XBC), F32),
                   jax.ShapeDtypeStruct((s, LANES), BF16), jax.ShapeDtypeStruct((1, SSD_D_INNER), F32),
                   jax.ShapeDtypeStruct((1, LANES), F32), jax.ShapeDtypeStruct((1, LANES), F32),
                   jax.ShapeDtypeStruct((1, LANES), F32)],
        scratch_shapes=[pltpu.VMEM((SSD_STATE, SSD_D_INNER), F32), pltpu.VMEM((CHUNK, SSD_D_INNER), F32),
                        pltpu.VMEM((1, SSD_D_INNER), F32)],
        compiler_params=_params(("arbitrary",)),
    )(dya, y, z, xbc, dt_raw, states, dt_bias, a_log, a_log_x, d_skip_x, norm_w, expand, expand_t)


GELU_K = math.sqrt(2.0 / math.pi)
GELU_C = 0.044715


def _gelu(x):
    return 0.5 * x * (1.0 + jnp.tanh(GELU_K * (x + GELU_C * x * x * x)))


def _gelu_grad(x):
    t = jnp.tanh(GELU_K * (x + GELU_C * x * x * x))
    return 0.5 * (1.0 + t) + 0.5 * x * (1.0 - t * t) * (GELU_K * (1.0 + 3.0 * GELU_C * x * x))


def _sgu_pre(uv_ref, uvb_ref, lnw_ref, lnb_ref):
    uv = uv_ref[...].astype(F32) + uvb_ref[...]
    guv = _gelu(uv)
    u = guv[:, :SGU_WIDTH]
    v = guv[:, SGU_WIDTH:]
    mu = jnp.mean(v, axis=-1, keepdims=True)
    vc = v - mu
    rstd = lax.rsqrt(jnp.mean(vc * vc, axis=-1, keepdims=True) + LN_EPS)
    vhat = vc * rstd
    vn = vhat * lnw_ref[...] + lnb_ref[...]
    return uv, u, vhat, rstd, vn


def _sgu_fwd(uv_raw, uv_b, ln_w, ln_b, w_sp, b_sp_t, *, name):
    s = uv_raw.shape[0]
    nc = s // CHUNK

    def body(uv_ref, uvb_ref, lnw_ref, lnb_ref, w_ref, bt_ref, o_ref):
        lower, _ = _tri_masks()
        _, u, _, _, vn = _sgu_pre(uv_ref, uvb_ref, lnw_ref, lnb_ref)
        vn_b = vn.astype(BF16)
        bt = bt_ref[...]
        for g in range(SGU_GROUPS):
            gs = slice(LANES * g, LANES * (g + 1))
            wc = jnp.where(lower, w_ref[g], 0.0).astype(BF16)
            mixed = _dot(wc, vn_b[:, gs]) + bt[:, g:g + 1]
            o_ref[:, gs] = (u[:, gs] * mixed).astype(BF16)

    def const(shape):
        return pl.BlockSpec(shape, lambda c: (0,) * len(shape))

    return pl.pallas_call(
        body, name=name, grid=(nc,),
        in_specs=[pl.BlockSpec((CHUNK, 2 * SGU_WIDTH), lambda c: (c, 0)), const((1, 2 * SGU_WIDTH)),
                  const((1, SGU_WIDTH)), const((1, SGU_WIDTH)), const((SGU_GROUPS, CHUNK, CHUNK)),
                  const((CHUNK, LANES))],
        out_specs=pl.BlockSpec((CHUNK, SGU_WIDTH), lambda c: (c, 0)),
        out_shape=jax.ShapeDtypeStruct((s, SGU_WIDTH), BF16),
        compiler_params=_params(("parallel",)),
    )(uv_raw, uv_b, ln_w, ln_b, w_sp, b_sp_t)


def _sgu_bwd(uv_raw, dyb, uv_b, ln_w, ln_b, w_sp, b_sp_t, group_sum, *, name):
    s = uv_raw.shape[0]
    nc = s // CHUNK

    def body(uv_ref, dy_ref, uvb_ref, lnw_ref, lnb_ref, w_ref, bt_ref, gsum_ref,
             duv_ref, dw_ref, dbt_ref, dlnw_ref, dlnb_ref, duvb_ref):
        @pl.when(pl.program_id(0) == 0)
        def _():
            dw_ref[...] = jnp.zeros_like(dw_ref)
            dbt_ref[...] = jnp.zeros_like(dbt_ref)
            dlnw_ref[...] = jnp.zeros_like(dlnw_ref)
            dlnb_ref[...] = jnp.zeros_like(dlnb_ref)
            duvb_ref[...] = jnp.zeros_like(duvb_ref)

        lower, _ = _tri_masks()
        uv, u, vhat, rstd, vn = _sgu_pre(uv_ref, uvb_ref, lnw_ref, lnb_ref)
        vn_b = vn.astype(BF16)
        bt = bt_ref[...]
        dy = dy_ref[...].astype(F32)
        du_parts, dvn_parts, dmix_parts = [], [], []
        for g in range(SGU_GROUPS):
            gs = slice(LANES * g, LANES * (g + 1))
            wc = jnp.where(lower, w_ref[g], 0.0).astype(BF16)
            mixed = _dot(wc, vn_b[:, gs]) + bt[:, g:g + 1]
            du_parts.append(dy[:, gs] * mixed)
            dmix = dy[:, gs] * u[:, gs]
            dmix_b = dmix.astype(BF16)
            dmix_parts.append(dmix)
            dw_ref[g] += jnp.where(lower, _dot(dmix_b, vn_b[:, gs], NT), 0.0)
            dvn_parts.append(_dot(wc, dmix_b, TN))
        dmixed = jnp.concatenate(dmix_parts, axis=1)
        dbt_ref[...] += _dot_terms(_split3(dmixed), gsum_ref[...])
        dvn = jnp.concatenate(dvn_parts, axis=1)
        dlnw_ref[...] += jnp.sum(dvn * vhat, axis=0, keepdims=True)
        dlnb_ref[...] += jnp.sum(dvn, axis=0, keepdims=True)
        dvhat = dvn * lnw_ref[...]
        dv = rstd * (dvhat - jnp.mean(dvhat, axis=-1, keepdims=True)
                     - vhat * jnp.mean(dvhat * vhat, axis=-1, keepdims=True))
        dguv = jnp.concatenate(du_parts + [dv], axis=1)
        duv = dguv * _gelu_grad(uv)
        duvb_ref[...] += jnp.sum(duv, axis=0, keepdims=True)
        duv_ref[...] = duv.astype(BF16)

    def const(shape):
        return pl.BlockSpec(shape, lambda c: (0,) * len(shape))

    return pl.pallas_call(
        body, name=name, grid=(nc,),
        in_specs=[pl.BlockSpec((CHUNK, 2 * SGU_WIDTH), lambda c: (c, 0)),
                  pl.BlockSpec((CHUNK, SGU_WIDTH), lambda c: (c, 0)), const((1, 2 * SGU_WIDTH)),
                  const((1, SGU_WIDTH)), const((1, SGU_WIDTH)), const((SGU_GROUPS, CHUNK, CHUNK)),
                  const((CHUNK, LANES)), const((SGU_WIDTH, LANES))],
        out_specs=[pl.BlockSpec((CHUNK, 2 * SGU_WIDTH), lambda c: (c, 0)), const((SGU_GROUPS, CHUNK, CHUNK)),
                   const((CHUNK, LANES)), const((1, SGU_WIDTH)), const((1, SGU_WIDTH)), const((1, 2 * SGU_WIDTH))],
        out_shape=[jax.ShapeDtypeStruct((s, 2 * SGU_WIDTH), BF16),
                   jax.ShapeDtypeStruct((SGU_GROUPS, CHUNK, CHUNK), F32), jax.ShapeDtypeStruct((CHUNK, LANES), F32),
                   jax.ShapeDtypeStruct((1, SGU_WIDTH), F32), jax.ShapeDtypeStruct((1, SGU_WIDTH), F32),
                   jax.ShapeDtypeStruct((1, 2 * SGU_WIDTH), F32)],
        compiler_params=_params(("arbitrary",)),
    )(uv_raw, dyb, uv_b, ln_w, ln_b, w_sp, b_sp_t, group_sum)


def _gate_fwd(gates_raw, b_gate, p_a, p_b, *, name, tm=512):
    s = p_a.shape[0]
    tm = min(tm, s)

    def body(ga_ref, gb_ref, ba_ref, bb_ref, pa_ref, pb_ref, o_ref):
        ga = _sigmoid(ga_ref[...].astype(F32) + ba_ref[...])
        gb = _sigmoid(gb_ref[...].astype(F32) + bb_ref[...])
        o_ref[...] = (ga * pa_ref[...].astype(F32) + gb * pb_ref[...].astype(F32)).astype(BF16)

    t_a = pl.BlockSpec((tm, D_MODEL), lambda i: (i, 0))
    t_b = pl.BlockSpec((tm, D_MODEL), lambda i: (i, 1))
    r_a = pl.BlockSpec((1, D_MODEL), lambda i: (0, 0))
    r_b = pl.BlockSpec((1, D_MODEL), lambda i: (0, 1))
    return pl.pallas_call(
        body, name=name, grid=(s // tm,),
        in_specs=[t_a, t_b, r_a, r_b, t_a, t_a], out_specs=t_a,
        out_shape=jax.ShapeDtypeStruct((s, D_MODEL), BF16),
        compiler_params=_params(("parallel",)),
    )(gates_raw, gates_raw, b_gate, b_gate, p_a, p_b)


def _gate_bwd(gates_raw, b_gate, p_a, p_b, dm, *, name, tm=512):
    s = p_a.shape[0]
    tm = min(tm, s)

    def body(ga_ref, gb_ref, ba_ref, bb_ref, pa_ref, pb_ref, dm_ref, dpa_ref, dpb_ref, dga_ref, dgb_ref,
             dba_ref, dbb_ref):
        @pl.when(pl.program_id(0) == 0)
        def _():
            dba_ref[...] = jnp.zeros_like(dba_ref)
            dbb_ref[...] = jnp.zeros_like(dbb_ref)

        d = dm_ref[...].astype(F32)
        for g_ref, b_ref, p_ref, dp_ref, dg_ref, db_ref in ((ga_ref, ba_ref, pa_ref, dpa_ref, dga_ref, dba_ref),
                                                            (gb_ref, bb_ref, pb_ref, dpb_ref, dgb_ref, dbb_ref)):
            sg = _sigmoid(g_ref[...].astype(F32) + b_ref[...])
            dp_ref[...] = (d * sg).astype(BF16)
            dg = d * p_ref[...].astype(F32) * (sg * (1.0 - sg))
            dg_ref[...] = dg.astype(BF16)
            db_ref[...] += jnp.sum(dg, axis=0, keepdims=True)

    t_a = pl.BlockSpec((tm, D_MODEL), lambda i: (i, 0))
    t_b = pl.BlockSpec((tm, D_MODEL), lambda i: (i, 1))
    r_a = pl.BlockSpec((1, D_MODEL), lambda i: (0, 0))
    r_b = pl.BlockSpec((1, D_MODEL), lambda i: (0, 1))
    big = jax.ShapeDtypeStruct((s, D_MODEL), BF16)
    row = jax.ShapeDtypeStruct((1, D_MODEL), F32)
    return pl.pallas_call(
        body, name=name, grid=(s // tm,),
        in_specs=[t_a, t_b, r_a, r_b, t_a, t_a, t_a], out_specs=[t_a, t_a, t_a, t_a, r_a, r_a],
        out_shape=[big, big, big, big, row, row],
        compiler_params=_params(("arbitrary",)),
    )(gates_raw, gates_raw, b_gate, b_gate, p_a, p_b, dm)


def _adamw_update(w_ref, g_ref, m_ref, v_ref, d_ref, mo_ref, vo_ref):
    gv = g_ref[...]
    mn = ADAM_B1 * m_ref[...] + (1.0 - ADAM_B1) * gv
    vn = ADAM_B2 * v_ref[...] + (1.0 - ADAM_B2) * (gv * gv)
    m_hat = mn / (1.0 - ADAM_B1 ** ADAM_STEP)
    v_hat = vn / (1.0 - ADAM_B2 ** ADAM_STEP)
    d_ref[...] = -ADAM_LR * (m_hat / (jnp.sqrt(v_hat) + ADAM_EPS) + ADAM_WD * w_ref[...])
    mo_ref[...] = mn
    vo_ref[...] = vn


def _adamw_many(ws, gs, ms, vs, *, name):
    n = len(ws)

    def body(*refs):
        for i in range(n):
            _adamw_update(*[refs[k * n + i] for k in range(7)])

    whole = pl.BlockSpec(memory_space=pltpu.VMEM)
    sds = [jax.ShapeDtypeStruct(w.shape, F32) for w in ws]
    outs = pl.pallas_call(
        body, name=name, in_specs=[whole] * (4 * n), out_specs=[whole] * (3 * n), out_shape=sds * 3,
        compiler_params=pltpu.CompilerParams(vmem_limit_bytes=VMEM_LIMIT),
    )(*ws, *gs, *ms, *vs)
    return outs[:n], outs[n:2 * n], outs[2 * n:]


def _adamw(w, g, m, v, *, name, tr=128):
    r, c = w.shape
    tr = min(tr, r)
    assert r % tr == 0, (name, r, tr)
    body = functools.partial(_adamw_update)

    blk = pl.BlockSpec((tr, c), lambda i: (i, 0))
    sds = jax.ShapeDtypeStruct((r, c), F32)
    return pl.pallas_call(
        body, name=name, grid=(r // tr,), in_specs=[blk] * 4, out_specs=[blk] * 3, out_shape=[sds] * 3,
        compiler_params=_params(("parallel",)),
    )(w, g, m, v)


def _adamw_two_sums(w, g_a, g_b, m, v, *, name, tr=128):
    r, c = w.shape
    tr = min(tr, r)
    assert r % tr == 0, (name, r, tr)

    def body(w_ref, ga_ref, gb_ref, m_ref, v_ref, g_ref, d_ref, mo_ref, vo_ref):
        g_ref[...] = ga_ref[...] + gb_ref[...]
        _adamw_update(w_ref, g_ref, m_ref, v_ref, d_ref, mo_ref, vo_ref)

    blk = pl.BlockSpec((tr, c), lambda i: (i, 0))
    sds = jax.ShapeDtypeStruct((r, c), F32)
    return pl.pallas_call(
        body, name=name, grid=(r // tr,), in_specs=[blk] * 5, out_specs=[blk] * 4, out_shape=[sds] * 4,
        compiler_params=_params(("parallel",)),
    )(w, g_a, g_b, m, v)


def _tile(n, pref):
    if n <= pref:
        return n
    best = LANES
    for t in range(LANES, pref + 1, LANES):
        if n % t == 0:
            best = t
    return best


MATMUL_BLOCK_BYTES = 20 * 1024 * 1024


def _mm(pairs, name, **kw):
    trans_b = kw.get("trans_b", False)
    m = (pairs[0][0][0] if isinstance(pairs[0][0], tuple) else pairs[0][0]).shape[0]
    ktot, n = 0, None
    for _, b in pairs:
        shape = b[0].shape[1:] if isinstance(b, tuple) else b.shape
        ktot += shape[1] if trans_b else shape[0]
        n = shape[0] if trans_b else shape[1]
    out_bytes = 4 * (2 if kw.get("add") is not None else 1)
    best = None
    for tm in (256, 512, 1024, 2048):
        for tn in range(LANES, min(n, 1536) + 1, LANES):
            if m % min(tm, m) or n % tn:
                continue
            fits = 2 * ktot * (min(tm, m) + tn) + out_bytes * min(tm, m) * tn <= MATMUL_BLOCK_BYTES
            if fits and (best is None or min(tm, m) * tn >= best[0] * best[1]):
                best = (min(tm, m), tn)
    return _matmul(pairs, tm=best[0], tn=best[1], name=name, **kw)


def _wgrad(a, b, name, **kw):
    return _matmul_tn(a, b, tk=_tile(a.shape[1], 1408), tn=kw.pop("tn", _tile(b.shape[1], 1024)), tm=2048,
                      name=name, **kw)


def _local_step(x, target, get_weight, small, emit_grad):
    heads = jnp.arange(SSD_D_INNER) // SSD_HEAD_DIM
    expand = (jnp.arange(LANES)[:, None] == heads[None, :]).astype(BF16)
    expand_t = expand.T
    group_sum = (jnp.arange(SGU_WIDTH)[:, None] // LANES == jnp.arange(LANES)[None, :]).astype(BF16)
    pad_h = LANES - SSD_HEADS
    dt_bias = jnp.pad(small["dt_bias"], ((0, 0), (0, pad_h)))
    a_log = jnp.pad(small["a_log"], ((0, 0), (0, pad_h)))
    a_log_x = jnp.repeat(small["a_log"], SSD_HEAD_DIM, axis=1)
    d_skip_x = jnp.repeat(small["d_skip"], SSD_HEAD_DIM, axis=1)
    b_sp_t = jnp.pad(small["b_spatial"][0].T, ((0, 0), (0, LANES - SGU_GROUPS)))
    w_sp = small["w_spatial"][0]
    conv_a_w = jnp.pad(small["conv_a_w"], ((0, 4), (0, 0)))
    conv_f_w = jnp.pad(small["conv_f_w"], ((0, 5), (0, 0)))
    final_w = small["final_norm_w"].reshape(1, D_MODEL)

    n1 = _rms_fwd(x, small["norm1_w"], after=small.get("gathers_started"), name="rms1_fwd")
    wts = dict(get_weight("w_in", n1))
    z = _mm([(n1, wts["in_z"])], "in_z")
    xbc_raw = _mm([(n1, wts["in_xbc"])], "in_xbc")
    dt_raw = _mm([(n1, wts["in_dt"])], "in_dt")
    uv_raw = _mm([(n1, wts["in_uv"])], "in_uv", out_dtype=BF16)
    gates_raw = _mm([(n1, wts["in_gate"])], "in_gate", out_dtype=BF16)
    xbc = _conv_a_fwd(xbc_raw, conv_a_w, small["conv_a_b"], name="conv_a_fwd")
    y, y_a, states = _ssd_fwd(xbc, dt_raw, z, dt_bias, a_log, a_log_x, d_skip_x, small["ssd_norm_w"], expand,
                              name="ssd_fwd")
    y_b = _sgu_fwd(uv_raw, small["uv_b"], small["v_ln_w"], small["v_ln_b"], w_sp, b_sp_t, name="sgu_fwd")
    wts.update(get_weight("w_branch", y_b))
    p_a = _mm([(y_a, wts["branch_a"])], "branch_a", out_dtype=BF16)
    p_b = _mm([(y_b, wts["branch_b"])], "branch_b", out_dtype=BF16)
    mix = _gate_fwd(gates_raw, small["b_gate"], p_a, p_b, name="gate_fwd")
    wts.update(get_weight("w_out", mix))
    h1 = _mm([(mix, wts["out"])], "out_proj", add=x)
    n2 = _rms_fwd(h1, small["norm2_w"], name="rms2_fwd")
    wts.update(get_weight("w_up", n2))
    up_w = wts["up"]
    up_cols = up_w.shape[2]
    up_raw = _matmul([(n2, (up_w, "cols"))], tm=2048, tn=up_cols, out_dtype=BF16, name="up_proj")
    act = _conv_f_fwd(up_raw, conv_f_w, small["conv_f_b"], name="conv_f_fwd")
    wts.update(get_weight("w_down", act))
    h2 = _mm([(act, wts["down"])], "down_proj", add=h1)
    loss, dh2, dh2_b, d_final = _final_fwd_bwd(h2, final_w, target, name="final_norm_loss")

    dact = _mm([(dh2_b, wts["down"])], "down_dgrad", trans_b=True)
    started = emit_grad("w_down", _wgrad(act, dh2_b, "down_wgrad"))
    dup_a, dup_v, dwf_a, dwf_v, dbf_a, dbf_v = _conv_f_bwd(up_raw, conv_f_w, small["conv_f_b"], dact,
                                                           name="conv_f_bwd")
    dn2 = _mm([((dup_a, 0), (up_w, 0)), ((dup_a, 1), (up_w, 1)), ((dup_v, 0), (up_w, 2)), ((dup_v, 1), (up_w, 3))],
              "up_dgrad", trans_b=True, after=started, out_dtype=BF16)
    started = emit_grad("w_up", jnp.concatenate([_wgrad(n2, dup_a, "up_wgrad_a", tn=up_cols, stack_out=True),
                                                 _wgrad(n2, dup_v, "up_wgrad_v", tn=up_cols, stack_out=True)], axis=0))
    dh1, dh1_b, d_norm2 = _rms_bwd(h1, small["norm2_w"], dn2, dh2, name="rms2_bwd")
    dmix = _mm([(dh1_b, wts["out"])], "out_dgrad", trans_b=True, after=started, out_dtype=BF16)
    started = emit_grad("w_out", _wgrad(mix, dh1_b, "out_wgrad"))
    dp_a, dp_b, dg_a, dg_b, dbg_a, dbg_b = _gate_bwd(gates_raw, small["b_gate"], p_a, p_b, dmix, name="gate_bwd")
    dya = _mm([(dp_a, wts["branch_a"])], "branch_a_dgrad", trans_b=True, after=started)
    dyb = _mm([(dp_b, wts["branch_b"])], "branch_b_dgrad", trans_b=True, out_dtype=BF16)
    started_branch = emit_grad("w_branch", jnp.concatenate([_wgrad(y_a, dp_a, "branch_a_wgrad"),
                                                            _wgrad(y_b, dp_b, "branch_b_wgrad")], axis=0))
    duv, d_wsp, d_bsp_t, d_lnw, d_lnb, d_uvb = _sgu_bwd(uv_raw, dyb, small["uv_b"], small["v_ln_w"],
                                                        small["v_ln_b"], w_sp, b_sp_t, group_sum, name="sgu_bwd")
    dz, dxbc, ddt, d_ssd_nw, d_dskip, d_alog, d_dtb = _ssd_bwd(
        dya, y, z, xbc, dt_raw, states, dt_bias, a_log, a_log_x, d_skip_x, small["ssd_norm_w"], expand, expand_t,
        name="ssd_bwd")
    dxbc_raw, d_conv_a_w, d_conv_a_b = _conv_a_bwd(xbc_raw, conv_a_w, small["conv_a_b"], dxbc, name="conv_a_bwd")
    started = emit_grad("w_in", {
        "in_z": _wgrad(n1, dz, "in_z_wgrad", after=started_branch), "in_xbc": _wgrad(n1, dxbc_raw, "in_xbc_wgrad"),
        "in_dt": _wgrad(n1, ddt, "in_dt_wgrad")[:, :SSD_HEADS], "in_uv": _wgrad(n1, duv, "in_uv_wgrad"),
        "in_gate_a": _wgrad(n1, dg_a, "in_gate_a_wgrad"), "in_gate_b": _wgrad(n1, dg_b, "in_gate_b_wgrad")})
    dn1 = _mm([(dz, wts["in_z"]), (dxbc_raw, wts["in_xbc"]), (ddt, wts["in_dt"]), (duv, wts["in_uv"]),
               (dg_a, wts["in_gate_a"]), (dg_b, wts["in_gate_b"])], "in_dgrad", trans_b=True, after=started,
              out_dtype=BF16)
    dx, _, d_norm1 = _rms_bwd(x, small["norm1_w"], dn1, dh1, name="rms1_bwd")

    grads_small = {
        "norm1_w": d_norm1, "b_gate": jnp.concatenate([dbg_a, dbg_b], axis=1),
        "conv_a_w": d_conv_a_w[:4], "conv_a_b": d_conv_a_b,
        "dt_bias": d_dtb[:, :SSD_HEADS], "a_log": d_alog[:, :SSD_HEADS], "d_skip": d_dskip[:, :SSD_HEADS],
        "ssd_norm_w": d_ssd_nw, "uv_b": d_uvb, "v_ln_w": d_lnw, "v_ln_b": d_lnb,
        "w_spatial": d_wsp[None], "b_spatial": d_bsp_t[:, :SGU_GROUPS].T[None],
        "norm2_w": d_norm2, "conv_f_w": jnp.concatenate([dwf_a[:3], dwf_v[:3]], axis=1),
        "conv_f_b": jnp.concatenate([dbf_a, dbf_v], axis=1), "final_norm_w": d_final.reshape(D_MODEL),
    }
    return loss, dx, grads_small


HBM = pl.BlockSpec(memory_space=pl.ANY)
MESH = pl.DeviceIdType.MESH


def _mesh_pos():
    return lax.axis_index("x"), lax.axis_index("y"), lax.axis_index("c")


def _other_chips(x, y):
    return [(1 - x, y), (x, 1 - y), (1 - x, 1 - y)]


def _remote(src, dst, send_sems, recv_sems, k, dev):
    return pltpu.make_async_remote_copy(src_ref=src, dst_ref=dst, send_sem=send_sems.at[k], recv_sem=recv_sems.at[k],
                                        device_id=dev, device_id_type=MESH)


def _dma_sems(n):
    return [pltpu.SemaphoreType.DMA((n,)), pltpu.SemaphoreType.DMA((n,))]


HBM_ONLY = pl.BlockSpec(memory_space=pltpu.HBM)
SEMAPHORES = pl.BlockSpec(memory_space=pltpu.SEMAPHORE)
DATAFLOW_EFFECT = pltpu.SideEffectType.DATAFLOW_SIDE_EFFECTING
N_PEER_CHIPS = N_CHIPS - 1


def _gather_sends(w_ref, land_ref, send_sems, recv_sems):
    x, y, c = _mesh_pos()
    return [_remote(w_ref.at[c], land_ref.at[2 * x + y, c], send_sems, recv_sems, k, (px, py, c))
            for k, (px, py) in enumerate(_other_chips(x, y))]


def _gather_arrivals(w_ref, land_ref, send_sems, recv_sems):
    x, y, c = _mesh_pos()
    return [_remote(w_ref.at[c], land_ref.at[2 * px + py, c], send_sems, recv_sems, k, (px, py, c))
            for k, (px, py) in enumerate(_other_chips(x, y))]


def _gather_whole_sends(w_ref, land_ref, send_sems, recv_sems):
    x, y, c = _mesh_pos()
    return [_remote(w_ref, land_ref.at[2 * x + y], send_sems, recv_sems, k, (px, py, c))
            for k, (px, py) in enumerate(_other_chips(x, y))]


def _gather_whole_arrivals(w_ref, land_ref, send_sems, recv_sems):
    x, y, c = _mesh_pos()
    return [_remote(w_ref, land_ref.at[2 * px + py], send_sems, recv_sems, k, (px, py, c))
            for k, (px, py) in enumerate(_other_chips(x, y))]


def _scatter_sends(h_ref, land_ref, send_sems, recv_sems):
    x, y, c = _mesh_pos()
    return [_remote(h_ref.at[2 * px + py], land_ref.at[2 * x + y], send_sems, recv_sems, k, (px, py, c))
            for k, (px, py) in enumerate(_other_chips(x, y))]


def _scatter_arrivals(h_ref, land_ref, send_sems, recv_sems):
    x, y, c = _mesh_pos()
    return [_remote(h_ref.at[2 * x + y], land_ref.at[2 * px + py], send_sems, recv_sems, k, (px, py, c))
            for k, (px, py) in enumerate(_other_chips(x, y))]


def _exchange_wait_many(pendings, after, sends, arrivals, *, name):
    n = len(pendings)

    def body(*refs):
        for i in range(n):
            src_ref, land_ref, send_ref, recv_ref = refs[i], refs[n + i], refs[2 * n + i], refs[3 * n + i]
            for cp in sends(src_ref, land_ref, send_ref, recv_ref):
                cp.wait_send()
            for cp in arrivals(src_ref, land_ref, send_ref, recv_ref):
                cp.wait_recv()

    sources = [p[2] for p in pendings]
    landings = [p[3] for p in pendings]
    outs = pl.pallas_call(
        body, name=name,
        out_shape=tuple(pltpu.HBM(a.shape, a.dtype) for a in sources + landings),
        in_specs=[HBM_ONLY] * (2 * n) + [SEMAPHORES] * (2 * n) + [pl.BlockSpec(memory_space=pl.ANY)],
        out_specs=tuple([HBM_ONLY] * (2 * n)), input_output_aliases={i: i for i in range(2 * n)},
        compiler_params=pltpu.CompilerParams(has_side_effects=DATAFLOW_EFFECT),
    )(*sources, *landings, *[p[0] for p in pendings], *[p[1] for p in pendings], after)
    return [(outs[i], outs[n + i]) for i in range(n)]


def _sibling_sends(src_ref, land_ref, send_sems, recv_sems):
    x, y, c = _mesh_pos()
    return [_remote(src_ref, land_ref, send_sems, recv_sems, 0, (x, y, 1 - c))]


def _exchange_start(sources, landing_shapes, sends, *, after=None, name):
    n = len(sources)
    extra = [] if after is None else [after]

    def body(*refs):
        sems = refs[2 * n + len(extra):4 * n + len(extra)]
        for i in range(n):
            send_i = sends[i] if isinstance(sends, (list, tuple)) else sends
            for cp in send_i(refs[i], refs[n + i], sems[2 * i], sems[2 * i + 1]):
                cp.start()
        refs[-1][...] = jnp.zeros_like(refs[-1])

    hbm = [pltpu.HBM(s.shape, s.dtype) for s in sources] + [pltpu.HBM(shp, s.dtype)
                                                             for shp, s in zip(landing_shapes, sources)]
    outs = pl.pallas_call(
        body, name=name,
        out_shape=tuple([pltpu.SemaphoreType.DMA((N_PEER_CHIPS,))] * (2 * n) + hbm
                        + [jax.ShapeDtypeStruct((8, LANES), F32)]),
        in_specs=[HBM_ONLY] * (2 * n) + [pl.BlockSpec(memory_space=pl.ANY)] * len(extra),
        out_specs=tuple([SEMAPHORES] * (2 * n) + [HBM_ONLY] * (2 * n) + [pl.BlockSpec(memory_space=pltpu.VMEM)]),
        input_output_aliases={i: 2 * n + i for i in range(2 * n)},
        compiler_params=pltpu.CompilerParams(has_side_effects=DATAFLOW_EFFECT),
    )(*[pltpu.with_memory_space_constraint(s, pltpu.HBM) for s in sources],
      *[pltpu.with_memory_space_constraint(lax.empty(shp, s.dtype), pltpu.HBM)
        for shp, s in zip(landing_shapes, sources)], *extra)
    pending = [(outs[2 * i], outs[2 * i + 1], outs[2 * n + i], outs[3 * n + i]) for i in range(n)]
    return pending, outs[-1]


def _exchange_wait(pending, after, sends, arrivals, *, name):
    send_sems, recv_sems, source, landing = pending

    def body(src_ref, land_ref, send_ref, recv_ref, after_ref, src_out, land_out):
        for cp in sends(src_ref, land_ref, send_ref, recv_ref):
            cp.wait_send()
        for cp in arrivals(src_ref, land_ref, send_ref, recv_ref):
            cp.wait_recv()

    return pl.pallas_call(
        body, name=name,
        out_shape=(pltpu.HBM(source.shape, source.dtype), pltpu.HBM(landing.shape, landing.dtype)),
        in_specs=[HBM_ONLY, HBM_ONLY, SEMAPHORES, SEMAPHORES, pl.BlockSpec(memory_space=pl.ANY)],
        out_specs=(HBM_ONLY, HBM_ONLY), input_output_aliases={0: 0, 1: 1},
        compiler_params=pltpu.CompilerParams(has_side_effects=DATAFLOW_EFFECT),
    )(source, landing, send_sems, recv_sems, after)


def _gather_ici(shard, *, name):
    _, rh, cols = shard.shape

    def body(w_ref, o_ref, send_sems, recv_sems):
        x, y, c = _mesh_pos()
        mine = 2 * x + y
        sends = []
        for k, (px, py) in enumerate(_other_chips(x, y)):
            cp = _remote(w_ref.at[c], o_ref.at[mine, c], send_sems, recv_sems, k, (px, py, c))
            cp.start()
            sends.append(cp)
        for k, (px, py) in enumerate(_other_chips(x, y)):
            _remote(w_ref.at[c], o_ref.at[2 * px + py, c], send_sems, recv_sems, k, (px, py, c)).wait_recv()
        for cp in sends:
            cp.wait_send()

    return pl.pallas_call(
        body, name=name, in_specs=[HBM], out_specs=HBM,
        out_shape=jax.ShapeDtypeStruct((N_CHIPS, 2, rh, cols), shard.dtype), scratch_shapes=_dma_sems(3),
    )(shard)


def _gather_d2d(parts, *, name):
    def body(a_ref, o_ref, send_sems, recv_sems):
        x, y, c = _mesh_pos()
        sibling = (x, y, 1 - c)
        sends = []
        for k, (px, py) in enumerate(_other_chips(x, y)):
            cp = _remote(a_ref.at[2 * px + py, c], o_ref.at[2 * px + py, c], send_sems, recv_sems, k, sibling)
            cp.start()
            sends.append(cp)
        for k, (px, py) in enumerate(_other_chips(x, y)):
            _remote(a_ref.at[2 * px + py, c], o_ref.at[2 * px + py, 1 - c], send_sems, recv_sems, k, sibling).wait_recv()
        for cp in sends:
            cp.wait_send()

    return pl.pallas_call(
        body, name=name, in_specs=[HBM], out_specs=HBM,
        out_shape=jax.ShapeDtypeStruct(parts.shape, parts.dtype),
        input_output_aliases={0: 0}, scratch_shapes=_dma_sems(3),
    )(parts)


def _all_gather_chips(shard_flat, name):
    rows, cols = shard_flat.shape
    parts = _gather_ici(shard_flat.reshape(2, rows // 2, cols), name=name + "_ici")
    others = _gather_d2d(parts, name=name + "_d2d").reshape(N_CHIPS, rows, cols)
    chip = 2 * lax.axis_index("x") + lax.axis_index("y")
    return lax.dynamic_update_slice(others, shard_flat[None], (chip, 0, 0))


def _row_tile(rows, mult, cap):
    best = mult
    for t in range(mult, min(rows, cap) + 1, mult):
        if rows % t == 0:
            best = t
    assert rows % best == 0, (rows, mult)
    return best


def _swap_halves_d2d(g, *, after=None, name):
    _, _, rh, cols = g.shape
    extra = [] if after is None else [after]

    def body(g_ref, *rest):
        o_ref, send_sems, recv_sems = rest[len(extra):]
        x, y, c = _mesh_pos()
        sibling = (x, y, 1 - c)
        sends = []
        for s in range(N_CHIPS):
            cp = _remote(g_ref.at[s, 1 - c], o_ref.at[s], send_sems, recv_sems, s, sibling)
            cp.start()
            sends.append(cp)
        for s in range(N_CHIPS):
            _remote(g_ref.at[s, c], o_ref.at[s], send_sems, recv_sems, s, sibling).wait_recv()
        for cp in sends:
            cp.wait_send()

    return pl.pallas_call(
        body, name=name, in_specs=[HBM] * (1 + len(extra)), out_specs=HBM,
        out_shape=jax.ShapeDtypeStruct((N_CHIPS, rh, cols), g.dtype), scratch_shapes=_dma_sems(N_CHIPS),
    )(g, *extra)


def _add_own_half(g, arrived, core, *, name):
    _, _, rh, cols = g.shape
    mult = 16 if g.dtype == BF16 else 8
    tr = _row_tile(rh, mult, max(mult, (512 * 1024) // cols))

    def body(core_ref, g_ref, a_ref, o_ref):
        o_ref[...] = (g_ref[0].astype(F32) + a_ref[...].astype(F32)).astype(o_ref.dtype)

    grid_spec = pltpu.PrefetchScalarGridSpec(
        num_scalar_prefetch=1, grid=(N_CHIPS, rh // tr),
        in_specs=[pl.BlockSpec((1, 1, tr, cols), lambda s, i, core_ref: (s, core_ref[0], i, 0)),
                  pl.BlockSpec((1, tr, cols), lambda s, i, core_ref: (s, i, 0))],
        out_specs=pl.BlockSpec((1, tr, cols), lambda s, i, core_ref: (s, i, 0)))
    return pl.pallas_call(
        body, name=name, grid_spec=grid_spec, out_shape=jax.ShapeDtypeStruct((N_CHIPS, rh, cols), g.dtype),
        compiler_params=_params(("parallel", "parallel")),
    )(core, g, arrived)


def _scatter_ici(h, *, name):
    def body(h_ref, o_ref, send_sems, recv_sems):
        x, y, c = _mesh_pos()
        mine = 2 * x + y
        sends = []
        for k, (px, py) in enumerate(_other_chips(x, y)):
            cp = _remote(h_ref.at[2 * px + py], o_ref.at[mine], send_sems, recv_sems, k, (px, py, c))
            cp.start()
            sends.append(cp)
        for k, (px, py) in enumerate(_other_chips(x, y)):
            _remote(h_ref.at[mine], o_ref.at[2 * px + py], send_sems, recv_sems, k, (px, py, c)).wait_recv()
        for cp in sends:
            cp.wait_send()

    others = pl.pallas_call(
        body, name=name, in_specs=[HBM], out_specs=HBM, out_shape=jax.ShapeDtypeStruct(h.shape, h.dtype),
        scratch_shapes=_dma_sems(3),
    )(h)
    chip = 2 * lax.axis_index("x") + lax.axis_index("y")
    own = lax.dynamic_slice_in_dim(h, chip, 1, axis=0)
    return lax.dynamic_update_slice(others, own, (chip, 0, 0))


def _sum_chips(parts, *, name):
    _, rh, cols = parts.shape
    mult = 16 if parts.dtype == BF16 else 8
    tr = _row_tile(rh, mult, max(mult, (512 * 1024) // cols))

    def body(p_ref, o_ref):
        acc = p_ref[0].astype(F32)
        for s in range(1, N_CHIPS):
            acc = acc + p_ref[s].astype(F32)
        o_ref[...] = acc

    return pl.pallas_call(
        body, name=name, grid=(rh // tr,),
        in_specs=[pl.BlockSpec((N_CHIPS, tr, cols), lambda i: (0, i, 0))],
        out_specs=pl.BlockSpec((tr, cols), lambda i: (i, 0)),
        out_shape=jax.ShapeDtypeStruct((rh, cols), F32), compiler_params=_params(("parallel",)),
    )(parts)


def _share_d2d(f, *, name):
    fs = f if isinstance(f, (list, tuple)) else [f]
    others = _swap_with_sibling(fs, name=name)
    first = lax.axis_index("c") == 0
    both = [jnp.stack([jnp.where(first, a, b), jnp.where(first, b, a)]) for a, b in zip(fs, others)]
    return both if isinstance(f, (list, tuple)) else both[0]


def _swap_with_sibling(fs, *, name):
    n = len(fs)

    def body(*refs):
        x, y, c = _mesh_pos()
        sibling = (x, y, 1 - c)
        send_sems, recv_sems = refs[2 * n:]
        copies = [_remote(refs[i], refs[n + i], send_sems, recv_sems, i, sibling) for i in range(n)]
        for cp in copies:
            cp.start()
        for cp in copies:
            cp.wait()

    return pl.pallas_call(
        body, name=name, in_specs=[HBM] * n, out_specs=[HBM] * n,
        out_shape=[jax.ShapeDtypeStruct(a.shape, a.dtype) for a in fs], scratch_shapes=_dma_sems(n),
    )(*fs)


def _reduce_scatter_chips(g, core, name, after=None):
    _, rows, cols = g.shape
    g = g.reshape(N_CHIPS, 2, rows // 2, cols)
    arrived = _swap_halves_d2d(g, after=after, name=name + "_swap")
    chip_sum = _add_own_half(g, arrived, core, name=name + "_add2")
    parts = _scatter_ici(chip_sum, name=name + "_ici")
    total = _sum_chips(parts, name=name + "_sum4")
    return _share_d2d(total, name=name + "_share").reshape(rows, cols)


BIG = ("w_in", "w_branch", "w_out", "w_up", "w_down")
BIG_COLUMN_SHARDED = ("w_in", "w_up")
CONV = ("conv_a_w", "conv_f_w")
REPLICATED = ("norm1_w", "b_gate", "conv_a_b", "dt_bias", "a_log", "d_skip", "ssd_norm_w", "uv_b", "v_ln_w",
              "v_ln_b", "w_spatial", "b_spatial", "norm2_w", "conv_f_b", "final_norm_w")
WEIGHT_ORDER = ("norm1_w", "w_in", "b_gate", "conv_a_w", "conv_a_b", "dt_bias", "a_log", "d_skip", "ssd_norm_w",
                "uv_b", "v_ln_w", "v_ln_b", "w_spatial", "b_spatial", "w_branch", "w_out", "norm2_w", "w_up",
                "conv_f_w", "conv_f_b", "w_down", "final_norm_w")
SMALL_EXCHANGE_ROWS = 64


_GATE0 = SSD_IN + 2 * SGU_WIDTH
IN_SEGMENTS = {
    "in_z": (0, SSD_D_INNER), "in_xbc": (SSD_D_INNER, SSD_D_INNER + SSD_XBC), "in_dt": (SSD_D_INNER + SSD_XBC, SSD_IN),
    "in_uv": (SSD_IN, _GATE0), "in_gate": (_GATE0, IN_COLS), "in_gate_a": (_GATE0, _GATE0 + D_MODEL),
    "in_gate_b": (_GATE0 + D_MODEL, IN_COLS),
}
IN_GRAD_SEGMENTS = ("in_z", "in_xbc", "in_dt", "in_uv", "in_gate_a", "in_gate_b")


def _take_columns(parts, start, stop):
    out = []
    for a, first in parts:
        lo, hi = max(start, first), min(stop, first + a.shape[1])
        if lo < hi:
            out.append(a[:, lo - first:hi - first])
    return out[0] if len(out) == 1 else jnp.concatenate(out, axis=1)


ALIGNED_SEGMENTS = (("in_z", SSD_D_INNER), ("in_xbc", SSD_XBC), ("in_uv", 2 * SGU_WIDTH), ("in_gate_a", D_MODEL),
                    ("in_gate_b", D_MODEL), ("in_dt", LANES))
WINDOW_FIRST_BLOCK = (0, 18, 36, 53)
WINDOW_COLS = 19 * LANES
DT_ALIGNED = SSD_D_INNER + SSD_XBC + 2 * SGU_WIDTH + 2 * D_MODEL


def _aligned_column(g):
    dt0 = SSD_D_INNER + SSD_XBC
    if g < dt0:
        return g
    return DT_ALIGNED + g - dt0 if g < SSD_IN else g - SSD_HEADS


def _grad_windows(segs):
    dt = jnp.pad(segs["in_dt"], ((0, 0), (0, LANES - SSD_HEADS)))
    parts, first = [], 0
    for name, width in ALIGNED_SEGMENTS[:-1]:
        parts.append((segs[name], first))
        first += width
    return jnp.stack([jnp.concatenate([_take_columns(parts, LANES * b, LANES * b + WINDOW_COLS), dt], axis=1)
                      for b in WINDOW_FIRST_BLOCK])


def _window_to_shard(window, chip):
    shard_cols = IN_COLS // N_CHIPS

    def branch(k):
        def take(w):
            pieces, g = [], shard_cols * k
            while g < shard_cols * (k + 1):
                stop = min(b for b in (SSD_D_INNER + SSD_XBC, SSD_IN, shard_cols * (k + 1)) if b > g)
                a = _aligned_column(g)
                in_dt_block = a >= DT_ALIGNED
                lo = a - DT_ALIGNED + WINDOW_COLS if in_dt_block else a - LANES * WINDOW_FIRST_BLOCK[k]
                pieces.append(w[:, lo:lo + stop - g])
                g = stop
            return pieces[0] if len(pieces) == 1 else jnp.concatenate(pieces, axis=1)
        return take

    return lax.switch(chip, [branch(k) for k in range(N_CHIPS)], window)


def _flat_rows(arrays, row_multiple):
    flat = jnp.concatenate([a.reshape(-1) for a in arrays])
    rows = -(-flat.shape[0] // (LANES * row_multiple)) * row_multiple
    return jnp.pad(flat, (0, rows * LANES - flat.shape[0])).reshape(rows, LANES)


def _unflatten(flat, shapes):
    flat = flat.reshape(-1)
    out, off = [], 0
    for shp in shapes:
        n = math.prod(shp)
        out.append(flat[off:off + n].reshape(shp))
        off += n
    return out


def _from_chip_blocks(blocks, name):
    if name in BIG_COLUMN_SHARDED or name in CONV:
        k = blocks.shape[1]
        return jnp.transpose(blocks, (1, 0, 2)).reshape(k, -1)
    return blocks.reshape(-1, blocks.shape[-1])


def _to_chip_blocks(whole, name):
    if name in BIG_COLUMN_SHARDED or name in CONV:
        k, n = whole.shape
        return jnp.transpose(whole.reshape(k, N_CHIPS, n // N_CHIPS), (1, 0, 2))
    return whole.reshape(N_CHIPS, whole.shape[0] // N_CHIPS, whole.shape[1])


def kernel(x, norm1_w, w_in, b_gate, conv_a_w, conv_a_b, dt_bias, a_log, d_skip, ssd_norm_w, uv_b, v_ln_w, v_ln_b, w_spatial, b_spatial, w_branch, w_out, norm2_w, w_up, conv_f_w, conv_f_b, w_down, final_norm_w, loss_target, m_norm1_w, m_w_in, m_b_gate, m_conv_a_w, m_conv_a_b, m_dt_bias, m_a_log, m_d_skip, m_ssd_norm_w, m_uv_b, m_v_ln_w, m_v_ln_b, m_w_spatial, m_b_spatial, m_w_branch, m_w_out, m_norm2_w, m_w_up, m_conv_f_w, m_conv_f_b, m_w_down, m_final_norm_w, v_norm1_w, v_w_in, v_b_gate, v_conv_a_w, v_conv_a_b, v_dt_bias, v_a_log, v_d_skip, v_ssd_norm_w, v_uv_b, v_v_ln_w, v_v_ln_b, v_w_spatial, v_b_spatial, v_w_branch, v_w_out, v_norm2_w, v_w_up, v_conv_f_w, v_conv_f_b, v_w_down, v_final_norm_w):
    weights = dict(norm1_w=norm1_w, w_in=w_in, b_gate=b_gate, conv_a_w=conv_a_w, conv_a_b=conv_a_b, dt_bias=dt_bias,
                   a_log=a_log, d_skip=d_skip, ssd_norm_w=ssd_norm_w, uv_b=uv_b, v_ln_w=v_ln_w, v_ln_b=v_ln_b,
                   w_spatial=w_spatial, b_spatial=b_spatial, w_branch=w_branch, w_out=w_out, norm2_w=norm2_w,
                   w_up=w_up, conv_f_w=conv_f_w, conv_f_b=conv_f_b, w_down=w_down, final_norm_w=final_norm_w)
    mom1 = dict(norm1_w=m_norm1_w, w_in=m_w_in, b_gate=m_b_gate, conv_a_w=m_conv_a_w, conv_a_b=m_conv_a_b,
                dt_bias=m_dt_bias, a_log=m_a_log, d_skip=m_d_skip, ssd_norm_w=m_ssd_norm_w, uv_b=m_uv_b,
                v_ln_w=m_v_ln_w, v_ln_b=m_v_ln_b, w_spatial=m_w_spatial, b_spatial=m_b_spatial, w_branch=m_w_branch,
                w_out=m_w_out, norm2_w=m_norm2_w, w_up=m_w_up, conv_f_w=m_conv_f_w, conv_f_b=m_conv_f_b,
                w_down=m_w_down, final_norm_w=m_final_norm_w)
    mom2 = dict(norm1_w=v_norm1_w, w_in=v_w_in, b_gate=v_b_gate, conv_a_w=v_conv_a_w, conv_a_b=v_conv_a_b,
                dt_bias=v_dt_bias, a_log=v_a_log, d_skip=v_d_skip, ssd_norm_w=v_ssd_norm_w, uv_b=v_uv_b,
                v_ln_w=v_v_ln_w, v_ln_b=v_v_ln_b, w_spatial=v_w_spatial, b_spatial=v_b_spatial, w_branch=v_w_branch,
                w_out=v_w_out, norm2_w=v_norm2_w, w_up=v_w_up, conv_f_w=v_conv_f_w, conv_f_b=v_conv_f_b,
                w_down=v_w_down, final_norm_w=v_final_norm_w)
    chip = 2 * lax.axis_index("x") + lax.axis_index("y")
    core = lax.axis_index("c").astype(jnp.int32).reshape(1)

    whole = {}
    conv_shapes = [weights[n].shape[1:] for n in CONV]
    conv_gathered = _all_gather_chips(_flat_rows([weights[n] for n in CONV], 16), "gather_conv").reshape(N_CHIPS, -1)
    off = 0
    for n, shp in zip(CONV, conv_shapes):
        size = math.prod(shp)
        whole[n] = _from_chip_blocks(conv_gathered[:, off:off + size].reshape((N_CHIPS,) + shp), n)
        off += size
    shard_shapes = {n: weights[n].shape[1:] for n in BIG}
    halves = [weights[n][0].astype(BF16).reshape(2, shard_shapes[n][0] // 2, shard_shapes[n][1]) for n in BIG]
    sends = [_gather_sends if n == "w_in" else _gather_whole_sends for n in BIG]
    gathers, gathers_started = _exchange_start(halves, [(N_CHIPS,) + h.shape for h in halves], sends,
                                               after=conv_gathered, name="gather_start")
    gathers = dict(zip(BIG, gathers))

    def get_weight(name, after):
        rows, cols = shard_shapes[name]
        if name == "w_in":
            own, landed = _exchange_wait(gathers[name], after, _gather_sends, _gather_arrivals,
                                         name="gather_" + name + "_wait")
            landed = _gather_d2d(landed, name="gather_" + name + "_d2d")
        else:
            own, landed = _exchange_wait(gathers[name], after, _gather_whole_sends, _gather_whole_arrivals,
                                         name="gather_" + name + "_wait")
        blocks = lax.dynamic_update_slice(landed.reshape(N_CHIPS, rows, cols), own.reshape(1, rows, cols),
                                          (chip, 0, 0))
        if name == "w_up":
            return {"up": blocks}
        if name == "w_in":
            parts = [(blocks[k], cols * k) for k in range(N_CHIPS)]
            segs = {n: _take_columns(parts, a, b) for n, (a, b) in IN_SEGMENTS.items()}
            segs["in_dt"] = jnp.pad(segs["in_dt"], ((0, 0), (0, LANES - SSD_HEADS)))
            return segs
        full = _from_chip_blocks(blocks, name)
        if name == "w_branch":
            return {"branch_a": full[:SSD_D_INNER], "branch_b": full[SSD_D_INNER:]}
        return {name[2:]: full}

    small = {n: weights[n] for n in REPLICATED}
    small["conv_a_w"] = whole["conv_a_w"]
    small["conv_f_w"] = whole["conv_f_w"]
    small["gathers_started"] = gathers_started

    reductions = {}

    def emit_grad(name, g):
        if name == "w_in":
            g_blocks = _grad_windows(g)
        else:
            g_blocks = g if name == "w_up" else _to_chip_blocks(g, name)
        if name == "w_in":
            _, rows, cols = g_blocks.shape
            g_halves = g_blocks.reshape(N_CHIPS, 2, rows // 2, cols)
            arrived = _swap_halves_d2d(g_halves, name="reduce_" + name + "_swap")
            g_blocks = _add_own_half(g_halves, arrived, core, name="reduce_" + name + "_add2")
        own = lax.dynamic_slice_in_dim(g_blocks, chip, 1, axis=0)
        (pending,), started = _exchange_start([g_blocks], [g_blocks.shape], _scatter_sends,
                                              name="reduce_" + name + "_start")
        reductions[name] = (pending, own)
        return started

    loss, dx, grads_small = _local_step(x[0], loss_target[0], get_weight, small, emit_grad)

    order = ("w_down", "w_up", "w_out", "w_branch", "w_in")
    core_sums = []
    for n in order:
        pending, own = reductions[n]
        _, landed = _exchange_wait(pending, dx, _scatter_sends, _scatter_arrivals, name="reduce_" + n + "_wait")
        parts = lax.dynamic_update_slice(landed, own, (chip, 0, 0))
        core_sums.append(_sum_chips(parts, name="reduce_" + n + "_sum4"))
    swaps, swaps_started = _exchange_start(core_sums, [a.shape for a in core_sums], _sibling_sends, name="reduce_swap_start")
    grads = {}

    small_names = REPLICATED + CONV
    small_shapes = [grads_small[n].shape for n in small_names]
    g_small = _flat_rows([grads_small[n] for n in small_names], N_CHIPS * 2 * SMALL_EXCHANGE_ROWS)
    red_small = _reduce_scatter_chips(g_small.reshape(N_CHIPS, -1, LANES), core, "reduce_small", after=swaps_started)
    all_small = _all_gather_chips(red_small, "gather_small")
    swapped = _exchange_wait_many(swaps, all_small, _sibling_sends, _sibling_sends, name="reduce_swap_wait")
    core_sums = {n: own for n, (own, _) in zip(order, swapped)}
    sibling_sums = {n: other for n, (_, other) in zip(order, swapped)}
    first = lax.axis_index("c") == 0
    w_in_halves = (core_sums["w_in"], sibling_sums["w_in"])
    w_in_grad = jnp.concatenate([jnp.where(first, w_in_halves[0], w_in_halves[1]),
                                 jnp.where(first, w_in_halves[1], w_in_halves[0])], axis=0)
    for n, g in zip(small_names, _unflatten(all_small, small_shapes)):
        if n in CONV:
            width = g.shape[1] // N_CHIPS
            g = lax.dynamic_slice_in_dim(g, chip * width, width, axis=1)
        grads[n] = g.reshape(weights[n].shape[1:]) if n != "final_norm_w" else g

    delta, new_m, new_v = {}, {}, {}
    for n in BIG:
        shp = weights[n].shape
        if n == "w_in":
            g_t = _window_to_shard(w_in_grad, chip).T
            results = [g_t] + list(_adamw(weights[n][0].T, g_t, mom1[n][0].T, mom2[n][0].T, name="adamw_" + n,
                                          tr=_row_tile(g_t.shape[0], 8, 136)))
            results = [a.T for a in results]
        else:
            results = _adamw_two_sums(weights[n][0], core_sums[n], sibling_sums[n], mom1[n][0], mom2[n][0],
                                      name="adamw_" + n, tr=_row_tile(shp[1], 8, 136))
        grads[n], delta[n], new_m[n], new_v[n] = [a.reshape(shp) for a in results]
    small_all = [n for n in WEIGHT_ORDER if n not in BIG]

    def as_2d(a):
        return a.reshape(-1, a.shape[-1])

    results = _adamw_many(*[[as_2d(src[n]) for n in small_all] for src in (weights, grads, mom1, mom2)],
                          name="adamw_small")
    for n, dv, mv, vv in zip(small_all, *results):
        shp = weights[n].shape
        delta[n], new_m[n], new_v[n] = dv.reshape(shp), mv.reshape(shp), vv.reshape(shp)

    total_loss = lax.psum(loss[0, 0], ("x", "y", "c"))
    grad_out = [grads[n].reshape(weights[n].shape) for n in WEIGHT_ORDER]
    return (total_loss, dx[None], *grad_out, *[delta[n] for n in WEIGHT_ORDER], *[new_m[n] for n in WEIGHT_ORDER],
            *[new_v[n] for n in WEIGHT_ORDER])
```

```python
import functools
import math

import jax
import jax.numpy as jnp
from jax import lax
from jax.experimental import pallas as pl
from jax.experimental.pallas import tpu as pltpu

F32 = jnp.float32
BF16 = jnp.bfloat16
HI = lax.Precision.HIGHEST

D_MODEL = 1024
SSD_D_INNER = 2048
SSD_HEADS = 32
SSD_HEAD_DIM = 64
SSD_GROUPS = 4
SSD_HEADS_PER_GROUP = 8
SSD_STATE = 128
SSD_BC = 512
SSD_XBC = 3072
SSD_IN = 5152
SGU_WIDTH = 1024
SGU_GROUPS = 8
CHUNK = 128
IN_COLS = 9248
D_FF = 2816
NORM_EPS = 1e-6
LN_EPS = 1e-5
GROUP_COLS = SSD_HEADS_PER_GROUP * SSD_HEAD_DIM
LANES = 128

ADAM_LR = 0.001
ADAM_B1 = 0.9
ADAM_B2 = 0.999
ADAM_EPS = 1e-08
ADAM_WD = 0.01
ADAM_STEP = 10

N_CHIPS = 4
VMEM_LIMIT = 56 * 1024 * 1024

NT = (((1,), (1,)), ((), ()))
TN = (((0,), (0,)), ((), ()))
NN = (((1,), (0,)), ((), ()))


def _params(dims):
    return pltpu.CompilerParams(dimension_semantics=dims, vmem_limit_bytes=VMEM_LIMIT)


def _dot(a, b, dn=NN, precision=None):
    return lax.dot_general(a, b, dn, precision=precision, preferred_element_type=F32)


def _split3(x):
    hi = x.astype(BF16)
    rest = x - hi.astype(F32)
    mid = rest.astype(BF16)
    return hi, mid, (rest - mid.astype(F32)).astype(BF16)


def _dot_terms(terms, exact, dn=NN):
    out = None
    for t in terms:
        p = _dot(t, exact, dn)
        out = p if out is None else out + p
    return out


def _dot_exact_lhs(exact, terms):
    out = None
    for t in terms:
        p = _dot(exact, t)
        out = p if out is None else out + p
    return out


def _sigmoid(x):
    return 1.0 / (1.0 + jnp.exp(-x))


def _softplus(x):
    return jnp.maximum(x, 0.0) + jnp.log(1.0 + jnp.exp(-jnp.abs(x)))


def _matmul(pairs, *, trans_b=False, add=None, after=None, out_dtype=F32, tm=512, tn=512, name):
    def mat_shape(b):
        if isinstance(b, tuple) and b[1] == "cols":
            return (b[0].shape[1], b[0].shape[0] * b[0].shape[2])
        return b[0].shape[1:] if isinstance(b, tuple) else b.shape

    if isinstance(pairs[0][1], tuple) and pairs[0][1][1] == "cols":
        assert not trans_b and tn % LANES == 0 and pairs[0][1][0].shape[2] % tn == 0, name

    m = (pairs[0][0][0] if isinstance(pairs[0][0], tuple) else pairs[0][0]).shape[0]
    n = mat_shape(pairs[0][1])[0] if trans_b else mat_shape(pairs[0][1])[1]
    tm, tn = min(tm, m), min(tn, n)
    assert m % tm == 0 and n % tn == 0, (name, m, n, tm, tn)
    npairs = len(pairs)
    dn = NT if trans_b else NN

    def body(*refs):
        o_ref = refs[-1]
        acc = None
        for i in range(npairs):
            p = _dot(refs[2 * i][...].astype(BF16), refs[2 * i + 1][...].astype(BF16), dn)
            acc = p if acc is None else acc + p
        if add is not None:
            acc = acc + refs[2 * npairs][...]
        o_ref[...] = acc.astype(out_dtype)

    in_specs, args = [], []
    for a, b in pairs:
        bshape = mat_shape(b)
        k = bshape[1] if trans_b else bshape[0]
        assert bshape == ((n, k) if trans_b else (k, n)), (name, bshape)
        a, qa = a if isinstance(a, tuple) else (a, 0)
        assert a.shape[0] == m and a.shape[1] % k == 0, (name, a.shape, k)
        in_specs.append(pl.BlockSpec((tm, k), lambda i, j, qa=qa: (i, qa)))
        if isinstance(b, tuple) and b[1] == "cols":
            b = b[0]
            per = b.shape[2] // tn
            in_specs.append(pl.BlockSpec((None, k, tn), lambda i, j, per=per: (j // per, 0, j % per)))
        elif isinstance(b, tuple):
            b, qb = b
            if trans_b:
                in_specs.append(pl.BlockSpec((None, tn, k), lambda i, j, qb=qb: (qb, j, 0)))
            else:
                in_specs.append(pl.BlockSpec((None, k, tn), lambda i, j, qb=qb: (qb, 0, j)))
        elif trans_b:
            in_specs.append(pl.BlockSpec((tn, k), lambda i, j: (j, 0)))
        else:
            in_specs.append(pl.BlockSpec((k, tn), lambda i, j: (0, j)))
        args += [a, b]
    if add is not None:
        in_specs.append(pl.BlockSpec((tm, tn), lambda i, j: (i, j)))
        args.append(add)
    if after is not None:
        in_specs.append(pl.BlockSpec(memory_space=pl.ANY))
        args.append(after)
    return pl.pallas_call(
        body, name=name, grid=(m // tm, n // tn), in_specs=in_specs,
        out_specs=pl.BlockSpec((tm, tn), lambda i, j: (i, j)),
        out_shape=jax.ShapeDtypeStruct((m, n), out_dtype),
        compiler_params=_params(("parallel", "parallel")),
    )(*args)


def _matmul_tn(a, b, *, tk, tn, tm=1024, out_dtype=BF16, stack_out=False, after=None, name):
    m, k = a.shape
    n = b.shape[1]
    tm, tk, tn = min(tm, m), min(tk, k), min(tn, n)
    assert m % tm == 0 and k % tk == 0 and n % tn == 0, (name, m, k, n)
    nm = m // tm
    if stack_out:
        out_spec = pl.BlockSpec((None, tk, tn), lambda i, j, l: (j, i, 0))
        out_shape = jax.ShapeDtypeStruct((n // tn, k, tn), out_dtype)
    else:
        out_spec = pl.BlockSpec((tk, tn), lambda i, j, l: (i, j))
        out_shape = jax.ShapeDtypeStruct((k, n), out_dtype)

    def body(a_ref, b_ref, *rest):
        o_ref, acc = rest[-2:]
        mi = pl.program_id(2)

        @pl.when(mi == 0)
        def _():
            acc[...] = jnp.zeros_like(acc)

        acc[...] += _dot(a_ref[...].astype(BF16), b_ref[...].astype(BF16), TN)

        @pl.when(mi == nm - 1)
        def _():
            o_ref[...] = acc[...].astype(out_dtype)

    in_specs = [pl.BlockSpec((tm, tk), lambda i, j, l: (l, i)), pl.BlockSpec((tm, tn), lambda i, j, l: (l, j))]
    args = [a, b]
    if after is not None:
        in_specs.append(pl.BlockSpec(memory_space=pl.ANY))
        args.append(after)
    return pl.pallas_call(
        body, name=name, grid=(k // tk, n // tn, nm), in_specs=in_specs,
        out_specs=out_spec, out_shape=out_shape,
        scratch_shapes=[pltpu.VMEM((tk, tn), F32)],
        compiler_params=_params(("parallel", "parallel", "arbitrary")),
    )(*args)


def _rms_fwd(x, w, *, after=None, name, tm=512):
    s, d = x.shape
    tm = min(tm, s)
    extra = [] if after is None else [after]

    def body(x_ref, w_ref, *rest):
        o_ref = rest[-1]
        xv = x_ref[...]
        r = lax.rsqrt(jnp.mean(xv * xv, axis=-1, keepdims=True) + NORM_EPS)
        o_ref[...] = (xv * r * w_ref[...]).astype(BF16)

    return pl.pallas_call(
        body, name=name, grid=(s // tm,),
        in_specs=[pl.BlockSpec((tm, d), lambda i: (i, 0)), pl.BlockSpec((1, d), lambda i: (0, 0))]
        + [pl.BlockSpec(memory_space=pl.ANY)] * len(extra),
        out_specs=pl.BlockSpec((tm, d), lambda i: (i, 0)),
        out_shape=jax.ShapeDtypeStruct((s, d), BF16),
        compiler_params=_params(("parallel",)),
    )(x, w, *extra)


def _rms_bwd(x, w, dn, dres, *, name, tm=512):
    s, d = x.shape
    tm = min(tm, s)

    def body(x_ref, w_ref, dn_ref, dres_ref, dx_ref, dxb_ref, dw_ref):
        @pl.when(pl.program_id(0) == 0)
        def _():
            dw_ref[...] = jnp.zeros_like(dw_ref)

        xv = x_ref[...]
        r = lax.rsqrt(jnp.mean(xv * xv, axis=-1, keepdims=True) + NORM_EPS)
        xhat = xv * r
        dnv = dn_ref[...].astype(F32)
        dxhat = dnv * w_ref[...]
        dx = dres_ref[...] + r * (dxhat - xhat * jnp.mean(dxhat * xhat, axis=-1, keepdims=True))
        dx_ref[...] = dx
        dxb_ref[...] = dx.astype(BF16)
        dw_ref[...] += jnp.sum(dnv * xhat, axis=0, keepdims=True)

    tile = pl.BlockSpec((tm, d), lambda i: (i, 0))
    row = pl.BlockSpec((1, d), lambda i: (0, 0))
    return pl.pallas_call(
        body, name=name, grid=(s // tm,),
        in_specs=[tile, row, tile, tile], out_specs=[tile, tile, row],
        out_shape=[jax.ShapeDtypeStruct((s, d), F32), jax.ShapeDtypeStruct((s, d), BF16),
                   jax.ShapeDtypeStruct((1, d), F32)],
        compiler_params=_params(("arbitrary",)),
    )(x, w, dn, dres)


def _final_fwd_bwd(h2, wf, target, *, name, tm=512):
    s, d = h2.shape
    tm = min(tm, s)

    def body(h_ref, w_ref, t_ref, loss_ref, dh_ref, dhb_ref, dw_ref):
        @pl.when(pl.program_id(0) == 0)
        def _():
            dw_ref[...] = jnp.zeros_like(dw_ref)
            loss_ref[...] = jnp.zeros_like(loss_ref)

        hv = h_ref[...]
        r = lax.rsqrt(jnp.mean(hv * hv, axis=-1, keepdims=True) + NORM_EPS)
        xhat = hv * r
        err = xhat * w_ref[...] - t_ref[...]
        per_tok = jnp.mean(err * err, axis=-1, keepdims=True)
        loss_ref[...] += 0.5 * jnp.sum(per_tok, axis=0, keepdims=True)
        dy = err * (1.0 / d)
        dxhat = dy * w_ref[...]
        dh = r * (dxhat - xhat * jnp.mean(dxhat * xhat, axis=-1, keepdims=True))
        dh_ref[...] = dh
        dhb_ref[...] = dh.astype(BF16)
        dw_ref[...] += jnp.sum(dy * xhat, axis=0, keepdims=True)

    tile = pl.BlockSpec((tm, d), lambda i: (i, 0))
    row = pl.BlockSpec((1, d), lambda i: (0, 0))
    return pl.pallas_call(
        body, name=name, grid=(s // tm,),
        in_specs=[tile, row, tile],
        out_specs=[pl.BlockSpec((1, 1), lambda i: (0, 0)), tile, tile, row],
        out_shape=[jax.ShapeDtypeStruct((1, 1), F32), jax.ShapeDtypeStruct((s, d), F32),
                   jax.ShapeDtypeStruct((s, d), BF16), jax.ShapeDtypeStruct((1, d), F32)],
        compiler_params=_params(("arbitrary",)),
    )(h2, wf, target)


CONV_ROWS = 512
HALO = 8


def _rows_with_halo(ref, r0, rows, s, before, after):
    tile = 16 if ref.dtype == BF16 else HALO
    parts = []
    if before:
        prev = ref[pl.ds(pl.multiple_of(jnp.maximum(r0 - tile, 0), tile), tile), :].astype(F32)[tile - HALO:]
        parts.append(jnp.where(r0 > 0, prev, 0.0))
    parts.append(ref[pl.ds(r0, rows), :].astype(F32))
    if after:
        nxt = ref[pl.ds(pl.multiple_of(jnp.minimum(r0 + rows, s - tile), tile), tile), :].astype(F32)[:HALO]
        parts.append(jnp.where(r0 + rows < s, nxt, 0.0))
    return jnp.concatenate(parts, axis=0) if len(parts) > 1 else parts[0]


def _window(x_ref, r0, s, after):
    return _rows_with_halo(x_ref, r0, CONV_ROWS, s, True, after).astype(F32)


def _shifted(window, k, rows):
    if k == 0:
        return window[HALO:HALO + rows]
    return pltpu.roll(window, k, 0)[HALO:HALO + rows]


def _conv_taps(window, w_ref, kk, rows):
    acc = None
    for i in range(kk):
        term = w_ref[i:i + 1, :] * _shifted(window, kk - 1 - i, rows)
        acc = term if acc is None else acc + term
    return acc


def _row_loop(s, step):
    def body(r, carry):
        return step(pl.multiple_of(r * CONV_ROWS, CONV_ROWS), carry)
    return body


def _conv_bwd_rows(window, dpe, w_ref, kk):
    dp = dpe[:CONV_ROWS]
    dx = None
    dws = []
    for i in range(kk):
        k = kk - 1 - i
        dws.append(jnp.sum(dp * _shifted(window, k, CONV_ROWS), axis=0, keepdims=True))
        later = dp if k == 0 else pltpu.roll(dpe, dpe.shape[0] - k, 0)[:CONV_ROWS]
        term = w_ref[i:i + 1, :] * later
        dx = term if dx is None else dx + term
    return dx, dws, jnp.sum(dp, axis=0, keepdims=True)


def _conv_a_fwd(xraw, w, b, *, name, tc=128):
    s, c = xraw.shape
    kk = 4

    def body(x_ref, w_ref, b_ref, o_ref):
        def step(r0, carry):
            pre = _conv_taps(_window(x_ref, r0, s, False), w_ref, kk, CONV_ROWS) + b_ref[...]
            o_ref[pl.ds(r0, CONV_ROWS), :] = pre * _sigmoid(pre)
            return carry

        lax.fori_loop(0, s // CONV_ROWS, _row_loop(s, step), 0)

    col = pl.BlockSpec((s, tc), lambda j: (0, j))
    return pl.pallas_call(
        body, name=name, grid=(c // tc,),
        in_specs=[col, pl.BlockSpec((8, tc), lambda j: (0, j)), pl.BlockSpec((1, tc), lambda j: (0, j))],
        out_specs=col, out_shape=jax.ShapeDtypeStruct((s, c), F32),
        compiler_params=_params(("parallel",)),
    )(xraw, w, b)


def _conv_a_bwd(xraw, w, b, dy, *, name, tc=128):
    s, c = xraw.shape
    kk = 4

    def body(x_ref, w_ref, b_ref, dy_ref, dx_ref, dw_ref, db_ref):
        def step(r0, carry):
            window = _window(x_ref, r0, s, True)
            pre = _conv_taps(window, w_ref, kk, CONV_ROWS + HALO) + b_ref[...]
            sg = _sigmoid(pre)
            dpe = _rows_with_halo(dy_ref, r0, CONV_ROWS, s, False, True) * (sg * (1.0 + pre * (1.0 - sg)))
            dx, dws, db = _conv_bwd_rows(window, dpe, w_ref, kk)
            dx_ref[pl.ds(r0, CONV_ROWS), :] = dx.astype(BF16)
            return tuple(acc + new for acc, new in zip(carry, dws + [db]))

        zero = jnp.zeros((1, tc), F32)
        sums = lax.fori_loop(0, s // CONV_ROWS, _row_loop(s, step), (zero,) * (kk + 1))
        db_ref[...] = sums[kk]
        dw_ref[...] = jnp.concatenate(list(sums[:kk]) + [jnp.zeros((8 - kk, tc), F32)], axis=0)

    col = pl.BlockSpec((s, tc), lambda j: (0, j))
    w8 = pl.BlockSpec((8, tc), lambda j: (0, j))
    row = pl.BlockSpec((1, tc), lambda j: (0, j))
    return pl.pallas_call(
        body, name=name, grid=(c // tc,),
        in_specs=[col, w8, row, col], out_specs=[col, w8, row],
        out_shape=[jax.ShapeDtypeStruct((s, c), BF16), jax.ShapeDtypeStruct((8, c), F32),
                   jax.ShapeDtypeStruct((1, c), F32)],
        compiler_params=_params(("parallel",)),
    )(xraw, w, b, dy)


def _conv_f_fwd(up_raw, w, b, *, name, tc=128):
    s, c2 = up_raw.shape
    c = c2 // 2
    nb = c // tc
    kk = 3

    def body(xa_ref, xv_ref, wa_ref, wv_ref, ba_ref, bv_ref, o_ref):
        def step(r0, carry):
            a = _conv_taps(_window(xa_ref, r0, s, False), wa_ref, kk, CONV_ROWS) + ba_ref[...]
            v = _conv_taps(_window(xv_ref, r0, s, False), wv_ref, kk, CONV_ROWS) + bv_ref[...]
            o_ref[pl.ds(r0, CONV_ROWS), :] = (a * _sigmoid(a) * v).astype(BF16)
            return carry

        lax.fori_loop(0, s // CONV_ROWS, _row_loop(s, step), 0)

    col_a = pl.BlockSpec((s, tc), lambda j: (0, j))
    col_v = pl.BlockSpec((s, tc), lambda j: (0, j + nb))
    return pl.pallas_call(
        body, name=name, grid=(nb,),
        in_specs=[col_a, col_v, pl.BlockSpec((8, tc), lambda j: (0, j)), pl.BlockSpec((8, tc), lambda j: (0, j + nb)),
                  pl.BlockSpec((1, tc), lambda j: (0, j)), pl.BlockSpec((1, tc), lambda j: (0, j + nb))],
        out_specs=col_a, out_shape=jax.ShapeDtypeStruct((s, c), BF16),
        compiler_params=_params(("parallel",)),
    )(up_raw, up_raw, w, w, b, b)


def _conv_f_bwd(up_raw, w, b, dact, *, name, tc=128):
    s, c2 = up_raw.shape
    c = c2 // 2
    nb = c // tc
    kk = 3

    def body(xa_ref, xv_ref, wa_ref, wv_ref, ba_ref, bv_ref, d_ref,
             dxa_ref, dxv_ref, dwa_ref, dwv_ref, dba_ref, dbv_ref):
        def step(r0, carry):
            win_a = _window(xa_ref, r0, s, True)
            win_v = _window(xv_ref, r0, s, True)
            a = _conv_taps(win_a, wa_ref, kk, CONV_ROWS + HALO) + ba_ref[...]
            v = _conv_taps(win_v, wv_ref, kk, CONV_ROWS + HALO) + bv_ref[...]
            sg = _sigmoid(a)
            d = _rows_with_halo(d_ref, r0, CONV_ROWS, s, False, True)
            dxa, dwas, dba = _conv_bwd_rows(win_a, d * v * (sg * (1.0 + a * (1.0 - sg))), wa_ref, kk)
            dxv, dwvs, dbv = _conv_bwd_rows(win_v, d * (a * sg), wv_ref, kk)
            dxa_ref[pl.ds(r0, CONV_ROWS), :] = dxa.astype(BF16)
            dxv_ref[pl.ds(r0, CONV_ROWS), :] = dxv.astype(BF16)
            return tuple(acc + new for acc, new in zip(carry, dwas + [dba] + dwvs + [dbv]))

        zero = jnp.zeros((1, tc), F32)
        sums = lax.fori_loop(0, s // CONV_ROWS, _row_loop(s, step), (zero,) * (2 * kk + 2))
        pad = [jnp.zeros((8 - kk, tc), F32)]
        dwa_ref[...] = jnp.concatenate(list(sums[:kk]) + pad, axis=0)
        dba_ref[...] = sums[kk]
        dwv_ref[...] = jnp.concatenate(list(sums[kk + 1:2 * kk + 1]) + pad, axis=0)
        dbv_ref[...] = sums[2 * kk + 1]

    col_a = pl.BlockSpec((s, tc), lambda j: (0, j))
    col_v = pl.BlockSpec((s, tc), lambda j: (0, j + nb))
    w_a = pl.BlockSpec((8, tc), lambda j: (0, j))
    w_v = pl.BlockSpec((8, tc), lambda j: (0, j + nb))
    r_a = pl.BlockSpec((1, tc), lambda j: (0, j))
    r_v = pl.BlockSpec((1, tc), lambda j: (0, j + nb))
    outs = pl.pallas_call(
        body, name=name, grid=(nb,),
        in_specs=[col_a, col_v, w_a, w_v, r_a, r_v, col_a],
        out_specs=[col_a, col_a, w_a, w_a, r_a, r_a],
        out_shape=[jax.ShapeDtypeStruct((s, c), BF16), jax.ShapeDtypeStruct((s, c), BF16),
                   jax.ShapeDtypeStruct((8, c), F32), jax.ShapeDtypeStruct((8, c), F32),
                   jax.ShapeDtypeStruct((1, c), F32), jax.ShapeDtypeStruct((1, c), F32)],
        compiler_params=_params(("parallel",)),
    )(up_raw, up_raw, w, w, b, b, dact)
    return outs


def _tri_masks():
    row = lax.broadcasted_iota(jnp.int32, (CHUNK, CHUNK), 0)
    col = lax.broadcasted_iota(jnp.int32, (CHUNK, CHUNK), 1)
    return row >= col, row <= col


def _ssd_fwd(xbc, dt_raw, z, dt_bias, a_log, a_log_x, d_skip_x, norm_w, expand, *, name):
    s = xbc.shape[0]
    nc = s // CHUNK

    def body(xbc_ref, dtr_ref, z_ref, dtb_ref, alog_ref, alogx_ref, dskx_ref, nw_ref, e_ref,
             y_ref, ya_ref, st_ref, state):
        @pl.when(pl.program_id(0) == 0)
        def _():
            state[...] = jnp.zeros_like(state)

        st_ref[0] = state[...]
        lower, _ = _tri_masks()
        dt = _softplus(dtr_ref[...] + dtb_ref[...])
        adt = dt * (-jnp.exp(alog_ref[...]))
        acum = _dot_exact_lhs(lower.astype(BF16), _split3(adt))
        acum_t = acum.T
        dt_terms, acum_terms = _split3(dt), _split3(acum)
        for g in range(SSD_GROUPS):
            sl = slice(GROUP_COLS * g, GROUP_COLS * (g + 1))
            dt_x = _dot_terms(dt_terms, e_ref[:, sl])
            acum_x = _dot_terms(acum_terms, e_ref[:, sl])
            tot_x = jnp.sum(dt_x * (-jnp.exp(alogx_ref[:, sl])), axis=0, keepdims=True)
            xs = xbc_ref[:, sl]
            xdt = xs * dt_x
            xdt_b = xdt.astype(BF16)
            bg = xbc_ref[:, SSD_D_INNER + SSD_STATE * g:SSD_D_INNER + SSD_STATE * (g + 1)].astype(BF16)
            cg = xbc_ref[:, SSD_D_INNER + SSD_BC + SSD_STATE * g:SSD_D_INNER + SSD_BC + SSD_STATE * (g + 1)].astype(BF16)
            cb = _dot(cg, bg, NT)
            st_g = state[:, sl]
            y_off = _dot(cg, st_g.astype(BF16)) * jnp.exp(acum_x)
            parts = []
            for r in range(SSD_HEADS_PER_GROUP):
                h = SSD_HEADS_PER_GROUP * g + r
                dec = jnp.exp(jnp.where(lower, acum[:, h:h + 1] - acum_t[h:h + 1, :], -jnp.inf))
                parts.append(_dot((cb * dec).astype(BF16), xdt_b[:, SSD_HEAD_DIM * r:SSD_HEAD_DIM * (r + 1)]))
            y_ref[:, sl] = jnp.concatenate(parts, axis=1) + y_off + dskx_ref[:, sl] * xs
            wgt = (xdt * jnp.exp(tot_x - acum_x)).astype(BF16)
            state[:, sl] = st_g * jnp.exp(tot_x) + _dot(bg, wgt, TN)
        zv = z_ref[...].astype(F32)
        q = y_ref[...] * (zv * _sigmoid(zv))
        r = lax.rsqrt(jnp.mean(q * q, axis=-1, keepdims=True) + NORM_EPS)
        ya_ref[...] = (q * r * nw_ref[...]).astype(BF16)

    def chunk(w):
        return pl.BlockSpec((CHUNK, w), lambda c: (c, 0))

    def const(shape):
        return pl.BlockSpec(shape, lambda c: (0,) * len(shape))

    return pl.pallas_call(
        body, name=name, grid=(nc,),
        in_specs=[chunk(SSD_XBC), chunk(LANES), chunk(SSD_D_INNER), const((1, LANES)), const((1, LANES)),
                  const((1, SSD_D_INNER)), const((1, SSD_D_INNER)), const((1, SSD_D_INNER)),
                  const((LANES, SSD_D_INNER))],
        out_specs=[chunk(SSD_D_INNER), chunk(SSD_D_INNER),
                   pl.BlockSpec((1, SSD_STATE, SSD_D_INNER), lambda c: (c, 0, 0))],
        out_shape=[jax.ShapeDtypeStruct((s, SSD_D_INNER), F32), jax.ShapeDtypeStruct((s, SSD_D_INNER), BF16),
                   jax.ShapeDtypeStruct((nc, SSD_STATE, SSD_D_INNER), F32)],
        scratch_shapes=[pltpu.VMEM((SSD_STATE, SSD_D_INNER), F32)],
        compiler_params=_params(("arbitrary",)),
    )(xbc, dt_raw, z, dt_bias, a_log, a_log_x, d_skip_x, norm_w, expand)


def _ssd_bwd(dya, y, z, xbc, dt_raw, states, dt_bias, a_log, a_log_x, d_skip_x, norm_w, expand, expand_t, *, name):
    s = xbc.shape[0]
    nc = s // CHUNK

    def body(dya_ref, y_ref, z_ref, xbc_ref, dtr_ref, stp_ref, dtb_ref, alog_ref, alogx_ref, dskx_ref, nw_ref,
             e_ref, et_ref, dz_ref, dxbc_ref, ddt_ref, dnw_ref, ddsk_ref, dalog_ref, ddtb_ref,
             dstate, dy_sc, dskcol):
        i = pl.program_id(0)

        @pl.when(i == 0)
        def _():
            dstate[...] = jnp.zeros_like(dstate)
            dskcol[...] = jnp.zeros_like(dskcol)
            dnw_ref[...] = jnp.zeros_like(dnw_ref)
            dalog_ref[...] = jnp.zeros_like(dalog_ref)
            ddtb_ref[...] = jnp.zeros_like(ddtb_ref)
            ddsk_ref[...] = jnp.zeros_like(ddsk_ref)

        lower, upper = _tri_masks()
        rows = lax.broadcasted_iota(jnp.int32, (CHUNK, LANES), 0)
        pre = dtr_ref[...] + dtb_ref[...]
        dt = _softplus(pre)
        a = -jnp.exp(alog_ref[...])
        acum = _dot_exact_lhs(lower.astype(BF16), _split3(dt * a))
        acum_t = acum.T
        dt_terms, acum_terms = _split3(dt), _split3(acum)

        yv = y_ref[...]
        zv = z_ref[...].astype(F32)
        sz = _sigmoid(zv)
        silu_z = zv * sz
        q = yv * silu_z
        r = lax.rsqrt(jnp.mean(q * q, axis=-1, keepdims=True) + NORM_EPS)
        qhat = q * r
        dyav = dya_ref[...]
        dqhat = dyav * nw_ref[...]
        dnw_ref[...] += jnp.sum(dyav * qhat, axis=0, keepdims=True)
        dq = r * (dqhat - qhat * jnp.mean(dqhat * qhat, axis=-1, keepdims=True))
        dy_sc[...] = dq * silu_z
        dz_ref[...] = (dq * yv * (sz * (1.0 + zv * (1.0 - sz)))).astype(BF16)

        da_cum = jnp.zeros((CHUNK, LANES), F32)
        ddt = jnp.zeros((CHUNK, LANES), F32)
        for g in range(SSD_GROUPS):
            sl = slice(GROUP_COLS * g, GROUP_COLS * (g + 1))
            et_g = et_ref[sl, :]
            dt_x = _dot_terms(dt_terms, e_ref[:, sl])
            acum_x = _dot_terms(acum_terms, e_ref[:, sl])
            tot_x = jnp.sum(dt_x * (-jnp.exp(alogx_ref[:, sl])), axis=0, keepdims=True)
            e_tot = jnp.exp(tot_x)
            dec_s = jnp.exp(tot_x - acum_x)
            xs = xbc_ref[:, sl]
            xdt = xs * dt_x
            xdt_b = xdt.astype(BF16)
            dy = dy_sc[:, sl]
            dy_b = dy.astype(BF16)
            dskx = dskx_ref[:, sl]
            y_ssd = y_ref[:, sl] - dskx * xs
            dskcol[:, sl] += jnp.sum(dy * xs, axis=0, keepdims=True)
            bg = xbc_ref[:, SSD_D_INNER + SSD_STATE * g:SSD_D_INNER + SSD_STATE * (g + 1)].astype(BF16)
            cg = xbc_ref[:, SSD_D_INNER + SSD_BC + SSD_STATE * g:SSD_D_INNER + SSD_BC + SSD_STATE * (g + 1)].astype(BF16)
            cb_t = _dot(bg, cg, NT)
            sp = stp_ref[0, :, sl]
            ds_g = dstate[:, sl]
            ds_b = ds_g.astype(BF16)
            dye_b = (dy * jnp.exp(acum_x)).astype(BF16)
            dc = _dot(dye_b, sp.astype(BF16), NT)
            dxdt_state = dec_s * _dot(bg, ds_b)
            db = _dot((xdt * dec_s).astype(BF16), ds_b, NT)
            dcb_t = jnp.zeros((CHUNK, CHUNK), F32)
            parts = []
            for rr in range(SSD_HEADS_PER_GROUP):
                h = SSD_HEADS_PER_GROUP * g + rr
                hs = slice(SSD_HEAD_DIM * rr, SSD_HEAD_DIM * (rr + 1))
                dec_t = jnp.exp(jnp.where(upper, acum_t[h:h + 1, :] - acum[:, h:h + 1], -jnp.inf))
                parts.append(_dot((cb_t * dec_t).astype(BF16), dy_b[:, hs]))
                dcb_t = dcb_t + _dot(xdt_b[:, hs], dy_b[:, hs], NT) * dec_t
            dxdt = jnp.concatenate(parts, axis=1) + dxdt_state
            dcb_tb = dcb_t.astype(BF16)
            dc = dc + _dot(dcb_tb, bg, TN)
            db = db + _dot(dcb_tb, cg)
            tot_col = jnp.sum(ds_g * sp, axis=0, keepdims=True) * e_tot + jnp.sum(dxdt_state * xdt, axis=0, keepdims=True)
            d_tot = _dot_terms(_split3(jnp.broadcast_to(tot_col, (8, GROUP_COLS))), et_g)
            d_tot = jnp.max(d_tot, axis=0, keepdims=True)
            pair_sums = dy_b.astype(F32) * y_ssd - xdt_b.astype(F32) * dxdt
            da_cum = da_cum + _dot_terms(_split3(pair_sums), et_g) + jnp.where(rows == CHUNK - 1, d_tot, 0.0)
            ddt = ddt + _dot_terms(_split3(dxdt * xs), et_g)
            dxbc_ref[:, sl] = dy * dskx + dxdt * dt_x
            dxbc_ref[:, SSD_D_INNER + SSD_STATE * g:SSD_D_INNER + SSD_STATE * (g + 1)] = db
            dxbc_ref[:, SSD_D_INNER + SSD_BC + SSD_STATE * g:SSD_D_INNER + SSD_BC + SSD_STATE * (g + 1)] = dc
            dstate[:, sl] = e_tot * ds_g + _dot(cg, dye_b, TN)

        dadt = _dot_exact_lhs(upper.astype(BF16), _split3(da_cum))
        ddt = ddt + dadt * a
        dalog_ref[...] += jnp.sum(dadt * dt, axis=0, keepdims=True)
        dpre = ddt * _sigmoid(pre)
        ddtb_ref[...] += jnp.sum(dpre, axis=0, keepdims=True)
        ddt_ref[...] = dpre.astype(BF16)

        @pl.when(i == nc - 1)
        def _():
            dalog_ref[...] = dalog_ref[...] * a
            dsk = _dot_terms(_split3(jnp.broadcast_to(dskcol[...], (8, SSD_D_INNER))), et_ref[...])
            ddsk_ref[...] = jnp.max(dsk, axis=0, keepdims=True)

    def chunk(w):
        return pl.BlockSpec((CHUNK, w), lambda i: (nc - 1 - i, 0))

    def const(shape):
        return pl.BlockSpec(shape, lambda i: (0,) * len(shape))

    return pl.pallas_call(
        body, name=name, grid=(nc,),
        in_specs=[chunk(SSD_D_INNER), chunk(SSD_D_INNER), chunk(SSD_D_INNER), chunk(SSD_XBC), chunk(LANES),
                  pl.BlockSpec((1, SSD_STATE, SSD_D_INNER), lambda i: (nc - 1 - i, 0, 0)),
                  const((1, LANES)), const((1, LANES)), const((1, SSD_D_INNER)), const((1, SSD_D_INNER)),
                  const((1, SSD_D_INNER)), const((LANES, SSD_D_INNER)), const((SSD_D_INNER, LANES))],
        out_specs=[chunk(SSD_D_INNER), chunk(SSD_XBC), chunk(LANES), const((1, SSD_D_INNER)), const((1, LANES)),
                   const((1, LANES)), const((1, LANES))],
        out_shape=[jax.ShapeDtypeStruct((s, SSD_D_INNER), BF16), jax.ShapeDtypeStruct((s, SSD_XBC), F32),
                   jax.ShapeDtypeStruct((s, LANES), BF16), jax.ShapeDtypeStruct((1, SSD_D_INNER), F32),
                   jax.ShapeDtypeStruct((1, LANES), F32), jax.ShapeDtypeStruct((1, LANES), F32),
                   jax.ShapeDtypeStruct((1, LANES), F32)],
        scratch_shapes=[pltpu.VMEM((SSD_STATE, SSD_D_INNER), F32), pltpu.VMEM((CHUNK, SSD_D_INNER), F32),
                        pltpu.VMEM((1, SSD_D_INNER), F32)],
        compiler_params=_params(("arbitrary",)),
    )(dya, y, z, xbc, dt_raw, states, dt_bias, a_log, a_log_x, d_skip_x, norm_w, expand, expand_t)


GELU_K = math.sqrt(2.0 / math.pi)
GELU_C = 0.044715


def _gelu(x):
    return 0.5 * x * (1.0 + jnp.tanh(GELU_K * (x + GELU_C * x * x * x)))


def _gelu_grad(x):
    t = jnp.tanh(GELU_K * (x + GELU_C * x * x * x))
    return 0.5 * (1.0 + t) + 0.5 * x * (1.0 - t * t) * (GELU_K * (1.0 + 3.0 * GELU_C * x * x))


def _sgu_pre(uv_ref, uvb_ref, lnw_ref, lnb_ref):
    uv = uv_ref[...].astype(F32) + uvb_ref[...]
    guv = _gelu(uv)
    u = guv[:, :SGU_WIDTH]
    v = guv[:, SGU_WIDTH:]
    mu = jnp.mean(v, axis=-1, keepdims=True)
    vc = v - mu
    rstd = lax.rsqrt(jnp.mean(vc * vc, axis=-1, keepdims=True) + LN_EPS)
    vhat = vc * rstd
    vn = vhat * lnw_ref[...] + lnb_ref[...]
    return uv, u, vhat, rstd, vn


def _sgu_fwd(uv_raw, uv_b, ln_w, ln_b, w_sp, b_sp_t, *, name):
    s = uv_raw.shape[0]
    nc = s // CHUNK

    def body(uv_ref, uvb_ref, lnw_ref, lnb_ref, w_ref, bt_ref, o_ref):
        lower, _ = _tri_masks()
        _, u, _, _, vn = _sgu_pre(uv_ref, uvb_ref, lnw_ref, lnb_ref)
        vn_b = vn.astype(BF16)
        bt = bt_ref[...]
        for g in range(SGU_GROUPS):
            gs = slice(LANES * g, LANES * (g + 1))
            wc = jnp.where(lower, w_ref[g], 0.0).astype(BF16)
            mixed = _dot(wc, vn_b[:, gs]) + bt[:, g:g + 1]
            o_ref[:, gs] = (u[:, gs] * mixed).astype(BF16)

    def const(shape):
        return pl.BlockSpec(shape, lambda c: (0,) * len(shape))

    return pl.pallas_call(
        body, name=name, grid=(nc,),
        in_specs=[pl.BlockSpec((CHUNK, 2 * SGU_WIDTH), lambda c: (c, 0)), const((1, 2 * SGU_WIDTH)),
                  const((1, SGU_WIDTH)), const((1, SGU_WIDTH)), const((SGU_GROUPS, CHUNK, CHUNK)),
                  const((CHUNK, LANES))],
        out_specs=pl.BlockSpec((CHUNK, SGU_WIDTH), lambda c: (c, 0)),
        out_shape=jax.ShapeDtypeStruct((s, SGU_WIDTH), BF16),
        compiler_params=_params(("parallel",)),
    )(uv_raw, uv_b, ln_w, ln_b, w_sp, b_sp_t)


def _sgu_bwd(uv_raw, dyb, uv_b, ln_w, ln_b, w_sp, b_sp_t, group_sum, *, name):
    s = uv_raw.shape[0]
    nc = s // CHUNK

    def body(uv_ref, dy_ref, uvb_ref, lnw_ref, lnb_ref, w_ref, bt_ref, gsum_ref,
             duv_ref, dw_ref, dbt_ref, dlnw_ref, dlnb_ref, duvb_ref):
        @pl.when(pl.program_id(0) == 0)
        def _():
            dw_ref[...] = jnp.zeros_like(dw_ref)
            dbt_ref[...] = jnp.zeros_like(dbt_ref)
            dlnw_ref[...] = jnp.zeros_like(dlnw_ref)
            dlnb_ref[...] = jnp.zeros_like(dlnb_ref)
            duvb_ref[...] = jnp.zeros_like(duvb_ref)

        lower, _ = _tri_masks()
        uv, u, vhat, rstd, vn = _sgu_pre(uv_ref, uvb_ref, lnw_ref, lnb_ref)
        vn_b = vn.astype(BF16)
        bt = bt_ref[...]
        dy = dy_ref[...].astype(F32)
        du_parts, dvn_parts, dmix_parts = [], [], []
        for g in range(SGU_GROUPS):
            gs = slice(LANES * g, LANES * (g + 1))
            wc = jnp.where(lower, w_ref[g], 0.0).astype(BF16)
            mixed = _dot(wc, vn_b[:, gs]) + bt[:, g:g + 1]
            du_parts.append(dy[:, gs] * mixed)
            dmix = dy[:, gs] * u[:, gs]
            dmix_b = dmix.astype(BF16)
            dmix_parts.append(dmix)
            dw_ref[g] += jnp.where(lower, _dot(dmix_b, vn_b[:, gs], NT), 0.0)
            dvn_parts.append(_dot(wc, dmix_b, TN))
        dmixed = jnp.concatenate(dmix_parts, axis=1)
        dbt_ref[...] += _dot_terms(_split3(dmixed), gsum_ref[...])
        dvn = jnp.concatenate(dvn_parts, axis=1)
        dlnw_ref[...] += jnp.sum(dvn * vhat, axis=0, keepdims=True)
        dlnb_ref[...] += jnp.sum(dvn, axis=0, keepdims=True)
        dvhat = dvn * lnw_ref[...]
        dv = rstd * (dvhat - jnp.mean(dvhat, axis=-1, keepdims=True)
                     - vhat * jnp.mean(dvhat * vhat, axis=-1, keepdims=True))
        dguv = jnp.concatenate(du_parts + [dv], axis=1)
        duv = dguv * _gelu_grad(uv)
        duvb_ref[...] += jnp.sum(duv, axis=0, keepdims=True)
        duv_ref[...] = duv.astype(BF16)

    def const(shape):
        return pl.BlockSpec(shape, lambda c: (0,) * len(shape))

    return pl.pallas_call(
        body, name=name, grid=(nc,),
        in_specs=[pl.BlockSpec((CHUNK, 2 * SGU_WIDTH), lambda c: (c, 0)),
                  pl.BlockSpec((CHUNK, SGU_WIDTH), lambda c: (c, 0)), const((1, 2 * SGU_WIDTH)),
                  const((1, SGU_WIDTH)), const((1, SGU_WIDTH)), const((SGU_GROUPS, CHUNK, CHUNK)),
                  const((CHUNK, LANES)), const((SGU_WIDTH, LANES))],
        out_specs=[pl.BlockSpec((CHUNK, 2 * SGU_WIDTH), lambda c: (c, 0)), const((SGU_GROUPS, CHUNK, CHUNK)),
                   const((CHUNK, LANES)), const((1, SGU_WIDTH)), const((1, SGU_WIDTH)), const((1, 2 * SGU_WIDTH))],
        out_shape=[jax.ShapeDtypeStruct((s, 2 * SGU_WIDTH), BF16),
                   jax.ShapeDtypeStruct((SGU_GROUPS, CHUNK, CHUNK), F32), jax.ShapeDtypeStruct((CHUNK, LANES), F32),
                   jax.ShapeDtypeStruct((1, SGU_WIDTH), F32), jax.ShapeDtypeStruct((1, SGU_WIDTH), F32),
                   jax.ShapeDtypeStruct((1, 2 * SGU_WIDTH), F32)],
        compiler_params=_params(("arbitrary",)),
    )(uv_raw, dyb, uv_b, ln_w, ln_b, w_sp, b_sp_t, group_sum)


def _gate_fwd(gates_raw, b_gate, p_a, p_b, *, name, tm=512):
    s = p_a.shape[0]
    tm = min(tm, s)

    def body(ga_ref, gb_ref, ba_ref, bb_ref, pa_ref, pb_ref, o_ref):
        ga = _sigmoid(ga_ref[...].astype(F32) + ba_ref[...])
        gb = _sigmoid(gb_ref[...].astype(F32) + bb_ref[...])
        o_ref[...] = (ga * pa_ref[...].astype(F32) + gb * pb_ref[...].astype(F32)).astype(BF16)

    t_a = pl.BlockSpec((tm, D_MODEL), lambda i: (i, 0))
    t_b = pl.BlockSpec((tm, D_MODEL), lambda i: (i, 1))
    r_a = pl.BlockSpec((1, D_MODEL), lambda i: (0, 0))
    r_b = pl.BlockSpec((1, D_MODEL), lambda i: (0, 1))
    return pl.pallas_call(
        body, name=name, grid=(s // tm,),
        in_specs=[t_a, t_b, r_a, r_b, t_a, t_a], out_specs=t_a,
        out_shape=jax.ShapeDtypeStruct((s, D_MODEL), BF16),
        compiler_params=_params(("parallel",)),
    )(gates_raw, gates_raw, b_gate, b_gate, p_a, p_b)


def _gate_bwd(gates_raw, b_gate, p_a, p_b, dm, *, name, tm=512):
    s = p_a.shape[0]
    tm = min(tm, s)

    def body(ga_ref, gb_ref, ba_ref, bb_ref, pa_ref, pb_ref, dm_ref, dpa_ref, dpb_ref, dga_ref, dgb_ref,
             dba_ref, dbb_ref):
        @pl.when(pl.program_id(0) == 0)
        def _():
            dba_ref[...] = jnp.zeros_like(dba_ref)
            dbb_ref[...] = jnp.zeros_like(dbb_ref)

        d = dm_ref[...].astype(F32)
        for g_ref, b_ref, p_ref, dp_ref, dg_ref, db_ref in ((ga_ref, ba_ref, pa_ref, dpa_ref, dga_ref, dba_ref),
                                                            (gb_ref, bb_ref, pb_ref, dpb_ref, dgb_ref, dbb_ref)):
            sg = _sigmoid(g_ref[...].astype(F32) + b_ref[...])
            dp_ref[...] = (d * sg).astype(BF16)
            dg = d * p_ref[...].astype(F32) * (sg * (1.0 - sg))
            dg_ref[...] = dg.astype(BF16)
            db_ref[...] += jnp.sum(dg, axis=0, keepdims=True)

    t_a = pl.BlockSpec((tm, D_MODEL), lambda i: (i, 0))
    t_b = pl.BlockSpec((tm, D_MODEL), lambda i: (i, 1))
    r_a = pl.BlockSpec((1, D_MODEL), lambda i: (0, 0))
    r_b = pl.BlockSpec((1, D_MODEL), lambda i: (0, 1))
    big = jax.ShapeDtypeStruct((s, D_MODEL), BF16)
    row = jax.ShapeDtypeStruct((1, D_MODEL), F32)
    return pl.pallas_call(
        body, name=name, grid=(s // tm,),
        in_specs=[t_a, t_b, r_a, r_b, t_a, t_a, t_a], out_specs=[t_a, t_a, t_a, t_a, r_a, r_a],
        out_shape=[big, big, big, big, row, row],
        compiler_params=_params(("arbitrary",)),
    )(gates_raw, gates_raw, b_gate, b_gate, p_a, p_b, dm)


def _adamw_update(w_ref, g_ref, m_ref, v_ref, d_ref, mo_ref, vo_ref):
    gv = g_ref[...]
    mn = ADAM_B1 * m_ref[...] + (1.0 - ADAM_B1) * gv
    vn = ADAM_B2 * v_ref[...] + (1.0 - ADAM_B2) * (gv * gv)
    m_hat = mn / (1.0 - ADAM_B1 ** ADAM_STEP)
    v_hat = vn / (1.0 - ADAM_B2 ** ADAM_STEP)
    d_ref[...] = -ADAM_LR * (m_hat / (jnp.sqrt(v_hat) + ADAM_EPS) + ADAM_WD * w_ref[...])
    mo_ref[...] = mn
    vo_ref[...] = vn


def _adamw_many(ws, gs, ms, vs, *, name):
    n = len(ws)

    def body(*refs):
        for i in range(n):
            _adamw_update(*[refs[k * n + i] for k in range(7)])

    whole = pl.BlockSpec(memory_space=pltpu.VMEM)
    sds = [jax.ShapeDtypeStruct(w.shape, F32) for w in ws]
    outs = pl.pallas_call(
        body, name=name, in_specs=[whole] * (4 * n), out_specs=[whole] * (3 * n), out_shape=sds * 3,
        compiler_params=pltpu.CompilerParams(vmem_limit_bytes=VMEM_LIMIT),
    )(*ws, *gs, *ms, *vs)
    return outs[:n], outs[n:2 * n], outs[2 * n:]


def _adamw(w, g, m, v, *, name, tr=128):
    r, c = w.shape
    tr = min(tr, r)
    assert r % tr == 0, (name, r, tr)
    body = functools.partial(_adamw_update)

    blk = pl.BlockSpec((tr, c), lambda i: (i, 0))
    sds = jax.ShapeDtypeStruct((r, c), F32)
    return pl.pallas_call(
        body, name=name, grid=(r // tr,), in_specs=[blk] * 4, out_specs=[blk] * 3, out_shape=[sds] * 3,
        compiler_params=_params(("parallel",)),
    )(w, g, m, v)


def _adamw_two_sums(w, g_a, g_b, m, v, *, name, tr=128):
    r, c = w.shape
    tr = min(tr, r)
    assert r % tr == 0, (name, r, tr)

    def body(w_ref, ga_ref, gb_ref, m_ref, v_ref, g_ref, d_ref, mo_ref, vo_ref):
        g_ref[...] = ga_ref[...] + gb_ref[...]
        _adamw_update(w_ref, g_ref, m_ref, v_ref, d_ref, mo_ref, vo_ref)

    blk = pl.BlockSpec((tr, c), lambda i: (i, 0))
    sds = jax.ShapeDtypeStruct((r, c), F32)
    return pl.pallas_call(
        body, name=name, grid=(r // tr,), in_specs=[blk] * 5, out_specs=[blk] * 4, out_shape=[sds] * 4,
        compiler_params=_params(("parallel",)),
    )(w, g_a, g_b, m, v)


def _tile(n, pref):
    if n <= pref:
        return n
    best = LANES
    for t in range(LANES, pref + 1, LANES):
        if n % t == 0:
            best = t
    return best


MATMUL_BLOCK_BYTES = 20 * 1024 * 1024


def _mm(pairs, name, **kw):
    trans_b = kw.get("trans_b", False)
    m = (pairs[0][0][0] if isinstance(pairs[0][0], tuple) else pairs[0][0]).shape[0]
    ktot, n = 0, None
    for _, b in pairs:
        shape = b[0].shape[1:] if isinstance(b, tuple) else b.shape
        ktot += shape[1] if trans_b else shape[0]
        n = shape[0] if trans_b else shape[1]
    out_bytes = 4 * (2 if kw.get("add") is not None else 1)
    best = None
    for tm in (256, 512, 1024, 2048):
        for tn in range(LANES, min(n, 1536) + 1, LANES):
            if m % min(tm, m) or n % tn:
                continue
            fits = 2 * ktot * (min(tm, m) + tn) + out_bytes * min(tm, m) * tn <= MATMUL_BLOCK_BYTES
            if fits and (best is None or min(tm, m) * tn >= best[0] * best[1]):
                best = (min(tm, m), tn)
    return _matmul(pairs, tm=best[0], tn=best[1], name=name, **kw)


def _wgrad(a, b, name, **kw):
    return _matmul_tn(a, b, tk=_tile(a.shape[1], 1408), tn=kw.pop("tn", _tile(b.shape[1], 1024)), tm=2048,
                      name=name, **kw)


def _local_step(x, target, get_weight, small, emit_grad):
    heads = jnp.arange(SSD_D_INNER) // SSD_HEAD_DIM
    expand = (jnp.arange(LANES)[:, None] == heads[None, :]).astype(BF16)
    expand_t = expand.T
    group_sum = (jnp.arange(SGU_WIDTH)[:, None] // LANES == jnp.arange(LANES)[None, :]).astype(BF16)
    pad_h = LANES - SSD_HEADS
    dt_bias = jnp.pad(small["dt_bias"], ((0, 0), (0, pad_h)))
    a_log = jnp.pad(small["a_log"], ((0, 0), (0, pad_h)))
    a_log_x = jnp.repeat(small["a_log"], SSD_HEAD_DIM, axis=1)
    d_skip_x = jnp.repeat(small["d_skip"], SSD_HEAD_DIM, axis=1)
    b_sp_t = jnp.pad(small["b_spatial"][0].T, ((0, 0), (0, LANES - SGU_GROUPS)))
    w_sp = small["w_spatial"][0]
    conv_a_w = jnp.pad(small["conv_a_w"], ((0, 4), (0, 0)))
    conv_f_w = jnp.pad(small["conv_f_w"], ((0, 5), (0, 0)))
    final_w = small["final_norm_w"].reshape(1, D_MODEL)

    n1 = _rms_fwd(x, small["norm1_w"], after=small.get("gathers_started"), name="rms1_fwd")
    wts = dict(get_weight("w_in", n1))
    z = _mm([(n1, wts["in_z"])], "in_z")
    xbc_raw = _mm([(n1, wts["in_xbc"])], "in_xbc")
    dt_raw = _mm([(n1, wts["in_dt"])], "in_dt")
    uv_raw = _mm([(n1, wts["in_uv"])], "in_uv", out_dtype=BF16)
    gates_raw = _mm([(n1, wts["in_gate"])], "in_gate", out_dtype=BF16)
    xbc = _conv_a_fwd(xbc_raw, conv_a_w, small["conv_a_b"], name="conv_a_fwd")
    y, y_a, states = _ssd_fwd(xbc, dt_raw, z, dt_bias, a_log, a_log_x, d_skip_x, small["ssd_norm_w"], expand,
                              name="ssd_fwd")
    y_b = _sgu_fwd(uv_raw, small["uv_b"], small["v_ln_w"], small["v_ln_b"], w_sp, b_sp_t, name="sgu_fwd")
    wts.update(get_weight("w_branch", y_b))
    p_a = _mm([(y_a, wts["branch_a"])], "branch_a", out_dtype=BF16)
    p_b = _mm([(y_b, wts["branch_b"])], "branch_b", out_dtype=BF16)
    mix = _gate_fwd(gates_raw, small["b_gate"], p_a, p_b, name="gate_fwd")
    wts.update(get_weight("w_out", mix))
    h1 = _mm([(mix, wts["out"])], "out_proj", add=x)
    n2 = _rms_fwd(h1, small["norm2_w"], name="rms2_fwd")
    wts.update(get_weight("w_up", n2))
    up_w = wts["up"]
    up_cols = up_w.shape[2]
    up_raw = _matmul([(n2, (up_w, "cols"))], tm=2048, tn=up_cols, out_dtype=BF16, name="up_proj")
    act = _conv_f_fwd(up_raw, conv_f_w, small["conv_f_b"], name="conv_f_fwd")
    wts.update(get_weight("w_down", act))
    h2 = _mm([(act, wts["down"])], "down_proj", add=h1)
    loss, dh2, dh2_b, d_final = _final_fwd_bwd(h2, final_w, target, name="final_norm_loss")

    dact = _mm([(dh2_b, wts["down"])], "down_dgrad", trans_b=True)
    started = emit_grad("w_down", _wgrad(act, dh2_b, "down_wgrad"))
    dup_a, dup_v, dwf_a, dwf_v, dbf_a, dbf_v = _conv_f_bwd(up_raw, conv_f_w, small["conv_f_b"], dact,
                                                           name="conv_f_bwd")
    dn2 = _mm([((dup_a, 0), (up_w, 0)), ((dup_a, 1), (up_w, 1)), ((dup_v, 0), (up_w, 2)), ((dup_v, 1), (up_w, 3))],
              "up_dgrad", trans_b=True, after=started, out_dtype=BF16)
    started = emit_grad("w_up", jnp.concatenate([_wgrad(n2, dup_a, "up_wgrad_a", tn=up_cols, stack_out=True),
                                                 _wgrad(n2, dup_v, "up_wgrad_v", tn=up_cols, stack_out=True)], axis=0))
    dh1, dh1_b, d_norm2 = _rms_bwd(h1, small["norm2_w"], dn2, dh2, name="rms2_bwd")
    dmix = _mm([(dh1_b, wts["out"])], "out_dgrad", trans_b=True, after=started, out_dtype=BF16)
    started = emit_grad("w_out", _wgrad(mix, dh1_b, "out_wgrad"))
    dp_a, dp_b, dg_a, dg_b, dbg_a, dbg_b = _gate_bwd(gates_raw, small["b_gate"], p_a, p_b, dmix, name="gate_bwd")
    dya = _mm([(dp_a, wts["branch_a"])], "branch_a_dgrad", trans_b=True, after=started)
    dyb = _mm([(dp_b, wts["branch_b"])], "branch_b_dgrad", trans_b=True, out_dtype=BF16)
    started_branch = emit_grad("w_branch", jnp.concatenate([_wgrad(y_a, dp_a, "branch_a_wgrad"),
                                                            _wgrad(y_b, dp_b, "branch_b_wgrad")], axis=0))
    duv, d_wsp, d_bsp_t, d_lnw, d_lnb, d_uvb = _sgu_bwd(uv_raw, dyb, small["uv_b"], small["v_ln_w"],
                                                        small["v_ln_b"], w_sp, b_sp_t, group_sum, name="sgu_bwd")
    dz, dxbc, ddt, d_ssd_nw, d_dskip, d_alog, d_dtb = _ssd_bwd(
        dya, y, z, xbc, dt_raw, states, dt_bias, a_log, a_log_x, d_skip_x, small["ssd_norm_w"], expand, expand_t,
        name="ssd_bwd")
    dxbc_raw, d_conv_a_w, d_conv_a_b = _conv_a_bwd(xbc_raw, conv_a_w, small["conv_a_b"], dxbc, name="conv_a_bwd")
    started = emit_grad("w_in", {
        "in_z": _wgrad(n1, dz, "in_z_wgrad", after=started_branch), "in_xbc": _wgrad(n1, dxbc_raw, "in_xbc_wgrad"),
        "in_dt": _wgrad(n1, ddt, "in_dt_wgrad")[:, :SSD_HEADS], "in_uv": _wgrad(n1, duv, "in_uv_wgrad"),
        "in_gate_a": _wgrad(n1, dg_a, "in_gate_a_wgrad"), "in_gate_b": _wgrad(n1, dg_b, "in_gate_b_wgrad")})
    dn1 = _mm([(dz, wts["in_z"]), (dxbc_raw, wts["in_xbc"]), (ddt, wts["in_dt"]), (duv, wts["in_uv"]),
               (dg_a, wts["in_gate_a"]), (dg_b, wts["in_gate_b"])], "in_dgrad", trans_b=True, after=started,
              out_dtype=BF16)
    dx, _, d_norm1 = _rms_bwd(x, small["norm1_w"], dn1, dh1, name="rms1_bwd")

    grads_small = {
        "norm1_w": d_norm1, "b_gate": jnp.concatenate([dbg_a, dbg_b], axis=1),
        "conv_a_w": d_conv_a_w[:4], "conv_a_b": d_conv_a_b,
        "dt_bias": d_dtb[:, :SSD_HEADS], "a_log": d_alog[:, :SSD_HEADS], "d_skip": d_dskip[:, :SSD_HEADS],
        "ssd_norm_w": d_ssd_nw, "uv_b": d_uvb, "v_ln_w": d_lnw, "v_ln_b": d_lnb,
        "w_spatial": d_wsp[None], "b_spatial": d_bsp_t[:, :SGU_GROUPS].T[None],
        "norm2_w": d_norm2, "conv_f_w": jnp.concatenate([dwf_a[:3], dwf_v[:3]], axis=1),
        "conv_f_b": jnp.concatenate([dbf_a, dbf_v], axis=1), "final_norm_w": d_final.reshape(D_MODEL),
    }
    return loss, dx, grads_small


HBM = pl.BlockSpec(memory_space=pl.ANY)
MESH = pl.DeviceIdType.MESH


def _mesh_pos():
    return lax.axis_index("x"), lax.axis_index("y"), lax.axis_index("c")


def _other_chips(x, y):
    return [(1 - x, y), (x, 1 - y), (1 - x, 1 - y)]


def _remote(src, dst, send_sems, recv_sems, k, dev):
    return pltpu.make_async_remote_copy(src_ref=src, dst_ref=dst, send_sem=send_sems.at[k], recv_sem=recv_sems.at[k],
                                        device_id=dev, device_id_type=MESH)


def _dma_sems(n):
    return [pltpu.SemaphoreType.DMA((n,)), pltpu.SemaphoreType.DMA((n,))]


HBM_ONLY = pl.BlockSpec(memory_space=pltpu.HBM)
SEMAPHORES = pl.BlockSpec(memory_space=pltpu.SEMAPHORE)
DATAFLOW_EFFECT = pltpu.SideEffectType.DATAFLOW_SIDE_EFFECTING
N_PEER_CHIPS = N_CHIPS - 1


def _gather_sends(w_ref, land_ref, send_sems, recv_sems):
    x, y, c = _mesh_pos()
    return [_remote(w_ref.at[c], land_ref.at[2 * x + y, c], send_sems, recv_sems, k, (px, py, c))
            for k, (px, py) in enumerate(_other_chips(x, y))]


def _gather_arrivals(w_ref, land_ref, send_sems, recv_sems):
    x, y, c = _mesh_pos()
    return [_remote(w_ref.at[c], land_ref.at[2 * px + py, c], send_sems, recv_sems, k, (px, py, c))
            for k, (px, py) in enumerate(_other_chips(x, y))]


def _gather_whole_sends(w_ref, land_ref, send_sems, recv_sems):
    x, y, c = _mesh_pos()
    return [_remote(w_ref, land_ref.at[2 * x + y], send_sems, recv_sems, k, (px, py, c))
            for k, (px, py) in enumerate(_other_chips(x, y))]


def _gather_whole_arrivals(w_ref, land_ref, send_sems, recv_sems):
    x, y, c = _mesh_pos()
    return [_remote(w_ref, land_ref.at[2 * px + py], send_sems, recv_sems, k, (px, py, c))
            for k, (px, py) in enumerate(_other_chips(x, y))]


def _scatter_sends(h_ref, land_ref, send_sems, recv_sems):
    x, y, c = _mesh_pos()
    return [_remote(h_ref.at[2 * px + py], land_ref.at[2 * x + y], send_sems, recv_sems, k, (px, py, c))
            for k, (px, py) in enumerate(_other_chips(x, y))]


def _scatter_arrivals(h_ref, land_ref, send_sems, recv_sems):
    x, y, c = _mesh_pos()
    return [_remote(h_ref.at[2 * x + y], land_ref.at[2 * px + py], send_sems, recv_sems, k, (px, py, c))
            for k, (px, py) in enumerate(_other_chips(x, y))]


def _exchange_wait_many(pendings, after, sends, arrivals, *, name):
    n = len(pendings)

    def body(*refs):
        for i in range(n):
            src_ref, land_ref, send_ref, recv_ref = refs[i], refs[n + i], refs[2 * n + i], refs[3 * n + i]
            for cp in sends(src_ref, land_ref, send_ref, recv_ref):
                cp.wait_send()
            for cp in arrivals(src_ref, land_ref, send_ref, recv_ref):
                cp.wait_recv()

    sources = [p[2] for p in pendings]
    landings = [p[3] for p in pendings]
    outs = pl.pallas_call(
        body, name=name,
        out_shape=tuple(pltpu.HBM(a.shape, a.dtype) for a in sources + landings),
        in_specs=[HBM_ONLY] * (2 * n) + [SEMAPHORES] * (2 * n) + [pl.BlockSpec(memory_space=pl.ANY)],
        out_specs=tuple([HBM_ONLY] * (2 * n)), input_output_aliases={i: i for i in range(2 * n)},
        compiler_params=pltpu.CompilerParams(has_side_effects=DATAFLOW_EFFECT),
    )(*sources, *landings, *[p[0] for p in pendings], *[p[1] for p in pendings], after)
    return [(outs[i], outs[n + i]) for i in range(n)]


def _sibling_sends(src_ref, land_ref, send_sems, recv_sems):
    x, y, c = _mesh_pos()
    return [_remote(src_ref, land_ref, send_sems, recv_sems, 0, (x, y, 1 - c))]


def _exchange_start(sources, landing_shapes, sends, *, after=None, name):
    n = len(sources)
    extra = [] if after is None else [after]

    def body(*refs):
        sems = refs[2 * n + len(extra):4 * n + len(extra)]
        for i in range(n):
            send_i = sends[i] if isinstance(sends, (list, tuple)) else sends
            for cp in send_i(refs[i], refs[n + i], sems[2 * i], sems[2 * i + 1]):
                cp.start()
        refs[-1][...] = jnp.zeros_like(refs[-1])

    hbm = [pltpu.HBM(s.shape, s.dtype) for s in sources] + [pltpu.HBM(shp, s.dtype)
                                                             for shp, s in zip(landing_shapes, sources)]
    outs = pl.pallas_call(
        body, name=name,
        out_shape=tuple([pltpu.SemaphoreType.DMA((N_PEER_CHIPS,))] * (2 * n) + hbm
                        + [jax.ShapeDtypeStruct((8, LANES), F32)]),
        in_specs=[HBM_ONLY] * (2 * n) + [pl.BlockSpec(memory_space=pl.ANY)] * len(extra),
        out_specs=tuple([SEMAPHORES] * (2 * n) + [HBM_ONLY] * (2 * n) + [pl.BlockSpec(memory_space=pltpu.VMEM)]),
        input_output_aliases={i: 2 * n + i for i in range(2 * n)},
        compiler_params=pltpu.CompilerParams(has_side_effects=DATAFLOW_EFFECT),
    )(*[pltpu.with_memory_space_constraint(s, pltpu.HBM) for s in sources],
      *[pltpu.with_memory_space_constraint(lax.empty(shp, s.dtype), pltpu.HBM)
        for shp, s in zip(landing_shapes, sources)], *extra)
    pending = [(outs[2 * i], outs[2 * i + 1], outs[2 * n + i], outs[3 * n + i]) for i in range(n)]
    return pending, outs[-1]


def _exchange_wait(pending, after, sends, arrivals, *, name):
    send_sems, recv_sems, source, landing = pending

    def body(src_ref, land_ref, send_ref, recv_ref, after_ref, src_out, land_out):
        for cp in sends(src_ref, land_ref, send_ref, recv_ref):
            cp.wait_send()
        for cp in arrivals(src_ref, land_ref, send_ref, recv_ref):
            cp.wait_recv()

    return pl.pallas_call(
        body, name=name,
        out_shape=(pltpu.HBM(source.shape, source.dtype), pltpu.HBM(landing.shape, landing.dtype)),
        in_specs=[HBM_ONLY, HBM_ONLY, SEMAPHORES, SEMAPHORES, pl.BlockSpec(memory_space=pl.ANY)],
        out_specs=(HBM_ONLY, HBM_ONLY), input_output_aliases={0: 0, 1: 1},
        compiler_params=pltpu.CompilerParams(has_side_effects=DATAFLOW_EFFECT),
    )(source, landing, send_sems, recv_sems, after)


def _gather_ici(shard, *, name):
    _, rh, cols = shard.shape

    def body(w_ref, o_ref, send_sems, recv_sems):
        x, y, c = _mesh_pos()
        mine = 2 * x + y
        sends = []
        for k, (px, py) in enumerate(_other_chips(x, y)):
            cp = _remote(w_ref.at[c], o_ref.at[mine, c], send_sems, recv_sems, k, (px, py, c))
            cp.start()
            sends.append(cp)
        for k, (px, py) in enumerate(_other_chips(x, y)):
            _remote(w_ref.at[c], o_ref.at[2 * px + py, c], send_sems, recv_sems, k, (px, py, c)).wait_recv()
        for cp in sends:
            cp.wait_send()

    return pl.pallas_call(
        body, name=name, in_specs=[HBM], out_specs=HBM,
        out_shape=jax.ShapeDtypeStruct((N_CHIPS, 2, rh, cols), shard.dtype), scratch_shapes=_dma_sems(3),
    )(shard)


def _gather_d2d(parts, *, name):
    def body(a_ref, o_ref, send_sems, recv_sems):
        x, y, c = _mesh_pos()
        sibling = (x, y, 1 - c)
        sends = []
        for k, (px, py) in enumerate(_other_chips(x, y)):
            cp = _remote(a_ref.at[2 * px + py, c], o_ref.at[2 * px + py, c], send_sems, recv_sems, k, sibling)
            cp.start()
            sends.append(cp)
        for k, (px, py) in enumerate(_other_chips(x, y)):
            _remote(a_ref.at[2 * px + py, c], o_ref.at[2 * px + py, 1 - c], send_sems, recv_sems, k, sibling).wait_recv()
        for cp in sends:
            cp.wait_send()

    return pl.pallas_call(
        body, name=name, in_specs=[HBM], out_specs=HBM,
        out_shape=jax.ShapeDtypeStruct(parts.shape, parts.dtype),
        input_output_aliases={0: 0}, scratch_shapes=_dma_sems(3),
    )(parts)


def _all_gather_chips(shard_flat, name):
    rows, cols = shard_flat.shape
    parts = _gather_ici(shard_flat.reshape(2, rows // 2, cols), name=name + "_ici")
    others = _gather_d2d(parts, name=name + "_d2d").reshape(N_CHIPS, rows, cols)
    chip = 2 * lax.axis_index("x") + lax.axis_index("y")
    return lax.dynamic_update_slice(others, shard_flat[None], (chip, 0, 0))


def _row_tile(rows, mult, cap):
    best = mult
    for t in range(mult, min(rows, cap) + 1, mult):
        if rows % t == 0:
            best = t
    assert rows % best == 0, (rows, mult)
    return best


def _swap_halves_d2d(g, *, after=None, name):
    _, _, rh, cols = g.shape
    extra = [] if after is None else [after]

    def body(g_ref, *rest):
        o_ref, send_sems, recv_sems = rest[len(extra):]
        x, y, c = _mesh_pos()
        sibling = (x, y, 1 - c)
        sends = []
        for s in range(N_CHIPS):
            cp = _remote(g_ref.at[s, 1 - c], o_ref.at[s], send_sems, recv_sems, s, sibling)
            cp.start()
            sends.append(cp)
        for s in range(N_CHIPS):
            _remote(g_ref.at[s, c], o_ref.at[s], send_sems, recv_sems, s, sibling).wait_recv()
        for cp in sends:
            cp.wait_send()

    return pl.pallas_call(
        body, name=name, in_specs=[HBM] * (1 + len(extra)), out_specs=HBM,
        out_shape=jax.ShapeDtypeStruct((N_CHIPS, rh, cols), g.dtype), scratch_shapes=_dma_sems(N_CHIPS),
    )(g, *extra)


def _add_own_half(g, arrived, core, *, name):
    _, _, rh, cols = g.shape
    mult = 16 if g.dtype == BF16 else 8
    tr = _row_tile(rh, mult, max(mult, (512 * 1024) // cols))

    def body(core_ref, g_ref, a_ref, o_ref):
        o_ref[...] = (g_ref[0].astype(F32) + a_ref[...].astype(F32)).astype(o_ref.dtype)

    grid_spec = pltpu.PrefetchScalarGridSpec(
        num_scalar_prefetch=1, grid=(N_CHIPS, rh // tr),
        in_specs=[pl.BlockSpec((1, 1, tr, cols), lambda s, i, core_ref: (s, core_ref[0], i, 0)),
                  pl.BlockSpec((1, tr, cols), lambda s, i, core_ref: (s, i, 0))],
        out_specs=pl.BlockSpec((1, tr, cols), lambda s, i, core_ref: (s, i, 0)))
    return pl.pallas_call(
        body, name=name, grid_spec=grid_spec, out_shape=jax.ShapeDtypeStruct((N_CHIPS, rh, cols), g.dtype),
        compiler_params=_params(("parallel", "parallel")),
    )(core, g, arrived)


def _scatter_ici(h, *, name):
    def body(h_ref, o_ref, send_sems, recv_sems):
        x, y, c = _mesh_pos()
        mine = 2 * x + y
        sends = []
        for k, (px, py) in enumerate(_other_chips(x, y)):
            cp = _remote(h_ref.at[2 * px + py], o_ref.at[mine], send_sems, recv_sems, k, (px, py, c))
            cp.start()
            sends.append(cp)
        for k, (px, py) in enumerate(_other_chips(x, y)):
            _remote(h_ref.at[mine], o_ref.at[2 * px + py], send_sems, recv_sems, k, (px, py, c)).wait_recv()
        for cp in sends:
            cp.wait_send()

    others = pl.pallas_call(
        body, name=name, in_specs=[HBM], out_specs=HBM, out_shape=jax.ShapeDtypeStruct(h.shape, h.dtype),
        scratch_shapes=_dma_sems(3),
    )(h)
    chip = 2 * lax.axis_index("x") + lax.axis_index("y")
    own = lax.dynamic_slice_in_dim(h, chip, 1, axis=0)
    return lax.dynamic_update_slice(others, own, (chip, 0, 0))


def _sum_chips(parts, *, name):
    _, rh, cols = parts.shape
    mult = 16 if parts.dtype == BF16 else 8
    tr = _row_tile(rh, mult, max(mult, (512 * 1024) // cols))

    def body(p_ref, o_ref):
        acc = p_ref[0].astype(F32)
        for s in range(1, N_CHIPS):
            acc = acc + p_ref[s].astype(F32)
        o_ref[...] = acc

    return pl.pallas_call(
        body, name=name, grid=(rh // tr,),
        in_specs=[pl.BlockSpec((N_CHIPS, tr, cols), lambda i: (0, i, 0))],
        out_specs=pl.BlockSpec((tr, cols), lambda i: (i, 0)),
        out_shape=jax.ShapeDtypeStruct((rh, cols), F32), compiler_params=_params(("parallel",)),
    )(parts)


def _share_d2d(f, *, name):
    fs = f if isinstance(f, (list, tuple)) else [f]
    others = _swap_with_sibling(fs, name=name)
    first = lax.axis_index("c") == 0
    both = [jnp.stack([jnp.where(first, a, b), jnp.where(first, b, a)]) for a, b in zip(fs, others)]
    return both if isinstance(f, (list, tuple)) else both[0]


def _swap_with_sibling(fs, *, name):
    n = len(fs)

    def body(*refs):
        x, y, c = _mesh_pos()
        sibling = (x, y, 1 - c)
        send_sems, recv_sems = refs[2 * n:]
        copies = [_remote(refs[i], refs[n + i], send_sems, recv_sems, i, sibling) for i in range(n)]
        for cp in copies:
            cp.start()
        for cp in copies:
            cp.wait()

    return pl.pallas_call(
        body, name=name, in_specs=[HBM] * n, out_specs=[HBM] * n,
        out_shape=[jax.ShapeDtypeStruct(a.shape, a.dtype) for a in fs], scratch_shapes=_dma_sems(n),
    )(*fs)


def _reduce_scatter_chips(g, core, name, after=None):
    _, rows, cols = g.shape
    g = g.reshape(N_CHIPS, 2, rows // 2, cols)
    arrived = _swap_halves_d2d(g, after=after, name=name + "_swap")
    chip_sum = _add_own_half(g, arrived, core, name=name + "_add2")
    parts = _scatter_ici(chip_sum, name=name + "_ici")
    total = _sum_chips(parts, name=name + "_sum4")
    return _share_d2d(total, name=name + "_share").reshape(rows, cols)


BIG = ("w_in", "w_branch", "w_out", "w_up", "w_down")
BIG_COLUMN_SHARDED = ("w_in", "w_up")
CONV = ("conv_a_w", "conv_f_w")
REPLICATED = ("norm1_w", "b_gate", "conv_a_b", "dt_bias", "a_log", "d_skip", "ssd_norm_w", "uv_b", "v_ln_w",
              "v_ln_b", "w_spatial", "b_spatial", "norm2_w", "conv_f_b", "final_norm_w")
WEIGHT_ORDER = ("norm1_w", "w_in", "b_gate", "conv_a_w", "conv_a_b", "dt_bias", "a_log", "d_skip", "ssd_norm_w",
                "uv_b", "v_ln_w", "v_ln_b", "w_spatial", "b_spatial", "w_branch", "w_out", "norm2_w", "w_up",
                "conv_f_w", "conv_f_b", "w_down", "final_norm_w")
SMALL_EXCHANGE_ROWS = 64


_GATE0 = SSD_IN + 2 * SGU_WIDTH
IN_SEGMENTS = {
    "in_z": (0, SSD_D_INNER), "in_xbc": (SSD_D_INNER, SSD_D_INNER + SSD_XBC), "in_dt": (SSD_D_INNER + SSD_XBC, SSD_IN),
    "in_uv": (SSD_IN, _GATE0), "in_gate": (_GATE0, IN_COLS), "in_gate_a": (_GATE0, _GATE0 + D_MODEL),
    "in_gate_b": (_GATE0 + D_MODEL, IN_COLS),
}
IN_GRAD_SEGMENTS = ("in_z", "in_xbc", "in_dt", "in_uv", "in_gate_a", "in_gate_b")


def _take_columns(parts, start, stop):
    out = []
    for a, first in parts:
        lo, hi = max(start, first), min(stop, first + a.shape[1])
        if lo < hi:
            out.append(a[:, lo - first:hi - first])
    return out[0] if len(out) == 1 else jnp.concatenate(out, axis=1)


def _flat_rows(arrays, row_multiple):
    flat = jnp.concatenate([a.reshape(-1) for a in arrays])
    rows = -(-flat.shape[0] // (LANES * row_multiple)) * row_multiple
    return jnp.pad(flat, (0, rows * LANES - flat.shape[0])).reshape(rows, LANES)


def _unflatten(flat, shapes):
    flat = flat.reshape(-1)
    out, off = [], 0
    for shp in shapes:
        n = math.prod(shp)
        out.append(flat[off:off + n].reshape(shp))
        off += n
    return out


def _from_chip_blocks(blocks, name):
    if name in BIG_COLUMN_SHARDED or name in CONV:
        k = blocks.shape[1]
        return jnp.transpose(blocks, (1, 0, 2)).reshape(k, -1)
    return blocks.reshape(-1, blocks.shape[-1])


def _to_chip_blocks(whole, name):
    if name in BIG_COLUMN_SHARDED or name in CONV:
        k, n = whole.shape
        return jnp.transpose(whole.reshape(k, N_CHIPS, n // N_CHIPS), (1, 0, 2))
    return whole.reshape(N_CHIPS, whole.shape[0] // N_CHIPS, whole.shape[1])


def kernel(x, norm1_w, w_in, b_gate, conv_a_w, conv_a_b, dt_bias, a_log, d_skip, ssd_norm_w, uv_b, v_ln_w, v_ln_b, w_spatial, b_spatial, w_branch, w_out, norm2_w, w_up, conv_f_w, conv_f_b, w_down, final_norm_w, loss_target, m_norm1_w, m_w_in, m_b_gate, m_conv_a_w, m_conv_a_b, m_dt_bias, m_a_log, m_d_skip, m_ssd_norm_w, m_uv_b, m_v_ln_w, m_v_ln_b, m_w_spatial, m_b_spatial, m_w_branch, m_w_out, m_norm2_w, m_w_up, m_conv_f_w, m_conv_f_b, m_w_down, m_final_norm_w, v_norm1_w, v_w_in, v_b_gate, v_conv_a_w, v_conv_a_b, v_dt_bias, v_a_log, v_d_skip, v_ssd_norm_w, v_uv_b, v_v_ln_w, v_v_ln_b, v_w_spatial, v_b_spatial, v_w_branch, v_w_out, v_norm2_w, v_w_up, v_conv_f_w, v_conv_f_b, v_w_down, v_final_norm_w):
    weights = dict(norm1_w=norm1_w, w_in=w_in, b_gate=b_gate, conv_a_w=conv_a_w, conv_a_b=conv_a_b, dt_bias=dt_bias,
                   a_log=a_log, d_skip=d_skip, ssd_norm_w=ssd_norm_w, uv_b=uv_b, v_ln_w=v_ln_w, v_ln_b=v_ln_b,
                   w_spatial=w_spatial, b_spatial=b_spatial, w_branch=w_branch, w_out=w_out, norm2_w=norm2_w,
                   w_up=w_up, conv_f_w=conv_f_w, conv_f_b=conv_f_b, w_down=w_down, final_norm_w=final_norm_w)
    mom1 = dict(norm1_w=m_norm1_w, w_in=m_w_in, b_gate=m_b_gate, conv_a_w=m_conv_a_w, conv_a_b=m_conv_a_b,
                dt_bias=m_dt_bias, a_log=m_a_log, d_skip=m_d_skip, ssd_norm_w=m_ssd_norm_w, uv_b=m_uv_b,
                v_ln_w=m_v_ln_w, v_ln_b=m_v_ln_b, w_spatial=m_w_spatial, b_spatial=m_b_spatial, w_branch=m_w_branch,
                w_out=m_w_out, norm2_w=m_norm2_w, w_up=m_w_up, conv_f_w=m_conv_f_w, conv_f_b=m_conv_f_b,
                w_down=m_w_down, final_norm_w=m_final_norm_w)
    mom2 = dict(norm1_w=v_norm1_w, w_in=v_w_in, b_gate=v_b_gate, conv_a_w=v_conv_a_w, conv_a_b=v_conv_a_b,
                dt_bias=v_dt_bias, a_log=v_a_log, d_skip=v_d_skip, ssd_norm_w=v_ssd_norm_w, uv_b=v_uv_b,
                v_ln_w=v_v_ln_w, v_ln_b=v_v_ln_b, w_spatial=v_w_spatial, b_spatial=v_b_spatial, w_branch=v_w_branch,
                w_out=v_w_out, norm2_w=v_norm2_w, w_up=v_w_up, conv_f_w=v_conv_f_w, conv_f_b=v_conv_f_b,
                w_down=v_w_down, final_norm_w=v_final_norm_w)
    chip = 2 * lax.axis_index("x") + lax.axis_index("y")
    core = lax.axis_index("c").astype(jnp.int32).reshape(1)

    whole = {}
    conv_shapes = [weights[n].shape[1:] for n in CONV]
    conv_gathered = _all_gather_chips(_flat_rows([weights[n] for n in CONV], 16), "gather_conv").reshape(N_CHIPS, -1)
    off = 0
    for n, shp in zip(CONV, conv_shapes):
        size = math.prod(shp)
        whole[n] = _from_chip_blocks(conv_gathered[:, off:off + size].reshape((N_CHIPS,) + shp), n)
        off += size
    shard_shapes = {n: weights[n].shape[1:] for n in BIG}
    halves = [weights[n][0].astype(BF16).reshape(2, shard_shapes[n][0] // 2, shard_shapes[n][1]) for n in BIG]
    sends = [_gather_sends if n == "w_in" else _gather_whole_sends for n in BIG]
    gathers, gathers_started = _exchange_start(halves, [(N_CHIPS,) + h.shape for h in halves], sends,
                                               after=conv_gathered, name="gather_start")
    gathers = dict(zip(BIG, gathers))

    def get_weight(name, after):
        rows, cols = shard_shapes[name]
        if name == "w_in":
            own, landed = _exchange_wait(gathers[name], after, _gather_sends, _gather_arrivals,
                                         name="gather_" + name + "_wait")
            landed = _gather_d2d(landed, name="gather_" + name + "_d2d")
        else:
            own, landed = _exchange_wait(gathers[name], after, _gather_whole_sends, _gather_whole_arrivals,
                                         name="gather_" + name + "_wait")
        blocks = lax.dynamic_update_slice(landed.reshape(N_CHIPS, rows, cols), own.reshape(1, rows, cols),
                                          (chip, 0, 0))
        if name == "w_up":
            return {"up": blocks}
        if name == "w_in":
            parts = [(blocks[k], cols * k) for k in range(N_CHIPS)]
            segs = {n: _take_columns(parts, a, b) for n, (a, b) in IN_SEGMENTS.items()}
            segs["in_dt"] = jnp.pad(segs["in_dt"], ((0, 0), (0, LANES - SSD_HEADS)))
            return segs
        full = _from_chip_blocks(blocks, name)
        if name == "w_branch":
            return {"branch_a": full[:SSD_D_INNER], "branch_b": full[SSD_D_INNER:]}
        return {name[2:]: full}

    small = {n: weights[n] for n in REPLICATED}
    small["conv_a_w"] = whole["conv_a_w"]
    small["conv_f_w"] = whole["conv_f_w"]
    small["gathers_started"] = gathers_started

    reductions = {}

    def emit_grad(name, g):
        if name == "w_in":
            parts = [(g[n], IN_SEGMENTS[n][0]) for n in IN_GRAD_SEGMENTS]
            cols = shard_shapes[name][1]
            g_blocks = jnp.stack([_take_columns(parts, cols * k, cols * (k + 1)) for k in range(N_CHIPS)])
        else:
            g_blocks = g if name == "w_up" else _to_chip_blocks(g, name)
        if name == "w_in":
            _, rows, cols = g_blocks.shape
            g_halves = g_blocks.reshape(N_CHIPS, 2, rows // 2, cols)
            arrived = _swap_halves_d2d(g_halves, name="reduce_" + name + "_swap")
            g_blocks = _add_own_half(g_halves, arrived, core, name="reduce_" + name + "_add2")
        own = lax.dynamic_slice_in_dim(g_blocks, chip, 1, axis=0)
        (pending,), started = _exchange_start([g_blocks], [g_blocks.shape], _scatter_sends,
                                              name="reduce_" + name + "_start")
        reductions[name] = (pending, own)
        return started

    loss, dx, grads_small = _local_step(x[0], loss_target[0], get_weight, small, emit_grad)

    order = ("w_down", "w_up", "w_out", "w_branch", "w_in")
    core_sums = []
    for n in order:
        pending, own = reductions[n]
        _, landed = _exchange_wait(pending, dx, _scatter_sends, _scatter_arrivals, name="reduce_" + n + "_wait")
        parts = lax.dynamic_update_slice(landed, own, (chip, 0, 0))
        core_sums.append(_sum_chips(parts, name="reduce_" + n + "_sum4"))
    swaps, swaps_started = _exchange_start(core_sums, [a.shape for a in core_sums], _sibling_sends, name="reduce_swap_start")
    grads = {}

    small_names = REPLICATED + CONV + ("loss",)
    grads_small = dict(grads_small, loss=loss)
    small_shapes = [grads_small[n].shape for n in small_names]
    g_small = _flat_rows([grads_small[n] for n in small_names], N_CHIPS * 2 * SMALL_EXCHANGE_ROWS)
    red_small = _reduce_scatter_chips(g_small.reshape(N_CHIPS, -1, LANES), core, "reduce_small", after=swaps_started)
    all_small = _all_gather_chips(red_small, "gather_small")
    swapped = _exchange_wait_many(swaps, all_small, _sibling_sends, _sibling_sends, name="reduce_swap_wait")
    core_sums = {n: own for n, (own, _) in zip(order, swapped)}
    sibling_sums = {n: other for n, (_, other) in zip(order, swapped)}
    first = lax.axis_index("c") == 0
    w_in_halves = (core_sums["w_in"], sibling_sums["w_in"])
    w_in_grad = jnp.concatenate([jnp.where(first, w_in_halves[0], w_in_halves[1]),
                                 jnp.where(first, w_in_halves[1], w_in_halves[0])], axis=0)
    for n, g in zip(small_names, _unflatten(all_small, small_shapes)):
        if n == "loss":
            total_loss = g[0, 0]
            continue
        if n in CONV:
            width = g.shape[1] // N_CHIPS
            g = lax.dynamic_slice_in_dim(g, chip * width, width, axis=1)
        grads[n] = g.reshape(weights[n].shape[1:]) if n != "final_norm_w" else g

    delta, new_m, new_v = {}, {}, {}
    for n in BIG:
        shp = weights[n].shape
        if n == "w_in":
            g_t = w_in_grad.T
            results = [g_t] + list(_adamw(weights[n][0].T, g_t, mom1[n][0].T, mom2[n][0].T, name="adamw_" + n,
                                          tr=_row_tile(g_t.shape[0], 8, 136)))
            results = [a.T for a in results]
        else:
            results = _adamw_two_sums(weights[n][0], core_sums[n], sibling_sums[n], mom1[n][0], mom2[n][0],
                                      name="adamw_" + n, tr=_row_tile(shp[1], 8, 136))
        grads[n], delta[n], new_m[n], new_v[n] = [a.reshape(shp) for a in results]
    small_all = [n for n in WEIGHT_ORDER if n not in BIG]

    def as_2d(a):
        return a.reshape(-1, a.shape[-1])

    results = _adamw_many(*[[as_2d(src[n]) for n in small_all] for src in (weights, grads, mom1, mom2)],
                          name="adamw_small")
    for n, dv, mv, vv in zip(small_all, *results):
        shp = weights[n].shape
        delta[n], new_m[n], new_v[n] = dv.reshape(shp), mv.reshape(shp), vv.reshape(shp)

    grad_out = [grads[n].reshape(weights[n].shape) for n in WEIGHT_ORDER]
    return (total_loss, dx[None], *grad_out, *[delta[n] for n in WEIGHT_ORDER], *[new_m[n] for n in WEIGHT_ORDER],
            *[new_v[n] for n in WEIGHT_ORDER])
```

```python
import functools
import math

import jax
import jax.numpy as jnp
from jax import lax
from jax.experimental import pallas as pl
from jax.experimental.pallas import tpu as pltpu

F32 = jnp.float32
BF16 = jnp.bfloat16
HI = lax.Precision.HIGHEST

D_MODEL = 1024
SSD_D_INNER = 2048
SSD_HEADS = 32
SSD_HEAD_DIM = 64
SSD_GROUPS = 4
SSD_HEADS_PER_GROUP = 8
SSD_STATE = 128
SSD_BC = 512
SSD_XBC = 3072
SSD_IN = 5152
SGU_WIDTH = 1024
SGU_GROUPS = 8
CHUNK = 128
IN_COLS = 9248
D_FF = 2816
NORM_EPS = 1e-6
LN_EPS = 1e-5
GROUP_COLS = SSD_HEADS_PER_GROUP * SSD_HEAD_DIM
LANES = 128

ADAM_LR = 0.001
ADAM_B1 = 0.9
ADAM_B2 = 0.999
ADAM_EPS = 1e-08
ADAM_WD = 0.01
ADAM_STEP = 10

N_CHIPS = 4
VMEM_LIMIT = 56 * 1024 * 1024

NT = (((1,), (1,)), ((), ()))
TN = (((0,), (0,)), ((), ()))
NN = (((1,), (0,)), ((), ()))


def _params(dims):
    return pltpu.CompilerParams(dimension_semantics=dims, vmem_limit_bytes=VMEM_LIMIT)


def _dot(a, b, dn=NN, precision=None):
    return lax.dot_general(a, b, dn, precision=precision, preferred_element_type=F32)


def _split3(x):
    hi = x.astype(BF16)
    rest = x - hi.astype(F32)
    mid = rest.astype(BF16)
    return hi, mid, (rest - mid.astype(F32)).astype(BF16)


def _dot_terms(terms, exact, dn=NN):
    out = None
    for t in terms:
        p = _dot(t, exact, dn)
        out = p if out is None else out + p
    return out


def _dot_exact_lhs(exact, terms):
    out = None
    for t in terms:
        p = _dot(exact, t)
        out = p if out is None else out + p
    return out


def _sigmoid(x):
    return 1.0 / (1.0 + jnp.exp(-x))


def _softplus(x):
    return jnp.maximum(x, 0.0) + jnp.log(1.0 + jnp.exp(-jnp.abs(x)))


def _matmul(pairs, *, trans_b=False, add=None, after=None, out_dtype=F32, tm=512, tn=512, name):
    def mat_shape(b):
        if isinstance(b, tuple) and b[1] == "cols":
            return (b[0].shape[1], b[0].shape[0] * b[0].shape[2])
        return b[0].shape[1:] if isinstance(b, tuple) else b.shape

    if isinstance(pairs[0][1], tuple) and pairs[0][1][1] == "cols":
        assert not trans_b and tn % LANES == 0 and pairs[0][1][0].shape[2] % tn == 0, name

    m = (pairs[0][0][0] if isinstance(pairs[0][0], tuple) else pairs[0][0]).shape[0]
    n = mat_shape(pairs[0][1])[0] if trans_b else mat_shape(pairs[0][1])[1]
    tm, tn = min(tm, m), min(tn, n)
    assert m % tm == 0 and n % tn == 0, (name, m, n, tm, tn)
    npairs = len(pairs)
    dn = NT if trans_b else NN

    def body(*refs):
        o_ref = refs[-1]
        acc = None
        for i in range(npairs):
            p = _dot(refs[2 * i][...].astype(BF16), refs[2 * i + 1][...].astype(BF16), dn)
            acc = p if acc is None else acc + p
        if add is not None:
            acc = acc + refs[2 * npairs][...]
        o_ref[...] = acc.astype(out_dtype)

    in_specs, args = [], []
    for a, b in pairs:
        bshape = mat_shape(b)
        k = bshape[1] if trans_b else bshape[0]
        assert bshape == ((n, k) if trans_b else (k, n)), (name, bshape)
        a, qa = a if isinstance(a, tuple) else (a, 0)
        assert a.shape[0] == m and a.shape[1] % k == 0, (name, a.shape, k)
        in_specs.append(pl.BlockSpec((tm, k), lambda i, j, qa=qa: (i, qa)))
        if isinstance(b, tuple) and b[1] == "cols":
            b = b[0]
            per = b.shape[2] // tn
            in_specs.append(pl.BlockSpec((None, k, tn), lambda i, j, per=per: (j // per, 0, j % per)))
        elif isinstance(b, tuple):
            b, qb = b
            if trans_b:
                in_specs.append(pl.BlockSpec((None, tn, k), lambda i, j, qb=qb: (qb, j, 0)))
            else:
                in_specs.append(pl.BlockSpec((None, k, tn), lambda i, j, qb=qb: (qb, 0, j)))
        elif trans_b:
            in_specs.append(pl.BlockSpec((tn, k), lambda i, j: (j, 0)))
        else:
            in_specs.append(pl.BlockSpec((k, tn), lambda i, j: (0, j)))
        args += [a, b]
    if add is not None:
        in_specs.append(pl.BlockSpec((tm, tn), lambda i, j: (i, j)))
        args.append(add)
    if after is not None:
        in_specs.append(pl.BlockSpec(memory_space=pl.ANY))
        args.append(after)
    return pl.pallas_call(
        body, name=name, grid=(m // tm, n // tn), in_specs=in_specs,
        out_specs=pl.BlockSpec((tm, tn), lambda i, j: (i, j)),
        out_shape=jax.ShapeDtypeStruct((m, n), out_dtype),
        compiler_params=_params(("parallel", "parallel")),
    )(*args)


def _matmul_tn(a, b, *, tk, tn, tm=1024, out_dtype=BF16, stack_out=False, after=None, name):
    m, k = a.shape
    n = b.shape[1]
    tm, tk, tn = min(tm, m), min(tk, k), min(tn, n)
    assert m % tm == 0 and k % tk == 0 and n % tn == 0, (name, m, k, n)
    nm = m // tm
    if stack_out:
        out_spec = pl.BlockSpec((None, tk, tn), lambda i, j, l: (j, i, 0))
        out_shape = jax.ShapeDtypeStruct((n // tn, k, tn), out_dtype)
    else:
        out_spec = pl.BlockSpec((tk, tn), lambda i, j, l: (i, j))
        out_shape = jax.ShapeDtypeStruct((k, n), out_dtype)

    def body(a_ref, b_ref, *rest):
        o_ref, acc = rest[-2:]
        mi = pl.program_id(2)

        @pl.when(mi == 0)
        def _():
            acc[...] = jnp.zeros_like(acc)

        acc[...] += _dot(a_ref[...].astype(BF16), b_ref[...].astype(BF16), TN)

        @pl.when(mi == nm - 1)
        def _():
            o_ref[...] = acc[...].astype(out_dtype)

    in_specs = [pl.BlockSpec((tm, tk), lambda i, j, l: (l, i)), pl.BlockSpec((tm, tn), lambda i, j, l: (l, j))]
    args = [a, b]
    if after is not None:
        in_specs.append(pl.BlockSpec(memory_space=pl.ANY))
        args.append(after)
    return pl.pallas_call(
        body, name=name, grid=(k // tk, n // tn, nm), in_specs=in_specs,
        out_specs=out_spec, out_shape=out_shape,
        scratch_shapes=[pltpu.VMEM((tk, tn), F32)],
        compiler_params=_params(("parallel", "parallel", "arbitrary")),
    )(*args)


def _rms_fwd(x, w, *, after=None, name, tm=512):
    s, d = x.shape
    tm = min(tm, s)
    extra = [] if after is None else [after]

    def body(x_ref, w_ref, *rest):
        o_ref = rest[-1]
        xv = x_ref[...]
        r = lax.rsqrt(jnp.mean(xv * xv, axis=-1, keepdims=True) + NORM_EPS)
        o_ref[...] = (xv * r * w_ref[...]).astype(BF16)

    return pl.pallas_call(
        body, name=name, grid=(s // tm,),
        in_specs=[pl.BlockSpec((tm, d), lambda i: (i, 0)), pl.BlockSpec((1, d), lambda i: (0, 0))]
        + [pl.BlockSpec(memory_space=pl.ANY)] * len(extra),
        out_specs=pl.BlockSpec((tm, d), lambda i: (i, 0)),
        out_shape=jax.ShapeDtypeStruct((s, d), BF16),
        compiler_params=_params(("parallel",)),
    )(x, w, *extra)


def _rms_bwd(x, w, dn, dres, *, name, tm=512):
    s, d = x.shape
    tm = min(tm, s)

    def body(x_ref, w_ref, dn_ref, dres_ref, dx_ref, dxb_ref, dw_ref):
        @pl.when(pl.program_id(0) == 0)
        def _():
            dw_ref[...] = jnp.zeros_like(dw_ref)

        xv = x_ref[...]
        r = lax.rsqrt(jnp.mean(xv * xv, axis=-1, keepdims=True) + NORM_EPS)
        xhat = xv * r
        dnv = dn_ref[...].astype(F32)
        dxhat = dnv * w_ref[...]
        dx = dres_ref[...] + r * (dxhat - xhat * jnp.mean(dxhat * xhat, axis=-1, keepdims=True))
        dx_ref[...] = dx
        dxb_ref[...] = dx.astype(BF16)
        dw_ref[...] += jnp.sum(dnv * xhat, axis=0, keepdims=True)

    tile = pl.BlockSpec((tm, d), lambda i: (i, 0))
    row = pl.BlockSpec((1, d), lambda i: (0, 0))
    return pl.pallas_call(
        body, name=name, grid=(s // tm,),
        in_specs=[tile, row, tile, tile], out_specs=[tile, tile, row],
        out_shape=[jax.ShapeDtypeStruct((s, d), F32), jax.ShapeDtypeStruct((s, d), BF16),
                   jax.ShapeDtypeStruct((1, d), F32)],
        compiler_params=_params(("arbitrary",)),
    )(x, w, dn, dres)


def _final_fwd_bwd(h2, wf, target, *, name, tm=512):
    s, d = h2.shape
    tm = min(tm, s)

    def body(h_ref, w_ref, t_ref, loss_ref, dh_ref, dhb_ref, dw_ref):
        @pl.when(pl.program_id(0) == 0)
        def _():
            dw_ref[...] = jnp.zeros_like(dw_ref)
            loss_ref[...] = jnp.zeros_like(loss_ref)

        hv = h_ref[...]
        r = lax.rsqrt(jnp.mean(hv * hv, axis=-1, keepdims=True) + NORM_EPS)
        xhat = hv * r
        err = xhat * w_ref[...] - t_ref[...]
        per_tok = jnp.mean(err * err, axis=-1, keepdims=True)
        loss_ref[...] += 0.5 * jnp.sum(per_tok, axis=0, keepdims=True)
        dy = err * (1.0 / d)
        dxhat = dy * w_ref[...]
        dh = r * (dxhat - xhat * jnp.mean(dxhat * xhat, axis=-1, keepdims=True))
        dh_ref[...] = dh
        dhb_ref[...] = dh.astype(BF16)
        dw_ref[...] += jnp.sum(dy * xhat, axis=0, keepdims=True)

    tile = pl.BlockSpec((tm, d), lambda i: (i, 0))
    row = pl.BlockSpec((1, d), lambda i: (0, 0))
    return pl.pallas_call(
        body, name=name, grid=(s // tm,),
        in_specs=[tile, row, tile],
        out_specs=[pl.BlockSpec((1, 1), lambda i: (0, 0)), tile, tile, row],
        out_shape=[jax.ShapeDtypeStruct((1, 1), F32), jax.ShapeDtypeStruct((s, d), F32),
                   jax.ShapeDtypeStruct((s, d), BF16), jax.ShapeDtypeStruct((1, d), F32)],
        compiler_params=_params(("arbitrary",)),
    )(h2, wf, target)


CONV_ROWS = 512
HALO = 8


def _rows_with_halo(ref, r0, rows, s, before, after):
    tile = 16 if ref.dtype == BF16 else HALO
    parts = []
    if before:
        prev = ref[pl.ds(pl.multiple_of(jnp.maximum(r0 - tile, 0), tile), tile), :].astype(F32)[tile - HALO:]
        parts.append(jnp.where(r0 > 0, prev, 0.0))
    parts.append(ref[pl.ds(r0, rows), :].astype(F32))
    if after:
        nxt = ref[pl.ds(pl.multiple_of(jnp.minimum(r0 + rows, s - tile), tile), tile), :].astype(F32)[:HALO]
        parts.append(jnp.where(r0 + rows < s, nxt, 0.0))
    return jnp.concatenate(parts, axis=0) if len(parts) > 1 else parts[0]


def _window(x_ref, r0, s, after):
    return _rows_with_halo(x_ref, r0, CONV_ROWS, s, True, after).astype(F32)


def _shifted(window, k, rows):
    if k == 0:
        return window[HALO:HALO + rows]
    return pltpu.roll(window, k, 0)[HALO:HALO + rows]


def _conv_taps(window, w_ref, kk, rows):
    acc = None
    for i in range(kk):
        term = w_ref[i:i + 1, :] * _shifted(window, kk - 1 - i, rows)
        acc = term if acc is None else acc + term
    return acc


def _row_loop(s, step):
    def body(r, carry):
        return step(pl.multiple_of(r * CONV_ROWS, CONV_ROWS), carry)
    return body


def _conv_bwd_rows(window, dpe, w_ref, kk):
    dp = dpe[:CONV_ROWS]
    dx = None
    dws = []
    for i in range(kk):
        k = kk - 1 - i
        dws.append(jnp.sum(dp * _shifted(window, k, CONV_ROWS), axis=0, keepdims=True))
        later = dp if k == 0 else pltpu.roll(dpe, dpe.shape[0] - k, 0)[:CONV_ROWS]
        term = w_ref[i:i + 1, :] * later
        dx = term if dx is None else dx + term
    return dx, dws, jnp.sum(dp, axis=0, keepdims=True)


def _conv_a_fwd(xraw, w, b, *, name, tc=128):
    s, c = xraw.shape
    kk = 4

    def body(x_ref, w_ref, b_ref, o_ref):
        def step(r0, carry):
            pre = _conv_taps(_window(x_ref, r0, s, False), w_ref, kk, CONV_ROWS) + b_ref[...]
            o_ref[pl.ds(r0, CONV_ROWS), :] = pre * _sigmoid(pre)
            return carry

        lax.fori_loop(0, s // CONV_ROWS, _row_loop(s, step), 0)

    col = pl.BlockSpec((s, tc), lambda j: (0, j))
    return pl.pallas_call(
        body, name=name, grid=(c // tc,),
        in_specs=[col, pl.BlockSpec((8, tc), lambda j: (0, j)), pl.BlockSpec((1, tc), lambda j: (0, j))],
        out_specs=col, out_shape=jax.ShapeDtypeStruct((s, c), F32),
        compiler_params=_params(("parallel",)),
    )(xraw, w, b)


def _conv_a_bwd(xraw, w, b, dy, *, name, tc=128):
    s, c = xraw.shape
    kk = 4

    def body(x_ref, w_ref, b_ref, dy_ref, dx_ref, dw_ref, db_ref):
        def step(r0, carry):
            window = _window(x_ref, r0, s, True)
            pre = _conv_taps(window, w_ref, kk, CONV_ROWS + HALO) + b_ref[...]
            sg = _sigmoid(pre)
            dpe = _rows_with_halo(dy_ref, r0, CONV_ROWS, s, False, True) * (sg * (1.0 + pre * (1.0 - sg)))
            dx, dws, db = _conv_bwd_rows(window, dpe, w_ref, kk)
            dx_ref[pl.ds(r0, CONV_ROWS), :] = dx.astype(BF16)
            return tuple(acc + new for acc, new in zip(carry, dws + [db]))

        zero = jnp.zeros((1, tc), F32)
        sums = lax.fori_loop(0, s // CONV_ROWS, _row_loop(s, step), (zero,) * (kk + 1))
        db_ref[...] = sums[kk]
        dw_ref[...] = jnp.concatenate(list(sums[:kk]) + [jnp.zeros((8 - kk, tc), F32)], axis=0)

    col = pl.BlockSpec((s, tc), lambda j: (0, j))
    w8 = pl.BlockSpec((8, tc), lambda j: (0, j))
    row = pl.BlockSpec((1, tc), lambda j: (0, j))
    return pl.pallas_call(
        body, name=name, grid=(c // tc,),
        in_specs=[col, w8, row, col], out_specs=[col, w8, row],
        out_shape=[jax.ShapeDtypeStruct((s, c), BF16), jax.ShapeDtypeStruct((8, c), F32),
                   jax.ShapeDtypeStruct((1, c), F32)],
        compiler_params=_params(("parallel",)),
    )(xraw, w, b, dy)


def _conv_f_fwd(up_raw, w, b, *, name, tc=128):
    s, c2 = up_raw.shape
    c = c2 // 2
    nb = c // tc
    kk = 3

    def body(xa_ref, xv_ref, wa_ref, wv_ref, ba_ref, bv_ref, o_ref):
        def step(r0, carry):
            a = _conv_taps(_window(xa_ref, r0, s, False), wa_ref, kk, CONV_ROWS) + ba_ref[...]
            v = _conv_taps(_window(xv_ref, r0, s, False), wv_ref, kk, CONV_ROWS) + bv_ref[...]
            o_ref[pl.ds(r0, CONV_ROWS), :] = (a * _sigmoid(a) * v).astype(BF16)
            return carry

        lax.fori_loop(0, s // CONV_ROWS, _row_loop(s, step), 0)

    col_a = pl.BlockSpec((s, tc), lambda j: (0, j))
    col_v = pl.BlockSpec((s, tc), lambda j: (0, j + nb))
    return pl.pallas_call(
        body, name=name, grid=(nb,),
        in_specs=[col_a, col_v, pl.BlockSpec((8, tc), lambda j: (0, j)), pl.BlockSpec((8, tc), lambda j: (0, j + nb)),
                  pl.BlockSpec((1, tc), lambda j: (0, j)), pl.BlockSpec((1, tc), lambda j: (0, j + nb))],
        out_specs=col_a, out_shape=jax.ShapeDtypeStruct((s, c), BF16),
        compiler_params=_params(("parallel",)),
    )(up_raw, up_raw, w, w, b, b)


def _conv_f_bwd(up_raw, w, b, dact, *, name, tc=128):
    s, c2 = up_raw.shape
    c = c2 // 2
    nb = c // tc
    kk = 3

    def body(xa_ref, xv_ref, wa_ref, wv_ref, ba_ref, bv_ref, d_ref,
             dxa_ref, dxv_ref, dwa_ref, dwv_ref, dba_ref, dbv_ref):
        def step(r0, carry):
            win_a = _window(xa_ref, r0, s, True)
            win_v = _window(xv_ref, r0, s, True)
            a = _conv_taps(win_a, wa_ref, kk, CONV_ROWS + HALO) + ba_ref[...]
            v = _conv_taps(win_v, wv_ref, kk, CONV_ROWS + HALO) + bv_ref[...]
            sg = _sigmoid(a)
            d = _rows_with_halo(d_ref, r0, CONV_ROWS, s, False, True)
            dxa, dwas, dba = _conv_bwd_rows(win_a, d * v * (sg * (1.0 + a * (1.0 - sg))), wa_ref, kk)
            dxv, dwvs, dbv = _conv_bwd_rows(win_v, d * (a * sg), wv_ref, kk)
            dxa_ref[pl.ds(r0, CONV_ROWS), :] = dxa.astype(BF16)
            dxv_ref[pl.ds(r0, CONV_ROWS), :] = dxv.astype(BF16)
            return tuple(acc + new for acc, new in zip(carry, dwas + [dba] + dwvs + [dbv]))

        zero = jnp.zeros((1, tc), F32)
        sums = lax.fori_loop(0, s // CONV_ROWS, _row_loop(s, step), (zero,) * (2 * kk + 2))
        pad = [jnp.zeros((8 - kk, tc), F32)]
        dwa_ref[...] = jnp.concatenate(list(sums[:kk]) + pad, axis=0)
        dba_ref[...] = sums[kk]
        dwv_ref[...] = jnp.concatenate(list(sums[kk + 1:2 * kk + 1]) + pad, axis=0)
        dbv_ref[...] = sums[2 * kk + 1]

    col_a = pl.BlockSpec((s, tc), lambda j: (0, j))
    col_v = pl.BlockSpec((s, tc), lambda j: (0, j + nb))
    w_a = pl.BlockSpec((8, tc), lambda j: (0, j))
    w_v = pl.BlockSpec((8, tc), lambda j: (0, j + nb))
    r_a = pl.BlockSpec((1, tc), lambda j: (0, j))
    r_v = pl.BlockSpec((1, tc), lambda j: (0, j + nb))
    outs = pl.pallas_call(
        body, name=name, grid=(nb,),
        in_specs=[col_a, col_v, w_a, w_v, r_a, r_v, col_a],
        out_specs=[col_a, col_a, w_a, w_a, r_a, r_a],
        out_shape=[jax.ShapeDtypeStruct((s, c), BF16), jax.ShapeDtypeStruct((s, c), BF16),
                   jax.ShapeDtypeStruct((8, c), F32), jax.ShapeDtypeStruct((8, c), F32),
                   jax.ShapeDtypeStruct((1, c), F32), jax.ShapeDtypeStruct((1, c), F32)],
        compiler_params=_params(("parallel",)),
    )(up_raw, up_raw, w, w, b, b, dact)
    return outs


def _tri_masks():
    row = lax.broadcasted_iota(jnp.int32, (CHUNK, CHUNK), 0)
    col = lax.broadcasted_iota(jnp.int32, (CHUNK, CHUNK), 1)
    return row >= col, row <= col


def _ssd_fwd(xbc, dt_raw, z, dt_bias, a_log, a_log_x, d_skip_x, norm_w, expand, *, name):
    s = xbc.shape[0]
    nc = s // CHUNK

    def body(xbc_ref, dtr_ref, z_ref, dtb_ref, alog_ref, alogx_ref, dskx_ref, nw_ref, e_ref,
             y_ref, ya_ref, st_ref, state):
        @pl.when(pl.program_id(0) == 0)
        def _():
            state[...] = jnp.zeros_like(state)

        st_ref[0] = state[...]
        lower, _ = _tri_masks()
        dt = _softplus(dtr_ref[...] + dtb_ref[...])
        adt = dt * (-jnp.exp(alog_ref[...]))
        acum = _dot_exact_lhs(lower.astype(BF16), _split3(adt))
        acum_t = acum.T
        dt_terms, acum_terms = _split3(dt), _split3(acum)
        for g in range(SSD_GROUPS):
            sl = slice(GROUP_COLS * g, GROUP_COLS * (g + 1))
            dt_x = _dot_terms(dt_terms, e_ref[:, sl])
            acum_x = _dot_terms(acum_terms, e_ref[:, sl])
            tot_x = jnp.sum(dt_x * (-jnp.exp(alogx_ref[:, sl])), axis=0, keepdims=True)
            xs = xbc_ref[:, sl]
            xdt = xs * dt_x
            xdt_b = xdt.astype(BF16)
            bg = xbc_ref[:, SSD_D_INNER + SSD_STATE * g:SSD_D_INNER + SSD_STATE * (g + 1)].astype(BF16)
            cg = xbc_ref[:, SSD_D_INNER + SSD_BC + SSD_STATE * g:SSD_D_INNER + SSD_BC + SSD_STATE * (g + 1)].astype(BF16)
            cb = _dot(cg, bg, NT)
            st_g = state[:, sl]
            y_off = _dot(cg, st_g.astype(BF16)) * jnp.exp(acum_x)
            parts = []
            for r in range(SSD_HEADS_PER_GROUP):
                h = SSD_HEADS_PER_GROUP * g + r
                dec = jnp.exp(jnp.where(lower, acum[:, h:h + 1] - acum_t[h:h + 1, :], -jnp.inf))
                parts.append(_dot((cb * dec).astype(BF16), xdt_b[:, SSD_HEAD_DIM * r:SSD_HEAD_DIM * (r + 1)]))
            y_ref[:, sl] = jnp.concatenate(parts, axis=1) + y_off + dskx_ref[:, sl] * xs
            wgt = (xdt * jnp.exp(tot_x - acum_x)).astype(BF16)
            state[:, sl] = st_g * jnp.exp(tot_x) + _dot(bg, wgt, TN)
        zv = z_ref[...].astype(F32)
        q = y_ref[...] * (zv * _sigmoid(zv))
        r = lax.rsqrt(jnp.mean(q * q, axis=-1, keepdims=True) + NORM_EPS)
        ya_ref[...] = (q * r * nw_ref[...]).astype(BF16)

    def chunk(w):
        return pl.BlockSpec((CHUNK, w), lambda c: (c, 0))

    def const(shape):
        return pl.BlockSpec(shape, lambda c: (0,) * len(shape))

    return pl.pallas_call(
        body, name=name, grid=(nc,),
        in_specs=[chunk(SSD_XBC), chunk(LANES), chunk(SSD_D_INNER), const((1, LANES)), const((1, LANES)),
                  const((1, SSD_D_INNER)), const((1, SSD_D_INNER)), const((1, SSD_D_INNER)),
                  const((LANES, SSD_D_INNER))],
        out_specs=[chunk(SSD_D_INNER), chunk(SSD_D_INNER),
                   pl.BlockSpec((1, SSD_STATE, SSD_D_INNER), lambda c: (c, 0, 0))],
        out_shape=[jax.ShapeDtypeStruct((s, SSD_D_INNER), F32), jax.ShapeDtypeStruct((s, SSD_D_INNER), BF16),
                   jax.ShapeDtypeStruct((nc, SSD_STATE, SSD_D_INNER), F32)],
        scratch_shapes=[pltpu.VMEM((SSD_STATE, SSD_D_INNER), F32)],
        compiler_params=_params(("arbitrary",)),
    )(xbc, dt_raw, z, dt_bias, a_log, a_log_x, d_skip_x, norm_w, expand)


def _ssd_bwd(dya, y, z, xbc, dt_raw, states, dt_bias, a_log, a_log_x, d_skip_x, norm_w, expand, expand_t, *, name):
    s = xbc.shape[0]
    nc = s // CHUNK

    def body(dya_ref, y_ref, z_ref, xbc_ref, dtr_ref, stp_ref, dtb_ref, alog_ref, alogx_ref, dskx_ref, nw_ref,
             e_ref, et_ref, dz_ref, dxbc_ref, ddt_ref, dnw_ref, ddsk_ref, dalog_ref, ddtb_ref,
             dstate, dy_sc, dskcol):
        i = pl.program_id(0)

        @pl.when(i == 0)
        def _():
            dstate[...] = jnp.zeros_like(dstate)
            dskcol[...] = jnp.zeros_like(dskcol)
            dnw_ref[...] = jnp.zeros_like(dnw_ref)
            dalog_ref[...] = jnp.zeros_like(dalog_ref)
            ddtb_ref[...] = jnp.zeros_like(ddtb_ref)
            ddsk_ref[...] = jnp.zeros_like(ddsk_ref)

        lower, upper = _tri_masks()
        rows = lax.broadcasted_iota(jnp.int32, (CHUNK, LANES), 0)
        pre = dtr_ref[...] + dtb_ref[...]
        dt = _softplus(pre)
        a = -jnp.exp(alog_ref[...])
        acum = _dot_exact_lhs(lower.astype(BF16), _split3(dt * a))
        acum_t = acum.T
        dt_terms, acum_terms = _split3(dt), _split3(acum)

        yv = y_ref[...]
        zv = z_ref[...].astype(F32)
        sz = _sigmoid(zv)
        silu_z = zv * sz
        q = yv * silu_z
        r = lax.rsqrt(jnp.mean(q * q, axis=-1, keepdims=True) + NORM_EPS)
        qhat = q * r
        dyav = dya_ref[...]
        dqhat = dyav * nw_ref[...]
        dnw_ref[...] += jnp.sum(dyav * qhat, axis=0, keepdims=True)
        dq = r * (dqhat - qhat * jnp.mean(dqhat * qhat, axis=-1, keepdims=True))
        dy_sc[...] = dq * silu_z
        dz_ref[...] = (dq * yv * (sz * (1.0 + zv * (1.0 - sz)))).astype(BF16)

        da_cum = jnp.zeros((CHUNK, LANES), F32)
        ddt = jnp.zeros((CHUNK, LANES), F32)
        for g in range(SSD_GROUPS):
            sl = slice(GROUP_COLS * g, GROUP_COLS * (g + 1))
            et_g = et_ref[sl, :]
            dt_x = _dot_terms(dt_terms, e_ref[:, sl])
            acum_x = _dot_terms(acum_terms, e_ref[:, sl])
            tot_x = jnp.sum(dt_x * (-jnp.exp(alogx_ref[:, sl])), axis=0, keepdims=True)
            e_tot = jnp.exp(tot_x)
            dec_s = jnp.exp(tot_x - acum_x)
            xs = xbc_ref[:, sl]
            xdt = xs * dt_x
            xdt_b = xdt.astype(BF16)
            dy = dy_sc[:, sl]
            dy_b = dy.astype(BF16)
            dskx = dskx_ref[:, sl]
            y_ssd = y_ref[:, sl] - dskx * xs
            dskcol[:, sl] += jnp.sum(dy * xs, axis=0, keepdims=True)
            bg = xbc_ref[:, SSD_D_INNER + SSD_STATE * g:SSD_D_INNER + SSD_STATE * (g + 1)].astype(BF16)
            cg = xbc_ref[:, SSD_D_INNER + SSD_BC + SSD_STATE * g:SSD_D_INNER + SSD_BC + SSD_STATE * (g + 1)].astype(BF16)
            cb_t = _dot(bg, cg, NT)
            sp = stp_ref[0, :, sl]
            ds_g = dstate[:, sl]
            ds_b = ds_g.astype(BF16)
            dye_b = (dy * jnp.exp(acum_x)).astype(BF16)
            dc = _dot(dye_b, sp.astype(BF16), NT)
            dxdt_state = dec_s * _dot(bg, ds_b)
            db = _dot((xdt * dec_s).astype(BF16), ds_b, NT)
            dcb_t = jnp.zeros((CHUNK, CHUNK), F32)
            parts = []
            for rr in range(SSD_HEADS_PER_GROUP):
                h = SSD_HEADS_PER_GROUP * g + rr
                hs = slice(SSD_HEAD_DIM * rr, SSD_HEAD_DIM * (rr + 1))
                dec_t = jnp.exp(jnp.where(upper, acum_t[h:h + 1, :] - acum[:, h:h + 1], -jnp.inf))
                parts.append(_dot((cb_t * dec_t).astype(BF16), dy_b[:, hs]))
                dcb_t = dcb_t + _dot(xdt_b[:, hs], dy_b[:, hs], NT) * dec_t
            dxdt = jnp.concatenate(parts, axis=1) + dxdt_state
            dcb_tb = dcb_t.astype(BF16)
            dc = dc + _dot(dcb_tb, bg, TN)
            db = db + _dot(dcb_tb, cg)
            tot_col = jnp.sum(ds_g * sp, axis=0, keepdims=True) * e_tot + jnp.sum(dxdt_state * xdt, axis=0, keepdims=True)
            d_tot = _dot_terms(_split3(jnp.broadcast_to(tot_col, (8, GROUP_COLS))), et_g)
            d_tot = jnp.max(d_tot, axis=0, keepdims=True)
            pair_sums = dy_b.astype(F32) * y_ssd - xdt_b.astype(F32) * dxdt
            da_cum = da_cum + _dot_terms(_split3(pair_sums), et_g) + jnp.where(rows == CHUNK - 1, d_tot, 0.0)
            ddt = ddt + _dot_terms(_split3(dxdt * xs), et_g)
            dxbc_ref[:, sl] = dy * dskx + dxdt * dt_x
            dxbc_ref[:, SSD_D_INNER + SSD_STATE * g:SSD_D_INNER + SSD_STATE * (g + 1)] = db
            dxbc_ref[:, SSD_D_INNER + SSD_BC + SSD_STATE * g:SSD_D_INNER + SSD_BC + SSD_STATE * (g + 1)] = dc
            dstate[:, sl] = e_tot * ds_g + _dot(cg, dye_b, TN)

        dadt = _dot_exact_lhs(upper.astype(BF16), _split3(da_cum))
        ddt = ddt + dadt * a
        dalog_ref[...] += jnp.sum(dadt * dt, axis=0, keepdims=True)
        dpre = ddt * _sigmoid(pre)
        ddtb_ref[...] += jnp.sum(dpre, axis=0, keepdims=True)
        ddt_ref[...] = dpre.astype(BF16)

        @pl.when(i == nc - 1)
        def _():
            dalog_ref[...] = dalog_ref[...] * a
            dsk = _dot_terms(_split3(jnp.broadcast_to(dskcol[...], (8, SSD_D_INNER))), et_ref[...])
            ddsk_ref[...] = jnp.max(dsk, axis=0, keepdims=True)

    def chunk(w):
        return pl.BlockSpec((CHUNK, w), lambda i: (nc - 1 - i, 0))

    def const(shape):
        return pl.BlockSpec(shape, lambda i: (0,) * len(shape))

    return pl.pallas_call(
        body, name=name, grid=(nc,),
        in_specs=[chunk(SSD_D_INNER), chunk(SSD_D_INNER), chunk(SSD_D_INNER), chunk(SSD_XBC), chunk(LANES),
                  pl.BlockSpec((1, SSD_STATE, SSD_D_INNER), lambda i: (nc - 1 - i, 0, 0)),
                  const((1, LANES)), const((1, LANES)), const((1, SSD_D_INNER)), const((1, SSD_D_INNER)),
                  const((1, SSD_D_INNER)), const((LANES, SSD_D_INNER)), const((SSD_D_INNER, LANES))],
        out_specs=[chunk(SSD_D_INNER), chunk(SSD_XBC), chunk(LANES), const((1, SSD_D_INNER)), const((1, LANES)),
                   const((1, LANES)), const((1, LANES))],
        out_shape=[jax.ShapeDtypeStruct((s, SSD_D_INNER), BF16), jax.ShapeDtypeStruct((s, SSD_XBC), F32),
                   jax.ShapeDtypeStruct((s, LANES), BF16), jax.ShapeDtypeStruct((1, SSD_D_INNER), F32),
                   jax.ShapeDtypeStruct((1, LANES), F32), jax.ShapeDtypeStruct((1, LANES), F32),
                   jax.ShapeDtypeStruct((1, LANES), F32)],
        scratch_shapes=[pltpu.VMEM((SSD_STATE, SSD_D_INNER), F32), pltpu.VMEM((CHUNK, SSD_D_INNER), F32),
                        pltpu.VMEM((1, SSD_D_INNER), F32)],
        compiler_params=_params(("arbitrary",)),
    )(dya, y, z, xbc, dt_raw, states, dt_bias, a_log, a_log_x, d_skip_x, norm_w, expand, expand_t)


GELU_K = math.sqrt(2.0 / math.pi)
GELU_C = 0.044715


def _gelu(x):
    return 0.5 * x * (1.0 + jnp.tanh(GELU_K * (x + GELU_C * x * x * x)))


def _gelu_grad(x):
    t = jnp.tanh(GELU_K * (x + GELU_C * x * x * x))
    return 0.5 * (1.0 + t) + 0.5 * x * (1.0 - t * t) * (GELU_K * (1.0 + 3.0 * GELU_C * x * x))


def _sgu_pre(uv_ref, uvb_ref, lnw_ref, lnb_ref):
    uv = uv_ref[...].astype(F32) + uvb_ref[...]
    guv = _gelu(uv)
    u = guv[:, :SGU_WIDTH]
    v = guv[:, SGU_WIDTH:]
    mu = jnp.mean(v, axis=-1, keepdims=True)
    vc = v - mu
    rstd = lax.rsqrt(jnp.mean(vc * vc, axis=-1, keepdims=True) + LN_EPS)
    vhat = vc * rstd
    vn = vhat * lnw_ref[...] + lnb_ref[...]
    return uv, u, vhat, rstd, vn


def _sgu_fwd(uv_raw, uv_b, ln_w, ln_b, w_sp, b_sp_t, *, name):
    s = uv_raw.shape[0]
    nc = s // CHUNK

    def body(uv_ref, uvb_ref, lnw_ref, lnb_ref, w_ref, bt_ref, o_ref):
        lower, _ = _tri_masks()
        _, u, _, _, vn = _sgu_pre(uv_ref, uvb_ref, lnw_ref, lnb_ref)
        vn_b = vn.astype(BF16)
        bt = bt_ref[...]
        for g in range(SGU_GROUPS):
            gs = slice(LANES * g, LANES * (g + 1))
            wc = jnp.where(lower, w_ref[g], 0.0).astype(BF16)
            mixed = _dot(wc, vn_b[:, gs]) + bt[:, g:g + 1]
            o_ref[:, gs] = (u[:, gs] * mixed).astype(BF16)

    def const(shape):
        return pl.BlockSpec(shape, lambda c: (0,) * len(shape))

    return pl.pallas_call(
        body, name=name, grid=(nc,),
        in_specs=[pl.BlockSpec((CHUNK, 2 * SGU_WIDTH), lambda c: (c, 0)), const((1, 2 * SGU_WIDTH)),
                  const((1, SGU_WIDTH)), const((1, SGU_WIDTH)), const((SGU_GROUPS, CHUNK, CHUNK)),
                  const((CHUNK, LANES))],
        out_specs=pl.BlockSpec((CHUNK, SGU_WIDTH), lambda c: (c, 0)),
        out_shape=jax.ShapeDtypeStruct((s, SGU_WIDTH), BF16),
        compiler_params=_params(("parallel",)),
    )(uv_raw, uv_b, ln_w, ln_b, w_sp, b_sp_t)


def _sgu_bwd(uv_raw, dyb, uv_b, ln_w, ln_b, w_sp, b_sp_t, group_sum, *, name):
    s = uv_raw.shape[0]
    nc = s // CHUNK

    def body(uv_ref, dy_ref, uvb_ref, lnw_ref, lnb_ref, w_ref, bt_ref, gsum_ref,
             duv_ref, dw_ref, dbt_ref, dlnw_ref, dlnb_ref, duvb_ref):
        @pl.when(pl.program_id(0) == 0)
        def _():
            dw_ref[...] = jnp.zeros_like(dw_ref)
            dbt_ref[...] = jnp.zeros_like(dbt_ref)
            dlnw_ref[...] = jnp.zeros_like(dlnw_ref)
            dlnb_ref[...] = jnp.zeros_like(dlnb_ref)
            duvb_ref[...] = jnp.zeros_like(duvb_ref)

        lower, _ = _tri_masks()
        uv, u, vhat, rstd, vn = _sgu_pre(uv_ref, uvb_ref, lnw_ref, lnb_ref)
        vn_b = vn.astype(BF16)
        bt = bt_ref[...]
        dy = dy_ref[...].astype(F32)
        du_parts, dvn_parts, dmix_parts = [], [], []
        for g in range(SGU_GROUPS):
            gs = slice(LANES * g, LANES * (g + 1))
            wc = jnp.where(lower, w_ref[g], 0.0).astype(BF16)
            mixed = _dot(wc, vn_b[:, gs]) + bt[:, g:g + 1]
            du_parts.append(dy[:, gs] * mixed)
            dmix = dy[:, gs] * u[:, gs]
            dmix_b = dmix.astype(BF16)
            dmix_parts.append(dmix)
            dw_ref[g] += jnp.where(lower, _dot(dmix_b, vn_b[:, gs], NT), 0.0)
            dvn_parts.append(_dot(wc, dmix_b, TN))
        dmixed = jnp.concatenate(dmix_parts, axis=1)
        dbt_ref[...] += _dot_terms(_split3(dmixed), gsum_ref[...])
        dvn = jnp.concatenate(dvn_parts, axis=1)
        dlnw_ref[...] += jnp.sum(dvn * vhat, axis=0, keepdims=True)
        dlnb_ref[...] += jnp.sum(dvn, axis=0, keepdims=True)
        dvhat = dvn * lnw_ref[...]
        dv = rstd * (dvhat - jnp.mean(dvhat, axis=-1, keepdims=True)
                     - vhat * jnp.mean(dvhat * vhat, axis=-1, keepdims=True))
        dguv = jnp.concatenate(du_parts + [dv], axis=1)
        duv = dguv * _gelu_grad(uv)
        duvb_ref[...] += jnp.sum(duv, axis=0, keepdims=True)
        duv_ref[...] = duv.astype(BF16)

    def const(shape):
        return pl.BlockSpec(shape, lambda c: (0,) * len(shape))

    return pl.pallas_call(
        body, name=name, grid=(nc,),
        in_specs=[pl.BlockSpec((CHUNK, 2 * SGU_WIDTH), lambda c: (c, 0)),
                  pl.BlockSpec((CHUNK, SGU_WIDTH), lambda c: (c, 0)), const((1, 2 * SGU_WIDTH)),
                  const((1, SGU_WIDTH)), const((1, SGU_WIDTH)), const((SGU_GROUPS, CHUNK, CHUNK)),
                  const((CHUNK, LANES)), const((SGU_WIDTH, LANES))],
        out_specs=[pl.BlockSpec((CHUNK, 2 * SGU_WIDTH), lambda c: (c, 0)), const((SGU_GROUPS, CHUNK, CHUNK)),
                   const((CHUNK, LANES)), const((1, SGU_WIDTH)), const((1, SGU_WIDTH)), const((1, 2 * SGU_WIDTH))],
        out_shape=[jax.ShapeDtypeStruct((s, 2 * SGU_WIDTH), BF16),
                   jax.ShapeDtypeStruct((SGU_GROUPS, CHUNK, CHUNK), F32), jax.ShapeDtypeStruct((CHUNK, LANES), F32),
                   jax.ShapeDtypeStruct((1, SGU_WIDTH), F32), jax.ShapeDtypeStruct((1, SGU_WIDTH), F32),
                   jax.ShapeDtypeStruct((1, 2 * SGU_WIDTH), F32)],
        compiler_params=_params(("arbitrary",)),
    )(uv_raw, dyb, uv_b, ln_w, ln_b, w_sp, b_sp_t, group_sum)


def _gate_fwd(gates_raw, b_gate, p_a, p_b, *, name, tm=512):
    s = p_a.shape[0]
    tm = min(tm, s)

    def body(ga_ref, gb_ref, ba_ref, bb_ref, pa_ref, pb_ref, o_ref):
        ga = _sigmoid(ga_ref[...].astype(F32) + ba_ref[...])
        gb = _sigmoid(gb_ref[...].astype(F32) + bb_ref[...])
        o_ref[...] = (ga * pa_ref[...].astype(F32) + gb * pb_ref[...].astype(F32)).astype(BF16)

    t_a = pl.BlockSpec((tm, D_MODEL), lambda i: (i, 0))
    t_b = pl.BlockSpec((tm, D_MODEL), lambda i: (i, 1))
    r_a = pl.BlockSpec((1, D_MODEL), lambda i: (0, 0))
    r_b = pl.BlockSpec((1, D_MODEL), lambda i: (0, 1))
    return pl.pallas_call(
        body, name=name, grid=(s // tm,),
        in_specs=[t_a, t_b, r_a, r_b, t_a, t_a], out_specs=t_a,
        out_shape=jax.ShapeDtypeStruct((s, D_MODEL), BF16),
        compiler_params=_params(("parallel",)),
    )(gates_raw, gates_raw, b_gate, b_gate, p_a, p_b)


def _gate_bwd(gates_raw, b_gate, p_a, p_b, dm, *, name, tm=512):
    s = p_a.shape[0]
    tm = min(tm, s)

    def body(ga_ref, gb_ref, ba_ref, bb_ref, pa_ref, pb_ref, dm_ref, dpa_ref, dpb_ref, dga_ref, dgb_ref,
             dba_ref, dbb_ref):
        @pl.when(pl.program_id(0) == 0)
        def _():
            dba_ref[...] = jnp.zeros_like(dba_ref)
            dbb_ref[...] = jnp.zeros_like(dbb_ref)

        d = dm_ref[...].astype(F32)
        for g_ref, b_ref, p_ref, dp_ref, dg_ref, db_ref in ((ga_ref, ba_ref, pa_ref, dpa_ref, dga_ref, dba_ref),
                                                            (gb_ref, bb_ref, pb_ref, dpb_ref, dgb_ref, dbb_ref)):
            sg = _sigmoid(g_ref[...].astype(F32) + b_ref[...])
            dp_ref[...] = (d * sg).astype(BF16)
            dg = d * p_ref[...].astype(F32) * (sg * (1.0 - sg))
            dg_ref[...] = dg.astype(BF16)
            db_ref[...] += jnp.sum(dg, axis=0, keepdims=True)

    t_a = pl.BlockSpec((tm, D_MODEL), lambda i: (i, 0))
    t_b = pl.BlockSpec((tm, D_MODEL), lambda i: (i, 1))
    r_a = pl.BlockSpec((1, D_MODEL), lambda i: (0, 0))
    r_b = pl.BlockSpec((1, D_MODEL), lambda i: (0, 1))
    big = jax.ShapeDtypeStruct((s, D_MODEL), BF16)
    row = jax.ShapeDtypeStruct((1, D_MODEL), F32)
    return pl.pallas_call(
        body, name=name, grid=(s // tm,),
        in_specs=[t_a, t_b, r_a, r_b, t_a, t_a, t_a], out_specs=[t_a, t_a, t_a, t_a, r_a, r_a],
        out_shape=[big, big, big, big, row, row],
        compiler_params=_params(("arbitrary",)),
    )(gates_raw, gates_raw, b_gate, b_gate, p_a, p_b, dm)


def _adamw_update(w_ref, g_ref, m_ref, v_ref, d_ref, mo_ref, vo_ref):
    gv = g_ref[...]
    mn = ADAM_B1 * m_ref[...] + (1.0 - ADAM_B1) * gv
    vn = ADAM_B2 * v_ref[...] + (1.0 - ADAM_B2) * (gv * gv)
    m_hat = mn / (1.0 - ADAM_B1 ** ADAM_STEP)
    v_hat = vn / (1.0 - ADAM_B2 ** ADAM_STEP)
    d_ref[...] = -ADAM_LR * (m_hat / (jnp.sqrt(v_hat) + ADAM_EPS) + ADAM_WD * w_ref[...])
    mo_ref[...] = mn
    vo_ref[...] = vn


def _adamw_many(ws, gs, ms, vs, *, name):
    n = len(ws)

    def body(*refs):
        for i in range(n):
            _adamw_update(*[refs[k * n + i] for k in range(7)])

    whole = pl.BlockSpec(memory_space=pltpu.VMEM)
    sds = [jax.ShapeDtypeStruct(w.shape, F32) for w in ws]
    outs = pl.pallas_call(
        body, name=name, in_specs=[whole] * (4 * n), out_specs=[whole] * (3 * n), out_shape=sds * 3,
        compiler_params=pltpu.CompilerParams(vmem_limit_bytes=VMEM_LIMIT),
    )(*ws, *gs, *ms, *vs)
    return outs[:n], outs[n:2 * n], outs[2 * n:]


def _adamw(w, g, m, v, *, name, tr=128):
    r, c = w.shape
    tr = min(tr, r)
    assert r % tr == 0, (name, r, tr)
    body = functools.partial(_adamw_update)

    blk = pl.BlockSpec((tr, c), lambda i: (i, 0))
    sds = jax.ShapeDtypeStruct((r, c), F32)
    return pl.pallas_call(
        body, name=name, grid=(r // tr,), in_specs=[blk] * 4, out_specs=[blk] * 3, out_shape=[sds] * 3,
        compiler_params=_params(("parallel",)),
    )(w, g, m, v)


def _adamw_two_sums(w, g_a, g_b, m, v, *, name, tr=128):
    r, c = w.shape
    tr = min(tr, r)
    assert r % tr == 0, (name, r, tr)

    def body(w_ref, ga_ref, gb_ref, m_ref, v_ref, g_ref, d_ref, mo_ref, vo_ref):
        g_ref[...] = ga_ref[...] + gb_ref[...]
        _adamw_update(w_ref, g_ref, m_ref, v_ref, d_ref, mo_ref, vo_ref)

    blk = pl.BlockSpec((tr, c), lambda i: (i, 0))
    sds = jax.ShapeDtypeStruct((r, c), F32)
    return pl.pallas_call(
        body, name=name, grid=(r // tr,), in_specs=[blk] * 5, out_specs=[blk] * 4, out_shape=[sds] * 4,
        compiler_params=_params(("parallel",)),
    )(w, g_a, g_b, m, v)


def _tile(n, pref):
    if n <= pref:
        return n
    best = LANES
    for t in range(LANES, pref + 1, LANES):
        if n % t == 0:
            best = t
    return best


MATMUL_BLOCK_BYTES = 20 * 1024 * 1024


def _mm(pairs, name, **kw):
    trans_b = kw.get("trans_b", False)
    m = (pairs[0][0][0] if isinstance(pairs[0][0], tuple) else pairs[0][0]).shape[0]
    ktot, n = 0, None
    for _, b in pairs:
        shape = b[0].shape[1:] if isinstance(b, tuple) else b.shape
        ktot += shape[1] if trans_b else shape[0]
        n = shape[0] if trans_b else shape[1]
    out_bytes = 4 * (2 if kw.get("add") is not None else 1)
    best = None
    for tm in (256, 512, 1024, 2048):
        for tn in range(LANES, min(n, 1536) + 1, LANES):
            if m % min(tm, m) or n % tn:
                continue
            fits = 2 * ktot * (min(tm, m) + tn) + out_bytes * min(tm, m) * tn <= MATMUL_BLOCK_BYTES
            if fits and (best is None or min(tm, m) * tn >= best[0] * best[1]):
                best = (min(tm, m), tn)
    return _matmul(pairs, tm=best[0], tn=best[1], name=name, **kw)


def _wgrad(a, b, name, **kw):
    return _matmul_tn(a, b, tk=_tile(a.shape[1], 1408), tn=kw.pop("tn", _tile(b.shape[1], 1024)), tm=2048,
                      name=name, **kw)


def _local_step(x, target, get_weight, small, emit_grad):
    heads = jnp.arange(SSD_D_INNER) // SSD_HEAD_DIM
    expand = (jnp.arange(LANES)[:, None] == heads[None, :]).astype(BF16)
    expand_t = expand.T
    group_sum = (jnp.arange(SGU_WIDTH)[:, None] // LANES == jnp.arange(LANES)[None, :]).astype(BF16)
    pad_h = LANES - SSD_HEADS
    dt_bias = jnp.pad(small["dt_bias"], ((0, 0), (0, pad_h)))
    a_log = jnp.pad(small["a_log"], ((0, 0), (0, pad_h)))
    a_log_x = jnp.repeat(small["a_log"], SSD_HEAD_DIM, axis=1)
    d_skip_x = jnp.repeat(small["d_skip"], SSD_HEAD_DIM, axis=1)
    b_sp_t = jnp.pad(small["b_spatial"][0].T, ((0, 0), (0, LANES - SGU_GROUPS)))
    w_sp = small["w_spatial"][0]
    conv_a_w = jnp.pad(small["conv_a_w"], ((0, 4), (0, 0)))
    conv_f_w = jnp.pad(small["conv_f_w"], ((0, 5), (0, 0)))
    final_w = small["final_norm_w"].reshape(1, D_MODEL)

    n1 = _rms_fwd(x, small["norm1_w"], after=small.get("gathers_started"), name="rms1_fwd")
    wts = dict(get_weight("w_in", n1))
    z = _mm([(n1, wts["in_z"])], "in_z")
    xbc_raw = _mm([(n1, wts["in_xbc"])], "in_xbc")
    dt_raw = _mm([(n1, wts["in_dt"])], "in_dt")
    uv_raw = _mm([(n1, wts["in_uv"])], "in_uv", out_dtype=BF16)
    gates_raw = _mm([(n1, wts["in_gate"])], "in_gate", out_dtype=BF16)
    xbc = _conv_a_fwd(xbc_raw, conv_a_w, small["conv_a_b"], name="conv_a_fwd")
    y, y_a, states = _ssd_fwd(xbc, dt_raw, z, dt_bias, a_log, a_log_x, d_skip_x, small["ssd_norm_w"], expand,
                              name="ssd_fwd")
    y_b = _sgu_fwd(uv_raw, small["uv_b"], small["v_ln_w"], small["v_ln_b"], w_sp, b_sp_t, name="sgu_fwd")
    wts.update(get_weight("w_branch", y_b))
    p_a = _mm([(y_a, wts["branch_a"])], "branch_a", out_dtype=BF16)
    p_b = _mm([(y_b, wts["branch_b"])], "branch_b", out_dtype=BF16)
    mix = _gate_fwd(gates_raw, small["b_gate"], p_a, p_b, name="gate_fwd")
    wts.update(get_weight("w_out", mix))
    h1 = _mm([(mix, wts["out"])], "out_proj", add=x)
    n2 = _rms_fwd(h1, small["norm2_w"], name="rms2_fwd")
    wts.update(get_weight("w_up", n2))
    up_w = wts["up"]
    up_cols = up_w.shape[2]
    up_raw = _matmul([(n2, (up_w, "cols"))], tm=2048, tn=up_cols, out_dtype=BF16, name="up_proj")
    act = _conv_f_fwd(up_raw, conv_f_w, small["conv_f_b"], name="conv_f_fwd")
    wts.update(get_weight("w_down", act))
    h2 = _mm([(act, wts["down"])], "down_proj", add=h1)
    loss, dh2, dh2_b, d_final = _final_fwd_bwd(h2, final_w, target, name="final_norm_loss")

    dact = _mm([(dh2_b, wts["down"])], "down_dgrad", trans_b=True)
    started = emit_grad("w_down", _wgrad(act, dh2_b, "down_wgrad"))
    dup_a, dup_v, dwf_a, dwf_v, dbf_a, dbf_v = _conv_f_bwd(up_raw, conv_f_w, small["conv_f_b"], dact,
                                                           name="conv_f_bwd")
    dn2 = _mm([((dup_a, 0), (up_w, 0)), ((dup_a, 1), (up_w, 1)), ((dup_v, 0), (up_w, 2)), ((dup_v, 1), (up_w, 3))],
              "up_dgrad", trans_b=True, after=started, out_dtype=BF16)
    started = emit_grad("w_up", jnp.concatenate([_wgrad(n2, dup_a, "up_wgrad_a", tn=up_cols, stack_out=True),
                                                 _wgrad(n2, dup_v, "up_wgrad_v", tn=up_cols, stack_out=True)], axis=0))
    dh1, dh1_b, d_norm2 = _rms_bwd(h1, small["norm2_w"], dn2, dh2, name="rms2_bwd")
    dmix = _mm([(dh1_b, wts["out"])], "out_dgrad", trans_b=True, after=started, out_dtype=BF16)
    started = emit_grad("w_out", _wgrad(mix, dh1_b, "out_wgrad"))
    dp_a, dp_b, dg_a, dg_b, dbg_a, dbg_b = _gate_bwd(gates_raw, small["b_gate"], p_a, p_b, dmix, name="gate_bwd")
    dya = _mm([(dp_a, wts["branch_a"])], "branch_a_dgrad", trans_b=True, after=started)
    dyb = _mm([(dp_b, wts["branch_b"])], "branch_b_dgrad", trans_b=True, out_dtype=BF16)
    started_branch = emit_grad("w_branch", jnp.concatenate([_wgrad(y_a, dp_a, "branch_a_wgrad"),
                                                            _wgrad(y_b, dp_b, "branch_b_wgrad")], axis=0))
    duv, d_wsp, d_bsp_t, d_lnw, d_lnb, d_uvb = _sgu_bwd(uv_raw, dyb, small["uv_b"], small["v_ln_w"],
                                                        small["v_ln_b"], w_sp, b_sp_t, group_sum, name="sgu_bwd")
    dz, dxbc, ddt, d_ssd_nw, d_dskip, d_alog, d_dtb = _ssd_bwd(
        dya, y, z, xbc, dt_raw, states, dt_bias, a_log, a_log_x, d_skip_x, small["ssd_norm_w"], expand, expand_t,
        name="ssd_bwd")
    dxbc_raw, d_conv_a_w, d_conv_a_b = _conv_a_bwd(xbc_raw, conv_a_w, small["conv_a_b"], dxbc, name="conv_a_bwd")
    started = emit_grad("w_in", {
        "in_z": _wgrad(n1, dz, "in_z_wgrad", after=started_branch), "in_xbc": _wgrad(n1, dxbc_raw, "in_xbc_wgrad"),
        "in_dt": _wgrad(n1, ddt, "in_dt_wgrad")[:, :SSD_HEADS], "in_uv": _wgrad(n1, duv, "in_uv_wgrad"),
        "in_gate_a": _wgrad(n1, dg_a, "in_gate_a_wgrad"), "in_gate_b": _wgrad(n1, dg_b, "in_gate_b_wgrad")})
    dn1 = _mm([(dz, wts["in_z"]), (dxbc_raw, wts["in_xbc"]), (ddt, wts["in_dt"]), (duv, wts["in_uv"]),
               (dg_a, wts["in_gate_a"]), (dg_b, wts["in_gate_b"])], "in_dgrad", trans_b=True, after=started,
              out_dtype=BF16)
    dx, _, d_norm1 = _rms_bwd(x, small["norm1_w"], dn1, dh1, name="rms1_bwd")

    grads_small = {
        "norm1_w": d_norm1, "b_gate": jnp.concatenate([dbg_a, dbg_b], axis=1),
        "conv_a_w": d_conv_a_w[:4], "conv_a_b": d_conv_a_b,
        "dt_bias": d_dtb[:, :SSD_HEADS], "a_log": d_alog[:, :SSD_HEADS], "d_skip": d_dskip[:, :SSD_HEADS],
        "ssd_norm_w": d_ssd_nw, "uv_b": d_uvb, "v_ln_w": d_lnw, "v_ln_b": d_lnb,
        "w_spatial": d_wsp[None], "b_spatial": d_bsp_t[:, :SGU_GROUPS].T[None],
        "norm2_w": d_norm2, "conv_f_w": jnp.concatenate([dwf_a[:3], dwf_v[:3]], axis=1),
        "conv_f_b": jnp.concatenate([dbf_a, dbf_v], axis=1), "final_norm_w": d_final.reshape(D_MODEL),
    }
    return loss, dx, grads_small


HBM = pl.BlockSpec(memory_space=pl.ANY)
MESH = pl.DeviceIdType.MESH


def _mesh_pos():
    return lax.axis_index("x"), lax.axis_index("y"), lax.axis_index("c")


def _other_chips(x, y):
    return [(1 - x, y), (x, 1 - y), (1 - x, 1 - y)]


def _remote(src, dst, send_sems, recv_sems, k, dev):
    return pltpu.make_async_remote_copy(src_ref=src, dst_ref=dst, send_sem=send_sems.at[k], recv_sem=recv_sems.at[k],
                                        device_id=dev, device_id_type=MESH)


def _dma_sems(n):
    return [pltpu.SemaphoreType.DMA((n,)), pltpu.SemaphoreType.DMA((n,))]


HBM_ONLY = pl.BlockSpec(memory_space=pltpu.HBM)
SEMAPHORES = pl.BlockSpec(memory_space=pltpu.SEMAPHORE)
DATAFLOW_EFFECT = pltpu.SideEffectType.DATAFLOW_SIDE_EFFECTING
N_PEER_CHIPS = N_CHIPS - 1


def _gather_sends(w_ref, land_ref, send_sems, recv_sems):
    x, y, c = _mesh_pos()
    return [_remote(w_ref.at[c], land_ref.at[2 * x + y, c], send_sems, recv_sems, k, (px, py, c))
            for k, (px, py) in enumerate(_other_chips(x, y))]


def _gather_arrivals(w_ref, land_ref, send_sems, recv_sems):
    x, y, c = _mesh_pos()
    return [_remote(w_ref.at[c], land_ref.at[2 * px + py, c], send_sems, recv_sems, k, (px, py, c))
            for k, (px, py) in enumerate(_other_chips(x, y))]


def _gather_whole_sends(w_ref, land_ref, send_sems, recv_sems):
    x, y, c = _mesh_pos()
    return [_remote(w_ref, land_ref.at[2 * x + y], send_sems, recv_sems, k, (px, py, c))
            for k, (px, py) in enumerate(_other_chips(x, y))]


def _gather_whole_arrivals(w_ref, land_ref, send_sems, recv_sems):
    x, y, c = _mesh_pos()
    return [_remote(w_ref, land_ref.at[2 * px + py], send_sems, recv_sems, k, (px, py, c))
            for k, (px, py) in enumerate(_other_chips(x, y))]


def _scatter_sends(h_ref, land_ref, send_sems, recv_sems):
    x, y, c = _mesh_pos()
    return [_remote(h_ref.at[2 * px + py], land_ref.at[2 * x + y], send_sems, recv_sems, k, (px, py, c))
            for k, (px, py) in enumerate(_other_chips(x, y))]


def _scatter_arrivals(h_ref, land_ref, send_sems, recv_sems):
    x, y, c = _mesh_pos()
    return [_remote(h_ref.at[2 * x + y], land_ref.at[2 * px + py], send_sems, recv_sems, k, (px, py, c))
            for k, (px, py) in enumerate(_other_chips(x, y))]


def _exchange_wait_many(pendings, after, sends, arrivals, *, name):
    n = len(pendings)

    def body(*refs):
        for i in range(n):
            src_ref, land_ref, send_ref, recv_ref = refs[i], refs[n + i], refs[2 * n + i], refs[3 * n + i]
            for cp in sends(src_ref, land_ref, send_ref, recv_ref):
                cp.wait_send()
            for cp in arrivals(src_ref, land_ref, send_ref, recv_ref):
                cp.wait_recv()

    sources = [p[2] for p in pendings]
    landings = [p[3] for p in pendings]
    outs = pl.pallas_call(
        body, name=name,
        out_shape=tuple(pltpu.HBM(a.shape, a.dtype) for a in sources + landings),
        in_specs=[HBM_ONLY] * (2 * n) + [SEMAPHORES] * (2 * n) + [pl.BlockSpec(memory_space=pl.ANY)],
        out_specs=tuple([HBM_ONLY] * (2 * n)), input_output_aliases={i: i for i in range(2 * n)},
        compiler_params=pltpu.CompilerParams(has_side_effects=DATAFLOW_EFFECT),
    )(*sources, *landings, *[p[0] for p in pendings], *[p[1] for p in pendings], after)
    return [(outs[i], outs[n + i]) for i in range(n)]


def _sibling_sends(src_ref, land_ref, send_sems, recv_sems):
    x, y, c = _mesh_pos()
    return [_remote(src_ref, land_ref, send_sems, recv_sems, 0, (x, y, 1 - c))]


def _exchange_start(sources, landing_shapes, sends, *, after=None, name):
    n = len(sources)
    extra = [] if after is None else [after]

    def body(*refs):
        sems = refs[2 * n + len(extra):4 * n + len(extra)]
        for i in range(n):
            send_i = sends[i] if isinstance(sends, (list, tuple)) else sends
            for cp in send_i(refs[i], refs[n + i], sems[2 * i], sems[2 * i + 1]):
                cp.start()
        refs[-1][...] = jnp.zeros_like(refs[-1])

    hbm = [pltpu.HBM(s.shape, s.dtype) for s in sources] + [pltpu.HBM(shp, s.dtype)
                                                             for shp, s in zip(landing_shapes, sources)]
    outs = pl.pallas_call(
        body, name=name,
        out_shape=tuple([pltpu.SemaphoreType.DMA((N_PEER_CHIPS,))] * (2 * n) + hbm
                        + [jax.ShapeDtypeStruct((8, LANES), F32)]),
        in_specs=[HBM_ONLY] * (2 * n) + [pl.BlockSpec(memory_space=pl.ANY)] * len(extra),
        out_specs=tuple([SEMAPHORES] * (2 * n) + [HBM_ONLY] * (2 * n) + [pl.BlockSpec(memory_space=pltpu.VMEM)]),
        input_output_aliases={i: 2 * n + i for i in range(2 * n)},
        compiler_params=pltpu.CompilerParams(has_side_effects=DATAFLOW_EFFECT),
    )(*[pltpu.with_memory_space_constraint(s, pltpu.HBM) for s in sources],
      *[pltpu.with_memory_space_constraint(lax.empty(shp, s.dtype), pltpu.HBM)
        for shp, s in zip(landing_shapes, sources)], *extra)
    pending = [(outs[2 * i], outs[2 * i + 1], outs[2 * n + i], outs[3 * n + i]) for i in range(n)]
    return pending, outs[-1]


def _exchange_wait(pending, after, sends, arrivals, *, name):
    send_sems, recv_sems, source, landing = pending

    def body(src_ref, land_ref, send_ref, recv_ref, after_ref, src_out, land_out):
        for cp in sends(src_ref, land_ref, send_ref, recv_ref):
            cp.wait_send()
        for cp in arrivals(src_ref, land_ref, send_ref, recv_ref):
            cp.wait_recv()

    return pl.pallas_call(
        body, name=name,
        out_shape=(pltpu.HBM(source.shape, source.dtype), pltpu.HBM(landing.shape, landing.dtype)),
        in_specs=[HBM_ONLY, HBM_ONLY, SEMAPHORES, SEMAPHORES, pl.BlockSpec(memory_space=pl.ANY)],
        out_specs=(HBM_ONLY, HBM_ONLY), input_output_aliases={0: 0, 1: 1},
        compiler_params=pltpu.CompilerParams(has_side_effects=DATAFLOW_EFFECT),
    )(source, landing, send_sems, recv_sems, after)


def _gather_ici(shard, *, name):
    _, rh, cols = shard.shape

    def body(w_ref, o_ref, send_sems, recv_sems):
        x, y, c = _mesh_pos()
        mine = 2 * x + y
        sends = []
        for k, (px, py) in enumerate(_other_chips(x, y)):
            cp = _remote(w_ref.at[c], o_ref.at[mine, c], send_sems, recv_sems, k, (px, py, c))
            cp.start()
            sends.append(cp)
        for k, (px, py) in enumerate(_other_chips(x, y)):
            _remote(w_ref.at[c], o_ref.at[2 * px + py, c], send_sems, recv_sems, k, (px, py, c)).wait_recv()
        for cp in sends:
            cp.wait_send()

    return pl.pallas_call(
        body, name=name, in_specs=[HBM], out_specs=HBM,
        out_shape=jax.ShapeDtypeStruct((N_CHIPS, 2, rh, cols), shard.dtype), scratch_shapes=_dma_sems(3),
    )(shard)


def _gather_d2d(parts, *, name):
    def body(a_ref, o_ref, send_sems, recv_sems):
        x, y, c = _mesh_pos()
        sibling = (x, y, 1 - c)
        sends = []
        for k, (px, py) in enumerate(_other_chips(x, y)):
            cp = _remote(a_ref.at[2 * px + py, c], o_ref.at[2 * px + py, c], send_sems, recv_sems, k, sibling)
            cp.start()
            sends.append(cp)
        for k, (px, py) in enumerate(_other_chips(x, y)):
            _remote(a_ref.at[2 * px + py, c], o_ref.at[2 * px + py, 1 - c], send_sems, recv_sems, k, sibling).wait_recv()
        for cp in sends:
            cp.wait_send()

    return pl.pallas_call(
        body, name=name, in_specs=[HBM], out_specs=HBM,
        out_shape=jax.ShapeDtypeStruct(parts.shape, parts.dtype),
        input_output_aliases={0: 0}, scratch_shapes=_dma_sems(3),
    )(parts)


def _all_gather_chips(shard_flat, name):
    rows, cols = shard_flat.shape
    parts = _gather_ici(shard_flat.reshape(2, rows // 2, cols), name=name + "_ici")
    others = _gather_d2d(parts, name=name + "_d2d").reshape(N_CHIPS, rows, cols)
    chip = 2 * lax.axis_index("x") + lax.axis_index("y")
    return lax.dynamic_update_slice(others, shard_flat[None], (chip, 0, 0))


def _row_tile(rows, mult, cap):
    best = mult
    for t in range(mult, min(rows, cap) + 1, mult):
        if rows % t == 0:
            best = t
    assert rows % best == 0, (rows, mult)
    return best


def _swap_halves_d2d(g, *, after=None, name):
    _, _, rh, cols = g.shape
    extra = [] if after is None else [after]

    def body(g_ref, *rest):
        o_ref, send_sems, recv_sems = rest[len(extra):]
        x, y, c = _mesh_pos()
        sibling = (x, y, 1 - c)
        sends = []
        for s in range(N_CHIPS):
            cp = _remote(g_ref.at[s, 1 - c], o_ref.at[s], send_sems, recv_sems, s, sibling)
            cp.start()
            sends.append(cp)
        for s in range(N_CHIPS):
            _remote(g_ref.at[s, c], o_ref.at[s], send_sems, recv_sems, s, sibling).wait_recv()
        for cp in sends:
            cp.wait_send()

    return pl.pallas_call(
        body, name=name, in_specs=[HBM] * (1 + len(extra)), out_specs=HBM,
        out_shape=jax.ShapeDtypeStruct((N_CHIPS, rh, cols), g.dtype), scratch_shapes=_dma_sems(N_CHIPS),
    )(g, *extra)


def _add_own_half(g, arrived, core, *, name):
    _, _, rh, cols = g.shape
    mult = 16 if g.dtype == BF16 else 8
    tr = _row_tile(rh, mult, max(mult, (512 * 1024) // cols))

    def body(core_ref, g_ref, a_ref, o_ref):
        o_ref[...] = (g_ref[0].astype(F32) + a_ref[...].astype(F32)).astype(o_ref.dtype)

    grid_spec = pltpu.PrefetchScalarGridSpec(
        num_scalar_prefetch=1, grid=(N_CHIPS, rh // tr),
        in_specs=[pl.BlockSpec((1, 1, tr, cols), lambda s, i, core_ref: (s, core_ref[0], i, 0)),
                  pl.BlockSpec((1, tr, cols), lambda s, i, core_ref: (s, i, 0))],
        out_specs=pl.BlockSpec((1, tr, cols), lambda s, i, core_ref: (s, i, 0)))
    return pl.pallas_call(
        body, name=name, grid_spec=grid_spec, out_shape=jax.ShapeDtypeStruct((N_CHIPS, rh, cols), g.dtype),
        compiler_params=_params(("parallel", "parallel")),
    )(core, g, arrived)


def _scatter_ici(h, *, name):
    def body(h_ref, o_ref, send_sems, recv_sems):
        x, y, c = _mesh_pos()
        mine = 2 * x + y
        sends = []
        for k, (px, py) in enumerate(_other_chips(x, y)):
            cp = _remote(h_ref.at[2 * px + py], o_ref.at[mine], send_sems, recv_sems, k, (px, py, c))
            cp.start()
            sends.append(cp)
        for k, (px, py) in enumerate(_other_chips(x, y)):
            _remote(h_ref.at[mine], o_ref.at[2 * px + py], send_sems, recv_sems, k, (px, py, c)).wait_recv()
        for cp in sends:
            cp.wait_send()

    others = pl.pallas_call(
        body, name=name, in_specs=[HBM], out_specs=HBM, out_shape=jax.ShapeDtypeStruct(h.shape, h.dtype),
        scratch_shapes=_dma_sems(3),
    )(h)
    chip = 2 * lax.axis_index("x") + lax.axis_index("y")
    own = lax.dynamic_slice_in_dim(h, chip, 1, axis=0)
    return lax.dynamic_update_slice(others, own, (chip, 0, 0))


def _sum_chips(parts, *, name):
    _, rh, cols = parts.shape
    mult = 16 if parts.dtype == BF16 else 8
    tr = _row_tile(rh, mult, max(mult, (512 * 1024) // cols))

    def body(p_ref, o_ref):
        acc = p_ref[0].astype(F32)
        for s in range(1, N_CHIPS):
            acc = acc + p_ref[s].astype(F32)
        o_ref[...] = acc

    return pl.pallas_call(
        body, name=name, grid=(rh // tr,),
        in_specs=[pl.BlockSpec((N_CHIPS, tr, cols), lambda i: (0, i, 0))],
        out_specs=pl.BlockSpec((tr, cols), lambda i: (i, 0)),
        out_shape=jax.ShapeDtypeStruct((rh, cols), F32), compiler_params=_params(("parallel",)),
    )(parts)


def _sum_chips_with_own(landed, sent, chip, *, name):
    _, rh, cols = landed.shape
    mult = 16 if landed.dtype == BF16 else 8
    tr = _row_tile(rh, mult, max(mult, (512 * 1024) // cols))

    def body(chip_ref, own_ref, px_ref, py_ref, pxy_ref, o_ref):
        acc = own_ref[0].astype(F32)
        for p_ref in (px_ref, py_ref, pxy_ref):
            acc = acc + p_ref[0].astype(F32)
        o_ref[...] = acc

    def block_of(flip):
        return pl.BlockSpec((1, tr, cols), lambda i, chip_ref: (chip_ref[0] ^ flip, i, 0))

    grid_spec = pltpu.PrefetchScalarGridSpec(
        num_scalar_prefetch=1, grid=(rh // tr,),
        in_specs=[block_of(0), block_of(2), block_of(1), block_of(3)],
        out_specs=pl.BlockSpec((tr, cols), lambda i, chip_ref: (i, 0)))
    return pl.pallas_call(
        body, name=name, grid_spec=grid_spec, out_shape=jax.ShapeDtypeStruct((rh, cols), F32),
        compiler_params=_params(("parallel",)),
    )(chip, sent, landed, landed, landed)


def _share_d2d(f, *, name):
    fs = f if isinstance(f, (list, tuple)) else [f]
    others = _swap_with_sibling(fs, name=name)
    first = lax.axis_index("c") == 0
    both = [jnp.stack([jnp.where(first, a, b), jnp.where(first, b, a)]) for a, b in zip(fs, others)]
    return both if isinstance(f, (list, tuple)) else both[0]


def _swap_with_sibling(fs, *, name):
    n = len(fs)

    def body(*refs):
        x, y, c = _mesh_pos()
        sibling = (x, y, 1 - c)
        send_sems, recv_sems = refs[2 * n:]
        copies = [_remote(refs[i], refs[n + i], send_sems, recv_sems, i, sibling) for i in range(n)]
        for cp in copies:
            cp.start()
        for cp in copies:
            cp.wait()

    return pl.pallas_call(
        body, name=name, in_specs=[HBM] * n, out_specs=[HBM] * n,
        out_shape=[jax.ShapeDtypeStruct(a.shape, a.dtype) for a in fs], scratch_shapes=_dma_sems(n),
    )(*fs)


def _reduce_scatter_chips(g, core, name, after=None):
    _, rows, cols = g.shape
    g = g.reshape(N_CHIPS, 2, rows // 2, cols)
    arrived = _swap_halves_d2d(g, after=after, name=name + "_swap")
    chip_sum = _add_own_half(g, arrived, core, name=name + "_add2")
    parts = _scatter_ici(chip_sum, name=name + "_ici")
    total = _sum_chips(parts, name=name + "_sum4")
    return _share_d2d(total, name=name + "_share").reshape(rows, cols)


BIG = ("w_in", "w_branch", "w_out", "w_up", "w_down")
BIG_COLUMN_SHARDED = ("w_in", "w_up")
CONV = ("conv_a_w", "conv_f_w")
REPLICATED = ("norm1_w", "b_gate", "conv_a_b", "dt_bias", "a_log", "d_skip", "ssd_norm_w", "uv_b", "v_ln_w",
              "v_ln_b", "w_spatial", "b_spatial", "norm2_w", "conv_f_b", "final_norm_w")
WEIGHT_ORDER = ("norm1_w", "w_in", "b_gate", "conv_a_w", "conv_a_b", "dt_bias", "a_log", "d_skip", "ssd_norm_w",
                "uv_b", "v_ln_w", "v_ln_b", "w_spatial", "b_spatial", "w_branch", "w_out", "norm2_w", "w_up",
                "conv_f_w", "conv_f_b", "w_down", "final_norm_w")
SMALL_EXCHANGE_ROWS = 64


_GATE0 = SSD_IN + 2 * SGU_WIDTH
IN_SEGMENTS = {
    "in_z": (0, SSD_D_INNER), "in_xbc": (SSD_D_INNER, SSD_D_INNER + SSD_XBC), "in_dt": (SSD_D_INNER + SSD_XBC, SSD_IN),
    "in_uv": (SSD_IN, _GATE0), "in_gate": (_GATE0, IN_COLS), "in_gate_a": (_GATE0, _GATE0 + D_MODEL),
    "in_gate_b": (_GATE0 + D_MODEL, IN_COLS),
}
IN_GRAD_SEGMENTS = ("in_z", "in_xbc", "in_dt", "in_uv", "in_gate_a", "in_gate_b")


def _take_columns(parts, start, stop):
    out = []
    for a, first in parts:
        lo, hi = max(start, first), min(stop, first + a.shape[1])
        if lo < hi:
            out.append(a[:, lo - first:hi - first])
    return out[0] if len(out) == 1 else jnp.concatenate(out, axis=1)


def _flat_rows(arrays, row_multiple):
    flat = jnp.concatenate([a.reshape(-1) for a in arrays])
    rows = -(-flat.shape[0] // (LANES * row_multiple)) * row_multiple
    return jnp.pad(flat, (0, rows * LANES - flat.shape[0])).reshape(rows, LANES)


def _unflatten(flat, shapes):
    flat = flat.reshape(-1)
    out, off = [], 0
    for shp in shapes:
        n = math.prod(shp)
        out.append(flat[off:off + n].reshape(shp))
        off += n
    return out


def _from_chip_blocks(blocks, name):
    if name in BIG_COLUMN_SHARDED or name in CONV:
        k = blocks.shape[1]
        return jnp.transpose(blocks, (1, 0, 2)).reshape(k, -1)
    return blocks.reshape(-1, blocks.shape[-1])


def _to_chip_blocks(whole, name):
    if name in BIG_COLUMN_SHARDED or name in CONV:
        k, n = whole.shape
        return jnp.transpose(whole.reshape(k, N_CHIPS, n // N_CHIPS), (1, 0, 2))
    return whole.reshape(N_CHIPS, whole.shape[0] // N_CHIPS, whole.shape[1])


def kernel(x, norm1_w, w_in, b_gate, conv_a_w, conv_a_b, dt_bias, a_log, d_skip, ssd_norm_w, uv_b, v_ln_w, v_ln_b, w_spatial, b_spatial, w_branch, w_out, norm2_w, w_up, conv_f_w, conv_f_b, w_down, final_norm_w, loss_target, m_norm1_w, m_w_in, m_b_gate, m_conv_a_w, m_conv_a_b, m_dt_bias, m_a_log, m_d_skip, m_ssd_norm_w, m_uv_b, m_v_ln_w, m_v_ln_b, m_w_spatial, m_b_spatial, m_w_branch, m_w_out, m_norm2_w, m_w_up, m_conv_f_w, m_conv_f_b, m_w_down, m_final_norm_w, v_norm1_w, v_w_in, v_b_gate, v_conv_a_w, v_conv_a_b, v_dt_bias, v_a_log, v_d_skip, v_ssd_norm_w, v_uv_b, v_v_ln_w, v_v_ln_b, v_w_spatial, v_b_spatial, v_w_branch, v_w_out, v_norm2_w, v_w_up, v_conv_f_w, v_conv_f_b, v_w_down, v_final_norm_w):
    weights = dict(norm1_w=norm1_w, w_in=w_in, b_gate=b_gate, conv_a_w=conv_a_w, conv_a_b=conv_a_b, dt_bias=dt_bias,
                   a_log=a_log, d_skip=d_skip, ssd_norm_w=ssd_norm_w, uv_b=uv_b, v_ln_w=v_ln_w, v_ln_b=v_ln_b,
                   w_spatial=w_spatial, b_spatial=b_spatial, w_branch=w_branch, w_out=w_out, norm2_w=norm2_w,
                   w_up=w_up, conv_f_w=conv_f_w, conv_f_b=conv_f_b, w_down=w_down, final_norm_w=final_norm_w)
    mom1 = dict(norm1_w=m_norm1_w, w_in=m_w_in, b_gate=m_b_gate, conv_a_w=m_conv_a_w, conv_a_b=m_conv_a_b,
                dt_bias=m_dt_bias, a_log=m_a_log, d_skip=m_d_skip, ssd_norm_w=m_ssd_norm_w, uv_b=m_uv_b,
                v_ln_w=m_v_ln_w, v_ln_b=m_v_ln_b, w_spatial=m_w_spatial, b_spatial=m_b_spatial, w_branch=m_w_branch,
                w_out=m_w_out, norm2_w=m_norm2_w, w_up=m_w_up, conv_f_w=m_conv_f_w, conv_f_b=m_conv_f_b,
                w_down=m_w_down, final_norm_w=m_final_norm_w)
    mom2 = dict(norm1_w=v_norm1_w, w_in=v_w_in, b_gate=v_b_gate, conv_a_w=v_conv_a_w, conv_a_b=v_conv_a_b,
                dt_bias=v_dt_bias, a_log=v_a_log, d_skip=v_d_skip, ssd_norm_w=v_ssd_norm_w, uv_b=v_uv_b,
                v_ln_w=v_v_ln_w, v_ln_b=v_v_ln_b, w_spatial=v_w_spatial, b_spatial=v_b_spatial, w_branch=v_w_branch,
                w_out=v_w_out, norm2_w=v_norm2_w, w_up=v_w_up, conv_f_w=v_conv_f_w, conv_f_b=v_conv_f_b,
                w_down=v_w_down, final_norm_w=v_final_norm_w)
    chip = 2 * lax.axis_index("x") + lax.axis_index("y")
    core = lax.axis_index("c").astype(jnp.int32).reshape(1)

    whole = {}
    conv_shapes = [weights[n].shape[1:] for n in CONV]
    conv_gathered = _all_gather_chips(_flat_rows([weights[n] for n in CONV], 16), "gather_conv").reshape(N_CHIPS, -1)
    off = 0
    for n, shp in zip(CONV, conv_shapes):
        size = math.prod(shp)
        whole[n] = _from_chip_blocks(conv_gathered[:, off:off + size].reshape((N_CHIPS,) + shp), n)
        off += size
    shard_shapes = {n: weights[n].shape[1:] for n in BIG}
    halves = [weights[n][0].astype(BF16).reshape(2, shard_shapes[n][0] // 2, shard_shapes[n][1]) for n in BIG]
    sends = [_gather_sends if n == "w_in" else _gather_whole_sends for n in BIG]
    gathers, gathers_started = _exchange_start(halves, [(N_CHIPS,) + h.shape for h in halves], sends,
                                               after=conv_gathered, name="gather_start")
    gathers = dict(zip(BIG, gathers))

    def get_weight(name, after):
        rows, cols = shard_shapes[name]
        if name == "w_in":
            own, landed = _exchange_wait(gathers[name], after, _gather_sends, _gather_arrivals,
                                         name="gather_" + name + "_wait")
            landed = _gather_d2d(landed, name="gather_" + name + "_d2d")
        else:
            own, landed = _exchange_wait(gathers[name], after, _gather_whole_sends, _gather_whole_arrivals,
                                         name="gather_" + name + "_wait")
        blocks = lax.dynamic_update_slice(landed.reshape(N_CHIPS, rows, cols), own.reshape(1, rows, cols),
                                          (chip, 0, 0))
        if name == "w_up":
            return {"up": blocks}
        if name == "w_in":
            parts = [(blocks[k], cols * k) for k in range(N_CHIPS)]
            segs = {n: _take_columns(parts, a, b) for n, (a, b) in IN_SEGMENTS.items()}
            segs["in_dt"] = jnp.pad(segs["in_dt"], ((0, 0), (0, LANES - SSD_HEADS)))
            return segs
        full = _from_chip_blocks(blocks, name)
        if name == "w_branch":
            return {"branch_a": full[:SSD_D_INNER], "branch_b": full[SSD_D_INNER:]}
        return {name[2:]: full}

    small = {n: weights[n] for n in REPLICATED}
    small["conv_a_w"] = whole["conv_a_w"]
    small["conv_f_w"] = whole["conv_f_w"]
    small["gathers_started"] = gathers_started

    reductions = {}

    def emit_grad(name, g):
        if name == "w_in":
            parts = [(g[n], IN_SEGMENTS[n][0]) for n in IN_GRAD_SEGMENTS]
            cols = shard_shapes[name][1]
            g_blocks = jnp.stack([_take_columns(parts, cols * k, cols * (k + 1)) for k in range(N_CHIPS)])
        else:
            g_blocks = g if name == "w_up" else _to_chip_blocks(g, name)
        if name == "w_in":
            _, rows, cols = g_blocks.shape
            g_halves = g_blocks.reshape(N_CHIPS, 2, rows // 2, cols)
            arrived = _swap_halves_d2d(g_halves, name="reduce_" + name + "_swap")
            g_blocks = _add_own_half(g_halves, arrived, core, name="reduce_" + name + "_add2")
        (pending,), started = _exchange_start([g_blocks], [g_blocks.shape], _scatter_sends,
                                              name="reduce_" + name + "_start")
        reductions[name] = pending
        return started

    loss, dx, grads_small = _local_step(x[0], loss_target[0], get_weight, small, emit_grad)

    order = ("w_down", "w_up", "w_out", "w_branch", "w_in")
    core_sums = []
    chip_index = chip.astype(jnp.int32).reshape(1)
    for n in order:
        sent, landed = _exchange_wait(reductions[n], dx, _scatter_sends, _scatter_arrivals,
                                      name="reduce_" + n + "_wait")
        core_sums.append(_sum_chips_with_own(landed, sent, chip_index, name="reduce_" + n + "_sum4"))
    swaps, swaps_started = _exchange_start(core_sums, [a.shape for a in core_sums], _sibling_sends, name="reduce_swap_start")
    grads = {}

    small_names = REPLICATED + CONV + ("loss",)
    grads_small = dict(grads_small, loss=loss)
    small_shapes = [grads_small[n].shape for n in small_names]
    g_small = _flat_rows([grads_small[n] for n in small_names], N_CHIPS * 2 * SMALL_EXCHANGE_ROWS)
    red_small = _reduce_scatter_chips(g_small.reshape(N_CHIPS, -1, LANES), core, "reduce_small", after=swaps_started)
    all_small = _all_gather_chips(red_small, "gather_small")
    swapped = _exchange_wait_many(swaps, all_small, _sibling_sends, _sibling_sends, name="reduce_swap_wait")
    core_sums = {n: own for n, (own, _) in zip(order, swapped)}
    sibling_sums = {n: other for n, (_, other) in zip(order, swapped)}
    first = lax.axis_index("c") == 0
    w_in_halves = (core_sums["w_in"], sibling_sums["w_in"])
    w_in_grad = jnp.concatenate([jnp.where(first, w_in_halves[0], w_in_halves[1]),
                                 jnp.where(first, w_in_halves[1], w_in_halves[0])], axis=0)
    for n, g in zip(small_names, _unflatten(all_small, small_shapes)):
        if n == "loss":
            total_loss = g[0, 0]
            continue
        if n in CONV:
            width = g.shape[1] // N_CHIPS
            g = lax.dynamic_slice_in_dim(g, chip * width, width, axis=1)
        grads[n] = g.reshape(weights[n].shape[1:]) if n != "final_norm_w" else g

    delta, new_m, new_v = {}, {}, {}
    for n in BIG:
        shp = weights[n].shape
        if n == "w_in":
            g_t = w_in_grad.T
            results = [g_t] + list(_adamw(weights[n][0].T, g_t, mom1[n][0].T, mom2[n][0].T, name="adamw_" + n,
                                          tr=_row_tile(g_t.shape[0], 8, 136)))
            results = [a.T for a in results]
        else:
            results = _adamw_two_sums(weights[n][0], core_sums[n], sibling_sums[n], mom1[n][0], mom2[n][0],
                                      name="adamw_" + n, tr=_row_tile(shp[1], 8, 136))
        grads[n], delta[n], new_m[n], new_v[n] = [a.reshape(shp) for a in results]
    small_all = [n for n in WEIGHT_ORDER if n not in BIG]

    def as_2d(a):
        return a.reshape(-1, a.shape[-1])

    results = _adamw_many(*[[as_2d(src[n]) for n in small_all] for src in (weights, grads, mom1, mom2)],
                          name="adamw_small")
    for n, dv, mv, vv in zip(small_all, *results):
        shp = weights[n].shape
        delta[n], new_m[n], new_v[n] = dv.reshape(shp), mv.reshape(shp), vv.reshape(shp)

    grad_out = [grads[n].reshape(weights[n].shape) for n in WEIGHT_ORDER]
    return (total_loss, dx[None], *grad_out, *[delta[n] for n in WEIGHT_ORDER], *[new_m[n] for n in WEIGHT_ORDER],
            *[new_v[n] for n in WEIGHT_ORDER])
```

```python
import functools
import math

import jax
import jax.numpy as jnp
from jax import lax
from jax.experimental import pallas as pl
from jax.experimental.pallas import tpu as pltpu

F32 = jnp.float32
BF16 = jnp.bfloat16

D_MODEL = 1024
SSD_D_INNER = 2048
SSD_HEADS = 32
SSD_HEAD_DIM = 64
SSD_GROUPS = 4
SSD_HEADS_PER_GROUP = 8
SSD_STATE = 128
SSD_BC = 512
SSD_XBC = 3072
SSD_IN = 5152
SGU_WIDTH = 1024
SGU_GROUPS = 8
CHUNK = 128
IN_COLS = 9248
D_FF = 2816
NORM_EPS = 1e-6
LN_EPS = 1e-5
GROUP_COLS = SSD_HEADS_PER_GROUP * SSD_HEAD_DIM
LANES = 128

ADAM_LR = 0.001
ADAM_B1 = 0.9
ADAM_B2 = 0.999
ADAM_EPS = 1e-08
ADAM_WD = 0.01
ADAM_STEP = 10

N_CHIPS = 4
VMEM_LIMIT = 56 * 1024 * 1024

NT = (((1,), (1,)), ((), ()))
TN = (((0,), (0,)), ((), ()))
NN = (((1,), (0,)), ((), ()))


def _params(dims):
    return pltpu.CompilerParams(dimension_semantics=dims, vmem_limit_bytes=VMEM_LIMIT)


def _dot(a, b, dn=NN, precision=None):
    return lax.dot_general(a, b, dn, precision=precision, preferred_element_type=F32)


def _split3(x):
    hi = x.astype(BF16)
    rest = x - hi.astype(F32)
    mid = rest.astype(BF16)
    return hi, mid, (rest - mid.astype(F32)).astype(BF16)


def _dot_terms(terms, exact, dn=NN):
    out = None
    for t in terms:
        p = _dot(t, exact, dn)
        out = p if out is None else out + p
    return out


def _dot_exact_lhs(exact, terms):
    out = None
    for t in terms:
        p = _dot(exact, t)
        out = p if out is None else out + p
    return out


def _sigmoid(x):
    return 1.0 / (1.0 + jnp.exp(-x))


def _softplus(x):
    return jnp.maximum(x, 0.0) + jnp.log(1.0 + jnp.exp(-jnp.abs(x)))


def _matmul(pairs, *, trans_b=False, add=None, after=None, out_dtype=F32, tm=512, tn=512, name):
    def mat_shape(b):
        if isinstance(b, tuple) and b[1] == "cols":
            return (b[0].shape[1], b[0].shape[0] * b[0].shape[2])
        return b[0].shape[1:] if isinstance(b, tuple) else b.shape

    if isinstance(pairs[0][1], tuple) and pairs[0][1][1] == "cols":
        assert not trans_b and tn % LANES == 0 and pairs[0][1][0].shape[2] % tn == 0, name

    m = (pairs[0][0][0] if isinstance(pairs[0][0], tuple) else pairs[0][0]).shape[0]
    n = mat_shape(pairs[0][1])[0] if trans_b else mat_shape(pairs[0][1])[1]
    tm, tn = min(tm, m), min(tn, n)
    assert m % tm == 0 and n % tn == 0, (name, m, n, tm, tn)
    npairs = len(pairs)
    dn = NT if trans_b else NN

    def body(*refs):
        o_ref = refs[-1]
        acc = None
        for i in range(npairs):
            p = _dot(refs[2 * i][...].astype(BF16), refs[2 * i + 1][...].astype(BF16), dn)
            acc = p if acc is None else acc + p
        if add is not None:
            acc = acc + refs[2 * npairs][...]
        o_ref[...] = acc.astype(out_dtype)

    in_specs, args = [], []
    for a, b in pairs:
        bshape = mat_shape(b)
        k = bshape[1] if trans_b else bshape[0]
        assert bshape == ((n, k) if trans_b else (k, n)), (name, bshape)
        a, qa = a if isinstance(a, tuple) else (a, 0)
        assert a.shape[0] == m and a.shape[1] % k == 0, (name, a.shape, k)
        in_specs.append(pl.BlockSpec((tm, k), lambda i, j, qa=qa: (i, qa)))
        if isinstance(b, tuple) and b[1] == "cols":
            b = b[0]
            per = b.shape[2] // tn
            in_specs.append(pl.BlockSpec((None, k, tn), lambda i, j, per=per: (j // per, 0, j % per)))
        elif isinstance(b, tuple):
            b, qb = b
            if trans_b:
                in_specs.append(pl.BlockSpec((None, tn, k), lambda i, j, qb=qb: (qb, j, 0)))
            else:
                in_specs.append(pl.BlockSpec((None, k, tn), lambda i, j, qb=qb: (qb, 0, j)))
        elif trans_b:
            in_specs.append(pl.BlockSpec((tn, k), lambda i, j: (j, 0)))
        else:
            in_specs.append(pl.BlockSpec((k, tn), lambda i, j: (0, j)))
        args += [a, b]
    if add is not None:
        in_specs.append(pl.BlockSpec((tm, tn), lambda i, j: (i, j)))
        args.append(add)
    if after is not None:
        in_specs.append(pl.BlockSpec(memory_space=pl.ANY))
        args.append(after)
    return pl.pallas_call(
        body, name=name, grid=(m // tm, n // tn), in_specs=in_specs,
        out_specs=pl.BlockSpec((tm, tn), lambda i, j: (i, j)),
        out_shape=jax.ShapeDtypeStruct((m, n), out_dtype),
        compiler_params=_params(("parallel", "parallel")),
    )(*args)


def _matmul_tn(a, b, *, tk, tn, tm=1024, out_dtype=BF16, stack_out=False, after=None, name):
    m, k = a.shape
    n = b.shape[1]
    tm, tk, tn = min(tm, m), min(tk, k), min(tn, n)
    assert m % tm == 0 and k % tk == 0 and n % tn == 0, (name, m, k, n)
    nm = m // tm
    if stack_out:
        out_spec = pl.BlockSpec((None, tk, tn), lambda i, j, l: (j, i, 0))
        out_shape = jax.ShapeDtypeStruct((n // tn, k, tn), out_dtype)
    else:
        out_spec = pl.BlockSpec((tk, tn), lambda i, j, l: (i, j))
        out_shape = jax.ShapeDtypeStruct((k, n), out_dtype)

    def body(a_ref, b_ref, *rest):
        o_ref, acc = rest[-2:]
        mi = pl.program_id(2)

        @pl.when(mi == 0)
        def _():
            acc[...] = jnp.zeros_like(acc)

        acc[...] += _dot(a_ref[...].astype(BF16), b_ref[...].astype(BF16), TN)

        @pl.when(mi == nm - 1)
        def _():
            o_ref[...] = acc[...].astype(out_dtype)

    in_specs = [pl.BlockSpec((tm, tk), lambda i, j, l: (l, i)), pl.BlockSpec((tm, tn), lambda i, j, l: (l, j))]
    args = [a, b]
    if after is not None:
        in_specs.append(pl.BlockSpec(memory_space=pl.ANY))
        args.append(after)
    return pl.pallas_call(
        body, name=name, grid=(k // tk, n // tn, nm), in_specs=in_specs,
        out_specs=out_spec, out_shape=out_shape,
        scratch_shapes=[pltpu.VMEM((tk, tn), F32)],
        compiler_params=_params(("parallel", "parallel", "arbitrary")),
    )(*args)


def _rms_fwd(x, w, *, after=None, name, tm=512):
    s, d = x.shape
    tm = min(tm, s)
    extra = [] if after is None else [after]

    def body(x_ref, w_ref, *rest):
        o_ref = rest[-1]
        xv = x_ref[...]
        r = lax.rsqrt(jnp.mean(xv * xv, axis=-1, keepdims=True) + NORM_EPS)
        o_ref[...] = (xv * r * w_ref[...]).astype(BF16)

    return pl.pallas_call(
        body, name=name, grid=(s // tm,),
        in_specs=[pl.BlockSpec((tm, d), lambda i: (i, 0)), pl.BlockSpec((1, d), lambda i: (0, 0))]
        + [pl.BlockSpec(memory_space=pl.ANY)] * len(extra),
        out_specs=pl.BlockSpec((tm, d), lambda i: (i, 0)),
        out_shape=jax.ShapeDtypeStruct((s, d), BF16),
        compiler_params=_params(("parallel",)),
    )(x, w, *extra)


def _rms_bwd(x, w, dn, dres, *, name, tm=512):
    s, d = x.shape
    tm = min(tm, s)

    def body(x_ref, w_ref, dn_ref, dres_ref, dx_ref, dxb_ref, dw_ref):
        @pl.when(pl.program_id(0) == 0)
        def _():
            dw_ref[...] = jnp.zeros_like(dw_ref)

        xv = x_ref[...]
        r = lax.rsqrt(jnp.mean(xv * xv, axis=-1, keepdims=True) + NORM_EPS)
        xhat = xv * r
        dnv = dn_ref[...].astype(F32)
        dxhat = dnv * w_ref[...]
        dx = dres_ref[...] + r * (dxhat - xhat * jnp.mean(dxhat * xhat, axis=-1, keepdims=True))
        dx_ref[...] = dx
        dxb_ref[...] = dx.astype(BF16)
        dw_ref[...] += jnp.sum(dnv * xhat, axis=0, keepdims=True)

    tile = pl.BlockSpec((tm, d), lambda i: (i, 0))
    row = pl.BlockSpec((1, d), lambda i: (0, 0))
    return pl.pallas_call(
        body, name=name, grid=(s // tm,),
        in_specs=[tile, row, tile, tile], out_specs=[tile, tile, row],
        out_shape=[jax.ShapeDtypeStruct((s, d), F32), jax.ShapeDtypeStruct((s, d), BF16),
                   jax.ShapeDtypeStruct((1, d), F32)],
        compiler_params=_params(("arbitrary",)),
    )(x, w, dn, dres)


def _final_fwd_bwd(h2, wf, target, *, name, tm=512):
    s, d = h2.shape
    tm = min(tm, s)

    def body(h_ref, w_ref, t_ref, loss_ref, dh_ref, dhb_ref, dw_ref):
        @pl.when(pl.program_id(0) == 0)
        def _():
            dw_ref[...] = jnp.zeros_like(dw_ref)
            loss_ref[...] = jnp.zeros_like(loss_ref)

        hv = h_ref[...]
        r = lax.rsqrt(jnp.mean(hv * hv, axis=-1, keepdims=True) + NORM_EPS)
        xhat = hv * r
        err = xhat * w_ref[...] - t_ref[...]
        per_tok = jnp.mean(err * err, axis=-1, keepdims=True)
        loss_ref[...] += 0.5 * jnp.sum(per_tok, axis=0, keepdims=True)
        dy = err * (1.0 / d)
        dxhat = dy * w_ref[...]
        dh = r * (dxhat - xhat * jnp.mean(dxhat * xhat, axis=-1, keepdims=True))
        dh_ref[...] = dh
        dhb_ref[...] = dh.astype(BF16)
        dw_ref[...] += jnp.sum(dy * xhat, axis=0, keepdims=True)

    tile = pl.BlockSpec((tm, d), lambda i: (i, 0))
    row = pl.BlockSpec((1, d), lambda i: (0, 0))
    return pl.pallas_call(
        body, name=name, grid=(s // tm,),
        in_specs=[tile, row, tile],
        out_specs=[pl.BlockSpec((1, 1), lambda i: (0, 0)), tile, tile, row],
        out_shape=[jax.ShapeDtypeStruct((1, 1), F32), jax.ShapeDtypeStruct((s, d), F32),
                   jax.ShapeDtypeStruct((s, d), BF16), jax.ShapeDtypeStruct((1, d), F32)],
        compiler_params=_params(("arbitrary",)),
    )(h2, wf, target)


CONV_ROWS = 256
CONV_ROWS_FWD = 512
HALO = 8


def _rows_with_halo(ref, r0, rows, s, before, after):
    tile = 16 if ref.dtype == BF16 else HALO
    parts = []
    if before:
        prev = ref[pl.ds(pl.multiple_of(jnp.maximum(r0 - tile, 0), tile), tile), :].astype(F32)[tile - HALO:]
        parts.append(jnp.where(r0 > 0, prev, 0.0))
    parts.append(ref[pl.ds(r0, rows), :].astype(F32))
    if after:
        nxt = ref[pl.ds(pl.multiple_of(jnp.minimum(r0 + rows, s - tile), tile), tile), :].astype(F32)[:HALO]
        parts.append(jnp.where(r0 + rows < s, nxt, 0.0))
    return jnp.concatenate(parts, axis=0) if len(parts) > 1 else parts[0]


def _window(x_ref, r0, s, after):
    return _rows_with_halo(x_ref, r0, CONV_ROWS_FWD, s, True, after).astype(F32)


def _shifted(window, k, rows):
    if k == 0:
        return window[HALO:HALO + rows]
    return pltpu.roll(window, k, 0)[HALO:HALO + rows]


def _conv_taps(window, w_ref, kk, rows):
    acc = None
    for i in range(kk):
        term = w_ref[i:i + 1, :] * _shifted(window, kk - 1 - i, rows)
        acc = term if acc is None else acc + term
    return acc


def _row_loop(rows, step):
    def body(r, carry):
        return step(pl.multiple_of(r * rows, rows), carry)
    return body


def _conv_bwd_rows(x, dpe, w_ref, kk):
    dp = dpe[:CONV_ROWS]
    dx = None
    dws = []
    for i in range(kk):
        k = kk - 1 - i
        later = dp if k == 0 else pltpu.roll(dpe, dpe.shape[0] - k, 0)[:CONV_ROWS]
        dws.append(jnp.sum(later * x, axis=0, keepdims=True))
        term = w_ref[i:i + 1, :] * later
        dx = term if dx is None else dx + term
    return dx, dws, jnp.sum(dp, axis=0, keepdims=True)


def _conv_a_fwd(xraw, w, b, *, name, tc=128):
    s, c = xraw.shape
    kk = 4

    def body(x_ref, w_ref, b_ref, o_ref, pre_ref):
        def step(r0, carry):
            pre = _conv_taps(_window(x_ref, r0, s, False), w_ref, kk, CONV_ROWS_FWD) + b_ref[...]
            o_ref[pl.ds(r0, CONV_ROWS_FWD), :] = pre * _sigmoid(pre)
            pre_ref[pl.ds(r0, CONV_ROWS_FWD), :] = pre.astype(BF16)
            return carry

        lax.fori_loop(0, s // CONV_ROWS_FWD, _row_loop(CONV_ROWS_FWD, step), 0)

    col = pl.BlockSpec((s, tc), lambda j: (0, j))
    return pl.pallas_call(
        body, name=name, grid=(c // tc,),
        in_specs=[col, pl.BlockSpec((8, tc), lambda j: (0, j)), pl.BlockSpec((1, tc), lambda j: (0, j))],
        out_specs=[col, col], out_shape=[jax.ShapeDtypeStruct((s, c), F32), jax.ShapeDtypeStruct((s, c), BF16)],
        compiler_params=_params(("parallel",)),
    )(xraw, w, b)


def _conv_a_bwd(xraw, pre, w, dy, *, name, tc=128):
    s, c = xraw.shape
    kk = 4

    def body(x_ref, pre_ref, w_ref, dy_ref, dx_ref, dw_ref, db_ref):
        def step(r0, carry):
            pre = _rows_with_halo(pre_ref, r0, CONV_ROWS, s, False, True)
            sg = _sigmoid(pre)
            dpe = _rows_with_halo(dy_ref, r0, CONV_ROWS, s, False, True) * (sg * (1.0 + pre * (1.0 - sg)))
            dx, dws, db = _conv_bwd_rows(x_ref[pl.ds(r0, CONV_ROWS), :].astype(F32), dpe, w_ref, kk)
            dx_ref[pl.ds(r0, CONV_ROWS), :] = dx.astype(BF16)
            return tuple(acc + new for acc, new in zip(carry, dws + [db]))

        zero = jnp.zeros((1, tc), F32)
        sums = lax.fori_loop(0, s // CONV_ROWS, _row_loop(CONV_ROWS, step), (zero,) * (kk + 1))
        db_ref[...] = sums[kk]
        dw_ref[...] = jnp.concatenate(list(sums[:kk]) + [jnp.zeros((8 - kk, tc), F32)], axis=0)

    col = pl.BlockSpec((s, tc), lambda j: (0, j))
    w8 = pl.BlockSpec((8, tc), lambda j: (0, j))
    row = pl.BlockSpec((1, tc), lambda j: (0, j))
    return pl.pallas_call(
        body, name=name, grid=(c // tc,),
        in_specs=[col, col, w8, col], out_specs=[col, w8, row],
        out_shape=[jax.ShapeDtypeStruct((s, c), BF16), jax.ShapeDtypeStruct((8, c), F32),
                   jax.ShapeDtypeStruct((1, c), F32)],
        compiler_params=_params(("parallel",)),
    )(xraw, pre, w, dy)


def _conv_f_fwd(up_raw, w, b, *, name, tc=128):
    s, c2 = up_raw.shape
    c = c2 // 2
    nb = c // tc
    kk = 3

    def body(xa_ref, xv_ref, wa_ref, wv_ref, ba_ref, bv_ref, o_ref, a_out, v_out):
        def step(r0, carry):
            a = _conv_taps(_window(xa_ref, r0, s, False), wa_ref, kk, CONV_ROWS_FWD) + ba_ref[...]
            v = _conv_taps(_window(xv_ref, r0, s, False), wv_ref, kk, CONV_ROWS_FWD) + bv_ref[...]
            o_ref[pl.ds(r0, CONV_ROWS_FWD), :] = (a * _sigmoid(a) * v).astype(BF16)
            a_out[pl.ds(r0, CONV_ROWS_FWD), :] = a.astype(BF16)
            v_out[pl.ds(r0, CONV_ROWS_FWD), :] = v.astype(BF16)
            return carry

        lax.fori_loop(0, s // CONV_ROWS_FWD, _row_loop(CONV_ROWS_FWD, step), 0)

    col_a = pl.BlockSpec((s, tc), lambda j: (0, j))
    col_v = pl.BlockSpec((s, tc), lambda j: (0, j + nb))
    half = jax.ShapeDtypeStruct((s, c), BF16)
    return pl.pallas_call(
        body, name=name, grid=(nb,),
        in_specs=[col_a, col_v, pl.BlockSpec((8, tc), lambda j: (0, j)), pl.BlockSpec((8, tc), lambda j: (0, j + nb)),
                  pl.BlockSpec((1, tc), lambda j: (0, j)), pl.BlockSpec((1, tc), lambda j: (0, j + nb))],
        out_specs=[col_a, col_a, col_a], out_shape=[half, half, half],
        compiler_params=_params(("parallel",)),
    )(up_raw, up_raw, w, w, b, b)


def _conv_f_bwd(up_raw, a_pre, v_pre, w, dact, *, name, tc=128):
    s, c2 = up_raw.shape
    c = c2 // 2
    nb = c // tc
    kk = 3

    def body(xa_ref, xv_ref, a_ref, v_ref, wa_ref, wv_ref, d_ref,
             dxa_ref, dxv_ref, dwa_ref, dwv_ref, dba_ref, dbv_ref):
        def step(r0, carry):
            a = _rows_with_halo(a_ref, r0, CONV_ROWS, s, False, True)
            v = _rows_with_halo(v_ref, r0, CONV_ROWS, s, False, True)
            sg = _sigmoid(a)
            d = _rows_with_halo(d_ref, r0, CONV_ROWS, s, False, True)
            rows = pl.ds(r0, CONV_ROWS)
            dxa, dwas, dba = _conv_bwd_rows(xa_ref[rows, :].astype(F32), d * v * (sg * (1.0 + a * (1.0 - sg))),
                                            wa_ref, kk)
            dxv, dwvs, dbv = _conv_bwd_rows(xv_ref[rows, :].astype(F32), d * (a * sg), wv_ref, kk)
            dxa_ref[pl.ds(r0, CONV_ROWS), :] = dxa.astype(BF16)
            dxv_ref[pl.ds(r0, CONV_ROWS), :] = dxv.astype(BF16)
            return tuple(acc + new for acc, new in zip(carry, dwas + [dba] + dwvs + [dbv]))

        zero = jnp.zeros((1, tc), F32)
        sums = lax.fori_loop(0, s // CONV_ROWS, _row_loop(CONV_ROWS, step), (zero,) * (2 * kk + 2))
        pad = [jnp.zeros((8 - kk, tc), F32)]
        dwa_ref[...] = jnp.concatenate(list(sums[:kk]) + pad, axis=0)
        dba_ref[...] = sums[kk]
        dwv_ref[...] = jnp.concatenate(list(sums[kk + 1:2 * kk + 1]) + pad, axis=0)
        dbv_ref[...] = sums[2 * kk + 1]

    col_a = pl.BlockSpec((s, tc), lambda j: (0, j))
    col_v = pl.BlockSpec((s, tc), lambda j: (0, j + nb))
    w_a = pl.BlockSpec((8, tc), lambda j: (0, j))
    w_v = pl.BlockSpec((8, tc), lambda j: (0, j + nb))
    r_a = pl.BlockSpec((1, tc), lambda j: (0, j))
    r_v = pl.BlockSpec((1, tc), lambda j: (0, j + nb))
    outs = pl.pallas_call(
        body, name=name, grid=(nb,),
        in_specs=[col_a, col_v, col_a, col_a, w_a, w_v, col_a],
        out_specs=[col_a, col_a, w_a, w_a, r_a, r_a],
        out_shape=[jax.ShapeDtypeStruct((s, c), BF16), jax.ShapeDtypeStruct((s, c), BF16),
                   jax.ShapeDtypeStruct((8, c), F32), jax.ShapeDtypeStruct((8, c), F32),
                   jax.ShapeDtypeStruct((1, c), F32), jax.ShapeDtypeStruct((1, c), F32)],
        compiler_params=_params(("parallel",)),
    )(up_raw, up_raw, a_pre, v_pre, w, w, dact)
    return outs


def _tri_masks():
    row = lax.broadcasted_iota(jnp.int32, (CHUNK, CHUNK), 0)
    col = lax.broadcasted_iota(jnp.int32, (CHUNK, CHUNK), 1)
    return row >= col, row <= col


def _ssd_fwd(xbc, dt_raw, z, dt_bias, a_log, a_log_x, d_skip_x, norm_w, expand, *, name):
    s = xbc.shape[0]
    nc = s // CHUNK

    def body(xbc_ref, dtr_ref, z_ref, dtb_ref, alog_ref, alogx_ref, dskx_ref, nw_ref, e_ref,
             y_ref, ya_ref, st_ref, state):
        @pl.when(pl.program_id(0) == 0)
        def _():
            state[...] = jnp.zeros_like(state)

        st_ref[0] = state[...]
        lower, _ = _tri_masks()
        dt = _softplus(dtr_ref[...] + dtb_ref[...])
        adt = dt * (-jnp.exp(alog_ref[...]))
        acum = _dot_exact_lhs(lower.astype(BF16), _split3(adt))
        acum_t = acum.T
        dt_terms, acum_terms = _split3(dt), _split3(acum)
        for g in range(SSD_GROUPS):
            sl = slice(GROUP_COLS * g, GROUP_COLS * (g + 1))
            dt_x = _dot_terms(dt_terms, e_ref[:, sl])
            acum_x = _dot_terms(acum_terms, e_ref[:, sl])
            tot_x = jnp.sum(dt_x * (-jnp.exp(alogx_ref[:, sl])), axis=0, keepdims=True)
            xs = xbc_ref[:, sl]
            xdt = xs * dt_x
            xdt_b = xdt.astype(BF16)
            bg = xbc_ref[:, SSD_D_INNER + SSD_STATE * g:SSD_D_INNER + SSD_STATE * (g + 1)].astype(BF16)
            cg = xbc_ref[:, SSD_D_INNER + SSD_BC + SSD_STATE * g:SSD_D_INNER + SSD_BC + SSD_STATE * (g + 1)].astype(BF16)
            cb = _dot(cg, bg, NT)
            st_g = state[:, sl]
            y_off = _dot(cg, st_g.astype(BF16)) * jnp.exp(acum_x)
            parts = []
            for r in range(SSD_HEADS_PER_GROUP):
                h = SSD_HEADS_PER_GROUP * g + r
                dec = jnp.exp(jnp.where(lower, acum[:, h:h + 1] - acum_t[h:h + 1, :], -jnp.inf))
                parts.append(_dot((cb * dec).astype(BF16), xdt_b[:, SSD_HEAD_DIM * r:SSD_HEAD_DIM * (r + 1)]))
            y_ref[:, sl] = jnp.concatenate(parts, axis=1) + y_off + dskx_ref[:, sl] * xs
            wgt = (xdt * jnp.exp(tot_x - acum_x)).astype(BF16)
            state[:, sl] = st_g * jnp.exp(tot_x) + _dot(bg, wgt, TN)
        zv = z_ref[...].astype(F32)
        q = y_ref[...] * (zv * _sigmoid(zv))
        r = lax.rsqrt(jnp.mean(q * q, axis=-1, keepdims=True) + NORM_EPS)
        ya_ref[...] = (q * r * nw_ref[...]).astype(BF16)

    def chunk(w):
        return pl.BlockSpec((CHUNK, w), lambda c: (c, 0))

    def const(shape):
        return pl.BlockSpec(shape, lambda c: (0,) * len(shape))

    return pl.pallas_call(
        body, name=name, grid=(nc,),
        in_specs=[chunk(SSD_XBC), chunk(LANES), chunk(SSD_D_INNER), const((1, LANES)), const((1, LANES)),
                  const((1, SSD_D_INNER)), const((1, SSD_D_INNER)), const((1, SSD_D_INNER)),
                  const((LANES, SSD_D_INNER))],
        out_specs=[chunk(SSD_D_INNER), chunk(SSD_D_INNER),
                   pl.BlockSpec((1, SSD_STATE, SSD_D_INNER), lambda c: (c, 0, 0))],
        out_shape=[jax.ShapeDtypeStruct((s, SSD_D_INNER), F32), jax.ShapeDtypeStruct((s, SSD_D_INNER), BF16),
                   jax.ShapeDtypeStruct((nc, SSD_STATE, SSD_D_INNER), F32)],
        scratch_shapes=[pltpu.VMEM((SSD_STATE, SSD_D_INNER), F32)],
        compiler_params=_params(("arbitrary",)),
    )(xbc, dt_raw, z, dt_bias, a_log, a_log_x, d_skip_x, norm_w, expand)


def _ssd_bwd(dya, y, z, xbc, dt_raw, states, dt_bias, a_log, a_log_x, d_skip_x, norm_w, expand, expand_t, *, name):
    s = xbc.shape[0]
    nc = s // CHUNK

    def body(dya_ref, y_ref, z_ref, xbc_ref, dtr_ref, stp_ref, dtb_ref, alog_ref, alogx_ref, dskx_ref, nw_ref,
             e_ref, et_ref, dz_ref, dxbc_ref, ddt_ref, dnw_ref, ddsk_ref, dalog_ref, ddtb_ref,
             dstate, dy_sc, dskcol):
        i = pl.program_id(0)

        @pl.when(i == 0)
        def _():
            dstate[...] = jnp.zeros_like(dstate)
            dskcol[...] = jnp.zeros_like(dskcol)
            dnw_ref[...] = jnp.zeros_like(dnw_ref)
            dalog_ref[...] = jnp.zeros_like(dalog_ref)
            ddtb_ref[...] = jnp.zeros_like(ddtb_ref)
            ddsk_ref[...] = jnp.zeros_like(ddsk_ref)

        lower, upper = _tri_masks()
        rows = lax.broadcasted_iota(jnp.int32, (CHUNK, LANES), 0)
        pre = dtr_ref[...] + dtb_ref[...]
        dt = _softplus(pre)
        a = -jnp.exp(alog_ref[...])
        acum = _dot_exact_lhs(lower.astype(BF16), _split3(dt * a))
        acum_t = acum.T
        dt_terms, acum_terms = _split3(dt), _split3(acum)

        yv = y_ref[...]
        zv = z_ref[...].astype(F32)
        sz = _sigmoid(zv)
        silu_z = zv * sz
        q = yv * silu_z
        r = lax.rsqrt(jnp.mean(q * q, axis=-1, keepdims=True) + NORM_EPS)
        qhat = q * r
        dyav = dya_ref[...]
        dqhat = dyav * nw_ref[...]
        dnw_ref[...] += jnp.sum(dyav * qhat, axis=0, keepdims=True)
        dq = r * (dqhat - qhat * jnp.mean(dqhat * qhat, axis=-1, keepdims=True))
        dy_sc[...] = dq * silu_z
        dz_ref[...] = (dq * yv * (sz * (1.0 + zv * (1.0 - sz)))).astype(BF16)

        da_cum = jnp.zeros((CHUNK, LANES), F32)
        ddt = jnp.zeros((CHUNK, LANES), F32)
        for g in range(SSD_GROUPS):
            sl = slice(GROUP_COLS * g, GROUP_COLS * (g + 1))
            et_g = et_ref[sl, :]
            dt_x = _dot_terms(dt_terms, e_ref[:, sl])
            acum_x = _dot_terms(acum_terms, e_ref[:, sl])
            tot_x = jnp.sum(dt_x * (-jnp.exp(alogx_ref[:, sl])), axis=0, keepdims=True)
            e_tot = jnp.exp(tot_x)
            dec_s = jnp.exp(tot_x - acum_x)
            xs = xbc_ref[:, sl]
            xdt = xs * dt_x
            xdt_b = xdt.astype(BF16)
            dy = dy_sc[:, sl]
            dy_b = dy.astype(BF16)
            dskx = dskx_ref[:, sl]
            y_ssd = y_ref[:, sl] - dskx * xs
            dskcol[:, sl] += jnp.sum(dy * xs, axis=0, keepdims=True)
            bg = xbc_ref[:, SSD_D_INNER + SSD_STATE * g:SSD_D_INNER + SSD_STATE * (g + 1)].astype(BF16)
            cg = xbc_ref[:, SSD_D_INNER + SSD_BC + SSD_STATE * g:SSD_D_INNER + SSD_BC + SSD_STATE * (g + 1)].astype(BF16)
            cb_t = _dot(bg, cg, NT)
            sp = stp_ref[0, :, sl]
            ds_g = dstate[:, sl]
            ds_b = ds_g.astype(BF16)
            dye_b = (dy * jnp.exp(acum_x)).astype(BF16)
            dc = _dot(dye_b, sp.astype(BF16), NT)
            dxdt_state = dec_s * _dot(bg, ds_b)
            db = _dot((xdt * dec_s).astype(BF16), ds_b, NT)
            dcb_t = jnp.zeros((CHUNK, CHUNK), F32)
            parts = []
            for rr in range(SSD_HEADS_PER_GROUP):
                h = SSD_HEADS_PER_GROUP * g + rr
                hs = slice(SSD_HEAD_DIM * rr, SSD_HEAD_DIM * (rr + 1))
                dec_t = jnp.exp(jnp.where(upper, acum_t[h:h + 1, :] - acum[:, h:h + 1], -jnp.inf))
                parts.append(_dot((cb_t * dec_t).astype(BF16), dy_b[:, hs]))
                dcb_t = dcb_t + _dot(xdt_b[:, hs], dy_b[:, hs], NT) * dec_t
            dxdt = jnp.concatenate(parts, axis=1) + dxdt_state
            dcb_tb = dcb_t.astype(BF16)
            dc = dc + _dot(dcb_tb, bg, TN)
            db = db + _dot(dcb_tb, cg)
            tot_col = jnp.sum(ds_g * sp, axis=0, keepdims=True) * e_tot + jnp.sum(dxdt_state * xdt, axis=0, keepdims=True)
            d_tot = _dot_terms(_split3(jnp.broadcast_to(tot_col, (8, GROUP_COLS))), et_g)
            d_tot = jnp.max(d_tot, axis=0, keepdims=True)
            pair_sums = dy_b.astype(F32) * y_ssd - xdt_b.astype(F32) * dxdt
            da_cum = da_cum + _dot_terms(_split3(pair_sums), et_g) + jnp.where(rows == CHUNK - 1, d_tot, 0.0)
            ddt = ddt + _dot_terms(_split3(dxdt * xs), et_g)
            dxbc_ref[:, sl] = dy * dskx + dxdt * dt_x
            dxbc_ref[:, SSD_D_INNER + SSD_STATE * g:SSD_D_INNER + SSD_STATE * (g + 1)] = db
            dxbc_ref[:, SSD_D_INNER + SSD_BC + SSD_STATE * g:SSD_D_INNER + SSD_BC + SSD_STATE * (g + 1)] = dc
            dstate[:, sl] = e_tot * ds_g + _dot(cg, dye_b, TN)

        dadt = _dot_exact_lhs(upper.astype(BF16), _split3(da_cum))
        ddt = ddt + dadt * a
        dalog_ref[...] += jnp.sum(dadt * dt, axis=0, keepdims=True)
        dpre = ddt * _sigmoid(pre)
        ddtb_ref[...] += jnp.sum(dpre, axis=0, keepdims=True)
        ddt_ref[...] = dpre.astype(BF16)

        @pl.when(i == nc - 1)
        def _():
            dalog_ref[...] = dalog_ref[...] * a
            dsk = _dot_terms(_split3(jnp.broadcast_to(dskcol[...], (8, SSD_D_INNER))), et_ref[...])
            ddsk_ref[...] = jnp.max(dsk, axis=0, keepdims=True)

    def chunk(w):
        return pl.BlockSpec((CHUNK, w), lambda i: (nc - 1 - i, 0))

    def const(shape):
        return pl.BlockSpec(shape, lambda i: (0,) * len(shape))

    return pl.pallas_call(
        body, name=name, grid=(nc,),
        in_specs=[chunk(SSD_D_INNER), chunk(SSD_D_INNER), chunk(SSD_D_INNER), chunk(SSD_XBC), chunk(LANES),
                  pl.BlockSpec((1, SSD_STATE, SSD_D_INNER), lambda i: (nc - 1 - i, 0, 0)),
                  const((1, LANES)), const((1, LANES)), const((1, SSD_D_INNER)), const((1, SSD_D_INNER)),
                  const((1, SSD_D_INNER)), const((LANES, SSD_D_INNER)), const((SSD_D_INNER, LANES))],
        out_specs=[chunk(SSD_D_INNER), chunk(SSD_XBC), chunk(LANES), const((1, SSD_D_INNER)), const((1, LANES)),
                   const((1, LANES)), const((1, LANES))],
        out_shape=[jax.ShapeDtypeStruct((s, SSD_D_INNER), BF16), jax.ShapeDtypeStruct((s, SSD_XBC), F32),
                   jax.ShapeDtypeStruct((s, LANES), BF16), jax.ShapeDtypeStruct((1, SSD_D_INNER), F32),
                   jax.ShapeDtypeStruct((1, LANES), F32), jax.ShapeDtypeStruct((1, LANES), F32),
                   jax.ShapeDtypeStruct((1, LANES), F32)],
        scratch_shapes=[pltpu.VMEM((SSD_STATE, SSD_D_INNER), F32), pltpu.VMEM((CHUNK, SSD_D_INNER), F32),
                        pltpu.VMEM((1, SSD_D_INNER), F32)],
        compiler_params=_params(("arbitrary",)),
    )(dya, y, z, xbc, dt_raw, states, dt_bias, a_log, a_log_x, d_skip_x, norm_w, expand, expand_t)


GELU_K = math.sqrt(2.0 / math.pi)
GELU_C = 0.044715


def _gelu(x):
    return 0.5 * x * (1.0 + jnp.tanh(GELU_K * (x + GELU_C * x * x * x)))


def _gelu_grad(x):
    t = jnp.tanh(GELU_K * (x + GELU_C * x * x * x))
    return 0.5 * (1.0 + t) + 0.5 * x * (1.0 - t * t) * (GELU_K * (1.0 + 3.0 * GELU_C * x * x))


def _sgu_pre(uv_ref, uvb_ref, lnw_ref, lnb_ref):
    uv = uv_ref[...].astype(F32) + uvb_ref[...]
    guv = _gelu(uv)
    u = guv[:, :SGU_WIDTH]
    v = guv[:, SGU_WIDTH:]
    mu = jnp.mean(v, axis=-1, keepdims=True)
    vc = v - mu
    rstd = lax.rsqrt(jnp.mean(vc * vc, axis=-1, keepdims=True) + LN_EPS)
    vhat = vc * rstd
    vn = vhat * lnw_ref[...] + lnb_ref[...]
    return uv, u, vhat, rstd, vn


def _sgu_fwd(uv_raw, uv_b, ln_w, ln_b, w_sp, b_sp_t, *, name):
    s = uv_raw.shape[0]
    nc = s // CHUNK

    def body(uv_ref, uvb_ref, lnw_ref, lnb_ref, w_ref, bt_ref, o_ref):
        lower, _ = _tri_masks()
        _, u, _, _, vn = _sgu_pre(uv_ref, uvb_ref, lnw_ref, lnb_ref)
        vn_b = vn.astype(BF16)
        bt = bt_ref[...]
        for g in range(SGU_GROUPS):
            gs = slice(LANES * g, LANES * (g + 1))
            wc = jnp.where(lower, w_ref[g], 0.0).astype(BF16)
            mixed = _dot(wc, vn_b[:, gs]) + bt[:, g:g + 1]
            o_ref[:, gs] = (u[:, gs] * mixed).astype(BF16)

    def const(shape):
        return pl.BlockSpec(shape, lambda c: (0,) * len(shape))

    return pl.pallas_call(
        body, name=name, grid=(nc,),
        in_specs=[pl.BlockSpec((CHUNK, 2 * SGU_WIDTH), lambda c: (c, 0)), const((1, 2 * SGU_WIDTH)),
                  const((1, SGU_WIDTH)), const((1, SGU_WIDTH)), const((SGU_GROUPS, CHUNK, CHUNK)),
                  const((CHUNK, LANES))],
        out_specs=pl.BlockSpec((CHUNK, SGU_WIDTH), lambda c: (c, 0)),
        out_shape=jax.ShapeDtypeStruct((s, SGU_WIDTH), BF16),
        compiler_params=_params(("parallel",)),
    )(uv_raw, uv_b, ln_w, ln_b, w_sp, b_sp_t)


def _sgu_bwd(uv_raw, dyb, uv_b, ln_w, ln_b, w_sp, b_sp_t, group_sum, *, name):
    s = uv_raw.shape[0]
    nc = s // CHUNK

    def body(uv_ref, dy_ref, uvb_ref, lnw_ref, lnb_ref, w_ref, bt_ref, gsum_ref,
             duv_ref, dw_ref, dbt_ref, dlnw_ref, dlnb_ref, duvb_ref):
        @pl.when(pl.program_id(0) == 0)
        def _():
            dw_ref[...] = jnp.zeros_like(dw_ref)
            dbt_ref[...] = jnp.zeros_like(dbt_ref)
            dlnw_ref[...] = jnp.zeros_like(dlnw_ref)
            dlnb_ref[...] = jnp.zeros_like(dlnb_ref)
            duvb_ref[...] = jnp.zeros_like(duvb_ref)

        lower, _ = _tri_masks()
        uv, u, vhat, rstd, vn = _sgu_pre(uv_ref, uvb_ref, lnw_ref, lnb_ref)
        vn_b = vn.astype(BF16)
        bt = bt_ref[...]
        dy = dy_ref[...].astype(F32)
        du_parts, dvn_parts, dmix_parts = [], [], []
        for g in range(SGU_GROUPS):
            gs = slice(LANES * g, LANES * (g + 1))
            wc = jnp.where(lower, w_ref[g], 0.0).astype(BF16)
            mixed = _dot(wc, vn_b[:, gs]) + bt[:, g:g + 1]
            du_parts.append(dy[:, gs] * mixed)
            dmix = dy[:, gs] * u[:, gs]
            dmix_b = dmix.astype(BF16)
            dmix_parts.append(dmix)
            dw_ref[g] += jnp.where(lower, _dot(dmix_b, vn_b[:, gs], NT), 0.0)
            dvn_parts.append(_dot(wc, dmix_b, TN))
        dmixed = jnp.concatenate(dmix_parts, axis=1)
        dbt_ref[...] += _dot_terms(_split3(dmixed), gsum_ref[...])
        dvn = jnp.concatenate(dvn_parts, axis=1)
        dlnw_ref[...] += jnp.sum(dvn * vhat, axis=0, keepdims=True)
        dlnb_ref[...] += jnp.sum(dvn, axis=0, keepdims=True)
        dvhat = dvn * lnw_ref[...]
        dv = rstd * (dvhat - jnp.mean(dvhat, axis=-1, keepdims=True)
                     - vhat * jnp.mean(dvhat * vhat, axis=-1, keepdims=True))
        dguv = jnp.concatenate(du_parts + [dv], axis=1)
        duv = dguv * _gelu_grad(uv)
        duvb_ref[...] += jnp.sum(duv, axis=0, keepdims=True)
        duv_ref[...] = duv.astype(BF16)

    def const(shape):
        return pl.BlockSpec(shape, lambda c: (0,) * len(shape))

    return pl.pallas_call(
        body, name=name, grid=(nc,),
        in_specs=[pl.BlockSpec((CHUNK, 2 * SGU_WIDTH), lambda c: (c, 0)),
                  pl.BlockSpec((CHUNK, SGU_WIDTH), lambda c: (c, 0)), const((1, 2 * SGU_WIDTH)),
                  const((1, SGU_WIDTH)), const((1, SGU_WIDTH)), const((SGU_GROUPS, CHUNK, CHUNK)),
                  const((CHUNK, LANES)), const((SGU_WIDTH, LANES))],
        out_specs=[pl.BlockSpec((CHUNK, 2 * SGU_WIDTH), lambda c: (c, 0)), const((SGU_GROUPS, CHUNK, CHUNK)),
                   const((CHUNK, LANES)), const((1, SGU_WIDTH)), const((1, SGU_WIDTH)), const((1, 2 * SGU_WIDTH))],
        out_shape=[jax.ShapeDtypeStruct((s, 2 * SGU_WIDTH), BF16),
                   jax.ShapeDtypeStruct((SGU_GROUPS, CHUNK, CHUNK), F32), jax.ShapeDtypeStruct((CHUNK, LANES), F32),
                   jax.ShapeDtypeStruct((1, SGU_WIDTH), F32), jax.ShapeDtypeStruct((1, SGU_WIDTH), F32),
                   jax.ShapeDtypeStruct((1, 2 * SGU_WIDTH), F32)],
        compiler_params=_params(("arbitrary",)),
    )(uv_raw, dyb, uv_b, ln_w, ln_b, w_sp, b_sp_t, group_sum)


def _gate_fwd(gates_raw, b_gate, p_a, p_b, *, name, tm=512):
    s = p_a.shape[0]
    tm = min(tm, s)

    def body(ga_ref, gb_ref, ba_ref, bb_ref, pa_ref, pb_ref, o_ref):
        ga = _sigmoid(ga_ref[...].astype(F32) + ba_ref[...])
        gb = _sigmoid(gb_ref[...].astype(F32) + bb_ref[...])
        o_ref[...] = (ga * pa_ref[...].astype(F32) + gb * pb_ref[...].astype(F32)).astype(BF16)

    t_a = pl.BlockSpec((tm, D_MODEL), lambda i: (i, 0))
    t_b = pl.BlockSpec((tm, D_MODEL), lambda i: (i, 1))
    r_a = pl.BlockSpec((1, D_MODEL), lambda i: (0, 0))
    r_b = pl.BlockSpec((1, D_MODEL), lambda i: (0, 1))
    return pl.pallas_call(
        body, name=name, grid=(s // tm,),
        in_specs=[t_a, t_b, r_a, r_b, t_a, t_a], out_specs=t_a,
        out_shape=jax.ShapeDtypeStruct((s, D_MODEL), BF16),
        compiler_params=_params(("parallel",)),
    )(gates_raw, gates_raw, b_gate, b_gate, p_a, p_b)


def _gate_bwd(gates_raw, b_gate, p_a, p_b, dm, *, name, tm=512):
    s = p_a.shape[0]
    tm = min(tm, s)

    def body(ga_ref, gb_ref, ba_ref, bb_ref, pa_ref, pb_ref, dm_ref, dpa_ref, dpb_ref, dga_ref, dgb_ref,
             dba_ref, dbb_ref):
        @pl.when(pl.program_id(0) == 0)
        def _():
            dba_ref[...] = jnp.zeros_like(dba_ref)
            dbb_ref[...] = jnp.zeros_like(dbb_ref)

        d = dm_ref[...].astype(F32)
        for g_ref, b_ref, p_ref, dp_ref, dg_ref, db_ref in ((ga_ref, ba_ref, pa_ref, dpa_ref, dga_ref, dba_ref),
                                                            (gb_ref, bb_ref, pb_ref, dpb_ref, dgb_ref, dbb_ref)):
            sg = _sigmoid(g_ref[...].astype(F32) + b_ref[...])
            dp_ref[...] = (d * sg).astype(BF16)
            dg = d * p_ref[...].astype(F32) * (sg * (1.0 - sg))
            dg_ref[...] = dg.astype(BF16)
            db_ref[...] += jnp.sum(dg, axis=0, keepdims=True)

    t_a = pl.BlockSpec((tm, D_MODEL), lambda i: (i, 0))
    t_b = pl.BlockSpec((tm, D_MODEL), lambda i: (i, 1))
    r_a = pl.BlockSpec((1, D_MODEL), lambda i: (0, 0))
    r_b = pl.BlockSpec((1, D_MODEL), lambda i: (0, 1))
    big = jax.ShapeDtypeStruct((s, D_MODEL), BF16)
    row = jax.ShapeDtypeStruct((1, D_MODEL), F32)
    return pl.pallas_call(
        body, name=name, grid=(s // tm,),
        in_specs=[t_a, t_b, r_a, r_b, t_a, t_a, t_a], out_specs=[t_a, t_a, t_a, t_a, r_a, r_a],
        out_shape=[big, big, big, big, row, row],
        compiler_params=_params(("arbitrary",)),
    )(gates_raw, gates_raw, b_gate, b_gate, p_a, p_b, dm)


def _adamw_update(w_ref, g_ref, m_ref, v_ref, d_ref, mo_ref, vo_ref):
    gv = g_ref[...]
    mn = ADAM_B1 * m_ref[...] + (1.0 - ADAM_B1) * gv
    vn = ADAM_B2 * v_ref[...] + (1.0 - ADAM_B2) * (gv * gv)
    m_hat = mn / (1.0 - ADAM_B1 ** ADAM_STEP)
    v_hat = vn / (1.0 - ADAM_B2 ** ADAM_STEP)
    d_ref[...] = -ADAM_LR * (m_hat / (jnp.sqrt(v_hat) + ADAM_EPS) + ADAM_WD * w_ref[...])
    mo_ref[...] = mn
    vo_ref[...] = vn


def _adamw_many(ws, gs, ms, vs, *, name):
    n = len(ws)

    def body(*refs):
        for i in range(n):
            _adamw_update(*[refs[k * n + i] for k in range(7)])

    whole = pl.BlockSpec(memory_space=pltpu.VMEM)
    sds = [jax.ShapeDtypeStruct(w.shape, F32) for w in ws]
    outs = pl.pallas_call(
        body, name=name, in_specs=[whole] * (4 * n), out_specs=[whole] * (3 * n), out_shape=sds * 3,
        compiler_params=pltpu.CompilerParams(vmem_limit_bytes=VMEM_LIMIT),
    )(*ws, *gs, *ms, *vs)
    return outs[:n], outs[n:2 * n], outs[2 * n:]


def _adamw(w, g, m, v, *, name, tr=128):
    r, c = w.shape
    tr = min(tr, r)
    assert r % tr == 0, (name, r, tr)
    body = functools.partial(_adamw_update)

    blk = pl.BlockSpec((tr, c), lambda i: (i, 0))
    sds = jax.ShapeDtypeStruct((r, c), F32)
    return pl.pallas_call(
        body, name=name, grid=(r // tr,), in_specs=[blk] * 4, out_specs=[blk] * 3, out_shape=[sds] * 3,
        compiler_params=_params(("parallel",)),
    )(w, g, m, v)


def _adamw_two_sums(w, g_a, g_b, m, v, *, name, tr=128):
    r, c = w.shape
    tr = min(tr, r)
    assert r % tr == 0, (name, r, tr)

    def body(w_ref, ga_ref, gb_ref, m_ref, v_ref, g_ref, d_ref, mo_ref, vo_ref):
        g_ref[...] = ga_ref[...] + gb_ref[...]
        _adamw_update(w_ref, g_ref, m_ref, v_ref, d_ref, mo_ref, vo_ref)

    blk = pl.BlockSpec((tr, c), lambda i: (i, 0))
    sds = jax.ShapeDtypeStruct((r, c), F32)
    return pl.pallas_call(
        body, name=name, grid=(r // tr,), in_specs=[blk] * 5, out_specs=[blk] * 4, out_shape=[sds] * 4,
        compiler_params=_params(("parallel",)),
    )(w, g_a, g_b, m, v)


def _tile(n, pref):
    if n <= pref:
        return n
    best = LANES
    for t in range(LANES, pref + 1, LANES):
        if n % t == 0:
            best = t
    return best


MATMUL_BLOCK_BYTES = 20 * 1024 * 1024


def _mm(pairs, name, **kw):
    trans_b = kw.get("trans_b", False)
    m = (pairs[0][0][0] if isinstance(pairs[0][0], tuple) else pairs[0][0]).shape[0]
    ktot, n = 0, None
    for _, b in pairs:
        shape = b[0].shape[1:] if isinstance(b, tuple) else b.shape
        ktot += shape[1] if trans_b else shape[0]
        n = shape[0] if trans_b else shape[1]
    out_bytes = 4 * (2 if kw.get("add") is not None else 1)
    best = None
    for tm in (256, 512, 1024, 2048):
        for tn in range(LANES, min(n, 1536) + 1, LANES):
            if m % min(tm, m) or n % tn:
                continue
            fits = 2 * ktot * (min(tm, m) + tn) + out_bytes * min(tm, m) * tn <= MATMUL_BLOCK_BYTES
            if fits and (best is None or min(tm, m) * tn >= best[0] * best[1]):
                best = (min(tm, m), tn)
    return _matmul(pairs, tm=best[0], tn=best[1], name=name, **kw)


def _wgrad(a, b, name, **kw):
    return _matmul_tn(a, b, tk=_tile(a.shape[1], 1408), tn=kw.pop("tn", _tile(b.shape[1], 1024)), tm=2048,
                      name=name, **kw)


def _local_step(x, target, get_weight, small, emit_grad):
    heads = jnp.arange(SSD_D_INNER) // SSD_HEAD_DIM
    expand = (jnp.arange(LANES)[:, None] == heads[None, :]).astype(BF16)
    expand_t = expand.T
    group_sum = (jnp.arange(SGU_WIDTH)[:, None] // LANES == jnp.arange(LANES)[None, :]).astype(BF16)
    pad_h = LANES - SSD_HEADS
    dt_bias = jnp.pad(small["dt_bias"], ((0, 0), (0, pad_h)))
    a_log = jnp.pad(small["a_log"], ((0, 0), (0, pad_h)))
    a_log_x = jnp.repeat(small["a_log"], SSD_HEAD_DIM, axis=1)
    d_skip_x = jnp.repeat(small["d_skip"], SSD_HEAD_DIM, axis=1)
    b_sp_t = jnp.pad(small["b_spatial"][0].T, ((0, 0), (0, LANES - SGU_GROUPS)))
    w_sp = small["w_spatial"][0]
    conv_a_w = jnp.pad(small["conv_a_w"], ((0, 4), (0, 0)))
    conv_f_w = jnp.pad(small["conv_f_w"], ((0, 5), (0, 0)))
    final_w = small["final_norm_w"].reshape(1, D_MODEL)

    n1 = _rms_fwd(x, small["norm1_w"], after=small.get("gathers_started"), name="rms1_fwd")
    wts = dict(get_weight("w_in", n1))
    z = _mm([(n1, wts["in_z"])], "in_z")
    xbc_raw = _mm([(n1, wts["in_xbc"])], "in_xbc")
    dt_raw = _mm([(n1, wts["in_dt"])], "in_dt")
    uv_raw = _mm([(n1, wts["in_uv"])], "in_uv", out_dtype=BF16)
    gates_raw = _mm([(n1, wts["in_gate"])], "in_gate", out_dtype=BF16)
    xbc, xbc_pre = _conv_a_fwd(xbc_raw, conv_a_w, small["conv_a_b"], name="conv_a_fwd")
    y, y_a, states = _ssd_fwd(xbc, dt_raw, z, dt_bias, a_log, a_log_x, d_skip_x, small["ssd_norm_w"], expand,
                              name="ssd_fwd")
    y_b = _sgu_fwd(uv_raw, small["uv_b"], small["v_ln_w"], small["v_ln_b"], w_sp, b_sp_t, name="sgu_fwd")
    wts.update(get_weight("w_branch", y_b))
    p_a = _mm([(y_a, wts["branch_a"])], "branch_a", out_dtype=BF16)
    p_b = _mm([(y_b, wts["branch_b"])], "branch_b", out_dtype=BF16)
    mix = _gate_fwd(gates_raw, small["b_gate"], p_a, p_b, name="gate_fwd")
    wts.update(get_weight("w_out", mix))
    h1 = _mm([(mix, wts["out"])], "out_proj", add=x)
    n2 = _rms_fwd(h1, small["norm2_w"], name="rms2_fwd")
    wts.update(get_weight("w_up", n2))
    up_w = wts["up"]
    up_cols = up_w.shape[2]
    up_raw = _matmul([(n2, (up_w, "cols"))], tm=2048, tn=up_cols, out_dtype=BF16, name="up_proj")
    act, up_a, up_v = _conv_f_fwd(up_raw, conv_f_w, small["conv_f_b"], name="conv_f_fwd")
    wts.update(get_weight("w_down", act))
    h2 = _mm([(act, wts["down"])], "down_proj", add=h1)
    loss, dh2, dh2_b, d_final = _final_fwd_bwd(h2, final_w, target, name="final_norm_loss")

    dact = _mm([(dh2_b, wts["down"])], "down_dgrad", trans_b=True)
    started = emit_grad("w_down", _wgrad(act, dh2_b, "down_wgrad"))
    dup_a, dup_v, dwf_a, dwf_v, dbf_a, dbf_v = _conv_f_bwd(up_raw, up_a, up_v, conv_f_w, dact, name="conv_f_bwd")
    dn2 = _mm([((dup_a, 0), (up_w, 0)), ((dup_a, 1), (up_w, 1)), ((dup_v, 0), (up_w, 2)), ((dup_v, 1), (up_w, 3))],
              "up_dgrad", trans_b=True, after=started, out_dtype=BF16)
    started = emit_grad("w_up", jnp.concatenate([_wgrad(n2, dup_a, "up_wgrad_a", tn=up_cols, stack_out=True),
                                                 _wgrad(n2, dup_v, "up_wgrad_v", tn=up_cols, stack_out=True)], axis=0))
    dh1, dh1_b, d_norm2 = _rms_bwd(h1, small["norm2_w"], dn2, dh2, name="rms2_bwd")
    dmix = _mm([(dh1_b, wts["out"])], "out_dgrad", trans_b=True, after=started, out_dtype=BF16)
    started = emit_grad("w_out", _wgrad(mix, dh1_b, "out_wgrad"))
    dp_a, dp_b, dg_a, dg_b, dbg_a, dbg_b = _gate_bwd(gates_raw, small["b_gate"], p_a, p_b, dmix, name="gate_bwd")
    dya = _mm([(dp_a, wts["branch_a"])], "branch_a_dgrad", trans_b=True, after=started)
    dyb = _mm([(dp_b, wts["branch_b"])], "branch_b_dgrad", trans_b=True, out_dtype=BF16)
    started_branch = emit_grad("w_branch", jnp.concatenate([_wgrad(y_a, dp_a, "branch_a_wgrad"),
                                                            _wgrad(y_b, dp_b, "branch_b_wgrad")], axis=0))
    duv, d_wsp, d_bsp_t, d_lnw, d_lnb, d_uvb = _sgu_bwd(uv_raw, dyb, small["uv_b"], small["v_ln_w"],
                                                        small["v_ln_b"], w_sp, b_sp_t, group_sum, name="sgu_bwd")
    dz, dxbc, ddt, d_ssd_nw, d_dskip, d_alog, d_dtb = _ssd_bwd(
        dya, y, z, xbc, dt_raw, states, dt_bias, a_log, a_log_x, d_skip_x, small["ssd_norm_w"], expand, expand_t,
        name="ssd_bwd")
    dxbc_raw, d_conv_a_w, d_conv_a_b = _conv_a_bwd(xbc_raw, xbc_pre, conv_a_w, dxbc, name="conv_a_bwd")
    started = emit_grad("w_in", {
        "in_z": _wgrad(n1, dz, "in_z_wgrad", after=started_branch), "in_xbc": _wgrad(n1, dxbc_raw, "in_xbc_wgrad"),
        "in_dt": _wgrad(n1, ddt, "in_dt_wgrad")[:, :SSD_HEADS], "in_uv": _wgrad(n1, duv, "in_uv_wgrad"),
        "in_gate_a": _wgrad(n1, dg_a, "in_gate_a_wgrad"), "in_gate_b": _wgrad(n1, dg_b, "in_gate_b_wgrad")})
    dn1 = _mm([(dz, wts["in_z"]), (dxbc_raw, wts["in_xbc"]), (ddt, wts["in_dt"]), (duv, wts["in_uv"]),
               (dg_a, wts["in_gate_a"]), (dg_b, wts["in_gate_b"])], "in_dgrad", trans_b=True, after=started,
              out_dtype=BF16)
    dx, _, d_norm1 = _rms_bwd(x, small["norm1_w"], dn1, dh1, name="rms1_bwd")

    grads_small = {
        "norm1_w": d_norm1, "b_gate": jnp.concatenate([dbg_a, dbg_b], axis=1),
        "conv_a_w": d_conv_a_w[:4], "conv_a_b": d_conv_a_b,
        "dt_bias": d_dtb[:, :SSD_HEADS], "a_log": d_alog[:, :SSD_HEADS], "d_skip": d_dskip[:, :SSD_HEADS],
        "ssd_norm_w": d_ssd_nw, "uv_b": d_uvb, "v_ln_w": d_lnw, "v_ln_b": d_lnb,
        "w_spatial": d_wsp[None], "b_spatial": d_bsp_t[:, :SGU_GROUPS].T[None],
        "norm2_w": d_norm2, "conv_f_w": jnp.concatenate([dwf_a[:3], dwf_v[:3]], axis=1),
        "conv_f_b": jnp.concatenate([dbf_a, dbf_v], axis=1), "final_norm_w": d_final.reshape(D_MODEL),
    }
    return loss, dx, grads_small


HBM = pl.BlockSpec(memory_space=pl.ANY)
MESH = pl.DeviceIdType.MESH


def _mesh_pos():
    return lax.axis_index("x"), lax.axis_index("y"), lax.axis_index("c")


def _other_chips(x, y):
    return [(1 - x, y), (x, 1 - y), (1 - x, 1 - y)]


def _remote(src, dst, send_sems, recv_sems, k, dev):
    return pltpu.make_async_remote_copy(src_ref=src, dst_ref=dst, send_sem=send_sems.at[k], recv_sem=recv_sems.at[k],
                                        device_id=dev, device_id_type=MESH)


def _dma_sems(n):
    return [pltpu.SemaphoreType.DMA((n,)), pltpu.SemaphoreType.DMA((n,))]


HBM_ONLY = pl.BlockSpec(memory_space=pltpu.HBM)
SEMAPHORES = pl.BlockSpec(memory_space=pltpu.SEMAPHORE)
DATAFLOW_EFFECT = pltpu.SideEffectType.DATAFLOW_SIDE_EFFECTING
N_PEER_CHIPS = N_CHIPS - 1


def _gather_sends(w_ref, land_ref, send_sems, recv_sems):
    x, y, c = _mesh_pos()
    return [_remote(w_ref.at[c], land_ref.at[2 * x + y, c], send_sems, recv_sems, k, (px, py, c))
            for k, (px, py) in enumerate(_other_chips(x, y))]


def _gather_arrivals(w_ref, land_ref, send_sems, recv_sems):
    x, y, c = _mesh_pos()
    return [_remote(w_ref.at[c], land_ref.at[2 * px + py, c], send_sems, recv_sems, k, (px, py, c))
            for k, (px, py) in enumerate(_other_chips(x, y))]


def _gather_whole_sends(w_ref, land_ref, send_sems, recv_sems):
    x, y, c = _mesh_pos()
    return [_remote(w_ref, land_ref.at[2 * x + y], send_sems, recv_sems, k, (px, py, c))
            for k, (px, py) in enumerate(_other_chips(x, y))]


def _gather_whole_arrivals(w_ref, land_ref, send_sems, recv_sems):
    x, y, c = _mesh_pos()
    return [_remote(w_ref, land_ref.at[2 * px + py], send_sems, recv_sems, k, (px, py, c))
            for k, (px, py) in enumerate(_other_chips(x, y))]


def _scatter_sends(h_ref, land_ref, send_sems, recv_sems):
    x, y, c = _mesh_pos()
    return [_remote(h_ref.at[2 * px + py], land_ref.at[2 * x + y], send_sems, recv_sems, k, (px, py, c))
            for k, (px, py) in enumerate(_other_chips(x, y))]


def _scatter_arrivals(h_ref, land_ref, send_sems, recv_sems):
    x, y, c = _mesh_pos()
    return [_remote(h_ref.at[2 * x + y], land_ref.at[2 * px + py], send_sems, recv_sems, k, (px, py, c))
            for k, (px, py) in enumerate(_other_chips(x, y))]


def _exchange_wait_many(pendings, after, sends, arrivals, *, name):
    n = len(pendings)

    def body(*refs):
        for i in range(n):
            src_ref, land_ref, send_ref, recv_ref = refs[i], refs[n + i], refs[2 * n + i], refs[3 * n + i]
            for cp in sends(src_ref, land_ref, send_ref, recv_ref):
                cp.wait_send()
            for cp in arrivals(src_ref, land_ref, send_ref, recv_ref):
                cp.wait_recv()

    sources = [p[2] for p in pendings]
    landings = [p[3] for p in pendings]
    outs = pl.pallas_call(
        body, name=name,
        out_shape=tuple(pltpu.HBM(a.shape, a.dtype) for a in sources + landings),
        in_specs=[HBM_ONLY] * (2 * n) + [SEMAPHORES] * (2 * n) + [pl.BlockSpec(memory_space=pl.ANY)],
        out_specs=tuple([HBM_ONLY] * (2 * n)), input_output_aliases={i: i for i in range(2 * n)},
        compiler_params=pltpu.CompilerParams(has_side_effects=DATAFLOW_EFFECT),
    )(*sources, *landings, *[p[0] for p in pendings], *[p[1] for p in pendings], after)
    return [(outs[i], outs[n + i]) for i in range(n)]


def _sibling_sends(src_ref, land_ref, send_sems, recv_sems):
    x, y, c = _mesh_pos()
    return [_remote(src_ref, land_ref, send_sems, recv_sems, 0, (x, y, 1 - c))]


def _exchange_start(sources, landing_shapes, sends, *, after=None, name):
    n = len(sources)
    extra = [] if after is None else [after]

    def body(*refs):
        sems = refs[2 * n + len(extra):4 * n + len(extra)]
        for i in range(n):
            send_i = sends[i] if isinstance(sends, (list, tuple)) else sends
            for cp in send_i(refs[i], refs[n + i], sems[2 * i], sems[2 * i + 1]):
                cp.start()
        refs[-1][...] = jnp.zeros_like(refs[-1])

    hbm = [pltpu.HBM(s.shape, s.dtype) for s in sources] + [pltpu.HBM(shp, s.dtype)
                                                             for shp, s in zip(landing_shapes, sources)]
    outs = pl.pallas_call(
        body, name=name,
        out_shape=tuple([pltpu.SemaphoreType.DMA((N_PEER_CHIPS,))] * (2 * n) + hbm
                        + [jax.ShapeDtypeStruct((8, LANES), F32)]),
        in_specs=[HBM_ONLY] * (2 * n) + [pl.BlockSpec(memory_space=pl.ANY)] * len(extra),
        out_specs=tuple([SEMAPHORES] * (2 * n) + [HBM_ONLY] * (2 * n) + [pl.BlockSpec(memory_space=pltpu.VMEM)]),
        input_output_aliases={i: 2 * n + i for i in range(2 * n)},
        compiler_params=pltpu.CompilerParams(has_side_effects=DATAFLOW_EFFECT),
    )(*[pltpu.with_memory_space_constraint(s, pltpu.HBM) for s in sources],
      *[pltpu.with_memory_space_constraint(lax.empty(shp, s.dtype), pltpu.HBM)
        for shp, s in zip(landing_shapes, sources)], *extra)
    pending = [(outs[2 * i], outs[2 * i + 1], outs[2 * n + i], outs[3 * n + i]) for i in range(n)]
    return pending, outs[-1]


def _exchange_wait(pending, after, sends, arrivals, *, name):
    send_sems, recv_sems, source, landing = pending

    def body(src_ref, land_ref, send_ref, recv_ref, after_ref, src_out, land_out):
        for cp in sends(src_ref, land_ref, send_ref, recv_ref):
            cp.wait_send()
        for cp in arrivals(src_ref, land_ref, send_ref, recv_ref):
            cp.wait_recv()

    return pl.pallas_call(
        body, name=name,
        out_shape=(pltpu.HBM(source.shape, source.dtype), pltpu.HBM(landing.shape, landing.dtype)),
        in_specs=[HBM_ONLY, HBM_ONLY, SEMAPHORES, SEMAPHORES, pl.BlockSpec(memory_space=pl.ANY)],
        out_specs=(HBM_ONLY, HBM_ONLY), input_output_aliases={0: 0, 1: 1},
        compiler_params=pltpu.CompilerParams(has_side_effects=DATAFLOW_EFFECT),
    )(source, landing, send_sems, recv_sems, after)


def _gather_ici(shard, *, name):
    _, rh, cols = shard.shape

    def body(w_ref, o_ref, send_sems, recv_sems):
        x, y, c = _mesh_pos()
        mine = 2 * x + y
        sends = []
        for k, (px, py) in enumerate(_other_chips(x, y)):
            cp = _remote(w_ref.at[c], o_ref.at[mine, c], send_sems, recv_sems, k, (px, py, c))
            cp.start()
            sends.append(cp)
        for k, (px, py) in enumerate(_other_chips(x, y)):
            _remote(w_ref.at[c], o_ref.at[2 * px + py, c], send_sems, recv_sems, k, (px, py, c)).wait_recv()
        for cp in sends:
            cp.wait_send()

    return pl.pallas_call(
        body, name=name, in_specs=[HBM], out_specs=HBM,
        out_shape=jax.ShapeDtypeStruct((N_CHIPS, 2, rh, cols), shard.dtype), scratch_shapes=_dma_sems(3),
    )(shard)


def _gather_d2d(parts, *, name):
    def body(a_ref, o_ref, send_sems, recv_sems):
        x, y, c = _mesh_pos()
        sibling = (x, y, 1 - c)
        sends = []
        for k, (px, py) in enumerate(_other_chips(x, y)):
            cp = _remote(a_ref.at[2 * px + py, c], o_ref.at[2 * px + py, c], send_sems, recv_sems, k, sibling)
            cp.start()
            sends.append(cp)
        for k, (px, py) in enumerate(_other_chips(x, y)):
            _remote(a_ref.at[2 * px + py, c], o_ref.at[2 * px + py, 1 - c], send_sems, recv_sems, k, sibling).wait_recv()
        for cp in sends:
            cp.wait_send()

    return pl.pallas_call(
        body, name=name, in_specs=[HBM], out_specs=HBM,
        out_shape=jax.ShapeDtypeStruct(parts.shape, parts.dtype),
        input_output_aliases={0: 0}, scratch_shapes=_dma_sems(3),
    )(parts)


def _all_gather_chips(shard_flat, name):
    rows, cols = shard_flat.shape
    parts = _gather_ici(shard_flat.reshape(2, rows // 2, cols), name=name + "_ici")
    others = _gather_d2d(parts, name=name + "_d2d").reshape(N_CHIPS, rows, cols)
    chip = 2 * lax.axis_index("x") + lax.axis_index("y")
    return lax.dynamic_update_slice(others, shard_flat[None], (chip, 0, 0))


def _row_tile(rows, mult, cap):
    best = mult
    for t in range(mult, min(rows, cap) + 1, mult):
        if rows % t == 0:
            best = t
    assert rows % best == 0, (rows, mult)
    return best


def _swap_halves_d2d(g, *, after=None, name):
    _, _, rh, cols = g.shape
    extra = [] if after is None else [after]

    def body(g_ref, *rest):
        o_ref, send_sems, recv_sems = rest[len(extra):]
        x, y, c = _mesh_pos()
        sibling = (x, y, 1 - c)
        sends = []
        for s in range(N_CHIPS):
            cp = _remote(g_ref.at[s, 1 - c], o_ref.at[s], send_sems, recv_sems, s, sibling)
            cp.start()
            sends.append(cp)
        for s in range(N_CHIPS):
            _remote(g_ref.at[s, c], o_ref.at[s], send_sems, recv_sems, s, sibling).wait_recv()
        for cp in sends:
            cp.wait_send()

    return pl.pallas_call(
        body, name=name, in_specs=[HBM] * (1 + len(extra)), out_specs=HBM,
        out_shape=jax.ShapeDtypeStruct((N_CHIPS, rh, cols), g.dtype), scratch_shapes=_dma_sems(N_CHIPS),
    )(g, *extra)


def _add_own_half(g, arrived, core, *, name):
    _, _, rh, cols = g.shape
    mult = 16 if g.dtype == BF16 else 8
    tr = _row_tile(rh, mult, max(mult, (512 * 1024) // cols))

    def body(core_ref, g_ref, a_ref, o_ref):
        o_ref[...] = (g_ref[0].astype(F32) + a_ref[...].astype(F32)).astype(o_ref.dtype)

    grid_spec = pltpu.PrefetchScalarGridSpec(
        num_scalar_prefetch=1, grid=(N_CHIPS, rh // tr),
        in_specs=[pl.BlockSpec((1, 1, tr, cols), lambda s, i, core_ref: (s, core_ref[0], i, 0)),
                  pl.BlockSpec((1, tr, cols), lambda s, i, core_ref: (s, i, 0))],
        out_specs=pl.BlockSpec((1, tr, cols), lambda s, i, core_ref: (s, i, 0)))
    return pl.pallas_call(
        body, name=name, grid_spec=grid_spec, out_shape=jax.ShapeDtypeStruct((N_CHIPS, rh, cols), g.dtype),
        compiler_params=_params(("parallel", "parallel")),
    )(core, g, arrived)


def _scatter_ici(h, *, name):
    def body(h_ref, o_ref, send_sems, recv_sems):
        x, y, c = _mesh_pos()
        mine = 2 * x + y
        sends = []
        for k, (px, py) in enumerate(_other_chips(x, y)):
            cp = _remote(h_ref.at[2 * px + py], o_ref.at[mine], send_sems, recv_sems, k, (px, py, c))
            cp.start()
            sends.append(cp)
        for k, (px, py) in enumerate(_other_chips(x, y)):
            _remote(h_ref.at[mine], o_ref.at[2 * px + py], send_sems, recv_sems, k, (px, py, c)).wait_recv()
        for cp in sends:
            cp.wait_send()

    others = pl.pallas_call(
        body, name=name, in_specs=[HBM], out_specs=HBM, out_shape=jax.ShapeDtypeStruct(h.shape, h.dtype),
        scratch_shapes=_dma_sems(3),
    )(h)
    chip = 2 * lax.axis_index("x") + lax.axis_index("y")
    own = lax.dynamic_slice_in_dim(h, chip, 1, axis=0)
    return lax.dynamic_update_slice(others, own, (chip, 0, 0))


def _sum_chips(parts, *, name):
    _, rh, cols = parts.shape
    mult = 16 if parts.dtype == BF16 else 8
    tr = _row_tile(rh, mult, max(mult, (512 * 1024) // cols))

    def body(p_ref, o_ref):
        acc = p_ref[0].astype(F32)
        for s in range(1, N_CHIPS):
            acc = acc + p_ref[s].astype(F32)
        o_ref[...] = acc

    return pl.pallas_call(
        body, name=name, grid=(rh // tr,),
        in_specs=[pl.BlockSpec((N_CHIPS, tr, cols), lambda i: (0, i, 0))],
        out_specs=pl.BlockSpec((tr, cols), lambda i: (i, 0)),
        out_shape=jax.ShapeDtypeStruct((rh, cols), F32), compiler_params=_params(("parallel",)),
    )(parts)


def _sum_chips_with_own(landed, sent, chip, *, name):
    _, rh, cols = landed.shape
    mult = 16 if landed.dtype == BF16 else 8
    tr = _row_tile(rh, mult, max(mult, (512 * 1024) // cols))

    def body(chip_ref, own_ref, px_ref, py_ref, pxy_ref, o_ref):
        acc = own_ref[0].astype(F32)
        for p_ref in (px_ref, py_ref, pxy_ref):
            acc = acc + p_ref[0].astype(F32)
        o_ref[...] = acc

    def block_of(flip):
        return pl.BlockSpec((1, tr, cols), lambda i, chip_ref: (chip_ref[0] ^ flip, i, 0))

    grid_spec = pltpu.PrefetchScalarGridSpec(
        num_scalar_prefetch=1, grid=(rh // tr,),
        in_specs=[block_of(0), block_of(2), block_of(1), block_of(3)],
        out_specs=pl.BlockSpec((tr, cols), lambda i, chip_ref: (i, 0)))
    return pl.pallas_call(
        body, name=name, grid_spec=grid_spec, out_shape=jax.ShapeDtypeStruct((rh, cols), F32),
        compiler_params=_params(("parallel",)),
    )(chip, sent, landed, landed, landed)


def _share_d2d(f, *, name):
    fs = f if isinstance(f, (list, tuple)) else [f]
    others = _swap_with_sibling(fs, name=name)
    first = lax.axis_index("c") == 0
    both = [jnp.stack([jnp.where(first, a, b), jnp.where(first, b, a)]) for a, b in zip(fs, others)]
    return both if isinstance(f, (list, tuple)) else both[0]


def _swap_with_sibling(fs, *, name):
    n = len(fs)

    def body(*refs):
        x, y, c = _mesh_pos()
        sibling = (x, y, 1 - c)
        send_sems, recv_sems = refs[2 * n:]
        copies = [_remote(refs[i], refs[n + i], send_sems, recv_sems, i, sibling) for i in range(n)]
        for cp in copies:
            cp.start()
        for cp in copies:
            cp.wait()

    return pl.pallas_call(
        body, name=name, in_specs=[HBM] * n, out_specs=[HBM] * n,
        out_shape=[jax.ShapeDtypeStruct(a.shape, a.dtype) for a in fs], scratch_shapes=_dma_sems(n),
    )(*fs)


def _reduce_scatter_chips(g, core, name, after=None):
    _, rows, cols = g.shape
    g = g.reshape(N_CHIPS, 2, rows // 2, cols)
    arrived = _swap_halves_d2d(g, after=after, name=name + "_swap")
    chip_sum = _add_own_half(g, arrived, core, name=name + "_add2")
    parts = _scatter_ici(chip_sum, name=name + "_ici")
    total = _sum_chips(parts, name=name + "_sum4")
    return _share_d2d(total, name=name + "_share").reshape(rows, cols)


BIG = ("w_in", "w_branch", "w_out", "w_up", "w_down")
BIG_COLUMN_SHARDED = ("w_in", "w_up")
CONV = ("conv_a_w", "conv_f_w")
REPLICATED = ("norm1_w", "b_gate", "conv_a_b", "dt_bias", "a_log", "d_skip", "ssd_norm_w", "uv_b", "v_ln_w",
              "v_ln_b", "w_spatial", "b_spatial", "norm2_w", "conv_f_b", "final_norm_w")
WEIGHT_ORDER = ("norm1_w", "w_in", "b_gate", "conv_a_w", "conv_a_b", "dt_bias", "a_log", "d_skip", "ssd_norm_w",
                "uv_b", "v_ln_w", "v_ln_b", "w_spatial", "b_spatial", "w_branch", "w_out", "norm2_w", "w_up",
                "conv_f_w", "conv_f_b", "w_down", "final_norm_w")
SMALL_EXCHANGE_ROWS = 64


_GATE0 = SSD_IN + 2 * SGU_WIDTH
IN_SEGMENTS = {
    "in_z": (0, SSD_D_INNER), "in_xbc": (SSD_D_INNER, SSD_D_INNER + SSD_XBC), "in_dt": (SSD_D_INNER + SSD_XBC, SSD_IN),
    "in_uv": (SSD_IN, _GATE0), "in_gate": (_GATE0, IN_COLS), "in_gate_a": (_GATE0, _GATE0 + D_MODEL),
    "in_gate_b": (_GATE0 + D_MODEL, IN_COLS),
}
IN_GRAD_SEGMENTS = ("in_z", "in_xbc", "in_dt", "in_uv", "in_gate_a", "in_gate_b")


def _take_columns(parts, start, stop):
    out = []
    for a, first in parts:
        lo, hi = max(start, first), min(stop, first + a.shape[1])
        if lo < hi:
            out.append(a[:, lo - first:hi - first])
    return out[0] if len(out) == 1 else jnp.concatenate(out, axis=1)


def _flat_rows(arrays, row_multiple):
    flat = jnp.concatenate([a.reshape(-1) for a in arrays])
    rows = -(-flat.shape[0] // (LANES * row_multiple)) * row_multiple
    return jnp.pad(flat, (0, rows * LANES - flat.shape[0])).reshape(rows, LANES)


def _unflatten(flat, shapes):
    flat = flat.reshape(-1)
    out, off = [], 0
    for shp in shapes:
        n = math.prod(shp)
        out.append(flat[off:off + n].reshape(shp))
        off += n
    return out


def _from_chip_blocks(blocks, name):
    if name in BIG_COLUMN_SHARDED or name in CONV:
        k = blocks.shape[1]
        return jnp.transpose(blocks, (1, 0, 2)).reshape(k, -1)
    return blocks.reshape(-1, blocks.shape[-1])


def _to_chip_blocks(whole, name):
    if name in BIG_COLUMN_SHARDED or name in CONV:
        k, n = whole.shape
        return jnp.transpose(whole.reshape(k, N_CHIPS, n // N_CHIPS), (1, 0, 2))
    return whole.reshape(N_CHIPS, whole.shape[0] // N_CHIPS, whole.shape[1])


def kernel(x, norm1_w, w_in, b_gate, conv_a_w, conv_a_b, dt_bias, a_log, d_skip, ssd_norm_w, uv_b, v_ln_w, v_ln_b, w_spatial, b_spatial, w_branch, w_out, norm2_w, w_up, conv_f_w, conv_f_b, w_down, final_norm_w, loss_target, m_norm1_w, m_w_in, m_b_gate, m_conv_a_w, m_conv_a_b, m_dt_bias, m_a_log, m_d_skip, m_ssd_norm_w, m_uv_b, m_v_ln_w, m_v_ln_b, m_w_spatial, m_b_spatial, m_w_branch, m_w_out, m_norm2_w, m_w_up, m_conv_f_w, m_conv_f_b, m_w_down, m_final_norm_w, v_norm1_w, v_w_in, v_b_gate, v_conv_a_w, v_conv_a_b, v_dt_bias, v_a_log, v_d_skip, v_ssd_norm_w, v_uv_b, v_v_ln_w, v_v_ln_b, v_w_spatial, v_b_spatial, v_w_branch, v_w_out, v_norm2_w, v_w_up, v_conv_f_w, v_conv_f_b, v_w_down, v_final_norm_w):
    weights = dict(norm1_w=norm1_w, w_in=w_in, b_gate=b_gate, conv_a_w=conv_a_w, conv_a_b=conv_a_b, dt_bias=dt_bias,
                   a_log=a_log, d_skip=d_skip, ssd_norm_w=ssd_norm_w, uv_b=uv_b, v_ln_w=v_ln_w, v_ln_b=v_ln_b,
                   w_spatial=w_spatial, b_spatial=b_spatial, w_branch=w_branch, w_out=w_out, norm2_w=norm2_w,
                   w_up=w_up, conv_f_w=conv_f_w, conv_f_b=conv_f_b, w_down=w_down, final_norm_w=final_norm_w)
    mom1 = dict(norm1_w=m_norm1_w, w_in=m_w_in, b_gate=m_b_gate, conv_a_w=m_conv_a_w, conv_a_b=m_conv_a_b,
                dt_bias=m_dt_bias, a_log=m_a_log, d_skip=m_d_skip, ssd_norm_w=m_ssd_norm_w, uv_b=m_uv_b,
                v_ln_w=m_v_ln_w, v_ln_b=m_v_ln_b, w_spatial=m_w_spatial, b_spatial=m_b_spatial, w_branch=m_w_branch,
                w_out=m_w_out, norm2_w=m_norm2_w, w_up=m_w_up, conv_f_w=m_conv_f_w, conv_f_b=m_conv_f_b,
                w_down=m_w_down, final_norm_w=m_final_norm_w)
    mom2 = dict(norm1_w=v_norm1_w, w_in=v_w_in, b_gate=v_b_gate, conv_a_w=v_conv_a_w, conv_a_b=v_conv_a_b,
                dt_bias=v_dt_bias, a_log=v_a_log, d_skip=v_d_skip, ssd_norm_w=v_ssd_norm_w, uv_b=v_uv_b,
                v_ln_w=v_v_ln_w, v_ln_b=v_v_ln_b, w_spatial=v_w_spatial, b_spatial=v_b_spatial, w_branch=v_w_branch,
                w_out=v_w_out, norm2_w=v_norm2_w, w_up=v_w_up, conv_f_w=v_conv_f_w, conv_f_b=v_conv_f_b,
                w_down=v_w_down, final_norm_w=v_final_norm_w)
    chip = 2 * lax.axis_index("x") + lax.axis_index("y")
    core = lax.axis_index("c").astype(jnp.int32).reshape(1)

    whole = {}
    conv_shapes = [weights[n].shape[1:] for n in CONV]
    conv_gathered = _all_gather_chips(_flat_rows([weights[n] for n in CONV], 16), "gather_conv").reshape(N_CHIPS, -1)
    off = 0
    for n, shp in zip(CONV, conv_shapes):
        size = math.prod(shp)
        whole[n] = _from_chip_blocks(conv_gathered[:, off:off + size].reshape((N_CHIPS,) + shp), n)
        off += size
    shard_shapes = {n: weights[n].shape[1:] for n in BIG}
    halves = [weights[n][0].astype(BF16).reshape(2, shard_shapes[n][0] // 2, shard_shapes[n][1]) for n in BIG]
    sends = [_gather_sends if n == "w_in" else _gather_whole_sends for n in BIG]
    gathers, gathers_started = _exchange_start(halves, [(N_CHIPS,) + h.shape for h in halves], sends,
                                               after=conv_gathered, name="gather_start")
    gathers = dict(zip(BIG, gathers))

    def get_weight(name, after):
        rows, cols = shard_shapes[name]
        if name == "w_in":
            own, landed = _exchange_wait(gathers[name], after, _gather_sends, _gather_arrivals,
                                         name="gather_" + name + "_wait")
            landed = _gather_d2d(landed, name="gather_" + name + "_d2d")
        else:
            own, landed = _exchange_wait(gathers[name], after, _gather_whole_sends, _gather_whole_arrivals,
                                         name="gather_" + name + "_wait")
        blocks = lax.dynamic_update_slice(landed.reshape(N_CHIPS, rows, cols), own.reshape(1, rows, cols),
                                          (chip, 0, 0))
        if name == "w_up":
            return {"up": blocks}
        if name == "w_in":
            parts = [(blocks[k], cols * k) for k in range(N_CHIPS)]
            segs = {n: _take_columns(parts, a, b) for n, (a, b) in IN_SEGMENTS.items()}
            segs["in_dt"] = jnp.pad(segs["in_dt"], ((0, 0), (0, LANES - SSD_HEADS)))
            return segs
        full = _from_chip_blocks(blocks, name)
        if name == "w_branch":
            return {"branch_a": full[:SSD_D_INNER], "branch_b": full[SSD_D_INNER:]}
        return {name[2:]: full}

    small = {n: weights[n] for n in REPLICATED}
    small["conv_a_w"] = whole["conv_a_w"]
    small["conv_f_w"] = whole["conv_f_w"]
    small["gathers_started"] = gathers_started

    reductions = {}

    def emit_grad(name, g):
        if name == "w_in":
            parts = [(g[n], IN_SEGMENTS[n][0]) for n in IN_GRAD_SEGMENTS]
            cols = shard_shapes[name][1]
            g_blocks = jnp.stack([_take_columns(parts, cols * k, cols * (k + 1)) for k in range(N_CHIPS)])
        else:
            g_blocks = g if name == "w_up" else _to_chip_blocks(g, name)
        if name == "w_in":
            _, rows, cols = g_blocks.shape
            g_halves = g_blocks.reshape(N_CHIPS, 2, rows // 2, cols)
            arrived = _swap_halves_d2d(g_halves, name="reduce_" + name + "_swap")
            g_blocks = _add_own_half(g_halves, arrived, core, name="reduce_" + name + "_add2")
        (pending,), started = _exchange_start([g_blocks], [g_blocks.shape], _scatter_sends,
                                              name="reduce_" + name + "_start")
        reductions[name] = pending
        return started

    loss, dx, grads_small = _local_step(x[0], loss_target[0], get_weight, small, emit_grad)

    order = ("w_down", "w_up", "w_out", "w_branch", "w_in")
    core_sums = []
    chip_index = chip.astype(jnp.int32).reshape(1)
    for n in order:
        sent, landed = _exchange_wait(reductions[n], dx, _scatter_sends, _scatter_arrivals,
                                      name="reduce_" + n + "_wait")
        core_sums.append(_sum_chips_with_own(landed, sent, chip_index, name="reduce_" + n + "_sum4"))
    swaps, swaps_started = _exchange_start(core_sums, [a.shape for a in core_sums], _sibling_sends, name="reduce_swap_start")
    grads = {}

    small_names = REPLICATED + CONV + ("loss",)
    grads_small = dict(grads_small, loss=loss)
    small_shapes = [grads_small[n].shape for n in small_names]
    g_small = _flat_rows([grads_small[n] for n in small_names], N_CHIPS * 2 * SMALL_EXCHANGE_ROWS)
    red_small = _reduce_scatter_chips(g_small.reshape(N_CHIPS, -1, LANES), core, "reduce_small", after=swaps_started)
    all_small = _all_gather_chips(red_small, "gather_small")
    swapped = _exchange_wait_many(swaps, all_small, _sibling_sends, _sibling_sends, name="reduce_swap_wait")
    core_sums = {n: own for n, (own, _) in zip(order, swapped)}
    sibling_sums = {n: other for n, (_, other) in zip(order, swapped)}
    first = lax.axis_index("c") == 0
    w_in_halves = (core_sums["w_in"], sibling_sums["w_in"])
    w_in_grad = jnp.concatenate([jnp.where(first, w_in_halves[0], w_in_halves[1]),
                                 jnp.where(first, w_in_halves[1], w_in_halves[0])], axis=0)
    for n, g in zip(small_names, _unflatten(all_small, small_shapes)):
        if n == "loss":
            total_loss = g[0, 0]
            continue
        if n in CONV:
            width = g.shape[1] // N_CHIPS
            g = lax.dynamic_slice_in_dim(g, chip * width, width, axis=1)
        grads[n] = g.reshape(weights[n].shape[1:]) if n != "final_norm_w" else g

    delta, new_m, new_v = {}, {}, {}
    for n in BIG:
        shp = weights[n].shape
        if n == "w_in":
            g_t = w_in_grad.T
            results = [g_t] + list(_adamw(weights[n][0].T, g_t, mom1[n][0].T, mom2[n][0].T, name="adamw_" + n,
                                          tr=_row_tile(g_t.shape[0], 8, 136)))
            results = [a.T for a in results]
        else:
            results = _adamw_two_sums(weights[n][0], core_sums[n], sibling_sums[n], mom1[n][0], mom2[n][0],
                                      name="adamw_" + n, tr=_row_tile(shp[1], 8, 136))
        grads[n], delta[n], new_m[n], new_v[n] = [a.reshape(shp) for a in results]
    small_all = [n for n in WEIGHT_ORDER if n not in BIG]

    def as_2d(a):
        return a.reshape(-1, a.shape[-1])

    results = _adamw_many(*[[as_2d(src[n]) for n in small_all] for src in (weights, grads, mom1, mom2)],
                          name="adamw_small")
    for n, dv, mv, vv in zip(small_all, *results):
        shp = weights[n].shape
        delta[n], new_m[n], new_v[n] = dv.reshape(shp), mv.reshape(shp), vv.reshape(shp)

    grad_out = [grads[n].reshape(weights[n].shape) for n in WEIGHT_ORDER]
    return (total_loss, dx[None], *grad_out, *[delta[n] for n in WEIGHT_ORDER], *[new_m[n] for n in WEIGHT_ORDER],
            *[new_v[n] for n in WEIGHT_ORDER])
```

```python
import functools
import math

import jax
import jax.numpy as jnp
from jax import lax
from jax.experimental import pallas as pl
from jax.experimental.pallas import tpu as pltpu

F32 = jnp.float32
BF16 = jnp.bfloat16

D_MODEL = 1024
SSD_D_INNER = 2048
SSD_HEADS = 32
SSD_HEAD_DIM = 64
SSD_GROUPS = 4
SSD_HEADS_PER_GROUP = 8
SSD_STATE = 128
SSD_BC = 512
SSD_XBC = 3072
SSD_IN = 5152
SGU_WIDTH = 1024
SGU_GROUPS = 8
CHUNK = 128
IN_COLS = 9248
D_FF = 2816
NORM_EPS = 1e-6
LN_EPS = 1e-5
GROUP_COLS = SSD_HEADS_PER_GROUP * SSD_HEAD_DIM
LANES = 128

ADAM_LR = 0.001
ADAM_B1 = 0.9
ADAM_B2 = 0.999
ADAM_EPS = 1e-08
ADAM_WD = 0.01
ADAM_STEP = 10

N_CHIPS = 4
VMEM_LIMIT = 56 * 1024 * 1024

NT = (((1,), (1,)), ((), ()))
TN = (((0,), (0,)), ((), ()))
NN = (((1,), (0,)), ((), ()))


def _params(dims):
    return pltpu.CompilerParams(dimension_semantics=dims, vmem_limit_bytes=VMEM_LIMIT)


def _dot(a, b, dn=NN, precision=None):
    return lax.dot_general(a, b, dn, precision=precision, preferred_element_type=F32)


def _split3(x):
    hi = x.astype(BF16)
    rest = x - hi.astype(F32)
    mid = rest.astype(BF16)
    return hi, mid, (rest - mid.astype(F32)).astype(BF16)


def _dot_terms(terms, exact, dn=NN):
    out = None
    for t in terms:
        p = _dot(t, exact, dn)
        out = p if out is None else out + p
    return out


def _dot_exact_lhs(exact, terms):
    out = None
    for t in terms:
        p = _dot(exact, t)
        out = p if out is None else out + p
    return out


def _sigmoid(x):
    return 1.0 / (1.0 + jnp.exp(-x))


def _softplus(x):
    return jnp.maximum(x, 0.0) + jnp.log(1.0 + jnp.exp(-jnp.abs(x)))


def _matmul(pairs, *, trans_b=False, add=None, after=None, out_dtype=F32, tm=512, tn=512, name):
    def mat_shape(b):
        if isinstance(b, tuple) and b[1] == "cols":
            return (b[0].shape[1], b[0].shape[0] * b[0].shape[2])
        return b[0].shape[1:] if isinstance(b, tuple) else b.shape

    if isinstance(pairs[0][1], tuple) and pairs[0][1][1] == "cols":
        assert not trans_b and tn % LANES == 0 and pairs[0][1][0].shape[2] % tn == 0, name

    m = (pairs[0][0][0] if isinstance(pairs[0][0], tuple) else pairs[0][0]).shape[0]
    n = mat_shape(pairs[0][1])[0] if trans_b else mat_shape(pairs[0][1])[1]
    tm, tn = min(tm, m), min(tn, n)
    assert m % tm == 0 and n % tn == 0, (name, m, n, tm, tn)
    npairs = len(pairs)
    dn = NT if trans_b else NN

    def body(*refs):
        o_ref = refs[-1]
        acc = None
        for i in range(npairs):
            p = _dot(refs[2 * i][...].astype(BF16), refs[2 * i + 1][...].astype(BF16), dn)
            acc = p if acc is None else acc + p
        if add is not None:
            acc = acc + refs[2 * npairs][...]
        o_ref[...] = acc.astype(out_dtype)

    in_specs, args = [], []
    for a, b in pairs:
        bshape = mat_shape(b)
        k = bshape[1] if trans_b else bshape[0]
        assert bshape == ((n, k) if trans_b else (k, n)), (name, bshape)
        a, qa = a if isinstance(a, tuple) else (a, 0)
        assert a.shape[0] == m and a.shape[1] % k == 0, (name, a.shape, k)
        in_specs.append(pl.BlockSpec((tm, k), lambda i, j, qa=qa: (i, qa)))
        if isinstance(b, tuple) and b[1] == "cols":
            b = b[0]
            per = b.shape[2] // tn
            in_specs.append(pl.BlockSpec((None, k, tn), lambda i, j, per=per: (j // per, 0, j % per)))
        elif isinstance(b, tuple):
            b, qb = b
            if trans_b:
                in_specs.append(pl.BlockSpec((None, tn, k), lambda i, j, qb=qb: (qb, j, 0)))
            else:
                in_specs.append(pl.BlockSpec((None, k, tn), lambda i, j, qb=qb: (qb, 0, j)))
        elif trans_b:
            in_specs.append(pl.BlockSpec((tn, k), lambda i, j: (j, 0)))
        else:
            in_specs.append(pl.BlockSpec((k, tn), lambda i, j: (0, j)))
        args += [a, b]
    if add is not None:
        in_specs.append(pl.BlockSpec((tm, tn), lambda i, j: (i, j)))
        args.append(add)
    if after is not None:
        in_specs.append(pl.BlockSpec(memory_space=pl.ANY))
        args.append(after)
    return pl.pallas_call(
        body, name=name, grid=(m // tm, n // tn), in_specs=in_specs,
        out_specs=pl.BlockSpec((tm, tn), lambda i, j: (i, j)),
        out_shape=jax.ShapeDtypeStruct((m, n), out_dtype),
        compiler_params=_params(("parallel", "parallel")),
    )(*args)


def _matmul_tn(a, b, *, tk, tn, tm=1024, out_dtype=BF16, stack_out=False, after=None, name):
    m, k = a.shape
    n = b.shape[1]
    tm, tk, tn = min(tm, m), min(tk, k), min(tn, n)
    assert m % tm == 0 and k % tk == 0 and n % tn == 0, (name, m, k, n)
    nm = m // tm
    if stack_out:
        out_spec = pl.BlockSpec((None, tk, tn), lambda i, j, l: (j, i, 0))
        out_shape = jax.ShapeDtypeStruct((n // tn, k, tn), out_dtype)
    else:
        out_spec = pl.BlockSpec((tk, tn), lambda i, j, l: (i, j))
        out_shape = jax.ShapeDtypeStruct((k, n), out_dtype)

    def body(a_ref, b_ref, *rest):
        o_ref, acc = rest[-2:]
        mi = pl.program_id(2)

        @pl.when(mi == 0)
        def _():
            acc[...] = jnp.zeros_like(acc)

        acc[...] += _dot(a_ref[...].astype(BF16), b_ref[...].astype(BF16), TN)

        @pl.when(mi == nm - 1)
        def _():
            o_ref[...] = acc[...].astype(out_dtype)

    in_specs = [pl.BlockSpec((tm, tk), lambda i, j, l: (l, i)), pl.BlockSpec((tm, tn), lambda i, j, l: (l, j))]
    args = [a, b]
    if after is not None:
        in_specs.append(pl.BlockSpec(memory_space=pl.ANY))
        args.append(after)
    return pl.pallas_call(
        body, name=name, grid=(k // tk, n // tn, nm), in_specs=in_specs,
        out_specs=out_spec, out_shape=out_shape,
        scratch_shapes=[pltpu.VMEM((tk, tn), F32)],
        compiler_params=_params(("parallel", "parallel", "arbitrary")),
    )(*args)


def _rms_fwd(x, w, *, after=None, name, tm=512):
    s, d = x.shape
    tm = min(tm, s)
    extra = [] if after is None else [after]

    def body(x_ref, w_ref, *rest):
        o_ref = rest[-1]
        xv = x_ref[...]
        r = lax.rsqrt(jnp.mean(xv * xv, axis=-1, keepdims=True) + NORM_EPS)
        o_ref[...] = (xv * r * w_ref[...]).astype(BF16)

    return pl.pallas_call(
        body, name=name, grid=(s // tm,),
        in_specs=[pl.BlockSpec((tm, d), lambda i: (i, 0)), pl.BlockSpec((1, d), lambda i: (0, 0))]
        + [pl.BlockSpec(memory_space=pl.ANY)] * len(extra),
        out_specs=pl.BlockSpec((tm, d), lambda i: (i, 0)),
        out_shape=jax.ShapeDtypeStruct((s, d), BF16),
        compiler_params=_params(("parallel",)),
    )(x, w, *extra)


def _rms_bwd(x, w, dn, dres, *, name, tm=512):
    s, d = x.shape
    tm = min(tm, s)

    def body(x_ref, w_ref, dn_ref, dres_ref, dx_ref, dxb_ref, dw_ref):
        @pl.when(pl.program_id(0) == 0)
        def _():
            dw_ref[...] = jnp.zeros_like(dw_ref)

        xv = x_ref[...]
        r = lax.rsqrt(jnp.mean(xv * xv, axis=-1, keepdims=True) + NORM_EPS)
        xhat = xv * r
        dnv = dn_ref[...].astype(F32)
        dxhat = dnv * w_ref[...]
        dx = dres_ref[...] + r * (dxhat - xhat * jnp.mean(dxhat * xhat, axis=-1, keepdims=True))
        dx_ref[...] = dx
        dxb_ref[...] = dx.astype(BF16)
        dw_ref[...] += jnp.sum(dnv * xhat, axis=0, keepdims=True)

    tile = pl.BlockSpec((tm, d), lambda i: (i, 0))
    row = pl.BlockSpec((1, d), lambda i: (0, 0))
    return pl.pallas_call(
        body, name=name, grid=(s // tm,),
        in_specs=[tile, row, tile, tile], out_specs=[tile, tile, row],
        out_shape=[jax.ShapeDtypeStruct((s, d), F32), jax.ShapeDtypeStruct((s, d), BF16),
                   jax.ShapeDtypeStruct((1, d), F32)],
        compiler_params=_params(("arbitrary",)),
    )(x, w, dn, dres)


def _final_fwd_bwd(h2, wf, target, *, name, tm=512):
    s, d = h2.shape
    tm = min(tm, s)

    def body(h_ref, w_ref, t_ref, loss_ref, dh_ref, dhb_ref, dw_ref):
        @pl.when(pl.program_id(0) == 0)
        def _():
            dw_ref[...] = jnp.zeros_like(dw_ref)
            loss_ref[...] = jnp.zeros_like(loss_ref)

        hv = h_ref[...]
        r = lax.rsqrt(jnp.mean(hv * hv, axis=-1, keepdims=True) + NORM_EPS)
        xhat = hv * r
        err = xhat * w_ref[...] - t_ref[...]
        per_tok = jnp.mean(err * err, axis=-1, keepdims=True)
        loss_ref[...] += 0.5 * jnp.sum(per_tok, axis=0, keepdims=True)
        dy = err * (1.0 / d)
        dxhat = dy * w_ref[...]
        dh = r * (dxhat - xhat * jnp.mean(dxhat * xhat, axis=-1, keepdims=True))
        dh_ref[...] = dh
        dhb_ref[...] = dh.astype(BF16)
        dw_ref[...] += jnp.sum(dy * xhat, axis=0, keepdims=True)

    tile = pl.BlockSpec((tm, d), lambda i: (i, 0))
    row = pl.BlockSpec((1, d), lambda i: (0, 0))
    return pl.pallas_call(
        body, name=name, grid=(s // tm,),
        in_specs=[tile, row, tile],
        out_specs=[pl.BlockSpec((1, 1), lambda i: (0, 0)), tile, tile, row],
        out_shape=[jax.ShapeDtypeStruct((1, 1), F32), jax.ShapeDtypeStruct((s, d), F32),
                   jax.ShapeDtypeStruct((s, d), BF16), jax.ShapeDtypeStruct((1, d), F32)],
        compiler_params=_params(("arbitrary",)),
    )(h2, wf, target)


CONV_ROWS = 256
CONV_ROWS_FWD = 512
HALO = 8


def _rows_with_halo(ref, r0, rows, s, before, after):
    tile = 16 if ref.dtype == BF16 else HALO
    parts = []
    if before:
        prev = ref[pl.ds(pl.multiple_of(jnp.maximum(r0 - tile, 0), tile), tile), :].astype(F32)[tile - HALO:]
        parts.append(jnp.where(r0 > 0, prev, 0.0))
    parts.append(ref[pl.ds(r0, rows), :].astype(F32))
    if after:
        nxt = ref[pl.ds(pl.multiple_of(jnp.minimum(r0 + rows, s - tile), tile), tile), :].astype(F32)[:HALO]
        parts.append(jnp.where(r0 + rows < s, nxt, 0.0))
    return jnp.concatenate(parts, axis=0) if len(parts) > 1 else parts[0]


def _window(x_ref, r0, s, after):
    return _rows_with_halo(x_ref, r0, CONV_ROWS_FWD, s, True, after).astype(F32)


def _shifted(window, k, rows):
    if k == 0:
        return window[HALO:HALO + rows]
    return pltpu.roll(window, k, 0)[HALO:HALO + rows]


def _conv_taps(window, w_ref, kk, rows):
    acc = None
    for i in range(kk):
        term = w_ref[i:i + 1, :] * _shifted(window, kk - 1 - i, rows)
        acc = term if acc is None else acc + term
    return acc


def _row_loop(rows, step):
    def body(r, carry):
        return step(pl.multiple_of(r * rows, rows), carry)
    return body


def _conv_bwd_rows(x, dpe, w_ref, kk):
    dp = dpe[:CONV_ROWS]
    dx = None
    dws = []
    for i in range(kk):
        k = kk - 1 - i
        later = dp if k == 0 else pltpu.roll(dpe, dpe.shape[0] - k, 0)[:CONV_ROWS]
        dws.append(jnp.sum(later * x, axis=0, keepdims=True))
        term = w_ref[i:i + 1, :] * later
        dx = term if dx is None else dx + term
    return dx, dws, jnp.sum(dp, axis=0, keepdims=True)


def _conv_a_fwd(xraw, w, b, *, name, tc=128):
    s, c = xraw.shape
    kk = 4

    def body(x_ref, w_ref, b_ref, o_ref, pre_ref):
        def step(r0, carry):
            pre = _conv_taps(_window(x_ref, r0, s, False), w_ref, kk, CONV_ROWS_FWD) + b_ref[...]
            o_ref[pl.ds(r0, CONV_ROWS_FWD), :] = pre * _sigmoid(pre)
            pre_ref[pl.ds(r0, CONV_ROWS_FWD), :] = pre.astype(BF16)
            return carry

        lax.fori_loop(0, s // CONV_ROWS_FWD, _row_loop(CONV_ROWS_FWD, step), 0)

    col = pl.BlockSpec((s, tc), lambda j: (0, j))
    return pl.pallas_call(
        body, name=name, grid=(c // tc,),
        in_specs=[col, pl.BlockSpec((8, tc), lambda j: (0, j)), pl.BlockSpec((1, tc), lambda j: (0, j))],
        out_specs=[col, col], out_shape=[jax.ShapeDtypeStruct((s, c), F32), jax.ShapeDtypeStruct((s, c), BF16)],
        compiler_params=_params(("parallel",)),
    )(xraw, w, b)


def _conv_a_bwd(xraw, pre, w, dy, *, name, tc=128):
    s, c = xraw.shape
    kk = 4

    def body(x_ref, pre_ref, w_ref, dy_ref, dx_ref, dw_ref, db_ref):
        def step(r0, carry):
            pre = _rows_with_halo(pre_ref, r0, CONV_ROWS, s, False, True)
            sg = _sigmoid(pre)
            dpe = _rows_with_halo(dy_ref, r0, CONV_ROWS, s, False, True) * (sg * (1.0 + pre * (1.0 - sg)))
            dx, dws, db = _conv_bwd_rows(x_ref[pl.ds(r0, CONV_ROWS), :].astype(F32), dpe, w_ref, kk)
            dx_ref[pl.ds(r0, CONV_ROWS), :] = dx.astype(BF16)
            return tuple(acc + new for acc, new in zip(carry, dws + [db]))

        zero = jnp.zeros((1, tc), F32)
        sums = lax.fori_loop(0, s // CONV_ROWS, _row_loop(CONV_ROWS, step), (zero,) * (kk + 1))
        db_ref[...] = sums[kk]
        dw_ref[...] = jnp.concatenate(list(sums[:kk]) + [jnp.zeros((8 - kk, tc), F32)], axis=0)

    col = pl.BlockSpec((s, tc), lambda j: (0, j))
    w8 = pl.BlockSpec((8, tc), lambda j: (0, j))
    row = pl.BlockSpec((1, tc), lambda j: (0, j))
    return pl.pallas_call(
        body, name=name, grid=(c // tc,),
        in_specs=[col, col, w8, col], out_specs=[col, w8, row],
        out_shape=[jax.ShapeDtypeStruct((s, c), BF16), jax.ShapeDtypeStruct((8, c), F32),
                   jax.ShapeDtypeStruct((1, c), F32)],
        compiler_params=_params(("parallel",)),
    )(xraw, pre, w, dy)


def _conv_f_fwd(up_raw, w, b, *, name, tc=128):
    s, c2 = up_raw.shape
    c = c2 // 2
    nb = c // tc
    kk = 3

    def body(xa_ref, xv_ref, wa_ref, wv_ref, ba_ref, bv_ref, o_ref, a_out, v_out):
        def step(r0, carry):
            a = _conv_taps(_window(xa_ref, r0, s, False), wa_ref, kk, CONV_ROWS_FWD) + ba_ref[...]
            v = _conv_taps(_window(xv_ref, r0, s, False), wv_ref, kk, CONV_ROWS_FWD) + bv_ref[...]
            o_ref[pl.ds(r0, CONV_ROWS_FWD), :] = (a * _sigmoid(a) * v).astype(BF16)
            a_out[pl.ds(r0, CONV_ROWS_FWD), :] = a.astype(BF16)
            v_out[pl.ds(r0, CONV_ROWS_FWD), :] = v.astype(BF16)
            return carry

        lax.fori_loop(0, s // CONV_ROWS_FWD, _row_loop(CONV_ROWS_FWD, step), 0)

    col_a = pl.BlockSpec((s, tc), lambda j: (0, j))
    col_v = pl.BlockSpec((s, tc), lambda j: (0, j + nb))
    half = jax.ShapeDtypeStruct((s, c), BF16)
    return pl.pallas_call(
        body, name=name, grid=(nb,),
        in_specs=[col_a, col_v, pl.BlockSpec((8, tc), lambda j: (0, j)), pl.BlockSpec((8, tc), lambda j: (0, j + nb)),
                  pl.BlockSpec((1, tc), lambda j: (0, j)), pl.BlockSpec((1, tc), lambda j: (0, j + nb))],
        out_specs=[col_a, col_a, col_a], out_shape=[half, half, half],
        compiler_params=_params(("parallel",)),
    )(up_raw, up_raw, w, w, b, b)


def _conv_f_bwd(up_raw, a_pre, v_pre, w, dact, *, name, tc=128):
    s, c2 = up_raw.shape
    c = c2 // 2
    nb = c // tc
    kk = 3

    def body(xa_ref, xv_ref, a_ref, v_ref, wa_ref, wv_ref, d_ref,
             dxa_ref, dxv_ref, dwa_ref, dwv_ref, dba_ref, dbv_ref):
        def step(r0, carry):
            a = _rows_with_halo(a_ref, r0, CONV_ROWS, s, False, True)
            v = _rows_with_halo(v_ref, r0, CONV_ROWS, s, False, True)
            sg = _sigmoid(a)
            d = _rows_with_halo(d_ref, r0, CONV_ROWS, s, False, True)
            rows = pl.ds(r0, CONV_ROWS)
            dxa, dwas, dba = _conv_bwd_rows(xa_ref[rows, :].astype(F32), d * v * (sg * (1.0 + a * (1.0 - sg))),
                                            wa_ref, kk)
            dxv, dwvs, dbv = _conv_bwd_rows(xv_ref[rows, :].astype(F32), d * (a * sg), wv_ref, kk)
            dxa_ref[pl.ds(r0, CONV_ROWS), :] = dxa.astype(BF16)
            dxv_ref[pl.ds(r0, CONV_ROWS), :] = dxv.astype(BF16)
            return tuple(acc + new for acc, new in zip(carry, dwas + [dba] + dwvs + [dbv]))

        zero = jnp.zeros((1, tc), F32)
        sums = lax.fori_loop(0, s // CONV_ROWS, _row_loop(CONV_ROWS, step), (zero,) * (2 * kk + 2))
        pad = [jnp.zeros((8 - kk, tc), F32)]
        dwa_ref[...] = jnp.concatenate(list(sums[:kk]) + pad, axis=0)
        dba_ref[...] = sums[kk]
        dwv_ref[...] = jnp.concatenate(list(sums[kk + 1:2 * kk + 1]) + pad, axis=0)
        dbv_ref[...] = sums[2 * kk + 1]

    col_a = pl.BlockSpec((s, tc), lambda j: (0, j))
    col_v = pl.BlockSpec((s, tc), lambda j: (0, j + nb))
    w_a = pl.BlockSpec((8, tc), lambda j: (0, j))
    w_v = pl.BlockSpec((8, tc), lambda j: (0, j + nb))
    r_a = pl.BlockSpec((1, tc), lambda j: (0, j))
    r_v = pl.BlockSpec((1, tc), lambda j: (0, j + nb))
    outs = pl.pallas_call(
        body, name=name, grid=(nb,),
        in_specs=[col_a, col_v, col_a, col_a, w_a, w_v, col_a],
        out_specs=[col_a, col_a, w_a, w_a, r_a, r_a],
        out_shape=[jax.ShapeDtypeStruct((s, c), BF16), jax.ShapeDtypeStruct((s, c), BF16),
                   jax.ShapeDtypeStruct((8, c), F32), jax.ShapeDtypeStruct((8, c), F32),
                   jax.ShapeDtypeStruct((1, c), F32), jax.ShapeDtypeStruct((1, c), F32)],
        compiler_params=_params(("parallel",)),
    )(up_raw, up_raw, a_pre, v_pre, w, w, dact)
    return outs


def _tri_masks():
    row = lax.broadcasted_iota(jnp.int32, (CHUNK, CHUNK), 0)
    col = lax.broadcasted_iota(jnp.int32, (CHUNK, CHUNK), 1)
    return row >= col, row <= col


def _ssd_fwd(xbc, dt_raw, z, dt_bias, a_log, a_log_x, d_skip_x, norm_w, expand, *, name):
    s = xbc.shape[0]
    nc = s // CHUNK

    def body(xbc_ref, dtr_ref, z_ref, dtb_ref, alog_ref, alogx_ref, dskx_ref, nw_ref, e_ref,
             y_ref, ya_ref, st_ref, state):
        @pl.when(pl.program_id(0) == 0)
        def _():
            state[...] = jnp.zeros_like(state)

        st_ref[0] = state[...]
        lower, _ = _tri_masks()
        dt = _softplus(dtr_ref[...] + dtb_ref[...])
        adt = dt * (-jnp.exp(alog_ref[...]))
        acum = _dot_exact_lhs(lower.astype(BF16), _split3(adt))
        acum_t = acum.T
        dt_terms, acum_terms = _split3(dt), _split3(acum)
        for g in range(SSD_GROUPS):
            sl = slice(GROUP_COLS * g, GROUP_COLS * (g + 1))
            dt_x = _dot_terms(dt_terms, e_ref[:, sl])
            acum_x = _dot_terms(acum_terms, e_ref[:, sl])
            tot_x = jnp.sum(dt_x * (-jnp.exp(alogx_ref[:, sl])), axis=0, keepdims=True)
            xs = xbc_ref[:, sl]
            xdt = xs * dt_x
            xdt_b = xdt.astype(BF16)
            bg = xbc_ref[:, SSD_D_INNER + SSD_STATE * g:SSD_D_INNER + SSD_STATE * (g + 1)].astype(BF16)
            cg = xbc_ref[:, SSD_D_INNER + SSD_BC + SSD_STATE * g:SSD_D_INNER + SSD_BC + SSD_STATE * (g + 1)].astype(BF16)
            cb = _dot(cg, bg, NT)
            st_g = state[:, sl]
            y_off = _dot(cg, st_g.astype(BF16)) * jnp.exp(acum_x)
            parts = []
            for r in range(SSD_HEADS_PER_GROUP):
                h = SSD_HEADS_PER_GROUP * g + r
                dec = jnp.exp(jnp.where(lower, acum[:, h:h + 1] - acum_t[h:h + 1, :], -jnp.inf))
                parts.append(_dot((cb * dec).astype(BF16), xdt_b[:, SSD_HEAD_DIM * r:SSD_HEAD_DIM * (r + 1)]))
            y_ref[:, sl] = jnp.concatenate(parts, axis=1) + y_off + dskx_ref[:, sl] * xs
            wgt = (xdt * jnp.exp(tot_x - acum_x)).astype(BF16)
            state[:, sl] = st_g * jnp.exp(tot_x) + _dot(bg, wgt, TN)
        zv = z_ref[...].astype(F32)
        q = y_ref[...] * (zv * _sigmoid(zv))
        r = lax.rsqrt(jnp.mean(q * q, axis=-1, keepdims=True) + NORM_EPS)
        ya_ref[...] = (q * r * nw_ref[...]).astype(BF16)

    def chunk(w):
        return pl.BlockSpec((CHUNK, w), lambda c: (c, 0))

    def const(shape):
        return pl.BlockSpec(shape, lambda c: (0,) * len(shape))

    return pl.pallas_call(
        body, name=name, grid=(nc,),
        in_specs=[chunk(SSD_XBC), chunk(LANES), chunk(SSD_D_INNER), const((1, LANES)), const((1, LANES)),
                  const((1, SSD_D_INNER)), const((1, SSD_D_INNER)), const((1, SSD_D_INNER)),
                  const((LANES, SSD_D_INNER))],
        out_specs=[chunk(SSD_D_INNER), chunk(SSD_D_INNER),
                   pl.BlockSpec((1, SSD_STATE, SSD_D_INNER), lambda c: (c, 0, 0))],
        out_shape=[jax.ShapeDtypeStruct((s, SSD_D_INNER), F32), jax.ShapeDtypeStruct((s, SSD_D_INNER), BF16),
                   jax.ShapeDtypeStruct((nc, SSD_STATE, SSD_D_INNER), F32)],
        scratch_shapes=[pltpu.VMEM((SSD_STATE, SSD_D_INNER), F32)],
        compiler_params=_params(("arbitrary",)),
    )(xbc, dt_raw, z, dt_bias, a_log, a_log_x, d_skip_x, norm_w, expand)


def _ssd_bwd(dya, y, z, xbc, dt_raw, states, dt_bias, a_log, a_log_x, d_skip_x, norm_w, expand, expand_t, *, name):
    s = xbc.shape[0]
    nc = s // CHUNK

    def body(dya_ref, y_ref, z_ref, xbc_ref, dtr_ref, stp_ref, dtb_ref, alog_ref, alogx_ref, dskx_ref, nw_ref,
             e_ref, et_ref, dz_ref, dxbc_ref, ddt_ref, dnw_ref, ddsk_ref, dalog_ref, ddtb_ref,
             dstate, dy_sc, dskcol):
        i = pl.program_id(0)

        @pl.when(i == 0)
        def _():
            dstate[...] = jnp.zeros_like(dstate)
            dskcol[...] = jnp.zeros_like(dskcol)
            dnw_ref[...] = jnp.zeros_like(dnw_ref)
            dalog_ref[...] = jnp.zeros_like(dalog_ref)
            ddtb_ref[...] = jnp.zeros_like(ddtb_ref)
            ddsk_ref[...] = jnp.zeros_like(ddsk_ref)

        lower, upper = _tri_masks()
        rows = lax.broadcasted_iota(jnp.int32, (CHUNK, LANES), 0)
        pre = dtr_ref[...] + dtb_ref[...]
        dt = _softplus(pre)
        a = -jnp.exp(alog_ref[...])
        acum = _dot_exact_lhs(lower.astype(BF16), _split3(dt * a))
        acum_t = acum.T
        dt_terms, acum_terms = _split3(dt), _split3(acum)

        yv = y_ref[...]
        zv = z_ref[...].astype(F32)
        sz = _sigmoid(zv)
        silu_z = zv * sz
        q = yv * silu_z
        r = lax.rsqrt(jnp.mean(q * q, axis=-1, keepdims=True) + NORM_EPS)
        qhat = q * r
        dyav = dya_ref[...]
        dqhat = dyav * nw_ref[...]
        dnw_ref[...] += jnp.sum(dyav * qhat, axis=0, keepdims=True)
        dq = r * (dqhat - qhat * jnp.mean(dqhat * qhat, axis=-1, keepdims=True))
        dy_sc[...] = dq * silu_z
        dz_ref[...] = (dq * yv * (sz * (1.0 + zv * (1.0 - sz)))).astype(BF16)

        da_cum = jnp.zeros((CHUNK, LANES), F32)
        ddt = jnp.zeros((CHUNK, LANES), F32)
        for g in range(SSD_GROUPS):
            sl = slice(GROUP_COLS * g, GROUP_COLS * (g + 1))
            et_g = et_ref[sl, :]
            dt_x = _dot_terms(dt_terms, e_ref[:, sl])
            acum_x = _dot_terms(acum_terms, e_ref[:, sl])
            tot_x = jnp.sum(dt_x * (-jnp.exp(alogx_ref[:, sl])), axis=0, keepdims=True)
            e_tot = jnp.exp(tot_x)
            dec_s = jnp.exp(tot_x - acum_x)
            xs = xbc_ref[:, sl]
            xdt = xs * dt_x
            xdt_b = xdt.astype(BF16)
            dy = dy_sc[:, sl]
            dy_b = dy.astype(BF16)
            dskx = dskx_ref[:, sl]
            y_ssd = y_ref[:, sl] - dskx * xs
            dskcol[:, sl] += jnp.sum(dy * xs, axis=0, keepdims=True)
            bg = xbc_ref[:, SSD_D_INNER + SSD_STATE * g:SSD_D_INNER + SSD_STATE * (g + 1)].astype(BF16)
            cg = xbc_ref[:, SSD_D_INNER + SSD_BC + SSD_STATE * g:SSD_D_INNER + SSD_BC + SSD_STATE * (g + 1)].astype(BF16)
            cb_t = _dot(bg, cg, NT)
            sp = stp_ref[0, :, sl]
            ds_g = dstate[:, sl]
            ds_b = ds_g.astype(BF16)
            dye_b = (dy * jnp.exp(acum_x)).astype(BF16)
            dc = _dot(dye_b, sp.astype(BF16), NT)
            dxdt_state = dec_s * _dot(bg, ds_b)
            db = _dot((xdt * dec_s).astype(BF16), ds_b, NT)
            dcb_t = jnp.zeros((CHUNK, CHUNK), F32)
            parts = []
            for rr in range(SSD_HEADS_PER_GROUP):
                h = SSD_HEADS_PER_GROUP * g + rr
                hs = slice(SSD_HEAD_DIM * rr, SSD_HEAD_DIM * (rr + 1))
                dec_t = jnp.exp(jnp.where(upper, acum_t[h:h + 1, :] - acum[:, h:h + 1], -jnp.inf))
                parts.append(_dot((cb_t * dec_t).astype(BF16), dy_b[:, hs]))
                dcb_t = dcb_t + _dot(xdt_b[:, hs], dy_b[:, hs], NT) * dec_t
            dxdt = jnp.concatenate(parts, axis=1) + dxdt_state
            dcb_tb = dcb_t.astype(BF16)
            dc = dc + _dot(dcb_tb, bg, TN)
            db = db + _dot(dcb_tb, cg)
            tot_col = jnp.sum(ds_g * sp, axis=0, keepdims=True) * e_tot + jnp.sum(dxdt_state * xdt, axis=0, keepdims=True)
            d_tot = _dot_terms(_split3(jnp.broadcast_to(tot_col, (8, GROUP_COLS))), et_g)
            d_tot = jnp.max(d_tot, axis=0, keepdims=True)
            pair_sums = dy_b.astype(F32) * y_ssd - xdt_b.astype(F32) * dxdt
            da_cum = da_cum + _dot_terms(_split3(pair_sums), et_g) + jnp.where(rows == CHUNK - 1, d_tot, 0.0)
            ddt = ddt + _dot_terms(_split3(dxdt * xs), et_g)
            dxbc_ref[:, sl] = dy * dskx + dxdt * dt_x
            dxbc_ref[:, SSD_D_INNER + SSD_STATE * g:SSD_D_INNER + SSD_STATE * (g + 1)] = db
            dxbc_ref[:, SSD_D_INNER + SSD_BC + SSD_STATE * g:SSD_D_INNER + SSD_BC + SSD_STATE * (g + 1)] = dc
            dstate[:, sl] = e_tot * ds_g + _dot(cg, dye_b, TN)

        dadt = _dot_exact_lhs(upper.astype(BF16), _split3(da_cum))
        ddt = ddt + dadt * a
        dalog_ref[...] += jnp.sum(dadt * dt, axis=0, keepdims=True)
        dpre = ddt * _sigmoid(pre)
        ddtb_ref[...] += jnp.sum(dpre, axis=0, keepdims=True)
        ddt_ref[...] = dpre.astype(BF16)

        @pl.when(i == nc - 1)
        def _():
            dalog_ref[...] = dalog_ref[...] * a
            dsk = _dot_terms(_split3(jnp.broadcast_to(dskcol[...], (8, SSD_D_INNER))), et_ref[...])
            ddsk_ref[...] = jnp.max(dsk, axis=0, keepdims=True)

    def chunk(w):
        return pl.BlockSpec((CHUNK, w), lambda i: (nc - 1 - i, 0))

    def const(shape):
        return pl.BlockSpec(shape, lambda i: (0,) * len(shape))

    return pl.pallas_call(
        body, name=name, grid=(nc,),
        in_specs=[chunk(SSD_D_INNER), chunk(SSD_D_INNER), chunk(SSD_D_INNER), chunk(SSD_XBC), chunk(LANES),
                  pl.BlockSpec((1, SSD_STATE, SSD_D_INNER), lambda i: (nc - 1 - i, 0, 0)),
                  const((1, LANES)), const((1, LANES)), const((1, SSD_D_INNER)), const((1, SSD_D_INNER)),
                  const((1, SSD_D_INNER)), const((LANES, SSD_D_INNER)), const((SSD_D_INNER, LANES))],
        out_specs=[chunk(SSD_D_INNER), chunk(SSD_XBC), chunk(LANES), const((1, SSD_D_INNER)), const((1, LANES)),
                   const((1, LANES)), const((1, LANES))],
        out_shape=[jax.ShapeDtypeStruct((s, SSD_D_INNER), BF16), jax.ShapeDtypeStruct((s, SSD_XBC), F32),
                   jax.ShapeDtypeStruct((s, LANES), BF16), jax.ShapeDtypeStruct((1, SSD_D_INNER), F32),
                   jax.ShapeDtypeStruct((1, LANES), F32), jax.ShapeDtypeStruct((1, LANES), F32),
                   jax.ShapeDtypeStruct((1, LANES), F32)],
        scratch_shapes=[pltpu.VMEM((SSD_STATE, SSD_D_INNER), F32), pltpu.VMEM((CHUNK, SSD_D_INNER), F32),
                        pltpu.VMEM((1, SSD_D_INNER), F32)],
        compiler_params=_params(("arbitrary",)),
    )(dya, y, z, xbc, dt_raw, states, dt_bias, a_log, a_log_x, d_skip_x, norm_w, expand, expand_t)


GELU_K = math.sqrt(2.0 / math.pi)
GELU_C = 0.044715


def _gelu(x):
    return 0.5 * x * (1.0 + jnp.tanh(GELU_K * (x + GELU_C * x * x * x)))


def _gelu_grad(x):
    t = jnp.tanh(GELU_K * (x + GELU_C * x * x * x))
    return 0.5 * (1.0 + t) + 0.5 * x * (1.0 - t * t) * (GELU_K * (1.0 + 3.0 * GELU_C * x * x))


def _sgu_pre(uv_ref, uvb_ref, lnw_ref, lnb_ref):
    uv = uv_ref[...].astype(F32) + uvb_ref[...]
    guv = _gelu(uv)
    u = guv[:, :SGU_WIDTH]
    v = guv[:, SGU_WIDTH:]
    mu = jnp.mean(v, axis=-1, keepdims=True)
    vc = v - mu
    rstd = lax.rsqrt(jnp.mean(vc * vc, axis=-1, keepdims=True) + LN_EPS)
    vhat = vc * rstd
    vn = vhat * lnw_ref[...] + lnb_ref[...]
    return uv, u, vhat, rstd, vn


def _sgu_fwd(uv_raw, uv_b, ln_w, ln_b, w_sp, b_sp_t, *, name):
    s = uv_raw.shape[0]
    nc = s // CHUNK

    def body(uv_ref, uvb_ref, lnw_ref, lnb_ref, w_ref, bt_ref, o_ref):
        lower, _ = _tri_masks()
        _, u, _, _, vn = _sgu_pre(uv_ref, uvb_ref, lnw_ref, lnb_ref)
        vn_b = vn.astype(BF16)
        bt = bt_ref[...]
        for g in range(SGU_GROUPS):
            gs = slice(LANES * g, LANES * (g + 1))
            wc = jnp.where(lower, w_ref[g], 0.0).astype(BF16)
            mixed = _dot(wc, vn_b[:, gs]) + bt[:, g:g + 1]
            o_ref[:, gs] = (u[:, gs] * mixed).astype(BF16)

    def const(shape):
        return pl.BlockSpec(shape, lambda c: (0,) * len(shape))

    return pl.pallas_call(
        body, name=name, grid=(nc,),
        in_specs=[pl.BlockSpec((CHUNK, 2 * SGU_WIDTH), lambda c: (c, 0)), const((1, 2 * SGU_WIDTH)),
                  const((1, SGU_WIDTH)), const((1, SGU_WIDTH)), const((SGU_GROUPS, CHUNK, CHUNK)),
                  const((CHUNK, LANES))],
        out_specs=pl.BlockSpec((CHUNK, SGU_WIDTH), lambda c: (c, 0)),
        out_shape=jax.ShapeDtypeStruct((s, SGU_WIDTH), BF16),
        compiler_params=_params(("parallel",)),
    )(uv_raw, uv_b, ln_w, ln_b, w_sp, b_sp_t)


def _sgu_bwd(uv_raw, dyb, uv_b, ln_w, ln_b, w_sp, b_sp_t, group_sum, *, name):
    s = uv_raw.shape[0]
    nc = s // CHUNK

    def body(uv_ref, dy_ref, uvb_ref, lnw_ref, lnb_ref, w_ref, bt_ref, gsum_ref,
             duv_ref, dw_ref, dbt_ref, dlnw_ref, dlnb_ref, duvb_ref):
        @pl.when(pl.program_id(0) == 0)
        def _():
            dw_ref[...] = jnp.zeros_like(dw_ref)
            dbt_ref[...] = jnp.zeros_like(dbt_ref)
            dlnw_ref[...] = jnp.zeros_like(dlnw_ref)
            dlnb_ref[...] = jnp.zeros_like(dlnb_ref)
            duvb_ref[...] = jnp.zeros_like(duvb_ref)

        lower, _ = _tri_masks()
        uv, u, vhat, rstd, vn = _sgu_pre(uv_ref, uvb_ref, lnw_ref, lnb_ref)
        vn_b = vn.astype(BF16)
        bt = bt_ref[...]
        dy = dy_ref[...].astype(F32)
        du_parts, dvn_parts, dmix_parts = [], [], []
        for g in range(SGU_GROUPS):
            gs = slice(LANES * g, LANES * (g + 1))
            wc = jnp.where(lower, w_ref[g], 0.0).astype(BF16)
            mixed = _dot(wc, vn_b[:, gs]) + bt[:, g:g + 1]
            du_parts.append(dy[:, gs] * mixed)
            dmix = dy[:, gs] * u[:, gs]
            dmix_b = dmix.astype(BF16)
            dmix_parts.append(dmix)
            dw_ref[g] += jnp.where(lower, _dot(dmix_b, vn_b[:, gs], NT), 0.0)
            dvn_parts.append(_dot(wc, dmix_b, TN))
        dmixed = jnp.concatenate(dmix_parts, axis=1)
        dbt_ref[...] += _dot_terms(_split3(dmixed), gsum_ref[...])
        dvn = jnp.concatenate(dvn_parts, axis=1)
        dlnw_ref[...] += jnp.sum(dvn * vhat, axis=0, keepdims=True)
        dlnb_ref[...] += jnp.sum(dvn, axis=0, keepdims=True)
        dvhat = dvn * lnw_ref[...]
        dv = rstd * (dvhat - jnp.mean(dvhat, axis=-1, keepdims=True)
                     - vhat * jnp.mean(dvhat * vhat, axis=-1, keepdims=True))
        dguv = jnp.concatenate(du_parts + [dv], axis=1)
        duv = dguv * _gelu_grad(uv)
        duvb_ref[...] += jnp.sum(duv, axis=0, keepdims=True)
        duv_ref[...] = duv.astype(BF16)

    def const(shape):
        return pl.BlockSpec(shape, lambda c: (0,) * len(shape))

    return pl.pallas_call(
        body, name=name, grid=(nc,),
        in_specs=[pl.BlockSpec((CHUNK, 2 * SGU_WIDTH), lambda c: (c, 0)),
                  pl.BlockSpec((CHUNK, SGU_WIDTH), lambda c: (c, 0)), const((1, 2 * SGU_WIDTH)),
                  const((1, SGU_WIDTH)), const((1, SGU_WIDTH)), const((SGU_GROUPS, CHUNK, CHUNK)),
                  const((CHUNK, LANES)), const((SGU_WIDTH, LANES))],
        out_specs=[pl.BlockSpec((CHUNK, 2 * SGU_WIDTH), lambda c: (c, 0)), const((SGU_GROUPS, CHUNK, CHUNK)),
                   const((CHUNK, LANES)), const((1, SGU_WIDTH)), const((1, SGU_WIDTH)), const((1, 2 * SGU_WIDTH))],
        out_shape=[jax.ShapeDtypeStruct((s, 2 * SGU_WIDTH), BF16),
                   jax.ShapeDtypeStruct((SGU_GROUPS, CHUNK, CHUNK), F32), jax.ShapeDtypeStruct((CHUNK, LANES), F32),
                   jax.ShapeDtypeStruct((1, SGU_WIDTH), F32), jax.ShapeDtypeStruct((1, SGU_WIDTH), F32),
                   jax.ShapeDtypeStruct((1, 2 * SGU_WIDTH), F32)],
        compiler_params=_params(("arbitrary",)),
    )(uv_raw, dyb, uv_b, ln_w, ln_b, w_sp, b_sp_t, group_sum)


def _gate_fwd(gates_raw, b_gate, p_a, p_b, *, name, tm=512):
    s = p_a.shape[0]
    tm = min(tm, s)

    def body(ga_ref, gb_ref, ba_ref, bb_ref, pa_ref, pb_ref, o_ref):
        ga = _sigmoid(ga_ref[...].astype(F32) + ba_ref[...])
        gb = _sigmoid(gb_ref[...].astype(F32) + bb_ref[...])
        o_ref[...] = (ga * pa_ref[...].astype(F32) + gb * pb_ref[...].astype(F32)).astype(BF16)

    t_a = pl.BlockSpec((tm, D_MODEL), lambda i: (i, 0))
    t_b = pl.BlockSpec((tm, D_MODEL), lambda i: (i, 1))
    r_a = pl.BlockSpec((1, D_MODEL), lambda i: (0, 0))
    r_b = pl.BlockSpec((1, D_MODEL), lambda i: (0, 1))
    return pl.pallas_call(
        body, name=name, grid=(s // tm,),
        in_specs=[t_a, t_b, r_a, r_b, t_a, t_a], out_specs=t_a,
        out_shape=jax.ShapeDtypeStruct((s, D_MODEL), BF16),
        compiler_params=_params(("parallel",)),
    )(gates_raw, gates_raw, b_gate, b_gate, p_a, p_b)


def _gate_bwd(gates_raw, b_gate, p_a, p_b, dm, *, name, tm=512):
    s = p_a.shape[0]
    tm = min(tm, s)

    def body(ga_ref, gb_ref, ba_ref, bb_ref, pa_ref, pb_ref, dm_ref, dpa_ref, dpb_ref, dga_ref, dgb_ref,
             dba_ref, dbb_ref):
        @pl.when(pl.program_id(0) == 0)
        def _():
            dba_ref[...] = jnp.zeros_like(dba_ref)
            dbb_ref[...] = jnp.zeros_like(dbb_ref)

        d = dm_ref[...].astype(F32)
        for g_ref, b_ref, p_ref, dp_ref, dg_ref, db_ref in ((ga_ref, ba_ref, pa_ref, dpa_ref, dga_ref, dba_ref),
                                                            (gb_ref, bb_ref, pb_ref, dpb_ref, dgb_ref, dbb_ref)):
            sg = _sigmoid(g_ref[...].astype(F32) + b_ref[...])
            dp_ref[...] = (d * sg).astype(BF16)
            dg = d * p_ref[...].astype(F32) * (sg * (1.0 - sg))
            dg_ref[...] = dg.astype(BF16)
            db_ref[...] += jnp.sum(dg, axis=0, keepdims=True)

    t_a = pl.BlockSpec((tm, D_MODEL), lambda i: (i, 0))
    t_b = pl.BlockSpec((tm, D_MODEL), lambda i: (i, 1))
    r_a = pl.BlockSpec((1, D_MODEL), lambda i: (0, 0))
    r_b = pl.BlockSpec((1, D_MODEL), lambda i: (0, 1))
    big = jax.ShapeDtypeStruct((s, D_MODEL), BF16)
    row = jax.ShapeDtypeStruct((1, D_MODEL), F32)
    return pl.pallas_call(
        body, name=name, grid=(s // tm,),
        in_specs=[t_a, t_b, r_a, r_b, t_a, t_a, t_a], out_specs=[t_a, t_a, t_a, t_a, r_a, r_a],
        out_shape=[big, big, big, big, row, row],
        compiler_params=_params(("arbitrary",)),
    )(gates_raw, gates_raw, b_gate, b_gate, p_a, p_b, dm)


def _adamw_update(w_ref, g_ref, m_ref, v_ref, d_ref, mo_ref, vo_ref):
    gv = g_ref[...]
    mn = ADAM_B1 * m_ref[...] + (1.0 - ADAM_B1) * gv
    vn = ADAM_B2 * v_ref[...] + (1.0 - ADAM_B2) * (gv * gv)
    m_hat = mn / (1.0 - ADAM_B1 ** ADAM_STEP)
    v_hat = vn / (1.0 - ADAM_B2 ** ADAM_STEP)
    d_ref[...] = -ADAM_LR * (m_hat / (jnp.sqrt(v_hat) + ADAM_EPS) + ADAM_WD * w_ref[...])
    mo_ref[...] = mn
    vo_ref[...] = vn


def _adamw_many(ws, gs, ms, vs, *, name):
    n = len(ws)

    def body(*refs):
        for i in range(n):
            _adamw_update(*[refs[k * n + i] for k in range(7)])

    whole = pl.BlockSpec(memory_space=pltpu.VMEM)
    sds = [jax.ShapeDtypeStruct(w.shape, F32) for w in ws]
    outs = pl.pallas_call(
        body, name=name, in_specs=[whole] * (4 * n), out_specs=[whole] * (3 * n), out_shape=sds * 3,
        compiler_params=pltpu.CompilerParams(vmem_limit_bytes=VMEM_LIMIT),
    )(*ws, *gs, *ms, *vs)
    return outs[:n], outs[n:2 * n], outs[2 * n:]


def _adamw(w, g, m, v, *, name, tr=128):
    r, c = w.shape
    tr = min(tr, r)
    assert r % tr == 0, (name, r, tr)
    body = functools.partial(_adamw_update)

    blk = pl.BlockSpec((tr, c), lambda i: (i, 0))
    sds = jax.ShapeDtypeStruct((r, c), F32)
    return pl.pallas_call(
        body, name=name, grid=(r // tr,), in_specs=[blk] * 4, out_specs=[blk] * 3, out_shape=[sds] * 3,
        compiler_params=_params(("parallel",)),
    )(w, g, m, v)


def _adamw_two_sums(w, g_a, g_b, m, v, *, name, tr=128):
    r, c = w.shape
    tr = min(tr, r)
    assert r % tr == 0, (name, r, tr)

    def body(w_ref, ga_ref, gb_ref, m_ref, v_ref, g_ref, d_ref, mo_ref, vo_ref):
        g_ref[...] = ga_ref[...] + gb_ref[...]
        _adamw_update(w_ref, g_ref, m_ref, v_ref, d_ref, mo_ref, vo_ref)

    blk = pl.BlockSpec((tr, c), lambda i: (i, 0))
    sds = jax.ShapeDtypeStruct((r, c), F32)
    return pl.pallas_call(
        body, name=name, grid=(r // tr,), in_specs=[blk] * 5, out_specs=[blk] * 4, out_shape=[sds] * 4,
        compiler_params=_params(("parallel",)),
    )(w, g_a, g_b, m, v)


def _tile(n, pref):
    if n <= pref:
        return n
    best = LANES
    for t in range(LANES, pref + 1, LANES):
        if n % t == 0:
            best = t
    return best


MATMUL_BLOCK_BYTES = 20 * 1024 * 1024


def _mm(pairs, name, **kw):
    trans_b = kw.get("trans_b", False)
    m = (pairs[0][0][0] if isinstance(pairs[0][0], tuple) else pairs[0][0]).shape[0]
    ktot, n = 0, None
    for _, b in pairs:
        shape = b[0].shape[1:] if isinstance(b, tuple) else b.shape
        ktot += shape[1] if trans_b else shape[0]
        n = shape[0] if trans_b else shape[1]
    out_bytes = 4 * (2 if kw.get("add") is not None else 1)
    best = None
    for tm in (256, 512, 1024, 2048):
        for tn in range(LANES, min(n, 1536) + 1, LANES):
            if m % min(tm, m) or n % tn:
                continue
            fits = 2 * ktot * (min(tm, m) + tn) + out_bytes * min(tm, m) * tn <= MATMUL_BLOCK_BYTES
            if fits and (best is None or min(tm, m) * tn >= best[0] * best[1]):
                best = (min(tm, m), tn)
    return _matmul(pairs, tm=best[0], tn=best[1], name=name, **kw)


def _wgrad(a, b, name, **kw):
    return _matmul_tn(a, b, tk=_tile(a.shape[1], 1408), tn=kw.pop("tn", _tile(b.shape[1], 1024)), tm=2048,
                      name=name, **kw)


def _local_step(x, target, get_weight, small, emit_grad):
    heads = jnp.arange(SSD_D_INNER) // SSD_HEAD_DIM
    expand = (jnp.arange(LANES)[:, None] == heads[None, :]).astype(BF16)
    expand_t = expand.T
    group_sum = (jnp.arange(SGU_WIDTH)[:, None] // LANES == jnp.arange(LANES)[None, :]).astype(BF16)
    pad_h = LANES - SSD_HEADS
    dt_bias = jnp.pad(small["dt_bias"], ((0, 0), (0, pad_h)))
    a_log = jnp.pad(small["a_log"], ((0, 0), (0, pad_h)))
    a_log_x = jnp.repeat(small["a_log"], SSD_HEAD_DIM, axis=1)
    d_skip_x = jnp.repeat(small["d_skip"], SSD_HEAD_DIM, axis=1)
    b_sp_t = jnp.pad(small["b_spatial"][0].T, ((0, 0), (0, LANES - SGU_GROUPS)))
    w_sp = small["w_spatial"][0]
    conv_a_w = jnp.pad(small["conv_a_w"], ((0, 4), (0, 0)))
    conv_f_w = jnp.pad(small["conv_f_w"], ((0, 5), (0, 0)))
    final_w = small["final_norm_w"].reshape(1, D_MODEL)

    n1 = _rms_fwd(x, small["norm1_w"], after=small.get("gathers_started"), name="rms1_fwd")
    wts = dict(get_weight("w_in", n1))
    z = _mm([(n1, wts["in_z"])], "in_z")
    xbc_raw = _mm([(n1, wts["in_xbc"])], "in_xbc")
    dt_raw = _mm([(n1, wts["in_dt"])], "in_dt")
    uv_raw = _mm([(n1, wts["in_uv"])], "in_uv", out_dtype=BF16)
    gates_raw = _mm([(n1, wts["in_gate"])], "in_gate", out_dtype=BF16)
    xbc, xbc_pre = _conv_a_fwd(xbc_raw, conv_a_w, small["conv_a_b"], name="conv_a_fwd")
    y, y_a, states = _ssd_fwd(xbc, dt_raw, z, dt_bias, a_log, a_log_x, d_skip_x, small["ssd_norm_w"], expand,
                              name="ssd_fwd")
    y_b = _sgu_fwd(uv_raw, small["uv_b"], small["v_ln_w"], small["v_ln_b"], w_sp, b_sp_t, name="sgu_fwd")
    wts.update(get_weight("w_branch", y_b))
    p_a = _mm([(y_a, wts["branch_a"])], "branch_a", out_dtype=BF16)
    p_b = _mm([(y_b, wts["branch_b"])], "branch_b", out_dtype=BF16)
    mix = _gate_fwd(gates_raw, small["b_gate"], p_a, p_b, name="gate_fwd")
    wts.update(get_weight("w_out", mix))
    h1 = _mm([(mix, wts["out"])], "out_proj", add=x)
    n2 = _rms_fwd(h1, small["norm2_w"], name="rms2_fwd")
    wts.update(get_weight("w_up", n2))
    up_w = wts["up"]
    up_cols = up_w.shape[2]
    up_raw = _matmul([(n2, (up_w, "cols"))], tm=2048, tn=up_cols, out_dtype=BF16, name="up_proj")
    act, up_a, up_v = _conv_f_fwd(up_raw, conv_f_w, small["conv_f_b"], name="conv_f_fwd")
    wts.update(get_weight("w_down", act))
    h2 = _mm([(act, wts["down"])], "down_proj", add=h1)
    loss, dh2, dh2_b, d_final = _final_fwd_bwd(h2, final_w, target, name="final_norm_loss")

    dact = _mm([(dh2_b, wts["down"])], "down_dgrad", trans_b=True)
    started = emit_grad("w_down", _wgrad(act, dh2_b, "down_wgrad"))
    dup_a, dup_v, dwf_a, dwf_v, dbf_a, dbf_v = _conv_f_bwd(up_raw, up_a, up_v, conv_f_w, dact, name="conv_f_bwd")
    dn2 = _mm([((dup_a, 0), (up_w, 0)), ((dup_a, 1), (up_w, 1)), ((dup_v, 0), (up_w, 2)), ((dup_v, 1), (up_w, 3))],
              "up_dgrad", trans_b=True, after=started, out_dtype=BF16)
    started = emit_grad("w_up", jnp.concatenate([_wgrad(n2, dup_a, "up_wgrad_a", tn=up_cols, stack_out=True),
                                                 _wgrad(n2, dup_v, "up_wgrad_v", tn=up_cols, stack_out=True)], axis=0))
    dh1, dh1_b, d_norm2 = _rms_bwd(h1, small["norm2_w"], dn2, dh2, name="rms2_bwd")
    dmix = _mm([(dh1_b, wts["out"])], "out_dgrad", trans_b=True, after=started, out_dtype=BF16)
    started = emit_grad("w_out", _wgrad(mix, dh1_b, "out_wgrad"))
    dp_a, dp_b, dg_a, dg_b, dbg_a, dbg_b = _gate_bwd(gates_raw, small["b_gate"], p_a, p_b, dmix, name="gate_bwd")
    dya = _mm([(dp_a, wts["branch_a"])], "branch_a_dgrad", trans_b=True, after=started)
    dyb = _mm([(dp_b, wts["branch_b"])], "branch_b_dgrad", trans_b=True, out_dtype=BF16)
    started_branch = emit_grad("w_branch", jnp.concatenate([_wgrad(y_a, dp_a, "branch_a_wgrad"),
                                                            _wgrad(y_b, dp_b, "branch_b_wgrad")], axis=0))
    duv, d_wsp, d_bsp_t, d_lnw, d_lnb, d_uvb = _sgu_bwd(uv_raw, dyb, small["uv_b"], small["v_ln_w"],
                                                        small["v_ln_b"], w_sp, b_sp_t, group_sum, name="sgu_bwd")
    dz, dxbc, ddt, d_ssd_nw, d_dskip, d_alog, d_dtb = _ssd_bwd(
        dya, y, z, xbc, dt_raw, states, dt_bias, a_log, a_log_x, d_skip_x, small["ssd_norm_w"], expand, expand_t,
        name="ssd_bwd")
    dxbc_raw, d_conv_a_w, d_conv_a_b = _conv_a_bwd(xbc_raw, xbc_pre, conv_a_w, dxbc, name="conv_a_bwd")
    started = emit_grad("w_in", {
        "in_z": _wgrad(n1, dz, "in_z_wgrad", after=started_branch), "in_xbc": _wgrad(n1, dxbc_raw, "in_xbc_wgrad"),
        "in_dt": _wgrad(n1, ddt, "in_dt_wgrad")[:, :SSD_HEADS], "in_uv": _wgrad(n1, duv, "in_uv_wgrad"),
        "in_gate_a": _wgrad(n1, dg_a, "in_gate_a_wgrad"), "in_gate_b": _wgrad(n1, dg_b, "in_gate_b_wgrad")})
    dn1 = _mm([(dz, wts["in_z"]), (dxbc_raw, wts["in_xbc"]), (ddt, wts["in_dt"]), (duv, wts["in_uv"]),
               (dg_a, wts["in_gate_a"]), (dg_b, wts["in_gate_b"])], "in_dgrad", trans_b=True, after=started,
              out_dtype=BF16)
    dx, _, d_norm1 = _rms_bwd(x, small["norm1_w"], dn1, dh1, name="rms1_bwd")

    grads_small = {
        "norm1_w": d_norm1, "b_gate": jnp.concatenate([dbg_a, dbg_b], axis=1),
        "conv_a_w": d_conv_a_w[:4], "conv_a_b": d_conv_a_b,
        "dt_bias": d_dtb[:, :SSD_HEADS], "a_log": d_alog[:, :SSD_HEADS], "d_skip": d_dskip[:, :SSD_HEADS],
        "ssd_norm_w": d_ssd_nw, "uv_b": d_uvb, "v_ln_w": d_lnw, "v_ln_b": d_lnb,
        "w_spatial": d_wsp[None], "b_spatial": d_bsp_t[:, :SGU_GROUPS].T[None],
        "norm2_w": d_norm2, "conv_f_w": jnp.concatenate([dwf_a[:3], dwf_v[:3]], axis=1),
        "conv_f_b": jnp.concatenate([dbf_a, dbf_v], axis=1), "final_norm_w": d_final.reshape(D_MODEL),
    }
    return loss, dx, grads_small


HBM = pl.BlockSpec(memory_space=pl.ANY)
MESH = pl.DeviceIdType.MESH


def _mesh_pos():
    return lax.axis_index("x"), lax.axis_index("y"), lax.axis_index("c")


def _other_chips(x, y):
    return [(1 - x, y), (x, 1 - y), (1 - x, 1 - y)]


def _remote(src, dst, send_sems, recv_sems, k, dev):
    return pltpu.make_async_remote_copy(src_ref=src, dst_ref=dst, send_sem=send_sems.at[k], recv_sem=recv_sems.at[k],
                                        device_id=dev, device_id_type=MESH)


def _dma_sems(n):
    return [pltpu.SemaphoreType.DMA((n,)), pltpu.SemaphoreType.DMA((n,))]


HBM_ONLY = pl.BlockSpec(memory_space=pltpu.HBM)
SEMAPHORES = pl.BlockSpec(memory_space=pltpu.SEMAPHORE)
DATAFLOW_EFFECT = pltpu.SideEffectType.DATAFLOW_SIDE_EFFECTING
N_PEER_CHIPS = N_CHIPS - 1


def _gather_sends(w_ref, land_ref, send_sems, recv_sems):
    x, y, c = _mesh_pos()
    return [_remote(w_ref.at[c], land_ref.at[2 * x + y, c], send_sems, recv_sems, k, (px, py, c))
            for k, (px, py) in enumerate(_other_chips(x, y))]


def _gather_arrivals(w_ref, land_ref, send_sems, recv_sems):
    x, y, c = _mesh_pos()
    return [_remote(w_ref.at[c], land_ref.at[2 * px + py, c], send_sems, recv_sems, k, (px, py, c))
            for k, (px, py) in enumerate(_other_chips(x, y))]


def _gather_whole_sends(w_ref, land_ref, send_sems, recv_sems):
    x, y, c = _mesh_pos()
    return [_remote(w_ref, land_ref.at[2 * x + y], send_sems, recv_sems, k, (px, py, c))
            for k, (px, py) in enumerate(_other_chips(x, y))]


def _gather_whole_arrivals(w_ref, land_ref, send_sems, recv_sems):
    x, y, c = _mesh_pos()
    return [_remote(w_ref, land_ref.at[2 * px + py], send_sems, recv_sems, k, (px, py, c))
            for k, (px, py) in enumerate(_other_chips(x, y))]


def _scatter_sends(h_ref, land_ref, send_sems, recv_sems):
    x, y, c = _mesh_pos()
    return [_remote(h_ref.at[2 * px + py], land_ref.at[2 * x + y], send_sems, recv_sems, k, (px, py, c))
            for k, (px, py) in enumerate(_other_chips(x, y))]


def _scatter_arrivals(h_ref, land_ref, send_sems, recv_sems):
    x, y, c = _mesh_pos()
    return [_remote(h_ref.at[2 * x + y], land_ref.at[2 * px + py], send_sems, recv_sems, k, (px, py, c))
            for k, (px, py) in enumerate(_other_chips(x, y))]


def _exchange_wait_many(pendings, after, sends, arrivals, *, name):
    n = len(pendings)

    def body(*refs):
        for i in range(n):
            src_ref, land_ref, send_ref, recv_ref = refs[i], refs[n + i], refs[2 * n + i], refs[3 * n + i]
            for cp in sends(src_ref, land_ref, send_ref, recv_ref):
                cp.wait_send()
            for cp in arrivals(src_ref, land_ref, send_ref, recv_ref):
                cp.wait_recv()

    sources = [p[2] for p in pendings]
    landings = [p[3] for p in pendings]
    outs = pl.pallas_call(
        body, name=name,
        out_shape=tuple(pltpu.HBM(a.shape, a.dtype) for a in sources + landings),
        in_specs=[HBM_ONLY] * (2 * n) + [SEMAPHORES] * (2 * n) + [pl.BlockSpec(memory_space=pl.ANY)],
        out_specs=tuple([HBM_ONLY] * (2 * n)), input_output_aliases={i: i for i in range(2 * n)},
        compiler_params=pltpu.CompilerParams(has_side_effects=DATAFLOW_EFFECT),
    )(*sources, *landings, *[p[0] for p in pendings], *[p[1] for p in pendings], after)
    return [(outs[i], outs[n + i]) for i in range(n)]


def _sibling_sends(src_ref, land_ref, send_sems, recv_sems):
    x, y, c = _mesh_pos()
    return [_remote(src_ref, land_ref, send_sems, recv_sems, 0, (x, y, 1 - c))]


def _exchange_start(sources, landing_shapes, sends, *, after=None, name):
    n = len(sources)
    extra = [] if after is None else [after]

    def body(*refs):
        sems = refs[2 * n + len(extra):4 * n + len(extra)]
        for i in range(n):
            send_i = sends[i] if isinstance(sends, (list, tuple)) else sends
            for cp in send_i(refs[i], refs[n + i], sems[2 * i], sems[2 * i + 1]):
                cp.start()
        refs[-1][...] = jnp.zeros_like(refs[-1])

    hbm = [pltpu.HBM(s.shape, s.dtype) for s in sources] + [pltpu.HBM(shp, s.dtype)
                                                             for shp, s in zip(landing_shapes, sources)]
    outs = pl.pallas_call(
        body, name=name,
        out_shape=tuple([pltpu.SemaphoreType.DMA((N_PEER_CHIPS,))] * (2 * n) + hbm
                        + [jax.ShapeDtypeStruct((8, LANES), F32)]),
        in_specs=[HBM_ONLY] * (2 * n) + [pl.BlockSpec(memory_space=pl.ANY)] * len(extra),
        out_specs=tuple([SEMAPHORES] * (2 * n) + [HBM_ONLY] * (2 * n) + [pl.BlockSpec(memory_space=pltpu.VMEM)]),
        input_output_aliases={i: 2 * n + i for i in range(2 * n)},
        compiler_params=pltpu.CompilerParams(has_side_effects=DATAFLOW_EFFECT),
    )(*[pltpu.with_memory_space_constraint(s, pltpu.HBM) for s in sources],
      *[pltpu.with_memory_space_constraint(lax.empty(shp, s.dtype), pltpu.HBM)
        for shp, s in zip(landing_shapes, sources)], *extra)
    pending = [(outs[2 * i], outs[2 * i + 1], outs[2 * n + i], outs[3 * n + i]) for i in range(n)]
    return pending, outs[-1]


def _exchange_wait(pending, after, sends, arrivals, *, name):
    send_sems, recv_sems, source, landing = pending

    def body(src_ref, land_ref, send_ref, recv_ref, after_ref, src_out, land_out):
        for cp in sends(src_ref, land_ref, send_ref, recv_ref):
            cp.wait_send()
        for cp in arrivals(src_ref, land_ref, send_ref, recv_ref):
            cp.wait_recv()

    return pl.pallas_call(
        body, name=name,
        out_shape=(pltpu.HBM(source.shape, source.dtype), pltpu.HBM(landing.shape, landing.dtype)),
        in_specs=[HBM_ONLY, HBM_ONLY, SEMAPHORES, SEMAPHORES, pl.BlockSpec(memory_space=pl.ANY)],
        out_specs=(HBM_ONLY, HBM_ONLY), input_output_aliases={0: 0, 1: 1},
        compiler_params=pltpu.CompilerParams(has_side_effects=DATAFLOW_EFFECT),
    )(source, landing, send_sems, recv_sems, after)


def _gather_ici(shard, *, name):
    _, rh, cols = shard.shape

    def body(w_ref, o_ref, send_sems, recv_sems):
        x, y, c = _mesh_pos()
        mine = 2 * x + y
        sends = []
        for k, (px, py) in enumerate(_other_chips(x, y)):
            cp = _remote(w_ref.at[c], o_ref.at[mine, c], send_sems, recv_sems, k, (px, py, c))
            cp.start()
            sends.append(cp)
        for k, (px, py) in enumerate(_other_chips(x, y)):
            _remote(w_ref.at[c], o_ref.at[2 * px + py, c], send_sems, recv_sems, k, (px, py, c)).wait_recv()
        for cp in sends:
            cp.wait_send()

    return pl.pallas_call(
        body, name=name, in_specs=[HBM], out_specs=HBM,
        out_shape=jax.ShapeDtypeStruct((N_CHIPS, 2, rh, cols), shard.dtype), scratch_shapes=_dma_sems(3),
    )(shard)


def _gather_d2d(parts, *, name):
    def body(a_ref, o_ref, send_sems, recv_sems):
        x, y, c = _mesh_pos()
        sibling = (x, y, 1 - c)
        sends = []
        for k, (px, py) in enumerate(_other_chips(x, y)):
            cp = _remote(a_ref.at[2 * px + py, c], o_ref.at[2 * px + py, c], send_sems, recv_sems, k, sibling)
            cp.start()
            sends.append(cp)
        for k, (px, py) in enumerate(_other_chips(x, y)):
            _remote(a_ref.at[2 * px + py, c], o_ref.at[2 * px + py, 1 - c], send_sems, recv_sems, k, sibling).wait_recv()
        for cp in sends:
            cp.wait_send()

    return pl.pallas_call(
        body, name=name, in_specs=[HBM], out_specs=HBM,
        out_shape=jax.ShapeDtypeStruct(parts.shape, parts.dtype),
        input_output_aliases={0: 0}, scratch_shapes=_dma_sems(3),
    )(parts)


def _all_gather_chips(shard_flat, name):
    rows, cols = shard_flat.shape
    parts = _gather_ici(shard_flat.reshape(2, rows // 2, cols), name=name + "_ici")
    others = _gather_d2d(parts, name=name + "_d2d").reshape(N_CHIPS, rows, cols)
    chip = 2 * lax.axis_index("x") + lax.axis_index("y")
    return lax.dynamic_update_slice(others, shard_flat[None], (chip, 0, 0))


def _row_tile(rows, mult, cap):
    best = mult
    for t in range(mult, min(rows, cap) + 1, mult):
        if rows % t == 0:
            best = t
    assert rows % best == 0, (rows, mult)
    return best


def _swap_halves_d2d(g, *, after=None, name):
    _, _, rh, cols = g.shape
    extra = [] if after is None else [after]

    def body(g_ref, *rest):
        o_ref, send_sems, recv_sems = rest[len(extra):]
        x, y, c = _mesh_pos()
        sibling = (x, y, 1 - c)
        sends = []
        for s in range(N_CHIPS):
            cp = _remote(g_ref.at[s, 1 - c], o_ref.at[s], send_sems, recv_sems, s, sibling)
            cp.start()
            sends.append(cp)
        for s in range(N_CHIPS):
            _remote(g_ref.at[s, c], o_ref.at[s], send_sems, recv_sems, s, sibling).wait_recv()
        for cp in sends:
            cp.wait_send()

    return pl.pallas_call(
        body, name=name, in_specs=[HBM] * (1 + len(extra)), out_specs=HBM,
        out_shape=jax.ShapeDtypeStruct((N_CHIPS, rh, cols), g.dtype), scratch_shapes=_dma_sems(N_CHIPS),
    )(g, *extra)


def _add_own_half(g, arrived, core, *, name):
    _, _, rh, cols = g.shape
    mult = 16 if g.dtype == BF16 else 8
    tr = _row_tile(rh, mult, max(mult, (512 * 1024) // cols))

    def body(core_ref, g_ref, a_ref, o_ref):
        o_ref[...] = (g_ref[0].astype(F32) + a_ref[...].astype(F32)).astype(o_ref.dtype)

    grid_spec = pltpu.PrefetchScalarGridSpec(
        num_scalar_prefetch=1, grid=(N_CHIPS, rh // tr),
        in_specs=[pl.BlockSpec((1, 1, tr, cols), lambda s, i, core_ref: (s, core_ref[0], i, 0)),
                  pl.BlockSpec((1, tr, cols), lambda s, i, core_ref: (s, i, 0))],
        out_specs=pl.BlockSpec((1, tr, cols), lambda s, i, core_ref: (s, i, 0)))
    return pl.pallas_call(
        body, name=name, grid_spec=grid_spec, out_shape=jax.ShapeDtypeStruct((N_CHIPS, rh, cols), g.dtype),
        compiler_params=_params(("parallel", "parallel")),
    )(core, g, arrived)


def _scatter_ici(h, *, name):
    def body(h_ref, o_ref, send_sems, recv_sems):
        x, y, c = _mesh_pos()
        mine = 2 * x + y
        sends = []
        for k, (px, py) in enumerate(_other_chips(x, y)):
            cp = _remote(h_ref.at[2 * px + py], o_ref.at[mine], send_sems, recv_sems, k, (px, py, c))
            cp.start()
            sends.append(cp)
        for k, (px, py) in enumerate(_other_chips(x, y)):
            _remote(h_ref.at[mine], o_ref.at[2 * px + py], send_sems, recv_sems, k, (px, py, c)).wait_recv()
        for cp in sends:
            cp.wait_send()

    others = pl.pallas_call(
        body, name=name, in_specs=[HBM], out_specs=HBM, out_shape=jax.ShapeDtypeStruct(h.shape, h.dtype),
        scratch_shapes=_dma_sems(3),
    )(h)
    chip = 2 * lax.axis_index("x") + lax.axis_index("y")
    own = lax.dynamic_slice_in_dim(h, chip, 1, axis=0)
    return lax.dynamic_update_slice(others, own, (chip, 0, 0))


def _sum_chips(parts, *, name):
    _, rh, cols = parts.shape
    mult = 16 if parts.dtype == BF16 else 8
    tr = _row_tile(rh, mult, max(mult, (512 * 1024) // cols))

    def body(p_ref, o_ref):
        acc = p_ref[0].astype(F32)
        for s in range(1, N_CHIPS):
            acc = acc + p_ref[s].astype(F32)
        o_ref[...] = acc

    return pl.pallas_call(
        body, name=name, grid=(rh // tr,),
        in_specs=[pl.BlockSpec((N_CHIPS, tr, cols), lambda i: (0, i, 0))],
        out_specs=pl.BlockSpec((tr, cols), lambda i: (i, 0)),
        out_shape=jax.ShapeDtypeStruct((rh, cols), F32), compiler_params=_params(("parallel",)),
    )(parts)


def _sum_chips_with_own(landed, sent, chip, *, name):
    _, rh, cols = landed.shape
    mult = 16 if landed.dtype == BF16 else 8
    tr = _row_tile(rh, mult, max(mult, (512 * 1024) // cols))

    def body(chip_ref, own_ref, px_ref, py_ref, pxy_ref, o_ref):
        acc = own_ref[0].astype(F32)
        for p_ref in (px_ref, py_ref, pxy_ref):
            acc = acc + p_ref[0].astype(F32)
        o_ref[...] = acc

    def block_of(flip):
        return pl.BlockSpec((1, tr, cols), lambda i, chip_ref: (chip_ref[0] ^ flip, i, 0))

    grid_spec = pltpu.PrefetchScalarGridSpec(
        num_scalar_prefetch=1, grid=(rh // tr,),
        in_specs=[block_of(0), block_of(2), block_of(1), block_of(3)],
        out_specs=pl.BlockSpec((tr, cols), lambda i, chip_ref: (i, 0)))
    return pl.pallas_call(
        body, name=name, grid_spec=grid_spec, out_shape=jax.ShapeDtypeStruct((rh, cols), F32),
        compiler_params=_params(("parallel",)),
    )(chip, sent, landed, landed, landed)


def _share_d2d(f, *, name):
    fs = f if isinstance(f, (list, tuple)) else [f]
    others = _swap_with_sibling(fs, name=name)
    first = lax.axis_index("c") == 0
    both = [jnp.stack([jnp.where(first, a, b), jnp.where(first, b, a)]) for a, b in zip(fs, others)]
    return both if isinstance(f, (list, tuple)) else both[0]


def _swap_with_sibling(fs, *, name):
    n = len(fs)

    def body(*refs):
        x, y, c = _mesh_pos()
        sibling = (x, y, 1 - c)
        send_sems, recv_sems = refs[2 * n:]
        copies = [_remote(refs[i], refs[n + i], send_sems, recv_sems, i, sibling) for i in range(n)]
        for cp in copies:
            cp.start()
        for cp in copies:
            cp.wait()

    return pl.pallas_call(
        body, name=name, in_specs=[HBM] * n, out_specs=[HBM] * n,
        out_shape=[jax.ShapeDtypeStruct(a.shape, a.dtype) for a in fs], scratch_shapes=_dma_sems(n),
    )(*fs)


def _reduce_scatter_chips(g, core, name, after=None):
    _, rows, cols = g.shape
    g = g.reshape(N_CHIPS, 2, rows // 2, cols)
    arrived = _swap_halves_d2d(g, after=after, name=name + "_swap")
    chip_sum = _add_own_half(g, arrived, core, name=name + "_add2")
    parts = _scatter_ici(chip_sum, name=name + "_ici")
    total = _sum_chips(parts, name=name + "_sum4")
    return _share_d2d(total, name=name + "_share").reshape(rows, cols)


BIG = ("w_in", "w_branch", "w_out", "w_up", "w_down")
BIG_COLUMN_SHARDED = ("w_in", "w_up")
CONV = ("conv_a_w", "conv_f_w")
REPLICATED = ("norm1_w", "b_gate", "conv_a_b", "dt_bias", "a_log", "d_skip", "ssd_norm_w", "uv_b", "v_ln_w",
              "v_ln_b", "w_spatial", "b_spatial", "norm2_w", "conv_f_b", "final_norm_w")
WEIGHT_ORDER = ("norm1_w", "w_in", "b_gate", "conv_a_w", "conv_a_b", "dt_bias", "a_log", "d_skip", "ssd_norm_w",
                "uv_b", "v_ln_w", "v_ln_b", "w_spatial", "b_spatial", "w_branch", "w_out", "norm2_w", "w_up",
                "conv_f_w", "conv_f_b", "w_down", "final_norm_w")
SMALL_EXCHANGE_ROWS = 64


_GATE0 = SSD_IN + 2 * SGU_WIDTH
IN_SEGMENTS = {
    "in_z": (0, SSD_D_INNER), "in_xbc": (SSD_D_INNER, SSD_D_INNER + SSD_XBC), "in_dt": (SSD_D_INNER + SSD_XBC, SSD_IN),
    "in_uv": (SSD_IN, _GATE0), "in_gate": (_GATE0, IN_COLS), "in_gate_a": (_GATE0, _GATE0 + D_MODEL),
    "in_gate_b": (_GATE0 + D_MODEL, IN_COLS),
}
IN_GRAD_SEGMENTS = ("in_z", "in_xbc", "in_dt", "in_uv", "in_gate_a", "in_gate_b")


def _take_columns(parts, start, stop):
    out = []
    for a, first in parts:
        lo, hi = max(start, first), min(stop, first + a.shape[1])
        if lo < hi:
            out.append(a[:, lo - first:hi - first])
    return out[0] if len(out) == 1 else jnp.concatenate(out, axis=1)


def _flat_rows(arrays, row_multiple):
    flat = jnp.concatenate([a.reshape(-1) for a in arrays])
    rows = -(-flat.shape[0] // (LANES * row_multiple)) * row_multiple
    return jnp.pad(flat, (0, rows * LANES - flat.shape[0])).reshape(rows, LANES)


def _unflatten(flat, shapes):
    flat = flat.reshape(-1)
    out, off = [], 0
    for shp in shapes:
        n = math.prod(shp)
        out.append(flat[off:off + n].reshape(shp))
        off += n
    return out


def _from_chip_blocks(blocks, name):
    if name in BIG_COLUMN_SHARDED or name in CONV:
        k = blocks.shape[1]
        return jnp.transpose(blocks, (1, 0, 2)).reshape(k, -1)
    return blocks.reshape(-1, blocks.shape[-1])


def _to_chip_blocks(whole, name):
    if name in BIG_COLUMN_SHARDED or name in CONV:
        k, n = whole.shape
        return jnp.transpose(whole.reshape(k, N_CHIPS, n // N_CHIPS), (1, 0, 2))
    return whole.reshape(N_CHIPS, whole.shape[0] // N_CHIPS, whole.shape[1])


def kernel(x, norm1_w, w_in, b_gate, conv_a_w, conv_a_b, dt_bias, a_log, d_skip, ssd_norm_w, uv_b, v_ln_w, v_ln_b, w_spatial, b_spatial, w_branch, w_out, norm2_w, w_up, conv_f_w, conv_f_b, w_down, final_norm_w, loss_target, m_norm1_w, m_w_in, m_b_gate, m_conv_a_w, m_conv_a_b, m_dt_bias, m_a_log, m_d_skip, m_ssd_norm_w, m_uv_b, m_v_ln_w, m_v_ln_b, m_w_spatial, m_b_spatial, m_w_branch, m_w_out, m_norm2_w, m_w_up, m_conv_f_w, m_conv_f_b, m_w_down, m_final_norm_w, v_norm1_w, v_w_in, v_b_gate, v_conv_a_w, v_conv_a_b, v_dt_bias, v_a_log, v_d_skip, v_ssd_norm_w, v_uv_b, v_v_ln_w, v_v_ln_b, v_w_spatial, v_b_spatial, v_w_branch, v_w_out, v_norm2_w, v_w_up, v_conv_f_w, v_conv_f_b, v_w_down, v_final_norm_w):
    weights = dict(norm1_w=norm1_w, w_in=w_in, b_gate=b_gate, conv_a_w=conv_a_w, conv_a_b=conv_a_b, dt_bias=dt_bias,
                   a_log=a_log, d_skip=d_skip, ssd_norm_w=ssd_norm_w, uv_b=uv_b, v_ln_w=v_ln_w, v_ln_b=v_ln_b,
                   w_spatial=w_spatial, b_spatial=b_spatial, w_branch=w_branch, w_out=w_out, norm2_w=norm2_w,
                   w_up=w_up, conv_f_w=conv_f_w, conv_f_b=conv_f_b, w_down=w_down, final_norm_w=final_norm_w)
    mom1 = dict(norm1_w=m_norm1_w, w_in=m_w_in, b_gate=m_b_gate, conv_a_w=m_conv_a_w, conv_a_b=m_conv_a_b,
                dt_bias=m_dt_bias, a_log=m_a_log, d_skip=m_d_skip, ssd_norm_w=m_ssd_norm_w, uv_b=m_uv_b,
                v_ln_w=m_v_ln_w, v_ln_b=m_v_ln_b, w_spatial=m_w_spatial, b_spatial=m_b_spatial, w_branch=m_w_branch,
                w_out=m_w_out, norm2_w=m_norm2_w, w_up=m_w_up, conv_f_w=m_conv_f_w, conv_f_b=m_conv_f_b,
                w_down=m_w_down, final_norm_w=m_final_norm_w)
    mom2 = dict(norm1_w=v_norm1_w, w_in=v_w_in, b_gate=v_b_gate, conv_a_w=v_conv_a_w, conv_a_b=v_conv_a_b,
                dt_bias=v_dt_bias, a_log=v_a_log, d_skip=v_d_skip, ssd_norm_w=v_ssd_norm_w, uv_b=v_uv_b,
                v_ln_w=v_v_ln_w, v_ln_b=v_v_ln_b, w_spatial=v_w_spatial, b_spatial=v_b_spatial, w_branch=v_w_branch,
                w_out=v_w_out, norm2_w=v_norm2_w, w_up=v_w_up, conv_f_w=v_conv_f_w, conv_f_b=v_conv_f_b,
                w_down=v_w_down, final_norm_w=v_final_norm_w)
    chip = 2 * lax.axis_index("x") + lax.axis_index("y")
    core = lax.axis_index("c").astype(jnp.int32).reshape(1)

    whole = {}
    conv_shapes = [weights[n].shape[1:] for n in CONV]
    conv_gathered = _all_gather_chips(_flat_rows([weights[n] for n in CONV], 16), "gather_conv").reshape(N_CHIPS, -1)
    off = 0
    for n, shp in zip(CONV, conv_shapes):
        size = math.prod(shp)
        whole[n] = _from_chip_blocks(conv_gathered[:, off:off + size].reshape((N_CHIPS,) + shp), n)
        off += size
    shard_shapes = {n: weights[n].shape[1:] for n in BIG}
    halves = [weights[n][0].astype(BF16).reshape(2, shard_shapes[n][0] // 2, shard_shapes[n][1]) for n in BIG]
    sends = [_gather_sends if n == "w_in" else _gather_whole_sends for n in BIG]
    gathers, gathers_started = _exchange_start(halves, [(N_CHIPS,) + h.shape for h in halves], sends,
                                               after=conv_gathered, name="gather_start")
    gathers = dict(zip(BIG, gathers))

    def get_weight(name, after):
        rows, cols = shard_shapes[name]
        if name == "w_in":
            own, landed = _exchange_wait(gathers[name], after, _gather_sends, _gather_arrivals,
                                         name="gather_" + name + "_wait")
            landed = _gather_d2d(landed, name="gather_" + name + "_d2d")
        else:
            own, landed = _exchange_wait(gathers[name], after, _gather_whole_sends, _gather_whole_arrivals,
                                         name="gather_" + name + "_wait")
        blocks = lax.dynamic_update_slice(landed.reshape(N_CHIPS, rows, cols), own.reshape(1, rows, cols),
                                          (chip, 0, 0))
        if name == "w_up":
            return {"up": blocks}
        if name == "w_in":
            parts = [(blocks[k], cols * k) for k in range(N_CHIPS)]
            segs = {n: _take_columns(parts, a, b) for n, (a, b) in IN_SEGMENTS.items()}
            segs["in_dt"] = jnp.pad(segs["in_dt"], ((0, 0), (0, LANES - SSD_HEADS)))
            return segs
        full = _from_chip_blocks(blocks, name)
        if name == "w_branch":
            return {"branch_a": full[:SSD_D_INNER], "branch_b": full[SSD_D_INNER:]}
        return {name[2:]: full}

    small = {n: weights[n] for n in REPLICATED}
    small["conv_a_w"] = whole["conv_a_w"]
    small["conv_f_w"] = whole["conv_f_w"]
    small["gathers_started"] = gathers_started

    reductions = {}

    def emit_grad(name, g):
        if name == "w_in":
            parts = [(g[n], IN_SEGMENTS[n][0]) for n in IN_GRAD_SEGMENTS]
            cols = shard_shapes[name][1]
            g_blocks = jnp.stack([_take_columns(parts, cols * k, cols * (k + 1)) for k in range(N_CHIPS)])
        else:
            g_blocks = g if name == "w_up" else _to_chip_blocks(g, name)
        if name == "w_in":
            _, rows, cols = g_blocks.shape
            g_halves = g_blocks.reshape(N_CHIPS, 2, rows // 2, cols)
            arrived = _swap_halves_d2d(g_halves, name="reduce_" + name + "_swap")
            g_blocks = _add_own_half(g_halves, arrived, core, name="reduce_" + name + "_add2")
        (pending,), started = _exchange_start([g_blocks], [g_blocks.shape], _scatter_sends,
                                              name="reduce_" + name + "_start")
        reductions[name] = pending
        return started

    loss, dx, grads_small = _local_step(x[0], loss_target[0], get_weight, small, emit_grad)

    order = ("w_down", "w_up", "w_out", "w_branch", "w_in")
    core_sums = []
    chip_index = chip.astype(jnp.int32).reshape(1)
    for n in order:
        sent, landed = _exchange_wait(reductions[n], dx, _scatter_sends, _scatter_arrivals,
                                      name="reduce_" + n + "_wait")
        core_sums.append(_sum_chips_with_own(landed, sent, chip_index, name="reduce_" + n + "_sum4"))
    swaps, swaps_started = _exchange_start(core_sums, [a.shape for a in core_sums], _sibling_sends, name="reduce_swap_start")
    grads = {}

    small_names = REPLICATED + CONV + ("loss",)
    grads_small = dict(grads_small, loss=loss)
    small_shapes = [grads_small[n].shape for n in small_names]
    g_small = _flat_rows([grads_small[n] for n in small_names], N_CHIPS * 2 * SMALL_EXCHANGE_ROWS)
    red_small = _reduce_scatter_chips(g_small.reshape(N_CHIPS, -1, LANES), core, "reduce_small", after=swaps_started)
    all_small = _all_gather_chips(red_small, "gather_small")
    swapped = _exchange_wait_many(swaps, all_small, _sibling_sends, _sibling_sends, name="reduce_swap_wait")
    core_sums = {n: own for n, (own, _) in zip(order, swapped)}
    sibling_sums = {n: other for n, (_, other) in zip(order, swapped)}
    first = lax.axis_index("c") == 0
    w_in_halves = (core_sums["w_in"], sibling_sums["w_in"])
    w_in_grad = jnp.concatenate([jnp.where(first, w_in_halves[0], w_in_halves[1]),
                                 jnp.where(first, w_in_halves[1], w_in_halves[0])], axis=0)
    for n, g in zip(small_names, _unflatten(all_small, small_shapes)):
        if n == "loss":
            total_loss = g[0, 0]
            continue
        if n in CONV:
            width = g.shape[1] // N_CHIPS
            g = lax.dynamic_slice_in_dim(g, chip * width, width, axis=1)
        grads[n] = g.reshape(weights[n].shape[1:]) if n != "final_norm_w" else g

    delta, new_m, new_v = {}, {}, {}
    for n in BIG:
        shp = weights[n].shape
        if n == "w_in":
            g_t = w_in_grad.T
            results = [g_t] + list(_adamw(weights[n][0].T, g_t, mom1[n][0].T, mom2[n][0].T, name="adamw_" + n,
                                          tr=_row_tile(g_t.shape[0], 8, 136)))
            results = [a.T for a in results]
        else:
            results = _adamw_two_sums(weights[n][0], core_sums[n], sibling_sums[n], mom1[n][0], mom2[n][0],
                                      name="adamw_" + n, tr=_row_tile(shp[1], 8, 352))
        grads[n], delta[n], new_m[n], new_v[n] = [a.reshape(shp) for a in results]
    small_all = [n for n in WEIGHT_ORDER if n not in BIG]

    def as_2d(a):
        return a.reshape(-1, a.shape[-1])

    results = _adamw_many(*[[as_2d(src[n]) for n in small_all] for src in (weights, grads, mom1, mom2)],
                          name="adamw_small")
    for n, dv, mv, vv in zip(small_all, *results):
        shp = weights[n].shape
        delta[n], new_m[n], new_v[n] = dv.reshape(shp), mv.reshape(shp), vv.reshape(shp)

    grad_out = [grads[n].reshape(weights[n].shape) for n in WEIGHT_ORDER]
    return (total_loss, dx[None], *grad_out, *[delta[n] for n in WEIGHT_ORDER], *[new_m[n] for n in WEIGHT_ORDER],
            *[new_v[n] for n in WEIGHT_ORDER])
```

```python
import functools
import math

import jax
import jax.numpy as jnp
from jax import lax
from jax.experimental import pallas as pl
from jax.experimental.pallas import tpu as pltpu

F32 = jnp.float32
BF16 = jnp.bfloat16

D_MODEL = 1024
SSD_D_INNER = 2048
SSD_HEADS = 32
SSD_HEAD_DIM = 64
SSD_GROUPS = 4
SSD_HEADS_PER_GROUP = 8
SSD_STATE = 128
SSD_BC = 512
SSD_XBC = 3072
SSD_IN = 5152
SGU_WIDTH = 1024
SGU_GROUPS = 8
CHUNK = 128
IN_COLS = 9248
D_FF = 2816
NORM_EPS = 1e-6
LN_EPS = 1e-5
GROUP_COLS = SSD_HEADS_PER_GROUP * SSD_HEAD_DIM
LANES = 128

ADAM_LR = 0.001
ADAM_B1 = 0.9
ADAM_B2 = 0.999
ADAM_EPS = 1e-08
ADAM_WD = 0.01
ADAM_STEP = 10

N_CHIPS = 4
VMEM_LIMIT = 56 * 1024 * 1024

NT = (((1,), (1,)), ((), ()))
TN = (((0,), (0,)), ((), ()))
NN = (((1,), (0,)), ((), ()))


def _params(dims):
    return pltpu.CompilerParams(dimension_semantics=dims, vmem_limit_bytes=VMEM_LIMIT)


def _dot(a, b, dn=NN, precision=None):
    return lax.dot_general(a, b, dn, precision=precision, preferred_element_type=F32)


def _split3(x):
    hi = x.astype(BF16)
    rest = x - hi.astype(F32)
    mid = rest.astype(BF16)
    return hi, mid, (rest - mid.astype(F32)).astype(BF16)


def _dot_terms(terms, exact, dn=NN):
    out = None
    for t in terms:
        p = _dot(t, exact, dn)
        out = p if out is None else out + p
    return out


def _dot_exact_lhs(exact, terms):
    out = None
    for t in terms:
        p = _dot(exact, t)
        out = p if out is None else out + p
    return out


def _sigmoid(x):
    return 1.0 / (1.0 + jnp.exp(-x))


def _softplus(x):
    return jnp.maximum(x, 0.0) + jnp.log(1.0 + jnp.exp(-jnp.abs(x)))


def _matmul(pairs, *, trans_b=False, add=None, after=None, out_dtype=F32, tm=512, tn=512, name):
    def mat_shape(b):
        if isinstance(b, tuple) and b[1] == "cols":
            return (b[0].shape[1], b[0].shape[0] * b[0].shape[2])
        return b[0].shape[1:] if isinstance(b, tuple) else b.shape

    if isinstance(pairs[0][1], tuple) and pairs[0][1][1] == "cols":
        assert not trans_b and tn % LANES == 0 and pairs[0][1][0].shape[2] % tn == 0, name

    m = (pairs[0][0][0] if isinstance(pairs[0][0], tuple) else pairs[0][0]).shape[0]
    n = mat_shape(pairs[0][1])[0] if trans_b else mat_shape(pairs[0][1])[1]
    tm, tn = min(tm, m), min(tn, n)
    assert m % tm == 0 and n % tn == 0, (name, m, n, tm, tn)
    npairs = len(pairs)
    dn = NT if trans_b else NN

    def body(*refs):
        o_ref = refs[-1]
        acc = None
        for i in range(npairs):
            p = _dot(refs[2 * i][...].astype(BF16), refs[2 * i + 1][...].astype(BF16), dn)
            acc = p if acc is None else acc + p
        if add is not None:
            acc = acc + refs[2 * npairs][...]
        o_ref[...] = acc.astype(out_dtype)

    in_specs, args = [], []
    for a, b in pairs:
        bshape = mat_shape(b)
        k = bshape[1] if trans_b else bshape[0]
        assert bshape == ((n, k) if trans_b else (k, n)), (name, bshape)
        a, qa = a if isinstance(a, tuple) else (a, 0)
        assert a.shape[0] == m and a.shape[1] % k == 0, (name, a.shape, k)
        in_specs.append(pl.BlockSpec((tm, k), lambda i, j, qa=qa: (i, qa)))
        if isinstance(b, tuple) and b[1] == "cols":
            b = b[0]
            per = b.shape[2] // tn
            in_specs.append(pl.BlockSpec((None, k, tn), lambda i, j, per=per: (j // per, 0, j % per)))
        elif isinstance(b, tuple):
            b, qb = b
            if trans_b:
                in_specs.append(pl.BlockSpec((None, tn, k), lambda i, j, qb=qb: (qb, j, 0)))
            else:
                in_specs.append(pl.BlockSpec((None, k, tn), lambda i, j, qb=qb: (qb, 0, j)))
        elif trans_b:
            in_specs.append(pl.BlockSpec((tn, k), lambda i, j: (j, 0)))
        else:
            in_specs.append(pl.BlockSpec((k, tn), lambda i, j: (0, j)))
        args += [a, b]
    if add is not None:
        in_specs.append(pl.BlockSpec((tm, tn), lambda i, j: (i, j)))
        args.append(add)
    if after is not None:
        in_specs.append(pl.BlockSpec(memory_space=pl.ANY))
        args.append(after)
    return pl.pallas_call(
        body, name=name, grid=(m // tm, n // tn), in_specs=in_specs,
        out_specs=pl.BlockSpec((tm, tn), lambda i, j: (i, j)),
        out_shape=jax.ShapeDtypeStruct((m, n), out_dtype),
        compiler_params=_params(("parallel", "parallel")),
    )(*args)


def _matmul_tn(a, b, *, tk, tn, tm=1024, out_dtype=BF16, stack_out=False, after=None, name):
    m, k = a.shape
    n = b.shape[1]
    tm, tk, tn = min(tm, m), min(tk, k), min(tn, n)
    assert m % tm == 0 and k % tk == 0 and n % tn == 0, (name, m, k, n)
    nm = m // tm
    if stack_out:
        out_spec = pl.BlockSpec((None, tk, tn), lambda i, j, l: (j, i, 0))
        out_shape = jax.ShapeDtypeStruct((n // tn, k, tn), out_dtype)
    else:
        out_spec = pl.BlockSpec((tk, tn), lambda i, j, l: (i, j))
        out_shape = jax.ShapeDtypeStruct((k, n), out_dtype)

    def body(a_ref, b_ref, *rest):
        o_ref, acc = rest[-2:]
        mi = pl.program_id(2)

        @pl.when(mi == 0)
        def _():
            acc[...] = jnp.zeros_like(acc)

        acc[...] += _dot(a_ref[...].astype(BF16), b_ref[...].astype(BF16), TN)

        @pl.when(mi == nm - 1)
        def _():
            o_ref[...] = acc[...].astype(out_dtype)

    in_specs = [pl.BlockSpec((tm, tk), lambda i, j, l: (l, i)), pl.BlockSpec((tm, tn), lambda i, j, l: (l, j))]
    args = [a, b]
    if after is not None:
        in_specs.append(pl.BlockSpec(memory_space=pl.ANY))
        args.append(after)
    return pl.pallas_call(
        body, name=name, grid=(k // tk, n // tn, nm), in_specs=in_specs,
        out_specs=out_spec, out_shape=out_shape,
        scratch_shapes=[pltpu.VMEM((tk, tn), F32)],
        compiler_params=_params(("parallel", "parallel", "arbitrary")),
    )(*args)


def _rms_fwd(x, w, *, after=None, name, tm=512):
    s, d = x.shape
    tm = min(tm, s)
    extra = [] if after is None else [after]

    def body(x_ref, w_ref, *rest):
        o_ref = rest[-1]
        xv = x_ref[...]
        r = lax.rsqrt(jnp.mean(xv * xv, axis=-1, keepdims=True) + NORM_EPS)
        o_ref[...] = (xv * r * w_ref[...]).astype(BF16)

    return pl.pallas_call(
        body, name=name, grid=(s // tm,),
        in_specs=[pl.BlockSpec((tm, d), lambda i: (i, 0)), pl.BlockSpec((1, d), lambda i: (0, 0))]
        + [pl.BlockSpec(memory_space=pl.ANY)] * len(extra),
        out_specs=pl.BlockSpec((tm, d), lambda i: (i, 0)),
        out_shape=jax.ShapeDtypeStruct((s, d), BF16),
        compiler_params=_params(("parallel",)),
    )(x, w, *extra)


def _rms_bwd(x, w, dn, dres, *, name, tm=512):
    s, d = x.shape
    tm = min(tm, s)

    def body(x_ref, w_ref, dn_ref, dres_ref, dx_ref, dxb_ref, dw_ref):
        @pl.when(pl.program_id(0) == 0)
        def _():
            dw_ref[...] = jnp.zeros_like(dw_ref)

        xv = x_ref[...]
        r = lax.rsqrt(jnp.mean(xv * xv, axis=-1, keepdims=True) + NORM_EPS)
        xhat = xv * r
        dnv = dn_ref[...].astype(F32)
        dxhat = dnv * w_ref[...]
        dx = dres_ref[...] + r * (dxhat - xhat * jnp.mean(dxhat * xhat, axis=-1, keepdims=True))
        dx_ref[...] = dx
        dxb_ref[...] = dx.astype(BF16)
        dw_ref[...] += jnp.sum(dnv * xhat, axis=0, keepdims=True)

    tile = pl.BlockSpec((tm, d), lambda i: (i, 0))
    row = pl.BlockSpec((1, d), lambda i: (0, 0))
    return pl.pallas_call(
        body, name=name, grid=(s // tm,),
        in_specs=[tile, row, tile, tile], out_specs=[tile, tile, row],
        out_shape=[jax.ShapeDtypeStruct((s, d), F32), jax.ShapeDtypeStruct((s, d), BF16),
                   jax.ShapeDtypeStruct((1, d), F32)],
        compiler_params=_params(("arbitrary",)),
    )(x, w, dn, dres)


def _final_fwd_bwd(h2, wf, target, *, name, tm=512):
    s, d = h2.shape
    tm = min(tm, s)

    def body(h_ref, w_ref, t_ref, loss_ref, dh_ref, dhb_ref, dw_ref):
        @pl.when(pl.program_id(0) == 0)
        def _():
            dw_ref[...] = jnp.zeros_like(dw_ref)
            loss_ref[...] = jnp.zeros_like(loss_ref)

        hv = h_ref[...]
        r = lax.rsqrt(jnp.mean(hv * hv, axis=-1, keepdims=True) + NORM_EPS)
        xhat = hv * r
        err = xhat * w_ref[...] - t_ref[...]
        per_tok = jnp.mean(err * err, axis=-1, keepdims=True)
        loss_ref[...] += 0.5 * jnp.sum(per_tok, axis=0, keepdims=True)
        dy = err * (1.0 / d)
        dxhat = dy * w_ref[...]
        dh = r * (dxhat - xhat * jnp.mean(dxhat * xhat, axis=-1, keepdims=True))
        dh_ref[...] = dh
        dhb_ref[...] = dh.astype(BF16)
        dw_ref[...] += jnp.sum(dy * xhat, axis=0, keepdims=True)

    tile = pl.BlockSpec((tm, d), lambda i: (i, 0))
    row = pl.BlockSpec((1, d), lambda i: (0, 0))
    return pl.pallas_call(
        body, name=name, grid=(s // tm,),
        in_specs=[tile, row, tile],
        out_specs=[pl.BlockSpec((1, 1), lambda i: (0, 0)), tile, tile, row],
        out_shape=[jax.ShapeDtypeStruct((1, 1), F32), jax.ShapeDtypeStruct((s, d), F32),
                   jax.ShapeDtypeStruct((s, d), BF16), jax.ShapeDtypeStruct((1, d), F32)],
        compiler_params=_params(("arbitrary",)),
    )(h2, wf, target)


CONV_ROWS = 256
CONV_ROWS_FWD = 512
HALO = 8


def _rows_with_halo(ref, r0, rows, s, before, after):
    tile = 16 if ref.dtype == BF16 else HALO
    parts = []
    if before:
        prev = ref[pl.ds(pl.multiple_of(jnp.maximum(r0 - tile, 0), tile), tile), :].astype(F32)[tile - HALO:]
        parts.append(jnp.where(r0 > 0, prev, 0.0))
    parts.append(ref[pl.ds(r0, rows), :].astype(F32))
    if after:
        nxt = ref[pl.ds(pl.multiple_of(jnp.minimum(r0 + rows, s - tile), tile), tile), :].astype(F32)[:HALO]
        parts.append(jnp.where(r0 + rows < s, nxt, 0.0))
    return jnp.concatenate(parts, axis=0) if len(parts) > 1 else parts[0]


def _window(x_ref, r0, s, after):
    return _rows_with_halo(x_ref, r0, CONV_ROWS_FWD, s, True, after).astype(F32)


def _shifted(window, k, rows):
    if k == 0:
        return window[HALO:HALO + rows]
    return pltpu.roll(window, k, 0)[HALO:HALO + rows]


def _conv_taps(window, w_ref, kk, rows):
    acc = None
    for i in range(kk):
        term = w_ref[i:i + 1, :] * _shifted(window, kk - 1 - i, rows)
        acc = term if acc is None else acc + term
    return acc


def _row_loop(rows, step):
    def body(r, carry):
        return step(pl.multiple_of(r * rows, rows), carry)
    return body


def _conv_bwd_rows(x, dpe, w_ref, kk):
    dp = dpe[:CONV_ROWS]
    dx = None
    dws = []
    for i in range(kk):
        k = kk - 1 - i
        later = dp if k == 0 else pltpu.roll(dpe, dpe.shape[0] - k, 0)[:CONV_ROWS]
        dws.append(jnp.sum(later * x, axis=0, keepdims=True))
        term = w_ref[i:i + 1, :] * later
        dx = term if dx is None else dx + term
    return dx, dws, jnp.sum(dp, axis=0, keepdims=True)


def _conv_a_fwd(xraw, w, b, *, name, tc=128):
    s, c = xraw.shape
    kk = 4

    def body(x_ref, w_ref, b_ref, o_ref, pre_ref):
        def step(r0, carry):
            pre = _conv_taps(_window(x_ref, r0, s, False), w_ref, kk, CONV_ROWS_FWD) + b_ref[...]
            o_ref[pl.ds(r0, CONV_ROWS_FWD), :] = pre * _sigmoid(pre)
            pre_ref[pl.ds(r0, CONV_ROWS_FWD), :] = pre.astype(BF16)
            return carry

        lax.fori_loop(0, s // CONV_ROWS_FWD, _row_loop(CONV_ROWS_FWD, step), 0)

    col = pl.BlockSpec((s, tc), lambda j: (0, j))
    return pl.pallas_call(
        body, name=name, grid=(c // tc,),
        in_specs=[col, pl.BlockSpec((8, tc), lambda j: (0, j)), pl.BlockSpec((1, tc), lambda j: (0, j))],
        out_specs=[col, col], out_shape=[jax.ShapeDtypeStruct((s, c), F32), jax.ShapeDtypeStruct((s, c), BF16)],
        compiler_params=_params(("parallel",)),
    )(xraw, w, b)


def _conv_a_bwd(xraw, pre, w, dy, *, name, tc=128):
    s, c = xraw.shape
    kk = 4

    def body(x_ref, pre_ref, w_ref, dy_ref, dx_ref, dw_ref, db_ref):
        def step(r0, carry):
            pre = _rows_with_halo(pre_ref, r0, CONV_ROWS, s, False, True)
            sg = _sigmoid(pre)
            dpe = _rows_with_halo(dy_ref, r0, CONV_ROWS, s, False, True) * (sg * (1.0 + pre * (1.0 - sg)))
            dx, dws, db = _conv_bwd_rows(x_ref[pl.ds(r0, CONV_ROWS), :].astype(F32), dpe, w_ref, kk)
            dx_ref[pl.ds(r0, CONV_ROWS), :] = dx.astype(BF16)
            return tuple(acc + new for acc, new in zip(carry, dws + [db]))

        zero = jnp.zeros((1, tc), F32)
        sums = lax.fori_loop(0, s // CONV_ROWS, _row_loop(CONV_ROWS, step), (zero,) * (kk + 1))
        db_ref[...] = sums[kk]
        dw_ref[...] = jnp.concatenate(list(sums[:kk]) + [jnp.zeros((8 - kk, tc), F32)], axis=0)

    col = pl.BlockSpec((s, tc), lambda j: (0, j))
    w8 = pl.BlockSpec((8, tc), lambda j: (0, j))
    row = pl.BlockSpec((1, tc), lambda j: (0, j))
    return pl.pallas_call(
        body, name=name, grid=(c // tc,),
        in_specs=[col, col, w8, col], out_specs=[col, w8, row],
        out_shape=[jax.ShapeDtypeStruct((s, c), BF16), jax.ShapeDtypeStruct((8, c), F32),
                   jax.ShapeDtypeStruct((1, c), F32)],
        compiler_params=_params(("parallel",)),
    )(xraw, pre, w, dy)


def _conv_f_fwd(up_raw, w, b, *, name, tc=128):
    s, c2 = up_raw.shape
    c = c2 // 2
    nb = c // tc
    kk = 3

    def body(xa_ref, xv_ref, wa_ref, wv_ref, ba_ref, bv_ref, o_ref, a_out, v_out):
        def step(r0, carry):
            a = _conv_taps(_window(xa_ref, r0, s, False), wa_ref, kk, CONV_ROWS_FWD) + ba_ref[...]
            v = _conv_taps(_window(xv_ref, r0, s, False), wv_ref, kk, CONV_ROWS_FWD) + bv_ref[...]
            o_ref[pl.ds(r0, CONV_ROWS_FWD), :] = (a * _sigmoid(a) * v).astype(BF16)
            a_out[pl.ds(r0, CONV_ROWS_FWD), :] = a.astype(BF16)
            v_out[pl.ds(r0, CONV_ROWS_FWD), :] = v.astype(BF16)
            return carry

        lax.fori_loop(0, s // CONV_ROWS_FWD, _row_loop(CONV_ROWS_FWD, step), 0)

    col_a = pl.BlockSpec((s, tc), lambda j: (0, j))
    col_v = pl.BlockSpec((s, tc), lambda j: (0, j + nb))
    half = jax.ShapeDtypeStruct((s, c), BF16)
    return pl.pallas_call(
        body, name=name, grid=(nb,),
        in_specs=[col_a, col_v, pl.BlockSpec((8, tc), lambda j: (0, j)), pl.BlockSpec((8, tc), lambda j: (0, j + nb)),
                  pl.BlockSpec((1, tc), lambda j: (0, j)), pl.BlockSpec((1, tc), lambda j: (0, j + nb))],
        out_specs=[col_a, col_a, col_a], out_shape=[half, half, half],
        compiler_params=_params(("parallel",)),
    )(up_raw, up_raw, w, w, b, b)


def _conv_f_bwd(up_raw, a_pre, v_pre, w, dact, *, name, tc=128):
    s, c2 = up_raw.shape
    c = c2 // 2
    nb = c // tc
    kk = 3

    def body(xa_ref, xv_ref, a_ref, v_ref, wa_ref, wv_ref, d_ref,
             dxa_ref, dxv_ref, dwa_ref, dwv_ref, dba_ref, dbv_ref):
        def step(r0, carry):
            a = _rows_with_halo(a_ref, r0, CONV_ROWS, s, False, True)
            v = _rows_with_halo(v_ref, r0, CONV_ROWS, s, False, True)
            sg = _sigmoid(a)
            d = _rows_with_halo(d_ref, r0, CONV_ROWS, s, False, True)
            rows = pl.ds(r0, CONV_ROWS)
            dxa, dwas, dba = _conv_bwd_rows(xa_ref[rows, :].astype(F32), d * v * (sg * (1.0 + a * (1.0 - sg))),
                                            wa_ref, kk)
            dxv, dwvs, dbv = _conv_bwd_rows(xv_ref[rows, :].astype(F32), d * (a * sg), wv_ref, kk)
            dxa_ref[pl.ds(r0, CONV_ROWS), :] = dxa.astype(BF16)
            dxv_ref[pl.ds(r0, CONV_ROWS), :] = dxv.astype(BF16)
            return tuple(acc + new for acc, new in zip(carry, dwas + [dba] + dwvs + [dbv]))

        zero = jnp.zeros((1, tc), F32)
        sums = lax.fori_loop(0, s // CONV_ROWS, _row_loop(CONV_ROWS, step), (zero,) * (2 * kk + 2))
        pad = [jnp.zeros((8 - kk, tc), F32)]
        dwa_ref[...] = jnp.concatenate(list(sums[:kk]) + pad, axis=0)
        dba_ref[...] = sums[kk]
        dwv_ref[...] = jnp.concatenate(list(sums[kk + 1:2 * kk + 1]) + pad, axis=0)
        dbv_ref[...] = sums[2 * kk + 1]

    col_a = pl.BlockSpec((s, tc), lambda j: (0, j))
    col_v = pl.BlockSpec((s, tc), lambda j: (0, j + nb))
    w_a = pl.BlockSpec((8, tc), lambda j: (0, j))
    w_v = pl.BlockSpec((8, tc), lambda j: (0, j + nb))
    r_a = pl.BlockSpec((1, tc), lambda j: (0, j))
    r_v = pl.BlockSpec((1, tc), lambda j: (0, j + nb))
    outs = pl.pallas_call(
        body, name=name, grid=(nb,),
        in_specs=[col_a, col_v, col_a, col_a, w_a, w_v, col_a],
        out_specs=[col_a, col_a, w_a, w_a, r_a, r_a],
        out_shape=[jax.ShapeDtypeStruct((s, c), BF16), jax.ShapeDtypeStruct((s, c), BF16),
                   jax.ShapeDtypeStruct((8, c), F32), jax.ShapeDtypeStruct((8, c), F32),
                   jax.ShapeDtypeStruct((1, c), F32), jax.ShapeDtypeStruct((1, c), F32)],
        compiler_params=_params(("parallel",)),
    )(up_raw, up_raw, a_pre, v_pre, w, w, dact)
    return outs


def _tri_masks():
    row = lax.broadcasted_iota(jnp.int32, (CHUNK, CHUNK), 0)
    col = lax.broadcasted_iota(jnp.int32, (CHUNK, CHUNK), 1)
    return row >= col, row <= col


def _ssd_fwd(xbc, dt_raw, z, dt_bias, a_log, a_log_x, d_skip_x, norm_w, expand, *, name):
    s = xbc.shape[0]
    nc = s // CHUNK

    def body(xbc_ref, dtr_ref, z_ref, dtb_ref, alog_ref, alogx_ref, dskx_ref, nw_ref, e_ref,
             y_ref, ya_ref, st_ref, state):
        @pl.when(pl.program_id(0) == 0)
        def _():
            state[...] = jnp.zeros_like(state)

        st_ref[0] = state[...]
        lower, _ = _tri_masks()
        dt = _softplus(dtr_ref[...] + dtb_ref[...])
        adt = dt * (-jnp.exp(alog_ref[...]))
        acum = _dot_exact_lhs(lower.astype(BF16), _split3(adt))
        acum_t = acum.T
        dt_terms, acum_terms = _split3(dt), _split3(acum)
        for g in range(SSD_GROUPS):
            sl = slice(GROUP_COLS * g, GROUP_COLS * (g + 1))
            dt_x = _dot_terms(dt_terms, e_ref[:, sl])
            acum_x = _dot_terms(acum_terms, e_ref[:, sl])
            tot_x = jnp.sum(dt_x * (-jnp.exp(alogx_ref[:, sl])), axis=0, keepdims=True)
            xs = xbc_ref[:, sl]
            xdt = xs * dt_x
            xdt_b = xdt.astype(BF16)
            bg = xbc_ref[:, SSD_D_INNER + SSD_STATE * g:SSD_D_INNER + SSD_STATE * (g + 1)].astype(BF16)
            cg = xbc_ref[:, SSD_D_INNER + SSD_BC + SSD_STATE * g:SSD_D_INNER + SSD_BC + SSD_STATE * (g + 1)].astype(BF16)
            cb = _dot(cg, bg, NT)
            st_g = state[:, sl]
            y_off = _dot(cg, st_g.astype(BF16)) * jnp.exp(acum_x)
            parts = []
            for r in range(SSD_HEADS_PER_GROUP):
                h = SSD_HEADS_PER_GROUP * g + r
                dec = jnp.exp(jnp.where(lower, acum[:, h:h + 1] - acum_t[h:h + 1, :], -jnp.inf))
                parts.append(_dot((cb * dec).astype(BF16), xdt_b[:, SSD_HEAD_DIM * r:SSD_HEAD_DIM * (r + 1)]))
            y_ref[:, sl] = jnp.concatenate(parts, axis=1) + y_off + dskx_ref[:, sl] * xs
            wgt = (xdt * jnp.exp(tot_x - acum_x)).astype(BF16)
            state[:, sl] = st_g * jnp.exp(tot_x) + _dot(bg, wgt, TN)
        zv = z_ref[...].astype(F32)
        q = y_ref[...] * (zv * _sigmoid(zv))
        r = lax.rsqrt(jnp.mean(q * q, axis=-1, keepdims=True) + NORM_EPS)
        ya_ref[...] = (q * r * nw_ref[...]).astype(BF16)

    def chunk(w):
        return pl.BlockSpec((CHUNK, w), lambda c: (c, 0))

    def const(shape):
        return pl.BlockSpec(shape, lambda c: (0,) * len(shape))

    return pl.pallas_call(
        body, name=name, grid=(nc,),
        in_specs=[chunk(SSD_XBC), chunk(LANES), chunk(SSD_D_INNER), const((1, LANES)), const((1, LANES)),
                  const((1, SSD_D_INNER)), const((1, SSD_D_INNER)), const((1, SSD_D_INNER)),
                  const((LANES, SSD_D_INNER))],
        out_specs=[chunk(SSD_D_INNER), chunk(SSD_D_INNER),
                   pl.BlockSpec((1, SSD_STATE, SSD_D_INNER), lambda c: (c, 0, 0))],
        out_shape=[jax.ShapeDtypeStruct((s, SSD_D_INNER), F32), jax.ShapeDtypeStruct((s, SSD_D_INNER), BF16),
                   jax.ShapeDtypeStruct((nc, SSD_STATE, SSD_D_INNER), F32)],
        scratch_shapes=[pltpu.VMEM((SSD_STATE, SSD_D_INNER), F32)],
        compiler_params=_params(("arbitrary",)),
    )(xbc, dt_raw, z, dt_bias, a_log, a_log_x, d_skip_x, norm_w, expand)


def _ssd_bwd(dya, y, z, xbc, dt_raw, states, dt_bias, a_log, a_log_x, d_skip_x, norm_w, expand, expand_t, *, name):
    s = xbc.shape[0]
    nc = s // CHUNK

    def body(dya_ref, y_ref, z_ref, xbc_ref, dtr_ref, stp_ref, dtb_ref, alog_ref, alogx_ref, dskx_ref, nw_ref,
             e_ref, et_ref, dz_ref, dxbc_ref, ddt_ref, dnw_ref, ddsk_ref, dalog_ref, ddtb_ref,
             dstate, dy_sc, dskcol):
        i = pl.program_id(0)

        @pl.when(i == 0)
        def _():
            dstate[...] = jnp.zeros_like(dstate)
            dskcol[...] = jnp.zeros_like(dskcol)
            dnw_ref[...] = jnp.zeros_like(dnw_ref)
            dalog_ref[...] = jnp.zeros_like(dalog_ref)
            ddtb_ref[...] = jnp.zeros_like(ddtb_ref)
            ddsk_ref[...] = jnp.zeros_like(ddsk_ref)

        lower, upper = _tri_masks()
        rows = lax.broadcasted_iota(jnp.int32, (CHUNK, LANES), 0)
        pre = dtr_ref[...] + dtb_ref[...]
        dt = _softplus(pre)
        a = -jnp.exp(alog_ref[...])
        acum = _dot_exact_lhs(lower.astype(BF16), _split3(dt * a))
        acum_t = acum.T
        dt_terms, acum_terms = _split3(dt), _split3(acum)

        yv = y_ref[...]
        zv = z_ref[...].astype(F32)
        sz = _sigmoid(zv)
        silu_z = zv * sz
        q = yv * silu_z
        r = lax.rsqrt(jnp.mean(q * q, axis=-1, keepdims=True) + NORM_EPS)
        qhat = q * r
        dyav = dya_ref[...]
        dqhat = dyav * nw_ref[...]
        dnw_ref[...] += jnp.sum(dyav * qhat, axis=0, keepdims=True)
        dq = r * (dqhat - qhat * jnp.mean(dqhat * qhat, axis=-1, keepdims=True))
        dy_sc[...] = dq * silu_z
        dz_ref[...] = (dq * yv * (sz * (1.0 + zv * (1.0 - sz)))).astype(BF16)

        da_cum = jnp.zeros((CHUNK, LANES), F32)
        ddt = jnp.zeros((CHUNK, LANES), F32)
        for g in range(SSD_GROUPS):
            sl = slice(GROUP_COLS * g, GROUP_COLS * (g + 1))
            et_g = et_ref[sl, :]
            dt_x = _dot_terms(dt_terms, e_ref[:, sl])
            acum_x = _dot_terms(acum_terms, e_ref[:, sl])
            tot_x = jnp.sum(dt_x * (-jnp.exp(alogx_ref[:, sl])), axis=0, keepdims=True)
            e_tot = jnp.exp(tot_x)
            dec_s = jnp.exp(tot_x - acum_x)
            xs = xbc_ref[:, sl]
            xdt = xs * dt_x
            xdt_b = xdt.astype(BF16)
            dy = dy_sc[:, sl]
            dy_b = dy.astype(BF16)
            dskx = dskx_ref[:, sl]
            y_ssd = y_ref[:, sl] - dskx * xs
            dskcol[:, sl] += jnp.sum(dy * xs, axis=0, keepdims=True)
            bg = xbc_ref[:, SSD_D_INNER + SSD_STATE * g:SSD_D_INNER + SSD_STATE * (g + 1)].astype(BF16)
            cg = xbc_ref[:, SSD_D_INNER + SSD_BC + SSD_STATE * g:SSD_D_INNER + SSD_BC + SSD_STATE * (g + 1)].astype(BF16)
            cb_t = _dot(bg, cg, NT)
            sp = stp_ref[0, :, sl]
            ds_g = dstate[:, sl]
            ds_b = ds_g.astype(BF16)
            dye_b = (dy * jnp.exp(acum_x)).astype(BF16)
            dc = _dot(dye_b, sp.astype(BF16), NT)
            dxdt_state = dec_s * _dot(bg, ds_b)
            db = _dot((xdt * dec_s).astype(BF16), ds_b, NT)
            dcb_t = jnp.zeros((CHUNK, CHUNK), F32)
            parts = []
            for rr in range(SSD_HEADS_PER_GROUP):
                h = SSD_HEADS_PER_GROUP * g + rr
                hs = slice(SSD_HEAD_DIM * rr, SSD_HEAD_DIM * (rr + 1))
                dec_t = jnp.exp(jnp.where(upper, acum_t[h:h + 1, :] - acum[:, h:h + 1], -jnp.inf))
                parts.append(_dot((cb_t * dec_t).astype(BF16), dy_b[:, hs]))
                dcb_t = dcb_t + _dot(xdt_b[:, hs], dy_b[:, hs], NT) * dec_t
            dxdt = jnp.concatenate(parts, axis=1) + dxdt_state
            dcb_tb = dcb_t.astype(BF16)
            dc = dc + _dot(dcb_tb, bg, TN)
            db = db + _dot(dcb_tb, cg)
            tot_col = jnp.sum(ds_g * sp, axis=0, keepdims=True) * e_tot + jnp.sum(dxdt_state * xdt, axis=0, keepdims=True)
            d_tot = _dot_terms(_split3(jnp.broadcast_to(tot_col, (8, GROUP_COLS))), et_g)
            d_tot = jnp.max(d_tot, axis=0, keepdims=True)
            pair_sums = dy_b.astype(F32) * y_ssd - xdt_b.astype(F32) * dxdt
            da_cum = da_cum + _dot_terms(_split3(pair_sums), et_g) + jnp.where(rows == CHUNK - 1, d_tot, 0.0)
            ddt = ddt + _dot_terms(_split3(dxdt * xs), et_g)
            dxbc_ref[:, sl] = dy * dskx + dxdt * dt_x
            dxbc_ref[:, SSD_D_INNER + SSD_STATE * g:SSD_D_INNER + SSD_STATE * (g + 1)] = db
            dxbc_ref[:, SSD_D_INNER + SSD_BC + SSD_STATE * g:SSD_D_INNER + SSD_BC + SSD_STATE * (g + 1)] = dc
            dstate[:, sl] = e_tot * ds_g + _dot(cg, dye_b, TN)

        dadt = _dot_exact_lhs(upper.astype(BF16), _split3(da_cum))
        ddt = ddt + dadt * a
        dalog_ref[...] += jnp.sum(dadt * dt, axis=0, keepdims=True)
        dpre = ddt * _sigmoid(pre)
        ddtb_ref[...] += jnp.sum(dpre, axis=0, keepdims=True)
        ddt_ref[...] = dpre.astype(BF16)

        @pl.when(i == nc - 1)
        def _():
            dalog_ref[...] = dalog_ref[...] * a
            dsk = _dot_terms(_split3(jnp.broadcast_to(dskcol[...], (8, SSD_D_INNER))), et_ref[...])
            ddsk_ref[...] = jnp.max(dsk, axis=0, keepdims=True)

    def chunk(w):
        return pl.BlockSpec((CHUNK, w), lambda i: (nc - 1 - i, 0))

    def const(shape):
        return pl.BlockSpec(shape, lambda i: (0,) * len(shape))

    return pl.pallas_call(
        body, name=name, grid=(nc,),
        in_specs=[chunk(SSD_D_INNER), chunk(SSD_D_INNER), chunk(SSD_D_INNER), chunk(SSD_XBC), chunk(LANES),
                  pl.BlockSpec((1, SSD_STATE, SSD_D_INNER), lambda i: (nc - 1 - i, 0, 0)),
                  const((1, LANES)), const((1, LANES)), const((1, SSD_D_INNER)), const((1, SSD_D_INNER)),
                  const((1, SSD_D_INNER)), const((LANES, SSD_D_INNER)), const((SSD_D_INNER, LANES))],
        out_specs=[chunk(SSD_D_INNER), chunk(SSD_XBC), chunk(LANES), const((1, SSD_D_INNER)), const((1, LANES)),
                   const((1, LANES)), const((1, LANES))],
        out_shape=[jax.ShapeDtypeStruct((s, SSD_D_INNER), BF16), jax.ShapeDtypeStruct((s, SSD_XBC), F32),
                   jax.ShapeDtypeStruct((s, LANES), BF16), jax.ShapeDtypeStruct((1, SSD_D_INNER), F32),
                   jax.ShapeDtypeStruct((1, LANES), F32), jax.ShapeDtypeStruct((1, LANES), F32),
                   jax.ShapeDtypeStruct((1, LANES), F32)],
        scratch_shapes=[pltpu.VMEM((SSD_STATE, SSD_D_INNER), F32), pltpu.VMEM((CHUNK, SSD_D_INNER), F32),
                        pltpu.VMEM((1, SSD_D_INNER), F32)],
        compiler_params=_params(("arbitrary",)),
    )(dya, y, z, xbc, dt_raw, states, dt_bias, a_log, a_log_x, d_skip_x, norm_w, expand, expand_t)


GELU_K = math.sqrt(2.0 / math.pi)
GELU_C = 0.044715


def _gelu(x):
    return 0.5 * x * (1.0 + jnp.tanh(GELU_K * (x + GELU_C * x * x * x)))


def _gelu_grad(x):
    t = jnp.tanh(GELU_K * (x + GELU_C * x * x * x))
    return 0.5 * (1.0 + t) + 0.5 * x * (1.0 - t * t) * (GELU_K * (1.0 + 3.0 * GELU_C * x * x))


def _sgu_pre(uv_ref, uvb_ref, lnw_ref, lnb_ref):
    uv = uv_ref[...].astype(F32) + uvb_ref[...]
    guv = _gelu(uv)
    u = guv[:, :SGU_WIDTH]
    v = guv[:, SGU_WIDTH:]
    mu = jnp.mean(v, axis=-1, keepdims=True)
    vc = v - mu
    rstd = lax.rsqrt(jnp.mean(vc * vc, axis=-1, keepdims=True) + LN_EPS)
    vhat = vc * rstd
    vn = vhat * lnw_ref[...] + lnb_ref[...]
    return uv, u, vhat, rstd, vn


def _sgu_fwd(uv_raw, uv_b, ln_w, ln_b, w_sp, b_sp_t, *, name):
    s = uv_raw.shape[0]
    nc = s // CHUNK

    def body(uv_ref, uvb_ref, lnw_ref, lnb_ref, w_ref, bt_ref, o_ref):
        lower, _ = _tri_masks()
        _, u, _, _, vn = _sgu_pre(uv_ref, uvb_ref, lnw_ref, lnb_ref)
        vn_b = vn.astype(BF16)
        bt = bt_ref[...]
        for g in range(SGU_GROUPS):
            gs = slice(LANES * g, LANES * (g + 1))
            wc = jnp.where(lower, w_ref[g], 0.0).astype(BF16)
            mixed = _dot(wc, vn_b[:, gs]) + bt[:, g:g + 1]
            o_ref[:, gs] = (u[:, gs] * mixed).astype(BF16)

    def const(shape):
        return pl.BlockSpec(shape, lambda c: (0,) * len(shape))

    return pl.pallas_call(
        body, name=name, grid=(nc,),
        in_specs=[pl.BlockSpec((CHUNK, 2 * SGU_WIDTH), lambda c: (c, 0)), const((1, 2 * SGU_WIDTH)),
                  const((1, SGU_WIDTH)), const((1, SGU_WIDTH)), const((SGU_GROUPS, CHUNK, CHUNK)),
                  const((CHUNK, LANES))],
        out_specs=pl.BlockSpec((CHUNK, SGU_WIDTH), lambda c: (c, 0)),
        out_shape=jax.ShapeDtypeStruct((s, SGU_WIDTH), BF16),
        compiler_params=_params(("parallel",)),
    )(uv_raw, uv_b, ln_w, ln_b, w_sp, b_sp_t)


def _sgu_bwd(uv_raw, dyb, uv_b, ln_w, ln_b, w_sp, b_sp_t, group_sum, *, name):
    s = uv_raw.shape[0]
    nc = s // CHUNK

    def body(uv_ref, dy_ref, uvb_ref, lnw_ref, lnb_ref, w_ref, bt_ref, gsum_ref,
             duv_ref, dw_ref, dbt_ref, dlnw_ref, dlnb_ref, duvb_ref):
        @pl.when(pl.program_id(0) == 0)
        def _():
            dw_ref[...] = jnp.zeros_like(dw_ref)
            dbt_ref[...] = jnp.zeros_like(dbt_ref)
            dlnw_ref[...] = jnp.zeros_like(dlnw_ref)
            dlnb_ref[...] = jnp.zeros_like(dlnb_ref)
            duvb_ref[...] = jnp.zeros_like(duvb_ref)

        lower, _ = _tri_masks()
        uv, u, vhat, rstd, vn = _sgu_pre(uv_ref, uvb_ref, lnw_ref, lnb_ref)
        vn_b = vn.astype(BF16)
        bt = bt_ref[...]
        dy = dy_ref[...].astype(F32)
        du_parts, dvn_parts, dmix_parts = [], [], []
        for g in range(SGU_GROUPS):
            gs = slice(LANES * g, LANES * (g + 1))
            wc = jnp.where(lower, w_ref[g], 0.0).astype(BF16)
            mixed = _dot(wc, vn_b[:, gs]) + bt[:, g:g + 1]
            du_parts.append(dy[:, gs] * mixed)
            dmix = dy[:, gs] * u[:, gs]
            dmix_b = dmix.astype(BF16)
            dmix_parts.append(dmix)
            dw_ref[g] += jnp.where(lower, _dot(dmix_b, vn_b[:, gs], NT), 0.0)
            dvn_parts.append(_dot(wc, dmix_b, TN))
        dmixed = jnp.concatenate(dmix_parts, axis=1)
        dbt_ref[...] += _dot_terms(_split3(dmixed), gsum_ref[...])
        dvn = jnp.concatenate(dvn_parts, axis=1)
        dlnw_ref[...] += jnp.sum(dvn * vhat, axis=0, keepdims=True)
        dlnb_ref[...] += jnp.sum(dvn, axis=0, keepdims=True)
        dvhat = dvn * lnw_ref[...]
        dv = rstd * (dvhat - jnp.mean(dvhat, axis=-1, keepdims=True)
                     - vhat * jnp.mean(dvhat * vhat, axis=-1, keepdims=True))
        dguv = jnp.concatenate(du_parts + [dv], axis=1)
        duv = dguv * _gelu_grad(uv)
        duvb_ref[...] += jnp.sum(duv, axis=0, keepdims=True)
        duv_ref[...] = duv.astype(BF16)

    def const(shape):
        return pl.BlockSpec(shape, lambda c: (0,) * len(shape))

    return pl.pallas_call(
        body, name=name, grid=(nc,),
        in_specs=[pl.BlockSpec((CHUNK, 2 * SGU_WIDTH), lambda c: (c, 0)),
                  pl.BlockSpec((CHUNK, SGU_WIDTH), lambda c: (c, 0)), const((1, 2 * SGU_WIDTH)),
                  const((1, SGU_WIDTH)), const((1, SGU_WIDTH)), const((SGU_GROUPS, CHUNK, CHUNK)),
                  const((CHUNK, LANES)), const((SGU_WIDTH, LANES))],
        out_specs=[pl.BlockSpec((CHUNK, 2 * SGU_WIDTH), lambda c: (c, 0)), const((SGU_GROUPS, CHUNK, CHUNK)),
                   const((CHUNK, LANES)), const((1, SGU_WIDTH)), const((1, SGU_WIDTH)), const((1, 2 * SGU_WIDTH))],
        out_shape=[jax.ShapeDtypeStruct((s, 2 * SGU_WIDTH), BF16),
                   jax.ShapeDtypeStruct((SGU_GROUPS, CHUNK, CHUNK), F32), jax.ShapeDtypeStruct((CHUNK, LANES), F32),
                   jax.ShapeDtypeStruct((1, SGU_WIDTH), F32), jax.ShapeDtypeStruct((1, SGU_WIDTH), F32),
                   jax.ShapeDtypeStruct((1, 2 * SGU_WIDTH), F32)],
        compiler_params=_params(("arbitrary",)),
    )(uv_raw, dyb, uv_b, ln_w, ln_b, w_sp, b_sp_t, group_sum)


def _gate_fwd(gates_raw, b_gate, p_a, p_b, *, name, tm=512):
    s = p_a.shape[0]
    tm = min(tm, s)

    def body(ga_ref, gb_ref, ba_ref, bb_ref, pa_ref, pb_ref, o_ref):
        ga = _sigmoid(ga_ref[...].astype(F32) + ba_ref[...])
        gb = _sigmoid(gb_ref[...].astype(F32) + bb_ref[...])
        o_ref[...] = (ga * pa_ref[...].astype(F32) + gb * pb_ref[...].astype(F32)).astype(BF16)

    t_a = pl.BlockSpec((tm, D_MODEL), lambda i: (i, 0))
    t_b = pl.BlockSpec((tm, D_MODEL), lambda i: (i, 1))
    r_a = pl.BlockSpec((1, D_MODEL), lambda i: (0, 0))
    r_b = pl.BlockSpec((1, D_MODEL), lambda i: (0, 1))
    return pl.pallas_call(
        body, name=name, grid=(s // tm,),
        in_specs=[t_a, t_b, r_a, r_b, t_a, t_a], out_specs=t_a,
        out_shape=jax.ShapeDtypeStruct((s, D_MODEL), BF16),
        compiler_params=_params(("parallel",)),
    )(gates_raw, gates_raw, b_gate, b_gate, p_a, p_b)


def _gate_bwd(gates_raw, b_gate, p_a, p_b, dm, *, name, tm=512):
    s = p_a.shape[0]
    tm = min(tm, s)

    def body(ga_ref, gb_ref, ba_ref, bb_ref, pa_ref, pb_ref, dm_ref, dpa_ref, dpb_ref, dga_ref, dgb_ref,
             dba_ref, dbb_ref):
        @pl.when(pl.program_id(0) == 0)
        def _():
            dba_ref[...] = jnp.zeros_like(dba_ref)
            dbb_ref[...] = jnp.zeros_like(dbb_ref)

        d = dm_ref[...].astype(F32)
        for g_ref, b_ref, p_ref, dp_ref, dg_ref, db_ref in ((ga_ref, ba_ref, pa_ref, dpa_ref, dga_ref, dba_ref),
                                                            (gb_ref, bb_ref, pb_ref, dpb_ref, dgb_ref, dbb_ref)):
            sg = _sigmoid(g_ref[...].astype(F32) + b_ref[...])
            dp_ref[...] = (d * sg).astype(BF16)
            dg = d * p_ref[...].astype(F32) * (sg * (1.0 - sg))
            dg_ref[...] = dg.astype(BF16)
            db_ref[...] += jnp.sum(dg, axis=0, keepdims=True)

    t_a = pl.BlockSpec((tm, D_MODEL), lambda i: (i, 0))
    t_b = pl.BlockSpec((tm, D_MODEL), lambda i: (i, 1))
    r_a = pl.BlockSpec((1, D_MODEL), lambda i: (0, 0))
    r_b = pl.BlockSpec((1, D_MODEL), lambda i: (0, 1))
    big = jax.ShapeDtypeStruct((s, D_MODEL), BF16)
    row = jax.ShapeDtypeStruct((1, D_MODEL), F32)
    return pl.pallas_call(
        body, name=name, grid=(s // tm,),
        in_specs=[t_a, t_b, r_a, r_b, t_a, t_a, t_a], out_specs=[t_a, t_a, t_a, t_a, r_a, r_a],
        out_shape=[big, big, big, big, row, row],
        compiler_params=_params(("arbitrary",)),
    )(gates_raw, gates_raw, b_gate, b_gate, p_a, p_b, dm)


def _adamw_update(w_ref, g_ref, m_ref, v_ref, d_ref, mo_ref, vo_ref):
    gv = g_ref[...]
    mn = ADAM_B1 * m_ref[...] + (1.0 - ADAM_B1) * gv
    vn = ADAM_B2 * v_ref[...] + (1.0 - ADAM_B2) * (gv * gv)
    m_hat = mn / (1.0 - ADAM_B1 ** ADAM_STEP)
    v_hat = vn / (1.0 - ADAM_B2 ** ADAM_STEP)
    d_ref[...] = -ADAM_LR * (m_hat / (jnp.sqrt(v_hat) + ADAM_EPS) + ADAM_WD * w_ref[...])
    mo_ref[...] = mn
    vo_ref[...] = vn


def _adamw_many(ws, gs, ms, vs, *, name):
    n = len(ws)

    def body(*refs):
        for i in range(n):
            _adamw_update(*[refs[k * n + i] for k in range(7)])

    whole = pl.BlockSpec(memory_space=pltpu.VMEM)
    sds = [jax.ShapeDtypeStruct(w.shape, F32) for w in ws]
    outs = pl.pallas_call(
        body, name=name, in_specs=[whole] * (4 * n), out_specs=[whole] * (3 * n), out_shape=sds * 3,
        compiler_params=pltpu.CompilerParams(vmem_limit_bytes=VMEM_LIMIT),
    )(*ws, *gs, *ms, *vs)
    return outs[:n], outs[n:2 * n], outs[2 * n:]


def _adamw(w, g, m, v, *, name, tr=128):
    r, c = w.shape
    tr = min(tr, r)
    assert r % tr == 0, (name, r, tr)
    body = functools.partial(_adamw_update)

    blk = pl.BlockSpec((tr, c), lambda i: (i, 0))
    sds = jax.ShapeDtypeStruct((r, c), F32)
    return pl.pallas_call(
        body, name=name, grid=(r // tr,), in_specs=[blk] * 4, out_specs=[blk] * 3, out_shape=[sds] * 3,
        compiler_params=_params(("parallel",)),
    )(w, g, m, v)


def _adamw_two_sums(w, g_a, g_b, m, v, *, name, tr=128):
    r, c = w.shape
    tr = min(tr, r)
    assert r % tr == 0, (name, r, tr)

    def body(w_ref, ga_ref, gb_ref, m_ref, v_ref, g_ref, d_ref, mo_ref, vo_ref):
        g_ref[...] = ga_ref[...] + gb_ref[...]
        _adamw_update(w_ref, g_ref, m_ref, v_ref, d_ref, mo_ref, vo_ref)

    blk = pl.BlockSpec((tr, c), lambda i: (i, 0))
    sds = jax.ShapeDtypeStruct((r, c), F32)
    return pl.pallas_call(
        body, name=name, grid=(r // tr,), in_specs=[blk] * 5, out_specs=[blk] * 4, out_shape=[sds] * 4,
        compiler_params=_params(("parallel",)),
    )(w, g_a, g_b, m, v)


def _tile(n, pref):
    if n <= pref:
        return n
    best = LANES
    for t in range(LANES, pref + 1, LANES):
        if n % t == 0:
            best = t
    return best


MATMUL_BLOCK_BYTES = 20 * 1024 * 1024


def _mm(pairs, name, **kw):
    trans_b = kw.get("trans_b", False)
    m = (pairs[0][0][0] if isinstance(pairs[0][0], tuple) else pairs[0][0]).shape[0]
    ktot, n = 0, None
    for _, b in pairs:
        shape = b[0].shape[1:] if isinstance(b, tuple) else b.shape
        ktot += shape[1] if trans_b else shape[0]
        n = shape[0] if trans_b else shape[1]
    out_bytes = 4 * (2 if kw.get("add") is not None else 1)
    best = None
    widest = 512 if n % 512 == 0 else 1536
    for tm in (256, 512, 1024, 2048):
        for tn in range(LANES, min(n, widest) + 1, LANES):
            if m % min(tm, m) or n % tn:
                continue
            fits = 2 * ktot * (min(tm, m) + tn) + out_bytes * min(tm, m) * tn <= MATMUL_BLOCK_BYTES
            if fits and (best is None or min(tm, m) * tn >= best[0] * best[1]):
                best = (min(tm, m), tn)
    return _matmul(pairs, tm=best[0], tn=best[1], name=name, **kw)


def _wgrad(a, b, name, **kw):
    return _matmul_tn(a, b, tk=_tile(a.shape[1], 1408), tn=kw.pop("tn", _tile(b.shape[1], 1024)), tm=2048,
                      name=name, **kw)


def _local_step(x, target, get_weight, small, emit_grad):
    heads = jnp.arange(SSD_D_INNER) // SSD_HEAD_DIM
    expand = (jnp.arange(LANES)[:, None] == heads[None, :]).astype(BF16)
    expand_t = expand.T
    group_sum = (jnp.arange(SGU_WIDTH)[:, None] // LANES == jnp.arange(LANES)[None, :]).astype(BF16)
    pad_h = LANES - SSD_HEADS
    dt_bias = jnp.pad(small["dt_bias"], ((0, 0), (0, pad_h)))
    a_log = jnp.pad(small["a_log"], ((0, 0), (0, pad_h)))
    a_log_x = jnp.repeat(small["a_log"], SSD_HEAD_DIM, axis=1)
    d_skip_x = jnp.repeat(small["d_skip"], SSD_HEAD_DIM, axis=1)
    b_sp_t = jnp.pad(small["b_spatial"][0].T, ((0, 0), (0, LANES - SGU_GROUPS)))
    w_sp = small["w_spatial"][0]
    conv_a_w = jnp.pad(small["conv_a_w"], ((0, 4), (0, 0)))
    conv_f_w = jnp.pad(small["conv_f_w"], ((0, 5), (0, 0)))
    final_w = small["final_norm_w"].reshape(1, D_MODEL)

    n1 = _rms_fwd(x, small["norm1_w"], after=small.get("gathers_started"), name="rms1_fwd")
    wts = dict(get_weight("w_in", n1))
    z = _mm([(n1, wts["in_z"])], "in_z")
    xbc_raw = _mm([(n1, wts["in_xbc"])], "in_xbc")
    dt_raw = _mm([(n1, wts["in_dt"])], "in_dt")
    uv_raw = _mm([(n1, wts["in_uv"])], "in_uv", out_dtype=BF16)
    gates_raw = _mm([(n1, wts["in_gate"])], "in_gate", out_dtype=BF16)
    xbc, xbc_pre = _conv_a_fwd(xbc_raw, conv_a_w, small["conv_a_b"], name="conv_a_fwd")
    y, y_a, states = _ssd_fwd(xbc, dt_raw, z, dt_bias, a_log, a_log_x, d_skip_x, small["ssd_norm_w"], expand,
                              name="ssd_fwd")
    y_b = _sgu_fwd(uv_raw, small["uv_b"], small["v_ln_w"], small["v_ln_b"], w_sp, b_sp_t, name="sgu_fwd")
    wts.update(get_weight("w_branch", y_b))
    p_a = _mm([(y_a, wts["branch_a"])], "branch_a", out_dtype=BF16)
    p_b = _mm([(y_b, wts["branch_b"])], "branch_b", out_dtype=BF16)
    mix = _gate_fwd(gates_raw, small["b_gate"], p_a, p_b, name="gate_fwd")
    wts.update(get_weight("w_out", mix))
    h1 = _mm([(mix, wts["out"])], "out_proj", add=x)
    n2 = _rms_fwd(h1, small["norm2_w"], name="rms2_fwd")
    wts.update(get_weight("w_up", n2))
    up_w = wts["up"]
    up_cols = up_w.shape[2]
    up_raw = _matmul([(n2, (up_w, "cols"))], tm=2048, tn=up_cols, out_dtype=BF16, name="up_proj")
    act, up_a, up_v = _conv_f_fwd(up_raw, conv_f_w, small["conv_f_b"], name="conv_f_fwd")
    wts.update(get_weight("w_down", act))
    h2 = _mm([(act, wts["down"])], "down_proj", add=h1)
    loss, dh2, dh2_b, d_final = _final_fwd_bwd(h2, final_w, target, name="final_norm_loss")

    dact = _mm([(dh2_b, wts["down"])], "down_dgrad", trans_b=True)
    started = emit_grad("w_down", _wgrad(act, dh2_b, "down_wgrad"))
    dup_a, dup_v, dwf_a, dwf_v, dbf_a, dbf_v = _conv_f_bwd(up_raw, up_a, up_v, conv_f_w, dact, name="conv_f_bwd")
    dn2 = _mm([((dup_a, 0), (up_w, 0)), ((dup_a, 1), (up_w, 1)), ((dup_v, 0), (up_w, 2)), ((dup_v, 1), (up_w, 3))],
              "up_dgrad", trans_b=True, after=started, out_dtype=BF16)
    started = emit_grad("w_up", jnp.concatenate([_wgrad(n2, dup_a, "up_wgrad_a", tn=up_cols, stack_out=True),
                                                 _wgrad(n2, dup_v, "up_wgrad_v", tn=up_cols, stack_out=True)], axis=0))
    dh1, dh1_b, d_norm2 = _rms_bwd(h1, small["norm2_w"], dn2, dh2, name="rms2_bwd")
    dmix = _mm([(dh1_b, wts["out"])], "out_dgrad", trans_b=True, after=started, out_dtype=BF16)
    started = emit_grad("w_out", _wgrad(mix, dh1_b, "out_wgrad"))
    dp_a, dp_b, dg_a, dg_b, dbg_a, dbg_b = _gate_bwd(gates_raw, small["b_gate"], p_a, p_b, dmix, name="gate_bwd")
    dya = _mm([(dp_a, wts["branch_a"])], "branch_a_dgrad", trans_b=True, after=started)
    dyb = _mm([(dp_b, wts["branch_b"])], "branch_b_dgrad", trans_b=True, out_dtype=BF16)
    started_branch = emit_grad("w_branch", jnp.concatenate([_wgrad(y_a, dp_a, "branch_a_wgrad"),
                                                            _wgrad(y_b, dp_b, "branch_b_wgrad")], axis=0))
    duv, d_wsp, d_bsp_t, d_lnw, d_lnb, d_uvb = _sgu_bwd(uv_raw, dyb, small["uv_b"], small["v_ln_w"],
                                                        small["v_ln_b"], w_sp, b_sp_t, group_sum, name="sgu_bwd")
    dz, dxbc, ddt, d_ssd_nw, d_dskip, d_alog, d_dtb = _ssd_bwd(
        dya, y, z, xbc, dt_raw, states, dt_bias, a_log, a_log_x, d_skip_x, small["ssd_norm_w"], expand, expand_t,
        name="ssd_bwd")
    dxbc_raw, d_conv_a_w, d_conv_a_b = _conv_a_bwd(xbc_raw, xbc_pre, conv_a_w, dxbc, name="conv_a_bwd")
    started = emit_grad("w_in", {
        "in_z": _wgrad(n1, dz, "in_z_wgrad", after=started_branch), "in_xbc": _wgrad(n1, dxbc_raw, "in_xbc_wgrad"),
        "in_dt": _wgrad(n1, ddt, "in_dt_wgrad")[:, :SSD_HEADS], "in_uv": _wgrad(n1, duv, "in_uv_wgrad"),
        "in_gate_a": _wgrad(n1, dg_a, "in_gate_a_wgrad"), "in_gate_b": _wgrad(n1, dg_b, "in_gate_b_wgrad")})
    dn1 = _mm([(dz, wts["in_z"]), (dxbc_raw, wts["in_xbc"]), (ddt, wts["in_dt"]), (duv, wts["in_uv"]),
               (dg_a, wts["in_gate_a"]), (dg_b, wts["in_gate_b"])], "in_dgrad", trans_b=True, after=started,
              out_dtype=BF16)
    dx, _, d_norm1 = _rms_bwd(x, small["norm1_w"], dn1, dh1, name="rms1_bwd")

    grads_small = {
        "norm1_w": d_norm1, "b_gate": jnp.concatenate([dbg_a, dbg_b], axis=1),
        "conv_a_w": d_conv_a_w[:4], "conv_a_b": d_conv_a_b,
        "dt_bias": d_dtb[:, :SSD_HEADS], "a_log": d_alog[:, :SSD_HEADS], "d_skip": d_dskip[:, :SSD_HEADS],
        "ssd_norm_w": d_ssd_nw, "uv_b": d_uvb, "v_ln_w": d_lnw, "v_ln_b": d_lnb,
        "w_spatial": d_wsp[None], "b_spatial": d_bsp_t[:, :SGU_GROUPS].T[None],
        "norm2_w": d_norm2, "conv_f_w": jnp.concatenate([dwf_a[:3], dwf_v[:3]], axis=1),
        "conv_f_b": jnp.concatenate([dbf_a, dbf_v], axis=1), "final_norm_w": d_final.reshape(D_MODEL),
    }
    return loss, dx, grads_small


HBM = pl.BlockSpec(memory_space=pl.ANY)
MESH = pl.DeviceIdType.MESH


def _mesh_pos():
    return lax.axis_index("x"), lax.axis_index("y"), lax.axis_index("c")


def _other_chips(x, y):
    return [(1 - x, y), (x, 1 - y), (1 - x, 1 - y)]


def _remote(src, dst, send_sems, recv_sems, k, dev):
    return pltpu.make_async_remote_copy(src_ref=src, dst_ref=dst, send_sem=send_sems.at[k], recv_sem=recv_sems.at[k],
                                        device_id=dev, device_id_type=MESH)


def _dma_sems(n):
    return [pltpu.SemaphoreType.DMA((n,)), pltpu.SemaphoreType.DMA((n,))]


HBM_ONLY = pl.BlockSpec(memory_space=pltpu.HBM)
SEMAPHORES = pl.BlockSpec(memory_space=pltpu.SEMAPHORE)
DATAFLOW_EFFECT = pltpu.SideEffectType.DATAFLOW_SIDE_EFFECTING
N_PEER_CHIPS = N_CHIPS - 1


def _gather_sends(w_ref, land_ref, send_sems, recv_sems):
    x, y, c = _mesh_pos()
    return [_remote(w_ref.at[c], land_ref.at[2 * x + y, c], send_sems, recv_sems, k, (px, py, c))
            for k, (px, py) in enumerate(_other_chips(x, y))]


def _gather_arrivals(w_ref, land_ref, send_sems, recv_sems):
    x, y, c = _mesh_pos()
    return [_remote(w_ref.at[c], land_ref.at[2 * px + py, c], send_sems, recv_sems, k, (px, py, c))
            for k, (px, py) in enumerate(_other_chips(x, y))]


def _gather_whole_sends(w_ref, land_ref, send_sems, recv_sems):
    x, y, c = _mesh_pos()
    return [_remote(w_ref, land_ref.at[2 * x + y], send_sems, recv_sems, k, (px, py, c))
            for k, (px, py) in enumerate(_other_chips(x, y))]


def _gather_whole_arrivals(w_ref, land_ref, send_sems, recv_sems):
    x, y, c = _mesh_pos()
    return [_remote(w_ref, land_ref.at[2 * px + py], send_sems, recv_sems, k, (px, py, c))
            for k, (px, py) in enumerate(_other_chips(x, y))]


def _scatter_sends(h_ref, land_ref, send_sems, recv_sems):
    x, y, c = _mesh_pos()
    return [_remote(h_ref.at[2 * px + py], land_ref.at[2 * x + y], send_sems, recv_sems, k, (px, py, c))
            for k, (px, py) in enumerate(_other_chips(x, y))]


def _scatter_arrivals(h_ref, land_ref, send_sems, recv_sems):
    x, y, c = _mesh_pos()
    return [_remote(h_ref.at[2 * x + y], land_ref.at[2 * px + py], send_sems, recv_sems, k, (px, py, c))
            for k, (px, py) in enumerate(_other_chips(x, y))]


def _exchange_wait_many(pendings, after, sends, arrivals, *, name):
    n = len(pendings)

    def body(*refs):
        for i in range(n):
            src_ref, land_ref, send_ref, recv_ref = refs[i], refs[n + i], refs[2 * n + i], refs[3 * n + i]
            for cp in sends(src_ref, land_ref, send_ref, recv_ref):
                cp.wait_send()
            for cp in arrivals(src_ref, land_ref, send_ref, recv_ref):
                cp.wait_recv()

    sources = [p[2] for p in pendings]
    landings = [p[3] for p in pendings]
    outs = pl.pallas_call(
        body, name=name,
        out_shape=tuple(pltpu.HBM(a.shape, a.dtype) for a in sources + landings),
        in_specs=[HBM_ONLY] * (2 * n) + [SEMAPHORES] * (2 * n) + [pl.BlockSpec(memory_space=pl.ANY)],
        out_specs=tuple([HBM_ONLY] * (2 * n)), input_output_aliases={i: i for i in range(2 * n)},
        compiler_params=pltpu.CompilerParams(has_side_effects=DATAFLOW_EFFECT),
    )(*sources, *landings, *[p[0] for p in pendings], *[p[1] for p in pendings], after)
    return [(outs[i], outs[n + i]) for i in range(n)]


def _sibling_sends(src_ref, land_ref, send_sems, recv_sems):
    x, y, c = _mesh_pos()
    return [_remote(src_ref, land_ref, send_sems, recv_sems, 0, (x, y, 1 - c))]


def _exchange_start(sources, landing_shapes, sends, *, after=None, name):
    n = len(sources)
    extra = [] if after is None else [after]

    def body(*refs):
        sems = refs[2 * n + len(extra):4 * n + len(extra)]
        for i in range(n):
            send_i = sends[i] if isinstance(sends, (list, tuple)) else sends
            for cp in send_i(refs[i], refs[n + i], sems[2 * i], sems[2 * i + 1]):
                cp.start()
        refs[-1][...] = jnp.zeros_like(refs[-1])

    hbm = [pltpu.HBM(s.shape, s.dtype) for s in sources] + [pltpu.HBM(shp, s.dtype)
                                                             for shp, s in zip(landing_shapes, sources)]
    outs = pl.pallas_call(
        body, name=name,
        out_shape=tuple([pltpu.SemaphoreType.DMA((N_PEER_CHIPS,))] * (2 * n) + hbm
                        + [jax.ShapeDtypeStruct((8, LANES), F32)]),
        in_specs=[HBM_ONLY] * (2 * n) + [pl.BlockSpec(memory_space=pl.ANY)] * len(extra),
        out_specs=tuple([SEMAPHORES] * (2 * n) + [HBM_ONLY] * (2 * n) + [pl.BlockSpec(memory_space=pltpu.VMEM)]),
        input_output_aliases={i: 2 * n + i for i in range(2 * n)},
        compiler_params=pltpu.CompilerParams(has_side_effects=DATAFLOW_EFFECT),
    )(*[pltpu.with_memory_space_constraint(s, pltpu.HBM) for s in sources],
      *[pltpu.with_memory_space_constraint(lax.empty(shp, s.dtype), pltpu.HBM)
        for shp, s in zip(landing_shapes, sources)], *extra)
    pending = [(outs[2 * i], outs[2 * i + 1], outs[2 * n + i], outs[3 * n + i]) for i in range(n)]
    return pending, outs[-1]


def _exchange_wait(pending, after, sends, arrivals, *, name):
    send_sems, recv_sems, source, landing = pending

    def body(src_ref, land_ref, send_ref, recv_ref, after_ref, src_out, land_out):
        for cp in sends(src_ref, land_ref, send_ref, recv_ref):
            cp.wait_send()
        for cp in arrivals(src_ref, land_ref, send_ref, recv_ref):
            cp.wait_recv()

    return pl.pallas_call(
        body, name=name,
        out_shape=(pltpu.HBM(source.shape, source.dtype), pltpu.HBM(landing.shape, landing.dtype)),
        in_specs=[HBM_ONLY, HBM_ONLY, SEMAPHORES, SEMAPHORES, pl.BlockSpec(memory_space=pl.ANY)],
        out_specs=(HBM_ONLY, HBM_ONLY), input_output_aliases={0: 0, 1: 1},
        compiler_params=pltpu.CompilerParams(has_side_effects=DATAFLOW_EFFECT),
    )(source, landing, send_sems, recv_sems, after)


def _gather_ici(shard, *, name):
    _, rh, cols = shard.shape

    def body(w_ref, o_ref, send_sems, recv_sems):
        x, y, c = _mesh_pos()
        mine = 2 * x + y
        sends = []
        for k, (px, py) in enumerate(_other_chips(x, y)):
            cp = _remote(w_ref.at[c], o_ref.at[mine, c], send_sems, recv_sems, k, (px, py, c))
            cp.start()
            sends.append(cp)
        for k, (px, py) in enumerate(_other_chips(x, y)):
            _remote(w_ref.at[c], o_ref.at[2 * px + py, c], send_sems, recv_sems, k, (px, py, c)).wait_recv()
        for cp in sends:
            cp.wait_send()

    return pl.pallas_call(
        body, name=name, in_specs=[HBM], out_specs=HBM,
        out_shape=jax.ShapeDtypeStruct((N_CHIPS, 2, rh, cols), shard.dtype), scratch_shapes=_dma_sems(3),
    )(shard)


def _gather_d2d(parts, *, name):
    def body(a_ref, o_ref, send_sems, recv_sems):
        x, y, c = _mesh_pos()
        sibling = (x, y, 1 - c)
        sends = []
        for k, (px, py) in enumerate(_other_chips(x, y)):
            cp = _remote(a_ref.at[2 * px + py, c], o_ref.at[2 * px + py, c], send_sems, recv_sems, k, sibling)
            cp.start()
            sends.append(cp)
        for k, (px, py) in enumerate(_other_chips(x, y)):
            _remote(a_ref.at[2 * px + py, c], o_ref.at[2 * px + py, 1 - c], send_sems, recv_sems, k, sibling).wait_recv()
        for cp in sends:
            cp.wait_send()

    return pl.pallas_call(
        body, name=name, in_specs=[HBM], out_specs=HBM,
        out_shape=jax.ShapeDtypeStruct(parts.shape, parts.dtype),
        input_output_aliases={0: 0}, scratch_shapes=_dma_sems(3),
    )(parts)


def _all_gather_chips(shard_flat, name):
    rows, cols = shard_flat.shape
    parts = _gather_ici(shard_flat.reshape(2, rows // 2, cols), name=name + "_ici")
    others = _gather_d2d(parts, name=name + "_d2d").reshape(N_CHIPS, rows, cols)
    chip = 2 * lax.axis_index("x") + lax.axis_index("y")
    return lax.dynamic_update_slice(others, shard_flat[None], (chip, 0, 0))


def _row_tile(rows, mult, cap):
    best = mult
    for t in range(mult, min(rows, cap) + 1, mult):
        if rows % t == 0:
            best = t
    assert rows % best == 0, (rows, mult)
    return best


def _swap_halves_d2d(g, *, after=None, name):
    _, _, rh, cols = g.shape
    extra = [] if after is None else [after]

    def body(g_ref, *rest):
        o_ref, send_sems, recv_sems = rest[len(extra):]
        x, y, c = _mesh_pos()
        sibling = (x, y, 1 - c)
        sends = []
        for s in range(N_CHIPS):
            cp = _remote(g_ref.at[s, 1 - c], o_ref.at[s], send_sems, recv_sems, s, sibling)
            cp.start()
            sends.append(cp)
        for s in range(N_CHIPS):
            _remote(g_ref.at[s, c], o_ref.at[s], send_sems, recv_sems, s, sibling).wait_recv()
        for cp in sends:
            cp.wait_send()

    return pl.pallas_call(
        body, name=name, in_specs=[HBM] * (1 + len(extra)), out_specs=HBM,
        out_shape=jax.ShapeDtypeStruct((N_CHIPS, rh, cols), g.dtype), scratch_shapes=_dma_sems(N_CHIPS),
    )(g, *extra)


def _add_own_half(g, arrived, core, *, name):
    _, _, rh, cols = g.shape
    mult = 16 if g.dtype == BF16 else 8
    tr = _row_tile(rh, mult, max(mult, (512 * 1024) // cols))

    def body(core_ref, g_ref, a_ref, o_ref):
        o_ref[...] = (g_ref[0].astype(F32) + a_ref[...].astype(F32)).astype(o_ref.dtype)

    grid_spec = pltpu.PrefetchScalarGridSpec(
        num_scalar_prefetch=1, grid=(N_CHIPS, rh // tr),
        in_specs=[pl.BlockSpec((1, 1, tr, cols), lambda s, i, core_ref: (s, core_ref[0], i, 0)),
                  pl.BlockSpec((1, tr, cols), lambda s, i, core_ref: (s, i, 0))],
        out_specs=pl.BlockSpec((1, tr, cols), lambda s, i, core_ref: (s, i, 0)))
    return pl.pallas_call(
        body, name=name, grid_spec=grid_spec, out_shape=jax.ShapeDtypeStruct((N_CHIPS, rh, cols), g.dtype),
        compiler_params=_params(("parallel", "parallel")),
    )(core, g, arrived)


def _scatter_ici(h, *, name):
    def body(h_ref, o_ref, send_sems, recv_sems):
        x, y, c = _mesh_pos()
        mine = 2 * x + y
        sends = []
        for k, (px, py) in enumerate(_other_chips(x, y)):
            cp = _remote(h_ref.at[2 * px + py], o_ref.at[mine], send_sems, recv_sems, k, (px, py, c))
            cp.start()
            sends.append(cp)
        for k, (px, py) in enumerate(_other_chips(x, y)):
            _remote(h_ref.at[mine], o_ref.at[2 * px + py], send_sems, recv_sems, k, (px, py, c)).wait_recv()
        for cp in sends:
            cp.wait_send()

    others = pl.pallas_call(
        body, name=name, in_specs=[HBM], out_specs=HBM, out_shape=jax.ShapeDtypeStruct(h.shape, h.dtype),
        scratch_shapes=_dma_sems(3),
    )(h)
    chip = 2 * lax.axis_index("x") + lax.axis_index("y")
    own = lax.dynamic_slice_in_dim(h, chip, 1, axis=0)
    return lax.dynamic_update_slice(others, own, (chip, 0, 0))


def _sum_chips(parts, *, name):
    _, rh, cols = parts.shape
    mult = 16 if parts.dtype == BF16 else 8
    tr = _row_tile(rh, mult, max(mult, (512 * 1024) // cols))

    def body(p_ref, o_ref):
        acc = p_ref[0].astype(F32)
        for s in range(1, N_CHIPS):
            acc = acc + p_ref[s].astype(F32)
        o_ref[...] = acc

    return pl.pallas_call(
        body, name=name, grid=(rh // tr,),
        in_specs=[pl.BlockSpec((N_CHIPS, tr, cols), lambda i: (0, i, 0))],
        out_specs=pl.BlockSpec((tr, cols), lambda i: (i, 0)),
        out_shape=jax.ShapeDtypeStruct((rh, cols), F32), compiler_params=_params(("parallel",)),
    )(parts)


def _sum_chips_with_own(landed, sent, chip, *, name):
    _, rh, cols = landed.shape
    mult = 16 if landed.dtype == BF16 else 8
    tr = _row_tile(rh, mult, max(mult, (512 * 1024) // cols))

    def body(chip_ref, own_ref, px_ref, py_ref, pxy_ref, o_ref):
        acc = own_ref[0].astype(F32)
        for p_ref in (px_ref, py_ref, pxy_ref):
            acc = acc + p_ref[0].astype(F32)
        o_ref[...] = acc

    def block_of(flip):
        return pl.BlockSpec((1, tr, cols), lambda i, chip_ref: (chip_ref[0] ^ flip, i, 0))

    grid_spec = pltpu.PrefetchScalarGridSpec(
        num_scalar_prefetch=1, grid=(rh // tr,),
        in_specs=[block_of(0), block_of(2), block_of(1), block_of(3)],
        out_specs=pl.BlockSpec((tr, cols), lambda i, chip_ref: (i, 0)))
    return pl.pallas_call(
        body, name=name, grid_spec=grid_spec, out_shape=jax.ShapeDtypeStruct((rh, cols), F32),
        compiler_params=_params(("parallel",)),
    )(chip, sent, landed, landed, landed)


def _share_d2d(f, *, name):
    fs = f if isinstance(f, (list, tuple)) else [f]
    others = _swap_with_sibling(fs, name=name)
    first = lax.axis_index("c") == 0
    both = [jnp.stack([jnp.where(first, a, b), jnp.where(first, b, a)]) for a, b in zip(fs, others)]
    return both if isinstance(f, (list, tuple)) else both[0]


def _swap_with_sibling(fs, *, name):
    n = len(fs)

    def body(*refs):
        x, y, c = _mesh_pos()
        sibling = (x, y, 1 - c)
        send_sems, recv_sems = refs[2 * n:]
        copies = [_remote(refs[i], refs[n + i], send_sems, recv_sems, i, sibling) for i in range(n)]
        for cp in copies:
            cp.start()
        for cp in copies:
            cp.wait()

    return pl.pallas_call(
        body, name=name, in_specs=[HBM] * n, out_specs=[HBM] * n,
        out_shape=[jax.ShapeDtypeStruct(a.shape, a.dtype) for a in fs], scratch_shapes=_dma_sems(n),
    )(*fs)


def _reduce_scatter_chips(g, core, name, after=None):
    _, rows, cols = g.shape
    g = g.reshape(N_CHIPS, 2, rows // 2, cols)
    arrived = _swap_halves_d2d(g, after=after, name=name + "_swap")
    chip_sum = _add_own_half(g, arrived, core, name=name + "_add2")
    parts = _scatter_ici(chip_sum, name=name + "_ici")
    total = _sum_chips(parts, name=name + "_sum4")
    return _share_d2d(total, name=name + "_share").reshape(rows, cols)


BIG = ("w_in", "w_branch", "w_out", "w_up", "w_down")
BIG_COLUMN_SHARDED = ("w_in", "w_up")
CONV = ("conv_a_w", "conv_f_w")
REPLICATED = ("norm1_w", "b_gate", "conv_a_b", "dt_bias", "a_log", "d_skip", "ssd_norm_w", "uv_b", "v_ln_w",
              "v_ln_b", "w_spatial", "b_spatial", "norm2_w", "conv_f_b", "final_norm_w")
WEIGHT_ORDER = ("norm1_w", "w_in", "b_gate", "conv_a_w", "conv_a_b", "dt_bias", "a_log", "d_skip", "ssd_norm_w",
                "uv_b", "v_ln_w", "v_ln_b", "w_spatial", "b_spatial", "w_branch", "w_out", "norm2_w", "w_up",
                "conv_f_w", "conv_f_b", "w_down", "final_norm_w")
SMALL_EXCHANGE_ROWS = 64


_GATE0 = SSD_IN + 2 * SGU_WIDTH
IN_SEGMENTS = {
    "in_z": (0, SSD_D_INNER), "in_xbc": (SSD_D_INNER, SSD_D_INNER + SSD_XBC), "in_dt": (SSD_D_INNER + SSD_XBC, SSD_IN),
    "in_uv": (SSD_IN, _GATE0), "in_gate": (_GATE0, IN_COLS), "in_gate_a": (_GATE0, _GATE0 + D_MODEL),
    "in_gate_b": (_GATE0 + D_MODEL, IN_COLS),
}
IN_GRAD_SEGMENTS = ("in_z", "in_xbc", "in_dt", "in_uv", "in_gate_a", "in_gate_b")


def _take_columns(parts, start, stop):
    out = []
    for a, first in parts:
        lo, hi = max(start, first), min(stop, first + a.shape[1])
        if lo < hi:
            out.append(a[:, lo - first:hi - first])
    return out[0] if len(out) == 1 else jnp.concatenate(out, axis=1)


def _flat_rows(arrays, row_multiple):
    flat = jnp.concatenate([a.reshape(-1) for a in arrays])
    rows = -(-flat.shape[0] // (LANES * row_multiple)) * row_multiple
    return jnp.pad(flat, (0, rows * LANES - flat.shape[0])).reshape(rows, LANES)


def _unflatten(flat, shapes):
    flat = flat.reshape(-1)
    out, off = [], 0
    for shp in shapes:
        n = math.prod(shp)
        out.append(flat[off:off + n].reshape(shp))
        off += n
    return out


def _from_chip_blocks(blocks, name):
    if name in BIG_COLUMN_SHARDED or name in CONV:
        k = blocks.shape[1]
        return jnp.transpose(blocks, (1, 0, 2)).reshape(k, -1)
    return blocks.reshape(-1, blocks.shape[-1])


def _to_chip_blocks(whole, name):
    if name in BIG_COLUMN_SHARDED or name in CONV:
        k, n = whole.shape
        return jnp.transpose(whole.reshape(k, N_CHIPS, n // N_CHIPS), (1, 0, 2))
    return whole.reshape(N_CHIPS, whole.shape[0] // N_CHIPS, whole.shape[1])


def kernel(x, norm1_w, w_in, b_gate, conv_a_w, conv_a_b, dt_bias, a_log, d_skip, ssd_norm_w, uv_b, v_ln_w, v_ln_b, w_spatial, b_spatial, w_branch, w_out, norm2_w, w_up, conv_f_w, conv_f_b, w_down, final_norm_w, loss_target, m_norm1_w, m_w_in, m_b_gate, m_conv_a_w, m_conv_a_b, m_dt_bias, m_a_log, m_d_skip, m_ssd_norm_w, m_uv_b, m_v_ln_w, m_v_ln_b, m_w_spatial, m_b_spatial, m_w_branch, m_w_out, m_norm2_w, m_w_up, m_conv_f_w, m_conv_f_b, m_w_down, m_final_norm_w, v_norm1_w, v_w_in, v_b_gate, v_conv_a_w, v_conv_a_b, v_dt_bias, v_a_log, v_d_skip, v_ssd_norm_w, v_uv_b, v_v_ln_w, v_v_ln_b, v_w_spatial, v_b_spatial, v_w_branch, v_w_out, v_norm2_w, v_w_up, v_conv_f_w, v_conv_f_b, v_w_down, v_final_norm_w):
    weights = dict(norm1_w=norm1_w, w_in=w_in, b_gate=b_gate, conv_a_w=conv_a_w, conv_a_b=conv_a_b, dt_bias=dt_bias,
                   a_log=a_log, d_skip=d_skip, ssd_norm_w=ssd_norm_w, uv_b=uv_b, v_ln_w=v_ln_w, v_ln_b=v_ln_b,
                   w_spatial=w_spatial, b_spatial=b_spatial, w_branch=w_branch, w_out=w_out, norm2_w=norm2_w,
                   w_up=w_up, conv_f_w=conv_f_w, conv_f_b=conv_f_b, w_down=w_down, final_norm_w=final_norm_w)
    mom1 = dict(norm1_w=m_norm1_w, w_in=m_w_in, b_gate=m_b_gate, conv_a_w=m_conv_a_w, conv_a_b=m_conv_a_b,
                dt_bias=m_dt_bias, a_log=m_a_log, d_skip=m_d_skip, ssd_norm_w=m_ssd_norm_w, uv_b=m_uv_b,
                v_ln_w=m_v_ln_w, v_ln_b=m_v_ln_b, w_spatial=m_w_spatial, b_spatial=m_b_spatial, w_branch=m_w_branch,
                w_out=m_w_out, norm2_w=m_norm2_w, w_up=m_w_up, conv_f_w=m_conv_f_w, conv_f_b=m_conv_f_b,
                w_down=m_w_down, final_norm_w=m_final_norm_w)
    mom2 = dict(norm1_w=v_norm1_w, w_in=v_w_in, b_gate=v_b_gate, conv_a_w=v_conv_a_w, conv_a_b=v_conv_a_b,
                dt_bias=v_dt_bias, a_log=v_a_log, d_skip=v_d_skip, ssd_norm_w=v_ssd_norm_w, uv_b=v_uv_b,
                v_ln_w=v_v_ln_w, v_ln_b=v_v_ln_b, w_spatial=v_w_spatial, b_spatial=v_b_spatial, w_branch=v_w_branch,
                w_out=v_w_out, norm2_w=v_norm2_w, w_up=v_w_up, conv_f_w=v_conv_f_w, conv_f_b=v_conv_f_b,
                w_down=v_w_down, final_norm_w=v_final_norm_w)
    chip = 2 * lax.axis_index("x") + lax.axis_index("y")
    core = lax.axis_index("c").astype(jnp.int32).reshape(1)

    whole = {}
    conv_shapes = [weights[n].shape[1:] for n in CONV]
    conv_gathered = _all_gather_chips(_flat_rows([weights[n] for n in CONV], 16), "gather_conv").reshape(N_CHIPS, -1)
    off = 0
    for n, shp in zip(CONV, conv_shapes):
        size = math.prod(shp)
        whole[n] = _from_chip_blocks(conv_gathered[:, off:off + size].reshape((N_CHIPS,) + shp), n)
        off += size
    shard_shapes = {n: weights[n].shape[1:] for n in BIG}
    halves = [weights[n][0].astype(BF16).reshape(2, shard_shapes[n][0] // 2, shard_shapes[n][1]) for n in BIG]
    sends = [_gather_sends if n == "w_in" else _gather_whole_sends for n in BIG]
    gathers, gathers_started = _exchange_start(halves, [(N_CHIPS,) + h.shape for h in halves], sends,
                                               after=conv_gathered, name="gather_start")
    gathers = dict(zip(BIG, gathers))

    def get_weight(name, after):
        rows, cols = shard_shapes[name]
        if name == "w_in":
            own, landed = _exchange_wait(gathers[name], after, _gather_sends, _gather_arrivals,
                                         name="gather_" + name + "_wait")
            landed = _gather_d2d(landed, name="gather_" + name + "_d2d")
        else:
            own, landed = _exchange_wait(gathers[name], after, _gather_whole_sends, _gather_whole_arrivals,
                                         name="gather_" + name + "_wait")
        blocks = lax.dynamic_update_slice(landed.reshape(N_CHIPS, rows, cols), own.reshape(1, rows, cols),
                                          (chip, 0, 0))
        if name == "w_up":
            return {"up": blocks}
        if name == "w_in":
            parts = [(blocks[k], cols * k) for k in range(N_CHIPS)]
            segs = {n: _take_columns(parts, a, b) for n, (a, b) in IN_SEGMENTS.items()}
            segs["in_dt"] = jnp.pad(segs["in_dt"], ((0, 0), (0, LANES - SSD_HEADS)))
            return segs
        full = _from_chip_blocks(blocks, name)
        if name == "w_branch":
            return {"branch_a": full[:SSD_D_INNER], "branch_b": full[SSD_D_INNER:]}
        return {name[2:]: full}

    small = {n: weights[n] for n in REPLICATED}
    small["conv_a_w"] = whole["conv_a_w"]
    small["conv_f_w"] = whole["conv_f_w"]
    small["gathers_started"] = gathers_started

    reductions = {}

    def emit_grad(name, g):
        if name == "w_in":
            parts = [(g[n], IN_SEGMENTS[n][0]) for n in IN_GRAD_SEGMENTS]
            cols = shard_shapes[name][1]
            g_blocks = jnp.stack([_take_columns(parts, cols * k, cols * (k + 1)) for k in range(N_CHIPS)])
        else:
            g_blocks = g if name == "w_up" else _to_chip_blocks(g, name)
        if name == "w_in":
            _, rows, cols = g_blocks.shape
            g_halves = g_blocks.reshape(N_CHIPS, 2, rows // 2, cols)
            arrived = _swap_halves_d2d(g_halves, name="reduce_" + name + "_swap")
            g_blocks = _add_own_half(g_halves, arrived, core, name="reduce_" + name + "_add2")
        (pending,), started = _exchange_start([g_blocks], [g_blocks.shape], _scatter_sends,
                                              name="reduce_" + name + "_start")
        reductions[name] = pending
        return started

    loss, dx, grads_small = _local_step(x[0], loss_target[0], get_weight, small, emit_grad)

    order = ("w_down", "w_up", "w_out", "w_branch", "w_in")
    core_sums = []
    chip_index = chip.astype(jnp.int32).reshape(1)
    for n in order:
        sent, landed = _exchange_wait(reductions[n], dx, _scatter_sends, _scatter_arrivals,
                                      name="reduce_" + n + "_wait")
        core_sums.append(_sum_chips_with_own(landed, sent, chip_index, name="reduce_" + n + "_sum4"))
    swaps, swaps_started = _exchange_start(core_sums, [a.shape for a in core_sums], _sibling_sends, name="reduce_swap_start")
    grads = {}

    small_names = REPLICATED + CONV + ("loss",)
    grads_small = dict(grads_small, loss=loss)
    small_shapes = [grads_small[n].shape for n in small_names]
    g_small = _flat_rows([grads_small[n] for n in small_names], N_CHIPS * 2 * SMALL_EXCHANGE_ROWS)
    red_small = _reduce_scatter_chips(g_small.reshape(N_CHIPS, -1, LANES), core, "reduce_small", after=swaps_started)
    all_small = _all_gather_chips(red_small, "gather_small")
    swapped = _exchange_wait_many(swaps, all_small, _sibling_sends, _sibling_sends, name="reduce_swap_wait")
    core_sums = {n: own for n, (own, _) in zip(order, swapped)}
    sibling_sums = {n: other for n, (_, other) in zip(order, swapped)}
    first = lax.axis_index("c") == 0
    w_in_halves = (core_sums["w_in"], sibling_sums["w_in"])
    w_in_grad = jnp.concatenate([jnp.where(first, w_in_halves[0], w_in_halves[1]),
                                 jnp.where(first, w_in_halves[1], w_in_halves[0])], axis=0)
    for n, g in zip(small_names, _unflatten(all_small, small_shapes)):
        if n == "loss":
            total_loss = g[0, 0]
            continue
        if n in CONV:
            width = g.shape[1] // N_CHIPS
            g = lax.dynamic_slice_in_dim(g, chip * width, width, axis=1)
        grads[n] = g.reshape(weights[n].shape[1:]) if n != "final_norm_w" else g

    delta, new_m, new_v = {}, {}, {}
    for n in BIG:
        shp = weights[n].shape
        if n == "w_in":
            g_t = w_in_grad.T
            results = [g_t] + list(_adamw(weights[n][0].T, g_t, mom1[n][0].T, mom2[n][0].T, name="adamw_" + n,
                                          tr=_row_tile(g_t.shape[0], 8, 136)))
            results = [a.T for a in results]
        else:
            results = _adamw_two_sums(weights[n][0], core_sums[n], sibling_sums[n], mom1[n][0], mom2[n][0],
                                      name="adamw_" + n, tr=_row_tile(shp[1], 8, 352))
        grads[n], delta[n], new_m[n], new_v[n] = [a.reshape(shp) for a in results]
    small_all = [n for n in WEIGHT_ORDER if n not in BIG]

    def as_2d(a):
        return a.reshape(-1, a.shape[-1])

    results = _adamw_many(*[[as_2d(src[n]) for n in small_all] for src in (weights, grads, mom1, mom2)],
                          name="adamw_small")
    for n, dv, mv, vv in zip(small_all, *results):
        shp = weights[n].shape
        delta[n], new_m[n], new_v[n] = dv.reshape(shp), mv.reshape(shp), vv.reshape(shp)

    grad_out = [grads[n].reshape(weights[n].shape) for n in WEIGHT_ORDER]
    return (total_loss, dx[None], *grad_out, *[delta[n] for n in WEIGHT_ORDER], *[new_m[n] for n in WEIGHT_ORDER],
            *[new_v[n] for n in WEIGHT_ORDER])
```

```python
import functools
import math

import jax
import jax.numpy as jnp
from jax import lax
from jax.experimental import pallas as pl
from jax.experimental.pallas import tpu as pltpu

F32 = jnp.float32
BF16 = jnp.bfloat16

D_MODEL = 1024
SSD_D_INNER = 2048
SSD_HEADS = 32
SSD_HEAD_DIM = 64
SSD_GROUPS = 4
SSD_HEADS_PER_GROUP = 8
SSD_STATE = 128
SSD_BC = 512
SSD_XBC = 3072
SSD_IN = 5152
SGU_WIDTH = 1024
SGU_GROUPS = 8
CHUNK = 128
IN_COLS = 9248
D_FF = 2816
NORM_EPS = 1e-6
LN_EPS = 1e-5
GROUP_COLS = SSD_HEADS_PER_GROUP * SSD_HEAD_DIM
LANES = 128

ADAM_LR = 0.001
ADAM_B1 = 0.9
ADAM_B2 = 0.999
ADAM_EPS = 1e-08
ADAM_WD = 0.01
ADAM_STEP = 10

N_CHIPS = 4
VMEM_LIMIT = 56 * 1024 * 1024

NT = (((1,), (1,)), ((), ()))
TN = (((0,), (0,)), ((), ()))
NN = (((1,), (0,)), ((), ()))


def _params(dims):
    return pltpu.CompilerParams(dimension_semantics=dims, vmem_limit_bytes=VMEM_LIMIT)


def _dot(a, b, dn=NN, precision=None):
    return lax.dot_general(a, b, dn, precision=precision, preferred_element_type=F32)


def _split3(x):
    hi = x.astype(BF16)
    rest = x - hi.astype(F32)
    mid = rest.astype(BF16)
    return hi, mid, (rest - mid.astype(F32)).astype(BF16)


def _dot_terms(terms, exact, dn=NN):
    out = None
    for t in terms:
        p = _dot(t, exact, dn)
        out = p if out is None else out + p
    return out


def _dot_exact_lhs(exact, terms):
    out = None
    for t in terms:
        p = _dot(exact, t)
        out = p if out is None else out + p
    return out


def _sigmoid(x):
    return 1.0 / (1.0 + jnp.exp(-x))


def _softplus(x):
    return jnp.maximum(x, 0.0) + jnp.log(1.0 + jnp.exp(-jnp.abs(x)))


def _matmul(pairs, *, trans_b=False, add=None, after=None, out_dtype=F32, tm=512, tn=512, name):
    def mat_shape(b):
        if isinstance(b, tuple) and b[1] == "cols":
            return (b[0].shape[1], b[0].shape[0] * b[0].shape[2])
        return b[0].shape[1:] if isinstance(b, tuple) else b.shape

    if isinstance(pairs[0][1], tuple) and pairs[0][1][1] == "cols":
        assert not trans_b and tn % LANES == 0 and pairs[0][1][0].shape[2] % tn == 0, name

    m = (pairs[0][0][0] if isinstance(pairs[0][0], tuple) else pairs[0][0]).shape[0]
    n = mat_shape(pairs[0][1])[0] if trans_b else mat_shape(pairs[0][1])[1]
    tm, tn = min(tm, m), min(tn, n)
    assert m % tm == 0 and n % tn == 0, (name, m, n, tm, tn)
    npairs = len(pairs)
    dn = NT if trans_b else NN

    def body(*refs):
        o_ref = refs[-1]
        acc = None
        for i in range(npairs):
            p = _dot(refs[2 * i][...].astype(BF16), refs[2 * i + 1][...].astype(BF16), dn)
            acc = p if acc is None else acc + p
        if add is not None:
            acc = acc + refs[2 * npairs][...]
        o_ref[...] = acc.astype(out_dtype)

    in_specs, args = [], []
    for a, b in pairs:
        bshape = mat_shape(b)
        k = bshape[1] if trans_b else bshape[0]
        assert bshape == ((n, k) if trans_b else (k, n)), (name, bshape)
        a, qa = a if isinstance(a, tuple) else (a, 0)
        assert a.shape[0] == m and a.shape[1] % k == 0, (name, a.shape, k)
        in_specs.append(pl.BlockSpec((tm, k), lambda i, j, qa=qa: (i, qa)))
        if isinstance(b, tuple) and b[1] == "cols":
            b = b[0]
            per = b.shape[2] // tn
            in_specs.append(pl.BlockSpec((None, k, tn), lambda i, j, per=per: (j // per, 0, j % per)))
        elif isinstance(b, tuple):
            b, qb = b
            if trans_b:
                in_specs.append(pl.BlockSpec((None, tn, k), lambda i, j, qb=qb: (qb, j, 0)))
            else:
                in_specs.append(pl.BlockSpec((None, k, tn), lambda i, j, qb=qb: (qb, 0, j)))
        elif trans_b:
            in_specs.append(pl.BlockSpec((tn, k), lambda i, j: (j, 0)))
        else:
            in_specs.append(pl.BlockSpec((k, tn), lambda i, j: (0, j)))
        args += [a, b]
    if add is not None:
        in_specs.append(pl.BlockSpec((tm, tn), lambda i, j: (i, j)))
        args.append(add)
    if after is not None:
        in_specs.append(pl.BlockSpec(memory_space=pl.ANY))
        args.append(after)
    return pl.pallas_call(
        body, name=name, grid=(m // tm, n // tn), in_specs=in_specs,
        out_specs=pl.BlockSpec((tm, tn), lambda i, j: (i, j)),
        out_shape=jax.ShapeDtypeStruct((m, n), out_dtype),
        compiler_params=_params(("parallel", "parallel")),
    )(*args)


def _matmul_tn(a, b, *, tk, tn, tm=1024, out_dtype=BF16, stack_out=False, after=None, name):
    m, k = a.shape
    n = b.shape[1]
    tm, tk, tn = min(tm, m), min(tk, k), min(tn, n)
    assert m % tm == 0 and k % tk == 0 and n % tn == 0, (name, m, k, n)
    nm = m // tm
    if stack_out:
        out_spec = pl.BlockSpec((None, tk, tn), lambda i, j, l: (j, i, 0))
        out_shape = jax.ShapeDtypeStruct((n // tn, k, tn), out_dtype)
    else:
        out_spec = pl.BlockSpec((tk, tn), lambda i, j, l: (i, j))
        out_shape = jax.ShapeDtypeStruct((k, n), out_dtype)

    def body(a_ref, b_ref, *rest):
        o_ref, acc = rest[-2:]
        mi = pl.program_id(2)

        @pl.when(mi == 0)
        def _():
            acc[...] = jnp.zeros_like(acc)

        acc[...] += _dot(a_ref[...].astype(BF16), b_ref[...].astype(BF16), TN)

        @pl.when(mi == nm - 1)
        def _():
            o_ref[...] = acc[...].astype(out_dtype)

    in_specs = [pl.BlockSpec((tm, tk), lambda i, j, l: (l, i)), pl.BlockSpec((tm, tn), lambda i, j, l: (l, j))]
    args = [a, b]
    if after is not None:
        in_specs.append(pl.BlockSpec(memory_space=pl.ANY))
        args.append(after)
    return pl.pallas_call(
        body, name=name, grid=(k // tk, n // tn, nm), in_specs=in_specs,
        out_specs=out_spec, out_shape=out_shape,
        scratch_shapes=[pltpu.VMEM((tk, tn), F32)],
        compiler_params=_params(("parallel", "parallel", "arbitrary")),
    )(*args)


def _rms_fwd(x, w, *, after=None, name, tm=512):
    s, d = x.shape
    tm = min(tm, s)
    extra = [] if after is None else [after]

    def body(x_ref, w_ref, *rest):
        o_ref = rest[-1]
        xv = x_ref[...]
        r = lax.rsqrt(jnp.mean(xv * xv, axis=-1, keepdims=True) + NORM_EPS)
        o_ref[...] = (xv * r * w_ref[...]).astype(BF16)

    return pl.pallas_call(
        body, name=name, grid=(s // tm,),
        in_specs=[pl.BlockSpec((tm, d), lambda i: (i, 0)), pl.BlockSpec((1, d), lambda i: (0, 0))]
        + [pl.BlockSpec(memory_space=pl.ANY)] * len(extra),
        out_specs=pl.BlockSpec((tm, d), lambda i: (i, 0)),
        out_shape=jax.ShapeDtypeStruct((s, d), BF16),
        compiler_params=_params(("parallel",)),
    )(x, w, *extra)


def _rms_bwd(x, w, dn, dres, *, name, tm=512):
    s, d = x.shape
    tm = min(tm, s)

    def body(x_ref, w_ref, dn_ref, dres_ref, dx_ref, dxb_ref, dw_ref):
        @pl.when(pl.program_id(0) == 0)
        def _():
            dw_ref[...] = jnp.zeros_like(dw_ref)

        xv = x_ref[...]
        r = lax.rsqrt(jnp.mean(xv * xv, axis=-1, keepdims=True) + NORM_EPS)
        xhat = xv * r
        dnv = dn_ref[...].astype(F32)
        dxhat = dnv * w_ref[...]
        dx = dres_ref[...] + r * (dxhat - xhat * jnp.mean(dxhat * xhat, axis=-1, keepdims=True))
        dx_ref[...] = dx
        dxb_ref[...] = dx.astype(BF16)
        dw_ref[...] += jnp.sum(dnv * xhat, axis=0, keepdims=True)

    tile = pl.BlockSpec((tm, d), lambda i: (i, 0))
    row = pl.BlockSpec((1, d), lambda i: (0, 0))
    return pl.pallas_call(
        body, name=name, grid=(s // tm,),
        in_specs=[tile, row, tile, tile], out_specs=[tile, tile, row],
        out_shape=[jax.ShapeDtypeStruct((s, d), F32), jax.ShapeDtypeStruct((s, d), BF16),
                   jax.ShapeDtypeStruct((1, d), F32)],
        compiler_params=_params(("arbitrary",)),
    )(x, w, dn, dres)


def _final_fwd_bwd(h2, wf, target, *, name, tm=512):
    s, d = h2.shape
    tm = min(tm, s)

    def body(h_ref, w_ref, t_ref, loss_ref, dh_ref, dhb_ref, dw_ref):
        @pl.when(pl.program_id(0) == 0)
        def _():
            dw_ref[...] = jnp.zeros_like(dw_ref)
            loss_ref[...] = jnp.zeros_like(loss_ref)

        hv = h_ref[...]
        r = lax.rsqrt(jnp.mean(hv * hv, axis=-1, keepdims=True) + NORM_EPS)
        xhat = hv * r
        err = xhat * w_ref[...] - t_ref[...]
        per_tok = jnp.mean(err * err, axis=-1, keepdims=True)
        loss_ref[...] += 0.5 * jnp.sum(per_tok, axis=0, keepdims=True)
        dy = err * (1.0 / d)
        dxhat = dy * w_ref[...]
        dh = r * (dxhat - xhat * jnp.mean(dxhat * xhat, axis=-1, keepdims=True))
        dh_ref[...] = dh
        dhb_ref[...] = dh.astype(BF16)
        dw_ref[...] += jnp.sum(dy * xhat, axis=0, keepdims=True)

    tile = pl.BlockSpec((tm, d), lambda i: (i, 0))
    row = pl.BlockSpec((1, d), lambda i: (0, 0))
    return pl.pallas_call(
        body, name=name, grid=(s // tm,),
        in_specs=[tile, row, tile],
        out_specs=[pl.BlockSpec((1, 1), lambda i: (0, 0)), tile, tile, row],
        out_shape=[jax.ShapeDtypeStruct((1, 1), F32), jax.ShapeDtypeStruct((s, d), F32),
                   jax.ShapeDtypeStruct((s, d), BF16), jax.ShapeDtypeStruct((1, d), F32)],
        compiler_params=_params(("arbitrary",)),
    )(h2, wf, target)


CONV_ROWS = 256
CONV_ROWS_FWD = 512
HALO = 8


def _rows_with_halo(ref, r0, rows, s, before, after):
    tile = 16 if ref.dtype == BF16 else HALO
    parts = []
    if before:
        prev = ref[pl.ds(pl.multiple_of(jnp.maximum(r0 - tile, 0), tile), tile), :].astype(F32)[tile - HALO:]
        parts.append(jnp.where(r0 > 0, prev, 0.0))
    parts.append(ref[pl.ds(r0, rows), :].astype(F32))
    if after:
        nxt = ref[pl.ds(pl.multiple_of(jnp.minimum(r0 + rows, s - tile), tile), tile), :].astype(F32)[:HALO]
        parts.append(jnp.where(r0 + rows < s, nxt, 0.0))
    return jnp.concatenate(parts, axis=0) if len(parts) > 1 else parts[0]


def _window(x_ref, r0, s, after):
    return _rows_with_halo(x_ref, r0, CONV_ROWS_FWD, s, True, after).astype(F32)


def _shifted(window, k, rows):
    if k == 0:
        return window[HALO:HALO + rows]
    return pltpu.roll(window, k, 0)[HALO:HALO + rows]


def _conv_taps(window, w_ref, kk, rows):
    acc = None
    for i in range(kk):
        term = w_ref[i:i + 1, :] * _shifted(window, kk - 1 - i, rows)
        acc = term if acc is None else acc + term
    return acc


def _row_loop(rows, step):
    def body(r, carry):
        return step(pl.multiple_of(r * rows, rows), carry)
    return body


def _conv_bwd_rows(x, dpe, w_ref, kk):
    dp = dpe[:CONV_ROWS]
    dx = None
    dws = []
    for i in range(kk):
        k = kk - 1 - i
        later = dp if k == 0 else pltpu.roll(dpe, dpe.shape[0] - k, 0)[:CONV_ROWS]
        dws.append(jnp.sum(later * x, axis=0, keepdims=True))
        term = w_ref[i:i + 1, :] * later
        dx = term if dx is None else dx + term
    return dx, dws, jnp.sum(dp, axis=0, keepdims=True)


def _conv_a_fwd(xraw, w, b, *, name, tc=128):
    s, c = xraw.shape
    kk = 4

    def body(x_ref, w_ref, b_ref, o_ref, pre_ref):
        def step(r0, carry):
            pre = _conv_taps(_window(x_ref, r0, s, False), w_ref, kk, CONV_ROWS_FWD) + b_ref[...]
            o_ref[pl.ds(r0, CONV_ROWS_FWD), :] = pre * _sigmoid(pre)
            pre_ref[pl.ds(r0, CONV_ROWS_FWD), :] = pre.astype(BF16)
            return carry

        lax.fori_loop(0, s // CONV_ROWS_FWD, _row_loop(CONV_ROWS_FWD, step), 0)

    col = pl.BlockSpec((s, tc), lambda j: (0, j))
    return pl.pallas_call(
        body, name=name, grid=(c // tc,),
        in_specs=[col, pl.BlockSpec((8, tc), lambda j: (0, j)), pl.BlockSpec((1, tc), lambda j: (0, j))],
        out_specs=[col, col], out_shape=[jax.ShapeDtypeStruct((s, c), F32), jax.ShapeDtypeStruct((s, c), BF16)],
        compiler_params=_params(("parallel",)),
    )(xraw, w, b)


def _conv_a_bwd(xraw, pre, w, dy, *, name, tc=128):
    s, c = xraw.shape
    kk = 4

    def body(x_ref, pre_ref, w_ref, dy_ref, dx_ref, dw_ref, db_ref):
        def step(r0, carry):
            pre = _rows_with_halo(pre_ref, r0, CONV_ROWS, s, False, True)
            sg = _sigmoid(pre)
            dpe = _rows_with_halo(dy_ref, r0, CONV_ROWS, s, False, True) * (sg * (1.0 + pre * (1.0 - sg)))
            dx, dws, db = _conv_bwd_rows(x_ref[pl.ds(r0, CONV_ROWS), :].astype(F32), dpe, w_ref, kk)
            dx_ref[pl.ds(r0, CONV_ROWS), :] = dx.astype(BF16)
            return tuple(acc + new for acc, new in zip(carry, dws + [db]))

        zero = jnp.zeros((1, tc), F32)
        sums = lax.fori_loop(0, s // CONV_ROWS, _row_loop(CONV_ROWS, step), (zero,) * (kk + 1))
        db_ref[...] = sums[kk]
        dw_ref[...] = jnp.concatenate(list(sums[:kk]) + [jnp.zeros((8 - kk, tc), F32)], axis=0)

    col = pl.BlockSpec((s, tc), lambda j: (0, j))
    w8 = pl.BlockSpec((8, tc), lambda j: (0, j))
    row = pl.BlockSpec((1, tc), lambda j: (0, j))
    return pl.pallas_call(
        body, name=name, grid=(c // tc,),
        in_specs=[col, col, w8, col], out_specs=[col, w8, row],
        out_shape=[jax.ShapeDtypeStruct((s, c), BF16), jax.ShapeDtypeStruct((8, c), F32),
                   jax.ShapeDtypeStruct((1, c), F32)],
        compiler_params=_params(("parallel",)),
    )(xraw, pre, w, dy)


def _conv_f_fwd(up_raw, w, b, *, name, tc=128):
    s, c2 = up_raw.shape
    c = c2 // 2
    nb = c // tc
    kk = 3

    def body(xa_ref, xv_ref, wa_ref, wv_ref, ba_ref, bv_ref, o_ref, a_out, v_out):
        def step(r0, carry):
            a = _conv_taps(_window(xa_ref, r0, s, False), wa_ref, kk, CONV_ROWS_FWD) + ba_ref[...]
            v = _conv_taps(_window(xv_ref, r0, s, False), wv_ref, kk, CONV_ROWS_FWD) + bv_ref[...]
            o_ref[pl.ds(r0, CONV_ROWS_FWD), :] = (a * _sigmoid(a) * v).astype(BF16)
            a_out[pl.ds(r0, CONV_ROWS_FWD), :] = a.astype(BF16)
            v_out[pl.ds(r0, CONV_ROWS_FWD), :] = v.astype(BF16)
            return carry

        lax.fori_loop(0, s // CONV_ROWS_FWD, _row_loop(CONV_ROWS_FWD, step), 0)

    col_a = pl.BlockSpec((s, tc), lambda j: (0, j))
    col_v = pl.BlockSpec((s, tc), lambda j: (0, j + nb))
    half = jax.ShapeDtypeStruct((s, c), BF16)
    return pl.pallas_call(
        body, name=name, grid=(nb,),
        in_specs=[col_a, col_v, pl.BlockSpec((8, tc), lambda j: (0, j)), pl.BlockSpec((8, tc), lambda j: (0, j + nb)),
                  pl.BlockSpec((1, tc), lambda j: (0, j)), pl.BlockSpec((1, tc), lambda j: (0, j + nb))],
        out_specs=[col_a, col_a, col_a], out_shape=[half, half, half],
        compiler_params=_params(("parallel",)),
    )(up_raw, up_raw, w, w, b, b)


def _conv_f_bwd(up_raw, a_pre, v_pre, w, dact, *, name, tc=128):
    s, c2 = up_raw.shape
    c = c2 // 2
    nb = c // tc
    kk = 3

    def body(xa_ref, xv_ref, a_ref, v_ref, wa_ref, wv_ref, d_ref,
             dxa_ref, dxv_ref, dwa_ref, dwv_ref, dba_ref, dbv_ref):
        def step(r0, carry):
            a = _rows_with_halo(a_ref, r0, CONV_ROWS, s, False, True)
            v = _rows_with_halo(v_ref, r0, CONV_ROWS, s, False, True)
            sg = _sigmoid(a)
            d = _rows_with_halo(d_ref, r0, CONV_ROWS, s, False, True)
            rows = pl.ds(r0, CONV_ROWS)
            dxa, dwas, dba = _conv_bwd_rows(xa_ref[rows, :].astype(F32), d * v * (sg * (1.0 + a * (1.0 - sg))),
                                            wa_ref, kk)
            dxv, dwvs, dbv = _conv_bwd_rows(xv_ref[rows, :].astype(F32), d * (a * sg), wv_ref, kk)
            dxa_ref[pl.ds(r0, CONV_ROWS), :] = dxa.astype(BF16)
            dxv_ref[pl.ds(r0, CONV_ROWS), :] = dxv.astype(BF16)
            return tuple(acc + new for acc, new in zip(carry, dwas + [dba] + dwvs + [dbv]))

        zero = jnp.zeros((1, tc), F32)
        sums = lax.fori_loop(0, s // CONV_ROWS, _row_loop(CONV_ROWS, step), (zero,) * (2 * kk + 2))
        pad = [jnp.zeros((8 - kk, tc), F32)]
        dwa_ref[...] = jnp.concatenate(list(sums[:kk]) + pad, axis=0)
        dba_ref[...] = sums[kk]
        dwv_ref[...] = jnp.concatenate(list(sums[kk + 1:2 * kk + 1]) + pad, axis=0)
        dbv_ref[...] = sums[2 * kk + 1]

    col_a = pl.BlockSpec((s, tc), lambda j: (0, j))
    col_v = pl.BlockSpec((s, tc), lambda j: (0, j + nb))
    w_a = pl.BlockSpec((8, tc), lambda j: (0, j))
    w_v = pl.BlockSpec((8, tc), lambda j: (0, j + nb))
    r_a = pl.BlockSpec((1, tc), lambda j: (0, j))
    r_v = pl.BlockSpec((1, tc), lambda j: (0, j + nb))
    outs = pl.pallas_call(
        body, name=name, grid=(nb,),
        in_specs=[col_a, col_v, col_a, col_a, w_a, w_v, col_a],
        out_specs=[col_a, col_a, w_a, w_a, r_a, r_a],
        out_shape=[jax.ShapeDtypeStruct((s, c), BF16), jax.ShapeDtypeStruct((s, c), BF16),
                   jax.ShapeDtypeStruct((8, c), F32), jax.ShapeDtypeStruct((8, c), F32),
                   jax.ShapeDtypeStruct((1, c), F32), jax.ShapeDtypeStruct((1, c), F32)],
        compiler_params=_params(("parallel",)),
    )(up_raw, up_raw, a_pre, v_pre, w, w, dact)
    return outs


def _tri_masks():
    row = lax.broadcasted_iota(jnp.int32, (CHUNK, CHUNK), 0)
    col = lax.broadcasted_iota(jnp.int32, (CHUNK, CHUNK), 1)
    return row >= col, row <= col


def _ssd_fwd(xbc, dt_raw, z, dt_bias, a_log, a_log_x, d_skip_x, norm_w, expand, *, name):
    s = xbc.shape[0]
    nc = s // CHUNK

    def body(xbc_ref, dtr_ref, z_ref, dtb_ref, alog_ref, alogx_ref, dskx_ref, nw_ref, e_ref,
             y_ref, ya_ref, st_ref, state):
        @pl.when(pl.program_id(0) == 0)
        def _():
            state[...] = jnp.zeros_like(state)

        st_ref[0] = state[...]
        lower, _ = _tri_masks()
        dt = _softplus(dtr_ref[...] + dtb_ref[...])
        adt = dt * (-jnp.exp(alog_ref[...]))
        acum = _dot_exact_lhs(lower.astype(BF16), _split3(adt))
        acum_t = acum.T
        dt_terms, acum_terms = _split3(dt), _split3(acum)
        for g in range(SSD_GROUPS):
            sl = slice(GROUP_COLS * g, GROUP_COLS * (g + 1))
            dt_x = _dot_terms(dt_terms[:2], e_ref[:, sl])
            acum_x = _dot_terms(acum_terms, e_ref[:, sl])
            tot_x = jnp.sum(dt_x * (-jnp.exp(alogx_ref[:, sl])), axis=0, keepdims=True)
            xs = xbc_ref[:, sl]
            xdt = xs * dt_x
            xdt_b = xdt.astype(BF16)
            bg = xbc_ref[:, SSD_D_INNER + SSD_STATE * g:SSD_D_INNER + SSD_STATE * (g + 1)].astype(BF16)
            cg = xbc_ref[:, SSD_D_INNER + SSD_BC + SSD_STATE * g:SSD_D_INNER + SSD_BC + SSD_STATE * (g + 1)].astype(BF16)
            cb = _dot(cg, bg, NT)
            st_g = state[:, sl]
            y_off = _dot(cg, st_g.astype(BF16)) * jnp.exp(acum_x)
            parts = []
            for r in range(SSD_HEADS_PER_GROUP):
                h = SSD_HEADS_PER_GROUP * g + r
                dec = jnp.exp(jnp.where(lower, acum[:, h:h + 1] - acum_t[h:h + 1, :], -jnp.inf))
                parts.append(_dot((cb * dec).astype(BF16), xdt_b[:, SSD_HEAD_DIM * r:SSD_HEAD_DIM * (r + 1)]))
            y_ref[:, sl] = jnp.concatenate(parts, axis=1) + y_off + dskx_ref[:, sl] * xs
            wgt = (xdt * jnp.exp(tot_x - acum_x)).astype(BF16)
            state[:, sl] = st_g * jnp.exp(tot_x) + _dot(bg, wgt, TN)
        zv = z_ref[...].astype(F32)
        q = y_ref[...] * (zv * _sigmoid(zv))
        r = lax.rsqrt(jnp.mean(q * q, axis=-1, keepdims=True) + NORM_EPS)
        ya_ref[...] = (q * r * nw_ref[...]).astype(BF16)

    def chunk(w):
        return pl.BlockSpec((CHUNK, w), lambda c: (c, 0))

    def const(shape):
        return pl.BlockSpec(shape, lambda c: (0,) * len(shape))

    return pl.pallas_call(
        body, name=name, grid=(nc,),
        in_specs=[chunk(SSD_XBC), chunk(LANES), chunk(SSD_D_INNER), const((1, LANES)), const((1, LANES)),
                  const((1, SSD_D_INNER)), const((1, SSD_D_INNER)), const((1, SSD_D_INNER)),
                  const((LANES, SSD_D_INNER))],
        out_specs=[chunk(SSD_D_INNER), chunk(SSD_D_INNER),
                   pl.BlockSpec((1, SSD_STATE, SSD_D_INNER), lambda c: (c, 0, 0))],
        out_shape=[jax.ShapeDtypeStruct((s, SSD_D_INNER), F32), jax.ShapeDtypeStruct((s, SSD_D_INNER), BF16),
                   jax.ShapeDtypeStruct((nc, SSD_STATE, SSD_D_INNER), F32)],
        scratch_shapes=[pltpu.VMEM((SSD_STATE, SSD_D_INNER), F32)],
        compiler_params=_params(("arbitrary",)),
    )(xbc, dt_raw, z, dt_bias, a_log, a_log_x, d_skip_x, norm_w, expand)


def _ssd_bwd(dya, y, z, xbc, dt_raw, states, dt_bias, a_log, a_log_x, d_skip_x, norm_w, expand, expand_t, *, name):
    s = xbc.shape[0]
    nc = s // CHUNK

    def body(dya_ref, y_ref, z_ref, xbc_ref, dtr_ref, stp_ref, dtb_ref, alog_ref, alogx_ref, dskx_ref, nw_ref,
             e_ref, et_ref, dz_ref, dxbc_ref, ddt_ref, dnw_ref, ddsk_ref, dalog_ref, ddtb_ref,
             dstate, dy_sc, dskcol):
        i = pl.program_id(0)

        @pl.when(i == 0)
        def _():
            dstate[...] = jnp.zeros_like(dstate)
            dskcol[...] = jnp.zeros_like(dskcol)
            dnw_ref[...] = jnp.zeros_like(dnw_ref)
            dalog_ref[...] = jnp.zeros_like(dalog_ref)
            ddtb_ref[...] = jnp.zeros_like(ddtb_ref)
            ddsk_ref[...] = jnp.zeros_like(ddsk_ref)

        lower, upper = _tri_masks()
        rows = lax.broadcasted_iota(jnp.int32, (CHUNK, LANES), 0)
        pre = dtr_ref[...] + dtb_ref[...]
        dt = _softplus(pre)
        a = -jnp.exp(alog_ref[...])
        acum = _dot_exact_lhs(lower.astype(BF16), _split3(dt * a))
        acum_t = acum.T
        dt_terms, acum_terms = _split3(dt), _split3(acum)

        yv = y_ref[...]
        zv = z_ref[...].astype(F32)
        sz = _sigmoid(zv)
        silu_z = zv * sz
        q = yv * silu_z
        r = lax.rsqrt(jnp.mean(q * q, axis=-1, keepdims=True) + NORM_EPS)
        qhat = q * r
        dyav = dya_ref[...]
        dqhat = dyav * nw_ref[...]
        dnw_ref[...] += jnp.sum(dyav * qhat, axis=0, keepdims=True)
        dq = r * (dqhat - qhat * jnp.mean(dqhat * qhat, axis=-1, keepdims=True))
        dy_sc[...] = dq * silu_z
        dz_ref[...] = (dq * yv * (sz * (1.0 + zv * (1.0 - sz)))).astype(BF16)

        da_cum = jnp.zeros((CHUNK, LANES), F32)
        ddt = jnp.zeros((CHUNK, LANES), F32)
        for g in range(SSD_GROUPS):
            sl = slice(GROUP_COLS * g, GROUP_COLS * (g + 1))
            et_g = et_ref[sl, :]
            dt_x = _dot_terms(dt_terms[:2], e_ref[:, sl])
            acum_x = _dot_terms(acum_terms, e_ref[:, sl])
            tot_x = jnp.sum(dt_x * (-jnp.exp(alogx_ref[:, sl])), axis=0, keepdims=True)
            e_tot = jnp.exp(tot_x)
            dec_s = jnp.exp(tot_x - acum_x)
            xs = xbc_ref[:, sl]
            xdt = xs * dt_x
            xdt_b = xdt.astype(BF16)
            dy = dy_sc[:, sl]
            dy_b = dy.astype(BF16)
            dskx = dskx_ref[:, sl]
            y_ssd = y_ref[:, sl] - dskx * xs
            dskcol[:, sl] += jnp.sum(dy * xs, axis=0, keepdims=True)
            bg = xbc_ref[:, SSD_D_INNER + SSD_STATE * g:SSD_D_INNER + SSD_STATE * (g + 1)].astype(BF16)
            cg = xbc_ref[:, SSD_D_INNER + SSD_BC + SSD_STATE * g:SSD_D_INNER + SSD_BC + SSD_STATE * (g + 1)].astype(BF16)
            cb_t = _dot(bg, cg, NT)
            sp = stp_ref[0, :, sl]
            ds_g = dstate[:, sl]
            ds_b = ds_g.astype(BF16)
            dye_b = (dy * jnp.exp(acum_x)).astype(BF16)
            dc = _dot(dye_b, sp.astype(BF16), NT)
            dxdt_state = dec_s * _dot(bg, ds_b)
            db = _dot((xdt * dec_s).astype(BF16), ds_b, NT)
            dcb_t = jnp.zeros((CHUNK, CHUNK), F32)
            parts = []
            for rr in range(SSD_HEADS_PER_GROUP):
                h = SSD_HEADS_PER_GROUP * g + rr
                hs = slice(SSD_HEAD_DIM * rr, SSD_HEAD_DIM * (rr + 1))
                dec_t = jnp.exp(jnp.where(upper, acum_t[h:h + 1, :] - acum[:, h:h + 1], -jnp.inf))
                parts.append(_dot((cb_t * dec_t).astype(BF16), dy_b[:, hs]))
                dcb_t = dcb_t + _dot(xdt_b[:, hs], dy_b[:, hs], NT) * dec_t
            dxdt = jnp.concatenate(parts, axis=1) + dxdt_state
            dcb_tb = dcb_t.astype(BF16)
            dc = dc + _dot(dcb_tb, bg, TN)
            db = db + _dot(dcb_tb, cg)
            tot_col = jnp.sum(ds_g * sp, axis=0, keepdims=True) * e_tot + jnp.sum(dxdt_state * xdt, axis=0, keepdims=True)
            d_tot = _dot_terms(_split3(jnp.broadcast_to(tot_col, (8, GROUP_COLS))), et_g)
            d_tot = jnp.max(d_tot, axis=0, keepdims=True)
            pair_sums = dy_b.astype(F32) * y_ssd - xdt_b.astype(F32) * dxdt
            da_cum = da_cum + _dot_terms(_split3(pair_sums), et_g) + jnp.where(rows == CHUNK - 1, d_tot, 0.0)
            ddt = ddt + _dot_terms(_split3(dxdt * xs)[:2], et_g)
            dxbc_ref[:, sl] = dy * dskx + dxdt * dt_x
            dxbc_ref[:, SSD_D_INNER + SSD_STATE * g:SSD_D_INNER + SSD_STATE * (g + 1)] = db
            dxbc_ref[:, SSD_D_INNER + SSD_BC + SSD_STATE * g:SSD_D_INNER + SSD_BC + SSD_STATE * (g + 1)] = dc
            dstate[:, sl] = e_tot * ds_g + _dot(cg, dye_b, TN)

        dadt = _dot_exact_lhs(upper.astype(BF16), _split3(da_cum))
        ddt = ddt + dadt * a
        dalog_ref[...] += jnp.sum(dadt * dt, axis=0, keepdims=True)
        dpre = ddt * _sigmoid(pre)
        ddtb_ref[...] += jnp.sum(dpre, axis=0, keepdims=True)
        ddt_ref[...] = dpre.astype(BF16)

        @pl.when(i == nc - 1)
        def _():
            dalog_ref[...] = dalog_ref[...] * a
            dsk = _dot_terms(_split3(jnp.broadcast_to(dskcol[...], (8, SSD_D_INNER))), et_ref[...])
            ddsk_ref[...] = jnp.max(dsk, axis=0, keepdims=True)

    def chunk(w):
        return pl.BlockSpec((CHUNK, w), lambda i: (nc - 1 - i, 0))

    def const(shape):
        return pl.BlockSpec(shape, lambda i: (0,) * len(shape))

    return pl.pallas_call(
        body, name=name, grid=(nc,),
        in_specs=[chunk(SSD_D_INNER), chunk(SSD_D_INNER), chunk(SSD_D_INNER), chunk(SSD_XBC), chunk(LANES),
                  pl.BlockSpec((1, SSD_STATE, SSD_D_INNER), lambda i: (nc - 1 - i, 0, 0)),
                  const((1, LANES)), const((1, LANES)), const((1, SSD_D_INNER)), const((1, SSD_D_INNER)),
                  const((1, SSD_D_INNER)), const((LANES, SSD_D_INNER)), const((SSD_D_INNER, LANES))],
        out_specs=[chunk(SSD_D_INNER), chunk(SSD_XBC), chunk(LANES), const((1, SSD_D_INNER)), const((1, LANES)),
                   const((1, LANES)), const((1, LANES))],
        out_shape=[jax.ShapeDtypeStruct((s, SSD_D_INNER), BF16), jax.ShapeDtypeStruct((s, SSD_XBC), F32),
                   jax.ShapeDtypeStruct((s, LANES), BF16), jax.ShapeDtypeStruct((1, SSD_D_INNER), F32),
                   jax.ShapeDtypeStruct((1, LANES), F32), jax.ShapeDtypeStruct((1, LANES), F32),
                   jax.ShapeDtypeStruct((1, LANES), F32)],
        scratch_shapes=[pltpu.VMEM((SSD_STATE, SSD_D_INNER), F32), pltpu.VMEM((CHUNK, SSD_D_INNER), F32),
                        pltpu.VMEM((1, SSD_D_INNER), F32)],
        compiler_params=_params(("arbitrary",)),
    )(dya, y, z, xbc, dt_raw, states, dt_bias, a_log, a_log_x, d_skip_x, norm_w, expand, expand_t)


GELU_K = math.sqrt(2.0 / math.pi)
GELU_C = 0.044715


def _gelu(x):
    return 0.5 * x * (1.0 + jnp.tanh(GELU_K * (x + GELU_C * x * x * x)))


def _gelu_grad(x):
    t = jnp.tanh(GELU_K * (x + GELU_C * x * x * x))
    return 0.5 * (1.0 + t) + 0.5 * x * (1.0 - t * t) * (GELU_K * (1.0 + 3.0 * GELU_C * x * x))


def _sgu_pre(uv_ref, uvb_ref, lnw_ref, lnb_ref):
    uv = uv_ref[...].astype(F32) + uvb_ref[...]
    guv = _gelu(uv)
    u = guv[:, :SGU_WIDTH]
    v = guv[:, SGU_WIDTH:]
    mu = jnp.mean(v, axis=-1, keepdims=True)
    vc = v - mu
    rstd = lax.rsqrt(jnp.mean(vc * vc, axis=-1, keepdims=True) + LN_EPS)
    vhat = vc * rstd
    vn = vhat * lnw_ref[...] + lnb_ref[...]
    return uv, u, vhat, rstd, vn


def _sgu_fwd(uv_raw, uv_b, ln_w, ln_b, w_sp, b_sp_t, *, name):
    s = uv_raw.shape[0]
    nc = s // CHUNK

    def body(uv_ref, uvb_ref, lnw_ref, lnb_ref, w_ref, bt_ref, o_ref):
        lower, _ = _tri_masks()
        _, u, _, _, vn = _sgu_pre(uv_ref, uvb_ref, lnw_ref, lnb_ref)
        vn_b = vn.astype(BF16)
        bt = bt_ref[...]
        for g in range(SGU_GROUPS):
            gs = slice(LANES * g, LANES * (g + 1))
            wc = jnp.where(lower, w_ref[g], 0.0).astype(BF16)
            mixed = _dot(wc, vn_b[:, gs]) + bt[:, g:g + 1]
            o_ref[:, gs] = (u[:, gs] * mixed).astype(BF16)

    def const(shape):
        return pl.BlockSpec(shape, lambda c: (0,) * len(shape))

    return pl.pallas_call(
        body, name=name, grid=(nc,),
        in_specs=[pl.BlockSpec((CHUNK, 2 * SGU_WIDTH), lambda c: (c, 0)), const((1, 2 * SGU_WIDTH)),
                  const((1, SGU_WIDTH)), const((1, SGU_WIDTH)), const((SGU_GROUPS, CHUNK, CHUNK)),
                  const((CHUNK, LANES))],
        out_specs=pl.BlockSpec((CHUNK, SGU_WIDTH), lambda c: (c, 0)),
        out_shape=jax.ShapeDtypeStruct((s, SGU_WIDTH), BF16),
        compiler_params=_params(("parallel",)),
    )(uv_raw, uv_b, ln_w, ln_b, w_sp, b_sp_t)


def _sgu_bwd(uv_raw, dyb, uv_b, ln_w, ln_b, w_sp, b_sp_t, group_sum, *, name):
    s = uv_raw.shape[0]
    nc = s // CHUNK

    def body(uv_ref, dy_ref, uvb_ref, lnw_ref, lnb_ref, w_ref, bt_ref, gsum_ref,
             duv_ref, dw_ref, dbt_ref, dlnw_ref, dlnb_ref, duvb_ref):
        @pl.when(pl.program_id(0) == 0)
        def _():
            dw_ref[...] = jnp.zeros_like(dw_ref)
            dbt_ref[...] = jnp.zeros_like(dbt_ref)
            dlnw_ref[...] = jnp.zeros_like(dlnw_ref)
            dlnb_ref[...] = jnp.zeros_like(dlnb_ref)
            duvb_ref[...] = jnp.zeros_like(duvb_ref)

        lower, _ = _tri_masks()
        uv, u, vhat, rstd, vn = _sgu_pre(uv_ref, uvb_ref, lnw_ref, lnb_ref)
        vn_b = vn.astype(BF16)
        bt = bt_ref[...]
        dy = dy_ref[...].astype(F32)
        du_parts, dvn_parts, dmix_parts = [], [], []
        for g in range(SGU_GROUPS):
            gs = slice(LANES * g, LANES * (g + 1))
            wc = jnp.where(lower, w_ref[g], 0.0).astype(BF16)
            mixed = _dot(wc, vn_b[:, gs]) + bt[:, g:g + 1]
            du_parts.append(dy[:, gs] * mixed)
            dmix = dy[:, gs] * u[:, gs]
            dmix_b = dmix.astype(BF16)
            dmix_parts.append(dmix)
            dw_ref[g] += jnp.where(lower, _dot(dmix_b, vn_b[:, gs], NT), 0.0)
            dvn_parts.append(_dot(wc, dmix_b, TN))
        dmixed = jnp.concatenate(dmix_parts, axis=1)
        dbt_ref[...] += _dot_terms(_split3(dmixed), gsum_ref[...])
        dvn = jnp.concatenate(dvn_parts, axis=1)
        dlnw_ref[...] += jnp.sum(dvn * vhat, axis=0, keepdims=True)
        dlnb_ref[...] += jnp.sum(dvn, axis=0, keepdims=True)
        dvhat = dvn * lnw_ref[...]
        dv = rstd * (dvhat - jnp.mean(dvhat, axis=-1, keepdims=True)
                     - vhat * jnp.mean(dvhat * vhat, axis=-1, keepdims=True))
        dguv = jnp.concatenate(du_parts + [dv], axis=1)
        duv = dguv * _gelu_grad(uv)
        duvb_ref[...] += jnp.sum(duv, axis=0, keepdims=True)
        duv_ref[...] = duv.astype(BF16)

    def const(shape):
        return pl.BlockSpec(shape, lambda c: (0,) * len(shape))

    return pl.pallas_call(
        body, name=name, grid=(nc,),
        in_specs=[pl.BlockSpec((CHUNK, 2 * SGU_WIDTH), lambda c: (c, 0)),
                  pl.BlockSpec((CHUNK, SGU_WIDTH), lambda c: (c, 0)), const((1, 2 * SGU_WIDTH)),
                  const((1, SGU_WIDTH)), const((1, SGU_WIDTH)), const((SGU_GROUPS, CHUNK, CHUNK)),
                  const((CHUNK, LANES)), const((SGU_WIDTH, LANES))],
        out_specs=[pl.BlockSpec((CHUNK, 2 * SGU_WIDTH), lambda c: (c, 0)), const((SGU_GROUPS, CHUNK, CHUNK)),
                   const((CHUNK, LANES)), const((1, SGU_WIDTH)), const((1, SGU_WIDTH)), const((1, 2 * SGU_WIDTH))],
        out_shape=[jax.ShapeDtypeStruct((s, 2 * SGU_WIDTH), BF16),
                   jax.ShapeDtypeStruct((SGU_GROUPS, CHUNK, CHUNK), F32), jax.ShapeDtypeStruct((CHUNK, LANES), F32),
                   jax.ShapeDtypeStruct((1, SGU_WIDTH), F32), jax.ShapeDtypeStruct((1, SGU_WIDTH), F32),
                   jax.ShapeDtypeStruct((1, 2 * SGU_WIDTH), F32)],
        compiler_params=_params(("arbitrary",)),
    )(uv_raw, dyb, uv_b, ln_w, ln_b, w_sp, b_sp_t, group_sum)


def _gate_fwd(gates_raw, b_gate, p_a, p_b, *, name, tm=512):
    s = p_a.shape[0]
    tm = min(tm, s)

    def body(ga_ref, gb_ref, ba_ref, bb_ref, pa_ref, pb_ref, o_ref):
        ga = _sigmoid(ga_ref[...].astype(F32) + ba_ref[...])
        gb = _sigmoid(gb_ref[...].astype(F32) + bb_ref[...])
        o_ref[...] = (ga * pa_ref[...].astype(F32) + gb * pb_ref[...].astype(F32)).astype(BF16)

    t_a = pl.BlockSpec((tm, D_MODEL), lambda i: (i, 0))
    t_b = pl.BlockSpec((tm, D_MODEL), lambda i: (i, 1))
    r_a = pl.BlockSpec((1, D_MODEL), lambda i: (0, 0))
    r_b = pl.BlockSpec((1, D_MODEL), lambda i: (0, 1))
    return pl.pallas_call(
        body, name=name, grid=(s // tm,),
        in_specs=[t_a, t_b, r_a, r_b, t_a, t_a], out_specs=t_a,
        out_shape=jax.ShapeDtypeStruct((s, D_MODEL), BF16),
        compiler_params=_params(("parallel",)),
    )(gates_raw, gates_raw, b_gate, b_gate, p_a, p_b)


def _gate_bwd(gates_raw, b_gate, p_a, p_b, dm, *, name, tm=512):
    s = p_a.shape[0]
    tm = min(tm, s)

    def body(ga_ref, gb_ref, ba_ref, bb_ref, pa_ref, pb_ref, dm_ref, dpa_ref, dpb_ref, dga_ref, dgb_ref,
             dba_ref, dbb_ref):
        @pl.when(pl.program_id(0) == 0)
        def _():
            dba_ref[...] = jnp.zeros_like(dba_ref)
            dbb_ref[...] = jnp.zeros_like(dbb_ref)

        d = dm_ref[...].astype(F32)
        for g_ref, b_ref, p_ref, dp_ref, dg_ref, db_ref in ((ga_ref, ba_ref, pa_ref, dpa_ref, dga_ref, dba_ref),
                                                            (gb_ref, bb_ref, pb_ref, dpb_ref, dgb_ref, dbb_ref)):
            sg = _sigmoid(g_ref[...].astype(F32) + b_ref[...])
            dp_ref[...] = (d * sg).astype(BF16)
            dg = d * p_ref[...].astype(F32) * (sg * (1.0 - sg))
            dg_ref[...] = dg.astype(BF16)
            db_ref[...] += jnp.sum(dg, axis=0, keepdims=True)

    t_a = pl.BlockSpec((tm, D_MODEL), lambda i: (i, 0))
    t_b = pl.BlockSpec((tm, D_MODEL), lambda i: (i, 1))
    r_a = pl.BlockSpec((1, D_MODEL), lambda i: (0, 0))
    r_b = pl.BlockSpec((1, D_MODEL), lambda i: (0, 1))
    big = jax.ShapeDtypeStruct((s, D_MODEL), BF16)
    row = jax.ShapeDtypeStruct((1, D_MODEL), F32)
    return pl.pallas_call(
        body, name=name, grid=(s // tm,),
        in_specs=[t_a, t_b, r_a, r_b, t_a, t_a, t_a], out_specs=[t_a, t_a, t_a, t_a, r_a, r_a],
        out_shape=[big, big, big, big, row, row],
        compiler_params=_params(("arbitrary",)),
    )(gates_raw, gates_raw, b_gate, b_gate, p_a, p_b, dm)


def _adamw_update(w_ref, g_ref, m_ref, v_ref, d_ref, mo_ref, vo_ref):
    gv = g_ref[...]
    mn = ADAM_B1 * m_ref[...] + (1.0 - ADAM_B1) * gv
    vn = ADAM_B2 * v_ref[...] + (1.0 - ADAM_B2) * (gv * gv)
    m_hat = mn / (1.0 - ADAM_B1 ** ADAM_STEP)
    v_hat = vn / (1.0 - ADAM_B2 ** ADAM_STEP)
    d_ref[...] = -ADAM_LR * (m_hat / (jnp.sqrt(v_hat) + ADAM_EPS) + ADAM_WD * w_ref[...])
    mo_ref[...] = mn
    vo_ref[...] = vn


def _adamw_many(ws, gs, ms, vs, *, name):
    n = len(ws)

    def body(*refs):
        for i in range(n):
            _adamw_update(*[refs[k * n + i] for k in range(7)])

    whole = pl.BlockSpec(memory_space=pltpu.VMEM)
    sds = [jax.ShapeDtypeStruct(w.shape, F32) for w in ws]
    outs = pl.pallas_call(
        body, name=name, in_specs=[whole] * (4 * n), out_specs=[whole] * (3 * n), out_shape=sds * 3,
        compiler_params=pltpu.CompilerParams(vmem_limit_bytes=VMEM_LIMIT),
    )(*ws, *gs, *ms, *vs)
    return outs[:n], outs[n:2 * n], outs[2 * n:]


def _adamw(w, g, m, v, *, name, tr=128):
    r, c = w.shape
    tr = min(tr, r)
    assert r % tr == 0, (name, r, tr)
    body = functools.partial(_adamw_update)

    blk = pl.BlockSpec((tr, c), lambda i: (i, 0))
    sds = jax.ShapeDtypeStruct((r, c), F32)
    return pl.pallas_call(
        body, name=name, grid=(r // tr,), in_specs=[blk] * 4, out_specs=[blk] * 3, out_shape=[sds] * 3,
        compiler_params=_params(("parallel",)),
    )(w, g, m, v)


def _adamw_two_sums(w, g_a, g_b, m, v, *, name, tr=128):
    r, c = w.shape
    tr = min(tr, r)
    assert r % tr == 0, (name, r, tr)

    def body(w_ref, ga_ref, gb_ref, m_ref, v_ref, g_ref, d_ref, mo_ref, vo_ref):
        g_ref[...] = ga_ref[...] + gb_ref[...]
        _adamw_update(w_ref, g_ref, m_ref, v_ref, d_ref, mo_ref, vo_ref)

    blk = pl.BlockSpec((tr, c), lambda i: (i, 0))
    sds = jax.ShapeDtypeStruct((r, c), F32)
    return pl.pallas_call(
        body, name=name, grid=(r // tr,), in_specs=[blk] * 5, out_specs=[blk] * 4, out_shape=[sds] * 4,
        compiler_params=_params(("parallel",)),
    )(w, g_a, g_b, m, v)


def _tile(n, pref):
    if n <= pref:
        return n
    best = LANES
    for t in range(LANES, pref + 1, LANES):
        if n % t == 0:
            best = t
    return best


MATMUL_BLOCK_BYTES = 20 * 1024 * 1024


def _mm(pairs, name, **kw):
    trans_b = kw.get("trans_b", False)
    m = (pairs[0][0][0] if isinstance(pairs[0][0], tuple) else pairs[0][0]).shape[0]
    ktot, n = 0, None
    for _, b in pairs:
        shape = b[0].shape[1:] if isinstance(b, tuple) else b.shape
        ktot += shape[1] if trans_b else shape[0]
        n = shape[0] if trans_b else shape[1]
    out_bytes = 4 * (2 if kw.get("add") is not None else 1)
    best = None
    for tm in (256, 512, 1024, 2048):
        for tn in range(LANES, min(n, 1536) + 1, LANES):
            if m % min(tm, m) or n % tn:
                continue
            fits = 2 * ktot * (min(tm, m) + tn) + out_bytes * min(tm, m) * tn <= MATMUL_BLOCK_BYTES
            if fits and (best is None or min(tm, m) * tn >= best[0] * best[1]):
                best = (min(tm, m), tn)
    return _matmul(pairs, tm=best[0], tn=best[1], name=name, **kw)


def _wgrad(a, b, name, **kw):
    return _matmul_tn(a, b, tk=_tile(a.shape[1], 1408), tn=kw.pop("tn", _tile(b.shape[1], 1024)), tm=2048,
                      name=name, **kw)


def _local_step(x, target, get_weight, small, emit_grad):
    heads = jnp.arange(SSD_D_INNER) // SSD_HEAD_DIM
    expand = (jnp.arange(LANES)[:, None] == heads[None, :]).astype(BF16)
    expand_t = expand.T
    group_sum = (jnp.arange(SGU_WIDTH)[:, None] // LANES == jnp.arange(LANES)[None, :]).astype(BF16)
    pad_h = LANES - SSD_HEADS
    dt_bias = jnp.pad(small["dt_bias"], ((0, 0), (0, pad_h)))
    a_log = jnp.pad(small["a_log"], ((0, 0), (0, pad_h)))
    a_log_x = jnp.repeat(small["a_log"], SSD_HEAD_DIM, axis=1)
    d_skip_x = jnp.repeat(small["d_skip"], SSD_HEAD_DIM, axis=1)
    b_sp_t = jnp.pad(small["b_spatial"][0].T, ((0, 0), (0, LANES - SGU_GROUPS)))
    w_sp = small["w_spatial"][0]
    conv_a_w = jnp.pad(small["conv_a_w"], ((0, 4), (0, 0)))
    conv_f_w = jnp.pad(small["conv_f_w"], ((0, 5), (0, 0)))
    final_w = small["final_norm_w"].reshape(1, D_MODEL)

    n1 = _rms_fwd(x, small["norm1_w"], after=small.get("gathers_started"), name="rms1_fwd")
    wts = dict(get_weight("w_in", n1))
    z = _mm([(n1, wts["in_z"])], "in_z")
    xbc_raw = _mm([(n1, wts["in_xbc"])], "in_xbc")
    dt_raw = _mm([(n1, wts["in_dt"])], "in_dt")
    uv_raw = _mm([(n1, wts["in_uv"])], "in_uv", out_dtype=BF16)
    gates_raw = _mm([(n1, wts["in_gate"])], "in_gate", out_dtype=BF16)
    xbc, xbc_pre = _conv_a_fwd(xbc_raw, conv_a_w, small["conv_a_b"], name="conv_a_fwd")
    y, y_a, states = _ssd_fwd(xbc, dt_raw, z, dt_bias, a_log, a_log_x, d_skip_x, small["ssd_norm_w"], expand,
                              name="ssd_fwd")
    y_b = _sgu_fwd(uv_raw, small["uv_b"], small["v_ln_w"], small["v_ln_b"], w_sp, b_sp_t, name="sgu_fwd")
    wts.update(get_weight("w_branch", y_b))
    p_a = _mm([(y_a, wts["branch_a"])], "branch_a", out_dtype=BF16)
    p_b = _mm([(y_b, wts["branch_b"])], "branch_b", out_dtype=BF16)
    mix = _gate_fwd(gates_raw, small["b_gate"], p_a, p_b, name="gate_fwd")
    wts.update(get_weight("w_out", mix))
    h1 = _mm([(mix, wts["out"])], "out_proj", add=x)
    n2 = _rms_fwd(h1, small["norm2_w"], name="rms2_fwd")
    wts.update(get_weight("w_up", n2))
    up_w = wts["up"]
    up_cols = up_w.shape[2]
    up_raw = _matmul([(n2, (up_w, "cols"))], tm=2048, tn=up_cols, out_dtype=BF16, name="up_proj")
    act, up_a, up_v = _conv_f_fwd(up_raw, conv_f_w, small["conv_f_b"], name="conv_f_fwd")
    wts.update(get_weight("w_down", act))
    h2 = _mm([(act, wts["down"])], "down_proj", add=h1)
    loss, dh2, dh2_b, d_final = _final_fwd_bwd(h2, final_w, target, name="final_norm_loss")

    dact = _mm([(dh2_b, wts["down"])], "down_dgrad", trans_b=True)
    started = emit_grad("w_down", _wgrad(act, dh2_b, "down_wgrad"))
    dup_a, dup_v, dwf_a, dwf_v, dbf_a, dbf_v = _conv_f_bwd(up_raw, up_a, up_v, conv_f_w, dact, name="conv_f_bwd")
    dn2 = _mm([((dup_a, 0), (up_w, 0)), ((dup_a, 1), (up_w, 1)), ((dup_v, 0), (up_w, 2)), ((dup_v, 1), (up_w, 3))],
              "up_dgrad", trans_b=True, after=started, out_dtype=BF16)
    started = emit_grad("w_up", jnp.concatenate([_wgrad(n2, dup_a, "up_wgrad_a", tn=up_cols, stack_out=True),
                                                 _wgrad(n2, dup_v, "up_wgrad_v", tn=up_cols, stack_out=True)], axis=0))
    dh1, dh1_b, d_norm2 = _rms_bwd(h1, small["norm2_w"], dn2, dh2, name="rms2_bwd")
    dmix = _mm([(dh1_b, wts["out"])], "out_dgrad", trans_b=True, after=started, out_dtype=BF16)
    started = emit_grad("w_out", _wgrad(mix, dh1_b, "out_wgrad"))
    dp_a, dp_b, dg_a, dg_b, dbg_a, dbg_b = _gate_bwd(gates_raw, small["b_gate"], p_a, p_b, dmix, name="gate_bwd")
    dya = _mm([(dp_a, wts["branch_a"])], "branch_a_dgrad", trans_b=True, after=started)
    dyb = _mm([(dp_b, wts["branch_b"])], "branch_b_dgrad", trans_b=True, out_dtype=BF16)
    started_branch = emit_grad("w_branch", jnp.concatenate([_wgrad(y_a, dp_a, "branch_a_wgrad"),
                                                            _wgrad(y_b, dp_b, "branch_b_wgrad")], axis=0))
    duv, d_wsp, d_bsp_t, d_lnw, d_lnb, d_uvb = _sgu_bwd(uv_raw, dyb, small["uv_b"], small["v_ln_w"],
                                                        small["v_ln_b"], w_sp, b_sp_t, group_sum, name="sgu_bwd")
    dz, dxbc, ddt, d_ssd_nw, d_dskip, d_alog, d_dtb = _ssd_bwd(
        dya, y, z, xbc, dt_raw, states, dt_bias, a_log, a_log_x, d_skip_x, small["ssd_norm_w"], expand, expand_t,
        name="ssd_bwd")
    dxbc_raw, d_conv_a_w, d_conv_a_b = _conv_a_bwd(xbc_raw, xbc_pre, conv_a_w, dxbc, name="conv_a_bwd")
    started = emit_grad("w_in", {
        "in_z": _wgrad(n1, dz, "in_z_wgrad", after=started_branch), "in_xbc": _wgrad(n1, dxbc_raw, "in_xbc_wgrad"),
        "in_dt": _wgrad(n1, ddt, "in_dt_wgrad")[:, :SSD_HEADS], "in_uv": _wgrad(n1, duv, "in_uv_wgrad"),
        "in_gate_a": _wgrad(n1, dg_a, "in_gate_a_wgrad"), "in_gate_b": _wgrad(n1, dg_b, "in_gate_b_wgrad")})
    dn1 = _mm([(dz, wts["in_z"]), (dxbc_raw, wts["in_xbc"]), (ddt, wts["in_dt"]), (duv, wts["in_uv"]),
               (dg_a, wts["in_gate_a"]), (dg_b, wts["in_gate_b"])], "in_dgrad", trans_b=True, after=started,
              out_dtype=BF16)
    dx, _, d_norm1 = _rms_bwd(x, small["norm1_w"], dn1, dh1, name="rms1_bwd")

    grads_small = {
        "norm1_w": d_norm1, "b_gate": jnp.concatenate([dbg_a, dbg_b], axis=1),
        "conv_a_w": d_conv_a_w[:4], "conv_a_b": d_conv_a_b,
        "dt_bias": d_dtb[:, :SSD_HEADS], "a_log": d_alog[:, :SSD_HEADS], "d_skip": d_dskip[:, :SSD_HEADS],
        "ssd_norm_w": d_ssd_nw, "uv_b": d_uvb, "v_ln_w": d_lnw, "v_ln_b": d_lnb,
        "w_spatial": d_wsp[None], "b_spatial": d_bsp_t[:, :SGU_GROUPS].T[None],
        "norm2_w": d_norm2, "conv_f_w": jnp.concatenate([dwf_a[:3], dwf_v[:3]], axis=1),
        "conv_f_b": jnp.concatenate([dbf_a, dbf_v], axis=1), "final_norm_w": d_final.reshape(D_MODEL),
    }
    return loss, dx, grads_small


HBM = pl.BlockSpec(memory_space=pl.ANY)
MESH = pl.DeviceIdType.MESH


def _mesh_pos():
    return lax.axis_index("x"), lax.axis_index("y"), lax.axis_index("c")


def _other_chips(x, y):
    return [(1 - x, y), (x, 1 - y), (1 - x, 1 - y)]


def _remote(src, dst, send_sems, recv_sems, k, dev):
    return pltpu.make_async_remote_copy(src_ref=src, dst_ref=dst, send_sem=send_sems.at[k], recv_sem=recv_sems.at[k],
                                        device_id=dev, device_id_type=MESH)


def _dma_sems(n):
    return [pltpu.SemaphoreType.DMA((n,)), pltpu.SemaphoreType.DMA((n,))]


HBM_ONLY = pl.BlockSpec(memory_space=pltpu.HBM)
SEMAPHORES = pl.BlockSpec(memory_space=pltpu.SEMAPHORE)
DATAFLOW_EFFECT = pltpu.SideEffectType.DATAFLOW_SIDE_EFFECTING
N_PEER_CHIPS = N_CHIPS - 1


def _gather_sends(w_ref, land_ref, send_sems, recv_sems):
    x, y, c = _mesh_pos()
    return [_remote(w_ref.at[c], land_ref.at[2 * x + y, c], send_sems, recv_sems, k, (px, py, c))
            for k, (px, py) in enumerate(_other_chips(x, y))]


def _gather_arrivals(w_ref, land_ref, send_sems, recv_sems):
    x, y, c = _mesh_pos()
    return [_remote(w_ref.at[c], land_ref.at[2 * px + py, c], send_sems, recv_sems, k, (px, py, c))
            for k, (px, py) in enumerate(_other_chips(x, y))]


def _gather_whole_sends(w_ref, land_ref, send_sems, recv_sems):
    x, y, c = _mesh_pos()
    return [_remote(w_ref, land_ref.at[2 * x + y], send_sems, recv_sems, k, (px, py, c))
            for k, (px, py) in enumerate(_other_chips(x, y))]


def _gather_whole_arrivals(w_ref, land_ref, send_sems, recv_sems):
    x, y, c = _mesh_pos()
    return [_remote(w_ref, land_ref.at[2 * px + py], send_sems, recv_sems, k, (px, py, c))
            for k, (px, py) in enumerate(_other_chips(x, y))]


def _scatter_sends(h_ref, land_ref, send_sems, recv_sems):
    x, y, c = _mesh_pos()
    return [_remote(h_ref.at[2 * px + py], land_ref.at[2 * x + y], send_sems, recv_sems, k, (px, py, c))
            for k, (px, py) in enumerate(_other_chips(x, y))]


def _scatter_arrivals(h_ref, land_ref, send_sems, recv_sems):
    x, y, c = _mesh_pos()
    return [_remote(h_ref.at[2 * x + y], land_ref.at[2 * px + py], send_sems, recv_sems, k, (px, py, c))
            for k, (px, py) in enumerate(_other_chips(x, y))]


def _exchange_wait_many(pendings, after, sends, arrivals, *, name):
    n = len(pendings)

    def body(*refs):
        for i in range(n):
            src_ref, land_ref, send_ref, recv_ref = refs[i], refs[n + i], refs[2 * n + i], refs[3 * n + i]
            for cp in sends(src_ref, land_ref, send_ref, recv_ref):
                cp.wait_send()
            for cp in arrivals(src_ref, land_ref, send_ref, recv_ref):
                cp.wait_recv()

    sources = [p[2] for p in pendings]
    landings = [p[3] for p in pendings]
    outs = pl.pallas_call(
        body, name=name,
        out_shape=tuple(pltpu.HBM(a.shape, a.dtype) for a in sources + landings),
        in_specs=[HBM_ONLY] * (2 * n) + [SEMAPHORES] * (2 * n) + [pl.BlockSpec(memory_space=pl.ANY)],
        out_specs=tuple([HBM_ONLY] * (2 * n)), input_output_aliases={i: i for i in range(2 * n)},
        compiler_params=pltpu.CompilerParams(has_side_effects=DATAFLOW_EFFECT),
    )(*sources, *landings, *[p[0] for p in pendings], *[p[1] for p in pendings], after)
    return [(outs[i], outs[n + i]) for i in range(n)]


def _sibling_sends(src_ref, land_ref, send_sems, recv_sems):
    x, y, c = _mesh_pos()
    return [_remote(src_ref, land_ref, send_sems, recv_sems, 0, (x, y, 1 - c))]


def _exchange_start(sources, landing_shapes, sends, *, after=None, name):
    n = len(sources)
    extra = [] if after is None else [after]

    def body(*refs):
        sems = refs[2 * n + len(extra):4 * n + len(extra)]
        for i in range(n):
            send_i = sends[i] if isinstance(sends, (list, tuple)) else sends
            for cp in send_i(refs[i], refs[n + i], sems[2 * i], sems[2 * i + 1]):
                cp.start()
        refs[-1][...] = jnp.zeros_like(refs[-1])

    hbm = [pltpu.HBM(s.shape, s.dtype) for s in sources] + [pltpu.HBM(shp, s.dtype)
                                                             for shp, s in zip(landing_shapes, sources)]
    outs = pl.pallas_call(
        body, name=name,
        out_shape=tuple([pltpu.SemaphoreType.DMA((N_PEER_CHIPS,))] * (2 * n) + hbm
                        + [jax.ShapeDtypeStruct((8, LANES), F32)]),
        in_specs=[HBM_ONLY] * (2 * n) + [pl.BlockSpec(memory_space=pl.ANY)] * len(extra),
        out_specs=tuple([SEMAPHORES] * (2 * n) + [HBM_ONLY] * (2 * n) + [pl.BlockSpec(memory_space=pltpu.VMEM)]),
        input_output_aliases={i: 2 * n + i for i in range(2 * n)},
        compiler_params=pltpu.CompilerParams(has_side_effects=DATAFLOW_EFFECT),
    )(*[pltpu.with_memory_space_constraint(s, pltpu.HBM) for s in sources],
      *[pltpu.with_memory_space_constraint(lax.empty(shp, s.dtype), pltpu.HBM)
        for shp, s in zip(landing_shapes, sources)], *extra)
    pending = [(outs[2 * i], outs[2 * i + 1], outs[2 * n + i], outs[3 * n + i]) for i in range(n)]
    return pending, outs[-1]


def _exchange_wait(pending, after, sends, arrivals, *, name):
    send_sems, recv_sems, source, landing = pending

    def body(src_ref, land_ref, send_ref, recv_ref, after_ref, src_out, land_out):
        for cp in sends(src_ref, land_ref, send_ref, recv_ref):
            cp.wait_send()
        for cp in arrivals(src_ref, land_ref, send_ref, recv_ref):
            cp.wait_recv()

    return pl.pallas_call(
        body, name=name,
        out_shape=(pltpu.HBM(source.shape, source.dtype), pltpu.HBM(landing.shape, landing.dtype)),
        in_specs=[HBM_ONLY, HBM_ONLY, SEMAPHORES, SEMAPHORES, pl.BlockSpec(memory_space=pl.ANY)],
        out_specs=(HBM_ONLY, HBM_ONLY), input_output_aliases={0: 0, 1: 1},
        compiler_params=pltpu.CompilerParams(has_side_effects=DATAFLOW_EFFECT),
    )(source, landing, send_sems, recv_sems, after)


def _gather_ici(shard, *, name):
    _, rh, cols = shard.shape

    def body(w_ref, o_ref, send_sems, recv_sems):
        x, y, c = _mesh_pos()
        mine = 2 * x + y
        sends = []
        for k, (px, py) in enumerate(_other_chips(x, y)):
            cp = _remote(w_ref.at[c], o_ref.at[mine, c], send_sems, recv_sems, k, (px, py, c))
            cp.start()
            sends.append(cp)
        for k, (px, py) in enumerate(_other_chips(x, y)):
            _remote(w_ref.at[c], o_ref.at[2 * px + py, c], send_sems, recv_sems, k, (px, py, c)).wait_recv()
        for cp in sends:
            cp.wait_send()

    return pl.pallas_call(
        body, name=name, in_specs=[HBM], out_specs=HBM,
        out_shape=jax.ShapeDtypeStruct((N_CHIPS, 2, rh, cols), shard.dtype), scratch_shapes=_dma_sems(3),
    )(shard)


def _gather_d2d(parts, *, name):
    def body(a_ref, o_ref, send_sems, recv_sems):
        x, y, c = _mesh_pos()
        sibling = (x, y, 1 - c)
        sends = []
        for k, (px, py) in enumerate(_other_chips(x, y)):
            cp = _remote(a_ref.at[2 * px + py, c], o_ref.at[2 * px + py, c], send_sems, recv_sems, k, sibling)
            cp.start()
            sends.append(cp)
        for k, (px, py) in enumerate(_other_chips(x, y)):
            _remote(a_ref.at[2 * px + py, c], o_ref.at[2 * px + py, 1 - c], send_sems, recv_sems, k, sibling).wait_recv()
        for cp in sends:
            cp.wait_send()

    return pl.pallas_call(
        body, name=name, in_specs=[HBM], out_specs=HBM,
        out_shape=jax.ShapeDtypeStruct(parts.shape, parts.dtype),
        input_output_aliases={0: 0}, scratch_shapes=_dma_sems(3),
    )(parts)


def _all_gather_chips(shard_flat, name):
    rows, cols = shard_flat.shape
    parts = _gather_ici(shard_flat.reshape(2, rows // 2, cols), name=name + "_ici")
    others = _gather_d2d(parts, name=name + "_d2d").reshape(N_CHIPS, rows, cols)
    chip = 2 * lax.axis_index("x") + lax.axis_index("y")
    return lax.dynamic_update_slice(others, shard_flat[None], (chip, 0, 0))


def _row_tile(rows, mult, cap):
    best = mult
    for t in range(mult, min(rows, cap) + 1, mult):
        if rows % t == 0:
            best = t
    assert rows % best == 0, (rows, mult)
    return best


def _swap_halves_d2d(g, *, after=None, name):
    _, _, rh, cols = g.shape
    extra = [] if after is None else [after]

    def body(g_ref, *rest):
        o_ref, send_sems, recv_sems = rest[len(extra):]
        x, y, c = _mesh_pos()
        sibling = (x, y, 1 - c)
        sends = []
        for s in range(N_CHIPS):
            cp = _remote(g_ref.at[s, 1 - c], o_ref.at[s], send_sems, recv_sems, s, sibling)
            cp.start()
            sends.append(cp)
        for s in range(N_CHIPS):
            _remote(g_ref.at[s, c], o_ref.at[s], send_sems, recv_sems, s, sibling).wait_recv()
        for cp in sends:
            cp.wait_send()

    return pl.pallas_call(
        body, name=name, in_specs=[HBM] * (1 + len(extra)), out_specs=HBM,
        out_shape=jax.ShapeDtypeStruct((N_CHIPS, rh, cols), g.dtype), scratch_shapes=_dma_sems(N_CHIPS),
    )(g, *extra)


def _add_own_half(g, arrived, core, *, name):
    _, _, rh, cols = g.shape
    mult = 16 if g.dtype == BF16 else 8
    tr = _row_tile(rh, mult, max(mult, (512 * 1024) // cols))

    def body(core_ref, g_ref, a_ref, o_ref):
        o_ref[...] = (g_ref[0].astype(F32) + a_ref[...].astype(F32)).astype(o_ref.dtype)

    grid_spec = pltpu.PrefetchScalarGridSpec(
        num_scalar_prefetch=1, grid=(N_CHIPS, rh // tr),
        in_specs=[pl.BlockSpec((1, 1, tr, cols), lambda s, i, core_ref: (s, core_ref[0], i, 0)),
                  pl.BlockSpec((1, tr, cols), lambda s, i, core_ref: (s, i, 0))],
        out_specs=pl.BlockSpec((1, tr, cols), lambda s, i, core_ref: (s, i, 0)))
    return pl.pallas_call(
        body, name=name, grid_spec=grid_spec, out_shape=jax.ShapeDtypeStruct((N_CHIPS, rh, cols), g.dtype),
        compiler_params=_params(("parallel", "parallel")),
    )(core, g, arrived)


def _scatter_ici(h, *, name):
    def body(h_ref, o_ref, send_sems, recv_sems):
        x, y, c = _mesh_pos()
        mine = 2 * x + y
        sends = []
        for k, (px, py) in enumerate(_other_chips(x, y)):
            cp = _remote(h_ref.at[2 * px + py], o_ref.at[mine], send_sems, recv_sems, k, (px, py, c))
            cp.start()
            sends.append(cp)
        for k, (px, py) in enumerate(_other_chips(x, y)):
            _remote(h_ref.at[mine], o_ref.at[2 * px + py], send_sems, recv_sems, k, (px, py, c)).wait_recv()
        for cp in sends:
            cp.wait_send()

    others = pl.pallas_call(
        body, name=name, in_specs=[HBM], out_specs=HBM, out_shape=jax.ShapeDtypeStruct(h.shape, h.dtype),
        scratch_shapes=_dma_sems(3),
    )(h)
    chip = 2 * lax.axis_index("x") + lax.axis_index("y")
    own = lax.dynamic_slice_in_dim(h, chip, 1, axis=0)
    return lax.dynamic_update_slice(others, own, (chip, 0, 0))


def _sum_chips(parts, *, name):
    _, rh, cols = parts.shape
    mult = 16 if parts.dtype == BF16 else 8
    tr = _row_tile(rh, mult, max(mult, (512 * 1024) // cols))

    def body(p_ref, o_ref):
        acc = p_ref[0].astype(F32)
        for s in range(1, N_CHIPS):
            acc = acc + p_ref[s].astype(F32)
        o_ref[...] = acc

    return pl.pallas_call(
        body, name=name, grid=(rh // tr,),
        in_specs=[pl.BlockSpec((N_CHIPS, tr, cols), lambda i: (0, i, 0))],
        out_specs=pl.BlockSpec((tr, cols), lambda i: (i, 0)),
        out_shape=jax.ShapeDtypeStruct((rh, cols), F32), compiler_params=_params(("parallel",)),
    )(parts)


def _sum_chips_with_own(landed, sent, chip, *, name):
    _, rh, cols = landed.shape
    mult = 16 if landed.dtype == BF16 else 8
    tr = _row_tile(rh, mult, max(mult, (512 * 1024) // cols))

    def body(chip_ref, own_ref, px_ref, py_ref, pxy_ref, o_ref):
        acc = own_ref[0].astype(F32)
        for p_ref in (px_ref, py_ref, pxy_ref):
            acc = acc + p_ref[0].astype(F32)
        o_ref[...] = acc

    def block_of(flip):
        return pl.BlockSpec((1, tr, cols), lambda i, chip_ref: (chip_ref[0] ^ flip, i, 0))

    grid_spec = pltpu.PrefetchScalarGridSpec(
        num_scalar_prefetch=1, grid=(rh // tr,),
        in_specs=[block_of(0), block_of(2), block_of(1), block_of(3)],
        out_specs=pl.BlockSpec((tr, cols), lambda i, chip_ref: (i, 0)))
    return pl.pallas_call(
        body, name=name, grid_spec=grid_spec, out_shape=jax.ShapeDtypeStruct((rh, cols), F32),
        compiler_params=_params(("parallel",)),
    )(chip, sent, landed, landed, landed)


def _share_d2d(f, *, name):
    fs = f if isinstance(f, (list, tuple)) else [f]
    others = _swap_with_sibling(fs, name=name)
    first = lax.axis_index("c") == 0
    both = [jnp.stack([jnp.where(first, a, b), jnp.where(first, b, a)]) for a, b in zip(fs, others)]
    return both if isinstance(f, (list, tuple)) else both[0]


def _swap_with_sibling(fs, *, name):
    n = len(fs)

    def body(*refs):
        x, y, c = _mesh_pos()
        sibling = (x, y, 1 - c)
        send_sems, recv_sems = refs[2 * n:]
        copies = [_remote(refs[i], refs[n + i], send_sems, recv_sems, i, sibling) for i in range(n)]
        for cp in copies:
            cp.start()
        for cp in copies:
            cp.wait()

    return pl.pallas_call(
        body, name=name, in_specs=[HBM] * n, out_specs=[HBM] * n,
        out_shape=[jax.ShapeDtypeStruct(a.shape, a.dtype) for a in fs], scratch_shapes=_dma_sems(n),
    )(*fs)


def _reduce_scatter_chips(g, core, name, after=None):
    _, rows, cols = g.shape
    g = g.reshape(N_CHIPS, 2, rows // 2, cols)
    arrived = _swap_halves_d2d(g, after=after, name=name + "_swap")
    chip_sum = _add_own_half(g, arrived, core, name=name + "_add2")
    parts = _scatter_ici(chip_sum, name=name + "_ici")
    total = _sum_chips(parts, name=name + "_sum4")
    return _share_d2d(total, name=name + "_share").reshape(rows, cols)


BIG = ("w_in", "w_branch", "w_out", "w_up", "w_down")
BIG_COLUMN_SHARDED = ("w_in", "w_up")
CONV = ("conv_a_w", "conv_f_w")
REPLICATED = ("norm1_w", "b_gate", "conv_a_b", "dt_bias", "a_log", "d_skip", "ssd_norm_w", "uv_b", "v_ln_w",
              "v_ln_b", "w_spatial", "b_spatial", "norm2_w", "conv_f_b", "final_norm_w")
WEIGHT_ORDER = ("norm1_w", "w_in", "b_gate", "conv_a_w", "conv_a_b", "dt_bias", "a_log", "d_skip", "ssd_norm_w",
                "uv_b", "v_ln_w", "v_ln_b", "w_spatial", "b_spatial", "w_branch", "w_out", "norm2_w", "w_up",
                "conv_f_w", "conv_f_b", "w_down", "final_norm_w")
SMALL_EXCHANGE_ROWS = 64


_GATE0 = SSD_IN + 2 * SGU_WIDTH
IN_SEGMENTS = {
    "in_z": (0, SSD_D_INNER), "in_xbc": (SSD_D_INNER, SSD_D_INNER + SSD_XBC), "in_dt": (SSD_D_INNER + SSD_XBC, SSD_IN),
    "in_uv": (SSD_IN, _GATE0), "in_gate": (_GATE0, IN_COLS), "in_gate_a": (_GATE0, _GATE0 + D_MODEL),
    "in_gate_b": (_GATE0 + D_MODEL, IN_COLS),
}
IN_GRAD_SEGMENTS = ("in_z", "in_xbc", "in_dt", "in_uv", "in_gate_a", "in_gate_b")


def _take_columns(parts, start, stop):
    out = []
    for a, first in parts:
        lo, hi = max(start, first), min(stop, first + a.shape[1])
        if lo < hi:
            out.append(a[:, lo - first:hi - first])
    return out[0] if len(out) == 1 else jnp.concatenate(out, axis=1)


def _flat_rows(arrays, row_multiple):
    flat = jnp.concatenate([a.reshape(-1) for a in arrays])
    rows = -(-flat.shape[0] // (LANES * row_multiple)) * row_multiple
    return jnp.pad(flat, (0, rows * LANES - flat.shape[0])).reshape(rows, LANES)


def _unflatten(flat, shapes):
    flat = flat.reshape(-1)
    out, off = [], 0
    for shp in shapes:
        n = math.prod(shp)
        out.append(flat[off:off + n].reshape(shp))
        off += n
    return out


def _from_chip_blocks(blocks, name):
    if name in BIG_COLUMN_SHARDED or name in CONV:
        k = blocks.shape[1]
        return jnp.transpose(blocks, (1, 0, 2)).reshape(k, -1)
    return blocks.reshape(-1, blocks.shape[-1])


def _to_chip_blocks(whole, name):
    if name in BIG_COLUMN_SHARDED or name in CONV:
        k, n = whole.shape
        return jnp.transpose(whole.reshape(k, N_CHIPS, n // N_CHIPS), (1, 0, 2))
    return whole.reshape(N_CHIPS, whole.shape[0] // N_CHIPS, whole.shape[1])


def kernel(x, norm1_w, w_in, b_gate, conv_a_w, conv_a_b, dt_bias, a_log, d_skip, ssd_norm_w, uv_b, v_ln_w, v_ln_b, w_spatial, b_spatial, w_branch, w_out, norm2_w, w_up, conv_f_w, conv_f_b, w_down, final_norm_w, loss_target, m_norm1_w, m_w_in, m_b_gate, m_conv_a_w, m_conv_a_b, m_dt_bias, m_a_log, m_d_skip, m_ssd_norm_w, m_uv_b, m_v_ln_w, m_v_ln_b, m_w_spatial, m_b_spatial, m_w_branch, m_w_out, m_norm2_w, m_w_up, m_conv_f_w, m_conv_f_b, m_w_down, m_final_norm_w, v_norm1_w, v_w_in, v_b_gate, v_conv_a_w, v_conv_a_b, v_dt_bias, v_a_log, v_d_skip, v_ssd_norm_w, v_uv_b, v_v_ln_w, v_v_ln_b, v_w_spatial, v_b_spatial, v_w_branch, v_w_out, v_norm2_w, v_w_up, v_conv_f_w, v_conv_f_b, v_w_down, v_final_norm_w):
    weights = dict(norm1_w=norm1_w, w_in=w_in, b_gate=b_gate, conv_a_w=conv_a_w, conv_a_b=conv_a_b, dt_bias=dt_bias,
                   a_log=a_log, d_skip=d_skip, ssd_norm_w=ssd_norm_w, uv_b=uv_b, v_ln_w=v_ln_w, v_ln_b=v_ln_b,
                   w_spatial=w_spatial, b_spatial=b_spatial, w_branch=w_branch, w_out=w_out, norm2_w=norm2_w,
                   w_up=w_up, conv_f_w=conv_f_w, conv_f_b=conv_f_b, w_down=w_down, final_norm_w=final_norm_w)
    mom1 = dict(norm1_w=m_norm1_w, w_in=m_w_in, b_gate=m_b_gate, conv_a_w=m_conv_a_w, conv_a_b=m_conv_a_b,
                dt_bias=m_dt_bias, a_log=m_a_log, d_skip=m_d_skip, ssd_norm_w=m_ssd_norm_w, uv_b=m_uv_b,
                v_ln_w=m_v_ln_w, v_ln_b=m_v_ln_b, w_spatial=m_w_spatial, b_spatial=m_b_spatial, w_branch=m_w_branch,
                w_out=m_w_out, norm2_w=m_norm2_w, w_up=m_w_up, conv_f_w=m_conv_f_w, conv_f_b=m_conv_f_b,
                w_down=m_w_down, final_norm_w=m_final_norm_w)
    mom2 = dict(norm1_w=v_norm1_w, w_in=v_w_in, b_gate=v_b_gate, conv_a_w=v_conv_a_w, conv_a_b=v_conv_a_b,
                dt_bias=v_dt_bias, a_log=v_a_log, d_skip=v_d_skip, ssd_norm_w=v_ssd_norm_w, uv_b=v_uv_b,
                v_ln_w=v_v_ln_w, v_ln_b=v_v_ln_b, w_spatial=v_w_spatial, b_spatial=v_b_spatial, w_branch=v_w_branch,
                w_out=v_w_out, norm2_w=v_norm2_w, w_up=v_w_up, conv_f_w=v_conv_f_w, conv_f_b=v_conv_f_b,
                w_down=v_w_down, final_norm_w=v_final_norm_w)
    chip = 2 * lax.axis_index("x") + lax.axis_index("y")
    core = lax.axis_index("c").astype(jnp.int32).reshape(1)

    whole = {}
    conv_shapes = [weights[n].shape[1:] for n in CONV]
    conv_gathered = _all_gather_chips(_flat_rows([weights[n] for n in CONV], 16), "gather_conv").reshape(N_CHIPS, -1)
    off = 0
    for n, shp in zip(CONV, conv_shapes):
        size = math.prod(shp)
        whole[n] = _from_chip_blocks(conv_gathered[:, off:off + size].reshape((N_CHIPS,) + shp), n)
        off += size
    shard_shapes = {n: weights[n].shape[1:] for n in BIG}
    halves = [weights[n][0].astype(BF16).reshape(2, shard_shapes[n][0] // 2, shard_shapes[n][1]) for n in BIG]
    sends = [_gather_sends if n == "w_in" else _gather_whole_sends for n in BIG]
    gathers, gathers_started = _exchange_start(halves, [(N_CHIPS,) + h.shape for h in halves], sends,
                                               after=conv_gathered, name="gather_start")
    gathers = dict(zip(BIG, gathers))

    def get_weight(name, after):
        rows, cols = shard_shapes[name]
        if name == "w_in":
            own, landed = _exchange_wait(gathers[name], after, _gather_sends, _gather_arrivals,
                                         name="gather_" + name + "_wait")
            landed = _gather_d2d(landed, name="gather_" + name + "_d2d")
        else:
            own, landed = _exchange_wait(gathers[name], after, _gather_whole_sends, _gather_whole_arrivals,
                                         name="gather_" + name + "_wait")
        blocks = lax.dynamic_update_slice(landed.reshape(N_CHIPS, rows, cols), own.reshape(1, rows, cols),
                                          (chip, 0, 0))
        if name == "w_up":
            return {"up": blocks}
        if name == "w_in":
            parts = [(blocks[k], cols * k) for k in range(N_CHIPS)]
            segs = {n: _take_columns(parts, a, b) for n, (a, b) in IN_SEGMENTS.items()}
            segs["in_dt"] = jnp.pad(segs["in_dt"], ((0, 0), (0, LANES - SSD_HEADS)))
            return segs
        full = _from_chip_blocks(blocks, name)
        if name == "w_branch":
            return {"branch_a": full[:SSD_D_INNER], "branch_b": full[SSD_D_INNER:]}
        return {name[2:]: full}

    small = {n: weights[n] for n in REPLICATED}
    small["conv_a_w"] = whole["conv_a_w"]
    small["conv_f_w"] = whole["conv_f_w"]
    small["gathers_started"] = gathers_started

    reductions = {}

    def emit_grad(name, g):
        if name == "w_in":
            parts = [(g[n], IN_SEGMENTS[n][0]) for n in IN_GRAD_SEGMENTS]
            cols = shard_shapes[name][1]
            g_blocks = jnp.stack([_take_columns(parts, cols * k, cols * (k + 1)) for k in range(N_CHIPS)])
        else:
            g_blocks = g if name == "w_up" else _to_chip_blocks(g, name)
        if name == "w_in":
            _, rows, cols = g_blocks.shape
            g_halves = g_blocks.reshape(N_CHIPS, 2, rows // 2, cols)
            arrived = _swap_halves_d2d(g_halves, name="reduce_" + name + "_swap")
            g_blocks = _add_own_half(g_halves, arrived, core, name="reduce_" + name + "_add2")
        (pending,), started = _exchange_start([g_blocks], [g_blocks.shape], _scatter_sends,
                                              name="reduce_" + name + "_start")
        reductions[name] = pending
        return started

    loss, dx, grads_small = _local_step(x[0], loss_target[0], get_weight, small, emit_grad)

    order = ("w_down", "w_up", "w_out", "w_branch", "w_in")
    core_sums = []
    chip_index = chip.astype(jnp.int32).reshape(1)
    for n in order:
        sent, landed = _exchange_wait(reductions[n], dx, _scatter_sends, _scatter_arrivals,
                                      name="reduce_" + n + "_wait")
        core_sums.append(_sum_chips_with_own(landed, sent, chip_index, name="reduce_" + n + "_sum4"))
    swaps, swaps_started = _exchange_start(core_sums, [a.shape for a in core_sums], _sibling_sends, name="reduce_swap_start")
    grads = {}

    small_names = REPLICATED + CONV + ("loss",)
    grads_small = dict(grads_small, loss=loss)
    small_shapes = [grads_small[n].shape for n in small_names]
    g_small = _flat_rows([grads_small[n] for n in small_names], N_CHIPS * 2 * SMALL_EXCHANGE_ROWS)
    red_small = _reduce_scatter_chips(g_small.reshape(N_CHIPS, -1, LANES), core, "reduce_small", after=swaps_started)
    all_small = _all_gather_chips(red_small, "gather_small")
    swapped = _exchange_wait_many(swaps, all_small, _sibling_sends, _sibling_sends, name="reduce_swap_wait")
    core_sums = {n: own for n, (own, _) in zip(order, swapped)}
    sibling_sums = {n: other for n, (_, other) in zip(order, swapped)}
    first = lax.axis_index("c") == 0
    w_in_halves = (core_sums["w_in"], sibling_sums["w_in"])
    w_in_grad = jnp.concatenate([jnp.where(first, w_in_halves[0], w_in_halves[1]),
                                 jnp.where(first, w_in_halves[1], w_in_halves[0])], axis=0)
    for n, g in zip(small_names, _unflatten(all_small, small_shapes)):
        if n == "loss":
            total_loss = g[0, 0]
            continue
        if n in CONV:
            width = g.shape[1] // N_CHIPS
            g = lax.dynamic_slice_in_dim(g, chip * width, width, axis=1)
        grads[n] = g.reshape(weights[n].shape[1:]) if n != "final_norm_w" else g

    delta, new_m, new_v = {}, {}, {}
    for n in BIG:
        shp = weights[n].shape
        if n == "w_in":
            g_t = w_in_grad.T
            results = [g_t] + list(_adamw(weights[n][0].T, g_t, mom1[n][0].T, mom2[n][0].T, name="adamw_" + n,
                                          tr=_row_tile(g_t.shape[0], 8, 136)))
            results = [a.T for a in results]
        else:
            results = _adamw_two_sums(weights[n][0], core_sums[n], sibling_sums[n], mom1[n][0], mom2[n][0],
                                      name="adamw_" + n, tr=_row_tile(shp[1], 8, 352))
        grads[n], delta[n], new_m[n], new_v[n] = [a.reshape(shp) for a in results]
    small_all = [n for n in WEIGHT_ORDER if n not in BIG]

    def as_2d(a):
        return a.reshape(-1, a.shape[-1])

    results = _adamw_many(*[[as_2d(src[n]) for n in small_all] for src in (weights, grads, mom1, mom2)],
                          name="adamw_small")
    for n, dv, mv, vv in zip(small_all, *results):
        shp = weights[n].shape
        delta[n], new_m[n], new_v[n] = dv.reshape(shp), mv.reshape(shp), vv.reshape(shp)

    grad_out = [grads[n].reshape(weights[n].shape) for n in WEIGHT_ORDER]
    return (total_loss, dx[None], *grad_out, *[delta[n] for n in WEIGHT_ORDER], *[new_m[n] for n in WEIGHT_ORDER],
            *[new_v[n] for n in WEIGHT_ORDER])
```

```python
import functools
import math

import jax
import jax.numpy as jnp
from jax import lax
from jax.experimental import pallas as pl
from jax.experimental.pallas import tpu as pltpu

F32 = jnp.float32
BF16 = jnp.bfloat16

D_MODEL = 1024
SSD_D_INNER = 2048
SSD_HEADS = 32
SSD_HEAD_DIM = 64
SSD_GROUPS = 4
SSD_HEADS_PER_GROUP = 8
SSD_STATE = 128
SSD_BC = 512
SSD_XBC = 3072
SSD_IN = 5152
SGU_WIDTH = 1024
SGU_GROUPS = 8
CHUNK = 128
IN_COLS = 9248
D_FF = 2816
NORM_EPS = 1e-6
LN_EPS = 1e-5
GROUP_COLS = SSD_HEADS_PER_GROUP * SSD_HEAD_DIM
LANES = 128

ADAM_LR = 0.001
ADAM_B1 = 0.9
ADAM_B2 = 0.999
ADAM_EPS = 1e-08
ADAM_WD = 0.01
ADAM_STEP = 10

N_CHIPS = 4
VMEM_LIMIT = 56 * 1024 * 1024

NT = (((1,), (1,)), ((), ()))
TN = (((0,), (0,)), ((), ()))
NN = (((1,), (0,)), ((), ()))


def _params(dims):
    return pltpu.CompilerParams(dimension_semantics=dims, vmem_limit_bytes=VMEM_LIMIT)


def _dot(a, b, dn=NN, precision=None):
    return lax.dot_general(a, b, dn, precision=precision, preferred_element_type=F32)


def _split3(x):
    hi = x.astype(BF16)
    rest = x - hi.astype(F32)
    mid = rest.astype(BF16)
    return hi, mid, (rest - mid.astype(F32)).astype(BF16)


def _dot_terms(terms, exact, dn=NN):
    out = None
    for t in terms:
        p = _dot(t, exact, dn)
        out = p if out is None else out + p
    return out


def _dot_exact_lhs(exact, terms):
    out = None
    for t in terms:
        p = _dot(exact, t)
        out = p if out is None else out + p
    return out


def _sigmoid(x):
    return 1.0 / (1.0 + jnp.exp(-x))


def _softplus(x):
    return jnp.maximum(x, 0.0) + jnp.log(1.0 + jnp.exp(-jnp.abs(x)))


def _matmul(pairs, *, trans_b=False, add=None, after=None, out_dtype=F32, tm=512, tn=512, name):
    def mat_shape(b):
        if isinstance(b, tuple) and b[1] == "cols":
            return (b[0].shape[1], b[0].shape[0] * b[0].shape[2])
        return b[0].shape[1:] if isinstance(b, tuple) else b.shape

    if isinstance(pairs[0][1], tuple) and pairs[0][1][1] == "cols":
        assert not trans_b and tn % LANES == 0 and pairs[0][1][0].shape[2] % tn == 0, name

    m = (pairs[0][0][0] if isinstance(pairs[0][0], tuple) else pairs[0][0]).shape[0]
    n = mat_shape(pairs[0][1])[0] if trans_b else mat_shape(pairs[0][1])[1]
    tm, tn = min(tm, m), min(tn, n)
    assert m % tm == 0 and n % tn == 0, (name, m, n, tm, tn)
    npairs = len(pairs)
    dn = NT if trans_b else NN

    def body(*refs):
        o_ref = refs[-1]
        acc = None
        for i in range(npairs):
            p = _dot(refs[2 * i][...].astype(BF16), refs[2 * i + 1][...].astype(BF16), dn)
            acc = p if acc is None else acc + p
        if add is not None:
            acc = acc + refs[2 * npairs][...]
        o_ref[...] = acc.astype(out_dtype)

    in_specs, args = [], []
    for a, b in pairs:
        bshape = mat_shape(b)
        k = bshape[1] if trans_b else bshape[0]
        assert bshape == ((n, k) if trans_b else (k, n)), (name, bshape)
        a, qa = a if isinstance(a, tuple) else (a, 0)
        assert a.shape[0] == m and a.shape[1] % k == 0, (name, a.shape, k)
        in_specs.append(pl.BlockSpec((tm, k), lambda i, j, qa=qa: (i, qa)))
        if isinstance(b, tuple) and b[1] == "cols":
            b = b[0]
            per = b.shape[2] // tn
            in_specs.append(pl.BlockSpec((None, k, tn), lambda i, j, per=per: (j // per, 0, j % per)))
        elif isinstance(b, tuple):
            b, qb = b
            if trans_b:
                in_specs.append(pl.BlockSpec((None, tn, k), lambda i, j, qb=qb: (qb, j, 0)))
            else:
                in_specs.append(pl.BlockSpec((None, k, tn), lambda i, j, qb=qb: (qb, 0, j)))
        elif trans_b:
            in_specs.append(pl.BlockSpec((tn, k), lambda i, j: (j, 0)))
        else:
            in_specs.append(pl.BlockSpec((k, tn), lambda i, j: (0, j)))
        args += [a, b]
    if add is not None:
        in_specs.append(pl.BlockSpec((tm, tn), lambda i, j: (i, j)))
        args.append(add)
    if after is not None:
        in_specs.append(pl.BlockSpec(memory_space=pl.ANY))
        args.append(after)
    return pl.pallas_call(
        body, name=name, grid=(m // tm, n // tn), in_specs=in_specs,
        out_specs=pl.BlockSpec((tm, tn), lambda i, j: (i, j)),
        out_shape=jax.ShapeDtypeStruct((m, n), out_dtype),
        compiler_params=_params(("parallel", "parallel")),
    )(*args)


def _matmul_tn(a, b, *, tk, tn, tm=1024, out_dtype=BF16, stack_out=False, after=None, part_of=None, name):
    m, k = a.shape
    n = b.shape[1]
    tm, tk, tn = min(tm, m), min(tk, k), min(tn, n)
    assert m % tm == 0 and k % tk == 0 and n % tn == 0, (name, m, k, n)
    nm = m // tm
    blocks, first, buffer = part_of if part_of is not None else (None, 0, None)
    if stack_out:
        out_spec = pl.BlockSpec((None, tk, tn), lambda i, j, l: (j + first, i, 0))
        out_shape = jax.ShapeDtypeStruct((blocks or n // tn, k, tn), out_dtype)
    else:
        out_spec = pl.BlockSpec((tk, tn), lambda i, j, l: (i + first, j))
        out_shape = jax.ShapeDtypeStruct((blocks * tk if blocks else k, n), out_dtype)

    def body(a_ref, b_ref, *rest):
        o_ref, acc = rest[-2:]
        mi = pl.program_id(2)

        @pl.when(mi == 0)
        def _():
            acc[...] = jnp.zeros_like(acc)

        acc[...] += _dot(a_ref[...].astype(BF16), b_ref[...].astype(BF16), TN)

        @pl.when(mi == nm - 1)
        def _():
            o_ref[...] = acc[...].astype(out_dtype)

    in_specs = [pl.BlockSpec((tm, tk), lambda i, j, l: (l, i)), pl.BlockSpec((tm, tn), lambda i, j, l: (l, j))]
    args = [a, b]
    if after is not None:
        in_specs.append(pl.BlockSpec(memory_space=pl.ANY))
        args.append(after)
    aliases = {}
    if buffer is not None:
        aliases = {len(args): 0}
        in_specs.append(pl.BlockSpec(memory_space=pl.ANY))
        args.append(buffer)
    return pl.pallas_call(
        body, name=name, grid=(k // tk, n // tn, nm), in_specs=in_specs,
        out_specs=out_spec, out_shape=out_shape, input_output_aliases=aliases,
        scratch_shapes=[pltpu.VMEM((tk, tn), F32)],
        compiler_params=_params(("parallel", "parallel", "arbitrary")),
    )(*args)


def _rms_fwd(x, w, *, after=None, name, tm=512):
    s, d = x.shape
    tm = min(tm, s)
    extra = [] if after is None else [after]

    def body(x_ref, w_ref, *rest):
        o_ref = rest[-1]
        xv = x_ref[...]
        r = lax.rsqrt(jnp.mean(xv * xv, axis=-1, keepdims=True) + NORM_EPS)
        o_ref[...] = (xv * r * w_ref[...]).astype(BF16)

    return pl.pallas_call(
        body, name=name, grid=(s // tm,),
        in_specs=[pl.BlockSpec((tm, d), lambda i: (i, 0)), pl.BlockSpec((1, d), lambda i: (0, 0))]
        + [pl.BlockSpec(memory_space=pl.ANY)] * len(extra),
        out_specs=pl.BlockSpec((tm, d), lambda i: (i, 0)),
        out_shape=jax.ShapeDtypeStruct((s, d), BF16),
        compiler_params=_params(("parallel",)),
    )(x, w, *extra)


def _rms_bwd(x, w, dn, dres, *, name, tm=512):
    s, d = x.shape
    tm = min(tm, s)

    def body(x_ref, w_ref, dn_ref, dres_ref, dx_ref, dxb_ref, dw_ref):
        @pl.when(pl.program_id(0) == 0)
        def _():
            dw_ref[...] = jnp.zeros_like(dw_ref)

        xv = x_ref[...]
        r = lax.rsqrt(jnp.mean(xv * xv, axis=-1, keepdims=True) + NORM_EPS)
        xhat = xv * r
        dnv = dn_ref[...].astype(F32)
        dxhat = dnv * w_ref[...]
        dx = dres_ref[...] + r * (dxhat - xhat * jnp.mean(dxhat * xhat, axis=-1, keepdims=True))
        dx_ref[...] = dx
        dxb_ref[...] = dx.astype(BF16)
        dw_ref[...] += jnp.sum(dnv * xhat, axis=0, keepdims=True)

    tile = pl.BlockSpec((tm, d), lambda i: (i, 0))
    row = pl.BlockSpec((1, d), lambda i: (0, 0))
    return pl.pallas_call(
        body, name=name, grid=(s // tm,),
        in_specs=[tile, row, tile, tile], out_specs=[tile, tile, row],
        out_shape=[jax.ShapeDtypeStruct((s, d), F32), jax.ShapeDtypeStruct((s, d), BF16),
                   jax.ShapeDtypeStruct((1, d), F32)],
        compiler_params=_params(("arbitrary",)),
    )(x, w, dn, dres)


def _final_fwd_bwd(h2, wf, target, *, name, tm=512):
    s, d = h2.shape
    tm = min(tm, s)

    def body(h_ref, w_ref, t_ref, loss_ref, dh_ref, dhb_ref, dw_ref):
        @pl.when(pl.program_id(0) == 0)
        def _():
            dw_ref[...] = jnp.zeros_like(dw_ref)
            loss_ref[...] = jnp.zeros_like(loss_ref)

        hv = h_ref[...]
        r = lax.rsqrt(jnp.mean(hv * hv, axis=-1, keepdims=True) + NORM_EPS)
        xhat = hv * r
        err = xhat * w_ref[...] - t_ref[...]
        per_tok = jnp.mean(err * err, axis=-1, keepdims=True)
        loss_ref[...] += 0.5 * jnp.sum(per_tok, axis=0, keepdims=True)
        dy = err * (1.0 / d)
        dxhat = dy * w_ref[...]
        dh = r * (dxhat - xhat * jnp.mean(dxhat * xhat, axis=-1, keepdims=True))
        dh_ref[...] = dh
        dhb_ref[...] = dh.astype(BF16)
        dw_ref[...] += jnp.sum(dy * xhat, axis=0, keepdims=True)

    tile = pl.BlockSpec((tm, d), lambda i: (i, 0))
    row = pl.BlockSpec((1, d), lambda i: (0, 0))
    return pl.pallas_call(
        body, name=name, grid=(s // tm,),
        in_specs=[tile, row, tile],
        out_specs=[pl.BlockSpec((1, 1), lambda i: (0, 0)), tile, tile, row],
        out_shape=[jax.ShapeDtypeStruct((1, 1), F32), jax.ShapeDtypeStruct((s, d), F32),
                   jax.ShapeDtypeStruct((s, d), BF16), jax.ShapeDtypeStruct((1, d), F32)],
        compiler_params=_params(("arbitrary",)),
    )(h2, wf, target)


CONV_ROWS = 256
CONV_ROWS_FWD = 512
HALO = 8


def _rows_with_halo(ref, r0, rows, s, before, after):
    tile = 16 if ref.dtype == BF16 else HALO
    parts = []
    if before:
        prev = ref[pl.ds(pl.multiple_of(jnp.maximum(r0 - tile, 0), tile), tile), :].astype(F32)[tile - HALO:]
        parts.append(jnp.where(r0 > 0, prev, 0.0))
    parts.append(ref[pl.ds(r0, rows), :].astype(F32))
    if after:
        nxt = ref[pl.ds(pl.multiple_of(jnp.minimum(r0 + rows, s - tile), tile), tile), :].astype(F32)[:HALO]
        parts.append(jnp.where(r0 + rows < s, nxt, 0.0))
    return jnp.concatenate(parts, axis=0) if len(parts) > 1 else parts[0]


def _window(x_ref, r0, s, after):
    return _rows_with_halo(x_ref, r0, CONV_ROWS_FWD, s, True, after).astype(F32)


def _shifted(window, k, rows):
    if k == 0:
        return window[HALO:HALO + rows]
    return pltpu.roll(window, k, 0)[HALO:HALO + rows]


def _conv_taps(window, w_ref, kk, rows):
    acc = None
    for i in range(kk):
        term = w_ref[i:i + 1, :] * _shifted(window, kk - 1 - i, rows)
        acc = term if acc is None else acc + term
    return acc


def _row_loop(rows, step):
    def body(r, carry):
        return step(pl.multiple_of(r * rows, rows), carry)
    return body


def _conv_bwd_rows(x, dpe, w_ref, kk):
    dp = dpe[:CONV_ROWS]
    dx = None
    dws = []
    for i in range(kk):
        k = kk - 1 - i
        later = dp if k == 0 else pltpu.roll(dpe, dpe.shape[0] - k, 0)[:CONV_ROWS]
        dws.append(jnp.sum(later * x, axis=0, keepdims=True))
        term = w_ref[i:i + 1, :] * later
        dx = term if dx is None else dx + term
    return dx, dws, jnp.sum(dp, axis=0, keepdims=True)


def _conv_a_fwd(xraw, w, b, *, name, tc=128):
    s, c = xraw.shape
    kk = 4

    def body(x_ref, w_ref, b_ref, o_ref, pre_ref):
        def step(r0, carry):
            pre = _conv_taps(_window(x_ref, r0, s, False), w_ref, kk, CONV_ROWS_FWD) + b_ref[...]
            o_ref[pl.ds(r0, CONV_ROWS_FWD), :] = pre * _sigmoid(pre)
            pre_ref[pl.ds(r0, CONV_ROWS_FWD), :] = pre.astype(BF16)
            return carry

        lax.fori_loop(0, s // CONV_ROWS_FWD, _row_loop(CONV_ROWS_FWD, step), 0)

    col = pl.BlockSpec((s, tc), lambda j: (0, j))
    return pl.pallas_call(
        body, name=name, grid=(c // tc,),
        in_specs=[col, pl.BlockSpec((8, tc), lambda j: (0, j)), pl.BlockSpec((1, tc), lambda j: (0, j))],
        out_specs=[col, col], out_shape=[jax.ShapeDtypeStruct((s, c), F32), jax.ShapeDtypeStruct((s, c), BF16)],
        compiler_params=_params(("parallel",)),
    )(xraw, w, b)


def _conv_a_bwd(xraw, pre, w, dy, *, name, tc=128):
    s, c = xraw.shape
    kk = 4

    def body(x_ref, pre_ref, w_ref, dy_ref, dx_ref, dw_ref, db_ref):
        def step(r0, carry):
            pre = _rows_with_halo(pre_ref, r0, CONV_ROWS, s, False, True)
            sg = _sigmoid(pre)
            dpe = _rows_with_halo(dy_ref, r0, CONV_ROWS, s, False, True) * (sg * (1.0 + pre * (1.0 - sg)))
            dx, dws, db = _conv_bwd_rows(x_ref[pl.ds(r0, CONV_ROWS), :].astype(F32), dpe, w_ref, kk)
            dx_ref[pl.ds(r0, CONV_ROWS), :] = dx.astype(BF16)
            return tuple(acc + new for acc, new in zip(carry, dws + [db]))

        zero = jnp.zeros((1, tc), F32)
        sums = lax.fori_loop(0, s // CONV_ROWS, _row_loop(CONV_ROWS, step), (zero,) * (kk + 1))
        db_ref[...] = sums[kk]
        dw_ref[...] = jnp.concatenate(list(sums[:kk]) + [jnp.zeros((8 - kk, tc), F32)], axis=0)

    col = pl.BlockSpec((s, tc), lambda j: (0, j))
    w8 = pl.BlockSpec((8, tc), lambda j: (0, j))
    row = pl.BlockSpec((1, tc), lambda j: (0, j))
    return pl.pallas_call(
        body, name=name, grid=(c // tc,),
        in_specs=[col, col, w8, col], out_specs=[col, w8, row],
        out_shape=[jax.ShapeDtypeStruct((s, c), BF16), jax.ShapeDtypeStruct((8, c), F32),
                   jax.ShapeDtypeStruct((1, c), F32)],
        compiler_params=_params(("parallel",)),
    )(xraw, pre, w, dy)


def _conv_f_fwd(up_raw, w, b, *, name, tc=128):
    s, c2 = up_raw.shape
    c = c2 // 2
    nb = c // tc
    kk = 3

    def body(xa_ref, xv_ref, wa_ref, wv_ref, ba_ref, bv_ref, o_ref, a_out, v_out):
        def step(r0, carry):
            a = _conv_taps(_window(xa_ref, r0, s, False), wa_ref, kk, CONV_ROWS_FWD) + ba_ref[...]
            v = _conv_taps(_window(xv_ref, r0, s, False), wv_ref, kk, CONV_ROWS_FWD) + bv_ref[...]
            o_ref[pl.ds(r0, CONV_ROWS_FWD), :] = (a * _sigmoid(a) * v).astype(BF16)
            a_out[pl.ds(r0, CONV_ROWS_FWD), :] = a.astype(BF16)
            v_out[pl.ds(r0, CONV_ROWS_FWD), :] = v.astype(BF16)
            return carry

        lax.fori_loop(0, s // CONV_ROWS_FWD, _row_loop(CONV_ROWS_FWD, step), 0)

    col_a = pl.BlockSpec((s, tc), lambda j: (0, j))
    col_v = pl.BlockSpec((s, tc), lambda j: (0, j + nb))
    half = jax.ShapeDtypeStruct((s, c), BF16)
    return pl.pallas_call(
        body, name=name, grid=(nb,),
        in_specs=[col_a, col_v, pl.BlockSpec((8, tc), lambda j: (0, j)), pl.BlockSpec((8, tc), lambda j: (0, j + nb)),
                  pl.BlockSpec((1, tc), lambda j: (0, j)), pl.BlockSpec((1, tc), lambda j: (0, j + nb))],
        out_specs=[col_a, col_a, col_a], out_shape=[half, half, half],
        compiler_params=_params(("parallel",)),
    )(up_raw, up_raw, w, w, b, b)


def _conv_f_bwd(up_raw, a_pre, v_pre, w, dact, *, name, tc=128):
    s, c2 = up_raw.shape
    c = c2 // 2
    nb = c // tc
    kk = 3

    def body(xa_ref, xv_ref, a_ref, v_ref, wa_ref, wv_ref, d_ref,
             dxa_ref, dxv_ref, dwa_ref, dwv_ref, dba_ref, dbv_ref):
        def step(r0, carry):
            a = _rows_with_halo(a_ref, r0, CONV_ROWS, s, False, True)
            v = _rows_with_halo(v_ref, r0, CONV_ROWS, s, False, True)
            sg = _sigmoid(a)
            d = _rows_with_halo(d_ref, r0, CONV_ROWS, s, False, True)
            rows = pl.ds(r0, CONV_ROWS)
            dxa, dwas, dba = _conv_bwd_rows(xa_ref[rows, :].astype(F32), d * v * (sg * (1.0 + a * (1.0 - sg))),
                                            wa_ref, kk)
            dxv, dwvs, dbv = _conv_bwd_rows(xv_ref[rows, :].astype(F32), d * (a * sg), wv_ref, kk)
            dxa_ref[pl.ds(r0, CONV_ROWS), :] = dxa.astype(BF16)
            dxv_ref[pl.ds(r0, CONV_ROWS), :] = dxv.astype(BF16)
            return tuple(acc + new for acc, new in zip(carry, dwas + [dba] + dwvs + [dbv]))

        zero = jnp.zeros((1, tc), F32)
        sums = lax.fori_loop(0, s // CONV_ROWS, _row_loop(CONV_ROWS, step), (zero,) * (2 * kk + 2))
        pad = [jnp.zeros((8 - kk, tc), F32)]
        dwa_ref[...] = jnp.concatenate(list(sums[:kk]) + pad, axis=0)
        dba_ref[...] = sums[kk]
        dwv_ref[...] = jnp.concatenate(list(sums[kk + 1:2 * kk + 1]) + pad, axis=0)
        dbv_ref[...] = sums[2 * kk + 1]

    col_a = pl.BlockSpec((s, tc), lambda j: (0, j))
    col_v = pl.BlockSpec((s, tc), lambda j: (0, j + nb))
    w_a = pl.BlockSpec((8, tc), lambda j: (0, j))
    w_v = pl.BlockSpec((8, tc), lambda j: (0, j + nb))
    r_a = pl.BlockSpec((1, tc), lambda j: (0, j))
    r_v = pl.BlockSpec((1, tc), lambda j: (0, j + nb))
    outs = pl.pallas_call(
        body, name=name, grid=(nb,),
        in_specs=[col_a, col_v, col_a, col_a, w_a, w_v, col_a],
        out_specs=[col_a, col_a, w_a, w_a, r_a, r_a],
        out_shape=[jax.ShapeDtypeStruct((s, c), BF16), jax.ShapeDtypeStruct((s, c), BF16),
                   jax.ShapeDtypeStruct((8, c), F32), jax.ShapeDtypeStruct((8, c), F32),
                   jax.ShapeDtypeStruct((1, c), F32), jax.ShapeDtypeStruct((1, c), F32)],
        compiler_params=_params(("parallel",)),
    )(up_raw, up_raw, a_pre, v_pre, w, w, dact)
    return outs


def _tri_masks():
    row = lax.broadcasted_iota(jnp.int32, (CHUNK, CHUNK), 0)
    col = lax.broadcasted_iota(jnp.int32, (CHUNK, CHUNK), 1)
    return row >= col, row <= col


def _ssd_fwd(xbc, dt_raw, z, dt_bias, a_log, a_log_x, d_skip_x, norm_w, expand, *, name):
    s = xbc.shape[0]
    nc = s // CHUNK

    def body(xbc_ref, dtr_ref, z_ref, dtb_ref, alog_ref, alogx_ref, dskx_ref, nw_ref, e_ref,
             y_ref, ya_ref, st_ref, state):
        @pl.when(pl.program_id(0) == 0)
        def _():
            state[...] = jnp.zeros_like(state)

        st_ref[0] = state[...]
        lower, _ = _tri_masks()
        dt = _softplus(dtr_ref[...] + dtb_ref[...])
        adt = dt * (-jnp.exp(alog_ref[...]))
        acum = _dot_exact_lhs(lower.astype(BF16), _split3(adt))
        acum_t = acum.T
        dt_terms, acum_terms = _split3(dt), _split3(acum)
        for g in range(SSD_GROUPS):
            sl = slice(GROUP_COLS * g, GROUP_COLS * (g + 1))
            dt_x = _dot_terms(dt_terms[:2], e_ref[:, sl])
            acum_x = _dot_terms(acum_terms, e_ref[:, sl])
            tot_x = jnp.sum(dt_x * (-jnp.exp(alogx_ref[:, sl])), axis=0, keepdims=True)
            xs = xbc_ref[:, sl]
            xdt = xs * dt_x
            xdt_b = xdt.astype(BF16)
            bg = xbc_ref[:, SSD_D_INNER + SSD_STATE * g:SSD_D_INNER + SSD_STATE * (g + 1)].astype(BF16)
            cg = xbc_ref[:, SSD_D_INNER + SSD_BC + SSD_STATE * g:SSD_D_INNER + SSD_BC + SSD_STATE * (g + 1)].astype(BF16)
            cb = _dot(cg, bg, NT)
            st_g = state[:, sl]
            y_off = _dot(cg, st_g.astype(BF16)) * jnp.exp(acum_x)
            parts = []
            for r in range(SSD_HEADS_PER_GROUP):
                h = SSD_HEADS_PER_GROUP * g + r
                dec = jnp.exp(jnp.where(lower, acum[:, h:h + 1] - acum_t[h:h + 1, :], -jnp.inf))
                parts.append(_dot((cb * dec).astype(BF16), xdt_b[:, SSD_HEAD_DIM * r:SSD_HEAD_DIM * (r + 1)]))
            y_ref[:, sl] = jnp.concatenate(parts, axis=1) + y_off + dskx_ref[:, sl] * xs
            wgt = (xdt * jnp.exp(tot_x - acum_x)).astype(BF16)
            state[:, sl] = st_g * jnp.exp(tot_x) + _dot(bg, wgt, TN)
        zv = z_ref[...].astype(F32)
        q = y_ref[...] * (zv * _sigmoid(zv))
        r = lax.rsqrt(jnp.mean(q * q, axis=-1, keepdims=True) + NORM_EPS)
        ya_ref[...] = (q * r * nw_ref[...]).astype(BF16)

    def chunk(w):
        return pl.BlockSpec((CHUNK, w), lambda c: (c, 0))

    def const(shape):
        return pl.BlockSpec(shape, lambda c: (0,) * len(shape))

    return pl.pallas_call(
        body, name=name, grid=(nc,),
        in_specs=[chunk(SSD_XBC), chunk(LANES), chunk(SSD_D_INNER), const((1, LANES)), const((1, LANES)),
                  const((1, SSD_D_INNER)), const((1, SSD_D_INNER)), const((1, SSD_D_INNER)),
                  const((LANES, SSD_D_INNER))],
        out_specs=[chunk(SSD_D_INNER), chunk(SSD_D_INNER),
                   pl.BlockSpec((1, SSD_STATE, SSD_D_INNER), lambda c: (c, 0, 0))],
        out_shape=[jax.ShapeDtypeStruct((s, SSD_D_INNER), F32), jax.ShapeDtypeStruct((s, SSD_D_INNER), BF16),
                   jax.ShapeDtypeStruct((nc, SSD_STATE, SSD_D_INNER), F32)],
        scratch_shapes=[pltpu.VMEM((SSD_STATE, SSD_D_INNER), F32)],
        compiler_params=_params(("arbitrary",)),
    )(xbc, dt_raw, z, dt_bias, a_log, a_log_x, d_skip_x, norm_w, expand)


def _ssd_bwd(dya, y, z, xbc, dt_raw, states, dt_bias, a_log, a_log_x, d_skip_x, norm_w, expand, expand_t, *, name):
    s = xbc.shape[0]
    nc = s // CHUNK

    def body(dya_ref, y_ref, z_ref, xbc_ref, dtr_ref, stp_ref, dtb_ref, alog_ref, alogx_ref, dskx_ref, nw_ref,
             e_ref, et_ref, dz_ref, dxbc_ref, ddt_ref, dnw_ref, ddsk_ref, dalog_ref, ddtb_ref,
             dstate, dy_sc, dskcol):
        i = pl.program_id(0)

        @pl.when(i == 0)
        def _():
            dstate[...] = jnp.zeros_like(dstate)
            dskcol[...] = jnp.zeros_like(dskcol)
            dnw_ref[...] = jnp.zeros_like(dnw_ref)
            dalog_ref[...] = jnp.zeros_like(dalog_ref)
            ddtb_ref[...] = jnp.zeros_like(ddtb_ref)
            ddsk_ref[...] = jnp.zeros_like(ddsk_ref)

        lower, upper = _tri_masks()
        rows = lax.broadcasted_iota(jnp.int32, (CHUNK, LANES), 0)
        pre = dtr_ref[...] + dtb_ref[...]
        dt = _softplus(pre)
        a = -jnp.exp(alog_ref[...])
        acum = _dot_exact_lhs(lower.astype(BF16), _split3(dt * a))
        acum_t = acum.T
        dt_terms, acum_terms = _split3(dt), _split3(acum)

        yv = y_ref[...]
        zv = z_ref[...].astype(F32)
        sz = _sigmoid(zv)
        silu_z = zv * sz
        q = yv * silu_z
        r = lax.rsqrt(jnp.mean(q * q, axis=-1, keepdims=True) + NORM_EPS)
        qhat = q * r
        dyav = dya_ref[...]
        dqhat = dyav * nw_ref[...]
        dnw_ref[...] += jnp.sum(dyav * qhat, axis=0, keepdims=True)
        dq = r * (dqhat - qhat * jnp.mean(dqhat * qhat, axis=-1, keepdims=True))
        dy_sc[...] = dq * silu_z
        dz_ref[...] = (dq * yv * (sz * (1.0 + zv * (1.0 - sz)))).astype(BF16)

        da_cum = jnp.zeros((CHUNK, LANES), F32)
        ddt = jnp.zeros((CHUNK, LANES), F32)
        for g in range(SSD_GROUPS):
            sl = slice(GROUP_COLS * g, GROUP_COLS * (g + 1))
            et_g = et_ref[sl, :]
            dt_x = _dot_terms(dt_terms[:2], e_ref[:, sl])
            acum_x = _dot_terms(acum_terms, e_ref[:, sl])
            tot_x = jnp.sum(dt_x * (-jnp.exp(alogx_ref[:, sl])), axis=0, keepdims=True)
            e_tot = jnp.exp(tot_x)
            dec_s = jnp.exp(tot_x - acum_x)
            xs = xbc_ref[:, sl]
            xdt = xs * dt_x
            xdt_b = xdt.astype(BF16)
            dy = dy_sc[:, sl]
            dy_b = dy.astype(BF16)
            dskx = dskx_ref[:, sl]
            y_ssd = y_ref[:, sl] - dskx * xs
            dskcol[:, sl] += jnp.sum(dy * xs, axis=0, keepdims=True)
            bg = xbc_ref[:, SSD_D_INNER + SSD_STATE * g:SSD_D_INNER + SSD_STATE * (g + 1)].astype(BF16)
            cg = xbc_ref[:, SSD_D_INNER + SSD_BC + SSD_STATE * g:SSD_D_INNER + SSD_BC + SSD_STATE * (g + 1)].astype(BF16)
            cb_t = _dot(bg, cg, NT)
            sp = stp_ref[0, :, sl]
            ds_g = dstate[:, sl]
            ds_b = ds_g.astype(BF16)
            dye_b = (dy * jnp.exp(acum_x)).astype(BF16)
            dc = _dot(dye_b, sp.astype(BF16), NT)
            dxdt_state = dec_s * _dot(bg, ds_b)
            db = _dot((xdt * dec_s).astype(BF16), ds_b, NT)
            dcb_t = jnp.zeros((CHUNK, CHUNK), F32)
            parts = []
            for rr in range(SSD_HEADS_PER_GROUP):
                h = SSD_HEADS_PER_GROUP * g + rr
                hs = slice(SSD_HEAD_DIM * rr, SSD_HEAD_DIM * (rr + 1))
                dec_t = jnp.exp(jnp.where(upper, acum_t[h:h + 1, :] - acum[:, h:h + 1], -jnp.inf))
                parts.append(_dot((cb_t * dec_t).astype(BF16), dy_b[:, hs]))
                dcb_t = dcb_t + _dot(xdt_b[:, hs], dy_b[:, hs], NT) * dec_t
            dxdt = jnp.concatenate(parts, axis=1) + dxdt_state
            dcb_tb = dcb_t.astype(BF16)
            dc = dc + _dot(dcb_tb, bg, TN)
            db = db + _dot(dcb_tb, cg)
            tot_col = jnp.sum(ds_g * sp, axis=0, keepdims=True) * e_tot + jnp.sum(dxdt_state * xdt, axis=0, keepdims=True)
            d_tot = _dot_terms(_split3(jnp.broadcast_to(tot_col, (8, GROUP_COLS))), et_g)
            d_tot = jnp.max(d_tot, axis=0, keepdims=True)
            pair_sums = dy_b.astype(F32) * y_ssd - xdt_b.astype(F32) * dxdt
            da_cum = da_cum + _dot_terms(_split3(pair_sums), et_g) + jnp.where(rows == CHUNK - 1, d_tot, 0.0)
            ddt = ddt + _dot_terms(_split3(dxdt * xs)[:2], et_g)
            dxbc_ref[:, sl] = dy * dskx + dxdt * dt_x
            dxbc_ref[:, SSD_D_INNER + SSD_STATE * g:SSD_D_INNER + SSD_STATE * (g + 1)] = db
            dxbc_ref[:, SSD_D_INNER + SSD_BC + SSD_STATE * g:SSD_D_INNER + SSD_BC + SSD_STATE * (g + 1)] = dc
            dstate[:, sl] = e_tot * ds_g + _dot(cg, dye_b, TN)

        dadt = _dot_exact_lhs(upper.astype(BF16), _split3(da_cum))
        ddt = ddt + dadt * a
        dalog_ref[...] += jnp.sum(dadt * dt, axis=0, keepdims=True)
        dpre = ddt * _sigmoid(pre)
        ddtb_ref[...] += jnp.sum(dpre, axis=0, keepdims=True)
        ddt_ref[...] = dpre.astype(BF16)

        @pl.when(i == nc - 1)
        def _():
            dalog_ref[...] = dalog_ref[...] * a
            dsk = _dot_terms(_split3(jnp.broadcast_to(dskcol[...], (8, SSD_D_INNER))), et_ref[...])
            ddsk_ref[...] = jnp.max(dsk, axis=0, keepdims=True)

    def chunk(w):
        return pl.BlockSpec((CHUNK, w), lambda i: (nc - 1 - i, 0))

    def const(shape):
        return pl.BlockSpec(shape, lambda i: (0,) * len(shape))

    return pl.pallas_call(
        body, name=name, grid=(nc,),
        in_specs=[chunk(SSD_D_INNER), chunk(SSD_D_INNER), chunk(SSD_D_INNER), chunk(SSD_XBC), chunk(LANES),
                  pl.BlockSpec((1, SSD_STATE, SSD_D_INNER), lambda i: (nc - 1 - i, 0, 0)),
                  const((1, LANES)), const((1, LANES)), const((1, SSD_D_INNER)), const((1, SSD_D_INNER)),
                  const((1, SSD_D_INNER)), const((LANES, SSD_D_INNER)), const((SSD_D_INNER, LANES))],
        out_specs=[chunk(SSD_D_INNER), chunk(SSD_XBC), chunk(LANES), const((1, SSD_D_INNER)), const((1, LANES)),
                   const((1, LANES)), const((1, LANES))],
        out_shape=[jax.ShapeDtypeStruct((s, SSD_D_INNER), BF16), jax.ShapeDtypeStruct((s, SSD_XBC), F32),
                   jax.ShapeDtypeStruct((s, LANES), BF16), jax.ShapeDtypeStruct((1, SSD_D_INNER), F32),
                   jax.ShapeDtypeStruct((1, LANES), F32), jax.ShapeDtypeStruct((1, LANES), F32),
                   jax.ShapeDtypeStruct((1, LANES), F32)],
        scratch_shapes=[pltpu.VMEM((SSD_STATE, SSD_D_INNER), F32), pltpu.VMEM((CHUNK, SSD_D_INNER), F32),
                        pltpu.VMEM((1, SSD_D_INNER), F32)],
        compiler_params=_params(("arbitrary",)),
    )(dya, y, z, xbc, dt_raw, states, dt_bias, a_log, a_log_x, d_skip_x, norm_w, expand, expand_t)


GELU_K = math.sqrt(2.0 / math.pi)
GELU_C = 0.044715


def _gelu(x):
    return 0.5 * x * (1.0 + jnp.tanh(GELU_K * (x + GELU_C * x * x * x)))


def _gelu_grad(x):
    t = jnp.tanh(GELU_K * (x + GELU_C * x * x * x))
    return 0.5 * (1.0 + t) + 0.5 * x * (1.0 - t * t) * (GELU_K * (1.0 + 3.0 * GELU_C * x * x))


def _sgu_pre(uv_ref, uvb_ref, lnw_ref, lnb_ref):
    uv = uv_ref[...].astype(F32) + uvb_ref[...]
    guv = _gelu(uv)
    u = guv[:, :SGU_WIDTH]
    v = guv[:, SGU_WIDTH:]
    mu = jnp.mean(v, axis=-1, keepdims=True)
    vc = v - mu
    rstd = lax.rsqrt(jnp.mean(vc * vc, axis=-1, keepdims=True) + LN_EPS)
    vhat = vc * rstd
    vn = vhat * lnw_ref[...] + lnb_ref[...]
    return uv, u, vhat, rstd, vn


def _sgu_fwd(uv_raw, uv_b, ln_w, ln_b, w_sp, b_sp_t, *, name):
    s = uv_raw.shape[0]
    nc = s // CHUNK

    def body(uv_ref, uvb_ref, lnw_ref, lnb_ref, w_ref, bt_ref, o_ref):
        lower, _ = _tri_masks()
        _, u, _, _, vn = _sgu_pre(uv_ref, uvb_ref, lnw_ref, lnb_ref)
        vn_b = vn.astype(BF16)
        bt = bt_ref[...]
        for g in range(SGU_GROUPS):
            gs = slice(LANES * g, LANES * (g + 1))
            wc = jnp.where(lower, w_ref[g], 0.0).astype(BF16)
            mixed = _dot(wc, vn_b[:, gs]) + bt[:, g:g + 1]
            o_ref[:, gs] = (u[:, gs] * mixed).astype(BF16)

    def const(shape):
        return pl.BlockSpec(shape, lambda c: (0,) * len(shape))

    return pl.pallas_call(
        body, name=name, grid=(nc,),
        in_specs=[pl.BlockSpec((CHUNK, 2 * SGU_WIDTH), lambda c: (c, 0)), const((1, 2 * SGU_WIDTH)),
                  const((1, SGU_WIDTH)), const((1, SGU_WIDTH)), const((SGU_GROUPS, CHUNK, CHUNK)),
                  const((CHUNK, LANES))],
        out_specs=pl.BlockSpec((CHUNK, SGU_WIDTH), lambda c: (c, 0)),
        out_shape=jax.ShapeDtypeStruct((s, SGU_WIDTH), BF16),
        compiler_params=_params(("parallel",)),
    )(uv_raw, uv_b, ln_w, ln_b, w_sp, b_sp_t)


def _sgu_bwd(uv_raw, dyb, uv_b, ln_w, ln_b, w_sp, b_sp_t, group_sum, *, name):
    s = uv_raw.shape[0]
    nc = s // CHUNK

    def body(uv_ref, dy_ref, uvb_ref, lnw_ref, lnb_ref, w_ref, bt_ref, gsum_ref,
             duv_ref, dw_ref, dbt_ref, dlnw_ref, dlnb_ref, duvb_ref):
        @pl.when(pl.program_id(0) == 0)
        def _():
            dw_ref[...] = jnp.zeros_like(dw_ref)
            dbt_ref[...] = jnp.zeros_like(dbt_ref)
            dlnw_ref[...] = jnp.zeros_like(dlnw_ref)
            dlnb_ref[...] = jnp.zeros_like(dlnb_ref)
            duvb_ref[...] = jnp.zeros_like(duvb_ref)

        lower, _ = _tri_masks()
        uv, u, vhat, rstd, vn = _sgu_pre(uv_ref, uvb_ref, lnw_ref, lnb_ref)
        vn_b = vn.astype(BF16)
        bt = bt_ref[...]
        dy = dy_ref[...].astype(F32)
        du_parts, dvn_parts, dmix_parts = [], [], []
        for g in range(SGU_GROUPS):
            gs = slice(LANES * g, LANES * (g + 1))
            wc = jnp.where(lower, w_ref[g], 0.0).astype(BF16)
            mixed = _dot(wc, vn_b[:, gs]) + bt[:, g:g + 1]
            du_parts.append(dy[:, gs] * mixed)
            dmix = dy[:, gs] * u[:, gs]
            dmix_b = dmix.astype(BF16)
            dmix_parts.append(dmix)
            dw_ref[g] += jnp.where(lower, _dot(dmix_b, vn_b[:, gs], NT), 0.0)
            dvn_parts.append(_dot(wc, dmix_b, TN))
        dmixed = jnp.concatenate(dmix_parts, axis=1)
        dbt_ref[...] += _dot_terms(_split3(dmixed), gsum_ref[...])
        dvn = jnp.concatenate(dvn_parts, axis=1)
        dlnw_ref[...] += jnp.sum(dvn * vhat, axis=0, keepdims=True)
        dlnb_ref[...] += jnp.sum(dvn, axis=0, keepdims=True)
        dvhat = dvn * lnw_ref[...]
        dv = rstd * (dvhat - jnp.mean(dvhat, axis=-1, keepdims=True)
                     - vhat * jnp.mean(dvhat * vhat, axis=-1, keepdims=True))
        dguv = jnp.concatenate(du_parts + [dv], axis=1)
        duv = dguv * _gelu_grad(uv)
        duvb_ref[...] += jnp.sum(duv, axis=0, keepdims=True)
        duv_ref[...] = duv.astype(BF16)

    def const(shape):
        return pl.BlockSpec(shape, lambda c: (0,) * len(shape))

    return pl.pallas_call(
        body, name=name, grid=(nc,),
        in_specs=[pl.BlockSpec((CHUNK, 2 * SGU_WIDTH), lambda c: (c, 0)),
                  pl.BlockSpec((CHUNK, SGU_WIDTH), lambda c: (c, 0)), const((1, 2 * SGU_WIDTH)),
                  const((1, SGU_WIDTH)), const((1, SGU_WIDTH)), const((SGU_GROUPS, CHUNK, CHUNK)),
                  const((CHUNK, LANES)), const((SGU_WIDTH, LANES))],
        out_specs=[pl.BlockSpec((CHUNK, 2 * SGU_WIDTH), lambda c: (c, 0)), const((SGU_GROUPS, CHUNK, CHUNK)),
                   const((CHUNK, LANES)), const((1, SGU_WIDTH)), const((1, SGU_WIDTH)), const((1, 2 * SGU_WIDTH))],
        out_shape=[jax.ShapeDtypeStruct((s, 2 * SGU_WIDTH), BF16),
                   jax.ShapeDtypeStruct((SGU_GROUPS, CHUNK, CHUNK), F32), jax.ShapeDtypeStruct((CHUNK, LANES), F32),
                   jax.ShapeDtypeStruct((1, SGU_WIDTH), F32), jax.ShapeDtypeStruct((1, SGU_WIDTH), F32),
                   jax.ShapeDtypeStruct((1, 2 * SGU_WIDTH), F32)],
        compiler_params=_params(("arbitrary",)),
    )(uv_raw, dyb, uv_b, ln_w, ln_b, w_sp, b_sp_t, group_sum)


def _gate_fwd(gates_raw, b_gate, p_a, p_b, *, name, tm=512):
    s = p_a.shape[0]
    tm = min(tm, s)

    def body(ga_ref, gb_ref, ba_ref, bb_ref, pa_ref, pb_ref, o_ref):
        ga = _sigmoid(ga_ref[...].astype(F32) + ba_ref[...])
        gb = _sigmoid(gb_ref[...].astype(F32) + bb_ref[...])
        o_ref[...] = (ga * pa_ref[...].astype(F32) + gb * pb_ref[...].astype(F32)).astype(BF16)

    t_a = pl.BlockSpec((tm, D_MODEL), lambda i: (i, 0))
    t_b = pl.BlockSpec((tm, D_MODEL), lambda i: (i, 1))
    r_a = pl.BlockSpec((1, D_MODEL), lambda i: (0, 0))
    r_b = pl.BlockSpec((1, D_MODEL), lambda i: (0, 1))
    return pl.pallas_call(
        body, name=name, grid=(s // tm,),
        in_specs=[t_a, t_b, r_a, r_b, t_a, t_a], out_specs=t_a,
        out_shape=jax.ShapeDtypeStruct((s, D_MODEL), BF16),
        compiler_params=_params(("parallel",)),
    )(gates_raw, gates_raw, b_gate, b_gate, p_a, p_b)


def _gate_bwd(gates_raw, b_gate, p_a, p_b, dm, *, name, tm=512):
    s = p_a.shape[0]
    tm = min(tm, s)

    def body(ga_ref, gb_ref, ba_ref, bb_ref, pa_ref, pb_ref, dm_ref, dpa_ref, dpb_ref, dga_ref, dgb_ref,
             dba_ref, dbb_ref):
        @pl.when(pl.program_id(0) == 0)
        def _():
            dba_ref[...] = jnp.zeros_like(dba_ref)
            dbb_ref[...] = jnp.zeros_like(dbb_ref)

        d = dm_ref[...].astype(F32)
        for g_ref, b_ref, p_ref, dp_ref, dg_ref, db_ref in ((ga_ref, ba_ref, pa_ref, dpa_ref, dga_ref, dba_ref),
                                                            (gb_ref, bb_ref, pb_ref, dpb_ref, dgb_ref, dbb_ref)):
            sg = _sigmoid(g_ref[...].astype(F32) + b_ref[...])
            dp_ref[...] = (d * sg).astype(BF16)
            dg = d * p_ref[...].astype(F32) * (sg * (1.0 - sg))
            dg_ref[...] = dg.astype(BF16)
            db_ref[...] += jnp.sum(dg, axis=0, keepdims=True)

    t_a = pl.BlockSpec((tm, D_MODEL), lambda i: (i, 0))
    t_b = pl.BlockSpec((tm, D_MODEL), lambda i: (i, 1))
    r_a = pl.BlockSpec((1, D_MODEL), lambda i: (0, 0))
    r_b = pl.BlockSpec((1, D_MODEL), lambda i: (0, 1))
    big = jax.ShapeDtypeStruct((s, D_MODEL), BF16)
    row = jax.ShapeDtypeStruct((1, D_MODEL), F32)
    return pl.pallas_call(
        body, name=name, grid=(s // tm,),
        in_specs=[t_a, t_b, r_a, r_b, t_a, t_a, t_a], out_specs=[t_a, t_a, t_a, t_a, r_a, r_a],
        out_shape=[big, big, big, big, row, row],
        compiler_params=_params(("arbitrary",)),
    )(gates_raw, gates_raw, b_gate, b_gate, p_a, p_b, dm)


def _adamw_update(w_ref, g_ref, m_ref, v_ref, d_ref, mo_ref, vo_ref):
    gv = g_ref[...]
    mn = ADAM_B1 * m_ref[...] + (1.0 - ADAM_B1) * gv
    vn = ADAM_B2 * v_ref[...] + (1.0 - ADAM_B2) * (gv * gv)
    m_hat = mn / (1.0 - ADAM_B1 ** ADAM_STEP)
    v_hat = vn / (1.0 - ADAM_B2 ** ADAM_STEP)
    d_ref[...] = -ADAM_LR * (m_hat / (jnp.sqrt(v_hat) + ADAM_EPS) + ADAM_WD * w_ref[...])
    mo_ref[...] = mn
    vo_ref[...] = vn


def _adamw_many(ws, gs, ms, vs, *, name):
    n = len(ws)

    def body(*refs):
        for i in range(n):
            _adamw_update(*[refs[k * n + i] for k in range(7)])

    whole = pl.BlockSpec(memory_space=pltpu.VMEM)
    sds = [jax.ShapeDtypeStruct(w.shape, F32) for w in ws]
    outs = pl.pallas_call(
        body, name=name, in_specs=[whole] * (4 * n), out_specs=[whole] * (3 * n), out_shape=sds * 3,
        compiler_params=pltpu.CompilerParams(vmem_limit_bytes=VMEM_LIMIT),
    )(*ws, *gs, *ms, *vs)
    return outs[:n], outs[n:2 * n], outs[2 * n:]


def _adamw(w, g, m, v, *, name, tr=128):
    r, c = w.shape
    tr = min(tr, r)
    assert r % tr == 0, (name, r, tr)
    body = functools.partial(_adamw_update)

    blk = pl.BlockSpec((tr, c), lambda i: (i, 0))
    sds = jax.ShapeDtypeStruct((r, c), F32)
    return pl.pallas_call(
        body, name=name, grid=(r // tr,), in_specs=[blk] * 4, out_specs=[blk] * 3, out_shape=[sds] * 3,
        compiler_params=_params(("parallel",)),
    )(w, g, m, v)


def _adamw_two_sums(w, g_a, g_b, m, v, *, name, tr=128):
    r, c = w.shape
    tr = min(tr, r)
    assert r % tr == 0, (name, r, tr)

    def body(w_ref, ga_ref, gb_ref, m_ref, v_ref, g_ref, d_ref, mo_ref, vo_ref):
        g_ref[...] = ga_ref[...] + gb_ref[...]
        _adamw_update(w_ref, g_ref, m_ref, v_ref, d_ref, mo_ref, vo_ref)

    blk = pl.BlockSpec((tr, c), lambda i: (i, 0))
    sds = jax.ShapeDtypeStruct((r, c), F32)
    return pl.pallas_call(
        body, name=name, grid=(r // tr,), in_specs=[blk] * 5, out_specs=[blk] * 4, out_shape=[sds] * 4,
        compiler_params=_params(("parallel",)),
    )(w, g_a, g_b, m, v)


def _tile(n, pref):
    if n <= pref:
        return n
    best = LANES
    for t in range(LANES, pref + 1, LANES):
        if n % t == 0:
            best = t
    return best


MATMUL_BLOCK_BYTES = 20 * 1024 * 1024


def _mm(pairs, name, **kw):
    trans_b = kw.get("trans_b", False)
    m = (pairs[0][0][0] if isinstance(pairs[0][0], tuple) else pairs[0][0]).shape[0]
    ktot, n = 0, None
    for _, b in pairs:
        shape = b[0].shape[1:] if isinstance(b, tuple) else b.shape
        ktot += shape[1] if trans_b else shape[0]
        n = shape[0] if trans_b else shape[1]
    out_bytes = 4 * (2 if kw.get("add") is not None else 1)
    best = None
    for tm in (256, 512, 1024, 2048):
        for tn in range(LANES, min(n, 1536) + 1, LANES):
            if m % min(tm, m) or n % tn:
                continue
            fits = 2 * ktot * (min(tm, m) + tn) + out_bytes * min(tm, m) * tn <= MATMUL_BLOCK_BYTES
            if fits and (best is None or min(tm, m) * tn >= best[0] * best[1]):
                best = (min(tm, m), tn)
    return _matmul(pairs, tm=best[0], tn=best[1], name=name, **kw)


def _wgrad(a, b, name, **kw):
    return _matmul_tn(a, b, tk=_tile(a.shape[1], 1408), tn=kw.pop("tn", _tile(b.shape[1], 1024)), tm=2048,
                      name=name, **kw)


def _local_step(x, target, get_weight, small, emit_grad):
    heads = jnp.arange(SSD_D_INNER) // SSD_HEAD_DIM
    expand = (jnp.arange(LANES)[:, None] == heads[None, :]).astype(BF16)
    expand_t = expand.T
    group_sum = (jnp.arange(SGU_WIDTH)[:, None] // LANES == jnp.arange(LANES)[None, :]).astype(BF16)
    pad_h = LANES - SSD_HEADS
    dt_bias = jnp.pad(small["dt_bias"], ((0, 0), (0, pad_h)))
    a_log = jnp.pad(small["a_log"], ((0, 0), (0, pad_h)))
    a_log_x = jnp.repeat(small["a_log"], SSD_HEAD_DIM, axis=1)
    d_skip_x = jnp.repeat(small["d_skip"], SSD_HEAD_DIM, axis=1)
    b_sp_t = jnp.pad(small["b_spatial"][0].T, ((0, 0), (0, LANES - SGU_GROUPS)))
    w_sp = small["w_spatial"][0]
    conv_a_w = jnp.pad(small["conv_a_w"], ((0, 4), (0, 0)))
    conv_f_w = jnp.pad(small["conv_f_w"], ((0, 5), (0, 0)))
    final_w = small["final_norm_w"].reshape(1, D_MODEL)

    n1 = _rms_fwd(x, small["norm1_w"], after=small.get("gathers_started"), name="rms1_fwd")
    wts = dict(get_weight("w_in", n1))
    z = _mm([(n1, wts["in_z"])], "in_z")
    xbc_raw = _mm([(n1, wts["in_xbc"])], "in_xbc")
    dt_raw = _mm([(n1, wts["in_dt"])], "in_dt")
    uv_raw = _mm([(n1, wts["in_uv"])], "in_uv", out_dtype=BF16)
    gates_raw = _mm([(n1, wts["in_gate"])], "in_gate", out_dtype=BF16)
    xbc, xbc_pre = _conv_a_fwd(xbc_raw, conv_a_w, small["conv_a_b"], name="conv_a_fwd")
    y, y_a, states = _ssd_fwd(xbc, dt_raw, z, dt_bias, a_log, a_log_x, d_skip_x, small["ssd_norm_w"], expand,
                              name="ssd_fwd")
    y_b = _sgu_fwd(uv_raw, small["uv_b"], small["v_ln_w"], small["v_ln_b"], w_sp, b_sp_t, name="sgu_fwd")
    wts.update(get_weight("w_branch", y_b))
    p_a = _mm([(y_a, wts["branch_a"])], "branch_a", out_dtype=BF16)
    p_b = _mm([(y_b, wts["branch_b"])], "branch_b", out_dtype=BF16)
    mix = _gate_fwd(gates_raw, small["b_gate"], p_a, p_b, name="gate_fwd")
    wts.update(get_weight("w_out", mix))
    h1 = _mm([(mix, wts["out"])], "out_proj", add=x)
    n2 = _rms_fwd(h1, small["norm2_w"], name="rms2_fwd")
    wts.update(get_weight("w_up", n2))
    up_w = wts["up"]
    up_cols = up_w.shape[2]
    up_raw = _matmul([(n2, (up_w, "cols"))], tm=2048, tn=up_cols, out_dtype=BF16, name="up_proj")
    act, up_a, up_v = _conv_f_fwd(up_raw, conv_f_w, small["conv_f_b"], name="conv_f_fwd")
    wts.update(get_weight("w_down", act))
    h2 = _mm([(act, wts["down"])], "down_proj", add=h1)
    loss, dh2, dh2_b, d_final = _final_fwd_bwd(h2, final_w, target, name="final_norm_loss")

    dact = _mm([(dh2_b, wts["down"])], "down_dgrad", trans_b=True)
    started = emit_grad("w_down", _wgrad(act, dh2_b, "down_wgrad"))
    dup_a, dup_v, dwf_a, dwf_v, dbf_a, dbf_v = _conv_f_bwd(up_raw, up_a, up_v, conv_f_w, dact, name="conv_f_bwd")
    dn2 = _mm([((dup_a, 0), (up_w, 0)), ((dup_a, 1), (up_w, 1)), ((dup_v, 0), (up_w, 2)), ((dup_v, 1), (up_w, 3))],
              "up_dgrad", trans_b=True, after=started, out_dtype=BF16)
    g_up = _wgrad(n2, dup_a, "up_wgrad_a", tn=up_cols, stack_out=True, part_of=(N_CHIPS, 0, None))
    g_up = _wgrad(n2, dup_v, "up_wgrad_v", tn=up_cols, stack_out=True, part_of=(N_CHIPS, N_CHIPS // 2, g_up))
    started = emit_grad("w_up", g_up)
    dh1, dh1_b, d_norm2 = _rms_bwd(h1, small["norm2_w"], dn2, dh2, name="rms2_bwd")
    dmix = _mm([(dh1_b, wts["out"])], "out_dgrad", trans_b=True, after=started, out_dtype=BF16)
    started = emit_grad("w_out", _wgrad(mix, dh1_b, "out_wgrad"))
    dp_a, dp_b, dg_a, dg_b, dbg_a, dbg_b = _gate_bwd(gates_raw, small["b_gate"], p_a, p_b, dmix, name="gate_bwd")
    dya = _mm([(dp_a, wts["branch_a"])], "branch_a_dgrad", trans_b=True, after=started)
    dyb = _mm([(dp_b, wts["branch_b"])], "branch_b_dgrad", trans_b=True, out_dtype=BF16)
    g_branch = _wgrad(y_a, dp_a, "branch_a_wgrad", part_of=(3, 0, None))
    g_branch = _wgrad(y_b, dp_b, "branch_b_wgrad", part_of=(3, 2, g_branch))
    started_branch = emit_grad("w_branch", g_branch)
    duv, d_wsp, d_bsp_t, d_lnw, d_lnb, d_uvb = _sgu_bwd(uv_raw, dyb, small["uv_b"], small["v_ln_w"],
                                                        small["v_ln_b"], w_sp, b_sp_t, group_sum, name="sgu_bwd")
    dz, dxbc, ddt, d_ssd_nw, d_dskip, d_alog, d_dtb = _ssd_bwd(
        dya, y, z, xbc, dt_raw, states, dt_bias, a_log, a_log_x, d_skip_x, small["ssd_norm_w"], expand, expand_t,
        name="ssd_bwd")
    dxbc_raw, d_conv_a_w, d_conv_a_b = _conv_a_bwd(xbc_raw, xbc_pre, conv_a_w, dxbc, name="conv_a_bwd")
    started = emit_grad("w_in", {
        "in_z": _wgrad(n1, dz, "in_z_wgrad", after=started_branch), "in_xbc": _wgrad(n1, dxbc_raw, "in_xbc_wgrad"),
        "in_dt": _wgrad(n1, ddt, "in_dt_wgrad")[:, :SSD_HEADS], "in_uv": _wgrad(n1, duv, "in_uv_wgrad"),
        "in_gate_a": _wgrad(n1, dg_a, "in_gate_a_wgrad"), "in_gate_b": _wgrad(n1, dg_b, "in_gate_b_wgrad")})
    dn1 = _mm([(dz, wts["in_z"]), (dxbc_raw, wts["in_xbc"]), (ddt, wts["in_dt"]), (duv, wts["in_uv"]),
               (dg_a, wts["in_gate_a"]), (dg_b, wts["in_gate_b"])], "in_dgrad", trans_b=True, after=started,
              out_dtype=BF16)
    dx, _, d_norm1 = _rms_bwd(x, small["norm1_w"], dn1, dh1, name="rms1_bwd")

    grads_small = {
        "norm1_w": d_norm1, "b_gate": jnp.concatenate([dbg_a, dbg_b], axis=1),
        "conv_a_w": d_conv_a_w[:4], "conv_a_b": d_conv_a_b,
        "dt_bias": d_dtb[:, :SSD_HEADS], "a_log": d_alog[:, :SSD_HEADS], "d_skip": d_dskip[:, :SSD_HEADS],
        "ssd_norm_w": d_ssd_nw, "uv_b": d_uvb, "v_ln_w": d_lnw, "v_ln_b": d_lnb,
        "w_spatial": d_wsp[None], "b_spatial": d_bsp_t[:, :SGU_GROUPS].T[None],
        "norm2_w": d_norm2, "conv_f_w": jnp.concatenate([dwf_a[:3], dwf_v[:3]], axis=1),
        "conv_f_b": jnp.concatenate([dbf_a, dbf_v], axis=1), "final_norm_w": d_final.reshape(D_MODEL),
    }
    return loss, dx, grads_small


HBM = pl.BlockSpec(memory_space=pl.ANY)
MESH = pl.DeviceIdType.MESH


def _mesh_pos():
    return lax.axis_index("x"), lax.axis_index("y"), lax.axis_index("c")


def _other_chips(x, y):
    return [(1 - x, y), (x, 1 - y), (1 - x, 1 - y)]


def _remote(src, dst, send_sems, recv_sems, k, dev):
    return pltpu.make_async_remote_copy(src_ref=src, dst_ref=dst, send_sem=send_sems.at[k], recv_sem=recv_sems.at[k],
                                        device_id=dev, device_id_type=MESH)


def _dma_sems(n):
    return [pltpu.SemaphoreType.DMA((n,)), pltpu.SemaphoreType.DMA((n,))]


HBM_ONLY = pl.BlockSpec(memory_space=pltpu.HBM)
SEMAPHORES = pl.BlockSpec(memory_space=pltpu.SEMAPHORE)
DATAFLOW_EFFECT = pltpu.SideEffectType.DATAFLOW_SIDE_EFFECTING
N_PEER_CHIPS = N_CHIPS - 1


def _gather_sends(w_ref, land_ref, send_sems, recv_sems):
    x, y, c = _mesh_pos()
    return [_remote(w_ref.at[c], land_ref.at[2 * x + y, c], send_sems, recv_sems, k, (px, py, c))
            for k, (px, py) in enumerate(_other_chips(x, y))]


def _gather_arrivals(w_ref, land_ref, send_sems, recv_sems):
    x, y, c = _mesh_pos()
    return [_remote(w_ref.at[c], land_ref.at[2 * px + py, c], send_sems, recv_sems, k, (px, py, c))
            for k, (px, py) in enumerate(_other_chips(x, y))]


def _gather_whole_sends(w_ref, land_ref, send_sems, recv_sems):
    x, y, c = _mesh_pos()
    return [_remote(w_ref, land_ref.at[2 * x + y], send_sems, recv_sems, k, (px, py, c))
            for k, (px, py) in enumerate(_other_chips(x, y))]


def _gather_whole_arrivals(w_ref, land_ref, send_sems, recv_sems):
    x, y, c = _mesh_pos()
    return [_remote(w_ref, land_ref.at[2 * px + py], send_sems, recv_sems, k, (px, py, c))
            for k, (px, py) in enumerate(_other_chips(x, y))]


def _scatter_sends(h_ref, land_ref, send_sems, recv_sems):
    x, y, c = _mesh_pos()
    return [_remote(h_ref.at[2 * px + py], land_ref.at[2 * x + y], send_sems, recv_sems, k, (px, py, c))
            for k, (px, py) in enumerate(_other_chips(x, y))]


def _scatter_arrivals(h_ref, land_ref, send_sems, recv_sems):
    x, y, c = _mesh_pos()
    return [_remote(h_ref.at[2 * x + y], land_ref.at[2 * px + py], send_sems, recv_sems, k, (px, py, c))
            for k, (px, py) in enumerate(_other_chips(x, y))]


def _exchange_wait_many(pendings, after, sends, arrivals, *, name):
    n = len(pendings)

    def body(*refs):
        for i in range(n):
            src_ref, land_ref, send_ref, recv_ref = refs[i], refs[n + i], refs[2 * n + i], refs[3 * n + i]
            for cp in sends(src_ref, land_ref, send_ref, recv_ref):
                cp.wait_send()
            for cp in arrivals(src_ref, land_ref, send_ref, recv_ref):
                cp.wait_recv()

    sources = [p[2] for p in pendings]
    landings = [p[3] for p in pendings]
    outs = pl.pallas_call(
        body, name=name,
        out_shape=tuple(pltpu.HBM(a.shape, a.dtype) for a in sources + landings),
        in_specs=[HBM_ONLY] * (2 * n) + [SEMAPHORES] * (2 * n) + [pl.BlockSpec(memory_space=pl.ANY)],
        out_specs=tuple([HBM_ONLY] * (2 * n)), input_output_aliases={i: i for i in range(2 * n)},
        compiler_params=pltpu.CompilerParams(has_side_effects=DATAFLOW_EFFECT),
    )(*sources, *landings, *[p[0] for p in pendings], *[p[1] for p in pendings], after)
    return [(outs[i], outs[n + i]) for i in range(n)]


def _sibling_sends(src_ref, land_ref, send_sems, recv_sems):
    x, y, c = _mesh_pos()
    return [_remote(src_ref, land_ref, send_sems, recv_sems, 0, (x, y, 1 - c))]


def _exchange_start(sources, landing_shapes, sends, *, after=None, name):
    n = len(sources)
    extra = [] if after is None else [after]

    def body(*refs):
        sems = refs[2 * n + len(extra):4 * n + len(extra)]
        for i in range(n):
            send_i = sends[i] if isinstance(sends, (list, tuple)) else sends
            for cp in send_i(refs[i], refs[n + i], sems[2 * i], sems[2 * i + 1]):
                cp.start()
        refs[-1][...] = jnp.zeros_like(refs[-1])

    hbm = [pltpu.HBM(s.shape, s.dtype) for s in sources] + [pltpu.HBM(shp, s.dtype)
                                                             for shp, s in zip(landing_shapes, sources)]
    outs = pl.pallas_call(
        body, name=name,
        out_shape=tuple([pltpu.SemaphoreType.DMA((N_PEER_CHIPS,))] * (2 * n) + hbm
                        + [jax.ShapeDtypeStruct((8, LANES), F32)]),
        in_specs=[HBM_ONLY] * (2 * n) + [pl.BlockSpec(memory_space=pl.ANY)] * len(extra),
        out_specs=tuple([SEMAPHORES] * (2 * n) + [HBM_ONLY] * (2 * n) + [pl.BlockSpec(memory_space=pltpu.VMEM)]),
        input_output_aliases={i: 2 * n + i for i in range(2 * n)},
        compiler_params=pltpu.CompilerParams(has_side_effects=DATAFLOW_EFFECT),
    )(*[pltpu.with_memory_space_constraint(s, pltpu.HBM) for s in sources],
      *[pltpu.with_memory_space_constraint(lax.empty(shp, s.dtype), pltpu.HBM)
        for shp, s in zip(landing_shapes, sources)], *extra)
    pending = [(outs[2 * i], outs[2 * i + 1], outs[2 * n + i], outs[3 * n + i]) for i in range(n)]
    return pending, outs[-1]


def _exchange_wait(pending, after, sends, arrivals, *, name):
    send_sems, recv_sems, source, landing = pending

    def body(src_ref, land_ref, send_ref, recv_ref, after_ref, src_out, land_out):
        for cp in sends(src_ref, land_ref, send_ref, recv_ref):
            cp.wait_send()
        for cp in arrivals(src_ref, land_ref, send_ref, recv_ref):
            cp.wait_recv()

    return pl.pallas_call(
        body, name=name,
        out_shape=(pltpu.HBM(source.shape, source.dtype), pltpu.HBM(landing.shape, landing.dtype)),
        in_specs=[HBM_ONLY, HBM_ONLY, SEMAPHORES, SEMAPHORES, pl.BlockSpec(memory_space=pl.ANY)],
        out_specs=(HBM_ONLY, HBM_ONLY), input_output_aliases={0: 0, 1: 1},
        compiler_params=pltpu.CompilerParams(has_side_effects=DATAFLOW_EFFECT),
    )(source, landing, send_sems, recv_sems, after)


def _gather_ici(shard, *, name):
    _, rh, cols = shard.shape

    def body(w_ref, o_ref, send_sems, recv_sems):
        x, y, c = _mesh_pos()
        mine = 2 * x + y
        sends = []
        for k, (px, py) in enumerate(_other_chips(x, y)):
            cp = _remote(w_ref.at[c], o_ref.at[mine, c], send_sems, recv_sems, k, (px, py, c))
            cp.start()
            sends.append(cp)
        for k, (px, py) in enumerate(_other_chips(x, y)):
            _remote(w_ref.at[c], o_ref.at[2 * px + py, c], send_sems, recv_sems, k, (px, py, c)).wait_recv()
        for cp in sends:
            cp.wait_send()

    return pl.pallas_call(
        body, name=name, in_specs=[HBM], out_specs=HBM,
        out_shape=jax.ShapeDtypeStruct((N_CHIPS, 2, rh, cols), shard.dtype), scratch_shapes=_dma_sems(3),
    )(shard)


def _gather_d2d(parts, *, name):
    def body(a_ref, o_ref, send_sems, recv_sems):
        x, y, c = _mesh_pos()
        sibling = (x, y, 1 - c)
        sends = []
        for k, (px, py) in enumerate(_other_chips(x, y)):
            cp = _remote(a_ref.at[2 * px + py, c], o_ref.at[2 * px + py, c], send_sems, recv_sems, k, sibling)
            cp.start()
            sends.append(cp)
        for k, (px, py) in enumerate(_other_chips(x, y)):
            _remote(a_ref.at[2 * px + py, c], o_ref.at[2 * px + py, 1 - c], send_sems, recv_sems, k, sibling).wait_recv()
        for cp in sends:
            cp.wait_send()

    return pl.pallas_call(
        body, name=name, in_specs=[HBM], out_specs=HBM,
        out_shape=jax.ShapeDtypeStruct(parts.shape, parts.dtype),
        input_output_aliases={0: 0}, scratch_shapes=_dma_sems(3),
    )(parts)


def _all_gather_chips(shard_flat, name):
    rows, cols = shard_flat.shape
    parts = _gather_ici(shard_flat.reshape(2, rows // 2, cols), name=name + "_ici")
    others = _gather_d2d(parts, name=name + "_d2d").reshape(N_CHIPS, rows, cols)
    chip = 2 * lax.axis_index("x") + lax.axis_index("y")
    return lax.dynamic_update_slice(others, shard_flat[None], (chip, 0, 0))


def _row_tile(rows, mult, cap):
    best = mult
    for t in range(mult, min(rows, cap) + 1, mult):
        if rows % t == 0:
            best = t
    assert rows % best == 0, (rows, mult)
    return best


def _swap_halves_d2d(g, *, after=None, name):
    _, _, rh, cols = g.shape
    extra = [] if after is None else [after]

    def body(g_ref, *rest):
        o_ref, send_sems, recv_sems = rest[len(extra):]
        x, y, c = _mesh_pos()
        sibling = (x, y, 1 - c)
        sends = []
        for s in range(N_CHIPS):
            cp = _remote(g_ref.at[s, 1 - c], o_ref.at[s], send_sems, recv_sems, s, sibling)
            cp.start()
            sends.append(cp)
        for s in range(N_CHIPS):
            _remote(g_ref.at[s, c], o_ref.at[s], send_sems, recv_sems, s, sibling).wait_recv()
        for cp in sends:
            cp.wait_send()

    return pl.pallas_call(
        body, name=name, in_specs=[HBM] * (1 + len(extra)), out_specs=HBM,
        out_shape=jax.ShapeDtypeStruct((N_CHIPS, rh, cols), g.dtype), scratch_shapes=_dma_sems(N_CHIPS),
    )(g, *extra)


def _add_own_half(g, arrived, core, *, name):
    _, _, rh, cols = g.shape
    mult = 16 if g.dtype == BF16 else 8
    tr = _row_tile(rh, mult, max(mult, (512 * 1024) // cols))

    def body(core_ref, g_ref, a_ref, o_ref):
        o_ref[...] = (g_ref[0].astype(F32) + a_ref[...].astype(F32)).astype(o_ref.dtype)

    grid_spec = pltpu.PrefetchScalarGridSpec(
        num_scalar_prefetch=1, grid=(N_CHIPS, rh // tr),
        in_specs=[pl.BlockSpec((1, 1, tr, cols), lambda s, i, core_ref: (s, core_ref[0], i, 0)),
                  pl.BlockSpec((1, tr, cols), lambda s, i, core_ref: (s, i, 0))],
        out_specs=pl.BlockSpec((1, tr, cols), lambda s, i, core_ref: (s, i, 0)))
    return pl.pallas_call(
        body, name=name, grid_spec=grid_spec, out_shape=jax.ShapeDtypeStruct((N_CHIPS, rh, cols), g.dtype),
        compiler_params=_params(("parallel", "parallel")),
    )(core, g, arrived)


def _scatter_ici(h, *, name):
    def body(h_ref, o_ref, send_sems, recv_sems):
        x, y, c = _mesh_pos()
        mine = 2 * x + y
        sends = []
        for k, (px, py) in enumerate(_other_chips(x, y)):
            cp = _remote(h_ref.at[2 * px + py], o_ref.at[mine], send_sems, recv_sems, k, (px, py, c))
            cp.start()
            sends.append(cp)
        for k, (px, py) in enumerate(_other_chips(x, y)):
            _remote(h_ref.at[mine], o_ref.at[2 * px + py], send_sems, recv_sems, k, (px, py, c)).wait_recv()
        for cp in sends:
            cp.wait_send()

    others = pl.pallas_call(
        body, name=name, in_specs=[HBM], out_specs=HBM, out_shape=jax.ShapeDtypeStruct(h.shape, h.dtype),
        scratch_shapes=_dma_sems(3),
    )(h)
    chip = 2 * lax.axis_index("x") + lax.axis_index("y")
    own = lax.dynamic_slice_in_dim(h, chip, 1, axis=0)
    return lax.dynamic_update_slice(others, own, (chip, 0, 0))


def _sum_chips(parts, *, name):
    _, rh, cols = parts.shape
    mult = 16 if parts.dtype == BF16 else 8
    tr = _row_tile(rh, mult, max(mult, (512 * 1024) // cols))

    def body(p_ref, o_ref):
        acc = p_ref[0].astype(F32)
        for s in range(1, N_CHIPS):
            acc = acc + p_ref[s].astype(F32)
        o_ref[...] = acc

    return pl.pallas_call(
        body, name=name, grid=(rh // tr,),
        in_specs=[pl.BlockSpec((N_CHIPS, tr, cols), lambda i: (0, i, 0))],
        out_specs=pl.BlockSpec((tr, cols), lambda i: (i, 0)),
        out_shape=jax.ShapeDtypeStruct((rh, cols), F32), compiler_params=_params(("parallel",)),
    )(parts)


def _sum_chips_with_own(landed, sent, chip, *, name):
    _, rh, cols = landed.shape
    mult = 16 if landed.dtype == BF16 else 8
    tr = _row_tile(rh, mult, max(mult, (512 * 1024) // cols))

    def body(chip_ref, own_ref, px_ref, py_ref, pxy_ref, o_ref):
        acc = own_ref[0].astype(F32)
        for p_ref in (px_ref, py_ref, pxy_ref):
            acc = acc + p_ref[0].astype(F32)
        o_ref[...] = acc

    def block_of(flip):
        return pl.BlockSpec((1, tr, cols), lambda i, chip_ref: (chip_ref[0] ^ flip, i, 0))

    grid_spec = pltpu.PrefetchScalarGridSpec(
        num_scalar_prefetch=1, grid=(rh // tr,),
        in_specs=[block_of(0), block_of(2), block_of(1), block_of(3)],
        out_specs=pl.BlockSpec((tr, cols), lambda i, chip_ref: (i, 0)))
    return pl.pallas_call(
        body, name=name, grid_spec=grid_spec, out_shape=jax.ShapeDtypeStruct((rh, cols), F32),
        compiler_params=_params(("parallel",)),
    )(chip, sent, landed, landed, landed)


def _share_d2d(f, *, name):
    fs = f if isinstance(f, (list, tuple)) else [f]
    others = _swap_with_sibling(fs, name=name)
    first = lax.axis_index("c") == 0
    both = [jnp.stack([jnp.where(first, a, b), jnp.where(first, b, a)]) for a, b in zip(fs, others)]
    return both if isinstance(f, (list, tuple)) else both[0]


def _swap_with_sibling(fs, *, name):
    n = len(fs)

    def body(*refs):
        x, y, c = _mesh_pos()
        sibling = (x, y, 1 - c)
        send_sems, recv_sems = refs[2 * n:]
        copies = [_remote(refs[i], refs[n + i], send_sems, recv_sems, i, sibling) for i in range(n)]
        for cp in copies:
            cp.start()
        for cp in copies:
            cp.wait()

    return pl.pallas_call(
        body, name=name, in_specs=[HBM] * n, out_specs=[HBM] * n,
        out_shape=[jax.ShapeDtypeStruct(a.shape, a.dtype) for a in fs], scratch_shapes=_dma_sems(n),
    )(*fs)


def _reduce_scatter_chips(g, core, name, after=None):
    _, rows, cols = g.shape
    g = g.reshape(N_CHIPS, 2, rows // 2, cols)
    arrived = _swap_halves_d2d(g, after=after, name=name + "_swap")
    chip_sum = _add_own_half(g, arrived, core, name=name + "_add2")
    parts = _scatter_ici(chip_sum, name=name + "_ici")
    total = _sum_chips(parts, name=name + "_sum4")
    return _share_d2d(total, name=name + "_share").reshape(rows, cols)


BIG = ("w_in", "w_branch", "w_out", "w_up", "w_down")
BIG_COLUMN_SHARDED = ("w_in", "w_up")
CONV = ("conv_a_w", "conv_f_w")
REPLICATED = ("norm1_w", "b_gate", "conv_a_b", "dt_bias", "a_log", "d_skip", "ssd_norm_w", "uv_b", "v_ln_w",
              "v_ln_b", "w_spatial", "b_spatial", "norm2_w", "conv_f_b", "final_norm_w")
WEIGHT_ORDER = ("norm1_w", "w_in", "b_gate", "conv_a_w", "conv_a_b", "dt_bias", "a_log", "d_skip", "ssd_norm_w",
                "uv_b", "v_ln_w", "v_ln_b", "w_spatial", "b_spatial", "w_branch", "w_out", "norm2_w", "w_up",
                "conv_f_w", "conv_f_b", "w_down", "final_norm_w")
SMALL_EXCHANGE_ROWS = 64


_GATE0 = SSD_IN + 2 * SGU_WIDTH
IN_SEGMENTS = {
    "in_z": (0, SSD_D_INNER), "in_xbc": (SSD_D_INNER, SSD_D_INNER + SSD_XBC), "in_dt": (SSD_D_INNER + SSD_XBC, SSD_IN),
    "in_uv": (SSD_IN, _GATE0), "in_gate": (_GATE0, IN_COLS), "in_gate_a": (_GATE0, _GATE0 + D_MODEL),
    "in_gate_b": (_GATE0 + D_MODEL, IN_COLS),
}
IN_GRAD_SEGMENTS = ("in_z", "in_xbc", "in_dt", "in_uv", "in_gate_a", "in_gate_b")


def _take_columns(parts, start, stop):
    out = []
    for a, first in parts:
        lo, hi = max(start, first), min(stop, first + a.shape[1])
        if lo < hi:
            out.append(a[:, lo - first:hi - first])
    return out[0] if len(out) == 1 else jnp.concatenate(out, axis=1)


def _flat_rows(arrays, row_multiple):
    flat = jnp.concatenate([a.reshape(-1) for a in arrays])
    rows = -(-flat.shape[0] // (LANES * row_multiple)) * row_multiple
    return jnp.pad(flat, (0, rows * LANES - flat.shape[0])).reshape(rows, LANES)


def _unflatten(flat, shapes):
    flat = flat.reshape(-1)
    out, off = [], 0
    for shp in shapes:
        n = math.prod(shp)
        out.append(flat[off:off + n].reshape(shp))
        off += n
    return out


def _from_chip_blocks(blocks, name):
    if name in BIG_COLUMN_SHARDED or name in CONV:
        k = blocks.shape[1]
        return jnp.transpose(blocks, (1, 0, 2)).reshape(k, -1)
    return blocks.reshape(-1, blocks.shape[-1])


def _to_chip_blocks(whole, name):
    if name in BIG_COLUMN_SHARDED or name in CONV:
        k, n = whole.shape
        return jnp.transpose(whole.reshape(k, N_CHIPS, n // N_CHIPS), (1, 0, 2))
    return whole.reshape(N_CHIPS, whole.shape[0] // N_CHIPS, whole.shape[1])


def kernel(x, norm1_w, w_in, b_gate, conv_a_w, conv_a_b, dt_bias, a_log, d_skip, ssd_norm_w, uv_b, v_ln_w, v_ln_b, w_spatial, b_spatial, w_branch, w_out, norm2_w, w_up, conv_f_w, conv_f_b, w_down, final_norm_w, loss_target, m_norm1_w, m_w_in, m_b_gate, m_conv_a_w, m_conv_a_b, m_dt_bias, m_a_log, m_d_skip, m_ssd_norm_w, m_uv_b, m_v_ln_w, m_v_ln_b, m_w_spatial, m_b_spatial, m_w_branch, m_w_out, m_norm2_w, m_w_up, m_conv_f_w, m_conv_f_b, m_w_down, m_final_norm_w, v_norm1_w, v_w_in, v_b_gate, v_conv_a_w, v_conv_a_b, v_dt_bias, v_a_log, v_d_skip, v_ssd_norm_w, v_uv_b, v_v_ln_w, v_v_ln_b, v_w_spatial, v_b_spatial, v_w_branch, v_w_out, v_norm2_w, v_w_up, v_conv_f_w, v_conv_f_b, v_w_down, v_final_norm_w):
    weights = dict(norm1_w=norm1_w, w_in=w_in, b_gate=b_gate, conv_a_w=conv_a_w, conv_a_b=conv_a_b, dt_bias=dt_bias,
                   a_log=a_log, d_skip=d_skip, ssd_norm_w=ssd_norm_w, uv_b=uv_b, v_ln_w=v_ln_w, v_ln_b=v_ln_b,
                   w_spatial=w_spatial, b_spatial=b_spatial, w_branch=w_branch, w_out=w_out, norm2_w=norm2_w,
                   w_up=w_up, conv_f_w=conv_f_w, conv_f_b=conv_f_b, w_down=w_down, final_norm_w=final_norm_w)
    mom1 = dict(norm1_w=m_norm1_w, w_in=m_w_in, b_gate=m_b_gate, conv_a_w=m_conv_a_w, conv_a_b=m_conv_a_b,
                dt_bias=m_dt_bias, a_log=m_a_log, d_skip=m_d_skip, ssd_norm_w=m_ssd_norm_w, uv_b=m_uv_b,
                v_ln_w=m_v_ln_w, v_ln_b=m_v_ln_b, w_spatial=m_w_spatial, b_spatial=m_b_spatial, w_branch=m_w_branch,
                w_out=m_w_out, norm2_w=m_norm2_w, w_up=m_w_up, conv_f_w=m_conv_f_w, conv_f_b=m_conv_f_b,
                w_down=m_w_down, final_norm_w=m_final_norm_w)
    mom2 = dict(norm1_w=v_norm1_w, w_in=v_w_in, b_gate=v_b_gate, conv_a_w=v_conv_a_w, conv_a_b=v_conv_a_b,
                dt_bias=v_dt_bias, a_log=v_a_log, d_skip=v_d_skip, ssd_norm_w=v_ssd_norm_w, uv_b=v_uv_b,
                v_ln_w=v_v_ln_w, v_ln_b=v_v_ln_b, w_spatial=v_w_spatial, b_spatial=v_b_spatial, w_branch=v_w_branch,
                w_out=v_w_out, norm2_w=v_norm2_w, w_up=v_w_up, conv_f_w=v_conv_f_w, conv_f_b=v_conv_f_b,
                w_down=v_w_down, final_norm_w=v_final_norm_w)
    chip = 2 * lax.axis_index("x") + lax.axis_index("y")
    core = lax.axis_index("c").astype(jnp.int32).reshape(1)

    whole = {}
    conv_shapes = [weights[n].shape[1:] for n in CONV]
    conv_gathered = _all_gather_chips(_flat_rows([weights[n] for n in CONV], 16), "gather_conv").reshape(N_CHIPS, -1)
    off = 0
    for n, shp in zip(CONV, conv_shapes):
        size = math.prod(shp)
        whole[n] = _from_chip_blocks(conv_gathered[:, off:off + size].reshape((N_CHIPS,) + shp), n)
        off += size
    shard_shapes = {n: weights[n].shape[1:] for n in BIG}
    halves = [weights[n][0].astype(BF16).reshape(2, shard_shapes[n][0] // 2, shard_shapes[n][1]) for n in BIG]
    sends = [_gather_sends if n == "w_in" else _gather_whole_sends for n in BIG]
    gathers, gathers_started = _exchange_start(halves, [(N_CHIPS,) + h.shape for h in halves], sends,
                                               after=conv_gathered, name="gather_start")
    gathers = dict(zip(BIG, gathers))

    def get_weight(name, after):
        rows, cols = shard_shapes[name]
        if name == "w_in":
            own, landed = _exchange_wait(gathers[name], after, _gather_sends, _gather_arrivals,
                                         name="gather_" + name + "_wait")
            landed = _gather_d2d(landed, name="gather_" + name + "_d2d")
        else:
            own, landed = _exchange_wait(gathers[name], after, _gather_whole_sends, _gather_whole_arrivals,
                                         name="gather_" + name + "_wait")
        blocks = lax.dynamic_update_slice(landed.reshape(N_CHIPS, rows, cols), own.reshape(1, rows, cols),
                                          (chip, 0, 0))
        if name == "w_up":
            return {"up": blocks}
        if name == "w_in":
            parts = [(blocks[k], cols * k) for k in range(N_CHIPS)]
            segs = {n: _take_columns(parts, a, b) for n, (a, b) in IN_SEGMENTS.items()}
            segs["in_dt"] = jnp.pad(segs["in_dt"], ((0, 0), (0, LANES - SSD_HEADS)))
            return segs
        full = _from_chip_blocks(blocks, name)
        if name == "w_branch":
            return {"branch_a": full[:SSD_D_INNER], "branch_b": full[SSD_D_INNER:]}
        return {name[2:]: full}

    small = {n: weights[n] for n in REPLICATED}
    small["conv_a_w"] = whole["conv_a_w"]
    small["conv_f_w"] = whole["conv_f_w"]
    small["gathers_started"] = gathers_started

    reductions = {}

    def emit_grad(name, g):
        if name == "w_in":
            parts = [(g[n], IN_SEGMENTS[n][0]) for n in IN_GRAD_SEGMENTS]
            cols = shard_shapes[name][1]
            g_blocks = jnp.stack([_take_columns(parts, cols * k, cols * (k + 1)) for k in range(N_CHIPS)])
        else:
            g_blocks = g if name == "w_up" else _to_chip_blocks(g, name)
        if name == "w_in":
            _, rows, cols = g_blocks.shape
            g_halves = g_blocks.reshape(N_CHIPS, 2, rows // 2, cols)
            arrived = _swap_halves_d2d(g_halves, name="reduce_" + name + "_swap")
            g_blocks = _add_own_half(g_halves, arrived, core, name="reduce_" + name + "_add2")
        (pending,), started = _exchange_start([g_blocks], [g_blocks.shape], _scatter_sends,
                                              name="reduce_" + name + "_start")
        reductions[name] = pending
        return started

    loss, dx, grads_small = _local_step(x[0], loss_target[0], get_weight, small, emit_grad)

    order = ("w_down", "w_up", "w_out", "w_branch", "w_in")
    core_sums = []
    chip_index = chip.astype(jnp.int32).reshape(1)
    for n in order:
        sent, landed = _exchange_wait(reductions[n], dx, _scatter_sends, _scatter_arrivals,
                                      name="reduce_" + n + "_wait")
        core_sums.append(_sum_chips_with_own(landed, sent, chip_index, name="reduce_" + n + "_sum4"))
    swaps, swaps_started = _exchange_start(core_sums, [a.shape for a in core_sums], _sibling_sends, name="reduce_swap_start")
    grads = {}

    small_names = REPLICATED + CONV + ("loss",)
    grads_small = dict(grads_small, loss=loss)
    small_shapes = [grads_small[n].shape for n in small_names]
    g_small = _flat_rows([grads_small[n] for n in small_names], N_CHIPS * 2 * SMALL_EXCHANGE_ROWS)
    red_small = _reduce_scatter_chips(g_small.reshape(N_CHIPS, -1, LANES), core, "reduce_small", after=swaps_started)
    all_small = _all_gather_chips(red_small, "gather_small")
    swapped = _exchange_wait_many(swaps, all_small, _sibling_sends, _sibling_sends, name="reduce_swap_wait")
    core_sums = {n: own for n, (own, _) in zip(order, swapped)}
    sibling_sums = {n: other for n, (_, other) in zip(order, swapped)}
    first = lax.axis_index("c") == 0
    w_in_halves = (core_sums["w_in"], sibling_sums["w_in"])
    w_in_grad = jnp.concatenate([jnp.where(first, w_in_halves[0], w_in_halves[1]),
                                 jnp.where(first, w_in_halves[1], w_in_halves[0])], axis=0)
    for n, g in zip(small_names, _unflatten(all_small, small_shapes)):
        if n == "loss":
            total_loss = g[0, 0]
            continue
        if n in CONV:
            width = g.shape[1] // N_CHIPS
            g = lax.dynamic_slice_in_dim(g, chip * width, width, axis=1)
        grads[n] = g.reshape(weights[n].shape[1:]) if n != "final_norm_w" else g

    delta, new_m, new_v = {}, {}, {}
    for n in BIG:
        shp = weights[n].shape
        if n == "w_in":
            g_t = w_in_grad.T
            results = [g_t] + list(_adamw(weights[n][0].T, g_t, mom1[n][0].T, mom2[n][0].T, name="adamw_" + n,
                                          tr=_row_tile(g_t.shape[0], 8, 136)))
            results = [a.T for a in results]
        else:
            results = _adamw_two_sums(weights[n][0], core_sums[n], sibling_sums[n], mom1[n][0], mom2[n][0],
                                      name="adamw_" + n, tr=_row_tile(shp[1], 8, 352))
        grads[n], delta[n], new_m[n], new_v[n] = [a.reshape(shp) for a in results]
    small_all = [n for n in WEIGHT_ORDER if n not in BIG]

    def as_2d(a):
        return a.reshape(-1, a.shape[-1])

    results = _adamw_many(*[[as_2d(src[n]) for n in small_all] for src in (weights, grads, mom1, mom2)],
                          name="adamw_small")
    for n, dv, mv, vv in zip(small_all, *results):
        shp = weights[n].shape
        delta[n], new_m[n], new_v[n] = dv.reshape(shp), mv.reshape(shp), vv.reshape(shp)

    grad_out = [grads[n].reshape(weights[n].shape) for n in WEIGHT_ORDER]
    return (total_loss, dx[None], *grad_out, *[delta[n] for n in WEIGHT_ORDER], *[new_m[n] for n in WEIGHT_ORDER],
            *[new_v[n] for n in WEIGHT_ORDER])
```

```python
import functools
import math

import jax
import jax.numpy as jnp
from jax import lax
from jax.experimental import pallas as pl
from jax.experimental.pallas import tpu as pltpu

F32 = jnp.float32
BF16 = jnp.bfloat16

D_MODEL = 1024
SSD_D_INNER = 2048
SSD_HEADS = 32
SSD_HEAD_DIM = 64
SSD_GROUPS = 4
SSD_HEADS_PER_GROUP = 8
SSD_STATE = 128
SSD_BC = 512
SSD_XBC = 3072
SSD_IN = 5152
SGU_WIDTH = 1024
SGU_GROUPS = 8
CHUNK = 128
IN_COLS = 9248
D_FF = 2816
NORM_EPS = 1e-6
LN_EPS = 1e-5
GROUP_COLS = SSD_HEADS_PER_GROUP * SSD_HEAD_DIM
LANES = 128

ADAM_LR = 0.001
ADAM_B1 = 0.9
ADAM_B2 = 0.999
ADAM_EPS = 1e-08
ADAM_WD = 0.01
ADAM_STEP = 10

N_CHIPS = 4
VMEM_LIMIT = 56 * 1024 * 1024

NT = (((1,), (1,)), ((), ()))
TN = (((0,), (0,)), ((), ()))
NN = (((1,), (0,)), ((), ()))


def _params(dims):
    return pltpu.CompilerParams(dimension_semantics=dims, vmem_limit_bytes=VMEM_LIMIT)


def _dot(a, b, dn=NN, precision=None):
    return lax.dot_general(a, b, dn, precision=precision, preferred_element_type=F32)


def _split3(x):
    hi = x.astype(BF16)
    rest = x - hi.astype(F32)
    mid = rest.astype(BF16)
    return hi, mid, (rest - mid.astype(F32)).astype(BF16)


def _dot_terms(terms, exact, dn=NN):
    out = None
    for t in terms:
        p = _dot(t, exact, dn)
        out = p if out is None else out + p
    return out


def _dot_exact_lhs(exact, terms):
    out = None
    for t in terms:
        p = _dot(exact, t)
        out = p if out is None else out + p
    return out


def _sigmoid(x):
    return 1.0 / (1.0 + jnp.exp(-x))


def _softplus(x):
    return jnp.maximum(x, 0.0) + jnp.log(1.0 + jnp.exp(-jnp.abs(x)))


def _matmul(pairs, *, trans_b=False, add=None, after=None, out_dtype=F32, tm=512, tn=512, name):
    def mat_shape(b):
        if isinstance(b, tuple) and b[1] == "cols":
            return (b[0].shape[1], b[0].shape[0] * b[0].shape[2])
        return b[0].shape[1:] if isinstance(b, tuple) else b.shape

    if isinstance(pairs[0][1], tuple) and pairs[0][1][1] == "cols":
        assert not trans_b and tn % LANES == 0 and pairs[0][1][0].shape[2] % tn == 0, name

    m = (pairs[0][0][0] if isinstance(pairs[0][0], tuple) else pairs[0][0]).shape[0]
    n = mat_shape(pairs[0][1])[0] if trans_b else mat_shape(pairs[0][1])[1]
    tm, tn = min(tm, m), min(tn, n)
    assert m % tm == 0 and n % tn == 0, (name, m, n, tm, tn)
    npairs = len(pairs)
    dn = NT if trans_b else NN

    def body(*refs):
        o_ref = refs[-1]
        acc = None
        for i in range(npairs):
            p = _dot(refs[2 * i][...].astype(BF16), refs[2 * i + 1][...].astype(BF16), dn)
            acc = p if acc is None else acc + p
        if add is not None:
            acc = acc + refs[2 * npairs][...]
        o_ref[...] = acc.astype(out_dtype)

    in_specs, args = [], []
    for a, b in pairs:
        bshape = mat_shape(b)
        k = bshape[1] if trans_b else bshape[0]
        assert bshape == ((n, k) if trans_b else (k, n)), (name, bshape)
        a, qa = a if isinstance(a, tuple) else (a, 0)
        assert a.shape[0] == m and a.shape[1] % k == 0, (name, a.shape, k)
        in_specs.append(pl.BlockSpec((tm, k), lambda i, j, qa=qa: (i, qa)))
        if isinstance(b, tuple) and b[1] == "cols":
            b = b[0]
            per = b.shape[2] // tn
            in_specs.append(pl.BlockSpec((None, k, tn), lambda i, j, per=per: (j // per, 0, j % per)))
        elif isinstance(b, tuple):
            b, qb = b
            if trans_b:
                in_specs.append(pl.BlockSpec((None, tn, k), lambda i, j, qb=qb: (qb, j, 0)))
            else:
                in_specs.append(pl.BlockSpec((None, k, tn), lambda i, j, qb=qb: (qb, 0, j)))
        elif trans_b:
            in_specs.append(pl.BlockSpec((tn, k), lambda i, j: (j, 0)))
        else:
            in_specs.append(pl.BlockSpec((k, tn), lambda i, j: (0, j)))
        args += [a, b]
    if add is not None:
        in_specs.append(pl.BlockSpec((tm, tn), lambda i, j: (i, j)))
        args.append(add)
    if after is not None:
        in_specs.append(pl.BlockSpec(memory_space=pl.ANY))
        args.append(after)
    return pl.pallas_call(
        body, name=name, grid=(m // tm, n // tn), in_specs=in_specs,
        out_specs=pl.BlockSpec((tm, tn), lambda i, j: (i, j)),
        out_shape=jax.ShapeDtypeStruct((m, n), out_dtype),
        compiler_params=_params(("parallel", "parallel")),
    )(*args)


def _matmul_tn(a, b, *, tk, tn, tm=1024, out_dtype=BF16, stack_out=False, after=None, part_of=None, name):
    m, k = a.shape
    n = b.shape[1]
    tm, tk, tn = min(tm, m), min(tk, k), min(tn, n)
    assert m % tm == 0 and k % tk == 0 and n % tn == 0, (name, m, k, n)
    nm = m // tm
    blocks, first, buffer = part_of if part_of is not None else (None, 0, None)
    if stack_out:
        out_spec = pl.BlockSpec((None, tk, tn), lambda i, j, l: (j + first, i, 0))
        out_shape = jax.ShapeDtypeStruct((blocks or n // tn, k, tn), out_dtype)
    else:
        out_spec = pl.BlockSpec((tk, tn), lambda i, j, l: (i + first, j))
        out_shape = jax.ShapeDtypeStruct((blocks * tk if blocks else k, n), out_dtype)

    def body(a_ref, b_ref, *rest):
        o_ref, acc = rest[-2:]
        mi = pl.program_id(2)

        @pl.when(mi == 0)
        def _():
            acc[...] = jnp.zeros_like(acc)

        acc[...] += _dot(a_ref[...].astype(BF16), b_ref[...].astype(BF16), TN)

        @pl.when(mi == nm - 1)
        def _():
            o_ref[...] = acc[...].astype(out_dtype)

    in_specs = [pl.BlockSpec((tm, tk), lambda i, j, l: (l, i)), pl.BlockSpec((tm, tn), lambda i, j, l: (l, j))]
    args = [a, b]
    if after is not None:
        in_specs.append(pl.BlockSpec(memory_space=pl.ANY))
        args.append(after)
    aliases = {}
    if buffer is not None:
        aliases = {len(args): 0}
        in_specs.append(pl.BlockSpec(memory_space=pl.ANY))
        args.append(buffer)
    return pl.pallas_call(
        body, name=name, grid=(k // tk, n // tn, nm), in_specs=in_specs,
        out_specs=out_spec, out_shape=out_shape, input_output_aliases=aliases,
        scratch_shapes=[pltpu.VMEM((tk, tn), F32)],
        compiler_params=_params(("parallel", "parallel", "arbitrary")),
    )(*args)


def _rms_fwd(x, w, *, after=None, name, tm=512):
    s, d = x.shape
    tm = min(tm, s)
    extra = [] if after is None else [after]

    def body(x_ref, w_ref, *rest):
        o_ref = rest[-1]
        xv = x_ref[...]
        r = lax.rsqrt(jnp.mean(xv * xv, axis=-1, keepdims=True) + NORM_EPS)
        o_ref[...] = (xv * r * w_ref[...]).astype(BF16)

    return pl.pallas_call(
        body, name=name, grid=(s // tm,),
        in_specs=[pl.BlockSpec((tm, d), lambda i: (i, 0)), pl.BlockSpec((1, d), lambda i: (0, 0))]
        + [pl.BlockSpec(memory_space=pl.ANY)] * len(extra),
        out_specs=pl.BlockSpec((tm, d), lambda i: (i, 0)),
        out_shape=jax.ShapeDtypeStruct((s, d), BF16),
        compiler_params=_params(("parallel",)),
    )(x, w, *extra)


def _rms_bwd(x, w, dn, dres, *, name, tm=512):
    s, d = x.shape
    tm = min(tm, s)

    def body(x_ref, w_ref, dn_ref, dres_ref, dx_ref, dxb_ref, dw_ref):
        @pl.when(pl.program_id(0) == 0)
        def _():
            dw_ref[...] = jnp.zeros_like(dw_ref)

        xv = x_ref[...]
        r = lax.rsqrt(jnp.mean(xv * xv, axis=-1, keepdims=True) + NORM_EPS)
        xhat = xv * r
        dnv = dn_ref[...].astype(F32)
        dxhat = dnv * w_ref[...]
        dx = dres_ref[...] + r * (dxhat - xhat * jnp.mean(dxhat * xhat, axis=-1, keepdims=True))
        dx_ref[...] = dx
        dxb_ref[...] = dx.astype(BF16)
        dw_ref[...] += jnp.sum(dnv * xhat, axis=0, keepdims=True)

    tile = pl.BlockSpec((tm, d), lambda i: (i, 0))
    row = pl.BlockSpec((1, d), lambda i: (0, 0))
    return pl.pallas_call(
        body, name=name, grid=(s // tm,),
        in_specs=[tile, row, tile, tile], out_specs=[tile, tile, row],
        out_shape=[jax.ShapeDtypeStruct((s, d), F32), jax.ShapeDtypeStruct((s, d), BF16),
                   jax.ShapeDtypeStruct((1, d), F32)],
        compiler_params=_params(("arbitrary",)),
    )(x, w, dn, dres)


def _final_fwd_bwd(h2, wf, target, *, name, tm=512):
    s, d = h2.shape
    tm = min(tm, s)

    def body(h_ref, w_ref, t_ref, loss_ref, dh_ref, dhb_ref, dw_ref):
        @pl.when(pl.program_id(0) == 0)
        def _():
            dw_ref[...] = jnp.zeros_like(dw_ref)
            loss_ref[...] = jnp.zeros_like(loss_ref)

        hv = h_ref[...]
        r = lax.rsqrt(jnp.mean(hv * hv, axis=-1, keepdims=True) + NORM_EPS)
        xhat = hv * r
        err = xhat * w_ref[...] - t_ref[...]
        per_tok = jnp.mean(err * err, axis=-1, keepdims=True)
        loss_ref[...] += 0.5 * jnp.sum(per_tok, axis=0, keepdims=True)
        dy = err * (1.0 / d)
        dxhat = dy * w_ref[...]
        dh = r * (dxhat - xhat * jnp.mean(dxhat * xhat, axis=-1, keepdims=True))
        dh_ref[...] = dh
        dhb_ref[...] = dh.astype(BF16)
        dw_ref[...] += jnp.sum(dy * xhat, axis=0, keepdims=True)

    tile = pl.BlockSpec((tm, d), lambda i: (i, 0))
    row = pl.BlockSpec((1, d), lambda i: (0, 0))
    return pl.pallas_call(
        body, name=name, grid=(s // tm,),
        in_specs=[tile, row, tile],
        out_specs=[pl.BlockSpec((1, 1), lambda i: (0, 0)), tile, tile, row],
        out_shape=[jax.ShapeDtypeStruct((1, 1), F32), jax.ShapeDtypeStruct((s, d), F32),
                   jax.ShapeDtypeStruct((s, d), BF16), jax.ShapeDtypeStruct((1, d), F32)],
        compiler_params=_params(("arbitrary",)),
    )(h2, wf, target)


CONV_ROWS = 256
CONV_ROWS_FWD = 512
HALO = 8


def _rows_with_halo(ref, r0, rows, s, before, after):
    tile = 16 if ref.dtype == BF16 else HALO
    parts = []
    if before:
        prev = ref[pl.ds(pl.multiple_of(jnp.maximum(r0 - tile, 0), tile), tile), :].astype(F32)[tile - HALO:]
        parts.append(jnp.where(r0 > 0, prev, 0.0))
    parts.append(ref[pl.ds(r0, rows), :].astype(F32))
    if after:
        nxt = ref[pl.ds(pl.multiple_of(jnp.minimum(r0 + rows, s - tile), tile), tile), :].astype(F32)[:HALO]
        parts.append(jnp.where(r0 + rows < s, nxt, 0.0))
    return jnp.concatenate(parts, axis=0) if len(parts) > 1 else parts[0]


def _window(x_ref, r0, s, after):
    return _rows_with_halo(x_ref, r0, CONV_ROWS_FWD, s, True, after).astype(F32)


def _shifted(window, k, rows):
    if k == 0:
        return window[HALO:HALO + rows]
    return pltpu.roll(window, k, 0)[HALO:HALO + rows]


def _conv_taps(window, w_ref, kk, rows):
    acc = None
    for i in range(kk):
        term = w_ref[i:i + 1, :] * _shifted(window, kk - 1 - i, rows)
        acc = term if acc is None else acc + term
    return acc


def _row_loop(rows, step):
    def body(r, carry):
        return step(pl.multiple_of(r * rows, rows), carry)
    return body


def _conv_bwd_rows(x, dpe, w_ref, kk):
    dp = dpe[:CONV_ROWS]
    dx = None
    dws = []
    for i in range(kk):
        k = kk - 1 - i
        later = dp if k == 0 else pltpu.roll(dpe, dpe.shape[0] - k, 0)[:CONV_ROWS]
        dws.append(jnp.sum(later * x, axis=0, keepdims=True))
        term = w_ref[i:i + 1, :] * later
        dx = term if dx is None else dx + term
    return dx, dws, jnp.sum(dp, axis=0, keepdims=True)


def _conv_a_fwd(xraw, w, b, *, name, tc=128):
    s, c = xraw.shape
    kk = 4

    def body(x_ref, w_ref, b_ref, o_ref, pre_ref):
        def step(r0, carry):
            pre = _conv_taps(_window(x_ref, r0, s, False), w_ref, kk, CONV_ROWS_FWD) + b_ref[...]
            o_ref[pl.ds(r0, CONV_ROWS_FWD), :] = pre * _sigmoid(pre)
            pre_ref[pl.ds(r0, CONV_ROWS_FWD), :] = pre.astype(BF16)
            return carry

        lax.fori_loop(0, s // CONV_ROWS_FWD, _row_loop(CONV_ROWS_FWD, step), 0)

    col = pl.BlockSpec((s, tc), lambda j: (0, j))
    return pl.pallas_call(
        body, name=name, grid=(c // tc,),
        in_specs=[col, pl.BlockSpec((8, tc), lambda j: (0, j)), pl.BlockSpec((1, tc), lambda j: (0, j))],
        out_specs=[col, col], out_shape=[jax.ShapeDtypeStruct((s, c), F32), jax.ShapeDtypeStruct((s, c), BF16)],
        compiler_params=_params(("parallel",)),
    )(xraw, w, b)


def _conv_a_bwd(xraw, pre, w, dy, *, name, tc=128):
    s, c = xraw.shape
    kk = 4

    def body(x_ref, pre_ref, w_ref, dy_ref, dx_ref, dw_ref, db_ref):
        def step(r0, carry):
            pre = _rows_with_halo(pre_ref, r0, CONV_ROWS, s, False, True)
            sg = _sigmoid(pre)
            dpe = _rows_with_halo(dy_ref, r0, CONV_ROWS, s, False, True) * (sg * (1.0 + pre * (1.0 - sg)))
            dx, dws, db = _conv_bwd_rows(x_ref[pl.ds(r0, CONV_ROWS), :].astype(F32), dpe, w_ref, kk)
            dx_ref[pl.ds(r0, CONV_ROWS), :] = dx.astype(BF16)
            return tuple(acc + new for acc, new in zip(carry, dws + [db]))

        zero = jnp.zeros((1, tc), F32)
        sums = lax.fori_loop(0, s // CONV_ROWS, _row_loop(CONV_ROWS, step), (zero,) * (kk + 1))
        db_ref[...] = sums[kk]
        dw_ref[...] = jnp.concatenate(list(sums[:kk]) + [jnp.zeros((8 - kk, tc), F32)], axis=0)

    col = pl.BlockSpec((s, tc), lambda j: (0, j))
    w8 = pl.BlockSpec((8, tc), lambda j: (0, j))
    row = pl.BlockSpec((1, tc), lambda j: (0, j))
    return pl.pallas_call(
        body, name=name, grid=(c // tc,),
        in_specs=[col, col, w8, col], out_specs=[col, w8, row],
        out_shape=[jax.ShapeDtypeStruct((s, c), BF16), jax.ShapeDtypeStruct((8, c), F32),
                   jax.ShapeDtypeStruct((1, c), F32)],
        compiler_params=_params(("parallel",)),
    )(xraw, pre, w, dy)


def _conv_f_fwd(up_raw, w, b, *, name, tc=128):
    s, c2 = up_raw.shape
    c = c2 // 2
    nb = c // tc
    kk = 3

    def body(xa_ref, xv_ref, wa_ref, wv_ref, ba_ref, bv_ref, o_ref, a_out, v_out):
        def step(r0, carry):
            a = _conv_taps(_window(xa_ref, r0, s, False), wa_ref, kk, CONV_ROWS_FWD) + ba_ref[...]
            v = _conv_taps(_window(xv_ref, r0, s, False), wv_ref, kk, CONV_ROWS_FWD) + bv_ref[...]
            o_ref[pl.ds(r0, CONV_ROWS_FWD), :] = (a * _sigmoid(a) * v).astype(BF16)
            a_out[pl.ds(r0, CONV_ROWS_FWD), :] = a.astype(BF16)
            v_out[pl.ds(r0, CONV_ROWS_FWD), :] = v.astype(BF16)
            return carry

        lax.fori_loop(0, s // CONV_ROWS_FWD, _row_loop(CONV_ROWS_FWD, step), 0)

    col_a = pl.BlockSpec((s, tc), lambda j: (0, j))
    col_v = pl.BlockSpec((s, tc), lambda j: (0, j + nb))
    half = jax.ShapeDtypeStruct((s, c), BF16)
    return pl.pallas_call(
        body, name=name, grid=(nb,),
        in_specs=[col_a, col_v, pl.BlockSpec((8, tc), lambda j: (0, j)), pl.BlockSpec((8, tc), lambda j: (0, j + nb)),
                  pl.BlockSpec((1, tc), lambda j: (0, j)), pl.BlockSpec((1, tc), lambda j: (0, j + nb))],
        out_specs=[col_a, col_a, col_a], out_shape=[half, half, half],
        compiler_params=_params(("parallel",)),
    )(up_raw, up_raw, w, w, b, b)


def _conv_f_bwd(up_raw, a_pre, v_pre, w, dact, *, name, tc=128):
    s, c2 = up_raw.shape
    c = c2 // 2
    nb = c // tc
    kk = 3

    def body(xa_ref, xv_ref, a_ref, v_ref, wa_ref, wv_ref, d_ref,
             dxa_ref, dxv_ref, dwa_ref, dwv_ref, dba_ref, dbv_ref):
        def step(r0, carry):
            a = _rows_with_halo(a_ref, r0, CONV_ROWS, s, False, True)
            v = _rows_with_halo(v_ref, r0, CONV_ROWS, s, False, True)
            sg = _sigmoid(a)
            d = _rows_with_halo(d_ref, r0, CONV_ROWS, s, False, True)
            rows = pl.ds(r0, CONV_ROWS)
            dxa, dwas, dba = _conv_bwd_rows(xa_ref[rows, :].astype(F32), d * v * (sg * (1.0 + a * (1.0 - sg))),
                                            wa_ref, kk)
            dxv, dwvs, dbv = _conv_bwd_rows(xv_ref[rows, :].astype(F32), d * (a * sg), wv_ref, kk)
            dxa_ref[pl.ds(r0, CONV_ROWS), :] = dxa.astype(BF16)
            dxv_ref[pl.ds(r0, CONV_ROWS), :] = dxv.astype(BF16)
            return tuple(acc + new for acc, new in zip(carry, dwas + [dba] + dwvs + [dbv]))

        zero = jnp.zeros((1, tc), F32)
        sums = lax.fori_loop(0, s // CONV_ROWS, _row_loop(CONV_ROWS, step), (zero,) * (2 * kk + 2))
        pad = [jnp.zeros((8 - kk, tc), F32)]
        dwa_ref[...] = jnp.concatenate(list(sums[:kk]) + pad, axis=0)
        dba_ref[...] = sums[kk]
        dwv_ref[...] = jnp.concatenate(list(sums[kk + 1:2 * kk + 1]) + pad, axis=0)
        dbv_ref[...] = sums[2 * kk + 1]

    col_a = pl.BlockSpec((s, tc), lambda j: (0, j))
    col_v = pl.BlockSpec((s, tc), lambda j: (0, j + nb))
    w_a = pl.BlockSpec((8, tc), lambda j: (0, j))
    w_v = pl.BlockSpec((8, tc), lambda j: (0, j + nb))
    r_a = pl.BlockSpec((1, tc), lambda j: (0, j))
    r_v = pl.BlockSpec((1, tc), lambda j: (0, j + nb))
    outs = pl.pallas_call(
        body, name=name, grid=(nb,),
        in_specs=[col_a, col_v, col_a, col_a, w_a, w_v, col_a],
        out_specs=[col_a, col_a, w_a, w_a, r_a, r_a],
        out_shape=[jax.ShapeDtypeStruct((s, c), BF16), jax.ShapeDtypeStruct((s, c), BF16),
                   jax.ShapeDtypeStruct((8, c), F32), jax.ShapeDtypeStruct((8, c), F32),
                   jax.ShapeDtypeStruct((1, c), F32), jax.ShapeDtypeStruct((1, c), F32)],
        compiler_params=_params(("parallel",)),
    )(up_raw, up_raw, a_pre, v_pre, w, w, dact)
    return outs


def _tri_masks():
    row = lax.broadcasted_iota(jnp.int32, (CHUNK, CHUNK), 0)
    col = lax.broadcasted_iota(jnp.int32, (CHUNK, CHUNK), 1)
    return row >= col, row <= col


def _ssd_fwd(xbc, dt_raw, z, dt_bias, a_log, a_log_x, d_skip_x, norm_w, expand, *, name):
    s = xbc.shape[0]
    nc = s // CHUNK

    def body(xbc_ref, dtr_ref, z_ref, dtb_ref, alog_ref, alogx_ref, dskx_ref, nw_ref, e_ref,
             y_ref, ya_ref, st_ref, state):
        @pl.when(pl.program_id(0) == 0)
        def _():
            state[...] = jnp.zeros_like(state)

        st_ref[0] = state[...]
        lower, _ = _tri_masks()
        dt = _softplus(dtr_ref[...] + dtb_ref[...])
        adt = dt * (-jnp.exp(alog_ref[...]))
        acum = _dot_exact_lhs(lower.astype(BF16), _split3(adt))
        acum_t = acum.T
        dt_terms, acum_terms = _split3(dt), _split3(acum)
        for g in range(SSD_GROUPS):
            sl = slice(GROUP_COLS * g, GROUP_COLS * (g + 1))
            dt_x = _dot_terms(dt_terms[:2], e_ref[:, sl])
            acum_x = _dot_terms(acum_terms, e_ref[:, sl])
            tot_x = jnp.sum(dt_x * (-jnp.exp(alogx_ref[:, sl])), axis=0, keepdims=True)
            xs = xbc_ref[:, sl]
            xdt = xs * dt_x
            xdt_b = xdt.astype(BF16)
            bg = xbc_ref[:, SSD_D_INNER + SSD_STATE * g:SSD_D_INNER + SSD_STATE * (g + 1)].astype(BF16)
            cg = xbc_ref[:, SSD_D_INNER + SSD_BC + SSD_STATE * g:SSD_D_INNER + SSD_BC + SSD_STATE * (g + 1)].astype(BF16)
            cb = _dot(cg, bg, NT)
            st_g = state[:, sl]
            y_off = _dot(cg, st_g.astype(BF16)) * jnp.exp(acum_x)
            parts = []
            for r in range(SSD_HEADS_PER_GROUP):
                h = SSD_HEADS_PER_GROUP * g + r
                dec = jnp.exp(jnp.where(lower, acum[:, h:h + 1] - acum_t[h:h + 1, :], -jnp.inf))
                parts.append(_dot((cb * dec).astype(BF16), xdt_b[:, SSD_HEAD_DIM * r:SSD_HEAD_DIM * (r + 1)]))
            y_ref[:, sl] = jnp.concatenate(parts, axis=1) + y_off + dskx_ref[:, sl] * xs
            wgt = (xdt * jnp.exp(tot_x - acum_x)).astype(BF16)
            state[:, sl] = st_g * jnp.exp(tot_x) + _dot(bg, wgt, TN)
        zv = z_ref[...].astype(F32)
        q = y_ref[...] * (zv * _sigmoid(zv))
        r = lax.rsqrt(jnp.mean(q * q, axis=-1, keepdims=True) + NORM_EPS)
        ya_ref[...] = (q * r * nw_ref[...]).astype(BF16)

    def chunk(w):
        return pl.BlockSpec((CHUNK, w), lambda c: (c, 0))

    def const(shape):
        return pl.BlockSpec(shape, lambda c: (0,) * len(shape))

    return pl.pallas_call(
        body, name=name, grid=(nc,),
        in_specs=[chunk(SSD_XBC), chunk(LANES), chunk(SSD_D_INNER), const((1, LANES)), const((1, LANES)),
                  const((1, SSD_D_INNER)), const((1, SSD_D_INNER)), const((1, SSD_D_INNER)),
                  const((LANES, SSD_D_INNER))],
        out_specs=[chunk(SSD_D_INNER), chunk(SSD_D_INNER),
                   pl.BlockSpec((1, SSD_STATE, SSD_D_INNER), lambda c: (c, 0, 0))],
        out_shape=[jax.ShapeDtypeStruct((s, SSD_D_INNER), F32), jax.ShapeDtypeStruct((s, SSD_D_INNER), BF16),
                   jax.ShapeDtypeStruct((nc, SSD_STATE, SSD_D_INNER), F32)],
        scratch_shapes=[pltpu.VMEM((SSD_STATE, SSD_D_INNER), F32)],
        compiler_params=_params(("arbitrary",)),
    )(xbc, dt_raw, z, dt_bias, a_log, a_log_x, d_skip_x, norm_w, expand)


def _ssd_bwd(dya, y, z, xbc, dt_raw, states, dt_bias, a_log, a_log_x, d_skip_x, norm_w, expand, expand_t, *, name):
    s = xbc.shape[0]
    nc = s // CHUNK

    def body(dya_ref, y_ref, z_ref, xbc_ref, dtr_ref, stp_ref, dtb_ref, alog_ref, alogx_ref, dskx_ref, nw_ref,
             e_ref, et_ref, dz_ref, dxbc_ref, ddt_ref, dnw_ref, ddsk_ref, dalog_ref, ddtb_ref,
             dstate, dy_sc, dskcol):
        i = pl.program_id(0)

        @pl.when(i == 0)
        def _():
            dstate[...] = jnp.zeros_like(dstate)
            dskcol[...] = jnp.zeros_like(dskcol)
            dnw_ref[...] = jnp.zeros_like(dnw_ref)
            dalog_ref[...] = jnp.zeros_like(dalog_ref)
            ddtb_ref[...] = jnp.zeros_like(ddtb_ref)
            ddsk_ref[...] = jnp.zeros_like(ddsk_ref)

        lower, upper = _tri_masks()
        rows = lax.broadcasted_iota(jnp.int32, (CHUNK, LANES), 0)
        pre = dtr_ref[...] + dtb_ref[...]
        dt = _softplus(pre)
        a = -jnp.exp(alog_ref[...])
        acum = _dot_exact_lhs(lower.astype(BF16), _split3(dt * a))
        acum_t = acum.T
        dt_terms, acum_terms = _split3(dt), _split3(acum)

        yv = y_ref[...]
        zv = z_ref[...].astype(F32)
        sz = _sigmoid(zv)
        silu_z = zv * sz
        q = yv * silu_z
        r = lax.rsqrt(jnp.mean(q * q, axis=-1, keepdims=True) + NORM_EPS)
        qhat = q * r
        dyav = dya_ref[...]
        dqhat = dyav * nw_ref[...]
        dnw_ref[...] += jnp.sum(dyav * qhat, axis=0, keepdims=True)
        dq = r * (dqhat - qhat * jnp.mean(dqhat * qhat, axis=-1, keepdims=True))
        dy_sc[...] = dq * silu_z
        dz_ref[...] = (dq * yv * (sz * (1.0 + zv * (1.0 - sz)))).astype(BF16)

        da_cum = jnp.zeros((CHUNK, LANES), F32)
        ddt = jnp.zeros((CHUNK, LANES), F32)
        for g in range(SSD_GROUPS):
            sl = slice(GROUP_COLS * g, GROUP_COLS * (g + 1))
            et_g = et_ref[sl, :]
            dt_x = _dot_terms(dt_terms[:2], e_ref[:, sl])
            acum_x = _dot_terms(acum_terms, e_ref[:, sl])
            tot_x = jnp.sum(dt_x * (-jnp.exp(alogx_ref[:, sl])), axis=0, keepdims=True)
            e_tot = jnp.exp(tot_x)
            dec_s = jnp.exp(tot_x - acum_x)
            xs = xbc_ref[:, sl]
            xdt = xs * dt_x
            xdt_b = xdt.astype(BF16)
            dy = dy_sc[:, sl]
            dy_b = dy.astype(BF16)
            dskx = dskx_ref[:, sl]
            y_ssd = y_ref[:, sl] - dskx * xs
            dskcol[:, sl] += jnp.sum(dy * xs, axis=0, keepdims=True)
            bg = xbc_ref[:, SSD_D_INNER + SSD_STATE * g:SSD_D_INNER + SSD_STATE * (g + 1)].astype(BF16)
            cg = xbc_ref[:, SSD_D_INNER + SSD_BC + SSD_STATE * g:SSD_D_INNER + SSD_BC + SSD_STATE * (g + 1)].astype(BF16)
            cb_t = _dot(bg, cg, NT)
            sp = stp_ref[0, :, sl]
            ds_g = dstate[:, sl]
            ds_b = ds_g.astype(BF16)
            dye_b = (dy * jnp.exp(acum_x)).astype(BF16)
            dc = _dot(dye_b, sp.astype(BF16), NT)
            dxdt_state = dec_s * _dot(bg, ds_b)
            db = _dot((xdt * dec_s).astype(BF16), ds_b, NT)
            dcb_t = jnp.zeros((CHUNK, CHUNK), F32)
            parts = []
            for rr in range(SSD_HEADS_PER_GROUP):
                h = SSD_HEADS_PER_GROUP * g + rr
                hs = slice(SSD_HEAD_DIM * rr, SSD_HEAD_DIM * (rr + 1))
                dec_t = jnp.exp(jnp.where(upper, acum_t[h:h + 1, :] - acum[:, h:h + 1], -jnp.inf))
                parts.append(_dot((cb_t * dec_t).astype(BF16), dy_b[:, hs]))
                dcb_t = dcb_t + _dot(xdt_b[:, hs], dy_b[:, hs], NT) * dec_t
            dxdt = jnp.concatenate(parts, axis=1) + dxdt_state
            dcb_tb = dcb_t.astype(BF16)
            dc = dc + _dot(dcb_tb, bg, TN)
            db = db + _dot(dcb_tb, cg)
            tot_col = jnp.sum(ds_g * sp, axis=0, keepdims=True) * e_tot + jnp.sum(dxdt_state * xdt, axis=0, keepdims=True)
            d_tot = _dot_terms(_split3(jnp.broadcast_to(tot_col, (8, GROUP_COLS))), et_g)
            d_tot = jnp.max(d_tot, axis=0, keepdims=True)
            pair_sums = dy_b.astype(F32) * y_ssd - xdt_b.astype(F32) * dxdt
            da_cum = da_cum + _dot_terms(_split3(pair_sums), et_g) + jnp.where(rows == CHUNK - 1, d_tot, 0.0)
            ddt = ddt + _dot_terms(_split3(dxdt * xs)[:2], et_g)
            dxbc_ref[:, sl] = dy * dskx + dxdt * dt_x
            dxbc_ref[:, SSD_D_INNER + SSD_STATE * g:SSD_D_INNER + SSD_STATE * (g + 1)] = db
            dxbc_ref[:, SSD_D_INNER + SSD_BC + SSD_STATE * g:SSD_D_INNER + SSD_BC + SSD_STATE * (g + 1)] = dc
            dstate[:, sl] = e_tot * ds_g + _dot(cg, dye_b, TN)

        dadt = _dot_exact_lhs(upper.astype(BF16), _split3(da_cum))
        ddt = ddt + dadt * a
        dalog_ref[...] += jnp.sum(dadt * dt, axis=0, keepdims=True)
        dpre = ddt * _sigmoid(pre)
        ddtb_ref[...] += jnp.sum(dpre, axis=0, keepdims=True)
        ddt_ref[...] = dpre.astype(BF16)

        @pl.when(i == nc - 1)
        def _():
            dalog_ref[...] = dalog_ref[...] * a
            dsk = _dot_terms(_split3(jnp.broadcast_to(dskcol[...], (8, SSD_D_INNER))), et_ref[...])
            ddsk_ref[...] = jnp.max(dsk, axis=0, keepdims=True)

    def chunk(w):
        return pl.BlockSpec((CHUNK, w), lambda i: (nc - 1 - i, 0))

    def const(shape):
        return pl.BlockSpec(shape, lambda i: (0,) * len(shape))

    return pl.pallas_call(
        body, name=name, grid=(nc,),
        in_specs=[chunk(SSD_D_INNER), chunk(SSD_D_INNER), chunk(SSD_D_INNER), chunk(SSD_XBC), chunk(LANES),
                  pl.BlockSpec((1, SSD_STATE, SSD_D_INNER), lambda i: (nc - 1 - i, 0, 0)),
                  const((1, LANES)), const((1, LANES)), const((1, SSD_D_INNER)), const((1, SSD_D_INNER)),
                  const((1, SSD_D_INNER)), const((LANES, SSD_D_INNER)), const((SSD_D_INNER, LANES))],
        out_specs=[chunk(SSD_D_INNER), chunk(SSD_XBC), chunk(LANES), const((1, SSD_D_INNER)), const((1, LANES)),
                   const((1, LANES)), const((1, LANES))],
        out_shape=[jax.ShapeDtypeStruct((s, SSD_D_INNER), BF16), jax.ShapeDtypeStruct((s, SSD_XBC), F32),
                   jax.ShapeDtypeStruct((s, LANES), BF16), jax.ShapeDtypeStruct((1, SSD_D_INNER), F32),
                   jax.ShapeDtypeStruct((1, LANES), F32), jax.ShapeDtypeStruct((1, LANES), F32),
                   jax.ShapeDtypeStruct((1, LANES), F32)],
        scratch_shapes=[pltpu.VMEM((SSD_STATE, SSD_D_INNER), F32), pltpu.VMEM((CHUNK, SSD_D_INNER), F32),
                        pltpu.VMEM((1, SSD_D_INNER), F32)],
        compiler_params=_params(("arbitrary",)),
    )(dya, y, z, xbc, dt_raw, states, dt_bias, a_log, a_log_x, d_skip_x, norm_w, expand, expand_t)


GELU_K = math.sqrt(2.0 / math.pi)
GELU_C = 0.044715


def _gelu(x):
    return 0.5 * x * (1.0 + jnp.tanh(GELU_K * (x + GELU_C * x * x * x)))


def _gelu_grad(x):
    t = jnp.tanh(GELU_K * (x + GELU_C * x * x * x))
    return 0.5 * (1.0 + t) + 0.5 * x * (1.0 - t * t) * (GELU_K * (1.0 + 3.0 * GELU_C * x * x))


def _sgu_pre(uv_ref, uvb_ref, lnw_ref, lnb_ref):
    uv = uv_ref[...].astype(F32) + uvb_ref[...]
    guv = _gelu(uv)
    u = guv[:, :SGU_WIDTH]
    v = guv[:, SGU_WIDTH:]
    mu = jnp.mean(v, axis=-1, keepdims=True)
    vc = v - mu
    rstd = lax.rsqrt(jnp.mean(vc * vc, axis=-1, keepdims=True) + LN_EPS)
    vhat = vc * rstd
    vn = vhat * lnw_ref[...] + lnb_ref[...]
    return uv, u, vhat, rstd, vn


def _sgu_fwd(uv_raw, uv_b, ln_w, ln_b, w_sp, b_sp_t, *, name):
    s = uv_raw.shape[0]
    nc = s // CHUNK

    def body(uv_ref, uvb_ref, lnw_ref, lnb_ref, w_ref, bt_ref, o_ref):
        lower, _ = _tri_masks()
        _, u, _, _, vn = _sgu_pre(uv_ref, uvb_ref, lnw_ref, lnb_ref)
        vn_b = vn.astype(BF16)
        bt = bt_ref[...]
        for g in range(SGU_GROUPS):
            gs = slice(LANES * g, LANES * (g + 1))
            wc = jnp.where(lower, w_ref[g], 0.0).astype(BF16)
            mixed = _dot(wc, vn_b[:, gs]) + bt[:, g:g + 1]
            o_ref[:, gs] = (u[:, gs] * mixed).astype(BF16)

    def const(shape):
        return pl.BlockSpec(shape, lambda c: (0,) * len(shape))

    return pl.pallas_call(
        body, name=name, grid=(nc,),
        in_specs=[pl.BlockSpec((CHUNK, 2 * SGU_WIDTH), lambda c: (c, 0)), const((1, 2 * SGU_WIDTH)),
                  const((1, SGU_WIDTH)), const((1, SGU_WIDTH)), const((SGU_GROUPS, CHUNK, CHUNK)),
                  const((CHUNK, LANES))],
        out_specs=pl.BlockSpec((CHUNK, SGU_WIDTH), lambda c: (c, 0)),
        out_shape=jax.ShapeDtypeStruct((s, SGU_WIDTH), BF16),
        compiler_params=_params(("parallel",)),
    )(uv_raw, uv_b, ln_w, ln_b, w_sp, b_sp_t)


def _sgu_bwd(uv_raw, dyb, uv_b, ln_w, ln_b, w_sp, b_sp_t, group_sum, *, name):
    s = uv_raw.shape[0]
    nc = s // CHUNK

    def body(uv_ref, dy_ref, uvb_ref, lnw_ref, lnb_ref, w_ref, bt_ref, gsum_ref,
             duv_ref, dw_ref, dbt_ref, dlnw_ref, dlnb_ref, duvb_ref):
        @pl.when(pl.program_id(0) == 0)
        def _():
            dw_ref[...] = jnp.zeros_like(dw_ref)
            dbt_ref[...] = jnp.zeros_like(dbt_ref)
            dlnw_ref[...] = jnp.zeros_like(dlnw_ref)
            dlnb_ref[...] = jnp.zeros_like(dlnb_ref)
            duvb_ref[...] = jnp.zeros_like(duvb_ref)

        lower, _ = _tri_masks()
        uv, u, vhat, rstd, vn = _sgu_pre(uv_ref, uvb_ref, lnw_ref, lnb_ref)
        vn_b = vn.astype(BF16)
        bt = bt_ref[...]
        dy = dy_ref[...].astype(F32)
        du_parts, dvn_parts, dmix_parts = [], [], []
        for g in range(SGU_GROUPS):
            gs = slice(LANES * g, LANES * (g + 1))
            wc = jnp.where(lower, w_ref[g], 0.0).astype(BF16)
            mixed = _dot(wc, vn_b[:, gs]) + bt[:, g:g + 1]
            du_parts.append(dy[:, gs] * mixed)
            dmix = dy[:, gs] * u[:, gs]
            dmix_b = dmix.astype(BF16)
            dmix_parts.append(dmix)
            dw_ref[g] += jnp.where(lower, _dot(dmix_b, vn_b[:, gs], NT), 0.0)
            dvn_parts.append(_dot(wc, dmix_b, TN))
        dmixed = jnp.concatenate(dmix_parts, axis=1)
        dbt_ref[...] += _dot_terms(_split3(dmixed), gsum_ref[...])
        dvn = jnp.concatenate(dvn_parts, axis=1)
        dlnw_ref[...] += jnp.sum(dvn * vhat, axis=0, keepdims=True)
        dlnb_ref[...] += jnp.sum(dvn, axis=0, keepdims=True)
        dvhat = dvn * lnw_ref[...]
        dv = rstd * (dvhat - jnp.mean(dvhat, axis=-1, keepdims=True)
                     - vhat * jnp.mean(dvhat * vhat, axis=-1, keepdims=True))
        dguv = jnp.concatenate(du_parts + [dv], axis=1)
        duv = dguv * _gelu_grad(uv)
        duvb_ref[...] += jnp.sum(duv, axis=0, keepdims=True)
        duv_ref[...] = duv.astype(BF16)

    def const(shape):
        return pl.BlockSpec(shape, lambda c: (0,) * len(shape))

    return pl.pallas_call(
        body, name=name, grid=(nc,),
        in_specs=[pl.BlockSpec((CHUNK, 2 * SGU_WIDTH), lambda c: (c, 0)),
                  pl.BlockSpec((CHUNK, SGU_WIDTH), lambda c: (c, 0)), const((1, 2 * SGU_WIDTH)),
                  const((1, SGU_WIDTH)), const((1, SGU_WIDTH)), const((SGU_GROUPS, CHUNK, CHUNK)),
                  const((CHUNK, LANES)), const((SGU_WIDTH, LANES))],
        out_specs=[pl.BlockSpec((CHUNK, 2 * SGU_WIDTH), lambda c: (c, 0)), const((SGU_GROUPS, CHUNK, CHUNK)),
                   const((CHUNK, LANES)), const((1, SGU_WIDTH)), const((1, SGU_WIDTH)), const((1, 2 * SGU_WIDTH))],
        out_shape=[jax.ShapeDtypeStruct((s, 2 * SGU_WIDTH), BF16),
                   jax.ShapeDtypeStruct((SGU_GROUPS, CHUNK, CHUNK), F32), jax.ShapeDtypeStruct((CHUNK, LANES), F32),
                   jax.ShapeDtypeStruct((1, SGU_WIDTH), F32), jax.ShapeDtypeStruct((1, SGU_WIDTH), F32),
                   jax.ShapeDtypeStruct((1, 2 * SGU_WIDTH), F32)],
        compiler_params=_params(("arbitrary",)),
    )(uv_raw, dyb, uv_b, ln_w, ln_b, w_sp, b_sp_t, group_sum)


def _gate_fwd(gates_raw, b_gate, p_a, p_b, *, name, tm=512):
    s = p_a.shape[0]
    tm = min(tm, s)

    def body(ga_ref, gb_ref, ba_ref, bb_ref, pa_ref, pb_ref, o_ref):
        ga = _sigmoid(ga_ref[...].astype(F32) + ba_ref[...])
        gb = _sigmoid(gb_ref[...].astype(F32) + bb_ref[...])
        o_ref[...] = (ga * pa_ref[...].astype(F32) + gb * pb_ref[...].astype(F32)).astype(BF16)

    t_a = pl.BlockSpec((tm, D_MODEL), lambda i: (i, 0))
    t_b = pl.BlockSpec((tm, D_MODEL), lambda i: (i, 1))
    r_a = pl.BlockSpec((1, D_MODEL), lambda i: (0, 0))
    r_b = pl.BlockSpec((1, D_MODEL), lambda i: (0, 1))
    return pl.pallas_call(
        body, name=name, grid=(s // tm,),
        in_specs=[t_a, t_b, r_a, r_b, t_a, t_a], out_specs=t_a,
        out_shape=jax.ShapeDtypeStruct((s, D_MODEL), BF16),
        compiler_params=_params(("parallel",)),
    )(gates_raw, gates_raw, b_gate, b_gate, p_a, p_b)


def _gate_bwd(gates_raw, b_gate, p_a, p_b, dm, *, name, tm=512):
    s = p_a.shape[0]
    tm = min(tm, s)

    def body(ga_ref, gb_ref, ba_ref, bb_ref, pa_ref, pb_ref, dm_ref, dpa_ref, dpb_ref, dga_ref, dgb_ref,
             dba_ref, dbb_ref):
        @pl.when(pl.program_id(0) == 0)
        def _():
            dba_ref[...] = jnp.zeros_like(dba_ref)
            dbb_ref[...] = jnp.zeros_like(dbb_ref)

        d = dm_ref[...].astype(F32)
        for g_ref, b_ref, p_ref, dp_ref, dg_ref, db_ref in ((ga_ref, ba_ref, pa_ref, dpa_ref, dga_ref, dba_ref),
                                                            (gb_ref, bb_ref, pb_ref, dpb_ref, dgb_ref, dbb_ref)):
            sg = _sigmoid(g_ref[...].astype(F32) + b_ref[...])
            dp_ref[...] = (d * sg).astype(BF16)
            dg = d * p_ref[...].astype(F32) * (sg * (1.0 - sg))
            dg_ref[...] = dg.astype(BF16)
            db_ref[...] += jnp.sum(dg, axis=0, keepdims=True)

    t_a = pl.BlockSpec((tm, D_MODEL), lambda i: (i, 0))
    t_b = pl.BlockSpec((tm, D_MODEL), lambda i: (i, 1))
    r_a = pl.BlockSpec((1, D_MODEL), lambda i: (0, 0))
    r_b = pl.BlockSpec((1, D_MODEL), lambda i: (0, 1))
    big = jax.ShapeDtypeStruct((s, D_MODEL), BF16)
    row = jax.ShapeDtypeStruct((1, D_MODEL), F32)
    return pl.pallas_call(
        body, name=name, grid=(s // tm,),
        in_specs=[t_a, t_b, r_a, r_b, t_a, t_a, t_a], out_specs=[t_a, t_a, t_a, t_a, r_a, r_a],
        out_shape=[big, big, big, big, row, row],
        compiler_params=_params(("arbitrary",)),
    )(gates_raw, gates_raw, b_gate, b_gate, p_a, p_b, dm)


def _adamw_update(w_ref, g_ref, m_ref, v_ref, d_ref, mo_ref, vo_ref):
    gv = g_ref[...]
    mn = ADAM_B1 * m_ref[...] + (1.0 - ADAM_B1) * gv
    vn = ADAM_B2 * v_ref[...] + (1.0 - ADAM_B2) * (gv * gv)
    m_hat = mn / (1.0 - ADAM_B1 ** ADAM_STEP)
    v_hat = vn / (1.0 - ADAM_B2 ** ADAM_STEP)
    d_ref[...] = -ADAM_LR * (m_hat / (jnp.sqrt(v_hat) + ADAM_EPS) + ADAM_WD * w_ref[...])
    mo_ref[...] = mn
    vo_ref[...] = vn


def _adamw_many(ws, gs, ms, vs, *, name):
    n = len(ws)

    def body(*refs):
        for i in range(n):
            _adamw_update(*[refs[k * n + i] for k in range(7)])

    whole = pl.BlockSpec(memory_space=pltpu.VMEM)
    sds = [jax.ShapeDtypeStruct(w.shape, F32) for w in ws]
    outs = pl.pallas_call(
        body, name=name, in_specs=[whole] * (4 * n), out_specs=[whole] * (3 * n), out_shape=sds * 3,
        compiler_params=pltpu.CompilerParams(vmem_limit_bytes=VMEM_LIMIT),
    )(*ws, *gs, *ms, *vs)
    return outs[:n], outs[n:2 * n], outs[2 * n:]


def _adamw(w, g, m, v, *, name, tr=128):
    r, c = w.shape
    tr = min(tr, r)
    assert r % tr == 0, (name, r, tr)
    body = functools.partial(_adamw_update)

    blk = pl.BlockSpec((tr, c), lambda i: (i, 0))
    sds = jax.ShapeDtypeStruct((r, c), F32)
    return pl.pallas_call(
        body, name=name, grid=(r // tr,), in_specs=[blk] * 4, out_specs=[blk] * 3, out_shape=[sds] * 3,
        compiler_params=_params(("parallel",)),
    )(w, g, m, v)


def _adamw_two_sums(w, g_a, g_b, m, v, *, name, tr=128):
    r, c = w.shape
    tr = min(tr, r)
    assert r % tr == 0, (name, r, tr)

    def body(w_ref, ga_ref, gb_ref, m_ref, v_ref, g_ref, d_ref, mo_ref, vo_ref):
        g_ref[...] = ga_ref[...] + gb_ref[...]
        _adamw_update(w_ref, g_ref, m_ref, v_ref, d_ref, mo_ref, vo_ref)

    blk = pl.BlockSpec((tr, c), lambda i: (i, 0))
    sds = jax.ShapeDtypeStruct((r, c), F32)
    return pl.pallas_call(
        body, name=name, grid=(r // tr,), in_specs=[blk] * 5, out_specs=[blk] * 4, out_shape=[sds] * 4,
        compiler_params=_params(("parallel",)),
    )(w, g_a, g_b, m, v)


def _tile(n, pref):
    if n <= pref:
        return n
    best = LANES
    for t in range(LANES, pref + 1, LANES):
        if n % t == 0:
            best = t
    return best


MATMUL_BLOCK_BYTES = 20 * 1024 * 1024


def _mm(pairs, name, **kw):
    trans_b = kw.get("trans_b", False)
    m = (pairs[0][0][0] if isinstance(pairs[0][0], tuple) else pairs[0][0]).shape[0]
    ktot, n = 0, None
    for _, b in pairs:
        shape = b[0].shape[1:] if isinstance(b, tuple) else b.shape
        ktot += shape[1] if trans_b else shape[0]
        n = shape[0] if trans_b else shape[1]
    out_bytes = 4 * (2 if kw.get("add") is not None else 1)
    best = None
    for tm in (256, 512, 1024, 2048):
        for tn in range(LANES, min(n, 1536) + 1, LANES):
            if m % min(tm, m) or n % tn:
                continue
            fits = 2 * ktot * (min(tm, m) + tn) + out_bytes * min(tm, m) * tn <= MATMUL_BLOCK_BYTES
            if fits and (best is None or min(tm, m) * tn >= best[0] * best[1]):
                best = (min(tm, m), tn)
    return _matmul(pairs, tm=best[0], tn=best[1], name=name, **kw)


def _wgrad(a, b, name, **kw):
    return _matmul_tn(a, b, tk=_tile(a.shape[1], 1408), tn=kw.pop("tn", _tile(b.shape[1], 1024)), tm=2048,
                      name=name, **kw)


def _local_step(x, target, get_weight, small, emit_grad):
    heads = jnp.arange(SSD_D_INNER) // SSD_HEAD_DIM
    expand = (jnp.arange(LANES)[:, None] == heads[None, :]).astype(BF16)
    expand_t = expand.T
    group_sum = (jnp.arange(SGU_WIDTH)[:, None] // LANES == jnp.arange(LANES)[None, :]).astype(BF16)
    pad_h = LANES - SSD_HEADS
    dt_bias = jnp.pad(small["dt_bias"], ((0, 0), (0, pad_h)))
    a_log = jnp.pad(small["a_log"], ((0, 0), (0, pad_h)))
    a_log_x = jnp.repeat(small["a_log"], SSD_HEAD_DIM, axis=1)
    d_skip_x = jnp.repeat(small["d_skip"], SSD_HEAD_DIM, axis=1)
    b_sp_t = jnp.pad(small["b_spatial"][0].T, ((0, 0), (0, LANES - SGU_GROUPS)))
    w_sp = small["w_spatial"][0]
    conv_a_w = jnp.pad(small["conv_a_w"], ((0, 4), (0, 0)))
    conv_f_w = jnp.pad(small["conv_f_w"], ((0, 5), (0, 0)))
    final_w = small["final_norm_w"].reshape(1, D_MODEL)

    n1 = _rms_fwd(x, small["norm1_w"], after=small.get("gathers_started"), name="rms1_fwd")
    wts = dict(get_weight("w_in", n1))
    z = _mm([(n1, wts["in_z"])], "in_z")
    xbc_raw = _mm([(n1, wts["in_xbc"])], "in_xbc")
    dt_raw = _mm([(n1, wts["in_dt"])], "in_dt")
    uv_raw = _mm([(n1, wts["in_uv"])], "in_uv", out_dtype=BF16)
    gates_raw = _mm([(n1, wts["in_gate"])], "in_gate", out_dtype=BF16)
    xbc, xbc_pre = _conv_a_fwd(xbc_raw, conv_a_w, small["conv_a_b"], name="conv_a_fwd")
    y, y_a, states = _ssd_fwd(xbc, dt_raw, z, dt_bias, a_log, a_log_x, d_skip_x, small["ssd_norm_w"], expand,
                              name="ssd_fwd")
    y_b = _sgu_fwd(uv_raw, small["uv_b"], small["v_ln_w"], small["v_ln_b"], w_sp, b_sp_t, name="sgu_fwd")
    wts.update(get_weight("w_branch", y_b))
    p_a = _mm([(y_a, wts["branch_a"])], "branch_a", out_dtype=BF16)
    p_b = _mm([(y_b, wts["branch_b"])], "branch_b", out_dtype=BF16)
    mix = _gate_fwd(gates_raw, small["b_gate"], p_a, p_b, name="gate_fwd")
    wts.update(get_weight("w_out", mix))
    h1 = _mm([(mix, wts["out"])], "out_proj", add=x)
    n2 = _rms_fwd(h1, small["norm2_w"], name="rms2_fwd")
    wts.update(get_weight("w_up", n2))
    up_w = wts["up"]
    up_cols = up_w.shape[2]
    up_raw = _matmul([(n2, (up_w, "cols"))], tm=2048, tn=up_cols, out_dtype=BF16, name="up_proj")
    act, up_a, up_v = _conv_f_fwd(up_raw, conv_f_w, small["conv_f_b"], name="conv_f_fwd")
    wts.update(get_weight("w_down", act))
    h2 = _mm([(act, wts["down"])], "down_proj", add=h1)
    loss, dh2, dh2_b, d_final = _final_fwd_bwd(h2, final_w, target, name="final_norm_loss")

    dact = _mm([(dh2_b, wts["down"])], "down_dgrad", trans_b=True)
    started = emit_grad("w_down", _wgrad(act, dh2_b, "down_wgrad"))
    dup_a, dup_v, dwf_a, dwf_v, dbf_a, dbf_v = _conv_f_bwd(up_raw, up_a, up_v, conv_f_w, dact, name="conv_f_bwd")
    dn2 = _mm([((dup_a, 0), (up_w, 0)), ((dup_a, 1), (up_w, 1)), ((dup_v, 0), (up_w, 2)), ((dup_v, 1), (up_w, 3))],
              "up_dgrad", trans_b=True, after=started, out_dtype=BF16)
    g_up = _wgrad(n2, dup_a, "up_wgrad_a", tn=up_cols, stack_out=True, part_of=(N_CHIPS, 0, None))
    g_up = _wgrad(n2, dup_v, "up_wgrad_v", tn=up_cols, stack_out=True, part_of=(N_CHIPS, N_CHIPS // 2, g_up))
    started = emit_grad("w_up", g_up)
    dh1, dh1_b, d_norm2 = _rms_bwd(h1, small["norm2_w"], dn2, dh2, name="rms2_bwd")
    dmix = _mm([(dh1_b, wts["out"])], "out_dgrad", trans_b=True, after=started, out_dtype=BF16)
    started = emit_grad("w_out", _wgrad(mix, dh1_b, "out_wgrad"))
    dp_a, dp_b, dg_a, dg_b, dbg_a, dbg_b = _gate_bwd(gates_raw, small["b_gate"], p_a, p_b, dmix, name="gate_bwd")
    dya = _mm([(dp_a, wts["branch_a"])], "branch_a_dgrad", trans_b=True, after=started)
    dyb = _mm([(dp_b, wts["branch_b"])], "branch_b_dgrad", trans_b=True, out_dtype=BF16)
    g_branch = _wgrad(y_a, dp_a, "branch_a_wgrad", part_of=(3, 0, None))
    g_branch = _wgrad(y_b, dp_b, "branch_b_wgrad", part_of=(3, 2, g_branch))
    started_branch = emit_grad("w_branch", g_branch)
    duv, d_wsp, d_bsp_t, d_lnw, d_lnb, d_uvb = _sgu_bwd(uv_raw, dyb, small["uv_b"], small["v_ln_w"],
                                                        small["v_ln_b"], w_sp, b_sp_t, group_sum, name="sgu_bwd")
    dz, dxbc, ddt, d_ssd_nw, d_dskip, d_alog, d_dtb = _ssd_bwd(
        dya, y, z, xbc, dt_raw, states, dt_bias, a_log, a_log_x, d_skip_x, small["ssd_norm_w"], expand, expand_t,
        name="ssd_bwd")
    dxbc_raw, d_conv_a_w, d_conv_a_b = _conv_a_bwd(xbc_raw, xbc_pre, conv_a_w, dxbc, name="conv_a_bwd")
    started = emit_grad("w_in", {
        "in_z": _wgrad(n1, dz, "in_z_wgrad", after=started_branch), "in_xbc": _wgrad(n1, dxbc_raw, "in_xbc_wgrad"),
        "in_dt": _wgrad(n1, ddt, "in_dt_wgrad")[:, :SSD_HEADS], "in_uv": _wgrad(n1, duv, "in_uv_wgrad"),
        "in_gate_a": _wgrad(n1, dg_a, "in_gate_a_wgrad"), "in_gate_b": _wgrad(n1, dg_b, "in_gate_b_wgrad")})
    dn1 = _mm([(dz, wts["in_z"]), (dxbc_raw, wts["in_xbc"]), (ddt, wts["in_dt"]), (duv, wts["in_uv"]),
               (dg_a, wts["in_gate_a"]), (dg_b, wts["in_gate_b"])], "in_dgrad", trans_b=True, after=started,
              out_dtype=BF16)
    dx, _, d_norm1 = _rms_bwd(x, small["norm1_w"], dn1, dh1, name="rms1_bwd")

    grads_small = {
        "norm1_w": d_norm1, "b_gate": jnp.concatenate([dbg_a, dbg_b], axis=1),
        "conv_a_w": d_conv_a_w[:4], "conv_a_b": d_conv_a_b,
        "dt_bias": d_dtb[:, :SSD_HEADS], "a_log": d_alog[:, :SSD_HEADS], "d_skip": d_dskip[:, :SSD_HEADS],
        "ssd_norm_w": d_ssd_nw, "uv_b": d_uvb, "v_ln_w": d_lnw, "v_ln_b": d_lnb,
        "w_spatial": d_wsp[None], "b_spatial": d_bsp_t[:, :SGU_GROUPS].T[None],
        "norm2_w": d_norm2, "conv_f_w": jnp.concatenate([dwf_a[:3], dwf_v[:3]], axis=1),
        "conv_f_b": jnp.concatenate([dbf_a, dbf_v], axis=1), "final_norm_w": d_final.reshape(D_MODEL),
    }
    return loss, dx, grads_small


HBM = pl.BlockSpec(memory_space=pl.ANY)
MESH = pl.DeviceIdType.MESH


def _mesh_pos():
    return lax.axis_index("x"), lax.axis_index("y"), lax.axis_index("c")


def _other_chips(x, y):
    return [(1 - x, y), (x, 1 - y), (1 - x, 1 - y)]


def _remote(src, dst, send_sems, recv_sems, k, dev):
    return pltpu.make_async_remote_copy(src_ref=src, dst_ref=dst, send_sem=send_sems.at[k], recv_sem=recv_sems.at[k],
                                        device_id=dev, device_id_type=MESH)


def _dma_sems(n):
    return [pltpu.SemaphoreType.DMA((n,)), pltpu.SemaphoreType.DMA((n,))]


HBM_ONLY = pl.BlockSpec(memory_space=pltpu.HBM)
SEMAPHORES = pl.BlockSpec(memory_space=pltpu.SEMAPHORE)
DATAFLOW_EFFECT = pltpu.SideEffectType.DATAFLOW_SIDE_EFFECTING
N_PEER_CHIPS = N_CHIPS - 1


def _gather_sends(w_ref, land_ref, send_sems, recv_sems):
    x, y, c = _mesh_pos()
    return [_remote(w_ref.at[c], land_ref.at[2 * x + y, c], send_sems, recv_sems, k, (px, py, c))
            for k, (px, py) in enumerate(_other_chips(x, y))]


def _gather_arrivals(w_ref, land_ref, send_sems, recv_sems):
    x, y, c = _mesh_pos()
    return [_remote(w_ref.at[c], land_ref.at[2 * px + py, c], send_sems, recv_sems, k, (px, py, c))
            for k, (px, py) in enumerate(_other_chips(x, y))]


def _gather_whole_sends(w_ref, land_ref, send_sems, recv_sems):
    x, y, c = _mesh_pos()
    return [_remote(w_ref, land_ref.at[2 * x + y], send_sems, recv_sems, k, (px, py, c))
            for k, (px, py) in enumerate(_other_chips(x, y))]


def _gather_whole_arrivals(w_ref, land_ref, send_sems, recv_sems):
    x, y, c = _mesh_pos()
    return [_remote(w_ref, land_ref.at[2 * px + py], send_sems, recv_sems, k, (px, py, c))
            for k, (px, py) in enumerate(_other_chips(x, y))]


def _scatter_sends(h_ref, land_ref, send_sems, recv_sems):
    x, y, c = _mesh_pos()
    return [_remote(h_ref.at[2 * px + py], land_ref.at[2 * x + y], send_sems, recv_sems, k, (px, py, c))
            for k, (px, py) in enumerate(_other_chips(x, y))]


def _scatter_arrivals(h_ref, land_ref, send_sems, recv_sems):
    x, y, c = _mesh_pos()
    return [_remote(h_ref.at[2 * x + y], land_ref.at[2 * px + py], send_sems, recv_sems, k, (px, py, c))
            for k, (px, py) in enumerate(_other_chips(x, y))]


def _exchange_wait_many(pendings, after, sends, arrivals, *, name):
    n = len(pendings)

    def body(*refs):
        for i in range(n):
            src_ref, land_ref, send_ref, recv_ref = refs[i], refs[n + i], refs[2 * n + i], refs[3 * n + i]
            for cp in sends(src_ref, land_ref, send_ref, recv_ref):
                cp.wait_send()
            for cp in arrivals(src_ref, land_ref, send_ref, recv_ref):
                cp.wait_recv()

    sources = [p[2] for p in pendings]
    landings = [p[3] for p in pendings]
    outs = pl.pallas_call(
        body, name=name,
        out_shape=tuple(pltpu.HBM(a.shape, a.dtype) for a in sources + landings),
        in_specs=[HBM_ONLY] * (2 * n) + [SEMAPHORES] * (2 * n) + [pl.BlockSpec(memory_space=pl.ANY)],
        out_specs=tuple([HBM_ONLY] * (2 * n)), input_output_aliases={i: i for i in range(2 * n)},
        compiler_params=pltpu.CompilerParams(has_side_effects=DATAFLOW_EFFECT),
    )(*sources, *landings, *[p[0] for p in pendings], *[p[1] for p in pendings], after)
    return [(outs[i], outs[n + i]) for i in range(n)]


def _sibling_sends(src_ref, land_ref, send_sems, recv_sems):
    x, y, c = _mesh_pos()
    return [_remote(src_ref, land_ref, send_sems, recv_sems, 0, (x, y, 1 - c))]


def _exchange_start(sources, landing_shapes, sends, *, after=None, name):
    n = len(sources)
    extra = [] if after is None else [after]

    def body(*refs):
        sems = refs[2 * n + len(extra):4 * n + len(extra)]
        for i in range(n):
            send_i = sends[i] if isinstance(sends, (list, tuple)) else sends
            for cp in send_i(refs[i], refs[n + i], sems[2 * i], sems[2 * i + 1]):
                cp.start()
        refs[-1][...] = jnp.zeros_like(refs[-1])

    hbm = [pltpu.HBM(s.shape, s.dtype) for s in sources] + [pltpu.HBM(shp, s.dtype)
                                                             for shp, s in zip(landing_shapes, sources)]
    outs = pl.pallas_call(
        body, name=name,
        out_shape=tuple([pltpu.SemaphoreType.DMA((N_PEER_CHIPS,))] * (2 * n) + hbm
                        + [jax.ShapeDtypeStruct((8, LANES), F32)]),
        in_specs=[HBM_ONLY] * (2 * n) + [pl.BlockSpec(memory_space=pl.ANY)] * len(extra),
        out_specs=tuple([SEMAPHORES] * (2 * n) + [HBM_ONLY] * (2 * n) + [pl.BlockSpec(memory_space=pltpu.VMEM)]),
        input_output_aliases={i: 2 * n + i for i in range(2 * n)},
        compiler_params=pltpu.CompilerParams(has_side_effects=DATAFLOW_EFFECT),
    )(*[pltpu.with_memory_space_constraint(s, pltpu.HBM) for s in sources],
      *[pltpu.with_memory_space_constraint(lax.empty(shp, s.dtype), pltpu.HBM)
        for shp, s in zip(landing_shapes, sources)], *extra)
    pending = [(outs[2 * i], outs[2 * i + 1], outs[2 * n + i], outs[3 * n + i]) for i in range(n)]
    return pending, outs[-1]


def _exchange_wait(pending, after, sends, arrivals, *, name):
    send_sems, recv_sems, source, landing = pending

    def body(src_ref, land_ref, send_ref, recv_ref, after_ref, src_out, land_out):
        for cp in sends(src_ref, land_ref, send_ref, recv_ref):
            cp.wait_send()
        for cp in arrivals(src_ref, land_ref, send_ref, recv_ref):
            cp.wait_recv()

    return pl.pallas_call(
        body, name=name,
        out_shape=(pltpu.HBM(source.shape, source.dtype), pltpu.HBM(landing.shape, landing.dtype)),
        in_specs=[HBM_ONLY, HBM_ONLY, SEMAPHORES, SEMAPHORES, pl.BlockSpec(memory_space=pl.ANY)],
        out_specs=(HBM_ONLY, HBM_ONLY), input_output_aliases={0: 0, 1: 1},
        compiler_params=pltpu.CompilerParams(has_side_effects=DATAFLOW_EFFECT),
    )(source, landing, send_sems, recv_sems, after)


def _gather_ici(shard, *, name):
    _, rh, cols = shard.shape

    def body(w_ref, o_ref, send_sems, recv_sems):
        x, y, c = _mesh_pos()
        mine = 2 * x + y
        sends = []
        for k, (px, py) in enumerate(_other_chips(x, y)):
            cp = _remote(w_ref.at[c], o_ref.at[mine, c], send_sems, recv_sems, k, (px, py, c))
            cp.start()
            sends.append(cp)
        for k, (px, py) in enumerate(_other_chips(x, y)):
            _remote(w_ref.at[c], o_ref.at[2 * px + py, c], send_sems, recv_sems, k, (px, py, c)).wait_recv()
        for cp in sends:
            cp.wait_send()

    return pl.pallas_call(
        body, name=name, in_specs=[HBM], out_specs=HBM,
        out_shape=jax.ShapeDtypeStruct((N_CHIPS, 2, rh, cols), shard.dtype), scratch_shapes=_dma_sems(3),
    )(shard)


def _gather_d2d(parts, *, name):
    def body(a_ref, o_ref, send_sems, recv_sems):
        x, y, c = _mesh_pos()
        sibling = (x, y, 1 - c)
        sends = []
        for k, (px, py) in enumerate(_other_chips(x, y)):
            cp = _remote(a_ref.at[2 * px + py, c], o_ref.at[2 * px + py, c], send_sems, recv_sems, k, sibling)
            cp.start()
            sends.append(cp)
        for k, (px, py) in enumerate(_other_chips(x, y)):
            _remote(a_ref.at[2 * px + py, c], o_ref.at[2 * px + py, 1 - c], send_sems, recv_sems, k, sibling).wait_recv()
        for cp in sends:
            cp.wait_send()

    return pl.pallas_call(
        body, name=name, in_specs=[HBM], out_specs=HBM,
        out_shape=jax.ShapeDtypeStruct(parts.shape, parts.dtype),
        input_output_aliases={0: 0}, scratch_shapes=_dma_sems(3),
    )(parts)


def _all_gather_chips(shard_flat, name):
    rows, cols = shard_flat.shape
    parts = _gather_ici(shard_flat.reshape(2, rows // 2, cols), name=name + "_ici")
    others = _gather_d2d(parts, name=name + "_d2d").reshape(N_CHIPS, rows, cols)
    chip = 2 * lax.axis_index("x") + lax.axis_index("y")
    return lax.dynamic_update_slice(others, shard_flat[None], (chip, 0, 0))


def _row_tile(rows, mult, cap):
    best = mult
    for t in range(mult, min(rows, cap) + 1, mult):
        if rows % t == 0:
            best = t
    assert rows % best == 0, (rows, mult)
    return best


def _swap_halves_d2d(g, *, after=None, name):
    _, _, rh, cols = g.shape
    extra = [] if after is None else [after]

    def body(g_ref, *rest):
        o_ref, send_sems, recv_sems = rest[len(extra):]
        x, y, c = _mesh_pos()
        sibling = (x, y, 1 - c)
        sends = []
        for s in range(N_CHIPS):
            cp = _remote(g_ref.at[s, 1 - c], o_ref.at[s], send_sems, recv_sems, s, sibling)
            cp.start()
            sends.append(cp)
        for s in range(N_CHIPS):
            _remote(g_ref.at[s, c], o_ref.at[s], send_sems, recv_sems, s, sibling).wait_recv()
        for cp in sends:
            cp.wait_send()

    return pl.pallas_call(
        body, name=name, in_specs=[HBM] * (1 + len(extra)), out_specs=HBM,
        out_shape=jax.ShapeDtypeStruct((N_CHIPS, rh, cols), g.dtype), scratch_shapes=_dma_sems(N_CHIPS),
    )(g, *extra)


def _add_own_half(g, arrived, core, *, name):
    _, _, rh, cols = g.shape
    mult = 16 if g.dtype == BF16 else 8
    tr = _row_tile(rh, mult, max(mult, (512 * 1024) // cols))

    def body(core_ref, g_ref, a_ref, o_ref):
        o_ref[...] = (g_ref[0].astype(F32) + a_ref[...].astype(F32)).astype(o_ref.dtype)

    grid_spec = pltpu.PrefetchScalarGridSpec(
        num_scalar_prefetch=1, grid=(N_CHIPS, rh // tr),
        in_specs=[pl.BlockSpec((1, 1, tr, cols), lambda s, i, core_ref: (s, core_ref[0], i, 0)),
                  pl.BlockSpec((1, tr, cols), lambda s, i, core_ref: (s, i, 0))],
        out_specs=pl.BlockSpec((1, tr, cols), lambda s, i, core_ref: (s, i, 0)))
    return pl.pallas_call(
        body, name=name, grid_spec=grid_spec, out_shape=jax.ShapeDtypeStruct((N_CHIPS, rh, cols), g.dtype),
        compiler_params=_params(("parallel", "parallel")),
    )(core, g, arrived)


def _scatter_ici(h, *, after=None, name):
    extra = [] if after is None else [after]

    def body(h_ref, *rest):
        o_ref, send_sems, recv_sems = rest[len(extra):]
        x, y, c = _mesh_pos()
        mine = 2 * x + y
        sends = []
        for k, (px, py) in enumerate(_other_chips(x, y)):
            cp = _remote(h_ref.at[2 * px + py], o_ref.at[mine], send_sems, recv_sems, k, (px, py, c))
            cp.start()
            sends.append(cp)
        for k, (px, py) in enumerate(_other_chips(x, y)):
            _remote(h_ref.at[mine], o_ref.at[2 * px + py], send_sems, recv_sems, k, (px, py, c)).wait_recv()
        for cp in sends:
            cp.wait_send()

    others = pl.pallas_call(
        body, name=name, in_specs=[HBM] * (1 + len(extra)), out_specs=HBM,
        out_shape=jax.ShapeDtypeStruct(h.shape, h.dtype), scratch_shapes=_dma_sems(3),
    )(h, *extra)
    chip = 2 * lax.axis_index("x") + lax.axis_index("y")
    own = lax.dynamic_slice_in_dim(h, chip, 1, axis=0)
    return lax.dynamic_update_slice(others, own, (chip, 0, 0))


def _sum_chips(parts, *, name):
    _, rh, cols = parts.shape
    mult = 16 if parts.dtype == BF16 else 8
    tr = _row_tile(rh, mult, max(mult, (512 * 1024) // cols))

    def body(p_ref, o_ref):
        acc = p_ref[0].astype(F32)
        for s in range(1, N_CHIPS):
            acc = acc + p_ref[s].astype(F32)
        o_ref[...] = acc

    return pl.pallas_call(
        body, name=name, grid=(rh // tr,),
        in_specs=[pl.BlockSpec((N_CHIPS, tr, cols), lambda i: (0, i, 0))],
        out_specs=pl.BlockSpec((tr, cols), lambda i: (i, 0)),
        out_shape=jax.ShapeDtypeStruct((rh, cols), F32), compiler_params=_params(("parallel",)),
    )(parts)


def _sum_chips_with_own(landed, sent, chip, *, name):
    _, rh, cols = landed.shape
    mult = 16 if landed.dtype == BF16 else 8
    tr = _row_tile(rh, mult, max(mult, (512 * 1024) // cols))

    def body(chip_ref, own_ref, px_ref, py_ref, pxy_ref, o_ref):
        acc = own_ref[0].astype(F32)
        for p_ref in (px_ref, py_ref, pxy_ref):
            acc = acc + p_ref[0].astype(F32)
        o_ref[...] = acc

    def block_of(flip):
        return pl.BlockSpec((1, tr, cols), lambda i, chip_ref: (chip_ref[0] ^ flip, i, 0))

    grid_spec = pltpu.PrefetchScalarGridSpec(
        num_scalar_prefetch=1, grid=(rh // tr,),
        in_specs=[block_of(0), block_of(2), block_of(1), block_of(3)],
        out_specs=pl.BlockSpec((tr, cols), lambda i, chip_ref: (i, 0)))
    return pl.pallas_call(
        body, name=name, grid_spec=grid_spec, out_shape=jax.ShapeDtypeStruct((rh, cols), F32),
        compiler_params=_params(("parallel",)),
    )(chip, sent, landed, landed, landed)


def _share_d2d(f, *, name):
    fs = f if isinstance(f, (list, tuple)) else [f]
    others = _swap_with_sibling(fs, name=name)
    first = lax.axis_index("c") == 0
    both = [jnp.stack([jnp.where(first, a, b), jnp.where(first, b, a)]) for a, b in zip(fs, others)]
    return both if isinstance(f, (list, tuple)) else both[0]


def _swap_with_sibling(fs, *, name):
    n = len(fs)

    def body(*refs):
        x, y, c = _mesh_pos()
        sibling = (x, y, 1 - c)
        send_sems, recv_sems = refs[2 * n:]
        copies = [_remote(refs[i], refs[n + i], send_sems, recv_sems, i, sibling) for i in range(n)]
        for cp in copies:
            cp.start()
        for cp in copies:
            cp.wait()

    return pl.pallas_call(
        body, name=name, in_specs=[HBM] * n, out_specs=[HBM] * n,
        out_shape=[jax.ShapeDtypeStruct(a.shape, a.dtype) for a in fs], scratch_shapes=_dma_sems(n),
    )(*fs)


def _reduce_scatter_chips(g, core, name, after=None, after_swap=None):
    _, rows, cols = g.shape
    g = g.reshape(N_CHIPS, 2, rows // 2, cols)
    arrived = _swap_halves_d2d(g, after=after, name=name + "_swap")
    started = after_swap(arrived) if after_swap is not None else None
    chip_sum = _add_own_half(g, arrived, core, name=name + "_add2")
    parts = _scatter_ici(chip_sum, after=started, name=name + "_ici")
    total = _sum_chips(parts, name=name + "_sum4")
    return _share_d2d(total, name=name + "_share").reshape(rows, cols)


BIG = ("w_in", "w_branch", "w_out", "w_up", "w_down")
BIG_COLUMN_SHARDED = ("w_in", "w_up")
CONV = ("conv_a_w", "conv_f_w")
REPLICATED = ("norm1_w", "b_gate", "conv_a_b", "dt_bias", "a_log", "d_skip", "ssd_norm_w", "uv_b", "v_ln_w",
              "v_ln_b", "w_spatial", "b_spatial", "norm2_w", "conv_f_b", "final_norm_w")
WEIGHT_ORDER = ("norm1_w", "w_in", "b_gate", "conv_a_w", "conv_a_b", "dt_bias", "a_log", "d_skip", "ssd_norm_w",
                "uv_b", "v_ln_w", "v_ln_b", "w_spatial", "b_spatial", "w_branch", "w_out", "norm2_w", "w_up",
                "conv_f_w", "conv_f_b", "w_down", "final_norm_w")
SMALL_EXCHANGE_ROWS = 64


_GATE0 = SSD_IN + 2 * SGU_WIDTH
IN_SEGMENTS = {
    "in_z": (0, SSD_D_INNER), "in_xbc": (SSD_D_INNER, SSD_D_INNER + SSD_XBC), "in_dt": (SSD_D_INNER + SSD_XBC, SSD_IN),
    "in_uv": (SSD_IN, _GATE0), "in_gate": (_GATE0, IN_COLS), "in_gate_a": (_GATE0, _GATE0 + D_MODEL),
    "in_gate_b": (_GATE0 + D_MODEL, IN_COLS),
}
IN_GRAD_SEGMENTS = ("in_z", "in_xbc", "in_dt", "in_uv", "in_gate_a", "in_gate_b")


def _take_columns(parts, start, stop):
    out = []
    for a, first in parts:
        lo, hi = max(start, first), min(stop, first + a.shape[1])
        if lo < hi:
            out.append(a[:, lo - first:hi - first])
    return out[0] if len(out) == 1 else jnp.concatenate(out, axis=1)


def _flat_rows(arrays, row_multiple):
    flat = jnp.concatenate([a.reshape(-1) for a in arrays])
    rows = -(-flat.shape[0] // (LANES * row_multiple)) * row_multiple
    return jnp.pad(flat, (0, rows * LANES - flat.shape[0])).reshape(rows, LANES)


def _unflatten(flat, shapes):
    flat = flat.reshape(-1)
    out, off = [], 0
    for shp in shapes:
        n = math.prod(shp)
        out.append(flat[off:off + n].reshape(shp))
        off += n
    return out


def _from_chip_blocks(blocks, name):
    if name in BIG_COLUMN_SHARDED or name in CONV:
        k = blocks.shape[1]
        return jnp.transpose(blocks, (1, 0, 2)).reshape(k, -1)
    return blocks.reshape(-1, blocks.shape[-1])


def _to_chip_blocks(whole, name):
    if name in BIG_COLUMN_SHARDED or name in CONV:
        k, n = whole.shape
        return jnp.transpose(whole.reshape(k, N_CHIPS, n // N_CHIPS), (1, 0, 2))
    return whole.reshape(N_CHIPS, whole.shape[0] // N_CHIPS, whole.shape[1])


def kernel(x, norm1_w, w_in, b_gate, conv_a_w, conv_a_b, dt_bias, a_log, d_skip, ssd_norm_w, uv_b, v_ln_w, v_ln_b, w_spatial, b_spatial, w_branch, w_out, norm2_w, w_up, conv_f_w, conv_f_b, w_down, final_norm_w, loss_target, m_norm1_w, m_w_in, m_b_gate, m_conv_a_w, m_conv_a_b, m_dt_bias, m_a_log, m_d_skip, m_ssd_norm_w, m_uv_b, m_v_ln_w, m_v_ln_b, m_w_spatial, m_b_spatial, m_w_branch, m_w_out, m_norm2_w, m_w_up, m_conv_f_w, m_conv_f_b, m_w_down, m_final_norm_w, v_norm1_w, v_w_in, v_b_gate, v_conv_a_w, v_conv_a_b, v_dt_bias, v_a_log, v_d_skip, v_ssd_norm_w, v_uv_b, v_v_ln_w, v_v_ln_b, v_w_spatial, v_b_spatial, v_w_branch, v_w_out, v_norm2_w, v_w_up, v_conv_f_w, v_conv_f_b, v_w_down, v_final_norm_w):
    weights = dict(norm1_w=norm1_w, w_in=w_in, b_gate=b_gate, conv_a_w=conv_a_w, conv_a_b=conv_a_b, dt_bias=dt_bias,
                   a_log=a_log, d_skip=d_skip, ssd_norm_w=ssd_norm_w, uv_b=uv_b, v_ln_w=v_ln_w, v_ln_b=v_ln_b,
                   w_spatial=w_spatial, b_spatial=b_spatial, w_branch=w_branch, w_out=w_out, norm2_w=norm2_w,
                   w_up=w_up, conv_f_w=conv_f_w, conv_f_b=conv_f_b, w_down=w_down, final_norm_w=final_norm_w)
    mom1 = dict(norm1_w=m_norm1_w, w_in=m_w_in, b_gate=m_b_gate, conv_a_w=m_conv_a_w, conv_a_b=m_conv_a_b,
                dt_bias=m_dt_bias, a_log=m_a_log, d_skip=m_d_skip, ssd_norm_w=m_ssd_norm_w, uv_b=m_uv_b,
                v_ln_w=m_v_ln_w, v_ln_b=m_v_ln_b, w_spatial=m_w_spatial, b_spatial=m_b_spatial, w_branch=m_w_branch,
                w_out=m_w_out, norm2_w=m_norm2_w, w_up=m_w_up, conv_f_w=m_conv_f_w, conv_f_b=m_conv_f_b,
                w_down=m_w_down, final_norm_w=m_final_norm_w)
    mom2 = dict(norm1_w=v_norm1_w, w_in=v_w_in, b_gate=v_b_gate, conv_a_w=v_conv_a_w, conv_a_b=v_conv_a_b,
                dt_bias=v_dt_bias, a_log=v_a_log, d_skip=v_d_skip, ssd_norm_w=v_ssd_norm_w, uv_b=v_uv_b,
                v_ln_w=v_v_ln_w, v_ln_b=v_v_ln_b, w_spatial=v_w_spatial, b_spatial=v_b_spatial, w_branch=v_w_branch,
                w_out=v_w_out, norm2_w=v_norm2_w, w_up=v_w_up, conv_f_w=v_conv_f_w, conv_f_b=v_conv_f_b,
                w_down=v_w_down, final_norm_w=v_final_norm_w)
    chip = 2 * lax.axis_index("x") + lax.axis_index("y")
    core = lax.axis_index("c").astype(jnp.int32).reshape(1)

    whole = {}
    conv_shapes = [weights[n].shape[1:] for n in CONV]
    conv_gathered = _all_gather_chips(_flat_rows([weights[n] for n in CONV], 16), "gather_conv").reshape(N_CHIPS, -1)
    off = 0
    for n, shp in zip(CONV, conv_shapes):
        size = math.prod(shp)
        whole[n] = _from_chip_blocks(conv_gathered[:, off:off + size].reshape((N_CHIPS,) + shp), n)
        off += size
    shard_shapes = {n: weights[n].shape[1:] for n in BIG}
    halves = [weights[n][0].astype(BF16).reshape(2, shard_shapes[n][0] // 2, shard_shapes[n][1]) for n in BIG]
    sends = [_gather_sends if n == "w_in" else _gather_whole_sends for n in BIG]
    gathers, gathers_started = _exchange_start(halves, [(N_CHIPS,) + h.shape for h in halves], sends,
                                               after=conv_gathered, name="gather_start")
    gathers = dict(zip(BIG, gathers))

    def get_weight(name, after):
        rows, cols = shard_shapes[name]
        if name == "w_in":
            own, landed = _exchange_wait(gathers[name], after, _gather_sends, _gather_arrivals,
                                         name="gather_" + name + "_wait")
            landed = _gather_d2d(landed, name="gather_" + name + "_d2d")
        else:
            own, landed = _exchange_wait(gathers[name], after, _gather_whole_sends, _gather_whole_arrivals,
                                         name="gather_" + name + "_wait")
        blocks = lax.dynamic_update_slice(landed.reshape(N_CHIPS, rows, cols), own.reshape(1, rows, cols),
                                          (chip, 0, 0))
        if name == "w_up":
            return {"up": blocks}
        if name == "w_in":
            parts = [(blocks[k], cols * k) for k in range(N_CHIPS)]
            segs = {n: _take_columns(parts, a, b) for n, (a, b) in IN_SEGMENTS.items()}
            segs["in_dt"] = jnp.pad(segs["in_dt"], ((0, 0), (0, LANES - SSD_HEADS)))
            return segs
        full = _from_chip_blocks(blocks, name)
        if name == "w_branch":
            return {"branch_a": full[:SSD_D_INNER], "branch_b": full[SSD_D_INNER:]}
        return {name[2:]: full}

    small = {n: weights[n] for n in REPLICATED}
    small["conv_a_w"] = whole["conv_a_w"]
    small["conv_f_w"] = whole["conv_f_w"]
    small["gathers_started"] = gathers_started

    reductions = {}

    def emit_grad(name, g):
        if name == "w_in":
            parts = [(g[n], IN_SEGMENTS[n][0]) for n in IN_GRAD_SEGMENTS]
            cols = shard_shapes[name][1]
            g_blocks = jnp.stack([_take_columns(parts, cols * k, cols * (k + 1)) for k in range(N_CHIPS)])
        else:
            g_blocks = g if name == "w_up" else _to_chip_blocks(g, name)
        if name == "w_in":
            _, rows, cols = g_blocks.shape
            g_halves = g_blocks.reshape(N_CHIPS, 2, rows // 2, cols)
            arrived = _swap_halves_d2d(g_halves, name="reduce_" + name + "_swap")
            g_blocks = _add_own_half(g_halves, arrived, core, name="reduce_" + name + "_add2")
        (pending,), started = _exchange_start([g_blocks], [g_blocks.shape], _scatter_sends,
                                              name="reduce_" + name + "_start")
        reductions[name] = pending
        return started

    loss, dx, grads_small = _local_step(x[0], loss_target[0], get_weight, small, emit_grad)

    order = ("w_down", "w_up", "w_out", "w_branch", "w_in")
    core_sums = []
    chip_index = chip.astype(jnp.int32).reshape(1)
    for n in order:
        sent, landed = _exchange_wait(reductions[n], dx, _scatter_sends, _scatter_arrivals,
                                      name="reduce_" + n + "_wait")
        core_sums.append(_sum_chips_with_own(landed, sent, chip_index, name="reduce_" + n + "_sum4"))
    grads = {}
    swaps = []

    def start_sum_swap(small_swapped):
        pending, started = _exchange_start(core_sums, [a.shape for a in core_sums], _sibling_sends,
                                           after=small_swapped, name="reduce_swap_start")
        swaps.extend(pending)
        return started

    small_names = REPLICATED + CONV + ("loss",)
    grads_small = dict(grads_small, loss=loss)
    small_shapes = [grads_small[n].shape for n in small_names]
    g_small = _flat_rows([grads_small[n] for n in small_names], N_CHIPS * 2 * SMALL_EXCHANGE_ROWS)
    red_small = _reduce_scatter_chips(g_small.reshape(N_CHIPS, -1, LANES), core, "reduce_small", after=core_sums[-1],
                                      after_swap=start_sum_swap)
    all_small = _all_gather_chips(red_small, "gather_small")
    swapped = _exchange_wait_many(swaps, all_small, _sibling_sends, _sibling_sends, name="reduce_swap_wait")
    core_sums = {n: own for n, (own, _) in zip(order, swapped)}
    sibling_sums = {n: other for n, (_, other) in zip(order, swapped)}
    first = lax.axis_index("c") == 0
    w_in_halves = (core_sums["w_in"], sibling_sums["w_in"])
    w_in_grad = jnp.concatenate([jnp.where(first, w_in_halves[0], w_in_halves[1]),
                                 jnp.where(first, w_in_halves[1], w_in_halves[0])], axis=0)
    for n, g in zip(small_names, _unflatten(all_small, small_shapes)):
        if n == "loss":
            total_loss = g[0, 0]
            continue
        if n in CONV:
            width = g.shape[1] // N_CHIPS
            g = lax.dynamic_slice_in_dim(g, chip * width, width, axis=1)
        grads[n] = g.reshape(weights[n].shape[1:]) if n != "final_norm_w" else g

    delta, new_m, new_v = {}, {}, {}
    for n in BIG:
        shp = weights[n].shape
        if n == "w_in":
            g_t = w_in_grad.T
            results = [g_t] + list(_adamw(weights[n][0].T, g_t, mom1[n][0].T, mom2[n][0].T, name="adamw_" + n,
                                          tr=_row_tile(g_t.shape[0], 8, 136)))
            results = [a.T for a in results]
        else:
            results = _adamw_two_sums(weights[n][0], core_sums[n], sibling_sums[n], mom1[n][0], mom2[n][0],
                                      name="adamw_" + n, tr=_row_tile(shp[1], 8, 352))
        grads[n], delta[n], new_m[n], new_v[n] = [a.reshape(shp) for a in results]
    small_all = [n for n in WEIGHT_ORDER if n not in BIG]

    def as_2d(a):
        return a.reshape(-1, a.shape[-1])

    results = _adamw_many(*[[as_2d(src[n]) for n in small_all] for src in (weights, grads, mom1, mom2)],
                          name="adamw_small")
    for n, dv, mv, vv in zip(small_all, *results):
        shp = weights[n].shape
        delta[n], new_m[n], new_v[n] = dv.reshape(shp), mv.reshape(shp), vv.reshape(shp)

    grad_out = [grads[n].reshape(weights[n].shape) for n in WEIGHT_ORDER]
    return (total_loss, dx[None], *grad_out, *[delta[n] for n in WEIGHT_ORDER], *[new_m[n] for n in WEIGHT_ORDER],
            *[new_v[n] for n in WEIGHT_ORDER])
```

```python
import functools
import math

import jax
import jax.numpy as jnp
from jax import lax
from jax.experimental import pallas as pl
from jax.experimental.pallas import tpu as pltpu

F32 = jnp.float32
BF16 = jnp.bfloat16

D_MODEL = 1024
SSD_D_INNER = 2048
SSD_HEADS = 32
SSD_HEAD_DIM = 64
SSD_GROUPS = 4
SSD_HEADS_PER_GROUP = 8
SSD_STATE = 128
SSD_BC = 512
SSD_XBC = 3072
SSD_IN = 5152
SGU_WIDTH = 1024
SGU_GROUPS = 8
CHUNK = 128
IN_COLS = 9248
D_FF = 2816
NORM_EPS = 1e-6
LN_EPS = 1e-5
GROUP_COLS = SSD_HEADS_PER_GROUP * SSD_HEAD_DIM
LANES = 128

ADAM_LR = 0.001
ADAM_B1 = 0.9
ADAM_B2 = 0.999
ADAM_EPS = 1e-08
ADAM_WD = 0.01
ADAM_STEP = 10

N_CHIPS = 4
VMEM_LIMIT = 56 * 1024 * 1024

NT = (((1,), (1,)), ((), ()))
TN = (((0,), (0,)), ((), ()))
NN = (((1,), (0,)), ((), ()))


def _params(dims):
    return pltpu.CompilerParams(dimension_semantics=dims, vmem_limit_bytes=VMEM_LIMIT)


def _dot(a, b, dn=NN, precision=None):
    return lax.dot_general(a, b, dn, precision=precision, preferred_element_type=F32)


def _split3(x):
    hi = x.astype(BF16)
    rest = x - hi.astype(F32)
    mid = rest.astype(BF16)
    return hi, mid, (rest - mid.astype(F32)).astype(BF16)


def _dot_terms(terms, exact, dn=NN):
    out = None
    for t in terms:
        p = _dot(t, exact, dn)
        out = p if out is None else out + p
    return out


def _dot_exact_lhs(exact, terms):
    out = None
    for t in terms:
        p = _dot(exact, t)
        out = p if out is None else out + p
    return out


def _sigmoid(x):
    return 1.0 / (1.0 + jnp.exp(-x))


def _softplus(x):
    return jnp.maximum(x, 0.0) + jnp.log(1.0 + jnp.exp(-jnp.abs(x)))


def _matmul(pairs, *, trans_b=False, add=None, after=None, out_dtype=F32, tm=512, tn=512, name):
    def mat_shape(b):
        if isinstance(b, tuple) and b[1] == "cols":
            return (b[0].shape[1], b[0].shape[0] * b[0].shape[2])
        return b[0].shape[1:] if isinstance(b, tuple) else b.shape

    if isinstance(pairs[0][1], tuple) and pairs[0][1][1] == "cols":
        assert not trans_b and tn % LANES == 0 and pairs[0][1][0].shape[2] % tn == 0, name

    m = (pairs[0][0][0] if isinstance(pairs[0][0], tuple) else pairs[0][0]).shape[0]
    n = mat_shape(pairs[0][1])[0] if trans_b else mat_shape(pairs[0][1])[1]
    tm, tn = min(tm, m), min(tn, n)
    assert m % tm == 0 and n % tn == 0, (name, m, n, tm, tn)
    npairs = len(pairs)
    dn = NT if trans_b else NN

    def body(*refs):
        o_ref = refs[-1]
        acc = None
        for i in range(npairs):
            p = _dot(refs[2 * i][...].astype(BF16), refs[2 * i + 1][...].astype(BF16), dn)
            acc = p if acc is None else acc + p
        if add is not None:
            acc = acc + refs[2 * npairs][...]
        o_ref[...] = acc.astype(out_dtype)

    in_specs, args = [], []
    for a, b in pairs:
        bshape = mat_shape(b)
        k = bshape[1] if trans_b else bshape[0]
        assert bshape == ((n, k) if trans_b else (k, n)), (name, bshape)
        a, qa = a if isinstance(a, tuple) else (a, 0)
        assert a.shape[0] == m and a.shape[1] % k == 0, (name, a.shape, k)
        in_specs.append(pl.BlockSpec((tm, k), lambda i, j, qa=qa: (i, qa)))
        if isinstance(b, tuple) and b[1] == "cols":
            b = b[0]
            per = b.shape[2] // tn
            in_specs.append(pl.BlockSpec((None, k, tn), lambda i, j, per=per: (j // per, 0, j % per)))
        elif isinstance(b, tuple):
            b, qb = b
            if trans_b:
                in_specs.append(pl.BlockSpec((None, tn, k), lambda i, j, qb=qb: (qb, j, 0)))
            else:
                in_specs.append(pl.BlockSpec((None, k, tn), lambda i, j, qb=qb: (qb, 0, j)))
        elif trans_b:
            in_specs.append(pl.BlockSpec((tn, k), lambda i, j: (j, 0)))
        else:
            in_specs.append(pl.BlockSpec((k, tn), lambda i, j: (0, j)))
        args += [a, b]
    if add is not None:
        in_specs.append(pl.BlockSpec((tm, tn), lambda i, j: (i, j)))
        args.append(add)
    if after is not None:
        in_specs.append(pl.BlockSpec(memory_space=pl.ANY))
        args.append(after)
    return pl.pallas_call(
        body, name=name, grid=(m // tm, n // tn), in_specs=in_specs,
        out_specs=pl.BlockSpec((tm, tn), lambda i, j: (i, j)),
        out_shape=jax.ShapeDtypeStruct((m, n), out_dtype),
        compiler_params=_params(("parallel", "parallel")),
    )(*args)


def _matmul_tn(a, b, *, tk, tn, tm=1024, out_dtype=BF16, stack_out=False, after=None, part_of=None, name):
    m, k = a.shape
    n = b.shape[1]
    tm, tk, tn = min(tm, m), min(tk, k), min(tn, n)
    assert m % tm == 0 and k % tk == 0 and n % tn == 0, (name, m, k, n)
    nm = m // tm
    blocks, first, buffer = part_of if part_of is not None else (None, 0, None)
    if stack_out:
        out_spec = pl.BlockSpec((None, tk, tn), lambda i, j, l: (j + first, i, 0))
        out_shape = jax.ShapeDtypeStruct((blocks or n // tn, k, tn), out_dtype)
    else:
        out_spec = pl.BlockSpec((tk, tn), lambda i, j, l: (i + first, j))
        out_shape = jax.ShapeDtypeStruct((blocks * tk if blocks else k, n), out_dtype)

    def body(a_ref, b_ref, *rest):
        o_ref, acc = rest[-2:]
        mi = pl.program_id(2)

        @pl.when(mi == 0)
        def _():
            acc[...] = jnp.zeros_like(acc)

        acc[...] += _dot(a_ref[...].astype(BF16), b_ref[...].astype(BF16), TN)

        @pl.when(mi == nm - 1)
        def _():
            o_ref[...] = acc[...].astype(out_dtype)

    in_specs = [pl.BlockSpec((tm, tk), lambda i, j, l: (l, i)), pl.BlockSpec((tm, tn), lambda i, j, l: (l, j))]
    args = [a, b]
    if after is not None:
        in_specs.append(pl.BlockSpec(memory_space=pl.ANY))
        args.append(after)
    aliases = {}
    if buffer is not None:
        aliases = {len(args): 0}
        in_specs.append(pl.BlockSpec(memory_space=pl.ANY))
        args.append(buffer)
    return pl.pallas_call(
        body, name=name, grid=(k // tk, n // tn, nm), in_specs=in_specs,
        out_specs=out_spec, out_shape=out_shape, input_output_aliases=aliases,
        scratch_shapes=[pltpu.VMEM((tk, tn), F32)],
        compiler_params=_params(("parallel", "parallel", "arbitrary")),
    )(*args)


def _rms_fwd(x, w, *, after=None, name, tm=512):
    s, d = x.shape
    tm = min(tm, s)
    extra = [] if after is None else [after]

    def body(x_ref, w_ref, *rest):
        o_ref = rest[-1]
        xv = x_ref[...]
        r = lax.rsqrt(jnp.mean(xv * xv, axis=-1, keepdims=True) + NORM_EPS)
        o_ref[...] = (xv * r * w_ref[...]).astype(BF16)

    return pl.pallas_call(
        body, name=name, grid=(s // tm,),
        in_specs=[pl.BlockSpec((tm, d), lambda i: (i, 0)), pl.BlockSpec((1, d), lambda i: (0, 0))]
        + [pl.BlockSpec(memory_space=pl.ANY)] * len(extra),
        out_specs=pl.BlockSpec((tm, d), lambda i: (i, 0)),
        out_shape=jax.ShapeDtypeStruct((s, d), BF16),
        compiler_params=_params(("parallel",)),
    )(x, w, *extra)


def _rms_bwd(x, w, dn, dres, *, bf16_copy, name, tm=512):
    s, d = x.shape
    tm = min(tm, s)

    def body(x_ref, w_ref, dn_ref, dres_ref, dx_ref, *rest):
        dw_ref = rest[-1]

        @pl.when(pl.program_id(0) == 0)
        def _():
            dw_ref[...] = jnp.zeros_like(dw_ref)

        xv = x_ref[...]
        r = lax.rsqrt(jnp.mean(xv * xv, axis=-1, keepdims=True) + NORM_EPS)
        xhat = xv * r
        dnv = dn_ref[...].astype(F32)
        dxhat = dnv * w_ref[...]
        dx = dres_ref[...] + r * (dxhat - xhat * jnp.mean(dxhat * xhat, axis=-1, keepdims=True))
        dx_ref[...] = dx
        if bf16_copy:
            rest[0][...] = dx.astype(BF16)
        dw_ref[...] += jnp.sum(dnv * xhat, axis=0, keepdims=True)

    tile = pl.BlockSpec((tm, d), lambda i: (i, 0))
    row = pl.BlockSpec((1, d), lambda i: (0, 0))
    copies = [jax.ShapeDtypeStruct((s, d), BF16)] if bf16_copy else []
    return pl.pallas_call(
        body, name=name, grid=(s // tm,),
        in_specs=[tile, row, tile, tile], out_specs=[tile] + [tile] * len(copies) + [row],
        out_shape=[jax.ShapeDtypeStruct((s, d), F32)] + copies + [jax.ShapeDtypeStruct((1, d), F32)],
        compiler_params=_params(("arbitrary",)),
    )(x, w, dn, dres)


def _final_fwd_bwd(h2, wf, target, *, name, tm=512):
    s, d = h2.shape
    tm = min(tm, s)

    def body(h_ref, w_ref, t_ref, loss_ref, dh_ref, dhb_ref, dw_ref):
        @pl.when(pl.program_id(0) == 0)
        def _():
            dw_ref[...] = jnp.zeros_like(dw_ref)
            loss_ref[...] = jnp.zeros_like(loss_ref)

        hv = h_ref[...]
        r = lax.rsqrt(jnp.mean(hv * hv, axis=-1, keepdims=True) + NORM_EPS)
        xhat = hv * r
        err = xhat * w_ref[...] - t_ref[...]
        per_tok = jnp.mean(err * err, axis=-1, keepdims=True)
        loss_ref[...] += 0.5 * jnp.sum(per_tok, axis=0, keepdims=True)
        dy = err * (1.0 / d)
        dxhat = dy * w_ref[...]
        dh = r * (dxhat - xhat * jnp.mean(dxhat * xhat, axis=-1, keepdims=True))
        dh_ref[...] = dh
        dhb_ref[...] = dh.astype(BF16)
        dw_ref[...] += jnp.sum(dy * xhat, axis=0, keepdims=True)

    tile = pl.BlockSpec((tm, d), lambda i: (i, 0))
    row = pl.BlockSpec((1, d), lambda i: (0, 0))
    return pl.pallas_call(
        body, name=name, grid=(s // tm,),
        in_specs=[tile, row, tile],
        out_specs=[pl.BlockSpec((1, 1), lambda i: (0, 0)), tile, tile, row],
        out_shape=[jax.ShapeDtypeStruct((1, 1), F32), jax.ShapeDtypeStruct((s, d), F32),
                   jax.ShapeDtypeStruct((s, d), BF16), jax.ShapeDtypeStruct((1, d), F32)],
        compiler_params=_params(("arbitrary",)),
    )(h2, wf, target)


CONV_ROWS = 256
CONV_ROWS_FWD = 512
HALO = 8


def _rows_with_halo(ref, r0, rows, s, before, after):
    tile = 16 if ref.dtype == BF16 else HALO
    parts = []
    if before:
        prev = ref[pl.ds(pl.multiple_of(jnp.maximum(r0 - tile, 0), tile), tile), :].astype(F32)[tile - HALO:]
        parts.append(jnp.where(r0 > 0, prev, 0.0))
    parts.append(ref[pl.ds(r0, rows), :].astype(F32))
    if after:
        nxt = ref[pl.ds(pl.multiple_of(jnp.minimum(r0 + rows, s - tile), tile), tile), :].astype(F32)[:HALO]
        parts.append(jnp.where(r0 + rows < s, nxt, 0.0))
    return jnp.concatenate(parts, axis=0) if len(parts) > 1 else parts[0]


def _window(x_ref, r0, s, after):
    return _rows_with_halo(x_ref, r0, CONV_ROWS_FWD, s, True, after).astype(F32)


def _shifted(window, k, rows):
    if k == 0:
        return window[HALO:HALO + rows]
    return pltpu.roll(window, k, 0)[HALO:HALO + rows]


def _conv_taps(window, w_ref, kk, rows):
    acc = None
    for i in range(kk):
        term = w_ref[i:i + 1, :] * _shifted(window, kk - 1 - i, rows)
        acc = term if acc is None else acc + term
    return acc


def _row_loop(rows, step):
    def body(r, carry):
        return step(pl.multiple_of(r * rows, rows), carry)
    return body


def _conv_bwd_rows(x, dpe, w_ref, kk):
    dp = dpe[:CONV_ROWS]
    dx = None
    dws = []
    for i in range(kk):
        k = kk - 1 - i
        later = dp if k == 0 else pltpu.roll(dpe, dpe.shape[0] - k, 0)[:CONV_ROWS]
        dws.append(jnp.sum(later * x, axis=0, keepdims=True))
        term = w_ref[i:i + 1, :] * later
        dx = term if dx is None else dx + term
    return dx, dws, jnp.sum(dp, axis=0, keepdims=True)


def _conv_a_fwd(xraw, w, b, *, name, tc=128):
    s, c = xraw.shape
    kk = 4

    def body(x_ref, w_ref, b_ref, o_ref, pre_ref):
        def step(r0, carry):
            pre = _conv_taps(_window(x_ref, r0, s, False), w_ref, kk, CONV_ROWS_FWD) + b_ref[...]
            o_ref[pl.ds(r0, CONV_ROWS_FWD), :] = pre * _sigmoid(pre)
            pre_ref[pl.ds(r0, CONV_ROWS_FWD), :] = pre.astype(BF16)
            return carry

        lax.fori_loop(0, s // CONV_ROWS_FWD, _row_loop(CONV_ROWS_FWD, step), 0)

    col = pl.BlockSpec((s, tc), lambda j: (0, j))
    return pl.pallas_call(
        body, name=name, grid=(c // tc,),
        in_specs=[col, pl.BlockSpec((8, tc), lambda j: (0, j)), pl.BlockSpec((1, tc), lambda j: (0, j))],
        out_specs=[col, col], out_shape=[jax.ShapeDtypeStruct((s, c), F32), jax.ShapeDtypeStruct((s, c), BF16)],
        compiler_params=_params(("parallel",)),
    )(xraw, w, b)


def _conv_a_bwd(xraw, pre, w, dy, *, name, tc=128):
    s, c = xraw.shape
    kk = 4

    def body(x_ref, pre_ref, w_ref, dy_ref, dx_ref, dw_ref, db_ref):
        def step(r0, carry):
            pre = _rows_with_halo(pre_ref, r0, CONV_ROWS, s, False, True)
            sg = _sigmoid(pre)
            dpe = _rows_with_halo(dy_ref, r0, CONV_ROWS, s, False, True) * (sg * (1.0 + pre * (1.0 - sg)))
            dx, dws, db = _conv_bwd_rows(x_ref[pl.ds(r0, CONV_ROWS), :].astype(F32), dpe, w_ref, kk)
            dx_ref[pl.ds(r0, CONV_ROWS), :] = dx.astype(BF16)
            return tuple(acc + new for acc, new in zip(carry, dws + [db]))

        zero = jnp.zeros((1, tc), F32)
        sums = lax.fori_loop(0, s // CONV_ROWS, _row_loop(CONV_ROWS, step), (zero,) * (kk + 1))
        db_ref[...] = sums[kk]
        dw_ref[...] = jnp.concatenate(list(sums[:kk]) + [jnp.zeros((8 - kk, tc), F32)], axis=0)

    col = pl.BlockSpec((s, tc), lambda j: (0, j))
    w8 = pl.BlockSpec((8, tc), lambda j: (0, j))
    row = pl.BlockSpec((1, tc), lambda j: (0, j))
    return pl.pallas_call(
        body, name=name, grid=(c // tc,),
        in_specs=[col, col, w8, col], out_specs=[col, w8, row],
        out_shape=[jax.ShapeDtypeStruct((s, c), BF16), jax.ShapeDtypeStruct((8, c), F32),
                   jax.ShapeDtypeStruct((1, c), F32)],
        compiler_params=_params(("parallel",)),
    )(xraw, pre, w, dy)


def _conv_f_fwd(up_raw, w, b, *, name, tc=128):
    s, c2 = up_raw.shape
    c = c2 // 2
    nb = c // tc
    kk = 3

    def body(xa_ref, xv_ref, wa_ref, wv_ref, ba_ref, bv_ref, o_ref, a_out, v_out):
        def step(r0, carry):
            a = _conv_taps(_window(xa_ref, r0, s, False), wa_ref, kk, CONV_ROWS_FWD) + ba_ref[...]
            v = _conv_taps(_window(xv_ref, r0, s, False), wv_ref, kk, CONV_ROWS_FWD) + bv_ref[...]
            o_ref[pl.ds(r0, CONV_ROWS_FWD), :] = (a * _sigmoid(a) * v).astype(BF16)
            a_out[pl.ds(r0, CONV_ROWS_FWD), :] = a.astype(BF16)
            v_out[pl.ds(r0, CONV_ROWS_FWD), :] = v.astype(BF16)
            return carry

        lax.fori_loop(0, s // CONV_ROWS_FWD, _row_loop(CONV_ROWS_FWD, step), 0)

    col_a = pl.BlockSpec((s, tc), lambda j: (0, j))
    col_v = pl.BlockSpec((s, tc), lambda j: (0, j + nb))
    half = jax.ShapeDtypeStruct((s, c), BF16)
    return pl.pallas_call(
        body, name=name, grid=(nb,),
        in_specs=[col_a, col_v, pl.BlockSpec((8, tc), lambda j: (0, j)), pl.BlockSpec((8, tc), lambda j: (0, j + nb)),
                  pl.BlockSpec((1, tc), lambda j: (0, j)), pl.BlockSpec((1, tc), lambda j: (0, j + nb))],
        out_specs=[col_a, col_a, col_a], out_shape=[half, half, half],
        compiler_params=_params(("parallel",)),
    )(up_raw, up_raw, w, w, b, b)


def _conv_f_bwd(up_raw, a_pre, v_pre, w, dact, *, name, tc=128):
    s, c2 = up_raw.shape
    c = c2 // 2
    nb = c // tc
    kk = 3

    def body(xa_ref, xv_ref, a_ref, v_ref, wa_ref, wv_ref, d_ref,
             dxa_ref, dxv_ref, dwa_ref, dwv_ref, dba_ref, dbv_ref):
        def step(r0, carry):
            a = _rows_with_halo(a_ref, r0, CONV_ROWS, s, False, True)
            v = _rows_with_halo(v_ref, r0, CONV_ROWS, s, False, True)
            sg = _sigmoid(a)
            d = _rows_with_halo(d_ref, r0, CONV_ROWS, s, False, True)
            rows = pl.ds(r0, CONV_ROWS)
            dxa, dwas, dba = _conv_bwd_rows(xa_ref[rows, :].astype(F32), d * v * (sg * (1.0 + a * (1.0 - sg))),
                                            wa_ref, kk)
            dxv, dwvs, dbv = _conv_bwd_rows(xv_ref[rows, :].astype(F32), d * (a * sg), wv_ref, kk)
            dxa_ref[pl.ds(r0, CONV_ROWS), :] = dxa.astype(BF16)
            dxv_ref[pl.ds(r0, CONV_ROWS), :] = dxv.astype(BF16)
            return tuple(acc + new for acc, new in zip(carry, dwas + [dba] + dwvs + [dbv]))

        zero = jnp.zeros((1, tc), F32)
        sums = lax.fori_loop(0, s // CONV_ROWS, _row_loop(CONV_ROWS, step), (zero,) * (2 * kk + 2))
        pad = [jnp.zeros((8 - kk, tc), F32)]
        dwa_ref[...] = jnp.concatenate(list(sums[:kk]) + pad, axis=0)
        dba_ref[...] = sums[kk]
        dwv_ref[...] = jnp.concatenate(list(sums[kk + 1:2 * kk + 1]) + pad, axis=0)
        dbv_ref[...] = sums[2 * kk + 1]

    col_a = pl.BlockSpec((s, tc), lambda j: (0, j))
    col_v = pl.BlockSpec((s, tc), lambda j: (0, j + nb))
    w_a = pl.BlockSpec((8, tc), lambda j: (0, j))
    w_v = pl.BlockSpec((8, tc), lambda j: (0, j + nb))
    r_a = pl.BlockSpec((1, tc), lambda j: (0, j))
    r_v = pl.BlockSpec((1, tc), lambda j: (0, j + nb))
    outs = pl.pallas_call(
        body, name=name, grid=(nb,),
        in_specs=[col_a, col_v, col_a, col_a, w_a, w_v, col_a],
        out_specs=[col_a, col_a, w_a, w_a, r_a, r_a],
        out_shape=[jax.ShapeDtypeStruct((s, c), BF16), jax.ShapeDtypeStruct((s, c), BF16),
                   jax.ShapeDtypeStruct((8, c), F32), jax.ShapeDtypeStruct((8, c), F32),
                   jax.ShapeDtypeStruct((1, c), F32), jax.ShapeDtypeStruct((1, c), F32)],
        compiler_params=_params(("parallel",)),
    )(up_raw, up_raw, a_pre, v_pre, w, w, dact)
    return outs


def _tri_masks():
    row = lax.broadcasted_iota(jnp.int32, (CHUNK, CHUNK), 0)
    col = lax.broadcasted_iota(jnp.int32, (CHUNK, CHUNK), 1)
    return row >= col, row <= col


def _ssd_fwd(xbc, dt_raw, z, dt_bias, a_log, a_log_x, d_skip_x, norm_w, expand, *, name):
    s = xbc.shape[0]
    nc = s // CHUNK

    def body(xbc_ref, dtr_ref, z_ref, dtb_ref, alog_ref, alogx_ref, dskx_ref, nw_ref, e_ref,
             y_ref, ya_ref, st_ref, state):
        @pl.when(pl.program_id(0) == 0)
        def _():
            state[...] = jnp.zeros_like(state)

        st_ref[0] = state[...]
        lower, _ = _tri_masks()
        dt = _softplus(dtr_ref[...] + dtb_ref[...])
        adt = dt * (-jnp.exp(alog_ref[...]))
        acum = _dot_exact_lhs(lower.astype(BF16), _split3(adt))
        acum_t = acum.T
        dt_terms, acum_terms = _split3(dt), _split3(acum)
        for g in range(SSD_GROUPS):
            sl = slice(GROUP_COLS * g, GROUP_COLS * (g + 1))
            dt_x = _dot_terms(dt_terms[:2], e_ref[:, sl])
            acum_x = _dot_terms(acum_terms, e_ref[:, sl])
            tot_x = jnp.sum(dt_x * (-jnp.exp(alogx_ref[:, sl])), axis=0, keepdims=True)
            xs = xbc_ref[:, sl]
            xdt = xs * dt_x
            xdt_b = xdt.astype(BF16)
            bg = xbc_ref[:, SSD_D_INNER + SSD_STATE * g:SSD_D_INNER + SSD_STATE * (g + 1)].astype(BF16)
            cg = xbc_ref[:, SSD_D_INNER + SSD_BC + SSD_STATE * g:SSD_D_INNER + SSD_BC + SSD_STATE * (g + 1)].astype(BF16)
            cb = _dot(cg, bg, NT)
            st_g = state[:, sl]
            y_off = _dot(cg, st_g.astype(BF16)) * jnp.exp(acum_x)
            parts = []
            for r in range(SSD_HEADS_PER_GROUP):
                h = SSD_HEADS_PER_GROUP * g + r
                dec = jnp.exp(jnp.where(lower, acum[:, h:h + 1] - acum_t[h:h + 1, :], -jnp.inf))
                parts.append(_dot((cb * dec).astype(BF16), xdt_b[:, SSD_HEAD_DIM * r:SSD_HEAD_DIM * (r + 1)]))
            y_ref[:, sl] = jnp.concatenate(parts, axis=1) + y_off + dskx_ref[:, sl] * xs
            wgt = (xdt * jnp.exp(tot_x - acum_x)).astype(BF16)
            state[:, sl] = st_g * jnp.exp(tot_x) + _dot(bg, wgt, TN)
        zv = z_ref[...].astype(F32)
        q = y_ref[...] * (zv * _sigmoid(zv))
        r = lax.rsqrt(jnp.mean(q * q, axis=-1, keepdims=True) + NORM_EPS)
        ya_ref[...] = (q * r * nw_ref[...]).astype(BF16)

    def chunk(w):
        return pl.BlockSpec((CHUNK, w), lambda c: (c, 0))

    def const(shape):
        return pl.BlockSpec(shape, lambda c: (0,) * len(shape))

    return pl.pallas_call(
        body, name=name, grid=(nc,),
        in_specs=[chunk(SSD_XBC), chunk(LANES), chunk(SSD_D_INNER), const((1, LANES)), const((1, LANES)),
                  const((1, SSD_D_INNER)), const((1, SSD_D_INNER)), const((1, SSD_D_INNER)),
                  const((LANES, SSD_D_INNER))],
        out_specs=[chunk(SSD_D_INNER), chunk(SSD_D_INNER),
                   pl.BlockSpec((1, SSD_STATE, SSD_D_INNER), lambda c: (c, 0, 0))],
        out_shape=[jax.ShapeDtypeStruct((s, SSD_D_INNER), F32), jax.ShapeDtypeStruct((s, SSD_D_INNER), BF16),
                   jax.ShapeDtypeStruct((nc, SSD_STATE, SSD_D_INNER), F32)],
        scratch_shapes=[pltpu.VMEM((SSD_STATE, SSD_D_INNER), F32)],
        compiler_params=_params(("arbitrary",)),
    )(xbc, dt_raw, z, dt_bias, a_log, a_log_x, d_skip_x, norm_w, expand)


def _ssd_bwd(dya, y, z, xbc, dt_raw, states, dt_bias, a_log, a_log_x, d_skip_x, norm_w, expand, expand_t, *, name):
    s = xbc.shape[0]
    nc = s // CHUNK

    def body(dya_ref, y_ref, z_ref, xbc_ref, dtr_ref, stp_ref, dtb_ref, alog_ref, alogx_ref, dskx_ref, nw_ref,
             e_ref, et_ref, dz_ref, dxbc_ref, ddt_ref, dnw_ref, ddsk_ref, dalog_ref, ddtb_ref,
             dstate, dy_sc, dskcol):
        i = pl.program_id(0)

        @pl.when(i == 0)
        def _():
            dstate[...] = jnp.zeros_like(dstate)
            dskcol[...] = jnp.zeros_like(dskcol)
            dnw_ref[...] = jnp.zeros_like(dnw_ref)
            dalog_ref[...] = jnp.zeros_like(dalog_ref)
            ddtb_ref[...] = jnp.zeros_like(ddtb_ref)
            ddsk_ref[...] = jnp.zeros_like(ddsk_ref)

        lower, upper = _tri_masks()
        rows = lax.broadcasted_iota(jnp.int32, (CHUNK, LANES), 0)
        pre = dtr_ref[...] + dtb_ref[...]
        dt = _softplus(pre)
        a = -jnp.exp(alog_ref[...])
        acum = _dot_exact_lhs(lower.astype(BF16), _split3(dt * a))
        acum_t = acum.T
        dt_terms, acum_terms = _split3(dt), _split3(acum)

        yv = y_ref[...]
        zv = z_ref[...].astype(F32)
        sz = _sigmoid(zv)
        silu_z = zv * sz
        q = yv * silu_z
        r = lax.rsqrt(jnp.mean(q * q, axis=-1, keepdims=True) + NORM_EPS)
        qhat = q * r
        dyav = dya_ref[...]
        dqhat = dyav * nw_ref[...]
        dnw_ref[...] += jnp.sum(dyav * qhat, axis=0, keepdims=True)
        dq = r * (dqhat - qhat * jnp.mean(dqhat * qhat, axis=-1, keepdims=True))
        dy_sc[...] = dq * silu_z
        dz_ref[...] = (dq * yv * (sz * (1.0 + zv * (1.0 - sz)))).astype(BF16)

        da_cum = jnp.zeros((CHUNK, LANES), F32)
        ddt = jnp.zeros((CHUNK, LANES), F32)
        for g in range(SSD_GROUPS):
            sl = slice(GROUP_COLS * g, GROUP_COLS * (g + 1))
            et_g = et_ref[sl, :]
            dt_x = _dot_terms(dt_terms[:2], e_ref[:, sl])
            acum_x = _dot_terms(acum_terms, e_ref[:, sl])
            tot_x = jnp.sum(dt_x * (-jnp.exp(alogx_ref[:, sl])), axis=0, keepdims=True)
            e_tot = jnp.exp(tot_x)
            dec_s = jnp.exp(tot_x - acum_x)
            xs = xbc_ref[:, sl]
            xdt = xs * dt_x
            xdt_b = xdt.astype(BF16)
            dy = dy_sc[:, sl]
            dy_b = dy.astype(BF16)
            dskx = dskx_ref[:, sl]
            y_ssd = y_ref[:, sl] - dskx * xs
            dskcol[:, sl] += jnp.sum(dy * xs, axis=0, keepdims=True)
            bg = xbc_ref[:, SSD_D_INNER + SSD_STATE * g:SSD_D_INNER + SSD_STATE * (g + 1)].astype(BF16)
            cg = xbc_ref[:, SSD_D_INNER + SSD_BC + SSD_STATE * g:SSD_D_INNER + SSD_BC + SSD_STATE * (g + 1)].astype(BF16)
            cb_t = _dot(bg, cg, NT)
            sp = stp_ref[0, :, sl]
            ds_g = dstate[:, sl]
            ds_b = ds_g.astype(BF16)
            dye_b = (dy * jnp.exp(acum_x)).astype(BF16)
            dc = _dot(dye_b, sp.astype(BF16), NT)
            dxdt_state = dec_s * _dot(bg, ds_b)
            db = _dot((xdt * dec_s).astype(BF16), ds_b, NT)
            dcb_t = jnp.zeros((CHUNK, CHUNK), F32)
            parts = []
            for rr in range(SSD_HEADS_PER_GROUP):
                h = SSD_HEADS_PER_GROUP * g + rr
                hs = slice(SSD_HEAD_DIM * rr, SSD_HEAD_DIM * (rr + 1))
                dec_t = jnp.exp(jnp.where(upper, acum_t[h:h + 1, :] - acum[:, h:h + 1], -jnp.inf))
                parts.append(_dot((cb_t * dec_t).astype(BF16), dy_b[:, hs]))
                dcb_t = dcb_t + _dot(xdt_b[:, hs], dy_b[:, hs], NT) * dec_t
            dxdt = jnp.concatenate(parts, axis=1) + dxdt_state
            dcb_tb = dcb_t.astype(BF16)
            dc = dc + _dot(dcb_tb, bg, TN)
            db = db + _dot(dcb_tb, cg)
            tot_col = jnp.sum(ds_g * sp, axis=0, keepdims=True) * e_tot + jnp.sum(dxdt_state * xdt, axis=0, keepdims=True)
            d_tot = _dot_terms(_split3(jnp.broadcast_to(tot_col, (8, GROUP_COLS))), et_g)
            d_tot = jnp.max(d_tot, axis=0, keepdims=True)
            pair_sums = dy_b.astype(F32) * y_ssd - xdt_b.astype(F32) * dxdt
            da_cum = da_cum + _dot_terms(_split3(pair_sums), et_g) + jnp.where(rows == CHUNK - 1, d_tot, 0.0)
            ddt = ddt + _dot_terms(_split3(dxdt * xs)[:2], et_g)
            dxbc_ref[:, sl] = dy * dskx + dxdt * dt_x
            dxbc_ref[:, SSD_D_INNER + SSD_STATE * g:SSD_D_INNER + SSD_STATE * (g + 1)] = db
            dxbc_ref[:, SSD_D_INNER + SSD_BC + SSD_STATE * g:SSD_D_INNER + SSD_BC + SSD_STATE * (g + 1)] = dc
            dstate[:, sl] = e_tot * ds_g + _dot(cg, dye_b, TN)

        dadt = _dot_exact_lhs(upper.astype(BF16), _split3(da_cum))
        ddt = ddt + dadt * a
        dalog_ref[...] += jnp.sum(dadt * dt, axis=0, keepdims=True)
        dpre = ddt * _sigmoid(pre)
        ddtb_ref[...] += jnp.sum(dpre, axis=0, keepdims=True)
        ddt_ref[...] = dpre.astype(BF16)

        @pl.when(i == nc - 1)
        def _():
            dalog_ref[...] = dalog_ref[...] * a
            dsk = _dot_terms(_split3(jnp.broadcast_to(dskcol[...], (8, SSD_D_INNER))), et_ref[...])
            ddsk_ref[...] = jnp.max(dsk, axis=0, keepdims=True)

    def chunk(w):
        return pl.BlockSpec((CHUNK, w), lambda i: (nc - 1 - i, 0))

    def const(shape):
        return pl.BlockSpec(shape, lambda i: (0,) * len(shape))

    return pl.pallas_call(
        body, name=name, grid=(nc,),
        in_specs=[chunk(SSD_D_INNER), chunk(SSD_D_INNER), chunk(SSD_D_INNER), chunk(SSD_XBC), chunk(LANES),
                  pl.BlockSpec((1, SSD_STATE, SSD_D_INNER), lambda i: (nc - 1 - i, 0, 0)),
                  const((1, LANES)), const((1, LANES)), const((1, SSD_D_INNER)), const((1, SSD_D_INNER)),
                  const((1, SSD_D_INNER)), const((LANES, SSD_D_INNER)), const((SSD_D_INNER, LANES))],
        out_specs=[chunk(SSD_D_INNER), chunk(SSD_XBC), chunk(LANES), const((1, SSD_D_INNER)), const((1, LANES)),
                   const((1, LANES)), const((1, LANES))],
        out_shape=[jax.ShapeDtypeStruct((s, SSD_D_INNER), BF16), jax.ShapeDtypeStruct((s, SSD_XBC), F32),
                   jax.ShapeDtypeStruct((s, LANES), BF16), jax.ShapeDtypeStruct((1, SSD_D_INNER), F32),
                   jax.ShapeDtypeStruct((1, LANES), F32), jax.ShapeDtypeStruct((1, LANES), F32),
                   jax.ShapeDtypeStruct((1, LANES), F32)],
        scratch_shapes=[pltpu.VMEM((SSD_STATE, SSD_D_INNER), F32), pltpu.VMEM((CHUNK, SSD_D_INNER), F32),
                        pltpu.VMEM((1, SSD_D_INNER), F32)],
        compiler_params=_params(("arbitrary",)),
    )(dya, y, z, xbc, dt_raw, states, dt_bias, a_log, a_log_x, d_skip_x, norm_w, expand, expand_t)


GELU_K = math.sqrt(2.0 / math.pi)
GELU_C = 0.044715


def _gelu(x):
    return 0.5 * x * (1.0 + jnp.tanh(GELU_K * (x + GELU_C * x * x * x)))


def _gelu_grad(x):
    t = jnp.tanh(GELU_K * (x + GELU_C * x * x * x))
    return 0.5 * (1.0 + t) + 0.5 * x * (1.0 - t * t) * (GELU_K * (1.0 + 3.0 * GELU_C * x * x))


def _sgu_pre(uv_ref, uvb_ref, lnw_ref, lnb_ref):
    uv = uv_ref[...].astype(F32) + uvb_ref[...]
    guv = _gelu(uv)
    u = guv[:, :SGU_WIDTH]
    v = guv[:, SGU_WIDTH:]
    mu = jnp.mean(v, axis=-1, keepdims=True)
    vc = v - mu
    rstd = lax.rsqrt(jnp.mean(vc * vc, axis=-1, keepdims=True) + LN_EPS)
    vhat = vc * rstd
    vn = vhat * lnw_ref[...] + lnb_ref[...]
    return uv, u, vhat, rstd, vn


def _sgu_fwd(uv_raw, uv_b, ln_w, ln_b, w_sp, b_sp_t, *, name):
    s = uv_raw.shape[0]
    nc = s // CHUNK

    def body(uv_ref, uvb_ref, lnw_ref, lnb_ref, w_ref, bt_ref, o_ref):
        lower, _ = _tri_masks()
        _, u, _, _, vn = _sgu_pre(uv_ref, uvb_ref, lnw_ref, lnb_ref)
        vn_b = vn.astype(BF16)
        bt = bt_ref[...]
        for g in range(SGU_GROUPS):
            gs = slice(LANES * g, LANES * (g + 1))
            wc = jnp.where(lower, w_ref[g], 0.0).astype(BF16)
            mixed = _dot(wc, vn_b[:, gs]) + bt[:, g:g + 1]
            o_ref[:, gs] = (u[:, gs] * mixed).astype(BF16)

    def const(shape):
        return pl.BlockSpec(shape, lambda c: (0,) * len(shape))

    return pl.pallas_call(
        body, name=name, grid=(nc,),
        in_specs=[pl.BlockSpec((CHUNK, 2 * SGU_WIDTH), lambda c: (c, 0)), const((1, 2 * SGU_WIDTH)),
                  const((1, SGU_WIDTH)), const((1, SGU_WIDTH)), const((SGU_GROUPS, CHUNK, CHUNK)),
                  const((CHUNK, LANES))],
        out_specs=pl.BlockSpec((CHUNK, SGU_WIDTH), lambda c: (c, 0)),
        out_shape=jax.ShapeDtypeStruct((s, SGU_WIDTH), BF16),
        compiler_params=_params(("parallel",)),
    )(uv_raw, uv_b, ln_w, ln_b, w_sp, b_sp_t)


def _sgu_bwd(uv_raw, dyb, uv_b, ln_w, ln_b, w_sp, b_sp_t, group_sum, *, name):
    s = uv_raw.shape[0]
    nc = s // CHUNK

    def body(uv_ref, dy_ref, uvb_ref, lnw_ref, lnb_ref, w_ref, bt_ref, gsum_ref,
             duv_ref, dw_ref, dbt_ref, dlnw_ref, dlnb_ref, duvb_ref):
        @pl.when(pl.program_id(0) == 0)
        def _():
            dw_ref[...] = jnp.zeros_like(dw_ref)
            dbt_ref[...] = jnp.zeros_like(dbt_ref)
            dlnw_ref[...] = jnp.zeros_like(dlnw_ref)
            dlnb_ref[...] = jnp.zeros_like(dlnb_ref)
            duvb_ref[...] = jnp.zeros_like(duvb_ref)

        lower, _ = _tri_masks()
        uv, u, vhat, rstd, vn = _sgu_pre(uv_ref, uvb_ref, lnw_ref, lnb_ref)
        vn_b = vn.astype(BF16)
        bt = bt_ref[...]
        dy = dy_ref[...].astype(F32)
        du_parts, dvn_parts, dmix_parts = [], [], []
        for g in range(SGU_GROUPS):
            gs = slice(LANES * g, LANES * (g + 1))
            wc = jnp.where(lower, w_ref[g], 0.0).astype(BF16)
            mixed = _dot(wc, vn_b[:, gs]) + bt[:, g:g + 1]
            du_parts.append(dy[:, gs] * mixed)
            dmix = dy[:, gs] * u[:, gs]
            dmix_b = dmix.astype(BF16)
            dmix_parts.append(dmix)
            dw_ref[g] += jnp.where(lower, _dot(dmix_b, vn_b[:, gs], NT), 0.0)
            dvn_parts.append(_dot(wc, dmix_b, TN))
        dmixed = jnp.concatenate(dmix_parts, axis=1)
        dbt_ref[...] += _dot_terms(_split3(dmixed), gsum_ref[...])
        dvn = jnp.concatenate(dvn_parts, axis=1)
        dlnw_ref[...] += jnp.sum(dvn * vhat, axis=0, keepdims=True)
        dlnb_ref[...] += jnp.sum(dvn, axis=0, keepdims=True)
        dvhat = dvn * lnw_ref[...]
        dv = rstd * (dvhat - jnp.mean(dvhat, axis=-1, keepdims=True)
                     - vhat * jnp.mean(dvhat * vhat, axis=-1, keepdims=True))
        dguv = jnp.concatenate(du_parts + [dv], axis=1)
        duv = dguv * _gelu_grad(uv)
        duvb_ref[...] += jnp.sum(duv, axis=0, keepdims=True)
        duv_ref[...] = duv.astype(BF16)

    def const(shape):
        return pl.BlockSpec(shape, lambda c: (0,) * len(shape))

    return pl.pallas_call(
        body, name=name, grid=(nc,),
        in_specs=[pl.BlockSpec((CHUNK, 2 * SGU_WIDTH), lambda c: (c, 0)),
                  pl.BlockSpec((CHUNK, SGU_WIDTH), lambda c: (c, 0)), const((1, 2 * SGU_WIDTH)),
                  const((1, SGU_WIDTH)), const((1, SGU_WIDTH)), const((SGU_GROUPS, CHUNK, CHUNK)),
                  const((CHUNK, LANES)), const((SGU_WIDTH, LANES))],
        out_specs=[pl.BlockSpec((CHUNK, 2 * SGU_WIDTH), lambda c: (c, 0)), const((SGU_GROUPS, CHUNK, CHUNK)),
                   const((CHUNK, LANES)), const((1, SGU_WIDTH)), const((1, SGU_WIDTH)), const((1, 2 * SGU_WIDTH))],
        out_shape=[jax.ShapeDtypeStruct((s, 2 * SGU_WIDTH), BF16),
                   jax.ShapeDtypeStruct((SGU_GROUPS, CHUNK, CHUNK), F32), jax.ShapeDtypeStruct((CHUNK, LANES), F32),
                   jax.ShapeDtypeStruct((1, SGU_WIDTH), F32), jax.ShapeDtypeStruct((1, SGU_WIDTH), F32),
                   jax.ShapeDtypeStruct((1, 2 * SGU_WIDTH), F32)],
        compiler_params=_params(("arbitrary",)),
    )(uv_raw, dyb, uv_b, ln_w, ln_b, w_sp, b_sp_t, group_sum)


def _gate_fwd(gates_raw, b_gate, p_a, p_b, *, name, tm=512):
    s = p_a.shape[0]
    tm = min(tm, s)

    def body(ga_ref, gb_ref, ba_ref, bb_ref, pa_ref, pb_ref, o_ref):
        ga = _sigmoid(ga_ref[...].astype(F32) + ba_ref[...])
        gb = _sigmoid(gb_ref[...].astype(F32) + bb_ref[...])
        o_ref[...] = (ga * pa_ref[...].astype(F32) + gb * pb_ref[...].astype(F32)).astype(BF16)

    t_a = pl.BlockSpec((tm, D_MODEL), lambda i: (i, 0))
    t_b = pl.BlockSpec((tm, D_MODEL), lambda i: (i, 1))
    r_a = pl.BlockSpec((1, D_MODEL), lambda i: (0, 0))
    r_b = pl.BlockSpec((1, D_MODEL), lambda i: (0, 1))
    return pl.pallas_call(
        body, name=name, grid=(s // tm,),
        in_specs=[t_a, t_b, r_a, r_b, t_a, t_a], out_specs=t_a,
        out_shape=jax.ShapeDtypeStruct((s, D_MODEL), BF16),
        compiler_params=_params(("parallel",)),
    )(gates_raw, gates_raw, b_gate, b_gate, p_a, p_b)


def _gate_bwd(gates_raw, b_gate, p_a, p_b, dm, *, name, tm=512):
    s = p_a.shape[0]
    tm = min(tm, s)

    def body(ga_ref, gb_ref, ba_ref, bb_ref, pa_ref, pb_ref, dm_ref, dpa_ref, dpb_ref, dga_ref, dgb_ref,
             dba_ref, dbb_ref):
        @pl.when(pl.program_id(0) == 0)
        def _():
            dba_ref[...] = jnp.zeros_like(dba_ref)
            dbb_ref[...] = jnp.zeros_like(dbb_ref)

        d = dm_ref[...].astype(F32)
        for g_ref, b_ref, p_ref, dp_ref, dg_ref, db_ref in ((ga_ref, ba_ref, pa_ref, dpa_ref, dga_ref, dba_ref),
                                                            (gb_ref, bb_ref, pb_ref, dpb_ref, dgb_ref, dbb_ref)):
            sg = _sigmoid(g_ref[...].astype(F32) + b_ref[...])
            dp_ref[...] = (d * sg).astype(BF16)
            dg = d * p_ref[...].astype(F32) * (sg * (1.0 - sg))
            dg_ref[...] = dg.astype(BF16)
            db_ref[...] += jnp.sum(dg, axis=0, keepdims=True)

    t_a = pl.BlockSpec((tm, D_MODEL), lambda i: (i, 0))
    t_b = pl.BlockSpec((tm, D_MODEL), lambda i: (i, 1))
    r_a = pl.BlockSpec((1, D_MODEL), lambda i: (0, 0))
    r_b = pl.BlockSpec((1, D_MODEL), lambda i: (0, 1))
    big = jax.ShapeDtypeStruct((s, D_MODEL), BF16)
    row = jax.ShapeDtypeStruct((1, D_MODEL), F32)
    return pl.pallas_call(
        body, name=name, grid=(s // tm,),
        in_specs=[t_a, t_b, r_a, r_b, t_a, t_a, t_a], out_specs=[t_a, t_a, t_a, t_a, r_a, r_a],
        out_shape=[big, big, big, big, row, row],
        compiler_params=_params(("arbitrary",)),
    )(gates_raw, gates_raw, b_gate, b_gate, p_a, p_b, dm)


def _adamw_update(w_ref, g_ref, m_ref, v_ref, d_ref, mo_ref, vo_ref):
    gv = g_ref[...]
    mn = ADAM_B1 * m_ref[...] + (1.0 - ADAM_B1) * gv
    vn = ADAM_B2 * v_ref[...] + (1.0 - ADAM_B2) * (gv * gv)
    m_hat = mn / (1.0 - ADAM_B1 ** ADAM_STEP)
    v_hat = vn / (1.0 - ADAM_B2 ** ADAM_STEP)
    d_ref[...] = -ADAM_LR * (m_hat / (jnp.sqrt(v_hat) + ADAM_EPS) + ADAM_WD * w_ref[...])
    mo_ref[...] = mn
    vo_ref[...] = vn


def _adamw_many(ws, gs, ms, vs, *, name):
    n = len(ws)

    def body(*refs):
        for i in range(n):
            _adamw_update(*[refs[k * n + i] for k in range(7)])

    whole = pl.BlockSpec(memory_space=pltpu.VMEM)
    sds = [jax.ShapeDtypeStruct(w.shape, F32) for w in ws]
    outs = pl.pallas_call(
        body, name=name, in_specs=[whole] * (4 * n), out_specs=[whole] * (3 * n), out_shape=sds * 3,
        compiler_params=pltpu.CompilerParams(vmem_limit_bytes=VMEM_LIMIT),
    )(*ws, *gs, *ms, *vs)
    return outs[:n], outs[n:2 * n], outs[2 * n:]


def _adamw(w, g, m, v, *, name, tr=128):
    r, c = w.shape
    tr = min(tr, r)
    assert r % tr == 0, (name, r, tr)
    body = functools.partial(_adamw_update)

    blk = pl.BlockSpec((tr, c), lambda i: (i, 0))
    sds = jax.ShapeDtypeStruct((r, c), F32)
    return pl.pallas_call(
        body, name=name, grid=(r // tr,), in_specs=[blk] * 4, out_specs=[blk] * 3, out_shape=[sds] * 3,
        compiler_params=_params(("parallel",)),
    )(w, g, m, v)


def _adamw_two_sums(w, g_a, g_b, m, v, *, name, tr=128):
    r, c = w.shape
    tr = min(tr, r)
    assert r % tr == 0, (name, r, tr)

    def body(w_ref, ga_ref, gb_ref, m_ref, v_ref, g_ref, d_ref, mo_ref, vo_ref):
        g_ref[...] = ga_ref[...] + gb_ref[...]
        _adamw_update(w_ref, g_ref, m_ref, v_ref, d_ref, mo_ref, vo_ref)

    blk = pl.BlockSpec((tr, c), lambda i: (i, 0))
    sds = jax.ShapeDtypeStruct((r, c), F32)
    return pl.pallas_call(
        body, name=name, grid=(r // tr,), in_specs=[blk] * 5, out_specs=[blk] * 4, out_shape=[sds] * 4,
        compiler_params=_params(("parallel",)),
    )(w, g_a, g_b, m, v)


def _tile(n, pref):
    if n <= pref:
        return n
    best = LANES
    for t in range(LANES, pref + 1, LANES):
        if n % t == 0:
            best = t
    return best


MATMUL_BLOCK_BYTES = 20 * 1024 * 1024


def _mm(pairs, name, **kw):
    trans_b = kw.get("trans_b", False)
    m = (pairs[0][0][0] if isinstance(pairs[0][0], tuple) else pairs[0][0]).shape[0]
    ktot, n = 0, None
    for _, b in pairs:
        shape = b[0].shape[1:] if isinstance(b, tuple) else b.shape
        ktot += shape[1] if trans_b else shape[0]
        n = shape[0] if trans_b else shape[1]
    out_bytes = 4 * (2 if kw.get("add") is not None else 1)
    best = None
    for tm in (256, 512, 1024, 2048):
        for tn in range(LANES, min(n, 1536) + 1, LANES):
            if m % min(tm, m) or n % tn:
                continue
            fits = 2 * ktot * (min(tm, m) + tn) + out_bytes * min(tm, m) * tn <= MATMUL_BLOCK_BYTES
            if fits and (best is None or min(tm, m) * tn >= best[0] * best[1]):
                best = (min(tm, m), tn)
    return _matmul(pairs, tm=best[0], tn=best[1], name=name, **kw)


def _wgrad(a, b, name, **kw):
    return _matmul_tn(a, b, tk=_tile(a.shape[1], 1408), tn=kw.pop("tn", _tile(b.shape[1], 1024)), tm=2048,
                      name=name, **kw)


def _local_step(x, target, get_weight, small, emit_grad):
    heads = jnp.arange(SSD_D_INNER) // SSD_HEAD_DIM
    expand = (jnp.arange(LANES)[:, None] == heads[None, :]).astype(BF16)
    expand_t = expand.T
    group_sum = (jnp.arange(SGU_WIDTH)[:, None] // LANES == jnp.arange(LANES)[None, :]).astype(BF16)
    pad_h = LANES - SSD_HEADS
    dt_bias = jnp.pad(small["dt_bias"], ((0, 0), (0, pad_h)))
    a_log = jnp.pad(small["a_log"], ((0, 0), (0, pad_h)))
    a_log_x = jnp.repeat(small["a_log"], SSD_HEAD_DIM, axis=1)
    d_skip_x = jnp.repeat(small["d_skip"], SSD_HEAD_DIM, axis=1)
    b_sp_t = jnp.pad(small["b_spatial"][0].T, ((0, 0), (0, LANES - SGU_GROUPS)))
    w_sp = small["w_spatial"][0]
    conv_a_w = jnp.pad(small["conv_a_w"], ((0, 4), (0, 0)))
    conv_f_w = jnp.pad(small["conv_f_w"], ((0, 5), (0, 0)))
    final_w = small["final_norm_w"].reshape(1, D_MODEL)

    n1 = _rms_fwd(x, small["norm1_w"], after=small.get("gathers_started"), name="rms1_fwd")
    wts = dict(get_weight("w_in", n1))
    z = _mm([(n1, wts["in_z"])], "in_z")
    xbc_raw = _mm([(n1, wts["in_xbc"])], "in_xbc")
    dt_raw = _mm([(n1, wts["in_dt"])], "in_dt")
    uv_raw = _mm([(n1, wts["in_uv"])], "in_uv", out_dtype=BF16)
    gates_raw = _mm([(n1, wts["in_gate"])], "in_gate", out_dtype=BF16)
    xbc, xbc_pre = _conv_a_fwd(xbc_raw, conv_a_w, small["conv_a_b"], name="conv_a_fwd")
    y, y_a, states = _ssd_fwd(xbc, dt_raw, z, dt_bias, a_log, a_log_x, d_skip_x, small["ssd_norm_w"], expand,
                              name="ssd_fwd")
    y_b = _sgu_fwd(uv_raw, small["uv_b"], small["v_ln_w"], small["v_ln_b"], w_sp, b_sp_t, name="sgu_fwd")
    wts.update(get_weight("w_branch", y_b))
    p_a = _mm([(y_a, wts["branch_a"])], "branch_a", out_dtype=BF16)
    p_b = _mm([(y_b, wts["branch_b"])], "branch_b", out_dtype=BF16)
    mix = _gate_fwd(gates_raw, small["b_gate"], p_a, p_b, name="gate_fwd")
    wts.update(get_weight("w_out", mix))
    h1 = _mm([(mix, wts["out"])], "out_proj", add=x)
    n2 = _rms_fwd(h1, small["norm2_w"], name="rms2_fwd")
    wts.update(get_weight("w_up", n2))
    up_w = wts["up"]
    up_cols = up_w.shape[2]
    up_raw = _matmul([(n2, (up_w, "cols"))], tm=2048, tn=up_cols, out_dtype=BF16, name="up_proj")
    act, up_a, up_v = _conv_f_fwd(up_raw, conv_f_w, small["conv_f_b"], name="conv_f_fwd")
    wts.update(get_weight("w_down", act))
    h2 = _mm([(act, wts["down"])], "down_proj", add=h1)
    loss, dh2, dh2_b, d_final = _final_fwd_bwd(h2, final_w, target, name="final_norm_loss")

    dact = _mm([(dh2_b, wts["down"])], "down_dgrad", trans_b=True)
    started = emit_grad("w_down", _wgrad(act, dh2_b, "down_wgrad"))
    dup_a, dup_v, dwf_a, dwf_v, dbf_a, dbf_v = _conv_f_bwd(up_raw, up_a, up_v, conv_f_w, dact, name="conv_f_bwd")
    dn2 = _mm([((dup_a, 0), (up_w, 0)), ((dup_a, 1), (up_w, 1)), ((dup_v, 0), (up_w, 2)), ((dup_v, 1), (up_w, 3))],
              "up_dgrad", trans_b=True, after=started, out_dtype=BF16)
    g_up = _wgrad(n2, dup_a, "up_wgrad_a", tn=up_cols, stack_out=True, part_of=(N_CHIPS, 0, None))
    g_up = _wgrad(n2, dup_v, "up_wgrad_v", tn=up_cols, stack_out=True, part_of=(N_CHIPS, N_CHIPS // 2, g_up))
    started = emit_grad("w_up", g_up)
    dh1, dh1_b, d_norm2 = _rms_bwd(h1, small["norm2_w"], dn2, dh2, bf16_copy=True, name="rms2_bwd")
    dmix = _mm([(dh1_b, wts["out"])], "out_dgrad", trans_b=True, after=started, out_dtype=BF16)
    started = emit_grad("w_out", _wgrad(mix, dh1_b, "out_wgrad"))
    dp_a, dp_b, dg_a, dg_b, dbg_a, dbg_b = _gate_bwd(gates_raw, small["b_gate"], p_a, p_b, dmix, name="gate_bwd")
    dya = _mm([(dp_a, wts["branch_a"])], "branch_a_dgrad", trans_b=True, after=started)
    dyb = _mm([(dp_b, wts["branch_b"])], "branch_b_dgrad", trans_b=True, out_dtype=BF16)
    g_branch = _wgrad(y_a, dp_a, "branch_a_wgrad", part_of=(3, 0, None))
    g_branch = _wgrad(y_b, dp_b, "branch_b_wgrad", part_of=(3, 2, g_branch))
    started_branch = emit_grad("w_branch", g_branch)
    duv, d_wsp, d_bsp_t, d_lnw, d_lnb, d_uvb = _sgu_bwd(uv_raw, dyb, small["uv_b"], small["v_ln_w"],
                                                        small["v_ln_b"], w_sp, b_sp_t, group_sum, name="sgu_bwd")
    dz, dxbc, ddt, d_ssd_nw, d_dskip, d_alog, d_dtb = _ssd_bwd(
        dya, y, z, xbc, dt_raw, states, dt_bias, a_log, a_log_x, d_skip_x, small["ssd_norm_w"], expand, expand_t,
        name="ssd_bwd")
    dxbc_raw, d_conv_a_w, d_conv_a_b = _conv_a_bwd(xbc_raw, xbc_pre, conv_a_w, dxbc, name="conv_a_bwd")
    started = emit_grad("w_in", {
        "in_z": _wgrad(n1, dz, "in_z_wgrad", after=started_branch), "in_xbc": _wgrad(n1, dxbc_raw, "in_xbc_wgrad"),
        "in_dt": _wgrad(n1, ddt, "in_dt_wgrad")[:, :SSD_HEADS], "in_uv": _wgrad(n1, duv, "in_uv_wgrad"),
        "in_gate_a": _wgrad(n1, dg_a, "in_gate_a_wgrad"), "in_gate_b": _wgrad(n1, dg_b, "in_gate_b_wgrad")})
    dn1 = _mm([(dz, wts["in_z"]), (dxbc_raw, wts["in_xbc"]), (ddt, wts["in_dt"]), (duv, wts["in_uv"]),
               (dg_a, wts["in_gate_a"]), (dg_b, wts["in_gate_b"])], "in_dgrad", trans_b=True, after=started,
              out_dtype=BF16)
    dx, d_norm1 = _rms_bwd(x, small["norm1_w"], dn1, dh1, bf16_copy=False, name="rms1_bwd")

    grads_small = {
        "norm1_w": d_norm1, "b_gate": jnp.concatenate([dbg_a, dbg_b], axis=1),
        "conv_a_w": d_conv_a_w[:4], "conv_a_b": d_conv_a_b,
        "dt_bias": d_dtb[:, :SSD_HEADS], "a_log": d_alog[:, :SSD_HEADS], "d_skip": d_dskip[:, :SSD_HEADS],
        "ssd_norm_w": d_ssd_nw, "uv_b": d_uvb, "v_ln_w": d_lnw, "v_ln_b": d_lnb,
        "w_spatial": d_wsp[None], "b_spatial": d_bsp_t[:, :SGU_GROUPS].T[None],
        "norm2_w": d_norm2, "conv_f_w": jnp.concatenate([dwf_a[:3], dwf_v[:3]], axis=1),
        "conv_f_b": jnp.concatenate([dbf_a, dbf_v], axis=1), "final_norm_w": d_final.reshape(D_MODEL),
    }
    return loss, dx, grads_small


HBM = pl.BlockSpec(memory_space=pl.ANY)
MESH = pl.DeviceIdType.MESH


def _mesh_pos():
    return lax.axis_index("x"), lax.axis_index("y"), lax.axis_index("c")


def _other_chips(x, y):
    return [(1 - x, y), (x, 1 - y), (1 - x, 1 - y)]


def _remote(src, dst, send_sems, recv_sems, k, dev):
    return pltpu.make_async_remote_copy(src_ref=src, dst_ref=dst, send_sem=send_sems.at[k], recv_sem=recv_sems.at[k],
                                        device_id=dev, device_id_type=MESH)


def _dma_sems(n):
    return [pltpu.SemaphoreType.DMA((n,)), pltpu.SemaphoreType.DMA((n,))]


HBM_ONLY = pl.BlockSpec(memory_space=pltpu.HBM)
SEMAPHORES = pl.BlockSpec(memory_space=pltpu.SEMAPHORE)
DATAFLOW_EFFECT = pltpu.SideEffectType.DATAFLOW_SIDE_EFFECTING
N_PEER_CHIPS = N_CHIPS - 1


def _gather_sends(w_ref, land_ref, send_sems, recv_sems):
    x, y, c = _mesh_pos()
    return [_remote(w_ref.at[c], land_ref.at[2 * x + y, c], send_sems, recv_sems, k, (px, py, c))
            for k, (px, py) in enumerate(_other_chips(x, y))]


def _gather_arrivals(w_ref, land_ref, send_sems, recv_sems):
    x, y, c = _mesh_pos()
    return [_remote(w_ref.at[c], land_ref.at[2 * px + py, c], send_sems, recv_sems, k, (px, py, c))
            for k, (px, py) in enumerate(_other_chips(x, y))]


def _gather_whole_sends(w_ref, land_ref, send_sems, recv_sems):
    x, y, c = _mesh_pos()
    return [_remote(w_ref, land_ref.at[2 * x + y], send_sems, recv_sems, k, (px, py, c))
            for k, (px, py) in enumerate(_other_chips(x, y))]


def _gather_whole_arrivals(w_ref, land_ref, send_sems, recv_sems):
    x, y, c = _mesh_pos()
    return [_remote(w_ref, land_ref.at[2 * px + py], send_sems, recv_sems, k, (px, py, c))
            for k, (px, py) in enumerate(_other_chips(x, y))]


def _scatter_sends(h_ref, land_ref, send_sems, recv_sems):
    x, y, c = _mesh_pos()
    return [_remote(h_ref.at[2 * px + py], land_ref.at[2 * x + y], send_sems, recv_sems, k, (px, py, c))
            for k, (px, py) in enumerate(_other_chips(x, y))]


def _scatter_arrivals(h_ref, land_ref, send_sems, recv_sems):
    x, y, c = _mesh_pos()
    return [_remote(h_ref.at[2 * x + y], land_ref.at[2 * px + py], send_sems, recv_sems, k, (px, py, c))
            for k, (px, py) in enumerate(_other_chips(x, y))]


def _exchange_wait_many(pendings, after, sends, arrivals, *, name):
    n = len(pendings)

    def body(*refs):
        for i in range(n):
            src_ref, land_ref, send_ref, recv_ref = refs[i], refs[n + i], refs[2 * n + i], refs[3 * n + i]
            for cp in sends(src_ref, land_ref, send_ref, recv_ref):
                cp.wait_send()
            for cp in arrivals(src_ref, land_ref, send_ref, recv_ref):
                cp.wait_recv()

    sources = [p[2] for p in pendings]
    landings = [p[3] for p in pendings]
    outs = pl.pallas_call(
        body, name=name,
        out_shape=tuple(pltpu.HBM(a.shape, a.dtype) for a in sources + landings),
        in_specs=[HBM_ONLY] * (2 * n) + [SEMAPHORES] * (2 * n) + [pl.BlockSpec(memory_space=pl.ANY)],
        out_specs=tuple([HBM_ONLY] * (2 * n)), input_output_aliases={i: i for i in range(2 * n)},
        compiler_params=pltpu.CompilerParams(has_side_effects=DATAFLOW_EFFECT),
    )(*sources, *landings, *[p[0] for p in pendings], *[p[1] for p in pendings], after)
    return [(outs[i], outs[n + i]) for i in range(n)]


def _sibling_sends(src_ref, land_ref, send_sems, recv_sems):
    x, y, c = _mesh_pos()
    return [_remote(src_ref, land_ref, send_sems, recv_sems, 0, (x, y, 1 - c))]


def _exchange_start(sources, landing_shapes, sends, *, after=None, name):
    n = len(sources)
    extra = [] if after is None else [after]

    def body(*refs):
        sems = refs[2 * n + len(extra):4 * n + len(extra)]
        for i in range(n):
            send_i = sends[i] if isinstance(sends, (list, tuple)) else sends
            for cp in send_i(refs[i], refs[n + i], sems[2 * i], sems[2 * i + 1]):
                cp.start()
        refs[-1][...] = jnp.zeros_like(refs[-1])

    hbm = [pltpu.HBM(s.shape, s.dtype) for s in sources] + [pltpu.HBM(shp, s.dtype)
                                                             for shp, s in zip(landing_shapes, sources)]
    outs = pl.pallas_call(
        body, name=name,
        out_shape=tuple([pltpu.SemaphoreType.DMA((N_PEER_CHIPS,))] * (2 * n) + hbm
                        + [jax.ShapeDtypeStruct((8, LANES), F32)]),
        in_specs=[HBM_ONLY] * (2 * n) + [pl.BlockSpec(memory_space=pl.ANY)] * len(extra),
        out_specs=tuple([SEMAPHORES] * (2 * n) + [HBM_ONLY] * (2 * n) + [pl.BlockSpec(memory_space=pltpu.VMEM)]),
        input_output_aliases={i: 2 * n + i for i in range(2 * n)},
        compiler_params=pltpu.CompilerParams(has_side_effects=DATAFLOW_EFFECT),
    )(*[pltpu.with_memory_space_constraint(s, pltpu.HBM) for s in sources],
      *[pltpu.with_memory_space_constraint(lax.empty(shp, s.dtype), pltpu.HBM)
        for shp, s in zip(landing_shapes, sources)], *extra)
    pending = [(outs[2 * i], outs[2 * i + 1], outs[2 * n + i], outs[3 * n + i]) for i in range(n)]
    return pending, outs[-1]


def _exchange_wait(pending, after, sends, arrivals, *, name):
    send_sems, recv_sems, source, landing = pending

    def body(src_ref, land_ref, send_ref, recv_ref, after_ref, src_out, land_out):
        for cp in sends(src_ref, land_ref, send_ref, recv_ref):
            cp.wait_send()
        for cp in arrivals(src_ref, land_ref, send_ref, recv_ref):
            cp.wait_recv()

    return pl.pallas_call(
        body, name=name,
        out_shape=(pltpu.HBM(source.shape, source.dtype), pltpu.HBM(landing.shape, landing.dtype)),
        in_specs=[HBM_ONLY, HBM_ONLY, SEMAPHORES, SEMAPHORES, pl.BlockSpec(memory_space=pl.ANY)],
        out_specs=(HBM_ONLY, HBM_ONLY), input_output_aliases={0: 0, 1: 1},
        compiler_params=pltpu.CompilerParams(has_side_effects=DATAFLOW_EFFECT),
    )(source, landing, send_sems, recv_sems, after)


def _gather_ici(shard, *, name):
    _, rh, cols = shard.shape

    def body(w_ref, o_ref, send_sems, recv_sems):
        x, y, c = _mesh_pos()
        mine = 2 * x + y
        sends = []
        for k, (px, py) in enumerate(_other_chips(x, y)):
            cp = _remote(w_ref.at[c], o_ref.at[mine, c], send_sems, recv_sems, k, (px, py, c))
            cp.start()
            sends.append(cp)
        for k, (px, py) in enumerate(_other_chips(x, y)):
            _remote(w_ref.at[c], o_ref.at[2 * px + py, c], send_sems, recv_sems, k, (px, py, c)).wait_recv()
        for cp in sends:
            cp.wait_send()

    return pl.pallas_call(
        body, name=name, in_specs=[HBM], out_specs=HBM,
        out_shape=jax.ShapeDtypeStruct((N_CHIPS, 2, rh, cols), shard.dtype), scratch_shapes=_dma_sems(3),
    )(shard)


def _gather_d2d(parts, *, name):
    def body(a_ref, o_ref, send_sems, recv_sems):
        x, y, c = _mesh_pos()
        sibling = (x, y, 1 - c)
        sends = []
        for k, (px, py) in enumerate(_other_chips(x, y)):
            cp = _remote(a_ref.at[2 * px + py, c], o_ref.at[2 * px + py, c], send_sems, recv_sems, k, sibling)
            cp.start()
            sends.append(cp)
        for k, (px, py) in enumerate(_other_chips(x, y)):
            _remote(a_ref.at[2 * px + py, c], o_ref.at[2 * px + py, 1 - c], send_sems, recv_sems, k, sibling).wait_recv()
        for cp in sends:
            cp.wait_send()

    return pl.pallas_call(
        body, name=name, in_specs=[HBM], out_specs=HBM,
        out_shape=jax.ShapeDtypeStruct(parts.shape, parts.dtype),
        input_output_aliases={0: 0}, scratch_shapes=_dma_sems(3),
    )(parts)


def _all_gather_chips(shard_flat, name):
    rows, cols = shard_flat.shape
    parts = _gather_ici(shard_flat.reshape(2, rows // 2, cols), name=name + "_ici")
    others = _gather_d2d(parts, name=name + "_d2d").reshape(N_CHIPS, rows, cols)
    chip = 2 * lax.axis_index("x") + lax.axis_index("y")
    return lax.dynamic_update_slice(others, shard_flat[None], (chip, 0, 0))


def _row_tile(rows, mult, cap):
    best = mult
    for t in range(mult, min(rows, cap) + 1, mult):
        if rows % t == 0:
            best = t
    assert rows % best == 0, (rows, mult)
    return best


def _swap_halves_d2d(g, *, after=None, name):
    _, _, rh, cols = g.shape
    extra = [] if after is None else [after]

    def body(g_ref, *rest):
        o_ref, send_sems, recv_sems = rest[len(extra):]
        x, y, c = _mesh_pos()
        sibling = (x, y, 1 - c)
        sends = []
        for s in range(N_CHIPS):
            cp = _remote(g_ref.at[s, 1 - c], o_ref.at[s], send_sems, recv_sems, s, sibling)
            cp.start()
            sends.append(cp)
        for s in range(N_CHIPS):
            _remote(g_ref.at[s, c], o_ref.at[s], send_sems, recv_sems, s, sibling).wait_recv()
        for cp in sends:
            cp.wait_send()

    return pl.pallas_call(
        body, name=name, in_specs=[HBM] * (1 + len(extra)), out_specs=HBM,
        out_shape=jax.ShapeDtypeStruct((N_CHIPS, rh, cols), g.dtype), scratch_shapes=_dma_sems(N_CHIPS),
    )(g, *extra)


def _add_own_half(g, arrived, core, *, name):
    _, _, rh, cols = g.shape
    mult = 16 if g.dtype == BF16 else 8
    tr = _row_tile(rh, mult, max(mult, (512 * 1024) // cols))

    def body(core_ref, g_ref, a_ref, o_ref):
        o_ref[...] = (g_ref[0].astype(F32) + a_ref[...].astype(F32)).astype(o_ref.dtype)

    grid_spec = pltpu.PrefetchScalarGridSpec(
        num_scalar_prefetch=1, grid=(N_CHIPS, rh // tr),
        in_specs=[pl.BlockSpec((1, 1, tr, cols), lambda s, i, core_ref: (s, core_ref[0], i, 0)),
                  pl.BlockSpec((1, tr, cols), lambda s, i, core_ref: (s, i, 0))],
        out_specs=pl.BlockSpec((1, tr, cols), lambda s, i, core_ref: (s, i, 0)))
    return pl.pallas_call(
        body, name=name, grid_spec=grid_spec, out_shape=jax.ShapeDtypeStruct((N_CHIPS, rh, cols), g.dtype),
        compiler_params=_params(("parallel", "parallel")),
    )(core, g, arrived)


def _scatter_ici(h, *, after=None, name):
    extra = [] if after is None else [after]

    def body(h_ref, *rest):
        o_ref, send_sems, recv_sems = rest[len(extra):]
        x, y, c = _mesh_pos()
        mine = 2 * x + y
        sends = []
        for k, (px, py) in enumerate(_other_chips(x, y)):
            cp = _remote(h_ref.at[2 * px + py], o_ref.at[mine], send_sems, recv_sems, k, (px, py, c))
            cp.start()
            sends.append(cp)
        for k, (px, py) in enumerate(_other_chips(x, y)):
            _remote(h_ref.at[mine], o_ref.at[2 * px + py], send_sems, recv_sems, k, (px, py, c)).wait_recv()
        for cp in sends:
            cp.wait_send()

    others = pl.pallas_call(
        body, name=name, in_specs=[HBM] * (1 + len(extra)), out_specs=HBM,
        out_shape=jax.ShapeDtypeStruct(h.shape, h.dtype), scratch_shapes=_dma_sems(3),
    )(h, *extra)
    chip = 2 * lax.axis_index("x") + lax.axis_index("y")
    own = lax.dynamic_slice_in_dim(h, chip, 1, axis=0)
    return lax.dynamic_update_slice(others, own, (chip, 0, 0))


def _sum_chips(parts, *, name):
    _, rh, cols = parts.shape
    mult = 16 if parts.dtype == BF16 else 8
    tr = _row_tile(rh, mult, max(mult, (512 * 1024) // cols))

    def body(p_ref, o_ref):
        acc = p_ref[0].astype(F32)
        for s in range(1, N_CHIPS):
            acc = acc + p_ref[s].astype(F32)
        o_ref[...] = acc

    return pl.pallas_call(
        body, name=name, grid=(rh // tr,),
        in_specs=[pl.BlockSpec((N_CHIPS, tr, cols), lambda i: (0, i, 0))],
        out_specs=pl.BlockSpec((tr, cols), lambda i: (i, 0)),
        out_shape=jax.ShapeDtypeStruct((rh, cols), F32), compiler_params=_params(("parallel",)),
    )(parts)


def _sum_chips_with_own(landed, sent, chip, *, name):
    _, rh, cols = landed.shape
    mult = 16 if landed.dtype == BF16 else 8
    tr = _row_tile(rh, mult, max(mult, (512 * 1024) // cols))

    def body(chip_ref, own_ref, px_ref, py_ref, pxy_ref, o_ref):
        acc = own_ref[0].astype(F32)
        for p_ref in (px_ref, py_ref, pxy_ref):
            acc = acc + p_ref[0].astype(F32)
        o_ref[...] = acc

    def block_of(flip):
        return pl.BlockSpec((1, tr, cols), lambda i, chip_ref: (chip_ref[0] ^ flip, i, 0))

    grid_spec = pltpu.PrefetchScalarGridSpec(
        num_scalar_prefetch=1, grid=(rh // tr,),
        in_specs=[block_of(0), block_of(2), block_of(1), block_of(3)],
        out_specs=pl.BlockSpec((tr, cols), lambda i, chip_ref: (i, 0)))
    return pl.pallas_call(
        body, name=name, grid_spec=grid_spec, out_shape=jax.ShapeDtypeStruct((rh, cols), F32),
        compiler_params=_params(("parallel",)),
    )(chip, sent, landed, landed, landed)


def _share_d2d(f, *, name):
    fs = f if isinstance(f, (list, tuple)) else [f]
    others = _swap_with_sibling(fs, name=name)
    first = lax.axis_index("c") == 0
    both = [jnp.stack([jnp.where(first, a, b), jnp.where(first, b, a)]) for a, b in zip(fs, others)]
    return both if isinstance(f, (list, tuple)) else both[0]


def _swap_with_sibling(fs, *, name):
    n = len(fs)

    def body(*refs):
        x, y, c = _mesh_pos()
        sibling = (x, y, 1 - c)
        send_sems, recv_sems = refs[2 * n:]
        copies = [_remote(refs[i], refs[n + i], send_sems, recv_sems, i, sibling) for i in range(n)]
        for cp in copies:
            cp.start()
        for cp in copies:
            cp.wait()

    return pl.pallas_call(
        body, name=name, in_specs=[HBM] * n, out_specs=[HBM] * n,
        out_shape=[jax.ShapeDtypeStruct(a.shape, a.dtype) for a in fs], scratch_shapes=_dma_sems(n),
    )(*fs)


def _reduce_scatter_chips(g, core, name, after=None, after_swap=None):
    _, rows, cols = g.shape
    g = g.reshape(N_CHIPS, 2, rows // 2, cols)
    arrived = _swap_halves_d2d(g, after=after, name=name + "_swap")
    started = after_swap(arrived) if after_swap is not None else None
    chip_sum = _add_own_half(g, arrived, core, name=name + "_add2")
    parts = _scatter_ici(chip_sum, after=started, name=name + "_ici")
    total = _sum_chips(parts, name=name + "_sum4")
    return _share_d2d(total, name=name + "_share").reshape(rows, cols)


BIG = ("w_in", "w_branch", "w_out", "w_up", "w_down")
BIG_COLUMN_SHARDED = ("w_in", "w_up")
CONV = ("conv_a_w", "conv_f_w")
REPLICATED = ("norm1_w", "b_gate", "conv_a_b", "dt_bias", "a_log", "d_skip", "ssd_norm_w", "uv_b", "v_ln_w",
              "v_ln_b", "w_spatial", "b_spatial", "norm2_w", "conv_f_b", "final_norm_w")
WEIGHT_ORDER = ("norm1_w", "w_in", "b_gate", "conv_a_w", "conv_a_b", "dt_bias", "a_log", "d_skip", "ssd_norm_w",
                "uv_b", "v_ln_w", "v_ln_b", "w_spatial", "b_spatial", "w_branch", "w_out", "norm2_w", "w_up",
                "conv_f_w", "conv_f_b", "w_down", "final_norm_w")
SMALL_EXCHANGE_ROWS = 64


_GATE0 = SSD_IN + 2 * SGU_WIDTH
IN_SEGMENTS = {
    "in_z": (0, SSD_D_INNER), "in_xbc": (SSD_D_INNER, SSD_D_INNER + SSD_XBC), "in_dt": (SSD_D_INNER + SSD_XBC, SSD_IN),
    "in_uv": (SSD_IN, _GATE0), "in_gate": (_GATE0, IN_COLS), "in_gate_a": (_GATE0, _GATE0 + D_MODEL),
    "in_gate_b": (_GATE0 + D_MODEL, IN_COLS),
}
IN_GRAD_SEGMENTS = ("in_z", "in_xbc", "in_dt", "in_uv", "in_gate_a", "in_gate_b")


def _take_columns(parts, start, stop):
    out = []
    for a, first in parts:
        lo, hi = max(start, first), min(stop, first + a.shape[1])
        if lo < hi:
            out.append(a[:, lo - first:hi - first])
    return out[0] if len(out) == 1 else jnp.concatenate(out, axis=1)


def _flat_rows(arrays, row_multiple):
    flat = jnp.concatenate([a.reshape(-1) for a in arrays])
    rows = -(-flat.shape[0] // (LANES * row_multiple)) * row_multiple
    return jnp.pad(flat, (0, rows * LANES - flat.shape[0])).reshape(rows, LANES)


def _unflatten(flat, shapes):
    flat = flat.reshape(-1)
    out, off = [], 0
    for shp in shapes:
        n = math.prod(shp)
        out.append(flat[off:off + n].reshape(shp))
        off += n
    return out


def _from_chip_blocks(blocks, name):
    if name in BIG_COLUMN_SHARDED or name in CONV:
        k = blocks.shape[1]
        return jnp.transpose(blocks, (1, 0, 2)).reshape(k, -1)
    return blocks.reshape(-1, blocks.shape[-1])


def _to_chip_blocks(whole, name):
    if name in BIG_COLUMN_SHARDED or name in CONV:
        k, n = whole.shape
        return jnp.transpose(whole.reshape(k, N_CHIPS, n // N_CHIPS), (1, 0, 2))
    return whole.reshape(N_CHIPS, whole.shape[0] // N_CHIPS, whole.shape[1])


def kernel(x, norm1_w, w_in, b_gate, conv_a_w, conv_a_b, dt_bias, a_log, d_skip, ssd_norm_w, uv_b, v_ln_w, v_ln_b, w_spatial, b_spatial, w_branch, w_out, norm2_w, w_up, conv_f_w, conv_f_b, w_down, final_norm_w, loss_target, m_norm1_w, m_w_in, m_b_gate, m_conv_a_w, m_conv_a_b, m_dt_bias, m_a_log, m_d_skip, m_ssd_norm_w, m_uv_b, m_v_ln_w, m_v_ln_b, m_w_spatial, m_b_spatial, m_w_branch, m_w_out, m_norm2_w, m_w_up, m_conv_f_w, m_conv_f_b, m_w_down, m_final_norm_w, v_norm1_w, v_w_in, v_b_gate, v_conv_a_w, v_conv_a_b, v_dt_bias, v_a_log, v_d_skip, v_ssd_norm_w, v_uv_b, v_v_ln_w, v_v_ln_b, v_w_spatial, v_b_spatial, v_w_branch, v_w_out, v_norm2_w, v_w_up, v_conv_f_w, v_conv_f_b, v_w_down, v_final_norm_w):
    weights = dict(norm1_w=norm1_w, w_in=w_in, b_gate=b_gate, conv_a_w=conv_a_w, conv_a_b=conv_a_b, dt_bias=dt_bias,
                   a_log=a_log, d_skip=d_skip, ssd_norm_w=ssd_norm_w, uv_b=uv_b, v_ln_w=v_ln_w, v_ln_b=v_ln_b,
                   w_spatial=w_spatial, b_spatial=b_spatial, w_branch=w_branch, w_out=w_out, norm2_w=norm2_w,
                   w_up=w_up, conv_f_w=conv_f_w, conv_f_b=conv_f_b, w_down=w_down, final_norm_w=final_norm_w)
    mom1 = dict(norm1_w=m_norm1_w, w_in=m_w_in, b_gate=m_b_gate, conv_a_w=m_conv_a_w, conv_a_b=m_conv_a_b,
                dt_bias=m_dt_bias, a_log=m_a_log, d_skip=m_d_skip, ssd_norm_w=m_ssd_norm_w, uv_b=m_uv_b,
                v_ln_w=m_v_ln_w, v_ln_b=m_v_ln_b, w_spatial=m_w_spatial, b_spatial=m_b_spatial, w_branch=m_w_branch,
                w_out=m_w_out, norm2_w=m_norm2_w, w_up=m_w_up, conv_f_w=m_conv_f_w, conv_f_b=m_conv_f_b,
                w_down=m_w_down, final_norm_w=m_final_norm_w)
    mom2 = dict(norm1_w=v_norm1_w, w_in=v_w_in, b_gate=v_b_gate, conv_a_w=v_conv_a_w, conv_a_b=v_conv_a_b,
                dt_bias=v_dt_bias, a_log=v_a_log, d_skip=v_d_skip, ssd_norm_w=v_ssd_norm_w, uv_b=v_uv_b,
                v_ln_w=v_v_ln_w, v_ln_b=v_v_ln_b, w_spatial=v_w_spatial, b_spatial=v_b_spatial, w_branch=v_w_branch,
                w_out=v_w_out, norm2_w=v_norm2_w, w_up=v_w_up, conv_f_w=v_conv_f_w, conv_f_b=v_conv_f_b,
                w_down=v_w_down, final_norm_w=v_final_norm_w)
    chip = 2 * lax.axis_index("x") + lax.axis_index("y")
    core = lax.axis_index("c").astype(jnp.int32).reshape(1)

    whole = {}
    conv_shapes = [weights[n].shape[1:] for n in CONV]
    conv_gathered = _all_gather_chips(_flat_rows([weights[n] for n in CONV], 16), "gather_conv").reshape(N_CHIPS, -1)
    off = 0
    for n, shp in zip(CONV, conv_shapes):
        size = math.prod(shp)
        whole[n] = _from_chip_blocks(conv_gathered[:, off:off + size].reshape((N_CHIPS,) + shp), n)
        off += size
    shard_shapes = {n: weights[n].shape[1:] for n in BIG}
    halves = [weights[n][0].astype(BF16).reshape(2, shard_shapes[n][0] // 2, shard_shapes[n][1]) for n in BIG]
    sends = [_gather_sends if n == "w_in" else _gather_whole_sends for n in BIG]
    gathers, gathers_started = _exchange_start(halves, [(N_CHIPS,) + h.shape for h in halves], sends,
                                               after=conv_gathered, name="gather_start")
    gathers = dict(zip(BIG, gathers))

    def get_weight(name, after):
        rows, cols = shard_shapes[name]
        if name == "w_in":
            own, landed = _exchange_wait(gathers[name], after, _gather_sends, _gather_arrivals,
                                         name="gather_" + name + "_wait")
            landed = _gather_d2d(landed, name="gather_" + name + "_d2d")
        else:
            own, landed = _exchange_wait(gathers[name], after, _gather_whole_sends, _gather_whole_arrivals,
                                         name="gather_" + name + "_wait")
        blocks = lax.dynamic_update_slice(landed.reshape(N_CHIPS, rows, cols), own.reshape(1, rows, cols),
                                          (chip, 0, 0))
        if name == "w_up":
            return {"up": blocks}
        if name == "w_in":
            parts = [(blocks[k], cols * k) for k in range(N_CHIPS)]
            segs = {n: _take_columns(parts, a, b) for n, (a, b) in IN_SEGMENTS.items()}
            segs["in_dt"] = jnp.pad(segs["in_dt"], ((0, 0), (0, LANES - SSD_HEADS)))
            return segs
        full = _from_chip_blocks(blocks, name)
        if name == "w_branch":
            return {"branch_a": full[:SSD_D_INNER], "branch_b": full[SSD_D_INNER:]}
        return {name[2:]: full}

    small = {n: weights[n] for n in REPLICATED}
    small["conv_a_w"] = whole["conv_a_w"]
    small["conv_f_w"] = whole["conv_f_w"]
    small["gathers_started"] = gathers_started

    reductions = {}

    def emit_grad(name, g):
        if name == "w_in":
            parts = [(g[n], IN_SEGMENTS[n][0]) for n in IN_GRAD_SEGMENTS]
            cols = shard_shapes[name][1]
            g_blocks = jnp.stack([_take_columns(parts, cols * k, cols * (k + 1)) for k in range(N_CHIPS)])
        else:
            g_blocks = g if name == "w_up" else _to_chip_blocks(g, name)
        if name == "w_in":
            _, rows, cols = g_blocks.shape
            g_halves = g_blocks.reshape(N_CHIPS, 2, rows // 2, cols)
            arrived = _swap_halves_d2d(g_halves, name="reduce_" + name + "_swap")
            g_blocks = _add_own_half(g_halves, arrived, core, name="reduce_" + name + "_add2")
        (pending,), started = _exchange_start([g_blocks], [g_blocks.shape], _scatter_sends,
                                              name="reduce_" + name + "_start")
        reductions[name] = pending
        return started

    loss, dx, grads_small = _local_step(x[0], loss_target[0], get_weight, small, emit_grad)

    order = ("w_down", "w_up", "w_out", "w_branch", "w_in")
    core_sums = []
    chip_index = chip.astype(jnp.int32).reshape(1)
    for n in order:
        sent, landed = _exchange_wait(reductions[n], dx, _scatter_sends, _scatter_arrivals,
                                      name="reduce_" + n + "_wait")
        core_sums.append(_sum_chips_with_own(landed, sent, chip_index, name="reduce_" + n + "_sum4"))
    grads = {}
    swaps = []

    def start_sum_swap(small_swapped):
        pending, started = _exchange_start(core_sums, [a.shape for a in core_sums], _sibling_sends,
                                           after=small_swapped, name="reduce_swap_start")
        swaps.extend(pending)
        return started

    small_names = REPLICATED + CONV + ("loss",)
    grads_small = dict(grads_small, loss=loss)
    small_shapes = [grads_small[n].shape for n in small_names]
    g_small = _flat_rows([grads_small[n] for n in small_names], N_CHIPS * 2 * SMALL_EXCHANGE_ROWS)
    red_small = _reduce_scatter_chips(g_small.reshape(N_CHIPS, -1, LANES), core, "reduce_small", after=core_sums[-1],
                                      after_swap=start_sum_swap)
    all_small = _all_gather_chips(red_small, "gather_small")
    swapped = _exchange_wait_many(swaps, all_small, _sibling_sends, _sibling_sends, name="reduce_swap_wait")
    core_sums = {n: own for n, (own, _) in zip(order, swapped)}
    sibling_sums = {n: other for n, (_, other) in zip(order, swapped)}
    first = lax.axis_index("c") == 0
    w_in_halves = (core_sums["w_in"], sibling_sums["w_in"])
    w_in_grad = jnp.concatenate([jnp.where(first, w_in_halves[0], w_in_halves[1]),
                                 jnp.where(first, w_in_halves[1], w_in_halves[0])], axis=0)
    for n, g in zip(small_names, _unflatten(all_small, small_shapes)):
        if n == "loss":
            total_loss = g[0, 0]
            continue
        if n in CONV:
            width = g.shape[1] // N_CHIPS
            g = lax.dynamic_slice_in_dim(g, chip * width, width, axis=1)
        grads[n] = g.reshape(weights[n].shape[1:]) if n != "final_norm_w" else g

    delta, new_m, new_v = {}, {}, {}
    for n in BIG:
        shp = weights[n].shape
        if n == "w_in":
            g_t = w_in_grad.T
            results = [g_t] + list(_adamw(weights[n][0].T, g_t, mom1[n][0].T, mom2[n][0].T, name="adamw_" + n,
                                          tr=_row_tile(g_t.shape[0], 8, 136)))
            results = [a.T for a in results]
        else:
            results = _adamw_two_sums(weights[n][0], core_sums[n], sibling_sums[n], mom1[n][0], mom2[n][0],
                                      name="adamw_" + n, tr=_row_tile(shp[1], 8, 352))
        grads[n], delta[n], new_m[n], new_v[n] = [a.reshape(shp) for a in results]
    small_all = [n for n in WEIGHT_ORDER if n not in BIG]

    def as_2d(a):
        return a.reshape(-1, a.shape[-1])

    results = _adamw_many(*[[as_2d(src[n]) for n in small_all] for src in (weights, grads, mom1, mom2)],
                          name="adamw_small")
    for n, dv, mv, vv in zip(small_all, *results):
        shp = weights[n].shape
        delta[n], new_m[n], new_v[n] = dv.reshape(shp), mv.reshape(shp), vv.reshape(shp)

    grad_out = [grads[n].reshape(weights[n].shape) for n in WEIGHT_ORDER]
    return (total_loss, dx[None], *grad_out, *[delta[n] for n in WEIGHT_ORDER], *[new_m[n] for n in WEIGHT_ORDER],
            *[new_v[n] for n in WEIGHT_ORDER])
```

```python
import functools
import math

import jax
import jax.numpy as jnp
from jax import lax
from jax.experimental import pallas as pl
from jax.experimental.pallas import tpu as pltpu

F32 = jnp.float32
BF16 = jnp.bfloat16

D_MODEL = 1024
SSD_D_INNER = 2048
SSD_HEADS = 32
SSD_HEAD_DIM = 64
SSD_GROUPS = 4
SSD_HEADS_PER_GROUP = 8
SSD_STATE = 128
SSD_BC = 512
SSD_XBC = 3072
SSD_IN = 5152
SGU_WIDTH = 1024
SGU_GROUPS = 8
CHUNK = 128
IN_COLS = 9248
D_FF = 2816
NORM_EPS = 1e-6
LN_EPS = 1e-5
GROUP_COLS = SSD_HEADS_PER_GROUP * SSD_HEAD_DIM
LANES = 128

ADAM_LR = 0.001
ADAM_B1 = 0.9
ADAM_B2 = 0.999
ADAM_EPS = 1e-08
ADAM_WD = 0.01
ADAM_STEP = 10

N_CHIPS = 4
VMEM_LIMIT = 56 * 1024 * 1024

NT = (((1,), (1,)), ((), ()))
TN = (((0,), (0,)), ((), ()))
NN = (((1,), (0,)), ((), ()))


def _params(dims):
    return pltpu.CompilerParams(dimension_semantics=dims, vmem_limit_bytes=VMEM_LIMIT)


def _dot(a, b, dn=NN, precision=None):
    return lax.dot_general(a, b, dn, precision=precision, preferred_element_type=F32)


def _split3(x):
    hi = x.astype(BF16)
    rest = x - hi.astype(F32)
    mid = rest.astype(BF16)
    return hi, mid, (rest - mid.astype(F32)).astype(BF16)


def _dot_terms(terms, exact, dn=NN):
    out = None
    for t in terms:
        p = _dot(t, exact, dn)
        out = p if out is None else out + p
    return out


def _dot_exact_lhs(exact, terms):
    out = None
    for t in terms:
        p = _dot(exact, t)
        out = p if out is None else out + p
    return out


def _sigmoid(x):
    return 1.0 / (1.0 + jnp.exp(-x))


def _softplus(x):
    return jnp.maximum(x, 0.0) + jnp.log(1.0 + jnp.exp(-jnp.abs(x)))


def _matmul(pairs, *, trans_b=False, add=None, after=None, out_dtype=F32, tm=512, tn=512, name):
    def mat_shape(b):
        if isinstance(b, tuple) and b[1] == "cols":
            return (b[0].shape[1], b[0].shape[0] * b[0].shape[2])
        return b[0].shape[1:] if isinstance(b, tuple) else b.shape

    if isinstance(pairs[0][1], tuple) and pairs[0][1][1] == "cols":
        assert not trans_b and tn % LANES == 0 and pairs[0][1][0].shape[2] % tn == 0, name

    m = (pairs[0][0][0] if isinstance(pairs[0][0], tuple) else pairs[0][0]).shape[0]
    n = mat_shape(pairs[0][1])[0] if trans_b else mat_shape(pairs[0][1])[1]
    tm, tn = min(tm, m), min(tn, n)
    assert m % tm == 0 and n % tn == 0, (name, m, n, tm, tn)
    npairs = len(pairs)
    dn = NT if trans_b else NN

    def body(*refs):
        o_ref = refs[-1]
        acc = None
        for i in range(npairs):
            p = _dot(refs[2 * i][...].astype(BF16), refs[2 * i + 1][...].astype(BF16), dn)
            acc = p if acc is None else acc + p
        if add is not None:
            acc = acc + refs[2 * npairs][...]
        o_ref[...] = acc.astype(out_dtype)

    in_specs, args = [], []
    for a, b in pairs:
        bshape = mat_shape(b)
        k = bshape[1] if trans_b else bshape[0]
        assert bshape == ((n, k) if trans_b else (k, n)), (name, bshape)
        a, qa = a if isinstance(a, tuple) else (a, 0)
        assert a.shape[0] == m and a.shape[1] % k == 0, (name, a.shape, k)
        in_specs.append(pl.BlockSpec((tm, k), lambda i, j, qa=qa: (i, qa)))
        if isinstance(b, tuple) and b[1] == "cols":
            b = b[0]
            per = b.shape[2] // tn
            in_specs.append(pl.BlockSpec((None, k, tn), lambda i, j, per=per: (j // per, 0, j % per)))
        elif isinstance(b, tuple):
            b, qb = b
            if trans_b:
                in_specs.append(pl.BlockSpec((None, tn, k), lambda i, j, qb=qb: (qb, j, 0)))
            else:
                in_specs.append(pl.BlockSpec((None, k, tn), lambda i, j, qb=qb: (qb, 0, j)))
        elif trans_b:
            in_specs.append(pl.BlockSpec((tn, k), lambda i, j: (j, 0)))
        else:
            in_specs.append(pl.BlockSpec((k, tn), lambda i, j: (0, j)))
        args += [a, b]
    if add is not None:
        in_specs.append(pl.BlockSpec((tm, tn), lambda i, j: (i, j)))
        args.append(add)
    if after is not None:
        in_specs.append(pl.BlockSpec(memory_space=pl.ANY))
        args.append(after)
    return pl.pallas_call(
        body, name=name, grid=(m // tm, n // tn), in_specs=in_specs,
        out_specs=pl.BlockSpec((tm, tn), lambda i, j: (i, j)),
        out_shape=jax.ShapeDtypeStruct((m, n), out_dtype),
        compiler_params=_params(("parallel", "parallel")),
    )(*args)


def _matmul_tn(a, b, *, tk, tn, tm=1024, out_dtype=BF16, stack_out=False, after=None, part_of=None, name):
    m, k = a.shape
    n = b.shape[1]
    tm, tk, tn = min(tm, m), min(tk, k), min(tn, n)
    assert m % tm == 0 and k % tk == 0 and n % tn == 0, (name, m, k, n)
    nm = m // tm
    blocks, first, buffer = part_of if part_of is not None else (None, 0, None)
    if stack_out:
        out_spec = pl.BlockSpec((None, tk, tn), lambda i, j, l: (j + first, i, 0))
        out_shape = jax.ShapeDtypeStruct((blocks or n // tn, k, tn), out_dtype)
    else:
        out_spec = pl.BlockSpec((tk, tn), lambda i, j, l: (i + first, j))
        out_shape = jax.ShapeDtypeStruct((blocks * tk if blocks else k, n), out_dtype)

    def body(a_ref, b_ref, *rest):
        o_ref, acc = rest[-2:]
        mi = pl.program_id(2)

        @pl.when(mi == 0)
        def _():
            acc[...] = jnp.zeros_like(acc)

        acc[...] += _dot(a_ref[...].astype(BF16), b_ref[...].astype(BF16), TN)

        @pl.when(mi == nm - 1)
        def _():
            o_ref[...] = acc[...].astype(out_dtype)

    in_specs = [pl.BlockSpec((tm, tk), lambda i, j, l: (l, i)), pl.BlockSpec((tm, tn), lambda i, j, l: (l, j))]
    args = [a, b]
    if after is not None:
        in_specs.append(pl.BlockSpec(memory_space=pl.ANY))
        args.append(after)
    aliases = {}
    if buffer is not None:
        aliases = {len(args): 0}
        in_specs.append(pl.BlockSpec(memory_space=pl.ANY))
        args.append(buffer)
    return pl.pallas_call(
        body, name=name, grid=(k // tk, n // tn, nm), in_specs=in_specs,
        out_specs=out_spec, out_shape=out_shape, input_output_aliases=aliases,
        scratch_shapes=[pltpu.VMEM((tk, tn), F32)],
        compiler_params=_params(("parallel", "parallel", "arbitrary")),
    )(*args)


def _rms_fwd(x, w, *, after=None, name, tm=512):
    s, d = x.shape
    tm = min(tm, s)
    extra = [] if after is None else [after]

    def body(x_ref, w_ref, *rest):
        o_ref = rest[-1]
        xv = x_ref[...]
        r = lax.rsqrt(jnp.mean(xv * xv, axis=-1, keepdims=True) + NORM_EPS)
        o_ref[...] = (xv * r * w_ref[...]).astype(BF16)

    return pl.pallas_call(
        body, name=name, grid=(s // tm,),
        in_specs=[pl.BlockSpec((tm, d), lambda i: (i, 0)), pl.BlockSpec((1, d), lambda i: (0, 0))]
        + [pl.BlockSpec(memory_space=pl.ANY)] * len(extra),
        out_specs=pl.BlockSpec((tm, d), lambda i: (i, 0)),
        out_shape=jax.ShapeDtypeStruct((s, d), BF16),
        compiler_params=_params(("parallel",)),
    )(x, w, *extra)


def _rms_bwd(x, w, dn, dres, *, bf16_copy, name, tm=512):
    s, d = x.shape
    tm = min(tm, s)

    def body(x_ref, w_ref, dn_ref, dres_ref, dx_ref, *rest):
        dw_ref = rest[-1]

        @pl.when(pl.program_id(0) == 0)
        def _():
            dw_ref[...] = jnp.zeros_like(dw_ref)

        xv = x_ref[...]
        r = lax.rsqrt(jnp.mean(xv * xv, axis=-1, keepdims=True) + NORM_EPS)
        xhat = xv * r
        dnv = dn_ref[...].astype(F32)
        dxhat = dnv * w_ref[...]
        dx = dres_ref[...] + r * (dxhat - xhat * jnp.mean(dxhat * xhat, axis=-1, keepdims=True))
        dx_ref[...] = dx
        if bf16_copy:
            rest[0][...] = dx.astype(BF16)
        dw_ref[...] += jnp.sum(dnv * xhat, axis=0, keepdims=True)

    tile = pl.BlockSpec((tm, d), lambda i: (i, 0))
    row = pl.BlockSpec((1, d), lambda i: (0, 0))
    copies = [jax.ShapeDtypeStruct((s, d), BF16)] if bf16_copy else []
    return pl.pallas_call(
        body, name=name, grid=(s // tm,),
        in_specs=[tile, row, tile, tile], out_specs=[tile] + [tile] * len(copies) + [row],
        out_shape=[jax.ShapeDtypeStruct((s, d), F32)] + copies + [jax.ShapeDtypeStruct((1, d), F32)],
        compiler_params=_params(("arbitrary",)),
    )(x, w, dn, dres)


def _final_fwd_bwd(h2, wf, target, *, name, tm=512):
    s, d = h2.shape
    tm = min(tm, s)

    def body(h_ref, w_ref, t_ref, loss_ref, dh_ref, dhb_ref, dw_ref):
        @pl.when(pl.program_id(0) == 0)
        def _():
            dw_ref[...] = jnp.zeros_like(dw_ref)
            loss_ref[...] = jnp.zeros_like(loss_ref)

        hv = h_ref[...]
        r = lax.rsqrt(jnp.mean(hv * hv, axis=-1, keepdims=True) + NORM_EPS)
        xhat = hv * r
        err = xhat * w_ref[...] - t_ref[...]
        per_tok = jnp.mean(err * err, axis=-1, keepdims=True)
        loss_ref[...] += 0.5 * jnp.sum(per_tok, axis=0, keepdims=True)
        dy = err * (1.0 / d)
        dxhat = dy * w_ref[...]
        dh = r * (dxhat - xhat * jnp.mean(dxhat * xhat, axis=-1, keepdims=True))
        dh_ref[...] = dh
        dhb_ref[...] = dh.astype(BF16)
        dw_ref[...] += jnp.sum(dy * xhat, axis=0, keepdims=True)

    tile = pl.BlockSpec((tm, d), lambda i: (i, 0))
    row = pl.BlockSpec((1, d), lambda i: (0, 0))
    return pl.pallas_call(
        body, name=name, grid=(s // tm,),
        in_specs=[tile, row, tile],
        out_specs=[pl.BlockSpec((1, 1), lambda i: (0, 0)), tile, tile, row],
        out_shape=[jax.ShapeDtypeStruct((1, 1), F32), jax.ShapeDtypeStruct((s, d), F32),
                   jax.ShapeDtypeStruct((s, d), BF16), jax.ShapeDtypeStruct((1, d), F32)],
        compiler_params=_params(("arbitrary",)),
    )(h2, wf, target)


CONV_ROWS = 256
CONV_ROWS_FWD = 512
HALO = 8


def _rows_with_halo(ref, r0, rows, s, before, after):
    tile = 16 if ref.dtype == BF16 else HALO
    parts = []
    if before:
        prev = ref[pl.ds(pl.multiple_of(jnp.maximum(r0 - tile, 0), tile), tile), :].astype(F32)[tile - HALO:]
        parts.append(jnp.where(r0 > 0, prev, 0.0))
    parts.append(ref[pl.ds(r0, rows), :].astype(F32))
    if after:
        nxt = ref[pl.ds(pl.multiple_of(jnp.minimum(r0 + rows, s - tile), tile), tile), :].astype(F32)[:HALO]
        parts.append(jnp.where(r0 + rows < s, nxt, 0.0))
    return jnp.concatenate(parts, axis=0) if len(parts) > 1 else parts[0]


def _window(x_ref, r0, s, after):
    return _rows_with_halo(x_ref, r0, CONV_ROWS_FWD, s, True, after).astype(F32)


def _shifted(window, k, rows):
    if k == 0:
        return window[HALO:HALO + rows]
    return pltpu.roll(window, k, 0)[HALO:HALO + rows]


def _conv_taps(window, w_ref, kk, rows):
    acc = None
    for i in range(kk):
        term = w_ref[i:i + 1, :] * _shifted(window, kk - 1 - i, rows)
        acc = term if acc is None else acc + term
    return acc


def _row_loop(rows, step):
    def body(r, carry):
        return step(pl.multiple_of(r * rows, rows), carry)
    return body


def _conv_bwd_rows(x, dpe, w_ref, kk):
    dp = dpe[:CONV_ROWS]
    dx = None
    dws = []
    for i in range(kk):
        k = kk - 1 - i
        later = dp if k == 0 else pltpu.roll(dpe, dpe.shape[0] - k, 0)[:CONV_ROWS]
        dws.append(jnp.sum(later * x, axis=0, keepdims=True))
        term = w_ref[i:i + 1, :] * later
        dx = term if dx is None else dx + term
    return dx, dws, jnp.sum(dp, axis=0, keepdims=True)


def _conv_a_fwd(xraw, w, b, *, name, tc=128):
    s, c = xraw.shape
    kk = 4

    def body(x_ref, w_ref, b_ref, o_ref, pre_ref):
        def step(r0, carry):
            pre = _conv_taps(_window(x_ref, r0, s, False), w_ref, kk, CONV_ROWS_FWD) + b_ref[...]
            o_ref[pl.ds(r0, CONV_ROWS_FWD), :] = pre * _sigmoid(pre)
            pre_ref[pl.ds(r0, CONV_ROWS_FWD), :] = pre.astype(BF16)
            return carry

        lax.fori_loop(0, s // CONV_ROWS_FWD, _row_loop(CONV_ROWS_FWD, step), 0)

    col = pl.BlockSpec((s, tc), lambda j: (0, j))
    return pl.pallas_call(
        body, name=name, grid=(c // tc,),
        in_specs=[col, pl.BlockSpec((8, tc), lambda j: (0, j)), pl.BlockSpec((1, tc), lambda j: (0, j))],
        out_specs=[col, col], out_shape=[jax.ShapeDtypeStruct((s, c), F32), jax.ShapeDtypeStruct((s, c), BF16)],
        compiler_params=_params(("parallel",)),
    )(xraw, w, b)


def _conv_a_bwd(xraw, pre, w, dy, *, name, tc=128):
    s, c = xraw.shape
    kk = 4

    def body(x_ref, pre_ref, w_ref, dy_ref, dx_ref, dw_ref, db_ref):
        def step(r0, carry):
            pre = _rows_with_halo(pre_ref, r0, CONV_ROWS, s, False, True)
            sg = _sigmoid(pre)
            dpe = _rows_with_halo(dy_ref, r0, CONV_ROWS, s, False, True) * (sg * (1.0 + pre * (1.0 - sg)))
            dx, dws, db = _conv_bwd_rows(x_ref[pl.ds(r0, CONV_ROWS), :].astype(F32), dpe, w_ref, kk)
            dx_ref[pl.ds(r0, CONV_ROWS), :] = dx.astype(BF16)
            return tuple(acc + new for acc, new in zip(carry, dws + [db]))

        zero = jnp.zeros((1, tc), F32)
        sums = lax.fori_loop(0, s // CONV_ROWS, _row_loop(CONV_ROWS, step), (zero,) * (kk + 1))
        db_ref[...] = sums[kk]
        dw_ref[...] = jnp.concatenate(list(sums[:kk]) + [jnp.zeros((8 - kk, tc), F32)], axis=0)

    col = pl.BlockSpec((s, tc), lambda j: (0, j))
    w8 = pl.BlockSpec((8, tc), lambda j: (0, j))
    row = pl.BlockSpec((1, tc), lambda j: (0, j))
    return pl.pallas_call(
        body, name=name, grid=(c // tc,),
        in_specs=[col, col, w8, col], out_specs=[col, w8, row],
        out_shape=[jax.ShapeDtypeStruct((s, c), BF16), jax.ShapeDtypeStruct((8, c), F32),
                   jax.ShapeDtypeStruct((1, c), F32)],
        compiler_params=_params(("parallel",)),
    )(xraw, pre, w, dy)


def _conv_f_fwd(up_raw, w, b, *, name, tc=128):
    s, c2 = up_raw.shape
    c = c2 // 2
    nb = c // tc
    kk = 3

    def body(xa_ref, xv_ref, wa_ref, wv_ref, ba_ref, bv_ref, o_ref, a_out, v_out):
        def step(r0, carry):
            a = _conv_taps(_window(xa_ref, r0, s, False), wa_ref, kk, CONV_ROWS_FWD) + ba_ref[...]
            v = _conv_taps(_window(xv_ref, r0, s, False), wv_ref, kk, CONV_ROWS_FWD) + bv_ref[...]
            o_ref[pl.ds(r0, CONV_ROWS_FWD), :] = (a * _sigmoid(a) * v).astype(BF16)
            a_out[pl.ds(r0, CONV_ROWS_FWD), :] = a.astype(BF16)
            v_out[pl.ds(r0, CONV_ROWS_FWD), :] = v.astype(BF16)
            return carry

        lax.fori_loop(0, s // CONV_ROWS_FWD, _row_loop(CONV_ROWS_FWD, step), 0)

    col_a = pl.BlockSpec((s, tc), lambda j: (0, j))
    col_v = pl.BlockSpec((s, tc), lambda j: (0, j + nb))
    half = jax.ShapeDtypeStruct((s, c), BF16)
    return pl.pallas_call(
        body, name=name, grid=(nb,),
        in_specs=[col_a, col_v, pl.BlockSpec((8, tc), lambda j: (0, j)), pl.BlockSpec((8, tc), lambda j: (0, j + nb)),
                  pl.BlockSpec((1, tc), lambda j: (0, j)), pl.BlockSpec((1, tc), lambda j: (0, j + nb))],
        out_specs=[col_a, col_a, col_a], out_shape=[half, half, half],
        compiler_params=_params(("parallel",)),
    )(up_raw, up_raw, w, w, b, b)


def _conv_f_bwd(up_raw, a_pre, v_pre, w, dact, *, name, tc=128):
    s, c2 = up_raw.shape
    c = c2 // 2
    nb = c // tc
    kk = 3

    def body(xa_ref, xv_ref, a_ref, v_ref, wa_ref, wv_ref, d_ref,
             dxa_ref, dxv_ref, dwa_ref, dwv_ref, dba_ref, dbv_ref):
        def step(r0, carry):
            a = _rows_with_halo(a_ref, r0, CONV_ROWS, s, False, True)
            v = _rows_with_halo(v_ref, r0, CONV_ROWS, s, False, True)
            sg = _sigmoid(a)
            d = _rows_with_halo(d_ref, r0, CONV_ROWS, s, False, True)
            rows = pl.ds(r0, CONV_ROWS)
            dxa, dwas, dba = _conv_bwd_rows(xa_ref[rows, :].astype(F32), d * v * (sg * (1.0 + a * (1.0 - sg))),
                                            wa_ref, kk)
            dxv, dwvs, dbv = _conv_bwd_rows(xv_ref[rows, :].astype(F32), d * (a * sg), wv_ref, kk)
            dxa_ref[pl.ds(r0, CONV_ROWS), :] = dxa.astype(BF16)
            dxv_ref[pl.ds(r0, CONV_ROWS), :] = dxv.astype(BF16)
            return tuple(acc + new for acc, new in zip(carry, dwas + [dba] + dwvs + [dbv]))

        zero = jnp.zeros((1, tc), F32)
        sums = lax.fori_loop(0, s // CONV_ROWS, _row_loop(CONV_ROWS, step), (zero,) * (2 * kk + 2))
        pad = [jnp.zeros((8 - kk, tc), F32)]
        dwa_ref[...] = jnp.concatenate(list(sums[:kk]) + pad, axis=0)
        dba_ref[...] = sums[kk]
        dwv_ref[...] = jnp.concatenate(list(sums[kk + 1:2 * kk + 1]) + pad, axis=0)
        dbv_ref[...] = sums[2 * kk + 1]

    col_a = pl.BlockSpec((s, tc), lambda j: (0, j))
    col_v = pl.BlockSpec((s, tc), lambda j: (0, j + nb))
    w_a = pl.BlockSpec((8, tc), lambda j: (0, j))
    w_v = pl.BlockSpec((8, tc), lambda j: (0, j + nb))
    r_a = pl.BlockSpec((1, tc), lambda j: (0, j))
    r_v = pl.BlockSpec((1, tc), lambda j: (0, j + nb))
    outs = pl.pallas_call(
        body, name=name, grid=(nb,),
        in_specs=[col_a, col_v, col_a, col_a, w_a, w_v, col_a],
        out_specs=[col_a, col_a, w_a, w_a, r_a, r_a],
        out_shape=[jax.ShapeDtypeStruct((s, c), BF16), jax.ShapeDtypeStruct((s, c), BF16),
                   jax.ShapeDtypeStruct((8, c), F32), jax.ShapeDtypeStruct((8, c), F32),
                   jax.ShapeDtypeStruct((1, c), F32), jax.ShapeDtypeStruct((1, c), F32)],
        compiler_params=_params(("parallel",)),
    )(up_raw, up_raw, a_pre, v_pre, w, w, dact)
    return outs


def _tri_masks():
    row = lax.broadcasted_iota(jnp.int32, (CHUNK, CHUNK), 0)
    col = lax.broadcasted_iota(jnp.int32, (CHUNK, CHUNK), 1)
    return row >= col, row <= col


def _ssd_fwd(xbc, dt_raw, z, dt_bias, a_log, a_log_x, d_skip_x, norm_w, expand, *, name):
    s = xbc.shape[0]
    nc = s // CHUNK

    def body(xbc_ref, dtr_ref, z_ref, dtb_ref, alog_ref, alogx_ref, dskx_ref, nw_ref, e_ref,
             y_ref, ya_ref, st_ref, state):
        @pl.when(pl.program_id(0) == 0)
        def _():
            state[...] = jnp.zeros_like(state)

        st_ref[0] = state[...]
        lower, _ = _tri_masks()
        dt = _softplus(dtr_ref[...] + dtb_ref[...])
        adt = dt * (-jnp.exp(alog_ref[...]))
        acum = _dot_exact_lhs(lower.astype(BF16), _split3(adt))
        acum_t = acum.T
        dt_terms, acum_terms = _split3(dt), _split3(acum)
        for g in range(SSD_GROUPS):
            sl = slice(GROUP_COLS * g, GROUP_COLS * (g + 1))
            dt_x = _dot_terms(dt_terms[:2], e_ref[:, sl])
            acum_x = _dot_terms(acum_terms, e_ref[:, sl])
            tot_x = jnp.sum(dt_x * (-jnp.exp(alogx_ref[:, sl])), axis=0, keepdims=True)
            xs = xbc_ref[:, sl]
            xdt = xs * dt_x
            xdt_b = xdt.astype(BF16)
            bg = xbc_ref[:, SSD_D_INNER + SSD_STATE * g:SSD_D_INNER + SSD_STATE * (g + 1)].astype(BF16)
            cg = xbc_ref[:, SSD_D_INNER + SSD_BC + SSD_STATE * g:SSD_D_INNER + SSD_BC + SSD_STATE * (g + 1)].astype(BF16)
            cb = _dot(cg, bg, NT)
            st_g = state[:, sl]
            y_off = _dot(cg, st_g.astype(BF16)) * jnp.exp(acum_x)
            parts = []
            for r in range(SSD_HEADS_PER_GROUP):
                h = SSD_HEADS_PER_GROUP * g + r
                dec = jnp.exp(jnp.where(lower, acum[:, h:h + 1] - acum_t[h:h + 1, :], -jnp.inf))
                parts.append(_dot((cb * dec).astype(BF16), xdt_b[:, SSD_HEAD_DIM * r:SSD_HEAD_DIM * (r + 1)]))
            y_ref[:, sl] = jnp.concatenate(parts, axis=1) + y_off + dskx_ref[:, sl] * xs
            wgt = (xdt * jnp.exp(tot_x - acum_x)).astype(BF16)
            state[:, sl] = st_g * jnp.exp(tot_x) + _dot(bg, wgt, TN)
        zv = z_ref[...].astype(F32)
        q = y_ref[...] * (zv * _sigmoid(zv))
        r = lax.rsqrt(jnp.mean(q * q, axis=-1, keepdims=True) + NORM_EPS)
        ya_ref[...] = (q * r * nw_ref[...]).astype(BF16)

    def chunk(w):
        return pl.BlockSpec((CHUNK, w), lambda c: (c, 0))

    def const(shape):
        return pl.BlockSpec(shape, lambda c: (0,) * len(shape))

    return pl.pallas_call(
        body, name=name, grid=(nc,),
        in_specs=[chunk(SSD_XBC), chunk(LANES), chunk(SSD_D_INNER), const((1, LANES)), const((1, LANES)),
                  const((1, SSD_D_INNER)), const((1, SSD_D_INNER)), const((1, SSD_D_INNER)),
                  const((LANES, SSD_D_INNER))],
        out_specs=[chunk(SSD_D_INNER), chunk(SSD_D_INNER),
                   pl.BlockSpec((1, SSD_STATE, SSD_D_INNER), lambda c: (c, 0, 0))],
        out_shape=[jax.ShapeDtypeStruct((s, SSD_D_INNER), F32), jax.ShapeDtypeStruct((s, SSD_D_INNER), BF16),
                   jax.ShapeDtypeStruct((nc, SSD_STATE, SSD_D_INNER), F32)],
        scratch_shapes=[pltpu.VMEM((SSD_STATE, SSD_D_INNER), F32)],
        compiler_params=_params(("arbitrary",)),
    )(xbc, dt_raw, z, dt_bias, a_log, a_log_x, d_skip_x, norm_w, expand)


def _ssd_bwd(dya, y, z, xbc, dt_raw, states, dt_bias, a_log, a_log_x, d_skip_x, norm_w, expand, expand_t, *, name):
    s = xbc.shape[0]
    nc = s // CHUNK

    def body(dya_ref, y_ref, z_ref, xbc_ref, dtr_ref, stp_ref, dtb_ref, alog_ref, alogx_ref, dskx_ref, nw_ref,
             e_ref, et_ref, dz_ref, dxbc_ref, ddt_ref, dnw_ref, ddsk_ref, dalog_ref, ddtb_ref,
             dstate, dy_sc, dskcol):
        i = pl.program_id(0)

        @pl.when(i == 0)
        def _():
            dstate[...] = jnp.zeros_like(dstate)
            dskcol[...] = jnp.zeros_like(dskcol)
            dnw_ref[...] = jnp.zeros_like(dnw_ref)
            dalog_ref[...] = jnp.zeros_like(dalog_ref)
            ddtb_ref[...] = jnp.zeros_like(ddtb_ref)
            ddsk_ref[...] = jnp.zeros_like(ddsk_ref)

        lower, upper = _tri_masks()
        rows = lax.broadcasted_iota(jnp.int32, (CHUNK, LANES), 0)
        pre = dtr_ref[...] + dtb_ref[...]
        dt = _softplus(pre)
        a = -jnp.exp(alog_ref[...])
        acum = _dot_exact_lhs(lower.astype(BF16), _split3(dt * a))
        acum_t = acum.T
        dt_terms, acum_terms = _split3(dt), _split3(acum)

        yv = y_ref[...]
        zv = z_ref[...].astype(F32)
        sz = _sigmoid(zv)
        silu_z = zv * sz
        q = yv * silu_z
        r = lax.rsqrt(jnp.mean(q * q, axis=-1, keepdims=True) + NORM_EPS)
        qhat = q * r
        dyav = dya_ref[...]
        dqhat = dyav * nw_ref[...]
        dnw_ref[...] += jnp.sum(dyav * qhat, axis=0, keepdims=True)
        dq = r * (dqhat - qhat * jnp.mean(dqhat * qhat, axis=-1, keepdims=True))
        dy_sc[...] = dq * silu_z
        dz_ref[...] = (dq * yv * (sz * (1.0 + zv * (1.0 - sz)))).astype(BF16)

        da_cum = jnp.zeros((CHUNK, LANES), F32)
        ddt = jnp.zeros((CHUNK, LANES), F32)
        for g in range(SSD_GROUPS):
            sl = slice(GROUP_COLS * g, GROUP_COLS * (g + 1))
            et_g = et_ref[sl, :]
            dt_x = _dot_terms(dt_terms[:2], e_ref[:, sl])
            acum_x = _dot_terms(acum_terms, e_ref[:, sl])
            tot_x = jnp.sum(dt_x * (-jnp.exp(alogx_ref[:, sl])), axis=0, keepdims=True)
            e_tot = jnp.exp(tot_x)
            dec_s = jnp.exp(tot_x - acum_x)
            xs = xbc_ref[:, sl]
            xdt = xs * dt_x
            xdt_b = xdt.astype(BF16)
            dy = dy_sc[:, sl]
            dy_b = dy.astype(BF16)
            dskx = dskx_ref[:, sl]
            y_ssd = y_ref[:, sl] - dskx * xs
            dskcol[:, sl] += jnp.sum(dy * xs, axis=0, keepdims=True)
            bg = xbc_ref[:, SSD_D_INNER + SSD_STATE * g:SSD_D_INNER + SSD_STATE * (g + 1)].astype(BF16)
            cg = xbc_ref[:, SSD_D_INNER + SSD_BC + SSD_STATE * g:SSD_D_INNER + SSD_BC + SSD_STATE * (g + 1)].astype(BF16)
            cb_t = _dot(bg, cg, NT)
            sp = stp_ref[0, :, sl]
            ds_g = dstate[:, sl]
            ds_b = ds_g.astype(BF16)
            dye_b = (dy * jnp.exp(acum_x)).astype(BF16)
            dc = _dot(dye_b, sp.astype(BF16), NT)
            dxdt_state = dec_s * _dot(bg, ds_b)
            db = _dot((xdt * dec_s).astype(BF16), ds_b, NT)
            dcb_t = jnp.zeros((CHUNK, CHUNK), F32)
            parts = []
            for rr in range(SSD_HEADS_PER_GROUP):
                h = SSD_HEADS_PER_GROUP * g + rr
                hs = slice(SSD_HEAD_DIM * rr, SSD_HEAD_DIM * (rr + 1))
                dec_t = jnp.exp(jnp.where(upper, acum_t[h:h + 1, :] - acum[:, h:h + 1], -jnp.inf))
                parts.append(_dot((cb_t * dec_t).astype(BF16), dy_b[:, hs]))
                dcb_t = dcb_t + _dot(xdt_b[:, hs], dy_b[:, hs], NT) * dec_t
            dxdt = jnp.concatenate(parts, axis=1) + dxdt_state
            dcb_tb = dcb_t.astype(BF16)
            dc = dc + _dot(dcb_tb, bg, TN)
            db = db + _dot(dcb_tb, cg)
            tot_col = jnp.sum(ds_g * sp, axis=0, keepdims=True) * e_tot + jnp.sum(dxdt_state * xdt, axis=0, keepdims=True)
            d_tot = _dot_terms(_split3(jnp.broadcast_to(tot_col, (8, GROUP_COLS))), et_g)
            d_tot = jnp.max(d_tot, axis=0, keepdims=True)
            pair_sums = dy_b.astype(F32) * y_ssd - xdt_b.astype(F32) * dxdt
            da_cum = da_cum + _dot_terms(_split3(pair_sums), et_g) + jnp.where(rows == CHUNK - 1, d_tot, 0.0)
            ddt = ddt + _dot_terms(_split3(dxdt * xs)[:2], et_g)
            dxbc_ref[:, sl] = dy * dskx + dxdt * dt_x
            dxbc_ref[:, SSD_D_INNER + SSD_STATE * g:SSD_D_INNER + SSD_STATE * (g + 1)] = db
            dxbc_ref[:, SSD_D_INNER + SSD_BC + SSD_STATE * g:SSD_D_INNER + SSD_BC + SSD_STATE * (g + 1)] = dc
            dstate[:, sl] = e_tot * ds_g + _dot(cg, dye_b, TN)

        dadt = _dot_exact_lhs(upper.astype(BF16), _split3(da_cum))
        ddt = ddt + dadt * a
        dalog_ref[...] += jnp.sum(dadt * dt, axis=0, keepdims=True)
        dpre = ddt * _sigmoid(pre)
        ddtb_ref[...] += jnp.sum(dpre, axis=0, keepdims=True)
        ddt_ref[...] = dpre.astype(BF16)

        @pl.when(i == nc - 1)
        def _():
            dalog_ref[...] = dalog_ref[...] * a
            dsk = _dot_terms(_split3(jnp.broadcast_to(dskcol[...], (8, SSD_D_INNER))), et_ref[...])
            ddsk_ref[...] = jnp.max(dsk, axis=0, keepdims=True)

    def chunk(w):
        return pl.BlockSpec((CHUNK, w), lambda i: (nc - 1 - i, 0))

    def const(shape):
        return pl.BlockSpec(shape, lambda i: (0,) * len(shape))

    return pl.pallas_call(
        body, name=name, grid=(nc,),
        in_specs=[chunk(SSD_D_INNER), chunk(SSD_D_INNER), chunk(SSD_D_INNER), chunk(SSD_XBC), chunk(LANES),
                  pl.BlockSpec((1, SSD_STATE, SSD_D_INNER), lambda i: (nc - 1 - i, 0, 0)),
                  const((1, LANES)), const((1, LANES)), const((1, SSD_D_INNER)), const((1, SSD_D_INNER)),
                  const((1, SSD_D_INNER)), const((LANES, SSD_D_INNER)), const((SSD_D_INNER, LANES))],
        out_specs=[chunk(SSD_D_INNER), chunk(SSD_XBC), chunk(LANES), const((1, SSD_D_INNER)), const((1, LANES)),
                   const((1, LANES)), const((1, LANES))],
        out_shape=[jax.ShapeDtypeStruct((s, SSD_D_INNER), BF16), jax.ShapeDtypeStruct((s, SSD_XBC), F32),
                   jax.ShapeDtypeStruct((s, LANES), BF16), jax.ShapeDtypeStruct((1, SSD_D_INNER), F32),
                   jax.ShapeDtypeStruct((1, LANES), F32), jax.ShapeDtypeStruct((1, LANES), F32),
                   jax.ShapeDtypeStruct((1, LANES), F32)],
        scratch_shapes=[pltpu.VMEM((SSD_STATE, SSD_D_INNER), F32), pltpu.VMEM((CHUNK, SSD_D_INNER), F32),
                        pltpu.VMEM((1, SSD_D_INNER), F32)],
        compiler_params=_params(("arbitrary",)),
    )(dya, y, z, xbc, dt_raw, states, dt_bias, a_log, a_log_x, d_skip_x, norm_w, expand, expand_t)


GELU_K = math.sqrt(2.0 / math.pi)
GELU_C = 0.044715


def _gelu(x):
    return 0.5 * x * (1.0 + jnp.tanh(GELU_K * (x + GELU_C * x * x * x)))


def _gelu_grad(x):
    t = jnp.tanh(GELU_K * (x + GELU_C * x * x * x))
    return 0.5 * (1.0 + t) + 0.5 * x * (1.0 - t * t) * (GELU_K * (1.0 + 3.0 * GELU_C * x * x))


def _sgu_pre(uv_ref, uvb_ref, lnw_ref, lnb_ref):
    uv = uv_ref[...].astype(F32) + uvb_ref[...]
    guv = _gelu(uv)
    u = guv[:, :SGU_WIDTH]
    v = guv[:, SGU_WIDTH:]
    mu = jnp.mean(v, axis=-1, keepdims=True)
    vc = v - mu
    rstd = lax.rsqrt(jnp.mean(vc * vc, axis=-1, keepdims=True) + LN_EPS)
    vhat = vc * rstd
    vn = vhat * lnw_ref[...] + lnb_ref[...]
    return uv, u, vhat, rstd, vn


def _sgu_fwd(uv_raw, uv_b, ln_w, ln_b, w_sp, b_sp_t, *, name):
    s = uv_raw.shape[0]
    nc = s // CHUNK

    def body(uv_ref, uvb_ref, lnw_ref, lnb_ref, w_ref, bt_ref, o_ref):
        lower, _ = _tri_masks()
        _, u, _, _, vn = _sgu_pre(uv_ref, uvb_ref, lnw_ref, lnb_ref)
        vn_b = vn.astype(BF16)
        bt = bt_ref[...]
        for g in range(SGU_GROUPS):
            gs = slice(LANES * g, LANES * (g + 1))
            wc = jnp.where(lower, w_ref[g], 0.0).astype(BF16)
            mixed = _dot(wc, vn_b[:, gs]) + bt[:, g:g + 1]
            o_ref[:, gs] = (u[:, gs] * mixed).astype(BF16)

    def const(shape):
        return pl.BlockSpec(shape, lambda c: (0,) * len(shape))

    return pl.pallas_call(
        body, name=name, grid=(nc,),
        in_specs=[pl.BlockSpec((CHUNK, 2 * SGU_WIDTH), lambda c: (c, 0)), const((1, 2 * SGU_WIDTH)),
                  const((1, SGU_WIDTH)), const((1, SGU_WIDTH)), const((SGU_GROUPS, CHUNK, CHUNK)),
                  const((CHUNK, LANES))],
        out_specs=pl.BlockSpec((CHUNK, SGU_WIDTH), lambda c: (c, 0)),
        out_shape=jax.ShapeDtypeStruct((s, SGU_WIDTH), BF16),
        compiler_params=_params(("parallel",)),
    )(uv_raw, uv_b, ln_w, ln_b, w_sp, b_sp_t)


def _sgu_bwd(uv_raw, dyb, uv_b, ln_w, ln_b, w_sp, b_sp_t, group_sum, *, name):
    s = uv_raw.shape[0]
    nc = s // CHUNK

    def body(uv_ref, dy_ref, uvb_ref, lnw_ref, lnb_ref, w_ref, bt_ref, gsum_ref,
             duv_ref, dw_ref, dbt_ref, dlnw_ref, dlnb_ref, duvb_ref):
        @pl.when(pl.program_id(0) == 0)
        def _():
            dw_ref[...] = jnp.zeros_like(dw_ref)
            dbt_ref[...] = jnp.zeros_like(dbt_ref)
            dlnw_ref[...] = jnp.zeros_like(dlnw_ref)
            dlnb_ref[...] = jnp.zeros_like(dlnb_ref)
            duvb_ref[...] = jnp.zeros_like(duvb_ref)

        lower, _ = _tri_masks()
        uv, u, vhat, rstd, vn = _sgu_pre(uv_ref, uvb_ref, lnw_ref, lnb_ref)
        vn_b = vn.astype(BF16)
        bt = bt_ref[...]
        dy = dy_ref[...].astype(F32)
        du_parts, dvn_parts, dmix_parts = [], [], []
        for g in range(SGU_GROUPS):
            gs = slice(LANES * g, LANES * (g + 1))
            wc = jnp.where(lower, w_ref[g], 0.0).astype(BF16)
            mixed = _dot(wc, vn_b[:, gs]) + bt[:, g:g + 1]
            du_parts.append(dy[:, gs] * mixed)
            dmix = dy[:, gs] * u[:, gs]
            dmix_b = dmix.astype(BF16)
            dmix_parts.append(dmix)
            dw_ref[g] += jnp.where(lower, _dot(dmix_b, vn_b[:, gs], NT), 0.0)
            dvn_parts.append(_dot(wc, dmix_b, TN))
        dmixed = jnp.concatenate(dmix_parts, axis=1)
        dbt_ref[...] += _dot_terms(_split3(dmixed), gsum_ref[...])
        dvn = jnp.concatenate(dvn_parts, axis=1)
        dlnw_ref[...] += jnp.sum(dvn * vhat, axis=0, keepdims=True)
        dlnb_ref[...] += jnp.sum(dvn, axis=0, keepdims=True)
        dvhat = dvn * lnw_ref[...]
        dv = rstd * (dvhat - jnp.mean(dvhat, axis=-1, keepdims=True)
                     - vhat * jnp.mean(dvhat * vhat, axis=-1, keepdims=True))
        dguv = jnp.concatenate(du_parts + [dv], axis=1)
        duv = dguv * _gelu_grad(uv)
        duvb_ref[...] += jnp.sum(duv, axis=0, keepdims=True)
        duv_ref[...] = duv.astype(BF16)

    def const(shape):
        return pl.BlockSpec(shape, lambda c: (0,) * len(shape))

    return pl.pallas_call(
        body, name=name, grid=(nc,),
        in_specs=[pl.BlockSpec((CHUNK, 2 * SGU_WIDTH), lambda c: (c, 0)),
                  pl.BlockSpec((CHUNK, SGU_WIDTH), lambda c: (c, 0)), const((1, 2 * SGU_WIDTH)),
                  const((1, SGU_WIDTH)), const((1, SGU_WIDTH)), const((SGU_GROUPS, CHUNK, CHUNK)),
                  const((CHUNK, LANES)), const((SGU_WIDTH, LANES))],
        out_specs=[pl.BlockSpec((CHUNK, 2 * SGU_WIDTH), lambda c: (c, 0)), const((SGU_GROUPS, CHUNK, CHUNK)),
                   const((CHUNK, LANES)), const((1, SGU_WIDTH)), const((1, SGU_WIDTH)), const((1, 2 * SGU_WIDTH))],
        out_shape=[jax.ShapeDtypeStruct((s, 2 * SGU_WIDTH), BF16),
                   jax.ShapeDtypeStruct((SGU_GROUPS, CHUNK, CHUNK), F32), jax.ShapeDtypeStruct((CHUNK, LANES), F32),
                   jax.ShapeDtypeStruct((1, SGU_WIDTH), F32), jax.ShapeDtypeStruct((1, SGU_WIDTH), F32),
                   jax.ShapeDtypeStruct((1, 2 * SGU_WIDTH), F32)],
        compiler_params=_params(("arbitrary",)),
    )(uv_raw, dyb, uv_b, ln_w, ln_b, w_sp, b_sp_t, group_sum)


def _gate_fwd(gates_raw, b_gate, p_a, p_b, *, name, tm=512):
    s = p_a.shape[0]
    tm = min(tm, s)

    def body(ga_ref, gb_ref, ba_ref, bb_ref, pa_ref, pb_ref, o_ref):
        ga = _sigmoid(ga_ref[...].astype(F32) + ba_ref[...])
        gb = _sigmoid(gb_ref[...].astype(F32) + bb_ref[...])
        o_ref[...] = (ga * pa_ref[...].astype(F32) + gb * pb_ref[...].astype(F32)).astype(BF16)

    t_a = pl.BlockSpec((tm, D_MODEL), lambda i: (i, 0))
    t_b = pl.BlockSpec((tm, D_MODEL), lambda i: (i, 1))
    r_a = pl.BlockSpec((1, D_MODEL), lambda i: (0, 0))
    r_b = pl.BlockSpec((1, D_MODEL), lambda i: (0, 1))
    return pl.pallas_call(
        body, name=name, grid=(s // tm,),
        in_specs=[t_a, t_b, r_a, r_b, t_a, t_a], out_specs=t_a,
        out_shape=jax.ShapeDtypeStruct((s, D_MODEL), BF16),
        compiler_params=_params(("parallel",)),
    )(gates_raw, gates_raw, b_gate, b_gate, p_a, p_b)


def _gate_bwd(gates_raw, b_gate, p_a, p_b, dm, *, name, tm=512):
    s = p_a.shape[0]
    tm = min(tm, s)

    def body(ga_ref, gb_ref, ba_ref, bb_ref, pa_ref, pb_ref, dm_ref, dpa_ref, dpb_ref, dga_ref, dgb_ref,
             dba_ref, dbb_ref):
        @pl.when(pl.program_id(0) == 0)
        def _():
            dba_ref[...] = jnp.zeros_like(dba_ref)
            dbb_ref[...] = jnp.zeros_like(dbb_ref)

        d = dm_ref[...].astype(F32)
        for g_ref, b_ref, p_ref, dp_ref, dg_ref, db_ref in ((ga_ref, ba_ref, pa_ref, dpa_ref, dga_ref, dba_ref),
                                                            (gb_ref, bb_ref, pb_ref, dpb_ref, dgb_ref, dbb_ref)):
            sg = _sigmoid(g_ref[...].astype(F32) + b_ref[...])
            dp_ref[...] = (d * sg).astype(BF16)
            dg = d * p_ref[...].astype(F32) * (sg * (1.0 - sg))
            dg_ref[...] = dg.astype(BF16)
            db_ref[...] += jnp.sum(dg, axis=0, keepdims=True)

    t_a = pl.BlockSpec((tm, D_MODEL), lambda i: (i, 0))
    t_b = pl.BlockSpec((tm, D_MODEL), lambda i: (i, 1))
    r_a = pl.BlockSpec((1, D_MODEL), lambda i: (0, 0))
    r_b = pl.BlockSpec((1, D_MODEL), lambda i: (0, 1))
    big = jax.ShapeDtypeStruct((s, D_MODEL), BF16)
    row = jax.ShapeDtypeStruct((1, D_MODEL), F32)
    return pl.pallas_call(
        body, name=name, grid=(s // tm,),
        in_specs=[t_a, t_b, r_a, r_b, t_a, t_a, t_a], out_specs=[t_a, t_a, t_a, t_a, r_a, r_a],
        out_shape=[big, big, big, big, row, row],
        compiler_params=_params(("arbitrary",)),
    )(gates_raw, gates_raw, b_gate, b_gate, p_a, p_b, dm)


def _adamw_update(w_ref, g_ref, m_ref, v_ref, d_ref, mo_ref, vo_ref):
    gv = g_ref[...]
    mn = ADAM_B1 * m_ref[...] + (1.0 - ADAM_B1) * gv
    vn = ADAM_B2 * v_ref[...] + (1.0 - ADAM_B2) * (gv * gv)
    m_hat = mn / (1.0 - ADAM_B1 ** ADAM_STEP)
    v_hat = vn / (1.0 - ADAM_B2 ** ADAM_STEP)
    d_ref[...] = -ADAM_LR * (m_hat / (jnp.sqrt(v_hat) + ADAM_EPS) + ADAM_WD * w_ref[...])
    mo_ref[...] = mn
    vo_ref[...] = vn


def _adamw_many(ws, gs, ms, vs, *, name):
    n = len(ws)

    def body(*refs):
        for i in range(n):
            _adamw_update(*[refs[k * n + i] for k in range(7)])

    whole = pl.BlockSpec(memory_space=pltpu.VMEM)
    sds = [jax.ShapeDtypeStruct(w.shape, F32) for w in ws]
    outs = pl.pallas_call(
        body, name=name, in_specs=[whole] * (4 * n), out_specs=[whole] * (3 * n), out_shape=sds * 3,
        compiler_params=pltpu.CompilerParams(vmem_limit_bytes=VMEM_LIMIT),
    )(*ws, *gs, *ms, *vs)
    return outs[:n], outs[n:2 * n], outs[2 * n:]


def _adamw(w, g, m, v, *, name, tr=128):
    r, c = w.shape
    tr = min(tr, r)
    assert r % tr == 0, (name, r, tr)
    body = functools.partial(_adamw_update)

    blk = pl.BlockSpec((tr, c), lambda i: (i, 0))
    sds = jax.ShapeDtypeStruct((r, c), F32)
    return pl.pallas_call(
        body, name=name, grid=(r // tr,), in_specs=[blk] * 4, out_specs=[blk] * 3, out_shape=[sds] * 3,
        compiler_params=_params(("parallel",)),
    )(w, g, m, v)


def _adamw_two_sums(w, g_a, g_b, m, v, *, name, tr=128):
    r, c = w.shape
    tr = min(tr, r)
    assert r % tr == 0, (name, r, tr)

    def body(w_ref, ga_ref, gb_ref, m_ref, v_ref, g_ref, d_ref, mo_ref, vo_ref):
        g_ref[...] = ga_ref[...] + gb_ref[...]
        _adamw_update(w_ref, g_ref, m_ref, v_ref, d_ref, mo_ref, vo_ref)

    blk = pl.BlockSpec((tr, c), lambda i: (i, 0))
    sds = jax.ShapeDtypeStruct((r, c), F32)
    return pl.pallas_call(
        body, name=name, grid=(r // tr,), in_specs=[blk] * 5, out_specs=[blk] * 4, out_shape=[sds] * 4,
        compiler_params=_params(("parallel",)),
    )(w, g_a, g_b, m, v)


def _tile(n, pref):
    if n <= pref:
        return n
    best = LANES
    for t in range(LANES, pref + 1, LANES):
        if n % t == 0:
            best = t
    return best


MATMUL_BLOCK_BYTES = 20 * 1024 * 1024


def _mm(pairs, name, **kw):
    trans_b = kw.get("trans_b", False)
    m = (pairs[0][0][0] if isinstance(pairs[0][0], tuple) else pairs[0][0]).shape[0]
    ktot, n = 0, None
    for _, b in pairs:
        shape = b[0].shape[1:] if isinstance(b, tuple) else b.shape
        ktot += shape[1] if trans_b else shape[0]
        n = shape[0] if trans_b else shape[1]
    out_bytes = 4 * (2 if kw.get("add") is not None else 1)
    best = None
    for tm in (256, 512, 1024, 2048):
        for tn in range(LANES, min(n, 1536) + 1, LANES):
            if m % min(tm, m) or n % tn:
                continue
            fits = 2 * ktot * (min(tm, m) + tn) + out_bytes * min(tm, m) * tn <= MATMUL_BLOCK_BYTES
            if fits and (best is None or min(tm, m) * tn >= best[0] * best[1]):
                best = (min(tm, m), tn)
    return _matmul(pairs, tm=best[0], tn=best[1], name=name, **kw)


def _wgrad(a, b, name, **kw):
    return _matmul_tn(a, b, tk=_tile(a.shape[1], 1408), tn=kw.pop("tn", _tile(b.shape[1], 1024)), tm=2048,
                      name=name, **kw)


def _local_step(x, target, get_weight, small, emit_grad):
    heads = jnp.arange(SSD_D_INNER) // SSD_HEAD_DIM
    expand = (jnp.arange(LANES)[:, None] == heads[None, :]).astype(BF16)
    expand_t = expand.T
    group_sum = (jnp.arange(SGU_WIDTH)[:, None] // LANES == jnp.arange(LANES)[None, :]).astype(BF16)
    pad_h = LANES - SSD_HEADS
    dt_bias = jnp.pad(small["dt_bias"], ((0, 0), (0, pad_h)))
    a_log = jnp.pad(small["a_log"], ((0, 0), (0, pad_h)))
    a_log_x = jnp.repeat(small["a_log"], SSD_HEAD_DIM, axis=1)
    d_skip_x = jnp.repeat(small["d_skip"], SSD_HEAD_DIM, axis=1)
    b_sp_t = jnp.pad(small["b_spatial"][0].T, ((0, 0), (0, LANES - SGU_GROUPS)))
    w_sp = small["w_spatial"][0]
    conv_a_w = jnp.pad(small["conv_a_w"], ((0, 4), (0, 0)))
    conv_f_w = jnp.pad(small["conv_f_w"], ((0, 5), (0, 0)))
    final_w = small["final_norm_w"].reshape(1, D_MODEL)

    n1 = _rms_fwd(x, small["norm1_w"], after=small.get("gathers_started"), name="rms1_fwd")
    wts = dict(get_weight("w_in", n1))
    z = _mm([(n1, wts["in_z"])], "in_z")
    xbc_raw = _mm([(n1, wts["in_xbc"])], "in_xbc")
    dt_raw = _mm([(n1, wts["in_dt"])], "in_dt")
    uv_raw = _mm([(n1, wts["in_uv"])], "in_uv", out_dtype=BF16)
    gates_raw = _mm([(n1, wts["in_gate"])], "in_gate", out_dtype=BF16)
    xbc, xbc_pre = _conv_a_fwd(xbc_raw, conv_a_w, small["conv_a_b"], name="conv_a_fwd")
    y, y_a, states = _ssd_fwd(xbc, dt_raw, z, dt_bias, a_log, a_log_x, d_skip_x, small["ssd_norm_w"], expand,
                              name="ssd_fwd")
    y_b = _sgu_fwd(uv_raw, small["uv_b"], small["v_ln_w"], small["v_ln_b"], w_sp, b_sp_t, name="sgu_fwd")
    wts.update(get_weight("w_branch", y_b))
    p_a = _mm([(y_a, wts["branch_a"])], "branch_a", out_dtype=BF16)
    p_b = _mm([(y_b, wts["branch_b"])], "branch_b", out_dtype=BF16)
    mix = _gate_fwd(gates_raw, small["b_gate"], p_a, p_b, name="gate_fwd")
    wts.update(get_weight("w_out", mix))
    h1 = _mm([(mix, wts["out"])], "out_proj", add=x)
    n2 = _rms_fwd(h1, small["norm2_w"], name="rms2_fwd")
    wts.update(get_weight("w_up", n2))
    up_w = wts["up"]
    up_cols = up_w.shape[2]
    up_raw = _matmul([(n2, (up_w, "cols"))], tm=2048, tn=up_cols, out_dtype=BF16, name="up_proj")
    act, up_a, up_v = _conv_f_fwd(up_raw, conv_f_w, small["conv_f_b"], name="conv_f_fwd")
    wts.update(get_weight("w_down", act))
    h2 = _mm([(act, wts["down"])], "down_proj", add=h1)
    loss, dh2, dh2_b, d_final = _final_fwd_bwd(h2, final_w, target, name="final_norm_loss")

    dact = _mm([(dh2_b, wts["down"])], "down_dgrad", trans_b=True)
    started = emit_grad("w_down", _wgrad(act, dh2_b, "down_wgrad"))
    dup_a, dup_v, dwf_a, dwf_v, dbf_a, dbf_v = _conv_f_bwd(up_raw, up_a, up_v, conv_f_w, dact, name="conv_f_bwd")
    dn2 = _mm([((dup_a, 0), (up_w, 0)), ((dup_a, 1), (up_w, 1)), ((dup_v, 0), (up_w, 2)), ((dup_v, 1), (up_w, 3))],
              "up_dgrad", trans_b=True, after=started, out_dtype=BF16)
    g_up = _wgrad(n2, dup_a, "up_wgrad_a", tn=up_cols, stack_out=True, part_of=(N_CHIPS, 0, None))
    g_up = _wgrad(n2, dup_v, "up_wgrad_v", tn=up_cols, stack_out=True, part_of=(N_CHIPS, N_CHIPS // 2, g_up))
    started = emit_grad("w_up", g_up)
    dh1, dh1_b, d_norm2 = _rms_bwd(h1, small["norm2_w"], dn2, dh2, bf16_copy=True, name="rms2_bwd")
    dmix = _mm([(dh1_b, wts["out"])], "out_dgrad", trans_b=True, after=started, out_dtype=BF16)
    started = emit_grad("w_out", _wgrad(mix, dh1_b, "out_wgrad"))
    dp_a, dp_b, dg_a, dg_b, dbg_a, dbg_b = _gate_bwd(gates_raw, small["b_gate"], p_a, p_b, dmix, name="gate_bwd")
    dya = _mm([(dp_a, wts["branch_a"])], "branch_a_dgrad", trans_b=True, after=started)
    dyb = _mm([(dp_b, wts["branch_b"])], "branch_b_dgrad", trans_b=True, out_dtype=BF16)
    g_branch = _wgrad(y_a, dp_a, "branch_a_wgrad", part_of=(3, 0, None))
    g_branch = _wgrad(y_b, dp_b, "branch_b_wgrad", part_of=(3, 2, g_branch))
    started_branch = emit_grad("w_branch", g_branch)
    duv, d_wsp, d_bsp_t, d_lnw, d_lnb, d_uvb = _sgu_bwd(uv_raw, dyb, small["uv_b"], small["v_ln_w"],
                                                        small["v_ln_b"], w_sp, b_sp_t, group_sum, name="sgu_bwd")
    dz, dxbc, ddt, d_ssd_nw, d_dskip, d_alog, d_dtb = _ssd_bwd(
        dya, y, z, xbc, dt_raw, states, dt_bias, a_log, a_log_x, d_skip_x, small["ssd_norm_w"], expand, expand_t,
        name="ssd_bwd")
    dxbc_raw, d_conv_a_w, d_conv_a_b = _conv_a_bwd(xbc_raw, xbc_pre, conv_a_w, dxbc, name="conv_a_bwd")
    started = emit_grad("w_in", {
        "in_z": _wgrad(n1, dz, "in_z_wgrad", after=started_branch), "in_xbc": _wgrad(n1, dxbc_raw, "in_xbc_wgrad"),
        "in_dt": _wgrad(n1, ddt, "in_dt_wgrad")[:, :SSD_HEADS], "in_uv": _wgrad(n1, duv, "in_uv_wgrad"),
        "in_gate_a": _wgrad(n1, dg_a, "in_gate_a_wgrad"), "in_gate_b": _wgrad(n1, dg_b, "in_gate_b_wgrad")})
    dn1 = _mm([(dz, wts["in_z"]), (dxbc_raw, wts["in_xbc"]), (ddt, wts["in_dt"]), (duv, wts["in_uv"]),
               (dg_a, wts["in_gate_a"]), (dg_b, wts["in_gate_b"])], "in_dgrad", trans_b=True, after=started,
              out_dtype=BF16)
    dx, d_norm1 = _rms_bwd(x, small["norm1_w"], dn1, dh1, bf16_copy=False, name="rms1_bwd")

    grads_small = {
        "norm1_w": d_norm1, "b_gate": jnp.concatenate([dbg_a, dbg_b], axis=1),
        "conv_a_w": d_conv_a_w[:4], "conv_a_b": d_conv_a_b,
        "dt_bias": d_dtb[:, :SSD_HEADS], "a_log": d_alog[:, :SSD_HEADS], "d_skip": d_dskip[:, :SSD_HEADS],
        "ssd_norm_w": d_ssd_nw, "uv_b": d_uvb, "v_ln_w": d_lnw, "v_ln_b": d_lnb,
        "w_spatial": d_wsp[None], "b_spatial": d_bsp_t[:, :SGU_GROUPS].T[None],
        "norm2_w": d_norm2, "conv_f_w": jnp.concatenate([dwf_a[:3], dwf_v[:3]], axis=1),
        "conv_f_b": jnp.concatenate([dbf_a, dbf_v], axis=1), "final_norm_w": d_final.reshape(D_MODEL),
    }
    return loss, dx, grads_small


HBM = pl.BlockSpec(memory_space=pl.ANY)
MESH = pl.DeviceIdType.MESH


def _mesh_pos():
    return lax.axis_index("x"), lax.axis_index("y"), lax.axis_index("c")


def _other_chips(x, y):
    return [(1 - x, y), (x, 1 - y), (1 - x, 1 - y)]


def _remote(src, dst, send_sems, recv_sems, k, dev):
    return pltpu.make_async_remote_copy(src_ref=src, dst_ref=dst, send_sem=send_sems.at[k], recv_sem=recv_sems.at[k],
                                        device_id=dev, device_id_type=MESH)


def _dma_sems(n):
    return [pltpu.SemaphoreType.DMA((n,)), pltpu.SemaphoreType.DMA((n,))]


HBM_ONLY = pl.BlockSpec(memory_space=pltpu.HBM)
SEMAPHORES = pl.BlockSpec(memory_space=pltpu.SEMAPHORE)
DATAFLOW_EFFECT = pltpu.SideEffectType.DATAFLOW_SIDE_EFFECTING
N_PEER_CHIPS = N_CHIPS - 1


def _gather_sends(w_ref, land_ref, send_sems, recv_sems):
    x, y, c = _mesh_pos()
    return [_remote(w_ref.at[c], land_ref.at[2 * x + y, c], send_sems, recv_sems, k, (px, py, c))
            for k, (px, py) in enumerate(_other_chips(x, y))]


def _gather_arrivals(w_ref, land_ref, send_sems, recv_sems):
    x, y, c = _mesh_pos()
    return [_remote(w_ref.at[c], land_ref.at[2 * px + py, c], send_sems, recv_sems, k, (px, py, c))
            for k, (px, py) in enumerate(_other_chips(x, y))]


def _gather_whole_sends(w_ref, land_ref, send_sems, recv_sems):
    x, y, c = _mesh_pos()
    return [_remote(w_ref, land_ref.at[2 * x + y], send_sems, recv_sems, k, (px, py, c))
            for k, (px, py) in enumerate(_other_chips(x, y))]


def _gather_whole_arrivals(w_ref, land_ref, send_sems, recv_sems):
    x, y, c = _mesh_pos()
    return [_remote(w_ref, land_ref.at[2 * px + py], send_sems, recv_sems, k, (px, py, c))
            for k, (px, py) in enumerate(_other_chips(x, y))]


def _scatter_sends(h_ref, land_ref, send_sems, recv_sems):
    x, y, c = _mesh_pos()
    return [_remote(h_ref.at[2 * px + py], land_ref.at[2 * x + y], send_sems, recv_sems, k, (px, py, c))
            for k, (px, py) in enumerate(_other_chips(x, y))]


def _scatter_arrivals(h_ref, land_ref, send_sems, recv_sems):
    x, y, c = _mesh_pos()
    return [_remote(h_ref.at[2 * x + y], land_ref.at[2 * px + py], send_sems, recv_sems, k, (px, py, c))
            for k, (px, py) in enumerate(_other_chips(x, y))]


def _exchange_wait_many(pendings, after, sends, arrivals, *, name):
    n = len(pendings)

    def body(*refs):
        for i in range(n):
            src_ref, land_ref, send_ref, recv_ref = refs[i], refs[n + i], refs[2 * n + i], refs[3 * n + i]
            for cp in sends(src_ref, land_ref, send_ref, recv_ref):
                cp.wait_send()
            for cp in arrivals(src_ref, land_ref, send_ref, recv_ref):
                cp.wait_recv()

    sources = [p[2] for p in pendings]
    landings = [p[3] for p in pendings]
    outs = pl.pallas_call(
        body, name=name,
        out_shape=tuple(pltpu.HBM(a.shape, a.dtype) for a in sources + landings),
        in_specs=[HBM_ONLY] * (2 * n) + [SEMAPHORES] * (2 * n) + [pl.BlockSpec(memory_space=pl.ANY)],
        out_specs=tuple([HBM_ONLY] * (2 * n)), input_output_aliases={i: i for i in range(2 * n)},
        compiler_params=pltpu.CompilerParams(has_side_effects=DATAFLOW_EFFECT),
    )(*sources, *landings, *[p[0] for p in pendings], *[p[1] for p in pendings], after)
    return [(outs[i], outs[n + i]) for i in range(n)]


def _sibling_sends(src_ref, land_ref, send_sems, recv_sems):
    x, y, c = _mesh_pos()
    return [_remote(src_ref, land_ref, send_sems, recv_sems, 0, (x, y, 1 - c))]


def _exchange_start(sources, landing_shapes, sends, *, after=None, name):
    n = len(sources)
    extra = [] if after is None else [after]

    def body(*refs):
        sems = refs[2 * n + len(extra):4 * n + len(extra)]
        for i in range(n):
            send_i = sends[i] if isinstance(sends, (list, tuple)) else sends
            for cp in send_i(refs[i], refs[n + i], sems[2 * i], sems[2 * i + 1]):
                cp.start()
        refs[-1][...] = jnp.zeros_like(refs[-1])

    hbm = [pltpu.HBM(s.shape, s.dtype) for s in sources] + [pltpu.HBM(shp, s.dtype)
                                                             for shp, s in zip(landing_shapes, sources)]
    outs = pl.pallas_call(
        body, name=name,
        out_shape=tuple([pltpu.SemaphoreType.DMA((N_PEER_CHIPS,))] * (2 * n) + hbm
                        + [jax.ShapeDtypeStruct((8, LANES), F32)]),
        in_specs=[HBM_ONLY] * (2 * n) + [pl.BlockSpec(memory_space=pl.ANY)] * len(extra),
        out_specs=tuple([SEMAPHORES] * (2 * n) + [HBM_ONLY] * (2 * n) + [pl.BlockSpec(memory_space=pltpu.VMEM)]),
        input_output_aliases={i: 2 * n + i for i in range(2 * n)},
        compiler_params=pltpu.CompilerParams(has_side_effects=DATAFLOW_EFFECT),
    )(*[pltpu.with_memory_space_constraint(s, pltpu.HBM) for s in sources],
      *[pltpu.with_memory_space_constraint(lax.empty(shp, s.dtype), pltpu.HBM)
        for shp, s in zip(landing_shapes, sources)], *extra)
    pending = [(outs[2 * i], outs[2 * i + 1], outs[2 * n + i], outs[3 * n + i]) for i in range(n)]
    return pending, outs[-1]


def _exchange_wait(pending, after, sends, arrivals, *, name):
    send_sems, recv_sems, source, landing = pending

    def body(src_ref, land_ref, send_ref, recv_ref, after_ref, src_out, land_out):
        for cp in sends(src_ref, land_ref, send_ref, recv_ref):
            cp.wait_send()
        for cp in arrivals(src_ref, land_ref, send_ref, recv_ref):
            cp.wait_recv()

    return pl.pallas_call(
        body, name=name,
        out_shape=(pltpu.HBM(source.shape, source.dtype), pltpu.HBM(landing.shape, landing.dtype)),
        in_specs=[HBM_ONLY, HBM_ONLY, SEMAPHORES, SEMAPHORES, pl.BlockSpec(memory_space=pl.ANY)],
        out_specs=(HBM_ONLY, HBM_ONLY), input_output_aliases={0: 0, 1: 1},
        compiler_params=pltpu.CompilerParams(has_side_effects=DATAFLOW_EFFECT),
    )(source, landing, send_sems, recv_sems, after)


def _gather_ici(shard, *, name):
    _, rh, cols = shard.shape

    def body(w_ref, o_ref, send_sems, recv_sems):
        x, y, c = _mesh_pos()
        mine = 2 * x + y
        sends = []
        for k, (px, py) in enumerate(_other_chips(x, y)):
            cp = _remote(w_ref.at[c], o_ref.at[mine, c], send_sems, recv_sems, k, (px, py, c))
            cp.start()
            sends.append(cp)
        for k, (px, py) in enumerate(_other_chips(x, y)):
            _remote(w_ref.at[c], o_ref.at[2 * px + py, c], send_sems, recv_sems, k, (px, py, c)).wait_recv()
        for cp in sends:
            cp.wait_send()

    return pl.pallas_call(
        body, name=name, in_specs=[HBM], out_specs=HBM,
        out_shape=jax.ShapeDtypeStruct((N_CHIPS, 2, rh, cols), shard.dtype), scratch_shapes=_dma_sems(3),
    )(shard)


def _gather_d2d(parts, *, name):
    def body(a_ref, o_ref, send_sems, recv_sems):
        x, y, c = _mesh_pos()
        sibling = (x, y, 1 - c)
        sends = []
        for k, (px, py) in enumerate(_other_chips(x, y)):
            cp = _remote(a_ref.at[2 * px + py, c], o_ref.at[2 * px + py, c], send_sems, recv_sems, k, sibling)
            cp.start()
            sends.append(cp)
        for k, (px, py) in enumerate(_other_chips(x, y)):
            _remote(a_ref.at[2 * px + py, c], o_ref.at[2 * px + py, 1 - c], send_sems, recv_sems, k, sibling).wait_recv()
        for cp in sends:
            cp.wait_send()

    return pl.pallas_call(
        body, name=name, in_specs=[HBM], out_specs=HBM,
        out_shape=jax.ShapeDtypeStruct(parts.shape, parts.dtype),
        input_output_aliases={0: 0}, scratch_shapes=_dma_sems(3),
    )(parts)


def _all_gather_chips(shard_flat, name):
    rows, cols = shard_flat.shape
    parts = _gather_ici(shard_flat.reshape(2, rows // 2, cols), name=name + "_ici")
    others = _gather_d2d(parts, name=name + "_d2d").reshape(N_CHIPS, rows, cols)
    chip = 2 * lax.axis_index("x") + lax.axis_index("y")
    return lax.dynamic_update_slice(others, shard_flat[None], (chip, 0, 0))


def _row_tile(rows, mult, cap):
    best = mult
    for t in range(mult, min(rows, cap) + 1, mult):
        if rows % t == 0:
            best = t
    assert rows % best == 0, (rows, mult)
    return best


def _swap_halves_d2d(g, *, after=None, name):
    blocks = list(g) if isinstance(g, (list, tuple)) else [g]
    rh, cols = blocks[0].shape[-2:]
    extra = [] if after is None else [after]

    def body(*refs):
        g_refs = refs[:len(blocks)]
        o_ref, send_sems, recv_sems = refs[len(blocks) + len(extra):]
        x, y, c = _mesh_pos()
        sibling = (x, y, 1 - c)

        def half(s, k):
            return g_refs[s].at[k] if len(g_refs) > 1 else g_refs[0].at[s, k]

        sends = []
        for s in range(N_CHIPS):
            cp = _remote(half(s, 1 - c), o_ref.at[s], send_sems, recv_sems, s, sibling)
            cp.start()
            sends.append(cp)
        for s in range(N_CHIPS):
            _remote(half(s, c), o_ref.at[s], send_sems, recv_sems, s, sibling).wait_recv()
        for cp in sends:
            cp.wait_send()

    return pl.pallas_call(
        body, name=name, in_specs=[HBM] * (len(blocks) + len(extra)), out_specs=HBM,
        out_shape=jax.ShapeDtypeStruct((N_CHIPS, rh, cols), blocks[0].dtype), scratch_shapes=_dma_sems(N_CHIPS),
    )(*blocks, *extra)


def _add_own_half(g, arrived, core, *, name):
    blocks = list(g) if isinstance(g, (list, tuple)) else [g]
    dtype = blocks[0].dtype
    rh, cols = blocks[0].shape[-2:]
    mult = 16 if dtype == BF16 else 8
    tr = _row_tile(rh, mult, max(mult, (512 * 1024) // cols))

    def body(core_ref, *refs):
        g_refs, a_ref, o_ref = refs[:-2], refs[-2], refs[-1]
        if len(g_refs) == 1:
            o_ref[...] = (g_refs[0][0].astype(F32) + a_ref[...].astype(F32)).astype(o_ref.dtype)
            return
        for k, g_ref in enumerate(g_refs):
            @pl.when(pl.program_id(0) == k)
            def _(g_ref=g_ref):
                o_ref[...] = (g_ref[...].astype(F32) + a_ref[...].astype(F32)).astype(o_ref.dtype)

    if len(blocks) == 1:
        g_specs = [pl.BlockSpec((1, 1, tr, cols), lambda s, i, core_ref: (s, core_ref[0], i, 0))]
    else:
        g_specs = [pl.BlockSpec((1, tr, cols), lambda s, i, core_ref, k=k: (core_ref[0], jnp.where(s == k, i, 0), 0))
                   for k in range(N_CHIPS)]
    grid_spec = pltpu.PrefetchScalarGridSpec(
        num_scalar_prefetch=1, grid=(N_CHIPS, rh // tr),
        in_specs=g_specs + [pl.BlockSpec((1, tr, cols), lambda s, i, core_ref: (s, i, 0))],
        out_specs=pl.BlockSpec((1, tr, cols), lambda s, i, core_ref: (s, i, 0)))
    return pl.pallas_call(
        body, name=name, grid_spec=grid_spec, out_shape=jax.ShapeDtypeStruct((N_CHIPS, rh, cols), dtype),
        compiler_params=_params(("parallel", "parallel")),
    )(core, *blocks, arrived)


def _scatter_ici(h, *, after=None, name):
    extra = [] if after is None else [after]

    def body(h_ref, *rest):
        o_ref, send_sems, recv_sems = rest[len(extra):]
        x, y, c = _mesh_pos()
        mine = 2 * x + y
        sends = []
        for k, (px, py) in enumerate(_other_chips(x, y)):
            cp = _remote(h_ref.at[2 * px + py], o_ref.at[mine], send_sems, recv_sems, k, (px, py, c))
            cp.start()
            sends.append(cp)
        for k, (px, py) in enumerate(_other_chips(x, y)):
            _remote(h_ref.at[mine], o_ref.at[2 * px + py], send_sems, recv_sems, k, (px, py, c)).wait_recv()
        for cp in sends:
            cp.wait_send()

    others = pl.pallas_call(
        body, name=name, in_specs=[HBM] * (1 + len(extra)), out_specs=HBM,
        out_shape=jax.ShapeDtypeStruct(h.shape, h.dtype), scratch_shapes=_dma_sems(3),
    )(h, *extra)
    chip = 2 * lax.axis_index("x") + lax.axis_index("y")
    own = lax.dynamic_slice_in_dim(h, chip, 1, axis=0)
    return lax.dynamic_update_slice(others, own, (chip, 0, 0))


def _sum_chips(parts, *, name):
    _, rh, cols = parts.shape
    mult = 16 if parts.dtype == BF16 else 8
    tr = _row_tile(rh, mult, max(mult, (512 * 1024) // cols))

    def body(p_ref, o_ref):
        acc = p_ref[0].astype(F32)
        for s in range(1, N_CHIPS):
            acc = acc + p_ref[s].astype(F32)
        o_ref[...] = acc

    return pl.pallas_call(
        body, name=name, grid=(rh // tr,),
        in_specs=[pl.BlockSpec((N_CHIPS, tr, cols), lambda i: (0, i, 0))],
        out_specs=pl.BlockSpec((tr, cols), lambda i: (i, 0)),
        out_shape=jax.ShapeDtypeStruct((rh, cols), F32), compiler_params=_params(("parallel",)),
    )(parts)


def _sum_chips_with_own(landed, sent, chip, *, name):
    _, rh, cols = landed.shape
    mult = 16 if landed.dtype == BF16 else 8
    tr = _row_tile(rh, mult, max(mult, (512 * 1024) // cols))

    def body(chip_ref, own_ref, px_ref, py_ref, pxy_ref, o_ref):
        acc = own_ref[0].astype(F32)
        for p_ref in (px_ref, py_ref, pxy_ref):
            acc = acc + p_ref[0].astype(F32)
        o_ref[...] = acc

    def block_of(flip):
        return pl.BlockSpec((1, tr, cols), lambda i, chip_ref: (chip_ref[0] ^ flip, i, 0))

    grid_spec = pltpu.PrefetchScalarGridSpec(
        num_scalar_prefetch=1, grid=(rh // tr,),
        in_specs=[block_of(0), block_of(2), block_of(1), block_of(3)],
        out_specs=pl.BlockSpec((tr, cols), lambda i, chip_ref: (i, 0)))
    return pl.pallas_call(
        body, name=name, grid_spec=grid_spec, out_shape=jax.ShapeDtypeStruct((rh, cols), F32),
        compiler_params=_params(("parallel",)),
    )(chip, sent, landed, landed, landed)


def _share_d2d(f, *, name):
    fs = f if isinstance(f, (list, tuple)) else [f]
    others = _swap_with_sibling(fs, name=name)
    first = lax.axis_index("c") == 0
    both = [jnp.stack([jnp.where(first, a, b), jnp.where(first, b, a)]) for a, b in zip(fs, others)]
    return both if isinstance(f, (list, tuple)) else both[0]


def _swap_with_sibling(fs, *, name):
    n = len(fs)

    def body(*refs):
        x, y, c = _mesh_pos()
        sibling = (x, y, 1 - c)
        send_sems, recv_sems = refs[2 * n:]
        copies = [_remote(refs[i], refs[n + i], send_sems, recv_sems, i, sibling) for i in range(n)]
        for cp in copies:
            cp.start()
        for cp in copies:
            cp.wait()

    return pl.pallas_call(
        body, name=name, in_specs=[HBM] * n, out_specs=[HBM] * n,
        out_shape=[jax.ShapeDtypeStruct(a.shape, a.dtype) for a in fs], scratch_shapes=_dma_sems(n),
    )(*fs)


def _reduce_scatter_chips(g, core, name, after=None, after_swap=None):
    _, rows, cols = g.shape
    g = g.reshape(N_CHIPS, 2, rows // 2, cols)
    arrived = _swap_halves_d2d(g, after=after, name=name + "_swap")
    started = after_swap(arrived) if after_swap is not None else None
    chip_sum = _add_own_half(g, arrived, core, name=name + "_add2")
    parts = _scatter_ici(chip_sum, after=started, name=name + "_ici")
    total = _sum_chips(parts, name=name + "_sum4")
    return _share_d2d(total, name=name + "_share").reshape(rows, cols)


BIG = ("w_in", "w_branch", "w_out", "w_up", "w_down")
BIG_COLUMN_SHARDED = ("w_in", "w_up")
CONV = ("conv_a_w", "conv_f_w")
REPLICATED = ("norm1_w", "b_gate", "conv_a_b", "dt_bias", "a_log", "d_skip", "ssd_norm_w", "uv_b", "v_ln_w",
              "v_ln_b", "w_spatial", "b_spatial", "norm2_w", "conv_f_b", "final_norm_w")
WEIGHT_ORDER = ("norm1_w", "w_in", "b_gate", "conv_a_w", "conv_a_b", "dt_bias", "a_log", "d_skip", "ssd_norm_w",
                "uv_b", "v_ln_w", "v_ln_b", "w_spatial", "b_spatial", "w_branch", "w_out", "norm2_w", "w_up",
                "conv_f_w", "conv_f_b", "w_down", "final_norm_w")
SMALL_EXCHANGE_ROWS = 64


_GATE0 = SSD_IN + 2 * SGU_WIDTH
IN_SEGMENTS = {
    "in_z": (0, SSD_D_INNER), "in_xbc": (SSD_D_INNER, SSD_D_INNER + SSD_XBC), "in_dt": (SSD_D_INNER + SSD_XBC, SSD_IN),
    "in_uv": (SSD_IN, _GATE0), "in_gate": (_GATE0, IN_COLS), "in_gate_a": (_GATE0, _GATE0 + D_MODEL),
    "in_gate_b": (_GATE0 + D_MODEL, IN_COLS),
}
IN_GRAD_SEGMENTS = ("in_z", "in_xbc", "in_dt", "in_uv", "in_gate_a", "in_gate_b")


def _take_columns(parts, start, stop):
    out = []
    for a, first in parts:
        lo, hi = max(start, first), min(stop, first + a.shape[1])
        if lo < hi:
            out.append(a[:, lo - first:hi - first])
    return out[0] if len(out) == 1 else jnp.concatenate(out, axis=1)


def _flat_rows(arrays, row_multiple):
    flat = jnp.concatenate([a.reshape(-1) for a in arrays])
    rows = -(-flat.shape[0] // (LANES * row_multiple)) * row_multiple
    return jnp.pad(flat, (0, rows * LANES - flat.shape[0])).reshape(rows, LANES)


def _unflatten(flat, shapes):
    flat = flat.reshape(-1)
    out, off = [], 0
    for shp in shapes:
        n = math.prod(shp)
        out.append(flat[off:off + n].reshape(shp))
        off += n
    return out


def _from_chip_blocks(blocks, name):
    if name in BIG_COLUMN_SHARDED or name in CONV:
        k = blocks.shape[1]
        return jnp.transpose(blocks, (1, 0, 2)).reshape(k, -1)
    return blocks.reshape(-1, blocks.shape[-1])


def _to_chip_blocks(whole, name):
    if name in BIG_COLUMN_SHARDED or name in CONV:
        k, n = whole.shape
        return jnp.transpose(whole.reshape(k, N_CHIPS, n // N_CHIPS), (1, 0, 2))
    return whole.reshape(N_CHIPS, whole.shape[0] // N_CHIPS, whole.shape[1])


def kernel(x, norm1_w, w_in, b_gate, conv_a_w, conv_a_b, dt_bias, a_log, d_skip, ssd_norm_w, uv_b, v_ln_w, v_ln_b, w_spatial, b_spatial, w_branch, w_out, norm2_w, w_up, conv_f_w, conv_f_b, w_down, final_norm_w, loss_target, m_norm1_w, m_w_in, m_b_gate, m_conv_a_w, m_conv_a_b, m_dt_bias, m_a_log, m_d_skip, m_ssd_norm_w, m_uv_b, m_v_ln_w, m_v_ln_b, m_w_spatial, m_b_spatial, m_w_branch, m_w_out, m_norm2_w, m_w_up, m_conv_f_w, m_conv_f_b, m_w_down, m_final_norm_w, v_norm1_w, v_w_in, v_b_gate, v_conv_a_w, v_conv_a_b, v_dt_bias, v_a_log, v_d_skip, v_ssd_norm_w, v_uv_b, v_v_ln_w, v_v_ln_b, v_w_spatial, v_b_spatial, v_w_branch, v_w_out, v_norm2_w, v_w_up, v_conv_f_w, v_conv_f_b, v_w_down, v_final_norm_w):
    weights = dict(norm1_w=norm1_w, w_in=w_in, b_gate=b_gate, conv_a_w=conv_a_w, conv_a_b=conv_a_b, dt_bias=dt_bias,
                   a_log=a_log, d_skip=d_skip, ssd_norm_w=ssd_norm_w, uv_b=uv_b, v_ln_w=v_ln_w, v_ln_b=v_ln_b,
                   w_spatial=w_spatial, b_spatial=b_spatial, w_branch=w_branch, w_out=w_out, norm2_w=norm2_w,
                   w_up=w_up, conv_f_w=conv_f_w, conv_f_b=conv_f_b, w_down=w_down, final_norm_w=final_norm_w)
    mom1 = dict(norm1_w=m_norm1_w, w_in=m_w_in, b_gate=m_b_gate, conv_a_w=m_conv_a_w, conv_a_b=m_conv_a_b,
                dt_bias=m_dt_bias, a_log=m_a_log, d_skip=m_d_skip, ssd_norm_w=m_ssd_norm_w, uv_b=m_uv_b,
                v_ln_w=m_v_ln_w, v_ln_b=m_v_ln_b, w_spatial=m_w_spatial, b_spatial=m_b_spatial, w_branch=m_w_branch,
                w_out=m_w_out, norm2_w=m_norm2_w, w_up=m_w_up, conv_f_w=m_conv_f_w, conv_f_b=m_conv_f_b,
                w_down=m_w_down, final_norm_w=m_final_norm_w)
    mom2 = dict(norm1_w=v_norm1_w, w_in=v_w_in, b_gate=v_b_gate, conv_a_w=v_conv_a_w, conv_a_b=v_conv_a_b,
                dt_bias=v_dt_bias, a_log=v_a_log, d_skip=v_d_skip, ssd_norm_w=v_ssd_norm_w, uv_b=v_uv_b,
                v_ln_w=v_v_ln_w, v_ln_b=v_v_ln_b, w_spatial=v_w_spatial, b_spatial=v_b_spatial, w_branch=v_w_branch,
                w_out=v_w_out, norm2_w=v_norm2_w, w_up=v_w_up, conv_f_w=v_conv_f_w, conv_f_b=v_conv_f_b,
                w_down=v_w_down, final_norm_w=v_final_norm_w)
    chip = 2 * lax.axis_index("x") + lax.axis_index("y")
    core = lax.axis_index("c").astype(jnp.int32).reshape(1)

    whole = {}
    conv_shapes = [weights[n].shape[1:] for n in CONV]
    conv_gathered = _all_gather_chips(_flat_rows([weights[n] for n in CONV], 16), "gather_conv").reshape(N_CHIPS, -1)
    off = 0
    for n, shp in zip(CONV, conv_shapes):
        size = math.prod(shp)
        whole[n] = _from_chip_blocks(conv_gathered[:, off:off + size].reshape((N_CHIPS,) + shp), n)
        off += size
    shard_shapes = {n: weights[n].shape[1:] for n in BIG}
    halves = [weights[n][0].astype(BF16).reshape(2, shard_shapes[n][0] // 2, shard_shapes[n][1]) for n in BIG]
    sends = [_gather_sends if n == "w_in" else _gather_whole_sends for n in BIG]
    gathers, gathers_started = _exchange_start(halves, [(N_CHIPS,) + h.shape for h in halves], sends,
                                               after=conv_gathered, name="gather_start")
    gathers = dict(zip(BIG, gathers))

    def get_weight(name, after):
        rows, cols = shard_shapes[name]
        if name == "w_in":
            own, landed = _exchange_wait(gathers[name], after, _gather_sends, _gather_arrivals,
                                         name="gather_" + name + "_wait")
            landed = _gather_d2d(landed, name="gather_" + name + "_d2d")
        else:
            own, landed = _exchange_wait(gathers[name], after, _gather_whole_sends, _gather_whole_arrivals,
                                         name="gather_" + name + "_wait")
        blocks = lax.dynamic_update_slice(landed.reshape(N_CHIPS, rows, cols), own.reshape(1, rows, cols),
                                          (chip, 0, 0))
        if name == "w_up":
            return {"up": blocks}
        if name == "w_in":
            parts = [(blocks[k], cols * k) for k in range(N_CHIPS)]
            segs = {n: _take_columns(parts, a, b) for n, (a, b) in IN_SEGMENTS.items()}
            segs["in_dt"] = jnp.pad(segs["in_dt"], ((0, 0), (0, LANES - SSD_HEADS)))
            return segs
        full = _from_chip_blocks(blocks, name)
        if name == "w_branch":
            return {"branch_a": full[:SSD_D_INNER], "branch_b": full[SSD_D_INNER:]}
        return {name[2:]: full}

    small = {n: weights[n] for n in REPLICATED}
    small["conv_a_w"] = whole["conv_a_w"]
    small["conv_f_w"] = whole["conv_f_w"]
    small["gathers_started"] = gathers_started

    reductions = {}

    def emit_grad(name, g):
        if name == "w_in":
            parts = [(g[n], IN_SEGMENTS[n][0]) for n in IN_GRAD_SEGMENTS]
            rows, cols = shard_shapes[name]
            g_halves = [_take_columns(parts, cols * k, cols * (k + 1)).reshape(2, rows // 2, cols)
                        for k in range(N_CHIPS)]
            arrived = _swap_halves_d2d(g_halves, name="reduce_" + name + "_swap")
            g_blocks = _add_own_half(g_halves, arrived, core, name="reduce_" + name + "_add2")
        else:
            g_blocks = g if name == "w_up" else _to_chip_blocks(g, name)
        (pending,), started = _exchange_start([g_blocks], [g_blocks.shape], _scatter_sends,
                                              name="reduce_" + name + "_start")
        reductions[name] = pending
        return started

    loss, dx, grads_small = _local_step(x[0], loss_target[0], get_weight, small, emit_grad)

    order = ("w_down", "w_up", "w_out", "w_branch", "w_in")
    core_sums = []
    chip_index = chip.astype(jnp.int32).reshape(1)
    for n in order:
        sent, landed = _exchange_wait(reductions[n], dx, _scatter_sends, _scatter_arrivals,
                                      name="reduce_" + n + "_wait")
        core_sums.append(_sum_chips_with_own(landed, sent, chip_index, name="reduce_" + n + "_sum4"))
    grads = {}
    swaps = []

    def start_sum_swap(small_swapped):
        pending, started = _exchange_start(core_sums, [a.shape for a in core_sums], _sibling_sends,
                                           after=small_swapped, name="reduce_swap_start")
        swaps.extend(pending)
        return started

    small_names = REPLICATED + CONV + ("loss",)
    grads_small = dict(grads_small, loss=loss)
    small_shapes = [grads_small[n].shape for n in small_names]
    g_small = _flat_rows([grads_small[n] for n in small_names], N_CHIPS * 2 * SMALL_EXCHANGE_ROWS)
    red_small = _reduce_scatter_chips(g_small.reshape(N_CHIPS, -1, LANES), core, "reduce_small", after=core_sums[-1],
                                      after_swap=start_sum_swap)
    all_small = _all_gather_chips(red_small, "gather_small")
    swapped = _exchange_wait_many(swaps, all_small, _sibling_sends, _sibling_sends, name="reduce_swap_wait")
    core_sums = {n: own for n, (own, _) in zip(order, swapped)}
    sibling_sums = {n: other for n, (_, other) in zip(order, swapped)}
    first = lax.axis_index("c") == 0
    w_in_halves = (core_sums["w_in"], sibling_sums["w_in"])
    w_in_grad = jnp.concatenate([jnp.where(first, w_in_halves[0], w_in_halves[1]),
                                 jnp.where(first, w_in_halves[1], w_in_halves[0])], axis=0)
    for n, g in zip(small_names, _unflatten(all_small, small_shapes)):
        if n == "loss":
            total_loss = g[0, 0]
            continue
        if n in CONV:
            width = g.shape[1] // N_CHIPS
            g = lax.dynamic_slice_in_dim(g, chip * width, width, axis=1)
        grads[n] = g.reshape(weights[n].shape[1:]) if n != "final_norm_w" else g

    delta, new_m, new_v = {}, {}, {}
    for n in BIG:
        shp = weights[n].shape
        if n == "w_in":
            g_t = w_in_grad.T
            results = [g_t] + list(_adamw(weights[n][0].T, g_t, mom1[n][0].T, mom2[n][0].T, name="adamw_" + n,
                                          tr=_row_tile(g_t.shape[0], 8, 136)))
            results = [a.T for a in results]
        else:
            results = _adamw_two_sums(weights[n][0], core_sums[n], sibling_sums[n], mom1[n][0], mom2[n][0],
                                      name="adamw_" + n, tr=_row_tile(shp[1], 8, 352))
        grads[n], delta[n], new_m[n], new_v[n] = [a.reshape(shp) for a in results]
    small_all = [n for n in WEIGHT_ORDER if n not in BIG]

    def as_2d(a):
        return a.reshape(-1, a.shape[-1])

    results = _adamw_many(*[[as_2d(src[n]) for n in small_all] for src in (weights, grads, mom1, mom2)],
                          name="adamw_small")
    for n, dv, mv, vv in zip(small_all, *results):
        shp = weights[n].shape
        delta[n], new_m[n], new_v[n] = dv.reshape(shp), mv.reshape(shp), vv.reshape(shp)

    grad_out = [grads[n].reshape(weights[n].shape) for n in WEIGHT_ORDER]
    return (total_loss, dx[None], *grad_out, *[delta[n] for n in WEIGHT_ORDER], *[new_m[n] for n in WEIGHT_ORDER],
            *[new_v[n] for n in WEIGHT_ORDER])
```

```python
import functools
import math

import jax
import jax.numpy as jnp
from jax import lax
from jax.experimental import pallas as pl
from jax.experimental.pallas import tpu as pltpu

F32 = jnp.float32
BF16 = jnp.bfloat16

D_MODEL = 1024
SSD_D_INNER = 2048
SSD_HEADS = 32
SSD_HEAD_DIM = 64
SSD_GROUPS = 4
SSD_HEADS_PER_GROUP = 8
SSD_STATE = 128
SSD_BC = 512
SSD_XBC = 3072
SSD_IN = 5152
SGU_WIDTH = 1024
SGU_GROUPS = 8
CHUNK = 128
IN_COLS = 9248
D_FF = 2816
NORM_EPS = 1e-6
LN_EPS = 1e-5
GROUP_COLS = SSD_HEADS_PER_GROUP * SSD_HEAD_DIM
LANES = 128

ADAM_LR = 0.001
ADAM_B1 = 0.9
ADAM_B2 = 0.999
ADAM_EPS = 1e-08
ADAM_WD = 0.01
ADAM_STEP = 10

N_CHIPS = 4
VMEM_LIMIT = 56 * 1024 * 1024

NT = (((1,), (1,)), ((), ()))
TN = (((0,), (0,)), ((), ()))
NN = (((1,), (0,)), ((), ()))


def _params(dims):
    return pltpu.CompilerParams(dimension_semantics=dims, vmem_limit_bytes=VMEM_LIMIT)


def _dot(a, b, dn=NN, precision=None):
    return lax.dot_general(a, b, dn, precision=precision, preferred_element_type=F32)


def _split3(x):
    hi = x.astype(BF16)
    rest = x - hi.astype(F32)
    mid = rest.astype(BF16)
    return hi, mid, (rest - mid.astype(F32)).astype(BF16)


def _dot_terms(terms, exact, dn=NN):
    out = None
    for t in terms:
        p = _dot(t, exact, dn)
        out = p if out is None else out + p
    return out


def _dot_exact_lhs(exact, terms):
    out = None
    for t in terms:
        p = _dot(exact, t)
        out = p if out is None else out + p
    return out


def _sigmoid(x):
    return 1.0 / (1.0 + jnp.exp(-x))


def _softplus(x):
    return jnp.maximum(x, 0.0) + jnp.log(1.0 + jnp.exp(-jnp.abs(x)))


def _matmul(pairs, *, trans_b=False, add=None, after=None, out_dtype=F32, tm=512, tn=512, name):
    def mat_shape(b):
        if isinstance(b, tuple) and b[1] == "cols":
            return (b[0].shape[1], b[0].shape[0] * b[0].shape[2])
        return b[0].shape[1:] if isinstance(b, tuple) else b.shape

    if isinstance(pairs[0][1], tuple) and pairs[0][1][1] == "cols":
        assert not trans_b and tn % LANES == 0 and pairs[0][1][0].shape[2] % tn == 0, name

    m = (pairs[0][0][0] if isinstance(pairs[0][0], tuple) else pairs[0][0]).shape[0]
    n = mat_shape(pairs[0][1])[0] if trans_b else mat_shape(pairs[0][1])[1]
    tm, tn = min(tm, m), min(tn, n)
    assert m % tm == 0 and n % tn == 0, (name, m, n, tm, tn)
    npairs = len(pairs)
    dn = NT if trans_b else NN

    def body(*refs):
        o_ref = refs[-1]
        acc = None
        for i in range(npairs):
            p = _dot(refs[2 * i][...].astype(BF16), refs[2 * i + 1][...].astype(BF16), dn)
            acc = p if acc is None else acc + p
        if add is not None:
            acc = acc + refs[2 * npairs][...]
        o_ref[...] = acc.astype(out_dtype)

    in_specs, args = [], []
    for a, b in pairs:
        bshape = mat_shape(b)
        k = bshape[1] if trans_b else bshape[0]
        assert bshape == ((n, k) if trans_b else (k, n)), (name, bshape)
        a, qa = a if isinstance(a, tuple) else (a, 0)
        assert a.shape[0] == m and a.shape[1] % k == 0, (name, a.shape, k)
        in_specs.append(pl.BlockSpec((tm, k), lambda i, j, qa=qa: (i, qa)))
        if isinstance(b, tuple) and b[1] == "cols":
            b = b[0]
            per = b.shape[2] // tn
            in_specs.append(pl.BlockSpec((None, k, tn), lambda i, j, per=per: (j // per, 0, j % per)))
        elif isinstance(b, tuple):
            b, qb = b
            if trans_b:
                in_specs.append(pl.BlockSpec((None, tn, k), lambda i, j, qb=qb: (qb, j, 0)))
            else:
                in_specs.append(pl.BlockSpec((None, k, tn), lambda i, j, qb=qb: (qb, 0, j)))
        elif trans_b:
            in_specs.append(pl.BlockSpec((tn, k), lambda i, j: (j, 0)))
        else:
            in_specs.append(pl.BlockSpec((k, tn), lambda i, j: (0, j)))
        args += [a, b]
    if add is not None:
        in_specs.append(pl.BlockSpec((tm, tn), lambda i, j: (i, j)))
        args.append(add)
    if after is not None:
        in_specs.append(pl.BlockSpec(memory_space=pl.ANY))
        args.append(after)
    return pl.pallas_call(
        body, name=name, grid=(m // tm, n // tn), in_specs=in_specs,
        out_specs=pl.BlockSpec((tm, tn), lambda i, j: (i, j)),
        out_shape=jax.ShapeDtypeStruct((m, n), out_dtype),
        compiler_params=_params(("parallel", "parallel")),
    )(*args)


def _matmul_tn(a, b, *, tk, tn, tm=1024, out_dtype=BF16, stack_out=False, after=None, part_of=None, name):
    m, k = a.shape
    n = b.shape[1]
    tm, tk, tn = min(tm, m), min(tk, k), min(tn, n)
    assert m % tm == 0 and k % tk == 0 and n % tn == 0, (name, m, k, n)
    nm = m // tm
    blocks, first, buffer = part_of if part_of is not None else (None, 0, None)
    if stack_out:
        out_spec = pl.BlockSpec((None, tk, tn), lambda i, j, l: (j + first, i, 0))
        out_shape = jax.ShapeDtypeStruct((blocks or n // tn, k, tn), out_dtype)
    else:
        out_spec = pl.BlockSpec((tk, tn), lambda i, j, l: (i + first, j))
        out_shape = jax.ShapeDtypeStruct((blocks * tk if blocks else k, n), out_dtype)

    def body(a_ref, b_ref, *rest):
        o_ref, acc = rest[-2:]
        mi = pl.program_id(2)

        @pl.when(mi == 0)
        def _():
            acc[...] = jnp.zeros_like(acc)

        acc[...] += _dot(a_ref[...].astype(BF16), b_ref[...].astype(BF16), TN)

        @pl.when(mi == nm - 1)
        def _():
            o_ref[...] = acc[...].astype(out_dtype)

    in_specs = [pl.BlockSpec((tm, tk), lambda i, j, l: (l, i)), pl.BlockSpec((tm, tn), lambda i, j, l: (l, j))]
    args = [a, b]
    if after is not None:
        in_specs.append(pl.BlockSpec(memory_space=pl.ANY))
        args.append(after)
    aliases = {}
    if buffer is not None:
        aliases = {len(args): 0}
        in_specs.append(pl.BlockSpec(memory_space=pl.ANY))
        args.append(buffer)
    return pl.pallas_call(
        body, name=name, grid=(k // tk, n // tn, nm), in_specs=in_specs,
        out_specs=out_spec, out_shape=out_shape, input_output_aliases=aliases,
        scratch_shapes=[pltpu.VMEM((tk, tn), F32)],
        compiler_params=_params(("parallel", "parallel", "arbitrary")),
    )(*args)


def _rms_fwd(x, w, *, after=None, name, tm=512):
    s, d = x.shape
    tm = min(tm, s)
    extra = [] if after is None else [after]

    def body(x_ref, w_ref, *rest):
        o_ref = rest[-1]
        xv = x_ref[...]
        r = lax.rsqrt(jnp.mean(xv * xv, axis=-1, keepdims=True) + NORM_EPS)
        o_ref[...] = (xv * r * w_ref[...]).astype(BF16)

    return pl.pallas_call(
        body, name=name, grid=(s // tm,),
        in_specs=[pl.BlockSpec((tm, d), lambda i: (i, 0)), pl.BlockSpec((1, d), lambda i: (0, 0))]
        + [pl.BlockSpec(memory_space=pl.ANY)] * len(extra),
        out_specs=pl.BlockSpec((tm, d), lambda i: (i, 0)),
        out_shape=jax.ShapeDtypeStruct((s, d), BF16),
        compiler_params=_params(("parallel",)),
    )(x, w, *extra)


def _rms_bwd(x, w, dn, dres, *, bf16_copy, name, tm=512):
    s, d = x.shape
    tm = min(tm, s)

    def body(x_ref, w_ref, dn_ref, dres_ref, dx_ref, *rest):
        dw_ref = rest[-1]

        @pl.when(pl.program_id(0) == 0)
        def _():
            dw_ref[...] = jnp.zeros_like(dw_ref)

        xv = x_ref[...]
        r = lax.rsqrt(jnp.mean(xv * xv, axis=-1, keepdims=True) + NORM_EPS)
        xhat = xv * r
        dnv = dn_ref[...].astype(F32)
        dxhat = dnv * w_ref[...]
        dx = dres_ref[...] + r * (dxhat - xhat * jnp.mean(dxhat * xhat, axis=-1, keepdims=True))
        dx_ref[...] = dx
        if bf16_copy:
            rest[0][...] = dx.astype(BF16)
        dw_ref[...] += jnp.sum(dnv * xhat, axis=0, keepdims=True)

    tile = pl.BlockSpec((tm, d), lambda i: (i, 0))
    row = pl.BlockSpec((1, d), lambda i: (0, 0))
    copies = [jax.ShapeDtypeStruct((s, d), BF16)] if bf16_copy else []
    return pl.pallas_call(
        body, name=name, grid=(s // tm,),
        in_specs=[tile, row, tile, tile], out_specs=[tile] + [tile] * len(copies) + [row],
        out_shape=[jax.ShapeDtypeStruct((s, d), F32)] + copies + [jax.ShapeDtypeStruct((1, d), F32)],
        compiler_params=_params(("arbitrary",)),
    )(x, w, dn, dres)


def _final_fwd_bwd(h2, wf, target, *, name, tm=512):
    s, d = h2.shape
    tm = min(tm, s)

    def body(h_ref, w_ref, t_ref, loss_ref, dh_ref, dhb_ref, dw_ref):
        @pl.when(pl.program_id(0) == 0)
        def _():
            dw_ref[...] = jnp.zeros_like(dw_ref)
            loss_ref[...] = jnp.zeros_like(loss_ref)

        hv = h_ref[...]
        r = lax.rsqrt(jnp.mean(hv * hv, axis=-1, keepdims=True) + NORM_EPS)
        xhat = hv * r
        err = xhat * w_ref[...] - t_ref[...]
        per_tok = jnp.mean(err * err, axis=-1, keepdims=True)
        loss_ref[...] += 0.5 * jnp.sum(per_tok, axis=0, keepdims=True)
        dy = err * (1.0 / d)
        dxhat = dy * w_ref[...]
        dh = r * (dxhat - xhat * jnp.mean(dxhat * xhat, axis=-1, keepdims=True))
        dh_ref[...] = dh
        dhb_ref[...] = dh.astype(BF16)
        dw_ref[...] += jnp.sum(dy * xhat, axis=0, keepdims=True)

    tile = pl.BlockSpec((tm, d), lambda i: (i, 0))
    row = pl.BlockSpec((1, d), lambda i: (0, 0))
    return pl.pallas_call(
        body, name=name, grid=(s // tm,),
        in_specs=[tile, row, tile],
        out_specs=[pl.BlockSpec((1, 1), lambda i: (0, 0)), tile, tile, row],
        out_shape=[jax.ShapeDtypeStruct((1, 1), F32), jax.ShapeDtypeStruct((s, d), F32),
                   jax.ShapeDtypeStruct((s, d), BF16), jax.ShapeDtypeStruct((1, d), F32)],
        compiler_params=_params(("arbitrary",)),
    )(h2, wf, target)


CONV_ROWS = 256
CONV_ROWS_FWD = 512
HALO = 8


def _rows_with_halo(ref, r0, rows, s, before, after):
    tile = 16 if ref.dtype == BF16 else HALO
    parts = []
    if before:
        prev = ref[pl.ds(pl.multiple_of(jnp.maximum(r0 - tile, 0), tile), tile), :].astype(F32)[tile - HALO:]
        parts.append(jnp.where(r0 > 0, prev, 0.0))
    parts.append(ref[pl.ds(r0, rows), :].astype(F32))
    if after:
        nxt = ref[pl.ds(pl.multiple_of(jnp.minimum(r0 + rows, s - tile), tile), tile), :].astype(F32)[:HALO]
        parts.append(jnp.where(r0 + rows < s, nxt, 0.0))
    return jnp.concatenate(parts, axis=0) if len(parts) > 1 else parts[0]


def _window(x_ref, r0, s, after):
    return _rows_with_halo(x_ref, r0, CONV_ROWS_FWD, s, True, after).astype(F32)


def _shifted(window, k, rows):
    if k == 0:
        return window[HALO:HALO + rows]
    return pltpu.roll(window, k, 0)[HALO:HALO + rows]


def _conv_taps(window, w_ref, kk, rows):
    acc = None
    for i in range(kk):
        term = w_ref[i:i + 1, :] * _shifted(window, kk - 1 - i, rows)
        acc = term if acc is None else acc + term
    return acc


def _row_loop(rows, step):
    def body(r, carry):
        return step(pl.multiple_of(r * rows, rows), carry)
    return body


def _conv_bwd_rows(x, dpe, w_ref, kk):
    dp = dpe[:CONV_ROWS]
    dx = None
    dws = []
    for i in range(kk):
        k = kk - 1 - i
        later = dp if k == 0 else pltpu.roll(dpe, dpe.shape[0] - k, 0)[:CONV_ROWS]
        dws.append(jnp.sum(later * x, axis=0, keepdims=True))
        term = w_ref[i:i + 1, :] * later
        dx = term if dx is None else dx + term
    return dx, dws, jnp.sum(dp, axis=0, keepdims=True)


def _conv_a_fwd(xraw, w, b, *, name, tc=128):
    s, c = xraw.shape
    kk = 4

    def body(x_ref, w_ref, b_ref, o_ref, pre_ref):
        def step(r0, carry):
            pre = _conv_taps(_window(x_ref, r0, s, False), w_ref, kk, CONV_ROWS_FWD) + b_ref[...]
            o_ref[pl.ds(r0, CONV_ROWS_FWD), :] = pre * _sigmoid(pre)
            pre_ref[pl.ds(r0, CONV_ROWS_FWD), :] = pre.astype(BF16)
            return carry

        lax.fori_loop(0, s // CONV_ROWS_FWD, _row_loop(CONV_ROWS_FWD, step), 0)

    col = pl.BlockSpec((s, tc), lambda j: (0, j))
    return pl.pallas_call(
        body, name=name, grid=(c // tc,),
        in_specs=[col, pl.BlockSpec((8, tc), lambda j: (0, j)), pl.BlockSpec((1, tc), lambda j: (0, j))],
        out_specs=[col, col], out_shape=[jax.ShapeDtypeStruct((s, c), F32), jax.ShapeDtypeStruct((s, c), BF16)],
        compiler_params=_params(("parallel",)),
    )(xraw, w, b)


def _conv_a_bwd(xraw, pre, w, dy, *, name, tc=128):
    s, c = xraw.shape
    kk = 4

    def body(x_ref, pre_ref, w_ref, dy_ref, dx_ref, dw_ref, db_ref):
        def step(r0, carry):
            pre = _rows_with_halo(pre_ref, r0, CONV_ROWS, s, False, True)
            sg = _sigmoid(pre)
            dpe = _rows_with_halo(dy_ref, r0, CONV_ROWS, s, False, True) * (sg * (1.0 + pre * (1.0 - sg)))
            dx, dws, db = _conv_bwd_rows(x_ref[pl.ds(r0, CONV_ROWS), :].astype(F32), dpe, w_ref, kk)
            dx_ref[pl.ds(r0, CONV_ROWS), :] = dx.astype(BF16)
            return tuple(acc + new for acc, new in zip(carry, dws + [db]))

        zero = jnp.zeros((1, tc), F32)
        sums = lax.fori_loop(0, s // CONV_ROWS, _row_loop(CONV_ROWS, step), (zero,) * (kk + 1))
        db_ref[...] = sums[kk]
        dw_ref[...] = jnp.concatenate(list(sums[:kk]) + [jnp.zeros((8 - kk, tc), F32)], axis=0)

    col = pl.BlockSpec((s, tc), lambda j: (0, j))
    w8 = pl.BlockSpec((8, tc), lambda j: (0, j))
    row = pl.BlockSpec((1, tc), lambda j: (0, j))
    return pl.pallas_call(
        body, name=name, grid=(c // tc,),
        in_specs=[col, col, w8, col], out_specs=[col, w8, row],
        out_shape=[jax.ShapeDtypeStruct((s, c), BF16), jax.ShapeDtypeStruct((8, c), F32),
                   jax.ShapeDtypeStruct((1, c), F32)],
        compiler_params=_params(("parallel",)),
    )(xraw, pre, w, dy)


def _conv_f_fwd(up_raw, w, b, *, name, tc=128):
    s, c2 = up_raw.shape
    c = c2 // 2
    nb = c // tc
    kk = 3

    def body(xa_ref, xv_ref, wa_ref, wv_ref, ba_ref, bv_ref, o_ref, a_out, v_out):
        def step(r0, carry):
            a = _conv_taps(_window(xa_ref, r0, s, False), wa_ref, kk, CONV_ROWS_FWD) + ba_ref[...]
            v = _conv_taps(_window(xv_ref, r0, s, False), wv_ref, kk, CONV_ROWS_FWD) + bv_ref[...]
            o_ref[pl.ds(r0, CONV_ROWS_FWD), :] = (a * _sigmoid(a) * v).astype(BF16)
            a_out[pl.ds(r0, CONV_ROWS_FWD), :] = a.astype(BF16)
            v_out[pl.ds(r0, CONV_ROWS_FWD), :] = v.astype(BF16)
            return carry

        lax.fori_loop(0, s // CONV_ROWS_FWD, _row_loop(CONV_ROWS_FWD, step), 0)

    col_a = pl.BlockSpec((s, tc), lambda j: (0, j))
    col_v = pl.BlockSpec((s, tc), lambda j: (0, j + nb))
    half = jax.ShapeDtypeStruct((s, c), BF16)
    return pl.pallas_call(
        body, name=name, grid=(nb,),
        in_specs=[col_a, col_v, pl.BlockSpec((8, tc), lambda j: (0, j)), pl.BlockSpec((8, tc), lambda j: (0, j + nb)),
                  pl.BlockSpec((1, tc), lambda j: (0, j)), pl.BlockSpec((1, tc), lambda j: (0, j + nb))],
        out_specs=[col_a, col_a, col_a], out_shape=[half, half, half],
        compiler_params=_params(("parallel",)),
    )(up_raw, up_raw, w, w, b, b)


def _conv_f_bwd(up_raw, a_pre, v_pre, w, dact, *, name, tc=128):
    s, c2 = up_raw.shape
    c = c2 // 2
    nb = c // tc
    kk = 3

    def body(xa_ref, xv_ref, a_ref, v_ref, wa_ref, wv_ref, d_ref,
             dxa_ref, dxv_ref, dwa_ref, dwv_ref, dba_ref, dbv_ref):
        def step(r0, carry):
            a = _rows_with_halo(a_ref, r0, CONV_ROWS, s, False, True)
            v = _rows_with_halo(v_ref, r0, CONV_ROWS, s, False, True)
            sg = _sigmoid(a)
            d = _rows_with_halo(d_ref, r0, CONV_ROWS, s, False, True)
            rows = pl.ds(r0, CONV_ROWS)
            dxa, dwas, dba = _conv_bwd_rows(xa_ref[rows, :].astype(F32), d * v * (sg * (1.0 + a * (1.0 - sg))),
                                            wa_ref, kk)
            dxv, dwvs, dbv = _conv_bwd_rows(xv_ref[rows, :].astype(F32), d * (a * sg), wv_ref, kk)
            dxa_ref[pl.ds(r0, CONV_ROWS), :] = dxa.astype(BF16)
            dxv_ref[pl.ds(r0, CONV_ROWS), :] = dxv.astype(BF16)
            return tuple(acc + new for acc, new in zip(carry, dwas + [dba] + dwvs + [dbv]))

        zero = jnp.zeros((1, tc), F32)
        sums = lax.fori_loop(0, s // CONV_ROWS, _row_loop(CONV_ROWS, step), (zero,) * (2 * kk + 2))
        pad = [jnp.zeros((8 - kk, tc), F32)]
        dwa_ref[...] = jnp.concatenate(list(sums[:kk]) + pad, axis=0)
        dba_ref[...] = sums[kk]
        dwv_ref[...] = jnp.concatenate(list(sums[kk + 1:2 * kk + 1]) + pad, axis=0)
        dbv_ref[...] = sums[2 * kk + 1]

    col_a = pl.BlockSpec((s, tc), lambda j: (0, j))
    col_v = pl.BlockSpec((s, tc), lambda j: (0, j + nb))
    w_a = pl.BlockSpec((8, tc), lambda j: (0, j))
    w_v = pl.BlockSpec((8, tc), lambda j: (0, j + nb))
    r_a = pl.BlockSpec((1, tc), lambda j: (0, j))
    r_v = pl.BlockSpec((1, tc), lambda j: (0, j + nb))
    outs = pl.pallas_call(
        body, name=name, grid=(nb,),
        in_specs=[col_a, col_v, col_a, col_a, w_a, w_v, col_a],
        out_specs=[col_a, col_a, w_a, w_a, r_a, r_a],
        out_shape=[jax.ShapeDtypeStruct((s, c), BF16), jax.ShapeDtypeStruct((s, c), BF16),
                   jax.ShapeDtypeStruct((8, c), F32), jax.ShapeDtypeStruct((8, c), F32),
                   jax.ShapeDtypeStruct((1, c), F32), jax.ShapeDtypeStruct((1, c), F32)],
        compiler_params=_params(("parallel",)),
    )(up_raw, up_raw, a_pre, v_pre, w, w, dact)
    return outs


def _tri_masks():
    row = lax.broadcasted_iota(jnp.int32, (CHUNK, CHUNK), 0)
    col = lax.broadcasted_iota(jnp.int32, (CHUNK, CHUNK), 1)
    return row >= col, row <= col


def _ssd_fwd(xbc, dt_raw, z, dt_bias, a_log, a_log_x, d_skip_x, norm_w, expand, *, name):
    s = xbc.shape[0]
    nc = s // CHUNK

    def body(xbc_ref, dtr_ref, z_ref, dtb_ref, alog_ref, alogx_ref, dskx_ref, nw_ref, e_ref,
             y_ref, ya_ref, st_ref, state):
        @pl.when(pl.program_id(0) == 0)
        def _():
            state[...] = jnp.zeros_like(state)

        st_ref[0] = state[...]
        lower, _ = _tri_masks()
        dt = _softplus(dtr_ref[...] + dtb_ref[...])
        adt = dt * (-jnp.exp(alog_ref[...]))
        acum = _dot_exact_lhs(lower.astype(BF16), _split3(adt))
        acum_t = acum.T
        dt_terms, acum_terms = _split3(dt), _split3(acum)
        for g in range(SSD_GROUPS):
            sl = slice(GROUP_COLS * g, GROUP_COLS * (g + 1))
            dt_x = _dot_terms(dt_terms[:2], e_ref[:, sl])
            acum_x = _dot_terms(acum_terms, e_ref[:, sl])
            tot_x = jnp.sum(dt_x * (-jnp.exp(alogx_ref[:, sl])), axis=0, keepdims=True)
            xs = xbc_ref[:, sl]
            xdt = xs * dt_x
            xdt_b = xdt.astype(BF16)
            bg = xbc_ref[:, SSD_D_INNER + SSD_STATE * g:SSD_D_INNER + SSD_STATE * (g + 1)].astype(BF16)
            cg = xbc_ref[:, SSD_D_INNER + SSD_BC + SSD_STATE * g:SSD_D_INNER + SSD_BC + SSD_STATE * (g + 1)].astype(BF16)
            cb = _dot(cg, bg, NT)
            st_g = state[:, sl]
            y_off = _dot(cg, st_g.astype(BF16)) * jnp.exp(acum_x)
            parts = []
            for r in range(SSD_HEADS_PER_GROUP):
                h = SSD_HEADS_PER_GROUP * g + r
                dec = jnp.exp(jnp.where(lower, acum[:, h:h + 1] - acum_t[h:h + 1, :], -jnp.inf))
                parts.append(_dot((cb * dec).astype(BF16), xdt_b[:, SSD_HEAD_DIM * r:SSD_HEAD_DIM * (r + 1)]))
            y_ref[:, sl] = jnp.concatenate(parts, axis=1) + y_off + dskx_ref[:, sl] * xs
            wgt = (xdt * jnp.exp(tot_x - acum_x)).astype(BF16)
            state[:, sl] = st_g * jnp.exp(tot_x) + _dot(bg, wgt, TN)
        zv = z_ref[...].astype(F32)
        q = y_ref[...] * (zv * _sigmoid(zv))
        r = lax.rsqrt(jnp.mean(q * q, axis=-1, keepdims=True) + NORM_EPS)
        ya_ref[...] = (q * r * nw_ref[...]).astype(BF16)

    def chunk(w):
        return pl.BlockSpec((CHUNK, w), lambda c: (c, 0))

    def const(shape):
        return pl.BlockSpec(shape, lambda c: (0,) * len(shape))

    return pl.pallas_call(
        body, name=name, grid=(nc,),
        in_specs=[chunk(SSD_XBC), chunk(LANES), chunk(SSD_D_INNER), const((1, LANES)), const((1, LANES)),
                  const((1, SSD_D_INNER)), const((1, SSD_D_INNER)), const((1, SSD_D_INNER)),
                  const((LANES, SSD_D_INNER))],
        out_specs=[chunk(SSD_D_INNER), chunk(SSD_D_INNER),
                   pl.BlockSpec((1, SSD_STATE, SSD_D_INNER), lambda c: (c, 0, 0))],
        out_shape=[jax.ShapeDtypeStruct((s, SSD_D_INNER), F32), jax.ShapeDtypeStruct((s, SSD_D_INNER), BF16),
                   jax.ShapeDtypeStruct((nc, SSD_STATE, SSD_D_INNER), F32)],
        scratch_shapes=[pltpu.VMEM((SSD_STATE, SSD_D_INNER), F32)],
        compiler_params=_params(("arbitrary",)),
    )(xbc, dt_raw, z, dt_bias, a_log, a_log_x, d_skip_x, norm_w, expand)


def _ssd_bwd(dya, y, z, xbc, dt_raw, states, dt_bias, a_log, a_log_x, d_skip_x, norm_w, expand, expand_t, *, name):
    s = xbc.shape[0]
    nc = s // CHUNK

    def body(dya_ref, y_ref, z_ref, xbc_ref, dtr_ref, stp_ref, dtb_ref, alog_ref, alogx_ref, dskx_ref, nw_ref,
             e_ref, et_ref, dz_ref, dxbc_ref, ddt_ref, dnw_ref, ddsk_ref, dalog_ref, ddtb_ref,
             dstate, dy_sc, dskcol):
        i = pl.program_id(0)

        @pl.when(i == 0)
        def _():
            dstate[...] = jnp.zeros_like(dstate)
            dskcol[...] = jnp.zeros_like(dskcol)
            dnw_ref[...] = jnp.zeros_like(dnw_ref)
            dalog_ref[...] = jnp.zeros_like(dalog_ref)
            ddtb_ref[...] = jnp.zeros_like(ddtb_ref)
            ddsk_ref[...] = jnp.zeros_like(ddsk_ref)

        lower, upper = _tri_masks()
        rows = lax.broadcasted_iota(jnp.int32, (CHUNK, LANES), 0)
        pre = dtr_ref[...] + dtb_ref[...]
        dt = _softplus(pre)
        a = -jnp.exp(alog_ref[...])
        acum = _dot_exact_lhs(lower.astype(BF16), _split3(dt * a))
        acum_t = acum.T
        dt_terms, acum_terms = _split3(dt), _split3(acum)

        yv = y_ref[...]
        zv = z_ref[...].astype(F32)
        sz = _sigmoid(zv)
        silu_z = zv * sz
        q = yv * silu_z
        r = lax.rsqrt(jnp.mean(q * q, axis=-1, keepdims=True) + NORM_EPS)
        qhat = q * r
        dyav = dya_ref[...]
        dqhat = dyav * nw_ref[...]
        dnw_ref[...] += jnp.sum(dyav * qhat, axis=0, keepdims=True)
        dq = r * (dqhat - qhat * jnp.mean(dqhat * qhat, axis=-1, keepdims=True))
        dy_sc[...] = dq * silu_z
        dz_ref[...] = (dq * yv * (sz * (1.0 + zv * (1.0 - sz)))).astype(BF16)

        da_cum = jnp.zeros((CHUNK, LANES), F32)
        ddt = jnp.zeros((CHUNK, LANES), F32)
        for g in range(SSD_GROUPS):
            sl = slice(GROUP_COLS * g, GROUP_COLS * (g + 1))
            et_g = et_ref[sl, :]
            dt_x = _dot_terms(dt_terms[:2], e_ref[:, sl])
            acum_x = _dot_terms(acum_terms, e_ref[:, sl])
            tot_x = jnp.sum(dt_x * (-jnp.exp(alogx_ref[:, sl])), axis=0, keepdims=True)
            e_tot = jnp.exp(tot_x)
            dec_s = jnp.exp(tot_x - acum_x)
            xs = xbc_ref[:, sl]
            xdt = xs * dt_x
            xdt_b = xdt.astype(BF16)
            dy = dy_sc[:, sl]
            dy_b = dy.astype(BF16)
            dskx = dskx_ref[:, sl]
            y_ssd = y_ref[:, sl] - dskx * xs
            dskcol[:, sl] += jnp.sum(dy * xs, axis=0, keepdims=True)
            bg = xbc_ref[:, SSD_D_INNER + SSD_STATE * g:SSD_D_INNER + SSD_STATE * (g + 1)].astype(BF16)
            cg = xbc_ref[:, SSD_D_INNER + SSD_BC + SSD_STATE * g:SSD_D_INNER + SSD_BC + SSD_STATE * (g + 1)].astype(BF16)
            cb_t = _dot(bg, cg, NT)
            sp = stp_ref[0, :, sl]
            ds_g = dstate[:, sl]
            ds_b = ds_g.astype(BF16)
            dye_b = (dy * jnp.exp(acum_x)).astype(BF16)
            dc = _dot(dye_b, sp.astype(BF16), NT)
            dxdt_state = dec_s * _dot(bg, ds_b)
            db = _dot((xdt * dec_s).astype(BF16), ds_b, NT)
            dcb_t = jnp.zeros((CHUNK, CHUNK), F32)
            parts = []
            for rr in range(SSD_HEADS_PER_GROUP):
                h = SSD_HEADS_PER_GROUP * g + rr
                hs = slice(SSD_HEAD_DIM * rr, SSD_HEAD_DIM * (rr + 1))
                dec_t = jnp.exp(jnp.where(upper, acum_t[h:h + 1, :] - acum[:, h:h + 1], -jnp.inf))
                parts.append(_dot((cb_t * dec_t).astype(BF16), dy_b[:, hs]))
                dcb_t = dcb_t + _dot(xdt_b[:, hs], dy_b[:, hs], NT) * dec_t
            dxdt = jnp.concatenate(parts, axis=1) + dxdt_state
            dcb_tb = dcb_t.astype(BF16)
            dc = dc + _dot(dcb_tb, bg, TN)
            db = db + _dot(dcb_tb, cg)
            tot_col = jnp.sum(ds_g * sp, axis=0, keepdims=True) * e_tot + jnp.sum(dxdt_state * xdt, axis=0, keepdims=True)
            d_tot = _dot_terms(_split3(jnp.broadcast_to(tot_col, (8, GROUP_COLS))), et_g)
            d_tot = jnp.max(d_tot, axis=0, keepdims=True)
            pair_sums = dy_b.astype(F32) * y_ssd - xdt_b.astype(F32) * dxdt
            da_cum = da_cum + _dot_terms(_split3(pair_sums), et_g) + jnp.where(rows == CHUNK - 1, d_tot, 0.0)
            ddt = ddt + _dot_terms(_split3(dxdt * xs)[:2], et_g)
            dxbc_ref[:, sl] = dy * dskx + dxdt * dt_x
            dxbc_ref[:, SSD_D_INNER + SSD_STATE * g:SSD_D_INNER + SSD_STATE * (g + 1)] = db
            dxbc_ref[:, SSD_D_INNER + SSD_BC + SSD_STATE * g:SSD_D_INNER + SSD_BC + SSD_STATE * (g + 1)] = dc
            dstate[:, sl] = e_tot * ds_g + _dot(cg, dye_b, TN)

        dadt = _dot_exact_lhs(upper.astype(BF16), _split3(da_cum))
        ddt = ddt + dadt * a
        dalog_ref[...] += jnp.sum(dadt * dt, axis=0, keepdims=True)
        dpre = ddt * _sigmoid(pre)
        ddtb_ref[...] += jnp.sum(dpre, axis=0, keepdims=True)
        ddt_ref[...] = dpre.astype(BF16)

        @pl.when(i == nc - 1)
        def _():
            dalog_ref[...] = dalog_ref[...] * a
            dsk = _dot_terms(_split3(jnp.broadcast_to(dskcol[...], (8, SSD_D_INNER))), et_ref[...])
            ddsk_ref[...] = jnp.max(dsk, axis=0, keepdims=True)

    def chunk(w):
        return pl.BlockSpec((CHUNK, w), lambda i: (nc - 1 - i, 0))

    def const(shape):
        return pl.BlockSpec(shape, lambda i: (0,) * len(shape))

    return pl.pallas_call(
        body, name=name, grid=(nc,),
        in_specs=[chunk(SSD_D_INNER), chunk(SSD_D_INNER), chunk(SSD_D_INNER), chunk(SSD_XBC), chunk(LANES),
                  pl.BlockSpec((1, SSD_STATE, SSD_D_INNER), lambda i: (nc - 1 - i, 0, 0)),
                  const((1, LANES)), const((1, LANES)), const((1, SSD_D_INNER)), const((1, SSD_D_INNER)),
                  const((1, SSD_D_INNER)), const((LANES, SSD_D_INNER)), const((SSD_D_INNER, LANES))],
        out_specs=[chunk(SSD_D_INNER), chunk(SSD_XBC), chunk(LANES), const((1, SSD_D_INNER)), const((1, LANES)),
                   const((1, LANES)), const((1, LANES))],
        out_shape=[jax.ShapeDtypeStruct((s, SSD_D_INNER), BF16), jax.ShapeDtypeStruct((s, SSD_XBC), F32),
                   jax.ShapeDtypeStruct((s, LANES), BF16), jax.ShapeDtypeStruct((1, SSD_D_INNER), F32),
                   jax.ShapeDtypeStruct((1, LANES), F32), jax.ShapeDtypeStruct((1, LANES), F32),
                   jax.ShapeDtypeStruct((1, LANES), F32)],
        scratch_shapes=[pltpu.VMEM((SSD_STATE, SSD_D_INNER), F32), pltpu.VMEM((CHUNK, SSD_D_INNER), F32),
                        pltpu.VMEM((1, SSD_D_INNER), F32)],
        compiler_params=_params(("arbitrary",)),
    )(dya, y, z, xbc, dt_raw, states, dt_bias, a_log, a_log_x, d_skip_x, norm_w, expand, expand_t)


GELU_K = math.sqrt(2.0 / math.pi)
GELU_C = 0.044715


def _gelu(x):
    return 0.5 * x * (1.0 + jnp.tanh(GELU_K * (x + GELU_C * x * x * x)))


def _gelu_grad(x):
    t = jnp.tanh(GELU_K * (x + GELU_C * x * x * x))
    return 0.5 * (1.0 + t) + 0.5 * x * (1.0 - t * t) * (GELU_K * (1.0 + 3.0 * GELU_C * x * x))


def _sgu_pre(uv_ref, uvb_ref, lnw_ref, lnb_ref):
    uv = uv_ref[...].astype(F32) + uvb_ref[...]
    guv = _gelu(uv)
    u = guv[:, :SGU_WIDTH]
    v = guv[:, SGU_WIDTH:]
    mu = jnp.mean(v, axis=-1, keepdims=True)
    vc = v - mu
    rstd = lax.rsqrt(jnp.mean(vc * vc, axis=-1, keepdims=True) + LN_EPS)
    vhat = vc * rstd
    vn = vhat * lnw_ref[...] + lnb_ref[...]
    return uv, u, vhat, rstd, vn


def _sgu_fwd(uv_raw, uv_b, ln_w, ln_b, w_sp, b_sp_t, *, name):
    s = uv_raw.shape[0]
    nc = s // CHUNK

    def body(uv_ref, uvb_ref, lnw_ref, lnb_ref, w_ref, bt_ref, o_ref):
        lower, _ = _tri_masks()
        _, u, _, _, vn = _sgu_pre(uv_ref, uvb_ref, lnw_ref, lnb_ref)
        vn_b = vn.astype(BF16)
        bt = bt_ref[...]
        for g in range(SGU_GROUPS):
            gs = slice(LANES * g, LANES * (g + 1))
            wc = jnp.where(lower, w_ref[g], 0.0).astype(BF16)
            mixed = _dot(wc, vn_b[:, gs]) + bt[:, g:g + 1]
            o_ref[:, gs] = (u[:, gs] * mixed).astype(BF16)

    def const(shape):
        return pl.BlockSpec(shape, lambda c: (0,) * len(shape))

    return pl.pallas_call(
        body, name=name, grid=(nc,),
        in_specs=[pl.BlockSpec((CHUNK, 2 * SGU_WIDTH), lambda c: (c, 0)), const((1, 2 * SGU_WIDTH)),
                  const((1, SGU_WIDTH)), const((1, SGU_WIDTH)), const((SGU_GROUPS, CHUNK, CHUNK)),
                  const((CHUNK, LANES))],
        out_specs=pl.BlockSpec((CHUNK, SGU_WIDTH), lambda c: (c, 0)),
        out_shape=jax.ShapeDtypeStruct((s, SGU_WIDTH), BF16),
        compiler_params=_params(("parallel",)),
    )(uv_raw, uv_b, ln_w, ln_b, w_sp, b_sp_t)


def _sgu_bwd(uv_raw, dyb, uv_b, ln_w, ln_b, w_sp, b_sp_t, group_sum, *, name):
    s = uv_raw.shape[0]
    nc = s // CHUNK

    def body(uv_ref, dy_ref, uvb_ref, lnw_ref, lnb_ref, w_ref, bt_ref, gsum_ref,
             duv_ref, dw_ref, dbt_ref, dlnw_ref, dlnb_ref, duvb_ref):
        @pl.when(pl.program_id(0) == 0)
        def _():
            dw_ref[...] = jnp.zeros_like(dw_ref)
            dbt_ref[...] = jnp.zeros_like(dbt_ref)
            dlnw_ref[...] = jnp.zeros_like(dlnw_ref)
            dlnb_ref[...] = jnp.zeros_like(dlnb_ref)
            duvb_ref[...] = jnp.zeros_like(duvb_ref)

        lower, _ = _tri_masks()
        uv, u, vhat, rstd, vn = _sgu_pre(uv_ref, uvb_ref, lnw_ref, lnb_ref)
        vn_b = vn.astype(BF16)
        bt = bt_ref[...]
        dy = dy_ref[...].astype(F32)
        du_parts, dvn_parts, dmix_parts = [], [], []
        for g in range(SGU_GROUPS):
            gs = slice(LANES * g, LANES * (g + 1))
            wc = jnp.where(lower, w_ref[g], 0.0).astype(BF16)
            mixed = _dot(wc, vn_b[:, gs]) + bt[:, g:g + 1]
            du_parts.append(dy[:, gs] * mixed)
            dmix = dy[:, gs] * u[:, gs]
            dmix_b = dmix.astype(BF16)
            dmix_parts.append(dmix)
            dw_ref[g] += jnp.where(lower, _dot(dmix_b, vn_b[:, gs], NT), 0.0)
            dvn_parts.append(_dot(wc, dmix_b, TN))
        dmixed = jnp.concatenate(dmix_parts, axis=1)
        dbt_ref[...] += _dot_terms(_split3(dmixed), gsum_ref[...])
        dvn = jnp.concatenate(dvn_parts, axis=1)
        dlnw_ref[...] += jnp.sum(dvn * vhat, axis=0, keepdims=True)
        dlnb_ref[...] += jnp.sum(dvn, axis=0, keepdims=True)
        dvhat = dvn * lnw_ref[...]
        dv = rstd * (dvhat - jnp.mean(dvhat, axis=-1, keepdims=True)
                     - vhat * jnp.mean(dvhat * vhat, axis=-1, keepdims=True))
        dguv = jnp.concatenate(du_parts + [dv], axis=1)
        duv = dguv * _gelu_grad(uv)
        duvb_ref[...] += jnp.sum(duv, axis=0, keepdims=True)
        duv_ref[...] = duv.astype(BF16)

    def const(shape):
        return pl.BlockSpec(shape, lambda c: (0,) * len(shape))

    return pl.pallas_call(
        body, name=name, grid=(nc,),
        in_specs=[pl.BlockSpec((CHUNK, 2 * SGU_WIDTH), lambda c: (c, 0)),
                  pl.BlockSpec((CHUNK, SGU_WIDTH), lambda c: (c, 0)), const((1, 2 * SGU_WIDTH)),
                  const((1, SGU_WIDTH)), const((1, SGU_WIDTH)), const((SGU_GROUPS, CHUNK, CHUNK)),
                  const((CHUNK, LANES)), const((SGU_WIDTH, LANES))],
        out_specs=[pl.BlockSpec((CHUNK, 2 * SGU_WIDTH), lambda c: (c, 0)), const((SGU_GROUPS, CHUNK, CHUNK)),
                   const((CHUNK, LANES)), const((1, SGU_WIDTH)), const((1, SGU_WIDTH)), const((1, 2 * SGU_WIDTH))],
        out_shape=[jax.ShapeDtypeStruct((s, 2 * SGU_WIDTH), BF16),
                   jax.ShapeDtypeStruct((SGU_GROUPS, CHUNK, CHUNK), F32), jax.ShapeDtypeStruct((CHUNK, LANES), F32),
                   jax.ShapeDtypeStruct((1, SGU_WIDTH), F32), jax.ShapeDtypeStruct((1, SGU_WIDTH), F32),
                   jax.ShapeDtypeStruct((1, 2 * SGU_WIDTH), F32)],
        compiler_params=_params(("arbitrary",)),
    )(uv_raw, dyb, uv_b, ln_w, ln_b, w_sp, b_sp_t, group_sum)


def _gate_fwd(gates_raw, b_gate, p_a, p_b, *, name, tm=512):
    s = p_a.shape[0]
    tm = min(tm, s)

    def body(ga_ref, gb_ref, ba_ref, bb_ref, pa_ref, pb_ref, o_ref):
        ga = _sigmoid(ga_ref[...].astype(F32) + ba_ref[...])
        gb = _sigmoid(gb_ref[...].astype(F32) + bb_ref[...])
        o_ref[...] = (ga * pa_ref[...].astype(F32) + gb * pb_ref[...].astype(F32)).astype(BF16)

    t_a = pl.BlockSpec((tm, D_MODEL), lambda i: (i, 0))
    t_b = pl.BlockSpec((tm, D_MODEL), lambda i: (i, 1))
    r_a = pl.BlockSpec((1, D_MODEL), lambda i: (0, 0))
    r_b = pl.BlockSpec((1, D_MODEL), lambda i: (0, 1))
    return pl.pallas_call(
        body, name=name, grid=(s // tm,),
        in_specs=[t_a, t_b, r_a, r_b, t_a, t_a], out_specs=t_a,
        out_shape=jax.ShapeDtypeStruct((s, D_MODEL), BF16),
        compiler_params=_params(("parallel",)),
    )(gates_raw, gates_raw, b_gate, b_gate, p_a, p_b)


def _gate_bwd(gates_raw, b_gate, p_a, p_b, dm, *, name, tm=512):
    s = p_a.shape[0]
    tm = min(tm, s)

    def body(ga_ref, gb_ref, ba_ref, bb_ref, pa_ref, pb_ref, dm_ref, dpa_ref, dpb_ref, dga_ref, dgb_ref,
             dba_ref, dbb_ref):
        @pl.when(pl.program_id(0) == 0)
        def _():
            dba_ref[...] = jnp.zeros_like(dba_ref)
            dbb_ref[...] = jnp.zeros_like(dbb_ref)

        d = dm_ref[...].astype(F32)
        for g_ref, b_ref, p_ref, dp_ref, dg_ref, db_ref in ((ga_ref, ba_ref, pa_ref, dpa_ref, dga_ref, dba_ref),
                                                            (gb_ref, bb_ref, pb_ref, dpb_ref, dgb_ref, dbb_ref)):
            sg = _sigmoid(g_ref[...].astype(F32) + b_ref[...])
            dp_ref[...] = (d * sg).astype(BF16)
            dg = d * p_ref[...].astype(F32) * (sg * (1.0 - sg))
            dg_ref[...] = dg.astype(BF16)
            db_ref[...] += jnp.sum(dg, axis=0, keepdims=True)

    t_a = pl.BlockSpec((tm, D_MODEL), lambda i: (i, 0))
    t_b = pl.BlockSpec((tm, D_MODEL), lambda i: (i, 1))
    r_a = pl.BlockSpec((1, D_MODEL), lambda i: (0, 0))
    r_b = pl.BlockSpec((1, D_MODEL), lambda i: (0, 1))
    big = jax.ShapeDtypeStruct((s, D_MODEL), BF16)
    row = jax.ShapeDtypeStruct((1, D_MODEL), F32)
    return pl.pallas_call(
        body, name=name, grid=(s // tm,),
        in_specs=[t_a, t_b, r_a, r_b, t_a, t_a, t_a], out_specs=[t_a, t_a, t_a, t_a, r_a, r_a],
        out_shape=[big, big, big, big, row, row],
        compiler_params=_params(("arbitrary",)),
    )(gates_raw, gates_raw, b_gate, b_gate, p_a, p_b, dm)


def _adamw_update(w_ref, g_ref, m_ref, v_ref, d_ref, mo_ref, vo_ref):
    gv = g_ref[...]
    mn = ADAM_B1 * m_ref[...] + (1.0 - ADAM_B1) * gv
    vn = ADAM_B2 * v_ref[...] + (1.0 - ADAM_B2) * (gv * gv)
    m_hat = mn / (1.0 - ADAM_B1 ** ADAM_STEP)
    v_hat = vn / (1.0 - ADAM_B2 ** ADAM_STEP)
    d_ref[...] = -ADAM_LR * (m_hat / (jnp.sqrt(v_hat) + ADAM_EPS) + ADAM_WD * w_ref[...])
    mo_ref[...] = mn
    vo_ref[...] = vn


def _adamw_many(ws, gs, ms, vs, *, name):
    n = len(ws)

    def body(*refs):
        for i in range(n):
            _adamw_update(*[refs[k * n + i] for k in range(7)])

    whole = pl.BlockSpec(memory_space=pltpu.VMEM)
    sds = [jax.ShapeDtypeStruct(w.shape, F32) for w in ws]
    outs = pl.pallas_call(
        body, name=name, in_specs=[whole] * (4 * n), out_specs=[whole] * (3 * n), out_shape=sds * 3,
        compiler_params=pltpu.CompilerParams(vmem_limit_bytes=VMEM_LIMIT),
    )(*ws, *gs, *ms, *vs)
    return outs[:n], outs[n:2 * n], outs[2 * n:]


def _adamw(w, g, m, v, *, name, tr=128):
    r, c = w.shape
    tr = min(tr, r)
    assert r % tr == 0, (name, r, tr)
    body = functools.partial(_adamw_update)

    blk = pl.BlockSpec((tr, c), lambda i: (i, 0))
    sds = jax.ShapeDtypeStruct((r, c), F32)
    return pl.pallas_call(
        body, name=name, grid=(r // tr,), in_specs=[blk] * 4, out_specs=[blk] * 3, out_shape=[sds] * 3,
        compiler_params=_params(("parallel",)),
    )(w, g, m, v)


def _adamw_two_sums(w, g_a, g_b, m, v, *, name, tr=128):
    r, c = w.shape
    tr = min(tr, r)
    assert r % tr == 0, (name, r, tr)

    def body(w_ref, ga_ref, gb_ref, m_ref, v_ref, g_ref, d_ref, mo_ref, vo_ref):
        g_ref[...] = ga_ref[...] + gb_ref[...]
        _adamw_update(w_ref, g_ref, m_ref, v_ref, d_ref, mo_ref, vo_ref)

    blk = pl.BlockSpec((tr, c), lambda i: (i, 0))
    sds = jax.ShapeDtypeStruct((r, c), F32)
    return pl.pallas_call(
        body, name=name, grid=(r // tr,), in_specs=[blk] * 5, out_specs=[blk] * 4, out_shape=[sds] * 4,
        compiler_params=_params(("parallel",)),
    )(w, g_a, g_b, m, v)


def _tile(n, pref):
    if n <= pref:
        return n
    best = LANES
    for t in range(LANES, pref + 1, LANES):
        if n % t == 0:
            best = t
    return best


MATMUL_BLOCK_BYTES = 20 * 1024 * 1024


def _mm(pairs, name, **kw):
    trans_b = kw.get("trans_b", False)
    m = (pairs[0][0][0] if isinstance(pairs[0][0], tuple) else pairs[0][0]).shape[0]
    ktot, n = 0, None
    for _, b in pairs:
        shape = b[0].shape[1:] if isinstance(b, tuple) else b.shape
        ktot += shape[1] if trans_b else shape[0]
        n = shape[0] if trans_b else shape[1]
    out_bytes = 4 * (2 if kw.get("add") is not None else 1)
    best = None
    for tm in (256, 512, 1024, 2048):
        for tn in range(LANES, min(n, 1536) + 1, LANES):
            if m % min(tm, m) or n % tn:
                continue
            fits = 2 * ktot * (min(tm, m) + tn) + out_bytes * min(tm, m) * tn <= MATMUL_BLOCK_BYTES
            if fits and (best is None or min(tm, m) * tn >= best[0] * best[1]):
                best = (min(tm, m), tn)
    return _matmul(pairs, tm=best[0], tn=best[1], name=name, **kw)


def _wgrad(a, b, name, **kw):
    return _matmul_tn(a, b, tk=_tile(a.shape[1], 1408), tn=kw.pop("tn", _tile(b.shape[1], 1024)), tm=2048,
                      name=name, **kw)


def _local_step(x, target, get_weight, small, emit_grad):
    heads = jnp.arange(SSD_D_INNER) // SSD_HEAD_DIM
    expand = (jnp.arange(LANES)[:, None] == heads[None, :]).astype(BF16)
    expand_t = expand.T
    group_sum = (jnp.arange(SGU_WIDTH)[:, None] // LANES == jnp.arange(LANES)[None, :]).astype(BF16)
    pad_h = LANES - SSD_HEADS
    dt_bias = jnp.pad(small["dt_bias"], ((0, 0), (0, pad_h)))
    a_log = jnp.pad(small["a_log"], ((0, 0), (0, pad_h)))
    a_log_x = jnp.repeat(small["a_log"], SSD_HEAD_DIM, axis=1)
    d_skip_x = jnp.repeat(small["d_skip"], SSD_HEAD_DIM, axis=1)
    b_sp_t = jnp.pad(small["b_spatial"][0].T, ((0, 0), (0, LANES - SGU_GROUPS)))
    w_sp = small["w_spatial"][0]
    conv_a_w = jnp.pad(small["conv_a_w"], ((0, 4), (0, 0)))
    conv_f_w = jnp.pad(small["conv_f_w"], ((0, 5), (0, 0)))
    final_w = small["final_norm_w"].reshape(1, D_MODEL)

    n1 = _rms_fwd(x, small["norm1_w"], after=small.get("gathers_started"), name="rms1_fwd")
    wts = dict(get_weight("w_in", n1))
    z = _mm([(n1, wts["in_z"])], "in_z")
    xbc_raw = _mm([(n1, wts["in_xbc"])], "in_xbc")
    dt_raw = _mm([(n1, wts["in_dt"])], "in_dt")
    uv_raw = _mm([(n1, wts["in_uv"])], "in_uv", out_dtype=BF16)
    gates_raw = _mm([(n1, wts["in_gate"])], "in_gate", out_dtype=BF16)
    xbc, xbc_pre = _conv_a_fwd(xbc_raw, conv_a_w, small["conv_a_b"], name="conv_a_fwd")
    y, y_a, states = _ssd_fwd(xbc, dt_raw, z, dt_bias, a_log, a_log_x, d_skip_x, small["ssd_norm_w"], expand,
                              name="ssd_fwd")
    y_b = _sgu_fwd(uv_raw, small["uv_b"], small["v_ln_w"], small["v_ln_b"], w_sp, b_sp_t, name="sgu_fwd")
    wts.update(get_weight("w_branch", y_b))
    p_a = _mm([(y_a, wts["branch_a"])], "branch_a", out_dtype=BF16)
    p_b = _mm([(y_b, wts["branch_b"])], "branch_b", out_dtype=BF16)
    mix = _gate_fwd(gates_raw, small["b_gate"], p_a, p_b, name="gate_fwd")
    wts.update(get_weight("w_out", mix))
    h1 = _mm([(mix, wts["out"])], "out_proj", add=x)
    n2 = _rms_fwd(h1, small["norm2_w"], name="rms2_fwd")
    wts.update(get_weight("w_up", n2))
    up_w = wts["up"]
    up_cols = up_w.shape[2]
    up_raw = _matmul([(n2, (up_w, "cols"))], tm=2048, tn=up_cols, out_dtype=BF16, name="up_proj")
    act, up_a, up_v = _conv_f_fwd(up_raw, conv_f_w, small["conv_f_b"], name="conv_f_fwd")
    wts.update(get_weight("w_down", act))
    h2 = _mm([(act, wts["down"])], "down_proj", add=h1)
    loss, dh2, dh2_b, d_final = _final_fwd_bwd(h2, final_w, target, name="final_norm_loss")

    dact = _mm([(dh2_b, wts["down"])], "down_dgrad", trans_b=True)
    started = emit_grad("w_down", _wgrad(act, dh2_b, "down_wgrad"))
    dup_a, dup_v, dwf_a, dwf_v, dbf_a, dbf_v = _conv_f_bwd(up_raw, up_a, up_v, conv_f_w, dact, name="conv_f_bwd")
    dn2 = _mm([((dup_a, 0), (up_w, 0)), ((dup_a, 1), (up_w, 1)), ((dup_v, 0), (up_w, 2)), ((dup_v, 1), (up_w, 3))],
              "up_dgrad", trans_b=True, after=started, out_dtype=BF16)
    g_up = _wgrad(n2, dup_a, "up_wgrad_a", tn=up_cols, stack_out=True, part_of=(N_CHIPS, 0, None))
    g_up = _wgrad(n2, dup_v, "up_wgrad_v", tn=up_cols, stack_out=True, part_of=(N_CHIPS, N_CHIPS // 2, g_up))
    started = emit_grad("w_up", g_up)
    dh1, dh1_b, d_norm2 = _rms_bwd(h1, small["norm2_w"], dn2, dh2, bf16_copy=True, name="rms2_bwd")
    dmix = _mm([(dh1_b, wts["out"])], "out_dgrad", trans_b=True, after=started, out_dtype=BF16)
    started = emit_grad("w_out", _wgrad(mix, dh1_b, "out_wgrad"))
    dp_a, dp_b, dg_a, dg_b, dbg_a, dbg_b = _gate_bwd(gates_raw, small["b_gate"], p_a, p_b, dmix, name="gate_bwd")
    dya = _mm([(dp_a, wts["branch_a"])], "branch_a_dgrad", trans_b=True, after=started)
    dyb = _mm([(dp_b, wts["branch_b"])], "branch_b_dgrad", trans_b=True, out_dtype=BF16)
    g_branch = _wgrad(y_a, dp_a, "branch_a_wgrad", part_of=(3, 0, None))
    g_branch = _wgrad(y_b, dp_b, "branch_b_wgrad", part_of=(3, 2, g_branch))
    started_branch = emit_grad("w_branch", g_branch)
    duv, d_wsp, d_bsp_t, d_lnw, d_lnb, d_uvb = _sgu_bwd(uv_raw, dyb, small["uv_b"], small["v_ln_w"],
                                                        small["v_ln_b"], w_sp, b_sp_t, group_sum, name="sgu_bwd")
    dz, dxbc, ddt, d_ssd_nw, d_dskip, d_alog, d_dtb = _ssd_bwd(
        dya, y, z, xbc, dt_raw, states, dt_bias, a_log, a_log_x, d_skip_x, small["ssd_norm_w"], expand, expand_t,
        name="ssd_bwd")
    dxbc_raw, d_conv_a_w, d_conv_a_b = _conv_a_bwd(xbc_raw, xbc_pre, conv_a_w, dxbc, name="conv_a_bwd")
    started = emit_grad("w_in", {
        "in_z": _wgrad(n1, dz, "in_z_wgrad", after=started_branch), "in_xbc": _wgrad(n1, dxbc_raw, "in_xbc_wgrad"),
        "in_dt": _wgrad(n1, ddt, "in_dt_wgrad")[:, :SSD_HEADS], "in_uv": _wgrad(n1, duv, "in_uv_wgrad"),
        "in_gate_a": _wgrad(n1, dg_a, "in_gate_a_wgrad"), "in_gate_b": _wgrad(n1, dg_b, "in_gate_b_wgrad")})
    dn1 = _mm([(dz, wts["in_z"]), (dxbc_raw, wts["in_xbc"]), (ddt, wts["in_dt"]), (duv, wts["in_uv"]),
               (dg_a, wts["in_gate_a"]), (dg_b, wts["in_gate_b"])], "in_dgrad", trans_b=True, after=started,
              out_dtype=BF16)
    dx, d_norm1 = _rms_bwd(x, small["norm1_w"], dn1, dh1, bf16_copy=False, name="rms1_bwd")

    grads_small = {
        "norm1_w": d_norm1, "b_gate": jnp.concatenate([dbg_a, dbg_b], axis=1),
        "conv_a_w": d_conv_a_w[:4], "conv_a_b": d_conv_a_b,
        "dt_bias": d_dtb[:, :SSD_HEADS], "a_log": d_alog[:, :SSD_HEADS], "d_skip": d_dskip[:, :SSD_HEADS],
        "ssd_norm_w": d_ssd_nw, "uv_b": d_uvb, "v_ln_w": d_lnw, "v_ln_b": d_lnb,
        "w_spatial": d_wsp[None], "b_spatial": d_bsp_t[:, :SGU_GROUPS].T[None],
        "norm2_w": d_norm2, "conv_f_w": jnp.concatenate([dwf_a[:3], dwf_v[:3]], axis=1),
        "conv_f_b": jnp.concatenate([dbf_a, dbf_v], axis=1), "final_norm_w": d_final.reshape(D_MODEL),
    }
    return loss, dx, grads_small


HBM = pl.BlockSpec(memory_space=pl.ANY)
MESH = pl.DeviceIdType.MESH


def _mesh_pos():
    return lax.axis_index("x"), lax.axis_index("y"), lax.axis_index("c")


def _other_chips(x, y):
    return [(1 - x, y), (x, 1 - y), (1 - x, 1 - y)]


def _remote(src, dst, send_sems, recv_sems, k, dev):
    return pltpu.make_async_remote_copy(src_ref=src, dst_ref=dst, send_sem=send_sems.at[k], recv_sem=recv_sems.at[k],
                                        device_id=dev, device_id_type=MESH)


def _dma_sems(n):
    return [pltpu.SemaphoreType.DMA((n,)), pltpu.SemaphoreType.DMA((n,))]


HBM_ONLY = pl.BlockSpec(memory_space=pltpu.HBM)
SEMAPHORES = pl.BlockSpec(memory_space=pltpu.SEMAPHORE)
DATAFLOW_EFFECT = pltpu.SideEffectType.DATAFLOW_SIDE_EFFECTING
N_PEER_CHIPS = N_CHIPS - 1


def _gather_sends(w_ref, land_ref, send_sems, recv_sems):
    x, y, c = _mesh_pos()
    return [_remote(w_ref.at[c], land_ref.at[2 * x + y, c], send_sems, recv_sems, k, (px, py, c))
            for k, (px, py) in enumerate(_other_chips(x, y))]


def _gather_arrivals(w_ref, land_ref, send_sems, recv_sems):
    x, y, c = _mesh_pos()
    return [_remote(w_ref.at[c], land_ref.at[2 * px + py, c], send_sems, recv_sems, k, (px, py, c))
            for k, (px, py) in enumerate(_other_chips(x, y))]


def _gather_whole_sends(w_ref, land_ref, send_sems, recv_sems):
    x, y, c = _mesh_pos()
    return [_remote(w_ref, land_ref.at[2 * x + y], send_sems, recv_sems, k, (px, py, c))
            for k, (px, py) in enumerate(_other_chips(x, y))]


def _gather_whole_arrivals(w_ref, land_ref, send_sems, recv_sems):
    x, y, c = _mesh_pos()
    return [_remote(w_ref, land_ref.at[2 * px + py], send_sems, recv_sems, k, (px, py, c))
            for k, (px, py) in enumerate(_other_chips(x, y))]


def _scatter_sends(h_ref, land_ref, send_sems, recv_sems):
    x, y, c = _mesh_pos()
    return [_remote(h_ref.at[2 * px + py], land_ref.at[2 * x + y], send_sems, recv_sems, k, (px, py, c))
            for k, (px, py) in enumerate(_other_chips(x, y))]


def _scatter_arrivals(h_ref, land_ref, send_sems, recv_sems):
    x, y, c = _mesh_pos()
    return [_remote(h_ref.at[2 * x + y], land_ref.at[2 * px + py], send_sems, recv_sems, k, (px, py, c))
            for k, (px, py) in enumerate(_other_chips(x, y))]


def _exchange_wait_many(pendings, after, sends, arrivals, *, name):
    n = len(pendings)

    def body(*refs):
        for i in range(n):
            src_ref, land_ref, send_ref, recv_ref = refs[i], refs[n + i], refs[2 * n + i], refs[3 * n + i]
            for cp in sends(src_ref, land_ref, send_ref, recv_ref):
                cp.wait_send()
            for cp in arrivals(src_ref, land_ref, send_ref, recv_ref):
                cp.wait_recv()

    sources = [p[2] for p in pendings]
    landings = [p[3] for p in pendings]
    outs = pl.pallas_call(
        body, name=name,
        out_shape=tuple(pltpu.HBM(a.shape, a.dtype) for a in sources + landings),
        in_specs=[HBM_ONLY] * (2 * n) + [SEMAPHORES] * (2 * n) + [pl.BlockSpec(memory_space=pl.ANY)],
        out_specs=tuple([HBM_ONLY] * (2 * n)), input_output_aliases={i: i for i in range(2 * n)},
        compiler_params=pltpu.CompilerParams(has_side_effects=DATAFLOW_EFFECT),
    )(*sources, *landings, *[p[0] for p in pendings], *[p[1] for p in pendings], after)
    return [(outs[i], outs[n + i]) for i in range(n)]


def _sibling_sends(src_ref, land_ref, send_sems, recv_sems):
    x, y, c = _mesh_pos()
    return [_remote(src_ref, land_ref, send_sems, recv_sems, 0, (x, y, 1 - c))]


def _exchange_start(sources, landing_shapes, sends, *, after=None, name):
    n = len(sources)
    extra = [] if after is None else [after]

    def body(*refs):
        sems = refs[2 * n + len(extra):4 * n + len(extra)]
        for i in range(n):
            send_i = sends[i] if isinstance(sends, (list, tuple)) else sends
            for cp in send_i(refs[i], refs[n + i], sems[2 * i], sems[2 * i + 1]):
                cp.start()
        refs[-1][...] = jnp.zeros_like(refs[-1])

    hbm = [pltpu.HBM(s.shape, s.dtype) for s in sources] + [pltpu.HBM(shp, s.dtype)
                                                             for shp, s in zip(landing_shapes, sources)]
    outs = pl.pallas_call(
        body, name=name,
        out_shape=tuple([pltpu.SemaphoreType.DMA((N_PEER_CHIPS,))] * (2 * n) + hbm
                        + [jax.ShapeDtypeStruct((8, LANES), F32)]),
        in_specs=[HBM_ONLY] * (2 * n) + [pl.BlockSpec(memory_space=pl.ANY)] * len(extra),
        out_specs=tuple([SEMAPHORES] * (2 * n) + [HBM_ONLY] * (2 * n) + [pl.BlockSpec(memory_space=pltpu.VMEM)]),
        input_output_aliases={i: 2 * n + i for i in range(2 * n)},
        compiler_params=pltpu.CompilerParams(has_side_effects=DATAFLOW_EFFECT),
    )(*[pltpu.with_memory_space_constraint(s, pltpu.HBM) for s in sources],
      *[pltpu.with_memory_space_constraint(lax.empty(shp, s.dtype), pltpu.HBM)
        for shp, s in zip(landing_shapes, sources)], *extra)
    pending = [(outs[2 * i], outs[2 * i + 1], outs[2 * n + i], outs[3 * n + i]) for i in range(n)]
    return pending, outs[-1]


def _exchange_wait(pending, after, sends, arrivals, *, name):
    send_sems, recv_sems, source, landing = pending

    def body(src_ref, land_ref, send_ref, recv_ref, after_ref, src_out, land_out):
        for cp in sends(src_ref, land_ref, send_ref, recv_ref):
            cp.wait_send()
        for cp in arrivals(src_ref, land_ref, send_ref, recv_ref):
            cp.wait_recv()

    return pl.pallas_call(
        body, name=name,
        out_shape=(pltpu.HBM(source.shape, source.dtype), pltpu.HBM(landing.shape, landing.dtype)),
        in_specs=[HBM_ONLY, HBM_ONLY, SEMAPHORES, SEMAPHORES, pl.BlockSpec(memory_space=pl.ANY)],
        out_specs=(HBM_ONLY, HBM_ONLY), input_output_aliases={0: 0, 1: 1},
        compiler_params=pltpu.CompilerParams(has_side_effects=DATAFLOW_EFFECT),
    )(source, landing, send_sems, recv_sems, after)


def _gather_ici(shard, *, name):
    _, rh, cols = shard.shape

    def body(w_ref, o_ref, send_sems, recv_sems):
        x, y, c = _mesh_pos()
        mine = 2 * x + y
        sends = []
        for k, (px, py) in enumerate(_other_chips(x, y)):
            cp = _remote(w_ref.at[c], o_ref.at[mine, c], send_sems, recv_sems, k, (px, py, c))
            cp.start()
            sends.append(cp)
        for k, (px, py) in enumerate(_other_chips(x, y)):
            _remote(w_ref.at[c], o_ref.at[2 * px + py, c], send_sems, recv_sems, k, (px, py, c)).wait_recv()
        for cp in sends:
            cp.wait_send()

    return pl.pallas_call(
        body, name=name, in_specs=[HBM], out_specs=HBM,
        out_shape=jax.ShapeDtypeStruct((N_CHIPS, 2, rh, cols), shard.dtype), scratch_shapes=_dma_sems(3),
    )(shard)


def _gather_d2d(parts, *, name):
    def body(a_ref, o_ref, send_sems, recv_sems):
        x, y, c = _mesh_pos()
        sibling = (x, y, 1 - c)
        sends = []
        for k, (px, py) in enumerate(_other_chips(x, y)):
            cp = _remote(a_ref.at[2 * px + py, c], o_ref.at[2 * px + py, c], send_sems, recv_sems, k, sibling)
            cp.start()
            sends.append(cp)
        for k, (px, py) in enumerate(_other_chips(x, y)):
            _remote(a_ref.at[2 * px + py, c], o_ref.at[2 * px + py, 1 - c], send_sems, recv_sems, k, sibling).wait_recv()
        for cp in sends:
            cp.wait_send()

    return pl.pallas_call(
        body, name=name, in_specs=[HBM], out_specs=HBM,
        out_shape=jax.ShapeDtypeStruct(parts.shape, parts.dtype),
        input_output_aliases={0: 0}, scratch_shapes=_dma_sems(3),
    )(parts)


def _all_gather_chips(shard_flat, name):
    rows, cols = shard_flat.shape
    parts = _gather_ici(shard_flat.reshape(2, rows // 2, cols), name=name + "_ici")
    others = _gather_d2d(parts, name=name + "_d2d").reshape(N_CHIPS, rows, cols)
    chip = 2 * lax.axis_index("x") + lax.axis_index("y")
    return lax.dynamic_update_slice(others, shard_flat[None], (chip, 0, 0))


def _row_tile(rows, mult, cap):
    best = mult
    for t in range(mult, min(rows, cap) + 1, mult):
        if rows % t == 0:
            best = t
    assert rows % best == 0, (rows, mult)
    return best


def _swap_halves_d2d(g, *, after=None, name):
    blocks = list(g) if isinstance(g, (list, tuple)) else [g]
    rh, cols = blocks[0].shape[-2:]
    extra = [] if after is None else [after]

    def body(*refs):
        g_refs = refs[:len(blocks)]
        o_ref, send_sems, recv_sems = refs[len(blocks) + len(extra):]
        x, y, c = _mesh_pos()
        sibling = (x, y, 1 - c)

        def half(s, k):
            return g_refs[s].at[k] if len(g_refs) > 1 else g_refs[0].at[s, k]

        sends = []
        for s in range(N_CHIPS):
            cp = _remote(half(s, 1 - c), o_ref.at[s], send_sems, recv_sems, s, sibling)
            cp.start()
            sends.append(cp)
        for s in range(N_CHIPS):
            _remote(half(s, c), o_ref.at[s], send_sems, recv_sems, s, sibling).wait_recv()
        for cp in sends:
            cp.wait_send()

    return pl.pallas_call(
        body, name=name, in_specs=[HBM] * (len(blocks) + len(extra)), out_specs=HBM,
        out_shape=jax.ShapeDtypeStruct((N_CHIPS, rh, cols), blocks[0].dtype), scratch_shapes=_dma_sems(N_CHIPS),
    )(*blocks, *extra)


def _add_own_half(g, arrived, core, *, name):
    blocks = list(g) if isinstance(g, (list, tuple)) else [g]
    dtype = blocks[0].dtype
    rh, cols = blocks[0].shape[-2:]
    mult = 16 if dtype == BF16 else 8
    tr = _row_tile(rh, mult, max(mult, (512 * 1024) // cols))

    def body(core_ref, *refs):
        g_refs, a_ref, o_ref = refs[:-2], refs[-2], refs[-1]
        if len(g_refs) == 1:
            o_ref[...] = (g_refs[0][0].astype(F32) + a_ref[...].astype(F32)).astype(o_ref.dtype)
            return
        for k, g_ref in enumerate(g_refs):
            @pl.when(pl.program_id(0) == k)
            def _(g_ref=g_ref):
                o_ref[...] = (g_ref[...].astype(F32) + a_ref[...].astype(F32)).astype(o_ref.dtype)

    if len(blocks) == 1:
        g_specs = [pl.BlockSpec((1, 1, tr, cols), lambda s, i, core_ref: (s, core_ref[0], i, 0))]
    else:
        g_specs = [pl.BlockSpec((1, tr, cols), lambda s, i, core_ref, k=k: (core_ref[0], jnp.where(s == k, i, 0), 0))
                   for k in range(N_CHIPS)]
    grid_spec = pltpu.PrefetchScalarGridSpec(
        num_scalar_prefetch=1, grid=(N_CHIPS, rh // tr),
        in_specs=g_specs + [pl.BlockSpec((1, tr, cols), lambda s, i, core_ref: (s, i, 0))],
        out_specs=pl.BlockSpec((1, tr, cols), lambda s, i, core_ref: (s, i, 0)))
    return pl.pallas_call(
        body, name=name, grid_spec=grid_spec, out_shape=jax.ShapeDtypeStruct((N_CHIPS, rh, cols), dtype),
        compiler_params=_params(("parallel", "parallel")),
    )(core, *blocks, arrived)


def _scatter_ici(h, *, after=None, name):
    extra = [] if after is None else [after]

    def body(h_ref, *rest):
        o_ref, send_sems, recv_sems = rest[len(extra):]
        x, y, c = _mesh_pos()
        mine = 2 * x + y
        sends = []
        for k, (px, py) in enumerate(_other_chips(x, y)):
            cp = _remote(h_ref.at[2 * px + py], o_ref.at[mine], send_sems, recv_sems, k, (px, py, c))
            cp.start()
            sends.append(cp)
        for k, (px, py) in enumerate(_other_chips(x, y)):
            _remote(h_ref.at[mine], o_ref.at[2 * px + py], send_sems, recv_sems, k, (px, py, c)).wait_recv()
        for cp in sends:
            cp.wait_send()

    others = pl.pallas_call(
        body, name=name, in_specs=[HBM] * (1 + len(extra)), out_specs=HBM,
        out_shape=jax.ShapeDtypeStruct(h.shape, h.dtype), scratch_shapes=_dma_sems(3),
    )(h, *extra)
    chip = 2 * lax.axis_index("x") + lax.axis_index("y")
    own = lax.dynamic_slice_in_dim(h, chip, 1, axis=0)
    return lax.dynamic_update_slice(others, own, (chip, 0, 0))


def _sum_chips(parts, *, name):
    _, rh, cols = parts.shape
    mult = 16 if parts.dtype == BF16 else 8
    tr = _row_tile(rh, mult, max(mult, (512 * 1024) // cols))

    def body(p_ref, o_ref):
        acc = p_ref[0].astype(F32)
        for s in range(1, N_CHIPS):
            acc = acc + p_ref[s].astype(F32)
        o_ref[...] = acc

    return pl.pallas_call(
        body, name=name, grid=(rh // tr,),
        in_specs=[pl.BlockSpec((N_CHIPS, tr, cols), lambda i: (0, i, 0))],
        out_specs=pl.BlockSpec((tr, cols), lambda i: (i, 0)),
        out_shape=jax.ShapeDtypeStruct((rh, cols), F32), compiler_params=_params(("parallel",)),
    )(parts)


def _sum_chips_with_own(landed, sent, chip, *, name):
    _, rh, cols = landed.shape
    mult = 16 if landed.dtype == BF16 else 8
    tr = _row_tile(rh, mult, max(mult, (512 * 1024) // cols))

    def body(chip_ref, own_ref, px_ref, py_ref, pxy_ref, o_ref):
        acc = own_ref[0].astype(F32)
        for p_ref in (px_ref, py_ref, pxy_ref):
            acc = acc + p_ref[0].astype(F32)
        o_ref[...] = acc

    def block_of(flip):
        return pl.BlockSpec((1, tr, cols), lambda i, chip_ref: (chip_ref[0] ^ flip, i, 0))

    grid_spec = pltpu.PrefetchScalarGridSpec(
        num_scalar_prefetch=1, grid=(rh // tr,),
        in_specs=[block_of(0), block_of(2), block_of(1), block_of(3)],
        out_specs=pl.BlockSpec((tr, cols), lambda i, chip_ref: (i, 0)))
    return pl.pallas_call(
        body, name=name, grid_spec=grid_spec, out_shape=jax.ShapeDtypeStruct((rh, cols), F32),
        compiler_params=_params(("parallel",)),
    )(chip, sent, landed, landed, landed)


def _share_d2d(f, *, name):
    fs = f if isinstance(f, (list, tuple)) else [f]
    others = _swap_with_sibling(fs, name=name)
    first = lax.axis_index("c") == 0
    both = [jnp.stack([jnp.where(first, a, b), jnp.where(first, b, a)]) for a, b in zip(fs, others)]
    return both if isinstance(f, (list, tuple)) else both[0]


def _swap_with_sibling(fs, *, name):
    n = len(fs)

    def body(*refs):
        x, y, c = _mesh_pos()
        sibling = (x, y, 1 - c)
        send_sems, recv_sems = refs[2 * n:]
        copies = [_remote(refs[i], refs[n + i], send_sems, recv_sems, i, sibling) for i in range(n)]
        for cp in copies:
            cp.start()
        for cp in copies:
            cp.wait()

    return pl.pallas_call(
        body, name=name, in_specs=[HBM] * n, out_specs=[HBM] * n,
        out_shape=[jax.ShapeDtypeStruct(a.shape, a.dtype) for a in fs], scratch_shapes=_dma_sems(n),
    )(*fs)


def _reduce_scatter_chips(g, core, name, after=None, after_swap=None):
    _, rows, cols = g.shape
    g = g.reshape(N_CHIPS, 2, rows // 2, cols)
    arrived = _swap_halves_d2d(g, after=after, name=name + "_swap")
    started = after_swap(arrived) if after_swap is not None else None
    chip_sum = _add_own_half(g, arrived, core, name=name + "_add2")
    parts = _scatter_ici(chip_sum, after=started, name=name + "_ici")
    total = _sum_chips(parts, name=name + "_sum4")
    return _share_d2d(total, name=name + "_share").reshape(rows, cols)


BIG = ("w_in", "w_branch", "w_out", "w_up", "w_down")
BIG_COLUMN_SHARDED = ("w_in", "w_up")
CONV = ("conv_a_w", "conv_f_w")
REPLICATED = ("norm1_w", "b_gate", "conv_a_b", "dt_bias", "a_log", "d_skip", "ssd_norm_w", "uv_b", "v_ln_w",
              "v_ln_b", "w_spatial", "b_spatial", "norm2_w", "conv_f_b", "final_norm_w")
WEIGHT_ORDER = ("norm1_w", "w_in", "b_gate", "conv_a_w", "conv_a_b", "dt_bias", "a_log", "d_skip", "ssd_norm_w",
                "uv_b", "v_ln_w", "v_ln_b", "w_spatial", "b_spatial", "w_branch", "w_out", "norm2_w", "w_up",
                "conv_f_w", "conv_f_b", "w_down", "final_norm_w")
SMALL_EXCHANGE_ROWS = 64


_GATE0 = SSD_IN + 2 * SGU_WIDTH
IN_SEGMENTS = {
    "in_z": (0, SSD_D_INNER), "in_xbc": (SSD_D_INNER, SSD_D_INNER + SSD_XBC), "in_dt": (SSD_D_INNER + SSD_XBC, SSD_IN),
    "in_uv": (SSD_IN, _GATE0), "in_gate": (_GATE0, IN_COLS), "in_gate_a": (_GATE0, _GATE0 + D_MODEL),
    "in_gate_b": (_GATE0 + D_MODEL, IN_COLS),
}
IN_GRAD_SEGMENTS = ("in_z", "in_xbc", "in_dt", "in_uv", "in_gate_a", "in_gate_b")


def _take_columns(parts, start, stop):
    out = []
    for a, first in parts:
        lo, hi = max(start, first), min(stop, first + a.shape[1])
        if lo < hi:
            out.append(a[:, lo - first:hi - first])
    return out[0] if len(out) == 1 else jnp.concatenate(out, axis=1)


def _flat_rows(arrays, row_multiple):
    flat = jnp.concatenate([a.reshape(-1) for a in arrays])
    rows = -(-flat.shape[0] // (LANES * row_multiple)) * row_multiple
    return jnp.pad(flat, (0, rows * LANES - flat.shape[0])).reshape(rows, LANES)


def _unflatten(flat, shapes):
    flat = flat.reshape(-1)
    out, off = [], 0
    for shp in shapes:
        n = math.prod(shp)
        out.append(flat[off:off + n].reshape(shp))
        off += n
    return out


def _from_chip_blocks(blocks, name):
    if name in BIG_COLUMN_SHARDED or name in CONV:
        k = blocks.shape[1]
        return jnp.transpose(blocks, (1, 0, 2)).reshape(k, -1)
    return blocks.reshape(-1, blocks.shape[-1])


def _to_chip_blocks(whole, name):
    if name in BIG_COLUMN_SHARDED or name in CONV:
        k, n = whole.shape
        return jnp.transpose(whole.reshape(k, N_CHIPS, n // N_CHIPS), (1, 0, 2))
    return whole.reshape(N_CHIPS, whole.shape[0] // N_CHIPS, whole.shape[1])


def kernel(x, norm1_w, w_in, b_gate, conv_a_w, conv_a_b, dt_bias, a_log, d_skip, ssd_norm_w, uv_b, v_ln_w, v_ln_b, w_spatial, b_spatial, w_branch, w_out, norm2_w, w_up, conv_f_w, conv_f_b, w_down, final_norm_w, loss_target, m_norm1_w, m_w_in, m_b_gate, m_conv_a_w, m_conv_a_b, m_dt_bias, m_a_log, m_d_skip, m_ssd_norm_w, m_uv_b, m_v_ln_w, m_v_ln_b, m_w_spatial, m_b_spatial, m_w_branch, m_w_out, m_norm2_w, m_w_up, m_conv_f_w, m_conv_f_b, m_w_down, m_final_norm_w, v_norm1_w, v_w_in, v_b_gate, v_conv_a_w, v_conv_a_b, v_dt_bias, v_a_log, v_d_skip, v_ssd_norm_w, v_uv_b, v_v_ln_w, v_v_ln_b, v_w_spatial, v_b_spatial, v_w_branch, v_w_out, v_norm2_w, v_w_up, v_conv_f_w, v_conv_f_b, v_w_down, v_final_norm_w):
    weights = dict(norm1_w=norm1_w, w_in=w_in, b_gate=b_gate, conv_a_w=conv_a_w, conv_a_b=conv_a_b, dt_bias=dt_bias,
                   a_log=a_log, d_skip=d_skip, ssd_norm_w=ssd_norm_w, uv_b=uv_b, v_ln_w=v_ln_w, v_ln_b=v_ln_b,
                   w_spatial=w_spatial, b_spatial=b_spatial, w_branch=w_branch, w_out=w_out, norm2_w=norm2_w,
                   w_up=w_up, conv_f_w=conv_f_w, conv_f_b=conv_f_b, w_down=w_down, final_norm_w=final_norm_w)
    mom1 = dict(norm1_w=m_norm1_w, w_in=m_w_in, b_gate=m_b_gate, conv_a_w=m_conv_a_w, conv_a_b=m_conv_a_b,
                dt_bias=m_dt_bias, a_log=m_a_log, d_skip=m_d_skip, ssd_norm_w=m_ssd_norm_w, uv_b=m_uv_b,
                v_ln_w=m_v_ln_w, v_ln_b=m_v_ln_b, w_spatial=m_w_spatial, b_spatial=m_b_spatial, w_branch=m_w_branch,
                w_out=m_w_out, norm2_w=m_norm2_w, w_up=m_w_up, conv_f_w=m_conv_f_w, conv_f_b=m_conv_f_b,
                w_down=m_w_down, final_norm_w=m_final_norm_w)
    mom2 = dict(norm1_w=v_norm1_w, w_in=v_w_in, b_gate=v_b_gate, conv_a_w=v_conv_a_w, conv_a_b=v_conv_a_b,
                dt_bias=v_dt_bias, a_log=v_a_log, d_skip=v_d_skip, ssd_norm_w=v_ssd_norm_w, uv_b=v_uv_b,
                v_ln_w=v_v_ln_w, v_ln_b=v_v_ln_b, w_spatial=v_w_spatial, b_spatial=v_b_spatial, w_branch=v_w_branch,
                w_out=v_w_out, norm2_w=v_norm2_w, w_up=v_w_up, conv_f_w=v_conv_f_w, conv_f_b=v_conv_f_b,
                w_down=v_w_down, final_norm_w=v_final_norm_w)
    chip = 2 * lax.axis_index("x") + lax.axis_index("y")
    core = lax.axis_index("c").astype(jnp.int32).reshape(1)

    whole = {}
    conv_shapes = [weights[n].shape[1:] for n in CONV]
    conv_gathered = _all_gather_chips(_flat_rows([weights[n] for n in CONV], 16), "gather_conv").reshape(N_CHIPS, -1)
    off = 0
    for n, shp in zip(CONV, conv_shapes):
        size = math.prod(shp)
        whole[n] = _from_chip_blocks(conv_gathered[:, off:off + size].reshape((N_CHIPS,) + shp), n)
        off += size
    shard_shapes = {n: weights[n].shape[1:] for n in BIG}
    halves = [weights[n][0].astype(BF16).reshape(2, shard_shapes[n][0] // 2, shard_shapes[n][1]) for n in BIG]
    sends = [_gather_sends if n == "w_in" else _gather_whole_sends for n in BIG]
    gathers, gathers_started = _exchange_start(halves, [(N_CHIPS,) + h.shape for h in halves], sends,
                                               after=conv_gathered, name="gather_start")
    gathers = dict(zip(BIG, gathers))

    def get_weight(name, after):
        rows, cols = shard_shapes[name]
        if name == "w_in":
            own, landed = _exchange_wait(gathers[name], after, _gather_sends, _gather_arrivals,
                                         name="gather_" + name + "_wait")
            landed = _gather_d2d(landed, name="gather_" + name + "_d2d")
        else:
            own, landed = _exchange_wait(gathers[name], after, _gather_whole_sends, _gather_whole_arrivals,
                                         name="gather_" + name + "_wait")
        blocks = lax.dynamic_update_slice(landed.reshape(N_CHIPS, rows, cols), own.reshape(1, rows, cols),
                                          (chip, 0, 0))
        if name == "w_up":
            return {"up": blocks}
        if name == "w_in":
            parts = [(blocks[k], cols * k) for k in range(N_CHIPS)]
            segs = {n: _take_columns(parts, a, b) for n, (a, b) in IN_SEGMENTS.items()}
            segs["in_dt"] = jnp.pad(segs["in_dt"], ((0, 0), (0, LANES - SSD_HEADS)))
            return segs
        full = _from_chip_blocks(blocks, name)
        if name == "w_branch":
            return {"branch_a": full[:SSD_D_INNER], "branch_b": full[SSD_D_INNER:]}
        return {name[2:]: full}

    small = {n: weights[n] for n in REPLICATED}
    small["conv_a_w"] = whole["conv_a_w"]
    small["conv_f_w"] = whole["conv_f_w"]
    small["gathers_started"] = gathers_started

    reductions = {}

    def emit_grad(name, g):
        if name == "w_in":
            parts = [(g[n], IN_SEGMENTS[n][0]) for n in IN_GRAD_SEGMENTS]
            rows, cols = shard_shapes[name]
            g_halves = [_take_columns(parts, cols * k, cols * (k + 1)).reshape(2, rows // 2, cols)
                        for k in range(N_CHIPS)]
            arrived = _swap_halves_d2d(g_halves, name="reduce_" + name + "_swap")
            g_blocks = _add_own_half(g_halves, arrived, core, name="reduce_" + name + "_add2")
        else:
            g_blocks = g if name == "w_up" else _to_chip_blocks(g, name)
        (pending,), started = _exchange_start([g_blocks], [g_blocks.shape], _scatter_sends,
                                              name="reduce_" + name + "_start")
        reductions[name] = pending
        return started

    loss, dx, grads_small = _local_step(x[0], loss_target[0], get_weight, small, emit_grad)

    order = ("w_down", "w_up", "w_out", "w_branch", "w_in")
    core_sums = []
    chip_index = chip.astype(jnp.int32).reshape(1)
    for n in order:
        sent, landed = _exchange_wait(reductions[n], dx, _scatter_sends, _scatter_arrivals,
                                      name="reduce_" + n + "_wait")
        core_sums.append(_sum_chips_with_own(landed, sent, chip_index, name="reduce_" + n + "_sum4"))
    grads = {}
    swaps = []

    def start_sum_swap(small_swapped):
        pending, started = _exchange_start(core_sums, [a.shape for a in core_sums], _sibling_sends,
                                           after=small_swapped, name="reduce_swap_start")
        swaps.extend(pending)
        return started

    small_names = REPLICATED + CONV + ("loss",)
    grads_small = dict(grads_small, loss=loss)
    small_shapes = [grads_small[n].shape for n in small_names]
    g_small = _flat_rows([grads_small[n] for n in small_names], N_CHIPS * 2 * SMALL_EXCHANGE_ROWS)
    red_small = _reduce_scatter_chips(g_small.reshape(N_CHIPS, -1, LANES), core, "reduce_small", after=core_sums[-1],
                                      after_swap=start_sum_swap)
    all_small = _all_gather_chips(red_small, "gather_small")
    swapped = _exchange_wait_many(swaps, all_small, _sibling_sends, _sibling_sends, name="reduce_swap_wait")
    core_sums = {n: own for n, (own, _) in zip(order, swapped)}
    sibling_sums = {n: other for n, (_, other) in zip(order, swapped)}
    first = lax.axis_index("c") == 0
    w_in_halves = (core_sums["w_in"].T, sibling_sums["w_in"].T)
    w_in_grad_t = jnp.concatenate([jnp.where(first, w_in_halves[0], w_in_halves[1]),
                                   jnp.where(first, w_in_halves[1], w_in_halves[0])], axis=1)
    for n, g in zip(small_names, _unflatten(all_small, small_shapes)):
        if n == "loss":
            total_loss = g[0, 0]
            continue
        if n in CONV:
            width = g.shape[1] // N_CHIPS
            g = lax.dynamic_slice_in_dim(g, chip * width, width, axis=1)
        grads[n] = g.reshape(weights[n].shape[1:]) if n != "final_norm_w" else g

    delta, new_m, new_v = {}, {}, {}
    for n in BIG:
        shp = weights[n].shape
        if n == "w_in":
            g_t = w_in_grad_t
            results = [g_t] + list(_adamw(weights[n][0].T, g_t, mom1[n][0].T, mom2[n][0].T, name="adamw_" + n,
                                          tr=_row_tile(g_t.shape[0], 8, 136)))
            results = [a.T for a in results]
        else:
            results = _adamw_two_sums(weights[n][0], core_sums[n], sibling_sums[n], mom1[n][0], mom2[n][0],
                                      name="adamw_" + n, tr=_row_tile(shp[1], 8, 352))
        grads[n], delta[n], new_m[n], new_v[n] = [a.reshape(shp) for a in results]
    small_all = [n for n in WEIGHT_ORDER if n not in BIG]

    def as_2d(a):
        return a.reshape(-1, a.shape[-1])

    results = _adamw_many(*[[as_2d(src[n]) for n in small_all] for src in (weights, grads, mom1, mom2)],
                          name="adamw_small")
    for n, dv, mv, vv in zip(small_all, *results):
        shp = weights[n].shape
        delta[n], new_m[n], new_v[n] = dv.reshape(shp), mv.reshape(shp), vv.reshape(shp)

    grad_out = [grads[n].reshape(weights[n].shape) for n in WEIGHT_ORDER]
    return (total_loss, dx[None], *grad_out, *[delta[n] for n in WEIGHT_ORDER], *[new_m[n] for n in WEIGHT_ORDER],
            *[new_v[n] for n in WEIGHT_ORDER])
```

```python
import functools
import math

import jax
import jax.numpy as jnp
from jax import lax
from jax.experimental import pallas as pl
from jax.experimental.pallas import tpu as pltpu

F32 = jnp.float32
BF16 = jnp.bfloat16

D_MODEL = 1024
SSD_D_INNER = 2048
SSD_HEADS = 32
SSD_HEAD_DIM = 64
SSD_GROUPS = 4
SSD_HEADS_PER_GROUP = 8
SSD_STATE = 128
SSD_BC = 512
SSD_XBC = 3072
SSD_IN = 5152
SGU_WIDTH = 1024
SGU_GROUPS = 8
CHUNK = 128
IN_COLS = 9248
D_FF = 2816
NORM_EPS = 1e-6
LN_EPS = 1e-5
GROUP_COLS = SSD_HEADS_PER_GROUP * SSD_HEAD_DIM
LANES = 128

ADAM_LR = 0.001
ADAM_B1 = 0.9
ADAM_B2 = 0.999
ADAM_EPS = 1e-08
ADAM_WD = 0.01
ADAM_STEP = 10

N_CHIPS = 4
VMEM_LIMIT = 56 * 1024 * 1024

NT = (((1,), (1,)), ((), ()))
TN = (((0,), (0,)), ((), ()))
NN = (((1,), (0,)), ((), ()))


def _params(dims):
    return pltpu.CompilerParams(dimension_semantics=dims, vmem_limit_bytes=VMEM_LIMIT)


def _dot(a, b, dn=NN, precision=None):
    return lax.dot_general(a, b, dn, precision=precision, preferred_element_type=F32)


def _split3(x):
    hi = x.astype(BF16)
    rest = x - hi.astype(F32)
    mid = rest.astype(BF16)
    return hi, mid, (rest - mid.astype(F32)).astype(BF16)


def _dot_terms(terms, exact, dn=NN):
    out = None
    for t in terms:
        p = _dot(t, exact, dn)
        out = p if out is None else out + p
    return out


def _dot_exact_lhs(exact, terms):
    out = None
    for t in terms:
        p = _dot(exact, t)
        out = p if out is None else out + p
    return out


def _sigmoid(x):
    return 1.0 / (1.0 + jnp.exp(-x))


def _softplus(x):
    return jnp.maximum(x, 0.0) + jnp.log(1.0 + jnp.exp(-jnp.abs(x)))


def _matmul(pairs, *, trans_b=False, add=None, after=None, out_dtype=F32, tm=512, tn=512, name):
    def mat_shape(b):
        if isinstance(b, tuple) and b[1] == "cols":
            return (b[0].shape[1], b[0].shape[0] * b[0].shape[2])
        return b[0].shape[1:] if isinstance(b, tuple) else b.shape

    if isinstance(pairs[0][1], tuple) and pairs[0][1][1] == "cols":
        assert not trans_b and tn % LANES == 0 and pairs[0][1][0].shape[2] % tn == 0, name

    m = (pairs[0][0][0] if isinstance(pairs[0][0], tuple) else pairs[0][0]).shape[0]
    n = mat_shape(pairs[0][1])[0] if trans_b else mat_shape(pairs[0][1])[1]
    tm, tn = min(tm, m), min(tn, n)
    assert m % tm == 0 and n % tn == 0, (name, m, n, tm, tn)
    npairs = len(pairs)
    dn = NT if trans_b else NN

    def body(*refs):
        o_ref = refs[-1]
        acc = None
        for i in range(npairs):
            p = _dot(refs[2 * i][...].astype(BF16), refs[2 * i + 1][...].astype(BF16), dn)
            acc = p if acc is None else acc + p
        if add is not None:
            acc = acc + refs[2 * npairs][...]
        o_ref[...] = acc.astype(out_dtype)

    in_specs, args = [], []
    for a, b in pairs:
        bshape = mat_shape(b)
        k = bshape[1] if trans_b else bshape[0]
        assert bshape == ((n, k) if trans_b else (k, n)), (name, bshape)
        a, qa = a if isinstance(a, tuple) else (a, 0)
        assert a.shape[0] == m and a.shape[1] % k == 0, (name, a.shape, k)
        in_specs.append(pl.BlockSpec((tm, k), lambda i, j, qa=qa: (i, qa)))
        if isinstance(b, tuple) and b[1] == "cols":
            b = b[0]
            per = b.shape[2] // tn
            in_specs.append(pl.BlockSpec((None, k, tn), lambda i, j, per=per: (j // per, 0, j % per)))
        elif isinstance(b, tuple):
            b, qb = b
            if trans_b:
                in_specs.append(pl.BlockSpec((None, tn, k), lambda i, j, qb=qb: (qb, j, 0)))
            else:
                in_specs.append(pl.BlockSpec((None, k, tn), lambda i, j, qb=qb: (qb, 0, j)))
        elif trans_b:
            in_specs.append(pl.BlockSpec((tn, k), lambda i, j: (j, 0)))
        else:
            in_specs.append(pl.BlockSpec((k, tn), lambda i, j: (0, j)))
        args += [a, b]
    if add is not None:
        in_specs.append(pl.BlockSpec((tm, tn), lambda i, j: (i, j)))
        args.append(add)
    if after is not None:
        in_specs.append(pl.BlockSpec(memory_space=pl.ANY))
        args.append(after)
    return pl.pallas_call(
        body, name=name, grid=(m // tm, n // tn), in_specs=in_specs,
        out_specs=pl.BlockSpec((tm, tn), lambda i, j: (i, j)),
        out_shape=jax.ShapeDtypeStruct((m, n), out_dtype),
        compiler_params=_params(("parallel", "parallel")),
    )(*args)


def _matmul_tn(a, b, *, tk, tn, tm=1024, out_dtype=BF16, stack_out=False, after=None, part_of=None, name):
    m, k = a.shape
    n = b.shape[1]
    tm, tk, tn = min(tm, m), min(tk, k), min(tn, n)
    assert m % tm == 0 and k % tk == 0 and n % tn == 0, (name, m, k, n)
    nm = m // tm
    blocks, first, buffer = part_of if part_of is not None else (None, 0, None)
    if stack_out:
        out_spec = pl.BlockSpec((None, tk, tn), lambda i, j, l: (j + first, i, 0))
        out_shape = jax.ShapeDtypeStruct((blocks or n // tn, k, tn), out_dtype)
    else:
        out_spec = pl.BlockSpec((tk, tn), lambda i, j, l: (i + first, j))
        out_shape = jax.ShapeDtypeStruct((blocks * tk if blocks else k, n), out_dtype)

    def body(a_ref, b_ref, *rest):
        o_ref, acc = rest[-2:]
        mi = pl.program_id(2)

        @pl.when(mi == 0)
        def _():
            acc[...] = jnp.zeros_like(acc)

        acc[...] += _dot(a_ref[...].astype(BF16), b_ref[...].astype(BF16), TN)

        @pl.when(mi == nm - 1)
        def _():
            o_ref[...] = acc[...].astype(out_dtype)

    in_specs = [pl.BlockSpec((tm, tk), lambda i, j, l: (l, i)), pl.BlockSpec((tm, tn), lambda i, j, l: (l, j))]
    args = [a, b]
    if after is not None:
        in_specs.append(pl.BlockSpec(memory_space=pl.ANY))
        args.append(after)
    aliases = {}
    if buffer is not None:
        aliases = {len(args): 0}
        in_specs.append(pl.BlockSpec(memory_space=pl.ANY))
        args.append(buffer)
    return pl.pallas_call(
        body, name=name, grid=(k // tk, n // tn, nm), in_specs=in_specs,
        out_specs=out_spec, out_shape=out_shape, input_output_aliases=aliases,
        scratch_shapes=[pltpu.VMEM((tk, tn), F32)],
        compiler_params=_params(("parallel", "parallel", "arbitrary")),
    )(*args)


def _rms_fwd(x, w, *, after=None, name, tm=512):
    s, d = x.shape
    tm = min(tm, s)
    extra = [] if after is None else [after]

    def body(x_ref, w_ref, *rest):
        o_ref = rest[-1]
        xv = x_ref[...]
        r = lax.rsqrt(jnp.mean(xv * xv, axis=-1, keepdims=True) + NORM_EPS)
        o_ref[...] = (xv * r * w_ref[...]).astype(BF16)

    return pl.pallas_call(
        body, name=name, grid=(s // tm,),
        in_specs=[pl.BlockSpec((tm, d), lambda i: (i, 0)), pl.BlockSpec((1, d), lambda i: (0, 0))]
        + [pl.BlockSpec(memory_space=pl.ANY)] * len(extra),
        out_specs=pl.BlockSpec((tm, d), lambda i: (i, 0)),
        out_shape=jax.ShapeDtypeStruct((s, d), BF16),
        compiler_params=_params(("parallel",)),
    )(x, w, *extra)


def _rms_bwd(x, w, dn, dres, *, bf16_copy, name, tm=512):
    s, d = x.shape
    tm = min(tm, s)

    def body(x_ref, w_ref, dn_ref, dres_ref, dx_ref, *rest):
        dw_ref = rest[-1]

        @pl.when(pl.program_id(0) == 0)
        def _():
            dw_ref[...] = jnp.zeros_like(dw_ref)

        xv = x_ref[...]
        r = lax.rsqrt(jnp.mean(xv * xv, axis=-1, keepdims=True) + NORM_EPS)
        xhat = xv * r
        dnv = dn_ref[...].astype(F32)
        dxhat = dnv * w_ref[...]
        dx = dres_ref[...] + r * (dxhat - xhat * jnp.mean(dxhat * xhat, axis=-1, keepdims=True))
        dx_ref[...] = dx
        if bf16_copy:
            rest[0][...] = dx.astype(BF16)
        dw_ref[...] += jnp.sum(dnv * xhat, axis=0, keepdims=True)

    tile = pl.BlockSpec((tm, d), lambda i: (i, 0))
    row = pl.BlockSpec((1, d), lambda i: (0, 0))
    copies = [jax.ShapeDtypeStruct((s, d), BF16)] if bf16_copy else []
    return pl.pallas_call(
        body, name=name, grid=(s // tm,),
        in_specs=[tile, row, tile, tile], out_specs=[tile] + [tile] * len(copies) + [row],
        out_shape=[jax.ShapeDtypeStruct((s, d), F32)] + copies + [jax.ShapeDtypeStruct((1, d), F32)],
        compiler_params=_params(("arbitrary",)),
    )(x, w, dn, dres)


def _final_fwd_bwd(h2, wf, target, *, name, tm=512):
    s, d = h2.shape
    tm = min(tm, s)

    def body(h_ref, w_ref, t_ref, loss_ref, dh_ref, dhb_ref, dw_ref):
        @pl.when(pl.program_id(0) == 0)
        def _():
            dw_ref[...] = jnp.zeros_like(dw_ref)
            loss_ref[...] = jnp.zeros_like(loss_ref)

        hv = h_ref[...]
        r = lax.rsqrt(jnp.mean(hv * hv, axis=-1, keepdims=True) + NORM_EPS)
        xhat = hv * r
        err = xhat * w_ref[...] - t_ref[...]
        per_tok = jnp.mean(err * err, axis=-1, keepdims=True)
        loss_ref[...] += 0.5 * jnp.sum(per_tok, axis=0, keepdims=True)
        dy = err * (1.0 / d)
        dxhat = dy * w_ref[...]
        dh = r * (dxhat - xhat * jnp.mean(dxhat * xhat, axis=-1, keepdims=True))
        dh_ref[...] = dh
        dhb_ref[...] = dh.astype(BF16)
        dw_ref[...] += jnp.sum(dy * xhat, axis=0, keepdims=True)

    tile = pl.BlockSpec((tm, d), lambda i: (i, 0))
    row = pl.BlockSpec((1, d), lambda i: (0, 0))
    return pl.pallas_call(
        body, name=name, grid=(s // tm,),
        in_specs=[tile, row, tile],
        out_specs=[pl.BlockSpec((1, 1), lambda i: (0, 0)), tile, tile, row],
        out_shape=[jax.ShapeDtypeStruct((1, 1), F32), jax.ShapeDtypeStruct((s, d), F32),
                   jax.ShapeDtypeStruct((s, d), BF16), jax.ShapeDtypeStruct((1, d), F32)],
        compiler_params=_params(("arbitrary",)),
    )(h2, wf, target)


CONV_ROWS = 256
CONV_ROWS_FWD = 512
HALO = 8


def _rows_with_halo(ref, r0, rows, s, before, after):
    tile = 16 if ref.dtype == BF16 else HALO
    parts = []
    if before:
        prev = ref[pl.ds(pl.multiple_of(jnp.maximum(r0 - tile, 0), tile), tile), :].astype(F32)[tile - HALO:]
        parts.append(jnp.where(r0 > 0, prev, 0.0))
    parts.append(ref[pl.ds(r0, rows), :].astype(F32))
    if after:
        nxt = ref[pl.ds(pl.multiple_of(jnp.minimum(r0 + rows, s - tile), tile), tile), :].astype(F32)[:HALO]
        parts.append(jnp.where(r0 + rows < s, nxt, 0.0))
    return jnp.concatenate(parts, axis=0) if len(parts) > 1 else parts[0]


def _window(x_ref, r0, s, after):
    return _rows_with_halo(x_ref, r0, CONV_ROWS_FWD, s, True, after).astype(F32)


def _shifted(window, k, rows):
    if k == 0:
        return window[HALO:HALO + rows]
    return pltpu.roll(window, k, 0)[HALO:HALO + rows]


def _conv_taps(window, w_ref, kk, rows):
    acc = None
    for i in range(kk):
        term = w_ref[i:i + 1, :] * _shifted(window, kk - 1 - i, rows)
        acc = term if acc is None else acc + term
    return acc


def _row_loop(rows, step):
    def body(r, carry):
        return step(pl.multiple_of(r * rows, rows), carry)
    return body


def _conv_bwd_rows(x, dpe, w_ref, kk):
    dp = dpe[:CONV_ROWS]
    dx = None
    dws = []
    for i in range(kk):
        k = kk - 1 - i
        later = dp if k == 0 else pltpu.roll(dpe, dpe.shape[0] - k, 0)[:CONV_ROWS]
        dws.append(jnp.sum(later * x, axis=0, keepdims=True))
        term = w_ref[i:i + 1, :] * later
        dx = term if dx is None else dx + term
    return dx, dws, jnp.sum(dp, axis=0, keepdims=True)


def _conv_a_fwd(xraw, w, b, *, name, tc=128):
    s, c = xraw.shape
    kk = 4

    def body(x_ref, w_ref, b_ref, o_ref, pre_ref):
        def step(r0, carry):
            pre = _conv_taps(_window(x_ref, r0, s, False), w_ref, kk, CONV_ROWS_FWD) + b_ref[...]
            o_ref[pl.ds(r0, CONV_ROWS_FWD), :] = pre * _sigmoid(pre)
            pre_ref[pl.ds(r0, CONV_ROWS_FWD), :] = pre.astype(BF16)
            return carry

        lax.fori_loop(0, s // CONV_ROWS_FWD, _row_loop(CONV_ROWS_FWD, step), 0)

    col = pl.BlockSpec((s, tc), lambda j: (0, j))
    return pl.pallas_call(
        body, name=name, grid=(c // tc,),
        in_specs=[col, pl.BlockSpec((8, tc), lambda j: (0, j)), pl.BlockSpec((1, tc), lambda j: (0, j))],
        out_specs=[col, col], out_shape=[jax.ShapeDtypeStruct((s, c), F32), jax.ShapeDtypeStruct((s, c), BF16)],
        compiler_params=_params(("parallel",)),
    )(xraw, w, b)


def _conv_a_bwd(xraw, pre, w, dy, *, name, tc=128):
    s, c = xraw.shape
    kk = 4

    def body(x_ref, pre_ref, w_ref, dy_ref, dx_ref, dw_ref, db_ref):
        def step(r0, carry):
            pre = _rows_with_halo(pre_ref, r0, CONV_ROWS, s, False, True)
            sg = _sigmoid(pre)
            dpe = _rows_with_halo(dy_ref, r0, CONV_ROWS, s, False, True) * (sg * (1.0 + pre * (1.0 - sg)))
            dx, dws, db = _conv_bwd_rows(x_ref[pl.ds(r0, CONV_ROWS), :].astype(F32), dpe, w_ref, kk)
            dx_ref[pl.ds(r0, CONV_ROWS), :] = dx.astype(BF16)
            return tuple(acc + new for acc, new in zip(carry, dws + [db]))

        zero = jnp.zeros((1, tc), F32)
        sums = lax.fori_loop(0, s // CONV_ROWS, _row_loop(CONV_ROWS, step), (zero,) * (kk + 1))
        db_ref[...] = sums[kk]
        dw_ref[...] = jnp.concatenate(list(sums[:kk]) + [jnp.zeros((8 - kk, tc), F32)], axis=0)

    col = pl.BlockSpec((s, tc), lambda j: (0, j))
    w8 = pl.BlockSpec((8, tc), lambda j: (0, j))
    row = pl.BlockSpec((1, tc), lambda j: (0, j))
    return pl.pallas_call(
        body, name=name, grid=(c // tc,),
        in_specs=[col, col, w8, col], out_specs=[col, w8, row],
        out_shape=[jax.ShapeDtypeStruct((s, c), BF16), jax.ShapeDtypeStruct((8, c), F32),
                   jax.ShapeDtypeStruct((1, c), F32)],
        compiler_params=_params(("parallel",)),
    )(xraw, pre, w, dy)


def _conv_f_fwd(up_raw, w, b, *, name, tc=128):
    s, c2 = up_raw.shape
    c = c2 // 2
    nb = c // tc
    kk = 3

    def body(xa_ref, xv_ref, wa_ref, wv_ref, ba_ref, bv_ref, o_ref, a_out, v_out):
        def step(r0, carry):
            a = _conv_taps(_window(xa_ref, r0, s, False), wa_ref, kk, CONV_ROWS_FWD) + ba_ref[...]
            v = _conv_taps(_window(xv_ref, r0, s, False), wv_ref, kk, CONV_ROWS_FWD) + bv_ref[...]
            o_ref[pl.ds(r0, CONV_ROWS_FWD), :] = (a * _sigmoid(a) * v).astype(BF16)
            a_out[pl.ds(r0, CONV_ROWS_FWD), :] = a.astype(BF16)
            v_out[pl.ds(r0, CONV_ROWS_FWD), :] = v.astype(BF16)
            return carry

        lax.fori_loop(0, s // CONV_ROWS_FWD, _row_loop(CONV_ROWS_FWD, step), 0)

    col_a = pl.BlockSpec((s, tc), lambda j: (0, j))
    col_v = pl.BlockSpec((s, tc), lambda j: (0, j + nb))
    half = jax.ShapeDtypeStruct((s, c), BF16)
    return pl.pallas_call(
        body, name=name, grid=(nb,),
        in_specs=[col_a, col_v, pl.BlockSpec((8, tc), lambda j: (0, j)), pl.BlockSpec((8, tc), lambda j: (0, j + nb)),
                  pl.BlockSpec((1, tc), lambda j: (0, j)), pl.BlockSpec((1, tc), lambda j: (0, j + nb))],
        out_specs=[col_a, col_a, col_a], out_shape=[half, half, half],
        compiler_params=_params(("parallel",)),
    )(up_raw, up_raw, w, w, b, b)


def _conv_f_bwd(up_raw, a_pre, v_pre, w, dact, *, name, tc=128):
    s, c2 = up_raw.shape
    c = c2 // 2
    nb = c // tc
    kk = 3

    def body(xa_ref, xv_ref, a_ref, v_ref, wa_ref, wv_ref, d_ref,
             dxa_ref, dxv_ref, dwa_ref, dwv_ref, dba_ref, dbv_ref):
        def step(r0, carry):
            a = _rows_with_halo(a_ref, r0, CONV_ROWS, s, False, True)
            v = _rows_with_halo(v_ref, r0, CONV_ROWS, s, False, True)
            sg = _sigmoid(a)
            d = _rows_with_halo(d_ref, r0, CONV_ROWS, s, False, True)
            rows = pl.ds(r0, CONV_ROWS)
            dxa, dwas, dba = _conv_bwd_rows(xa_ref[rows, :].astype(F32), d * v * (sg * (1.0 + a * (1.0 - sg))),
                                            wa_ref, kk)
            dxv, dwvs, dbv = _conv_bwd_rows(xv_ref[rows, :].astype(F32), d * (a * sg), wv_ref, kk)
            dxa_ref[pl.ds(r0, CONV_ROWS), :] = dxa.astype(BF16)
            dxv_ref[pl.ds(r0, CONV_ROWS), :] = dxv.astype(BF16)
            return tuple(acc + new for acc, new in zip(carry, dwas + [dba] + dwvs + [dbv]))

        zero = jnp.zeros((1, tc), F32)
        sums = lax.fori_loop(0, s // CONV_ROWS, _row_loop(CONV_ROWS, step), (zero,) * (2 * kk + 2))
        pad = [jnp.zeros((8 - kk, tc), F32)]
        dwa_ref[...] = jnp.concatenate(list(sums[:kk]) + pad, axis=0)
        dba_ref[...] = sums[kk]
        dwv_ref[...] = jnp.concatenate(list(sums[kk + 1:2 * kk + 1]) + pad, axis=0)
        dbv_ref[...] = sums[2 * kk + 1]

    col_a = pl.BlockSpec((s, tc), lambda j: (0, j))
    col_v = pl.BlockSpec((s, tc), lambda j: (0, j + nb))
    w_a = pl.BlockSpec((8, tc), lambda j: (0, j))
    w_v = pl.BlockSpec((8, tc), lambda j: (0, j + nb))
    r_a = pl.BlockSpec((1, tc), lambda j: (0, j))
    r_v = pl.BlockSpec((1, tc), lambda j: (0, j + nb))
    outs = pl.pallas_call(
        body, name=name, grid=(nb,),
        in_specs=[col_a, col_v, col_a, col_a, w_a, w_v, col_a],
        out_specs=[col_a, col_a, w_a, w_a, r_a, r_a],
        out_shape=[jax.ShapeDtypeStruct((s, c), BF16), jax.ShapeDtypeStruct((s, c), BF16),
                   jax.ShapeDtypeStruct((8, c), F32), jax.ShapeDtypeStruct((8, c), F32),
                   jax.ShapeDtypeStruct((1, c), F32), jax.ShapeDtypeStruct((1, c), F32)],
        compiler_params=_params(("parallel",)),
    )(up_raw, up_raw, a_pre, v_pre, w, w, dact)
    return outs


def _tri_masks():
    row = lax.broadcasted_iota(jnp.int32, (CHUNK, CHUNK), 0)
    col = lax.broadcasted_iota(jnp.int32, (CHUNK, CHUNK), 1)
    return row >= col, row <= col


def _ssd_fwd(xbc, dt_raw, z, dt_bias, a_log, a_log_x, d_skip_x, norm_w, expand, *, name):
    s = xbc.shape[0]
    nc = s // CHUNK

    def body(xbc_ref, dtr_ref, z_ref, dtb_ref, alog_ref, alogx_ref, dskx_ref, nw_ref, e_ref,
             y_ref, ya_ref, st_ref, state):
        @pl.when(pl.program_id(0) == 0)
        def _():
            state[...] = jnp.zeros_like(state)

        st_ref[0] = state[...]
        lower, _ = _tri_masks()
        dt = _softplus(dtr_ref[...] + dtb_ref[...])
        adt = dt * (-jnp.exp(alog_ref[...]))
        acum = _dot_exact_lhs(lower.astype(BF16), _split3(adt))
        acum_t = acum.T
        dt_terms, acum_terms = _split3(dt), _split3(acum)
        for g in range(SSD_GROUPS):
            sl = slice(GROUP_COLS * g, GROUP_COLS * (g + 1))
            dt_x = _dot_terms(dt_terms[:2], e_ref[:, sl])
            acum_x = _dot_terms(acum_terms, e_ref[:, sl])
            tot_x = jnp.sum(dt_x * (-jnp.exp(alogx_ref[:, sl])), axis=0, keepdims=True)
            xs = xbc_ref[:, sl]
            xdt = xs * dt_x
            xdt_b = xdt.astype(BF16)
            bg = xbc_ref[:, SSD_D_INNER + SSD_STATE * g:SSD_D_INNER + SSD_STATE * (g + 1)].astype(BF16)
            cg = xbc_ref[:, SSD_D_INNER + SSD_BC + SSD_STATE * g:SSD_D_INNER + SSD_BC + SSD_STATE * (g + 1)].astype(BF16)
            cb = _dot(cg, bg, NT)
            st_g = state[:, sl]
            y_off = _dot(cg, st_g.astype(BF16)) * jnp.exp(acum_x)
            parts = []
            for r in range(SSD_HEADS_PER_GROUP):
                h = SSD_HEADS_PER_GROUP * g + r
                dec = jnp.exp(jnp.where(lower, acum[:, h:h + 1] - acum_t[h:h + 1, :], -jnp.inf))
                parts.append(_dot((cb * dec).astype(BF16), xdt_b[:, SSD_HEAD_DIM * r:SSD_HEAD_DIM * (r + 1)]))
            y_ref[:, sl] = jnp.concatenate(parts, axis=1) + y_off + dskx_ref[:, sl] * xs
            wgt = (xdt * jnp.exp(tot_x - acum_x)).astype(BF16)
            state[:, sl] = st_g * jnp.exp(tot_x) + _dot(bg, wgt, TN)
        zv = z_ref[...].astype(F32)
        q = y_ref[...] * (zv * _sigmoid(zv))
        r = lax.rsqrt(jnp.mean(q * q, axis=-1, keepdims=True) + NORM_EPS)
        ya_ref[...] = (q * r * nw_ref[...]).astype(BF16)

    def chunk(w):
        return pl.BlockSpec((CHUNK, w), lambda c: (c, 0))

    def const(shape):
        return pl.BlockSpec(shape, lambda c: (0,) * len(shape))

    return pl.pallas_call(
        body, name=name, grid=(nc,),
        in_specs=[chunk(SSD_XBC), chunk(LANES), chunk(SSD_D_INNER), const((1, LANES)), const((1, LANES)),
                  const((1, SSD_D_INNER)), const((1, SSD_D_INNER)), const((1, SSD_D_INNER)),
                  const((LANES, SSD_D_INNER))],
        out_specs=[chunk(SSD_D_INNER), chunk(SSD_D_INNER),
                   pl.BlockSpec((1, SSD_STATE, SSD_D_INNER), lambda c: (c, 0, 0))],
        out_shape=[jax.ShapeDtypeStruct((s, SSD_D_INNER), F32), jax.ShapeDtypeStruct((s, SSD_D_INNER), BF16),
                   jax.ShapeDtypeStruct((nc, SSD_STATE, SSD_D_INNER), F32)],
        scratch_shapes=[pltpu.VMEM((SSD_STATE, SSD_D_INNER), F32)],
        compiler_params=_params(("arbitrary",)),
    )(xbc, dt_raw, z, dt_bias, a_log, a_log_x, d_skip_x, norm_w, expand)


def _ssd_bwd(dya, y, z, xbc, dt_raw, states, dt_bias, a_log, a_log_x, d_skip_x, norm_w, expand, expand_t, *, name):
    s = xbc.shape[0]
    nc = s // CHUNK

    def body(dya_ref, y_ref, z_ref, xbc_ref, dtr_ref, stp_ref, dtb_ref, alog_ref, alogx_ref, dskx_ref, nw_ref,
             e_ref, et_ref, dz_ref, dxbc_ref, ddt_ref, dnw_ref, ddsk_ref, dalog_ref, ddtb_ref,
             dstate, dy_sc, dskcol):
        i = pl.program_id(0)

        @pl.when(i == 0)
        def _():
            dstate[...] = jnp.zeros_like(dstate)
            dskcol[...] = jnp.zeros_like(dskcol)
            dnw_ref[...] = jnp.zeros_like(dnw_ref)
            dalog_ref[...] = jnp.zeros_like(dalog_ref)
            ddtb_ref[...] = jnp.zeros_like(ddtb_ref)
            ddsk_ref[...] = jnp.zeros_like(ddsk_ref)

        lower, upper = _tri_masks()
        rows = lax.broadcasted_iota(jnp.int32, (CHUNK, LANES), 0)
        pre = dtr_ref[...] + dtb_ref[...]
        dt = _softplus(pre)
        a = -jnp.exp(alog_ref[...])
        acum = _dot_exact_lhs(lower.astype(BF16), _split3(dt * a))
        acum_t = acum.T
        dt_terms, acum_terms = _split3(dt), _split3(acum)

        yv = y_ref[...]
        zv = z_ref[...].astype(F32)
        sz = _sigmoid(zv)
        silu_z = zv * sz
        q = yv * silu_z
        r = lax.rsqrt(jnp.mean(q * q, axis=-1, keepdims=True) + NORM_EPS)
        qhat = q * r
        dyav = dya_ref[...]
        dqhat = dyav * nw_ref[...]
        dnw_ref[...] += jnp.sum(dyav * qhat, axis=0, keepdims=True)
        dq = r * (dqhat - qhat * jnp.mean(dqhat * qhat, axis=-1, keepdims=True))
        dy_sc[...] = dq * silu_z
        dz_ref[...] = (dq * yv * (sz * (1.0 + zv * (1.0 - sz)))).astype(BF16)

        da_cum = jnp.zeros((CHUNK, LANES), F32)
        ddt = jnp.zeros((CHUNK, LANES), F32)
        for g in range(SSD_GROUPS):
            sl = slice(GROUP_COLS * g, GROUP_COLS * (g + 1))
            et_g = et_ref[sl, :]
            dt_x = _dot_terms(dt_terms[:2], e_ref[:, sl])
            acum_x = _dot_terms(acum_terms, e_ref[:, sl])
            tot_x = jnp.sum(dt_x * (-jnp.exp(alogx_ref[:, sl])), axis=0, keepdims=True)
            e_tot = jnp.exp(tot_x)
            dec_s = jnp.exp(tot_x - acum_x)
            xs = xbc_ref[:, sl]
            xdt = xs * dt_x
            xdt_b = xdt.astype(BF16)
            dy = dy_sc[:, sl]
            dy_b = dy.astype(BF16)
            dskx = dskx_ref[:, sl]
            y_ssd = y_ref[:, sl] - dskx * xs
            dskcol[:, sl] += jnp.sum(dy * xs, axis=0, keepdims=True)
            bg = xbc_ref[:, SSD_D_INNER + SSD_STATE * g:SSD_D_INNER + SSD_STATE * (g + 1)].astype(BF16)
            cg = xbc_ref[:, SSD_D_INNER + SSD_BC + SSD_STATE * g:SSD_D_INNER + SSD_BC + SSD_STATE * (g + 1)].astype(BF16)
            cb_t = _dot(bg, cg, NT)
            sp = stp_ref[0, :, sl]
            ds_g = dstate[:, sl]
            ds_b = ds_g.astype(BF16)
            dye_b = (dy * jnp.exp(acum_x)).astype(BF16)
            dc = _dot(dye_b, sp.astype(BF16), NT)
            dxdt_state = dec_s * _dot(bg, ds_b)
            db = _dot((xdt * dec_s).astype(BF16), ds_b, NT)
            dcb_t = jnp.zeros((CHUNK, CHUNK), F32)
            parts = []
            for rr in range(SSD_HEADS_PER_GROUP):
                h = SSD_HEADS_PER_GROUP * g + rr
                hs = slice(SSD_HEAD_DIM * rr, SSD_HEAD_DIM * (rr + 1))
                dec_t = jnp.exp(jnp.where(upper, acum_t[h:h + 1, :] - acum[:, h:h + 1], -jnp.inf))
                parts.append(_dot((cb_t * dec_t).astype(BF16), dy_b[:, hs]))
                dcb_t = dcb_t + _dot(xdt_b[:, hs], dy_b[:, hs], NT) * dec_t
            dxdt = jnp.concatenate(parts, axis=1) + dxdt_state
            dcb_tb = dcb_t.astype(BF16)
            dc = dc + _dot(dcb_tb, bg, TN)
            db = db + _dot(dcb_tb, cg)
            tot_col = jnp.sum(ds_g * sp, axis=0, keepdims=True) * e_tot + jnp.sum(dxdt_state * xdt, axis=0, keepdims=True)
            d_tot = _dot_terms(_split3(jnp.broadcast_to(tot_col, (8, GROUP_COLS))), et_g)
            d_tot = jnp.max(d_tot, axis=0, keepdims=True)
            pair_sums = dy_b.astype(F32) * y_ssd - xdt_b.astype(F32) * dxdt
            da_cum = da_cum + _dot_terms(_split3(pair_sums), et_g) + jnp.where(rows == CHUNK - 1, d_tot, 0.0)
            ddt = ddt + _dot_terms(_split3(dxdt * xs)[:2], et_g)
            dxbc_ref[:, sl] = dy * dskx + dxdt * dt_x
            dxbc_ref[:, SSD_D_INNER + SSD_STATE * g:SSD_D_INNER + SSD_STATE * (g + 1)] = db
            dxbc_ref[:, SSD_D_INNER + SSD_BC + SSD_STATE * g:SSD_D_INNER + SSD_BC + SSD_STATE * (g + 1)] = dc
            dstate[:, sl] = e_tot * ds_g + _dot(cg, dye_b, TN)

        dadt = _dot_exact_lhs(upper.astype(BF16), _split3(da_cum))
        ddt = ddt + dadt * a
        dalog_ref[...] += jnp.sum(dadt * dt, axis=0, keepdims=True)
        dpre = ddt * _sigmoid(pre)
        ddtb_ref[...] += jnp.sum(dpre, axis=0, keepdims=True)
        ddt_ref[...] = dpre.astype(BF16)

        @pl.when(i == nc - 1)
        def _():
            dalog_ref[...] = dalog_ref[...] * a
            dsk = _dot_terms(_split3(jnp.broadcast_to(dskcol[...], (8, SSD_D_INNER))), et_ref[...])
            ddsk_ref[...] = jnp.max(dsk, axis=0, keepdims=True)

    def chunk(w):
        return pl.BlockSpec((CHUNK, w), lambda i: (nc - 1 - i, 0))

    def const(shape):
        return pl.BlockSpec(shape, lambda i: (0,) * len(shape))

    return pl.pallas_call(
        body, name=name, grid=(nc,),
        in_specs=[chunk(SSD_D_INNER), chunk(SSD_D_INNER), chunk(SSD_D_INNER), chunk(SSD_XBC), chunk(LANES),
                  pl.BlockSpec((1, SSD_STATE, SSD_D_INNER), lambda i: (nc - 1 - i, 0, 0)),
                  const((1, LANES)), const((1, LANES)), const((1, SSD_D_INNER)), const((1, SSD_D_INNER)),
                  const((1, SSD_D_INNER)), const((LANES, SSD_D_INNER)), const((SSD_D_INNER, LANES))],
        out_specs=[chunk(SSD_D_INNER), chunk(SSD_XBC), chunk(LANES), const((1, SSD_D_INNER)), const((1, LANES)),
                   const((1, LANES)), const((1, LANES))],
        out_shape=[jax.ShapeDtypeStruct((s, SSD_D_INNER), BF16), jax.ShapeDtypeStruct((s, SSD_XBC), F32),
                   jax.ShapeDtypeStruct((s, LANES), BF16), jax.ShapeDtypeStruct((1, SSD_D_INNER), F32),
                   jax.ShapeDtypeStruct((1, LANES), F32), jax.ShapeDtypeStruct((1, LANES), F32),
                   jax.ShapeDtypeStruct((1, LANES), F32)],
        scratch_shapes=[pltpu.VMEM((SSD_STATE, SSD_D_INNER), F32), pltpu.VMEM((CHUNK, SSD_D_INNER), F32),
                        pltpu.VMEM((1, SSD_D_INNER), F32)],
        compiler_params=_params(("arbitrary",)),
    )(dya, y, z, xbc, dt_raw, states, dt_bias, a_log, a_log_x, d_skip_x, norm_w, expand, expand_t)


GELU_K = math.sqrt(2.0 / math.pi)
GELU_C = 0.044715


def _gelu(x):
    return 0.5 * x * (1.0 + jnp.tanh(GELU_K * (x + GELU_C * x * x * x)))


def _gelu_grad(x):
    t = jnp.tanh(GELU_K * (x + GELU_C * x * x * x))
    return 0.5 * (1.0 + t) + 0.5 * x * (1.0 - t * t) * (GELU_K * (1.0 + 3.0 * GELU_C * x * x))


def _sgu_pre(uv_ref, uvb_ref, lnw_ref, lnb_ref):
    uv = uv_ref[...].astype(F32) + uvb_ref[...]
    guv = _gelu(uv)
    u = guv[:, :SGU_WIDTH]
    v = guv[:, SGU_WIDTH:]
    mu = jnp.mean(v, axis=-1, keepdims=True)
    vc = v - mu
    rstd = lax.rsqrt(jnp.mean(vc * vc, axis=-1, keepdims=True) + LN_EPS)
    vhat = vc * rstd
    vn = vhat * lnw_ref[...] + lnb_ref[...]
    return uv, u, vhat, rstd, vn


def _sgu_fwd(uv_raw, uv_b, ln_w, ln_b, w_sp, b_sp_t, *, name):
    s = uv_raw.shape[0]
    nc = s // CHUNK

    def body(uv_ref, uvb_ref, lnw_ref, lnb_ref, w_ref, bt_ref, o_ref):
        lower, _ = _tri_masks()
        _, u, _, _, vn = _sgu_pre(uv_ref, uvb_ref, lnw_ref, lnb_ref)
        vn_b = vn.astype(BF16)
        bt = bt_ref[...]
        for g in range(SGU_GROUPS):
            gs = slice(LANES * g, LANES * (g + 1))
            wc = jnp.where(lower, w_ref[g], 0.0).astype(BF16)
            mixed = _dot(wc, vn_b[:, gs]) + bt[:, g:g + 1]
            o_ref[:, gs] = (u[:, gs] * mixed).astype(BF16)

    def const(shape):
        return pl.BlockSpec(shape, lambda c: (0,) * len(shape))

    return pl.pallas_call(
        body, name=name, grid=(nc,),
        in_specs=[pl.BlockSpec((CHUNK, 2 * SGU_WIDTH), lambda c: (c, 0)), const((1, 2 * SGU_WIDTH)),
                  const((1, SGU_WIDTH)), const((1, SGU_WIDTH)), const((SGU_GROUPS, CHUNK, CHUNK)),
                  const((CHUNK, LANES))],
        out_specs=pl.BlockSpec((CHUNK, SGU_WIDTH), lambda c: (c, 0)),
        out_shape=jax.ShapeDtypeStruct((s, SGU_WIDTH), BF16),
        compiler_params=_params(("parallel",)),
    )(uv_raw, uv_b, ln_w, ln_b, w_sp, b_sp_t)


def _sgu_bwd(uv_raw, dyb, uv_b, ln_w, ln_b, w_sp, b_sp_t, group_sum, *, name):
    s = uv_raw.shape[0]
    nc = s // CHUNK

    def body(uv_ref, dy_ref, uvb_ref, lnw_ref, lnb_ref, w_ref, bt_ref, gsum_ref,
             duv_ref, dw_ref, dbt_ref, dlnw_ref, dlnb_ref, duvb_ref):
        @pl.when(pl.program_id(0) == 0)
        def _():
            dw_ref[...] = jnp.zeros_like(dw_ref)
            dbt_ref[...] = jnp.zeros_like(dbt_ref)
            dlnw_ref[...] = jnp.zeros_like(dlnw_ref)
            dlnb_ref[...] = jnp.zeros_like(dlnb_ref)
            duvb_ref[...] = jnp.zeros_like(duvb_ref)

        lower, _ = _tri_masks()
        uv, u, vhat, rstd, vn = _sgu_pre(uv_ref, uvb_ref, lnw_ref, lnb_ref)
        vn_b = vn.astype(BF16)
        bt = bt_ref[...]
        dy = dy_ref[...].astype(F32)
        du_parts, dvn_parts, dmix_parts = [], [], []
        for g in range(SGU_GROUPS):
            gs = slice(LANES * g, LANES * (g + 1))
            wc = jnp.where(lower, w_ref[g], 0.0).astype(BF16)
            mixed = _dot(wc, vn_b[:, gs]) + bt[:, g:g + 1]
            du_parts.append(dy[:, gs] * mixed)
            dmix = dy[:, gs] * u[:, gs]
            dmix_b = dmix.astype(BF16)
            dmix_parts.append(dmix)
            dw_ref[g] += jnp.where(lower, _dot(dmix_b, vn_b[:, gs], NT), 0.0)
            dvn_parts.append(_dot(wc, dmix_b, TN))
        dmixed = jnp.concatenate(dmix_parts, axis=1)
        dbt_ref[...] += _dot_terms(_split3(dmixed), gsum_ref[...])
        dvn = jnp.concatenate(dvn_parts, axis=1)
        dlnw_ref[...] += jnp.sum(dvn * vhat, axis=0, keepdims=True)
        dlnb_ref[...] += jnp.sum(dvn, axis=0, keepdims=True)
        dvhat = dvn * lnw_ref[...]
        dv = rstd * (dvhat - jnp.mean(dvhat, axis=-1, keepdims=True)
                     - vhat * jnp.mean(dvhat * vhat, axis=-1, keepdims=True))
        dguv = jnp.concatenate(du_parts + [dv], axis=1)
        duv = dguv * _gelu_grad(uv)
        duvb_ref[...] += jnp.sum(duv, axis=0, keepdims=True)
        duv_ref[...] = duv.astype(BF16)

    def const(shape):
        return pl.BlockSpec(shape, lambda c: (0,) * len(shape))

    return pl.pallas_call(
        body, name=name, grid=(nc,),
        in_specs=[pl.BlockSpec((CHUNK, 2 * SGU_WIDTH), lambda c: (c, 0)),
                  pl.BlockSpec((CHUNK, SGU_WIDTH), lambda c: (c, 0)), const((1, 2 * SGU_WIDTH)),
                  const((1, SGU_WIDTH)), const((1, SGU_WIDTH)), const((SGU_GROUPS, CHUNK, CHUNK)),
                  const((CHUNK, LANES)), const((SGU_WIDTH, LANES))],
        out_specs=[pl.BlockSpec((CHUNK, 2 * SGU_WIDTH), lambda c: (c, 0)), const((SGU_GROUPS, CHUNK, CHUNK)),
                   const((CHUNK, LANES)), const((1, SGU_WIDTH)), const((1, SGU_WIDTH)), const((1, 2 * SGU_WIDTH))],
        out_shape=[jax.ShapeDtypeStruct((s, 2 * SGU_WIDTH), BF16),
                   jax.ShapeDtypeStruct((SGU_GROUPS, CHUNK, CHUNK), F32), jax.ShapeDtypeStruct((CHUNK, LANES), F32),
                   jax.ShapeDtypeStruct((1, SGU_WIDTH), F32), jax.ShapeDtypeStruct((1, SGU_WIDTH), F32),
                   jax.ShapeDtypeStruct((1, 2 * SGU_WIDTH), F32)],
        compiler_params=_params(("arbitrary",)),
    )(uv_raw, dyb, uv_b, ln_w, ln_b, w_sp, b_sp_t, group_sum)


def _gate_fwd(gates_raw, b_gate, p_a, p_b, *, name, tm=512):
    s = p_a.shape[0]
    tm = min(tm, s)

    def body(ga_ref, gb_ref, ba_ref, bb_ref, pa_ref, pb_ref, o_ref):
        ga = _sigmoid(ga_ref[...].astype(F32) + ba_ref[...])
        gb = _sigmoid(gb_ref[...].astype(F32) + bb_ref[...])
        o_ref[...] = (ga * pa_ref[...].astype(F32) + gb * pb_ref[...].astype(F32)).astype(BF16)

    t_a = pl.BlockSpec((tm, D_MODEL), lambda i: (i, 0))
    t_b = pl.BlockSpec((tm, D_MODEL), lambda i: (i, 1))
    r_a = pl.BlockSpec((1, D_MODEL), lambda i: (0, 0))
    r_b = pl.BlockSpec((1, D_MODEL), lambda i: (0, 1))
    return pl.pallas_call(
        body, name=name, grid=(s // tm,),
        in_specs=[t_a, t_b, r_a, r_b, t_a, t_a], out_specs=t_a,
        out_shape=jax.ShapeDtypeStruct((s, D_MODEL), BF16),
        compiler_params=_params(("parallel",)),
    )(gates_raw, gates_raw, b_gate, b_gate, p_a, p_b)


def _gate_bwd(gates_raw, b_gate, p_a, p_b, dm, *, name, tm=512):
    s = p_a.shape[0]
    tm = min(tm, s)

    def body(ga_ref, gb_ref, ba_ref, bb_ref, pa_ref, pb_ref, dm_ref, dpa_ref, dpb_ref, dga_ref, dgb_ref,
             dba_ref, dbb_ref):
        @pl.when(pl.program_id(0) == 0)
        def _():
            dba_ref[...] = jnp.zeros_like(dba_ref)
            dbb_ref[...] = jnp.zeros_like(dbb_ref)

        d = dm_ref[...].astype(F32)
        for g_ref, b_ref, p_ref, dp_ref, dg_ref, db_ref in ((ga_ref, ba_ref, pa_ref, dpa_ref, dga_ref, dba_ref),
                                                            (gb_ref, bb_ref, pb_ref, dpb_ref, dgb_ref, dbb_ref)):
            sg = _sigmoid(g_ref[...].astype(F32) + b_ref[...])
            dp_ref[...] = (d * sg).astype(BF16)
            dg = d * p_ref[...].astype(F32) * (sg * (1.0 - sg))
            dg_ref[...] = dg.astype(BF16)
            db_ref[...] += jnp.sum(dg, axis=0, keepdims=True)

    t_a = pl.BlockSpec((tm, D_MODEL), lambda i: (i, 0))
    t_b = pl.BlockSpec((tm, D_MODEL), lambda i: (i, 1))
    r_a = pl.BlockSpec((1, D_MODEL), lambda i: (0, 0))
    r_b = pl.BlockSpec((1, D_MODEL), lambda i: (0, 1))
    big = jax.ShapeDtypeStruct((s, D_MODEL), BF16)
    row = jax.ShapeDtypeStruct((1, D_MODEL), F32)
    return pl.pallas_call(
        body, name=name, grid=(s // tm,),
        in_specs=[t_a, t_b, r_a, r_b, t_a, t_a, t_a], out_specs=[t_a, t_a, t_a, t_a, r_a, r_a],
        out_shape=[big, big, big, big, row, row],
        compiler_params=_params(("arbitrary",)),
    )(gates_raw, gates_raw, b_gate, b_gate, p_a, p_b, dm)


def _adamw_update(w_ref, g_ref, m_ref, v_ref, d_ref, mo_ref, vo_ref):
    gv = g_ref[...]
    mn = ADAM_B1 * m_ref[...] + (1.0 - ADAM_B1) * gv
    vn = ADAM_B2 * v_ref[...] + (1.0 - ADAM_B2) * (gv * gv)
    m_hat = mn / (1.0 - ADAM_B1 ** ADAM_STEP)
    v_hat = vn / (1.0 - ADAM_B2 ** ADAM_STEP)
    d_ref[...] = -ADAM_LR * (m_hat / (jnp.sqrt(v_hat) + ADAM_EPS) + ADAM_WD * w_ref[...])
    mo_ref[...] = mn
    vo_ref[...] = vn


def _adamw_many(ws, gs, ms, vs, *, name):
    n = len(ws)

    def body(*refs):
        for i in range(n):
            _adamw_update(*[refs[k * n + i] for k in range(7)])

    whole = pl.BlockSpec(memory_space=pltpu.VMEM)
    sds = [jax.ShapeDtypeStruct(w.shape, F32) for w in ws]
    outs = pl.pallas_call(
        body, name=name, in_specs=[whole] * (4 * n), out_specs=[whole] * (3 * n), out_shape=sds * 3,
        compiler_params=pltpu.CompilerParams(vmem_limit_bytes=VMEM_LIMIT),
    )(*ws, *gs, *ms, *vs)
    return outs[:n], outs[n:2 * n], outs[2 * n:]


def _adamw_column_halves(w, g_own, g_other, core, m, v, *, name, tr):
    r, c = w.shape
    assert r % tr == 0 and g_own.shape == g_other.shape == (r, c // 2), (name, r, tr)

    def body(core_ref, w_ref, a_ref, b_ref, m_ref, v_ref, g_ref, d_ref, mo_ref, vo_ref):
        first = core_ref[0] == 0
        a, b = a_ref[...], b_ref[...]
        g_ref[...] = jnp.concatenate([jnp.where(first, a, b), jnp.where(first, b, a)], axis=1)
        _adamw_update(w_ref, g_ref, m_ref, v_ref, d_ref, mo_ref, vo_ref)

    blk = pl.BlockSpec((tr, c), lambda i, core_ref: (i, 0))
    half = pl.BlockSpec((tr, c // 2), lambda i, core_ref: (i, 0))
    grid_spec = pltpu.PrefetchScalarGridSpec(
        num_scalar_prefetch=1, grid=(r // tr,), in_specs=[blk, half, half, blk, blk], out_specs=[blk] * 4)
    return pl.pallas_call(
        body, name=name, grid_spec=grid_spec, out_shape=[jax.ShapeDtypeStruct((r, c), F32)] * 4,
        compiler_params=pltpu.CompilerParams(
            dimension_semantics=("parallel",), vmem_limit_bytes=VMEM_LIMIT,
            allow_input_fusion=[False, False, True, True, False, False]),
    )(core, w, g_own, g_other, m, v)


def _adamw_two_sums(w, g_a, g_b, m, v, *, name, tr=128):
    r, c = w.shape
    tr = min(tr, r)
    assert r % tr == 0, (name, r, tr)

    def body(w_ref, ga_ref, gb_ref, m_ref, v_ref, g_ref, d_ref, mo_ref, vo_ref):
        g_ref[...] = ga_ref[...] + gb_ref[...]
        _adamw_update(w_ref, g_ref, m_ref, v_ref, d_ref, mo_ref, vo_ref)

    blk = pl.BlockSpec((tr, c), lambda i: (i, 0))
    sds = jax.ShapeDtypeStruct((r, c), F32)
    return pl.pallas_call(
        body, name=name, grid=(r // tr,), in_specs=[blk] * 5, out_specs=[blk] * 4, out_shape=[sds] * 4,
        compiler_params=_params(("parallel",)),
    )(w, g_a, g_b, m, v)


def _tile(n, pref):
    if n <= pref:
        return n
    best = LANES
    for t in range(LANES, pref + 1, LANES):
        if n % t == 0:
            best = t
    return best


MATMUL_BLOCK_BYTES = 20 * 1024 * 1024


def _mm(pairs, name, **kw):
    trans_b = kw.get("trans_b", False)
    m = (pairs[0][0][0] if isinstance(pairs[0][0], tuple) else pairs[0][0]).shape[0]
    ktot, n = 0, None
    for _, b in pairs:
        shape = b[0].shape[1:] if isinstance(b, tuple) else b.shape
        ktot += shape[1] if trans_b else shape[0]
        n = shape[0] if trans_b else shape[1]
    out_bytes = 4 * (2 if kw.get("add") is not None else 1)
    best = None
    for tm in (256, 512, 1024, 2048):
        for tn in range(LANES, min(n, 1536) + 1, LANES):
            if m % min(tm, m) or n % tn:
                continue
            fits = 2 * ktot * (min(tm, m) + tn) + out_bytes * min(tm, m) * tn <= MATMUL_BLOCK_BYTES
            if fits and (best is None or min(tm, m) * tn >= best[0] * best[1]):
                best = (min(tm, m), tn)
    return _matmul(pairs, tm=best[0], tn=best[1], name=name, **kw)


def _wgrad(a, b, name, **kw):
    return _matmul_tn(a, b, tk=_tile(a.shape[1], 1408), tn=kw.pop("tn", _tile(b.shape[1], 1024)), tm=2048,
                      name=name, **kw)


def _local_step(x, target, get_weight, small, emit_grad):
    heads = jnp.arange(SSD_D_INNER) // SSD_HEAD_DIM
    expand = (jnp.arange(LANES)[:, None] == heads[None, :]).astype(BF16)
    expand_t = expand.T
    group_sum = (jnp.arange(SGU_WIDTH)[:, None] // LANES == jnp.arange(LANES)[None, :]).astype(BF16)
    pad_h = LANES - SSD_HEADS
    dt_bias = jnp.pad(small["dt_bias"], ((0, 0), (0, pad_h)))
    a_log = jnp.pad(small["a_log"], ((0, 0), (0, pad_h)))
    a_log_x = jnp.repeat(small["a_log"], SSD_HEAD_DIM, axis=1)
    d_skip_x = jnp.repeat(small["d_skip"], SSD_HEAD_DIM, axis=1)
    b_sp_t = jnp.pad(small["b_spatial"][0].T, ((0, 0), (0, LANES - SGU_GROUPS)))
    w_sp = small["w_spatial"][0]
    conv_a_w = jnp.pad(small["conv_a_w"], ((0, 4), (0, 0)))
    conv_f_w = jnp.pad(small["conv_f_w"], ((0, 5), (0, 0)))
    final_w = small["final_norm_w"].reshape(1, D_MODEL)

    n1 = _rms_fwd(x, small["norm1_w"], after=small.get("gathers_started"), name="rms1_fwd")
    wts = dict(get_weight("w_in", n1))
    z = _mm([(n1, wts["in_z"])], "in_z")
    xbc_raw = _mm([(n1, wts["in_xbc"])], "in_xbc")
    dt_raw = _mm([(n1, wts["in_dt"])], "in_dt")
    uv_raw = _mm([(n1, wts["in_uv"])], "in_uv", out_dtype=BF16)
    gates_raw = _mm([(n1, wts["in_gate"])], "in_gate", out_dtype=BF16)
    xbc, xbc_pre = _conv_a_fwd(xbc_raw, conv_a_w, small["conv_a_b"], name="conv_a_fwd")
    y, y_a, states = _ssd_fwd(xbc, dt_raw, z, dt_bias, a_log, a_log_x, d_skip_x, small["ssd_norm_w"], expand,
                              name="ssd_fwd")
    y_b = _sgu_fwd(uv_raw, small["uv_b"], small["v_ln_w"], small["v_ln_b"], w_sp, b_sp_t, name="sgu_fwd")
    wts.update(get_weight("w_branch", y_b))
    p_a = _mm([(y_a, wts["branch_a"])], "branch_a", out_dtype=BF16)
    p_b = _mm([(y_b, wts["branch_b"])], "branch_b", out_dtype=BF16)
    mix = _gate_fwd(gates_raw, small["b_gate"], p_a, p_b, name="gate_fwd")
    wts.update(get_weight("w_out", mix))
    h1 = _mm([(mix, wts["out"])], "out_proj", add=x)
    n2 = _rms_fwd(h1, small["norm2_w"], name="rms2_fwd")
    wts.update(get_weight("w_up", n2))
    up_w = wts["up"]
    up_cols = up_w.shape[2]
    up_raw = _matmul([(n2, (up_w, "cols"))], tm=2048, tn=up_cols, out_dtype=BF16, name="up_proj")
    act, up_a, up_v = _conv_f_fwd(up_raw, conv_f_w, small["conv_f_b"], name="conv_f_fwd")
    wts.update(get_weight("w_down", act))
    h2 = _mm([(act, wts["down"])], "down_proj", add=h1)
    loss, dh2, dh2_b, d_final = _final_fwd_bwd(h2, final_w, target, name="final_norm_loss")

    dact = _mm([(dh2_b, wts["down"])], "down_dgrad", trans_b=True)
    started = emit_grad("w_down", _wgrad(act, dh2_b, "down_wgrad"))
    dup_a, dup_v, dwf_a, dwf_v, dbf_a, dbf_v = _conv_f_bwd(up_raw, up_a, up_v, conv_f_w, dact, name="conv_f_bwd")
    dn2 = _mm([((dup_a, 0), (up_w, 0)), ((dup_a, 1), (up_w, 1)), ((dup_v, 0), (up_w, 2)), ((dup_v, 1), (up_w, 3))],
              "up_dgrad", trans_b=True, after=started, out_dtype=BF16)
    g_up = _wgrad(n2, dup_a, "up_wgrad_a", tn=up_cols, stack_out=True, part_of=(N_CHIPS, 0, None))
    g_up = _wgrad(n2, dup_v, "up_wgrad_v", tn=up_cols, stack_out=True, part_of=(N_CHIPS, N_CHIPS // 2, g_up))
    started = emit_grad("w_up", g_up)
    dh1, dh1_b, d_norm2 = _rms_bwd(h1, small["norm2_w"], dn2, dh2, bf16_copy=True, name="rms2_bwd")
    dmix = _mm([(dh1_b, wts["out"])], "out_dgrad", trans_b=True, after=started, out_dtype=BF16)
    started = emit_grad("w_out", _wgrad(mix, dh1_b, "out_wgrad"))
    dp_a, dp_b, dg_a, dg_b, dbg_a, dbg_b = _gate_bwd(gates_raw, small["b_gate"], p_a, p_b, dmix, name="gate_bwd")
    dya = _mm([(dp_a, wts["branch_a"])], "branch_a_dgrad", trans_b=True, after=started)
    dyb = _mm([(dp_b, wts["branch_b"])], "branch_b_dgrad", trans_b=True, out_dtype=BF16)
    g_branch = _wgrad(y_a, dp_a, "branch_a_wgrad", part_of=(3, 0, None))
    g_branch = _wgrad(y_b, dp_b, "branch_b_wgrad", part_of=(3, 2, g_branch))
    started_branch = emit_grad("w_branch", g_branch)
    duv, d_wsp, d_bsp_t, d_lnw, d_lnb, d_uvb = _sgu_bwd(uv_raw, dyb, small["uv_b"], small["v_ln_w"],
                                                        small["v_ln_b"], w_sp, b_sp_t, group_sum, name="sgu_bwd")
    dz, dxbc, ddt, d_ssd_nw, d_dskip, d_alog, d_dtb = _ssd_bwd(
        dya, y, z, xbc, dt_raw, states, dt_bias, a_log, a_log_x, d_skip_x, small["ssd_norm_w"], expand, expand_t,
        name="ssd_bwd")
    dxbc_raw, d_conv_a_w, d_conv_a_b = _conv_a_bwd(xbc_raw, xbc_pre, conv_a_w, dxbc, name="conv_a_bwd")
    started = emit_grad("w_in", {
        "in_z": _wgrad(n1, dz, "in_z_wgrad", after=started_branch), "in_xbc": _wgrad(n1, dxbc_raw, "in_xbc_wgrad"),
        "in_dt": _wgrad(n1, ddt, "in_dt_wgrad")[:, :SSD_HEADS], "in_uv": _wgrad(n1, duv, "in_uv_wgrad"),
        "in_gate_a": _wgrad(n1, dg_a, "in_gate_a_wgrad"), "in_gate_b": _wgrad(n1, dg_b, "in_gate_b_wgrad")})
    dn1 = _mm([(dz, wts["in_z"]), (dxbc_raw, wts["in_xbc"]), (ddt, wts["in_dt"]), (duv, wts["in_uv"]),
               (dg_a, wts["in_gate_a"]), (dg_b, wts["in_gate_b"])], "in_dgrad", trans_b=True, after=started,
              out_dtype=BF16)
    dx, d_norm1 = _rms_bwd(x, small["norm1_w"], dn1, dh1, bf16_copy=False, name="rms1_bwd")

    grads_small = {
        "norm1_w": d_norm1, "b_gate": jnp.concatenate([dbg_a, dbg_b], axis=1),
        "conv_a_w": d_conv_a_w[:4], "conv_a_b": d_conv_a_b,
        "dt_bias": d_dtb[:, :SSD_HEADS], "a_log": d_alog[:, :SSD_HEADS], "d_skip": d_dskip[:, :SSD_HEADS],
        "ssd_norm_w": d_ssd_nw, "uv_b": d_uvb, "v_ln_w": d_lnw, "v_ln_b": d_lnb,
        "w_spatial": d_wsp[None], "b_spatial": d_bsp_t[:, :SGU_GROUPS].T[None],
        "norm2_w": d_norm2, "conv_f_w": jnp.concatenate([dwf_a[:3], dwf_v[:3]], axis=1),
        "conv_f_b": jnp.concatenate([dbf_a, dbf_v], axis=1), "final_norm_w": d_final.reshape(D_MODEL),
    }
    return loss, dx, grads_small


HBM = pl.BlockSpec(memory_space=pl.ANY)
MESH = pl.DeviceIdType.MESH


def _mesh_pos():
    return lax.axis_index("x"), lax.axis_index("y"), lax.axis_index("c")


def _other_chips(x, y):
    return [(1 - x, y), (x, 1 - y), (1 - x, 1 - y)]


def _remote(src, dst, send_sems, recv_sems, k, dev):
    return pltpu.make_async_remote_copy(src_ref=src, dst_ref=dst, send_sem=send_sems.at[k], recv_sem=recv_sems.at[k],
                                        device_id=dev, device_id_type=MESH)


def _dma_sems(n):
    return [pltpu.SemaphoreType.DMA((n,)), pltpu.SemaphoreType.DMA((n,))]


HBM_ONLY = pl.BlockSpec(memory_space=pltpu.HBM)
SEMAPHORES = pl.BlockSpec(memory_space=pltpu.SEMAPHORE)
DATAFLOW_EFFECT = pltpu.SideEffectType.DATAFLOW_SIDE_EFFECTING
N_PEER_CHIPS = N_CHIPS - 1


def _gather_sends(w_ref, land_ref, send_sems, recv_sems):
    x, y, c = _mesh_pos()
    return [_remote(w_ref.at[c], land_ref.at[2 * x + y, c], send_sems, recv_sems, k, (px, py, c))
            for k, (px, py) in enumerate(_other_chips(x, y))]


def _gather_arrivals(w_ref, land_ref, send_sems, recv_sems):
    x, y, c = _mesh_pos()
    return [_remote(w_ref.at[c], land_ref.at[2 * px + py, c], send_sems, recv_sems, k, (px, py, c))
            for k, (px, py) in enumerate(_other_chips(x, y))]


def _gather_whole_sends(w_ref, land_ref, send_sems, recv_sems):
    x, y, c = _mesh_pos()
    return [_remote(w_ref, land_ref.at[2 * x + y], send_sems, recv_sems, k, (px, py, c))
            for k, (px, py) in enumerate(_other_chips(x, y))]


def _gather_whole_arrivals(w_ref, land_ref, send_sems, recv_sems):
    x, y, c = _mesh_pos()
    return [_remote(w_ref, land_ref.at[2 * px + py], send_sems, recv_sems, k, (px, py, c))
            for k, (px, py) in enumerate(_other_chips(x, y))]


def _scatter_sends(h_ref, land_ref, send_sems, recv_sems):
    x, y, c = _mesh_pos()
    return [_remote(h_ref.at[2 * px + py], land_ref.at[2 * x + y], send_sems, recv_sems, k, (px, py, c))
            for k, (px, py) in enumerate(_other_chips(x, y))]


def _scatter_arrivals(h_ref, land_ref, send_sems, recv_sems):
    x, y, c = _mesh_pos()
    return [_remote(h_ref.at[2 * x + y], land_ref.at[2 * px + py], send_sems, recv_sems, k, (px, py, c))
            for k, (px, py) in enumerate(_other_chips(x, y))]


def _exchange_wait_many(pendings, after, sends, arrivals, *, name):
    n = len(pendings)

    def body(*refs):
        for i in range(n):
            src_ref, land_ref, send_ref, recv_ref = refs[i], refs[n + i], refs[2 * n + i], refs[3 * n + i]
            for cp in sends(src_ref, land_ref, send_ref, recv_ref):
                cp.wait_send()
            for cp in arrivals(src_ref, land_ref, send_ref, recv_ref):
                cp.wait_recv()

    sources = [p[2] for p in pendings]
    landings = [p[3] for p in pendings]
    outs = pl.pallas_call(
        body, name=name,
        out_shape=tuple(pltpu.HBM(a.shape, a.dtype) for a in sources + landings),
        in_specs=[HBM_ONLY] * (2 * n) + [SEMAPHORES] * (2 * n) + [pl.BlockSpec(memory_space=pl.ANY)],
        out_specs=tuple([HBM_ONLY] * (2 * n)), input_output_aliases={i: i for i in range(2 * n)},
        compiler_params=pltpu.CompilerParams(has_side_effects=DATAFLOW_EFFECT),
    )(*sources, *landings, *[p[0] for p in pendings], *[p[1] for p in pendings], after)
    return [(outs[i], outs[n + i]) for i in range(n)]


def _sibling_sends(src_ref, land_ref, send_sems, recv_sems):
    x, y, c = _mesh_pos()
    return [_remote(src_ref, land_ref, send_sems, recv_sems, 0, (x, y, 1 - c))]


def _exchange_start(sources, landing_shapes, sends, *, after=None, name):
    n = len(sources)
    extra = [] if after is None else [after]

    def body(*refs):
        sems = refs[2 * n + len(extra):4 * n + len(extra)]
        for i in range(n):
            send_i = sends[i] if isinstance(sends, (list, tuple)) else sends
            for cp in send_i(refs[i], refs[n + i], sems[2 * i], sems[2 * i + 1]):
                cp.start()
        refs[-1][...] = jnp.zeros_like(refs[-1])

    hbm = [pltpu.HBM(s.shape, s.dtype) for s in sources] + [pltpu.HBM(shp, s.dtype)
                                                             for shp, s in zip(landing_shapes, sources)]
    outs = pl.pallas_call(
        body, name=name,
        out_shape=tuple([pltpu.SemaphoreType.DMA((N_PEER_CHIPS,))] * (2 * n) + hbm
                        + [jax.ShapeDtypeStruct((8, LANES), F32)]),
        in_specs=[HBM_ONLY] * (2 * n) + [pl.BlockSpec(memory_space=pl.ANY)] * len(extra),
        out_specs=tuple([SEMAPHORES] * (2 * n) + [HBM_ONLY] * (2 * n) + [pl.BlockSpec(memory_space=pltpu.VMEM)]),
        input_output_aliases={i: 2 * n + i for i in range(2 * n)},
        compiler_params=pltpu.CompilerParams(has_side_effects=DATAFLOW_EFFECT),
    )(*[pltpu.with_memory_space_constraint(s, pltpu.HBM) for s in sources],
      *[pltpu.with_memory_space_constraint(lax.empty(shp, s.dtype), pltpu.HBM)
        for shp, s in zip(landing_shapes, sources)], *extra)
    pending = [(outs[2 * i], outs[2 * i + 1], outs[2 * n + i], outs[3 * n + i]) for i in range(n)]
    return pending, outs[-1]


def _exchange_wait(pending, after, sends, arrivals, *, name):
    send_sems, recv_sems, source, landing = pending

    def body(src_ref, land_ref, send_ref, recv_ref, after_ref, src_out, land_out):
        for cp in sends(src_ref, land_ref, send_ref, recv_ref):
            cp.wait_send()
        for cp in arrivals(src_ref, land_ref, send_ref, recv_ref):
            cp.wait_recv()

    return pl.pallas_call(
        body, name=name,
        out_shape=(pltpu.HBM(source.shape, source.dtype), pltpu.HBM(landing.shape, landing.dtype)),
        in_specs=[HBM_ONLY, HBM_ONLY, SEMAPHORES, SEMAPHORES, pl.BlockSpec(memory_space=pl.ANY)],
        out_specs=(HBM_ONLY, HBM_ONLY), input_output_aliases={0: 0, 1: 1},
        compiler_params=pltpu.CompilerParams(has_side_effects=DATAFLOW_EFFECT),
    )(source, landing, send_sems, recv_sems, after)


def _gather_ici(shard, *, name):
    _, rh, cols = shard.shape

    def body(w_ref, o_ref, send_sems, recv_sems):
        x, y, c = _mesh_pos()
        mine = 2 * x + y
        sends = []
        for k, (px, py) in enumerate(_other_chips(x, y)):
            cp = _remote(w_ref.at[c], o_ref.at[mine, c], send_sems, recv_sems, k, (px, py, c))
            cp.start()
            sends.append(cp)
        for k, (px, py) in enumerate(_other_chips(x, y)):
            _remote(w_ref.at[c], o_ref.at[2 * px + py, c], send_sems, recv_sems, k, (px, py, c)).wait_recv()
        for cp in sends:
            cp.wait_send()

    return pl.pallas_call(
        body, name=name, in_specs=[HBM], out_specs=HBM,
        out_shape=jax.ShapeDtypeStruct((N_CHIPS, 2, rh, cols), shard.dtype), scratch_shapes=_dma_sems(3),
    )(shard)


def _gather_d2d(parts, *, name):
    def body(a_ref, o_ref, send_sems, recv_sems):
        x, y, c = _mesh_pos()
        sibling = (x, y, 1 - c)
        sends = []
        for k, (px, py) in enumerate(_other_chips(x, y)):
            cp = _remote(a_ref.at[2 * px + py, c], o_ref.at[2 * px + py, c], send_sems, recv_sems, k, sibling)
            cp.start()
            sends.append(cp)
        for k, (px, py) in enumerate(_other_chips(x, y)):
            _remote(a_ref.at[2 * px + py, c], o_ref.at[2 * px + py, 1 - c], send_sems, recv_sems, k, sibling).wait_recv()
        for cp in sends:
            cp.wait_send()

    return pl.pallas_call(
        body, name=name, in_specs=[HBM], out_specs=HBM,
        out_shape=jax.ShapeDtypeStruct(parts.shape, parts.dtype),
        input_output_aliases={0: 0}, scratch_shapes=_dma_sems(3),
    )(parts)


def _all_gather_chips(shard_flat, name):
    rows, cols = shard_flat.shape
    parts = _gather_ici(shard_flat.reshape(2, rows // 2, cols), name=name + "_ici")
    others = _gather_d2d(parts, name=name + "_d2d").reshape(N_CHIPS, rows, cols)
    chip = 2 * lax.axis_index("x") + lax.axis_index("y")
    return lax.dynamic_update_slice(others, shard_flat[None], (chip, 0, 0))


def _row_tile(rows, mult, cap):
    best = mult
    for t in range(mult, min(rows, cap) + 1, mult):
        if rows % t == 0:
            best = t
    assert rows % best == 0, (rows, mult)
    return best


def _swap_halves_d2d(g, *, after=None, name):
    blocks = list(g) if isinstance(g, (list, tuple)) else [g]
    rh, cols = blocks[0].shape[-2:]
    extra = [] if after is None else [after]

    def body(*refs):
        g_refs = refs[:len(blocks)]
        o_ref, send_sems, recv_sems = refs[len(blocks) + len(extra):]
        x, y, c = _mesh_pos()
        sibling = (x, y, 1 - c)

        def half(s, k):
            return g_refs[s].at[k] if len(g_refs) > 1 else g_refs[0].at[s, k]

        sends = []
        for s in range(N_CHIPS):
            cp = _remote(half(s, 1 - c), o_ref.at[s], send_sems, recv_sems, s, sibling)
            cp.start()
            sends.append(cp)
        for s in range(N_CHIPS):
            _remote(half(s, c), o_ref.at[s], send_sems, recv_sems, s, sibling).wait_recv()
        for cp in sends:
            cp.wait_send()

    return pl.pallas_call(
        body, name=name, in_specs=[HBM] * (len(blocks) + len(extra)), out_specs=HBM,
        out_shape=jax.ShapeDtypeStruct((N_CHIPS, rh, cols), blocks[0].dtype), scratch_shapes=_dma_sems(N_CHIPS),
    )(*blocks, *extra)


def _add_own_half(g, arrived, core, *, name):
    blocks = list(g) if isinstance(g, (list, tuple)) else [g]
    dtype = blocks[0].dtype
    rh, cols = blocks[0].shape[-2:]
    mult = 16 if dtype == BF16 else 8
    tr = _row_tile(rh, mult, max(mult, (512 * 1024) // cols))

    def body(core_ref, *refs):
        g_refs, a_ref, o_ref = refs[:-2], refs[-2], refs[-1]
        if len(g_refs) == 1:
            o_ref[...] = (g_refs[0][0].astype(F32) + a_ref[...].astype(F32)).astype(o_ref.dtype)
            return
        for k, g_ref in enumerate(g_refs):
            @pl.when(pl.program_id(0) == k)
            def _(g_ref=g_ref):
                o_ref[...] = (g_ref[...].astype(F32) + a_ref[...].astype(F32)).astype(o_ref.dtype)

    if len(blocks) == 1:
        g_specs = [pl.BlockSpec((1, 1, tr, cols), lambda s, i, core_ref: (s, core_ref[0], i, 0))]
    else:
        g_specs = [pl.BlockSpec((1, tr, cols), lambda s, i, core_ref, k=k: (core_ref[0], jnp.where(s == k, i, 0), 0))
                   for k in range(N_CHIPS)]
    grid_spec = pltpu.PrefetchScalarGridSpec(
        num_scalar_prefetch=1, grid=(N_CHIPS, rh // tr),
        in_specs=g_specs + [pl.BlockSpec((1, tr, cols), lambda s, i, core_ref: (s, i, 0))],
        out_specs=pl.BlockSpec((1, tr, cols), lambda s, i, core_ref: (s, i, 0)))
    return pl.pallas_call(
        body, name=name, grid_spec=grid_spec, out_shape=jax.ShapeDtypeStruct((N_CHIPS, rh, cols), dtype),
        compiler_params=_params(("parallel", "parallel")),
    )(core, *blocks, arrived)


def _scatter_ici(h, *, after=None, name):
    extra = [] if after is None else [after]

    def body(h_ref, *rest):
        o_ref, send_sems, recv_sems = rest[len(extra):]
        x, y, c = _mesh_pos()
        mine = 2 * x + y
        sends = []
        for k, (px, py) in enumerate(_other_chips(x, y)):
            cp = _remote(h_ref.at[2 * px + py], o_ref.at[mine], send_sems, recv_sems, k, (px, py, c))
            cp.start()
            sends.append(cp)
        for k, (px, py) in enumerate(_other_chips(x, y)):
            _remote(h_ref.at[mine], o_ref.at[2 * px + py], send_sems, recv_sems, k, (px, py, c)).wait_recv()
        for cp in sends:
            cp.wait_send()

    others = pl.pallas_call(
        body, name=name, in_specs=[HBM] * (1 + len(extra)), out_specs=HBM,
        out_shape=jax.ShapeDtypeStruct(h.shape, h.dtype), scratch_shapes=_dma_sems(3),
    )(h, *extra)
    chip = 2 * lax.axis_index("x") + lax.axis_index("y")
    own = lax.dynamic_slice_in_dim(h, chip, 1, axis=0)
    return lax.dynamic_update_slice(others, own, (chip, 0, 0))


def _sum_chips(parts, *, name):
    _, rh, cols = parts.shape
    mult = 16 if parts.dtype == BF16 else 8
    tr = _row_tile(rh, mult, max(mult, (512 * 1024) // cols))

    def body(p_ref, o_ref):
        acc = p_ref[0].astype(F32)
        for s in range(1, N_CHIPS):
            acc = acc + p_ref[s].astype(F32)
        o_ref[...] = acc

    return pl.pallas_call(
        body, name=name, grid=(rh // tr,),
        in_specs=[pl.BlockSpec((N_CHIPS, tr, cols), lambda i: (0, i, 0))],
        out_specs=pl.BlockSpec((tr, cols), lambda i: (i, 0)),
        out_shape=jax.ShapeDtypeStruct((rh, cols), F32), compiler_params=_params(("parallel",)),
    )(parts)


def _sum_chips_with_own(landed, sent, chip, *, name):
    _, rh, cols = landed.shape
    mult = 16 if landed.dtype == BF16 else 8
    tr = _row_tile(rh, mult, max(mult, (512 * 1024) // cols))

    def body(chip_ref, own_ref, px_ref, py_ref, pxy_ref, o_ref):
        acc = own_ref[0].astype(F32)
        for p_ref in (px_ref, py_ref, pxy_ref):
            acc = acc + p_ref[0].astype(F32)
        o_ref[...] = acc

    def block_of(flip):
        return pl.BlockSpec((1, tr, cols), lambda i, chip_ref: (chip_ref[0] ^ flip, i, 0))

    grid_spec = pltpu.PrefetchScalarGridSpec(
        num_scalar_prefetch=1, grid=(rh // tr,),
        in_specs=[block_of(0), block_of(2), block_of(1), block_of(3)],
        out_specs=pl.BlockSpec((tr, cols), lambda i, chip_ref: (i, 0)))
    return pl.pallas_call(
        body, name=name, grid_spec=grid_spec, out_shape=jax.ShapeDtypeStruct((rh, cols), F32),
        compiler_params=_params(("parallel",)),
    )(chip, sent, landed, landed, landed)


def _share_d2d(f, *, name):
    fs = f if isinstance(f, (list, tuple)) else [f]
    others = _swap_with_sibling(fs, name=name)
    first = lax.axis_index("c") == 0
    both = [jnp.stack([jnp.where(first, a, b), jnp.where(first, b, a)]) for a, b in zip(fs, others)]
    return both if isinstance(f, (list, tuple)) else both[0]


def _swap_with_sibling(fs, *, name):
    n = len(fs)

    def body(*refs):
        x, y, c = _mesh_pos()
        sibling = (x, y, 1 - c)
        send_sems, recv_sems = refs[2 * n:]
        copies = [_remote(refs[i], refs[n + i], send_sems, recv_sems, i, sibling) for i in range(n)]
        for cp in copies:
            cp.start()
        for cp in copies:
            cp.wait()

    return pl.pallas_call(
        body, name=name, in_specs=[HBM] * n, out_specs=[HBM] * n,
        out_shape=[jax.ShapeDtypeStruct(a.shape, a.dtype) for a in fs], scratch_shapes=_dma_sems(n),
    )(*fs)


def _reduce_scatter_chips(g, core, name, after=None, after_swap=None):
    _, rows, cols = g.shape
    g = g.reshape(N_CHIPS, 2, rows // 2, cols)
    arrived = _swap_halves_d2d(g, after=after, name=name + "_swap")
    started = after_swap(arrived) if after_swap is not None else None
    chip_sum = _add_own_half(g, arrived, core, name=name + "_add2")
    parts = _scatter_ici(chip_sum, after=started, name=name + "_ici")
    total = _sum_chips(parts, name=name + "_sum4")
    return _share_d2d(total, name=name + "_share").reshape(rows, cols)


BIG = ("w_in", "w_branch", "w_out", "w_up", "w_down")
BIG_COLUMN_SHARDED = ("w_in", "w_up")
CONV = ("conv_a_w", "conv_f_w")
REPLICATED = ("norm1_w", "b_gate", "conv_a_b", "dt_bias", "a_log", "d_skip", "ssd_norm_w", "uv_b", "v_ln_w",
              "v_ln_b", "w_spatial", "b_spatial", "norm2_w", "conv_f_b", "final_norm_w")
WEIGHT_ORDER = ("norm1_w", "w_in", "b_gate", "conv_a_w", "conv_a_b", "dt_bias", "a_log", "d_skip", "ssd_norm_w",
                "uv_b", "v_ln_w", "v_ln_b", "w_spatial", "b_spatial", "w_branch", "w_out", "norm2_w", "w_up",
                "conv_f_w", "conv_f_b", "w_down", "final_norm_w")
SMALL_EXCHANGE_ROWS = 64


_GATE0 = SSD_IN + 2 * SGU_WIDTH
IN_SEGMENTS = {
    "in_z": (0, SSD_D_INNER), "in_xbc": (SSD_D_INNER, SSD_D_INNER + SSD_XBC), "in_dt": (SSD_D_INNER + SSD_XBC, SSD_IN),
    "in_uv": (SSD_IN, _GATE0), "in_gate": (_GATE0, IN_COLS), "in_gate_a": (_GATE0, _GATE0 + D_MODEL),
    "in_gate_b": (_GATE0 + D_MODEL, IN_COLS),
}
IN_GRAD_SEGMENTS = ("in_z", "in_xbc", "in_dt", "in_uv", "in_gate_a", "in_gate_b")


def _take_columns(parts, start, stop):
    out = []
    for a, first in parts:
        lo, hi = max(start, first), min(stop, first + a.shape[1])
        if lo < hi:
            out.append(a[:, lo - first:hi - first])
    return out[0] if len(out) == 1 else jnp.concatenate(out, axis=1)


def _flat_rows(arrays, row_multiple):
    flat = jnp.concatenate([a.reshape(-1) for a in arrays])
    rows = -(-flat.shape[0] // (LANES * row_multiple)) * row_multiple
    return jnp.pad(flat, (0, rows * LANES - flat.shape[0])).reshape(rows, LANES)


def _unflatten(flat, shapes):
    flat = flat.reshape(-1)
    out, off = [], 0
    for shp in shapes:
        n = math.prod(shp)
        out.append(flat[off:off + n].reshape(shp))
        off += n
    return out


def _from_chip_blocks(blocks, name):
    if name in BIG_COLUMN_SHARDED or name in CONV:
        k = blocks.shape[1]
        return jnp.transpose(blocks, (1, 0, 2)).reshape(k, -1)
    return blocks.reshape(-1, blocks.shape[-1])


def _to_chip_blocks(whole, name):
    if name in BIG_COLUMN_SHARDED or name in CONV:
        k, n = whole.shape
        return jnp.transpose(whole.reshape(k, N_CHIPS, n // N_CHIPS), (1, 0, 2))
    return whole.reshape(N_CHIPS, whole.shape[0] // N_CHIPS, whole.shape[1])


def kernel(x, norm1_w, w_in, b_gate, conv_a_w, conv_a_b, dt_bias, a_log, d_skip, ssd_norm_w, uv_b, v_ln_w, v_ln_b, w_spatial, b_spatial, w_branch, w_out, norm2_w, w_up, conv_f_w, conv_f_b, w_down, final_norm_w, loss_target, m_norm1_w, m_w_in, m_b_gate, m_conv_a_w, m_conv_a_b, m_dt_bias, m_a_log, m_d_skip, m_ssd_norm_w, m_uv_b, m_v_ln_w, m_v_ln_b, m_w_spatial, m_b_spatial, m_w_branch, m_w_out, m_norm2_w, m_w_up, m_conv_f_w, m_conv_f_b, m_w_down, m_final_norm_w, v_norm1_w, v_w_in, v_b_gate, v_conv_a_w, v_conv_a_b, v_dt_bias, v_a_log, v_d_skip, v_ssd_norm_w, v_uv_b, v_v_ln_w, v_v_ln_b, v_w_spatial, v_b_spatial, v_w_branch, v_w_out, v_norm2_w, v_w_up, v_conv_f_w, v_conv_f_b, v_w_down, v_final_norm_w):
    weights = dict(norm1_w=norm1_w, w_in=w_in, b_gate=b_gate, conv_a_w=conv_a_w, conv_a_b=conv_a_b, dt_bias=dt_bias,
                   a_log=a_log, d_skip=d_skip, ssd_norm_w=ssd_norm_w, uv_b=uv_b, v_ln_w=v_ln_w, v_ln_b=v_ln_b,
                   w_spatial=w_spatial, b_spatial=b_spatial, w_branch=w_branch, w_out=w_out, norm2_w=norm2_w,
                   w_up=w_up, conv_f_w=conv_f_w, conv_f_b=conv_f_b, w_down=w_down, final_norm_w=final_norm_w)
    mom1 = dict(norm1_w=m_norm1_w, w_in=m_w_in, b_gate=m_b_gate, conv_a_w=m_conv_a_w, conv_a_b=m_conv_a_b,
                dt_bias=m_dt_bias, a_log=m_a_log, d_skip=m_d_skip, ssd_norm_w=m_ssd_norm_w, uv_b=m_uv_b,
                v_ln_w=m_v_ln_w, v_ln_b=m_v_ln_b, w_spatial=m_w_spatial, b_spatial=m_b_spatial, w_branch=m_w_branch,
                w_out=m_w_out, norm2_w=m_norm2_w, w_up=m_w_up, conv_f_w=m_conv_f_w, conv_f_b=m_conv_f_b,
                w_down=m_w_down, final_norm_w=m_final_norm_w)
    mom2 = dict(norm1_w=v_norm1_w, w_in=v_w_in, b_gate=v_b_gate, conv_a_w=v_conv_a_w, conv_a_b=v_conv_a_b,
                dt_bias=v_dt_bias, a_log=v_a_log, d_skip=v_d_skip, ssd_norm_w=v_ssd_norm_w, uv_b=v_uv_b,
                v_ln_w=v_v_ln_w, v_ln_b=v_v_ln_b, w_spatial=v_w_spatial, b_spatial=v_b_spatial, w_branch=v_w_branch,
                w_out=v_w_out, norm2_w=v_norm2_w, w_up=v_w_up, conv_f_w=v_conv_f_w, conv_f_b=v_conv_f_b,
                w_down=v_w_down, final_norm_w=v_final_norm_w)
    chip = 2 * lax.axis_index("x") + lax.axis_index("y")
    core = lax.axis_index("c").astype(jnp.int32).reshape(1)

    whole = {}
    conv_shapes = [weights[n].shape[1:] for n in CONV]
    conv_gathered = _all_gather_chips(_flat_rows([weights[n] for n in CONV], 16), "gather_conv").reshape(N_CHIPS, -1)
    off = 0
    for n, shp in zip(CONV, conv_shapes):
        size = math.prod(shp)
        whole[n] = _from_chip_blocks(conv_gathered[:, off:off + size].reshape((N_CHIPS,) + shp), n)
        off += size
    shard_shapes = {n: weights[n].shape[1:] for n in BIG}
    halves = [weights[n][0].astype(BF16).reshape(2, shard_shapes[n][0] // 2, shard_shapes[n][1]) for n in BIG]
    sends = [_gather_sends if n == "w_in" else _gather_whole_sends for n in BIG]
    gathers, gathers_started = _exchange_start(halves, [(N_CHIPS,) + h.shape for h in halves], sends,
                                               after=conv_gathered, name="gather_start")
    gathers = dict(zip(BIG, gathers))

    def get_weight(name, after):
        rows, cols = shard_shapes[name]
        if name == "w_in":
            own, landed = _exchange_wait(gathers[name], after, _gather_sends, _gather_arrivals,
                                         name="gather_" + name + "_wait")
            landed = _gather_d2d(landed, name="gather_" + name + "_d2d")
        else:
            own, landed = _exchange_wait(gathers[name], after, _gather_whole_sends, _gather_whole_arrivals,
                                         name="gather_" + name + "_wait")
        blocks = lax.dynamic_update_slice(landed.reshape(N_CHIPS, rows, cols), own.reshape(1, rows, cols),
                                          (chip, 0, 0))
        if name == "w_up":
            return {"up": blocks}
        if name == "w_in":
            parts = [(blocks[k], cols * k) for k in range(N_CHIPS)]
            segs = {n: _take_columns(parts, a, b) for n, (a, b) in IN_SEGMENTS.items()}
            segs["in_dt"] = jnp.pad(segs["in_dt"], ((0, 0), (0, LANES - SSD_HEADS)))
            return segs
        full = _from_chip_blocks(blocks, name)
        if name == "w_branch":
            return {"branch_a": full[:SSD_D_INNER], "branch_b": full[SSD_D_INNER:]}
        return {name[2:]: full}

    small = {n: weights[n] for n in REPLICATED}
    small["conv_a_w"] = whole["conv_a_w"]
    small["conv_f_w"] = whole["conv_f_w"]
    small["gathers_started"] = gathers_started

    reductions = {}

    def emit_grad(name, g):
        if name == "w_in":
            parts = [(g[n], IN_SEGMENTS[n][0]) for n in IN_GRAD_SEGMENTS]
            rows, cols = shard_shapes[name]
            g_halves = [_take_columns(parts, cols * k, cols * (k + 1)).reshape(2, rows // 2, cols)
                        for k in range(N_CHIPS)]
            arrived = _swap_halves_d2d(g_halves, name="reduce_" + name + "_swap")
            g_blocks = _add_own_half(g_halves, arrived, core, name="reduce_" + name + "_add2")
        else:
            g_blocks = g if name == "w_up" else _to_chip_blocks(g, name)
        (pending,), started = _exchange_start([g_blocks], [g_blocks.shape], _scatter_sends,
                                              name="reduce_" + name + "_start")
        reductions[name] = pending
        return started

    loss, dx, grads_small = _local_step(x[0], loss_target[0], get_weight, small, emit_grad)

    order = ("w_down", "w_up", "w_out", "w_branch", "w_in")
    core_sums = []
    chip_index = chip.astype(jnp.int32).reshape(1)
    for n in order:
        sent, landed = _exchange_wait(reductions[n], dx, _scatter_sends, _scatter_arrivals,
                                      name="reduce_" + n + "_wait")
        core_sums.append(_sum_chips_with_own(landed, sent, chip_index, name="reduce_" + n + "_sum4"))
    grads = {}
    swaps = []

    def start_sum_swap(small_swapped):
        pending, started = _exchange_start(core_sums, [a.shape for a in core_sums], _sibling_sends,
                                           after=small_swapped, name="reduce_swap_start")
        swaps.extend(pending)
        return started

    small_names = REPLICATED + CONV + ("loss",)
    grads_small = dict(grads_small, loss=loss)
    small_shapes = [grads_small[n].shape for n in small_names]
    g_small = _flat_rows([grads_small[n] for n in small_names], N_CHIPS * 2 * SMALL_EXCHANGE_ROWS)
    red_small = _reduce_scatter_chips(g_small.reshape(N_CHIPS, -1, LANES), core, "reduce_small", after=core_sums[-1],
                                      after_swap=start_sum_swap)
    all_small = _all_gather_chips(red_small, "gather_small")
    swapped = _exchange_wait_many(swaps, all_small, _sibling_sends, _sibling_sends, name="reduce_swap_wait")
    core_sums = {n: own for n, (own, _) in zip(order, swapped)}
    sibling_sums = {n: other for n, (_, other) in zip(order, swapped)}
    for n, g in zip(small_names, _unflatten(all_small, small_shapes)):
        if n == "loss":
            total_loss = g[0, 0]
            continue
        if n in CONV:
            width = g.shape[1] // N_CHIPS
            g = lax.dynamic_slice_in_dim(g, chip * width, width, axis=1)
        grads[n] = g.reshape(weights[n].shape[1:]) if n != "final_norm_w" else g

    delta, new_m, new_v = {}, {}, {}
    for n in BIG:
        shp = weights[n].shape
        if n == "w_in":
            results = _adamw_column_halves(weights[n][0].T, core_sums[n].T, sibling_sums[n].T, core, mom1[n][0].T,
                                           mom2[n][0].T, name="adamw_" + n,
                                           tr=_row_tile(shp[2], 8, 136))
            results = [a.T for a in results]
        else:
            results = _adamw_two_sums(weights[n][0], core_sums[n], sibling_sums[n], mom1[n][0], mom2[n][0],
                                      name="adamw_" + n, tr=_row_tile(shp[1], 8, 352))
        grads[n], delta[n], new_m[n], new_v[n] = [a.reshape(shp) for a in results]
    small_all = [n for n in WEIGHT_ORDER if n not in BIG]

    def as_2d(a):
        return a.reshape(-1, a.shape[-1])

    results = _adamw_many(*[[as_2d(src[n]) for n in small_all] for src in (weights, grads, mom1, mom2)],
                          name="adamw_small")
    for n, dv, mv, vv in zip(small_all, *results):
        shp = weights[n].shape
        delta[n], new_m[n], new_v[n] = dv.reshape(shp), mv.reshape(shp), vv.reshape(shp)

    grad_out = [grads[n].reshape(weights[n].shape) for n in WEIGHT_ORDER]
    return (total_loss, dx[None], *grad_out, *[delta[n] for n in WEIGHT_ORDER], *[new_m[n] for n in WEIGHT_ORDER],
            *[new_v[n] for n in WEIGHT_ORDER])
```

```python
import functools
import math

import jax
import jax.numpy as jnp
from jax import lax
from jax.experimental import pallas as pl
from jax.experimental.pallas import tpu as pltpu

F32 = jnp.float32
BF16 = jnp.bfloat16

D_MODEL = 1024
SSD_D_INNER = 2048
SSD_HEADS = 32
SSD_HEAD_DIM = 64
SSD_GROUPS = 4
SSD_HEADS_PER_GROUP = 8
SSD_STATE = 128
SSD_BC = 512
SSD_XBC = 3072
SSD_IN = 5152
SGU_WIDTH = 1024
SGU_GROUPS = 8
CHUNK = 128
IN_COLS = 9248
D_FF = 2816
NORM_EPS = 1e-6
LN_EPS = 1e-5
GROUP_COLS = SSD_HEADS_PER_GROUP * SSD_HEAD_DIM
LANES = 128

ADAM_LR = 0.001
ADAM_B1 = 0.9
ADAM_B2 = 0.999
ADAM_EPS = 1e-08
ADAM_WD = 0.01
ADAM_STEP = 10

N_CHIPS = 4
VMEM_LIMIT = 56 * 1024 * 1024

NT = (((1,), (1,)), ((), ()))
TN = (((0,), (0,)), ((), ()))
NN = (((1,), (0,)), ((), ()))


def _params(dims):
    return pltpu.CompilerParams(dimension_semantics=dims, vmem_limit_bytes=VMEM_LIMIT)


def _dot(a, b, dn=NN, precision=None):
    return lax.dot_general(a, b, dn, precision=precision, preferred_element_type=F32)


def _split3(x):
    hi = x.astype(BF16)
    rest = x - hi.astype(F32)
    mid = rest.astype(BF16)
    return hi, mid, (rest - mid.astype(F32)).astype(BF16)


def _dot_terms(terms, exact, dn=NN):
    out = None
    for t in terms:
        p = _dot(t, exact, dn)
        out = p if out is None else out + p
    return out


def _dot_exact_lhs(exact, terms):
    out = None
    for t in terms:
        p = _dot(exact, t)
        out = p if out is None else out + p
    return out


def _sigmoid(x):
    return 1.0 / (1.0 + jnp.exp(-x))


def _softplus(x):
    return jnp.maximum(x, 0.0) + jnp.log(1.0 + jnp.exp(-jnp.abs(x)))


def _matmul(pairs, *, trans_b=False, add=None, after=None, out_dtype=F32, tm=512, tn=512, name):
    def mat_shape(b):
        if isinstance(b, tuple) and b[1] == "cols":
            return (b[0].shape[1], b[0].shape[0] * b[0].shape[2])
        return b[0].shape[1:] if isinstance(b, tuple) else b.shape

    if isinstance(pairs[0][1], tuple) and pairs[0][1][1] == "cols":
        assert not trans_b and tn % LANES == 0 and pairs[0][1][0].shape[2] % tn == 0, name

    m = (pairs[0][0][0] if isinstance(pairs[0][0], tuple) else pairs[0][0]).shape[0]
    n = mat_shape(pairs[0][1])[0] if trans_b else mat_shape(pairs[0][1])[1]
    tm, tn = min(tm, m), min(tn, n)
    assert m % tm == 0 and n % tn == 0, (name, m, n, tm, tn)
    npairs = len(pairs)
    dn = NT if trans_b else NN

    def body(*refs):
        o_ref = refs[-1]
        acc = None
        for i in range(npairs):
            p = _dot(refs[2 * i][...].astype(BF16), refs[2 * i + 1][...].astype(BF16), dn)
            acc = p if acc is None else acc + p
        if add is not None:
            acc = acc + refs[2 * npairs][...]
        o_ref[...] = acc.astype(out_dtype)

    in_specs, args, fusible = [], [], []
    for a, b in pairs:
        bshape = mat_shape(b)
        k = bshape[1] if trans_b else bshape[0]
        assert bshape == ((n, k) if trans_b else (k, n)), (name, bshape)
        a, qa = a if isinstance(a, tuple) else (a, 0)
        assert a.shape[0] == m and a.shape[1] % k == 0, (name, a.shape, k)
        in_specs.append(pl.BlockSpec((tm, k), lambda i, j, qa=qa: (i, qa)))
        if isinstance(b, tuple) and b[1] == "cols":
            b = b[0]
            per = b.shape[2] // tn
            in_specs.append(pl.BlockSpec((None, k, tn), lambda i, j, per=per: (j // per, 0, j % per)))
        elif isinstance(b, tuple):
            b, qb = b
            if trans_b:
                in_specs.append(pl.BlockSpec((None, tn, k), lambda i, j, qb=qb: (qb, j, 0)))
            else:
                in_specs.append(pl.BlockSpec((None, k, tn), lambda i, j, qb=qb: (qb, 0, j)))
        elif trans_b:
            in_specs.append(pl.BlockSpec((tn, k), lambda i, j: (j, 0)))
        else:
            in_specs.append(pl.BlockSpec((k, tn), lambda i, j: (0, j)))
        args += [a, b]
        fusible += [False, b.ndim == 2]
    if add is not None:
        in_specs.append(pl.BlockSpec((tm, tn), lambda i, j: (i, j)))
        args.append(add)
    if after is not None:
        in_specs.append(pl.BlockSpec(memory_space=pl.ANY))
        args.append(after)
    return pl.pallas_call(
        body, name=name, grid=(m // tm, n // tn), in_specs=in_specs,
        out_specs=pl.BlockSpec((tm, tn), lambda i, j: (i, j)),
        out_shape=jax.ShapeDtypeStruct((m, n), out_dtype),
        compiler_params=pltpu.CompilerParams(
            dimension_semantics=("parallel", "parallel"), vmem_limit_bytes=VMEM_LIMIT,
            allow_input_fusion=fusible + [False] * (len(args) - len(fusible))),
    )(*args)


def _matmul_tn(a, b, *, tk, tn, tm=1024, out_dtype=BF16, stack_out=False, after=None, part_of=None, name):
    m, k = a.shape
    n = b.shape[1]
    tm, tk, tn = min(tm, m), min(tk, k), min(tn, n)
    assert m % tm == 0 and k % tk == 0 and n % tn == 0, (name, m, k, n)
    nm = m // tm
    blocks, first, buffer = part_of if part_of is not None else (None, 0, None)
    if stack_out:
        out_spec = pl.BlockSpec((None, tk, tn), lambda i, j, l: (j + first, i, 0))
        out_shape = jax.ShapeDtypeStruct((blocks or n // tn, k, tn), out_dtype)
    else:
        out_spec = pl.BlockSpec((tk, tn), lambda i, j, l: (i + first, j))
        out_shape = jax.ShapeDtypeStruct((blocks * tk if blocks else k, n), out_dtype)

    def body(a_ref, b_ref, *rest):
        o_ref, acc = rest[-2:]
        mi = pl.program_id(2)

        @pl.when(mi == 0)
        def _():
            acc[...] = jnp.zeros_like(acc)

        acc[...] += _dot(a_ref[...].astype(BF16), b_ref[...].astype(BF16), TN)

        @pl.when(mi == nm - 1)
        def _():
            o_ref[...] = acc[...].astype(out_dtype)

    in_specs = [pl.BlockSpec((tm, tk), lambda i, j, l: (l, i)), pl.BlockSpec((tm, tn), lambda i, j, l: (l, j))]
    args = [a, b]
    if after is not None:
        in_specs.append(pl.BlockSpec(memory_space=pl.ANY))
        args.append(after)
    aliases = {}
    if buffer is not None:
        aliases = {len(args): 0}
        in_specs.append(pl.BlockSpec(memory_space=pl.ANY))
        args.append(buffer)
    return pl.pallas_call(
        body, name=name, grid=(k // tk, n // tn, nm), in_specs=in_specs,
        out_specs=out_spec, out_shape=out_shape, input_output_aliases=aliases,
        scratch_shapes=[pltpu.VMEM((tk, tn), F32)],
        compiler_params=_params(("parallel", "parallel", "arbitrary")),
    )(*args)


def _rms_fwd(x, w, *, after=None, name, tm=512):
    s, d = x.shape
    tm = min(tm, s)
    extra = [] if after is None else [after]

    def body(x_ref, w_ref, *rest):
        o_ref = rest[-1]
        xv = x_ref[...]
        r = lax.rsqrt(jnp.mean(xv * xv, axis=-1, keepdims=True) + NORM_EPS)
        o_ref[...] = (xv * r * w_ref[...]).astype(BF16)

    return pl.pallas_call(
        body, name=name, grid=(s // tm,),
        in_specs=[pl.BlockSpec((tm, d), lambda i: (i, 0)), pl.BlockSpec((1, d), lambda i: (0, 0))]
        + [pl.BlockSpec(memory_space=pl.ANY)] * len(extra),
        out_specs=pl.BlockSpec((tm, d), lambda i: (i, 0)),
        out_shape=jax.ShapeDtypeStruct((s, d), BF16),
        compiler_params=_params(("parallel",)),
    )(x, w, *extra)


def _rms_bwd(x, w, dn, dres, *, bf16_copy, name, tm=512):
    s, d = x.shape
    tm = min(tm, s)

    def body(x_ref, w_ref, dn_ref, dres_ref, dx_ref, *rest):
        dw_ref = rest[-1]

        @pl.when(pl.program_id(0) == 0)
        def _():
            dw_ref[...] = jnp.zeros_like(dw_ref)

        xv = x_ref[...]
        r = lax.rsqrt(jnp.mean(xv * xv, axis=-1, keepdims=True) + NORM_EPS)
        xhat = xv * r
        dnv = dn_ref[...].astype(F32)
        dxhat = dnv * w_ref[...]
        dx = dres_ref[...] + r * (dxhat - xhat * jnp.mean(dxhat * xhat, axis=-1, keepdims=True))
        dx_ref[...] = dx
        if bf16_copy:
            rest[0][...] = dx.astype(BF16)
        dw_ref[...] += jnp.sum(dnv * xhat, axis=0, keepdims=True)

    tile = pl.BlockSpec((tm, d), lambda i: (i, 0))
    row = pl.BlockSpec((1, d), lambda i: (0, 0))
    copies = [jax.ShapeDtypeStruct((s, d), BF16)] if bf16_copy else []
    return pl.pallas_call(
        body, name=name, grid=(s // tm,),
        in_specs=[tile, row, tile, tile], out_specs=[tile] + [tile] * len(copies) + [row],
        out_shape=[jax.ShapeDtypeStruct((s, d), F32)] + copies + [jax.ShapeDtypeStruct((1, d), F32)],
        compiler_params=_params(("arbitrary",)),
    )(x, w, dn, dres)


def _final_fwd_bwd(h2, wf, target, *, name, tm=512):
    s, d = h2.shape
    tm = min(tm, s)

    def body(h_ref, w_ref, t_ref, loss_ref, dh_ref, dhb_ref, dw_ref):
        @pl.when(pl.program_id(0) == 0)
        def _():
            dw_ref[...] = jnp.zeros_like(dw_ref)
            loss_ref[...] = jnp.zeros_like(loss_ref)

        hv = h_ref[...]
        r = lax.rsqrt(jnp.mean(hv * hv, axis=-1, keepdims=True) + NORM_EPS)
        xhat = hv * r
        err = xhat * w_ref[...] - t_ref[...]
        per_tok = jnp.mean(err * err, axis=-1, keepdims=True)
        loss_ref[...] += 0.5 * jnp.sum(per_tok, axis=0, keepdims=True)
        dy = err * (1.0 / d)
        dxhat = dy * w_ref[...]
        dh = r * (dxhat - xhat * jnp.mean(dxhat * xhat, axis=-1, keepdims=True))
        dh_ref[...] = dh
        dhb_ref[...] = dh.astype(BF16)
        dw_ref[...] += jnp.sum(dy * xhat, axis=0, keepdims=True)

    tile = pl.BlockSpec((tm, d), lambda i: (i, 0))
    row = pl.BlockSpec((1, d), lambda i: (0, 0))
    return pl.pallas_call(
        body, name=name, grid=(s // tm,),
        in_specs=[tile, row, tile],
        out_specs=[pl.BlockSpec((1, 1), lambda i: (0, 0)), tile, tile, row],
        out_shape=[jax.ShapeDtypeStruct((1, 1), F32), jax.ShapeDtypeStruct((s, d), F32),
                   jax.ShapeDtypeStruct((s, d), BF16), jax.ShapeDtypeStruct((1, d), F32)],
        compiler_params=_params(("arbitrary",)),
    )(h2, wf, target)


CONV_ROWS = 256
CONV_ROWS_FWD = 512
HALO = 8


def _rows_with_halo(ref, r0, rows, s, before, after):
    tile = 16 if ref.dtype == BF16 else HALO
    parts = []
    if before:
        prev = ref[pl.ds(pl.multiple_of(jnp.maximum(r0 - tile, 0), tile), tile), :].astype(F32)[tile - HALO:]
        parts.append(jnp.where(r0 > 0, prev, 0.0))
    parts.append(ref[pl.ds(r0, rows), :].astype(F32))
    if after:
        nxt = ref[pl.ds(pl.multiple_of(jnp.minimum(r0 + rows, s - tile), tile), tile), :].astype(F32)[:HALO]
        parts.append(jnp.where(r0 + rows < s, nxt, 0.0))
    return jnp.concatenate(parts, axis=0) if len(parts) > 1 else parts[0]


def _window(x_ref, r0, s, after):
    return _rows_with_halo(x_ref, r0, CONV_ROWS_FWD, s, True, after).astype(F32)


def _shifted(window, k, rows):
    if k == 0:
        return window[HALO:HALO + rows]
    return pltpu.roll(window, k, 0)[HALO:HALO + rows]


def _conv_taps(window, w_ref, kk, rows):
    acc = None
    for i in range(kk):
        term = w_ref[i:i + 1, :] * _shifted(window, kk - 1 - i, rows)
        acc = term if acc is None else acc + term
    return acc


def _row_loop(rows, step):
    def body(r, carry):
        return step(pl.multiple_of(r * rows, rows), carry)
    return body


def _conv_bwd_rows(x, dpe, w_ref, kk):
    dp = dpe[:CONV_ROWS]
    dx = None
    dws = []
    for i in range(kk):
        k = kk - 1 - i
        later = dp if k == 0 else pltpu.roll(dpe, dpe.shape[0] - k, 0)[:CONV_ROWS]
        dws.append(jnp.sum(later * x, axis=0, keepdims=True))
        term = w_ref[i:i + 1, :] * later
        dx = term if dx is None else dx + term
    return dx, dws, jnp.sum(dp, axis=0, keepdims=True)


def _conv_a_fwd(xraw, w, b, *, name, tc=128):
    s, c = xraw.shape
    kk = 4

    def body(x_ref, w_ref, b_ref, o_ref, pre_ref):
        def step(r0, carry):
            pre = _conv_taps(_window(x_ref, r0, s, False), w_ref, kk, CONV_ROWS_FWD) + b_ref[...]
            o_ref[pl.ds(r0, CONV_ROWS_FWD), :] = pre * _sigmoid(pre)
            pre_ref[pl.ds(r0, CONV_ROWS_FWD), :] = pre.astype(BF16)
            return carry

        lax.fori_loop(0, s // CONV_ROWS_FWD, _row_loop(CONV_ROWS_FWD, step), 0)

    col = pl.BlockSpec((s, tc), lambda j: (0, j))
    return pl.pallas_call(
        body, name=name, grid=(c // tc,),
        in_specs=[col, pl.BlockSpec((8, tc), lambda j: (0, j)), pl.BlockSpec((1, tc), lambda j: (0, j))],
        out_specs=[col, col], out_shape=[jax.ShapeDtypeStruct((s, c), F32), jax.ShapeDtypeStruct((s, c), BF16)],
        compiler_params=_params(("parallel",)),
    )(xraw, w, b)


def _conv_a_bwd(xraw, pre, w, dy, *, name, tc=128):
    s, c = xraw.shape
    kk = 4

    def body(x_ref, pre_ref, w_ref, dy_ref, dx_ref, dw_ref, db_ref):
        def step(r0, carry):
            pre = _rows_with_halo(pre_ref, r0, CONV_ROWS, s, False, True)
            sg = _sigmoid(pre)
            dpe = _rows_with_halo(dy_ref, r0, CONV_ROWS, s, False, True) * (sg * (1.0 + pre * (1.0 - sg)))
            dx, dws, db = _conv_bwd_rows(x_ref[pl.ds(r0, CONV_ROWS), :].astype(F32), dpe, w_ref, kk)
            dx_ref[pl.ds(r0, CONV_ROWS), :] = dx.astype(BF16)
            return tuple(acc + new for acc, new in zip(carry, dws + [db]))

        zero = jnp.zeros((1, tc), F32)
        sums = lax.fori_loop(0, s // CONV_ROWS, _row_loop(CONV_ROWS, step), (zero,) * (kk + 1))
        db_ref[...] = sums[kk]
        dw_ref[...] = jnp.concatenate(list(sums[:kk]) + [jnp.zeros((8 - kk, tc), F32)], axis=0)

    col = pl.BlockSpec((s, tc), lambda j: (0, j))
    w8 = pl.BlockSpec((8, tc), lambda j: (0, j))
    row = pl.BlockSpec((1, tc), lambda j: (0, j))
    return pl.pallas_call(
        body, name=name, grid=(c // tc,),
        in_specs=[col, col, w8, col], out_specs=[col, w8, row],
        out_shape=[jax.ShapeDtypeStruct((s, c), BF16), jax.ShapeDtypeStruct((8, c), F32),
                   jax.ShapeDtypeStruct((1, c), F32)],
        compiler_params=_params(("parallel",)),
    )(xraw, pre, w, dy)


def _conv_f_fwd(up_raw, w, b, *, name, tc=128):
    s, c2 = up_raw.shape
    c = c2 // 2
    nb = c // tc
    kk = 3

    def body(xa_ref, xv_ref, wa_ref, wv_ref, ba_ref, bv_ref, o_ref, a_out, v_out):
        def step(r0, carry):
            a = _conv_taps(_window(xa_ref, r0, s, False), wa_ref, kk, CONV_ROWS_FWD) + ba_ref[...]
            v = _conv_taps(_window(xv_ref, r0, s, False), wv_ref, kk, CONV_ROWS_FWD) + bv_ref[...]
            o_ref[pl.ds(r0, CONV_ROWS_FWD), :] = (a * _sigmoid(a) * v).astype(BF16)
            a_out[pl.ds(r0, CONV_ROWS_FWD), :] = a.astype(BF16)
            v_out[pl.ds(r0, CONV_ROWS_FWD), :] = v.astype(BF16)
            return carry

        lax.fori_loop(0, s // CONV_ROWS_FWD, _row_loop(CONV_ROWS_FWD, step), 0)

    col_a = pl.BlockSpec((s, tc), lambda j: (0, j))
    col_v = pl.BlockSpec((s, tc), lambda j: (0, j + nb))
    half = jax.ShapeDtypeStruct((s, c), BF16)
    return pl.pallas_call(
        body, name=name, grid=(nb,),
        in_specs=[col_a, col_v, pl.BlockSpec((8, tc), lambda j: (0, j)), pl.BlockSpec((8, tc), lambda j: (0, j + nb)),
                  pl.BlockSpec((1, tc), lambda j: (0, j)), pl.BlockSpec((1, tc), lambda j: (0, j + nb))],
        out_specs=[col_a, col_a, col_a], out_shape=[half, half, half],
        compiler_params=_params(("parallel",)),
    )(up_raw, up_raw, w, w, b, b)


def _conv_f_bwd(up_raw, a_pre, v_pre, w, dact, *, name, tc=128):
    s, c2 = up_raw.shape
    c = c2 // 2
    nb = c // tc
    kk = 3

    def body(xa_ref, xv_ref, a_ref, v_ref, wa_ref, wv_ref, d_ref,
             dxa_ref, dxv_ref, dwa_ref, dwv_ref, dba_ref, dbv_ref):
        def step(r0, carry):
            a = _rows_with_halo(a_ref, r0, CONV_ROWS, s, False, True)
            v = _rows_with_halo(v_ref, r0, CONV_ROWS, s, False, True)
            sg = _sigmoid(a)
            d = _rows_with_halo(d_ref, r0, CONV_ROWS, s, False, True)
            rows = pl.ds(r0, CONV_ROWS)
            dxa, dwas, dba = _conv_bwd_rows(xa_ref[rows, :].astype(F32), d * v * (sg * (1.0 + a * (1.0 - sg))),
                                            wa_ref, kk)
            dxv, dwvs, dbv = _conv_bwd_rows(xv_ref[rows, :].astype(F32), d * (a * sg), wv_ref, kk)
            dxa_ref[pl.ds(r0, CONV_ROWS), :] = dxa.astype(BF16)
            dxv_ref[pl.ds(r0, CONV_ROWS), :] = dxv.astype(BF16)
            return tuple(acc + new for acc, new in zip(carry, dwas + [dba] + dwvs + [dbv]))

        zero = jnp.zeros((1, tc), F32)
        sums = lax.fori_loop(0, s // CONV_ROWS, _row_loop(CONV_ROWS, step), (zero,) * (2 * kk + 2))
        pad = [jnp.zeros((8 - kk, tc), F32)]
        dwa_ref[...] = jnp.concatenate(list(sums[:kk]) + pad, axis=0)
        dba_ref[...] = sums[kk]
        dwv_ref[...] = jnp.concatenate(list(sums[kk + 1:2 * kk + 1]) + pad, axis=0)
        dbv_ref[...] = sums[2 * kk + 1]

    col_a = pl.BlockSpec((s, tc), lambda j: (0, j))
    col_v = pl.BlockSpec((s, tc), lambda j: (0, j + nb))
    w_a = pl.BlockSpec((8, tc), lambda j: (0, j))
    w_v = pl.BlockSpec((8, tc), lambda j: (0, j + nb))
    r_a = pl.BlockSpec((1, tc), lambda j: (0, j))
    r_v = pl.BlockSpec((1, tc), lambda j: (0, j + nb))
    outs = pl.pallas_call(
        body, name=name, grid=(nb,),
        in_specs=[col_a, col_v, col_a, col_a, w_a, w_v, col_a],
        out_specs=[col_a, col_a, w_a, w_a, r_a, r_a],
        out_shape=[jax.ShapeDtypeStruct((s, c), BF16), jax.ShapeDtypeStruct((s, c), BF16),
                   jax.ShapeDtypeStruct((8, c), F32), jax.ShapeDtypeStruct((8, c), F32),
                   jax.ShapeDtypeStruct((1, c), F32), jax.ShapeDtypeStruct((1, c), F32)],
        compiler_params=_params(("parallel",)),
    )(up_raw, up_raw, a_pre, v_pre, w, w, dact)
    return outs


def _tri_masks():
    row = lax.broadcasted_iota(jnp.int32, (CHUNK, CHUNK), 0)
    col = lax.broadcasted_iota(jnp.int32, (CHUNK, CHUNK), 1)
    return row >= col, row <= col


def _ssd_fwd(xbc, dt_raw, z, dt_bias, a_log, a_log_x, d_skip_x, norm_w, expand, *, name):
    s = xbc.shape[0]
    nc = s // CHUNK

    def body(xbc_ref, dtr_ref, z_ref, dtb_ref, alog_ref, alogx_ref, dskx_ref, nw_ref, e_ref,
             y_ref, ya_ref, st_ref, state):
        @pl.when(pl.program_id(0) == 0)
        def _():
            state[...] = jnp.zeros_like(state)

        st_ref[0] = state[...]
        lower, _ = _tri_masks()
        dt = _softplus(dtr_ref[...] + dtb_ref[...])
        adt = dt * (-jnp.exp(alog_ref[...]))
        acum = _dot_exact_lhs(lower.astype(BF16), _split3(adt))
        acum_t = acum.T
        dt_terms, acum_terms = _split3(dt), _split3(acum)
        for g in range(SSD_GROUPS):
            sl = slice(GROUP_COLS * g, GROUP_COLS * (g + 1))
            dt_x = _dot_terms(dt_terms[:2], e_ref[:, sl])
            acum_x = _dot_terms(acum_terms, e_ref[:, sl])
            tot_x = jnp.sum(dt_x * (-jnp.exp(alogx_ref[:, sl])), axis=0, keepdims=True)
            xs = xbc_ref[:, sl]
            xdt = xs * dt_x
            xdt_b = xdt.astype(BF16)
            bg = xbc_ref[:, SSD_D_INNER + SSD_STATE * g:SSD_D_INNER + SSD_STATE * (g + 1)].astype(BF16)
            cg = xbc_ref[:, SSD_D_INNER + SSD_BC + SSD_STATE * g:SSD_D_INNER + SSD_BC + SSD_STATE * (g + 1)].astype(BF16)
            cb = _dot(cg, bg, NT)
            st_g = state[:, sl]
            y_off = _dot(cg, st_g.astype(BF16)) * jnp.exp(acum_x)
            parts = []
            for r in range(SSD_HEADS_PER_GROUP):
                h = SSD_HEADS_PER_GROUP * g + r
                dec = jnp.exp(jnp.where(lower, acum[:, h:h + 1] - acum_t[h:h + 1, :], -jnp.inf))
                parts.append(_dot((cb * dec).astype(BF16), xdt_b[:, SSD_HEAD_DIM * r:SSD_HEAD_DIM * (r + 1)]))
            y_ref[:, sl] = jnp.concatenate(parts, axis=1) + y_off + dskx_ref[:, sl] * xs
            wgt = (xdt * jnp.exp(tot_x - acum_x)).astype(BF16)
            state[:, sl] = st_g * jnp.exp(tot_x) + _dot(bg, wgt, TN)
        zv = z_ref[...].astype(F32)
        q = y_ref[...] * (zv * _sigmoid(zv))
        r = lax.rsqrt(jnp.mean(q * q, axis=-1, keepdims=True) + NORM_EPS)
        ya_ref[...] = (q * r * nw_ref[...]).astype(BF16)

    def chunk(w):
        return pl.BlockSpec((CHUNK, w), lambda c: (c, 0))

    def const(shape):
        return pl.BlockSpec(shape, lambda c: (0,) * len(shape))

    return pl.pallas_call(
        body, name=name, grid=(nc,),
        in_specs=[chunk(SSD_XBC), chunk(LANES), chunk(SSD_D_INNER), const((1, LANES)), const((1, LANES)),
                  const((1, SSD_D_INNER)), const((1, SSD_D_INNER)), const((1, SSD_D_INNER)),
                  const((LANES, SSD_D_INNER))],
        out_specs=[chunk(SSD_D_INNER), chunk(SSD_D_INNER),
                   pl.BlockSpec((1, SSD_STATE, SSD_D_INNER), lambda c: (c, 0, 0))],
        out_shape=[jax.ShapeDtypeStruct((s, SSD_D_INNER), F32), jax.ShapeDtypeStruct((s, SSD_D_INNER), BF16),
                   jax.ShapeDtypeStruct((nc, SSD_STATE, SSD_D_INNER), F32)],
        scratch_shapes=[pltpu.VMEM((SSD_STATE, SSD_D_INNER), F32)],
        compiler_params=_params(("arbitrary",)),
    )(xbc, dt_raw, z, dt_bias, a_log, a_log_x, d_skip_x, norm_w, expand)


def _ssd_bwd(dya, y, z, xbc, dt_raw, states, dt_bias, a_log, a_log_x, d_skip_x, norm_w, expand, expand_t, *, name):
    s = xbc.shape[0]
    nc = s // CHUNK

    def body(dya_ref, y_ref, z_ref, xbc_ref, dtr_ref, stp_ref, dtb_ref, alog_ref, alogx_ref, dskx_ref, nw_ref,
             e_ref, et_ref, dz_ref, dxbc_ref, ddt_ref, dnw_ref, ddsk_ref, dalog_ref, ddtb_ref,
             dstate, dy_sc, dskcol):
        i = pl.program_id(0)

        @pl.when(i == 0)
        def _():
            dstate[...] = jnp.zeros_like(dstate)
            dskcol[...] = jnp.zeros_like(dskcol)
            dnw_ref[...] = jnp.zeros_like(dnw_ref)
            dalog_ref[...] = jnp.zeros_like(dalog_ref)
            ddtb_ref[...] = jnp.zeros_like(ddtb_ref)
            ddsk_ref[...] = jnp.zeros_like(ddsk_ref)

        lower, upper = _tri_masks()
        rows = lax.broadcasted_iota(jnp.int32, (CHUNK, LANES), 0)
        pre = dtr_ref[...] + dtb_ref[...]
        dt = _softplus(pre)
        a = -jnp.exp(alog_ref[...])
        acum = _dot_exact_lhs(lower.astype(BF16), _split3(dt * a))
        acum_t = acum.T
        dt_terms, acum_terms = _split3(dt), _split3(acum)

        yv = y_ref[...]
        zv = z_ref[...].astype(F32)
        sz = _sigmoid(zv)
        silu_z = zv * sz
        q = yv * silu_z
        r = lax.rsqrt(jnp.mean(q * q, axis=-1, keepdims=True) + NORM_EPS)
        qhat = q * r
        dyav = dya_ref[...]
        dqhat = dyav * nw_ref[...]
        dnw_ref[...] += jnp.sum(dyav * qhat, axis=0, keepdims=True)
        dq = r * (dqhat - qhat * jnp.mean(dqhat * qhat, axis=-1, keepdims=True))
        dy_sc[...] = dq * silu_z
        dz_ref[...] = (dq * yv * (sz * (1.0 + zv * (1.0 - sz)))).astype(BF16)

        da_cum = jnp.zeros((CHUNK, LANES), F32)
        ddt = jnp.zeros((CHUNK, LANES), F32)
        for g in range(SSD_GROUPS):
            sl = slice(GROUP_COLS * g, GROUP_COLS * (g + 1))
            et_g = et_ref[sl, :]
            dt_x = _dot_terms(dt_terms[:2], e_ref[:, sl])
            acum_x = _dot_terms(acum_terms, e_ref[:, sl])
            tot_x = jnp.sum(dt_x * (-jnp.exp(alogx_ref[:, sl])), axis=0, keepdims=True)
            e_tot = jnp.exp(tot_x)
            dec_s = jnp.exp(tot_x - acum_x)
            xs = xbc_ref[:, sl]
            xdt = xs * dt_x
            xdt_b = xdt.astype(BF16)
            dy = dy_sc[:, sl]
            dy_b = dy.astype(BF16)
            dskx = dskx_ref[:, sl]
            y_ssd = y_ref[:, sl] - dskx * xs
            dskcol[:, sl] += jnp.sum(dy * xs, axis=0, keepdims=True)
            bg = xbc_ref[:, SSD_D_INNER + SSD_STATE * g:SSD_D_INNER + SSD_STATE * (g + 1)].astype(BF16)
            cg = xbc_ref[:, SSD_D_INNER + SSD_BC + SSD_STATE * g:SSD_D_INNER + SSD_BC + SSD_STATE * (g + 1)].astype(BF16)
            cb_t = _dot(bg, cg, NT)
            sp = stp_ref[0, :, sl]
            ds_g = dstate[:, sl]
            ds_b = ds_g.astype(BF16)
            dye_b = (dy * jnp.exp(acum_x)).astype(BF16)
            dc = _dot(dye_b, sp.astype(BF16), NT)
            dxdt_state = dec_s * _dot(bg, ds_b)
            db = _dot((xdt * dec_s).astype(BF16), ds_b, NT)
            dcb_t = jnp.zeros((CHUNK, CHUNK), F32)
            parts = []
            for rr in range(SSD_HEADS_PER_GROUP):
                h = SSD_HEADS_PER_GROUP * g + rr
                hs = slice(SSD_HEAD_DIM * rr, SSD_HEAD_DIM * (rr + 1))
                dec_t = jnp.exp(jnp.where(upper, acum_t[h:h + 1, :] - acum[:, h:h + 1], -jnp.inf))
                parts.append(_dot((cb_t * dec_t).astype(BF16), dy_b[:, hs]))
                dcb_t = dcb_t + _dot(xdt_b[:, hs], dy_b[:, hs], NT) * dec_t
            dxdt = jnp.concatenate(parts, axis=1) + dxdt_state
            dcb_tb = dcb_t.astype(BF16)
            dc = dc + _dot(dcb_tb, bg, TN)
            db = db + _dot(dcb_tb, cg)
            tot_col = jnp.sum(ds_g * sp, axis=0, keepdims=True) * e_tot + jnp.sum(dxdt_state * xdt, axis=0, keepdims=True)
            d_tot = _dot_terms(_split3(jnp.broadcast_to(tot_col, (8, GROUP_COLS))), et_g)
            d_tot = jnp.max(d_tot, axis=0, keepdims=True)
            pair_sums = dy_b.astype(F32) * y_ssd - xdt_b.astype(F32) * dxdt
            da_cum = da_cum + _dot_terms(_split3(pair_sums), et_g) + jnp.where(rows == CHUNK - 1, d_tot, 0.0)
            ddt = ddt + _dot_terms(_split3(dxdt * xs)[:2], et_g)
            dxbc_ref[:, sl] = dy * dskx + dxdt * dt_x
            dxbc_ref[:, SSD_D_INNER + SSD_STATE * g:SSD_D_INNER + SSD_STATE * (g + 1)] = db
            dxbc_ref[:, SSD_D_INNER + SSD_BC + SSD_STATE * g:SSD_D_INNER + SSD_BC + SSD_STATE * (g + 1)] = dc
            dstate[:, sl] = e_tot * ds_g + _dot(cg, dye_b, TN)

        dadt = _dot_exact_lhs(upper.astype(BF16), _split3(da_cum))
        ddt = ddt + dadt * a
        dalog_ref[...] += jnp.sum(dadt * dt, axis=0, keepdims=True)
        dpre = ddt * _sigmoid(pre)
        ddtb_ref[...] += jnp.sum(dpre, axis=0, keepdims=True)
        ddt_ref[...] = dpre.astype(BF16)

        @pl.when(i == nc - 1)
        def _():
            dalog_ref[...] = dalog_ref[...] * a
            dsk = _dot_terms(_split3(jnp.broadcast_to(dskcol[...], (8, SSD_D_INNER))), et_ref[...])
            ddsk_ref[...] = jnp.max(dsk, axis=0, keepdims=True)

    def chunk(w):
        return pl.BlockSpec((CHUNK, w), lambda i: (nc - 1 - i, 0))

    def const(shape):
        return pl.BlockSpec(shape, lambda i: (0,) * len(shape))

    return pl.pallas_call(
        body, name=name, grid=(nc,),
        in_specs=[chunk(SSD_D_INNER), chunk(SSD_D_INNER), chunk(SSD_D_INNER), chunk(SSD_XBC), chunk(LANES),
                  pl.BlockSpec((1, SSD_STATE, SSD_D_INNER), lambda i: (nc - 1 - i, 0, 0)),
                  const((1, LANES)), const((1, LANES)), const((1, SSD_D_INNER)), const((1, SSD_D_INNER)),
                  const((1, SSD_D_INNER)), const((LANES, SSD_D_INNER)), const((SSD_D_INNER, LANES))],
        out_specs=[chunk(SSD_D_INNER), chunk(SSD_XBC), chunk(LANES), const((1, SSD_D_INNER)), const((1, LANES)),
                   const((1, LANES)), const((1, LANES))],
        out_shape=[jax.ShapeDtypeStruct((s, SSD_D_INNER), BF16), jax.ShapeDtypeStruct((s, SSD_XBC), F32),
                   jax.ShapeDtypeStruct((s, LANES), BF16), jax.ShapeDtypeStruct((1, SSD_D_INNER), F32),
                   jax.ShapeDtypeStruct((1, LANES), F32), jax.ShapeDtypeStruct((1, LANES), F32),
                   jax.ShapeDtypeStruct((1, LANES), F32)],
        scratch_shapes=[pltpu.VMEM((SSD_STATE, SSD_D_INNER), F32), pltpu.VMEM((CHUNK, SSD_D_INNER), F32),
                        pltpu.VMEM((1, SSD_D_INNER), F32)],
        compiler_params=_params(("arbitrary",)),
    )(dya, y, z, xbc, dt_raw, states, dt_bias, a_log, a_log_x, d_skip_x, norm_w, expand, expand_t)


GELU_K = math.sqrt(2.0 / math.pi)
GELU_C = 0.044715


def _gelu(x):
    return 0.5 * x * (1.0 + jnp.tanh(GELU_K * (x + GELU_C * x * x * x)))


def _gelu_grad(x):
    t = jnp.tanh(GELU_K * (x + GELU_C * x * x * x))
    return 0.5 * (1.0 + t) + 0.5 * x * (1.0 - t * t) * (GELU_K * (1.0 + 3.0 * GELU_C * x * x))


def _sgu_pre(uv_ref, uvb_ref, lnw_ref, lnb_ref):
    uv = uv_ref[...].astype(F32) + uvb_ref[...]
    guv = _gelu(uv)
    u = guv[:, :SGU_WIDTH]
    v = guv[:, SGU_WIDTH:]
    mu = jnp.mean(v, axis=-1, keepdims=True)
    vc = v - mu
    rstd = lax.rsqrt(jnp.mean(vc * vc, axis=-1, keepdims=True) + LN_EPS)
    vhat = vc * rstd
    vn = vhat * lnw_ref[...] + lnb_ref[...]
    return uv, u, vhat, rstd, vn


def _sgu_fwd(uv_raw, uv_b, ln_w, ln_b, w_sp, b_sp_t, *, name):
    s = uv_raw.shape[0]
    nc = s // CHUNK

    def body(uv_ref, uvb_ref, lnw_ref, lnb_ref, w_ref, bt_ref, o_ref):
        lower, _ = _tri_masks()
        _, u, _, _, vn = _sgu_pre(uv_ref, uvb_ref, lnw_ref, lnb_ref)
        vn_b = vn.astype(BF16)
        bt = bt_ref[...]
        for g in range(SGU_GROUPS):
            gs = slice(LANES * g, LANES * (g + 1))
            wc = jnp.where(lower, w_ref[g], 0.0).astype(BF16)
            mixed = _dot(wc, vn_b[:, gs]) + bt[:, g:g + 1]
            o_ref[:, gs] = (u[:, gs] * mixed).astype(BF16)

    def const(shape):
        return pl.BlockSpec(shape, lambda c: (0,) * len(shape))

    return pl.pallas_call(
        body, name=name, grid=(nc,),
        in_specs=[pl.BlockSpec((CHUNK, 2 * SGU_WIDTH), lambda c: (c, 0)), const((1, 2 * SGU_WIDTH)),
                  const((1, SGU_WIDTH)), const((1, SGU_WIDTH)), const((SGU_GROUPS, CHUNK, CHUNK)),
                  const((CHUNK, LANES))],
        out_specs=pl.BlockSpec((CHUNK, SGU_WIDTH), lambda c: (c, 0)),
        out_shape=jax.ShapeDtypeStruct((s, SGU_WIDTH), BF16),
        compiler_params=_params(("parallel",)),
    )(uv_raw, uv_b, ln_w, ln_b, w_sp, b_sp_t)


def _sgu_bwd(uv_raw, dyb, uv_b, ln_w, ln_b, w_sp, b_sp_t, group_sum, *, name):
    s = uv_raw.shape[0]
    nc = s // CHUNK

    def body(uv_ref, dy_ref, uvb_ref, lnw_ref, lnb_ref, w_ref, bt_ref, gsum_ref,
             duv_ref, dw_ref, dbt_ref, dlnw_ref, dlnb_ref, duvb_ref):
        @pl.when(pl.program_id(0) == 0)
        def _():
            dw_ref[...] = jnp.zeros_like(dw_ref)
            dbt_ref[...] = jnp.zeros_like(dbt_ref)
            dlnw_ref[...] = jnp.zeros_like(dlnw_ref)
            dlnb_ref[...] = jnp.zeros_like(dlnb_ref)
            duvb_ref[...] = jnp.zeros_like(duvb_ref)

        lower, _ = _tri_masks()
        uv, u, vhat, rstd, vn = _sgu_pre(uv_ref, uvb_ref, lnw_ref, lnb_ref)
        vn_b = vn.astype(BF16)
        bt = bt_ref[...]
        dy = dy_ref[...].astype(F32)
        du_parts, dvn_parts, dmix_parts = [], [], []
        for g in range(SGU_GROUPS):
            gs = slice(LANES * g, LANES * (g + 1))
            wc = jnp.where(lower, w_ref[g], 0.0).astype(BF16)
            mixed = _dot(wc, vn_b[:, gs]) + bt[:, g:g + 1]
            du_parts.append(dy[:, gs] * mixed)
            dmix = dy[:, gs] * u[:, gs]
            dmix_b = dmix.astype(BF16)
            dmix_parts.append(dmix)
            dw_ref[g] += jnp.where(lower, _dot(dmix_b, vn_b[:, gs], NT), 0.0)
            dvn_parts.append(_dot(wc, dmix_b, TN))
        dmixed = jnp.concatenate(dmix_parts, axis=1)
        dbt_ref[...] += _dot_terms(_split3(dmixed), gsum_ref[...])
        dvn = jnp.concatenate(dvn_parts, axis=1)
        dlnw_ref[...] += jnp.sum(dvn * vhat, axis=0, keepdims=True)
        dlnb_ref[...] += jnp.sum(dvn, axis=0, keepdims=True)
        dvhat = dvn * lnw_ref[...]
        dv = rstd * (dvhat - jnp.mean(dvhat, axis=-1, keepdims=True)
                     - vhat * jnp.mean(dvhat * vhat, axis=-1, keepdims=True))
        dguv = jnp.concatenate(du_parts + [dv], axis=1)
        duv = dguv * _gelu_grad(uv)
        duvb_ref[...] += jnp.sum(duv, axis=0, keepdims=True)
        duv_ref[...] = duv.astype(BF16)

    def const(shape):
        return pl.BlockSpec(shape, lambda c: (0,) * len(shape))

    return pl.pallas_call(
        body, name=name, grid=(nc,),
        in_specs=[pl.BlockSpec((CHUNK, 2 * SGU_WIDTH), lambda c: (c, 0)),
                  pl.BlockSpec((CHUNK, SGU_WIDTH), lambda c: (c, 0)), const((1, 2 * SGU_WIDTH)),
                  const((1, SGU_WIDTH)), const((1, SGU_WIDTH)), const((SGU_GROUPS, CHUNK, CHUNK)),
                  const((CHUNK, LANES)), const((SGU_WIDTH, LANES))],
        out_specs=[pl.BlockSpec((CHUNK, 2 * SGU_WIDTH), lambda c: (c, 0)), const((SGU_GROUPS, CHUNK, CHUNK)),
                   const((CHUNK, LANES)), const((1, SGU_WIDTH)), const((1, SGU_WIDTH)), const((1, 2 * SGU_WIDTH))],
        out_shape=[jax.ShapeDtypeStruct((s, 2 * SGU_WIDTH), BF16),
                   jax.ShapeDtypeStruct((SGU_GROUPS, CHUNK, CHUNK), F32), jax.ShapeDtypeStruct((CHUNK, LANES), F32),
                   jax.ShapeDtypeStruct((1, SGU_WIDTH), F32), jax.ShapeDtypeStruct((1, SGU_WIDTH), F32),
                   jax.ShapeDtypeStruct((1, 2 * SGU_WIDTH), F32)],
        compiler_params=_params(("arbitrary",)),
    )(uv_raw, dyb, uv_b, ln_w, ln_b, w_sp, b_sp_t, group_sum)


def _gate_fwd(gates_raw, b_gate, p_a, p_b, *, name, tm=512):
    s = p_a.shape[0]
    tm = min(tm, s)

    def body(ga_ref, gb_ref, ba_ref, bb_ref, pa_ref, pb_ref, o_ref):
        ga = _sigmoid(ga_ref[...].astype(F32) + ba_ref[...])
        gb = _sigmoid(gb_ref[...].astype(F32) + bb_ref[...])
        o_ref[...] = (ga * pa_ref[...].astype(F32) + gb * pb_ref[...].astype(F32)).astype(BF16)

    t_a = pl.BlockSpec((tm, D_MODEL), lambda i: (i, 0))
    t_b = pl.BlockSpec((tm, D_MODEL), lambda i: (i, 1))
    r_a = pl.BlockSpec((1, D_MODEL), lambda i: (0, 0))
    r_b = pl.BlockSpec((1, D_MODEL), lambda i: (0, 1))
    return pl.pallas_call(
        body, name=name, grid=(s // tm,),
        in_specs=[t_a, t_b, r_a, r_b, t_a, t_a], out_specs=t_a,
        out_shape=jax.ShapeDtypeStruct((s, D_MODEL), BF16),
        compiler_params=_params(("parallel",)),
    )(gates_raw, gates_raw, b_gate, b_gate, p_a, p_b)


def _gate_bwd(gates_raw, b_gate, p_a, p_b, dm, *, name, tm=512):
    s = p_a.shape[0]
    tm = min(tm, s)

    def body(ga_ref, gb_ref, ba_ref, bb_ref, pa_ref, pb_ref, dm_ref, dpa_ref, dpb_ref, dga_ref, dgb_ref,
             dba_ref, dbb_ref):
        @pl.when(pl.program_id(0) == 0)
        def _():
            dba_ref[...] = jnp.zeros_like(dba_ref)
            dbb_ref[...] = jnp.zeros_like(dbb_ref)

        d = dm_ref[...].astype(F32)
        for g_ref, b_ref, p_ref, dp_ref, dg_ref, db_ref in ((ga_ref, ba_ref, pa_ref, dpa_ref, dga_ref, dba_ref),
                                                            (gb_ref, bb_ref, pb_ref, dpb_ref, dgb_ref, dbb_ref)):
            sg = _sigmoid(g_ref[...].astype(F32) + b_ref[...])
            dp_ref[...] = (d * sg).astype(BF16)
            dg = d * p_ref[...].astype(F32) * (sg * (1.0 - sg))
            dg_ref[...] = dg.astype(BF16)
            db_ref[...] += jnp.sum(dg, axis=0, keepdims=True)

    t_a = pl.BlockSpec((tm, D_MODEL), lambda i: (i, 0))
    t_b = pl.BlockSpec((tm, D_MODEL), lambda i: (i, 1))
    r_a = pl.BlockSpec((1, D_MODEL), lambda i: (0, 0))
    r_b = pl.BlockSpec((1, D_MODEL), lambda i: (0, 1))
    big = jax.ShapeDtypeStruct((s, D_MODEL), BF16)
    row = jax.ShapeDtypeStruct((1, D_MODEL), F32)
    return pl.pallas_call(
        body, name=name, grid=(s // tm,),
        in_specs=[t_a, t_b, r_a, r_b, t_a, t_a, t_a], out_specs=[t_a, t_a, t_a, t_a, r_a, r_a],
        out_shape=[big, big, big, big, row, row],
        compiler_params=_params(("arbitrary",)),
    )(gates_raw, gates_raw, b_gate, b_gate, p_a, p_b, dm)


def _adamw_update(w_ref, g_ref, m_ref, v_ref, d_ref, mo_ref, vo_ref):
    gv = g_ref[...]
    mn = ADAM_B1 * m_ref[...] + (1.0 - ADAM_B1) * gv
    vn = ADAM_B2 * v_ref[...] + (1.0 - ADAM_B2) * (gv * gv)
    m_hat = mn / (1.0 - ADAM_B1 ** ADAM_STEP)
    v_hat = vn / (1.0 - ADAM_B2 ** ADAM_STEP)
    d_ref[...] = -ADAM_LR * (m_hat / (jnp.sqrt(v_hat) + ADAM_EPS) + ADAM_WD * w_ref[...])
    mo_ref[...] = mn
    vo_ref[...] = vn


def _adamw_many(ws, gs, ms, vs, *, name):
    n = len(ws)

    def body(*refs):
        for i in range(n):
            _adamw_update(*[refs[k * n + i] for k in range(7)])

    whole = pl.BlockSpec(memory_space=pltpu.VMEM)
    sds = [jax.ShapeDtypeStruct(w.shape, F32) for w in ws]
    outs = pl.pallas_call(
        body, name=name, in_specs=[whole] * (4 * n), out_specs=[whole] * (3 * n), out_shape=sds * 3,
        compiler_params=pltpu.CompilerParams(vmem_limit_bytes=VMEM_LIMIT),
    )(*ws, *gs, *ms, *vs)
    return outs[:n], outs[n:2 * n], outs[2 * n:]


def _adamw_column_halves(w, g_own, g_other, core, m, v, *, name, tr):
    r, c = w.shape
    assert r % tr == 0 and g_own.shape == g_other.shape == (r, c // 2), (name, r, tr)

    def body(core_ref, w_ref, a_ref, b_ref, m_ref, v_ref, g_ref, d_ref, mo_ref, vo_ref):
        first = core_ref[0] == 0
        a, b = a_ref[...], b_ref[...]
        g_ref[...] = jnp.concatenate([jnp.where(first, a, b), jnp.where(first, b, a)], axis=1)
        _adamw_update(w_ref, g_ref, m_ref, v_ref, d_ref, mo_ref, vo_ref)

    blk = pl.BlockSpec((tr, c), lambda i, core_ref: (i, 0))
    half = pl.BlockSpec((tr, c // 2), lambda i, core_ref: (i, 0))
    grid_spec = pltpu.PrefetchScalarGridSpec(
        num_scalar_prefetch=1, grid=(r // tr,), in_specs=[blk, half, half, blk, blk], out_specs=[blk] * 4)
    return pl.pallas_call(
        body, name=name, grid_spec=grid_spec, out_shape=[jax.ShapeDtypeStruct((r, c), F32)] * 4,
        compiler_params=pltpu.CompilerParams(
            dimension_semantics=("parallel",), vmem_limit_bytes=VMEM_LIMIT,
            allow_input_fusion=[False, False, True, True, False, False]),
    )(core, w, g_own, g_other, m, v)


def _adamw_two_sums(w, g_a, g_b, m, v, *, name, tr=128):
    r, c = w.shape
    tr = min(tr, r)
    assert r % tr == 0, (name, r, tr)

    def body(w_ref, ga_ref, gb_ref, m_ref, v_ref, g_ref, d_ref, mo_ref, vo_ref):
        g_ref[...] = ga_ref[...] + gb_ref[...]
        _adamw_update(w_ref, g_ref, m_ref, v_ref, d_ref, mo_ref, vo_ref)

    blk = pl.BlockSpec((tr, c), lambda i: (i, 0))
    sds = jax.ShapeDtypeStruct((r, c), F32)
    return pl.pallas_call(
        body, name=name, grid=(r // tr,), in_specs=[blk] * 5, out_specs=[blk] * 4, out_shape=[sds] * 4,
        compiler_params=_params(("parallel",)),
    )(w, g_a, g_b, m, v)


def _tile(n, pref):
    if n <= pref:
        return n
    best = LANES
    for t in range(LANES, pref + 1, LANES):
        if n % t == 0:
            best = t
    return best


MATMUL_BLOCK_BYTES = 20 * 1024 * 1024


def _mm(pairs, name, **kw):
    trans_b = kw.get("trans_b", False)
    m = (pairs[0][0][0] if isinstance(pairs[0][0], tuple) else pairs[0][0]).shape[0]
    ktot, n = 0, None
    for _, b in pairs:
        shape = b[0].shape[1:] if isinstance(b, tuple) else b.shape
        ktot += shape[1] if trans_b else shape[0]
        n = shape[0] if trans_b else shape[1]
    out_bytes = 4 * (2 if kw.get("add") is not None else 1)
    best = None
    for tm in (256, 512, 1024, 2048):
        for tn in range(LANES, min(n, 1536) + 1, LANES):
            if m % min(tm, m) or n % tn:
                continue
            fits = 2 * ktot * (min(tm, m) + tn) + out_bytes * min(tm, m) * tn <= MATMUL_BLOCK_BYTES
            if fits and (best is None or min(tm, m) * tn >= best[0] * best[1]):
                best = (min(tm, m), tn)
    return _matmul(pairs, tm=best[0], tn=best[1], name=name, **kw)


def _wgrad(a, b, name, **kw):
    return _matmul_tn(a, b, tk=_tile(a.shape[1], 1408), tn=kw.pop("tn", _tile(b.shape[1], 1024)), tm=2048,
                      name=name, **kw)


def _local_step(x, target, get_weight, small, emit_grad):
    heads = jnp.arange(SSD_D_INNER) // SSD_HEAD_DIM
    expand = (jnp.arange(LANES)[:, None] == heads[None, :]).astype(BF16)
    expand_t = expand.T
    group_sum = (jnp.arange(SGU_WIDTH)[:, None] // LANES == jnp.arange(LANES)[None, :]).astype(BF16)
    pad_h = LANES - SSD_HEADS
    dt_bias = jnp.pad(small["dt_bias"], ((0, 0), (0, pad_h)))
    a_log = jnp.pad(small["a_log"], ((0, 0), (0, pad_h)))
    a_log_x = jnp.repeat(small["a_log"], SSD_HEAD_DIM, axis=1)
    d_skip_x = jnp.repeat(small["d_skip"], SSD_HEAD_DIM, axis=1)
    b_sp_t = jnp.pad(small["b_spatial"][0].T, ((0, 0), (0, LANES - SGU_GROUPS)))
    w_sp = small["w_spatial"][0]
    conv_a_w = jnp.pad(small["conv_a_w"], ((0, 4), (0, 0)))
    conv_f_w = jnp.pad(small["conv_f_w"], ((0, 5), (0, 0)))
    final_w = small["final_norm_w"].reshape(1, D_MODEL)

    n1 = _rms_fwd(x, small["norm1_w"], after=small.get("gathers_started"), name="rms1_fwd")
    wts = dict(get_weight("w_in", n1))
    z = _mm([(n1, wts["in_z"])], "in_z")
    xbc_raw = _mm([(n1, wts["in_xbc"])], "in_xbc")
    dt_raw = _mm([(n1, wts["in_dt"])], "in_dt")
    uv_raw = _mm([(n1, wts["in_uv"])], "in_uv", out_dtype=BF16)
    gates_raw = _mm([(n1, wts["in_gate"])], "in_gate", out_dtype=BF16)
    xbc, xbc_pre = _conv_a_fwd(xbc_raw, conv_a_w, small["conv_a_b"], name="conv_a_fwd")
    y, y_a, states = _ssd_fwd(xbc, dt_raw, z, dt_bias, a_log, a_log_x, d_skip_x, small["ssd_norm_w"], expand,
                              name="ssd_fwd")
    y_b = _sgu_fwd(uv_raw, small["uv_b"], small["v_ln_w"], small["v_ln_b"], w_sp, b_sp_t, name="sgu_fwd")
    wts.update(get_weight("w_branch", y_b))
    p_a = _mm([(y_a, wts["branch_a"])], "branch_a", out_dtype=BF16)
    p_b = _mm([(y_b, wts["branch_b"])], "branch_b", out_dtype=BF16)
    mix = _gate_fwd(gates_raw, small["b_gate"], p_a, p_b, name="gate_fwd")
    wts.update(get_weight("w_out", mix))
    h1 = _mm([(mix, wts["out"])], "out_proj", add=x)
    n2 = _rms_fwd(h1, small["norm2_w"], name="rms2_fwd")
    wts.update(get_weight("w_up", n2))
    up_w = wts["up"]
    up_cols = up_w.shape[2]
    up_raw = _matmul([(n2, (up_w, "cols"))], tm=2048, tn=up_cols, out_dtype=BF16, name="up_proj")
    act, up_a, up_v = _conv_f_fwd(up_raw, conv_f_w, small["conv_f_b"], name="conv_f_fwd")
    wts.update(get_weight("w_down", act))
    h2 = _mm([(act, wts["down"])], "down_proj", add=h1)
    loss, dh2, dh2_b, d_final = _final_fwd_bwd(h2, final_w, target, name="final_norm_loss")

    dact = _mm([(dh2_b, wts["down"])], "down_dgrad", trans_b=True)
    started = emit_grad("w_down", _wgrad(act, dh2_b, "down_wgrad"))
    dup_a, dup_v, dwf_a, dwf_v, dbf_a, dbf_v = _conv_f_bwd(up_raw, up_a, up_v, conv_f_w, dact, name="conv_f_bwd")
    dn2 = _mm([((dup_a, 0), (up_w, 0)), ((dup_a, 1), (up_w, 1)), ((dup_v, 0), (up_w, 2)), ((dup_v, 1), (up_w, 3))],
              "up_dgrad", trans_b=True, after=started, out_dtype=BF16)
    g_up = _wgrad(n2, dup_a, "up_wgrad_a", tn=up_cols, stack_out=True, part_of=(N_CHIPS, 0, None))
    g_up = _wgrad(n2, dup_v, "up_wgrad_v", tn=up_cols, stack_out=True, part_of=(N_CHIPS, N_CHIPS // 2, g_up))
    started = emit_grad("w_up", g_up)
    dh1, dh1_b, d_norm2 = _rms_bwd(h1, small["norm2_w"], dn2, dh2, bf16_copy=True, name="rms2_bwd")
    dmix = _mm([(dh1_b, wts["out"])], "out_dgrad", trans_b=True, after=started, out_dtype=BF16)
    started = emit_grad("w_out", _wgrad(mix, dh1_b, "out_wgrad"))
    dp_a, dp_b, dg_a, dg_b, dbg_a, dbg_b = _gate_bwd(gates_raw, small["b_gate"], p_a, p_b, dmix, name="gate_bwd")
    dya = _mm([(dp_a, wts["branch_a"])], "branch_a_dgrad", trans_b=True, after=started)
    dyb = _mm([(dp_b, wts["branch_b"])], "branch_b_dgrad", trans_b=True, out_dtype=BF16)
    g_branch = _wgrad(y_a, dp_a, "branch_a_wgrad", part_of=(3, 0, None))
    g_branch = _wgrad(y_b, dp_b, "branch_b_wgrad", part_of=(3, 2, g_branch))
    started_branch = emit_grad("w_branch", g_branch)
    duv, d_wsp, d_bsp_t, d_lnw, d_lnb, d_uvb = _sgu_bwd(uv_raw, dyb, small["uv_b"], small["v_ln_w"],
                                                        small["v_ln_b"], w_sp, b_sp_t, group_sum, name="sgu_bwd")
    dz, dxbc, ddt, d_ssd_nw, d_dskip, d_alog, d_dtb = _ssd_bwd(
        dya, y, z, xbc, dt_raw, states, dt_bias, a_log, a_log_x, d_skip_x, small["ssd_norm_w"], expand, expand_t,
        name="ssd_bwd")
    dxbc_raw, d_conv_a_w, d_conv_a_b = _conv_a_bwd(xbc_raw, xbc_pre, conv_a_w, dxbc, name="conv_a_bwd")
    started = emit_grad("w_in", {
        "in_z": _wgrad(n1, dz, "in_z_wgrad", after=started_branch), "in_xbc": _wgrad(n1, dxbc_raw, "in_xbc_wgrad"),
        "in_dt": _wgrad(n1, ddt, "in_dt_wgrad")[:, :SSD_HEADS], "in_uv": _wgrad(n1, duv, "in_uv_wgrad"),
        "in_gate_a": _wgrad(n1, dg_a, "in_gate_a_wgrad"), "in_gate_b": _wgrad(n1, dg_b, "in_gate_b_wgrad")})
    dn1 = _mm([(dz, wts["in_z"]), (dxbc_raw, wts["in_xbc"]), (ddt, wts["in_dt"]), (duv, wts["in_uv"]),
               (dg_a, wts["in_gate_a"]), (dg_b, wts["in_gate_b"])], "in_dgrad", trans_b=True, after=started,
              out_dtype=BF16)
    dx, d_norm1 = _rms_bwd(x, small["norm1_w"], dn1, dh1, bf16_copy=False, name="rms1_bwd")

    grads_small = {
        "norm1_w": d_norm1, "b_gate": jnp.concatenate([dbg_a, dbg_b], axis=1),
        "conv_a_w": d_conv_a_w[:4], "conv_a_b": d_conv_a_b,
        "dt_bias": d_dtb[:, :SSD_HEADS], "a_log": d_alog[:, :SSD_HEADS], "d_skip": d_dskip[:, :SSD_HEADS],
        "ssd_norm_w": d_ssd_nw, "uv_b": d_uvb, "v_ln_w": d_lnw, "v_ln_b": d_lnb,
        "w_spatial": d_wsp[None], "b_spatial": d_bsp_t[:, :SGU_GROUPS].T[None],
        "norm2_w": d_norm2, "conv_f_w": jnp.concatenate([dwf_a[:3], dwf_v[:3]], axis=1),
        "conv_f_b": jnp.concatenate([dbf_a, dbf_v], axis=1), "final_norm_w": d_final.reshape(D_MODEL),
    }
    return loss, dx, grads_small


HBM = pl.BlockSpec(memory_space=pl.ANY)
MESH = pl.DeviceIdType.MESH


def _mesh_pos():
    return lax.axis_index("x"), lax.axis_index("y"), lax.axis_index("c")


def _other_chips(x, y):
    return [(1 - x, y), (x, 1 - y), (1 - x, 1 - y)]


def _remote(src, dst, send_sems, recv_sems, k, dev):
    return pltpu.make_async_remote_copy(src_ref=src, dst_ref=dst, send_sem=send_sems.at[k], recv_sem=recv_sems.at[k],
                                        device_id=dev, device_id_type=MESH)


def _dma_sems(n):
    return [pltpu.SemaphoreType.DMA((n,)), pltpu.SemaphoreType.DMA((n,))]


HBM_ONLY = pl.BlockSpec(memory_space=pltpu.HBM)
SEMAPHORES = pl.BlockSpec(memory_space=pltpu.SEMAPHORE)
DATAFLOW_EFFECT = pltpu.SideEffectType.DATAFLOW_SIDE_EFFECTING
N_PEER_CHIPS = N_CHIPS - 1


def _gather_sends(w_ref, land_ref, send_sems, recv_sems):
    x, y, c = _mesh_pos()
    return [_remote(w_ref.at[c], land_ref.at[2 * x + y, c], send_sems, recv_sems, k, (px, py, c))
            for k, (px, py) in enumerate(_other_chips(x, y))]


def _gather_arrivals(w_ref, land_ref, send_sems, recv_sems):
    x, y, c = _mesh_pos()
    return [_remote(w_ref.at[c], land_ref.at[2 * px + py, c], send_sems, recv_sems, k, (px, py, c))
            for k, (px, py) in enumerate(_other_chips(x, y))]


def _gather_whole_sends(w_ref, land_ref, send_sems, recv_sems):
    x, y, c = _mesh_pos()
    return [_remote(w_ref, land_ref.at[2 * x + y], send_sems, recv_sems, k, (px, py, c))
            for k, (px, py) in enumerate(_other_chips(x, y))]


def _gather_whole_arrivals(w_ref, land_ref, send_sems, recv_sems):
    x, y, c = _mesh_pos()
    return [_remote(w_ref, land_ref.at[2 * px + py], send_sems, recv_sems, k, (px, py, c))
            for k, (px, py) in enumerate(_other_chips(x, y))]


def _scatter_sends(h_ref, land_ref, send_sems, recv_sems):
    x, y, c = _mesh_pos()
    return [_remote(h_ref.at[2 * px + py], land_ref.at[2 * x + y], send_sems, recv_sems, k, (px, py, c))
            for k, (px, py) in enumerate(_other_chips(x, y))]


def _scatter_arrivals(h_ref, land_ref, send_sems, recv_sems):
    x, y, c = _mesh_pos()
    return [_remote(h_ref.at[2 * x + y], land_ref.at[2 * px + py], send_sems, recv_sems, k, (px, py, c))
            for k, (px, py) in enumerate(_other_chips(x, y))]


def _exchange_wait_many(pendings, after, sends, arrivals, *, name):
    n = len(pendings)

    def body(*refs):
        for i in range(n):
            src_ref, land_ref, send_ref, recv_ref = refs[i], refs[n + i], refs[2 * n + i], refs[3 * n + i]
            for cp in sends(src_ref, land_ref, send_ref, recv_ref):
                cp.wait_send()
            for cp in arrivals(src_ref, land_ref, send_ref, recv_ref):
                cp.wait_recv()

    sources = [p[2] for p in pendings]
    landings = [p[3] for p in pendings]
    outs = pl.pallas_call(
        body, name=name,
        out_shape=tuple(pltpu.HBM(a.shape, a.dtype) for a in sources + landings),
        in_specs=[HBM_ONLY] * (2 * n) + [SEMAPHORES] * (2 * n) + [pl.BlockSpec(memory_space=pl.ANY)],
        out_specs=tuple([HBM_ONLY] * (2 * n)), input_output_aliases={i: i for i in range(2 * n)},
        compiler_params=pltpu.CompilerParams(has_side_effects=DATAFLOW_EFFECT),
    )(*sources, *landings, *[p[0] for p in pendings], *[p[1] for p in pendings], after)
    return [(outs[i], outs[n + i]) for i in range(n)]


def _sibling_sends(src_ref, land_ref, send_sems, recv_sems):
    x, y, c = _mesh_pos()
    return [_remote(src_ref, land_ref, send_sems, recv_sems, 0, (x, y, 1 - c))]


def _exchange_start(sources, landing_shapes, sends, *, after=None, name):
    n = len(sources)
    extra = [] if after is None else [after]

    def body(*refs):
        sems = refs[2 * n + len(extra):4 * n + len(extra)]
        for i in range(n):
            send_i = sends[i] if isinstance(sends, (list, tuple)) else sends
            for cp in send_i(refs[i], refs[n + i], sems[2 * i], sems[2 * i + 1]):
                cp.start()
        refs[-1][...] = jnp.zeros_like(refs[-1])

    hbm = [pltpu.HBM(s.shape, s.dtype) for s in sources] + [pltpu.HBM(shp, s.dtype)
                                                             for shp, s in zip(landing_shapes, sources)]
    outs = pl.pallas_call(
        body, name=name,
        out_shape=tuple([pltpu.SemaphoreType.DMA((N_PEER_CHIPS,))] * (2 * n) + hbm
                        + [jax.ShapeDtypeStruct((8, LANES), F32)]),
        in_specs=[HBM_ONLY] * (2 * n) + [pl.BlockSpec(memory_space=pl.ANY)] * len(extra),
        out_specs=tuple([SEMAPHORES] * (2 * n) + [HBM_ONLY] * (2 * n) + [pl.BlockSpec(memory_space=pltpu.VMEM)]),
        input_output_aliases={i: 2 * n + i for i in range(2 * n)},
        compiler_params=pltpu.CompilerParams(has_side_effects=DATAFLOW_EFFECT),
    )(*[pltpu.with_memory_space_constraint(s, pltpu.HBM) for s in sources],
      *[pltpu.with_memory_space_constraint(lax.empty(shp, s.dtype), pltpu.HBM)
        for shp, s in zip(landing_shapes, sources)], *extra)
    pending = [(outs[2 * i], outs[2 * i + 1], outs[2 * n + i], outs[3 * n + i]) for i in range(n)]
    return pending, outs[-1]


def _exchange_wait(pending, after, sends, arrivals, *, name):
    send_sems, recv_sems, source, landing = pending

    def body(src_ref, land_ref, send_ref, recv_ref, after_ref, src_out, land_out):
        for cp in sends(src_ref, land_ref, send_ref, recv_ref):
            cp.wait_send()
        for cp in arrivals(src_ref, land_ref, send_ref, recv_ref):
            cp.wait_recv()

    return pl.pallas_call(
        body, name=name,
        out_shape=(pltpu.HBM(source.shape, source.dtype), pltpu.HBM(landing.shape, landing.dtype)),
        in_specs=[HBM_ONLY, HBM_ONLY, SEMAPHORES, SEMAPHORES, pl.BlockSpec(memory_space=pl.ANY)],
        out_specs=(HBM_ONLY, HBM_ONLY), input_output_aliases={0: 0, 1: 1},
        compiler_params=pltpu.CompilerParams(has_side_effects=DATAFLOW_EFFECT),
    )(source, landing, send_sems, recv_sems, after)


def _gather_ici(shard, *, name):
    _, rh, cols = shard.shape

    def body(w_ref, o_ref, send_sems, recv_sems):
        x, y, c = _mesh_pos()
        mine = 2 * x + y
        sends = []
        for k, (px, py) in enumerate(_other_chips(x, y)):
            cp = _remote(w_ref.at[c], o_ref.at[mine, c], send_sems, recv_sems, k, (px, py, c))
            cp.start()
            sends.append(cp)
        for k, (px, py) in enumerate(_other_chips(x, y)):
            _remote(w_ref.at[c], o_ref.at[2 * px + py, c], send_sems, recv_sems, k, (px, py, c)).wait_recv()
        for cp in sends:
            cp.wait_send()

    return pl.pallas_call(
        body, name=name, in_specs=[HBM], out_specs=HBM,
        out_shape=jax.ShapeDtypeStruct((N_CHIPS, 2, rh, cols), shard.dtype), scratch_shapes=_dma_sems(3),
    )(shard)


def _gather_d2d(parts, *, name):
    def body(a_ref, o_ref, send_sems, recv_sems):
        x, y, c = _mesh_pos()
        sibling = (x, y, 1 - c)
        sends = []
        for k, (px, py) in enumerate(_other_chips(x, y)):
            cp = _remote(a_ref.at[2 * px + py, c], o_ref.at[2 * px + py, c], send_sems, recv_sems, k, sibling)
            cp.start()
            sends.append(cp)
        for k, (px, py) in enumerate(_other_chips(x, y)):
            _remote(a_ref.at[2 * px + py, c], o_ref.at[2 * px + py, 1 - c], send_sems, recv_sems, k, sibling).wait_recv()
        for cp in sends:
            cp.wait_send()

    return pl.pallas_call(
        body, name=name, in_specs=[HBM], out_specs=HBM,
        out_shape=jax.ShapeDtypeStruct(parts.shape, parts.dtype),
        input_output_aliases={0: 0}, scratch_shapes=_dma_sems(3),
    )(parts)


def _all_gather_chips(shard_flat, name):
    rows, cols = shard_flat.shape
    parts = _gather_ici(shard_flat.reshape(2, rows // 2, cols), name=name + "_ici")
    others = _gather_d2d(parts, name=name + "_d2d").reshape(N_CHIPS, rows, cols)
    chip = 2 * lax.axis_index("x") + lax.axis_index("y")
    return lax.dynamic_update_slice(others, shard_flat[None], (chip, 0, 0))


def _row_tile(rows, mult, cap):
    best = mult
    for t in range(mult, min(rows, cap) + 1, mult):
        if rows % t == 0:
            best = t
    assert rows % best == 0, (rows, mult)
    return best


def _swap_halves_d2d(g, *, after=None, name):
    blocks = list(g) if isinstance(g, (list, tuple)) else [g]
    rh, cols = blocks[0].shape[-2:]
    extra = [] if after is None else [after]

    def body(*refs):
        g_refs = refs[:len(blocks)]
        o_ref, send_sems, recv_sems = refs[len(blocks) + len(extra):]
        x, y, c = _mesh_pos()
        sibling = (x, y, 1 - c)

        def half(s, k):
            return g_refs[s].at[k] if len(g_refs) > 1 else g_refs[0].at[s, k]

        sends = []
        for s in range(N_CHIPS):
            cp = _remote(half(s, 1 - c), o_ref.at[s], send_sems, recv_sems, s, sibling)
            cp.start()
            sends.append(cp)
        for s in range(N_CHIPS):
            _remote(half(s, c), o_ref.at[s], send_sems, recv_sems, s, sibling).wait_recv()
        for cp in sends:
            cp.wait_send()

    return pl.pallas_call(
        body, name=name, in_specs=[HBM] * (len(blocks) + len(extra)), out_specs=HBM,
        out_shape=jax.ShapeDtypeStruct((N_CHIPS, rh, cols), blocks[0].dtype), scratch_shapes=_dma_sems(N_CHIPS),
    )(*blocks, *extra)


def _add_own_half(g, arrived, core, *, name):
    blocks = list(g) if isinstance(g, (list, tuple)) else [g]
    dtype = blocks[0].dtype
    rh, cols = blocks[0].shape[-2:]
    mult = 16 if dtype == BF16 else 8
    tr = _row_tile(rh, mult, max(mult, (512 * 1024) // cols))

    def body(core_ref, *refs):
        g_refs, a_ref, o_ref = refs[:-2], refs[-2], refs[-1]
        if len(g_refs) == 1:
            o_ref[...] = (g_refs[0][0].astype(F32) + a_ref[...].astype(F32)).astype(o_ref.dtype)
            return
        for k, g_ref in enumerate(g_refs):
            @pl.when(pl.program_id(0) == k)
            def _(g_ref=g_ref):
                o_ref[...] = (g_ref[...].astype(F32) + a_ref[...].astype(F32)).astype(o_ref.dtype)

    if len(blocks) == 1:
        g_specs = [pl.BlockSpec((1, 1, tr, cols), lambda s, i, core_ref: (s, core_ref[0], i, 0))]
    else:
        g_specs = [pl.BlockSpec((1, tr, cols), lambda s, i, core_ref, k=k: (core_ref[0], jnp.where(s == k, i, 0), 0))
                   for k in range(N_CHIPS)]
    grid_spec = pltpu.PrefetchScalarGridSpec(
        num_scalar_prefetch=1, grid=(N_CHIPS, rh // tr),
        in_specs=g_specs + [pl.BlockSpec((1, tr, cols), lambda s, i, core_ref: (s, i, 0))],
        out_specs=pl.BlockSpec((1, tr, cols), lambda s, i, core_ref: (s, i, 0)))
    return pl.pallas_call(
        body, name=name, grid_spec=grid_spec, out_shape=jax.ShapeDtypeStruct((N_CHIPS, rh, cols), dtype),
        compiler_params=_params(("parallel", "parallel")),
    )(core, *blocks, arrived)


def _scatter_ici(h, *, after=None, name):
    extra = [] if after is None else [after]

    def body(h_ref, *rest):
        o_ref, send_sems, recv_sems = rest[len(extra):]
        x, y, c = _mesh_pos()
        mine = 2 * x + y
        sends = []
        for k, (px, py) in enumerate(_other_chips(x, y)):
            cp = _remote(h_ref.at[2 * px + py], o_ref.at[mine], send_sems, recv_sems, k, (px, py, c))
            cp.start()
            sends.append(cp)
        for k, (px, py) in enumerate(_other_chips(x, y)):
            _remote(h_ref.at[mine], o_ref.at[2 * px + py], send_sems, recv_sems, k, (px, py, c)).wait_recv()
        for cp in sends:
            cp.wait_send()

    others = pl.pallas_call(
        body, name=name, in_specs=[HBM] * (1 + len(extra)), out_specs=HBM,
        out_shape=jax.ShapeDtypeStruct(h.shape, h.dtype), scratch_shapes=_dma_sems(3),
    )(h, *extra)
    chip = 2 * lax.axis_index("x") + lax.axis_index("y")
    own = lax.dynamic_slice_in_dim(h, chip, 1, axis=0)
    return lax.dynamic_update_slice(others, own, (chip, 0, 0))


def _sum_chips(parts, *, name):
    _, rh, cols = parts.shape
    mult = 16 if parts.dtype == BF16 else 8
    tr = _row_tile(rh, mult, max(mult, (512 * 1024) // cols))

    def body(p_ref, o_ref):
        acc = p_ref[0].astype(F32)
        for s in range(1, N_CHIPS):
            acc = acc + p_ref[s].astype(F32)
        o_ref[...] = acc

    return pl.pallas_call(
        body, name=name, grid=(rh // tr,),
        in_specs=[pl.BlockSpec((N_CHIPS, tr, cols), lambda i: (0, i, 0))],
        out_specs=pl.BlockSpec((tr, cols), lambda i: (i, 0)),
        out_shape=jax.ShapeDtypeStruct((rh, cols), F32), compiler_params=_params(("parallel",)),
    )(parts)


def _sum_chips_with_own(landed, sent, chip, *, name):
    _, rh, cols = landed.shape
    mult = 16 if landed.dtype == BF16 else 8
    tr = _row_tile(rh, mult, max(mult, (512 * 1024) // cols))

    def body(chip_ref, own_ref, px_ref, py_ref, pxy_ref, o_ref):
        acc = own_ref[0].astype(F32)
        for p_ref in (px_ref, py_ref, pxy_ref):
            acc = acc + p_ref[0].astype(F32)
        o_ref[...] = acc

    def block_of(flip):
        return pl.BlockSpec((1, tr, cols), lambda i, chip_ref: (chip_ref[0] ^ flip, i, 0))

    grid_spec = pltpu.PrefetchScalarGridSpec(
        num_scalar_prefetch=1, grid=(rh // tr,),
        in_specs=[block_of(0), block_of(2), block_of(1), block_of(3)],
        out_specs=pl.BlockSpec((tr, cols), lambda i, chip_ref: (i, 0)))
    return pl.pallas_call(
        body, name=name, grid_spec=grid_spec, out_shape=jax.ShapeDtypeStruct((rh, cols), F32),
        compiler_params=_params(("parallel",)),
    )(chip, sent, landed, landed, landed)


def _share_d2d(f, *, name):
    fs = f if isinstance(f, (list, tuple)) else [f]
    others = _swap_with_sibling(fs, name=name)
    first = lax.axis_index("c") == 0
    both = [jnp.stack([jnp.where(first, a, b), jnp.where(first, b, a)]) for a, b in zip(fs, others)]
    return both if isinstance(f, (list, tuple)) else both[0]


def _swap_with_sibling(fs, *, name):
    n = len(fs)

    def body(*refs):
        x, y, c = _mesh_pos()
        sibling = (x, y, 1 - c)
        send_sems, recv_sems = refs[2 * n:]
        copies = [_remote(refs[i], refs[n + i], send_sems, recv_sems, i, sibling) for i in range(n)]
        for cp in copies:
            cp.start()
        for cp in copies:
            cp.wait()

    return pl.pallas_call(
        body, name=name, in_specs=[HBM] * n, out_specs=[HBM] * n,
        out_shape=[jax.ShapeDtypeStruct(a.shape, a.dtype) for a in fs], scratch_shapes=_dma_sems(n),
    )(*fs)


def _reduce_scatter_chips(g, core, name, after=None, after_swap=None):
    _, rows, cols = g.shape
    g = g.reshape(N_CHIPS, 2, rows // 2, cols)
    arrived = _swap_halves_d2d(g, after=after, name=name + "_swap")
    started = after_swap(arrived) if after_swap is not None else None
    chip_sum = _add_own_half(g, arrived, core, name=name + "_add2")
    parts = _scatter_ici(chip_sum, after=started, name=name + "_ici")
    total = _sum_chips(parts, name=name + "_sum4")
    return _share_d2d(total, name=name + "_share").reshape(rows, cols)


BIG = ("w_in", "w_branch", "w_out", "w_up", "w_down")
BIG_COLUMN_SHARDED = ("w_in", "w_up")
CONV = ("conv_a_w", "conv_f_w")
REPLICATED = ("norm1_w", "b_gate", "conv_a_b", "dt_bias", "a_log", "d_skip", "ssd_norm_w", "uv_b", "v_ln_w",
              "v_ln_b", "w_spatial", "b_spatial", "norm2_w", "conv_f_b", "final_norm_w")
WEIGHT_ORDER = ("norm1_w", "w_in", "b_gate", "conv_a_w", "conv_a_b", "dt_bias", "a_log", "d_skip", "ssd_norm_w",
                "uv_b", "v_ln_w", "v_ln_b", "w_spatial", "b_spatial", "w_branch", "w_out", "norm2_w", "w_up",
                "conv_f_w", "conv_f_b", "w_down", "final_norm_w")
SMALL_EXCHANGE_ROWS = 64


_GATE0 = SSD_IN + 2 * SGU_WIDTH
IN_SEGMENTS = {
    "in_z": (0, SSD_D_INNER), "in_xbc": (SSD_D_INNER, SSD_D_INNER + SSD_XBC), "in_dt": (SSD_D_INNER + SSD_XBC, SSD_IN),
    "in_uv": (SSD_IN, _GATE0), "in_gate": (_GATE0, IN_COLS), "in_gate_a": (_GATE0, _GATE0 + D_MODEL),
    "in_gate_b": (_GATE0 + D_MODEL, IN_COLS),
}
IN_GRAD_SEGMENTS = ("in_z", "in_xbc", "in_dt", "in_uv", "in_gate_a", "in_gate_b")


def _take_columns(parts, start, stop):
    out = []
    for a, first in parts:
        lo, hi = max(start, first), min(stop, first + a.shape[1])
        if lo < hi:
            out.append(a[:, lo - first:hi - first])
    return out[0] if len(out) == 1 else jnp.concatenate(out, axis=1)


def _flat_rows(arrays, row_multiple):
    flat = jnp.concatenate([a.reshape(-1) for a in arrays])
    rows = -(-flat.shape[0] // (LANES * row_multiple)) * row_multiple
    return jnp.pad(flat, (0, rows * LANES - flat.shape[0])).reshape(rows, LANES)


def _unflatten(flat, shapes):
    flat = flat.reshape(-1)
    out, off = [], 0
    for shp in shapes:
        n = math.prod(shp)
        out.append(flat[off:off + n].reshape(shp))
        off += n
    return out


def _from_chip_blocks(blocks, name):
    if name in BIG_COLUMN_SHARDED or name in CONV:
        k = blocks.shape[1]
        return jnp.transpose(blocks, (1, 0, 2)).reshape(k, -1)
    return blocks.reshape(-1, blocks.shape[-1])


def _to_chip_blocks(whole, name):
    if name in BIG_COLUMN_SHARDED or name in CONV:
        k, n = whole.shape
        return jnp.transpose(whole.reshape(k, N_CHIPS, n // N_CHIPS), (1, 0, 2))
    return whole.reshape(N_CHIPS, whole.shape[0] // N_CHIPS, whole.shape[1])


def kernel(x, norm1_w, w_in, b_gate, conv_a_w, conv_a_b, dt_bias, a_log, d_skip, ssd_norm_w, uv_b, v_ln_w, v_ln_b, w_spatial, b_spatial, w_branch, w_out, norm2_w, w_up, conv_f_w, conv_f_b, w_down, final_norm_w, loss_target, m_norm1_w, m_w_in, m_b_gate, m_conv_a_w, m_conv_a_b, m_dt_bias, m_a_log, m_d_skip, m_ssd_norm_w, m_uv_b, m_v_ln_w, m_v_ln_b, m_w_spatial, m_b_spatial, m_w_branch, m_w_out, m_norm2_w, m_w_up, m_conv_f_w, m_conv_f_b, m_w_down, m_final_norm_w, v_norm1_w, v_w_in, v_b_gate, v_conv_a_w, v_conv_a_b, v_dt_bias, v_a_log, v_d_skip, v_ssd_norm_w, v_uv_b, v_v_ln_w, v_v_ln_b, v_w_spatial, v_b_spatial, v_w_branch, v_w_out, v_norm2_w, v_w_up, v_conv_f_w, v_conv_f_b, v_w_down, v_final_norm_w):
    weights = dict(norm1_w=norm1_w, w_in=w_in, b_gate=b_gate, conv_a_w=conv_a_w, conv_a_b=conv_a_b, dt_bias=dt_bias,
                   a_log=a_log, d_skip=d_skip, ssd_norm_w=ssd_norm_w, uv_b=uv_b, v_ln_w=v_ln_w, v_ln_b=v_ln_b,
                   w_spatial=w_spatial, b_spatial=b_spatial, w_branch=w_branch, w_out=w_out, norm2_w=norm2_w,
                   w_up=w_up, conv_f_w=conv_f_w, conv_f_b=conv_f_b, w_down=w_down, final_norm_w=final_norm_w)
    mom1 = dict(norm1_w=m_norm1_w, w_in=m_w_in, b_gate=m_b_gate, conv_a_w=m_conv_a_w, conv_a_b=m_conv_a_b,
                dt_bias=m_dt_bias, a_log=m_a_log, d_skip=m_d_skip, ssd_norm_w=m_ssd_norm_w, uv_b=m_uv_b,
                v_ln_w=m_v_ln_w, v_ln_b=m_v_ln_b, w_spatial=m_w_spatial, b_spatial=m_b_spatial, w_branch=m_w_branch,
                w_out=m_w_out, norm2_w=m_norm2_w, w_up=m_w_up, conv_f_w=m_conv_f_w, conv_f_b=m_conv_f_b,
                w_down=m_w_down, final_norm_w=m_final_norm_w)
    mom2 = dict(norm1_w=v_norm1_w, w_in=v_w_in, b_gate=v_b_gate, conv_a_w=v_conv_a_w, conv_a_b=v_conv_a_b,
                dt_bias=v_dt_bias, a_log=v_a_log, d_skip=v_d_skip, ssd_norm_w=v_ssd_norm_w, uv_b=v_uv_b,
                v_ln_w=v_v_ln_w, v_ln_b=v_v_ln_b, w_spatial=v_w_spatial, b_spatial=v_b_spatial, w_branch=v_w_branch,
                w_out=v_w_out, norm2_w=v_norm2_w, w_up=v_w_up, conv_f_w=v_conv_f_w, conv_f_b=v_conv_f_b,
                w_down=v_w_down, final_norm_w=v_final_norm_w)
    chip = 2 * lax.axis_index("x") + lax.axis_index("y")
    core = lax.axis_index("c").astype(jnp.int32).reshape(1)

    whole = {}
    conv_shapes = [weights[n].shape[1:] for n in CONV]
    conv_gathered = _all_gather_chips(_flat_rows([weights[n] for n in CONV], 16), "gather_conv").reshape(N_CHIPS, -1)
    off = 0
    for n, shp in zip(CONV, conv_shapes):
        size = math.prod(shp)
        whole[n] = _from_chip_blocks(conv_gathered[:, off:off + size].reshape((N_CHIPS,) + shp), n)
        off += size
    shard_shapes = {n: weights[n].shape[1:] for n in BIG}
    halves = [weights[n][0].astype(BF16).reshape(2, shard_shapes[n][0] // 2, shard_shapes[n][1]) for n in BIG]
    sends = [_gather_sends if n == "w_in" else _gather_whole_sends for n in BIG]
    gathers, gathers_started = _exchange_start(halves, [(N_CHIPS,) + h.shape for h in halves], sends,
                                               after=conv_gathered, name="gather_start")
    gathers = dict(zip(BIG, gathers))

    def get_weight(name, after):
        rows, cols = shard_shapes[name]
        if name == "w_in":
            own, landed = _exchange_wait(gathers[name], after, _gather_sends, _gather_arrivals,
                                         name="gather_" + name + "_wait")
            landed = _gather_d2d(landed, name="gather_" + name + "_d2d")
        else:
            own, landed = _exchange_wait(gathers[name], after, _gather_whole_sends, _gather_whole_arrivals,
                                         name="gather_" + name + "_wait")
        blocks = lax.dynamic_update_slice(landed.reshape(N_CHIPS, rows, cols), own.reshape(1, rows, cols),
                                          (chip, 0, 0))
        if name == "w_up":
            return {"up": blocks}
        if name == "w_in":
            parts = [(blocks[k], cols * k) for k in range(N_CHIPS)]
            segs = {n: _take_columns(parts, a, b) for n, (a, b) in IN_SEGMENTS.items()}
            segs["in_dt"] = jnp.pad(segs["in_dt"], ((0, 0), (0, LANES - SSD_HEADS)))
            return segs
        full = _from_chip_blocks(blocks, name)
        if name == "w_branch":
            return {"branch_a": full[:SSD_D_INNER], "branch_b": full[SSD_D_INNER:]}
        return {name[2:]: full}

    small = {n: weights[n] for n in REPLICATED}
    small["conv_a_w"] = whole["conv_a_w"]
    small["conv_f_w"] = whole["conv_f_w"]
    small["gathers_started"] = gathers_started

    reductions = {}

    def emit_grad(name, g):
        if name == "w_in":
            parts = [(g[n], IN_SEGMENTS[n][0]) for n in IN_GRAD_SEGMENTS]
            rows, cols = shard_shapes[name]
            g_halves = [_take_columns(parts, cols * k, cols * (k + 1)).reshape(2, rows // 2, cols)
                        for k in range(N_CHIPS)]
            arrived = _swap_halves_d2d(g_halves, name="reduce_" + name + "_swap")
            g_blocks = _add_own_half(g_halves, arrived, core, name="reduce_" + name + "_add2")
        else:
            g_blocks = g if name == "w_up" else _to_chip_blocks(g, name)
        (pending,), started = _exchange_start([g_blocks], [g_blocks.shape], _scatter_sends,
                                              name="reduce_" + name + "_start")
        reductions[name] = pending
        return started

    loss, dx, grads_small = _local_step(x[0], loss_target[0], get_weight, small, emit_grad)

    order = ("w_down", "w_up", "w_out", "w_branch", "w_in")
    core_sums = []
    chip_index = chip.astype(jnp.int32).reshape(1)
    for n in order:
        sent, landed = _exchange_wait(reductions[n], dx, _scatter_sends, _scatter_arrivals,
                                      name="reduce_" + n + "_wait")
        core_sums.append(_sum_chips_with_own(landed, sent, chip_index, name="reduce_" + n + "_sum4"))
    grads = {}
    swaps = []

    def start_sum_swap(small_swapped):
        pending, started = _exchange_start(core_sums, [a.shape for a in core_sums], _sibling_sends,
                                           after=small_swapped, name="reduce_swap_start")
        swaps.extend(pending)
        return started

    small_names = REPLICATED + CONV + ("loss",)
    grads_small = dict(grads_small, loss=loss)
    small_shapes = [grads_small[n].shape for n in small_names]
    g_small = _flat_rows([grads_small[n] for n in small_names], N_CHIPS * 2 * SMALL_EXCHANGE_ROWS)
    red_small = _reduce_scatter_chips(g_small.reshape(N_CHIPS, -1, LANES), core, "reduce_small", after=core_sums[-1],
                                      after_swap=start_sum_swap)
    all_small = _all_gather_chips(red_small, "gather_small")
    swapped = _exchange_wait_many(swaps, all_small, _sibling_sends, _sibling_sends, name="reduce_swap_wait")
    core_sums = {n: own for n, (own, _) in zip(order, swapped)}
    sibling_sums = {n: other for n, (_, other) in zip(order, swapped)}
    for n, g in zip(small_names, _unflatten(all_small, small_shapes)):
        if n == "loss":
            total_loss = g[0, 0]
            continue
        if n in CONV:
            width = g.shape[1] // N_CHIPS
            g = lax.dynamic_slice_in_dim(g, chip * width, width, axis=1)
        grads[n] = g.reshape(weights[n].shape[1:]) if n != "final_norm_w" else g

    delta, new_m, new_v = {}, {}, {}
    for n in BIG:
        shp = weights[n].shape
        if n == "w_in":
            results = _adamw_column_halves(weights[n][0].T, core_sums[n].T, sibling_sums[n].T, core, mom1[n][0].T,
                                           mom2[n][0].T, name="adamw_" + n,
                                           tr=_row_tile(shp[2], 8, 136))
            results = [a.T for a in results]
        else:
            results = _adamw_two_sums(weights[n][0], core_sums[n], sibling_sums[n], mom1[n][0], mom2[n][0],
                                      name="adamw_" + n, tr=_row_tile(shp[1], 8, 352))
        grads[n], delta[n], new_m[n], new_v[n] = [a.reshape(shp) for a in results]
    small_all = [n for n in WEIGHT_ORDER if n not in BIG]

    def as_2d(a):
        return a.reshape(-1, a.shape[-1])

    results = _adamw_many(*[[as_2d(src[n]) for n in small_all] for src in (weights, grads, mom1, mom2)],
                          name="adamw_small")
    for n, dv, mv, vv in zip(small_all, *results):
        shp = weights[n].shape
        delta[n], new_m[n], new_v[n] = dv.reshape(shp), mv.reshape(shp), vv.reshape(shp)

    grad_out = [grads[n].reshape(weights[n].shape) for n in WEIGHT_ORDER]
    return (total_loss, dx[None], *grad_out, *[delta[n] for n in WEIGHT_ORDER], *[new_m[n] for n in WEIGHT_ORDER],
            *[new_v[n] for n in WEIGHT_ORDER])
```
